```python
import jax, jax.numpy as jnp
from jax import lax
import numpy as np

D_MODEL = 1024
BATCH = 8
SEQ = 4096
DEPTH = 1

PLE_DIM = 256
HG_HEADS = 4
HG_DK = 128
HG_DV = 128
HG_KW = HG_HEADS * HG_DK
HG_VW = HG_HEADS * HG_DV
HG_CHUNK = 64
FOX_HEADS = 8
FOX_DH = 64
FOX_W = FOX_HEADS * FOX_DH
FOX_BLOCK = 128
D_FF = ((-(-8 * D_MODEL // 3) + 255) // 256) * 256
EPS = 1e-6

OFF_HG_Q = 0
OFF_HG_F = OFF_HG_Q + HG_KW
OFF_HG_I = OFF_HG_F + HG_KW
OFF_HG_G = OFF_HG_I + HG_VW
OFF_FOX_Q = OFF_HG_G + HG_VW
OFF_FOX_K = OFF_FOX_Q + FOX_W
OFF_FOX_V = OFF_FOX_K + FOX_W
OFF_FOX_F = OFF_FOX_V + FOX_W
OFF_GATE = OFF_FOX_F + FOX_HEADS
IN_COLS = OFF_GATE + 2 * D_MODEL

kernel_name = "hgrn2_fox_gated_hybrid"


def rms_norm(x, g):
    xf = x.astype(jnp.float32)
    y = xf * lax.rsqrt(jnp.mean(xf * xf, axis=-1, keepdims=True) + EPS)
    return (y * g.astype(jnp.float32)).astype(x.dtype)


def hgrn2_mixer(q, f_logit, i, g, lb, o_gain):
    B, S, _ = q.shape
    f32 = jnp.float32
    qf = jax.nn.silu(q.astype(f32))
    fg = lb + (1.0 - lb) * jax.nn.sigmoid(f_logit.astype(f32))
    kf = 1.0 - fg
    logf = jnp.log(fg)
    vf = i.astype(f32)
    n = S // HG_CHUNK

    def to_chunks(t, d):
        return t.reshape(B, n, HG_CHUNK, HG_HEADS, d).transpose(1, 0, 3, 2, 4)

    qc = to_chunks(qf, HG_DK)
    kc = to_chunks(kf, HG_DK)
    vc = to_chunks(vf, HG_DV)
    bc = jnp.cumsum(to_chunks(logf, HG_DK), axis=3)
    causal = jnp.tril(jnp.ones((HG_CHUNK, HG_CHUNK), dtype=bool))[:, :, None]

    def step(state, xs):
        qt, kt, vt, bt = xs
        inter = jnp.einsum('bhtc,bhcv->bhtv', qt * jnp.exp(bt), state)
        diff = bt[:, :, :, None, :] - bt[:, :, None, :, :]
        decay = jnp.exp(jnp.where(causal, diff, -jnp.inf))
        scores = jnp.einsum('bhtc,bhsc,bhtsc->bhts', qt, kt, decay)
        intra = jnp.einsum('bhts,bhsv->bhtv', scores, vt)
        b_last = bt[:, :, -1:, :]
        new_state = (jnp.exp(b_last[:, :, 0, :])[..., None] * state
                     + jnp.einsum('bhsc,bhsv->bhcv', kt * jnp.exp(b_last - bt), vt))
        return new_state, inter + intra

    s0 = jnp.zeros((B, HG_HEADS, HG_DK, HG_DV), f32)
    _, o = lax.scan(step, s0, (qc, kc, vc, bc))
    o = o.transpose(1, 0, 3, 2, 4).reshape(B, S, HG_HEADS, HG_DV)
    o = rms_norm(o, o_gain).reshape(B, S, HG_VW)
    o = o * jax.nn.silu(g.astype(f32))
    return o.astype(q.dtype)


def forgetting_attention(q, k, v, f_logit, f_bias, q_gain, k_gain):
    B, S, _ = q.shape
    f32 = jnp.float32

    def heads(t):
        return t.reshape(B, S, FOX_HEADS, FOX_DH).transpose(0, 2, 1, 3)

    qh = rms_norm(heads(q), q_gain).astype(f32)
    kh = rms_norm(heads(k), k_gain).astype(f32)
    vh = heads(v)
    logf = jax.nn.log_sigmoid(f_logit.astype(f32) + f_bias.astype(f32))
    c = jnp.cumsum(logf, axis=1).transpose(0, 2, 1)
    scale = FOX_DH ** -0.5
    outs = []
    for blk in range(S // FOX_BLOCK):
        t0 = blk * FOX_BLOCK
        t1 = t0 + FOX_BLOCK
        s = jnp.einsum('bhtd,bhsd->bhts', qh[:, :, t0:t1], kh[:, :, :t1]) * scale
        s = s + c[:, :, t0:t1, None] - c[:, :, None, :t1]
        mask = (t0 + jnp.arange(FOX_BLOCK))[:, None] >= jnp.arange(t1)[None, :]
        s = jnp.where(mask, s, -jnp.inf)
        pr = jax.nn.softmax(s, axis=-1).astype(vh.dtype)
        outs.append(jnp.einsum('bhts,bhsd->bhtd', pr, vh[:, :, :t1]))
    o = jnp.concatenate(outs, axis=2)
    return o.transpose(0, 2, 1, 3).reshape(B, S, FOX_W)


def _fwd_setup_inputs(seed: int = 0) -> dict:
    key = jax.random.key(seed)
    ks = jax.random.split(key, 24)
    f32 = jnp.float32

    def w(k, shape, fan_in):
        return jax.random.normal(k, shape, f32) * (fan_in ** -0.5)

    def gain(k, shape):
        return 1.0 + 0.02 * jax.random.normal(k, shape, f32)

    return {
        "x": jax.random.normal(ks[0], (BATCH, SEQ, D_MODEL), f32),
        "p": jax.random.normal(ks[1], (DEPTH, BATCH, SEQ, PLE_DIM), f32),
        "norm_mix_g": gain(ks[2], (DEPTH, D_MODEL)),
        "w_in": w(ks[3], (DEPTH, D_MODEL, IN_COLS), D_MODEL),
        "hg_lb_logits": 0.1 * jax.random.normal(ks[4], (DEPTH + 1, HG_KW), f32),
        "hg_onorm_g": gain(ks[5], (DEPTH, HG_DV)),
        "fox_f_bias": jax.random.uniform(ks[6], (DEPTH, FOX_HEADS), f32, minval=1.0, maxval=4.0),
        "fox_q_norm_g": gain(ks[7], (DEPTH, FOX_DH)),
        "fox_k_norm_g": gain(ks[8], (DEPTH, FOX_DH)),
        "w_branch_a": w(ks[9], (DEPTH, HG_VW, D_MODEL), HG_VW),
        "w_branch_b": w(ks[10], (DEPTH, FOX_W, D_MODEL), FOX_W),
        "w_out": w(ks[11], (DEPTH, D_MODEL, D_MODEL), D_MODEL),
        "norm_ffn_g": gain(ks[12], (DEPTH, D_MODEL)),
        "w_ffn_gate": w(ks[13], (DEPTH, D_MODEL, D_FF), D_MODEL),
        "w_ffn_up": w(ks[14], (DEPTH, D_MODEL, D_FF), D_MODEL),
        "w_ffn_down": w(ks[15], (DEPTH, D_FF, D_MODEL), D_FF),
        "norm_ple_g": gain(ks[16], (DEPTH, D_MODEL)),
        "w_ple_gate": w(ks[17], (DEPTH, D_MODEL, D_MODEL), D_MODEL),
        "w_ple_proj": w(ks[18], (DEPTH, PLE_DIM, D_MODEL), PLE_DIM),
    }


def _fwd_reference(x, p, norm_mix_g, w_in, hg_lb_logits, hg_onorm_g, fox_f_bias, fox_q_norm_g,
              fox_k_norm_g, w_branch_a, w_branch_b, w_out, norm_ffn_g, w_ffn_gate, w_ffn_up,
              w_ffn_down, norm_ple_g, w_ple_gate, w_ple_proj):
    lower_bounds = jnp.cumsum(jax.nn.softmax(hg_lb_logits.astype(jnp.float32), axis=0), axis=0)
    for layer in range(DEPTH):
        h = rms_norm(x, norm_mix_g[layer])
        z = h @ w_in[layer]
        y_a = hgrn2_mixer(z[..., OFF_HG_Q:OFF_HG_F], z[..., OFF_HG_F:OFF_HG_I],
                          z[..., OFF_HG_I:OFF_HG_G], z[..., OFF_HG_G:OFF_FOX_Q],
                          lower_bounds[layer], hg_onorm_g[layer])
        y_b = forgetting_attention(z[..., OFF_FOX_Q:OFF_FOX_K], z[..., OFF_FOX_K:OFF_FOX_V],
                                   z[..., OFF_FOX_V:OFF_FOX_F], z[..., OFF_FOX_F:OFF_GATE],
                                   fox_f_bias[layer], fox_q_norm_g[layer], fox_k_norm_g[layer])
        gate_a = jax.nn.sigmoid(z[..., OFF_GATE:OFF_GATE + D_MODEL])
        gate_b = jax.nn.sigmoid(z[..., OFF_GATE + D_MODEL:IN_COLS])
        merged = gate_a * (y_a @ w_branch_a[layer]) + gate_b * (y_b @ w_branch_b[layer])
        x = x + merged @ w_out[layer]
        hf = rms_norm(x, norm_ffn_g[layer])
        x = x + (jax.nn.silu(hf @ w_ffn_gate[layer]) * (hf @ w_ffn_up[layer])) @ w_ffn_down[layer]
        hp = rms_norm(x, norm_ple_g[layer])
        x = x + jax.nn.sigmoid(hp @ w_ple_gate[layer]) * (p[layer] @ w_ple_proj[layer])
    return x


import jax as _jax
import jax.numpy as _jnp

TWIN_FORMAT = 'train_step'
FWD_PARAMS = ['x', 'p', 'norm_mix_g', 'w_in', 'hg_lb_logits', 'hg_onorm_g', 'fox_f_bias', 'fox_q_norm_g', 'fox_k_norm_g', 'w_branch_a', 'w_branch_b', 'w_out', 'norm_ffn_g', 'w_ffn_gate', 'w_ffn_up', 'w_ffn_down', 'norm_ple_g', 'w_ple_gate', 'w_ple_proj']
TWIN_WEIGHTS = ['norm_mix_g', 'w_in', 'hg_lb_logits', 'hg_onorm_g', 'fox_f_bias', 'fox_q_norm_g', 'fox_k_norm_g', 'w_branch_a', 'w_branch_b', 'w_out', 'norm_ffn_g', 'w_ffn_gate', 'w_ffn_up', 'w_ffn_down', 'norm_ple_g', 'w_ple_gate', 'w_ple_proj']
TWIN_DIFF_INPUT = 'x'
TWIN_INPUTS = ['x', 'p', 'norm_mix_g', 'w_in', 'hg_lb_logits', 'hg_onorm_g', 'fox_f_bias', 'fox_q_norm_g', 'fox_k_norm_g', 'w_branch_a', 'w_branch_b', 'w_out', 'norm_ffn_g', 'w_ffn_gate', 'w_ffn_up', 'w_ffn_down', 'norm_ple_g', 'w_ple_gate', 'w_ple_proj', 'loss_target', 'm_norm_mix_g', 'm_w_in', 'm_hg_lb_logits', 'm_hg_onorm_g', 'm_fox_f_bias', 'm_fox_q_norm_g', 'm_fox_k_norm_g', 'm_w_branch_a', 'm_w_branch_b', 'm_w_out', 'm_norm_ffn_g', 'm_w_ffn_gate', 'm_w_ffn_up', 'm_w_ffn_down', 'm_norm_ple_g', 'm_w_ple_gate', 'm_w_ple_proj', 'v_norm_mix_g', 'v_w_in', 'v_hg_lb_logits', 'v_hg_onorm_g', 'v_fox_f_bias', 'v_fox_q_norm_g', 'v_fox_k_norm_g', 'v_w_branch_a', 'v_w_branch_b', 'v_w_out', 'v_norm_ffn_g', 'v_w_ffn_gate', 'v_w_ffn_up', 'v_w_ffn_down', 'v_norm_ple_g', 'v_w_ple_gate', 'v_w_ple_proj']
TWIN_OUTPUTS = ['loss', 'grad_x', 'grad_norm_mix_g', 'grad_w_in', 'grad_hg_lb_logits', 'grad_hg_onorm_g', 'grad_fox_f_bias', 'grad_fox_q_norm_g', 'grad_fox_k_norm_g', 'grad_w_branch_a', 'grad_w_branch_b', 'grad_w_out', 'grad_norm_ffn_g', 'grad_w_ffn_gate', 'grad_w_ffn_up', 'grad_w_ffn_down', 'grad_norm_ple_g', 'grad_w_ple_gate', 'grad_w_ple_proj', 'delta_norm_mix_g', 'delta_w_in', 'delta_hg_lb_logits', 'delta_hg_onorm_g', 'delta_fox_f_bias', 'delta_fox_q_norm_g', 'delta_fox_k_norm_g', 'delta_w_branch_a', 'delta_w_branch_b', 'delta_w_out', 'delta_norm_ffn_g', 'delta_w_ffn_gate', 'delta_w_ffn_up', 'delta_w_ffn_down', 'delta_norm_ple_g', 'delta_w_ple_gate', 'delta_w_ple_proj', 'new_m_norm_mix_g', 'new_m_w_in', 'new_m_hg_lb_logits', 'new_m_hg_onorm_g', 'new_m_fox_f_bias', 'new_m_fox_q_norm_g', 'new_m_fox_k_norm_g', 'new_m_w_branch_a', 'new_m_w_branch_b', 'new_m_w_out', 'new_m_norm_ffn_g', 'new_m_w_ffn_gate', 'new_m_w_ffn_up', 'new_m_w_ffn_down', 'new_m_norm_ple_g', 'new_m_w_ple_gate', 'new_m_w_ple_proj', 'new_v_norm_mix_g', 'new_v_w_in', 'new_v_hg_lb_logits', 'new_v_hg_onorm_g', 'new_v_fox_f_bias', 'new_v_fox_q_norm_g', 'new_v_fox_k_norm_g', 'new_v_w_branch_a', 'new_v_w_branch_b', 'new_v_w_out', 'new_v_norm_ffn_g', 'new_v_w_ffn_gate', 'new_v_w_ffn_up', 'new_v_w_ffn_down', 'new_v_norm_ple_g', 'new_v_w_ple_gate', 'new_v_w_ple_proj']
TWIN_LEAF_KINDS = {'loss': 'loss', 'grad_x': 'grad_x', 'grad_norm_mix_g': 'grad_w', 'grad_w_in': 'grad_w', 'grad_hg_lb_logits': 'grad_w', 'grad_hg_onorm_g': 'grad_w', 'grad_fox_f_bias': 'grad_w', 'grad_fox_q_norm_g': 'grad_w', 'grad_fox_k_norm_g': 'grad_w', 'grad_w_branch_a': 'grad_w', 'grad_w_branch_b': 'grad_w', 'grad_w_out': 'grad_w', 'grad_norm_ffn_g': 'grad_w', 'grad_w_ffn_gate': 'grad_w', 'grad_w_ffn_up': 'grad_w', 'grad_w_ffn_down': 'grad_w', 'grad_norm_ple_g': 'grad_w', 'grad_w_ple_gate': 'grad_w', 'grad_w_ple_proj': 'grad_w', 'delta_norm_mix_g': 'delta_w', 'delta_w_in': 'delta_w', 'delta_hg_lb_logits': 'delta_w', 'delta_hg_onorm_g': 'delta_w', 'delta_fox_f_bias': 'delta_w', 'delta_fox_q_norm_g': 'delta_w', 'delta_fox_k_norm_g': 'delta_w', 'delta_w_branch_a': 'delta_w', 'delta_w_branch_b': 'delta_w', 'delta_w_out': 'delta_w', 'delta_norm_ffn_g': 'delta_w', 'delta_w_ffn_gate': 'delta_w', 'delta_w_ffn_up': 'delta_w', 'delta_w_ffn_down': 'delta_w', 'delta_norm_ple_g': 'delta_w', 'delta_w_ple_gate': 'delta_w', 'delta_w_ple_proj': 'delta_w', 'new_m_norm_mix_g': 'new_m', 'new_m_w_in': 'new_m', 'new_m_hg_lb_logits': 'new_m', 'new_m_hg_onorm_g': 'new_m', 'new_m_fox_f_bias': 'new_m', 'new_m_fox_q_norm_g': 'new_m', 'new_m_fox_k_norm_g': 'new_m', 'new_m_w_branch_a': 'new_m', 'new_m_w_branch_b': 'new_m', 'new_m_w_out': 'new_m', 'new_m_norm_ffn_g': 'new_m', 'new_m_w_ffn_gate': 'new_m', 'new_m_w_ffn_up': 'new_m', 'new_m_w_ffn_down': 'new_m', 'new_m_norm_ple_g': 'new_m', 'new_m_w_ple_gate': 'new_m', 'new_m_w_ple_proj': 'new_m', 'new_v_norm_mix_g': 'new_v', 'new_v_w_in': 'new_v', 'new_v_hg_lb_logits': 'new_v', 'new_v_hg_onorm_g': 'new_v', 'new_v_fox_f_bias': 'new_v', 'new_v_fox_q_norm_g': 'new_v', 'new_v_fox_k_norm_g': 'new_v', 'new_v_w_branch_a': 'new_v', 'new_v_w_branch_b': 'new_v', 'new_v_w_out': 'new_v', 'new_v_norm_ffn_g': 'new_v', 'new_v_w_ffn_gate': 'new_v', 'new_v_w_ffn_up': 'new_v', 'new_v_w_ffn_down': 'new_v', 'new_v_norm_ple_g': 'new_v', 'new_v_w_ple_gate': 'new_v', 'new_v_w_ple_proj': 'new_v'}


def _forward(args):
    return _fwd_reference(*[args[k] for k in FWD_PARAMS])


def _output_shape():
    out = _jax.eval_shape(lambda: _forward(_fwd_setup_inputs(0)))
    return out.shape, out.dtype

N_MICROBATCH = 1
ADAM_LR = 0.001
ADAM_B1 = 0.9
ADAM_B2 = 0.999
ADAM_EPS = 1e-08
ADAM_WD = 0.01
ADAM_STEP = 10
PER_EXAMPLE_BATCH_AXIS = {'x': 0, 'p': 1, 'loss_target': 0}
SHARED_INPUTS = []
_WEIGHT_DTYPES = {'norm_mix_g': _jnp.float32, 'w_in': _jnp.float32, 'hg_lb_logits': _jnp.float32, 'hg_onorm_g': _jnp.float32, 'fox_f_bias': _jnp.float32, 'fox_q_norm_g': _jnp.float32, 'fox_k_norm_g': _jnp.float32, 'w_branch_a': _jnp.float32, 'w_branch_b': _jnp.float32, 'w_out': _jnp.float32, 'norm_ffn_g': _jnp.float32, 'w_ffn_gate': _jnp.float32, 'w_ffn_up': _jnp.float32, 'w_ffn_down': _jnp.float32, 'norm_ple_g': _jnp.float32, 'w_ple_gate': _jnp.float32, 'w_ple_proj': _jnp.float32}
MOMENT_SCALE = {'norm_mix_g': 5.436677e+00, 'w_in': 1.231666e-01, 'hg_lb_logits': 2.060013e-02, 'hg_onorm_g': 2.664432e+01, 'fox_f_bias': 4.611451e+01, 'fox_q_norm_g': 6.752724e+00, 'fox_k_norm_g': 6.775776e+00, 'w_branch_a': 1.818643e-01, 'w_branch_b': 9.154144e-02, 'w_out': 1.976551e-01, 'norm_ffn_g': 2.473897e+01, 'w_ffn_gate': 1.570534e-01, 'w_ffn_up': 1.648784e-01, 'w_ffn_down': 2.532866e-01, 'norm_ple_g': 9.503609e-01, 'w_ple_gate': 7.549360e-02, 'w_ple_proj': 4.199971e-01}


def _to_microbatches(a, axis):
    t = _jnp.moveaxis(a, axis, 0)
    t = t.reshape((N_MICROBATCH, t.shape[0] // N_MICROBATCH) + t.shape[1:])
    return _jnp.moveaxis(t, 1, axis + 1)


def setup_inputs(seed: int = 0) -> dict:
    inp = _fwd_setup_inputs(seed)
    key = _jax.random.fold_in(_jax.random.key(seed), 7919)
    shape, _ = _output_shape()
    out = dict(inp)
    out["loss_target"] = _jax.random.normal(_jax.random.fold_in(key, 0), shape, _jnp.float32)
    for i, name in enumerate(TWIN_WEIGHTS):
        w = inp[name].astype(_jnp.float32)
        if MOMENT_SCALE is None:
            s = _jnp.sqrt(_jnp.mean(_jnp.square(w)) + 1e-30)
        else:
            s = MOMENT_SCALE[name]
        km, kv = _jax.random.split(_jax.random.fold_in(key, i + 1))
        out[name] = w
        out["m_" + name] = s * _jax.random.normal(km, w.shape, _jnp.float32)
        out["v_" + name] = (s * s) * _jax.random.uniform(kv, w.shape, _jnp.float32, 0.5, 1.5)
    if N_MICROBATCH > 1:
        for name, axis in PER_EXAMPLE_BATCH_AXIS.items():
            out[name] = _to_microbatches(out[name], axis)
    return {'x': out['x'], 'p': out['p'], 'norm_mix_g': out['norm_mix_g'], 'w_in': out['w_in'], 'hg_lb_logits': out['hg_lb_logits'], 'hg_onorm_g': out['hg_onorm_g'], 'fox_f_bias': out['fox_f_bias'], 'fox_q_norm_g': out['fox_q_norm_g'], 'fox_k_norm_g': out['fox_k_norm_g'], 'w_branch_a': out['w_branch_a'], 'w_branch_b': out['w_branch_b'], 'w_out': out['w_out'], 'norm_ffn_g': out['norm_ffn_g'], 'w_ffn_gate': out['w_ffn_gate'], 'w_ffn_up': out['w_ffn_up'], 'w_ffn_down': out['w_ffn_down'], 'norm_ple_g': out['norm_ple_g'], 'w_ple_gate': out['w_ple_gate'], 'w_ple_proj': out['w_ple_proj'], 'loss_target': out['loss_target'], 'm_norm_mix_g': out['m_norm_mix_g'], 'm_w_in': out['m_w_in'], 'm_hg_lb_logits': out['m_hg_lb_logits'], 'm_hg_onorm_g': out['m_hg_onorm_g'], 'm_fox_f_bias': out['m_fox_f_bias'], 'm_fox_q_norm_g': out['m_fox_q_norm_g'], 'm_fox_k_norm_g': out['m_fox_k_norm_g'], 'm_w_branch_a': out['m_w_branch_a'], 'm_w_branch_b': out['m_w_branch_b'], 'm_w_out': out['m_w_out'], 'm_norm_ffn_g': out['m_norm_ffn_g'], 'm_w_ffn_gate': out['m_w_ffn_gate'], 'm_w_ffn_up': out['m_w_ffn_up'], 'm_w_ffn_down': out['m_w_ffn_down'], 'm_norm_ple_g': out['m_norm_ple_g'], 'm_w_ple_gate': out['m_w_ple_gate'], 'm_w_ple_proj': out['m_w_ple_proj'], 'v_norm_mix_g': out['v_norm_mix_g'], 'v_w_in': out['v_w_in'], 'v_hg_lb_logits': out['v_hg_lb_logits'], 'v_hg_onorm_g': out['v_hg_onorm_g'], 'v_fox_f_bias': out['v_fox_f_bias'], 'v_fox_q_norm_g': out['v_fox_q_norm_g'], 'v_fox_k_norm_g': out['v_fox_k_norm_g'], 'v_w_branch_a': out['v_w_branch_a'], 'v_w_branch_b': out['v_w_branch_b'], 'v_w_out': out['v_w_out'], 'v_norm_ffn_g': out['v_norm_ffn_g'], 'v_w_ffn_gate': out['v_w_ffn_gate'], 'v_w_ffn_up': out['v_w_ffn_up'], 'v_w_ffn_down': out['v_w_ffn_down'], 'v_norm_ple_g': out['v_norm_ple_g'], 'v_w_ple_gate': out['v_w_ple_gate'], 'v_w_ple_proj': out['v_w_ple_proj']}


def _loss(weights, diff, rest, loss_target):
    with _jax.named_scope("forward"):
        args = {**rest, TWIN_DIFF_INPUT: diff, **{k: w.astype(_WEIGHT_DTYPES[k]) for k, w in weights.items()}}
        y = _forward(args)
    with _jax.named_scope("loss_head"):
        err = _jnp.square(y.astype(_jnp.float32) - loss_target)
        return 0.5 * _jnp.sum(_jnp.mean(err, axis=-1)) if err.ndim else 0.5 * err


def _adamw(w, g, m, v):
    m = ADAM_B1 * m + (1.0 - ADAM_B1) * g
    v = ADAM_B2 * v + (1.0 - ADAM_B2) * _jnp.square(g)
    m_hat = m / (1.0 - ADAM_B1 ** ADAM_STEP)
    v_hat = v / (1.0 - ADAM_B2 ** ADAM_STEP)
    delta = -ADAM_LR * (m_hat / (_jnp.sqrt(v_hat) + ADAM_EPS) + ADAM_WD * w)
    return delta, m, v


def reference(x, p, norm_mix_g, w_in, hg_lb_logits, hg_onorm_g, fox_f_bias, fox_q_norm_g, fox_k_norm_g, w_branch_a, w_branch_b, w_out, norm_ffn_g, w_ffn_gate, w_ffn_up, w_ffn_down, norm_ple_g, w_ple_gate, w_ple_proj, loss_target, m_norm_mix_g, m_w_in, m_hg_lb_logits, m_hg_onorm_g, m_fox_f_bias, m_fox_q_norm_g, m_fox_k_norm_g, m_w_branch_a, m_w_branch_b, m_w_out, m_norm_ffn_g, m_w_ffn_gate, m_w_ffn_up, m_w_ffn_down, m_norm_ple_g, m_w_ple_gate, m_w_ple_proj, v_norm_mix_g, v_w_in, v_hg_lb_logits, v_hg_onorm_g, v_fox_f_bias, v_fox_q_norm_g, v_fox_k_norm_g, v_w_branch_a, v_w_branch_b, v_w_out, v_norm_ffn_g, v_w_ffn_gate, v_w_ffn_up, v_w_ffn_down, v_norm_ple_g, v_w_ple_gate, v_w_ple_proj):
    given = dict(x=x, p=p, norm_mix_g=norm_mix_g, w_in=w_in, hg_lb_logits=hg_lb_logits, hg_onorm_g=hg_onorm_g, fox_f_bias=fox_f_bias, fox_q_norm_g=fox_q_norm_g, fox_k_norm_g=fox_k_norm_g, w_branch_a=w_branch_a, w_branch_b=w_branch_b, w_out=w_out, norm_ffn_g=norm_ffn_g, w_ffn_gate=w_ffn_gate, w_ffn_up=w_ffn_up, w_ffn_down=w_ffn_down, norm_ple_g=norm_ple_g, w_ple_gate=w_ple_gate, w_ple_proj=w_ple_proj, loss_target=loss_target, m_norm_mix_g=m_norm_mix_g, m_w_in=m_w_in, m_hg_lb_logits=m_hg_lb_logits, m_hg_onorm_g=m_hg_onorm_g, m_fox_f_bias=m_fox_f_bias, m_fox_q_norm_g=m_fox_q_norm_g, m_fox_k_norm_g=m_fox_k_norm_g, m_w_branch_a=m_w_branch_a, m_w_branch_b=m_w_branch_b, m_w_out=m_w_out, m_norm_ffn_g=m_norm_ffn_g, m_w_ffn_gate=m_w_ffn_gate, m_w_ffn_up=m_w_ffn_up, m_w_ffn_down=m_w_ffn_down, m_norm_ple_g=m_norm_ple_g, m_w_ple_gate=m_w_ple_gate, m_w_ple_proj=m_w_ple_proj, v_norm_mix_g=v_norm_mix_g, v_w_in=v_w_in, v_hg_lb_logits=v_hg_lb_logits, v_hg_onorm_g=v_hg_onorm_g, v_fox_f_bias=v_fox_f_bias, v_fox_q_norm_g=v_fox_q_norm_g, v_fox_k_norm_g=v_fox_k_norm_g, v_w_branch_a=v_w_branch_a, v_w_branch_b=v_w_branch_b, v_w_out=v_w_out, v_norm_ffn_g=v_norm_ffn_g, v_w_ffn_gate=v_w_ffn_gate, v_w_ffn_up=v_w_ffn_up, v_w_ffn_down=v_w_ffn_down, v_norm_ple_g=v_norm_ple_g, v_w_ple_gate=v_w_ple_gate, v_w_ple_proj=v_w_ple_proj)
    weights = {n: given[n] for n in TWIN_WEIGHTS}
    shared = {n: given[n] for n in SHARED_INPUTS}
    per_example = {n: given[n] for n in ['x', 'p']}
    grad_fn = _jax.value_and_grad(_loss, argnums=(0, 1))

    def one_microbatch(ex, loss_target):
        ex = dict(ex)
        diff = ex.pop(TWIN_DIFF_INPUT)
        return grad_fn(weights, diff, {**shared, **ex}, loss_target)

    if N_MICROBATCH == 1:
        loss, (grad_w, grad_x) = one_microbatch(per_example, given["loss_target"])
    else:
        def body(carry, xs):
            loss_sum, grad_sum = carry
            l_k, (gw_k, gx_k) = one_microbatch(xs[0], xs[1])
            with _jax.named_scope("update"):
                return (loss_sum + l_k, _jax.tree.map(_jnp.add, grad_sum, gw_k)), gx_k

        init = (_jnp.zeros((), _jnp.float32), _jax.tree.map(_jnp.zeros_like, weights))
        (loss, grad_w), grad_x = _jax.lax.scan(body, init, (per_example, given["loss_target"]))
    with _jax.named_scope("update"):
        delta_w, new_m, new_v = {}, {}, {}
        for n in TWIN_WEIGHTS:
            delta_w[n], new_m[n], new_v[n] = _adamw(weights[n], grad_w[n], given["m_" + n], given["v_" + n])
    return (loss, grad_x, *[grad_w[n] for n in TWIN_WEIGHTS], *[delta_w[n] for n in TWIN_WEIGHTS],
            *[new_m[n] for n in TWIN_WEIGHTS], *[new_v[n] for n in TWIN_WEIGHTS])
```

```python
import functools

import jax
import jax.numpy as jnp
from jax import lax
from jax.experimental import pallas as pl
from jax.experimental.pallas import tpu as pltpu

F32 = jnp.float32
BF16 = jnp.bfloat16

D_MODEL = 1024
PLE_DIM = 256
HG_HEADS = 4
HG_DK = 128
HG_CHUNK = 64
HG_SUB = 16
HG_W = HG_HEADS * HG_DK
FOX_HEADS = 8
FOX_DH = 64
FOX_W = FOX_HEADS * FOX_DH
D_FF = 2816
EPS = 1e-6
N_DEV = 8
LANES = 128
FOX_COLS = 3 * FOX_W + LANES
HG_COLS = 4 * HG_W
GATE_COLS = 2 * D_MODEL
IN_COLS = HG_COLS + 3 * FOX_W + FOX_HEADS + GATE_COLS
EXP_CLAMP = 80.0

ADAM_LR = 0.001
ADAM_B1 = 0.9
ADAM_B2 = 0.999
ADAM_EPS = 1e-08
ADAM_WD = 0.01
ADAM_STEP = 10

MESH = pl.DeviceIdType.MESH
VMEM_LIMIT = 56 * 1024 * 1024
ROW_BLOCK = 512

BIG = ["w_in", "w_branch_a", "w_branch_b", "w_out", "w_ffn_gate", "w_ffn_up", "w_ffn_down",
       "w_ple_gate", "w_ple_proj"]
BIG_SHAPE = {
    "w_in": (D_MODEL, IN_COLS, 1), "w_branch_a": (HG_W, D_MODEL, 1), "w_branch_b": (FOX_W, D_MODEL, 1),
    "w_out": (D_MODEL, D_MODEL, 0), "w_ffn_gate": (D_MODEL, D_FF, 1), "w_ffn_up": (D_MODEL, D_FF, 1),
    "w_ffn_down": (D_FF, D_MODEL, 0), "w_ple_gate": (D_MODEL, D_MODEL, 0), "w_ple_proj": (PLE_DIM, D_MODEL, 1),
}
PACK_ROWS = {n: (r * c) // (N_DEV * LANES) for n, (r, c, _) in BIG_SHAPE.items()}
PACK_BLOCK = 2560
PACK_TOTAL = -(-sum(PACK_ROWS.values()) // PACK_BLOCK) * PACK_BLOCK

SMALL = ["norm_mix_g", "hg_lb_logits", "hg_onorm_g", "fox_f_bias", "fox_q_norm_g", "fox_k_norm_g",
         "norm_ffn_g", "norm_ple_g"]
SMALL_ROWS = {"norm_mix_g": 8, "hg_lb_logits": 8, "hg_onorm_g": 1, "fox_f_bias": 1, "fox_q_norm_g": 1,
              "fox_k_norm_g": 1, "norm_ffn_g": 8, "norm_ple_g": 8}
SMALL_TOTAL = 40
LOSS_ROW = 36


def _pallas(body, **kw):
    return pl.pallas_call(body, **kw)


def _params(**kw):
    return pltpu.CompilerParams(vmem_limit_bytes=VMEM_LIMIT, **kw)


def _pick(n, target):
    if n <= target:
        return n
    best = None
    for t in range(LANES, target + 1, LANES):
        if n % t == 0:
            best = t
    assert best is not None, (n, target)
    return best


def _dot(a, b, ca, cb):
    return lax.dot_general(a, b, (((ca,), (cb,)), ((), ())), preferred_element_type=F32)


def _split_dot(mat, x, ca, cb, terms=2, mat_first=True):
    acc = None
    rem = x
    for _ in range(terms):
        part = rem.astype(BF16)
        rem = rem - part.astype(F32)
        p = _dot(mat, part, ca, cb) if mat_first else _dot(part, mat, ca, cb)
        acc = p if acc is None else acc + p
    return acc


def _sigmoid(x):
    return 1.0 / (1.0 + jnp.exp(-x))


def _iota(shape, dim):
    return lax.broadcasted_iota(jnp.int32, shape, dim)


def _matmul(a, b, *, name, ta=False, tb=False, out_dtype=F32, add=None):
    (K, M) = a.shape if ta else a.shape[::-1]
    (N, Kb) = b.shape if tb else b.shape[::-1]
    assert K == Kb, (a.shape, b.shape, ta, tb)
    if ta:
        tm, tn, tk = _pick(M, ROW_BLOCK), _pick(N, 2048), _pick(K, ROW_BLOCK)
    else:
        tm, tn, tk = _pick(M, ROW_BLOCK), _pick(N, 2048), _pick(K, 2048)
    nk = K // tk

    def body(*refs):
        if add is None:
            a_ref, b_ref, o_ref, acc_ref = refs
        else:
            a_ref, b_ref, add_ref, o_ref, acc_ref = refs
        k = pl.program_id(2)
        p = _dot(a_ref[...].astype(BF16), b_ref[...].astype(BF16), 0 if ta else 1, 1 if tb else 0)

        @pl.when(k == 0)
        def _():
            acc_ref[...] = p

        @pl.when(k > 0)
        def _():
            acc_ref[...] += p

        @pl.when(k == nk - 1)
        def _():
            r = acc_ref[...]
            if add is not None:
                r = r + add_ref[...].astype(F32)
            o_ref[...] = r.astype(out_dtype)

    a_spec = pl.BlockSpec((tk, tm), lambda i, j, k: (k, i)) if ta else pl.BlockSpec((tm, tk), lambda i, j, k: (i, k))
    b_spec = pl.BlockSpec((tn, tk), lambda i, j, k: (j, k)) if tb else pl.BlockSpec((tk, tn), lambda i, j, k: (k, j))
    o_spec = pl.BlockSpec((tm, tn), lambda i, j, k: (i, j))
    in_specs = [a_spec, b_spec] + ([o_spec] if add is not None else [])
    args = (a, b) + ((add,) if add is not None else ())
    return _pallas(
        body, name=name, grid=(M // tm, N // tn, nk), in_specs=in_specs, out_specs=o_spec,
        out_shape=jax.ShapeDtypeStruct((M, N), out_dtype), scratch_shapes=[pltpu.VMEM((tm, tn), F32)],
        compiler_params=_params(dimension_semantics=("arbitrary",) * 3),
    )(*args)


def _row_call(body, *, name, T, ins, outs, acc_outs=(), tm=ROW_BLOCK, reverse=False):
    tm = min(tm, T)
    nb = T // tm
    rmap = (lambda i: (nb - 1 - i, 0)) if reverse else (lambda i: (i, 0))
    in_specs, args = [], []
    for arr, blocked in ins:
        args.append(arr)
        if blocked:
            in_specs.append(pl.BlockSpec((tm, arr.shape[1]), rmap))
        else:
            in_specs.append(pl.BlockSpec(arr.shape, lambda i, _n=arr.ndim: (0,) * _n))
    out_specs, out_shape = [], []
    for c, dt in outs:
        out_specs.append(pl.BlockSpec((tm, c), rmap))
        out_shape.append(jax.ShapeDtypeStruct((T, c), dt))
    for shp, dt in acc_outs:
        out_specs.append(pl.BlockSpec(shp, lambda i, _n=len(shp): (0,) * _n))
        out_shape.append(jax.ShapeDtypeStruct(shp, dt))
    return _pallas(body, name=name, grid=(nb,), in_specs=in_specs, out_specs=out_specs, out_shape=out_shape,
                   compiler_params=_params(dimension_semantics=("arbitrary",)))(*args)


def _rms_fwd(x, g, *, name):
    T = x.shape[0]

    def body(x_ref, g_ref, h_ref):
        xv = x_ref[...]
        rstd = lax.rsqrt(jnp.mean(xv * xv, axis=-1, keepdims=True) + EPS)
        h_ref[...] = (xv * rstd * g_ref[...]).astype(BF16)

    return _row_call(body, name=name, T=T, ins=[(x, True), (g, False)], outs=[(D_MODEL, BF16)])[0]


def _rms_bwd(x, g, dh, dres, *, name):
    T = x.shape[0]

    def body(x_ref, g_ref, dh_ref, dres_ref, dx_ref, dg_ref):
        xv = x_ref[...]
        rstd = lax.rsqrt(jnp.mean(xv * xv, axis=-1, keepdims=True) + EPS)
        xh = xv * rstd
        dhv = dh_ref[...]
        part = jnp.sum(dhv * xh, axis=0, keepdims=True)

        @pl.when(pl.program_id(0) == 0)
        def _():
            dg_ref[...] = part

        @pl.when(pl.program_id(0) > 0)
        def _():
            dg_ref[...] += part

        dxh = dhv * g_ref[...]
        dx_ref[...] = rstd * (dxh - xh * jnp.mean(dxh * xh, axis=-1, keepdims=True)) + dres_ref[...]

    return _row_call(body, name=name, T=T, ins=[(x, True), (g, False), (dh, True), (dres, True)],
                     outs=[(D_MODEL, F32)], acc_outs=[((1, D_MODEL), F32)])


def _merge_fwd(ua, ub, zg):
    def body(ua_ref, ub_ref, zg_ref, m_ref):
        ga = _sigmoid(zg_ref[:, :D_MODEL])
        gb = _sigmoid(zg_ref[:, D_MODEL:])
        m_ref[...] = (ga * ua_ref[...] + gb * ub_ref[...]).astype(BF16)

    return _row_call(body, name="merge_fwd", T=ua.shape[0], ins=[(ua, True), (ub, True), (zg, True)],
                     outs=[(D_MODEL, BF16)])[0]


def _merge_bwd(dm, ua, ub, zg):
    def body(dm_ref, ua_ref, ub_ref, zg_ref, dua_ref, dub_ref, dzg_ref):
        ga = _sigmoid(zg_ref[:, :D_MODEL])
        gb = _sigmoid(zg_ref[:, D_MODEL:])
        dmv = dm_ref[...]
        dua_ref[...] = (dmv * ga).astype(BF16)
        dub_ref[...] = (dmv * gb).astype(BF16)
        dzg_ref[:, :D_MODEL] = (dmv * ua_ref[...] * ga * (1.0 - ga)).astype(BF16)
        dzg_ref[:, D_MODEL:] = (dmv * ub_ref[...] * gb * (1.0 - gb)).astype(BF16)

    return _row_call(body, name="merge_bwd", T=dm.shape[0], ins=[(dm, True), (ua, True), (ub, True), (zg, True)],
                     outs=[(D_MODEL, BF16), (D_MODEL, BF16), (GATE_COLS, BF16)])


def _swiglu_fwd(a, b):
    def body(a_ref, b_ref, o_ref):
        av = a_ref[...].astype(F32)
        o_ref[...] = (av * _sigmoid(av) * b_ref[...].astype(F32)).astype(BF16)

    return _row_call(body, name="swiglu_fwd", T=a.shape[0], ins=[(a, True), (b, True)], outs=[(D_FF, BF16)])[0]


def _swiglu_bwd(a, b, dact):
    def body(a_ref, b_ref, d_ref, da_ref, db_ref):
        av = a_ref[...].astype(F32)
        bv = b_ref[...].astype(F32)
        dv = d_ref[...]
        sg = _sigmoid(av)
        da_ref[...] = (dv * bv * sg * (1.0 + av * (1.0 - sg))).astype(BF16)
        db_ref[...] = (dv * av * sg).astype(BF16)

    return _row_call(body, name="swiglu_bwd", T=a.shape[0], ins=[(a, True), (b, True), (dact, True)],
                     outs=[(D_FF, BF16), (D_FF, BF16)])


def _ple_loss(x2, sp, pp, tgt):
    def body(x_ref, sp_ref, pp_ref, t_ref, dy_ref, dsp_ref, dpp_ref, loss_ref):
        gp = _sigmoid(sp_ref[...])
        ppv = pp_ref[...]
        err = x_ref[...] + gp * ppv - t_ref[...]
        part = 0.5 * jnp.sum(jnp.mean(err * err, axis=-1, keepdims=True), axis=0, keepdims=True)
        part = jnp.broadcast_to(part, loss_ref.shape)

        @pl.when(pl.program_id(0) == 0)
        def _():
            loss_ref[...] = part

        @pl.when(pl.program_id(0) > 0)
        def _():
            loss_ref[...] += part

        dy = err * (1.0 / D_MODEL)
        dy_ref[...] = dy
        dsp_ref[...] = (dy * ppv * gp * (1.0 - gp)).astype(BF16)
        dpp_ref[...] = (dy * gp).astype(BF16)

    return _row_call(body, name="ple_loss", T=x2.shape[0], ins=[(x2, True), (sp, True), (pp, True), (tgt, True)],
                     outs=[(D_MODEL, F32), (D_MODEL, BF16), (D_MODEL, BF16)], acc_outs=[((8, LANES), F32)])


def _hg_consts():
    C = HG_CHUNK
    r, c = _iota((C, C), 0), _iota((C, C), 1)
    tri = (c <= r)
    same = (r // HG_SUB) == (c // HG_SUB)
    return tri, (tri & same)


def _hg_chunk_fwd(q, f, lb, tri_b, sub_b):
    sgq = _sigmoid(q)
    qt = q * sgq
    sg = _sigmoid(f)
    fg = lb + (1.0 - lb) * sg
    kf = (1.0 - lb) * (1.0 - sg)
    logf = jnp.log(fg)
    b = _split_dot(tri_b, logf, 1, 0)
    w = _split_dot(sub_b, logf, 1, 0)
    return sgq, qt, sg, fg, kf, b, w


def _hg_scores(qs_b, kf, b, row):
    C, S = HG_CHUNK, HG_SUB
    parts, ks = [], []
    for blk in range(C // S):
        ref = jnp.zeros_like(b[0:1]) if blk == 0 else b[blk * S - 1:blk * S]
        e = jnp.exp(jnp.minimum(ref - b, EXP_CLAMP))
        e = jnp.where(row < (blk + 1) * S, e, 0.0)
        k_b = (kf * e).astype(BF16)
        ks.append((e, k_b))
        parts.append(_dot(qs_b[blk * S:(blk + 1) * S], k_b, 1, 1))
    return jnp.concatenate(parts, axis=0), ks


def _hgrn_fwd(z, lb_logits, gain):
    T = z.shape[0]
    RB = min(ROW_BLOCK, T)
    nb, cpb = T // RB, RB // HG_CHUNK
    C, DK = HG_CHUNK, HG_DK

    def body(z_ref, lg_ref, g_ref, o_ref, y_ref, st_ref, s_ref):
        @pl.when(pl.program_id(0) == 0)
        def _():
            s_ref[...] = jnp.zeros_like(s_ref)

        lg = lg_ref[...]
        lb_all = 1.0 / (1.0 + jnp.exp(lg[1:2] - lg[0:1]))
        gain_v = g_ref[...]
        tri, sub = _hg_consts()
        tri_b, sub_b = tri.astype(BF16), sub.astype(BF16)
        row = _iota((C, DK), 0)

        def chunk(ci, carry):
            r0 = pl.multiple_of(ci * C, C)
            rows = pl.ds(r0, C)
            for h in range(HG_HEADS):
                cs = slice(h * DK, (h + 1) * DK)
                q = z_ref[rows, cs]
                f = z_ref[rows, HG_W + h * DK:HG_W + (h + 1) * DK]
                v = z_ref[rows, 2 * HG_W + h * DK:2 * HG_W + (h + 1) * DK]
                g = z_ref[rows, 3 * HG_W + h * DK:3 * HG_W + (h + 1) * DK]
                lb = lb_all[:, cs]
                _, qt, _, _, kf, b, w = _hg_chunk_fwd(q, f, lb, tri_b, sub_b)
                st = s_ref[h]
                st_ref[pl.ds(pl.multiple_of((ci * HG_HEADS + h) * DK, DK), DK), :] = st
                v_b = v.astype(BF16)
                qs_b = (qt * jnp.exp(w)).astype(BF16)
                a, _ = _hg_scores(qs_b, kf, b, row)
                a = jnp.where(tri, a, 0.0)
                qd = qt * jnp.exp(b)
                o = _dot(qd.astype(BF16), st.astype(BF16), 1, 1) + _dot(a.astype(BF16), v_b, 1, 0)
                bl = b[C - 1:C]
                kd = kf * jnp.exp(bl - b)
                s_ref[h] = st * jnp.exp(bl) + _dot(v_b, kd.astype(BF16), 0, 0)
                o_ref[rows, cs] = o
                rstd = lax.rsqrt(jnp.mean(o * o, axis=-1, keepdims=True) + EPS)
                y_ref[rows, cs] = (o * rstd * gain_v * (g * _sigmoid(g))).astype(BF16)
            return carry

        lax.fori_loop(0, cpb, chunk, 0)

    return _pallas(
        body, name="hgrn_fwd", grid=(nb,),
        in_specs=[pl.BlockSpec((RB, HG_COLS), lambda i: (i, 0)), pl.BlockSpec((2, HG_W), lambda i: (0, 0)),
                  pl.BlockSpec((1, DK), lambda i: (0, 0))],
        out_specs=[pl.BlockSpec((RB, HG_W), lambda i: (i, 0)), pl.BlockSpec((RB, HG_W), lambda i: (i, 0)),
                   pl.BlockSpec((cpb * HG_HEADS * DK, DK), lambda i: (i, 0))],
        out_shape=[jax.ShapeDtypeStruct((T, HG_W), F32), jax.ShapeDtypeStruct((T, HG_W), BF16),
                   jax.ShapeDtypeStruct((T // C * HG_HEADS * DK, DK), F32)],
        scratch_shapes=[pltpu.VMEM((HG_HEADS, DK, DK), F32)],
        compiler_params=_params(dimension_semantics=("arbitrary",)),
    )(z, lb_logits, gain)


def _hgrn_bwd(z, o_raw, dy, states, lb_logits, gain):
    T = z.shape[0]
    RB = min(ROW_BLOCK, T)
    nb, cpb = T // RB, RB // HG_CHUNK
    C, DK, S = HG_CHUNK, HG_DK, HG_SUB

    def body(z_ref, o_ref, dy_ref, st_ref, lg_ref, g_ref, dz_ref, dlg_ref, dg_ref, ds_ref, dlb_ref):
        step = pl.program_id(0)

        @pl.when(step == 0)
        def _():
            ds_ref[...] = jnp.zeros_like(ds_ref)
            dlb_ref[...] = jnp.zeros_like(dlb_ref)
            dg_ref[...] = jnp.zeros_like(dg_ref)

        lg = lg_ref[...]
        lb_all = 1.0 / (1.0 + jnp.exp(lg[1:2] - lg[0:1]))
        gain_v = g_ref[...]
        tri, sub = _hg_consts()
        tri_b, sub_b = tri.astype(BF16), sub.astype(BF16)
        row = _iota((C, DK), 0)

        def chunk(cj, carry):
            ci = cpb - 1 - cj
            r0 = pl.multiple_of(ci * C, C)
            rows = pl.ds(r0, C)
            for h in range(HG_HEADS):
                cs = slice(h * DK, (h + 1) * DK)
                q = z_ref[rows, cs]
                f = z_ref[rows, HG_W + h * DK:HG_W + (h + 1) * DK]
                v = z_ref[rows, 2 * HG_W + h * DK:2 * HG_W + (h + 1) * DK]
                g = z_ref[rows, 3 * HG_W + h * DK:3 * HG_W + (h + 1) * DK]
                lb = lb_all[:, cs]
                sgq, qt, sg, fg, kf, b, w = _hg_chunk_fwd(q, f, lb, tri_b, sub_b)
                st = st_ref[pl.ds(pl.multiple_of((ci * HG_HEADS + h) * DK, DK), DK), :]
                dst = ds_ref[h]
                o = o_ref[rows, cs]
                dyv = dy_ref[rows, cs]
                rstd = lax.rsqrt(jnp.mean(o * o, axis=-1, keepdims=True) + EPS)
                n = o * rstd
                sgg = _sigmoid(g)
                t1 = dyv * (g * sgg)
                dg_ref[...] += jnp.sum(t1 * n, axis=0, keepdims=True)
                dn = t1 * gain_v
                do = rstd * (dn - n * jnp.mean(dn * n, axis=-1, keepdims=True))
                dgate = dyv * n * gain_v * sgg * (1.0 + g * (1.0 - sgg))
                do_b = do.astype(BF16)
                v_b = v.astype(BF16)
                ew = jnp.exp(w)
                qs_b = (qt * ew).astype(BF16)
                a, ks = _hg_scores(qs_b, kf, b, row)
                a = jnp.where(tri, a, 0.0)
                eb = jnp.exp(b)
                qd = qt * eb
                bl = b[C - 1:C]
                ebl = jnp.exp(bl)
                ekd = jnp.exp(bl - b)
                kd = kf * ekd
                kd_b = kd.astype(BF16)
                qd_b = qd.astype(BF16)
                dst_b = dst.astype(BF16)
                dqd = _dot(do_b, st.astype(BF16), 1, 0)
                da = jnp.where(tri, _dot(do_b, v_b, 1, 1), 0.0)
                dv = _dot(a.astype(BF16), do_b, 0, 0) + _dot(kd_b, dst_b, 1, 1)
                dkd = _dot(v_b, dst_b, 1, 0)
                ds_ref[h] = dst * ebl + _dot(do_b, qd_b, 0, 0)
                dkd_kd = dkd * kd_b.astype(F32)
                dbl = ebl * jnp.sum(dst * st, axis=0, keepdims=True) + jnp.sum(dkd_kd, axis=0, keepdims=True)
                da_b = da.astype(BF16)
                dqs_parts = []
                dk_in = jnp.zeros((C, DK), F32)
                db_k = jnp.zeros((C, DK), F32)
                for blk in range(C // S):
                    e, k_b = ks[blk]
                    da_blk = da_b[blk * S:(blk + 1) * S]
                    dqs_parts.append(_dot(da_blk, k_b, 1, 0))
                    dks = _dot(da_blk, qs_b[blk * S:(blk + 1) * S], 0, 0)
                    dk_in = dk_in + dks * e
                    db_k = db_k + dks * k_b.astype(F32)
                dqs = jnp.concatenate(dqs_parts, axis=0)
                dqt_in = dqs * ew
                db = qs_b.astype(F32) * dqs - db_k + dqd * qd_b.astype(F32) - dkd_kd
                db = db + jnp.where(row == C - 1, dbl, 0.0)
                dlogf = _split_dot(tri_b, db, 0, 0)
                dqt = dqt_in + dqd * eb
                dkf = dk_in + dkd * ekd
                dfg = dlogf / fg - dkf
                dlb_ref[:, cs] += jnp.sum(dfg * (1.0 - sg), axis=0, keepdims=True)
                dz_ref[rows, cs] = (dqt * sgq * (1.0 + q * (1.0 - sgq))).astype(BF16)
                dz_ref[rows, HG_W + h * DK:HG_W + (h + 1) * DK] = (dfg * (1.0 - lb) * sg * (1.0 - sg)).astype(BF16)
                dz_ref[rows, 2 * HG_W + h * DK:2 * HG_W + (h + 1) * DK] = dv.astype(BF16)
                dz_ref[rows, 3 * HG_W + h * DK:3 * HG_W + (h + 1) * DK] = dgate.astype(BF16)
            return carry

        lax.fori_loop(0, cpb, chunk, 0)

        @pl.when(step == nb - 1)
        def _():
            d0 = dlb_ref[...] * lb_all * (1.0 - lb_all)
            dlg_ref[0:1, :] = d0
            dlg_ref[1:2, :] = -d0

    rev = lambda i: (nb - 1 - i, 0)
    fix = lambda i: (0, 0)
    return _pallas(
        body, name="hgrn_bwd", grid=(nb,),
        in_specs=[pl.BlockSpec((RB, HG_COLS), rev), pl.BlockSpec((RB, HG_W), rev), pl.BlockSpec((RB, HG_W), rev),
                  pl.BlockSpec((cpb * HG_HEADS * DK, DK), rev), pl.BlockSpec((2, HG_W), fix),
                  pl.BlockSpec((1, DK), fix)],
        out_specs=[pl.BlockSpec((RB, HG_COLS), rev), pl.BlockSpec((2, HG_W), fix), pl.BlockSpec((1, DK), fix)],
        out_shape=[jax.ShapeDtypeStruct((T, HG_COLS), BF16), jax.ShapeDtypeStruct((2, HG_W), F32),
                   jax.ShapeDtypeStruct((1, DK), F32)],
        scratch_shapes=[pltpu.VMEM((HG_HEADS, DK, DK), F32), pltpu.VMEM((1, HG_W), F32)],
        compiler_params=_params(dimension_semantics=("arbitrary",)),
    )(z, o_raw, dy, states, lb_logits, gain)


def _head_ones():
    r, c = _iota((FOX_W, FOX_W), 0), _iota((FOX_W, FOX_W), 1)
    return ((r // FOX_DH) == (c // FOX_DH)).astype(BF16)


def _log_sigmoid(x):
    return jnp.minimum(x, 0.0) - jnp.log(1.0 + jnp.exp(-jnp.abs(x)))


def _fox_prep(z, bias, qg, kg):
    T = z.shape[0]
    tm = min(ROW_BLOCK, T)
    nb = T // tm

    def body(z_ref, b_ref, qg_ref, kg_ref, q_ref, k_ref, v_ref, c_ref, ct_ref, carry_ref):
        @pl.when(pl.program_id(0) == 0)
        def _():
            carry_ref[...] = jnp.zeros_like(carry_ref)

        ones = _head_ones()
        for src, g_ref, dst, scale in ((0, qg_ref, q_ref, FOX_DH ** -0.5), (1, kg_ref, k_ref, 1.0)):
            xv = z_ref[:, src * FOX_W:(src + 1) * FOX_W]
            ms = _split_dot(ones, xv * xv, 1, 0, mat_first=False) * (1.0 / FOX_DH)
            dst[...] = (xv * lax.rsqrt(ms + EPS) * (g_ref[...] * scale)).astype(BF16)
        v_ref[...] = z_ref[:, 2 * FOX_W:3 * FOX_W].astype(BF16)
        logf = _log_sigmoid(z_ref[:, 3 * FOX_W:] + b_ref[...])
        r, c = _iota((tm, tm), 0), _iota((tm, tm), 1)
        tri_b = (c <= r).astype(BF16)
        cum = _split_dot(tri_b, logf, 1, 0, terms=3) + carry_ref[...]
        carry_ref[...] = cum[tm - 1:tm]
        c_ref[...] = cum
        ct_ref[...] = cum.T

    return _pallas(
        body, name="fox_prep", grid=(nb,),
        in_specs=[pl.BlockSpec((tm, FOX_COLS), lambda i: (i, 0)), pl.BlockSpec((1, LANES), lambda i: (0, 0)),
                  pl.BlockSpec((1, FOX_W), lambda i: (0, 0)), pl.BlockSpec((1, FOX_W), lambda i: (0, 0))],
        out_specs=[pl.BlockSpec((tm, FOX_W), lambda i: (i, 0))] * 3
        + [pl.BlockSpec((tm, LANES), lambda i: (i, 0)), pl.BlockSpec((LANES, tm), lambda i: (0, i))],
        out_shape=[jax.ShapeDtypeStruct((T, FOX_W), BF16)] * 3
        + [jax.ShapeDtypeStruct((T, LANES), F32), jax.ShapeDtypeStruct((LANES, T), F32)],
        scratch_shapes=[pltpu.VMEM((1, LANES), F32)],
        compiler_params=_params(dimension_semantics=("arbitrary",)),
    )(z, bias, qg, kg)


def _fox_fwd(qn, kn, vb, c, ct):
    T = qn.shape[0]
    tq = min(ROW_BLOCK, T)
    nq = T // tq
    NEG = -1e30

    def body(q_ref, k_ref, v_ref, c_ref, ct_ref, o_ref, lse_ref, m_sc, l_sc, acc_sc):
        pr, qi, ki = pl.program_id(0), pl.program_id(1), pl.program_id(2)

        @pl.when(ki == 0)
        def _():
            m_sc[...] = jnp.full_like(m_sc, NEG)
            l_sc[...] = jnp.zeros_like(l_sc)
            acc_sc[...] = jnp.zeros_like(acc_sc)

        lane = _iota((tq, LANES), 1)

        @pl.when(ki <= qi)
        def _():
            qv, kv, vv = q_ref[...], k_ref[...], v_ref[...]
            keep = (_iota((tq, tq), 0) + (qi - ki) * tq) >= _iota((tq, tq), 1)
            for hh in range(2):
                in_head = (lane < FOX_DH) if hh == 0 else (lane >= FOX_DH)
                qm = jnp.where(in_head, qv, jnp.zeros_like(qv))
                s = _dot(qm, kv, 1, 1)
                s = s + (_pick_col(c_ref[...], pr, hh) - _pick_row(ct_ref[...], pr, hh))
                s = jnp.where(keep, s, NEG)
                m_old = m_sc[hh]
                m_new = jnp.maximum(m_old, jnp.max(s, axis=-1, keepdims=True))
                alpha = jnp.exp(m_old - m_new)
                p = jnp.exp(s - m_new)
                l_sc[hh] = alpha * l_sc[hh] + jnp.sum(p, axis=-1, keepdims=True)
                acc_sc[hh] = alpha * acc_sc[hh] + _dot(p.astype(BF16), vv, 1, 0)
                m_sc[hh] = m_new

        @pl.when(ki == qi)
        def _():
            inv0, inv1 = 1.0 / l_sc[0], 1.0 / l_sc[1]
            o_ref[...] = jnp.where(lane < FOX_DH, acc_sc[0] * inv0, acc_sc[1] * inv1)
            lse_ref[...] = jnp.where(lane < FOX_DH, m_sc[0] + jnp.log(l_sc[0]), m_sc[1] + jnp.log(l_sc[1]))

    kmap = lambda p, i, j: (jnp.minimum(i, j), p)
    return _pallas(
        body, name="fox_fwd", grid=(FOX_HEADS // 2, nq, nq),
        in_specs=[pl.BlockSpec((tq, LANES), lambda p, i, j: (i, p)), pl.BlockSpec((tq, LANES), kmap),
                  pl.BlockSpec((tq, LANES), kmap), pl.BlockSpec((tq, LANES), lambda p, i, j: (i, 0)),
                  pl.BlockSpec((LANES, tq), lambda p, i, j: (0, jnp.minimum(i, j)))],
        out_specs=[pl.BlockSpec((tq, LANES), lambda p, i, j: (i, p))] * 2,
        out_shape=[jax.ShapeDtypeStruct((T, FOX_W), F32)] * 2,
        scratch_shapes=[pltpu.VMEM((2, tq, 1), F32), pltpu.VMEM((2, tq, 1), F32), pltpu.VMEM((2, tq, LANES), F32)],
        compiler_params=_params(dimension_semantics=("arbitrary",) * 3),
    )(qn, kn, vb, c, ct)


def _pick_col(cv, pr, hh):
    lane = _iota(cv.shape, 1)
    return jnp.sum(jnp.where(lane == 2 * pr + hh, cv, 0.0), axis=1, keepdims=True)


def _pick_row(ctv, pr, hh):
    row = _iota(ctv.shape, 0)
    return jnp.sum(jnp.where(row == 2 * pr + hh, ctv, 0.0), axis=0, keepdims=True)


def _fox_bwd(qn, kn, vb, c, ct, o, do, lse):
    T = qn.shape[0]
    tq = min(ROW_BLOCK, T)
    nq = T // tq

    def body(q_ref, k_ref, v_ref, c_ref, ct_ref, o_ref, do_ref, lse_ref, dq_ref, dk_ref, dv_ref, dcs_ref, drs_ref):
        pr, ki, qi = pl.program_id(0), pl.program_id(1), pl.program_id(2)

        @pl.when(jnp.logical_and(ki == 0, qi == 0))
        def _():
            dq_ref[...] = jnp.zeros_like(dq_ref)

        @pl.when(jnp.logical_and(pr == 0, jnp.logical_and(ki == 0, qi == 0)))
        def _():
            drs_ref[...] = jnp.zeros_like(drs_ref)

        @pl.when(qi == ki)
        def _():
            dk_ref[...] = jnp.zeros_like(dk_ref)
            dv_ref[...] = jnp.zeros_like(dv_ref)
            dcs_ref[...] = jnp.zeros_like(dcs_ref)

        @pl.when(qi >= ki)
        def _():
            lane = _iota((tq, LANES), 1)
            qv, kv, vv = q_ref[...], k_ref[...], v_ref[...]
            ov, dov, lsev = o_ref[...], do_ref[...], lse_ref[...]
            keep = (_iota((tq, tq), 0) + (qi - ki) * tq) >= _iota((tq, tq), 1)
            qrows = pl.ds(pl.multiple_of(qi * tq, tq), tq)
            dq_acc = jnp.zeros((tq, LANES), F32)
            dk_acc = jnp.zeros((tq, LANES), F32)
            dv_acc = jnp.zeros((tq, LANES), F32)
            dcs_acc = jnp.zeros((8, tq), F32)
            drs_acc = jnp.zeros((tq, LANES), F32)
            prod = dov * ov
            for hh in range(2):
                in_head = (lane < FOX_DH) if hh == 0 else (lane >= FOX_DH)
                zb = jnp.zeros_like(qv)
                qm = jnp.where(in_head, qv, zb)
                km = jnp.where(in_head, kv, zb)
                dom = jnp.where(in_head, dov, 0.0).astype(BF16)
                delta = jnp.sum(jnp.where(in_head, prod, 0.0), axis=1, keepdims=True)
                lse_h = jnp.sum(jnp.where(lane == hh * FOX_DH, lsev, 0.0), axis=1, keepdims=True)
                s = _dot(qm, kv, 1, 1) + (_pick_col(c_ref[...], pr, hh) - _pick_row(ct_ref[...], pr, hh))
                p = jnp.exp(s - lse_h)
                p = jnp.where(keep, p, 0.0)
                p_b = p.astype(BF16)
                dv_acc = dv_acc + _dot(p_b, dom, 0, 0)
                dp = _dot(dom, vv, 1, 1)
                ds = p * (dp - delta)
                ds_b = ds.astype(BF16)
                dq_acc = dq_acc + _dot(ds_b, km, 1, 0)
                dk_acc = dk_acc + _dot(ds_b, qm, 0, 0)
                colsum = jnp.sum(ds, axis=0, keepdims=True)
                dcs_acc = dcs_acc + jnp.where(_iota((8, tq), 0) == hh, colsum, 0.0)
                rowsum = jnp.sum(ds, axis=1, keepdims=True)
                drs_acc = drs_acc + jnp.where(lane == 2 * pr + hh, rowsum, 0.0)
            drs_ref[qrows, :] += drs_acc
            dq_ref[qrows, :] += dq_acc
            dk_ref[...] += dk_acc
            dv_ref[...] += dv_acc
            dcs_ref[0] += dcs_acc

    qmap = lambda p, j, i: (jnp.maximum(i, j), p)
    return _pallas(
        body, name="fox_bwd", grid=(FOX_HEADS // 2, nq, nq),
        in_specs=[pl.BlockSpec((tq, LANES), qmap), pl.BlockSpec((tq, LANES), lambda p, j, i: (j, p)),
                  pl.BlockSpec((tq, LANES), lambda p, j, i: (j, p)),
                  pl.BlockSpec((tq, LANES), lambda p, j, i: (jnp.maximum(i, j), 0)),
                  pl.BlockSpec((LANES, tq), lambda p, j, i: (0, j)),
                  pl.BlockSpec((tq, LANES), qmap), pl.BlockSpec((tq, LANES), qmap), pl.BlockSpec((tq, LANES), qmap)],
        out_specs=[pl.BlockSpec((T, LANES), lambda p, j, i: (0, p)), pl.BlockSpec((tq, LANES), lambda p, j, i: (j, p)),
                   pl.BlockSpec((tq, LANES), lambda p, j, i: (j, p)), pl.BlockSpec((1, 8, tq), lambda p, j, i: (p, 0, j)),
                   pl.BlockSpec((T, LANES), lambda p, j, i: (0, 0))],
        out_shape=[jax.ShapeDtypeStruct((T, FOX_W), F32)] * 3 + [jax.ShapeDtypeStruct((FOX_HEADS // 2, 8, T), F32),
                                                                jax.ShapeDtypeStruct((T, LANES), F32)],
        compiler_params=_params(dimension_semantics=("arbitrary",) * 3),
    )(qn, kn, vb, c, ct, o, do, lse)


def _fox_post(z, dq, dk, dv, dcs, drs, bias, qg, kg):
    T = z.shape[0]
    tm = min(ROW_BLOCK, T)
    nb = T // tm

    def body(z_ref, dq_ref, dk_ref, dv_ref, dcs_ref, drs_ref, b_ref, qg_ref, kg_ref, dz_ref, dqg_ref, dkg_ref, db_ref,
             carry_ref):
        @pl.when(pl.program_id(0) == 0)
        def _():
            carry_ref[...] = jnp.zeros_like(carry_ref)
            dqg_ref[...] = jnp.zeros_like(dqg_ref)
            dkg_ref[...] = jnp.zeros_like(dkg_ref)
            db_ref[...] = jnp.zeros_like(db_ref)

        ones = _head_ones()
        for src, g_ref, d_ref, dg_ref, scale in ((0, qg_ref, dq_ref, dqg_ref, FOX_DH ** -0.5), (1, kg_ref, dk_ref, dkg_ref, 1.0)):
            xv = z_ref[:, src * FOX_W:(src + 1) * FOX_W]
            ms = _split_dot(ones, xv * xv, 1, 0, mat_first=False) * (1.0 / FOX_DH)
            rstd = lax.rsqrt(ms + EPS)
            xh = xv * rstd
            dn = d_ref[...] * scale
            dg_ref[...] += jnp.sum(dn * xh, axis=0, keepdims=True)
            dxh = dn * g_ref[...]
            mean = _split_dot(ones, dxh * xh, 1, 0, mat_first=False) * (1.0 / FOX_DH)
            dz_ref[:, src * FOX_W:(src + 1) * FOX_W] = (rstd * (dxh - xh * mean)).astype(BF16)
        dz_ref[:, 2 * FOX_W:3 * FOX_W] = dv_ref[...].astype(BF16)
        row8 = _iota((8, tm), 0)
        dct = jnp.zeros((8, tm), F32)
        for h in range(FOX_HEADS):
            src_row = dcs_ref[h // 2][h % 2:h % 2 + 1, :]
            dct = dct + jnp.where(row8 == h, src_row, 0.0)
        dct = drs_ref[...].T[0:8] - dct
        r, c = _iota((tm, tm), 0), _iota((tm, tm), 1)
        upper_b = (r >= c).astype(BF16)
        rc = _split_dot(upper_b, dct, 1, 0, mat_first=False) + carry_ref[...]
        carry_ref[...] = rc[:, 0:1]
        full = jnp.concatenate([rc, jnp.zeros((LANES - 8, tm), F32)], axis=0)
        dlogf = full.T
        xf = z_ref[:, 3 * FOX_W:] + b_ref[...]
        df = dlogf * (1.0 - _sigmoid(xf))
        dz_ref[:, 3 * FOX_W:] = df.astype(BF16)
        db_ref[...] += jnp.sum(df, axis=0, keepdims=True)

    rev = lambda i: (nb - 1 - i, 0)
    fix2 = lambda i: (0, 0)
    return _pallas(
        body, name="fox_post", grid=(nb,),
        in_specs=[pl.BlockSpec((tm, FOX_COLS), rev), pl.BlockSpec((tm, FOX_W), rev), pl.BlockSpec((tm, FOX_W), rev),
                  pl.BlockSpec((tm, FOX_W), rev), pl.BlockSpec((FOX_HEADS // 2, 8, tm), lambda i: (0, 0, nb - 1 - i)),
                  pl.BlockSpec((tm, LANES), rev),
                  pl.BlockSpec((1, LANES), fix2), pl.BlockSpec((1, FOX_W), fix2), pl.BlockSpec((1, FOX_W), fix2)],
        out_specs=[pl.BlockSpec((tm, FOX_COLS), rev), pl.BlockSpec((1, FOX_W), fix2), pl.BlockSpec((1, FOX_W), fix2),
                   pl.BlockSpec((1, LANES), fix2)],
        out_shape=[jax.ShapeDtypeStruct((T, FOX_COLS), BF16), jax.ShapeDtypeStruct((1, FOX_W), F32),
                   jax.ShapeDtypeStruct((1, FOX_W), F32), jax.ShapeDtypeStruct((1, LANES), F32)],
        scratch_shapes=[pltpu.VMEM((8, 1), F32)],
        compiler_params=_params(dimension_semantics=("arbitrary",)),
    )(z, dq, dk, dv, dcs, drs, bias, qg, kg)


def _local_step(x, p, tgt, sm, W):
    lbl, og, fb = sm["hg_lb_logits"], sm["hg_onorm_g"], sm["fox_f_bias"]
    fbias = jnp.pad(fb, ((0, 0), (0, LANES - FOX_HEADS)))
    qg = jnp.tile(sm["fox_q_norm_g"], (1, FOX_HEADS))
    kg = jnp.tile(sm["fox_k_norm_g"], (1, FOX_HEADS))

    h = _rms_fwd(x, sm["norm_mix_g"], name="rms_mix")
    z_hg = _matmul(h, W["in_hg"], name="mm_z_hg")
    z_fox = _matmul(h, W["in_fox"], name="mm_z_fox")
    z_gate = _matmul(h, W["in_gate"], name="mm_z_gate")
    o_raw, ya, states = _hgrn_fwd(z_hg, lbl, og)
    qn, kn, vb, c, ct = _fox_prep(z_fox, fbias, qg, kg)
    yb, lse = _fox_fwd(qn, kn, vb, c, ct)
    ua = _matmul(ya, W["w_branch_a"], name="mm_ua")
    ub = _matmul(yb, W["w_branch_b"], name="mm_ub")
    merged = _merge_fwd(ua, ub, z_gate)
    x1 = _matmul(merged, W["w_out"], add=x, name="mm_x1")
    hf = _rms_fwd(x1, sm["norm_ffn_g"], name="rms_ffn")
    a = _matmul(hf, W["w_ffn_gate"], out_dtype=BF16, name="mm_ffn_a")
    b = _matmul(hf, W["w_ffn_up"], out_dtype=BF16, name="mm_ffn_b")
    act = _swiglu_fwd(a, b)
    x2 = _matmul(act, W["w_ffn_down"], add=x1, name="mm_x2")
    hp = _rms_fwd(x2, sm["norm_ple_g"], name="rms_ple")
    sp = _matmul(hp, W["w_ple_gate"], name="mm_sp")
    pp = _matmul(p, W["w_ple_proj"], name="mm_pp")
    dy, dsp, dpp, loss = _ple_loss(x2, sp, pp, tgt)

    G = {}
    G["w_ple_proj"] = _matmul(p, dpp, ta=True, name="mm_dw_ple_proj")
    G["w_ple_gate"] = _matmul(hp, dsp, ta=True, name="mm_dw_ple_gate")
    dhp = _matmul(dsp, W["w_ple_gate"], tb=True, name="mm_dhp")
    dx2, d_ple_g = _rms_bwd(x2, sm["norm_ple_g"], dhp, dy, name="rms_ple_bwd")
    dact = _matmul(dx2, W["w_ffn_down"], tb=True, name="mm_dact")
    G["w_ffn_down"] = _matmul(act, dx2, ta=True, name="mm_dw_ffn_down")
    da, db = _swiglu_bwd(a, b, dact)
    G["w_ffn_gate"] = _matmul(hf, da, ta=True, name="mm_dw_ffn_gate")
    G["w_ffn_up"] = _matmul(hf, db, ta=True, name="mm_dw_ffn_up")
    dhf = _matmul(da, W["w_ffn_gate"], tb=True, name="mm_dhf_a")
    dhf = _matmul(db, W["w_ffn_up"], tb=True, add=dhf, name="mm_dhf_b")
    dx1, d_ffn_g = _rms_bwd(x1, sm["norm_ffn_g"], dhf, dx2, name="rms_ffn_bwd")
    dmerged = _matmul(dx1, W["w_out"], tb=True, name="mm_dmerged")
    G["w_out"] = _matmul(merged, dx1, ta=True, name="mm_dw_out")
    dua, dub, dz_gate = _merge_bwd(dmerged, ua, ub, z_gate)
    G["w_branch_a"] = _matmul(ya, dua, ta=True, name="mm_dw_branch_a")
    G["w_branch_b"] = _matmul(yb, dub, ta=True, name="mm_dw_branch_b")
    dya = _matmul(dua, W["w_branch_a"], tb=True, name="mm_dya")
    dyb = _matmul(dub, W["w_branch_b"], tb=True, name="mm_dyb")
    dq, dk, dv, dcs, drs = _fox_bwd(qn, kn, vb, c, ct, yb, dyb, lse)
    dz_fox, d_qg, d_kg, d_fb = _fox_post(z_fox, dq, dk, dv, dcs, drs, fbias, qg, kg)
    dz_hg, d_lbl, d_og = _hgrn_bwd(z_hg, o_raw, dya, states, lbl, og)
    dh = _matmul(dz_hg, W["in_hg"], tb=True, name="mm_dh_hg")
    dh = _matmul(dz_fox, W["in_fox"], tb=True, add=dh, name="mm_dh_fox")
    dh = _matmul(dz_gate, W["in_gate"], tb=True, add=dh, name="mm_dh_gate")
    g_hg = _matmul(h, dz_hg, ta=True, name="mm_dw_in_hg")
    g_fox = _matmul(h, dz_fox, ta=True, name="mm_dw_in_fox")
    g_gate = _matmul(h, dz_gate, ta=True, name="mm_dw_in_gate")
    G["w_in"] = jnp.concatenate([g_hg, g_fox[:, :3 * FOX_W + FOX_HEADS], g_gate], axis=1)
    grad_x, d_mix_g = _rms_bwd(x, sm["norm_mix_g"], dh, dx1, name="rms_mix_bwd")

    gs = {"norm_mix_g": d_mix_g, "hg_lb_logits": d_lbl, "hg_onorm_g": d_og, "fox_f_bias": d_fb[:, :FOX_HEADS],
          "fox_q_norm_g": d_qg.reshape(FOX_HEADS, FOX_DH).sum(0, keepdims=True),
          "fox_k_norm_g": d_kg.reshape(FOX_HEADS, FOX_DH).sum(0, keepdims=True),
          "norm_ffn_g": d_ffn_g, "norm_ple_g": d_ple_g}
    return loss, grad_x, gs, G


def _pack_rows(parts, total):
    buf = jnp.concatenate(parts, axis=-2)
    pad = total - buf.shape[-2]
    widths = [(0, 0)] * (buf.ndim - 2) + [(0, pad), (0, 0)]
    return jnp.pad(buf, widths)


def _pack_shards(shards):
    return _pack_rows([shards[n].reshape(PACK_ROWS[n], LANES) for n in BIG], PACK_TOTAL)


def _unpack_shards(buf):
    out, r0 = {}, 0
    for n in BIG:
        r, c, ax = BIG_SHAPE[n]
        shp = (1, r // N_DEV, c) if ax == 0 else (1, r, c // N_DEV)
        out[n] = buf[r0:r0 + PACK_ROWS[n]].reshape(shp)
        r0 += PACK_ROWS[n]
    return out


def _unpack_full(buf):
    out, r0 = {}, 0
    for n in BIG:
        r, c, ax = BIG_SHAPE[n]
        seg = buf[:, r0:r0 + PACK_ROWS[n]]
        if ax == 0:
            out[n] = seg.reshape(r, c)
        else:
            out[n] = seg.reshape(N_DEV, r, c // N_DEV).transpose(1, 0, 2).reshape(r, c)
        r0 += PACK_ROWS[n]
    return out


def _pack_full(G):
    parts = []
    for n in BIG:
        r, c, ax = BIG_SHAPE[n]
        g = G[n]
        if ax == 1:
            g = g.reshape(r, N_DEV, c // N_DEV).transpose(1, 0, 2)
        parts.append(g.reshape(N_DEV, PACK_ROWS[n], LANES))
    return _pack_rows(parts, PACK_TOTAL)


def _pack_small(vals, loss_row=None):
    parts = [vals[n].reshape(SMALL_ROWS[n], -1) for n in SMALL]
    parts = [jnp.pad(v, ((0, 0), (0, LANES - v.shape[1]))) for v in parts]
    if loss_row is not None:
        parts.append(loss_row)
    return _pack_rows(parts, SMALL_TOTAL)


def _unpack_small(buf, like):
    out, r0 = {}, 0
    for n in SMALL:
        rows, size = SMALL_ROWS[n], like[n].size
        blk = buf[r0:r0 + rows]
        out[n] = (blk if size == rows * LANES else blk[:, :size]).reshape(like[n].shape)
        r0 += rows
    return out


def _place():
    return lax.axis_index("x"), lax.axis_index("y"), lax.axis_index("c")


def _all_gather(block):
    R = block.shape[0]

    def body(x_ref, out_ref, send_sems, recv_sems, local_sem):
        x, y, c = _place()
        me, sibling = (x, y, c), (x, y, 1 - c)
        chips = [(1 - x, y), (x, 1 - y), (1 - x, 1 - y)]

        def slot(px, py, pc):
            return out_ref.at[4 * px + 2 * py + pc]

        def copy(k, blk, to, src=None):
            return pltpu.make_async_remote_copy(
                src_ref=slot(*blk) if src is None else src, dst_ref=slot(*blk),
                send_sem=send_sems.at[k], recv_sem=recv_sems.at[k], device_id=to, device_id_type=MESH)

        mine = pltpu.make_async_copy(x_ref, slot(*me), local_sem)
        mine.start()
        first = [copy(0, me, sibling, src=x_ref)]
        first += [copy(1 + j, me, (*chip, c), src=x_ref) for j, chip in enumerate(chips)]
        for cp in first:
            cp.start()
        passed = [copy(4 + j, (*chip, c), sibling) for j, chip in enumerate(chips)]
        for j, chip in enumerate(chips):
            copy(1 + j, (*chip, c), me).wait_recv()
            passed[j].start()
        copy(0, sibling, me).wait_recv()
        for j, chip in enumerate(chips):
            copy(4 + j, (*chip, 1 - c), me).wait_recv()
        for cp in first + passed:
            cp.wait_send()
        mine.wait()

    return _pallas(
        body, name="weights_all_gather", out_shape=jax.ShapeDtypeStruct((N_DEV, R, LANES), block.dtype),
        in_specs=[pl.BlockSpec(memory_space=pl.ANY)], out_specs=pl.BlockSpec(memory_space=pl.ANY),
        scratch_shapes=[pltpu.SemaphoreType.DMA((7,)), pltpu.SemaphoreType.DMA((7,)), pltpu.SemaphoreType.DMA],
            )(block)


def _sibling_exchange(g4):
    R = g4.shape[2]

    def body(g_ref, out_ref, send_sem, recv_sem):
        x, y, c = _place()
        cp = pltpu.make_async_remote_copy(
            src_ref=g_ref.at[:, pl.ds(1 - c, 1)], dst_ref=out_ref, send_sem=send_sem, recv_sem=recv_sem,
            device_id=(x, y, 1 - c), device_id_type=MESH)
        cp.start()
        cp.wait()

    return _pallas(
        body, name="grads_to_sibling", out_shape=jax.ShapeDtypeStruct((4, 1, R, LANES), g4.dtype),
        in_specs=[pl.BlockSpec(memory_space=pl.ANY)], out_specs=pl.BlockSpec(memory_space=pl.ANY),
        scratch_shapes=[pltpu.SemaphoreType.DMA, pltpu.SemaphoreType.DMA],
            )(g4)


def _chip_sum(g4, got, core):
    R = g4.shape[2]

    def body(c_ref, g_ref, r_ref, f_ref, h_ref):
        s = g_ref[0, 0] + r_ref[0, 0]
        f_ref[0] = s
        h_ref[0] = s.astype(BF16)

    grid_spec = pltpu.PrefetchScalarGridSpec(
        num_scalar_prefetch=1, grid=(4, R // PACK_BLOCK),
        in_specs=[pl.BlockSpec((1, 1, PACK_BLOCK, LANES), lambda j, i, c: (j, c[0], i, 0)),
                  pl.BlockSpec((1, 1, PACK_BLOCK, LANES), lambda j, i, c: (j, 0, i, 0))],
        out_specs=[pl.BlockSpec((1, PACK_BLOCK, LANES), lambda j, i, c: (j, i, 0))] * 2)
    return _pallas(
        body, name="chip_sum", grid_spec=grid_spec,
        out_shape=[jax.ShapeDtypeStruct((4, R, LANES), F32), jax.ShapeDtypeStruct((4, R, LANES), BF16)],
        compiler_params=_params(dimension_semantics=("arbitrary",) * 2),
    )(core, g4, got)


def _chip_exchange(hb):
    R = hb.shape[1]

    def body(h_ref, out_ref, send_sems, recv_sems):
        x, y, c = _place()
        chips = [(1 - x, y), (x, 1 - y), (1 - x, 1 - y)]
        cps = [pltpu.make_async_remote_copy(
            src_ref=h_ref.at[2 * px + py], dst_ref=out_ref.at[k], send_sem=send_sems.at[k], recv_sem=recv_sems.at[k],
            device_id=(px, py, c), device_id_type=MESH) for k, (px, py) in enumerate(chips)]
        for cp in cps:
            cp.start()
        for cp in cps:
            cp.wait()

    return _pallas(
        body, name="grads_to_chips", out_shape=jax.ShapeDtypeStruct((3, R, LANES), hb.dtype),
        in_specs=[pl.BlockSpec(memory_space=pl.ANY)], out_specs=pl.BlockSpec(memory_space=pl.ANY),
        scratch_shapes=[pltpu.SemaphoreType.DMA((3,)), pltpu.SemaphoreType.DMA((3,))],
            )(hb)


def _adam_math(w, g, m, v):
    m = ADAM_B1 * m + (1.0 - ADAM_B1) * g
    v = ADAM_B2 * v + (1.0 - ADAM_B2) * (g * g)
    m_hat = m / (1.0 - ADAM_B1 ** ADAM_STEP)
    v_hat = v / (1.0 - ADAM_B2 ** ADAM_STEP)
    delta = -ADAM_LR * (m_hat / (jnp.sqrt(v_hat) + ADAM_EPS) + ADAM_WD * w)
    return delta, m, v


def _adam_big(hf, got, chip, w, m, v):
    R = w.shape[0]

    def body(j_ref, h_ref, r_ref, w_ref, m_ref, v_ref, g_ref, d_ref, nm_ref, nv_ref):
        g = h_ref[0] + r_ref[0].astype(F32) + r_ref[1].astype(F32) + r_ref[2].astype(F32)
        d, nm, nv = _adam_math(w_ref[...], g, m_ref[...], v_ref[...])
        g_ref[...] = g
        d_ref[...] = d
        nm_ref[...] = nm
        nv_ref[...] = nv

    blk = pl.BlockSpec((PACK_BLOCK, LANES), lambda i, j: (i, 0))
    grid_spec = pltpu.PrefetchScalarGridSpec(
        num_scalar_prefetch=1, grid=(R // PACK_BLOCK,),
        in_specs=[pl.BlockSpec((1, PACK_BLOCK, LANES), lambda i, j: (j[0], i, 0)),
                  pl.BlockSpec((3, PACK_BLOCK, LANES), lambda i, j: (0, i, 0)), blk, blk, blk],
        out_specs=[blk] * 4)
    return _pallas(
        body, name="adam_big", grid_spec=grid_spec, out_shape=[jax.ShapeDtypeStruct((R, LANES), F32)] * 4,
        compiler_params=_params(dimension_semantics=("arbitrary",)),
    )(chip, hf, got, w, m, v)


def _small_all_reduce_adam(gs, w, m, v):
    def body(g_ref, w_ref, m_ref, v_ref, sum_ref, d_ref, nm_ref, nv_ref, gather, send_sems, recv_sems):
        x, y, c = _place()
        my = 4 * x + 2 * y + c
        gather[my] = g_ref[...]
        cps = []
        for k in range(1, N_DEV):
            to = (x ^ (k >> 2), y ^ ((k >> 1) & 1), c ^ (k & 1))
            cps.append(pltpu.make_async_remote_copy(
                src_ref=g_ref, dst_ref=gather.at[my], send_sem=send_sems.at[k - 1], recv_sem=recv_sems.at[k - 1],
                device_id=to, device_id_type=MESH))
        for cp in cps:
            cp.start()
        for cp in cps:
            cp.wait()
        total = gather[0]
        for d in range(1, N_DEV):
            total = total + gather[d]
        dlt, nm, nv = _adam_math(w_ref[...], total, m_ref[...], v_ref[...])
        sum_ref[...] = total
        d_ref[...] = dlt
        nm_ref[...] = nm
        nv_ref[...] = nv

    vm = pl.BlockSpec(memory_space=pltpu.VMEM)
    return _pallas(
        body, name="small_all_reduce_adam", out_shape=[jax.ShapeDtypeStruct((SMALL_TOTAL, LANES), F32)] * 4,
        in_specs=[vm] * 4, out_specs=[vm] * 4,
        scratch_shapes=[pltpu.VMEM((N_DEV, SMALL_TOTAL, LANES), F32), pltpu.SemaphoreType.DMA((7,)),
                        pltpu.SemaphoreType.DMA((7,))],
            )(gs, w, m, v)


def kernel(x, p, norm_mix_g, w_in, hg_lb_logits, hg_onorm_g, fox_f_bias, fox_q_norm_g, fox_k_norm_g, w_branch_a, w_branch_b, w_out, norm_ffn_g, w_ffn_gate, w_ffn_up, w_ffn_down, norm_ple_g, w_ple_gate, w_ple_proj, loss_target, m_norm_mix_g, m_w_in, m_hg_lb_logits, m_hg_onorm_g, m_fox_f_bias, m_fox_q_norm_g, m_fox_k_norm_g, m_w_branch_a, m_w_branch_b, m_w_out, m_norm_ffn_g, m_w_ffn_gate, m_w_ffn_up, m_w_ffn_down, m_norm_ple_g, m_w_ple_gate, m_w_ple_proj, v_norm_mix_g, v_w_in, v_hg_lb_logits, v_hg_onorm_g, v_fox_f_bias, v_fox_q_norm_g, v_fox_k_norm_g, v_w_branch_a, v_w_branch_b, v_w_out, v_norm_ffn_g, v_w_ffn_gate, v_w_ffn_up, v_w_ffn_down, v_norm_ple_g, v_w_ple_gate, v_w_ple_proj):
    args = dict(locals())
    wts = {n: args[n] for n in BIG + SMALL}
    mom = {n: args["m_" + n] for n in BIG + SMALL}
    var = {n: args["v_" + n] for n in BIG + SMALL}
    sm = {n: wts[n] for n in SMALL}

    w_pack = _pack_shards(wts)
    gathered = _all_gather(w_pack.astype(BF16))
    full = _unpack_full(gathered)
    w_in_full = full.pop("w_in")
    full["in_hg"] = w_in_full[:, :HG_COLS]
    full["in_fox"] = jnp.pad(w_in_full[:, HG_COLS:HG_COLS + 3 * FOX_W + FOX_HEADS], ((0, 0), (0, LANES - FOX_HEADS)))
    full["in_gate"] = w_in_full[:, HG_COLS + 3 * FOX_W + FOX_HEADS:]

    loss_blk, grad_x, gs, G = _local_step(x[0], p[0, 0], loss_target[0], sm, full)

    xi, yi, ci = _place()
    g_pack = _pack_full(G).reshape(4, 2, PACK_TOTAL, LANES)
    got_sib = _sibling_exchange(g_pack)
    core = jnp.reshape(ci, (1,)).astype(jnp.int32)
    chip = jnp.reshape(2 * xi + yi, (1,)).astype(jnp.int32)
    h_f32, h_bf16 = _chip_sum(g_pack, got_sib, core)
    got_chips = _chip_exchange(h_bf16)
    g_own, d_own, nm_own, nv_own = _adam_big(h_f32, got_chips, chip, w_pack, _pack_shards(mom), _pack_shards(var))
    g_big, d_big, nm_big, nv_big = (_unpack_shards(t) for t in (g_own, d_own, nm_own, nv_own))

    s_sum, s_d, s_nm, s_nv = _small_all_reduce_adam(
        _pack_small(gs, loss_blk[0:1]), _pack_small(sm), _pack_small({n: mom[n] for n in SMALL}),
        _pack_small({n: var[n] for n in SMALL}))
    loss = s_sum[LOSS_ROW, 0]
    g_small, d_small, nm_small, nv_small = (_unpack_small(t, sm) for t in (s_sum, s_d, s_nm, s_nv))

    order = ["norm_mix_g", "w_in", "hg_lb_logits", "hg_onorm_g", "fox_f_bias", "fox_q_norm_g", "fox_k_norm_g",
             "w_branch_a", "w_branch_b", "w_out", "norm_ffn_g", "w_ffn_gate", "w_ffn_up", "w_ffn_down", "norm_ple_g",
             "w_ple_gate", "w_ple_proj"]
    outs = [loss, grad_x[None]]
    for big, small in ((g_big, g_small), (d_big, d_small), (nm_big, nm_small), (nv_big, nv_small)):
        outs += [big[n] if n in big else small[n] for n in order]
    return tuple(outs)
```

```python
import functools

import jax
import jax.numpy as jnp
from jax import lax
from jax.experimental import pallas as pl
from jax.experimental.pallas import tpu as pltpu

F32 = jnp.float32
BF16 = jnp.bfloat16

D_MODEL = 1024
PLE_DIM = 256
HG_HEADS = 4
HG_DK = 128
HG_CHUNK = 64
HG_SUB = 16
HG_W = HG_HEADS * HG_DK
FOX_HEADS = 8
FOX_DH = 64
FOX_W = FOX_HEADS * FOX_DH
D_FF = 2816
EPS = 1e-6
N_DEV = 8
LANES = 128
FOX_COLS = 3 * FOX_W + LANES
HG_COLS = 4 * HG_W
GATE_COLS = 2 * D_MODEL
IN_COLS = HG_COLS + 3 * FOX_W + FOX_HEADS + GATE_COLS
EXP_CLAMP = 80.0

ADAM_LR = 0.001
ADAM_B1 = 0.9
ADAM_B2 = 0.999
ADAM_EPS = 1e-08
ADAM_WD = 0.01
ADAM_STEP = 10

MESH = pl.DeviceIdType.MESH
VMEM_LIMIT = 56 * 1024 * 1024
ROW_BLOCK = 512

BIG = ["w_in", "w_branch_a", "w_branch_b", "w_out", "w_ffn_gate", "w_ffn_up", "w_ffn_down",
       "w_ple_gate", "w_ple_proj"]
BIG_SHAPE = {
    "w_in": (D_MODEL, IN_COLS, 1), "w_branch_a": (HG_W, D_MODEL, 1), "w_branch_b": (FOX_W, D_MODEL, 1),
    "w_out": (D_MODEL, D_MODEL, 0), "w_ffn_gate": (D_MODEL, D_FF, 1), "w_ffn_up": (D_MODEL, D_FF, 1),
    "w_ffn_down": (D_FF, D_MODEL, 0), "w_ple_gate": (D_MODEL, D_MODEL, 0), "w_ple_proj": (PLE_DIM, D_MODEL, 1),
}
PACK_ROWS = {n: (r * c) // (N_DEV * LANES) for n, (r, c, _) in BIG_SHAPE.items()}
PACK_BLOCK = 2560
PACK_TOTAL = -(-sum(PACK_ROWS.values()) // PACK_BLOCK) * PACK_BLOCK

SMALL = ["norm_mix_g", "hg_lb_logits", "hg_onorm_g", "fox_f_bias", "fox_q_norm_g", "fox_k_norm_g",
         "norm_ffn_g", "norm_ple_g"]
SMALL_ROWS = {"norm_mix_g": 8, "hg_lb_logits": 8, "hg_onorm_g": 1, "fox_f_bias": 1, "fox_q_norm_g": 1,
              "fox_k_norm_g": 1, "norm_ffn_g": 8, "norm_ple_g": 8}
SMALL_TOTAL = 40
LOSS_ROW = 36


def _pallas(body, **kw):
    return pl.pallas_call(body, **kw)


def _params(**kw):
    return pltpu.CompilerParams(vmem_limit_bytes=VMEM_LIMIT, **kw)


def _pick(n, target):
    if n <= target:
        return n
    best = None
    for t in range(LANES, target + 1, LANES):
        if n % t == 0:
            best = t
    assert best is not None, (n, target)
    return best


def _dot(a, b, ca, cb):
    return lax.dot_general(a, b, (((ca,), (cb,)), ((), ())), preferred_element_type=F32)


def _split_dot(mat, x, ca, cb, terms=2, mat_first=True):
    acc = None
    rem = x
    for _ in range(terms):
        part = rem.astype(BF16)
        rem = rem - part.astype(F32)
        p = _dot(mat, part, ca, cb) if mat_first else _dot(part, mat, ca, cb)
        acc = p if acc is None else acc + p
    return acc


def _sigmoid(x):
    return 1.0 / (1.0 + jnp.exp(-x))


def _iota(shape, dim):
    return lax.broadcasted_iota(jnp.int32, shape, dim)


def _matmul(a, b, *, name, ta=False, tb=False, out_dtype=F32, add=None):
    (K, M) = a.shape if ta else a.shape[::-1]
    (N, Kb) = b.shape if tb else b.shape[::-1]
    assert K == Kb, (a.shape, b.shape, ta, tb)
    if ta:
        tm, tn, tk = _pick(M, 3 * ROW_BLOCK), _pick(N, 2048), _pick(K, ROW_BLOCK)
    else:
        tm, tn, tk = _pick(M, 2 * ROW_BLOCK), _pick(N, 2048), _pick(K, 2048)
    nk = K // tk
    use_scratch = nk > 1 and out_dtype != F32

    def body(*refs):
        refs = list(refs)
        a_ref, b_ref = refs[:2]
        add_ref = refs[2] if add is not None else None
        o_ref = refs[3] if add is not None else refs[2]
        k = pl.program_id(2)
        p = _dot(a_ref[...].astype(BF16), b_ref[...].astype(BF16), 0 if ta else 1, 1 if tb else 0)

        def with_add(r):
            return r if add is None else r + add_ref[...].astype(F32)

        if nk == 1:
            o_ref[...] = with_add(p).astype(out_dtype)
        elif not use_scratch:
            @pl.when(k == 0)
            def _():
                o_ref[...] = with_add(p)

            @pl.when(k > 0)
            def _():
                o_ref[...] += p
        else:
            acc_ref = refs[-1]

            @pl.when(k == 0)
            def _():
                acc_ref[...] = p

            @pl.when(k > 0)
            def _():
                acc_ref[...] += p

            @pl.when(k == nk - 1)
            def _():
                o_ref[...] = with_add(acc_ref[...]).astype(out_dtype)

    a_spec = pl.BlockSpec((tk, tm), lambda i, j, k: (k, i)) if ta else pl.BlockSpec((tm, tk), lambda i, j, k: (i, k))
    b_spec = pl.BlockSpec((tn, tk), lambda i, j, k: (j, k)) if tb else pl.BlockSpec((tk, tn), lambda i, j, k: (k, j))
    o_spec = pl.BlockSpec((tm, tn), lambda i, j, k: (i, j))
    in_specs = [a_spec, b_spec] + ([o_spec] if add is not None else [])
    args = (a, b) + ((add,) if add is not None else ())
    return _pallas(
        body, name=name, grid=(M // tm, N // tn, nk), in_specs=in_specs, out_specs=o_spec,
        out_shape=jax.ShapeDtypeStruct((M, N), out_dtype),
        scratch_shapes=[pltpu.VMEM((tm, tn), F32)] if use_scratch else [],
        compiler_params=_params(dimension_semantics=("arbitrary",) * 3),
    )(*args)


def _row_call(body, *, name, T, ins, outs, acc_outs=(), tm=ROW_BLOCK, reverse=False):
    tm = min(tm, T)
    nb = T // tm
    rmap = (lambda i: (nb - 1 - i, 0)) if reverse else (lambda i: (i, 0))
    in_specs, args = [], []
    for arr, blocked in ins:
        args.append(arr)
        if blocked:
            in_specs.append(pl.BlockSpec((tm, arr.shape[1]), rmap))
        else:
            in_specs.append(pl.BlockSpec(arr.shape, lambda i, _n=arr.ndim: (0,) * _n))
    out_specs, out_shape = [], []
    for c, dt in outs:
        out_specs.append(pl.BlockSpec((tm, c), rmap))
        out_shape.append(jax.ShapeDtypeStruct((T, c), dt))
    for shp, dt in acc_outs:
        out_specs.append(pl.BlockSpec(shp, lambda i, _n=len(shp): (0,) * _n))
        out_shape.append(jax.ShapeDtypeStruct(shp, dt))
    return _pallas(body, name=name, grid=(nb,), in_specs=in_specs, out_specs=out_specs, out_shape=out_shape,
                   compiler_params=_params(dimension_semantics=("arbitrary",)))(*args)


def _rms_fwd(x, g, *, name):
    T = x.shape[0]

    def body(x_ref, g_ref, h_ref):
        xv = x_ref[...]
        rstd = lax.rsqrt(jnp.mean(xv * xv, axis=-1, keepdims=True) + EPS)
        h_ref[...] = (xv * rstd * g_ref[...]).astype(BF16)

    return _row_call(body, name=name, T=T, ins=[(x, True), (g, False)], outs=[(D_MODEL, BF16)])[0]


def _rms_bwd(x, g, dh, dres, *, name):
    T = x.shape[0]

    def body(x_ref, g_ref, dh_ref, dres_ref, dx_ref, dg_ref):
        xv = x_ref[...]
        rstd = lax.rsqrt(jnp.mean(xv * xv, axis=-1, keepdims=True) + EPS)
        xh = xv * rstd
        dhv = dh_ref[...]
        part = jnp.sum(dhv * xh, axis=0, keepdims=True)

        @pl.when(pl.program_id(0) == 0)
        def _():
            dg_ref[...] = part

        @pl.when(pl.program_id(0) > 0)
        def _():
            dg_ref[...] += part

        dxh = dhv * g_ref[...]
        dx_ref[...] = rstd * (dxh - xh * jnp.mean(dxh * xh, axis=-1, keepdims=True)) + dres_ref[...]

    return _row_call(body, name=name, T=T, ins=[(x, True), (g, False), (dh, True), (dres, True)],
                     outs=[(D_MODEL, F32)], acc_outs=[((1, D_MODEL), F32)])


def _merge_fwd(ua, ub, zg):
    def body(ua_ref, ub_ref, zg_ref, m_ref):
        ga = _sigmoid(zg_ref[:, :D_MODEL])
        gb = _sigmoid(zg_ref[:, D_MODEL:])
        m_ref[...] = (ga * ua_ref[...] + gb * ub_ref[...]).astype(BF16)

    return _row_call(body, name="merge_fwd", T=ua.shape[0], ins=[(ua, True), (ub, True), (zg, True)],
                     outs=[(D_MODEL, BF16)])[0]


def _merge_bwd(dm, ua, ub, zg):
    def body(dm_ref, ua_ref, ub_ref, zg_ref, dua_ref, dub_ref, dzg_ref):
        ga = _sigmoid(zg_ref[:, :D_MODEL])
        gb = _sigmoid(zg_ref[:, D_MODEL:])
        dmv = dm_ref[...]
        dua_ref[...] = (dmv * ga).astype(BF16)
        dub_ref[...] = (dmv * gb).astype(BF16)
        dzg_ref[:, :D_MODEL] = (dmv * ua_ref[...] * ga * (1.0 - ga)).astype(BF16)
        dzg_ref[:, D_MODEL:] = (dmv * ub_ref[...] * gb * (1.0 - gb)).astype(BF16)

    return _row_call(body, name="merge_bwd", T=dm.shape[0], ins=[(dm, True), (ua, True), (ub, True), (zg, True)],
                     outs=[(D_MODEL, BF16), (D_MODEL, BF16), (GATE_COLS, BF16)])


def _swiglu_fwd(a, b):
    def body(a_ref, b_ref, o_ref):
        av = a_ref[...].astype(F32)
        o_ref[...] = (av * _sigmoid(av) * b_ref[...].astype(F32)).astype(BF16)

    return _row_call(body, name="swiglu_fwd", T=a.shape[0], ins=[(a, True), (b, True)], outs=[(D_FF, BF16)])[0]


def _swiglu_bwd(a, b, dact):
    def body(a_ref, b_ref, d_ref, da_ref, db_ref):
        av = a_ref[...].astype(F32)
        bv = b_ref[...].astype(F32)
        dv = d_ref[...]
        sg = _sigmoid(av)
        da_ref[...] = (dv * bv * sg * (1.0 + av * (1.0 - sg))).astype(BF16)
        db_ref[...] = (dv * av * sg).astype(BF16)

    return _row_call(body, name="swiglu_bwd", T=a.shape[0], ins=[(a, True), (b, True), (dact, True)],
                     outs=[(D_FF, BF16), (D_FF, BF16)])


def _ple_loss(x2, sp, pp, tgt):
    def body(x_ref, sp_ref, pp_ref, t_ref, dy_ref, dsp_ref, dpp_ref, loss_ref):
        gp = _sigmoid(sp_ref[...])
        ppv = pp_ref[...]
        err = x_ref[...] + gp * ppv - t_ref[...]
        part = 0.5 * jnp.sum(jnp.mean(err * err, axis=-1, keepdims=True), axis=0, keepdims=True)
        part = jnp.broadcast_to(part, loss_ref.shape)

        @pl.when(pl.program_id(0) == 0)
        def _():
            loss_ref[...] = part

        @pl.when(pl.program_id(0) > 0)
        def _():
            loss_ref[...] += part

        dy = err * (1.0 / D_MODEL)
        dy_ref[...] = dy
        dsp_ref[...] = (dy * ppv * gp * (1.0 - gp)).astype(BF16)
        dpp_ref[...] = (dy * gp).astype(BF16)

    return _row_call(body, name="ple_loss", T=x2.shape[0], ins=[(x2, True), (sp, True), (pp, True), (tgt, True)],
                     outs=[(D_MODEL, F32), (D_MODEL, BF16), (D_MODEL, BF16)], acc_outs=[((8, LANES), F32)])


def _hg_consts():
    C = HG_CHUNK
    r, c = _iota((C, C), 0), _iota((C, C), 1)
    tri = (c <= r)
    same = (r // HG_SUB) == (c // HG_SUB)
    return tri, (tri & same)


def _hg_chunk_fwd(q, f, lb, tri_b, sub_b):
    sgq = _sigmoid(q)
    qt = q * sgq
    sg = _sigmoid(f)
    fg = lb + (1.0 - lb) * sg
    kf = (1.0 - lb) * (1.0 - sg)
    logf = jnp.log(fg)
    b = _split_dot(tri_b, logf, 1, 0)
    w = _split_dot(sub_b, logf, 1, 0)
    return sgq, qt, sg, fg, kf, b, w


def _hg_scores(qs_b, kf, b, row):
    C, S = HG_CHUNK, HG_SUB
    parts, ks = [], []
    for blk in range(C // S):
        ref = jnp.zeros_like(b[0:1]) if blk == 0 else b[blk * S - 1:blk * S]
        e = jnp.exp(jnp.minimum(ref - b, EXP_CLAMP))
        e = jnp.where(row < (blk + 1) * S, e, 0.0)
        k_b = (kf * e).astype(BF16)
        ks.append((e, k_b))
        parts.append(_dot(qs_b[blk * S:(blk + 1) * S], k_b, 1, 1))
    return jnp.concatenate(parts, axis=0), ks


def _hgrn_fwd(z, lb_logits, gain):
    T = z.shape[0]
    RB = min(ROW_BLOCK, T)
    nb, cpb = T // RB, RB // HG_CHUNK
    C, DK = HG_CHUNK, HG_DK

    def body(z_ref, lg_ref, g_ref, o_ref, y_ref, st_ref, s_ref):
        @pl.when(pl.program_id(0) == 0)
        def _():
            s_ref[...] = jnp.zeros_like(s_ref)

        lg = lg_ref[...]
        lb_all = 1.0 / (1.0 + jnp.exp(lg[1:2] - lg[0:1]))
        gain_v = g_ref[...]
        tri, sub = _hg_consts()
        tri_b, sub_b = tri.astype(BF16), sub.astype(BF16)
        row = _iota((C, DK), 0)

        def chunk(ci, carry):
            r0 = pl.multiple_of(ci * C, C)
            rows = pl.ds(r0, C)
            for h in range(HG_HEADS):
                cs = slice(h * DK, (h + 1) * DK)
                q = z_ref[rows, cs]
                f = z_ref[rows, HG_W + h * DK:HG_W + (h + 1) * DK]
                v = z_ref[rows, 2 * HG_W + h * DK:2 * HG_W + (h + 1) * DK]
                g = z_ref[rows, 3 * HG_W + h * DK:3 * HG_W + (h + 1) * DK]
                lb = lb_all[:, cs]
                _, qt, _, _, kf, b, w = _hg_chunk_fwd(q, f, lb, tri_b, sub_b)
                st = s_ref[h]
                st_ref[pl.ds(pl.multiple_of((ci * HG_HEADS + h) * DK, DK), DK), :] = st
                v_b = v.astype(BF16)
                qs_b = (qt * jnp.exp(w)).astype(BF16)
                a, _ = _hg_scores(qs_b, kf, b, row)
                a = jnp.where(tri, a, 0.0)
                qd = qt * jnp.exp(b)
                o = _dot(qd.astype(BF16), st.astype(BF16), 1, 1) + _dot(a.astype(BF16), v_b, 1, 0)
                bl = b[C - 1:C]
                kd = kf * jnp.exp(bl - b)
                s_ref[h] = st * jnp.exp(bl) + _dot(v_b, kd.astype(BF16), 0, 0)
                o_ref[rows, cs] = o
                rstd = lax.rsqrt(jnp.mean(o * o, axis=-1, keepdims=True) + EPS)
                y_ref[rows, cs] = (o * rstd * gain_v * (g * _sigmoid(g))).astype(BF16)
            return carry

        lax.fori_loop(0, cpb, chunk, 0)

    return _pallas(
        body, name="hgrn_fwd", grid=(nb,),
        in_specs=[pl.BlockSpec((RB, HG_COLS), lambda i: (i, 0)), pl.BlockSpec((2, HG_W), lambda i: (0, 0)),
                  pl.BlockSpec((1, DK), lambda i: (0, 0))],
        out_specs=[pl.BlockSpec((RB, HG_W), lambda i: (i, 0)), pl.BlockSpec((RB, HG_W), lambda i: (i, 0)),
                   pl.BlockSpec((cpb * HG_HEADS * DK, DK), lambda i: (i, 0))],
        out_shape=[jax.ShapeDtypeStruct((T, HG_W), F32), jax.ShapeDtypeStruct((T, HG_W), BF16),
                   jax.ShapeDtypeStruct((T // C * HG_HEADS * DK, DK), F32)],
        scratch_shapes=[pltpu.VMEM((HG_HEADS, DK, DK), F32)],
        compiler_params=_params(dimension_semantics=("arbitrary",)),
    )(z, lb_logits, gain)


def _hgrn_bwd(z, o_raw, dy, states, lb_logits, gain):
    T = z.shape[0]
    RB = min(ROW_BLOCK, T)
    nb, cpb = T // RB, RB // HG_CHUNK
    C, DK, S = HG_CHUNK, HG_DK, HG_SUB

    def body(z_ref, o_ref, dy_ref, st_ref, lg_ref, g_ref, dz_ref, dlg_ref, dg_ref, ds_ref, dlb_ref):
        step = pl.program_id(0)

        @pl.when(step == 0)
        def _():
            ds_ref[...] = jnp.zeros_like(ds_ref)
            dlb_ref[...] = jnp.zeros_like(dlb_ref)
            dg_ref[...] = jnp.zeros_like(dg_ref)

        lg = lg_ref[...]
        lb_all = 1.0 / (1.0 + jnp.exp(lg[1:2] - lg[0:1]))
        gain_v = g_ref[...]
        tri, sub = _hg_consts()
        tri_b, sub_b = tri.astype(BF16), sub.astype(BF16)
        row = _iota((C, DK), 0)

        def chunk(cj, carry):
            ci = cpb - 1 - cj
            r0 = pl.multiple_of(ci * C, C)
            rows = pl.ds(r0, C)
            for h in range(HG_HEADS):
                cs = slice(h * DK, (h + 1) * DK)
                q = z_ref[rows, cs]
                f = z_ref[rows, HG_W + h * DK:HG_W + (h + 1) * DK]
                v = z_ref[rows, 2 * HG_W + h * DK:2 * HG_W + (h + 1) * DK]
                g = z_ref[rows, 3 * HG_W + h * DK:3 * HG_W + (h + 1) * DK]
                lb = lb_all[:, cs]
                sgq, qt, sg, fg, kf, b, w = _hg_chunk_fwd(q, f, lb, tri_b, sub_b)
                st = st_ref[pl.ds(pl.multiple_of((ci * HG_HEADS + h) * DK, DK), DK), :]
                dst = ds_ref[h]
                o = o_ref[rows, cs]
                dyv = dy_ref[rows, cs]
                rstd = lax.rsqrt(jnp.mean(o * o, axis=-1, keepdims=True) + EPS)
                n = o * rstd
                sgg = _sigmoid(g)
                t1 = dyv * (g * sgg)
                dg_ref[...] += jnp.sum(t1 * n, axis=0, keepdims=True)
                dn = t1 * gain_v
                do = rstd * (dn - n * jnp.mean(dn * n, axis=-1, keepdims=True))
                dgate = dyv * n * gain_v * sgg * (1.0 + g * (1.0 - sgg))
                do_b = do.astype(BF16)
                v_b = v.astype(BF16)
                ew = jnp.exp(w)
                qs_b = (qt * ew).astype(BF16)
                a, ks = _hg_scores(qs_b, kf, b, row)
                a = jnp.where(tri, a, 0.0)
                eb = jnp.exp(b)
                qd = qt * eb
                bl = b[C - 1:C]
                ebl = jnp.exp(bl)
                ekd = jnp.exp(bl - b)
                kd = kf * ekd
                kd_b = kd.astype(BF16)
                qd_b = qd.astype(BF16)
                dst_b = dst.astype(BF16)
                dqd = _dot(do_b, st.astype(BF16), 1, 0)
                da = jnp.where(tri, _dot(do_b, v_b, 1, 1), 0.0)
                dv = _dot(a.astype(BF16), do_b, 0, 0) + _dot(kd_b, dst_b, 1, 1)
                dkd = _dot(v_b, dst_b, 1, 0)
                ds_ref[h] = dst * ebl + _dot(do_b, qd_b, 0, 0)
                dkd_kd = dkd * kd_b.astype(F32)
                dbl = ebl * jnp.sum(dst * st, axis=0, keepdims=True) + jnp.sum(dkd_kd, axis=0, keepdims=True)
                da_b = da.astype(BF16)
                dqs_parts = []
                dk_in = jnp.zeros((C, DK), F32)
                db_k = jnp.zeros((C, DK), F32)
                for blk in range(C // S):
                    e, k_b = ks[blk]
                    da_blk = da_b[blk * S:(blk + 1) * S]
                    dqs_parts.append(_dot(da_blk, k_b, 1, 0))
                    dks = _dot(da_blk, qs_b[blk * S:(blk + 1) * S], 0, 0)
                    dk_in = dk_in + dks * e
                    db_k = db_k + dks * k_b.astype(F32)
                dqs = jnp.concatenate(dqs_parts, axis=0)
                dqt_in = dqs * ew
                db = qs_b.astype(F32) * dqs - db_k + dqd * qd_b.astype(F32) - dkd_kd
                db = db + jnp.where(row == C - 1, dbl, 0.0)
                dlogf = _split_dot(tri_b, db, 0, 0)
                dqt = dqt_in + dqd * eb
                dkf = dk_in + dkd * ekd
                dfg = dlogf / fg - dkf
                dlb_ref[:, cs] += jnp.sum(dfg * (1.0 - sg), axis=0, keepdims=True)
                dz_ref[rows, cs] = (dqt * sgq * (1.0 + q * (1.0 - sgq))).astype(BF16)
                dz_ref[rows, HG_W + h * DK:HG_W + (h + 1) * DK] = (dfg * (1.0 - lb) * sg * (1.0 - sg)).astype(BF16)
                dz_ref[rows, 2 * HG_W + h * DK:2 * HG_W + (h + 1) * DK] = dv.astype(BF16)
                dz_ref[rows, 3 * HG_W + h * DK:3 * HG_W + (h + 1) * DK] = dgate.astype(BF16)
            return carry

        lax.fori_loop(0, cpb, chunk, 0)

        @pl.when(step == nb - 1)
        def _():
            d0 = dlb_ref[...] * lb_all * (1.0 - lb_all)
            dlg_ref[0:1, :] = d0
            dlg_ref[1:2, :] = -d0

    rev = lambda i: (nb - 1 - i, 0)
    fix = lambda i: (0, 0)
    return _pallas(
        body, name="hgrn_bwd", grid=(nb,),
        in_specs=[pl.BlockSpec((RB, HG_COLS), rev), pl.BlockSpec((RB, HG_W), rev), pl.BlockSpec((RB, HG_W), rev),
                  pl.BlockSpec((cpb * HG_HEADS * DK, DK), rev), pl.BlockSpec((2, HG_W), fix),
                  pl.BlockSpec((1, DK), fix)],
        out_specs=[pl.BlockSpec((RB, HG_COLS), rev), pl.BlockSpec((2, HG_W), fix), pl.BlockSpec((1, DK), fix)],
        out_shape=[jax.ShapeDtypeStruct((T, HG_COLS), BF16), jax.ShapeDtypeStruct((2, HG_W), F32),
                   jax.ShapeDtypeStruct((1, DK), F32)],
        scratch_shapes=[pltpu.VMEM((HG_HEADS, DK, DK), F32), pltpu.VMEM((1, HG_W), F32)],
        compiler_params=_params(dimension_semantics=("arbitrary",)),
    )(z, o_raw, dy, states, lb_logits, gain)


def _head_ones():
    r, c = _iota((FOX_W, FOX_W), 0), _iota((FOX_W, FOX_W), 1)
    return ((r // FOX_DH) == (c // FOX_DH)).astype(BF16)


def _log_sigmoid(x):
    return jnp.minimum(x, 0.0) - jnp.log(1.0 + jnp.exp(-jnp.abs(x)))


def _fox_prep(z, bias, qg, kg):
    T = z.shape[0]
    tm = min(ROW_BLOCK, T)
    nb = T // tm

    def body(z_ref, b_ref, qg_ref, kg_ref, q_ref, k_ref, v_ref, ct_ref, carry_ref):
        @pl.when(pl.program_id(0) == 0)
        def _():
            carry_ref[...] = jnp.zeros_like(carry_ref)

        ones = _head_ones()
        for src, g_ref, dst, scale in ((0, qg_ref, q_ref, FOX_DH ** -0.5), (1, kg_ref, k_ref, 1.0)):
            xv = z_ref[:, src * FOX_W:(src + 1) * FOX_W]
            ms = _split_dot(ones, xv * xv, 1, 0, mat_first=False) * (1.0 / FOX_DH)
            dst[...] = (xv * lax.rsqrt(ms + EPS) * (g_ref[...] * scale)).astype(BF16)
        v_ref[...] = z_ref[:, 2 * FOX_W:3 * FOX_W].astype(BF16)
        logf = _log_sigmoid(z_ref[:, 3 * FOX_W:] + b_ref[...])
        r, c = _iota((tm, tm), 0), _iota((tm, tm), 1)
        tri_b = (c <= r).astype(BF16)
        cum = _split_dot(tri_b, logf, 1, 0, terms=3) + carry_ref[...]
        carry_ref[...] = cum[tm - 1:tm]
        ct_ref[...] = cum.T[0:FOX_HEADS]

    return _pallas(
        body, name="fox_prep", grid=(nb,),
        in_specs=[pl.BlockSpec((tm, FOX_COLS), lambda i: (i, 0)), pl.BlockSpec((1, LANES), lambda i: (0, 0)),
                  pl.BlockSpec((1, FOX_W), lambda i: (0, 0)), pl.BlockSpec((1, FOX_W), lambda i: (0, 0))],
        out_specs=[pl.BlockSpec((tm, FOX_W), lambda i: (i, 0))] * 3 + [pl.BlockSpec((FOX_HEADS, tm), lambda i: (0, i))],
        out_shape=[jax.ShapeDtypeStruct((T, FOX_W), BF16)] * 3 + [jax.ShapeDtypeStruct((FOX_HEADS, T), F32)],
        scratch_shapes=[pltpu.VMEM((1, LANES), F32)],
        compiler_params=_params(dimension_semantics=("arbitrary",)),
    )(z, bias, qg, kg)


def _fox_fwd(qn, kn, vb, ct):
    T = qn.shape[0]
    tq = min(ROW_BLOCK, T)
    nq = T // tq
    NEG = -1e30

    def body(q_ref, k_ref, v_ref, ct_ref, o_ref, lse_ref, m_sc, l_sc, acc_sc):
        pr, qi, ki = pl.program_id(0), pl.program_id(1), pl.program_id(2)

        @pl.when(ki == 0)
        def _():
            m_sc[...] = jnp.full_like(m_sc, NEG)
            l_sc[...] = jnp.zeros_like(l_sc)
            acc_sc[...] = jnp.zeros_like(acc_sc)

        lane = _iota((tq, LANES), 1)

        def block(masked):
            qv, kv, vv = q_ref[...], k_ref[...], v_ref[...]
            for hh in range(2):
                in_head = (lane < FOX_DH) if hh == 0 else (lane >= FOX_DH)
                qm = jnp.where(in_head, qv, jnp.zeros_like(qv))
                s = _dot(qm, kv, 1, 1) - ct_ref[pl.ds(2 * pr + hh, 1), :]
                if masked:
                    s = jnp.where(_iota((tq, tq), 0) >= _iota((tq, tq), 1), s, NEG)
                m_old = m_sc[hh]
                m_new = jnp.maximum(m_old, jnp.max(s, axis=-1, keepdims=True))
                alpha = jnp.exp(m_old - m_new)
                p = jnp.exp(s - m_new)
                l_sc[hh] = alpha * l_sc[hh] + jnp.sum(p, axis=-1, keepdims=True)
                acc_sc[hh] = alpha * acc_sc[hh] + _dot(p.astype(BF16), vv, 1, 0)
                m_sc[hh] = m_new

        @pl.when(ki < qi)
        def _():
            block(False)

        @pl.when(ki == qi)
        def _():
            block(True)
            inv0, inv1 = 1.0 / l_sc[0], 1.0 / l_sc[1]
            o_ref[...] = jnp.where(lane < FOX_DH, acc_sc[0] * inv0, acc_sc[1] * inv1)
            lse_ref[...] = jnp.where(lane < FOX_DH, m_sc[0] + jnp.log(l_sc[0]), m_sc[1] + jnp.log(l_sc[1]))

    kmap = lambda p, i, j: (jnp.minimum(i, j), p)
    return _pallas(
        body, name="fox_fwd", grid=(FOX_HEADS // 2, nq, nq),
        in_specs=[pl.BlockSpec((tq, LANES), lambda p, i, j: (i, p)), pl.BlockSpec((tq, LANES), kmap),
                  pl.BlockSpec((tq, LANES), kmap), pl.BlockSpec((FOX_HEADS, tq), lambda p, i, j: (0, jnp.minimum(i, j)))],
        out_specs=[pl.BlockSpec((tq, LANES), lambda p, i, j: (i, p))] * 2,
        out_shape=[jax.ShapeDtypeStruct((T, FOX_W), F32)] * 2,
        scratch_shapes=[pltpu.VMEM((2, tq, 1), F32), pltpu.VMEM((2, tq, 1), F32), pltpu.VMEM((2, tq, LANES), F32)],
        compiler_params=_params(dimension_semantics=("arbitrary",) * 3),
    )(qn, kn, vb, ct)


def _fox_bwd(qn, kn, vb, ct, o, do, lse):
    T = qn.shape[0]
    tq = min(ROW_BLOCK, T)
    nq = T // tq

    def body(q_ref, k_ref, v_ref, ct_ref, o_ref, do_ref, lse_ref, dq_ref, dk_ref, dv_ref, dcs_ref, drs_ref):
        pr, ki, qi = pl.program_id(0), pl.program_id(1), pl.program_id(2)

        @pl.when(jnp.logical_and(ki == 0, qi == 0))
        def _():
            dq_ref[...] = jnp.zeros_like(dq_ref)

        @pl.when(jnp.logical_and(pr == 0, jnp.logical_and(ki == 0, qi == 0)))
        def _():
            drs_ref[...] = jnp.zeros_like(drs_ref)

        @pl.when(qi == ki)
        def _():
            dk_ref[...] = jnp.zeros_like(dk_ref)
            dv_ref[...] = jnp.zeros_like(dv_ref)
            dcs_ref[...] = jnp.zeros_like(dcs_ref)

        def block(masked):
            lane = _iota((tq, LANES), 1)
            qv, kv, vv = q_ref[...], k_ref[...], v_ref[...]
            ov, dov, lsev = o_ref[...], do_ref[...], lse_ref[...]
            qrows = pl.ds(pl.multiple_of(qi * tq, tq), tq)
            dq_acc = jnp.zeros((tq, LANES), F32)
            dk_acc = jnp.zeros((tq, LANES), F32)
            dv_acc = jnp.zeros((tq, LANES), F32)
            dcs_acc = jnp.zeros((8, tq), F32)
            drs_acc = jnp.zeros((tq, LANES), F32)
            prod = dov * ov
            for hh in range(2):
                in_head = (lane < FOX_DH) if hh == 0 else (lane >= FOX_DH)
                zb = jnp.zeros_like(qv)
                qm = jnp.where(in_head, qv, zb)
                km = jnp.where(in_head, kv, zb)
                dom = jnp.where(in_head, dov, 0.0).astype(BF16)
                delta = jnp.sum(jnp.where(in_head, prod, 0.0), axis=1, keepdims=True)
                lse_h = jnp.sum(jnp.where(lane == hh * FOX_DH, lsev, 0.0), axis=1, keepdims=True)
                s = _dot(qm, kv, 1, 1) - ct_ref[pl.ds(2 * pr + hh, 1), :]
                p = jnp.exp(s - lse_h)
                if masked:
                    p = jnp.where(_iota((tq, tq), 0) >= _iota((tq, tq), 1), p, 0.0)
                p_b = p.astype(BF16)
                dv_acc = dv_acc + _dot(p_b, dom, 0, 0)
                dp = _dot(dom, vv, 1, 1)
                ds = p * (dp - delta)
                ds_b = ds.astype(BF16)
                dq_acc = dq_acc + _dot(ds_b, km, 1, 0)
                dk_acc = dk_acc + _dot(ds_b, qm, 0, 0)
                colsum = jnp.sum(ds, axis=0, keepdims=True)
                dcs_acc = dcs_acc + jnp.where(_iota((8, tq), 0) == hh, colsum, 0.0)
                rowsum = jnp.sum(ds, axis=1, keepdims=True)
                drs_acc = drs_acc + jnp.where(lane == 2 * pr + hh, rowsum, 0.0)
            drs_ref[qrows, :] += drs_acc
            dq_ref[qrows, :] += dq_acc
            dk_ref[...] += dk_acc
            dv_ref[...] += dv_acc
            dcs_ref[0] += dcs_acc

        @pl.when(qi > ki)
        def _():
            block(False)

        @pl.when(qi == ki)
        def _():
            block(True)

    qmap = lambda p, j, i: (jnp.maximum(i, j), p)
    return _pallas(
        body, name="fox_bwd", grid=(FOX_HEADS // 2, nq, nq),
        in_specs=[pl.BlockSpec((tq, LANES), qmap), pl.BlockSpec((tq, LANES), lambda p, j, i: (j, p)),
                  pl.BlockSpec((tq, LANES), lambda p, j, i: (j, p)),
                  pl.BlockSpec((FOX_HEADS, tq), lambda p, j, i: (0, j)),
                  pl.BlockSpec((tq, LANES), qmap), pl.BlockSpec((tq, LANES), qmap), pl.BlockSpec((tq, LANES), qmap)],
        out_specs=[pl.BlockSpec((T, LANES), lambda p, j, i: (0, p)), pl.BlockSpec((tq, LANES), lambda p, j, i: (j, p)),
                   pl.BlockSpec((tq, LANES), lambda p, j, i: (j, p)), pl.BlockSpec((1, 8, tq), lambda p, j, i: (p, 0, j)),
                   pl.BlockSpec((T, LANES), lambda p, j, i: (0, 0))],
        out_shape=[jax.ShapeDtypeStruct((T, FOX_W), F32)] * 3 + [jax.ShapeDtypeStruct((FOX_HEADS // 2, 8, T), F32),
                                                                jax.ShapeDtypeStruct((T, LANES), F32)],
        compiler_params=_params(dimension_semantics=("arbitrary",) * 3),
    )(qn, kn, vb, ct, o, do, lse)


def _fox_post(z, dq, dk, dv, dcs, drs, bias, qg, kg):
    T = z.shape[0]
    tm = min(ROW_BLOCK, T)
    nb = T // tm

    def body(z_ref, dq_ref, dk_ref, dv_ref, dcs_ref, drs_ref, b_ref, qg_ref, kg_ref, dz_ref, dqg_ref, dkg_ref, db_ref,
             carry_ref):
        @pl.when(pl.program_id(0) == 0)
        def _():
            carry_ref[...] = jnp.zeros_like(carry_ref)
            dqg_ref[...] = jnp.zeros_like(dqg_ref)
            dkg_ref[...] = jnp.zeros_like(dkg_ref)
            db_ref[...] = jnp.zeros_like(db_ref)

        ones = _head_ones()
        for src, g_ref, d_ref, dg_ref, scale in ((0, qg_ref, dq_ref, dqg_ref, FOX_DH ** -0.5), (1, kg_ref, dk_ref, dkg_ref, 1.0)):
            xv = z_ref[:, src * FOX_W:(src + 1) * FOX_W]
            ms = _split_dot(ones, xv * xv, 1, 0, mat_first=False) * (1.0 / FOX_DH)
            rstd = lax.rsqrt(ms + EPS)
            xh = xv * rstd
            dn = d_ref[...] * scale
            dg_ref[...] += jnp.sum(dn * xh, axis=0, keepdims=True)
            dxh = dn * g_ref[...]
            mean = _split_dot(ones, dxh * xh, 1, 0, mat_first=False) * (1.0 / FOX_DH)
            dz_ref[:, src * FOX_W:(src + 1) * FOX_W] = (rstd * (dxh - xh * mean)).astype(BF16)
        dz_ref[:, 2 * FOX_W:3 * FOX_W] = dv_ref[...].astype(BF16)
        row8 = _iota((8, tm), 0)
        dct = jnp.zeros((8, tm), F32)
        for h in range(FOX_HEADS):
            src_row = dcs_ref[h // 2][h % 2:h % 2 + 1, :]
            dct = dct + jnp.where(row8 == h, src_row, 0.0)
        dct = drs_ref[...].T[0:8] - dct
        r, c = _iota((tm, tm), 0), _iota((tm, tm), 1)
        upper_b = (r >= c).astype(BF16)
        rc = _split_dot(upper_b, dct, 1, 0, mat_first=False) + carry_ref[...]
        carry_ref[...] = rc[:, 0:1]
        full = jnp.concatenate([rc, jnp.zeros((LANES - 8, tm), F32)], axis=0)
        dlogf = full.T
        xf = z_ref[:, 3 * FOX_W:] + b_ref[...]
        df = dlogf * (1.0 - _sigmoid(xf))
        dz_ref[:, 3 * FOX_W:] = df.astype(BF16)
        db_ref[...] += jnp.sum(df, axis=0, keepdims=True)

    rev = lambda i: (nb - 1 - i, 0)
    fix2 = lambda i: (0, 0)
    return _pallas(
        body, name="fox_post", grid=(nb,),
        in_specs=[pl.BlockSpec((tm, FOX_COLS), rev), pl.BlockSpec((tm, FOX_W), rev), pl.BlockSpec((tm, FOX_W), rev),
                  pl.BlockSpec((tm, FOX_W), rev), pl.BlockSpec((FOX_HEADS // 2, 8, tm), lambda i: (0, 0, nb - 1 - i)),
                  pl.BlockSpec((tm, LANES), rev),
                  pl.BlockSpec((1, LANES), fix2), pl.BlockSpec((1, FOX_W), fix2), pl.BlockSpec((1, FOX_W), fix2)],
        out_specs=[pl.BlockSpec((tm, FOX_COLS), rev), pl.BlockSpec((1, FOX_W), fix2), pl.BlockSpec((1, FOX_W), fix2),
                   pl.BlockSpec((1, LANES), fix2)],
        out_shape=[jax.ShapeDtypeStruct((T, FOX_COLS), BF16), jax.ShapeDtypeStruct((1, FOX_W), F32),
                   jax.ShapeDtypeStruct((1, FOX_W), F32), jax.ShapeDtypeStruct((1, LANES), F32)],
        scratch_shapes=[pltpu.VMEM((8, 1), F32)],
        compiler_params=_params(dimension_semantics=("arbitrary",)),
    )(z, dq, dk, dv, dcs, drs, bias, qg, kg)


def _local_step(x, p, tgt, sm, W):
    lbl, og, fb = sm["hg_lb_logits"], sm["hg_onorm_g"], sm["fox_f_bias"]
    fbias = jnp.pad(fb, ((0, 0), (0, LANES - FOX_HEADS)))
    qg = jnp.tile(sm["fox_q_norm_g"], (1, FOX_HEADS))
    kg = jnp.tile(sm["fox_k_norm_g"], (1, FOX_HEADS))

    h = _rms_fwd(x, sm["norm_mix_g"], name="rms_mix")
    z_hg = _matmul(h, W["in_hg"], name="mm_z_hg")
    z_fox = _matmul(h, W["in_fox"], name="mm_z_fox")
    z_gate = _matmul(h, W["in_gate"], name="mm_z_gate")
    o_raw, ya, states = _hgrn_fwd(z_hg, lbl, og)
    qn, kn, vb, ct = _fox_prep(z_fox, fbias, qg, kg)
    yb, lse = _fox_fwd(qn, kn, vb, ct)
    ua = _matmul(ya, W["w_branch_a"], name="mm_ua")
    ub = _matmul(yb, W["w_branch_b"], name="mm_ub")
    merged = _merge_fwd(ua, ub, z_gate)
    x1 = _matmul(merged, W["w_out"], add=x, name="mm_x1")
    hf = _rms_fwd(x1, sm["norm_ffn_g"], name="rms_ffn")
    a = _matmul(hf, W["w_ffn_gate"], out_dtype=BF16, name="mm_ffn_a")
    b = _matmul(hf, W["w_ffn_up"], out_dtype=BF16, name="mm_ffn_b")
    act = _swiglu_fwd(a, b)
    x2 = _matmul(act, W["w_ffn_down"], add=x1, name="mm_x2")
    hp = _rms_fwd(x2, sm["norm_ple_g"], name="rms_ple")
    sp = _matmul(hp, W["w_ple_gate"], name="mm_sp")
    pp = _matmul(p, W["w_ple_proj"], name="mm_pp")
    dy, dsp, dpp, loss = _ple_loss(x2, sp, pp, tgt)

    G = {}
    G["w_ple_proj"] = _matmul(p, dpp, ta=True, name="mm_dw_ple_proj")
    G["w_ple_gate"] = _matmul(hp, dsp, ta=True, name="mm_dw_ple_gate")
    dhp = _matmul(dsp, W["w_ple_gate"], tb=True, name="mm_dhp")
    dx2, d_ple_g = _rms_bwd(x2, sm["norm_ple_g"], dhp, dy, name="rms_ple_bwd")
    dact = _matmul(dx2, W["w_ffn_down"], tb=True, name="mm_dact")
    G["w_ffn_down"] = _matmul(act, dx2, ta=True, name="mm_dw_ffn_down")
    da, db = _swiglu_bwd(a, b, dact)
    G["w_ffn_gate"] = _matmul(hf, da, ta=True, name="mm_dw_ffn_gate")
    G["w_ffn_up"] = _matmul(hf, db, ta=True, name="mm_dw_ffn_up")
    dhf = _matmul(da, W["w_ffn_gate"], tb=True, name="mm_dhf_a")
    dhf = _matmul(db, W["w_ffn_up"], tb=True, add=dhf, name="mm_dhf_b")
    dx1, d_ffn_g = _rms_bwd(x1, sm["norm_ffn_g"], dhf, dx2, name="rms_ffn_bwd")
    dmerged = _matmul(dx1, W["w_out"], tb=True, name="mm_dmerged")
    G["w_out"] = _matmul(merged, dx1, ta=True, name="mm_dw_out")
    dua, dub, dz_gate = _merge_bwd(dmerged, ua, ub, z_gate)
    G["w_branch_a"] = _matmul(ya, dua, ta=True, name="mm_dw_branch_a")
    G["w_branch_b"] = _matmul(yb, dub, ta=True, name="mm_dw_branch_b")
    dya = _matmul(dua, W["w_branch_a"], tb=True, name="mm_dya")
    dyb = _matmul(dub, W["w_branch_b"], tb=True, name="mm_dyb")
    dq, dk, dv, dcs, drs = _fox_bwd(qn, kn, vb, ct, yb, dyb, lse)
    dz_fox, d_qg, d_kg, d_fb = _fox_post(z_fox, dq, dk, dv, dcs, drs, fbias, qg, kg)
    dz_hg, d_lbl, d_og = _hgrn_bwd(z_hg, o_raw, dya, states, lbl, og)
    dh = _matmul(dz_hg, W["in_hg"], tb=True, name="mm_dh_hg")
    dh = _matmul(dz_fox, W["in_fox"], tb=True, add=dh, name="mm_dh_fox")
    dh = _matmul(dz_gate, W["in_gate"], tb=True, add=dh, name="mm_dh_gate")
    g_hg = _matmul(h, dz_hg, ta=True, name="mm_dw_in_hg")
    g_fox = _matmul(h, dz_fox, ta=True, name="mm_dw_in_fox")
    g_gate = _matmul(h, dz_gate, ta=True, name="mm_dw_in_gate")
    G["w_in"] = jnp.concatenate([g_hg, g_fox[:, :3 * FOX_W + FOX_HEADS], g_gate], axis=1)
    grad_x, d_mix_g = _rms_bwd(x, sm["norm_mix_g"], dh, dx1, name="rms_mix_bwd")

    gs = {"norm_mix_g": d_mix_g, "hg_lb_logits": d_lbl, "hg_onorm_g": d_og, "fox_f_bias": d_fb[:, :FOX_HEADS],
          "fox_q_norm_g": d_qg.reshape(FOX_HEADS, FOX_DH).sum(0, keepdims=True),
          "fox_k_norm_g": d_kg.reshape(FOX_HEADS, FOX_DH).sum(0, keepdims=True),
          "norm_ffn_g": d_ffn_g, "norm_ple_g": d_ple_g}
    return loss, grad_x, gs, G


def _pack_rows(parts, total):
    buf = jnp.concatenate(parts, axis=-2)
    pad = total - buf.shape[-2]
    widths = [(0, 0)] * (buf.ndim - 2) + [(0, pad), (0, 0)]
    return jnp.pad(buf, widths)


def _pack_shards(shards):
    return _pack_rows([shards[n].reshape(PACK_ROWS[n], LANES) for n in BIG], PACK_TOTAL)


def _unpack_shards(buf):
    out, r0 = {}, 0
    for n in BIG:
        r, c, ax = BIG_SHAPE[n]
        shp = (1, r // N_DEV, c) if ax == 0 else (1, r, c // N_DEV)
        out[n] = buf[r0:r0 + PACK_ROWS[n]].reshape(shp)
        r0 += PACK_ROWS[n]
    return out


def _unpack_full(buf):
    out, r0 = {}, 0
    for n in BIG:
        r, c, ax = BIG_SHAPE[n]
        seg = buf[:, r0:r0 + PACK_ROWS[n]]
        if ax == 0:
            out[n] = seg.reshape(r, c)
        else:
            out[n] = seg.reshape(N_DEV, r, c // N_DEV).transpose(1, 0, 2).reshape(r, c)
        r0 += PACK_ROWS[n]
    return out


def _pack_full(G):
    parts = []
    for n in BIG:
        r, c, ax = BIG_SHAPE[n]
        g = G[n]
        if ax == 1:
            g = g.reshape(r, N_DEV, c // N_DEV).transpose(1, 0, 2)
        parts.append(g.reshape(N_DEV, PACK_ROWS[n], LANES))
    return _pack_rows(parts, PACK_TOTAL)


def _pack_small(vals, loss_row=None):
    parts = [vals[n].reshape(SMALL_ROWS[n], -1) for n in SMALL]
    parts = [jnp.pad(v, ((0, 0), (0, LANES - v.shape[1]))) for v in parts]
    if loss_row is not None:
        parts.append(loss_row)
    return _pack_rows(parts, SMALL_TOTAL)


def _unpack_small(buf, like):
    out, r0 = {}, 0
    for n in SMALL:
        rows, size = SMALL_ROWS[n], like[n].size
        blk = buf[r0:r0 + rows]
        out[n] = (blk if size == rows * LANES else blk[:, :size]).reshape(like[n].shape)
        r0 += rows
    return out


def _place():
    return lax.axis_index("x"), lax.axis_index("y"), lax.axis_index("c")


def _all_gather(block):
    R = block.shape[0]

    def body(x_ref, out_ref, send_sems, recv_sems, local_sem):
        x, y, c = _place()
        me, sibling = (x, y, c), (x, y, 1 - c)
        chips = [(1 - x, y), (x, 1 - y), (1 - x, 1 - y)]

        def slot(px, py, pc):
            return out_ref.at[4 * px + 2 * py + pc]

        def copy(k, blk, to, src=None):
            return pltpu.make_async_remote_copy(
                src_ref=slot(*blk) if src is None else src, dst_ref=slot(*blk),
                send_sem=send_sems.at[k], recv_sem=recv_sems.at[k], device_id=to, device_id_type=MESH)

        mine = pltpu.make_async_copy(x_ref, slot(*me), local_sem)
        mine.start()
        first = [copy(0, me, sibling, src=x_ref)]
        first += [copy(1 + j, me, (*chip, c), src=x_ref) for j, chip in enumerate(chips)]
        for cp in first:
            cp.start()
        passed = [copy(4 + j, (*chip, c), sibling) for j, chip in enumerate(chips)]
        for j, chip in enumerate(chips):
            copy(1 + j, (*chip, c), me).wait_recv()
            passed[j].start()
        copy(0, sibling, me).wait_recv()
        for j, chip in enumerate(chips):
            copy(4 + j, (*chip, 1 - c), me).wait_recv()
        for cp in first + passed:
            cp.wait_send()
        mine.wait()

    return _pallas(
        body, name="weights_all_gather", out_shape=jax.ShapeDtypeStruct((N_DEV, R, LANES), block.dtype),
        in_specs=[pl.BlockSpec(memory_space=pl.ANY)], out_specs=pl.BlockSpec(memory_space=pl.ANY),
        scratch_shapes=[pltpu.SemaphoreType.DMA((7,)), pltpu.SemaphoreType.DMA((7,)), pltpu.SemaphoreType.DMA],
            )(block)


def _sibling_exchange(g4):
    R = g4.shape[2]

    def body(g_ref, out_ref, send_sem, recv_sem):
        x, y, c = _place()
        cp = pltpu.make_async_remote_copy(
            src_ref=g_ref.at[:, pl.ds(1 - c, 1)], dst_ref=out_ref, send_sem=send_sem, recv_sem=recv_sem,
            device_id=(x, y, 1 - c), device_id_type=MESH)
        cp.start()
        cp.wait()

    return _pallas(
        body, name="grads_to_sibling", out_shape=jax.ShapeDtypeStruct((4, 1, R, LANES), g4.dtype),
        in_specs=[pl.BlockSpec(memory_space=pl.ANY)], out_specs=pl.BlockSpec(memory_space=pl.ANY),
        scratch_shapes=[pltpu.SemaphoreType.DMA, pltpu.SemaphoreType.DMA],
            )(g4)


def _chip_sum(g4, got, core):
    R = g4.shape[2]

    def body(c_ref, g_ref, r_ref, f_ref, h_ref):
        s = g_ref[0, 0] + r_ref[0, 0]
        f_ref[0] = s
        h_ref[0] = s.astype(BF16)

    grid_spec = pltpu.PrefetchScalarGridSpec(
        num_scalar_prefetch=1, grid=(4, R // PACK_BLOCK),
        in_specs=[pl.BlockSpec((1, 1, PACK_BLOCK, LANES), lambda j, i, c: (j, c[0], i, 0)),
                  pl.BlockSpec((1, 1, PACK_BLOCK, LANES), lambda j, i, c: (j, 0, i, 0))],
        out_specs=[pl.BlockSpec((1, PACK_BLOCK, LANES), lambda j, i, c: (j, i, 0))] * 2)
    return _pallas(
        body, name="chip_sum", grid_spec=grid_spec,
        out_shape=[jax.ShapeDtypeStruct((4, R, LANES), F32), jax.ShapeDtypeStruct((4, R, LANES), BF16)],
        compiler_params=_params(dimension_semantics=("arbitrary",) * 2),
    )(core, g4, got)


def _chip_exchange(hb):
    R = hb.shape[1]

    def body(h_ref, out_ref, send_sems, recv_sems):
        x, y, c = _place()
        chips = [(1 - x, y), (x, 1 - y), (1 - x, 1 - y)]
        cps = [pltpu.make_async_remote_copy(
            src_ref=h_ref.at[2 * px + py], dst_ref=out_ref.at[k], send_sem=send_sems.at[k], recv_sem=recv_sems.at[k],
            device_id=(px, py, c), device_id_type=MESH) for k, (px, py) in enumerate(chips)]
        for cp in cps:
            cp.start()
        for cp in cps:
            cp.wait()

    return _pallas(
        body, name="grads_to_chips", out_shape=jax.ShapeDtypeStruct((3, R, LANES), hb.dtype),
        in_specs=[pl.BlockSpec(memory_space=pl.ANY)], out_specs=pl.BlockSpec(memory_space=pl.ANY),
        scratch_shapes=[pltpu.SemaphoreType.DMA((3,)), pltpu.SemaphoreType.DMA((3,))],
            )(hb)


def _adam_math(w, g, m, v):
    m = ADAM_B1 * m + (1.0 - ADAM_B1) * g
    v = ADAM_B2 * v + (1.0 - ADAM_B2) * (g * g)
    m_hat = m / (1.0 - ADAM_B1 ** ADAM_STEP)
    v_hat = v / (1.0 - ADAM_B2 ** ADAM_STEP)
    delta = -ADAM_LR * (m_hat / (jnp.sqrt(v_hat) + ADAM_EPS) + ADAM_WD * w)
    return delta, m, v


def _adam_big(hf, got, chip, w, m, v):
    R = w.shape[0]

    def body(j_ref, h_ref, r_ref, w_ref, m_ref, v_ref, g_ref, d_ref, nm_ref, nv_ref):
        g = h_ref[0] + r_ref[0].astype(F32) + r_ref[1].astype(F32) + r_ref[2].astype(F32)
        d, nm, nv = _adam_math(w_ref[...], g, m_ref[...], v_ref[...])
        g_ref[...] = g
        d_ref[...] = d
        nm_ref[...] = nm
        nv_ref[...] = nv

    blk = pl.BlockSpec((PACK_BLOCK, LANES), lambda i, j: (i, 0))
    grid_spec = pltpu.PrefetchScalarGridSpec(
        num_scalar_prefetch=1, grid=(R // PACK_BLOCK,),
        in_specs=[pl.BlockSpec((1, PACK_BLOCK, LANES), lambda i, j: (j[0], i, 0)),
                  pl.BlockSpec((3, PACK_BLOCK, LANES), lambda i, j: (0, i, 0)), blk, blk, blk],
        out_specs=[blk] * 4)
    return _pallas(
        body, name="adam_big", grid_spec=grid_spec, out_shape=[jax.ShapeDtypeStruct((R, LANES), F32)] * 4,
        compiler_params=_params(dimension_semantics=("arbitrary",)),
    )(chip, hf, got, w, m, v)


def _small_all_reduce_adam(gs, w, m, v):
    def body(g_ref, w_ref, m_ref, v_ref, sum_ref, d_ref, nm_ref, nv_ref, gather, send_sems, recv_sems):
        x, y, c = _place()
        my = 4 * x + 2 * y + c
        gather[my] = g_ref[...]
        cps = []
        for k in range(1, N_DEV):
            to = (x ^ (k >> 2), y ^ ((k >> 1) & 1), c ^ (k & 1))
            cps.append(pltpu.make_async_remote_copy(
                src_ref=g_ref, dst_ref=gather.at[my], send_sem=send_sems.at[k - 1], recv_sem=recv_sems.at[k - 1],
                device_id=to, device_id_type=MESH))
        for cp in cps:
            cp.start()
        for cp in cps:
            cp.wait()
        total = gather[0]
        for d in range(1, N_DEV):
            total = total + gather[d]
        dlt, nm, nv = _adam_math(w_ref[...], total, m_ref[...], v_ref[...])
        sum_ref[...] = total
        d_ref[...] = dlt
        nm_ref[...] = nm
        nv_ref[...] = nv

    vm = pl.BlockSpec(memory_space=pltpu.VMEM)
    return _pallas(
        body, name="small_all_reduce_adam", out_shape=[jax.ShapeDtypeStruct((SMALL_TOTAL, LANES), F32)] * 4,
        in_specs=[vm] * 4, out_specs=[vm] * 4,
        scratch_shapes=[pltpu.VMEM((N_DEV, SMALL_TOTAL, LANES), F32), pltpu.SemaphoreType.DMA((7,)),
                        pltpu.SemaphoreType.DMA((7,))],
            )(gs, w, m, v)


def kernel(x, p, norm_mix_g, w_in, hg_lb_logits, hg_onorm_g, fox_f_bias, fox_q_norm_g, fox_k_norm_g, w_branch_a, w_branch_b, w_out, norm_ffn_g, w_ffn_gate, w_ffn_up, w_ffn_down, norm_ple_g, w_ple_gate, w_ple_proj, loss_target, m_norm_mix_g, m_w_in, m_hg_lb_logits, m_hg_onorm_g, m_fox_f_bias, m_fox_q_norm_g, m_fox_k_norm_g, m_w_branch_a, m_w_branch_b, m_w_out, m_norm_ffn_g, m_w_ffn_gate, m_w_ffn_up, m_w_ffn_down, m_norm_ple_g, m_w_ple_gate, m_w_ple_proj, v_norm_mix_g, v_w_in, v_hg_lb_logits, v_hg_onorm_g, v_fox_f_bias, v_fox_q_norm_g, v_fox_k_norm_g, v_w_branch_a, v_w_branch_b, v_w_out, v_norm_ffn_g, v_w_ffn_gate, v_w_ffn_up, v_w_ffn_down, v_norm_ple_g, v_w_ple_gate, v_w_ple_proj):
    args = dict(locals())
    wts = {n: args[n] for n in BIG + SMALL}
    mom = {n: args["m_" + n] for n in BIG + SMALL}
    var = {n: args["v_" + n] for n in BIG + SMALL}
    sm = {n: wts[n] for n in SMALL}

    w_pack = _pack_shards(wts)
    gathered = _all_gather(w_pack.astype(BF16))
    full = _unpack_full(gathered)
    w_in_full = full.pop("w_in")
    full["in_hg"] = w_in_full[:, :HG_COLS]
    full["in_fox"] = jnp.pad(w_in_full[:, HG_COLS:HG_COLS + 3 * FOX_W + FOX_HEADS], ((0, 0), (0, LANES - FOX_HEADS)))
    full["in_gate"] = w_in_full[:, HG_COLS + 3 * FOX_W + FOX_HEADS:]

    loss_blk, grad_x, gs, G = _local_step(x[0], p[0, 0], loss_target[0], sm, full)

    xi, yi, ci = _place()
    g_pack = _pack_full(G).reshape(4, 2, PACK_TOTAL, LANES)
    got_sib = _sibling_exchange(g_pack)
    core = jnp.reshape(ci, (1,)).astype(jnp.int32)
    chip = jnp.reshape(2 * xi + yi, (1,)).astype(jnp.int32)
    h_f32, h_bf16 = _chip_sum(g_pack, got_sib, core)
    got_chips = _chip_exchange(h_bf16)
    g_own, d_own, nm_own, nv_own = _adam_big(h_f32, got_chips, chip, w_pack, _pack_shards(mom), _pack_shards(var))
    g_big, d_big, nm_big, nv_big = (_unpack_shards(t) for t in (g_own, d_own, nm_own, nv_own))

    s_sum, s_d, s_nm, s_nv = _small_all_reduce_adam(
        _pack_small(gs, loss_blk[0:1]), _pack_small(sm), _pack_small({n: mom[n] for n in SMALL}),
        _pack_small({n: var[n] for n in SMALL}))
    loss = s_sum[LOSS_ROW, 0]
    g_small, d_small, nm_small, nv_small = (_unpack_small(t, sm) for t in (s_sum, s_d, s_nm, s_nv))

    order = ["norm_mix_g", "w_in", "hg_lb_logits", "hg_onorm_g", "fox_f_bias", "fox_q_norm_g", "fox_k_norm_g",
             "w_branch_a", "w_branch_b", "w_out", "norm_ffn_g", "w_ffn_gate", "w_ffn_up", "w_ffn_down", "norm_ple_g",
             "w_ple_gate", "w_ple_proj"]
    outs = [loss, grad_x[None]]
    for big, small in ((g_big, g_small), (d_big, d_small), (nm_big, nm_small), (nv_big, nv_small)):
        outs += [big[n] if n in big else small[n] for n in order]
    return tuple(outs)
```

```python
import functools

import jax
import jax.numpy as jnp
from jax import lax
from jax.experimental import pallas as pl
from jax.experimental.pallas import tpu as pltpu

F32 = jnp.float32
BF16 = jnp.bfloat16

D_MODEL = 1024
PLE_DIM = 256
HG_HEADS = 4
HG_DK = 128
HG_CHUNK = 64
HG_SUB = 16
HG_W = HG_HEADS * HG_DK
FOX_HEADS = 8
FOX_DH = 64
FOX_W = FOX_HEADS * FOX_DH
D_FF = 2816
EPS = 1e-6
N_DEV = 8
N_CHIP = 4
LANES = 128
FOX_COLS = 3 * FOX_W + LANES
HG_COLS = 4 * HG_W
GATE_COLS = 2 * D_MODEL
IN_COLS = HG_COLS + 3 * FOX_W + FOX_HEADS + GATE_COLS
FOX_LOGICAL = 3 * FOX_W + FOX_HEADS
SEG = 2048
IN_PAD = 3 * SEG
IN_SHARD = IN_COLS // N_DEV
IN_SHARD_PAD = 768
FF_SHARD = D_FF // N_DEV
FF_SHARD_PAD = 384
FF_PAD = N_DEV * FF_SHARD_PAD
EXP_CLAMP = 80.0

ADAM_LR = 0.001
ADAM_B1 = 0.9
ADAM_B2 = 0.999
ADAM_EPS = 1e-08
ADAM_WD = 0.01
ADAM_STEP = 10

MESH = pl.DeviceIdType.MESH
VMEM_LIMIT = 56 * 1024 * 1024
ROW_BLOCK = 512

BIG = ["w_in", "w_branch_a", "w_branch_b", "w_out", "w_ffn_gate", "w_ffn_up", "w_ffn_down",
       "w_ple_gate", "w_ple_proj"]
BIG_SHAPE = {
    "w_in": (D_MODEL, IN_COLS, 1), "w_branch_a": (HG_W, D_MODEL, 1), "w_branch_b": (FOX_W, D_MODEL, 1),
    "w_out": (D_MODEL, D_MODEL, 0), "w_ffn_gate": (D_MODEL, D_FF, 1), "w_ffn_up": (D_MODEL, D_FF, 1),
    "w_ffn_down": (D_FF, D_MODEL, 0), "w_ple_gate": (D_MODEL, D_MODEL, 0), "w_ple_proj": (PLE_DIM, D_MODEL, 1),
}

SMALL = ["norm_mix_g", "hg_lb_logits", "hg_onorm_g", "fox_f_bias", "fox_q_norm_g", "fox_k_norm_g",
         "norm_ffn_g", "norm_ple_g"]
SMALL_ROWS = {"norm_mix_g": 8, "hg_lb_logits": 8, "hg_onorm_g": 1, "fox_f_bias": 1, "fox_q_norm_g": 1,
              "fox_k_norm_g": 1, "norm_ffn_g": 8, "norm_ple_g": 8}
SMALL_TOTAL = 40
LOSS_ROW = 36


def _pallas(body, **kw):
    return pl.pallas_call(body, **kw)


def _params(**kw):
    return pltpu.CompilerParams(vmem_limit_bytes=VMEM_LIMIT, **kw)


def _pick(n, target):
    if n <= target:
        return n
    best = None
    for t in range(LANES, target + 1, LANES):
        if n % t == 0:
            best = t
    assert best is not None, (n, target)
    return best


def _dot(a, b, ca, cb):
    return lax.dot_general(a, b, (((ca,), (cb,)), ((), ())), preferred_element_type=F32)


def _split_dot(mat, x, ca, cb, terms=2, mat_first=True):
    acc = None
    rem = x
    for _ in range(terms):
        part = rem.astype(BF16)
        rem = rem - part.astype(F32)
        p = _dot(mat, part, ca, cb) if mat_first else _dot(part, mat, ca, cb)
        acc = p if acc is None else acc + p
    return acc


def _sigmoid(x):
    return 1.0 / (1.0 + jnp.exp(-x))


def _iota(shape, dim):
    return lax.broadcasted_iota(jnp.int32, shape, dim)


def _matmul(a, b, *, name, ta=False, tb=False, out_dtype=F32, add=None):
    (K, M) = a.shape if ta else a.shape[::-1]
    (N, Kb) = b.shape if tb else b.shape[::-1]
    assert K == Kb, (a.shape, b.shape, ta, tb)
    if ta:
        tm, tn, tk = _pick(M, 3 * ROW_BLOCK), _pick(N, 2048), _pick(K, ROW_BLOCK)
    else:
        tm, tn, tk = _pick(M, 2 * ROW_BLOCK), _pick(N, 2048), _pick(K, 2048)
    nk = K // tk
    use_scratch = nk > 1 and out_dtype != F32

    def body(*refs):
        refs = list(refs)
        a_ref, b_ref = refs[:2]
        add_ref = refs[2] if add is not None else None
        o_ref = refs[3] if add is not None else refs[2]
        k = pl.program_id(2)
        p = _dot(a_ref[...].astype(BF16), b_ref[...].astype(BF16), 0 if ta else 1, 1 if tb else 0)

        def with_add(r):
            return r if add is None else r + add_ref[...].astype(F32)

        if nk == 1:
            o_ref[...] = with_add(p).astype(out_dtype)
        elif not use_scratch:
            @pl.when(k == 0)
            def _():
                o_ref[...] = with_add(p)

            @pl.when(k > 0)
            def _():
                o_ref[...] += p
        else:
            acc_ref = refs[-1]

            @pl.when(k == 0)
            def _():
                acc_ref[...] = p

            @pl.when(k > 0)
            def _():
                acc_ref[...] += p

            @pl.when(k == nk - 1)
            def _():
                o_ref[...] = with_add(acc_ref[...]).astype(out_dtype)

    a_spec = pl.BlockSpec((tk, tm), lambda i, j, k: (k, i)) if ta else pl.BlockSpec((tm, tk), lambda i, j, k: (i, k))
    b_spec = pl.BlockSpec((tn, tk), lambda i, j, k: (j, k)) if tb else pl.BlockSpec((tk, tn), lambda i, j, k: (k, j))
    o_spec = pl.BlockSpec((tm, tn), lambda i, j, k: (i, j))
    in_specs = [a_spec, b_spec] + ([o_spec] if add is not None else [])
    args = (a, b) + ((add,) if add is not None else ())
    return _pallas(
        body, name=name, grid=(M // tm, N // tn, nk), in_specs=in_specs, out_specs=o_spec,
        out_shape=jax.ShapeDtypeStruct((M, N), out_dtype),
        scratch_shapes=[pltpu.VMEM((tm, tn), F32)] if use_scratch else [],
        compiler_params=_params(dimension_semantics=("arbitrary",) * 3),
    )(*args)


def _row_map(nb, reverse, seg):
    if reverse:
        return lambda i: (nb - 1 - i, seg)
    return lambda i: (i, seg)


def _row_call(body, *, name, T, ins, outs, acc_outs=(), tm=ROW_BLOCK, reverse=False):
    tm = min(tm, T)
    nb = T // tm
    in_specs, args = [], []
    for arr, how in ins:
        args.append(arr)
        if how is True:
            in_specs.append(pl.BlockSpec((tm, arr.shape[1]), _row_map(nb, reverse, 0)))
        elif how is False:
            in_specs.append(pl.BlockSpec(arr.shape, lambda i, _n=arr.ndim: (0,) * _n))
        else:
            in_specs.append(pl.BlockSpec((tm, SEG), _row_map(nb, reverse, how[0])))
    out_specs, out_shape = [], []
    for o in outs:
        c, dt = o[0], o[1]
        total, seg = o[2] if len(o) > 2 else (c, 0)
        out_specs.append(pl.BlockSpec((tm, c), _row_map(nb, reverse, seg)))
        out_shape.append(jax.ShapeDtypeStruct((T, total), dt))
    for shp, dt in acc_outs:
        out_specs.append(pl.BlockSpec(shp, lambda i, _n=len(shp): (0,) * _n))
        out_shape.append(jax.ShapeDtypeStruct(shp, dt))
    return _pallas(body, name=name, grid=(nb,), in_specs=in_specs, out_specs=out_specs, out_shape=out_shape,
                   compiler_params=_params(dimension_semantics=("arbitrary",)))(*args)


def _rms_fwd(x, g, *, name):
    T = x.shape[0]

    def body(x_ref, g_ref, h_ref):
        xv = x_ref[...]
        rstd = lax.rsqrt(jnp.mean(xv * xv, axis=-1, keepdims=True) + EPS)
        h_ref[...] = (xv * rstd * g_ref[...]).astype(BF16)

    return _row_call(body, name=name, T=T, ins=[(x, True), (g, False)], outs=[(D_MODEL, BF16)])[0]


def _rms_bwd(x, g, dh, dres, *, name):
    T = x.shape[0]

    def body(x_ref, g_ref, dh_ref, dres_ref, dx_ref, dg_ref):
        xv = x_ref[...]
        rstd = lax.rsqrt(jnp.mean(xv * xv, axis=-1, keepdims=True) + EPS)
        xh = xv * rstd
        dhv = dh_ref[...]
        part = jnp.sum(dhv * xh, axis=0, keepdims=True)

        @pl.when(pl.program_id(0) == 0)
        def _():
            dg_ref[...] = part

        @pl.when(pl.program_id(0) > 0)
        def _():
            dg_ref[...] += part

        dxh = dhv * g_ref[...]
        dx_ref[...] = rstd * (dxh - xh * jnp.mean(dxh * xh, axis=-1, keepdims=True)) + dres_ref[...]

    return _row_call(body, name=name, T=T, ins=[(x, True), (g, False), (dh, True), (dres, True)],
                     outs=[(D_MODEL, F32)], acc_outs=[((1, D_MODEL), F32)])


def _merge_fwd(ua, ub, zg):
    def body(ua_ref, ub_ref, zg_ref, m_ref):
        ga = _sigmoid(zg_ref[:, :D_MODEL])
        gb = _sigmoid(zg_ref[:, D_MODEL:])
        m_ref[...] = (ga * ua_ref[...] + gb * ub_ref[...]).astype(BF16)

    return _row_call(body, name="merge_fwd", T=ua.shape[0], ins=[(ua, True), (ub, True), (zg, (2,))],
                     outs=[(D_MODEL, BF16)])[0]


def _merge_bwd(dm, ua, ub, zg):
    def body(dm_ref, ua_ref, ub_ref, zg_ref, dua_ref, dub_ref, dzg_ref):
        ga = _sigmoid(zg_ref[:, :D_MODEL])
        gb = _sigmoid(zg_ref[:, D_MODEL:])
        dmv = dm_ref[...]
        dua_ref[...] = (dmv * ga).astype(BF16)
        dub_ref[...] = (dmv * gb).astype(BF16)
        dzg_ref[:, :D_MODEL] = (dmv * ua_ref[...] * ga * (1.0 - ga)).astype(BF16)
        dzg_ref[:, D_MODEL:] = (dmv * ub_ref[...] * gb * (1.0 - gb)).astype(BF16)

    return _row_call(body, name="merge_bwd", T=dm.shape[0], ins=[(dm, True), (ua, True), (ub, True), (zg, (2,))],
                     outs=[(D_MODEL, BF16), (D_MODEL, BF16), (SEG, BF16, (IN_PAD, 2))])


def _swiglu_fwd(a, b):
    def body(a_ref, b_ref, o_ref):
        av = a_ref[...].astype(F32)
        o_ref[...] = (av * _sigmoid(av) * b_ref[...].astype(F32)).astype(BF16)

    return _row_call(body, name="swiglu_fwd", T=a.shape[0], ins=[(a, True), (b, True)], outs=[(a.shape[1], BF16)])[0]


def _swiglu_bwd(a, b, dact):
    def body(a_ref, b_ref, d_ref, da_ref, db_ref):
        av = a_ref[...].astype(F32)
        bv = b_ref[...].astype(F32)
        dv = d_ref[...]
        sg = _sigmoid(av)
        da_ref[...] = (dv * bv * sg * (1.0 + av * (1.0 - sg))).astype(BF16)
        db_ref[...] = (dv * av * sg).astype(BF16)

    return _row_call(body, name="swiglu_bwd", T=a.shape[0], ins=[(a, True), (b, True), (dact, True)],
                     outs=[(a.shape[1], BF16), (a.shape[1], BF16)])


def _ple_loss(x2, sp, pp, tgt):
    def body(x_ref, sp_ref, pp_ref, t_ref, dy_ref, dsp_ref, dpp_ref, loss_ref):
        gp = _sigmoid(sp_ref[...])
        ppv = pp_ref[...]
        err = x_ref[...] + gp * ppv - t_ref[...]
        part = 0.5 * jnp.sum(jnp.mean(err * err, axis=-1, keepdims=True), axis=0, keepdims=True)
        part = jnp.broadcast_to(part, loss_ref.shape)

        @pl.when(pl.program_id(0) == 0)
        def _():
            loss_ref[...] = part

        @pl.when(pl.program_id(0) > 0)
        def _():
            loss_ref[...] += part

        dy = err * (1.0 / D_MODEL)
        dy_ref[...] = dy
        dsp_ref[...] = (dy * ppv * gp * (1.0 - gp)).astype(BF16)
        dpp_ref[...] = (dy * gp).astype(BF16)

    return _row_call(body, name="ple_loss", T=x2.shape[0], ins=[(x2, True), (sp, True), (pp, True), (tgt, True)],
                     outs=[(D_MODEL, F32), (D_MODEL, BF16), (D_MODEL, BF16)], acc_outs=[((8, LANES), F32)])


def _hg_consts():
    C = HG_CHUNK
    r, c = _iota((C, C), 0), _iota((C, C), 1)
    tri = (c <= r)
    same = (r // HG_SUB) == (c // HG_SUB)
    return tri, (tri & same)


def _hg_chunk_fwd(q, f, lb, tri_b, sub_b):
    sgq = _sigmoid(q)
    qt = q * sgq
    sg = _sigmoid(f)
    fg = lb + (1.0 - lb) * sg
    kf = (1.0 - lb) * (1.0 - sg)
    logf = jnp.log(fg)
    b = _split_dot(tri_b, logf, 1, 0)
    w = _split_dot(sub_b, logf, 1, 0)
    return sgq, qt, sg, fg, kf, b, w


def _hg_scores(qs_b, kf, b, row):
    C, S = HG_CHUNK, HG_SUB
    parts, ks = [], []
    for blk in range(C // S):
        ref = jnp.zeros_like(b[0:1]) if blk == 0 else b[blk * S - 1:blk * S]
        e = jnp.exp(jnp.minimum(ref - b, EXP_CLAMP))
        e = jnp.where(row < (blk + 1) * S, e, 0.0)
        k_b = (kf * e).astype(BF16)
        ks.append((e, k_b))
        parts.append(_dot(qs_b[blk * S:(blk + 1) * S], k_b, 1, 1))
    return jnp.concatenate(parts, axis=0), ks


def _hgrn_fwd(z, lb_logits, gain):
    T = z.shape[0]
    RB = min(ROW_BLOCK, T)
    nb, cpb = T // RB, RB // HG_CHUNK
    C, DK = HG_CHUNK, HG_DK

    def body(z_ref, lg_ref, g_ref, o_ref, y_ref, st_ref, s_ref):
        @pl.when(pl.program_id(0) == 0)
        def _():
            s_ref[...] = jnp.zeros_like(s_ref)

        lg = lg_ref[...]
        lb_all = 1.0 / (1.0 + jnp.exp(lg[1:2] - lg[0:1]))
        gain_v = g_ref[...]
        tri, sub = _hg_consts()
        tri_b, sub_b = tri.astype(BF16), sub.astype(BF16)
        row = _iota((C, DK), 0)

        def chunk(ci, carry):
            r0 = pl.multiple_of(ci * C, C)
            rows = pl.ds(r0, C)
            for h in range(HG_HEADS):
                cs = slice(h * DK, (h + 1) * DK)
                q = z_ref[rows, cs]
                f = z_ref[rows, HG_W + h * DK:HG_W + (h + 1) * DK]
                v = z_ref[rows, 2 * HG_W + h * DK:2 * HG_W + (h + 1) * DK]
                g = z_ref[rows, 3 * HG_W + h * DK:3 * HG_W + (h + 1) * DK]
                lb = lb_all[:, cs]
                _, qt, _, _, kf, b, w = _hg_chunk_fwd(q, f, lb, tri_b, sub_b)
                st = s_ref[h]
                st_ref[pl.ds(pl.multiple_of((ci * HG_HEADS + h) * DK, DK), DK), :] = st
                v_b = v.astype(BF16)
                qs_b = (qt * jnp.exp(w)).astype(BF16)
                a, _ = _hg_scores(qs_b, kf, b, row)
                a = jnp.where(tri, a, 0.0)
                qd = qt * jnp.exp(b)
                o = _dot(qd.astype(BF16), st.astype(BF16), 1, 1) + _dot(a.astype(BF16), v_b, 1, 0)
                bl = b[C - 1:C]
                kd = kf * jnp.exp(bl - b)
                s_ref[h] = st * jnp.exp(bl) + _dot(v_b, kd.astype(BF16), 0, 0)
                o_ref[rows, cs] = o
                rstd = lax.rsqrt(jnp.mean(o * o, axis=-1, keepdims=True) + EPS)
                y_ref[rows, cs] = (o * rstd * gain_v * (g * _sigmoid(g))).astype(BF16)
            return carry

        lax.fori_loop(0, cpb, chunk, 0)

    return _pallas(
        body, name="hgrn_fwd", grid=(nb,),
        in_specs=[pl.BlockSpec((RB, HG_COLS), lambda i: (i, 0)), pl.BlockSpec((2, HG_W), lambda i: (0, 0)),
                  pl.BlockSpec((1, DK), lambda i: (0, 0))],
        out_specs=[pl.BlockSpec((RB, HG_W), lambda i: (i, 0)), pl.BlockSpec((RB, HG_W), lambda i: (i, 0)),
                   pl.BlockSpec((cpb * HG_HEADS * DK, DK), lambda i: (i, 0))],
        out_shape=[jax.ShapeDtypeStruct((T, HG_W), F32), jax.ShapeDtypeStruct((T, HG_W), BF16),
                   jax.ShapeDtypeStruct((T // C * HG_HEADS * DK, DK), F32)],
        scratch_shapes=[pltpu.VMEM((HG_HEADS, DK, DK), F32)],
        compiler_params=_params(dimension_semantics=("arbitrary",)),
    )(z, lb_logits, gain)


def _hgrn_bwd(z, o_raw, dy, states, lb_logits, gain, dz_buf):
    T = z.shape[0]
    RB = min(ROW_BLOCK, T)
    nb, cpb = T // RB, RB // HG_CHUNK
    C, DK, S = HG_CHUNK, HG_DK, HG_SUB

    def body(z_ref, o_ref, dy_ref, st_ref, lg_ref, g_ref, _buf_ref, dz_ref, dlg_ref, dg_ref, ds_ref, dlb_ref):
        step = pl.program_id(0)

        @pl.when(step == 0)
        def _():
            ds_ref[...] = jnp.zeros_like(ds_ref)
            dlb_ref[...] = jnp.zeros_like(dlb_ref)
            dg_ref[...] = jnp.zeros_like(dg_ref)

        lg = lg_ref[...]
        lb_all = 1.0 / (1.0 + jnp.exp(lg[1:2] - lg[0:1]))
        gain_v = g_ref[...]
        tri, sub = _hg_consts()
        tri_b, sub_b = tri.astype(BF16), sub.astype(BF16)
        row = _iota((C, DK), 0)

        def chunk(cj, carry):
            ci = cpb - 1 - cj
            r0 = pl.multiple_of(ci * C, C)
            rows = pl.ds(r0, C)
            for h in range(HG_HEADS):
                cs = slice(h * DK, (h + 1) * DK)
                q = z_ref[rows, cs]
                f = z_ref[rows, HG_W + h * DK:HG_W + (h + 1) * DK]
                v = z_ref[rows, 2 * HG_W + h * DK:2 * HG_W + (h + 1) * DK]
                g = z_ref[rows, 3 * HG_W + h * DK:3 * HG_W + (h + 1) * DK]
                lb = lb_all[:, cs]
                sgq, qt, sg, fg, kf, b, w = _hg_chunk_fwd(q, f, lb, tri_b, sub_b)
                st = st_ref[pl.ds(pl.multiple_of((ci * HG_HEADS + h) * DK, DK), DK), :]
                dst = ds_ref[h]
                o = o_ref[rows, cs]
                dyv = dy_ref[rows, cs]
                rstd = lax.rsqrt(jnp.mean(o * o, axis=-1, keepdims=True) + EPS)
                n = o * rstd
                sgg = _sigmoid(g)
                t1 = dyv * (g * sgg)
                dg_ref[...] += jnp.sum(t1 * n, axis=0, keepdims=True)
                dn = t1 * gain_v
                do = rstd * (dn - n * jnp.mean(dn * n, axis=-1, keepdims=True))
                dgate = dyv * n * gain_v * sgg * (1.0 + g * (1.0 - sgg))
                do_b = do.astype(BF16)
                v_b = v.astype(BF16)
                ew = jnp.exp(w)
                qs_b = (qt * ew).astype(BF16)
                a, ks = _hg_scores(qs_b, kf, b, row)
                a = jnp.where(tri, a, 0.0)
                eb = jnp.exp(b)
                qd = qt * eb
                bl = b[C - 1:C]
                ebl = jnp.exp(bl)
                ekd = jnp.exp(bl - b)
                kd = kf * ekd
                kd_b = kd.astype(BF16)
                qd_b = qd.astype(BF16)
                dst_b = dst.astype(BF16)
                dqd = _dot(do_b, st.astype(BF16), 1, 0)
                da = jnp.where(tri, _dot(do_b, v_b, 1, 1), 0.0)
                dv = _dot(a.astype(BF16), do_b, 0, 0) + _dot(kd_b, dst_b, 1, 1)
                dkd = _dot(v_b, dst_b, 1, 0)
                ds_ref[h] = dst * ebl + _dot(do_b, qd_b, 0, 0)
                dkd_kd = dkd * kd_b.astype(F32)
                dbl = ebl * jnp.sum(dst * st, axis=0, keepdims=True) + jnp.sum(dkd_kd, axis=0, keepdims=True)
                da_b = da.astype(BF16)
                dqs_parts = []
                dk_in = jnp.zeros((C, DK), F32)
                db_k = jnp.zeros((C, DK), F32)
                for blk in range(C // S):
                    e, k_b = ks[blk]
                    da_blk = da_b[blk * S:(blk + 1) * S]
                    dqs_parts.append(_dot(da_blk, k_b, 1, 0))
                    dks = _dot(da_blk, qs_b[blk * S:(blk + 1) * S], 0, 0)
                    dk_in = dk_in + dks * e
                    db_k = db_k + dks * k_b.astype(F32)
                dqs = jnp.concatenate(dqs_parts, axis=0)
                dqt_in = dqs * ew
                db = qs_b.astype(F32) * dqs - db_k + dqd * qd_b.astype(F32) - dkd_kd
                db = db + jnp.where(row == C - 1, dbl, 0.0)
                dlogf = _split_dot(tri_b, db, 0, 0)
                dqt = dqt_in + dqd * eb
                dkf = dk_in + dkd * ekd
                dfg = dlogf / fg - dkf
                dlb_ref[:, cs] += jnp.sum(dfg * (1.0 - sg), axis=0, keepdims=True)
                dz_ref[rows, cs] = (dqt * sgq * (1.0 + q * (1.0 - sgq))).astype(BF16)
                dz_ref[rows, HG_W + h * DK:HG_W + (h + 1) * DK] = (dfg * (1.0 - lb) * sg * (1.0 - sg)).astype(BF16)
                dz_ref[rows, 2 * HG_W + h * DK:2 * HG_W + (h + 1) * DK] = dv.astype(BF16)
                dz_ref[rows, 3 * HG_W + h * DK:3 * HG_W + (h + 1) * DK] = dgate.astype(BF16)
            return carry

        lax.fori_loop(0, cpb, chunk, 0)

        @pl.when(step == nb - 1)
        def _():
            d0 = dlb_ref[...] * lb_all * (1.0 - lb_all)
            dlg_ref[0:1, :] = d0
            dlg_ref[1:2, :] = -d0

    rev = lambda i: (nb - 1 - i, 0)
    fix = lambda i: (0, 0)
    return _pallas(
        body, name="hgrn_bwd", grid=(nb,),
        in_specs=[pl.BlockSpec((RB, HG_COLS), rev), pl.BlockSpec((RB, HG_W), rev), pl.BlockSpec((RB, HG_W), rev),
                  pl.BlockSpec((cpb * HG_HEADS * DK, DK), rev), pl.BlockSpec((2, HG_W), fix),
                  pl.BlockSpec((1, DK), fix), pl.BlockSpec(memory_space=pl.ANY)],
        out_specs=[pl.BlockSpec((RB, HG_COLS), rev), pl.BlockSpec((2, HG_W), fix), pl.BlockSpec((1, DK), fix)],
        out_shape=[jax.ShapeDtypeStruct(dz_buf.shape, BF16), jax.ShapeDtypeStruct((2, HG_W), F32),
                   jax.ShapeDtypeStruct((1, DK), F32)],
        scratch_shapes=[pltpu.VMEM((HG_HEADS, DK, DK), F32), pltpu.VMEM((1, HG_W), F32)],
        input_output_aliases={6: 0},
        compiler_params=_params(dimension_semantics=("arbitrary",)),
    )(z, o_raw, dy, states, lb_logits, gain, dz_buf)


def _head_ones():
    r, c = _iota((FOX_W, FOX_W), 0), _iota((FOX_W, FOX_W), 1)
    return ((r // FOX_DH) == (c // FOX_DH)).astype(BF16)


def _log_sigmoid(x):
    return jnp.minimum(x, 0.0) - jnp.log(1.0 + jnp.exp(-jnp.abs(x)))


def _fox_prep(z, bias, qg, kg):
    T = z.shape[0]
    tm = min(ROW_BLOCK, T)
    nb = T // tm

    def body(z_ref, b_ref, qg_ref, kg_ref, q_ref, k_ref, v_ref, ct_ref, carry_ref):
        @pl.when(pl.program_id(0) == 0)
        def _():
            carry_ref[...] = jnp.zeros_like(carry_ref)

        ones = _head_ones()
        for src, g_ref, dst, scale in ((0, qg_ref, q_ref, FOX_DH ** -0.5), (1, kg_ref, k_ref, 1.0)):
            xv = z_ref[:, src * FOX_W:(src + 1) * FOX_W]
            ms = _split_dot(ones, xv * xv, 1, 0, mat_first=False) * (1.0 / FOX_DH)
            dst[...] = (xv * lax.rsqrt(ms + EPS) * (g_ref[...] * scale)).astype(BF16)
        v_ref[...] = z_ref[:, 2 * FOX_W:3 * FOX_W].astype(BF16)
        logf = _log_sigmoid(z_ref[:, 3 * FOX_W:FOX_COLS] + b_ref[...])
        r, c = _iota((tm, tm), 0), _iota((tm, tm), 1)
        tri_b = (c <= r).astype(BF16)
        cum = _split_dot(tri_b, logf, 1, 0, terms=3) + carry_ref[...]
        carry_ref[...] = cum[tm - 1:tm]
        ct_ref[...] = cum.T[0:FOX_HEADS]

    return _pallas(
        body, name="fox_prep", grid=(nb,),
        in_specs=[pl.BlockSpec((tm, SEG), lambda i: (i, 1)), pl.BlockSpec((1, LANES), lambda i: (0, 0)),
                  pl.BlockSpec((1, FOX_W), lambda i: (0, 0)), pl.BlockSpec((1, FOX_W), lambda i: (0, 0))],
        out_specs=[pl.BlockSpec((tm, FOX_W), lambda i: (i, 0))] * 3 + [pl.BlockSpec((FOX_HEADS, tm), lambda i: (0, i))],
        out_shape=[jax.ShapeDtypeStruct((T, FOX_W), BF16)] * 3 + [jax.ShapeDtypeStruct((FOX_HEADS, T), F32)],
        scratch_shapes=[pltpu.VMEM((1, LANES), F32)],
        compiler_params=_params(dimension_semantics=("arbitrary",)),
    )(z, bias, qg, kg)


def _fox_fwd(qn, kn, vb, ct):
    T = qn.shape[0]
    tq = min(ROW_BLOCK, T)
    nq = T // tq
    NEG = -1e30

    def body(q_ref, k_ref, v_ref, ct_ref, o_ref, lse_ref, m_sc, l_sc, acc_sc):
        pr, qi, ki = pl.program_id(0), pl.program_id(1), pl.program_id(2)

        @pl.when(ki == 0)
        def _():
            m_sc[...] = jnp.full_like(m_sc, NEG)
            l_sc[...] = jnp.zeros_like(l_sc)
            acc_sc[...] = jnp.zeros_like(acc_sc)

        lane = _iota((tq, LANES), 1)

        def block(masked):
            qv, kv, vv = q_ref[...], k_ref[...], v_ref[...]
            for hh in range(2):
                in_head = (lane < FOX_DH) if hh == 0 else (lane >= FOX_DH)
                qm = jnp.where(in_head, qv, jnp.zeros_like(qv))
                s = _dot(qm, kv, 1, 1) - ct_ref[pl.ds(2 * pr + hh, 1), :]
                if masked:
                    s = jnp.where(_iota((tq, tq), 0) >= _iota((tq, tq), 1), s, NEG)
                m_old = m_sc[hh]
                m_new = jnp.maximum(m_old, jnp.max(s, axis=-1, keepdims=True))
                alpha = jnp.exp(m_old - m_new)
                p = jnp.exp(s - m_new)
                l_sc[hh] = alpha * l_sc[hh] + jnp.sum(p, axis=-1, keepdims=True)
                acc_sc[hh] = alpha * acc_sc[hh] + _dot(p.astype(BF16), vv, 1, 0)
                m_sc[hh] = m_new

        @pl.when(ki < qi)
        def _():
            block(False)

        @pl.when(ki == qi)
        def _():
            block(True)
            inv0, inv1 = 1.0 / l_sc[0], 1.0 / l_sc[1]
            o_ref[...] = jnp.where(lane < FOX_DH, acc_sc[0] * inv0, acc_sc[1] * inv1)
            lse_ref[...] = jnp.where(lane < FOX_DH, m_sc[0] + jnp.log(l_sc[0]), m_sc[1] + jnp.log(l_sc[1]))

    kmap = lambda p, i, j: (jnp.minimum(i, j), p)
    return _pallas(
        body, name="fox_fwd", grid=(FOX_HEADS // 2, nq, nq),
        in_specs=[pl.BlockSpec((tq, LANES), lambda p, i, j: (i, p)), pl.BlockSpec((tq, LANES), kmap),
                  pl.BlockSpec((tq, LANES), kmap), pl.BlockSpec((FOX_HEADS, tq), lambda p, i, j: (0, jnp.minimum(i, j)))],
        out_specs=[pl.BlockSpec((tq, LANES), lambda p, i, j: (i, p))] * 2,
        out_shape=[jax.ShapeDtypeStruct((T, FOX_W), F32)] * 2,
        scratch_shapes=[pltpu.VMEM((2, tq, 1), F32), pltpu.VMEM((2, tq, 1), F32), pltpu.VMEM((2, tq, LANES), F32)],
        compiler_params=_params(dimension_semantics=("arbitrary",) * 3),
    )(qn, kn, vb, ct)


def _fox_bwd(qn, kn, vb, ct, o, do, lse):
    T = qn.shape[0]
    tq = min(ROW_BLOCK, T)
    nq = T // tq

    def body(q_ref, k_ref, v_ref, ct_ref, o_ref, do_ref, lse_ref, dq_ref, dk_ref, dv_ref, dcs_ref, drs_ref):
        pr, ki, qi = pl.program_id(0), pl.program_id(1), pl.program_id(2)

        @pl.when(jnp.logical_and(ki == 0, qi == 0))
        def _():
            dq_ref[...] = jnp.zeros_like(dq_ref)

        @pl.when(jnp.logical_and(pr == 0, jnp.logical_and(ki == 0, qi == 0)))
        def _():
            drs_ref[...] = jnp.zeros_like(drs_ref)

        @pl.when(qi == ki)
        def _():
            dk_ref[...] = jnp.zeros_like(dk_ref)
            dv_ref[...] = jnp.zeros_like(dv_ref)
            dcs_ref[...] = jnp.zeros_like(dcs_ref)

        def block(masked):
            lane = _iota((tq, LANES), 1)
            qv, kv, vv = q_ref[...], k_ref[...], v_ref[...]
            ov, dov, lsev = o_ref[...], do_ref[...], lse_ref[...]
            qrows = pl.ds(pl.multiple_of(qi * tq, tq), tq)
            dq_acc = jnp.zeros((tq, LANES), F32)
            dk_acc = jnp.zeros((tq, LANES), F32)
            dv_acc = jnp.zeros((tq, LANES), F32)
            dcs_acc = jnp.zeros((8, tq), F32)
            drs_acc = jnp.zeros((tq, LANES), F32)
            prod = dov * ov
            for hh in range(2):
                in_head = (lane < FOX_DH) if hh == 0 else (lane >= FOX_DH)
                zb = jnp.zeros_like(qv)
                qm = jnp.where(in_head, qv, zb)
                km = jnp.where(in_head, kv, zb)
                dom = jnp.where(in_head, dov, 0.0).astype(BF16)
                delta = jnp.sum(jnp.where(in_head, prod, 0.0), axis=1, keepdims=True)
                lse_h = jnp.sum(jnp.where(lane == hh * FOX_DH, lsev, 0.0), axis=1, keepdims=True)
                s = _dot(qm, kv, 1, 1) - ct_ref[pl.ds(2 * pr + hh, 1), :]
                p = jnp.exp(s - lse_h)
                if masked:
                    p = jnp.where(_iota((tq, tq), 0) >= _iota((tq, tq), 1), p, 0.0)
                p_b = p.astype(BF16)
                dv_acc = dv_acc + _dot(p_b, dom, 0, 0)
                dp = _dot(dom, vv, 1, 1)
                ds = p * (dp - delta)
                ds_b = ds.astype(BF16)
                dq_acc = dq_acc + _dot(ds_b, km, 1, 0)
                dk_acc = dk_acc + _dot(ds_b, qm, 0, 0)
                colsum = jnp.sum(ds, axis=0, keepdims=True)
                dcs_acc = dcs_acc + jnp.where(_iota((8, tq), 0) == hh, colsum, 0.0)
                rowsum = jnp.sum(ds, axis=1, keepdims=True)
                drs_acc = drs_acc + jnp.where(lane == 2 * pr + hh, rowsum, 0.0)
            drs_ref[qrows, :] += drs_acc
            dq_ref[qrows, :] += dq_acc
            dk_ref[...] += dk_acc
            dv_ref[...] += dv_acc
            dcs_ref[0] += dcs_acc

        @pl.when(qi > ki)
        def _():
            block(False)

        @pl.when(qi == ki)
        def _():
            block(True)

    qmap = lambda p, j, i: (jnp.maximum(i, j), p)
    return _pallas(
        body, name="fox_bwd", grid=(FOX_HEADS // 2, nq, nq),
        in_specs=[pl.BlockSpec((tq, LANES), qmap), pl.BlockSpec((tq, LANES), lambda p, j, i: (j, p)),
                  pl.BlockSpec((tq, LANES), lambda p, j, i: (j, p)),
                  pl.BlockSpec((FOX_HEADS, tq), lambda p, j, i: (0, j)),
                  pl.BlockSpec((tq, LANES), qmap), pl.BlockSpec((tq, LANES), qmap), pl.BlockSpec((tq, LANES), qmap)],
        out_specs=[pl.BlockSpec((T, LANES), lambda p, j, i: (0, p)), pl.BlockSpec((tq, LANES), lambda p, j, i: (j, p)),
                   pl.BlockSpec((tq, LANES), lambda p, j, i: (j, p)), pl.BlockSpec((1, 8, tq), lambda p, j, i: (p, 0, j)),
                   pl.BlockSpec((T, LANES), lambda p, j, i: (0, 0))],
        out_shape=[jax.ShapeDtypeStruct((T, FOX_W), F32)] * 3 + [jax.ShapeDtypeStruct((FOX_HEADS // 2, 8, T), F32),
                                                                jax.ShapeDtypeStruct((T, LANES), F32)],
        compiler_params=_params(dimension_semantics=("arbitrary",) * 3),
    )(qn, kn, vb, ct, o, do, lse)


def _fox_post(z, dq, dk, dv, dcs, drs, bias, qg, kg, dz_buf):
    T = z.shape[0]
    tm = min(ROW_BLOCK, T)
    nb = T // tm

    def body(z_ref, dq_ref, dk_ref, dv_ref, dcs_ref, drs_ref, b_ref, qg_ref, kg_ref, _buf_ref, dz_ref, dqg_ref, dkg_ref,
             db_ref, carry_ref):
        @pl.when(pl.program_id(0) == 0)
        def _():
            carry_ref[...] = jnp.zeros_like(carry_ref)
            dqg_ref[...] = jnp.zeros_like(dqg_ref)
            dkg_ref[...] = jnp.zeros_like(dkg_ref)
            db_ref[...] = jnp.zeros_like(db_ref)

        ones = _head_ones()
        for src, g_ref, d_ref, dg_ref, scale in ((0, qg_ref, dq_ref, dqg_ref, FOX_DH ** -0.5), (1, kg_ref, dk_ref, dkg_ref, 1.0)):
            xv = z_ref[:, src * FOX_W:(src + 1) * FOX_W]
            ms = _split_dot(ones, xv * xv, 1, 0, mat_first=False) * (1.0 / FOX_DH)
            rstd = lax.rsqrt(ms + EPS)
            xh = xv * rstd
            dn = d_ref[...] * scale
            dg_ref[...] += jnp.sum(dn * xh, axis=0, keepdims=True)
            dxh = dn * g_ref[...]
            mean = _split_dot(ones, dxh * xh, 1, 0, mat_first=False) * (1.0 / FOX_DH)
            dz_ref[:, src * FOX_W:(src + 1) * FOX_W] = (rstd * (dxh - xh * mean)).astype(BF16)
        dz_ref[:, 2 * FOX_W:3 * FOX_W] = dv_ref[...].astype(BF16)
        row8 = _iota((8, tm), 0)
        dct = jnp.zeros((8, tm), F32)
        for h in range(FOX_HEADS):
            src_row = dcs_ref[h // 2][h % 2:h % 2 + 1, :]
            dct = dct + jnp.where(row8 == h, src_row, 0.0)
        dct = drs_ref[...].T[0:8] - dct
        r, c = _iota((tm, tm), 0), _iota((tm, tm), 1)
        upper_b = (r >= c).astype(BF16)
        rc = _split_dot(upper_b, dct, 1, 0, mat_first=False) + carry_ref[...]
        carry_ref[...] = rc[:, 0:1]
        full = jnp.concatenate([rc, jnp.zeros((LANES - 8, tm), F32)], axis=0)
        dlogf = full.T
        xf = z_ref[:, 3 * FOX_W:FOX_COLS] + b_ref[...]
        df = dlogf * (1.0 - _sigmoid(xf))
        dz_ref[:, 3 * FOX_W:FOX_COLS] = df.astype(BF16)
        dz_ref[:, FOX_COLS:] = jnp.zeros((tm, SEG - FOX_COLS), BF16)
        db_ref[...] += jnp.sum(df, axis=0, keepdims=True)

    rev = lambda i: (nb - 1 - i, 0)
    fix2 = lambda i: (0, 0)
    return _pallas(
        body, name="fox_post", grid=(nb,),
        in_specs=[pl.BlockSpec((tm, SEG), lambda i: (nb - 1 - i, 1)), pl.BlockSpec((tm, FOX_W), rev),
                  pl.BlockSpec((tm, FOX_W), rev),
                  pl.BlockSpec((tm, FOX_W), rev), pl.BlockSpec((FOX_HEADS // 2, 8, tm), lambda i: (0, 0, nb - 1 - i)),
                  pl.BlockSpec((tm, LANES), rev),
                  pl.BlockSpec((1, LANES), fix2), pl.BlockSpec((1, FOX_W), fix2), pl.BlockSpec((1, FOX_W), fix2),
                  pl.BlockSpec(memory_space=pl.ANY)],
        out_specs=[pl.BlockSpec((tm, SEG), lambda i: (nb - 1 - i, 1)), pl.BlockSpec((1, FOX_W), fix2),
                   pl.BlockSpec((1, FOX_W), fix2), pl.BlockSpec((1, LANES), fix2)],
        out_shape=[jax.ShapeDtypeStruct(dz_buf.shape, BF16), jax.ShapeDtypeStruct((1, FOX_W), F32),
                   jax.ShapeDtypeStruct((1, FOX_W), F32), jax.ShapeDtypeStruct((1, LANES), F32)],
        scratch_shapes=[pltpu.VMEM((8, 1), F32)],
        input_output_aliases={9: 0},
        compiler_params=_params(dimension_semantics=("arbitrary",)),
    )(z, dq, dk, dv, dcs, drs, bias, qg, kg, dz_buf)


def _local_step(x, p, tgt, sm, W):
    lbl, og, fb = sm["hg_lb_logits"], sm["hg_onorm_g"], sm["fox_f_bias"]
    fbias = jnp.pad(fb, ((0, 0), (0, LANES - FOX_HEADS)))
    qg = jnp.tile(sm["fox_q_norm_g"], (1, FOX_HEADS))
    kg = jnp.tile(sm["fox_k_norm_g"], (1, FOX_HEADS))

    h = _rms_fwd(x, sm["norm_mix_g"], name="rms_mix")
    z = _matmul(h, W["w_in"], name="mm_z")
    o_raw, ya, states = _hgrn_fwd(z, lbl, og)
    qn, kn, vb, ct = _fox_prep(z, fbias, qg, kg)
    yb, lse = _fox_fwd(qn, kn, vb, ct)
    ua = _matmul(ya, W["w_branch_a"], name="mm_ua")
    ub = _matmul(yb, W["w_branch_b"], name="mm_ub")
    merged = _merge_fwd(ua, ub, z)
    x1 = _matmul(merged, W["w_out"], add=x, name="mm_x1")
    hf = _rms_fwd(x1, sm["norm_ffn_g"], name="rms_ffn")
    a = _matmul(hf, W["w_ffn_gate"], out_dtype=BF16, name="mm_ffn_a")
    b = _matmul(hf, W["w_ffn_up"], out_dtype=BF16, name="mm_ffn_b")
    act = _swiglu_fwd(a, b)
    x2 = _matmul(act, W["w_ffn_down"], add=x1, name="mm_x2")
    hp = _rms_fwd(x2, sm["norm_ple_g"], name="rms_ple")
    sp = _matmul(hp, W["w_ple_gate"], name="mm_sp")
    pp = _matmul(p, W["w_ple_proj"], name="mm_pp")
    dy, dsp, dpp, loss = _ple_loss(x2, sp, pp, tgt)

    G = {}
    G["w_ple_proj"] = _matmul(p, dpp, ta=True, out_dtype=BF16, name="mm_dw_ple_proj")
    G["w_ple_gate"] = _matmul(hp, dsp, ta=True, out_dtype=BF16, name="mm_dw_ple_gate")
    dhp = _matmul(dsp, W["w_ple_gate"], tb=True, name="mm_dhp")
    dx2, d_ple_g = _rms_bwd(x2, sm["norm_ple_g"], dhp, dy, name="rms_ple_bwd")
    dact = _matmul(dx2, W["w_ffn_down"], tb=True, name="mm_dact")
    G["w_ffn_down"] = _matmul(act, dx2, ta=True, out_dtype=BF16, name="mm_dw_ffn_down")
    da, db = _swiglu_bwd(a, b, dact)
    G["w_ffn_gate"] = _matmul(hf, da, ta=True, out_dtype=BF16, name="mm_dw_ffn_gate")
    G["w_ffn_up"] = _matmul(hf, db, ta=True, out_dtype=BF16, name="mm_dw_ffn_up")
    dhf = _matmul(da, W["w_ffn_gate"], tb=True, name="mm_dhf_a")
    dhf = _matmul(db, W["w_ffn_up"], tb=True, add=dhf, name="mm_dhf_b")
    dx1, d_ffn_g = _rms_bwd(x1, sm["norm_ffn_g"], dhf, dx2, name="rms_ffn_bwd")
    dmerged = _matmul(dx1, W["w_out"], tb=True, name="mm_dmerged")
    G["w_out"] = _matmul(merged, dx1, ta=True, out_dtype=BF16, name="mm_dw_out")
    dua, dub, dz = _merge_bwd(dmerged, ua, ub, z)
    G["w_branch_a"] = _matmul(ya, dua, ta=True, out_dtype=BF16, name="mm_dw_branch_a")
    G["w_branch_b"] = _matmul(yb, dub, ta=True, out_dtype=BF16, name="mm_dw_branch_b")
    dya = _matmul(dua, W["w_branch_a"], tb=True, name="mm_dya")
    dyb = _matmul(dub, W["w_branch_b"], tb=True, name="mm_dyb")
    dq, dk, dv, dcs, drs = _fox_bwd(qn, kn, vb, ct, yb, dyb, lse)
    dz, d_qg, d_kg, d_fb = _fox_post(z, dq, dk, dv, dcs, drs, fbias, qg, kg, dz)
    dz, d_lbl, d_og = _hgrn_bwd(z, o_raw, dya, states, lbl, og, dz)
    dh = _matmul(dz, W["w_in"], tb=True, name="mm_dh")
    G["w_in"] = _matmul(h, dz, ta=True, out_dtype=BF16, name="mm_dw_in")
    grad_x, d_mix_g = _rms_bwd(x, sm["norm_mix_g"], dh, dx1, name="rms_mix_bwd")

    gs = {"norm_mix_g": d_mix_g, "hg_lb_logits": d_lbl, "hg_onorm_g": d_og, "fox_f_bias": d_fb[:, :FOX_HEADS],
          "fox_q_norm_g": d_qg.reshape(FOX_HEADS, FOX_DH).sum(0, keepdims=True),
          "fox_k_norm_g": d_kg.reshape(FOX_HEADS, FOX_DH).sum(0, keepdims=True),
          "norm_ffn_g": d_ffn_g, "norm_ple_g": d_ple_g}
    return loss, grad_x, gs, G


def _pack_rows(parts, total):
    buf = jnp.concatenate(parts, axis=-2)
    pad = total - buf.shape[-2]
    widths = [(0, 0)] * (buf.ndim - 2) + [(0, pad), (0, 0)]
    return jnp.pad(buf, widths)


def _chunk_of_shard(n, w):
    if n == "w_in":
        return jnp.pad(w, ((0, 0), (0, IN_SHARD_PAD - IN_SHARD)))
    if n in ("w_ffn_gate", "w_ffn_up"):
        return jnp.pad(w, ((0, 0), (0, FF_SHARD_PAD - FF_SHARD)))
    if n == "w_ffn_down":
        return jnp.pad(w, ((0, FF_SHARD_PAD - FF_SHARD), (0, 0)))
    return w


def _full_of_chunks(n, g):
    _, a, b = g.shape
    if BIG_SHAPE[n][2] == 0:
        return g.reshape(N_DEV * a, b)
    if n == "w_in":
        w = g[:, :, :IN_SHARD].transpose(1, 0, 2).reshape(a, IN_COLS)
        gap = jnp.zeros((a, SEG - FOX_LOGICAL), g.dtype)
        return jnp.concatenate([w[:, :HG_COLS], w[:, HG_COLS:HG_COLS + FOX_LOGICAL], gap, w[:, HG_COLS + FOX_LOGICAL:]],
                               axis=1)
    return g.transpose(1, 0, 2).reshape(a, N_DEV * b)


def _chunks_of_full(n, g):
    if BIG_SHAPE[n][2] == 0:
        return g.reshape(N_DEV, g.shape[0] // N_DEV, g.shape[1])
    if n == "w_in":
        a = g.shape[0]
        w = jnp.concatenate([g[:, :HG_COLS], g[:, SEG:SEG + FOX_LOGICAL], g[:, 2 * SEG:]], axis=1)
        w = w.reshape(a, N_DEV, IN_SHARD).transpose(1, 0, 2)
        return jnp.pad(w, ((0, 0), (0, 0), (0, IN_SHARD_PAD - IN_SHARD)))
    return g.reshape(g.shape[0], N_DEV, g.shape[1] // N_DEV).transpose(1, 0, 2)


def _pack_small(vals, loss_row=None):
    parts = [vals[n].reshape(SMALL_ROWS[n], -1) for n in SMALL]
    parts = [jnp.pad(v, ((0, 0), (0, LANES - v.shape[1]))) for v in parts]
    if loss_row is not None:
        parts.append(loss_row)
    return _pack_rows(parts, SMALL_TOTAL)


def _unpack_small(buf, like):
    out, r0 = {}, 0
    for n in SMALL:
        rows, size = SMALL_ROWS[n], like[n].size
        blk = buf[r0:r0 + rows]
        out[n] = (blk if size == rows * LANES else blk[:, :size]).reshape(like[n].shape)
        r0 += rows
    return out


def _place():
    return lax.axis_index("x"), lax.axis_index("y"), lax.axis_index("c")


def _all_gather(blocks, *, name):
    n = len(blocks)

    def body(*refs):
        x_refs, out_refs = refs[:n], refs[n:2 * n]
        send_sems, recv_sems, local_sems = refs[2 * n:]
        x, y, c = _place()
        me, sibling = (x, y, c), (x, y, 1 - c)
        chips = [(1 - x, y), (x, 1 - y), (1 - x, 1 - y)]

        def slot(i, px, py, pc):
            return out_refs[i].at[4 * px + 2 * py + pc]

        def copy(k, i, blk, to, own=False):
            return pltpu.make_async_remote_copy(
                src_ref=x_refs[i] if own else slot(i, *blk), dst_ref=slot(i, *blk),
                send_sem=send_sems.at[k, i], recv_sem=recv_sems.at[k, i], device_id=to, device_id_type=MESH)

        mine = [pltpu.make_async_copy(x_refs[i], slot(i, *me), local_sems.at[i]) for i in range(n)]
        for cp in mine:
            cp.start()
        first = [copy(0, i, me, sibling, own=True) for i in range(n)]
        first += [copy(1 + j, i, me, (*chip, c), own=True) for j, chip in enumerate(chips) for i in range(n)]
        for cp in first:
            cp.start()
        passed = []
        for j, chip in enumerate(chips):
            for i in range(n):
                copy(1 + j, i, (*chip, c), me).wait_recv()
                fw = copy(4 + j, i, (*chip, c), sibling)
                fw.start()
                passed.append(fw)
        for i in range(n):
            copy(0, i, sibling, me).wait_recv()
        for j, chip in enumerate(chips):
            for i in range(n):
                copy(4 + j, i, (*chip, 1 - c), me).wait_recv()
        for cp in first + passed:
            cp.wait_send()
        for cp in mine:
            cp.wait()

    hbm = pl.BlockSpec(memory_space=pl.ANY)
    return _pallas(
        body, name=name, out_shape=[jax.ShapeDtypeStruct((N_DEV,) + b.shape, b.dtype) for b in blocks],
        in_specs=[hbm] * n, out_specs=[hbm] * n,
        scratch_shapes=[pltpu.SemaphoreType.DMA((7, n)), pltpu.SemaphoreType.DMA((7, n)), pltpu.SemaphoreType.DMA((n,))],
    )(*blocks)


def _sibling_exchange(gs, *, name):
    n = len(gs)

    def body(*refs):
        g_refs, out_refs = refs[:n], refs[n:2 * n]
        send_sems, recv_sems = refs[2 * n:]
        x, y, c = _place()
        cps = [pltpu.make_async_remote_copy(
            src_ref=g_refs[i].at[:, pl.ds(1 - c, 1)], dst_ref=out_refs[i], send_sem=send_sems.at[i],
            recv_sem=recv_sems.at[i], device_id=(x, y, 1 - c), device_id_type=MESH) for i in range(n)]
        for cp in cps:
            cp.start()
        for cp in cps:
            cp.wait()

    hbm = pl.BlockSpec(memory_space=pl.ANY)
    return _pallas(
        body, name=name, out_shape=[jax.ShapeDtypeStruct((N_CHIP, 1) + g.shape[2:], g.dtype) for g in gs],
        in_specs=[hbm] * n, out_specs=[hbm] * n,
        scratch_shapes=[pltpu.SemaphoreType.DMA((n,)), pltpu.SemaphoreType.DMA((n,))],
    )(*gs)


def _chip_sum(g4, got, core, *, name):
    _, _, a, b = g4.shape

    def body(c_ref, g_ref, r_ref, h_ref):
        h_ref[0] = (g_ref[0, 0].astype(F32) + r_ref[0, 0].astype(F32)).astype(BF16)

    grid_spec = pltpu.PrefetchScalarGridSpec(
        num_scalar_prefetch=1, grid=(N_CHIP,),
        in_specs=[pl.BlockSpec((1, 1, a, b), lambda j, c: (j, c[0], 0, 0)),
                  pl.BlockSpec((1, 1, a, b), lambda j, c: (j, 0, 0, 0))],
        out_specs=[pl.BlockSpec((1, a, b), lambda j, c: (j, 0, 0))])
    return _pallas(
        body, name=name, grid_spec=grid_spec, out_shape=[jax.ShapeDtypeStruct((N_CHIP, a, b), BF16)],
        compiler_params=_params(dimension_semantics=("arbitrary",)),
    )(core, g4, got)[0]


def _chip_exchange(hs, *, name):
    n = len(hs)

    def body(*refs):
        h_refs, out_refs = refs[:n], refs[n:2 * n]
        send_sems, recv_sems = refs[2 * n:]
        x, y, c = _place()
        chips = [(1 - x, y), (x, 1 - y), (1 - x, 1 - y)]
        cps = [pltpu.make_async_remote_copy(
            src_ref=h_refs[i].at[2 * px + py], dst_ref=out_refs[i].at[k], send_sem=send_sems.at[k, i],
            recv_sem=recv_sems.at[k, i], device_id=(px, py, c), device_id_type=MESH)
            for k, (px, py) in enumerate(chips) for i in range(n)]
        for cp in cps:
            cp.start()
        for cp in cps:
            cp.wait()

    hbm = pl.BlockSpec(memory_space=pl.ANY)
    return _pallas(
        body, name=name, out_shape=[jax.ShapeDtypeStruct((3,) + h.shape[1:], h.dtype) for h in hs],
        in_specs=[hbm] * n, out_specs=[hbm] * n,
        scratch_shapes=[pltpu.SemaphoreType.DMA((3, n)), pltpu.SemaphoreType.DMA((3, n))],
    )(*hs)


def _adam_math(w, g, m, v):
    m = ADAM_B1 * m + (1.0 - ADAM_B1) * g
    v = ADAM_B2 * v + (1.0 - ADAM_B2) * (g * g)
    m_hat = m / (1.0 - ADAM_B1 ** ADAM_STEP)
    v_hat = v / (1.0 - ADAM_B2 ** ADAM_STEP)
    delta = -ADAM_LR * (m_hat / (jnp.sqrt(v_hat) + ADAM_EPS) + ADAM_WD * w)
    return delta, m, v


def _adam_shard(hb, got, chip, w, m, v, *, name):
    _, r, c = w.shape
    _, a, b = hb.shape
    tr = r if r <= 512 else 256
    ta = tr if r // tr > 1 else a

    def body(j_ref, h_ref, r_ref, w_ref, m_ref, v_ref, g_ref, d_ref, nm_ref, nv_ref):
        parts = [h_ref[0], r_ref[0], r_ref[1], r_ref[2]]
        g = None
        for part in parts:
            part = part[:tr, :c].astype(F32)
            g = part if g is None else g + part
        d, nm, nv = _adam_math(w_ref[0], g, m_ref[0], v_ref[0])
        g_ref[0] = g
        d_ref[0] = d
        nm_ref[0] = nm
        nv_ref[0] = nv

    blk = pl.BlockSpec((1, tr, c), lambda i, j: (0, i, 0))
    grid_spec = pltpu.PrefetchScalarGridSpec(
        num_scalar_prefetch=1, grid=(r // tr,),
        in_specs=[pl.BlockSpec((1, ta, b), lambda i, j: (j[0], i, 0)),
                  pl.BlockSpec((3, ta, b), lambda i, j: (0, i, 0)), blk, blk, blk],
        out_specs=[blk] * 4)
    return _pallas(
        body, name=name, grid_spec=grid_spec, out_shape=[jax.ShapeDtypeStruct((1, r, c), F32)] * 4,
        compiler_params=_params(dimension_semantics=("arbitrary",)),
    )(chip, hb, got, w, m, v)


def _small_all_reduce_adam(gs, w, m, v):
    def body(g_ref, w_ref, m_ref, v_ref, sum_ref, d_ref, nm_ref, nv_ref, gather, send_sems, recv_sems):
        x, y, c = _place()
        my = 4 * x + 2 * y + c
        gather[my] = g_ref[...]
        cps = []
        for k in range(1, N_DEV):
            to = (x ^ (k >> 2), y ^ ((k >> 1) & 1), c ^ (k & 1))
            cps.append(pltpu.make_async_remote_copy(
                src_ref=g_ref, dst_ref=gather.at[my], send_sem=send_sems.at[k - 1], recv_sem=recv_sems.at[k - 1],
                device_id=to, device_id_type=MESH))
        for cp in cps:
            cp.start()
        for cp in cps:
            cp.wait()
        total = gather[0]
        for d in range(1, N_DEV):
            total = total + gather[d]
        dlt, nm, nv = _adam_math(w_ref[...], total, m_ref[...], v_ref[...])
        sum_ref[...] = total
        d_ref[...] = dlt
        nm_ref[...] = nm
        nv_ref[...] = nv

    vm = pl.BlockSpec(memory_space=pltpu.VMEM)
    return _pallas(
        body, name="small_all_reduce_adam", out_shape=[jax.ShapeDtypeStruct((SMALL_TOTAL, LANES), F32)] * 4,
        in_specs=[vm] * 4, out_specs=[vm] * 4,
        scratch_shapes=[pltpu.VMEM((N_DEV, SMALL_TOTAL, LANES), F32), pltpu.SemaphoreType.DMA((7,)),
                        pltpu.SemaphoreType.DMA((7,))],
            )(gs, w, m, v)


def kernel(x, p, norm_mix_g, w_in, hg_lb_logits, hg_onorm_g, fox_f_bias, fox_q_norm_g, fox_k_norm_g, w_branch_a, w_branch_b, w_out, norm_ffn_g, w_ffn_gate, w_ffn_up, w_ffn_down, norm_ple_g, w_ple_gate, w_ple_proj, loss_target, m_norm_mix_g, m_w_in, m_hg_lb_logits, m_hg_onorm_g, m_fox_f_bias, m_fox_q_norm_g, m_fox_k_norm_g, m_w_branch_a, m_w_branch_b, m_w_out, m_norm_ffn_g, m_w_ffn_gate, m_w_ffn_up, m_w_ffn_down, m_norm_ple_g, m_w_ple_gate, m_w_ple_proj, v_norm_mix_g, v_w_in, v_hg_lb_logits, v_hg_onorm_g, v_fox_f_bias, v_fox_q_norm_g, v_fox_k_norm_g, v_w_branch_a, v_w_branch_b, v_w_out, v_norm_ffn_g, v_w_ffn_gate, v_w_ffn_up, v_w_ffn_down, v_norm_ple_g, v_w_ple_gate, v_w_ple_proj):
    args = dict(locals())
    wts = {n: args[n] for n in BIG + SMALL}
    mom = {n: args["m_" + n] for n in BIG + SMALL}
    var = {n: args["v_" + n] for n in BIG + SMALL}
    sm = {n: wts[n] for n in SMALL}

    chunks = [_chunk_of_shard(n, wts[n][0].astype(BF16)) for n in BIG]
    gathered = _all_gather(chunks, name="weights_all_gather")
    full = {n: _full_of_chunks(n, g) for n, g in zip(BIG, gathered)}

    loss_blk, grad_x, gs, G = _local_step(x[0], p[0, 0], loss_target[0], sm, full)

    xi, yi, ci = _place()
    core = jnp.reshape(ci, (1,)).astype(jnp.int32)
    chip = jnp.reshape(2 * xi + yi, (1,)).astype(jnp.int32)
    g4 = []
    for n in BIG:
        gc = _chunks_of_full(n, G[n])
        g4.append(gc.reshape((N_CHIP, 2) + gc.shape[1:]))
    got_sib = _sibling_exchange(g4, name="grads_to_sibling")
    hb = [_chip_sum(g, r, core, name="chip_sum_" + n) for n, g, r in zip(BIG, g4, got_sib)]
    got_chips = _chip_exchange(hb, name="grads_to_chips")
    g_big, d_big, nm_big, nv_big = {}, {}, {}, {}
    for n, h, r in zip(BIG, hb, got_chips):
        g_big[n], d_big[n], nm_big[n], nv_big[n] = _adam_shard(h, r, chip, wts[n], mom[n], var[n], name="adam_" + n)

    s_sum, s_d, s_nm, s_nv = _small_all_reduce_adam(
        _pack_small(gs, loss_blk[0:1]), _pack_small(sm), _pack_small({n: mom[n] for n in SMALL}),
        _pack_small({n: var[n] for n in SMALL}))
    loss = s_sum[LOSS_ROW, 0]
    g_small, d_small, nm_small, nv_small = (_unpack_small(t, sm) for t in (s_sum, s_d, s_nm, s_nv))

    order = ["norm_mix_g", "w_in", "hg_lb_logits", "hg_onorm_g", "fox_f_bias", "fox_q_norm_g", "fox_k_norm_g",
             "w_branch_a", "w_branch_b", "w_out", "norm_ffn_g", "w_ffn_gate", "w_ffn_up", "w_ffn_down", "norm_ple_g",
             "w_ple_gate", "w_ple_proj"]
    outs = [loss, grad_x[None]]
    for big, small in ((g_big, g_small), (d_big, d_small), (nm_big, nm_small), (nv_big, nv_small)):
        outs += [big[n] if n in big else small[n] for n in order]
    return tuple(outs)
```

```python
import functools

import jax
import jax.numpy as jnp
from jax import lax
from jax.experimental import pallas as pl
from jax.experimental.pallas import tpu as pltpu

F32 = jnp.float32
BF16 = jnp.bfloat16

D_MODEL = 1024
PLE_DIM = 256
HG_HEADS = 4
HG_DK = 128
HG_CHUNK = 64
HG_SUB = 16
HG_W = HG_HEADS * HG_DK
FOX_HEADS = 8
FOX_DH = 64
FOX_W = FOX_HEADS * FOX_DH
D_FF = 2816
EPS = 1e-6
N_DEV = 8
N_CHIP = 4
LANES = 128
FOX_COLS = 3 * FOX_W + LANES
HG_COLS = 4 * HG_W
GATE_COLS = 2 * D_MODEL
IN_COLS = HG_COLS + 3 * FOX_W + FOX_HEADS + GATE_COLS
FOX_LOGICAL = 3 * FOX_W + FOX_HEADS
SEG = 2048
IN_PAD = 3 * SEG
IN_SHARD = IN_COLS // N_DEV
IN_SHARD_PAD = 768
FF_SHARD = D_FF // N_DEV
FF_SHARD_PAD = 384
FF_PAD = N_DEV * FF_SHARD_PAD
EXP_CLAMP = 80.0

ADAM_LR = 0.001
ADAM_B1 = 0.9
ADAM_B2 = 0.999
ADAM_EPS = 1e-08
ADAM_WD = 0.01
ADAM_STEP = 10

MESH = pl.DeviceIdType.MESH
VMEM_LIMIT = 56 * 1024 * 1024
ROW_BLOCK = 512

BIG = ["w_in", "w_branch_a", "w_branch_b", "w_out", "w_ffn_gate", "w_ffn_up", "w_ffn_down",
       "w_ple_gate", "w_ple_proj"]
BIG_SHAPE = {
    "w_in": (D_MODEL, IN_COLS, 1), "w_branch_a": (HG_W, D_MODEL, 1), "w_branch_b": (FOX_W, D_MODEL, 1),
    "w_out": (D_MODEL, D_MODEL, 0), "w_ffn_gate": (D_MODEL, D_FF, 1), "w_ffn_up": (D_MODEL, D_FF, 1),
    "w_ffn_down": (D_FF, D_MODEL, 0), "w_ple_gate": (D_MODEL, D_MODEL, 0), "w_ple_proj": (PLE_DIM, D_MODEL, 1),
}

SMALL = ["norm_mix_g", "hg_lb_logits", "hg_onorm_g", "fox_f_bias", "fox_q_norm_g", "fox_k_norm_g",
         "norm_ffn_g", "norm_ple_g"]
SMALL_ROWS = {"norm_mix_g": 8, "hg_lb_logits": 8, "hg_onorm_g": 1, "fox_f_bias": 1, "fox_q_norm_g": 1,
              "fox_k_norm_g": 1, "norm_ffn_g": 8, "norm_ple_g": 8}
SMALL_TOTAL = 40
LOSS_ROW = 36


def _pallas(body, **kw):
    return pl.pallas_call(body, **kw)


def _params(**kw):
    return pltpu.CompilerParams(vmem_limit_bytes=VMEM_LIMIT, **kw)


def _pick(n, target):
    if n <= target:
        return n
    best = None
    for t in range(LANES, target + 1, LANES):
        if n % t == 0:
            best = t
    assert best is not None, (n, target)
    return best


def _dot(a, b, ca, cb):
    return lax.dot_general(a, b, (((ca,), (cb,)), ((), ())), preferred_element_type=F32)


def _split_dot(mat, x, ca, cb, terms=2, mat_first=True):
    acc = None
    rem = x
    for _ in range(terms):
        part = rem.astype(BF16)
        rem = rem - part.astype(F32)
        p = _dot(mat, part, ca, cb) if mat_first else _dot(part, mat, ca, cb)
        acc = p if acc is None else acc + p
    return acc


def _sigmoid(x):
    return 1.0 / (1.0 + jnp.exp(-x))


def _iota(shape, dim):
    return lax.broadcasted_iota(jnp.int32, shape, dim)


def _matmul(a, b, *, name, ta=False, tb=False, out_dtype=F32, add=None):
    (K, M) = a.shape if ta else a.shape[::-1]
    (N, Kb) = b.shape if tb else b.shape[::-1]
    assert K == Kb, (a.shape, b.shape, ta, tb)
    if ta:
        tm, tn, tk = _pick(M, 3 * ROW_BLOCK), _pick(N, 2048), _pick(K, ROW_BLOCK)
    else:
        tm, tn, tk = _pick(M, 2 * ROW_BLOCK), _pick(N, 2048), _pick(K, 2048)
    nk = K // tk
    use_scratch = nk > 1 and out_dtype != F32

    def body(*refs):
        refs = list(refs)
        a_ref, b_ref = refs[:2]
        add_ref = refs[2] if add is not None else None
        o_ref = refs[3] if add is not None else refs[2]
        k = pl.program_id(2)
        p = _dot(a_ref[...].astype(BF16), b_ref[...].astype(BF16), 0 if ta else 1, 1 if tb else 0)

        def with_add(r):
            return r if add is None else r + add_ref[...].astype(F32)

        if nk == 1:
            o_ref[...] = with_add(p).astype(out_dtype)
        elif not use_scratch:
            @pl.when(k == 0)
            def _():
                o_ref[...] = with_add(p)

            @pl.when(k > 0)
            def _():
                o_ref[...] += p
        else:
            acc_ref = refs[-1]

            @pl.when(k == 0)
            def _():
                acc_ref[...] = p

            @pl.when(k > 0)
            def _():
                acc_ref[...] += p

            @pl.when(k == nk - 1)
            def _():
                o_ref[...] = with_add(acc_ref[...]).astype(out_dtype)

    a_spec = pl.BlockSpec((tk, tm), lambda i, j, k: (k, i)) if ta else pl.BlockSpec((tm, tk), lambda i, j, k: (i, k))
    b_spec = pl.BlockSpec((tn, tk), lambda i, j, k: (j, k)) if tb else pl.BlockSpec((tk, tn), lambda i, j, k: (k, j))
    o_spec = pl.BlockSpec((tm, tn), lambda i, j, k: (i, j))
    in_specs = [a_spec, b_spec] + ([o_spec] if add is not None else [])
    args = (a, b) + ((add,) if add is not None else ())
    return _pallas(
        body, name=name, grid=(M // tm, N // tn, nk), in_specs=in_specs, out_specs=o_spec,
        out_shape=jax.ShapeDtypeStruct((M, N), out_dtype),
        scratch_shapes=[pltpu.VMEM((tm, tn), F32)] if use_scratch else [],
        compiler_params=_params(dimension_semantics=("arbitrary",) * 3),
    )(*args)


def _row_map(nb, reverse, seg):
    if reverse:
        return lambda i: (nb - 1 - i, seg)
    return lambda i: (i, seg)


def _row_call(body, *, name, T, ins, outs, acc_outs=(), tm=ROW_BLOCK, reverse=False):
    tm = min(tm, T)
    nb = T // tm
    in_specs, args = [], []
    for arr, how in ins:
        args.append(arr)
        if how is True:
            in_specs.append(pl.BlockSpec((tm, arr.shape[1]), _row_map(nb, reverse, 0)))
        elif how is False:
            in_specs.append(pl.BlockSpec(arr.shape, lambda i, _n=arr.ndim: (0,) * _n))
        else:
            in_specs.append(pl.BlockSpec((tm, SEG), _row_map(nb, reverse, how[0])))
    out_specs, out_shape = [], []
    for o in outs:
        c, dt = o[0], o[1]
        total, seg = o[2] if len(o) > 2 else (c, 0)
        out_specs.append(pl.BlockSpec((tm, c), _row_map(nb, reverse, seg)))
        out_shape.append(jax.ShapeDtypeStruct((T, total), dt))
    for shp, dt in acc_outs:
        out_specs.append(pl.BlockSpec(shp, lambda i, _n=len(shp): (0,) * _n))
        out_shape.append(jax.ShapeDtypeStruct(shp, dt))
    return _pallas(body, name=name, grid=(nb,), in_specs=in_specs, out_specs=out_specs, out_shape=out_shape,
                   compiler_params=_params(dimension_semantics=("arbitrary",)))(*args)


def _rms_fwd(x, g, *, name):
    T = x.shape[0]

    def body(x_ref, g_ref, h_ref):
        xv = x_ref[...]
        rstd = lax.rsqrt(jnp.mean(xv * xv, axis=-1, keepdims=True) + EPS)
        h_ref[...] = (xv * rstd * g_ref[...]).astype(BF16)

    return _row_call(body, name=name, T=T, ins=[(x, True), (g, False)], outs=[(D_MODEL, BF16)])[0]


def _rms_bwd(x, g, dh, dres, *, name):
    T = x.shape[0]

    def body(x_ref, g_ref, dh_ref, dres_ref, dx_ref, dg_ref):
        xv = x_ref[...]
        rstd = lax.rsqrt(jnp.mean(xv * xv, axis=-1, keepdims=True) + EPS)
        xh = xv * rstd
        dhv = dh_ref[...]
        part = jnp.sum(dhv * xh, axis=0, keepdims=True)

        @pl.when(pl.program_id(0) == 0)
        def _():
            dg_ref[...] = part

        @pl.when(pl.program_id(0) > 0)
        def _():
            dg_ref[...] += part

        dxh = dhv * g_ref[...]
        dx_ref[...] = rstd * (dxh - xh * jnp.mean(dxh * xh, axis=-1, keepdims=True)) + dres_ref[...]

    return _row_call(body, name=name, T=T, ins=[(x, True), (g, False), (dh, True), (dres, True)],
                     outs=[(D_MODEL, F32)], acc_outs=[((1, D_MODEL), F32)])


def _merge_fwd(ua, ub, zg):
    def body(ua_ref, ub_ref, zg_ref, m_ref):
        ga = _sigmoid(zg_ref[:, :D_MODEL])
        gb = _sigmoid(zg_ref[:, D_MODEL:])
        m_ref[...] = (ga * ua_ref[...] + gb * ub_ref[...]).astype(BF16)

    return _row_call(body, name="merge_fwd", T=ua.shape[0], ins=[(ua, True), (ub, True), (zg, (2,))],
                     outs=[(D_MODEL, BF16)])[0]


def _merge_bwd(dm, ua, ub, zg):
    def body(dm_ref, ua_ref, ub_ref, zg_ref, dua_ref, dub_ref, dzg_ref):
        ga = _sigmoid(zg_ref[:, :D_MODEL])
        gb = _sigmoid(zg_ref[:, D_MODEL:])
        dmv = dm_ref[...]
        dua_ref[...] = (dmv * ga).astype(BF16)
        dub_ref[...] = (dmv * gb).astype(BF16)
        dzg_ref[:, :D_MODEL] = (dmv * ua_ref[...] * ga * (1.0 - ga)).astype(BF16)
        dzg_ref[:, D_MODEL:] = (dmv * ub_ref[...] * gb * (1.0 - gb)).astype(BF16)

    return _row_call(body, name="merge_bwd", T=dm.shape[0], ins=[(dm, True), (ua, True), (ub, True), (zg, (2,))],
                     outs=[(D_MODEL, BF16), (D_MODEL, BF16), (SEG, BF16, (IN_PAD, 2))])


def _swiglu_fwd(a, b):
    def body(a_ref, b_ref, o_ref):
        av = a_ref[...].astype(F32)
        o_ref[...] = (av * _sigmoid(av) * b_ref[...].astype(F32)).astype(BF16)

    return _row_call(body, name="swiglu_fwd", T=a.shape[0], ins=[(a, True), (b, True)], outs=[(a.shape[1], BF16)])[0]


def _swiglu_bwd(a, b, dact):
    def body(a_ref, b_ref, d_ref, da_ref, db_ref):
        av = a_ref[...].astype(F32)
        bv = b_ref[...].astype(F32)
        dv = d_ref[...]
        sg = _sigmoid(av)
        da_ref[...] = (dv * bv * sg * (1.0 + av * (1.0 - sg))).astype(BF16)
        db_ref[...] = (dv * av * sg).astype(BF16)

    return _row_call(body, name="swiglu_bwd", T=a.shape[0], ins=[(a, True), (b, True), (dact, True)],
                     outs=[(a.shape[1], BF16), (a.shape[1], BF16)])


def _ple_loss(x2, sp, pp, tgt):
    def body(x_ref, sp_ref, pp_ref, t_ref, dy_ref, dsp_ref, dpp_ref, loss_ref):
        gp = _sigmoid(sp_ref[...])
        ppv = pp_ref[...]
        err = x_ref[...] + gp * ppv - t_ref[...]
        part = 0.5 * jnp.sum(jnp.mean(err * err, axis=-1, keepdims=True), axis=0, keepdims=True)
        part = jnp.broadcast_to(part, loss_ref.shape)

        @pl.when(pl.program_id(0) == 0)
        def _():
            loss_ref[...] = part

        @pl.when(pl.program_id(0) > 0)
        def _():
            loss_ref[...] += part

        dy = err * (1.0 / D_MODEL)
        dy_ref[...] = dy
        dsp_ref[...] = (dy * ppv * gp * (1.0 - gp)).astype(BF16)
        dpp_ref[...] = (dy * gp).astype(BF16)

    return _row_call(body, name="ple_loss", T=x2.shape[0], ins=[(x2, True), (sp, True), (pp, True), (tgt, True)],
                     outs=[(D_MODEL, F32), (D_MODEL, BF16), (D_MODEL, BF16)], acc_outs=[((8, LANES), F32)])


def _hg_consts():
    C = HG_CHUNK
    r, c = _iota((C, C), 0), _iota((C, C), 1)
    tri = (c <= r)
    same = (r // HG_SUB) == (c // HG_SUB)
    return tri, (tri & same)


def _hg_chunk_fwd(q, f, lb, tri_b, sub_b):
    sgq = _sigmoid(q)
    qt = q * sgq
    sg = _sigmoid(f)
    fg = lb + (1.0 - lb) * sg
    kf = (1.0 - lb) * (1.0 - sg)
    logf = jnp.log(fg)
    b = _split_dot(tri_b, logf, 1, 0)
    w = _split_dot(sub_b, logf, 1, 0)
    return sgq, qt, sg, fg, kf, b, w


def _hg_scores(qs_b, kf, b, row):
    C, S = HG_CHUNK, HG_SUB
    parts, ks = [], []
    for blk in range(C // S):
        ref = jnp.zeros_like(b[0:1]) if blk == 0 else b[blk * S - 1:blk * S]
        e = jnp.exp(jnp.minimum(ref - b, EXP_CLAMP))
        e = jnp.where(row < (blk + 1) * S, e, 0.0)
        k_b = (kf * e).astype(BF16)
        ks.append((e, k_b))
        parts.append(_dot(qs_b[blk * S:(blk + 1) * S], k_b, 1, 1))
    return jnp.concatenate(parts, axis=0), ks


def _hgrn_fwd(z, lb_logits, gain):
    T = z.shape[0]
    RB = min(ROW_BLOCK, T)
    nb, cpb = T // RB, RB // HG_CHUNK
    C, DK = HG_CHUNK, HG_DK

    def body(z_ref, lg_ref, g_ref, o_ref, y_ref, st_ref, s_ref):
        @pl.when(pl.program_id(0) == 0)
        def _():
            s_ref[...] = jnp.zeros_like(s_ref)

        lg = lg_ref[...]
        lb_all = 1.0 / (1.0 + jnp.exp(lg[1:2] - lg[0:1]))
        gain_v = g_ref[...]
        tri, sub = _hg_consts()
        tri_b, sub_b = tri.astype(BF16), sub.astype(BF16)
        row = _iota((C, DK), 0)

        def chunk(ci, carry):
            r0 = pl.multiple_of(ci * C, C)
            rows = pl.ds(r0, C)
            for h in range(HG_HEADS):
                cs = slice(h * DK, (h + 1) * DK)
                q = z_ref[rows, cs]
                f = z_ref[rows, HG_W + h * DK:HG_W + (h + 1) * DK]
                v = z_ref[rows, 2 * HG_W + h * DK:2 * HG_W + (h + 1) * DK]
                g = z_ref[rows, 3 * HG_W + h * DK:3 * HG_W + (h + 1) * DK]
                lb = lb_all[:, cs]
                _, qt, _, _, kf, b, w = _hg_chunk_fwd(q, f, lb, tri_b, sub_b)
                st = s_ref[h]
                st_ref[pl.ds(pl.multiple_of((ci * HG_HEADS + h) * DK, DK), DK), :] = st
                v_b = v.astype(BF16)
                qs_b = (qt * jnp.exp(w)).astype(BF16)
                a, _ = _hg_scores(qs_b, kf, b, row)
                a = jnp.where(tri, a, 0.0)
                qd = qt * jnp.exp(b)
                o = _dot(qd.astype(BF16), st.astype(BF16), 1, 1) + _dot(a.astype(BF16), v_b, 1, 0)
                bl = b[C - 1:C]
                kd = kf * jnp.exp(bl - b)
                s_ref[h] = st * jnp.exp(bl) + _dot(v_b, kd.astype(BF16), 0, 0)
                o_ref[rows, cs] = o
                rstd = lax.rsqrt(jnp.mean(o * o, axis=-1, keepdims=True) + EPS)
                y_ref[rows, cs] = (o * rstd * gain_v * (g * _sigmoid(g))).astype(BF16)
            return carry

        lax.fori_loop(0, cpb, chunk, 0)

    return _pallas(
        body, name="hgrn_fwd", grid=(nb,),
        in_specs=[pl.BlockSpec((RB, HG_COLS), lambda i: (i, 0)), pl.BlockSpec((2, HG_W), lambda i: (0, 0)),
                  pl.BlockSpec((1, DK), lambda i: (0, 0))],
        out_specs=[pl.BlockSpec((RB, HG_W), lambda i: (i, 0)), pl.BlockSpec((RB, HG_W), lambda i: (i, 0)),
                   pl.BlockSpec((cpb * HG_HEADS * DK, DK), lambda i: (i, 0))],
        out_shape=[jax.ShapeDtypeStruct((T, HG_W), F32), jax.ShapeDtypeStruct((T, HG_W), BF16),
                   jax.ShapeDtypeStruct((T // C * HG_HEADS * DK, DK), F32)],
        scratch_shapes=[pltpu.VMEM((HG_HEADS, DK, DK), F32)],
        compiler_params=_params(dimension_semantics=("arbitrary",)),
    )(z, lb_logits, gain)


def _hgrn_bwd(z, o_raw, dy, states, lb_logits, gain, dz_buf):
    T = z.shape[0]
    RB = min(ROW_BLOCK, T)
    nb, cpb = T // RB, RB // HG_CHUNK
    C, DK, S = HG_CHUNK, HG_DK, HG_SUB

    def body(z_ref, o_ref, dy_ref, st_ref, lg_ref, g_ref, _buf_ref, dz_ref, dlg_ref, dg_ref, ds_ref, dlb_ref):
        step = pl.program_id(0)

        @pl.when(step == 0)
        def _():
            ds_ref[...] = jnp.zeros_like(ds_ref)
            dlb_ref[...] = jnp.zeros_like(dlb_ref)
            dg_ref[...] = jnp.zeros_like(dg_ref)

        lg = lg_ref[...]
        lb_all = 1.0 / (1.0 + jnp.exp(lg[1:2] - lg[0:1]))
        gain_v = g_ref[...]
        tri, sub = _hg_consts()
        tri_b, sub_b = tri.astype(BF16), sub.astype(BF16)
        row = _iota((C, DK), 0)

        def chunk(cj, carry):
            ci = cpb - 1 - cj
            r0 = pl.multiple_of(ci * C, C)
            rows = pl.ds(r0, C)
            for h in range(HG_HEADS):
                cs = slice(h * DK, (h + 1) * DK)
                q = z_ref[rows, cs]
                f = z_ref[rows, HG_W + h * DK:HG_W + (h + 1) * DK]
                v = z_ref[rows, 2 * HG_W + h * DK:2 * HG_W + (h + 1) * DK]
                g = z_ref[rows, 3 * HG_W + h * DK:3 * HG_W + (h + 1) * DK]
                lb = lb_all[:, cs]
                sgq, qt, sg, fg, kf, b, w = _hg_chunk_fwd(q, f, lb, tri_b, sub_b)
                st = st_ref[pl.ds(pl.multiple_of((ci * HG_HEADS + h) * DK, DK), DK), :]
                dst = ds_ref[h]
                o = o_ref[rows, cs]
                dyv = dy_ref[rows, cs]
                rstd = lax.rsqrt(jnp.mean(o * o, axis=-1, keepdims=True) + EPS)
                n = o * rstd
                sgg = _sigmoid(g)
                t1 = dyv * (g * sgg)
                dg_ref[...] += jnp.sum(t1 * n, axis=0, keepdims=True)
                dn = t1 * gain_v
                do = rstd * (dn - n * jnp.mean(dn * n, axis=-1, keepdims=True))
                dgate = dyv * n * gain_v * sgg * (1.0 + g * (1.0 - sgg))
                do_b = do.astype(BF16)
                v_b = v.astype(BF16)
                ew = jnp.exp(w)
                qs_b = (qt * ew).astype(BF16)
                a, ks = _hg_scores(qs_b, kf, b, row)
                a = jnp.where(tri, a, 0.0)
                eb = jnp.exp(b)
                qd = qt * eb
                bl = b[C - 1:C]
                ebl = jnp.exp(bl)
                ekd = jnp.exp(bl - b)
                kd = kf * ekd
                kd_b = kd.astype(BF16)
                qd_b = qd.astype(BF16)
                dst_b = dst.astype(BF16)
                dqd = _dot(do_b, st.astype(BF16), 1, 0)
                da = jnp.where(tri, _dot(do_b, v_b, 1, 1), 0.0)
                dv = _dot(a.astype(BF16), do_b, 0, 0) + _dot(kd_b, dst_b, 1, 1)
                dkd = _dot(v_b, dst_b, 1, 0)
                ds_ref[h] = dst * ebl + _dot(do_b, qd_b, 0, 0)
                dkd_kd = dkd * kd_b.astype(F32)
                dbl = ebl * jnp.sum(dst * st, axis=0, keepdims=True) + jnp.sum(dkd_kd, axis=0, keepdims=True)
                da_b = da.astype(BF16)
                dqs_parts = []
                dk_in = jnp.zeros((C, DK), F32)
                db_k = jnp.zeros((C, DK), F32)
                for blk in range(C // S):
                    e, k_b = ks[blk]
                    da_blk = da_b[blk * S:(blk + 1) * S]
                    dqs_parts.append(_dot(da_blk, k_b, 1, 0))
                    dks = _dot(da_blk, qs_b[blk * S:(blk + 1) * S], 0, 0)
                    dk_in = dk_in + dks * e
                    db_k = db_k + dks * k_b.astype(F32)
                dqs = jnp.concatenate(dqs_parts, axis=0)
                dqt_in = dqs * ew
                db = qs_b.astype(F32) * dqs - db_k + dqd * qd_b.astype(F32) - dkd_kd
                db = db + jnp.where(row == C - 1, dbl, 0.0)
                dlogf = _split_dot(tri_b, db, 0, 0)
                dqt = dqt_in + dqd * eb
                dkf = dk_in + dkd * ekd
                dfg = dlogf / fg - dkf
                dlb_ref[:, cs] += jnp.sum(dfg * (1.0 - sg), axis=0, keepdims=True)
                dz_ref[rows, cs] = (dqt * sgq * (1.0 + q * (1.0 - sgq))).astype(BF16)
                dz_ref[rows, HG_W + h * DK:HG_W + (h + 1) * DK] = (dfg * (1.0 - lb) * sg * (1.0 - sg)).astype(BF16)
                dz_ref[rows, 2 * HG_W + h * DK:2 * HG_W + (h + 1) * DK] = dv.astype(BF16)
                dz_ref[rows, 3 * HG_W + h * DK:3 * HG_W + (h + 1) * DK] = dgate.astype(BF16)
            return carry

        lax.fori_loop(0, cpb, chunk, 0)

        @pl.when(step == nb - 1)
        def _():
            d0 = dlb_ref[...] * lb_all * (1.0 - lb_all)
            dlg_ref[0:1, :] = d0
            dlg_ref[1:2, :] = -d0

    rev = lambda i: (nb - 1 - i, 0)
    fix = lambda i: (0, 0)
    return _pallas(
        body, name="hgrn_bwd", grid=(nb,),
        in_specs=[pl.BlockSpec((RB, HG_COLS), rev), pl.BlockSpec((RB, HG_W), rev), pl.BlockSpec((RB, HG_W), rev),
                  pl.BlockSpec((cpb * HG_HEADS * DK, DK), rev), pl.BlockSpec((2, HG_W), fix),
                  pl.BlockSpec((1, DK), fix), pl.BlockSpec(memory_space=pl.ANY)],
        out_specs=[pl.BlockSpec((RB, HG_COLS), rev), pl.BlockSpec((2, HG_W), fix), pl.BlockSpec((1, DK), fix)],
        out_shape=[jax.ShapeDtypeStruct(dz_buf.shape, BF16), jax.ShapeDtypeStruct((2, HG_W), F32),
                   jax.ShapeDtypeStruct((1, DK), F32)],
        scratch_shapes=[pltpu.VMEM((HG_HEADS, DK, DK), F32), pltpu.VMEM((1, HG_W), F32)],
        input_output_aliases={6: 0},
        compiler_params=_params(dimension_semantics=("arbitrary",)),
    )(z, o_raw, dy, states, lb_logits, gain, dz_buf)


def _head_ones():
    r, c = _iota((FOX_W, FOX_W), 0), _iota((FOX_W, FOX_W), 1)
    return ((r // FOX_DH) == (c // FOX_DH)).astype(BF16)


def _log_sigmoid(x):
    return jnp.minimum(x, 0.0) - jnp.log(1.0 + jnp.exp(-jnp.abs(x)))


def _fox_prep(z, bias, qg, kg):
    T = z.shape[0]
    tm = min(ROW_BLOCK, T)
    nb = T // tm

    def body(z_ref, b_ref, qg_ref, kg_ref, q_ref, k_ref, v_ref, ct_ref, carry_ref):
        @pl.when(pl.program_id(0) == 0)
        def _():
            carry_ref[...] = jnp.zeros_like(carry_ref)

        ones = _head_ones()
        for src, g_ref, dst, scale in ((0, qg_ref, q_ref, FOX_DH ** -0.5), (1, kg_ref, k_ref, 1.0)):
            xv = z_ref[:, src * FOX_W:(src + 1) * FOX_W]
            ms = _split_dot(ones, xv * xv, 1, 0, mat_first=False) * (1.0 / FOX_DH)
            dst[...] = (xv * lax.rsqrt(ms + EPS) * (g_ref[...] * scale)).astype(BF16)
        v_ref[...] = z_ref[:, 2 * FOX_W:3 * FOX_W].astype(BF16)
        logf = _log_sigmoid(z_ref[:, 3 * FOX_W:FOX_COLS] + b_ref[...])
        r, c = _iota((tm, tm), 0), _iota((tm, tm), 1)
        tri_b = (c <= r).astype(BF16)
        cum = _split_dot(tri_b, logf, 1, 0, terms=3) + carry_ref[...]
        carry_ref[...] = cum[tm - 1:tm]
        ct_ref[...] = cum.T[0:FOX_HEADS]

    return _pallas(
        body, name="fox_prep", grid=(nb,),
        in_specs=[pl.BlockSpec((tm, SEG), lambda i: (i, 1)), pl.BlockSpec((1, LANES), lambda i: (0, 0)),
                  pl.BlockSpec((1, FOX_W), lambda i: (0, 0)), pl.BlockSpec((1, FOX_W), lambda i: (0, 0))],
        out_specs=[pl.BlockSpec((tm, FOX_W), lambda i: (i, 0))] * 3 + [pl.BlockSpec((FOX_HEADS, tm), lambda i: (0, i))],
        out_shape=[jax.ShapeDtypeStruct((T, FOX_W), BF16)] * 3 + [jax.ShapeDtypeStruct((FOX_HEADS, T), F32)],
        scratch_shapes=[pltpu.VMEM((1, LANES), F32)],
        compiler_params=_params(dimension_semantics=("arbitrary",)),
    )(z, bias, qg, kg)


def _fox_fwd(qn, kn, vb, ct, blocks):
    T = qn.shape[0]
    tq = min(ROW_BLOCK, T)
    nq = T // tq
    NEG = -1e30
    n = len(blocks)
    n_pairs = FOX_HEADS // 2

    def body(*refs):
        q_ref, k_ref, v_ref, ct_ref = refs[:4]
        o_ref, lse_ref = refs[4 + n:6 + n]
        m_sc, l_sc, acc_sc = refs[6 + 2 * n:9 + 2 * n]
        pr, qi, ki = pl.program_id(0), pl.program_id(1), pl.program_id(2)
        g_start, g_forward, g_finish = _gather_steps(refs[4:4 + n], refs[6 + n:6 + 2 * n], *refs[9 + 2 * n:])
        row_start = jnp.logical_and(qi == 0, ki == 0)

        @pl.when(jnp.logical_and(pr == 0, row_start))
        def _():
            g_start()

        @pl.when(jnp.logical_and(pr == n_pairs // 2, row_start))
        def _():
            g_forward()

        @pl.when(ki == 0)
        def _():
            m_sc[...] = jnp.full_like(m_sc, NEG)
            l_sc[...] = jnp.zeros_like(l_sc)
            acc_sc[...] = jnp.zeros_like(acc_sc)

        lane = _iota((tq, LANES), 1)

        def block(masked):
            qv, kv, vv = q_ref[...], k_ref[...], v_ref[...]
            for hh in range(2):
                in_head = (lane < FOX_DH) if hh == 0 else (lane >= FOX_DH)
                qm = jnp.where(in_head, qv, jnp.zeros_like(qv))
                s = _dot(qm, kv, 1, 1) - ct_ref[pl.ds(2 * pr + hh, 1), :]
                if masked:
                    s = jnp.where(_iota((tq, tq), 0) >= _iota((tq, tq), 1), s, NEG)
                m_old = m_sc[hh]
                m_new = jnp.maximum(m_old, jnp.max(s, axis=-1, keepdims=True))
                alpha = jnp.exp(m_old - m_new)
                p = jnp.exp(s - m_new)
                l_sc[hh] = alpha * l_sc[hh] + jnp.sum(p, axis=-1, keepdims=True)
                acc_sc[hh] = alpha * acc_sc[hh] + _dot(p.astype(BF16), vv, 1, 0)
                m_sc[hh] = m_new

        @pl.when(ki < qi)
        def _():
            block(False)

        @pl.when(ki == qi)
        def _():
            block(True)
            inv0, inv1 = 1.0 / l_sc[0], 1.0 / l_sc[1]
            o_ref[...] = jnp.where(lane < FOX_DH, acc_sc[0] * inv0, acc_sc[1] * inv1)
            lse_ref[...] = jnp.where(lane < FOX_DH, m_sc[0] + jnp.log(l_sc[0]), m_sc[1] + jnp.log(l_sc[1]))

        @pl.when(jnp.logical_and(pr == n_pairs - 1, jnp.logical_and(qi == nq - 1, ki == nq - 1)))
        def _():
            g_finish()

    kmap = lambda p, i, j: (jnp.minimum(i, j), p)
    hbm = pl.BlockSpec(memory_space=pl.ANY)
    res = _pallas(
        body, name="fox_fwd", grid=(n_pairs, nq, nq),
        in_specs=[pl.BlockSpec((tq, LANES), lambda p, i, j: (i, p)), pl.BlockSpec((tq, LANES), kmap),
                  pl.BlockSpec((tq, LANES), kmap), pl.BlockSpec((FOX_HEADS, tq), lambda p, i, j: (0, jnp.minimum(i, j)))]
        + [hbm] * n,
        out_specs=[pl.BlockSpec((tq, LANES), lambda p, i, j: (i, p))] * 2 + [hbm] * n,
        out_shape=[jax.ShapeDtypeStruct((T, FOX_W), F32)] * 2
        + [jax.ShapeDtypeStruct((N_DEV,) + b.shape, b.dtype) for b in blocks],
        scratch_shapes=[pltpu.VMEM((2, tq, 1), F32), pltpu.VMEM((2, tq, 1), F32), pltpu.VMEM((2, tq, LANES), F32)]
        + _gather_scratch(n),
        compiler_params=_params(dimension_semantics=("arbitrary",) * 3),
    )(qn, kn, vb, ct, *blocks)
    return res[0], res[1], res[2:]


def _fox_bwd(qn, kn, vb, ct, o, do, lse, hs):
    T = qn.shape[0]
    tq = min(ROW_BLOCK, T)
    nq = T // tq
    n = len(hs)
    n_pairs = FOX_HEADS // 2

    def body(*refs):
        q_ref, k_ref, v_ref, ct_ref, o_ref, do_ref, lse_ref = refs[:7]
        dq_ref, dk_ref, dv_ref, dcs_ref, drs_ref = refs[7 + n:12 + n]
        pr, ki, qi = pl.program_id(0), pl.program_id(1), pl.program_id(2)
        x_start, x_finish = _chip_exchange_steps(refs[7:7 + n], refs[12 + n:12 + 2 * n], *refs[12 + 2 * n:])

        @pl.when(jnp.logical_and(pr == 0, jnp.logical_and(ki == 0, qi == 0)))
        def _():
            x_start()

        @pl.when(jnp.logical_and(ki == 0, qi == 0))
        def _():
            dq_ref[...] = jnp.zeros_like(dq_ref)

        @pl.when(jnp.logical_and(pr == 0, jnp.logical_and(ki == 0, qi == 0)))
        def _():
            drs_ref[...] = jnp.zeros_like(drs_ref)

        @pl.when(qi == ki)
        def _():
            dk_ref[...] = jnp.zeros_like(dk_ref)
            dv_ref[...] = jnp.zeros_like(dv_ref)
            dcs_ref[...] = jnp.zeros_like(dcs_ref)

        def block(masked):
            lane = _iota((tq, LANES), 1)
            qv, kv, vv = q_ref[...], k_ref[...], v_ref[...]
            ov, dov, lsev = o_ref[...], do_ref[...], lse_ref[...]
            qrows = pl.ds(pl.multiple_of(qi * tq, tq), tq)
            dq_acc = jnp.zeros((tq, LANES), F32)
            dk_acc = jnp.zeros((tq, LANES), F32)
            dv_acc = jnp.zeros((tq, LANES), F32)
            dcs_acc = jnp.zeros((8, tq), F32)
            drs_acc = jnp.zeros((tq, LANES), F32)
            prod = dov * ov
            for hh in range(2):
                in_head = (lane < FOX_DH) if hh == 0 else (lane >= FOX_DH)
                zb = jnp.zeros_like(qv)
                qm = jnp.where(in_head, qv, zb)
                km = jnp.where(in_head, kv, zb)
                dom = jnp.where(in_head, dov, 0.0).astype(BF16)
                delta = jnp.sum(jnp.where(in_head, prod, 0.0), axis=1, keepdims=True)
                lse_h = jnp.sum(jnp.where(lane == hh * FOX_DH, lsev, 0.0), axis=1, keepdims=True)
                s = _dot(qm, kv, 1, 1) - ct_ref[pl.ds(2 * pr + hh, 1), :]
                p = jnp.exp(s - lse_h)
                if masked:
                    p = jnp.where(_iota((tq, tq), 0) >= _iota((tq, tq), 1), p, 0.0)
                p_b = p.astype(BF16)
                dv_acc = dv_acc + _dot(p_b, dom, 0, 0)
                dp = _dot(dom, vv, 1, 1)
                ds = p * (dp - delta)
                ds_b = ds.astype(BF16)
                dq_acc = dq_acc + _dot(ds_b, km, 1, 0)
                dk_acc = dk_acc + _dot(ds_b, qm, 0, 0)
                colsum = jnp.sum(ds, axis=0, keepdims=True)
                dcs_acc = dcs_acc + jnp.where(_iota((8, tq), 0) == hh, colsum, 0.0)
                rowsum = jnp.sum(ds, axis=1, keepdims=True)
                drs_acc = drs_acc + jnp.where(lane == 2 * pr + hh, rowsum, 0.0)
            drs_ref[qrows, :] += drs_acc
            dq_ref[qrows, :] += dq_acc
            dk_ref[...] += dk_acc
            dv_ref[...] += dv_acc
            dcs_ref[0] += dcs_acc

        @pl.when(qi > ki)
        def _():
            block(False)

        @pl.when(qi == ki)
        def _():
            block(True)

        @pl.when(jnp.logical_and(pr == n_pairs - 1, jnp.logical_and(ki == nq - 1, qi == nq - 1)))
        def _():
            x_finish()

    qmap = lambda p, j, i: (jnp.maximum(i, j), p)
    hbm = pl.BlockSpec(memory_space=pl.ANY)
    res = _pallas(
        body, name="fox_bwd", grid=(n_pairs, nq, nq),
        in_specs=[pl.BlockSpec((tq, LANES), qmap), pl.BlockSpec((tq, LANES), lambda p, j, i: (j, p)),
                  pl.BlockSpec((tq, LANES), lambda p, j, i: (j, p)),
                  pl.BlockSpec((FOX_HEADS, tq), lambda p, j, i: (0, j)),
                  pl.BlockSpec((tq, LANES), qmap), pl.BlockSpec((tq, LANES), qmap), pl.BlockSpec((tq, LANES), qmap)]
        + [hbm] * n,
        out_specs=[pl.BlockSpec((T, LANES), lambda p, j, i: (0, p)), pl.BlockSpec((tq, LANES), lambda p, j, i: (j, p)),
                   pl.BlockSpec((tq, LANES), lambda p, j, i: (j, p)), pl.BlockSpec((1, 8, tq), lambda p, j, i: (p, 0, j)),
                   pl.BlockSpec((T, LANES), lambda p, j, i: (0, 0))] + [hbm] * n,
        out_shape=[jax.ShapeDtypeStruct((T, FOX_W), F32)] * 3
        + [jax.ShapeDtypeStruct((n_pairs, 8, T), F32), jax.ShapeDtypeStruct((T, LANES), F32)]
        + _chip_exchange_shapes(hs),
        scratch_shapes=_chip_exchange_scratch(n),
        compiler_params=_params(dimension_semantics=("arbitrary",) * 3),
    )(qn, kn, vb, ct, o, do, lse, *hs)
    res = list(res)
    return res[:5] + [res[5:]]


def _fox_post(z, dq, dk, dv, dcs, drs, bias, qg, kg, dz_buf):
    T = z.shape[0]
    tm = min(ROW_BLOCK, T)
    nb = T // tm

    def body(z_ref, dq_ref, dk_ref, dv_ref, dcs_ref, drs_ref, b_ref, qg_ref, kg_ref, _buf_ref, dz_ref, dqg_ref, dkg_ref,
             db_ref, carry_ref):
        @pl.when(pl.program_id(0) == 0)
        def _():
            carry_ref[...] = jnp.zeros_like(carry_ref)
            dqg_ref[...] = jnp.zeros_like(dqg_ref)
            dkg_ref[...] = jnp.zeros_like(dkg_ref)
            db_ref[...] = jnp.zeros_like(db_ref)

        ones = _head_ones()
        for src, g_ref, d_ref, dg_ref, scale in ((0, qg_ref, dq_ref, dqg_ref, FOX_DH ** -0.5), (1, kg_ref, dk_ref, dkg_ref, 1.0)):
            xv = z_ref[:, src * FOX_W:(src + 1) * FOX_W]
            ms = _split_dot(ones, xv * xv, 1, 0, mat_first=False) * (1.0 / FOX_DH)
            rstd = lax.rsqrt(ms + EPS)
            xh = xv * rstd
            dn = d_ref[...] * scale
            dg_ref[...] += jnp.sum(dn * xh, axis=0, keepdims=True)
            dxh = dn * g_ref[...]
            mean = _split_dot(ones, dxh * xh, 1, 0, mat_first=False) * (1.0 / FOX_DH)
            dz_ref[:, src * FOX_W:(src + 1) * FOX_W] = (rstd * (dxh - xh * mean)).astype(BF16)
        dz_ref[:, 2 * FOX_W:3 * FOX_W] = dv_ref[...].astype(BF16)
        row8 = _iota((8, tm), 0)
        dct = jnp.zeros((8, tm), F32)
        for h in range(FOX_HEADS):
            src_row = dcs_ref[h // 2][h % 2:h % 2 + 1, :]
            dct = dct + jnp.where(row8 == h, src_row, 0.0)
        dct = drs_ref[...].T[0:8] - dct
        r, c = _iota((tm, tm), 0), _iota((tm, tm), 1)
        upper_b = (r >= c).astype(BF16)
        rc = _split_dot(upper_b, dct, 1, 0, mat_first=False) + carry_ref[...]
        carry_ref[...] = rc[:, 0:1]
        full = jnp.concatenate([rc, jnp.zeros((LANES - 8, tm), F32)], axis=0)
        dlogf = full.T
        xf = z_ref[:, 3 * FOX_W:FOX_COLS] + b_ref[...]
        df = dlogf * (1.0 - _sigmoid(xf))
        dz_ref[:, 3 * FOX_W:FOX_COLS] = df.astype(BF16)
        dz_ref[:, FOX_COLS:] = jnp.zeros((tm, SEG - FOX_COLS), BF16)
        db_ref[...] += jnp.sum(df, axis=0, keepdims=True)

    rev = lambda i: (nb - 1 - i, 0)
    fix2 = lambda i: (0, 0)
    return _pallas(
        body, name="fox_post", grid=(nb,),
        in_specs=[pl.BlockSpec((tm, SEG), lambda i: (nb - 1 - i, 1)), pl.BlockSpec((tm, FOX_W), rev),
                  pl.BlockSpec((tm, FOX_W), rev),
                  pl.BlockSpec((tm, FOX_W), rev), pl.BlockSpec((FOX_HEADS // 2, 8, tm), lambda i: (0, 0, nb - 1 - i)),
                  pl.BlockSpec((tm, LANES), rev),
                  pl.BlockSpec((1, LANES), fix2), pl.BlockSpec((1, FOX_W), fix2), pl.BlockSpec((1, FOX_W), fix2),
                  pl.BlockSpec(memory_space=pl.ANY)],
        out_specs=[pl.BlockSpec((tm, SEG), lambda i: (nb - 1 - i, 1)), pl.BlockSpec((1, FOX_W), fix2),
                   pl.BlockSpec((1, FOX_W), fix2), pl.BlockSpec((1, LANES), fix2)],
        out_shape=[jax.ShapeDtypeStruct(dz_buf.shape, BF16), jax.ShapeDtypeStruct((1, FOX_W), F32),
                   jax.ShapeDtypeStruct((1, FOX_W), F32), jax.ShapeDtypeStruct((1, LANES), F32)],
        scratch_shapes=[pltpu.VMEM((8, 1), F32)],
        input_output_aliases={9: 0},
        compiler_params=_params(dimension_semantics=("arbitrary",)),
    )(z, dq, dk, dv, dcs, drs, bias, qg, kg, dz_buf)


def _local_step(x, p, tgt, sm, W, rest_chunks, core):
    lbl, og, fb = sm["hg_lb_logits"], sm["hg_onorm_g"], sm["fox_f_bias"]
    fbias = jnp.pad(fb, ((0, 0), (0, LANES - FOX_HEADS)))
    qg = jnp.tile(sm["fox_q_norm_g"], (1, FOX_HEADS))
    kg = jnp.tile(sm["fox_k_norm_g"], (1, FOX_HEADS))

    h = _rms_fwd(x, sm["norm_mix_g"], name="rms_mix")
    z = _matmul(h, W["w_in"], name="mm_z")
    o_raw, ya, states = _hgrn_fwd(z, lbl, og)
    qn, kn, vb, ct = _fox_prep(z, fbias, qg, kg)
    yb, lse, gathered = _fox_fwd(qn, kn, vb, ct, rest_chunks)
    W = dict(W, **{n: _full_of_chunks(n, g) for n, g in zip(BIG[1:], gathered)})
    ua = _matmul(ya, W["w_branch_a"], name="mm_ua")
    ub = _matmul(yb, W["w_branch_b"], name="mm_ub")
    merged = _merge_fwd(ua, ub, z)
    x1 = _matmul(merged, W["w_out"], add=x, name="mm_x1")
    hf = _rms_fwd(x1, sm["norm_ffn_g"], name="rms_ffn")
    a = _matmul(hf, W["w_ffn_gate"], out_dtype=BF16, name="mm_ffn_a")
    b = _matmul(hf, W["w_ffn_up"], out_dtype=BF16, name="mm_ffn_b")
    act = _swiglu_fwd(a, b)
    x2 = _matmul(act, W["w_ffn_down"], add=x1, name="mm_x2")
    hp = _rms_fwd(x2, sm["norm_ple_g"], name="rms_ple")
    sp = _matmul(hp, W["w_ple_gate"], name="mm_sp")
    pp = _matmul(p, W["w_ple_proj"], name="mm_pp")
    dy, dsp, dpp, loss = _ple_loss(x2, sp, pp, tgt)

    G = {}
    G["w_ple_proj"] = _matmul(p, dpp, ta=True, out_dtype=BF16, name="mm_dw_ple_proj")
    G["w_ple_gate"] = _matmul(hp, dsp, ta=True, out_dtype=BF16, name="mm_dw_ple_gate")
    dhp = _matmul(dsp, W["w_ple_gate"], tb=True, name="mm_dhp")
    dx2, d_ple_g = _rms_bwd(x2, sm["norm_ple_g"], dhp, dy, name="rms_ple_bwd")
    dact = _matmul(dx2, W["w_ffn_down"], tb=True, name="mm_dact")
    G["w_ffn_down"] = _matmul(act, dx2, ta=True, out_dtype=BF16, name="mm_dw_ffn_down")
    da, db = _swiglu_bwd(a, b, dact)
    G["w_ffn_gate"] = _matmul(hf, da, ta=True, out_dtype=BF16, name="mm_dw_ffn_gate")
    G["w_ffn_up"] = _matmul(hf, db, ta=True, out_dtype=BF16, name="mm_dw_ffn_up")
    dhf = _matmul(da, W["w_ffn_gate"], tb=True, name="mm_dhf_a")
    dhf = _matmul(db, W["w_ffn_up"], tb=True, add=dhf, name="mm_dhf_b")
    dx1, d_ffn_g = _rms_bwd(x1, sm["norm_ffn_g"], dhf, dx2, name="rms_ffn_bwd")
    dmerged = _matmul(dx1, W["w_out"], tb=True, name="mm_dmerged")
    G["w_out"] = _matmul(merged, dx1, ta=True, out_dtype=BF16, name="mm_dw_out")
    dua, dub, dz = _merge_bwd(dmerged, ua, ub, z)
    G["w_branch_a"] = _matmul(ya, dua, ta=True, out_dtype=BF16, name="mm_dw_branch_a")
    G["w_branch_b"] = _matmul(yb, dub, ta=True, out_dtype=BF16, name="mm_dw_branch_b")
    dya = _matmul(dua, W["w_branch_a"], tb=True, name="mm_dya")
    dyb = _matmul(dub, W["w_branch_b"], tb=True, name="mm_dyb")
    hb_rest = _sibling_sums({n: G[n] for n in BIG[1:]}, core, tag="rest")
    dq, dk, dv, dcs, drs, got_rest = _fox_bwd(qn, kn, vb, ct, yb, dyb, lse, hb_rest)
    dz, d_qg, d_kg, d_fb = _fox_post(z, dq, dk, dv, dcs, drs, fbias, qg, kg, dz)
    dz, d_lbl, d_og = _hgrn_bwd(z, o_raw, dya, states, lbl, og, dz)
    dh = _matmul(dz, W["w_in"], tb=True, name="mm_dh")
    G["w_in"] = _matmul(h, dz, ta=True, out_dtype=BF16, name="mm_dw_in")
    grad_x, d_mix_g = _rms_bwd(x, sm["norm_mix_g"], dh, dx1, name="rms_mix_bwd")

    gs = {"norm_mix_g": d_mix_g, "hg_lb_logits": d_lbl, "hg_onorm_g": d_og, "fox_f_bias": d_fb[:, :FOX_HEADS],
          "fox_q_norm_g": d_qg.reshape(FOX_HEADS, FOX_DH).sum(0, keepdims=True),
          "fox_k_norm_g": d_kg.reshape(FOX_HEADS, FOX_DH).sum(0, keepdims=True),
          "norm_ffn_g": d_ffn_g, "norm_ple_g": d_ple_g}
    return loss, grad_x, gs, G["w_in"], hb_rest, got_rest


def _pack_rows(parts, total):
    buf = jnp.concatenate(parts, axis=-2)
    pad = total - buf.shape[-2]
    widths = [(0, 0)] * (buf.ndim - 2) + [(0, pad), (0, 0)]
    return jnp.pad(buf, widths)


def _chunk_of_shard(n, w):
    if n == "w_in":
        return jnp.pad(w, ((0, 0), (0, IN_SHARD_PAD - IN_SHARD)))
    if n in ("w_ffn_gate", "w_ffn_up"):
        return jnp.pad(w, ((0, 0), (0, FF_SHARD_PAD - FF_SHARD)))
    if n == "w_ffn_down":
        return jnp.pad(w, ((0, FF_SHARD_PAD - FF_SHARD), (0, 0)))
    return w


def _full_of_chunks(n, g):
    _, a, b = g.shape
    if BIG_SHAPE[n][2] == 0:
        return g.reshape(N_DEV * a, b)
    if n == "w_in":
        w = g[:, :, :IN_SHARD].transpose(1, 0, 2).reshape(a, IN_COLS)
        gap = jnp.zeros((a, SEG - FOX_LOGICAL), g.dtype)
        return jnp.concatenate([w[:, :HG_COLS], w[:, HG_COLS:HG_COLS + FOX_LOGICAL], gap, w[:, HG_COLS + FOX_LOGICAL:]],
                               axis=1)
    return g.transpose(1, 0, 2).reshape(a, N_DEV * b)


def _chunks_of_full(n, g):
    if BIG_SHAPE[n][2] == 0:
        return g.reshape(N_DEV, g.shape[0] // N_DEV, g.shape[1])
    if n == "w_in":
        a = g.shape[0]
        w = jnp.concatenate([g[:, :HG_COLS], g[:, SEG:SEG + FOX_LOGICAL], g[:, 2 * SEG:]], axis=1)
        w = w.reshape(a, N_DEV, IN_SHARD).transpose(1, 0, 2)
        return jnp.pad(w, ((0, 0), (0, 0), (0, IN_SHARD_PAD - IN_SHARD)))
    return g.reshape(g.shape[0], N_DEV, g.shape[1] // N_DEV).transpose(1, 0, 2)


def _pack_small(vals, loss_row=None):
    parts = [vals[n].reshape(SMALL_ROWS[n], -1) for n in SMALL]
    parts = [jnp.pad(v, ((0, 0), (0, LANES - v.shape[1]))) for v in parts]
    if loss_row is not None:
        parts.append(loss_row)
    return _pack_rows(parts, SMALL_TOTAL)


def _unpack_small(buf, like):
    out, r0 = {}, 0
    for n in SMALL:
        rows, size = SMALL_ROWS[n], like[n].size
        blk = buf[r0:r0 + rows]
        out[n] = (blk if size == rows * LANES else blk[:, :size]).reshape(like[n].shape)
        r0 += rows
    return out


def _place():
    return lax.axis_index("x"), lax.axis_index("y"), lax.axis_index("c")


def _gather_steps(x_refs, out_refs, send_sems, recv_sems, local_sems):
    n = len(x_refs)
    x, y, c = _place()
    me, sibling = (x, y, c), (x, y, 1 - c)
    chips = [(1 - x, y), (x, 1 - y), (1 - x, 1 - y)]

    def slot(i, px, py, pc):
        return out_refs[i].at[4 * px + 2 * py + pc]

    def copy(k, i, blk, to, own=False):
        return pltpu.make_async_remote_copy(
            src_ref=x_refs[i] if own else slot(i, *blk), dst_ref=slot(i, *blk),
            send_sem=send_sems.at[k, i], recv_sem=recv_sems.at[k, i], device_id=to, device_id_type=MESH)

    def mine():
        return [pltpu.make_async_copy(x_refs[i], slot(i, *me), local_sems.at[i]) for i in range(n)]

    def first():
        cps = [copy(0, i, me, sibling, own=True) for i in range(n)]
        return cps + [copy(1 + j, i, me, (*chip, c), own=True) for j, chip in enumerate(chips) for i in range(n)]

    def passed():
        return [copy(4 + j, i, (*chip, c), sibling) for j, chip in enumerate(chips) for i in range(n)]

    def start():
        for cp in mine() + first():
            cp.start()

    def forward():
        fws = passed()
        for j, chip in enumerate(chips):
            for i in range(n):
                copy(1 + j, i, (*chip, c), me).wait_recv()
                fws[j * n + i].start()

    def finish():
        for i in range(n):
            copy(0, i, sibling, me).wait_recv()
        for j, chip in enumerate(chips):
            for i in range(n):
                copy(4 + j, i, (*chip, 1 - c), me).wait_recv()
        for cp in first() + passed():
            cp.wait_send()
        for cp in mine():
            cp.wait()

    return start, forward, finish


def _gather_scratch(n):
    return [pltpu.SemaphoreType.DMA((7, n)), pltpu.SemaphoreType.DMA((7, n)), pltpu.SemaphoreType.DMA((n,))]


def _all_gather(blocks, *, name):
    n = len(blocks)

    def body(*refs):
        for step in _gather_steps(refs[:n], refs[n:2 * n], *refs[2 * n:]):
            step()

    hbm = pl.BlockSpec(memory_space=pl.ANY)
    return _pallas(
        body, name=name, out_shape=[jax.ShapeDtypeStruct((N_DEV,) + b.shape, b.dtype) for b in blocks],
        in_specs=[hbm] * n, out_specs=[hbm] * n, scratch_shapes=_gather_scratch(n),
    )(*blocks)


def _sibling_exchange(gs, *, name):
    n = len(gs)

    def body(*refs):
        g_refs, out_refs = refs[:n], refs[n:2 * n]
        send_sems, recv_sems = refs[2 * n:]
        x, y, c = _place()
        cps = [pltpu.make_async_remote_copy(
            src_ref=g_refs[i].at[:, pl.ds(1 - c, 1)], dst_ref=out_refs[i], send_sem=send_sems.at[i],
            recv_sem=recv_sems.at[i], device_id=(x, y, 1 - c), device_id_type=MESH) for i in range(n)]
        for cp in cps:
            cp.start()
        for cp in cps:
            cp.wait()

    hbm = pl.BlockSpec(memory_space=pl.ANY)
    return _pallas(
        body, name=name, out_shape=[jax.ShapeDtypeStruct((N_CHIP, 1) + g.shape[2:], g.dtype) for g in gs],
        in_specs=[hbm] * n, out_specs=[hbm] * n,
        scratch_shapes=[pltpu.SemaphoreType.DMA((n,)), pltpu.SemaphoreType.DMA((n,))],
    )(*gs)


def _chip_sum(g4, got, core, *, name):
    _, _, a, b = g4.shape

    def body(c_ref, g_ref, r_ref, h_ref):
        h_ref[0] = (g_ref[0, 0].astype(F32) + r_ref[0, 0].astype(F32)).astype(BF16)

    grid_spec = pltpu.PrefetchScalarGridSpec(
        num_scalar_prefetch=1, grid=(N_CHIP,),
        in_specs=[pl.BlockSpec((1, 1, a, b), lambda j, c: (j, c[0], 0, 0)),
                  pl.BlockSpec((1, 1, a, b), lambda j, c: (j, 0, 0, 0))],
        out_specs=[pl.BlockSpec((1, a, b), lambda j, c: (j, 0, 0))])
    return _pallas(
        body, name=name, grid_spec=grid_spec, out_shape=[jax.ShapeDtypeStruct((N_CHIP, a, b), BF16)],
        compiler_params=_params(dimension_semantics=("arbitrary",)),
    )(core, g4, got)[0]


def _chip_exchange(hs, *, name):
    n = len(hs)

    def body(*refs):
        for step in _chip_exchange_steps(refs[:n], refs[n:2 * n], *refs[2 * n:]):
            step()

    hbm = pl.BlockSpec(memory_space=pl.ANY)
    return _pallas(
        body, name=name, out_shape=_chip_exchange_shapes(hs), in_specs=[hbm] * n, out_specs=[hbm] * n,
        scratch_shapes=_chip_exchange_scratch(n),
    )(*hs)


def _chip_exchange_steps(h_refs, out_refs, send_sems, recv_sems):
    n = len(h_refs)
    x, y, c = _place()
    chips = [(1 - x, y), (x, 1 - y), (1 - x, 1 - y)]

    def copies():
        return [pltpu.make_async_remote_copy(
            src_ref=h_refs[i].at[2 * px + py], dst_ref=out_refs[i].at[k], send_sem=send_sems.at[k, i],
            recv_sem=recv_sems.at[k, i], device_id=(px, py, c), device_id_type=MESH)
            for k, (px, py) in enumerate(chips) for i in range(n)]

    def start():
        for cp in copies():
            cp.start()

    def finish():
        for cp in copies():
            cp.wait()

    return start, finish


def _chip_exchange_shapes(hs):
    return [jax.ShapeDtypeStruct((3,) + h.shape[1:], h.dtype) for h in hs]


def _chip_exchange_scratch(n):
    return [pltpu.SemaphoreType.DMA((3, n)), pltpu.SemaphoreType.DMA((3, n))]


def _sibling_sums(G, core, *, tag):
    g4 = []
    for n, g in G.items():
        gc = _chunks_of_full(n, g)
        g4.append(gc.reshape((N_CHIP, 2) + gc.shape[1:]))
    got = _sibling_exchange(g4, name="grads_to_sibling_" + tag)
    return [_chip_sum(g, r, core, name="chip_sum_" + n) for n, g, r in zip(G, g4, got)]


def _adam_math(w, g, m, v):
    m = ADAM_B1 * m + (1.0 - ADAM_B1) * g
    v = ADAM_B2 * v + (1.0 - ADAM_B2) * (g * g)
    m_hat = m / (1.0 - ADAM_B1 ** ADAM_STEP)
    v_hat = v / (1.0 - ADAM_B2 ** ADAM_STEP)
    delta = -ADAM_LR * (m_hat / (jnp.sqrt(v_hat) + ADAM_EPS) + ADAM_WD * w)
    return delta, m, v


def _adam_shard(hb, got, chip, w, m, v, *, name):
    _, r, c = w.shape
    _, a, b = hb.shape
    tr = r if r <= 512 else 256
    ta = tr if r // tr > 1 else a

    def body(j_ref, h_ref, r_ref, w_ref, m_ref, v_ref, g_ref, d_ref, nm_ref, nv_ref):
        parts = [h_ref[0], r_ref[0], r_ref[1], r_ref[2]]
        g = None
        for part in parts:
            part = part[:tr, :c].astype(F32)
            g = part if g is None else g + part
        d, nm, nv = _adam_math(w_ref[0], g, m_ref[0], v_ref[0])
        g_ref[0] = g
        d_ref[0] = d
        nm_ref[0] = nm
        nv_ref[0] = nv

    blk = pl.BlockSpec((1, tr, c), lambda i, j: (0, i, 0))
    grid_spec = pltpu.PrefetchScalarGridSpec(
        num_scalar_prefetch=1, grid=(r // tr,),
        in_specs=[pl.BlockSpec((1, ta, b), lambda i, j: (j[0], i, 0)),
                  pl.BlockSpec((3, ta, b), lambda i, j: (0, i, 0)), blk, blk, blk],
        out_specs=[blk] * 4)
    return _pallas(
        body, name=name, grid_spec=grid_spec, out_shape=[jax.ShapeDtypeStruct((1, r, c), F32)] * 4,
        compiler_params=_params(dimension_semantics=("arbitrary",)),
    )(chip, hb, got, w, m, v)


def _small_all_reduce_adam(gs, w, m, v):
    def body(g_ref, w_ref, m_ref, v_ref, sum_ref, d_ref, nm_ref, nv_ref, gather, send_sems, recv_sems):
        x, y, c = _place()
        my = 4 * x + 2 * y + c
        gather[my] = g_ref[...]
        cps = []
        for k in range(1, N_DEV):
            to = (x ^ (k >> 2), y ^ ((k >> 1) & 1), c ^ (k & 1))
            cps.append(pltpu.make_async_remote_copy(
                src_ref=g_ref, dst_ref=gather.at[my], send_sem=send_sems.at[k - 1], recv_sem=recv_sems.at[k - 1],
                device_id=to, device_id_type=MESH))
        for cp in cps:
            cp.start()
        for cp in cps:
            cp.wait()
        total = gather[0]
        for d in range(1, N_DEV):
            total = total + gather[d]
        dlt, nm, nv = _adam_math(w_ref[...], total, m_ref[...], v_ref[...])
        sum_ref[...] = total
        d_ref[...] = dlt
        nm_ref[...] = nm
        nv_ref[...] = nv

    vm = pl.BlockSpec(memory_space=pltpu.VMEM)
    return _pallas(
        body, name="small_all_reduce_adam", out_shape=[jax.ShapeDtypeStruct((SMALL_TOTAL, LANES), F32)] * 4,
        in_specs=[vm] * 4, out_specs=[vm] * 4,
        scratch_shapes=[pltpu.VMEM((N_DEV, SMALL_TOTAL, LANES), F32), pltpu.SemaphoreType.DMA((7,)),
                        pltpu.SemaphoreType.DMA((7,))],
            )(gs, w, m, v)


def kernel(x, p, norm_mix_g, w_in, hg_lb_logits, hg_onorm_g, fox_f_bias, fox_q_norm_g, fox_k_norm_g, w_branch_a, w_branch_b, w_out, norm_ffn_g, w_ffn_gate, w_ffn_up, w_ffn_down, norm_ple_g, w_ple_gate, w_ple_proj, loss_target, m_norm_mix_g, m_w_in, m_hg_lb_logits, m_hg_onorm_g, m_fox_f_bias, m_fox_q_norm_g, m_fox_k_norm_g, m_w_branch_a, m_w_branch_b, m_w_out, m_norm_ffn_g, m_w_ffn_gate, m_w_ffn_up, m_w_ffn_down, m_norm_ple_g, m_w_ple_gate, m_w_ple_proj, v_norm_mix_g, v_w_in, v_hg_lb_logits, v_hg_onorm_g, v_fox_f_bias, v_fox_q_norm_g, v_fox_k_norm_g, v_w_branch_a, v_w_branch_b, v_w_out, v_norm_ffn_g, v_w_ffn_gate, v_w_ffn_up, v_w_ffn_down, v_norm_ple_g, v_w_ple_gate, v_w_ple_proj):
    args = dict(locals())
    wts = {n: args[n] for n in BIG + SMALL}
    mom = {n: args["m_" + n] for n in BIG + SMALL}
    var = {n: args["v_" + n] for n in BIG + SMALL}
    sm = {n: wts[n] for n in SMALL}

    xi, yi, ci = _place()
    core = jnp.reshape(ci, (1,)).astype(jnp.int32)
    chip = jnp.reshape(2 * xi + yi, (1,)).astype(jnp.int32)
    chunks = [_chunk_of_shard(n, wts[n][0].astype(BF16)) for n in BIG]
    assert BIG[0] == "w_in"
    w_in_full = _full_of_chunks("w_in", _all_gather(chunks[:1], name="w_in_all_gather")[0])

    loss_blk, grad_x, gs, g_w_in, hb_rest, got_rest = _local_step(
        x[0], p[0, 0], loss_target[0], sm, {"w_in": w_in_full}, chunks[1:], core)

    hb_in = _sibling_sums({"w_in": g_w_in}, core, tag="w_in")
    got_in = _chip_exchange(hb_in, name="grads_to_chips_w_in")
    g_big, d_big, nm_big, nv_big = {}, {}, {}, {}
    for n, h, r in zip(BIG, hb_in + hb_rest, list(got_in) + list(got_rest)):
        g_big[n], d_big[n], nm_big[n], nv_big[n] = _adam_shard(h, r, chip, wts[n], mom[n], var[n], name="adam_" + n)

    s_sum, s_d, s_nm, s_nv = _small_all_reduce_adam(
        _pack_small(gs, loss_blk[0:1]), _pack_small(sm), _pack_small({n: mom[n] for n in SMALL}),
        _pack_small({n: var[n] for n in SMALL}))
    loss = s_sum[LOSS_ROW, 0]
    g_small, d_small, nm_small, nv_small = (_unpack_small(t, sm) for t in (s_sum, s_d, s_nm, s_nv))

    order = ["norm_mix_g", "w_in", "hg_lb_logits", "hg_onorm_g", "fox_f_bias", "fox_q_norm_g", "fox_k_norm_g",
             "w_branch_a", "w_branch_b", "w_out", "norm_ffn_g", "w_ffn_gate", "w_ffn_up", "w_ffn_down", "norm_ple_g",
             "w_ple_gate", "w_ple_proj"]
    outs = [loss, grad_x[None]]
    for big, small in ((g_big, g_small), (d_big, d_small), (nm_big, nm_small), (nv_big, nv_small)):
        outs += [big[n] if n in big else small[n] for n in order]
    return tuple(outs)
```

```python
import functools

import jax
import jax.numpy as jnp
from jax import lax
from jax.experimental import pallas as pl
from jax.experimental.pallas import tpu as pltpu

F32 = jnp.float32
BF16 = jnp.bfloat16

D_MODEL = 1024
PLE_DIM = 256
HG_HEADS = 4
HG_DK = 128
HG_CHUNK = 64
HG_SUB = 16
HG_W = HG_HEADS * HG_DK
FOX_HEADS = 8
FOX_DH = 64
FOX_W = FOX_HEADS * FOX_DH
D_FF = 2816
EPS = 1e-6
N_DEV = 8
N_CHIP = 4
LANES = 128
FOX_COLS = 3 * FOX_W + LANES
HG_COLS = 4 * HG_W
GATE_COLS = 2 * D_MODEL
IN_COLS = HG_COLS + 3 * FOX_W + FOX_HEADS + GATE_COLS
FOX_LOGICAL = 3 * FOX_W + FOX_HEADS
SEG = 2048
IN_PAD = 3 * SEG
IN_SHARD = IN_COLS // N_DEV
IN_SHARD_PAD = 768
FF_SHARD = D_FF // N_DEV
FF_SHARD_PAD = 384
FF_PAD = N_DEV * FF_SHARD_PAD
EXP_CLAMP = 80.0
LOG2E = 1.4426950408889634

ADAM_LR = 0.001
ADAM_B1 = 0.9
ADAM_B2 = 0.999
ADAM_EPS = 1e-08
ADAM_WD = 0.01
ADAM_STEP = 10

MESH = pl.DeviceIdType.MESH
VMEM_LIMIT = 56 * 1024 * 1024
ROW_BLOCK = 512

BIG = ["w_in", "w_branch_a", "w_branch_b", "w_out", "w_ffn_gate", "w_ffn_up", "w_ffn_down",
       "w_ple_gate", "w_ple_proj"]
BIG_SHAPE = {
    "w_in": (D_MODEL, IN_COLS, 1), "w_branch_a": (HG_W, D_MODEL, 1), "w_branch_b": (FOX_W, D_MODEL, 1),
    "w_out": (D_MODEL, D_MODEL, 0), "w_ffn_gate": (D_MODEL, D_FF, 1), "w_ffn_up": (D_MODEL, D_FF, 1),
    "w_ffn_down": (D_FF, D_MODEL, 0), "w_ple_gate": (D_MODEL, D_MODEL, 0), "w_ple_proj": (PLE_DIM, D_MODEL, 1),
}

SMALL = ["norm_mix_g", "hg_lb_logits", "hg_onorm_g", "fox_f_bias", "fox_q_norm_g", "fox_k_norm_g",
         "norm_ffn_g", "norm_ple_g"]
SMALL_ROWS = {"norm_mix_g": 8, "hg_lb_logits": 8, "hg_onorm_g": 1, "fox_f_bias": 1, "fox_q_norm_g": 1,
              "fox_k_norm_g": 1, "norm_ffn_g": 8, "norm_ple_g": 8}
SMALL_TOTAL = 40
LOSS_ROW = 36


def _pallas(body, **kw):
    return pl.pallas_call(body, **kw)


def _params(**kw):
    return pltpu.CompilerParams(vmem_limit_bytes=VMEM_LIMIT, **kw)


def _pick(n, target):
    if n <= target:
        return n
    best = None
    for t in range(LANES, target + 1, LANES):
        if n % t == 0:
            best = t
    assert best is not None, (n, target)
    return best


def _dot(a, b, ca, cb):
    return lax.dot_general(a, b, (((ca,), (cb,)), ((), ())), preferred_element_type=F32)


def _split_dot(mat, x, ca, cb, terms=2, mat_first=True):
    acc = None
    rem = x
    for _ in range(terms):
        part = rem.astype(BF16)
        rem = rem - part.astype(F32)
        p = _dot(mat, part, ca, cb) if mat_first else _dot(part, mat, ca, cb)
        acc = p if acc is None else acc + p
    return acc


def _sigmoid(x):
    return 1.0 / (1.0 + jnp.exp(-x))


def _iota(shape, dim):
    return lax.broadcasted_iota(jnp.int32, shape, dim)


def _matmul(a, b, *, name, ta=False, tb=False, out_dtype=F32, add=None):
    (K, M) = a.shape if ta else a.shape[::-1]
    (N, Kb) = b.shape if tb else b.shape[::-1]
    assert K == Kb, (a.shape, b.shape, ta, tb)
    if ta:
        tm, tn, tk = _pick(M, 2 * ROW_BLOCK), _pick(N, 2 * ROW_BLOCK), _pick(K, 4 * ROW_BLOCK)
    else:
        tm, tn, tk = _pick(M, 2 * ROW_BLOCK), _pick(N, 2048), _pick(K, 3072)
    nk = K // tk
    use_scratch = nk > 1 and out_dtype != F32

    def body(*refs):
        refs = list(refs)
        a_ref, b_ref = refs[:2]
        add_ref = refs[2] if add is not None else None
        o_ref = refs[3] if add is not None else refs[2]
        k = pl.program_id(2)
        p = _dot(a_ref[...].astype(BF16), b_ref[...].astype(BF16), 0 if ta else 1, 1 if tb else 0)

        def with_add(r):
            return r if add is None else r + add_ref[...].astype(F32)

        if nk == 1:
            o_ref[...] = with_add(p).astype(out_dtype)
        elif not use_scratch:
            @pl.when(k == 0)
            def _():
                o_ref[...] = with_add(p)

            @pl.when(k > 0)
            def _():
                o_ref[...] += p
        else:
            acc_ref = refs[-1]

            @pl.when(k == 0)
            def _():
                acc_ref[...] = p

            @pl.when(k > 0)
            def _():
                acc_ref[...] += p

            @pl.when(k == nk - 1)
            def _():
                o_ref[...] = with_add(acc_ref[...]).astype(out_dtype)

    a_spec = pl.BlockSpec((tk, tm), lambda i, j, k: (k, i)) if ta else pl.BlockSpec((tm, tk), lambda i, j, k: (i, k))
    b_spec = pl.BlockSpec((tn, tk), lambda i, j, k: (j, k)) if tb else pl.BlockSpec((tk, tn), lambda i, j, k: (k, j))
    o_spec = pl.BlockSpec((tm, tn), lambda i, j, k: (i, j))
    in_specs = [a_spec, b_spec] + ([o_spec] if add is not None else [])
    args = (a, b) + ((add,) if add is not None else ())
    return _pallas(
        body, name=name, grid=(M // tm, N // tn, nk), in_specs=in_specs, out_specs=o_spec,
        out_shape=jax.ShapeDtypeStruct((M, N), out_dtype),
        scratch_shapes=[pltpu.VMEM((tm, tn), F32)] if use_scratch else [],
        compiler_params=_params(dimension_semantics=("arbitrary",) * 3),
    )(*args)


def _row_map(nb, reverse, seg):
    if reverse:
        return lambda i: (nb - 1 - i, seg)
    return lambda i: (i, seg)


def _row_call(body, *, name, T, ins, outs, acc_outs=(), tm=ROW_BLOCK, reverse=False):
    tm = min(tm, T)
    nb = T // tm
    in_specs, args = [], []
    for arr, how in ins:
        args.append(arr)
        if how is True:
            in_specs.append(pl.BlockSpec((tm, arr.shape[1]), _row_map(nb, reverse, 0)))
        elif how is False:
            in_specs.append(pl.BlockSpec(arr.shape, lambda i, _n=arr.ndim: (0,) * _n))
        else:
            in_specs.append(pl.BlockSpec((tm, SEG), _row_map(nb, reverse, how[0])))
    out_specs, out_shape = [], []
    for o in outs:
        c, dt = o[0], o[1]
        total, seg = o[2] if len(o) > 2 else (c, 0)
        out_specs.append(pl.BlockSpec((tm, c), _row_map(nb, reverse, seg)))
        out_shape.append(jax.ShapeDtypeStruct((T, total), dt))
    for shp, dt in acc_outs:
        out_specs.append(pl.BlockSpec(shp, lambda i, _n=len(shp): (0,) * _n))
        out_shape.append(jax.ShapeDtypeStruct(shp, dt))
    return _pallas(body, name=name, grid=(nb,), in_specs=in_specs, out_specs=out_specs, out_shape=out_shape,
                   compiler_params=_params(dimension_semantics=("arbitrary",)))(*args)


def _rms_fwd(x, g, *, name):
    T = x.shape[0]

    def body(x_ref, g_ref, h_ref):
        xv = x_ref[...]
        rstd = lax.rsqrt(jnp.mean(xv * xv, axis=-1, keepdims=True) + EPS)
        h_ref[...] = (xv * rstd * g_ref[...]).astype(BF16)

    return _row_call(body, name=name, T=T, ins=[(x, True), (g, False)], outs=[(D_MODEL, BF16)])[0]


def _rms_bwd(x, g, dh, dres, *, name):
    T = x.shape[0]

    def body(x_ref, g_ref, dh_ref, dres_ref, dx_ref, dg_ref):
        xv = x_ref[...]
        rstd = lax.rsqrt(jnp.mean(xv * xv, axis=-1, keepdims=True) + EPS)
        xh = xv * rstd
        dhv = dh_ref[...]
        part = jnp.sum(dhv * xh, axis=0, keepdims=True)

        @pl.when(pl.program_id(0) == 0)
        def _():
            dg_ref[...] = part

        @pl.when(pl.program_id(0) > 0)
        def _():
            dg_ref[...] += part

        dxh = dhv * g_ref[...]
        dx_ref[...] = rstd * (dxh - xh * jnp.mean(dxh * xh, axis=-1, keepdims=True)) + dres_ref[...]

    return _row_call(body, name=name, T=T, ins=[(x, True), (g, False), (dh, True), (dres, True)],
                     outs=[(D_MODEL, F32)], acc_outs=[((1, D_MODEL), F32)])


def _merge_fwd(ua, ub, zg):
    def body(ua_ref, ub_ref, zg_ref, m_ref):
        ga = _sigmoid(zg_ref[:, :D_MODEL])
        gb = _sigmoid(zg_ref[:, D_MODEL:])
        m_ref[...] = (ga * ua_ref[...] + gb * ub_ref[...]).astype(BF16)

    return _row_call(body, name="merge_fwd", T=ua.shape[0], ins=[(ua, True), (ub, True), (zg, (2,))],
                     outs=[(D_MODEL, BF16)])[0]


def _merge_bwd(dm, ua, ub, zg):
    def body(dm_ref, ua_ref, ub_ref, zg_ref, dua_ref, dub_ref, dzg_ref):
        ga = _sigmoid(zg_ref[:, :D_MODEL])
        gb = _sigmoid(zg_ref[:, D_MODEL:])
        dmv = dm_ref[...]
        dua_ref[...] = (dmv * ga).astype(BF16)
        dub_ref[...] = (dmv * gb).astype(BF16)
        dzg_ref[:, :D_MODEL] = (dmv * ua_ref[...] * ga * (1.0 - ga)).astype(BF16)
        dzg_ref[:, D_MODEL:] = (dmv * ub_ref[...] * gb * (1.0 - gb)).astype(BF16)

    return _row_call(body, name="merge_bwd", T=dm.shape[0], ins=[(dm, True), (ua, True), (ub, True), (zg, (2,))],
                     outs=[(D_MODEL, BF16), (D_MODEL, BF16), (SEG, BF16, (IN_PAD, 2))])


def _swiglu_fwd(a, b):
    def body(a_ref, b_ref, o_ref):
        av = a_ref[...].astype(F32)
        o_ref[...] = (av * _sigmoid(av) * b_ref[...].astype(F32)).astype(BF16)

    return _row_call(body, name="swiglu_fwd", T=a.shape[0], ins=[(a, True), (b, True)], outs=[(a.shape[1], BF16)])[0]


def _swiglu_bwd(a, b, dact):
    def body(a_ref, b_ref, d_ref, da_ref, db_ref):
        av = a_ref[...].astype(F32)
        bv = b_ref[...].astype(F32)
        dv = d_ref[...]
        sg = _sigmoid(av)
        da_ref[...] = (dv * bv * sg * (1.0 + av * (1.0 - sg))).astype(BF16)
        db_ref[...] = (dv * av * sg).astype(BF16)

    return _row_call(body, name="swiglu_bwd", T=a.shape[0], ins=[(a, True), (b, True), (dact, True)],
                     outs=[(a.shape[1], BF16), (a.shape[1], BF16)])


def _ple_loss(x2, sp, pp, tgt):
    def body(x_ref, sp_ref, pp_ref, t_ref, dy_ref, dsp_ref, dpp_ref, loss_ref):
        gp = _sigmoid(sp_ref[...])
        ppv = pp_ref[...]
        err = x_ref[...] + gp * ppv - t_ref[...]
        part = 0.5 * jnp.sum(jnp.mean(err * err, axis=-1, keepdims=True), axis=0, keepdims=True)
        part = jnp.broadcast_to(part, loss_ref.shape)

        @pl.when(pl.program_id(0) == 0)
        def _():
            loss_ref[...] = part

        @pl.when(pl.program_id(0) > 0)
        def _():
            loss_ref[...] += part

        dy = err * (1.0 / D_MODEL)
        dy_ref[...] = dy
        dsp_ref[...] = (dy * ppv * gp * (1.0 - gp)).astype(BF16)
        dpp_ref[...] = (dy * gp).astype(BF16)

    return _row_call(body, name="ple_loss", T=x2.shape[0], ins=[(x2, True), (sp, True), (pp, True), (tgt, True)],
                     outs=[(D_MODEL, F32), (D_MODEL, BF16), (D_MODEL, BF16)], acc_outs=[((8, LANES), F32)])


def _hg_consts():
    C = HG_CHUNK
    r, c = _iota((C, C), 0), _iota((C, C), 1)
    tri = (c <= r)
    same = (r // HG_SUB) == (c // HG_SUB)
    return tri, (tri & same)


def _hg_chunk_fwd(q, f, lb, tri_b, sub_b):
    sgq = _sigmoid(q)
    qt = q * sgq
    sg = _sigmoid(f)
    fg = lb + (1.0 - lb) * sg
    kf = (1.0 - lb) * (1.0 - sg)
    logf = jnp.log(fg)
    b = _split_dot(tri_b, logf, 1, 0)
    w = _split_dot(sub_b, logf, 1, 0)
    return sgq, qt, sg, fg, kf, b, w


def _hg_scores(qs_b, kf, b, row):
    C, S = HG_CHUNK, HG_SUB
    parts, ks = [], []
    for blk in range(C // S):
        ref = jnp.zeros_like(b[0:1]) if blk == 0 else b[blk * S - 1:blk * S]
        e = jnp.exp(jnp.minimum(ref - b, EXP_CLAMP))
        e = jnp.where(row < (blk + 1) * S, e, 0.0)
        k_b = (kf * e).astype(BF16)
        ks.append((e, k_b))
        parts.append(_dot(qs_b[blk * S:(blk + 1) * S], k_b, 1, 1))
    return jnp.concatenate(parts, axis=0), ks


def _hgrn_fwd(z, lb_logits, gain):
    T = z.shape[0]
    RB = min(ROW_BLOCK, T)
    nb, cpb = T // RB, RB // HG_CHUNK
    C, DK = HG_CHUNK, HG_DK

    def body(z_ref, lg_ref, g_ref, o_ref, y_ref, st_ref, s_ref):
        @pl.when(pl.program_id(0) == 0)
        def _():
            s_ref[...] = jnp.zeros_like(s_ref)

        lg = lg_ref[...]
        lb_all = 1.0 / (1.0 + jnp.exp(lg[1:2] - lg[0:1]))
        gain_v = g_ref[...]
        tri, sub = _hg_consts()
        tri_b, sub_b = tri.astype(BF16), sub.astype(BF16)
        row = _iota((C, DK), 0)

        def chunk(ci, carry):
            r0 = pl.multiple_of(ci * C, C)
            rows = pl.ds(r0, C)
            for h in range(HG_HEADS):
                cs = slice(h * DK, (h + 1) * DK)
                q = z_ref[rows, cs]
                f = z_ref[rows, HG_W + h * DK:HG_W + (h + 1) * DK]
                v = z_ref[rows, 2 * HG_W + h * DK:2 * HG_W + (h + 1) * DK]
                g = z_ref[rows, 3 * HG_W + h * DK:3 * HG_W + (h + 1) * DK]
                lb = lb_all[:, cs]
                _, qt, _, _, kf, b, w = _hg_chunk_fwd(q, f, lb, tri_b, sub_b)
                st = s_ref[h]
                st_ref[pl.ds(pl.multiple_of((ci * HG_HEADS + h) * DK, DK), DK), :] = st
                v_b = v.astype(BF16)
                qs_b = (qt * jnp.exp(w)).astype(BF16)
                a, _ = _hg_scores(qs_b, kf, b, row)
                a = jnp.where(tri, a, 0.0)
                qd = qt * jnp.exp(b)
                o = _dot(qd.astype(BF16), st.astype(BF16), 1, 1) + _dot(a.astype(BF16), v_b, 1, 0)
                bl = b[C - 1:C]
                kd = kf * jnp.exp(bl - b)
                s_ref[h] = st * jnp.exp(bl) + _dot(v_b, kd.astype(BF16), 0, 0)
                o_ref[rows, cs] = o
                rstd = lax.rsqrt(jnp.mean(o * o, axis=-1, keepdims=True) + EPS)
                y_ref[rows, cs] = (o * rstd * gain_v * (g * _sigmoid(g))).astype(BF16)
            return carry

        lax.fori_loop(0, cpb, chunk, 0)

    return _pallas(
        body, name="hgrn_fwd", grid=(nb,),
        in_specs=[pl.BlockSpec((RB, HG_COLS), lambda i: (i, 0)), pl.BlockSpec((2, HG_W), lambda i: (0, 0)),
                  pl.BlockSpec((1, DK), lambda i: (0, 0))],
        out_specs=[pl.BlockSpec((RB, HG_W), lambda i: (i, 0)), pl.BlockSpec((RB, HG_W), lambda i: (i, 0)),
                   pl.BlockSpec((cpb * HG_HEADS * DK, DK), lambda i: (i, 0))],
        out_shape=[jax.ShapeDtypeStruct((T, HG_W), F32), jax.ShapeDtypeStruct((T, HG_W), BF16),
                   jax.ShapeDtypeStruct((T // C * HG_HEADS * DK, DK), F32)],
        scratch_shapes=[pltpu.VMEM((HG_HEADS, DK, DK), F32)],
        compiler_params=_params(dimension_semantics=("arbitrary",)),
    )(z, lb_logits, gain)


def _hgrn_bwd(z, o_raw, dy, states, lb_logits, gain, dz_buf):
    T = z.shape[0]
    RB = min(ROW_BLOCK, T)
    nb, cpb = T // RB, RB // HG_CHUNK
    C, DK, S = HG_CHUNK, HG_DK, HG_SUB

    def body(z_ref, o_ref, dy_ref, st_ref, lg_ref, g_ref, _buf_ref, dz_ref, dlg_ref, dg_ref, ds_ref, dlb_ref):
        step = pl.program_id(0)

        @pl.when(step == 0)
        def _():
            ds_ref[...] = jnp.zeros_like(ds_ref)
            dlb_ref[...] = jnp.zeros_like(dlb_ref)
            dg_ref[...] = jnp.zeros_like(dg_ref)

        lg = lg_ref[...]
        lb_all = 1.0 / (1.0 + jnp.exp(lg[1:2] - lg[0:1]))
        gain_v = g_ref[...]
        tri, sub = _hg_consts()
        tri_b, sub_b = tri.astype(BF16), sub.astype(BF16)
        row = _iota((C, DK), 0)

        def chunk(cj, carry):
            ci = cpb - 1 - cj
            r0 = pl.multiple_of(ci * C, C)
            rows = pl.ds(r0, C)
            for h in range(HG_HEADS):
                cs = slice(h * DK, (h + 1) * DK)
                q = z_ref[rows, cs]
                f = z_ref[rows, HG_W + h * DK:HG_W + (h + 1) * DK]
                v = z_ref[rows, 2 * HG_W + h * DK:2 * HG_W + (h + 1) * DK]
                g = z_ref[rows, 3 * HG_W + h * DK:3 * HG_W + (h + 1) * DK]
                lb = lb_all[:, cs]
                sgq, qt, sg, fg, kf, b, w = _hg_chunk_fwd(q, f, lb, tri_b, sub_b)
                st = st_ref[pl.ds(pl.multiple_of((ci * HG_HEADS + h) * DK, DK), DK), :]
                dst = ds_ref[h]
                o = o_ref[rows, cs]
                dyv = dy_ref[rows, cs]
                rstd = lax.rsqrt(jnp.mean(o * o, axis=-1, keepdims=True) + EPS)
                n = o * rstd
                sgg = _sigmoid(g)
                t1 = dyv * (g * sgg)
                dg_ref[...] += jnp.sum(t1 * n, axis=0, keepdims=True)
                dn = t1 * gain_v
                do = rstd * (dn - n * jnp.mean(dn * n, axis=-1, keepdims=True))
                dgate = dyv * n * gain_v * sgg * (1.0 + g * (1.0 - sgg))
                do_b = do.astype(BF16)
                v_b = v.astype(BF16)
                ew = jnp.exp(w)
                qs_b = (qt * ew).astype(BF16)
                a, ks = _hg_scores(qs_b, kf, b, row)
                a = jnp.where(tri, a, 0.0)
                eb = jnp.exp(b)
                qd = qt * eb
                bl = b[C - 1:C]
                ebl = jnp.exp(bl)
                ekd = jnp.exp(bl - b)
                kd = kf * ekd
                kd_b = kd.astype(BF16)
                qd_b = qd.astype(BF16)
                dst_b = dst.astype(BF16)
                dqd = _dot(do_b, st.astype(BF16), 1, 0)
                da = jnp.where(tri, _dot(do_b, v_b, 1, 1), 0.0)
                dv = _dot(a.astype(BF16), do_b, 0, 0) + _dot(kd_b, dst_b, 1, 1)
                dkd = _dot(v_b, dst_b, 1, 0)
                ds_ref[h] = dst * ebl + _dot(do_b, qd_b, 0, 0)
                dkd_kd = dkd * kd_b.astype(F32)
                dbl = ebl * jnp.sum(dst * st, axis=0, keepdims=True) + jnp.sum(dkd_kd, axis=0, keepdims=True)
                da_b = da.astype(BF16)
                dqs_parts = []
                dk_in = jnp.zeros((C, DK), F32)
                db_k = jnp.zeros((C, DK), F32)
                for blk in range(C // S):
                    e, k_b = ks[blk]
                    da_blk = da_b[blk * S:(blk + 1) * S]
                    dqs_parts.append(_dot(da_blk, k_b, 1, 0))
                    dks = _dot(da_blk, qs_b[blk * S:(blk + 1) * S], 0, 0)
                    dk_in = dk_in + dks * e
                    db_k = db_k + dks * k_b.astype(F32)
                dqs = jnp.concatenate(dqs_parts, axis=0)
                dqt_in = dqs * ew
                db = qs_b.astype(F32) * dqs - db_k + dqd * qd_b.astype(F32) - dkd_kd
                db = db + jnp.where(row == C - 1, dbl, 0.0)
                dlogf = _split_dot(tri_b, db, 0, 0)
                dqt = dqt_in + dqd * eb
                dkf = dk_in + dkd * ekd
                dfg = dlogf / fg - dkf
                dlb_ref[:, cs] += jnp.sum(dfg * (1.0 - sg), axis=0, keepdims=True)
                dz_ref[rows, cs] = (dqt * sgq * (1.0 + q * (1.0 - sgq))).astype(BF16)
                dz_ref[rows, HG_W + h * DK:HG_W + (h + 1) * DK] = (dfg * (1.0 - lb) * sg * (1.0 - sg)).astype(BF16)
                dz_ref[rows, 2 * HG_W + h * DK:2 * HG_W + (h + 1) * DK] = dv.astype(BF16)
                dz_ref[rows, 3 * HG_W + h * DK:3 * HG_W + (h + 1) * DK] = dgate.astype(BF16)
            return carry

        lax.fori_loop(0, cpb, chunk, 0)

        @pl.when(step == nb - 1)
        def _():
            d0 = dlb_ref[...] * lb_all * (1.0 - lb_all)
            dlg_ref[0:1, :] = d0
            dlg_ref[1:2, :] = -d0

    rev = lambda i: (nb - 1 - i, 0)
    fix = lambda i: (0, 0)
    return _pallas(
        body, name="hgrn_bwd", grid=(nb,),
        in_specs=[pl.BlockSpec((RB, HG_COLS), rev), pl.BlockSpec((RB, HG_W), rev), pl.BlockSpec((RB, HG_W), rev),
                  pl.BlockSpec((cpb * HG_HEADS * DK, DK), rev), pl.BlockSpec((2, HG_W), fix),
                  pl.BlockSpec((1, DK), fix), pl.BlockSpec(memory_space=pl.ANY)],
        out_specs=[pl.BlockSpec((RB, HG_COLS), rev), pl.BlockSpec((2, HG_W), fix), pl.BlockSpec((1, DK), fix)],
        out_shape=[jax.ShapeDtypeStruct(dz_buf.shape, BF16), jax.ShapeDtypeStruct((2, HG_W), F32),
                   jax.ShapeDtypeStruct((1, DK), F32)],
        scratch_shapes=[pltpu.VMEM((HG_HEADS, DK, DK), F32), pltpu.VMEM((1, HG_W), F32)],
        input_output_aliases={6: 0},
        compiler_params=_params(dimension_semantics=("arbitrary",)),
    )(z, o_raw, dy, states, lb_logits, gain, dz_buf)


def _head_ones():
    r, c = _iota((FOX_W, FOX_W), 0), _iota((FOX_W, FOX_W), 1)
    return ((r // FOX_DH) == (c // FOX_DH)).astype(BF16)


def _log_sigmoid(x):
    return jnp.minimum(x, 0.0) - jnp.log(1.0 + jnp.exp(-jnp.abs(x)))


def _fox_prep(z, bias, qg, kg):
    T = z.shape[0]
    tm = min(ROW_BLOCK, T)
    nb = T // tm

    def body(z_ref, b_ref, qg_ref, kg_ref, q_ref, k_ref, v_ref, qa_ref, ka_ref, carry_ref):
        @pl.when(pl.program_id(0) == 0)
        def _():
            carry_ref[...] = jnp.zeros_like(carry_ref)

        ones = _head_ones()
        normed = []
        for src, g_ref in ((0, qg_ref), (1, kg_ref)):
            xv = z_ref[:, src * FOX_W:(src + 1) * FOX_W]
            ms = _split_dot(ones, xv * xv, 1, 0, mat_first=False) * (1.0 / FOX_DH)
            normed.append(xv * lax.rsqrt(ms + EPS) * g_ref[...])
        qn, kn = normed
        q_ref[...] = (qn * FOX_DH ** -0.5).astype(BF16)
        k_b = kn.astype(BF16)
        k_ref[...] = k_b
        v_ref[...] = z_ref[:, 2 * FOX_W:3 * FOX_W].astype(BF16)
        logf = _log_sigmoid(z_ref[:, 3 * FOX_W:FOX_COLS] + b_ref[...])
        r, c = _iota((tm, tm), 0), _iota((tm, tm), 1)
        tri_b = (c <= r).astype(BF16)
        cum = _split_dot(tri_b, logf, 1, 0, terms=3) + carry_ref[...]
        carry_ref[...] = cum[tm - 1:tm]
        c2 = cum * LOG2E
        hi = c2.astype(BF16)
        rem = c2 - hi.astype(F32)
        mid = rem.astype(BF16)
        lo = (rem - mid.astype(F32)).astype(BF16)
        hrow, col = _iota((LANES, 2 * FOX_W), 0), _iota((LANES, 2 * FOX_W), 1)
        base = hrow * LANES + jnp.where(hrow % 2 == 0, FOX_DH, 0)
        placed = None
        for t, part in enumerate((hi, mid, lo)):
            place = jnp.logical_and(col == base + t, hrow < FOX_HEADS).astype(BF16)
            term = _dot(part, place, 1, 0)
            placed = term if placed is None else placed + term
        colw = _iota((tm, 2 * FOX_W), 1)
        head, lane = colw // LANES, colw % LANES
        own = (lane < FOX_DH) == (head % 2 == 0)
        other = jnp.where(head % 2 == 0, lane - FOX_DH, lane)
        ones_q = jnp.where(jnp.logical_and(other >= 0, other < 3), -1.0, 0.0)
        q2 = (qn * (FOX_DH ** -0.5 * LOG2E)).astype(BF16)
        q_exp = jnp.concatenate([q2[:, (h // 2) * LANES:(h // 2 + 1) * LANES] for h in range(FOX_HEADS)], axis=1)
        k_exp = jnp.concatenate([k_b[:, (h // 2) * LANES:(h // 2 + 1) * LANES] for h in range(FOX_HEADS)], axis=1)
        qa_ref[...] = jnp.where(own, q_exp, ones_q.astype(BF16))
        ka_ref[...] = jnp.where(own, k_exp, placed.astype(BF16))

    return _pallas(
        body, name="fox_prep", grid=(nb,),
        in_specs=[pl.BlockSpec((tm, SEG), lambda i: (i, 1)), pl.BlockSpec((1, LANES), lambda i: (0, 0)),
                  pl.BlockSpec((1, FOX_W), lambda i: (0, 0)), pl.BlockSpec((1, FOX_W), lambda i: (0, 0))],
        out_specs=[pl.BlockSpec((tm, FOX_W), lambda i: (i, 0))] * 3 + [pl.BlockSpec((tm, 2 * FOX_W), lambda i: (i, 0))] * 2,
        out_shape=[jax.ShapeDtypeStruct((T, FOX_W), BF16)] * 3 + [jax.ShapeDtypeStruct((T, 2 * FOX_W), BF16)] * 2,
        scratch_shapes=[pltpu.VMEM((1, LANES), F32)],
        compiler_params=_params(dimension_semantics=("arbitrary",)),
    )(z, bias, qg, kg)


def _fox_fwd(qa, ka, vb, blocks):
    T = qa.shape[0]
    tq = min(ROW_BLOCK, T)
    nq = T // tq
    NEG = -1e30
    n = len(blocks)
    n_pairs = FOX_HEADS // 2

    def body(*refs):
        q_ref, k_ref, v_ref = refs[:3]
        o_ref, lse_ref = refs[3 + n:5 + n]
        m_sc, l_sc, acc_sc = refs[5 + 2 * n:8 + 2 * n]
        pr, qi, ki = pl.program_id(0), pl.program_id(1), pl.program_id(2)
        g_start, g_forward, g_finish = _gather_steps(refs[3:3 + n], refs[5 + n:5 + 2 * n], *refs[8 + 2 * n:])
        row_start = jnp.logical_and(qi == 0, ki == 0)

        @pl.when(jnp.logical_and(pr == 0, row_start))
        def _():
            g_start()

        @pl.when(jnp.logical_and(pr == n_pairs // 2, row_start))
        def _():
            g_forward()

        @pl.when(ki == 0)
        def _():
            m_sc[...] = jnp.full_like(m_sc, NEG)
            l_sc[...] = jnp.zeros_like(l_sc)
            acc_sc[...] = jnp.zeros_like(acc_sc)

        lane = _iota((tq, LANES), 1)

        def block(masked):
            vv = v_ref[...]
            for hh in range(2):
                hs = slice(hh * LANES, (hh + 1) * LANES)
                s = _dot(q_ref[:, hs], k_ref[:, hs], 1, 1)
                tiles = [s[:, j * LANES:(j + 1) * LANES] for j in range(tq // LANES)]
                if masked:
                    row, col = _iota((tq, LANES), 0), _iota((tq, LANES), 1)
                    tiles = [jnp.where(row >= col + j * LANES, t, NEG) for j, t in enumerate(tiles)]
                tmax = functools.reduce(jnp.maximum, tiles)
                m_old = m_sc[hh]
                m_new = jnp.maximum(m_old, jnp.max(tmax, axis=-1, keepdims=True))
                m_b = jnp.broadcast_to(m_new, (tq, LANES))
                alpha_b = jnp.broadcast_to(jnp.exp2(m_old - m_new), (tq, LANES))
                ps = [jnp.exp2(t - m_b) for t in tiles]
                l_sc[hh] = alpha_b * l_sc[hh] + functools.reduce(jnp.add, ps)
                p_b = jnp.concatenate([p.astype(BF16) for p in ps], axis=1)
                acc_sc[hh] = alpha_b * acc_sc[hh] + _dot(p_b, vv, 1, 0)
                m_sc[hh] = m_new

        @pl.when(ki < qi)
        def _():
            block(False)

        @pl.when(ki == qi)
        def _():
            block(True)
            l0 = jnp.sum(l_sc[0], axis=-1, keepdims=True)
            l1 = jnp.sum(l_sc[1], axis=-1, keepdims=True)
            o_ref[...] = jnp.where(lane < FOX_DH, acc_sc[0] * (1.0 / l0), acc_sc[1] * (1.0 / l1))
            lse_ref[:, :LANES] = jnp.broadcast_to(m_sc[0] + jnp.log2(l0), (tq, LANES))
            lse_ref[:, LANES:] = jnp.broadcast_to(m_sc[1] + jnp.log2(l1), (tq, LANES))

        @pl.when(jnp.logical_and(pr == n_pairs - 1, jnp.logical_and(qi == nq - 1, ki == nq - 1)))
        def _():
            g_finish()

    kmap = lambda p, i, j: (jnp.minimum(i, j), p)
    hbm = pl.BlockSpec(memory_space=pl.ANY)
    res = _pallas(
        body, name="fox_fwd", grid=(n_pairs, nq, nq),
        in_specs=[pl.BlockSpec((tq, 2 * LANES), lambda p, i, j: (i, p)), pl.BlockSpec((tq, 2 * LANES), kmap),
                  pl.BlockSpec((tq, LANES), kmap)] + [hbm] * n,
        out_specs=[pl.BlockSpec((tq, LANES), lambda p, i, j: (i, p)),
                   pl.BlockSpec((tq, 2 * LANES), lambda p, i, j: (i, p))] + [hbm] * n,
        out_shape=[jax.ShapeDtypeStruct((T, FOX_W), F32), jax.ShapeDtypeStruct((T, 2 * FOX_W), F32)]
        + [jax.ShapeDtypeStruct((N_DEV,) + b.shape, b.dtype) for b in blocks],
        scratch_shapes=[pltpu.VMEM((2, tq, 1), F32), pltpu.VMEM((2, tq, LANES), F32), pltpu.VMEM((2, tq, LANES), F32)]
        + _gather_scratch(n),
        compiler_params=_params(dimension_semantics=("arbitrary",) * 3),
    )(qa, ka, vb, *blocks)
    return res[0], res[1], res[2:]


def _fox_bwd(qs, kn, vb, qa, ka, o, do, lse, hs):
    T = qs.shape[0]
    tq = min(ROW_BLOCK, T)
    nq = T // tq
    n = len(hs)
    n_pairs = FOX_HEADS // 2

    def body(*refs):
        q_ref, k_ref, v_ref, qa_ref, ka_ref, o_ref, do_ref, lse_ref = refs[:8]
        dq_ref, dk_ref, dv_ref, dcs_ref, drs_ref = refs[8 + n:13 + n]
        pr, ki, qi = pl.program_id(0), pl.program_id(1), pl.program_id(2)
        x_start, x_finish = _chip_exchange_steps(refs[8:8 + n], refs[13 + n:13 + 2 * n], *refs[13 + 2 * n:])

        @pl.when(jnp.logical_and(pr == 0, jnp.logical_and(ki == 0, qi == 0)))
        def _():
            x_start()

        @pl.when(jnp.logical_and(ki == 0, qi == 0))
        def _():
            dq_ref[...] = jnp.zeros_like(dq_ref)

        @pl.when(jnp.logical_and(pr == 0, jnp.logical_and(ki == 0, qi == 0)))
        def _():
            drs_ref[...] = jnp.zeros_like(drs_ref)

        @pl.when(qi == ki)
        def _():
            dk_ref[...] = jnp.zeros_like(dk_ref)
            dv_ref[...] = jnp.zeros_like(dv_ref)
            dcs_ref[...] = jnp.zeros_like(dcs_ref)

        def block(masked):
            lane = _iota((tq, LANES), 1)
            qv, kv, vv = q_ref[...], k_ref[...], v_ref[...]
            ov, dov = o_ref[...], do_ref[...]
            qrows = pl.ds(pl.multiple_of(qi * tq, tq), tq)
            dq_acc = jnp.zeros((tq, LANES), F32)
            dk_acc = jnp.zeros((tq, LANES), F32)
            dv_acc = jnp.zeros((tq, LANES), F32)
            dcs_acc = jnp.zeros((8, tq), F32)
            drs_acc = jnp.zeros((tq, LANES), F32)
            prod = dov * ov
            nt = tq // LANES
            for hh in range(2):
                in_head = (lane < FOX_DH) if hh == 0 else (lane >= FOX_DH)
                hs_ = slice(hh * LANES, (hh + 1) * LANES)
                zb = jnp.zeros_like(qv)
                qm = jnp.where(in_head, qv, zb)
                km = jnp.where(in_head, kv, zb)
                dom = jnp.where(in_head, dov, 0.0).astype(BF16)
                delta_b = jnp.broadcast_to(jnp.sum(jnp.where(in_head, prod, 0.0), axis=1, keepdims=True), (tq, LANES))
                lse_b = lse_ref[:, hs_]
                s = _dot(qa_ref[:, hs_], ka_ref[:, hs_], 1, 1)
                dp = _dot(dom, vv, 1, 1)
                ps, dss = [], []
                for j in range(nt):
                    js = slice(j * LANES, (j + 1) * LANES)
                    p = jnp.exp2(s[:, js] - lse_b)
                    if masked:
                        p = jnp.where(_iota((tq, LANES), 0) >= _iota((tq, LANES), 1) + j * LANES, p, 0.0)
                    ps.append(p)
                    dss.append(p * (dp[:, js] - delta_b))
                p_b = jnp.concatenate([p.astype(BF16) for p in ps], axis=1)
                ds_b = jnp.concatenate([d.astype(BF16) for d in dss], axis=1)
                dv_acc = dv_acc + _dot(p_b, dom, 0, 0)
                dq_acc = dq_acc + _dot(ds_b, km, 1, 0)
                dk_acc = dk_acc + _dot(ds_b, qm, 0, 0)
                colsum = jnp.concatenate([jnp.sum(d, axis=0, keepdims=True) for d in dss], axis=1)
                dcs_acc = dcs_acc + jnp.where(_iota((8, tq), 0) == hh, colsum, 0.0)
                rowsum = jnp.sum(functools.reduce(jnp.add, dss), axis=1, keepdims=True)
                drs_acc = drs_acc + jnp.where(lane == 2 * pr + hh, rowsum, 0.0)
            drs_ref[qrows, :] += drs_acc
            dq_ref[qrows, :] += dq_acc
            dk_ref[...] += dk_acc
            dv_ref[...] += dv_acc
            dcs_ref[0] += dcs_acc

        @pl.when(qi > ki)
        def _():
            block(False)

        @pl.when(qi == ki)
        def _():
            block(True)

        @pl.when(jnp.logical_and(pr == n_pairs - 1, jnp.logical_and(ki == nq - 1, qi == nq - 1)))
        def _():
            x_finish()

    qmap = lambda p, j, i: (jnp.maximum(i, j), p)
    hbm = pl.BlockSpec(memory_space=pl.ANY)
    res = _pallas(
        body, name="fox_bwd", grid=(n_pairs, nq, nq),
        in_specs=[pl.BlockSpec((tq, LANES), qmap), pl.BlockSpec((tq, LANES), lambda p, j, i: (j, p)),
                  pl.BlockSpec((tq, LANES), lambda p, j, i: (j, p)),
                  pl.BlockSpec((tq, 2 * LANES), qmap), pl.BlockSpec((tq, 2 * LANES), lambda p, j, i: (j, p)),
                  pl.BlockSpec((tq, LANES), qmap), pl.BlockSpec((tq, LANES), qmap), pl.BlockSpec((tq, 2 * LANES), qmap)]
        + [hbm] * n,
        out_specs=[pl.BlockSpec((T, LANES), lambda p, j, i: (0, p)), pl.BlockSpec((tq, LANES), lambda p, j, i: (j, p)),
                   pl.BlockSpec((tq, LANES), lambda p, j, i: (j, p)), pl.BlockSpec((1, 8, tq), lambda p, j, i: (p, 0, j)),
                   pl.BlockSpec((T, LANES), lambda p, j, i: (0, 0))] + [hbm] * n,
        out_shape=[jax.ShapeDtypeStruct((T, FOX_W), F32)] * 3
        + [jax.ShapeDtypeStruct((n_pairs, 8, T), F32), jax.ShapeDtypeStruct((T, LANES), F32)]
        + _chip_exchange_shapes(hs),
        scratch_shapes=_chip_exchange_scratch(n),
        compiler_params=_params(dimension_semantics=("arbitrary",) * 3),
    )(qs, kn, vb, qa, ka, o, do, lse, *hs)
    res = list(res)
    return res[:5] + [res[5:]]


def _fox_post(z, dq, dk, dv, dcs, drs, bias, qg, kg, dz_buf):
    T = z.shape[0]
    tm = min(ROW_BLOCK, T)
    nb = T // tm

    def body(z_ref, dq_ref, dk_ref, dv_ref, dcs_ref, drs_ref, b_ref, qg_ref, kg_ref, _buf_ref, dz_ref, dqg_ref, dkg_ref,
             db_ref, carry_ref):
        @pl.when(pl.program_id(0) == 0)
        def _():
            carry_ref[...] = jnp.zeros_like(carry_ref)
            dqg_ref[...] = jnp.zeros_like(dqg_ref)
            dkg_ref[...] = jnp.zeros_like(dkg_ref)
            db_ref[...] = jnp.zeros_like(db_ref)

        ones = _head_ones()
        for src, g_ref, d_ref, dg_ref, scale in ((0, qg_ref, dq_ref, dqg_ref, FOX_DH ** -0.5), (1, kg_ref, dk_ref, dkg_ref, 1.0)):
            xv = z_ref[:, src * FOX_W:(src + 1) * FOX_W]
            ms = _split_dot(ones, xv * xv, 1, 0, mat_first=False) * (1.0 / FOX_DH)
            rstd = lax.rsqrt(ms + EPS)
            xh = xv * rstd
            dn = d_ref[...] * scale
            dg_ref[...] += jnp.sum(dn * xh, axis=0, keepdims=True)
            dxh = dn * g_ref[...]
            mean = _split_dot(ones, dxh * xh, 1, 0, mat_first=False) * (1.0 / FOX_DH)
            dz_ref[:, src * FOX_W:(src + 1) * FOX_W] = (rstd * (dxh - xh * mean)).astype(BF16)
        dz_ref[:, 2 * FOX_W:3 * FOX_W] = dv_ref[...].astype(BF16)
        row8 = _iota((8, tm), 0)
        dct = jnp.zeros((8, tm), F32)
        for h in range(FOX_HEADS):
            src_row = dcs_ref[h // 2][h % 2:h % 2 + 1, :]
            dct = dct + jnp.where(row8 == h, src_row, 0.0)
        dct = drs_ref[...].T[0:8] - dct
        r, c = _iota((tm, tm), 0), _iota((tm, tm), 1)
        upper_b = (r >= c).astype(BF16)
        rc = _split_dot(upper_b, dct, 1, 0, mat_first=False) + carry_ref[...]
        carry_ref[...] = rc[:, 0:1]
        full = jnp.concatenate([rc, jnp.zeros((LANES - 8, tm), F32)], axis=0)
        dlogf = full.T
        xf = z_ref[:, 3 * FOX_W:FOX_COLS] + b_ref[...]
        df = dlogf * (1.0 - _sigmoid(xf))
        dz_ref[:, 3 * FOX_W:FOX_COLS] = df.astype(BF16)
        dz_ref[:, FOX_COLS:] = jnp.zeros((tm, SEG - FOX_COLS), BF16)
        db_ref[...] += jnp.sum(df, axis=0, keepdims=True)

    rev = lambda i: (nb - 1 - i, 0)
    fix2 = lambda i: (0, 0)
    return _pallas(
        body, name="fox_post", grid=(nb,),
        in_specs=[pl.BlockSpec((tm, SEG), lambda i: (nb - 1 - i, 1)), pl.BlockSpec((tm, FOX_W), rev),
                  pl.BlockSpec((tm, FOX_W), rev),
                  pl.BlockSpec((tm, FOX_W), rev), pl.BlockSpec((FOX_HEADS // 2, 8, tm), lambda i: (0, 0, nb - 1 - i)),
                  pl.BlockSpec((tm, LANES), rev),
                  pl.BlockSpec((1, LANES), fix2), pl.BlockSpec((1, FOX_W), fix2), pl.BlockSpec((1, FOX_W), fix2),
                  pl.BlockSpec(memory_space=pl.ANY)],
        out_specs=[pl.BlockSpec((tm, SEG), lambda i: (nb - 1 - i, 1)), pl.BlockSpec((1, FOX_W), fix2),
                   pl.BlockSpec((1, FOX_W), fix2), pl.BlockSpec((1, LANES), fix2)],
        out_shape=[jax.ShapeDtypeStruct(dz_buf.shape, BF16), jax.ShapeDtypeStruct((1, FOX_W), F32),
                   jax.ShapeDtypeStruct((1, FOX_W), F32), jax.ShapeDtypeStruct((1, LANES), F32)],
        scratch_shapes=[pltpu.VMEM((8, 1), F32)],
        input_output_aliases={9: 0},
        compiler_params=_params(dimension_semantics=("arbitrary",)),
    )(z, dq, dk, dv, dcs, drs, bias, qg, kg, dz_buf)


def _local_step(x, p, tgt, sm, W, rest_chunks, core):
    lbl, og, fb = sm["hg_lb_logits"], sm["hg_onorm_g"], sm["fox_f_bias"]
    fbias = jnp.pad(fb, ((0, 0), (0, LANES - FOX_HEADS)))
    qg = jnp.tile(sm["fox_q_norm_g"], (1, FOX_HEADS))
    kg = jnp.tile(sm["fox_k_norm_g"], (1, FOX_HEADS))

    h = _rms_fwd(x, sm["norm_mix_g"], name="rms_mix")
    z = _matmul(h, W["w_in"], name="mm_z")
    o_raw, ya, states = _hgrn_fwd(z, lbl, og)
    qs, kn, vb, qa, ka = _fox_prep(z, fbias, qg, kg)
    yb, lse, gathered = _fox_fwd(qa, ka, vb, rest_chunks)
    W = dict(W, **{n: _full_of_chunks(n, g) for n, g in zip(BIG[1:], gathered)})
    ua = _matmul(ya, W["w_branch_a"], name="mm_ua")
    ub = _matmul(yb, W["w_branch_b"], name="mm_ub")
    merged = _merge_fwd(ua, ub, z)
    x1 = _matmul(merged, W["w_out"], add=x, name="mm_x1")
    hf = _rms_fwd(x1, sm["norm_ffn_g"], name="rms_ffn")
    a = _matmul(hf, W["w_ffn_gate"], out_dtype=BF16, name="mm_ffn_a")
    b = _matmul(hf, W["w_ffn_up"], out_dtype=BF16, name="mm_ffn_b")
    act = _swiglu_fwd(a, b)
    x2 = _matmul(act, W["w_ffn_down"], add=x1, name="mm_x2")
    hp = _rms_fwd(x2, sm["norm_ple_g"], name="rms_ple")
    sp = _matmul(hp, W["w_ple_gate"], name="mm_sp")
    pp = _matmul(p, W["w_ple_proj"], name="mm_pp")
    dy, dsp, dpp, loss = _ple_loss(x2, sp, pp, tgt)

    G = {}
    G["w_ple_proj"] = _matmul(p, dpp, ta=True, out_dtype=BF16, name="mm_dw_ple_proj")
    G["w_ple_gate"] = _matmul(hp, dsp, ta=True, out_dtype=BF16, name="mm_dw_ple_gate")
    dhp = _matmul(dsp, W["w_ple_gate"], tb=True, name="mm_dhp")
    dx2, d_ple_g = _rms_bwd(x2, sm["norm_ple_g"], dhp, dy, name="rms_ple_bwd")
    dact = _matmul(dx2, W["w_ffn_down"], tb=True, name="mm_dact")
    G["w_ffn_down"] = _matmul(act, dx2, ta=True, out_dtype=BF16, name="mm_dw_ffn_down")
    da, db = _swiglu_bwd(a, b, dact)
    G["w_ffn_gate"] = _matmul(hf, da, ta=True, out_dtype=BF16, name="mm_dw_ffn_gate")
    G["w_ffn_up"] = _matmul(hf, db, ta=True, out_dtype=BF16, name="mm_dw_ffn_up")
    dhf = _matmul(da, W["w_ffn_gate"], tb=True, name="mm_dhf_a")
    dhf = _matmul(db, W["w_ffn_up"], tb=True, add=dhf, name="mm_dhf_b")
    dx1, d_ffn_g = _rms_bwd(x1, sm["norm_ffn_g"], dhf, dx2, name="rms_ffn_bwd")
    dmerged = _matmul(dx1, W["w_out"], tb=True, name="mm_dmerged")
    G["w_out"] = _matmul(merged, dx1, ta=True, out_dtype=BF16, name="mm_dw_out")
    dua, dub, dz = _merge_bwd(dmerged, ua, ub, z)
    G["w_branch_a"] = _matmul(ya, dua, ta=True, out_dtype=BF16, name="mm_dw_branch_a")
    G["w_branch_b"] = _matmul(yb, dub, ta=True, out_dtype=BF16, name="mm_dw_branch_b")
    dya = _matmul(dua, W["w_branch_a"], tb=True, name="mm_dya")
    dyb = _matmul(dub, W["w_branch_b"], tb=True, name="mm_dyb")
    hb_rest = _sibling_sums({n: G[n] for n in BIG[1:]}, core, tag="rest")
    dq, dk, dv, dcs, drs, got_rest = _fox_bwd(qs, kn, vb, qa, ka, yb, dyb, lse, hb_rest)
    dz, d_qg, d_kg, d_fb = _fox_post(z, dq, dk, dv, dcs, drs, fbias, qg, kg, dz)
    dz, d_lbl, d_og = _hgrn_bwd(z, o_raw, dya, states, lbl, og, dz)
    dh = _matmul(dz, W["w_in"], tb=True, name="mm_dh")
    G["w_in"] = _matmul(h, dz, ta=True, out_dtype=BF16, name="mm_dw_in")
    grad_x, d_mix_g = _rms_bwd(x, sm["norm_mix_g"], dh, dx1, name="rms_mix_bwd")

    gs = {"norm_mix_g": d_mix_g, "hg_lb_logits": d_lbl, "hg_onorm_g": d_og, "fox_f_bias": d_fb[:, :FOX_HEADS],
          "fox_q_norm_g": d_qg.reshape(FOX_HEADS, FOX_DH).sum(0, keepdims=True),
          "fox_k_norm_g": d_kg.reshape(FOX_HEADS, FOX_DH).sum(0, keepdims=True),
          "norm_ffn_g": d_ffn_g, "norm_ple_g": d_ple_g}
    return loss, grad_x, gs, G["w_in"], hb_rest, got_rest


def _pack_rows(parts, total):
    buf = jnp.concatenate(parts, axis=-2)
    pad = total - buf.shape[-2]
    widths = [(0, 0)] * (buf.ndim - 2) + [(0, pad), (0, 0)]
    return jnp.pad(buf, widths)


def _chunk_of_shard(n, w):
    if n == "w_in":
        return jnp.pad(w, ((0, 0), (0, IN_SHARD_PAD - IN_SHARD)))
    if n in ("w_ffn_gate", "w_ffn_up"):
        return jnp.pad(w, ((0, 0), (0, FF_SHARD_PAD - FF_SHARD)))
    if n == "w_ffn_down":
        return jnp.pad(w, ((0, FF_SHARD_PAD - FF_SHARD), (0, 0)))
    return w


def _full_of_chunks(n, g):
    _, a, b = g.shape
    if BIG_SHAPE[n][2] == 0:
        return g.reshape(N_DEV * a, b)
    if n == "w_in":
        w = g[:, :, :IN_SHARD].transpose(1, 0, 2).reshape(a, IN_COLS)
        gap = jnp.zeros((a, SEG - FOX_LOGICAL), g.dtype)
        return jnp.concatenate([w[:, :HG_COLS], w[:, HG_COLS:HG_COLS + FOX_LOGICAL], gap, w[:, HG_COLS + FOX_LOGICAL:]],
                               axis=1)
    return g.transpose(1, 0, 2).reshape(a, N_DEV * b)


def _chunks_of_full(n, g):
    if BIG_SHAPE[n][2] == 0:
        return g.reshape(N_DEV, g.shape[0] // N_DEV, g.shape[1])
    if n == "w_in":
        a = g.shape[0]
        w = jnp.concatenate([g[:, :HG_COLS], g[:, SEG:SEG + FOX_LOGICAL], g[:, 2 * SEG:]], axis=1)
        w = w.reshape(a, N_DEV, IN_SHARD).transpose(1, 0, 2)
        return jnp.pad(w, ((0, 0), (0, 0), (0, IN_SHARD_PAD - IN_SHARD)))
    return g.reshape(g.shape[0], N_DEV, g.shape[1] // N_DEV).transpose(1, 0, 2)


def _pack_small(vals, loss_row=None):
    parts = [vals[n].reshape(SMALL_ROWS[n], -1) for n in SMALL]
    parts = [jnp.pad(v, ((0, 0), (0, LANES - v.shape[1]))) for v in parts]
    if loss_row is not None:
        parts.append(loss_row)
    return _pack_rows(parts, SMALL_TOTAL)


def _unpack_small(buf, like):
    out, r0 = {}, 0
    for n in SMALL:
        rows, size = SMALL_ROWS[n], like[n].size
        blk = buf[r0:r0 + rows]
        out[n] = (blk if size == rows * LANES else blk[:, :size]).reshape(like[n].shape)
        r0 += rows
    return out


def _place():
    return lax.axis_index("x"), lax.axis_index("y"), lax.axis_index("c")


def _gather_steps(x_refs, out_refs, send_sems, recv_sems, local_sems):
    n = len(x_refs)
    x, y, c = _place()
    me, sibling = (x, y, c), (x, y, 1 - c)
    chips = [(1 - x, y), (x, 1 - y), (1 - x, 1 - y)]

    def slot(i, px, py, pc):
        return out_refs[i].at[4 * px + 2 * py + pc]

    def copy(k, i, blk, to, own=False):
        return pltpu.make_async_remote_copy(
            src_ref=x_refs[i] if own else slot(i, *blk), dst_ref=slot(i, *blk),
            send_sem=send_sems.at[k, i], recv_sem=recv_sems.at[k, i], device_id=to, device_id_type=MESH)

    def mine():
        return [pltpu.make_async_copy(x_refs[i], slot(i, *me), local_sems.at[i]) for i in range(n)]

    def first():
        cps = [copy(0, i, me, sibling, own=True) for i in range(n)]
        return cps + [copy(1 + j, i, me, (*chip, c), own=True) for j, chip in enumerate(chips) for i in range(n)]

    def passed():
        return [copy(4 + j, i, (*chip, c), sibling) for j, chip in enumerate(chips) for i in range(n)]

    def start():
        for cp in mine() + first():
            cp.start()

    def forward():
        fws = passed()
        for j, chip in enumerate(chips):
            for i in range(n):
                copy(1 + j, i, (*chip, c), me).wait_recv()
                fws[j * n + i].start()

    def finish():
        for i in range(n):
            copy(0, i, sibling, me).wait_recv()
        for j, chip in enumerate(chips):
            for i in range(n):
                copy(4 + j, i, (*chip, 1 - c), me).wait_recv()
        for cp in first() + passed():
            cp.wait_send()
        for cp in mine():
            cp.wait()

    return start, forward, finish


def _gather_scratch(n):
    return [pltpu.SemaphoreType.DMA((7, n)), pltpu.SemaphoreType.DMA((7, n)), pltpu.SemaphoreType.DMA((n,))]


def _all_gather(blocks, *, name):
    n = len(blocks)

    def body(*refs):
        for step in _gather_steps(refs[:n], refs[n:2 * n], *refs[2 * n:]):
            step()

    hbm = pl.BlockSpec(memory_space=pl.ANY)
    return _pallas(
        body, name=name, out_shape=[jax.ShapeDtypeStruct((N_DEV,) + b.shape, b.dtype) for b in blocks],
        in_specs=[hbm] * n, out_specs=[hbm] * n, scratch_shapes=_gather_scratch(n),
    )(*blocks)


def _sibling_exchange(gs, *, name):
    n = len(gs)

    def body(*refs):
        g_refs, out_refs = refs[:n], refs[n:2 * n]
        send_sems, recv_sems = refs[2 * n:]
        x, y, c = _place()
        cps = [pltpu.make_async_remote_copy(
            src_ref=g_refs[i].at[:, pl.ds(1 - c, 1)], dst_ref=out_refs[i], send_sem=send_sems.at[i],
            recv_sem=recv_sems.at[i], device_id=(x, y, 1 - c), device_id_type=MESH) for i in range(n)]
        for cp in cps:
            cp.start()
        for cp in cps:
            cp.wait()

    hbm = pl.BlockSpec(memory_space=pl.ANY)
    return _pallas(
        body, name=name, out_shape=[jax.ShapeDtypeStruct((N_CHIP, 1) + g.shape[2:], g.dtype) for g in gs],
        in_specs=[hbm] * n, out_specs=[hbm] * n,
        scratch_shapes=[pltpu.SemaphoreType.DMA((n,)), pltpu.SemaphoreType.DMA((n,))],
    )(*gs)


def _chip_sum(g4, got, core, *, name):
    _, _, a, b = g4.shape

    def body(c_ref, g_ref, r_ref, h_ref):
        h_ref[0] = (g_ref[0, 0].astype(F32) + r_ref[0, 0].astype(F32)).astype(BF16)

    grid_spec = pltpu.PrefetchScalarGridSpec(
        num_scalar_prefetch=1, grid=(N_CHIP,),
        in_specs=[pl.BlockSpec((1, 1, a, b), lambda j, c: (j, c[0], 0, 0)),
                  pl.BlockSpec((1, 1, a, b), lambda j, c: (j, 0, 0, 0))],
        out_specs=[pl.BlockSpec((1, a, b), lambda j, c: (j, 0, 0))])
    return _pallas(
        body, name=name, grid_spec=grid_spec, out_shape=[jax.ShapeDtypeStruct((N_CHIP, a, b), BF16)],
        compiler_params=_params(dimension_semantics=("arbitrary",)),
    )(core, g4, got)[0]


def _chip_exchange(hs, *, name):
    n = len(hs)

    def body(*refs):
        for step in _chip_exchange_steps(refs[:n], refs[n:2 * n], *refs[2 * n:]):
            step()

    hbm = pl.BlockSpec(memory_space=pl.ANY)
    return _pallas(
        body, name=name, out_shape=_chip_exchange_shapes(hs), in_specs=[hbm] * n, out_specs=[hbm] * n,
        scratch_shapes=_chip_exchange_scratch(n),
    )(*hs)


def _chip_exchange_steps(h_refs, out_refs, send_sems, recv_sems):
    n = len(h_refs)
    x, y, c = _place()
    chips = [(1 - x, y), (x, 1 - y), (1 - x, 1 - y)]

    def copies():
        return [pltpu.make_async_remote_copy(
            src_ref=h_refs[i].at[2 * px + py], dst_ref=out_refs[i].at[k], send_sem=send_sems.at[k, i],
            recv_sem=recv_sems.at[k, i], device_id=(px, py, c), device_id_type=MESH)
            for k, (px, py) in enumerate(chips) for i in range(n)]

    def start():
        for cp in copies():
            cp.start()

    def finish():
        for cp in copies():
            cp.wait()

    return start, finish


def _chip_exchange_shapes(hs):
    return [jax.ShapeDtypeStruct((3,) + h.shape[1:], h.dtype) for h in hs]


def _chip_exchange_scratch(n):
    return [pltpu.SemaphoreType.DMA((3, n)), pltpu.SemaphoreType.DMA((3, n))]


def _sibling_sums(G, core, *, tag):
    g4 = []
    for n, g in G.items():
        gc = _chunks_of_full(n, g)
        g4.append(gc.reshape((N_CHIP, 2) + gc.shape[1:]))
    got = _sibling_exchange(g4, name="grads_to_sibling_" + tag)
    return [_chip_sum(g, r, core, name="chip_sum_" + n) for n, g, r in zip(G, g4, got)]


def _adam_math(w, g, m, v):
    m = ADAM_B1 * m + (1.0 - ADAM_B1) * g
    v = ADAM_B2 * v + (1.0 - ADAM_B2) * (g * g)
    m_hat = m / (1.0 - ADAM_B1 ** ADAM_STEP)
    v_hat = v / (1.0 - ADAM_B2 ** ADAM_STEP)
    delta = -ADAM_LR * (m_hat / (jnp.sqrt(v_hat) + ADAM_EPS) + ADAM_WD * w)
    return delta, m, v


def _adam_shard(hb, got, chip, w, m, v, *, name):
    _, r, c = w.shape
    _, a, b = hb.shape
    tr = r if r <= 512 else 256
    ta = tr if r // tr > 1 else a

    def body(j_ref, h_ref, r_ref, w_ref, m_ref, v_ref, g_ref, d_ref, nm_ref, nv_ref):
        parts = [h_ref[0], r_ref[0], r_ref[1], r_ref[2]]
        g = None
        for part in parts:
            part = part[:tr, :c].astype(F32)
            g = part if g is None else g + part
        d, nm, nv = _adam_math(w_ref[0], g, m_ref[0], v_ref[0])
        g_ref[0] = g
        d_ref[0] = d
        nm_ref[0] = nm
        nv_ref[0] = nv

    blk = pl.BlockSpec((1, tr, c), lambda i, j: (0, i, 0))
    grid_spec = pltpu.PrefetchScalarGridSpec(
        num_scalar_prefetch=1, grid=(r // tr,),
        in_specs=[pl.BlockSpec((1, ta, b), lambda i, j: (j[0], i, 0)),
                  pl.BlockSpec((3, ta, b), lambda i, j: (0, i, 0)), blk, blk, blk],
        out_specs=[blk] * 4)
    return _pallas(
        body, name=name, grid_spec=grid_spec, out_shape=[jax.ShapeDtypeStruct((1, r, c), F32)] * 4,
        compiler_params=_params(dimension_semantics=("arbitrary",)),
    )(chip, hb, got, w, m, v)


def _small_all_reduce_adam(gs, w, m, v):
    def body(g_ref, w_ref, m_ref, v_ref, sum_ref, d_ref, nm_ref, nv_ref, gather, send_sems, recv_sems):
        x, y, c = _place()
        my = 4 * x + 2 * y + c
        gather[my] = g_ref[...]
        cps = []
        for k in range(1, N_DEV):
            to = (x ^ (k >> 2), y ^ ((k >> 1) & 1), c ^ (k & 1))
            cps.append(pltpu.make_async_remote_copy(
                src_ref=g_ref, dst_ref=gather.at[my], send_sem=send_sems.at[k - 1], recv_sem=recv_sems.at[k - 1],
                device_id=to, device_id_type=MESH))
        for cp in cps:
            cp.start()
        for cp in cps:
            cp.wait()
        total = gather[0]
        for d in range(1, N_DEV):
            total = total + gather[d]
        dlt, nm, nv = _adam_math(w_ref[...], total, m_ref[...], v_ref[...])
        sum_ref[...] = total
        d_ref[...] = dlt
        nm_ref[...] = nm
        nv_ref[...] = nv

    vm = pl.BlockSpec(memory_space=pltpu.VMEM)
    return _pallas(
        body, name="small_all_reduce_adam", out_shape=[jax.ShapeDtypeStruct((SMALL_TOTAL, LANES), F32)] * 4,
        in_specs=[vm] * 4, out_specs=[vm] * 4,
        scratch_shapes=[pltpu.VMEM((N_DEV, SMALL_TOTAL, LANES), F32), pltpu.SemaphoreType.DMA((7,)),
                        pltpu.SemaphoreType.DMA((7,))],
            )(gs, w, m, v)


def kernel(x, p, norm_mix_g, w_in, hg_lb_logits, hg_onorm_g, fox_f_bias, fox_q_norm_g, fox_k_norm_g, w_branch_a, w_branch_b, w_out, norm_ffn_g, w_ffn_gate, w_ffn_up, w_ffn_down, norm_ple_g, w_ple_gate, w_ple_proj, loss_target, m_norm_mix_g, m_w_in, m_hg_lb_logits, m_hg_onorm_g, m_fox_f_bias, m_fox_q_norm_g, m_fox_k_norm_g, m_w_branch_a, m_w_branch_b, m_w_out, m_norm_ffn_g, m_w_ffn_gate, m_w_ffn_up, m_w_ffn_down, m_norm_ple_g, m_w_ple_gate, m_w_ple_proj, v_norm_mix_g, v_w_in, v_hg_lb_logits, v_hg_onorm_g, v_fox_f_bias, v_fox_q_norm_g, v_fox_k_norm_g, v_w_branch_a, v_w_branch_b, v_w_out, v_norm_ffn_g, v_w_ffn_gate, v_w_ffn_up, v_w_ffn_down, v_norm_ple_g, v_w_ple_gate, v_w_ple_proj):
    args = dict(locals())
    wts = {n: args[n] for n in BIG + SMALL}
    mom = {n: args["m_" + n] for n in BIG + SMALL}
    var = {n: args["v_" + n] for n in BIG + SMALL}
    sm = {n: wts[n] for n in SMALL}

    xi, yi, ci = _place()
    core = jnp.reshape(ci, (1,)).astype(jnp.int32)
    chip = jnp.reshape(2 * xi + yi, (1,)).astype(jnp.int32)
    chunks = [_chunk_of_shard(n, wts[n][0].astype(BF16)) for n in BIG]
    assert BIG[0] == "w_in"
    w_in_full = _full_of_chunks("w_in", _all_gather(chunks[:1], name="w_in_all_gather")[0])

    loss_blk, grad_x, gs, g_w_in, hb_rest, got_rest = _local_step(
        x[0], p[0, 0], loss_target[0], sm, {"w_in": w_in_full}, chunks[1:], core)

    hb_in = _sibling_sums({"w_in": g_w_in}, core, tag="w_in")
    got_in = _chip_exchange(hb_in, name="grads_to_chips_w_in")
    g_big, d_big, nm_big, nv_big = {}, {}, {}, {}
    for n, h, r in zip(BIG, hb_in + hb_rest, list(got_in) + list(got_rest)):
        g_big[n], d_big[n], nm_big[n], nv_big[n] = _adam_shard(h, r, chip, wts[n], mom[n], var[n], name="adam_" + n)

    s_sum, s_d, s_nm, s_nv = _small_all_reduce_adam(
        _pack_small(gs, loss_blk[0:1]), _pack_small(sm), _pack_small({n: mom[n] for n in SMALL}),
        _pack_small({n: var[n] for n in SMALL}))
    loss = s_sum[LOSS_ROW, 0]
    g_small, d_small, nm_small, nv_small = (_unpack_small(t, sm) for t in (s_sum, s_d, s_nm, s_nv))

    order = ["norm_mix_g", "w_in", "hg_lb_logits", "hg_onorm_g", "fox_f_bias", "fox_q_norm_g", "fox_k_norm_g",
             "w_branch_a", "w_branch_b", "w_out", "norm_ffn_g", "w_ffn_gate", "w_ffn_up", "w_ffn_down", "norm_ple_g",
             "w_ple_gate", "w_ple_proj"]
    outs = [loss, grad_x[None]]
    for big, small in ((g_big, g_small), (d_big, d_small), (nm_big, nm_small), (nv_big, nv_small)):
        outs += [big[n] if n in big else small[n] for n in order]
    return tuple(outs)
```

```python
import functools

import jax
import jax.numpy as jnp
from jax import lax
from jax.experimental import pallas as pl
from jax.experimental.pallas import tpu as pltpu

F32 = jnp.float32
BF16 = jnp.bfloat16

D_MODEL = 1024
PLE_DIM = 256
HG_HEADS = 4
HG_DK = 128
HG_CHUNK = 64
HG_SUB = 16
HG_W = HG_HEADS * HG_DK
FOX_HEADS = 8
FOX_DH = 64
FOX_W = FOX_HEADS * FOX_DH
D_FF = 2816
EPS = 1e-6
N_DEV = 8
N_CHIP = 4
LANES = 128
FOX_COLS = 3 * FOX_W + LANES
HG_COLS = 4 * HG_W
GATE_COLS = 2 * D_MODEL
IN_COLS = HG_COLS + 3 * FOX_W + FOX_HEADS + GATE_COLS
FOX_LOGICAL = 3 * FOX_W + FOX_HEADS
SEG = 2048
IN_PAD = 3 * SEG
IN_SHARD = IN_COLS // N_DEV
IN_SHARD_PAD = 768
FF_SHARD = D_FF // N_DEV
FF_SHARD_PAD = 384
FF_PAD = N_DEV * FF_SHARD_PAD
EXP_CLAMP = 80.0
LOG2E = 1.4426950408889634

ADAM_LR = 0.001
ADAM_B1 = 0.9
ADAM_B2 = 0.999
ADAM_EPS = 1e-08
ADAM_WD = 0.01
ADAM_STEP = 10

MESH = pl.DeviceIdType.MESH
VMEM_LIMIT = 56 * 1024 * 1024
ROW_BLOCK = 512

BIG = ["w_in", "w_branch_a", "w_branch_b", "w_out", "w_ffn_gate", "w_ffn_up", "w_ffn_down",
       "w_ple_gate", "w_ple_proj"]
BIG_SHAPE = {
    "w_in": (D_MODEL, IN_COLS, 1), "w_branch_a": (HG_W, D_MODEL, 1), "w_branch_b": (FOX_W, D_MODEL, 1),
    "w_out": (D_MODEL, D_MODEL, 0), "w_ffn_gate": (D_MODEL, D_FF, 1), "w_ffn_up": (D_MODEL, D_FF, 1),
    "w_ffn_down": (D_FF, D_MODEL, 0), "w_ple_gate": (D_MODEL, D_MODEL, 0), "w_ple_proj": (PLE_DIM, D_MODEL, 1),
}

SMALL = ["norm_mix_g", "hg_lb_logits", "hg_onorm_g", "fox_f_bias", "fox_q_norm_g", "fox_k_norm_g",
         "norm_ffn_g", "norm_ple_g"]
SMALL_ROWS = {"norm_mix_g": 8, "hg_lb_logits": 8, "hg_onorm_g": 1, "fox_f_bias": 1, "fox_q_norm_g": 1,
              "fox_k_norm_g": 1, "norm_ffn_g": 8, "norm_ple_g": 8}
SMALL_TOTAL = 40
LOSS_ROW = 36


def _pallas(body, **kw):
    return pl.pallas_call(body, **kw)


def _params(**kw):
    return pltpu.CompilerParams(vmem_limit_bytes=VMEM_LIMIT, **kw)


def _pick(n, target):
    if n <= target:
        return n
    best = None
    for t in range(LANES, target + 1, LANES):
        if n % t == 0:
            best = t
    assert best is not None, (n, target)
    return best


def _dot(a, b, ca, cb):
    return lax.dot_general(a, b, (((ca,), (cb,)), ((), ())), preferred_element_type=F32)


def _split_dot(mat, x, ca, cb, terms=2, mat_first=True):
    acc = None
    rem = x
    for _ in range(terms):
        part = rem.astype(BF16)
        rem = rem - part.astype(F32)
        p = _dot(mat, part, ca, cb) if mat_first else _dot(part, mat, ca, cb)
        acc = p if acc is None else acc + p
    return acc


def _sigmoid(x):
    return 1.0 / (1.0 + jnp.exp(-x))


def _iota(shape, dim):
    return lax.broadcasted_iota(jnp.int32, shape, dim)


def _matmul(a, b, *, name, ta=False, tb=False, out_dtype=F32, add=None, exchange=None):
    (K, M) = a.shape if ta else a.shape[::-1]
    (N, Kb) = b.shape if tb else b.shape[::-1]
    assert K == Kb, (a.shape, b.shape, ta, tb)
    if ta:
        tm, tn, tk = _pick(M, 2 * ROW_BLOCK), _pick(N, 2 * ROW_BLOCK), _pick(K, 4 * ROW_BLOCK)
    else:
        tm, tn, tk = _pick(M, 2 * ROW_BLOCK), _pick(N, 2048), _pick(K, 3072)
    nk = K // tk
    use_scratch = nk > 1 and out_dtype != F32

    n_in = 2 + (add is not None)
    hs = list(exchange or [])
    n_x = len(hs)
    grid = (M // tm, N // tn, nk)

    def body(*refs):
        refs = list(refs)
        a_ref, b_ref = refs[:2]
        add_ref = refs[2] if add is not None else None
        o_ref = refs[n_in + n_x]
        k = pl.program_id(2)
        if n_x:
            x_start, x_finish = _chip_exchange_steps(
                refs[n_in:n_in + n_x], refs[n_in + n_x + 1:n_in + 2 * n_x + 1], *refs[-2:])
            at = [pl.program_id(d) for d in range(3)]

            @pl.when(jnp.logical_and(at[0] == 0, jnp.logical_and(at[1] == 0, at[2] == 0)))
            def _():
                x_start()
        p = _dot(a_ref[...].astype(BF16), b_ref[...].astype(BF16), 0 if ta else 1, 1 if tb else 0)

        def with_add(r):
            return r if add is None else r + add_ref[...].astype(F32)

        if nk == 1:
            o_ref[...] = with_add(p).astype(out_dtype)
        elif not use_scratch:
            @pl.when(k == 0)
            def _():
                o_ref[...] = with_add(p)

            @pl.when(k > 0)
            def _():
                o_ref[...] += p
        else:
            acc_ref = refs[n_in + 2 * n_x + 1]

            @pl.when(k == 0)
            def _():
                acc_ref[...] = p

            @pl.when(k > 0)
            def _():
                acc_ref[...] += p

            @pl.when(k == nk - 1)
            def _():
                o_ref[...] = with_add(acc_ref[...]).astype(out_dtype)

        if n_x:
            @pl.when(jnp.logical_and(at[0] == grid[0] - 1, jnp.logical_and(at[1] == grid[1] - 1, at[2] == nk - 1)))
            def _():
                x_finish()

    a_spec = pl.BlockSpec((tk, tm), lambda i, j, k: (k, i)) if ta else pl.BlockSpec((tm, tk), lambda i, j, k: (i, k))
    b_spec = pl.BlockSpec((tn, tk), lambda i, j, k: (j, k)) if tb else pl.BlockSpec((tk, tn), lambda i, j, k: (k, j))
    o_spec = pl.BlockSpec((tm, tn), lambda i, j, k: (i, j))
    hbm = pl.BlockSpec(memory_space=pl.ANY)
    in_specs = [a_spec, b_spec] + ([o_spec] if add is not None else []) + [hbm] * n_x
    args = (a, b) + ((add,) if add is not None else ()) + tuple(hs)
    res = _pallas(
        body, name=name, grid=grid, in_specs=in_specs, out_specs=[o_spec] + [hbm] * n_x,
        out_shape=[jax.ShapeDtypeStruct((M, N), out_dtype)] + (_chip_exchange_shapes(hs) if n_x else []),
        scratch_shapes=([pltpu.VMEM((tm, tn), F32)] if use_scratch else []) + (_chip_exchange_scratch(n_x) if n_x else []),
        compiler_params=_params(dimension_semantics=("arbitrary",) * 3),
    )(*args)
    return (res[0], list(res[1:])) if n_x else res[0]


def _row_map(nb, reverse, seg):
    if reverse:
        return lambda i: (nb - 1 - i, seg)
    return lambda i: (i, seg)


def _row_call(body, *, name, T, ins, outs, acc_outs=(), tm=ROW_BLOCK, reverse=False):
    tm = min(tm, T)
    nb = T // tm
    in_specs, args = [], []
    for arr, how in ins:
        args.append(arr)
        if how is True:
            in_specs.append(pl.BlockSpec((tm, arr.shape[1]), _row_map(nb, reverse, 0)))
        elif how is False:
            in_specs.append(pl.BlockSpec(arr.shape, lambda i, _n=arr.ndim: (0,) * _n))
        else:
            in_specs.append(pl.BlockSpec((tm, SEG), _row_map(nb, reverse, how[0])))
    out_specs, out_shape = [], []
    for o in outs:
        c, dt = o[0], o[1]
        total, seg = o[2] if len(o) > 2 else (c, 0)
        out_specs.append(pl.BlockSpec((tm, c), _row_map(nb, reverse, seg)))
        out_shape.append(jax.ShapeDtypeStruct((T, total), dt))
    for shp, dt in acc_outs:
        out_specs.append(pl.BlockSpec(shp, lambda i, _n=len(shp): (0,) * _n))
        out_shape.append(jax.ShapeDtypeStruct(shp, dt))
    return _pallas(body, name=name, grid=(nb,), in_specs=in_specs, out_specs=out_specs, out_shape=out_shape,
                   compiler_params=_params(dimension_semantics=("arbitrary",)))(*args)


def _rms_fwd(x, g, *, name):
    T = x.shape[0]

    def body(x_ref, g_ref, h_ref):
        xv = x_ref[...]
        rstd = lax.rsqrt(jnp.mean(xv * xv, axis=-1, keepdims=True) + EPS)
        h_ref[...] = (xv * rstd * g_ref[...]).astype(BF16)

    return _row_call(body, name=name, T=T, ins=[(x, True), (g, False)], outs=[(D_MODEL, BF16)])[0]


def _rms_bwd(x, g, dh, dres, *, name):
    T = x.shape[0]

    def body(x_ref, g_ref, dh_ref, dres_ref, dx_ref, dg_ref):
        xv = x_ref[...]
        rstd = lax.rsqrt(jnp.mean(xv * xv, axis=-1, keepdims=True) + EPS)
        xh = xv * rstd
        dhv = dh_ref[...]
        part = jnp.sum(dhv * xh, axis=0, keepdims=True)

        @pl.when(pl.program_id(0) == 0)
        def _():
            dg_ref[...] = part

        @pl.when(pl.program_id(0) > 0)
        def _():
            dg_ref[...] += part

        dxh = dhv * g_ref[...]
        dx_ref[...] = rstd * (dxh - xh * jnp.mean(dxh * xh, axis=-1, keepdims=True)) + dres_ref[...]

    return _row_call(body, name=name, T=T, ins=[(x, True), (g, False), (dh, True), (dres, True)],
                     outs=[(D_MODEL, F32)], acc_outs=[((1, D_MODEL), F32)])


def _merge_fwd(ua, ub, zg):
    def body(ua_ref, ub_ref, zg_ref, m_ref):
        ga = _sigmoid(zg_ref[:, :D_MODEL])
        gb = _sigmoid(zg_ref[:, D_MODEL:])
        m_ref[...] = (ga * ua_ref[...] + gb * ub_ref[...]).astype(BF16)

    return _row_call(body, name="merge_fwd", T=ua.shape[0], ins=[(ua, True), (ub, True), (zg, (2,))],
                     outs=[(D_MODEL, BF16)])[0]


def _merge_bwd(dm, ua, ub, zg):
    def body(dm_ref, ua_ref, ub_ref, zg_ref, dua_ref, dub_ref, dzg_ref):
        ga = _sigmoid(zg_ref[:, :D_MODEL])
        gb = _sigmoid(zg_ref[:, D_MODEL:])
        dmv = dm_ref[...]
        dua_ref[...] = (dmv * ga).astype(BF16)
        dub_ref[...] = (dmv * gb).astype(BF16)
        dzg_ref[:, :D_MODEL] = (dmv * ua_ref[...] * ga * (1.0 - ga)).astype(BF16)
        dzg_ref[:, D_MODEL:] = (dmv * ub_ref[...] * gb * (1.0 - gb)).astype(BF16)

    return _row_call(body, name="merge_bwd", T=dm.shape[0], ins=[(dm, True), (ua, True), (ub, True), (zg, (2,))],
                     outs=[(D_MODEL, BF16), (D_MODEL, BF16), (SEG, BF16, (IN_PAD, 2))])


def _swiglu_fwd(a, b):
    def body(a_ref, b_ref, o_ref):
        av = a_ref[...].astype(F32)
        o_ref[...] = (av * _sigmoid(av) * b_ref[...].astype(F32)).astype(BF16)

    return _row_call(body, name="swiglu_fwd", T=a.shape[0], ins=[(a, True), (b, True)], outs=[(a.shape[1], BF16)])[0]


def _swiglu_bwd(a, b, dact):
    def body(a_ref, b_ref, d_ref, da_ref, db_ref):
        av = a_ref[...].astype(F32)
        bv = b_ref[...].astype(F32)
        dv = d_ref[...]
        sg = _sigmoid(av)
        da_ref[...] = (dv * bv * sg * (1.0 + av * (1.0 - sg))).astype(BF16)
        db_ref[...] = (dv * av * sg).astype(BF16)

    return _row_call(body, name="swiglu_bwd", T=a.shape[0], ins=[(a, True), (b, True), (dact, True)],
                     outs=[(a.shape[1], BF16), (a.shape[1], BF16)])


def _ple_loss(x2, sp, pp, tgt):
    def body(x_ref, sp_ref, pp_ref, t_ref, dy_ref, dsp_ref, dpp_ref, loss_ref):
        gp = _sigmoid(sp_ref[...])
        ppv = pp_ref[...]
        err = x_ref[...] + gp * ppv - t_ref[...]
        part = 0.5 * jnp.sum(jnp.mean(err * err, axis=-1, keepdims=True), axis=0, keepdims=True)
        part = jnp.broadcast_to(part, loss_ref.shape)

        @pl.when(pl.program_id(0) == 0)
        def _():
            loss_ref[...] = part

        @pl.when(pl.program_id(0) > 0)
        def _():
            loss_ref[...] += part

        dy = err * (1.0 / D_MODEL)
        dy_ref[...] = dy
        dsp_ref[...] = (dy * ppv * gp * (1.0 - gp)).astype(BF16)
        dpp_ref[...] = (dy * gp).astype(BF16)

    return _row_call(body, name="ple_loss", T=x2.shape[0], ins=[(x2, True), (sp, True), (pp, True), (tgt, True)],
                     outs=[(D_MODEL, F32), (D_MODEL, BF16), (D_MODEL, BF16)], acc_outs=[((8, LANES), F32)])


def _hg_consts():
    C = HG_CHUNK
    r, c = _iota((C, C), 0), _iota((C, C), 1)
    tri = (c <= r)
    same = (r // HG_SUB) == (c // HG_SUB)
    return tri, (tri & same)


def _hg_chunk_fwd(q, f, lb, tri_b, sub_b):
    sgq = _sigmoid(q)
    qt = q * sgq
    sg = _sigmoid(f)
    fg = lb + (1.0 - lb) * sg
    kf = (1.0 - lb) * (1.0 - sg)
    logf = jnp.log(fg)
    b = _split_dot(tri_b, logf, 1, 0)
    w = _split_dot(sub_b, logf, 1, 0)
    return sgq, qt, sg, fg, kf, b, w


def _hg_scores(qs_b, kf, b, row):
    C, S = HG_CHUNK, HG_SUB
    parts, ks = [], []
    for blk in range(C // S):
        ref = jnp.zeros_like(b[0:1]) if blk == 0 else b[blk * S - 1:blk * S]
        e = jnp.exp(jnp.minimum(ref - b, EXP_CLAMP))
        e = jnp.where(row < (blk + 1) * S, e, 0.0)
        k_b = (kf * e).astype(BF16)
        ks.append((e, k_b))
        parts.append(_dot(qs_b[blk * S:(blk + 1) * S], k_b, 1, 1))
    return jnp.concatenate(parts, axis=0), ks


def _hgrn_fwd(z, lb_logits, gain):
    T = z.shape[0]
    RB = min(ROW_BLOCK, T)
    nb, cpb = T // RB, RB // HG_CHUNK
    C, DK = HG_CHUNK, HG_DK

    def body(z_ref, lg_ref, g_ref, o_ref, y_ref, st_ref, s_ref):
        @pl.when(pl.program_id(0) == 0)
        def _():
            s_ref[...] = jnp.zeros_like(s_ref)

        lg = lg_ref[...]
        lb_all = 1.0 / (1.0 + jnp.exp(lg[1:2] - lg[0:1]))
        gain_v = g_ref[...]
        tri, sub = _hg_consts()
        tri_b, sub_b = tri.astype(BF16), sub.astype(BF16)
        row = _iota((C, DK), 0)

        def chunk(ci, carry):
            r0 = pl.multiple_of(ci * C, C)
            rows = pl.ds(r0, C)
            for h in range(HG_HEADS):
                cs = slice(h * DK, (h + 1) * DK)
                q = z_ref[rows, cs]
                f = z_ref[rows, HG_W + h * DK:HG_W + (h + 1) * DK]
                v = z_ref[rows, 2 * HG_W + h * DK:2 * HG_W + (h + 1) * DK]
                g = z_ref[rows, 3 * HG_W + h * DK:3 * HG_W + (h + 1) * DK]
                lb = lb_all[:, cs]
                _, qt, _, _, kf, b, w = _hg_chunk_fwd(q, f, lb, tri_b, sub_b)
                st = s_ref[h]
                st_ref[pl.ds(pl.multiple_of((ci * HG_HEADS + h) * DK, DK), DK), :] = st
                v_b = v.astype(BF16)
                qs_b = (qt * jnp.exp(w)).astype(BF16)
                a, _ = _hg_scores(qs_b, kf, b, row)
                a = jnp.where(tri, a, 0.0)
                qd = qt * jnp.exp(b)
                o = _dot(qd.astype(BF16), st.astype(BF16), 1, 1) + _dot(a.astype(BF16), v_b, 1, 0)
                bl = b[C - 1:C]
                kd = kf * jnp.exp(bl - b)
                s_ref[h] = st * jnp.exp(bl) + _dot(v_b, kd.astype(BF16), 0, 0)
                o_ref[rows, cs] = o
                rstd = lax.rsqrt(jnp.mean(o * o, axis=-1, keepdims=True) + EPS)
                y_ref[rows, cs] = (o * rstd * gain_v * (g * _sigmoid(g))).astype(BF16)
            return carry

        lax.fori_loop(0, cpb, chunk, 0)

    return _pallas(
        body, name="hgrn_fwd", grid=(nb,),
        in_specs=[pl.BlockSpec((RB, HG_COLS), lambda i: (i, 0)), pl.BlockSpec((2, HG_W), lambda i: (0, 0)),
                  pl.BlockSpec((1, DK), lambda i: (0, 0))],
        out_specs=[pl.BlockSpec((RB, HG_W), lambda i: (i, 0)), pl.BlockSpec((RB, HG_W), lambda i: (i, 0)),
                   pl.BlockSpec((cpb * HG_HEADS * DK, DK), lambda i: (i, 0))],
        out_shape=[jax.ShapeDtypeStruct((T, HG_W), F32), jax.ShapeDtypeStruct((T, HG_W), BF16),
                   jax.ShapeDtypeStruct((T // C * HG_HEADS * DK, DK), F32)],
        scratch_shapes=[pltpu.VMEM((HG_HEADS, DK, DK), F32)],
        compiler_params=_params(dimension_semantics=("arbitrary",)),
    )(z, lb_logits, gain)


def _hgrn_bwd(z, o_raw, dy, states, lb_logits, gain, dz_buf):
    T = z.shape[0]
    RB = min(ROW_BLOCK, T)
    nb, cpb = T // RB, RB // HG_CHUNK
    C, DK, S = HG_CHUNK, HG_DK, HG_SUB

    def body(z_ref, o_ref, dy_ref, st_ref, lg_ref, g_ref, _buf_ref, dz_ref, dlg_ref, dg_ref, ds_ref, dlb_ref):
        step = pl.program_id(0)

        @pl.when(step == 0)
        def _():
            ds_ref[...] = jnp.zeros_like(ds_ref)
            dlb_ref[...] = jnp.zeros_like(dlb_ref)
            dg_ref[...] = jnp.zeros_like(dg_ref)

        lg = lg_ref[...]
        lb_all = 1.0 / (1.0 + jnp.exp(lg[1:2] - lg[0:1]))
        gain_v = g_ref[...]
        tri, sub = _hg_consts()
        tri_b, sub_b = tri.astype(BF16), sub.astype(BF16)
        row = _iota((C, DK), 0)

        def chunk(cj, carry):
            ci = cpb - 1 - cj
            r0 = pl.multiple_of(ci * C, C)
            rows = pl.ds(r0, C)
            for h in range(HG_HEADS):
                cs = slice(h * DK, (h + 1) * DK)
                q = z_ref[rows, cs]
                f = z_ref[rows, HG_W + h * DK:HG_W + (h + 1) * DK]
                v = z_ref[rows, 2 * HG_W + h * DK:2 * HG_W + (h + 1) * DK]
                g = z_ref[rows, 3 * HG_W + h * DK:3 * HG_W + (h + 1) * DK]
                lb = lb_all[:, cs]
                sgq, qt, sg, fg, kf, b, w = _hg_chunk_fwd(q, f, lb, tri_b, sub_b)
                st = st_ref[pl.ds(pl.multiple_of((ci * HG_HEADS + h) * DK, DK), DK), :]
                dst = ds_ref[h]
                o = o_ref[rows, cs]
                dyv = dy_ref[rows, cs]
                rstd = lax.rsqrt(jnp.mean(o * o, axis=-1, keepdims=True) + EPS)
                n = o * rstd
                sgg = _sigmoid(g)
                t1 = dyv * (g * sgg)
                dg_ref[...] += jnp.sum(t1 * n, axis=0, keepdims=True)
                dn = t1 * gain_v
                do = rstd * (dn - n * jnp.mean(dn * n, axis=-1, keepdims=True))
                dgate = dyv * n * gain_v * sgg * (1.0 + g * (1.0 - sgg))
                do_b = do.astype(BF16)
                v_b = v.astype(BF16)
                ew = jnp.exp(w)
                qs_b = (qt * ew).astype(BF16)
                a, ks = _hg_scores(qs_b, kf, b, row)
                a = jnp.where(tri, a, 0.0)
                eb = jnp.exp(b)
                qd = qt * eb
                bl = b[C - 1:C]
                ebl = jnp.exp(bl)
                ekd = jnp.exp(bl - b)
                kd = kf * ekd
                kd_b = kd.astype(BF16)
                qd_b = qd.astype(BF16)
                dst_b = dst.astype(BF16)
                dqd = _dot(do_b, st.astype(BF16), 1, 0)
                da = jnp.where(tri, _dot(do_b, v_b, 1, 1), 0.0)
                dv = _dot(a.astype(BF16), do_b, 0, 0) + _dot(kd_b, dst_b, 1, 1)
                dkd = _dot(v_b, dst_b, 1, 0)
                ds_ref[h] = dst * ebl + _dot(do_b, qd_b, 0, 0)
                dkd_kd = dkd * kd_b.astype(F32)
                dbl = ebl * jnp.sum(dst * st, axis=0, keepdims=True) + jnp.sum(dkd_kd, axis=0, keepdims=True)
                da_b = da.astype(BF16)
                dqs_parts = []
                dk_in = jnp.zeros((C, DK), F32)
                db_k = jnp.zeros((C, DK), F32)
                for blk in range(C // S):
                    e, k_b = ks[blk]
                    da_blk = da_b[blk * S:(blk + 1) * S]
                    dqs_parts.append(_dot(da_blk, k_b, 1, 0))
                    dks = _dot(da_blk, qs_b[blk * S:(blk + 1) * S], 0, 0)
                    dk_in = dk_in + dks * e
                    db_k = db_k + dks * k_b.astype(F32)
                dqs = jnp.concatenate(dqs_parts, axis=0)
                dqt_in = dqs * ew
                db = qs_b.astype(F32) * dqs - db_k + dqd * qd_b.astype(F32) - dkd_kd
                db = db + jnp.where(row == C - 1, dbl, 0.0)
                dlogf = _split_dot(tri_b, db, 0, 0)
                dqt = dqt_in + dqd * eb
                dkf = dk_in + dkd * ekd
                dfg = dlogf / fg - dkf
                dlb_ref[:, cs] += jnp.sum(dfg * (1.0 - sg), axis=0, keepdims=True)
                dz_ref[rows, cs] = (dqt * sgq * (1.0 + q * (1.0 - sgq))).astype(BF16)
                dz_ref[rows, HG_W + h * DK:HG_W + (h + 1) * DK] = (dfg * (1.0 - lb) * sg * (1.0 - sg)).astype(BF16)
                dz_ref[rows, 2 * HG_W + h * DK:2 * HG_W + (h + 1) * DK] = dv.astype(BF16)
                dz_ref[rows, 3 * HG_W + h * DK:3 * HG_W + (h + 1) * DK] = dgate.astype(BF16)
            return carry

        lax.fori_loop(0, cpb, chunk, 0)

        @pl.when(step == nb - 1)
        def _():
            d0 = dlb_ref[...] * lb_all * (1.0 - lb_all)
            dlg_ref[0:1, :] = d0
            dlg_ref[1:2, :] = -d0

    rev = lambda i: (nb - 1 - i, 0)
    fix = lambda i: (0, 0)
    return _pallas(
        body, name="hgrn_bwd", grid=(nb,),
        in_specs=[pl.BlockSpec((RB, HG_COLS), rev), pl.BlockSpec((RB, HG_W), rev), pl.BlockSpec((RB, HG_W), rev),
                  pl.BlockSpec((cpb * HG_HEADS * DK, DK), rev), pl.BlockSpec((2, HG_W), fix),
                  pl.BlockSpec((1, DK), fix), pl.BlockSpec(memory_space=pl.ANY)],
        out_specs=[pl.BlockSpec((RB, HG_COLS), rev), pl.BlockSpec((2, HG_W), fix), pl.BlockSpec((1, DK), fix)],
        out_shape=[jax.ShapeDtypeStruct(dz_buf.shape, BF16), jax.ShapeDtypeStruct((2, HG_W), F32),
                   jax.ShapeDtypeStruct((1, DK), F32)],
        scratch_shapes=[pltpu.VMEM((HG_HEADS, DK, DK), F32), pltpu.VMEM((1, HG_W), F32)],
        input_output_aliases={6: 0},
        compiler_params=_params(dimension_semantics=("arbitrary",)),
    )(z, o_raw, dy, states, lb_logits, gain, dz_buf)


def _head_ones():
    r, c = _iota((FOX_W, FOX_W), 0), _iota((FOX_W, FOX_W), 1)
    return ((r // FOX_DH) == (c // FOX_DH)).astype(BF16)


def _log_sigmoid(x):
    return jnp.minimum(x, 0.0) - jnp.log(1.0 + jnp.exp(-jnp.abs(x)))


def _fox_prep(z, bias, qg, kg):
    T = z.shape[0]
    tm = min(ROW_BLOCK, T)
    nb = T // tm

    def body(z_ref, b_ref, qg_ref, kg_ref, q_ref, k_ref, v_ref, qa_ref, ka_ref, carry_ref):
        @pl.when(pl.program_id(0) == 0)
        def _():
            carry_ref[...] = jnp.zeros_like(carry_ref)

        ones = _head_ones()
        normed = []
        for src, g_ref in ((0, qg_ref), (1, kg_ref)):
            xv = z_ref[:, src * FOX_W:(src + 1) * FOX_W]
            ms = _split_dot(ones, xv * xv, 1, 0, mat_first=False) * (1.0 / FOX_DH)
            normed.append(xv * lax.rsqrt(ms + EPS) * g_ref[...])
        qn, kn = normed
        q_ref[...] = (qn * FOX_DH ** -0.5).astype(BF16)
        k_b = kn.astype(BF16)
        k_ref[...] = k_b
        v_ref[...] = z_ref[:, 2 * FOX_W:3 * FOX_W].astype(BF16)
        logf = _log_sigmoid(z_ref[:, 3 * FOX_W:FOX_COLS] + b_ref[...])
        r, c = _iota((tm, tm), 0), _iota((tm, tm), 1)
        tri_b = (c <= r).astype(BF16)
        cum = _split_dot(tri_b, logf, 1, 0, terms=3) + carry_ref[...]
        carry_ref[...] = cum[tm - 1:tm]
        c2 = cum * LOG2E
        hi = c2.astype(BF16)
        rem = c2 - hi.astype(F32)
        mid = rem.astype(BF16)
        lo = (rem - mid.astype(F32)).astype(BF16)
        hrow, col = _iota((LANES, 2 * FOX_W), 0), _iota((LANES, 2 * FOX_W), 1)
        base = hrow * LANES + jnp.where(hrow % 2 == 0, FOX_DH, 0)
        placed = None
        for t, part in enumerate((hi, mid, lo)):
            place = jnp.logical_and(col == base + t, hrow < FOX_HEADS).astype(BF16)
            term = _dot(part, place, 1, 0)
            placed = term if placed is None else placed + term
        colw = _iota((tm, 2 * FOX_W), 1)
        head, lane = colw // LANES, colw % LANES
        own = (lane < FOX_DH) == (head % 2 == 0)
        other = jnp.where(head % 2 == 0, lane - FOX_DH, lane)
        ones_q = jnp.where(jnp.logical_and(other >= 0, other < 3), -1.0, 0.0)
        q2 = (qn * (FOX_DH ** -0.5 * LOG2E)).astype(BF16)
        q_exp = jnp.concatenate([q2[:, (h // 2) * LANES:(h // 2 + 1) * LANES] for h in range(FOX_HEADS)], axis=1)
        k_exp = jnp.concatenate([k_b[:, (h // 2) * LANES:(h // 2 + 1) * LANES] for h in range(FOX_HEADS)], axis=1)
        qa_ref[...] = jnp.where(own, q_exp, ones_q.astype(BF16))
        ka_ref[...] = jnp.where(own, k_exp, placed.astype(BF16))

    return _pallas(
        body, name="fox_prep", grid=(nb,),
        in_specs=[pl.BlockSpec((tm, SEG), lambda i: (i, 1)), pl.BlockSpec((1, LANES), lambda i: (0, 0)),
                  pl.BlockSpec((1, FOX_W), lambda i: (0, 0)), pl.BlockSpec((1, FOX_W), lambda i: (0, 0))],
        out_specs=[pl.BlockSpec((tm, FOX_W), lambda i: (i, 0))] * 3 + [pl.BlockSpec((tm, 2 * FOX_W), lambda i: (i, 0))] * 2,
        out_shape=[jax.ShapeDtypeStruct((T, FOX_W), BF16)] * 3 + [jax.ShapeDtypeStruct((T, 2 * FOX_W), BF16)] * 2,
        scratch_shapes=[pltpu.VMEM((1, LANES), F32)],
        compiler_params=_params(dimension_semantics=("arbitrary",)),
    )(z, bias, qg, kg)


def _fox_fwd(qa, ka, vb, blocks):
    T = qa.shape[0]
    tq = min(ROW_BLOCK, T)
    nq = T // tq
    NEG = -1e30
    n = len(blocks)
    n_pairs = FOX_HEADS // 2

    def body(*refs):
        q_ref, k_ref, v_ref = refs[:3]
        o_ref, lse_ref = refs[3 + n:5 + n]
        m_sc, l_sc, acc_sc = refs[5 + 2 * n:8 + 2 * n]
        pr, qi, ki = pl.program_id(0), pl.program_id(1), pl.program_id(2)
        g_start, g_forward, g_finish = _gather_steps(refs[3:3 + n], refs[5 + n:5 + 2 * n], *refs[8 + 2 * n:])
        row_start = jnp.logical_and(qi == 0, ki == 0)

        @pl.when(jnp.logical_and(pr == 0, row_start))
        def _():
            g_start()

        @pl.when(jnp.logical_and(pr == n_pairs // 2, row_start))
        def _():
            g_forward()

        @pl.when(ki == 0)
        def _():
            m_sc[...] = jnp.full_like(m_sc, NEG)
            l_sc[...] = jnp.zeros_like(l_sc)
            acc_sc[...] = jnp.zeros_like(acc_sc)

        lane = _iota((tq, LANES), 1)

        def block(masked):
            vv = v_ref[...]
            scores = [_dot(q_ref[:, hh * LANES:(hh + 1) * LANES], k_ref[:, hh * LANES:(hh + 1) * LANES], 1, 1)
                      for hh in range(2)]
            probs = []
            for hh in range(2):
                s = scores[hh]
                tiles = [s[:, j * LANES:(j + 1) * LANES] for j in range(tq // LANES)]
                if masked:
                    row, col = _iota((tq, LANES), 0), _iota((tq, LANES), 1)
                    tiles = [jnp.where(row >= col + j * LANES, t, NEG) for j, t in enumerate(tiles)]
                tmax = functools.reduce(jnp.maximum, tiles)
                m_old = m_sc[hh]
                m_new = jnp.maximum(m_old, jnp.max(tmax, axis=-1, keepdims=True))
                m_b = jnp.broadcast_to(m_new, (tq, LANES))
                alpha_b = jnp.broadcast_to(jnp.exp2(m_old - m_new), (tq, LANES))
                ps = [jnp.exp2(t - m_b) for t in tiles]
                l_sc[hh] = alpha_b * l_sc[hh] + functools.reduce(jnp.add, ps)
                m_sc[hh] = m_new
                probs.append((jnp.concatenate([p.astype(BF16) for p in ps], axis=1), alpha_b))
            for hh in range(2):
                p_b, alpha_b = probs[hh]
                acc_sc[hh] = alpha_b * acc_sc[hh] + _dot(p_b, vv, 1, 0)

        @pl.when(ki < qi)
        def _():
            block(False)

        @pl.when(ki == qi)
        def _():
            block(True)
            l0 = jnp.sum(l_sc[0], axis=-1, keepdims=True)
            l1 = jnp.sum(l_sc[1], axis=-1, keepdims=True)
            o_ref[...] = jnp.where(lane < FOX_DH, acc_sc[0] * (1.0 / l0), acc_sc[1] * (1.0 / l1))
            lse_ref[:, :LANES] = jnp.broadcast_to(m_sc[0] + jnp.log2(l0), (tq, LANES))
            lse_ref[:, LANES:] = jnp.broadcast_to(m_sc[1] + jnp.log2(l1), (tq, LANES))

        @pl.when(jnp.logical_and(pr == n_pairs - 1, jnp.logical_and(qi == nq - 1, ki == nq - 1)))
        def _():
            g_finish()

    kmap = lambda p, i, j: (jnp.minimum(i, j), p)
    hbm = pl.BlockSpec(memory_space=pl.ANY)
    res = _pallas(
        body, name="fox_fwd", grid=(n_pairs, nq, nq),
        in_specs=[pl.BlockSpec((tq, 2 * LANES), lambda p, i, j: (i, p)), pl.BlockSpec((tq, 2 * LANES), kmap),
                  pl.BlockSpec((tq, LANES), kmap)] + [hbm] * n,
        out_specs=[pl.BlockSpec((tq, LANES), lambda p, i, j: (i, p)),
                   pl.BlockSpec((tq, 2 * LANES), lambda p, i, j: (i, p))] + [hbm] * n,
        out_shape=[jax.ShapeDtypeStruct((T, FOX_W), F32), jax.ShapeDtypeStruct((T, 2 * FOX_W), F32)]
        + [jax.ShapeDtypeStruct((N_DEV,) + b.shape, b.dtype) for b in blocks],
        scratch_shapes=[pltpu.VMEM((2, tq, 1), F32), pltpu.VMEM((2, tq, LANES), F32), pltpu.VMEM((2, tq, LANES), F32)]
        + _gather_scratch(n),
        compiler_params=_params(dimension_semantics=("arbitrary",) * 3),
    )(qa, ka, vb, *blocks)
    return res[0], res[1], res[2:]


def _fox_bwd(qs, kn, vb, qa, ka, o, do, lse, hs):
    T = qs.shape[0]
    tq = min(ROW_BLOCK, T)
    nq = T // tq
    n = len(hs)
    n_pairs = FOX_HEADS // 2

    def body(*refs):
        q_ref, k_ref, v_ref, qa_ref, ka_ref, o_ref, do_ref, lse_ref = refs[:8]
        dq_ref, dk_ref, dv_ref, dcs_ref, drs_ref = refs[8 + n:13 + n]
        pr, ki, qi = pl.program_id(0), pl.program_id(1), pl.program_id(2)
        x_start, x_finish = _chip_exchange_steps(refs[8:8 + n], refs[13 + n:13 + 2 * n], *refs[13 + 2 * n:])

        @pl.when(jnp.logical_and(pr == 0, jnp.logical_and(ki == 0, qi == 0)))
        def _():
            x_start()

        @pl.when(jnp.logical_and(ki == 0, qi == 0))
        def _():
            dq_ref[...] = jnp.zeros_like(dq_ref)

        @pl.when(jnp.logical_and(pr == 0, jnp.logical_and(ki == 0, qi == 0)))
        def _():
            drs_ref[...] = jnp.zeros_like(drs_ref)

        @pl.when(qi == ki)
        def _():
            dk_ref[...] = jnp.zeros_like(dk_ref)
            dv_ref[...] = jnp.zeros_like(dv_ref)
            dcs_ref[...] = jnp.zeros_like(dcs_ref)

        def block(masked):
            lane = _iota((tq, LANES), 1)
            qv, kv, vv = q_ref[...], k_ref[...], v_ref[...]
            ov, dov = o_ref[...], do_ref[...]
            qrows = pl.ds(pl.multiple_of(qi * tq, tq), tq)
            dq_acc = jnp.zeros((tq, LANES), F32)
            dk_acc = jnp.zeros((tq, LANES), F32)
            dv_acc = jnp.zeros((tq, LANES), F32)
            dcs_acc = jnp.zeros((8, tq), F32)
            drs_acc = jnp.zeros((tq, LANES), F32)
            prod = dov * ov
            nt = tq // LANES
            for hh in range(2):
                in_head = (lane < FOX_DH) if hh == 0 else (lane >= FOX_DH)
                hs_ = slice(hh * LANES, (hh + 1) * LANES)
                zb = jnp.zeros_like(qv)
                qm = jnp.where(in_head, qv, zb)
                km = jnp.where(in_head, kv, zb)
                dom = jnp.where(in_head, dov, 0.0).astype(BF16)
                delta_b = jnp.broadcast_to(jnp.sum(jnp.where(in_head, prod, 0.0), axis=1, keepdims=True), (tq, LANES))
                lse_b = lse_ref[:, hs_]
                s = _dot(qa_ref[:, hs_], ka_ref[:, hs_], 1, 1)
                dp = _dot(dom, vv, 1, 1)
                ps, dss = [], []
                for j in range(nt):
                    js = slice(j * LANES, (j + 1) * LANES)
                    p = jnp.exp2(s[:, js] - lse_b)
                    if masked:
                        p = jnp.where(_iota((tq, LANES), 0) >= _iota((tq, LANES), 1) + j * LANES, p, 0.0)
                    ps.append(p)
                    dss.append(p * (dp[:, js] - delta_b))
                p_b = jnp.concatenate([p.astype(BF16) for p in ps], axis=1)
                ds_b = jnp.concatenate([d.astype(BF16) for d in dss], axis=1)
                dv_acc = dv_acc + _dot(p_b, dom, 0, 0)
                dq_acc = dq_acc + _dot(ds_b, km, 1, 0)
                dk_acc = dk_acc + _dot(ds_b, qm, 0, 0)
                colsum = jnp.concatenate([jnp.sum(d, axis=0, keepdims=True) for d in dss], axis=1)
                dcs_acc = dcs_acc + jnp.where(_iota((8, tq), 0) == hh, colsum, 0.0)
                rowsum = jnp.sum(functools.reduce(jnp.add, dss), axis=1, keepdims=True)
                drs_acc = drs_acc + jnp.where(lane == 2 * pr + hh, rowsum, 0.0)
            drs_ref[qrows, :] += drs_acc
            dq_ref[qrows, :] += dq_acc
            dk_ref[...] += dk_acc
            dv_ref[...] += dv_acc
            dcs_ref[0] += dcs_acc

        @pl.when(qi > ki)
        def _():
            block(False)

        @pl.when(qi == ki)
        def _():
            block(True)

        @pl.when(jnp.logical_and(pr == n_pairs - 1, jnp.logical_and(ki == nq - 1, qi == nq - 1)))
        def _():
            x_finish()

    qmap = lambda p, j, i: (jnp.maximum(i, j), p)
    hbm = pl.BlockSpec(memory_space=pl.ANY)
    res = _pallas(
        body, name="fox_bwd", grid=(n_pairs, nq, nq),
        in_specs=[pl.BlockSpec((tq, LANES), qmap), pl.BlockSpec((tq, LANES), lambda p, j, i: (j, p)),
                  pl.BlockSpec((tq, LANES), lambda p, j, i: (j, p)),
                  pl.BlockSpec((tq, 2 * LANES), qmap), pl.BlockSpec((tq, 2 * LANES), lambda p, j, i: (j, p)),
                  pl.BlockSpec((tq, LANES), qmap), pl.BlockSpec((tq, LANES), qmap), pl.BlockSpec((tq, 2 * LANES), qmap)]
        + [hbm] * n,
        out_specs=[pl.BlockSpec((T, LANES), lambda p, j, i: (0, p)), pl.BlockSpec((tq, LANES), lambda p, j, i: (j, p)),
                   pl.BlockSpec((tq, LANES), lambda p, j, i: (j, p)), pl.BlockSpec((1, 8, tq), lambda p, j, i: (p, 0, j)),
                   pl.BlockSpec((T, LANES), lambda p, j, i: (0, 0))] + [hbm] * n,
        out_shape=[jax.ShapeDtypeStruct((T, FOX_W), F32)] * 3
        + [jax.ShapeDtypeStruct((n_pairs, 8, T), F32), jax.ShapeDtypeStruct((T, LANES), F32)]
        + _chip_exchange_shapes(hs),
        scratch_shapes=_chip_exchange_scratch(n),
        compiler_params=_params(dimension_semantics=("arbitrary",) * 3),
    )(qs, kn, vb, qa, ka, o, do, lse, *hs)
    res = list(res)
    return res[:5] + [res[5:]]


def _fox_post(z, dq, dk, dv, dcs, drs, bias, qg, kg, dz_buf):
    T = z.shape[0]
    tm = min(ROW_BLOCK, T)
    nb = T // tm

    def body(z_ref, dq_ref, dk_ref, dv_ref, dcs_ref, drs_ref, b_ref, qg_ref, kg_ref, _buf_ref, dz_ref, dqg_ref, dkg_ref,
             db_ref, carry_ref):
        @pl.when(pl.program_id(0) == 0)
        def _():
            carry_ref[...] = jnp.zeros_like(carry_ref)
            dqg_ref[...] = jnp.zeros_like(dqg_ref)
            dkg_ref[...] = jnp.zeros_like(dkg_ref)
            db_ref[...] = jnp.zeros_like(db_ref)

        ones = _head_ones()
        for src, g_ref, d_ref, dg_ref, scale in ((0, qg_ref, dq_ref, dqg_ref, FOX_DH ** -0.5), (1, kg_ref, dk_ref, dkg_ref, 1.0)):
            xv = z_ref[:, src * FOX_W:(src + 1) * FOX_W]
            ms = _split_dot(ones, xv * xv, 1, 0, mat_first=False) * (1.0 / FOX_DH)
            rstd = lax.rsqrt(ms + EPS)
            xh = xv * rstd
            dn = d_ref[...] * scale
            dg_ref[...] += jnp.sum(dn * xh, axis=0, keepdims=True)
            dxh = dn * g_ref[...]
            mean = _split_dot(ones, dxh * xh, 1, 0, mat_first=False) * (1.0 / FOX_DH)
            dz_ref[:, src * FOX_W:(src + 1) * FOX_W] = (rstd * (dxh - xh * mean)).astype(BF16)
        dz_ref[:, 2 * FOX_W:3 * FOX_W] = dv_ref[...].astype(BF16)
        row8 = _iota((8, tm), 0)
        dct = jnp.zeros((8, tm), F32)
        for h in range(FOX_HEADS):
            src_row = dcs_ref[h // 2][h % 2:h % 2 + 1, :]
            dct = dct + jnp.where(row8 == h, src_row, 0.0)
        dct = drs_ref[...].T[0:8] - dct
        r, c = _iota((tm, tm), 0), _iota((tm, tm), 1)
        upper_b = (r >= c).astype(BF16)
        rc = _split_dot(upper_b, dct, 1, 0, mat_first=False) + carry_ref[...]
        carry_ref[...] = rc[:, 0:1]
        full = jnp.concatenate([rc, jnp.zeros((LANES - 8, tm), F32)], axis=0)
        dlogf = full.T
        xf = z_ref[:, 3 * FOX_W:FOX_COLS] + b_ref[...]
        df = dlogf * (1.0 - _sigmoid(xf))
        dz_ref[:, 3 * FOX_W:FOX_COLS] = df.astype(BF16)
        dz_ref[:, FOX_COLS:] = jnp.zeros((tm, SEG - FOX_COLS), BF16)
        db_ref[...] += jnp.sum(df, axis=0, keepdims=True)

    rev = lambda i: (nb - 1 - i, 0)
    fix2 = lambda i: (0, 0)
    return _pallas(
        body, name="fox_post", grid=(nb,),
        in_specs=[pl.BlockSpec((tm, SEG), lambda i: (nb - 1 - i, 1)), pl.BlockSpec((tm, FOX_W), rev),
                  pl.BlockSpec((tm, FOX_W), rev),
                  pl.BlockSpec((tm, FOX_W), rev), pl.BlockSpec((FOX_HEADS // 2, 8, tm), lambda i: (0, 0, nb - 1 - i)),
                  pl.BlockSpec((tm, LANES), rev),
                  pl.BlockSpec((1, LANES), fix2), pl.BlockSpec((1, FOX_W), fix2), pl.BlockSpec((1, FOX_W), fix2),
                  pl.BlockSpec(memory_space=pl.ANY)],
        out_specs=[pl.BlockSpec((tm, SEG), lambda i: (nb - 1 - i, 1)), pl.BlockSpec((1, FOX_W), fix2),
                   pl.BlockSpec((1, FOX_W), fix2), pl.BlockSpec((1, LANES), fix2)],
        out_shape=[jax.ShapeDtypeStruct(dz_buf.shape, BF16), jax.ShapeDtypeStruct((1, FOX_W), F32),
                   jax.ShapeDtypeStruct((1, FOX_W), F32), jax.ShapeDtypeStruct((1, LANES), F32)],
        scratch_shapes=[pltpu.VMEM((8, 1), F32)],
        input_output_aliases={9: 0},
        compiler_params=_params(dimension_semantics=("arbitrary",)),
    )(z, dq, dk, dv, dcs, drs, bias, qg, kg, dz_buf)


def _local_step(x, p, tgt, sm, W, rest_chunks, core):
    lbl, og, fb = sm["hg_lb_logits"], sm["hg_onorm_g"], sm["fox_f_bias"]
    fbias = jnp.pad(fb, ((0, 0), (0, LANES - FOX_HEADS)))
    qg = jnp.tile(sm["fox_q_norm_g"], (1, FOX_HEADS))
    kg = jnp.tile(sm["fox_k_norm_g"], (1, FOX_HEADS))

    h = _rms_fwd(x, sm["norm_mix_g"], name="rms_mix")
    z = _matmul(h, W["w_in"], name="mm_z")
    o_raw, ya, states = _hgrn_fwd(z, lbl, og)
    qs, kn, vb, qa, ka = _fox_prep(z, fbias, qg, kg)
    yb, lse, gathered = _fox_fwd(qa, ka, vb, rest_chunks)
    W = dict(W, **{n: _full_of_chunks(n, g) for n, g in zip(BIG[1:], gathered)})
    ua = _matmul(ya, W["w_branch_a"], name="mm_ua")
    ub = _matmul(yb, W["w_branch_b"], name="mm_ub")
    merged = _merge_fwd(ua, ub, z)
    x1 = _matmul(merged, W["w_out"], add=x, name="mm_x1")
    hf = _rms_fwd(x1, sm["norm_ffn_g"], name="rms_ffn")
    a = _matmul(hf, W["w_ffn_gate"], out_dtype=BF16, name="mm_ffn_a")
    b = _matmul(hf, W["w_ffn_up"], out_dtype=BF16, name="mm_ffn_b")
    act = _swiglu_fwd(a, b)
    x2 = _matmul(act, W["w_ffn_down"], add=x1, name="mm_x2")
    hp = _rms_fwd(x2, sm["norm_ple_g"], name="rms_ple")
    sp = _matmul(hp, W["w_ple_gate"], name="mm_sp")
    pp = _matmul(p, W["w_ple_proj"], name="mm_pp")
    dy, dsp, dpp, loss = _ple_loss(x2, sp, pp, tgt)

    G = {}
    G["w_ple_proj"] = _matmul(p, dpp, ta=True, out_dtype=BF16, name="mm_dw_ple_proj")
    G["w_ple_gate"] = _matmul(hp, dsp, ta=True, out_dtype=BF16, name="mm_dw_ple_gate")
    dhp = _matmul(dsp, W["w_ple_gate"], tb=True, name="mm_dhp")
    dx2, d_ple_g = _rms_bwd(x2, sm["norm_ple_g"], dhp, dy, name="rms_ple_bwd")
    dact = _matmul(dx2, W["w_ffn_down"], tb=True, name="mm_dact")
    G["w_ffn_down"] = _matmul(act, dx2, ta=True, out_dtype=BF16, name="mm_dw_ffn_down")
    da, db = _swiglu_bwd(a, b, dact)
    G["w_ffn_gate"] = _matmul(hf, da, ta=True, out_dtype=BF16, name="mm_dw_ffn_gate")
    G["w_ffn_up"] = _matmul(hf, db, ta=True, out_dtype=BF16, name="mm_dw_ffn_up")
    dhf = _matmul(da, W["w_ffn_gate"], tb=True, name="mm_dhf_a")
    dhf = _matmul(db, W["w_ffn_up"], tb=True, add=dhf, name="mm_dhf_b")
    dx1, d_ffn_g = _rms_bwd(x1, sm["norm_ffn_g"], dhf, dx2, name="rms_ffn_bwd")
    dmerged = _matmul(dx1, W["w_out"], tb=True, name="mm_dmerged")
    G["w_out"] = _matmul(merged, dx1, ta=True, out_dtype=BF16, name="mm_dw_out")
    dua, dub, dz = _merge_bwd(dmerged, ua, ub, z)
    G["w_branch_a"] = _matmul(ya, dua, ta=True, out_dtype=BF16, name="mm_dw_branch_a")
    G["w_branch_b"] = _matmul(yb, dub, ta=True, out_dtype=BF16, name="mm_dw_branch_b")
    dya = _matmul(dua, W["w_branch_a"], tb=True, name="mm_dya")
    dyb = _matmul(dub, W["w_branch_b"], tb=True, name="mm_dyb")
    hb_rest = _sibling_sums({n: G[n] for n in BIG[1:]}, core, tag="rest")
    dq, dk, dv, dcs, drs, got_rest = _fox_bwd(qs, kn, vb, qa, ka, yb, dyb, lse, hb_rest)
    dz, d_qg, d_kg, d_fb = _fox_post(z, dq, dk, dv, dcs, drs, fbias, qg, kg, dz)
    dz, d_lbl, d_og = _hgrn_bwd(z, o_raw, dya, states, lbl, og, dz)
    G["w_in"] = _matmul(h, dz, ta=True, out_dtype=BF16, name="mm_dw_in")
    hb_in = _sibling_sums({"w_in": G["w_in"]}, core, tag="w_in")
    dh, got_in = _matmul(dz, W["w_in"], tb=True, exchange=hb_in, name="mm_dh")
    grad_x, d_mix_g = _rms_bwd(x, sm["norm_mix_g"], dh, dx1, name="rms_mix_bwd")

    gs = {"norm_mix_g": d_mix_g, "hg_lb_logits": d_lbl, "hg_onorm_g": d_og, "fox_f_bias": d_fb[:, :FOX_HEADS],
          "fox_q_norm_g": d_qg.reshape(FOX_HEADS, FOX_DH).sum(0, keepdims=True),
          "fox_k_norm_g": d_kg.reshape(FOX_HEADS, FOX_DH).sum(0, keepdims=True),
          "norm_ffn_g": d_ffn_g, "norm_ple_g": d_ple_g}
    return loss, grad_x, gs, hb_in + hb_rest, list(got_in) + list(got_rest)


def _pack_rows(parts, total):
    buf = jnp.concatenate(parts, axis=-2)
    pad = total - buf.shape[-2]
    widths = [(0, 0)] * (buf.ndim - 2) + [(0, pad), (0, 0)]
    return jnp.pad(buf, widths)


def _chunk_of_shard(n, w):
    if n == "w_in":
        return jnp.pad(w, ((0, 0), (0, IN_SHARD_PAD - IN_SHARD)))
    if n in ("w_ffn_gate", "w_ffn_up"):
        return jnp.pad(w, ((0, 0), (0, FF_SHARD_PAD - FF_SHARD)))
    if n == "w_ffn_down":
        return jnp.pad(w, ((0, FF_SHARD_PAD - FF_SHARD), (0, 0)))
    return w


def _full_of_chunks(n, g):
    _, a, b = g.shape
    if BIG_SHAPE[n][2] == 0:
        return g.reshape(N_DEV * a, b)
    if n == "w_in":
        w = g[:, :, :IN_SHARD].transpose(1, 0, 2).reshape(a, IN_COLS)
        gap = jnp.zeros((a, SEG - FOX_LOGICAL), g.dtype)
        return jnp.concatenate([w[:, :HG_COLS], w[:, HG_COLS:HG_COLS + FOX_LOGICAL], gap, w[:, HG_COLS + FOX_LOGICAL:]],
                               axis=1)
    return g.transpose(1, 0, 2).reshape(a, N_DEV * b)


def _chunks_of_full(n, g):
    if BIG_SHAPE[n][2] == 0:
        return g.reshape(N_DEV, g.shape[0] // N_DEV, g.shape[1])
    if n == "w_in":
        a = g.shape[0]
        w = jnp.concatenate([g[:, :HG_COLS], g[:, SEG:SEG + FOX_LOGICAL], g[:, 2 * SEG:]], axis=1)
        w = w.reshape(a, N_DEV, IN_SHARD).transpose(1, 0, 2)
        return jnp.pad(w, ((0, 0), (0, 0), (0, IN_SHARD_PAD - IN_SHARD)))
    return g.reshape(g.shape[0], N_DEV, g.shape[1] // N_DEV).transpose(1, 0, 2)


def _pack_small(vals, loss_row=None):
    parts = [vals[n].reshape(SMALL_ROWS[n], -1) for n in SMALL]
    parts = [jnp.pad(v, ((0, 0), (0, LANES - v.shape[1]))) for v in parts]
    if loss_row is not None:
        parts.append(loss_row)
    return _pack_rows(parts, SMALL_TOTAL)


def _unpack_small(buf, like):
    out, r0 = {}, 0
    for n in SMALL:
        rows, size = SMALL_ROWS[n], like[n].size
        blk = buf[r0:r0 + rows]
        out[n] = (blk if size == rows * LANES else blk[:, :size]).reshape(like[n].shape)
        r0 += rows
    return out


def _place():
    return lax.axis_index("x"), lax.axis_index("y"), lax.axis_index("c")


def _gather_steps(x_refs, out_refs, send_sems, recv_sems, local_sems):
    n = len(x_refs)
    x, y, c = _place()
    me, sibling = (x, y, c), (x, y, 1 - c)
    chips = [(1 - x, y), (x, 1 - y), (1 - x, 1 - y)]

    def slot(i, px, py, pc):
        return out_refs[i].at[4 * px + 2 * py + pc]

    def copy(k, i, blk, to, own=False):
        return pltpu.make_async_remote_copy(
            src_ref=x_refs[i] if own else slot(i, *blk), dst_ref=slot(i, *blk),
            send_sem=send_sems.at[k, i], recv_sem=recv_sems.at[k, i], device_id=to, device_id_type=MESH)

    def mine():
        return [pltpu.make_async_copy(x_refs[i], slot(i, *me), local_sems.at[i]) for i in range(n)]

    def first():
        cps = [copy(0, i, me, sibling, own=True) for i in range(n)]
        return cps + [copy(1 + j, i, me, (*chip, c), own=True) for j, chip in enumerate(chips) for i in range(n)]

    def passed():
        return [copy(4 + j, i, (*chip, c), sibling) for j, chip in enumerate(chips) for i in range(n)]

    def start():
        for cp in mine() + first():
            cp.start()

    def forward():
        fws = passed()
        for j, chip in enumerate(chips):
            for i in range(n):
                copy(1 + j, i, (*chip, c), me).wait_recv()
                fws[j * n + i].start()

    def finish():
        for i in range(n):
            copy(0, i, sibling, me).wait_recv()
        for j, chip in enumerate(chips):
            for i in range(n):
                copy(4 + j, i, (*chip, 1 - c), me).wait_recv()
        for cp in first() + passed():
            cp.wait_send()
        for cp in mine():
            cp.wait()

    return start, forward, finish


def _gather_scratch(n):
    return [pltpu.SemaphoreType.DMA((7, n)), pltpu.SemaphoreType.DMA((7, n)), pltpu.SemaphoreType.DMA((n,))]


def _all_gather(blocks, *, name):
    n = len(blocks)

    def body(*refs):
        for step in _gather_steps(refs[:n], refs[n:2 * n], *refs[2 * n:]):
            step()

    hbm = pl.BlockSpec(memory_space=pl.ANY)
    return _pallas(
        body, name=name, out_shape=[jax.ShapeDtypeStruct((N_DEV,) + b.shape, b.dtype) for b in blocks],
        in_specs=[hbm] * n, out_specs=[hbm] * n, scratch_shapes=_gather_scratch(n),
    )(*blocks)


def _sibling_exchange(gs, *, name):
    n = len(gs)

    def body(*refs):
        g_refs, out_refs = refs[:n], refs[n:2 * n]
        send_sems, recv_sems = refs[2 * n:]
        x, y, c = _place()
        cps = [pltpu.make_async_remote_copy(
            src_ref=g_refs[i].at[:, pl.ds(1 - c, 1)], dst_ref=out_refs[i], send_sem=send_sems.at[i],
            recv_sem=recv_sems.at[i], device_id=(x, y, 1 - c), device_id_type=MESH) for i in range(n)]
        for cp in cps:
            cp.start()
        for cp in cps:
            cp.wait()

    hbm = pl.BlockSpec(memory_space=pl.ANY)
    return _pallas(
        body, name=name, out_shape=[jax.ShapeDtypeStruct((N_CHIP, 1) + g.shape[2:], g.dtype) for g in gs],
        in_specs=[hbm] * n, out_specs=[hbm] * n,
        scratch_shapes=[pltpu.SemaphoreType.DMA((n,)), pltpu.SemaphoreType.DMA((n,))],
    )(*gs)


def _chip_sum(g4, got, core, *, name):
    _, _, a, b = g4.shape

    def body(c_ref, g_ref, r_ref, h_ref):
        h_ref[0] = (g_ref[0, 0].astype(F32) + r_ref[0, 0].astype(F32)).astype(BF16)

    grid_spec = pltpu.PrefetchScalarGridSpec(
        num_scalar_prefetch=1, grid=(N_CHIP,),
        in_specs=[pl.BlockSpec((1, 1, a, b), lambda j, c: (j, c[0], 0, 0)),
                  pl.BlockSpec((1, 1, a, b), lambda j, c: (j, 0, 0, 0))],
        out_specs=[pl.BlockSpec((1, a, b), lambda j, c: (j, 0, 0))])
    return _pallas(
        body, name=name, grid_spec=grid_spec, out_shape=[jax.ShapeDtypeStruct((N_CHIP, a, b), BF16)],
        compiler_params=_params(dimension_semantics=("arbitrary",)),
    )(core, g4, got)[0]


def _chip_exchange(hs, *, name):
    n = len(hs)

    def body(*refs):
        for step in _chip_exchange_steps(refs[:n], refs[n:2 * n], *refs[2 * n:]):
            step()

    hbm = pl.BlockSpec(memory_space=pl.ANY)
    return _pallas(
        body, name=name, out_shape=_chip_exchange_shapes(hs), in_specs=[hbm] * n, out_specs=[hbm] * n,
        scratch_shapes=_chip_exchange_scratch(n),
    )(*hs)


def _chip_exchange_steps(h_refs, out_refs, send_sems, recv_sems):
    n = len(h_refs)
    x, y, c = _place()
    chips = [(1 - x, y), (x, 1 - y), (1 - x, 1 - y)]

    def copies():
        return [pltpu.make_async_remote_copy(
            src_ref=h_refs[i].at[2 * px + py], dst_ref=out_refs[i].at[k], send_sem=send_sems.at[k, i],
            recv_sem=recv_sems.at[k, i], device_id=(px, py, c), device_id_type=MESH)
            for k, (px, py) in enumerate(chips) for i in range(n)]

    def start():
        for cp in copies():
            cp.start()

    def finish():
        for cp in copies():
            cp.wait()

    return start, finish


def _chip_exchange_shapes(hs):
    return [jax.ShapeDtypeStruct((3,) + h.shape[1:], h.dtype) for h in hs]


def _chip_exchange_scratch(n):
    return [pltpu.SemaphoreType.DMA((3, n)), pltpu.SemaphoreType.DMA((3, n))]


def _sibling_sums(G, core, *, tag):
    g4 = []
    for n, g in G.items():
        gc = _chunks_of_full(n, g)
        g4.append(gc.reshape((N_CHIP, 2) + gc.shape[1:]))
    got = _sibling_exchange(g4, name="grads_to_sibling_" + tag)
    return [_chip_sum(g, r, core, name="chip_sum_" + n) for n, g, r in zip(G, g4, got)]


def _adam_math(w, g, m, v):
    m = ADAM_B1 * m + (1.0 - ADAM_B1) * g
    v = ADAM_B2 * v + (1.0 - ADAM_B2) * (g * g)
    m_hat = m / (1.0 - ADAM_B1 ** ADAM_STEP)
    v_hat = v / (1.0 - ADAM_B2 ** ADAM_STEP)
    delta = -ADAM_LR * (m_hat / (jnp.sqrt(v_hat) + ADAM_EPS) + ADAM_WD * w)
    return delta, m, v


def _adam_shard(hb, got, chip, w, m, v, *, name):
    _, r, c = w.shape
    _, a, b = hb.shape
    tr = r if r <= 512 else 256
    ta = tr if r // tr > 1 else a

    def body(j_ref, h_ref, r_ref, w_ref, m_ref, v_ref, g_ref, d_ref, nm_ref, nv_ref):
        parts = [h_ref[0], r_ref[0], r_ref[1], r_ref[2]]
        g = None
        for part in parts:
            part = part[:tr, :c].astype(F32)
            g = part if g is None else g + part
        d, nm, nv = _adam_math(w_ref[0], g, m_ref[0], v_ref[0])
        g_ref[0] = g
        d_ref[0] = d
        nm_ref[0] = nm
        nv_ref[0] = nv

    blk = pl.BlockSpec((1, tr, c), lambda i, j: (0, i, 0))
    grid_spec = pltpu.PrefetchScalarGridSpec(
        num_scalar_prefetch=1, grid=(r // tr,),
        in_specs=[pl.BlockSpec((1, ta, b), lambda i, j: (j[0], i, 0)),
                  pl.BlockSpec((3, ta, b), lambda i, j: (0, i, 0)), blk, blk, blk],
        out_specs=[blk] * 4)
    return _pallas(
        body, name=name, grid_spec=grid_spec, out_shape=[jax.ShapeDtypeStruct((1, r, c), F32)] * 4,
        compiler_params=_params(dimension_semantics=("arbitrary",)),
    )(chip, hb, got, w, m, v)


def _small_all_reduce_adam(gs, w, m, v):
    def body(g_ref, w_ref, m_ref, v_ref, sum_ref, d_ref, nm_ref, nv_ref, gather, send_sems, recv_sems):
        x, y, c = _place()
        my = 4 * x + 2 * y + c
        gather[my] = g_ref[...]
        cps = []
        for k in range(1, N_DEV):
            to = (x ^ (k >> 2), y ^ ((k >> 1) & 1), c ^ (k & 1))
            cps.append(pltpu.make_async_remote_copy(
                src_ref=g_ref, dst_ref=gather.at[my], send_sem=send_sems.at[k - 1], recv_sem=recv_sems.at[k - 1],
                device_id=to, device_id_type=MESH))
        for cp in cps:
            cp.start()
        for cp in cps:
            cp.wait()
        total = gather[0]
        for d in range(1, N_DEV):
            total = total + gather[d]
        dlt, nm, nv = _adam_math(w_ref[...], total, m_ref[...], v_ref[...])
        sum_ref[...] = total
        d_ref[...] = dlt
        nm_ref[...] = nm
        nv_ref[...] = nv

    vm = pl.BlockSpec(memory_space=pltpu.VMEM)
    return _pallas(
        body, name="small_all_reduce_adam", out_shape=[jax.ShapeDtypeStruct((SMALL_TOTAL, LANES), F32)] * 4,
        in_specs=[vm] * 4, out_specs=[vm] * 4,
        scratch_shapes=[pltpu.VMEM((N_DEV, SMALL_TOTAL, LANES), F32), pltpu.SemaphoreType.DMA((7,)),
                        pltpu.SemaphoreType.DMA((7,))],
            )(gs, w, m, v)


def kernel(x, p, norm_mix_g, w_in, hg_lb_logits, hg_onorm_g, fox_f_bias, fox_q_norm_g, fox_k_norm_g, w_branch_a, w_branch_b, w_out, norm_ffn_g, w_ffn_gate, w_ffn_up, w_ffn_down, norm_ple_g, w_ple_gate, w_ple_proj, loss_target, m_norm_mix_g, m_w_in, m_hg_lb_logits, m_hg_onorm_g, m_fox_f_bias, m_fox_q_norm_g, m_fox_k_norm_g, m_w_branch_a, m_w_branch_b, m_w_out, m_norm_ffn_g, m_w_ffn_gate, m_w_ffn_up, m_w_ffn_down, m_norm_ple_g, m_w_ple_gate, m_w_ple_proj, v_norm_mix_g, v_w_in, v_hg_lb_logits, v_hg_onorm_g, v_fox_f_bias, v_fox_q_norm_g, v_fox_k_norm_g, v_w_branch_a, v_w_branch_b, v_w_out, v_norm_ffn_g, v_w_ffn_gate, v_w_ffn_up, v_w_ffn_down, v_norm_ple_g, v_w_ple_gate, v_w_ple_proj):
    args = dict(locals())
    wts = {n: args[n] for n in BIG + SMALL}
    mom = {n: args["m_" + n] for n in BIG + SMALL}
    var = {n: args["v_" + n] for n in BIG + SMALL}
    sm = {n: wts[n] for n in SMALL}

    xi, yi, ci = _place()
    core = jnp.reshape(ci, (1,)).astype(jnp.int32)
    chip = jnp.reshape(2 * xi + yi, (1,)).astype(jnp.int32)
    chunks = [_chunk_of_shard(n, wts[n][0].astype(BF16)) for n in BIG]
    assert BIG[0] == "w_in"
    w_in_full = _full_of_chunks("w_in", _all_gather(chunks[:1], name="w_in_all_gather")[0])

    loss_blk, grad_x, gs, hb, got = _local_step(
        x[0], p[0, 0], loss_target[0], sm, {"w_in": w_in_full}, chunks[1:], core)

    g_big, d_big, nm_big, nv_big = {}, {}, {}, {}
    for n, h, r in zip(BIG, hb, got):
        g_big[n], d_big[n], nm_big[n], nv_big[n] = _adam_shard(h, r, chip, wts[n], mom[n], var[n], name="adam_" + n)

    s_sum, s_d, s_nm, s_nv = _small_all_reduce_adam(
        _pack_small(gs, loss_blk[0:1]), _pack_small(sm), _pack_small({n: mom[n] for n in SMALL}),
        _pack_small({n: var[n] for n in SMALL}))
    loss = s_sum[LOSS_ROW, 0]
    g_small, d_small, nm_small, nv_small = (_unpack_small(t, sm) for t in (s_sum, s_d, s_nm, s_nv))

    order = ["norm_mix_g", "w_in", "hg_lb_logits", "hg_onorm_g", "fox_f_bias", "fox_q_norm_g", "fox_k_norm_g",
             "w_branch_a", "w_branch_b", "w_out", "norm_ffn_g", "w_ffn_gate", "w_ffn_up", "w_ffn_down", "norm_ple_g",
             "w_ple_gate", "w_ple_proj"]
    outs = [loss, grad_x[None]]
    for big, small in ((g_big, g_small), (d_big, d_small), (nm_big, nm_small), (nv_big, nv_small)):
        outs += [big[n] if n in big else small[n] for n in order]
    return tuple(outs)
```

```python
import functools

import jax
import jax.numpy as jnp
from jax import lax
from jax.experimental import pallas as pl
from jax.experimental.pallas import tpu as pltpu

F32 = jnp.float32
BF16 = jnp.bfloat16

D_MODEL = 1024
PLE_DIM = 256
HG_HEADS = 4
HG_DK = 128
HG_CHUNK = 64
HG_SUB = 16
HG_W = HG_HEADS * HG_DK
FOX_HEADS = 8
FOX_DH = 64
FOX_W = FOX_HEADS * FOX_DH
D_FF = 2816
EPS = 1e-6
N_DEV = 8
N_CHIP = 4
LANES = 128
FOX_COLS = 3 * FOX_W + LANES
HG_COLS = 4 * HG_W
GATE_COLS = 2 * D_MODEL
IN_COLS = HG_COLS + 3 * FOX_W + FOX_HEADS + GATE_COLS
FOX_LOGICAL = 3 * FOX_W + FOX_HEADS
SEG = 2048
IN_PAD = 3 * SEG
IN_SHARD = IN_COLS // N_DEV
IN_SHARD_PAD = 768
FF_SHARD = D_FF // N_DEV
FF_SHARD_PAD = 384
FF_PAD = N_DEV * FF_SHARD_PAD
EXP_CLAMP = 80.0
LOG2E = 1.4426950408889634
BOUND_SLACK = 1.02
BOUND_LIMIT = 100.0

ADAM_LR = 0.001
ADAM_B1 = 0.9
ADAM_B2 = 0.999
ADAM_EPS = 1e-08
ADAM_WD = 0.01
ADAM_STEP = 10

MESH = pl.DeviceIdType.MESH
VMEM_LIMIT = 56 * 1024 * 1024
ROW_BLOCK = 512

BIG = ["w_in", "w_branch_a", "w_branch_b", "w_out", "w_ffn_gate", "w_ffn_up", "w_ffn_down",
       "w_ple_gate", "w_ple_proj"]
BIG_SHAPE = {
    "w_in": (D_MODEL, IN_COLS, 1), "w_branch_a": (HG_W, D_MODEL, 1), "w_branch_b": (FOX_W, D_MODEL, 1),
    "w_out": (D_MODEL, D_MODEL, 0), "w_ffn_gate": (D_MODEL, D_FF, 1), "w_ffn_up": (D_MODEL, D_FF, 1),
    "w_ffn_down": (D_FF, D_MODEL, 0), "w_ple_gate": (D_MODEL, D_MODEL, 0), "w_ple_proj": (PLE_DIM, D_MODEL, 1),
}

SMALL = ["norm_mix_g", "hg_lb_logits", "hg_onorm_g", "fox_f_bias", "fox_q_norm_g", "fox_k_norm_g",
         "norm_ffn_g", "norm_ple_g"]
SMALL_ROWS = {"norm_mix_g": 8, "hg_lb_logits": 8, "hg_onorm_g": 1, "fox_f_bias": 1, "fox_q_norm_g": 1,
              "fox_k_norm_g": 1, "norm_ffn_g": 8, "norm_ple_g": 8}
SMALL_TOTAL = 40
LOSS_ROW = 36


def _pallas(body, **kw):
    return pl.pallas_call(body, **kw)


def _params(**kw):
    return pltpu.CompilerParams(vmem_limit_bytes=VMEM_LIMIT, **kw)


def _pick(n, target):
    if n <= target:
        return n
    best = None
    for t in range(LANES, target + 1, LANES):
        if n % t == 0:
            best = t
    assert best is not None, (n, target)
    return best


def _dot(a, b, ca, cb):
    return lax.dot_general(a, b, (((ca,), (cb,)), ((), ())), preferred_element_type=F32)


def _split_dot(mat, x, ca, cb, terms=2, mat_first=True):
    acc = None
    rem = x
    for _ in range(terms):
        part = rem.astype(BF16)
        rem = rem - part.astype(F32)
        p = _dot(mat, part, ca, cb) if mat_first else _dot(part, mat, ca, cb)
        acc = p if acc is None else acc + p
    return acc


def _sigmoid(x):
    return 1.0 / (1.0 + jnp.exp(-x))


def _iota(shape, dim):
    return lax.broadcasted_iota(jnp.int32, shape, dim)


def _matmul(a, b, *, name, ta=False, tb=False, out_dtype=F32, add=None, exchange=None):
    (K, M) = a.shape if ta else a.shape[::-1]
    (N, Kb) = b.shape if tb else b.shape[::-1]
    assert K == Kb, (a.shape, b.shape, ta, tb)
    if ta:
        tm, tn, tk = _pick(M, 2 * ROW_BLOCK), _pick(N, 2 * ROW_BLOCK), _pick(K, 4 * ROW_BLOCK)
    else:
        tm, tn, tk = _pick(M, 2 * ROW_BLOCK), _pick(N, 2048), _pick(K, 3072)
    nk = K // tk
    use_scratch = nk > 1 and out_dtype != F32

    n_in = 2 + (add is not None)
    hs = list(exchange or [])
    n_x = len(hs)
    grid = (M // tm, N // tn, nk)

    def body(*refs):
        refs = list(refs)
        a_ref, b_ref = refs[:2]
        add_ref = refs[2] if add is not None else None
        o_ref = refs[n_in + n_x]
        k = pl.program_id(2)
        if n_x:
            x_start, x_finish = _chip_exchange_steps(
                refs[n_in:n_in + n_x], refs[n_in + n_x + 1:n_in + 2 * n_x + 1], *refs[-2:])
            at = [pl.program_id(d) for d in range(3)]

            @pl.when(jnp.logical_and(at[0] == 0, jnp.logical_and(at[1] == 0, at[2] == 0)))
            def _():
                x_start()
        p = _dot(a_ref[...].astype(BF16), b_ref[...].astype(BF16), 0 if ta else 1, 1 if tb else 0)

        def with_add(r):
            return r if add is None else r + add_ref[...].astype(F32)

        if nk == 1:
            o_ref[...] = with_add(p).astype(out_dtype)
        elif not use_scratch:
            @pl.when(k == 0)
            def _():
                o_ref[...] = with_add(p)

            @pl.when(k > 0)
            def _():
                o_ref[...] += p
        else:
            acc_ref = refs[n_in + 2 * n_x + 1]

            @pl.when(k == 0)
            def _():
                acc_ref[...] = p

            @pl.when(k > 0)
            def _():
                acc_ref[...] += p

            @pl.when(k == nk - 1)
            def _():
                o_ref[...] = with_add(acc_ref[...]).astype(out_dtype)

        if n_x:
            @pl.when(jnp.logical_and(at[0] == grid[0] - 1, jnp.logical_and(at[1] == grid[1] - 1, at[2] == nk - 1)))
            def _():
                x_finish()

    a_spec = pl.BlockSpec((tk, tm), lambda i, j, k: (k, i)) if ta else pl.BlockSpec((tm, tk), lambda i, j, k: (i, k))
    b_spec = pl.BlockSpec((tn, tk), lambda i, j, k: (j, k)) if tb else pl.BlockSpec((tk, tn), lambda i, j, k: (k, j))
    o_spec = pl.BlockSpec((tm, tn), lambda i, j, k: (i, j))
    hbm = pl.BlockSpec(memory_space=pl.ANY)
    in_specs = [a_spec, b_spec] + ([o_spec] if add is not None else []) + [hbm] * n_x
    args = (a, b) + ((add,) if add is not None else ()) + tuple(hs)
    res = _pallas(
        body, name=name, grid=grid, in_specs=in_specs, out_specs=[o_spec] + [hbm] * n_x,
        out_shape=[jax.ShapeDtypeStruct((M, N), out_dtype)] + (_chip_exchange_shapes(hs) if n_x else []),
        scratch_shapes=([pltpu.VMEM((tm, tn), F32)] if use_scratch else []) + (_chip_exchange_scratch(n_x) if n_x else []),
        compiler_params=_params(dimension_semantics=("arbitrary",) * 3),
    )(*args)
    return (res[0], list(res[1:])) if n_x else res[0]


def _row_map(nb, reverse, seg):
    if reverse:
        return lambda i: (nb - 1 - i, seg)
    return lambda i: (i, seg)


def _row_call(body, *, name, T, ins, outs, acc_outs=(), tm=ROW_BLOCK, reverse=False):
    tm = min(tm, T)
    nb = T // tm
    in_specs, args = [], []
    for arr, how in ins:
        args.append(arr)
        if how is True:
            in_specs.append(pl.BlockSpec((tm, arr.shape[1]), _row_map(nb, reverse, 0)))
        elif how is False:
            in_specs.append(pl.BlockSpec(arr.shape, lambda i, _n=arr.ndim: (0,) * _n))
        else:
            in_specs.append(pl.BlockSpec((tm, SEG), _row_map(nb, reverse, how[0])))
    out_specs, out_shape = [], []
    for o in outs:
        c, dt = o[0], o[1]
        total, seg = o[2] if len(o) > 2 else (c, 0)
        out_specs.append(pl.BlockSpec((tm, c), _row_map(nb, reverse, seg)))
        out_shape.append(jax.ShapeDtypeStruct((T, total), dt))
    for shp, dt in acc_outs:
        out_specs.append(pl.BlockSpec(shp, lambda i, _n=len(shp): (0,) * _n))
        out_shape.append(jax.ShapeDtypeStruct(shp, dt))
    return _pallas(body, name=name, grid=(nb,), in_specs=in_specs, out_specs=out_specs, out_shape=out_shape,
                   compiler_params=_params(dimension_semantics=("arbitrary",)))(*args)


def _rms_fwd(x, g, *, name):
    T = x.shape[0]

    def body(x_ref, g_ref, h_ref):
        xv = x_ref[...]
        rstd = lax.rsqrt(jnp.mean(xv * xv, axis=-1, keepdims=True) + EPS)
        h_ref[...] = (xv * rstd * g_ref[...]).astype(BF16)

    return _row_call(body, name=name, T=T, ins=[(x, True), (g, False)], outs=[(D_MODEL, BF16)])[0]


def _rms_bwd(x, g, dh, dres, *, name):
    T = x.shape[0]

    def body(x_ref, g_ref, dh_ref, dres_ref, dx_ref, dg_ref):
        xv = x_ref[...]
        rstd = lax.rsqrt(jnp.mean(xv * xv, axis=-1, keepdims=True) + EPS)
        xh = xv * rstd
        dhv = dh_ref[...]
        part = jnp.sum(dhv * xh, axis=0, keepdims=True)

        @pl.when(pl.program_id(0) == 0)
        def _():
            dg_ref[...] = part

        @pl.when(pl.program_id(0) > 0)
        def _():
            dg_ref[...] += part

        dxh = dhv * g_ref[...]
        dx_ref[...] = rstd * (dxh - xh * jnp.mean(dxh * xh, axis=-1, keepdims=True)) + dres_ref[...]

    return _row_call(body, name=name, T=T, ins=[(x, True), (g, False), (dh, True), (dres, True)],
                     outs=[(D_MODEL, F32)], acc_outs=[((1, D_MODEL), F32)])


def _merge_fwd(ua, ub, zg):
    def body(ua_ref, ub_ref, zg_ref, m_ref):
        ga = _sigmoid(zg_ref[:, :D_MODEL])
        gb = _sigmoid(zg_ref[:, D_MODEL:])
        m_ref[...] = (ga * ua_ref[...] + gb * ub_ref[...]).astype(BF16)

    return _row_call(body, name="merge_fwd", T=ua.shape[0], ins=[(ua, True), (ub, True), (zg, (2,))],
                     outs=[(D_MODEL, BF16)])[0]


def _merge_bwd(dm, ua, ub, zg):
    def body(dm_ref, ua_ref, ub_ref, zg_ref, dua_ref, dub_ref, dzg_ref):
        ga = _sigmoid(zg_ref[:, :D_MODEL])
        gb = _sigmoid(zg_ref[:, D_MODEL:])
        dmv = dm_ref[...]
        dua_ref[...] = (dmv * ga).astype(BF16)
        dub_ref[...] = (dmv * gb).astype(BF16)
        dzg_ref[:, :D_MODEL] = (dmv * ua_ref[...] * ga * (1.0 - ga)).astype(BF16)
        dzg_ref[:, D_MODEL:] = (dmv * ub_ref[...] * gb * (1.0 - gb)).astype(BF16)

    return _row_call(body, name="merge_bwd", T=dm.shape[0], ins=[(dm, True), (ua, True), (ub, True), (zg, (2,))],
                     outs=[(D_MODEL, BF16), (D_MODEL, BF16), (SEG, BF16, (IN_PAD, 2))])


def _swiglu_fwd(a, b):
    def body(a_ref, b_ref, o_ref):
        av = a_ref[...].astype(F32)
        o_ref[...] = (av * _sigmoid(av) * b_ref[...].astype(F32)).astype(BF16)

    return _row_call(body, name="swiglu_fwd", T=a.shape[0], ins=[(a, True), (b, True)], outs=[(a.shape[1], BF16)])[0]


def _swiglu_bwd(a, b, dact):
    def body(a_ref, b_ref, d_ref, da_ref, db_ref):
        av = a_ref[...].astype(F32)
        bv = b_ref[...].astype(F32)
        dv = d_ref[...]
        sg = _sigmoid(av)
        da_ref[...] = (dv * bv * sg * (1.0 + av * (1.0 - sg))).astype(BF16)
        db_ref[...] = (dv * av * sg).astype(BF16)

    return _row_call(body, name="swiglu_bwd", T=a.shape[0], ins=[(a, True), (b, True), (dact, True)],
                     outs=[(a.shape[1], BF16), (a.shape[1], BF16)])


def _ple_loss(x2, sp, pp, tgt):
    def body(x_ref, sp_ref, pp_ref, t_ref, dy_ref, dsp_ref, dpp_ref, loss_ref):
        gp = _sigmoid(sp_ref[...])
        ppv = pp_ref[...]
        err = x_ref[...] + gp * ppv - t_ref[...]
        part = 0.5 * jnp.sum(jnp.mean(err * err, axis=-1, keepdims=True), axis=0, keepdims=True)
        part = jnp.broadcast_to(part, loss_ref.shape)

        @pl.when(pl.program_id(0) == 0)
        def _():
            loss_ref[...] = part

        @pl.when(pl.program_id(0) > 0)
        def _():
            loss_ref[...] += part

        dy = err * (1.0 / D_MODEL)
        dy_ref[...] = dy
        dsp_ref[...] = (dy * ppv * gp * (1.0 - gp)).astype(BF16)
        dpp_ref[...] = (dy * gp).astype(BF16)

    return _row_call(body, name="ple_loss", T=x2.shape[0], ins=[(x2, True), (sp, True), (pp, True), (tgt, True)],
                     outs=[(D_MODEL, F32), (D_MODEL, BF16), (D_MODEL, BF16)], acc_outs=[((8, LANES), F32)])


def _hg_consts():
    C = HG_CHUNK
    r, c = _iota((C, C), 0), _iota((C, C), 1)
    tri = (c <= r)
    same = (r // HG_SUB) == (c // HG_SUB)
    return tri, (tri & same)


def _hg_chunk_fwd(q, f, lb, tri_b, sub_b):
    sgq = _sigmoid(q)
    qt = q * sgq
    sg = _sigmoid(f)
    fg = lb + (1.0 - lb) * sg
    kf = (1.0 - lb) * (1.0 - sg)
    logf = jnp.log(fg)
    b = _split_dot(tri_b, logf, 1, 0)
    w = _split_dot(sub_b, logf, 1, 0)
    return sgq, qt, sg, fg, kf, b, w


def _hg_scores(qs_b, kf, b, row):
    C, S = HG_CHUNK, HG_SUB
    parts, ks = [], []
    for blk in range(C // S):
        ref = jnp.zeros_like(b[0:1]) if blk == 0 else b[blk * S - 1:blk * S]
        e = jnp.exp(jnp.minimum(ref - b, EXP_CLAMP))
        e = jnp.where(row < (blk + 1) * S, e, 0.0)
        k_b = (kf * e).astype(BF16)
        ks.append((e, k_b))
        parts.append(_dot(qs_b[blk * S:(blk + 1) * S], k_b, 1, 1))
    return jnp.concatenate(parts, axis=0), ks


def _hgrn_fwd(z, lb_logits, gain):
    T = z.shape[0]
    RB = min(ROW_BLOCK, T)
    nb, cpb = T // RB, RB // HG_CHUNK
    C, DK = HG_CHUNK, HG_DK

    def body(z_ref, lg_ref, g_ref, o_ref, y_ref, st_ref, s_ref):
        @pl.when(pl.program_id(0) == 0)
        def _():
            s_ref[...] = jnp.zeros_like(s_ref)

        lg = lg_ref[...]
        lb_all = 1.0 / (1.0 + jnp.exp(lg[1:2] - lg[0:1]))
        gain_v = g_ref[...]
        tri, sub = _hg_consts()
        tri_b, sub_b = tri.astype(BF16), sub.astype(BF16)
        row = _iota((C, DK), 0)

        def chunk(ci, carry):
            r0 = pl.multiple_of(ci * C, C)
            rows = pl.ds(r0, C)
            for h in range(HG_HEADS):
                cs = slice(h * DK, (h + 1) * DK)
                q = z_ref[rows, cs]
                f = z_ref[rows, HG_W + h * DK:HG_W + (h + 1) * DK]
                v = z_ref[rows, 2 * HG_W + h * DK:2 * HG_W + (h + 1) * DK]
                g = z_ref[rows, 3 * HG_W + h * DK:3 * HG_W + (h + 1) * DK]
                lb = lb_all[:, cs]
                _, qt, _, _, kf, b, w = _hg_chunk_fwd(q, f, lb, tri_b, sub_b)
                st = s_ref[h]
                st_ref[pl.ds(pl.multiple_of((ci * HG_HEADS + h) * DK, DK), DK), :] = st
                v_b = v.astype(BF16)
                qs_b = (qt * jnp.exp(w)).astype(BF16)
                a, _ = _hg_scores(qs_b, kf, b, row)
                a = jnp.where(tri, a, 0.0)
                qd = qt * jnp.exp(b)
                o = _dot(qd.astype(BF16), st.astype(BF16), 1, 1) + _dot(a.astype(BF16), v_b, 1, 0)
                bl = b[C - 1:C]
                kd = kf * jnp.exp(bl - b)
                s_ref[h] = st * jnp.exp(bl) + _dot(v_b, kd.astype(BF16), 0, 0)
                o_ref[rows, cs] = o
                rstd = lax.rsqrt(jnp.mean(o * o, axis=-1, keepdims=True) + EPS)
                y_ref[rows, cs] = (o * rstd * gain_v * (g * _sigmoid(g))).astype(BF16)
            return carry

        lax.fori_loop(0, cpb, chunk, 0)

    return _pallas(
        body, name="hgrn_fwd", grid=(nb,),
        in_specs=[pl.BlockSpec((RB, HG_COLS), lambda i: (i, 0)), pl.BlockSpec((2, HG_W), lambda i: (0, 0)),
                  pl.BlockSpec((1, DK), lambda i: (0, 0))],
        out_specs=[pl.BlockSpec((RB, HG_W), lambda i: (i, 0)), pl.BlockSpec((RB, HG_W), lambda i: (i, 0)),
                   pl.BlockSpec((cpb * HG_HEADS * DK, DK), lambda i: (i, 0))],
        out_shape=[jax.ShapeDtypeStruct((T, HG_W), F32), jax.ShapeDtypeStruct((T, HG_W), BF16),
                   jax.ShapeDtypeStruct((T // C * HG_HEADS * DK, DK), F32)],
        scratch_shapes=[pltpu.VMEM((HG_HEADS, DK, DK), F32)],
        compiler_params=_params(dimension_semantics=("arbitrary",)),
    )(z, lb_logits, gain)


def _hgrn_bwd(z, o_raw, dy, states, lb_logits, gain, dz_buf):
    T = z.shape[0]
    RB = min(ROW_BLOCK, T)
    nb, cpb = T // RB, RB // HG_CHUNK
    C, DK, S = HG_CHUNK, HG_DK, HG_SUB

    def body(z_ref, o_ref, dy_ref, st_ref, lg_ref, g_ref, _buf_ref, dz_ref, dlg_ref, dg_ref, ds_ref, dlb_ref):
        step = pl.program_id(0)

        @pl.when(step == 0)
        def _():
            ds_ref[...] = jnp.zeros_like(ds_ref)
            dlb_ref[...] = jnp.zeros_like(dlb_ref)
            dg_ref[...] = jnp.zeros_like(dg_ref)

        lg = lg_ref[...]
        lb_all = 1.0 / (1.0 + jnp.exp(lg[1:2] - lg[0:1]))
        gain_v = g_ref[...]
        tri, sub = _hg_consts()
        tri_b, sub_b = tri.astype(BF16), sub.astype(BF16)
        row = _iota((C, DK), 0)

        def chunk(cj, carry):
            ci = cpb - 1 - cj
            r0 = pl.multiple_of(ci * C, C)
            rows = pl.ds(r0, C)
            for h in range(HG_HEADS):
                cs = slice(h * DK, (h + 1) * DK)
                q = z_ref[rows, cs]
                f = z_ref[rows, HG_W + h * DK:HG_W + (h + 1) * DK]
                v = z_ref[rows, 2 * HG_W + h * DK:2 * HG_W + (h + 1) * DK]
                g = z_ref[rows, 3 * HG_W + h * DK:3 * HG_W + (h + 1) * DK]
                lb = lb_all[:, cs]
                sgq, qt, sg, fg, kf, b, w = _hg_chunk_fwd(q, f, lb, tri_b, sub_b)
                st = st_ref[pl.ds(pl.multiple_of((ci * HG_HEADS + h) * DK, DK), DK), :]
                dst = ds_ref[h]
                o = o_ref[rows, cs]
                dyv = dy_ref[rows, cs]
                rstd = lax.rsqrt(jnp.mean(o * o, axis=-1, keepdims=True) + EPS)
                n = o * rstd
                sgg = _sigmoid(g)
                t1 = dyv * (g * sgg)
                dg_ref[...] += jnp.sum(t1 * n, axis=0, keepdims=True)
                dn = t1 * gain_v
                do = rstd * (dn - n * jnp.mean(dn * n, axis=-1, keepdims=True))
                dgate = dyv * n * gain_v * sgg * (1.0 + g * (1.0 - sgg))
                do_b = do.astype(BF16)
                v_b = v.astype(BF16)
                ew = jnp.exp(w)
                qs_b = (qt * ew).astype(BF16)
                a, ks = _hg_scores(qs_b, kf, b, row)
                a = jnp.where(tri, a, 0.0)
                eb = jnp.exp(b)
                qd = qt * eb
                bl = b[C - 1:C]
                ebl = jnp.exp(bl)
                ekd = jnp.exp(bl - b)
                kd = kf * ekd
                kd_b = kd.astype(BF16)
                qd_b = qd.astype(BF16)
                dst_b = dst.astype(BF16)
                dqd = _dot(do_b, st.astype(BF16), 1, 0)
                da = jnp.where(tri, _dot(do_b, v_b, 1, 1), 0.0)
                dv = _dot(a.astype(BF16), do_b, 0, 0) + _dot(kd_b, dst_b, 1, 1)
                dkd = _dot(v_b, dst_b, 1, 0)
                ds_ref[h] = dst * ebl + _dot(do_b, qd_b, 0, 0)
                dkd_kd = dkd * kd_b.astype(F32)
                dbl = ebl * jnp.sum(dst * st, axis=0, keepdims=True) + jnp.sum(dkd_kd, axis=0, keepdims=True)
                da_b = da.astype(BF16)
                dqs_parts = []
                dk_in = jnp.zeros((C, DK), F32)
                db_k = jnp.zeros((C, DK), F32)
                for blk in range(C // S):
                    e, k_b = ks[blk]
                    da_blk = da_b[blk * S:(blk + 1) * S]
                    dqs_parts.append(_dot(da_blk, k_b, 1, 0))
                    dks = _dot(da_blk, qs_b[blk * S:(blk + 1) * S], 0, 0)
                    dk_in = dk_in + dks * e
                    db_k = db_k + dks * k_b.astype(F32)
                dqs = jnp.concatenate(dqs_parts, axis=0)
                dqt_in = dqs * ew
                db = qs_b.astype(F32) * dqs - db_k + dqd * qd_b.astype(F32) - dkd_kd
                db = db + jnp.where(row == C - 1, dbl, 0.0)
                dlogf = _split_dot(tri_b, db, 0, 0)
                dqt = dqt_in + dqd * eb
                dkf = dk_in + dkd * ekd
                dfg = dlogf / fg - dkf
                dlb_ref[:, cs] += jnp.sum(dfg * (1.0 - sg), axis=0, keepdims=True)
                dz_ref[rows, cs] = (dqt * sgq * (1.0 + q * (1.0 - sgq))).astype(BF16)
                dz_ref[rows, HG_W + h * DK:HG_W + (h + 1) * DK] = (dfg * (1.0 - lb) * sg * (1.0 - sg)).astype(BF16)
                dz_ref[rows, 2 * HG_W + h * DK:2 * HG_W + (h + 1) * DK] = dv.astype(BF16)
                dz_ref[rows, 3 * HG_W + h * DK:3 * HG_W + (h + 1) * DK] = dgate.astype(BF16)
            return carry

        lax.fori_loop(0, cpb, chunk, 0)

        @pl.when(step == nb - 1)
        def _():
            d0 = dlb_ref[...] * lb_all * (1.0 - lb_all)
            dlg_ref[0:1, :] = d0
            dlg_ref[1:2, :] = -d0

    rev = lambda i: (nb - 1 - i, 0)
    fix = lambda i: (0, 0)
    return _pallas(
        body, name="hgrn_bwd", grid=(nb,),
        in_specs=[pl.BlockSpec((RB, HG_COLS), rev), pl.BlockSpec((RB, HG_W), rev), pl.BlockSpec((RB, HG_W), rev),
                  pl.BlockSpec((cpb * HG_HEADS * DK, DK), rev), pl.BlockSpec((2, HG_W), fix),
                  pl.BlockSpec((1, DK), fix), pl.BlockSpec(memory_space=pl.ANY)],
        out_specs=[pl.BlockSpec((RB, HG_COLS), rev), pl.BlockSpec((2, HG_W), fix), pl.BlockSpec((1, DK), fix)],
        out_shape=[jax.ShapeDtypeStruct(dz_buf.shape, BF16), jax.ShapeDtypeStruct((2, HG_W), F32),
                   jax.ShapeDtypeStruct((1, DK), F32)],
        scratch_shapes=[pltpu.VMEM((HG_HEADS, DK, DK), F32), pltpu.VMEM((1, HG_W), F32)],
        input_output_aliases={6: 0},
        compiler_params=_params(dimension_semantics=("arbitrary",)),
    )(z, o_raw, dy, states, lb_logits, gain, dz_buf)


def _head_ones():
    r, c = _iota((FOX_W, FOX_W), 0), _iota((FOX_W, FOX_W), 1)
    return ((r // FOX_DH) == (c // FOX_DH)).astype(BF16)


def _log_sigmoid(x):
    return jnp.minimum(x, 0.0) - jnp.log(1.0 + jnp.exp(-jnp.abs(x)))


def _fox_prep(z, bias, qg, kg):
    T = z.shape[0]
    tm = min(ROW_BLOCK, T)
    nb = T // tm

    def body(z_ref, b_ref, qg_ref, kg_ref, q_ref, k_ref, v_ref, qa_ref, ka_ref, qn_ref, c2_ref, st_ref, carry_ref):
        @pl.when(pl.program_id(0) == 0)
        def _():
            carry_ref[...] = jnp.zeros_like(carry_ref)

        ones = _head_ones()
        normed = []
        for src, g_ref in ((0, qg_ref), (1, kg_ref)):
            xv = z_ref[:, src * FOX_W:(src + 1) * FOX_W]
            ms = _split_dot(ones, xv * xv, 1, 0, mat_first=False) * (1.0 / FOX_DH)
            normed.append(xv * lax.rsqrt(ms + EPS) * g_ref[...])
        qn, kn = normed
        q_ref[...] = (qn * FOX_DH ** -0.5).astype(BF16)
        k_b = kn.astype(BF16)
        k_ref[...] = k_b
        v_ref[...] = z_ref[:, 2 * FOX_W:3 * FOX_W].astype(BF16)
        logf = _log_sigmoid(z_ref[:, 3 * FOX_W:FOX_COLS] + b_ref[...])
        r, c = _iota((tm, tm), 0), _iota((tm, tm), 1)
        tri_b = (c <= r).astype(BF16)
        cum = _split_dot(tri_b, logf, 1, 0, terms=3) + carry_ref[...]
        carry_ref[...] = cum[tm - 1:tm]
        c2 = cum * LOG2E
        hi = c2.astype(BF16)
        rem = c2 - hi.astype(F32)
        mid = rem.astype(BF16)
        lo = (rem - mid.astype(F32)).astype(BF16)
        hrow, col = _iota((LANES, 2 * FOX_W), 0), _iota((LANES, 2 * FOX_W), 1)
        base = hrow * LANES + jnp.where(hrow % 2 == 0, FOX_DH, 0)
        placed = None
        for t, part in enumerate((hi, mid, lo)):
            place = jnp.logical_and(col == base + t, hrow < FOX_HEADS).astype(BF16)
            term = _dot(part, place, 1, 0)
            placed = term if placed is None else placed + term
        colw = _iota((tm, 2 * FOX_W), 1)
        head, lane = colw // LANES, colw % LANES
        own = (lane < FOX_DH) == (head % 2 == 0)
        other = jnp.where(head % 2 == 0, lane - FOX_DH, lane)
        ones_q = jnp.where(jnp.logical_and(other >= 0, other < 3), -1.0, 0.0)
        q2 = (qn * (FOX_DH ** -0.5 * LOG2E)).astype(BF16)
        q_exp = jnp.concatenate([q2[:, (h // 2) * LANES:(h // 2 + 1) * LANES] for h in range(FOX_HEADS)], axis=1)
        k_exp = jnp.concatenate([k_b[:, (h // 2) * LANES:(h // 2 + 1) * LANES] for h in range(FOX_HEADS)], axis=1)
        qa_ref[...] = jnp.where(own, q_exp, ones_q.astype(BF16))
        ka_ref[...] = jnp.where(own, k_exp, placed.astype(BF16))
        hsrc, hdst = _iota((FOX_W, 2 * FOX_W), 0) // FOX_DH, _iota((FOX_W, 2 * FOX_W), 1) // LANES
        spread = (hsrc == hdst).astype(BF16)
        q2f, kf = q2.astype(F32), k_b.astype(F32)
        qnorm = jnp.sqrt(_dot((q2f * q2f).astype(BF16), spread, 1, 0)) * BOUND_SLACK
        knorm = jnp.sqrt(_dot((kf * kf).astype(BF16), spread, 1, 0)) * BOUND_SLACK
        lane_head = (_iota((LANES, 2 * FOX_W), 0) == _iota((LANES, 2 * FOX_W), 1) // LANES).astype(BF16)
        c2r = _dot(hi, lane_head, 1, 0) + _dot(mid, lane_head, 1, 0) + _dot(lo, lane_head, 1, 0)
        qn_ref[...] = qnorm
        c2_ref[...] = c2r
        row8 = _iota((8, 2 * FOX_W), 0)
        kmax = jnp.max(knorm, axis=0, keepdims=True)
        qmax = jnp.max(qnorm, axis=0, keepdims=True)
        st_ref[0] = jnp.where(row8 == 0, kmax, jnp.where(row8 == 1, c2r[tm - 1:tm], jnp.where(row8 == 2, qmax, 0.0)))

    wide = pl.BlockSpec((tm, 2 * FOX_W), lambda i: (i, 0))
    return _pallas(
        body, name="fox_prep", grid=(nb,),
        in_specs=[pl.BlockSpec((tm, SEG), lambda i: (i, 1)), pl.BlockSpec((1, LANES), lambda i: (0, 0)),
                  pl.BlockSpec((1, FOX_W), lambda i: (0, 0)), pl.BlockSpec((1, FOX_W), lambda i: (0, 0))],
        out_specs=[pl.BlockSpec((tm, FOX_W), lambda i: (i, 0))] * 3 + [wide] * 4
        + [pl.BlockSpec((1, 8, 2 * FOX_W), lambda i: (i, 0, 0))],
        out_shape=[jax.ShapeDtypeStruct((T, FOX_W), BF16)] * 3 + [jax.ShapeDtypeStruct((T, 2 * FOX_W), BF16)] * 2
        + [jax.ShapeDtypeStruct((T, 2 * FOX_W), F32)] * 2 + [jax.ShapeDtypeStruct((nb, 8, 2 * FOX_W), F32)],
        scratch_shapes=[pltpu.VMEM((1, LANES), F32)],
        compiler_params=_params(dimension_semantics=("arbitrary",)),
    )(z, bias, qg, kg)


def _fox_fwd(qa, ka, vb, qnorm, c2r, stats, blocks):
    T = qa.shape[0]
    tq = min(ROW_BLOCK, T)
    nq = T // tq
    NEG = -1e30
    n = len(blocks)
    n_pairs = FOX_HEADS // 2

    n_in = 7

    def body(*refs):
        q_ref, k_ref, v_ref, qn_ref, c2_ref, st_ref, tight_ref = refs[:n_in]
        o_ref, lse_ref = refs[n_in + n:n_in + n + 2]
        m_sc, l_sc, acc_sc = refs[n_in + 2 * n + 2:n_in + 2 * n + 5]
        pr, qi, ki = pl.program_id(0), pl.program_id(1), pl.program_id(2)
        g_start, g_forward, g_finish = _gather_steps(
            refs[n_in:n_in + n], refs[n_in + n + 2:n_in + 2 * n + 2], *refs[n_in + 2 * n + 5:])
        row_start = jnp.logical_and(qi == 0, ki == 0)

        @pl.when(jnp.logical_and(pr == 0, row_start))
        def _():
            g_start()

        @pl.when(jnp.logical_and(pr == n_pairs // 2, row_start))
        def _():
            g_forward()

        @pl.when(ki == 0)
        def _():
            m_sc[...] = jnp.full_like(m_sc, NEG)
            l_sc[...] = jnp.zeros_like(l_sc)
            acc_sc[...] = jnp.zeros_like(acc_sc)

        lane = _iota((tq, LANES), 1)

        def block(masked, bounded):
            vv = v_ref[...]
            for hh in range(2):
                hs = slice(hh * LANES, (hh + 1) * LANES)
                s = _dot(q_ref[:, hs], k_ref[:, hs], 1, 1)
                tiles = [s[:, j * LANES:(j + 1) * LANES] for j in range(tq // LANES)]
                if masked:
                    row, col = _iota((tq, LANES), 0), _iota((tq, LANES), 1)
                    tiles = [jnp.where(row >= col + j * LANES, t, NEG) for j, t in enumerate(tiles)]
                m_old = m_sc[hh]
                if bounded:
                    c_ref_key = c2_ref[:, hs] if masked else st_ref[0, 1:2, hs]
                    top = qn_ref[:, hs] * st_ref[0, 0:1, hs] - c_ref_key
                else:
                    top = jnp.broadcast_to(jnp.max(functools.reduce(jnp.maximum, tiles), axis=-1, keepdims=True),
                                           (tq, LANES))
                m_new = jnp.maximum(m_old, top)
                alpha = jnp.exp2(m_old - m_new)
                ps = [jnp.exp2(t - m_new) for t in tiles]
                l_sc[hh] = alpha * l_sc[hh] + functools.reduce(jnp.add, ps)
                m_sc[hh] = m_new
                p_b = jnp.concatenate([p.astype(BF16) for p in ps], axis=1)
                acc_sc[hh] = alpha * acc_sc[hh] + _dot(p_b, vv, 1, 0)

        tight = tight_ref[pr, qi] > 0.5
        for masked in (False, True):
            here = (ki == qi) if masked else (ki < qi)
            for bounded in (False, True):
                @pl.when(jnp.logical_and(here, tight == bounded))
                def _(masked=masked, bounded=bounded):
                    block(masked, bounded)

        @pl.when(ki == qi)
        def _():
            l0 = jnp.sum(l_sc[0], axis=-1, keepdims=True)
            l1 = jnp.sum(l_sc[1], axis=-1, keepdims=True)
            o_ref[...] = jnp.where(lane < FOX_DH, acc_sc[0] * (1.0 / l0), acc_sc[1] * (1.0 / l1))
            lse_ref[:, :LANES] = m_sc[0] + jnp.log2(l0)
            lse_ref[:, LANES:] = m_sc[1] + jnp.log2(l1)

        @pl.when(jnp.logical_and(pr == n_pairs - 1, jnp.logical_and(qi == nq - 1, ki == nq - 1)))
        def _():
            g_finish()

    kmap = lambda p, i, j: (jnp.minimum(i, j), p)
    hbm = pl.BlockSpec(memory_space=pl.ANY)
    qmax = stats[:, 2, ::LANES]
    kmax = jnp.max(stats[:, 0, ::LANES], axis=0)
    tight = (2.0 * qmax * kmax[None, :] < BOUND_LIMIT).astype(F32)
    tight = jnp.minimum(tight[:, 0::2], tight[:, 1::2]).T
    qrow = pl.BlockSpec((tq, 2 * LANES), lambda p, i, j: (i, p))
    res = _pallas(
        body, name="fox_fwd", grid=(n_pairs, nq, nq),
        in_specs=[qrow, pl.BlockSpec((tq, 2 * LANES), kmap), pl.BlockSpec((tq, LANES), kmap), qrow, qrow,
                  pl.BlockSpec((1, 8, 2 * LANES), lambda p, i, j: (jnp.minimum(i, j), 0, p)),
                  pl.BlockSpec(memory_space=pltpu.SMEM)] + [hbm] * n,
        out_specs=[pl.BlockSpec((tq, LANES), lambda p, i, j: (i, p)), qrow] + [hbm] * n,
        out_shape=[jax.ShapeDtypeStruct((T, FOX_W), F32), jax.ShapeDtypeStruct((T, 2 * FOX_W), F32)]
        + [jax.ShapeDtypeStruct((N_DEV,) + b.shape, b.dtype) for b in blocks],
        scratch_shapes=[pltpu.VMEM((2, tq, LANES), F32)] * 3 + _gather_scratch(n),
        compiler_params=_params(dimension_semantics=("arbitrary",) * 3),
    )(qa, ka, vb, qnorm, c2r, stats, tight, *blocks)
    return res[0], res[1], res[2:]


def _fox_bwd(qs, kn, vb, qa, ka, o, do, lse, hs):
    T = qs.shape[0]
    tq = min(ROW_BLOCK, T)
    nq = T // tq
    n = len(hs)
    n_pairs = FOX_HEADS // 2

    def body(*refs):
        q_ref, k_ref, v_ref, qa_ref, ka_ref, o_ref, do_ref, lse_ref = refs[:8]
        dq_ref, dk_ref, dv_ref, dcs_ref, drs_ref = refs[8 + n:13 + n]
        pr, ki, qi = pl.program_id(0), pl.program_id(1), pl.program_id(2)
        x_start, x_finish = _chip_exchange_steps(refs[8:8 + n], refs[13 + n:13 + 2 * n], *refs[13 + 2 * n:])

        @pl.when(jnp.logical_and(pr == 0, jnp.logical_and(ki == 0, qi == 0)))
        def _():
            x_start()

        @pl.when(jnp.logical_and(ki == 0, qi == 0))
        def _():
            dq_ref[...] = jnp.zeros_like(dq_ref)

        @pl.when(jnp.logical_and(pr == 0, jnp.logical_and(ki == 0, qi == 0)))
        def _():
            drs_ref[...] = jnp.zeros_like(drs_ref)

        @pl.when(qi == ki)
        def _():
            dk_ref[...] = jnp.zeros_like(dk_ref)
            dv_ref[...] = jnp.zeros_like(dv_ref)
            dcs_ref[...] = jnp.zeros_like(dcs_ref)

        def block(masked):
            lane = _iota((tq, LANES), 1)
            qv, kv, vv = q_ref[...], k_ref[...], v_ref[...]
            ov, dov = o_ref[...], do_ref[...]
            qrows = pl.ds(pl.multiple_of(qi * tq, tq), tq)
            dq_acc = jnp.zeros((tq, LANES), F32)
            dk_acc = jnp.zeros((tq, LANES), F32)
            dv_acc = jnp.zeros((tq, LANES), F32)
            dcs_acc = jnp.zeros((8, tq), F32)
            drs_acc = jnp.zeros((tq, LANES), F32)
            prod = dov * ov
            nt = tq // LANES
            for hh in range(2):
                in_head = (lane < FOX_DH) if hh == 0 else (lane >= FOX_DH)
                hs_ = slice(hh * LANES, (hh + 1) * LANES)
                zb = jnp.zeros_like(qv)
                qm = jnp.where(in_head, qv, zb)
                km = jnp.where(in_head, kv, zb)
                dom = jnp.where(in_head, dov, 0.0).astype(BF16)
                delta_b = jnp.broadcast_to(jnp.sum(jnp.where(in_head, prod, 0.0), axis=1, keepdims=True), (tq, LANES))
                lse_b = lse_ref[:, hs_]
                s = _dot(qa_ref[:, hs_], ka_ref[:, hs_], 1, 1)
                dp = _dot(dom, vv, 1, 1)
                ps, dss = [], []
                for j in range(nt):
                    js = slice(j * LANES, (j + 1) * LANES)
                    p = jnp.exp2(s[:, js] - lse_b)
                    if masked:
                        p = jnp.where(_iota((tq, LANES), 0) >= _iota((tq, LANES), 1) + j * LANES, p, 0.0)
                    ps.append(p)
                    dss.append(p * (dp[:, js] - delta_b))
                p_b = jnp.concatenate([p.astype(BF16) for p in ps], axis=1)
                ds_b = jnp.concatenate([d.astype(BF16) for d in dss], axis=1)
                dv_acc = dv_acc + _dot(p_b, dom, 0, 0)
                dq_acc = dq_acc + _dot(ds_b, km, 1, 0)
                dk_acc = dk_acc + _dot(ds_b, qm, 0, 0)
                colsum = jnp.concatenate([jnp.sum(d, axis=0, keepdims=True) for d in dss], axis=1)
                dcs_acc = dcs_acc + jnp.where(_iota((8, tq), 0) == hh, colsum, 0.0)
                rowsum = jnp.sum(functools.reduce(jnp.add, dss), axis=1, keepdims=True)
                drs_acc = drs_acc + jnp.where(lane == 2 * pr + hh, rowsum, 0.0)
            drs_ref[qrows, :] += drs_acc
            dq_ref[qrows, :] += dq_acc
            dk_ref[...] += dk_acc
            dv_ref[...] += dv_acc
            dcs_ref[0] += dcs_acc

        @pl.when(qi > ki)
        def _():
            block(False)

        @pl.when(qi == ki)
        def _():
            block(True)

        @pl.when(jnp.logical_and(pr == n_pairs - 1, jnp.logical_and(ki == nq - 1, qi == nq - 1)))
        def _():
            x_finish()

    qmap = lambda p, j, i: (jnp.maximum(i, j), p)
    hbm = pl.BlockSpec(memory_space=pl.ANY)
    res = _pallas(
        body, name="fox_bwd", grid=(n_pairs, nq, nq),
        in_specs=[pl.BlockSpec((tq, LANES), qmap), pl.BlockSpec((tq, LANES), lambda p, j, i: (j, p)),
                  pl.BlockSpec((tq, LANES), lambda p, j, i: (j, p)),
                  pl.BlockSpec((tq, 2 * LANES), qmap), pl.BlockSpec((tq, 2 * LANES), lambda p, j, i: (j, p)),
                  pl.BlockSpec((tq, LANES), qmap), pl.BlockSpec((tq, LANES), qmap), pl.BlockSpec((tq, 2 * LANES), qmap)]
        + [hbm] * n,
        out_specs=[pl.BlockSpec((T, LANES), lambda p, j, i: (0, p)), pl.BlockSpec((tq, LANES), lambda p, j, i: (j, p)),
                   pl.BlockSpec((tq, LANES), lambda p, j, i: (j, p)), pl.BlockSpec((1, 8, tq), lambda p, j, i: (p, 0, j)),
                   pl.BlockSpec((T, LANES), lambda p, j, i: (0, 0))] + [hbm] * n,
        out_shape=[jax.ShapeDtypeStruct((T, FOX_W), F32)] * 3
        + [jax.ShapeDtypeStruct((n_pairs, 8, T), F32), jax.ShapeDtypeStruct((T, LANES), F32)]
        + _chip_exchange_shapes(hs),
        scratch_shapes=_chip_exchange_scratch(n),
        compiler_params=_params(dimension_semantics=("arbitrary",) * 3),
    )(qs, kn, vb, qa, ka, o, do, lse, *hs)
    res = list(res)
    return res[:5] + [res[5:]]


def _fox_post(z, dq, dk, dv, dcs, drs, bias, qg, kg, dz_buf):
    T = z.shape[0]
    tm = min(ROW_BLOCK, T)
    nb = T // tm

    def body(z_ref, dq_ref, dk_ref, dv_ref, dcs_ref, drs_ref, b_ref, qg_ref, kg_ref, _buf_ref, dz_ref, dqg_ref, dkg_ref,
             db_ref, carry_ref):
        @pl.when(pl.program_id(0) == 0)
        def _():
            carry_ref[...] = jnp.zeros_like(carry_ref)
            dqg_ref[...] = jnp.zeros_like(dqg_ref)
            dkg_ref[...] = jnp.zeros_like(dkg_ref)
            db_ref[...] = jnp.zeros_like(db_ref)

        ones = _head_ones()
        for src, g_ref, d_ref, dg_ref, scale in ((0, qg_ref, dq_ref, dqg_ref, FOX_DH ** -0.5), (1, kg_ref, dk_ref, dkg_ref, 1.0)):
            xv = z_ref[:, src * FOX_W:(src + 1) * FOX_W]
            ms = _split_dot(ones, xv * xv, 1, 0, mat_first=False) * (1.0 / FOX_DH)
            rstd = lax.rsqrt(ms + EPS)
            xh = xv * rstd
            dn = d_ref[...] * scale
            dg_ref[...] += jnp.sum(dn * xh, axis=0, keepdims=True)
            dxh = dn * g_ref[...]
            mean = _split_dot(ones, dxh * xh, 1, 0, mat_first=False) * (1.0 / FOX_DH)
            dz_ref[:, src * FOX_W:(src + 1) * FOX_W] = (rstd * (dxh - xh * mean)).astype(BF16)
        dz_ref[:, 2 * FOX_W:3 * FOX_W] = dv_ref[...].astype(BF16)
        row8 = _iota((8, tm), 0)
        dct = jnp.zeros((8, tm), F32)
        for h in range(FOX_HEADS):
            src_row = dcs_ref[h // 2][h % 2:h % 2 + 1, :]
            dct = dct + jnp.where(row8 == h, src_row, 0.0)
        dct = drs_ref[...].T[0:8] - dct
        r, c = _iota((tm, tm), 0), _iota((tm, tm), 1)
        upper_b = (r >= c).astype(BF16)
        rc = _split_dot(upper_b, dct, 1, 0, mat_first=False) + carry_ref[...]
        carry_ref[...] = rc[:, 0:1]
        full = jnp.concatenate([rc, jnp.zeros((LANES - 8, tm), F32)], axis=0)
        dlogf = full.T
        xf = z_ref[:, 3 * FOX_W:FOX_COLS] + b_ref[...]
        df = dlogf * (1.0 - _sigmoid(xf))
        dz_ref[:, 3 * FOX_W:FOX_COLS] = df.astype(BF16)
        dz_ref[:, FOX_COLS:] = jnp.zeros((tm, SEG - FOX_COLS), BF16)
        db_ref[...] += jnp.sum(df, axis=0, keepdims=True)

    rev = lambda i: (nb - 1 - i, 0)
    fix2 = lambda i: (0, 0)
    return _pallas(
        body, name="fox_post", grid=(nb,),
        in_specs=[pl.BlockSpec((tm, SEG), lambda i: (nb - 1 - i, 1)), pl.BlockSpec((tm, FOX_W), rev),
                  pl.BlockSpec((tm, FOX_W), rev),
                  pl.BlockSpec((tm, FOX_W), rev), pl.BlockSpec((FOX_HEADS // 2, 8, tm), lambda i: (0, 0, nb - 1 - i)),
                  pl.BlockSpec((tm, LANES), rev),
                  pl.BlockSpec((1, LANES), fix2), pl.BlockSpec((1, FOX_W), fix2), pl.BlockSpec((1, FOX_W), fix2),
                  pl.BlockSpec(memory_space=pl.ANY)],
        out_specs=[pl.BlockSpec((tm, SEG), lambda i: (nb - 1 - i, 1)), pl.BlockSpec((1, FOX_W), fix2),
                   pl.BlockSpec((1, FOX_W), fix2), pl.BlockSpec((1, LANES), fix2)],
        out_shape=[jax.ShapeDtypeStruct(dz_buf.shape, BF16), jax.ShapeDtypeStruct((1, FOX_W), F32),
                   jax.ShapeDtypeStruct((1, FOX_W), F32), jax.ShapeDtypeStruct((1, LANES), F32)],
        scratch_shapes=[pltpu.VMEM((8, 1), F32)],
        input_output_aliases={9: 0},
        compiler_params=_params(dimension_semantics=("arbitrary",)),
    )(z, dq, dk, dv, dcs, drs, bias, qg, kg, dz_buf)


def _local_step(x, p, tgt, sm, W, rest_chunks, core):
    lbl, og, fb = sm["hg_lb_logits"], sm["hg_onorm_g"], sm["fox_f_bias"]
    fbias = jnp.pad(fb, ((0, 0), (0, LANES - FOX_HEADS)))
    qg = jnp.tile(sm["fox_q_norm_g"], (1, FOX_HEADS))
    kg = jnp.tile(sm["fox_k_norm_g"], (1, FOX_HEADS))

    h = _rms_fwd(x, sm["norm_mix_g"], name="rms_mix")
    z = _matmul(h, W["w_in"], name="mm_z")
    o_raw, ya, states = _hgrn_fwd(z, lbl, og)
    qs, kn, vb, qa, ka, qnorm, c2r, stats = _fox_prep(z, fbias, qg, kg)
    yb, lse, gathered = _fox_fwd(qa, ka, vb, qnorm, c2r, stats, rest_chunks)
    W = dict(W, **{n: _full_of_chunks(n, g) for n, g in zip(BIG[1:], gathered)})
    ua = _matmul(ya, W["w_branch_a"], name="mm_ua")
    ub = _matmul(yb, W["w_branch_b"], name="mm_ub")
    merged = _merge_fwd(ua, ub, z)
    x1 = _matmul(merged, W["w_out"], add=x, name="mm_x1")
    hf = _rms_fwd(x1, sm["norm_ffn_g"], name="rms_ffn")
    a = _matmul(hf, W["w_ffn_gate"], out_dtype=BF16, name="mm_ffn_a")
    b = _matmul(hf, W["w_ffn_up"], out_dtype=BF16, name="mm_ffn_b")
    act = _swiglu_fwd(a, b)
    x2 = _matmul(act, W["w_ffn_down"], add=x1, name="mm_x2")
    hp = _rms_fwd(x2, sm["norm_ple_g"], name="rms_ple")
    sp = _matmul(hp, W["w_ple_gate"], name="mm_sp")
    pp = _matmul(p, W["w_ple_proj"], name="mm_pp")
    dy, dsp, dpp, loss = _ple_loss(x2, sp, pp, tgt)

    G = {}
    G["w_ple_proj"] = _matmul(p, dpp, ta=True, out_dtype=BF16, name="mm_dw_ple_proj")
    G["w_ple_gate"] = _matmul(hp, dsp, ta=True, out_dtype=BF16, name="mm_dw_ple_gate")
    dhp = _matmul(dsp, W["w_ple_gate"], tb=True, name="mm_dhp")
    dx2, d_ple_g = _rms_bwd(x2, sm["norm_ple_g"], dhp, dy, name="rms_ple_bwd")
    dact = _matmul(dx2, W["w_ffn_down"], tb=True, name="mm_dact")
    G["w_ffn_down"] = _matmul(act, dx2, ta=True, out_dtype=BF16, name="mm_dw_ffn_down")
    da, db = _swiglu_bwd(a, b, dact)
    G["w_ffn_gate"] = _matmul(hf, da, ta=True, out_dtype=BF16, name="mm_dw_ffn_gate")
    G["w_ffn_up"] = _matmul(hf, db, ta=True, out_dtype=BF16, name="mm_dw_ffn_up")
    dhf = _matmul(da, W["w_ffn_gate"], tb=True, name="mm_dhf_a")
    dhf = _matmul(db, W["w_ffn_up"], tb=True, add=dhf, name="mm_dhf_b")
    dx1, d_ffn_g = _rms_bwd(x1, sm["norm_ffn_g"], dhf, dx2, name="rms_ffn_bwd")
    dmerged = _matmul(dx1, W["w_out"], tb=True, name="mm_dmerged")
    G["w_out"] = _matmul(merged, dx1, ta=True, out_dtype=BF16, name="mm_dw_out")
    dua, dub, dz = _merge_bwd(dmerged, ua, ub, z)
    G["w_branch_a"] = _matmul(ya, dua, ta=True, out_dtype=BF16, name="mm_dw_branch_a")
    G["w_branch_b"] = _matmul(yb, dub, ta=True, out_dtype=BF16, name="mm_dw_branch_b")
    dya = _matmul(dua, W["w_branch_a"], tb=True, name="mm_dya")
    dyb = _matmul(dub, W["w_branch_b"], tb=True, name="mm_dyb")
    hb_rest = _sibling_sums({n: G[n] for n in BIG[1:]}, core, tag="rest")
    dq, dk, dv, dcs, drs, got_rest = _fox_bwd(qs, kn, vb, qa, ka, yb, dyb, lse, hb_rest)
    dz, d_qg, d_kg, d_fb = _fox_post(z, dq, dk, dv, dcs, drs, fbias, qg, kg, dz)
    dz, d_lbl, d_og = _hgrn_bwd(z, o_raw, dya, states, lbl, og, dz)
    G["w_in"] = _matmul(h, dz, ta=True, out_dtype=BF16, name="mm_dw_in")
    hb_in = _sibling_sums({"w_in": G["w_in"]}, core, tag="w_in")
    dh, got_in = _matmul(dz, W["w_in"], tb=True, exchange=hb_in, name="mm_dh")
    grad_x, d_mix_g = _rms_bwd(x, sm["norm_mix_g"], dh, dx1, name="rms_mix_bwd")

    gs = {"norm_mix_g": d_mix_g, "hg_lb_logits": d_lbl, "hg_onorm_g": d_og, "fox_f_bias": d_fb[:, :FOX_HEADS],
          "fox_q_norm_g": d_qg.reshape(FOX_HEADS, FOX_DH).sum(0, keepdims=True),
          "fox_k_norm_g": d_kg.reshape(FOX_HEADS, FOX_DH).sum(0, keepdims=True),
          "norm_ffn_g": d_ffn_g, "norm_ple_g": d_ple_g}
    return loss, grad_x, gs, hb_in + hb_rest, list(got_in) + list(got_rest)


def _pack_rows(parts, total):
    buf = jnp.concatenate(parts, axis=-2)
    pad = total - buf.shape[-2]
    widths = [(0, 0)] * (buf.ndim - 2) + [(0, pad), (0, 0)]
    return jnp.pad(buf, widths)


def _chunk_of_shard(n, w):
    if n == "w_in":
        return jnp.pad(w, ((0, 0), (0, IN_SHARD_PAD - IN_SHARD)))
    if n in ("w_ffn_gate", "w_ffn_up"):
        return jnp.pad(w, ((0, 0), (0, FF_SHARD_PAD - FF_SHARD)))
    if n == "w_ffn_down":
        return jnp.pad(w, ((0, FF_SHARD_PAD - FF_SHARD), (0, 0)))
    return w


def _full_of_chunks(n, g):
    _, a, b = g.shape
    if BIG_SHAPE[n][2] == 0:
        return g.reshape(N_DEV * a, b)
    if n == "w_in":
        w = g[:, :, :IN_SHARD].transpose(1, 0, 2).reshape(a, IN_COLS)
        gap = jnp.zeros((a, SEG - FOX_LOGICAL), g.dtype)
        return jnp.concatenate([w[:, :HG_COLS], w[:, HG_COLS:HG_COLS + FOX_LOGICAL], gap, w[:, HG_COLS + FOX_LOGICAL:]],
                               axis=1)
    return g.transpose(1, 0, 2).reshape(a, N_DEV * b)


def _chunks_of_full(n, g):
    if BIG_SHAPE[n][2] == 0:
        return g.reshape(N_DEV, g.shape[0] // N_DEV, g.shape[1])
    if n == "w_in":
        a = g.shape[0]
        w = jnp.concatenate([g[:, :HG_COLS], g[:, SEG:SEG + FOX_LOGICAL], g[:, 2 * SEG:]], axis=1)
        w = w.reshape(a, N_DEV, IN_SHARD).transpose(1, 0, 2)
        return jnp.pad(w, ((0, 0), (0, 0), (0, IN_SHARD_PAD - IN_SHARD)))
    return g.reshape(g.shape[0], N_DEV, g.shape[1] // N_DEV).transpose(1, 0, 2)


def _pack_small(vals, loss_row=None):
    parts = [vals[n].reshape(SMALL_ROWS[n], -1) for n in SMALL]
    parts = [jnp.pad(v, ((0, 0), (0, LANES - v.shape[1]))) for v in parts]
    if loss_row is not None:
        parts.append(loss_row)
    return _pack_rows(parts, SMALL_TOTAL)


def _unpack_small(buf, like):
    out, r0 = {}, 0
    for n in SMALL:
        rows, size = SMALL_ROWS[n], like[n].size
        blk = buf[r0:r0 + rows]
        out[n] = (blk if size == rows * LANES else blk[:, :size]).reshape(like[n].shape)
        r0 += rows
    return out


def _place():
    return lax.axis_index("x"), lax.axis_index("y"), lax.axis_index("c")


def _gather_steps(x_refs, out_refs, send_sems, recv_sems, local_sems):
    n = len(x_refs)
    x, y, c = _place()
    me, sibling = (x, y, c), (x, y, 1 - c)
    chips = [(1 - x, y), (x, 1 - y), (1 - x, 1 - y)]

    def slot(i, px, py, pc):
        return out_refs[i].at[4 * px + 2 * py + pc]

    def copy(k, i, blk, to, own=False):
        return pltpu.make_async_remote_copy(
            src_ref=x_refs[i] if own else slot(i, *blk), dst_ref=slot(i, *blk),
            send_sem=send_sems.at[k, i], recv_sem=recv_sems.at[k, i], device_id=to, device_id_type=MESH)

    def mine():
        return [pltpu.make_async_copy(x_refs[i], slot(i, *me), local_sems.at[i]) for i in range(n)]

    def first():
        cps = [copy(0, i, me, sibling, own=True) for i in range(n)]
        return cps + [copy(1 + j, i, me, (*chip, c), own=True) for j, chip in enumerate(chips) for i in range(n)]

    def passed():
        return [copy(4 + j, i, (*chip, c), sibling) for j, chip in enumerate(chips) for i in range(n)]

    def start():
        for cp in mine() + first():
            cp.start()

    def forward():
        fws = passed()
        for j, chip in enumerate(chips):
            for i in range(n):
                copy(1 + j, i, (*chip, c), me).wait_recv()
                fws[j * n + i].start()

    def finish():
        for i in range(n):
            copy(0, i, sibling, me).wait_recv()
        for j, chip in enumerate(chips):
            for i in range(n):
                copy(4 + j, i, (*chip, 1 - c), me).wait_recv()
        for cp in first() + passed():
            cp.wait_send()
        for cp in mine():
            cp.wait()

    return start, forward, finish


def _gather_scratch(n):
    return [pltpu.SemaphoreType.DMA((7, n)), pltpu.SemaphoreType.DMA((7, n)), pltpu.SemaphoreType.DMA((n,))]


def _all_gather(blocks, *, name):
    n = len(blocks)

    def body(*refs):
        for step in _gather_steps(refs[:n], refs[n:2 * n], *refs[2 * n:]):
            step()

    hbm = pl.BlockSpec(memory_space=pl.ANY)
    return _pallas(
        body, name=name, out_shape=[jax.ShapeDtypeStruct((N_DEV,) + b.shape, b.dtype) for b in blocks],
        in_specs=[hbm] * n, out_specs=[hbm] * n, scratch_shapes=_gather_scratch(n),
    )(*blocks)


def _sibling_exchange(gs, *, name):
    n = len(gs)

    def body(*refs):
        g_refs, out_refs = refs[:n], refs[n:2 * n]
        send_sems, recv_sems = refs[2 * n:]
        x, y, c = _place()
        cps = [pltpu.make_async_remote_copy(
            src_ref=g_refs[i].at[:, pl.ds(1 - c, 1)], dst_ref=out_refs[i], send_sem=send_sems.at[i],
            recv_sem=recv_sems.at[i], device_id=(x, y, 1 - c), device_id_type=MESH) for i in range(n)]
        for cp in cps:
            cp.start()
        for cp in cps:
            cp.wait()

    hbm = pl.BlockSpec(memory_space=pl.ANY)
    return _pallas(
        body, name=name, out_shape=[jax.ShapeDtypeStruct((N_CHIP, 1) + g.shape[2:], g.dtype) for g in gs],
        in_specs=[hbm] * n, out_specs=[hbm] * n,
        scratch_shapes=[pltpu.SemaphoreType.DMA((n,)), pltpu.SemaphoreType.DMA((n,))],
    )(*gs)


def _chip_sum(g4, got, core, *, name):
    _, _, a, b = g4.shape

    def body(c_ref, g_ref, r_ref, h_ref):
        h_ref[0] = (g_ref[0, 0].astype(F32) + r_ref[0, 0].astype(F32)).astype(BF16)

    grid_spec = pltpu.PrefetchScalarGridSpec(
        num_scalar_prefetch=1, grid=(N_CHIP,),
        in_specs=[pl.BlockSpec((1, 1, a, b), lambda j, c: (j, c[0], 0, 0)),
                  pl.BlockSpec((1, 1, a, b), lambda j, c: (j, 0, 0, 0))],
        out_specs=[pl.BlockSpec((1, a, b), lambda j, c: (j, 0, 0))])
    return _pallas(
        body, name=name, grid_spec=grid_spec, out_shape=[jax.ShapeDtypeStruct((N_CHIP, a, b), BF16)],
        compiler_params=_params(dimension_semantics=("arbitrary",)),
    )(core, g4, got)[0]


def _chip_exchange(hs, *, name):
    n = len(hs)

    def body(*refs):
        for step in _chip_exchange_steps(refs[:n], refs[n:2 * n], *refs[2 * n:]):
            step()

    hbm = pl.BlockSpec(memory_space=pl.ANY)
    return _pallas(
        body, name=name, out_shape=_chip_exchange_shapes(hs), in_specs=[hbm] * n, out_specs=[hbm] * n,
        scratch_shapes=_chip_exchange_scratch(n),
    )(*hs)


def _chip_exchange_steps(h_refs, out_refs, send_sems, recv_sems):
    n = len(h_refs)
    x, y, c = _place()
    chips = [(1 - x, y), (x, 1 - y), (1 - x, 1 - y)]

    def copies():
        return [pltpu.make_async_remote_copy(
            src_ref=h_refs[i].at[2 * px + py], dst_ref=out_refs[i].at[k], send_sem=send_sems.at[k, i],
            recv_sem=recv_sems.at[k, i], device_id=(px, py, c), device_id_type=MESH)
            for k, (px, py) in enumerate(chips) for i in range(n)]

    def start():
        for cp in copies():
            cp.start()

    def finish():
        for cp in copies():
            cp.wait()

    return start, finish


def _chip_exchange_shapes(hs):
    return [jax.ShapeDtypeStruct((3,) + h.shape[1:], h.dtype) for h in hs]


def _chip_exchange_scratch(n):
    return [pltpu.SemaphoreType.DMA((3, n)), pltpu.SemaphoreType.DMA((3, n))]


def _sibling_sums(G, core, *, tag):
    g4 = []
    for n, g in G.items():
        gc = _chunks_of_full(n, g)
        g4.append(gc.reshape((N_CHIP, 2) + gc.shape[1:]))
    got = _sibling_exchange(g4, name="grads_to_sibling_" + tag)
    return [_chip_sum(g, r, core, name="chip_sum_" + n) for n, g, r in zip(G, g4, got)]


def _adam_math(w, g, m, v):
    m = ADAM_B1 * m + (1.0 - ADAM_B1) * g
    v = ADAM_B2 * v + (1.0 - ADAM_B2) * (g * g)
    m_hat = m / (1.0 - ADAM_B1 ** ADAM_STEP)
    v_hat = v / (1.0 - ADAM_B2 ** ADAM_STEP)
    delta = -ADAM_LR * (m_hat / (jnp.sqrt(v_hat) + ADAM_EPS) + ADAM_WD * w)
    return delta, m, v


def _adam_shard(hb, got, chip, w, m, v, *, name):
    _, r, c = w.shape
    _, a, b = hb.shape
    tr = r if r <= 512 else 256
    ta = tr if r // tr > 1 else a

    def body(j_ref, h_ref, r_ref, w_ref, m_ref, v_ref, g_ref, d_ref, nm_ref, nv_ref):
        parts = [h_ref[0], r_ref[0], r_ref[1], r_ref[2]]
        g = None
        for part in parts:
            part = part[:tr, :c].astype(F32)
            g = part if g is None else g + part
        d, nm, nv = _adam_math(w_ref[0], g, m_ref[0], v_ref[0])
        g_ref[0] = g
        d_ref[0] = d
        nm_ref[0] = nm
        nv_ref[0] = nv

    blk = pl.BlockSpec((1, tr, c), lambda i, j: (0, i, 0))
    grid_spec = pltpu.PrefetchScalarGridSpec(
        num_scalar_prefetch=1, grid=(r // tr,),
        in_specs=[pl.BlockSpec((1, ta, b), lambda i, j: (j[0], i, 0)),
                  pl.BlockSpec((3, ta, b), lambda i, j: (0, i, 0)), blk, blk, blk],
        out_specs=[blk] * 4)
    return _pallas(
        body, name=name, grid_spec=grid_spec, out_shape=[jax.ShapeDtypeStruct((1, r, c), F32)] * 4,
        compiler_params=_params(dimension_semantics=("arbitrary",)),
    )(chip, hb, got, w, m, v)


def _small_all_reduce_adam(gs, w, m, v):
    def body(g_ref, w_ref, m_ref, v_ref, sum_ref, d_ref, nm_ref, nv_ref, gather, send_sems, recv_sems):
        x, y, c = _place()
        my = 4 * x + 2 * y + c
        gather[my] = g_ref[...]
        cps = []
        for k in range(1, N_DEV):
            to = (x ^ (k >> 2), y ^ ((k >> 1) & 1), c ^ (k & 1))
            cps.append(pltpu.make_async_remote_copy(
                src_ref=g_ref, dst_ref=gather.at[my], send_sem=send_sems.at[k - 1], recv_sem=recv_sems.at[k - 1],
                device_id=to, device_id_type=MESH))
        for cp in cps:
            cp.start()
        for cp in cps:
            cp.wait()
        total = gather[0]
        for d in range(1, N_DEV):
            total = total + gather[d]
        dlt, nm, nv = _adam_math(w_ref[...], total, m_ref[...], v_ref[...])
        sum_ref[...] = total
        d_ref[...] = dlt
        nm_ref[...] = nm
        nv_ref[...] = nv

    vm = pl.BlockSpec(memory_space=pltpu.VMEM)
    return _pallas(
        body, name="small_all_reduce_adam", out_shape=[jax.ShapeDtypeStruct((SMALL_TOTAL, LANES), F32)] * 4,
        in_specs=[vm] * 4, out_specs=[vm] * 4,
        scratch_shapes=[pltpu.VMEM((N_DEV, SMALL_TOTAL, LANES), F32), pltpu.SemaphoreType.DMA((7,)),
                        pltpu.SemaphoreType.DMA((7,))],
            )(gs, w, m, v)


def kernel(x, p, norm_mix_g, w_in, hg_lb_logits, hg_onorm_g, fox_f_bias, fox_q_norm_g, fox_k_norm_g, w_branch_a, w_branch_b, w_out, norm_ffn_g, w_ffn_gate, w_ffn_up, w_ffn_down, norm_ple_g, w_ple_gate, w_ple_proj, loss_target, m_norm_mix_g, m_w_in, m_hg_lb_logits, m_hg_onorm_g, m_fox_f_bias, m_fox_q_norm_g, m_fox_k_norm_g, m_w_branch_a, m_w_branch_b, m_w_out, m_norm_ffn_g, m_w_ffn_gate, m_w_ffn_up, m_w_ffn_down, m_norm_ple_g, m_w_ple_gate, m_w_ple_proj, v_norm_mix_g, v_w_in, v_hg_lb_logits, v_hg_onorm_g, v_fox_f_bias, v_fox_q_norm_g, v_fox_k_norm_g, v_w_branch_a, v_w_branch_b, v_w_out, v_norm_ffn_g, v_w_ffn_gate, v_w_ffn_up, v_w_ffn_down, v_norm_ple_g, v_w_ple_gate, v_w_ple_proj):
    args = dict(locals())
    wts = {n: args[n] for n in BIG + SMALL}
    mom = {n: args["m_" + n] for n in BIG + SMALL}
    var = {n: args["v_" + n] for n in BIG + SMALL}
    sm = {n: wts[n] for n in SMALL}

    xi, yi, ci = _place()
    core = jnp.reshape(ci, (1,)).astype(jnp.int32)
    chip = jnp.reshape(2 * xi + yi, (1,)).astype(jnp.int32)
    chunks = [_chunk_of_shard(n, wts[n][0].astype(BF16)) for n in BIG]
    assert BIG[0] == "w_in"
    w_in_full = _full_of_chunks("w_in", _all_gather(chunks[:1], name="w_in_all_gather")[0])

    loss_blk, grad_x, gs, hb, got = _local_step(
        x[0], p[0, 0], loss_target[0], sm, {"w_in": w_in_full}, chunks[1:], core)

    g_big, d_big, nm_big, nv_big = {}, {}, {}, {}
    for n, h, r in zip(BIG, hb, got):
        g_big[n], d_big[n], nm_big[n], nv_big[n] = _adam_shard(h, r, chip, wts[n], mom[n], var[n], name="adam_" + n)

    s_sum, s_d, s_nm, s_nv = _small_all_reduce_adam(
        _pack_small(gs, loss_blk[0:1]), _pack_small(sm), _pack_small({n: mom[n] for n in SMALL}),
        _pack_small({n: var[n] for n in SMALL}))
    loss = s_sum[LOSS_ROW, 0]
    g_small, d_small, nm_small, nv_small = (_unpack_small(t, sm) for t in (s_sum, s_d, s_nm, s_nv))

    order = ["norm_mix_g", "w_in", "hg_lb_logits", "hg_onorm_g", "fox_f_bias", "fox_q_norm_g", "fox_k_norm_g",
             "w_branch_a", "w_branch_b", "w_out", "norm_ffn_g", "w_ffn_gate", "w_ffn_up", "w_ffn_down", "norm_ple_g",
             "w_ple_gate", "w_ple_proj"]
    outs = [loss, grad_x[None]]
    for big, small in ((g_big, g_small), (d_big, d_small), (nm_big, nm_small), (nv_big, nv_small)):
        outs += [big[n] if n in big else small[n] for n in order]
    return tuple(outs)
```

```python
import functools

import jax
import jax.numpy as jnp
from jax import lax
from jax.experimental import pallas as pl
from jax.experimental.pallas import tpu as pltpu

F32 = jnp.float32
BF16 = jnp.bfloat16

D_MODEL = 1024
PLE_DIM = 256
HG_HEADS = 4
HG_DK = 128
HG_CHUNK = 64
HG_SUB = 16
HG_W = HG_HEADS * HG_DK
FOX_HEADS = 8
FOX_DH = 64
FOX_W = FOX_HEADS * FOX_DH
D_FF = 2816
EPS = 1e-6
N_DEV = 8
N_CHIP = 4
LANES = 128
FOX_COLS = 3 * FOX_W + LANES
HG_COLS = 4 * HG_W
GATE_COLS = 2 * D_MODEL
IN_COLS = HG_COLS + 3 * FOX_W + FOX_HEADS + GATE_COLS
FOX_LOGICAL = 3 * FOX_W + FOX_HEADS
SEG = 2048
IN_PAD = 3 * SEG
IN_SHARD = IN_COLS // N_DEV
IN_SHARD_PAD = 768
FF_SHARD = D_FF // N_DEV
FF_SHARD_PAD = 384
FF_PAD = N_DEV * FF_SHARD_PAD
EXP_CLAMP = 80.0
LOG2E = 1.4426950408889634

ADAM_LR = 0.001
ADAM_B1 = 0.9
ADAM_B2 = 0.999
ADAM_EPS = 1e-08
ADAM_WD = 0.01
ADAM_STEP = 10

MESH = pl.DeviceIdType.MESH
VMEM_LIMIT = 56 * 1024 * 1024
ROW_BLOCK = 512

BIG = ["w_in", "w_branch_a", "w_branch_b", "w_out", "w_ffn_gate", "w_ffn_up", "w_ffn_down",
       "w_ple_gate", "w_ple_proj"]
BIG_SHAPE = {
    "w_in": (D_MODEL, IN_COLS, 1), "w_branch_a": (HG_W, D_MODEL, 1), "w_branch_b": (FOX_W, D_MODEL, 1),
    "w_out": (D_MODEL, D_MODEL, 0), "w_ffn_gate": (D_MODEL, D_FF, 1), "w_ffn_up": (D_MODEL, D_FF, 1),
    "w_ffn_down": (D_FF, D_MODEL, 0), "w_ple_gate": (D_MODEL, D_MODEL, 0), "w_ple_proj": (PLE_DIM, D_MODEL, 1),
}

SMALL = ["norm_mix_g", "hg_lb_logits", "hg_onorm_g", "fox_f_bias", "fox_q_norm_g", "fox_k_norm_g",
         "norm_ffn_g", "norm_ple_g"]
SMALL_ROWS = {"norm_mix_g": 8, "hg_lb_logits": 8, "hg_onorm_g": 1, "fox_f_bias": 1, "fox_q_norm_g": 1,
              "fox_k_norm_g": 1, "norm_ffn_g": 8, "norm_ple_g": 8}
SMALL_TOTAL = 40
LOSS_ROW = 36


def _pallas(body, **kw):
    return pl.pallas_call(body, **kw)


def _params(**kw):
    return pltpu.CompilerParams(vmem_limit_bytes=VMEM_LIMIT, **kw)


def _pick(n, target):
    if n <= target:
        return n
    best = None
    for t in range(LANES, target + 1, LANES):
        if n % t == 0:
            best = t
    assert best is not None, (n, target)
    return best


def _dot(a, b, ca, cb):
    return lax.dot_general(a, b, (((ca,), (cb,)), ((), ())), preferred_element_type=F32)


def _split_dot(mat, x, ca, cb, terms=2, mat_first=True):
    acc = None
    rem = x
    for _ in range(terms):
        part = rem.astype(BF16)
        rem = rem - part.astype(F32)
        p = _dot(mat, part, ca, cb) if mat_first else _dot(part, mat, ca, cb)
        acc = p if acc is None else acc + p
    return acc


def _sigmoid(x):
    return 1.0 / (1.0 + jnp.exp(-x))


def _iota(shape, dim):
    return lax.broadcasted_iota(jnp.int32, shape, dim)


def _matmul(a, b, *, name, ta=False, tb=False, out_dtype=F32, add=None, exchange=None):
    (K, M) = a.shape if ta else a.shape[::-1]
    (N, Kb) = b.shape if tb else b.shape[::-1]
    assert K == Kb, (a.shape, b.shape, ta, tb)
    if ta:
        tm, tn, tk = _pick(M, 2 * ROW_BLOCK), _pick(N, 2 * ROW_BLOCK), _pick(K, 4 * ROW_BLOCK)
    else:
        tm, tn, tk = _pick(M, 2 * ROW_BLOCK), _pick(N, 2048), _pick(K, 3072)
    nk = K // tk
    use_scratch = nk > 1 and out_dtype != F32

    n_in = 2 + (add is not None)
    hs = list(exchange or [])
    n_x = len(hs)
    grid = (M // tm, N // tn, nk)

    def body(*refs):
        refs = list(refs)
        a_ref, b_ref = refs[:2]
        add_ref = refs[2] if add is not None else None
        o_ref = refs[n_in + n_x]
        k = pl.program_id(2)
        if n_x:
            x_start, x_finish = _chip_exchange_steps(
                refs[n_in:n_in + n_x], refs[n_in + n_x + 1:n_in + 2 * n_x + 1], *refs[-2:])
            at = [pl.program_id(d) for d in range(3)]

            @pl.when(jnp.logical_and(at[0] == 0, jnp.logical_and(at[1] == 0, at[2] == 0)))
            def _():
                x_start()
        p = _dot(a_ref[...].astype(BF16), b_ref[...].astype(BF16), 0 if ta else 1, 1 if tb else 0)

        def with_add(r):
            return r if add is None else r + add_ref[...].astype(F32)

        if nk == 1:
            o_ref[...] = with_add(p).astype(out_dtype)
        elif not use_scratch:
            @pl.when(k == 0)
            def _():
                o_ref[...] = with_add(p)

            @pl.when(k > 0)
            def _():
                o_ref[...] += p
        else:
            acc_ref = refs[n_in + 2 * n_x + 1]

            @pl.when(k == 0)
            def _():
                acc_ref[...] = p

            @pl.when(k > 0)
            def _():
                acc_ref[...] += p

            @pl.when(k == nk - 1)
            def _():
                o_ref[...] = with_add(acc_ref[...]).astype(out_dtype)

        if n_x:
            @pl.when(jnp.logical_and(at[0] == grid[0] - 1, jnp.logical_and(at[1] == grid[1] - 1, at[2] == nk - 1)))
            def _():
                x_finish()

    a_spec = pl.BlockSpec((tk, tm), lambda i, j, k: (k, i)) if ta else pl.BlockSpec((tm, tk), lambda i, j, k: (i, k))
    b_spec = pl.BlockSpec((tn, tk), lambda i, j, k: (j, k)) if tb else pl.BlockSpec((tk, tn), lambda i, j, k: (k, j))
    o_spec = pl.BlockSpec((tm, tn), lambda i, j, k: (i, j))
    hbm = pl.BlockSpec(memory_space=pl.ANY)
    in_specs = [a_spec, b_spec] + ([o_spec] if add is not None else []) + [hbm] * n_x
    args = (a, b) + ((add,) if add is not None else ()) + tuple(hs)
    res = _pallas(
        body, name=name, grid=grid, in_specs=in_specs, out_specs=[o_spec] + [hbm] * n_x,
        out_shape=[jax.ShapeDtypeStruct((M, N), out_dtype)] + (_chip_exchange_shapes(hs) if n_x else []),
        scratch_shapes=([pltpu.VMEM((tm, tn), F32)] if use_scratch else []) + (_chip_exchange_scratch(n_x) if n_x else []),
        compiler_params=_params(dimension_semantics=("arbitrary",) * 3),
    )(*args)
    return (res[0], list(res[1:])) if n_x else res[0]


def _row_map(nb, reverse, seg):
    if reverse:
        return lambda i: (nb - 1 - i, seg)
    return lambda i: (i, seg)


def _row_call(body, *, name, T, ins, outs, acc_outs=(), tm=ROW_BLOCK, reverse=False):
    tm = min(tm, T)
    nb = T // tm
    in_specs, args = [], []
    for arr, how in ins:
        args.append(arr)
        if how is True:
            in_specs.append(pl.BlockSpec((tm, arr.shape[1]), _row_map(nb, reverse, 0)))
        elif how is False:
            in_specs.append(pl.BlockSpec(arr.shape, lambda i, _n=arr.ndim: (0,) * _n))
        else:
            in_specs.append(pl.BlockSpec((tm, SEG), _row_map(nb, reverse, how[0])))
    out_specs, out_shape = [], []
    for o in outs:
        c, dt = o[0], o[1]
        total, seg = o[2] if len(o) > 2 else (c, 0)
        out_specs.append(pl.BlockSpec((tm, c), _row_map(nb, reverse, seg)))
        out_shape.append(jax.ShapeDtypeStruct((T, total), dt))
    for shp, dt in acc_outs:
        out_specs.append(pl.BlockSpec(shp, lambda i, _n=len(shp): (0,) * _n))
        out_shape.append(jax.ShapeDtypeStruct(shp, dt))
    return _pallas(body, name=name, grid=(nb,), in_specs=in_specs, out_specs=out_specs, out_shape=out_shape,
                   compiler_params=_params(dimension_semantics=("arbitrary",)))(*args)


def _rms_fwd(x, g, *, name):
    T = x.shape[0]

    def body(x_ref, g_ref, h_ref):
        xv = x_ref[...]
        rstd = lax.rsqrt(jnp.mean(xv * xv, axis=-1, keepdims=True) + EPS)
        h_ref[...] = (xv * rstd * g_ref[...]).astype(BF16)

    return _row_call(body, name=name, T=T, ins=[(x, True), (g, False)], outs=[(D_MODEL, BF16)])[0]


def _rms_bwd(x, g, dh, dres, *, name):
    T = x.shape[0]

    def body(x_ref, g_ref, dh_ref, dres_ref, dx_ref, dg_ref):
        xv = x_ref[...]
        rstd = lax.rsqrt(jnp.mean(xv * xv, axis=-1, keepdims=True) + EPS)
        xh = xv * rstd
        dhv = dh_ref[...]
        part = jnp.sum(dhv * xh, axis=0, keepdims=True)

        @pl.when(pl.program_id(0) == 0)
        def _():
            dg_ref[...] = part

        @pl.when(pl.program_id(0) > 0)
        def _():
            dg_ref[...] += part

        dxh = dhv * g_ref[...]
        dx_ref[...] = rstd * (dxh - xh * jnp.mean(dxh * xh, axis=-1, keepdims=True)) + dres_ref[...]

    return _row_call(body, name=name, T=T, ins=[(x, True), (g, False), (dh, True), (dres, True)],
                     outs=[(D_MODEL, F32)], acc_outs=[((1, D_MODEL), F32)])


def _merge_fwd(ua, ub, zg):
    def body(ua_ref, ub_ref, zg_ref, m_ref):
        ga = _sigmoid(zg_ref[:, :D_MODEL])
        gb = _sigmoid(zg_ref[:, D_MODEL:])
        m_ref[...] = (ga * ua_ref[...] + gb * ub_ref[...]).astype(BF16)

    return _row_call(body, name="merge_fwd", T=ua.shape[0], ins=[(ua, True), (ub, True), (zg, (2,))],
                     outs=[(D_MODEL, BF16)])[0]


def _merge_bwd(dm, ua, ub, zg):
    def body(dm_ref, ua_ref, ub_ref, zg_ref, dua_ref, dub_ref, dzg_ref):
        ga = _sigmoid(zg_ref[:, :D_MODEL])
        gb = _sigmoid(zg_ref[:, D_MODEL:])
        dmv = dm_ref[...]
        dua_ref[...] = (dmv * ga).astype(BF16)
        dub_ref[...] = (dmv * gb).astype(BF16)
        dzg_ref[:, :D_MODEL] = (dmv * ua_ref[...] * ga * (1.0 - ga)).astype(BF16)
        dzg_ref[:, D_MODEL:] = (dmv * ub_ref[...] * gb * (1.0 - gb)).astype(BF16)

    return _row_call(body, name="merge_bwd", T=dm.shape[0], ins=[(dm, True), (ua, True), (ub, True), (zg, (2,))],
                     outs=[(D_MODEL, BF16), (D_MODEL, BF16), (SEG, BF16, (IN_PAD, 2))])


def _swiglu_fwd(a, b):
    def body(a_ref, b_ref, o_ref):
        av = a_ref[...].astype(F32)
        o_ref[...] = (av * _sigmoid(av) * b_ref[...].astype(F32)).astype(BF16)

    return _row_call(body, name="swiglu_fwd", T=a.shape[0], ins=[(a, True), (b, True)], outs=[(a.shape[1], BF16)])[0]


def _swiglu_bwd(a, b, dact):
    def body(a_ref, b_ref, d_ref, da_ref, db_ref):
        av = a_ref[...].astype(F32)
        bv = b_ref[...].astype(F32)
        dv = d_ref[...]
        sg = _sigmoid(av)
        da_ref[...] = (dv * bv * sg * (1.0 + av * (1.0 - sg))).astype(BF16)
        db_ref[...] = (dv * av * sg).astype(BF16)

    return _row_call(body, name="swiglu_bwd", T=a.shape[0], ins=[(a, True), (b, True), (dact, True)],
                     outs=[(a.shape[1], BF16), (a.shape[1], BF16)])


def _ple_loss(x2, sp, pp, tgt):
    def body(x_ref, sp_ref, pp_ref, t_ref, dy_ref, dsp_ref, dpp_ref, loss_ref):
        gp = _sigmoid(sp_ref[...])
        ppv = pp_ref[...]
        err = x_ref[...] + gp * ppv - t_ref[...]
        part = 0.5 * jnp.sum(jnp.mean(err * err, axis=-1, keepdims=True), axis=0, keepdims=True)
        part = jnp.broadcast_to(part, loss_ref.shape)

        @pl.when(pl.program_id(0) == 0)
        def _():
            loss_ref[...] = part

        @pl.when(pl.program_id(0) > 0)
        def _():
            loss_ref[...] += part

        dy = err * (1.0 / D_MODEL)
        dy_ref[...] = dy
        dsp_ref[...] = (dy * ppv * gp * (1.0 - gp)).astype(BF16)
        dpp_ref[...] = (dy * gp).astype(BF16)

    return _row_call(body, name="ple_loss", T=x2.shape[0], ins=[(x2, True), (sp, True), (pp, True), (tgt, True)],
                     outs=[(D_MODEL, F32), (D_MODEL, BF16), (D_MODEL, BF16)], acc_outs=[((8, LANES), F32)])


def _hg_consts():
    C = HG_CHUNK
    r, c = _iota((C, C), 0), _iota((C, C), 1)
    tri = (c <= r)
    same = (r // HG_SUB) == (c // HG_SUB)
    return tri, (tri & same)


def _hg_chunk_fwd(q, f, lb, tri_b, sub_b):
    sgq = _sigmoid(q)
    qt = q * sgq
    sg = _sigmoid(f)
    fg = lb + (1.0 - lb) * sg
    kf = (1.0 - lb) * (1.0 - sg)
    logf = jnp.log(fg)
    b = _split_dot(tri_b, logf, 1, 0)
    w = _split_dot(sub_b, logf, 1, 0)
    return sgq, qt, sg, fg, kf, b, w


def _hg_scores(qs_b, kf, b, row):
    C, S = HG_CHUNK, HG_SUB
    parts, ks = [], []
    for blk in range(C // S):
        ref = jnp.zeros_like(b[0:1]) if blk == 0 else b[blk * S - 1:blk * S]
        e = jnp.exp(jnp.minimum(ref - b, EXP_CLAMP))
        e = jnp.where(row < (blk + 1) * S, e, 0.0)
        k_b = (kf * e).astype(BF16)
        ks.append((e, k_b))
        parts.append(_dot(qs_b[blk * S:(blk + 1) * S], k_b, 1, 1))
    return jnp.concatenate(parts, axis=0), ks


def _hgrn_fwd(z, lb_logits, gain, blocks):
    T = z.shape[0]
    RB = min(ROW_BLOCK, T)
    nb, cpb = T // RB, RB // HG_CHUNK
    C, DK = HG_CHUNK, HG_DK
    n = len(blocks)

    def body(*refs):
        z_ref, lg_ref, g_ref = refs[:3]
        o_ref, y_ref, st_ref = refs[3 + n:6 + n]
        s_ref = refs[6 + 2 * n]
        g_start, g_forward, g_finish = _gather_steps(refs[3:3 + n], refs[6 + n:6 + 2 * n], *refs[7 + 2 * n:])

        @pl.when(pl.program_id(0) == 0)
        def _():
            s_ref[...] = jnp.zeros_like(s_ref)
            g_start()

        @pl.when(pl.program_id(0) == nb // 2)
        def _():
            g_forward()

        lg = lg_ref[...]
        lb_all = 1.0 / (1.0 + jnp.exp(lg[1:2] - lg[0:1]))
        gain_v = g_ref[...]
        tri, sub = _hg_consts()
        tri_b, sub_b = tri.astype(BF16), sub.astype(BF16)
        row = _iota((C, DK), 0)

        def chunk(ci, carry):
            r0 = pl.multiple_of(ci * C, C)
            rows = pl.ds(r0, C)
            for h in range(HG_HEADS):
                cs = slice(h * DK, (h + 1) * DK)
                q = z_ref[rows, cs]
                f = z_ref[rows, HG_W + h * DK:HG_W + (h + 1) * DK]
                v = z_ref[rows, 2 * HG_W + h * DK:2 * HG_W + (h + 1) * DK]
                g = z_ref[rows, 3 * HG_W + h * DK:3 * HG_W + (h + 1) * DK]
                lb = lb_all[:, cs]
                _, qt, _, _, kf, b, w = _hg_chunk_fwd(q, f, lb, tri_b, sub_b)
                st = s_ref[h]
                st_ref[pl.ds(pl.multiple_of((ci * HG_HEADS + h) * DK, DK), DK), :] = st
                v_b = v.astype(BF16)
                qs_b = (qt * jnp.exp(w)).astype(BF16)
                a, _ = _hg_scores(qs_b, kf, b, row)
                a = jnp.where(tri, a, 0.0)
                qd = qt * jnp.exp(b)
                o = _dot(qd.astype(BF16), st.astype(BF16), 1, 1) + _dot(a.astype(BF16), v_b, 1, 0)
                bl = b[C - 1:C]
                kd = kf * jnp.exp(bl - b)
                s_ref[h] = st * jnp.exp(bl) + _dot(v_b, kd.astype(BF16), 0, 0)
                o_ref[rows, cs] = o
                rstd = lax.rsqrt(jnp.mean(o * o, axis=-1, keepdims=True) + EPS)
                y_ref[rows, cs] = (o * rstd * gain_v * (g * _sigmoid(g))).astype(BF16)
            return carry

        lax.fori_loop(0, cpb, chunk, 0)

        @pl.when(pl.program_id(0) == nb - 1)
        def _():
            g_finish()

    hbm = pl.BlockSpec(memory_space=pl.ANY)
    res = _pallas(
        body, name="hgrn_fwd", grid=(nb,),
        in_specs=[pl.BlockSpec((RB, HG_COLS), lambda i: (i, 0)), pl.BlockSpec((2, HG_W), lambda i: (0, 0)),
                  pl.BlockSpec((1, DK), lambda i: (0, 0))] + [hbm] * n,
        out_specs=[pl.BlockSpec((RB, HG_W), lambda i: (i, 0)), pl.BlockSpec((RB, HG_W), lambda i: (i, 0)),
                   pl.BlockSpec((cpb * HG_HEADS * DK, DK), lambda i: (i, 0))] + [hbm] * n,
        out_shape=[jax.ShapeDtypeStruct((T, HG_W), F32), jax.ShapeDtypeStruct((T, HG_W), BF16),
                   jax.ShapeDtypeStruct((T // C * HG_HEADS * DK, DK), F32)]
        + [jax.ShapeDtypeStruct((N_DEV,) + b.shape, b.dtype) for b in blocks],
        scratch_shapes=[pltpu.VMEM((HG_HEADS, DK, DK), F32)] + _gather_scratch(n),
        compiler_params=_params(dimension_semantics=("arbitrary",)),
    )(z, lb_logits, gain, *blocks)
    return res[0], res[1], res[2], res[3:]


def _hgrn_bwd(z, o_raw, dy, states, lb_logits, gain, dz_buf):
    T = z.shape[0]
    RB = min(ROW_BLOCK, T)
    nb, cpb = T // RB, RB // HG_CHUNK
    C, DK, S = HG_CHUNK, HG_DK, HG_SUB

    def body(z_ref, o_ref, dy_ref, st_ref, lg_ref, g_ref, _buf_ref, dz_ref, dlg_ref, dg_ref, ds_ref, dlb_ref):
        step = pl.program_id(0)

        @pl.when(step == 0)
        def _():
            ds_ref[...] = jnp.zeros_like(ds_ref)
            dlb_ref[...] = jnp.zeros_like(dlb_ref)
            dg_ref[...] = jnp.zeros_like(dg_ref)

        lg = lg_ref[...]
        lb_all = 1.0 / (1.0 + jnp.exp(lg[1:2] - lg[0:1]))
        gain_v = g_ref[...]
        tri, sub = _hg_consts()
        tri_b, sub_b = tri.astype(BF16), sub.astype(BF16)
        row = _iota((C, DK), 0)

        def chunk(cj, carry):
            ci = cpb - 1 - cj
            r0 = pl.multiple_of(ci * C, C)
            rows = pl.ds(r0, C)
            for h in range(HG_HEADS):
                cs = slice(h * DK, (h + 1) * DK)
                q = z_ref[rows, cs]
                f = z_ref[rows, HG_W + h * DK:HG_W + (h + 1) * DK]
                v = z_ref[rows, 2 * HG_W + h * DK:2 * HG_W + (h + 1) * DK]
                g = z_ref[rows, 3 * HG_W + h * DK:3 * HG_W + (h + 1) * DK]
                lb = lb_all[:, cs]
                sgq, qt, sg, fg, kf, b, w = _hg_chunk_fwd(q, f, lb, tri_b, sub_b)
                st = st_ref[pl.ds(pl.multiple_of((ci * HG_HEADS + h) * DK, DK), DK), :]
                dst = ds_ref[h]
                o = o_ref[rows, cs]
                dyv = dy_ref[rows, cs]
                rstd = lax.rsqrt(jnp.mean(o * o, axis=-1, keepdims=True) + EPS)
                n = o * rstd
                sgg = _sigmoid(g)
                t1 = dyv * (g * sgg)
                dg_ref[...] += jnp.sum(t1 * n, axis=0, keepdims=True)
                dn = t1 * gain_v
                do = rstd * (dn - n * jnp.mean(dn * n, axis=-1, keepdims=True))
                dgate = dyv * n * gain_v * sgg * (1.0 + g * (1.0 - sgg))
                do_b = do.astype(BF16)
                v_b = v.astype(BF16)
                ew = jnp.exp(w)
                qs_b = (qt * ew).astype(BF16)
                a, ks = _hg_scores(qs_b, kf, b, row)
                a = jnp.where(tri, a, 0.0)
                eb = jnp.exp(b)
                qd = qt * eb
                bl = b[C - 1:C]
                ebl = jnp.exp(bl)
                ekd = jnp.exp(bl - b)
                kd = kf * ekd
                kd_b = kd.astype(BF16)
                qd_b = qd.astype(BF16)
                dst_b = dst.astype(BF16)
                dqd = _dot(do_b, st.astype(BF16), 1, 0)
                da = jnp.where(tri, _dot(do_b, v_b, 1, 1), 0.0)
                dv = _dot(a.astype(BF16), do_b, 0, 0) + _dot(kd_b, dst_b, 1, 1)
                dkd = _dot(v_b, dst_b, 1, 0)
                ds_ref[h] = dst * ebl + _dot(do_b, qd_b, 0, 0)
                dkd_kd = dkd * kd_b.astype(F32)
                dbl = ebl * jnp.sum(dst * st, axis=0, keepdims=True) + jnp.sum(dkd_kd, axis=0, keepdims=True)
                da_b = da.astype(BF16)
                dqs_parts = []
                dk_in = jnp.zeros((C, DK), F32)
                db_k = jnp.zeros((C, DK), F32)
                for blk in range(C // S):
                    e, k_b = ks[blk]
                    da_blk = da_b[blk * S:(blk + 1) * S]
                    dqs_parts.append(_dot(da_blk, k_b, 1, 0))
                    dks = _dot(da_blk, qs_b[blk * S:(blk + 1) * S], 0, 0)
                    dk_in = dk_in + dks * e
                    db_k = db_k + dks * k_b.astype(F32)
                dqs = jnp.concatenate(dqs_parts, axis=0)
                dqt_in = dqs * ew
                db = qs_b.astype(F32) * dqs - db_k + dqd * qd_b.astype(F32) - dkd_kd
                db = db + jnp.where(row == C - 1, dbl, 0.0)
                dlogf = _split_dot(tri_b, db, 0, 0)
                dqt = dqt_in + dqd * eb
                dkf = dk_in + dkd * ekd
                dfg = dlogf / fg - dkf
                dlb_ref[:, cs] += jnp.sum(dfg * (1.0 - sg), axis=0, keepdims=True)
                dz_ref[rows, cs] = (dqt * sgq * (1.0 + q * (1.0 - sgq))).astype(BF16)
                dz_ref[rows, HG_W + h * DK:HG_W + (h + 1) * DK] = (dfg * (1.0 - lb) * sg * (1.0 - sg)).astype(BF16)
                dz_ref[rows, 2 * HG_W + h * DK:2 * HG_W + (h + 1) * DK] = dv.astype(BF16)
                dz_ref[rows, 3 * HG_W + h * DK:3 * HG_W + (h + 1) * DK] = dgate.astype(BF16)
            return carry

        lax.fori_loop(0, cpb, chunk, 0)

        @pl.when(step == nb - 1)
        def _():
            d0 = dlb_ref[...] * lb_all * (1.0 - lb_all)
            dlg_ref[0:1, :] = d0
            dlg_ref[1:2, :] = -d0

    rev = lambda i: (nb - 1 - i, 0)
    fix = lambda i: (0, 0)
    return _pallas(
        body, name="hgrn_bwd", grid=(nb,),
        in_specs=[pl.BlockSpec((RB, HG_COLS), rev), pl.BlockSpec((RB, HG_W), rev), pl.BlockSpec((RB, HG_W), rev),
                  pl.BlockSpec((cpb * HG_HEADS * DK, DK), rev), pl.BlockSpec((2, HG_W), fix),
                  pl.BlockSpec((1, DK), fix), pl.BlockSpec(memory_space=pl.ANY)],
        out_specs=[pl.BlockSpec((RB, HG_COLS), rev), pl.BlockSpec((2, HG_W), fix), pl.BlockSpec((1, DK), fix)],
        out_shape=[jax.ShapeDtypeStruct(dz_buf.shape, BF16), jax.ShapeDtypeStruct((2, HG_W), F32),
                   jax.ShapeDtypeStruct((1, DK), F32)],
        scratch_shapes=[pltpu.VMEM((HG_HEADS, DK, DK), F32), pltpu.VMEM((1, HG_W), F32)],
        input_output_aliases={6: 0},
        compiler_params=_params(dimension_semantics=("arbitrary",)),
    )(z, o_raw, dy, states, lb_logits, gain, dz_buf)


def _head_ones():
    r, c = _iota((FOX_W, FOX_W), 0), _iota((FOX_W, FOX_W), 1)
    return ((r // FOX_DH) == (c // FOX_DH)).astype(BF16)


def _log_sigmoid(x):
    return jnp.minimum(x, 0.0) - jnp.log(1.0 + jnp.exp(-jnp.abs(x)))


def _fox_prep(z, bias, qg, kg):
    T = z.shape[0]
    tm = min(ROW_BLOCK, T)
    nb = T // tm

    def body(z_ref, b_ref, qg_ref, kg_ref, q_ref, k_ref, v_ref, qa_ref, ka_ref, carry_ref):
        @pl.when(pl.program_id(0) == 0)
        def _():
            carry_ref[...] = jnp.zeros_like(carry_ref)

        ones = _head_ones()
        normed = []
        for src, g_ref in ((0, qg_ref), (1, kg_ref)):
            xv = z_ref[:, src * FOX_W:(src + 1) * FOX_W]
            ms = _split_dot(ones, xv * xv, 1, 0, mat_first=False) * (1.0 / FOX_DH)
            normed.append(xv * lax.rsqrt(ms + EPS) * g_ref[...])
        qn, kn = normed
        q_ref[...] = (qn * FOX_DH ** -0.5).astype(BF16)
        k_b = kn.astype(BF16)
        k_ref[...] = k_b
        v_ref[...] = z_ref[:, 2 * FOX_W:3 * FOX_W].astype(BF16)
        logf = _log_sigmoid(z_ref[:, 3 * FOX_W:FOX_COLS] + b_ref[...])
        r, c = _iota((tm, tm), 0), _iota((tm, tm), 1)
        tri_b = (c <= r).astype(BF16)
        cum = _split_dot(tri_b, logf, 1, 0, terms=3) + carry_ref[...]
        carry_ref[...] = cum[tm - 1:tm]
        c2 = cum * LOG2E
        hi = c2.astype(BF16)
        rem = c2 - hi.astype(F32)
        mid = rem.astype(BF16)
        lo = (rem - mid.astype(F32)).astype(BF16)
        hrow, col = _iota((LANES, 2 * FOX_W), 0), _iota((LANES, 2 * FOX_W), 1)
        base = hrow * LANES + jnp.where(hrow % 2 == 0, FOX_DH, 0)
        placed = None
        for t, part in enumerate((hi, mid, lo)):
            place = jnp.logical_and(col == base + t, hrow < FOX_HEADS).astype(BF16)
            term = _dot(part, place, 1, 0)
            placed = term if placed is None else placed + term
        colw = _iota((tm, 2 * FOX_W), 1)
        head, lane = colw // LANES, colw % LANES
        own = (lane < FOX_DH) == (head % 2 == 0)
        other = jnp.where(head % 2 == 0, lane - FOX_DH, lane)
        ones_q = jnp.where(jnp.logical_and(other >= 0, other < 3), -1.0, 0.0)
        q2 = (qn * (FOX_DH ** -0.5 * LOG2E)).astype(BF16)
        q_exp = jnp.concatenate([q2[:, (h // 2) * LANES:(h // 2 + 1) * LANES] for h in range(FOX_HEADS)], axis=1)
        k_exp = jnp.concatenate([k_b[:, (h // 2) * LANES:(h // 2 + 1) * LANES] for h in range(FOX_HEADS)], axis=1)
        qa_ref[...] = jnp.where(own, q_exp, ones_q.astype(BF16))
        ka_ref[...] = jnp.where(own, k_exp, placed.astype(BF16))

    wide = pl.BlockSpec((tm, 2 * FOX_W), lambda i: (i, 0))
    return _pallas(
        body, name="fox_prep", grid=(nb,),
        in_specs=[pl.BlockSpec((tm, SEG), lambda i: (i, 1)), pl.BlockSpec((1, LANES), lambda i: (0, 0)),
                  pl.BlockSpec((1, FOX_W), lambda i: (0, 0)), pl.BlockSpec((1, FOX_W), lambda i: (0, 0))],
        out_specs=[pl.BlockSpec((tm, FOX_W), lambda i: (i, 0))] * 3 + [wide] * 2,
        out_shape=[jax.ShapeDtypeStruct((T, FOX_W), BF16)] * 3 + [jax.ShapeDtypeStruct((T, 2 * FOX_W), BF16)] * 2,
        scratch_shapes=[pltpu.VMEM((1, LANES), F32)],
        compiler_params=_params(dimension_semantics=("arbitrary",)),
    )(z, bias, qg, kg)


def _fox_fwd(qa, ka, vb, blocks):
    T = qa.shape[0]
    tq = min(ROW_BLOCK, T)
    nq = T // tq
    NEG = -1e30
    n = len(blocks)
    n_pairs = FOX_HEADS // 2

    n_in = 3

    def body(*refs):
        q_ref, k_ref, v_ref = refs[:n_in]
        o_ref, lse_ref = refs[n_in + n:n_in + n + 2]
        m_sc, l_sc, acc_sc = refs[n_in + 2 * n + 2:n_in + 2 * n + 5]
        pr, qi, ki = pl.program_id(0), pl.program_id(1), pl.program_id(2)
        g_start, g_forward, g_finish = _gather_steps(
            refs[n_in:n_in + n], refs[n_in + n + 2:n_in + 2 * n + 2], *refs[n_in + 2 * n + 5:])
        row_start = jnp.logical_and(qi == 0, ki == 0)

        @pl.when(jnp.logical_and(pr == 0, row_start))
        def _():
            g_start()

        @pl.when(jnp.logical_and(pr == n_pairs // 2, row_start))
        def _():
            g_forward()

        @pl.when(ki == 0)
        def _():
            m_sc[...] = jnp.full_like(m_sc, NEG)
            l_sc[...] = jnp.zeros_like(l_sc)
            acc_sc[...] = jnp.zeros_like(acc_sc)

        lane = _iota((tq, LANES), 1)

        def block(masked):
            vv = v_ref[...]
            for hh in range(2):
                hs = slice(hh * LANES, (hh + 1) * LANES)
                s = _dot(q_ref[:, hs], k_ref[:, hs], 1, 1)
                tiles = [s[:, j * LANES:(j + 1) * LANES] for j in range(tq // LANES)]
                if masked:
                    row, col = _iota((tq, LANES), 0), _iota((tq, LANES), 1)
                    tiles = [jnp.where(row >= col + j * LANES, t, NEG) for j, t in enumerate(tiles)]
                m_old = m_sc[hh]
                top = jnp.broadcast_to(jnp.max(functools.reduce(jnp.maximum, tiles), axis=-1, keepdims=True),
                                       (tq, LANES))
                m_new = jnp.maximum(m_old, top)
                alpha = jnp.exp2(m_old - m_new)
                ps = [jnp.exp2(t - m_new) for t in tiles]
                l_sc[hh] = alpha * l_sc[hh] + functools.reduce(jnp.add, ps)
                m_sc[hh] = m_new
                p_b = jnp.concatenate([p.astype(BF16) for p in ps], axis=1)
                acc_sc[hh] = alpha * acc_sc[hh] + _dot(p_b, vv, 1, 0)

        @pl.when(ki < qi)
        def _():
            block(False)

        @pl.when(ki == qi)
        def _():
            block(True)
            l0 = jnp.sum(l_sc[0], axis=-1, keepdims=True)
            l1 = jnp.sum(l_sc[1], axis=-1, keepdims=True)
            o_ref[...] = jnp.where(lane < FOX_DH, acc_sc[0] * (1.0 / l0), acc_sc[1] * (1.0 / l1))
            lse_ref[:, :LANES] = m_sc[0] + jnp.log2(l0)
            lse_ref[:, LANES:] = m_sc[1] + jnp.log2(l1)

        @pl.when(jnp.logical_and(pr == n_pairs - 1, jnp.logical_and(qi == nq - 1, ki == nq - 1)))
        def _():
            g_finish()

    kmap = lambda p, i, j: (jnp.minimum(i, j), p)
    hbm = pl.BlockSpec(memory_space=pl.ANY)
    qrow = pl.BlockSpec((tq, 2 * LANES), lambda p, i, j: (i, p))
    res = _pallas(
        body, name="fox_fwd", grid=(n_pairs, nq, nq),
        in_specs=[qrow, pl.BlockSpec((tq, 2 * LANES), kmap), pl.BlockSpec((tq, LANES), kmap)] + [hbm] * n,
        out_specs=[pl.BlockSpec((tq, LANES), lambda p, i, j: (i, p)), qrow] + [hbm] * n,
        out_shape=[jax.ShapeDtypeStruct((T, FOX_W), F32), jax.ShapeDtypeStruct((T, 2 * FOX_W), F32)]
        + [jax.ShapeDtypeStruct((N_DEV,) + b.shape, b.dtype) for b in blocks],
        scratch_shapes=[pltpu.VMEM((2, tq, LANES), F32)] * 3 + _gather_scratch(n),
        compiler_params=_params(dimension_semantics=("arbitrary",) * 3),
    )(qa, ka, vb, *blocks)
    return res[0], res[1], res[2:]


def _fox_bwd(qs, kn, vb, qa, ka, o, do, lse, hs):
    T = qs.shape[0]
    tq = min(ROW_BLOCK, T)
    nq = T // tq
    n = len(hs)
    n_pairs = FOX_HEADS // 2

    def body(*refs):
        q_ref, k_ref, v_ref, qa_ref, ka_ref, o_ref, do_ref, lse_ref = refs[:8]
        dq_ref, dk_ref, dv_ref, dcs_ref, drs_ref = refs[8 + n:13 + n]
        pr, ki, qi = pl.program_id(0), pl.program_id(1), pl.program_id(2)
        x_start, x_finish = _chip_exchange_steps(refs[8:8 + n], refs[13 + n:13 + 2 * n], *refs[13 + 2 * n:])

        @pl.when(jnp.logical_and(pr == 0, jnp.logical_and(ki == 0, qi == 0)))
        def _():
            x_start()

        @pl.when(jnp.logical_and(ki == 0, qi == 0))
        def _():
            dq_ref[...] = jnp.zeros_like(dq_ref)

        @pl.when(jnp.logical_and(pr == 0, jnp.logical_and(ki == 0, qi == 0)))
        def _():
            drs_ref[...] = jnp.zeros_like(drs_ref)

        @pl.when(qi == ki)
        def _():
            dk_ref[...] = jnp.zeros_like(dk_ref)
            dv_ref[...] = jnp.zeros_like(dv_ref)
            dcs_ref[...] = jnp.zeros_like(dcs_ref)

        def block(masked):
            lane = _iota((tq, LANES), 1)
            qv, kv, vv = q_ref[...], k_ref[...], v_ref[...]
            ov, dov = o_ref[...], do_ref[...]
            qrows = pl.ds(pl.multiple_of(qi * tq, tq), tq)
            dq_acc = jnp.zeros((tq, LANES), F32)
            dk_acc = jnp.zeros((tq, LANES), F32)
            dv_acc = jnp.zeros((tq, LANES), F32)
            dcs_acc = jnp.zeros((8, tq), F32)
            drs_acc = jnp.zeros((tq, LANES), F32)
            prod = dov * ov
            nt = tq // LANES
            for hh in range(2):
                in_head = (lane < FOX_DH) if hh == 0 else (lane >= FOX_DH)
                hs_ = slice(hh * LANES, (hh + 1) * LANES)
                zb = jnp.zeros_like(qv)
                qm = jnp.where(in_head, qv, zb)
                km = jnp.where(in_head, kv, zb)
                dom = jnp.where(in_head, dov, 0.0).astype(BF16)
                delta_b = jnp.broadcast_to(jnp.sum(jnp.where(in_head, prod, 0.0), axis=1, keepdims=True), (tq, LANES))
                lse_b = lse_ref[:, hs_]
                s = _dot(qa_ref[:, hs_], ka_ref[:, hs_], 1, 1)
                dp = _dot(dom, vv, 1, 1)
                ps, dss = [], []
                for j in range(nt):
                    js = slice(j * LANES, (j + 1) * LANES)
                    p = jnp.exp2(s[:, js] - lse_b)
                    if masked:
                        p = jnp.where(_iota((tq, LANES), 0) >= _iota((tq, LANES), 1) + j * LANES, p, 0.0)
                    ps.append(p)
                    dss.append(p * (dp[:, js] - delta_b))
                p_b = jnp.concatenate([p.astype(BF16) for p in ps], axis=1)
                ds_b = jnp.concatenate([d.astype(BF16) for d in dss], axis=1)
                dv_acc = dv_acc + _dot(p_b, dom, 0, 0)
                dq_acc = dq_acc + _dot(ds_b, km, 1, 0)
                dk_acc = dk_acc + _dot(ds_b, qm, 0, 0)
                colsum = jnp.concatenate([jnp.sum(d, axis=0, keepdims=True) for d in dss], axis=1)
                dcs_acc = dcs_acc + jnp.where(_iota((8, tq), 0) == hh, colsum, 0.0)
                rowsum = jnp.sum(functools.reduce(jnp.add, dss), axis=1, keepdims=True)
                drs_acc = drs_acc + jnp.where(lane == 2 * pr + hh, rowsum, 0.0)
            drs_ref[qrows, :] += drs_acc
            dq_ref[qrows, :] += dq_acc
            dk_ref[...] += dk_acc
            dv_ref[...] += dv_acc
            dcs_ref[0] += dcs_acc

        @pl.when(qi > ki)
        def _():
            block(False)

        @pl.when(qi == ki)
        def _():
            block(True)

        @pl.when(jnp.logical_and(pr == n_pairs - 1, jnp.logical_and(ki == nq - 1, qi == nq - 1)))
        def _():
            x_finish()

    qmap = lambda p, j, i: (jnp.maximum(i, j), p)
    hbm = pl.BlockSpec(memory_space=pl.ANY)
    res = _pallas(
        body, name="fox_bwd", grid=(n_pairs, nq, nq),
        in_specs=[pl.BlockSpec((tq, LANES), qmap), pl.BlockSpec((tq, LANES), lambda p, j, i: (j, p)),
                  pl.BlockSpec((tq, LANES), lambda p, j, i: (j, p)),
                  pl.BlockSpec((tq, 2 * LANES), qmap), pl.BlockSpec((tq, 2 * LANES), lambda p, j, i: (j, p)),
                  pl.BlockSpec((tq, LANES), qmap), pl.BlockSpec((tq, LANES), qmap), pl.BlockSpec((tq, 2 * LANES), qmap)]
        + [hbm] * n,
        out_specs=[pl.BlockSpec((T, LANES), lambda p, j, i: (0, p)), pl.BlockSpec((tq, LANES), lambda p, j, i: (j, p)),
                   pl.BlockSpec((tq, LANES), lambda p, j, i: (j, p)), pl.BlockSpec((1, 8, tq), lambda p, j, i: (p, 0, j)),
                   pl.BlockSpec((T, LANES), lambda p, j, i: (0, 0))] + [hbm] * n,
        out_shape=[jax.ShapeDtypeStruct((T, FOX_W), F32)] * 3
        + [jax.ShapeDtypeStruct((n_pairs, 8, T), F32), jax.ShapeDtypeStruct((T, LANES), F32)]
        + _chip_exchange_shapes(hs),
        scratch_shapes=_chip_exchange_scratch(n),
        compiler_params=_params(dimension_semantics=("arbitrary",) * 3),
    )(qs, kn, vb, qa, ka, o, do, lse, *hs)
    res = list(res)
    return res[:5] + [res[5:]]


def _fox_post(z, dq, dk, dv, dcs, drs, bias, qg, kg, dz_buf):
    T = z.shape[0]
    tm = min(ROW_BLOCK, T)
    nb = T // tm

    def body(z_ref, dq_ref, dk_ref, dv_ref, dcs_ref, drs_ref, b_ref, qg_ref, kg_ref, _buf_ref, dz_ref, dqg_ref, dkg_ref,
             db_ref, carry_ref):
        @pl.when(pl.program_id(0) == 0)
        def _():
            carry_ref[...] = jnp.zeros_like(carry_ref)
            dqg_ref[...] = jnp.zeros_like(dqg_ref)
            dkg_ref[...] = jnp.zeros_like(dkg_ref)
            db_ref[...] = jnp.zeros_like(db_ref)

        ones = _head_ones()
        for src, g_ref, d_ref, dg_ref, scale in ((0, qg_ref, dq_ref, dqg_ref, FOX_DH ** -0.5), (1, kg_ref, dk_ref, dkg_ref, 1.0)):
            xv = z_ref[:, src * FOX_W:(src + 1) * FOX_W]
            ms = _split_dot(ones, xv * xv, 1, 0, mat_first=False) * (1.0 / FOX_DH)
            rstd = lax.rsqrt(ms + EPS)
            xh = xv * rstd
            dn = d_ref[...] * scale
            dg_ref[...] += jnp.sum(dn * xh, axis=0, keepdims=True)
            dxh = dn * g_ref[...]
            mean = _split_dot(ones, dxh * xh, 1, 0, mat_first=False) * (1.0 / FOX_DH)
            dz_ref[:, src * FOX_W:(src + 1) * FOX_W] = (rstd * (dxh - xh * mean)).astype(BF16)
        dz_ref[:, 2 * FOX_W:3 * FOX_W] = dv_ref[...].astype(BF16)
        row8 = _iota((8, tm), 0)
        dct = jnp.zeros((8, tm), F32)
        for h in range(FOX_HEADS):
            src_row = dcs_ref[h // 2][h % 2:h % 2 + 1, :]
            dct = dct + jnp.where(row8 == h, src_row, 0.0)
        dct = drs_ref[...].T[0:8] - dct
        r, c = _iota((tm, tm), 0), _iota((tm, tm), 1)
        upper_b = (r >= c).astype(BF16)
        rc = _split_dot(upper_b, dct, 1, 0, mat_first=False) + carry_ref[...]
        carry_ref[...] = rc[:, 0:1]
        full = jnp.concatenate([rc, jnp.zeros((LANES - 8, tm), F32)], axis=0)
        dlogf = full.T
        xf = z_ref[:, 3 * FOX_W:FOX_COLS] + b_ref[...]
        df = dlogf * (1.0 - _sigmoid(xf))
        dz_ref[:, 3 * FOX_W:FOX_COLS] = df.astype(BF16)
        dz_ref[:, FOX_COLS:] = jnp.zeros((tm, SEG - FOX_COLS), BF16)
        db_ref[...] += jnp.sum(df, axis=0, keepdims=True)

    rev = lambda i: (nb - 1 - i, 0)
    fix2 = lambda i: (0, 0)
    return _pallas(
        body, name="fox_post", grid=(nb,),
        in_specs=[pl.BlockSpec((tm, SEG), lambda i: (nb - 1 - i, 1)), pl.BlockSpec((tm, FOX_W), rev),
                  pl.BlockSpec((tm, FOX_W), rev),
                  pl.BlockSpec((tm, FOX_W), rev), pl.BlockSpec((FOX_HEADS // 2, 8, tm), lambda i: (0, 0, nb - 1 - i)),
                  pl.BlockSpec((tm, LANES), rev),
                  pl.BlockSpec((1, LANES), fix2), pl.BlockSpec((1, FOX_W), fix2), pl.BlockSpec((1, FOX_W), fix2),
                  pl.BlockSpec(memory_space=pl.ANY)],
        out_specs=[pl.BlockSpec((tm, SEG), lambda i: (nb - 1 - i, 1)), pl.BlockSpec((1, FOX_W), fix2),
                   pl.BlockSpec((1, FOX_W), fix2), pl.BlockSpec((1, LANES), fix2)],
        out_shape=[jax.ShapeDtypeStruct(dz_buf.shape, BF16), jax.ShapeDtypeStruct((1, FOX_W), F32),
                   jax.ShapeDtypeStruct((1, FOX_W), F32), jax.ShapeDtypeStruct((1, LANES), F32)],
        scratch_shapes=[pltpu.VMEM((8, 1), F32)],
        input_output_aliases={9: 0},
        compiler_params=_params(dimension_semantics=("arbitrary",)),
    )(z, dq, dk, dv, dcs, drs, bias, qg, kg, dz_buf)


def _local_step(x, p, tgt, sm, W, rest_chunks, core):
    lbl, og, fb = sm["hg_lb_logits"], sm["hg_onorm_g"], sm["fox_f_bias"]
    fbias = jnp.pad(fb, ((0, 0), (0, LANES - FOX_HEADS)))
    qg = jnp.tile(sm["fox_q_norm_g"], (1, FOX_HEADS))
    kg = jnp.tile(sm["fox_k_norm_g"], (1, FOX_HEADS))

    h = _rms_fwd(x, sm["norm_mix_g"], name="rms_mix")
    z = _matmul(h, W["w_in"], name="mm_z")
    rest = dict(zip(BIG[1:], rest_chunks))
    early = ["w_ffn_gate", "w_ffn_up"]
    late = [n for n in BIG[1:] if n not in early]
    o_raw, ya, states, got_early = _hgrn_fwd(z, lbl, og, [rest[n] for n in early])
    qs, kn, vb, qa, ka = _fox_prep(z, fbias, qg, kg)
    yb, lse, got_late = _fox_fwd(qa, ka, vb, [rest[n] for n in late])
    W = dict(W, **{n: _full_of_chunks(n, g) for n, g in zip(early + late, list(got_early) + list(got_late))})
    ua = _matmul(ya, W["w_branch_a"], name="mm_ua")
    ub = _matmul(yb, W["w_branch_b"], name="mm_ub")
    merged = _merge_fwd(ua, ub, z)
    x1 = _matmul(merged, W["w_out"], add=x, name="mm_x1")
    hf = _rms_fwd(x1, sm["norm_ffn_g"], name="rms_ffn")
    a = _matmul(hf, W["w_ffn_gate"], out_dtype=BF16, name="mm_ffn_a")
    b = _matmul(hf, W["w_ffn_up"], out_dtype=BF16, name="mm_ffn_b")
    act = _swiglu_fwd(a, b)
    x2 = _matmul(act, W["w_ffn_down"], add=x1, name="mm_x2")
    hp = _rms_fwd(x2, sm["norm_ple_g"], name="rms_ple")
    sp = _matmul(hp, W["w_ple_gate"], name="mm_sp")
    pp = _matmul(p, W["w_ple_proj"], name="mm_pp")
    dy, dsp, dpp, loss = _ple_loss(x2, sp, pp, tgt)

    G = {}
    G["w_ple_proj"] = _matmul(p, dpp, ta=True, out_dtype=BF16, name="mm_dw_ple_proj")
    G["w_ple_gate"] = _matmul(hp, dsp, ta=True, out_dtype=BF16, name="mm_dw_ple_gate")
    dhp = _matmul(dsp, W["w_ple_gate"], tb=True, name="mm_dhp")
    dx2, d_ple_g = _rms_bwd(x2, sm["norm_ple_g"], dhp, dy, name="rms_ple_bwd")
    dact = _matmul(dx2, W["w_ffn_down"], tb=True, name="mm_dact")
    G["w_ffn_down"] = _matmul(act, dx2, ta=True, out_dtype=BF16, name="mm_dw_ffn_down")
    da, db = _swiglu_bwd(a, b, dact)
    G["w_ffn_gate"] = _matmul(hf, da, ta=True, out_dtype=BF16, name="mm_dw_ffn_gate")
    G["w_ffn_up"] = _matmul(hf, db, ta=True, out_dtype=BF16, name="mm_dw_ffn_up")
    dhf = _matmul(da, W["w_ffn_gate"], tb=True, name="mm_dhf_a")
    dhf = _matmul(db, W["w_ffn_up"], tb=True, add=dhf, name="mm_dhf_b")
    dx1, d_ffn_g = _rms_bwd(x1, sm["norm_ffn_g"], dhf, dx2, name="rms_ffn_bwd")
    dmerged = _matmul(dx1, W["w_out"], tb=True, name="mm_dmerged")
    G["w_out"] = _matmul(merged, dx1, ta=True, out_dtype=BF16, name="mm_dw_out")
    dua, dub, dz = _merge_bwd(dmerged, ua, ub, z)
    G["w_branch_a"] = _matmul(ya, dua, ta=True, out_dtype=BF16, name="mm_dw_branch_a")
    G["w_branch_b"] = _matmul(yb, dub, ta=True, out_dtype=BF16, name="mm_dw_branch_b")
    dya = _matmul(dua, W["w_branch_a"], tb=True, name="mm_dya")
    dyb = _matmul(dub, W["w_branch_b"], tb=True, name="mm_dyb")
    hb_rest = _sibling_sums({n: G[n] for n in BIG[1:]}, core, tag="rest")
    dq, dk, dv, dcs, drs, got_rest = _fox_bwd(qs, kn, vb, qa, ka, yb, dyb, lse, hb_rest)
    dz, d_qg, d_kg, d_fb = _fox_post(z, dq, dk, dv, dcs, drs, fbias, qg, kg, dz)
    dz, d_lbl, d_og = _hgrn_bwd(z, o_raw, dya, states, lbl, og, dz)
    G["w_in"] = _matmul(h, dz, ta=True, out_dtype=BF16, name="mm_dw_in")
    hb_in = _sibling_sums({"w_in": G["w_in"]}, core, tag="w_in")
    dh, got_in = _matmul(dz, W["w_in"], tb=True, exchange=hb_in, name="mm_dh")
    grad_x, d_mix_g = _rms_bwd(x, sm["norm_mix_g"], dh, dx1, name="rms_mix_bwd")

    gs = {"norm_mix_g": d_mix_g, "hg_lb_logits": d_lbl, "hg_onorm_g": d_og, "fox_f_bias": d_fb[:, :FOX_HEADS],
          "fox_q_norm_g": d_qg.reshape(FOX_HEADS, FOX_DH).sum(0, keepdims=True),
          "fox_k_norm_g": d_kg.reshape(FOX_HEADS, FOX_DH).sum(0, keepdims=True),
          "norm_ffn_g": d_ffn_g, "norm_ple_g": d_ple_g}
    return loss, grad_x, gs, hb_in + hb_rest, list(got_in) + list(got_rest)


def _pack_rows(parts, total):
    buf = jnp.concatenate(parts, axis=-2)
    pad = total - buf.shape[-2]
    widths = [(0, 0)] * (buf.ndim - 2) + [(0, pad), (0, 0)]
    return jnp.pad(buf, widths)


def _chunk_of_shard(n, w):
    if n == "w_in":
        return jnp.pad(w, ((0, 0), (0, IN_SHARD_PAD - IN_SHARD)))
    if n in ("w_ffn_gate", "w_ffn_up"):
        return jnp.pad(w, ((0, 0), (0, FF_SHARD_PAD - FF_SHARD)))
    if n == "w_ffn_down":
        return jnp.pad(w, ((0, FF_SHARD_PAD - FF_SHARD), (0, 0)))
    return w


def _full_of_chunks(n, g):
    _, a, b = g.shape
    if BIG_SHAPE[n][2] == 0:
        return g.reshape(N_DEV * a, b)
    if n == "w_in":
        w = g[:, :, :IN_SHARD].transpose(1, 0, 2).reshape(a, IN_COLS)
        gap = jnp.zeros((a, SEG - FOX_LOGICAL), g.dtype)
        return jnp.concatenate([w[:, :HG_COLS], w[:, HG_COLS:HG_COLS + FOX_LOGICAL], gap, w[:, HG_COLS + FOX_LOGICAL:]],
                               axis=1)
    return g.transpose(1, 0, 2).reshape(a, N_DEV * b)


def _chunks_of_full(n, g):
    if BIG_SHAPE[n][2] == 0:
        return g.reshape(N_DEV, g.shape[0] // N_DEV, g.shape[1])
    if n == "w_in":
        a = g.shape[0]
        w = jnp.concatenate([g[:, :HG_COLS], g[:, SEG:SEG + FOX_LOGICAL], g[:, 2 * SEG:]], axis=1)
        w = w.reshape(a, N_DEV, IN_SHARD).transpose(1, 0, 2)
        return jnp.pad(w, ((0, 0), (0, 0), (0, IN_SHARD_PAD - IN_SHARD)))
    return g.reshape(g.shape[0], N_DEV, g.shape[1] // N_DEV).transpose(1, 0, 2)


def _pack_small(vals, loss_row=None):
    parts = [vals[n].reshape(SMALL_ROWS[n], -1) for n in SMALL]
    parts = [jnp.pad(v, ((0, 0), (0, LANES - v.shape[1]))) for v in parts]
    if loss_row is not None:
        parts.append(loss_row)
    return _pack_rows(parts, SMALL_TOTAL)


def _unpack_small(buf, like):
    out, r0 = {}, 0
    for n in SMALL:
        rows, size = SMALL_ROWS[n], like[n].size
        blk = buf[r0:r0 + rows]
        out[n] = (blk if size == rows * LANES else blk[:, :size]).reshape(like[n].shape)
        r0 += rows
    return out


def _place():
    return lax.axis_index("x"), lax.axis_index("y"), lax.axis_index("c")


def _gather_steps(x_refs, out_refs, send_sems, recv_sems, local_sems):
    n = len(x_refs)
    x, y, c = _place()
    me, sibling = (x, y, c), (x, y, 1 - c)
    chips = [(1 - x, y), (x, 1 - y), (1 - x, 1 - y)]

    def slot(i, px, py, pc):
        return out_refs[i].at[4 * px + 2 * py + pc]

    def copy(k, i, blk, to, own=False):
        return pltpu.make_async_remote_copy(
            src_ref=x_refs[i] if own else slot(i, *blk), dst_ref=slot(i, *blk),
            send_sem=send_sems.at[k, i], recv_sem=recv_sems.at[k, i], device_id=to, device_id_type=MESH)

    def mine():
        return [pltpu.make_async_copy(x_refs[i], slot(i, *me), local_sems.at[i]) for i in range(n)]

    def first():
        cps = [copy(0, i, me, sibling, own=True) for i in range(n)]
        return cps + [copy(1 + j, i, me, (*chip, c), own=True) for j, chip in enumerate(chips) for i in range(n)]

    def passed():
        return [copy(4 + j, i, (*chip, c), sibling) for j, chip in enumerate(chips) for i in range(n)]

    def start():
        for cp in mine() + first():
            cp.start()

    def forward():
        fws = passed()
        for j, chip in enumerate(chips):
            for i in range(n):
                copy(1 + j, i, (*chip, c), me).wait_recv()
                fws[j * n + i].start()

    def finish():
        for i in range(n):
            copy(0, i, sibling, me).wait_recv()
        for j, chip in enumerate(chips):
            for i in range(n):
                copy(4 + j, i, (*chip, 1 - c), me).wait_recv()
        for cp in first() + passed():
            cp.wait_send()
        for cp in mine():
            cp.wait()

    return start, forward, finish


def _gather_scratch(n):
    return [pltpu.SemaphoreType.DMA((7, n)), pltpu.SemaphoreType.DMA((7, n)), pltpu.SemaphoreType.DMA((n,))]


def _all_gather(blocks, *, name):
    n = len(blocks)

    def body(*refs):
        for step in _gather_steps(refs[:n], refs[n:2 * n], *refs[2 * n:]):
            step()

    hbm = pl.BlockSpec(memory_space=pl.ANY)
    return _pallas(
        body, name=name, out_shape=[jax.ShapeDtypeStruct((N_DEV,) + b.shape, b.dtype) for b in blocks],
        in_specs=[hbm] * n, out_specs=[hbm] * n, scratch_shapes=_gather_scratch(n),
    )(*blocks)


def _sibling_exchange(gs, *, name):
    n = len(gs)

    def body(*refs):
        g_refs, out_refs = refs[:n], refs[n:2 * n]
        send_sems, recv_sems = refs[2 * n:]
        x, y, c = _place()
        cps = [pltpu.make_async_remote_copy(
            src_ref=g_refs[i].at[:, pl.ds(1 - c, 1)], dst_ref=out_refs[i], send_sem=send_sems.at[i],
            recv_sem=recv_sems.at[i], device_id=(x, y, 1 - c), device_id_type=MESH) for i in range(n)]
        for cp in cps:
            cp.start()
        for cp in cps:
            cp.wait()

    hbm = pl.BlockSpec(memory_space=pl.ANY)
    return _pallas(
        body, name=name, out_shape=[jax.ShapeDtypeStruct((N_CHIP, 1) + g.shape[2:], g.dtype) for g in gs],
        in_specs=[hbm] * n, out_specs=[hbm] * n,
        scratch_shapes=[pltpu.SemaphoreType.DMA((n,)), pltpu.SemaphoreType.DMA((n,))],
    )(*gs)


def _chip_sum(g4, got, core, *, name):
    _, _, a, b = g4.shape

    def body(c_ref, g_ref, r_ref, h_ref):
        h_ref[0] = (g_ref[0, 0].astype(F32) + r_ref[0, 0].astype(F32)).astype(BF16)

    grid_spec = pltpu.PrefetchScalarGridSpec(
        num_scalar_prefetch=1, grid=(N_CHIP,),
        in_specs=[pl.BlockSpec((1, 1, a, b), lambda j, c: (j, c[0], 0, 0)),
                  pl.BlockSpec((1, 1, a, b), lambda j, c: (j, 0, 0, 0))],
        out_specs=[pl.BlockSpec((1, a, b), lambda j, c: (j, 0, 0))])
    return _pallas(
        body, name=name, grid_spec=grid_spec, out_shape=[jax.ShapeDtypeStruct((N_CHIP, a, b), BF16)],
        compiler_params=_params(dimension_semantics=("arbitrary",)),
    )(core, g4, got)[0]


def _chip_exchange(hs, *, name):
    n = len(hs)

    def body(*refs):
        for step in _chip_exchange_steps(refs[:n], refs[n:2 * n], *refs[2 * n:]):
            step()

    hbm = pl.BlockSpec(memory_space=pl.ANY)
    return _pallas(
        body, name=name, out_shape=_chip_exchange_shapes(hs), in_specs=[hbm] * n, out_specs=[hbm] * n,
        scratch_shapes=_chip_exchange_scratch(n),
    )(*hs)


def _chip_exchange_steps(h_refs, out_refs, send_sems, recv_sems):
    n = len(h_refs)
    x, y, c = _place()
    chips = [(1 - x, y), (x, 1 - y), (1 - x, 1 - y)]

    def copies():
        return [pltpu.make_async_remote_copy(
            src_ref=h_refs[i].at[2 * px + py], dst_ref=out_refs[i].at[k], send_sem=send_sems.at[k, i],
            recv_sem=recv_sems.at[k, i], device_id=(px, py, c), device_id_type=MESH)
            for k, (px, py) in enumerate(chips) for i in range(n)]

    def start():
        for cp in copies():
            cp.start()

    def finish():
        for cp in copies():
            cp.wait()

    return start, finish


def _chip_exchange_shapes(hs):
    return [jax.ShapeDtypeStruct((3,) + h.shape[1:], h.dtype) for h in hs]


def _chip_exchange_scratch(n):
    return [pltpu.SemaphoreType.DMA((3, n)), pltpu.SemaphoreType.DMA((3, n))]


def _sibling_sums(G, core, *, tag):
    g4 = []
    for n, g in G.items():
        gc = _chunks_of_full(n, g)
        g4.append(gc.reshape((N_CHIP, 2) + gc.shape[1:]))
    got = _sibling_exchange(g4, name="grads_to_sibling_" + tag)
    return [_chip_sum(g, r, core, name="chip_sum_" + n) for n, g, r in zip(G, g4, got)]


def _adam_math(w, g, m, v):
    m = ADAM_B1 * m + (1.0 - ADAM_B1) * g
    v = ADAM_B2 * v + (1.0 - ADAM_B2) * (g * g)
    m_hat = m / (1.0 - ADAM_B1 ** ADAM_STEP)
    v_hat = v / (1.0 - ADAM_B2 ** ADAM_STEP)
    delta = -ADAM_LR * (m_hat / (jnp.sqrt(v_hat) + ADAM_EPS) + ADAM_WD * w)
    return delta, m, v


def _adam_shard(hb, got, chip, w, m, v, *, name):
    _, r, c = w.shape
    _, a, b = hb.shape
    tr = r if r <= 512 else 256
    ta = tr if r // tr > 1 else a

    def body(j_ref, h_ref, r_ref, w_ref, m_ref, v_ref, g_ref, d_ref, nm_ref, nv_ref):
        parts = [h_ref[0], r_ref[0], r_ref[1], r_ref[2]]
        g = None
        for part in parts:
            part = part[:tr, :c].astype(F32)
            g = part if g is None else g + part
        d, nm, nv = _adam_math(w_ref[0], g, m_ref[0], v_ref[0])
        g_ref[0] = g
        d_ref[0] = d
        nm_ref[0] = nm
        nv_ref[0] = nv

    blk = pl.BlockSpec((1, tr, c), lambda i, j: (0, i, 0))
    grid_spec = pltpu.PrefetchScalarGridSpec(
        num_scalar_prefetch=1, grid=(r // tr,),
        in_specs=[pl.BlockSpec((1, ta, b), lambda i, j: (j[0], i, 0)),
                  pl.BlockSpec((3, ta, b), lambda i, j: (0, i, 0)), blk, blk, blk],
        out_specs=[blk] * 4)
    return _pallas(
        body, name=name, grid_spec=grid_spec, out_shape=[jax.ShapeDtypeStruct((1, r, c), F32)] * 4,
        compiler_params=_params(dimension_semantics=("arbitrary",)),
    )(chip, hb, got, w, m, v)


def _small_all_reduce_adam(gs, w, m, v):
    def body(g_ref, w_ref, m_ref, v_ref, sum_ref, d_ref, nm_ref, nv_ref, gather, send_sems, recv_sems):
        x, y, c = _place()
        my = 4 * x + 2 * y + c
        gather[my] = g_ref[...]
        cps = []
        for k in range(1, N_DEV):
            to = (x ^ (k >> 2), y ^ ((k >> 1) & 1), c ^ (k & 1))
            cps.append(pltpu.make_async_remote_copy(
                src_ref=g_ref, dst_ref=gather.at[my], send_sem=send_sems.at[k - 1], recv_sem=recv_sems.at[k - 1],
                device_id=to, device_id_type=MESH))
        for cp in cps:
            cp.start()
        for cp in cps:
            cp.wait()
        total = gather[0]
        for d in range(1, N_DEV):
            total = total + gather[d]
        dlt, nm, nv = _adam_math(w_ref[...], total, m_ref[...], v_ref[...])
        sum_ref[...] = total
        d_ref[...] = dlt
        nm_ref[...] = nm
        nv_ref[...] = nv

    vm = pl.BlockSpec(memory_space=pltpu.VMEM)
    return _pallas(
        body, name="small_all_reduce_adam", out_shape=[jax.ShapeDtypeStruct((SMALL_TOTAL, LANES), F32)] * 4,
        in_specs=[vm] * 4, out_specs=[vm] * 4,
        scratch_shapes=[pltpu.VMEM((N_DEV, SMALL_TOTAL, LANES), F32), pltpu.SemaphoreType.DMA((7,)),
                        pltpu.SemaphoreType.DMA((7,))],
            )(gs, w, m, v)


def kernel(x, p, norm_mix_g, w_in, hg_lb_logits, hg_onorm_g, fox_f_bias, fox_q_norm_g, fox_k_norm_g, w_branch_a, w_branch_b, w_out, norm_ffn_g, w_ffn_gate, w_ffn_up, w_ffn_down, norm_ple_g, w_ple_gate, w_ple_proj, loss_target, m_norm_mix_g, m_w_in, m_hg_lb_logits, m_hg_onorm_g, m_fox_f_bias, m_fox_q_norm_g, m_fox_k_norm_g, m_w_branch_a, m_w_branch_b, m_w_out, m_norm_ffn_g, m_w_ffn_gate, m_w_ffn_up, m_w_ffn_down, m_norm_ple_g, m_w_ple_gate, m_w_ple_proj, v_norm_mix_g, v_w_in, v_hg_lb_logits, v_hg_onorm_g, v_fox_f_bias, v_fox_q_norm_g, v_fox_k_norm_g, v_w_branch_a, v_w_branch_b, v_w_out, v_norm_ffn_g, v_w_ffn_gate, v_w_ffn_up, v_w_ffn_down, v_norm_ple_g, v_w_ple_gate, v_w_ple_proj):
    args = dict(locals())
    wts = {n: args[n] for n in BIG + SMALL}
    mom = {n: args["m_" + n] for n in BIG + SMALL}
    var = {n: args["v_" + n] for n in BIG + SMALL}
    sm = {n: wts[n] for n in SMALL}

    xi, yi, ci = _place()
    core = jnp.reshape(ci, (1,)).astype(jnp.int32)
    chip = jnp.reshape(2 * xi + yi, (1,)).astype(jnp.int32)
    chunks = [_chunk_of_shard(n, wts[n][0].astype(BF16)) for n in BIG]
    assert BIG[0] == "w_in"
    w_in_full = _full_of_chunks("w_in", _all_gather(chunks[:1], name="w_in_all_gather")[0])

    loss_blk, grad_x, gs, hb, got = _local_step(
        x[0], p[0, 0], loss_target[0], sm, {"w_in": w_in_full}, chunks[1:], core)

    g_big, d_big, nm_big, nv_big = {}, {}, {}, {}
    for n, h, r in zip(BIG, hb, got):
        g_big[n], d_big[n], nm_big[n], nv_big[n] = _adam_shard(h, r, chip, wts[n], mom[n], var[n], name="adam_" + n)

    s_sum, s_d, s_nm, s_nv = _small_all_reduce_adam(
        _pack_small(gs, loss_blk[0:1]), _pack_small(sm), _pack_small({n: mom[n] for n in SMALL}),
        _pack_small({n: var[n] for n in SMALL}))
    loss = s_sum[LOSS_ROW, 0]
    g_small, d_small, nm_small, nv_small = (_unpack_small(t, sm) for t in (s_sum, s_d, s_nm, s_nv))

    order = ["norm_mix_g", "w_in", "hg_lb_logits", "hg_onorm_g", "fox_f_bias", "fox_q_norm_g", "fox_k_norm_g",
             "w_branch_a", "w_branch_b", "w_out", "norm_ffn_g", "w_ffn_gate", "w_ffn_up", "w_ffn_down", "norm_ple_g",
             "w_ple_gate", "w_ple_proj"]
    outs = [loss, grad_x[None]]
    for big, small in ((g_big, g_small), (d_big, d_small), (nm_big, nm_small), (nv_big, nv_small)):
        outs += [big[n] if n in big else small[n] for n in order]
    return tuple(outs)
```

```python
import functools

import jax
import jax.numpy as jnp
from jax import lax
from jax.experimental import pallas as pl
from jax.experimental.pallas import tpu as pltpu

F32 = jnp.float32
BF16 = jnp.bfloat16

D_MODEL = 1024
PLE_DIM = 256
HG_HEADS = 4
HG_DK = 128
HG_CHUNK = 64
HG_SUB = 16
HG_W = HG_HEADS * HG_DK
FOX_HEADS = 8
FOX_DH = 64
FOX_W = FOX_HEADS * FOX_DH
D_FF = 2816
EPS = 1e-6
N_DEV = 8
N_CHIP = 4
LANES = 128
FOX_COLS = 3 * FOX_W + LANES
HG_COLS = 4 * HG_W
GATE_COLS = 2 * D_MODEL
IN_COLS = HG_COLS + 3 * FOX_W + FOX_HEADS + GATE_COLS
FOX_LOGICAL = 3 * FOX_W + FOX_HEADS
SEG = 2048
IN_PAD = 3 * SEG
IN_SHARD = IN_COLS // N_DEV
IN_SHARD_PAD = 768
FF_SHARD = D_FF // N_DEV
FF_SHARD_PAD = 384
FF_PAD = N_DEV * FF_SHARD_PAD
EXP_CLAMP = 80.0
LOG2E = 1.4426950408889634

ADAM_LR = 0.001
ADAM_B1 = 0.9
ADAM_B2 = 0.999
ADAM_EPS = 1e-08
ADAM_WD = 0.01
ADAM_STEP = 10

MESH = pl.DeviceIdType.MESH
VMEM_LIMIT = 56 * 1024 * 1024
ROW_BLOCK = 512

BIG = ["w_in", "w_branch_a", "w_branch_b", "w_out", "w_ffn_gate", "w_ffn_up", "w_ffn_down",
       "w_ple_gate", "w_ple_proj"]
BIG_SHAPE = {
    "w_in": (D_MODEL, IN_COLS, 1), "w_branch_a": (HG_W, D_MODEL, 1), "w_branch_b": (FOX_W, D_MODEL, 1),
    "w_out": (D_MODEL, D_MODEL, 0), "w_ffn_gate": (D_MODEL, D_FF, 1), "w_ffn_up": (D_MODEL, D_FF, 1),
    "w_ffn_down": (D_FF, D_MODEL, 0), "w_ple_gate": (D_MODEL, D_MODEL, 0), "w_ple_proj": (PLE_DIM, D_MODEL, 1),
}

SMALL = ["norm_mix_g", "hg_lb_logits", "hg_onorm_g", "fox_f_bias", "fox_q_norm_g", "fox_k_norm_g",
         "norm_ffn_g", "norm_ple_g"]
SMALL_ROWS = {"norm_mix_g": 8, "hg_lb_logits": 8, "hg_onorm_g": 1, "fox_f_bias": 1, "fox_q_norm_g": 1,
              "fox_k_norm_g": 1, "norm_ffn_g": 8, "norm_ple_g": 8}
SMALL_TOTAL = 40
LOSS_ROW = 36


def _pallas(body, **kw):
    return pl.pallas_call(body, **kw)


def _params(**kw):
    return pltpu.CompilerParams(vmem_limit_bytes=VMEM_LIMIT, **kw)


def _pick(n, target):
    if n <= target:
        return n
    best = None
    for t in range(LANES, target + 1, LANES):
        if n % t == 0:
            best = t
    assert best is not None, (n, target)
    return best


def _dot(a, b, ca, cb):
    return lax.dot_general(a, b, (((ca,), (cb,)), ((), ())), preferred_element_type=F32)


def _split_dot(mat, x, ca, cb, terms=2, mat_first=True):
    acc = None
    rem = x
    for _ in range(terms):
        part = rem.astype(BF16)
        rem = rem - part.astype(F32)
        p = _dot(mat, part, ca, cb) if mat_first else _dot(part, mat, ca, cb)
        acc = p if acc is None else acc + p
    return acc


def _sigmoid(x):
    return 1.0 / (1.0 + jnp.exp(-x))


def _iota(shape, dim):
    return lax.broadcasted_iota(jnp.int32, shape, dim)


def _matmul(a, b, *, name, ta=False, tb=False, out_dtype=F32, add=None, exchange=None, gather=None):
    assert exchange is None or gather is None
    (K, M) = a.shape if ta else a.shape[::-1]
    (N, Kb) = b.shape if tb else b.shape[::-1]
    assert K == Kb, (a.shape, b.shape, ta, tb)
    if ta:
        tm, tn, tk = _pick(M, 2 * ROW_BLOCK), _pick(N, 2 * ROW_BLOCK), _pick(K, 4 * ROW_BLOCK)
    else:
        tm, tn, tk = _pick(M, 2 * ROW_BLOCK), _pick(N, 2048), _pick(K, 3072)
    nk = K // tk
    use_scratch = nk > 1 and out_dtype != F32

    n_in = 2 + (add is not None)
    hs = list(exchange or gather or [])
    n_x = len(hs)
    n_sem = 3 if gather is not None else 2
    grid = (M // tm, N // tn, nk)

    def body(*refs):
        refs = list(refs)
        a_ref, b_ref = refs[:2]
        add_ref = refs[2] if add is not None else None
        o_ref = refs[n_in + n_x]
        k = pl.program_id(2)
        if n_x:
            steps = _gather_steps if gather is not None else _chip_exchange_steps
            ride = steps(refs[n_in:n_in + n_x], refs[n_in + n_x + 1:n_in + 2 * n_x + 1], *refs[-n_sem:])
            at = [pl.program_id(d) for d in range(3)]
            inner_first = jnp.logical_and(at[1] == 0, at[2] == 0)

            @pl.when(jnp.logical_and(at[0] == 0, inner_first))
            def _():
                ride[0]()

            if len(ride) == 3:
                @pl.when(jnp.logical_and(at[0] == grid[0] // 2, inner_first))
                def _():
                    ride[1]()
        p = _dot(a_ref[...].astype(BF16), b_ref[...].astype(BF16), 0 if ta else 1, 1 if tb else 0)

        def with_add(r):
            return r if add is None else r + add_ref[...].astype(F32)

        if nk == 1:
            o_ref[...] = with_add(p).astype(out_dtype)
        elif not use_scratch:
            @pl.when(k == 0)
            def _():
                o_ref[...] = with_add(p)

            @pl.when(k > 0)
            def _():
                o_ref[...] += p
        else:
            acc_ref = refs[n_in + 2 * n_x + 1]

            @pl.when(k == 0)
            def _():
                acc_ref[...] = p

            @pl.when(k > 0)
            def _():
                acc_ref[...] += p

            @pl.when(k == nk - 1)
            def _():
                o_ref[...] = with_add(acc_ref[...]).astype(out_dtype)

        if n_x:
            @pl.when(jnp.logical_and(at[0] == grid[0] - 1, jnp.logical_and(at[1] == grid[1] - 1, at[2] == nk - 1)))
            def _():
                ride[-1]()

    a_spec = pl.BlockSpec((tk, tm), lambda i, j, k: (k, i)) if ta else pl.BlockSpec((tm, tk), lambda i, j, k: (i, k))
    b_spec = pl.BlockSpec((tn, tk), lambda i, j, k: (j, k)) if tb else pl.BlockSpec((tk, tn), lambda i, j, k: (k, j))
    o_spec = pl.BlockSpec((tm, tn), lambda i, j, k: (i, j))
    hbm = pl.BlockSpec(memory_space=pl.ANY)
    in_specs = [a_spec, b_spec] + ([o_spec] if add is not None else []) + [hbm] * n_x
    args = (a, b) + ((add,) if add is not None else ()) + tuple(hs)
    if gather is not None:
        ride_shapes = [jax.ShapeDtypeStruct((N_DEV,) + h.shape, h.dtype) for h in hs]
        ride_scratch = _gather_scratch(n_x)
    else:
        ride_shapes = _chip_exchange_shapes(hs)
        ride_scratch = _chip_exchange_scratch(n_x) if n_x else []
    res = _pallas(
        body, name=name, grid=grid, in_specs=in_specs, out_specs=[o_spec] + [hbm] * n_x,
        out_shape=[jax.ShapeDtypeStruct((M, N), out_dtype)] + ride_shapes,
        scratch_shapes=([pltpu.VMEM((tm, tn), F32)] if use_scratch else []) + ride_scratch,
        compiler_params=_params(dimension_semantics=("arbitrary",) * 3),
    )(*args)
    return (res[0], list(res[1:])) if n_x else res[0]


def _row_map(nb, reverse, seg):
    if reverse:
        return lambda i: (nb - 1 - i, seg)
    return lambda i: (i, seg)


def _row_call(body, *, name, T, ins, outs, acc_outs=(), tm=ROW_BLOCK, reverse=False):
    tm = min(tm, T)
    nb = T // tm
    in_specs, args = [], []
    for arr, how in ins:
        args.append(arr)
        if how is True:
            in_specs.append(pl.BlockSpec((tm, arr.shape[1]), _row_map(nb, reverse, 0)))
        elif how is False:
            in_specs.append(pl.BlockSpec(arr.shape, lambda i, _n=arr.ndim: (0,) * _n))
        else:
            in_specs.append(pl.BlockSpec((tm, SEG), _row_map(nb, reverse, how[0])))
    out_specs, out_shape = [], []
    for o in outs:
        c, dt = o[0], o[1]
        total, seg = o[2] if len(o) > 2 else (c, 0)
        out_specs.append(pl.BlockSpec((tm, c), _row_map(nb, reverse, seg)))
        out_shape.append(jax.ShapeDtypeStruct((T, total), dt))
    for shp, dt in acc_outs:
        out_specs.append(pl.BlockSpec(shp, lambda i, _n=len(shp): (0,) * _n))
        out_shape.append(jax.ShapeDtypeStruct(shp, dt))
    return _pallas(body, name=name, grid=(nb,), in_specs=in_specs, out_specs=out_specs, out_shape=out_shape,
                   compiler_params=_params(dimension_semantics=("arbitrary",)))(*args)


def _rms_fwd(x, g, *, name):
    T = x.shape[0]

    def body(x_ref, g_ref, h_ref):
        xv = x_ref[...]
        rstd = lax.rsqrt(jnp.mean(xv * xv, axis=-1, keepdims=True) + EPS)
        h_ref[...] = (xv * rstd * g_ref[...]).astype(BF16)

    return _row_call(body, name=name, T=T, ins=[(x, True), (g, False)], outs=[(D_MODEL, BF16)])[0]


def _rms_bwd(x, g, dh, dres, *, name):
    T = x.shape[0]

    def body(x_ref, g_ref, dh_ref, dres_ref, dx_ref, dg_ref):
        xv = x_ref[...]
        rstd = lax.rsqrt(jnp.mean(xv * xv, axis=-1, keepdims=True) + EPS)
        xh = xv * rstd
        dhv = dh_ref[...]
        part = jnp.sum(dhv * xh, axis=0, keepdims=True)

        @pl.when(pl.program_id(0) == 0)
        def _():
            dg_ref[...] = part

        @pl.when(pl.program_id(0) > 0)
        def _():
            dg_ref[...] += part

        dxh = dhv * g_ref[...]
        dx_ref[...] = rstd * (dxh - xh * jnp.mean(dxh * xh, axis=-1, keepdims=True)) + dres_ref[...]

    return _row_call(body, name=name, T=T, ins=[(x, True), (g, False), (dh, True), (dres, True)],
                     outs=[(D_MODEL, F32)], acc_outs=[((1, D_MODEL), F32)])


def _merge_fwd(ua, ub, zg):
    def body(ua_ref, ub_ref, zg_ref, m_ref):
        ga = _sigmoid(zg_ref[:, :D_MODEL])
        gb = _sigmoid(zg_ref[:, D_MODEL:])
        m_ref[...] = (ga * ua_ref[...] + gb * ub_ref[...]).astype(BF16)

    return _row_call(body, name="merge_fwd", T=ua.shape[0], ins=[(ua, True), (ub, True), (zg, (2,))],
                     outs=[(D_MODEL, BF16)])[0]


def _merge_bwd(dm, ua, ub, zg):
    def body(dm_ref, ua_ref, ub_ref, zg_ref, dua_ref, dub_ref, dzg_ref):
        ga = _sigmoid(zg_ref[:, :D_MODEL])
        gb = _sigmoid(zg_ref[:, D_MODEL:])
        dmv = dm_ref[...]
        dua_ref[...] = (dmv * ga).astype(BF16)
        dub_ref[...] = (dmv * gb).astype(BF16)
        dzg_ref[:, :D_MODEL] = (dmv * ua_ref[...] * ga * (1.0 - ga)).astype(BF16)
        dzg_ref[:, D_MODEL:] = (dmv * ub_ref[...] * gb * (1.0 - gb)).astype(BF16)

    return _row_call(body, name="merge_bwd", T=dm.shape[0], ins=[(dm, True), (ua, True), (ub, True), (zg, (2,))],
                     outs=[(D_MODEL, BF16), (D_MODEL, BF16), (SEG, BF16, (IN_PAD, 2))])


def _swiglu_fwd(a, b):
    def body(a_ref, b_ref, o_ref):
        av = a_ref[...].astype(F32)
        o_ref[...] = (av * _sigmoid(av) * b_ref[...].astype(F32)).astype(BF16)

    return _row_call(body, name="swiglu_fwd", T=a.shape[0], ins=[(a, True), (b, True)], outs=[(a.shape[1], BF16)])[0]


def _swiglu_bwd(a, b, dact):
    def body(a_ref, b_ref, d_ref, da_ref, db_ref):
        av = a_ref[...].astype(F32)
        bv = b_ref[...].astype(F32)
        dv = d_ref[...]
        sg = _sigmoid(av)
        da_ref[...] = (dv * bv * sg * (1.0 + av * (1.0 - sg))).astype(BF16)
        db_ref[...] = (dv * av * sg).astype(BF16)

    return _row_call(body, name="swiglu_bwd", T=a.shape[0], ins=[(a, True), (b, True), (dact, True)],
                     outs=[(a.shape[1], BF16), (a.shape[1], BF16)])


def _ple_loss(x2, sp, pp, tgt):
    def body(x_ref, sp_ref, pp_ref, t_ref, dy_ref, dsp_ref, dpp_ref, loss_ref):
        gp = _sigmoid(sp_ref[...])
        ppv = pp_ref[...]
        err = x_ref[...] + gp * ppv - t_ref[...]
        part = 0.5 * jnp.sum(jnp.mean(err * err, axis=-1, keepdims=True), axis=0, keepdims=True)
        part = jnp.broadcast_to(part, loss_ref.shape)

        @pl.when(pl.program_id(0) == 0)
        def _():
            loss_ref[...] = part

        @pl.when(pl.program_id(0) > 0)
        def _():
            loss_ref[...] += part

        dy = err * (1.0 / D_MODEL)
        dy_ref[...] = dy
        dsp_ref[...] = (dy * ppv * gp * (1.0 - gp)).astype(BF16)
        dpp_ref[...] = (dy * gp).astype(BF16)

    return _row_call(body, name="ple_loss", T=x2.shape[0], ins=[(x2, True), (sp, True), (pp, True), (tgt, True)],
                     outs=[(D_MODEL, F32), (D_MODEL, BF16), (D_MODEL, BF16)], acc_outs=[((8, LANES), F32)])


def _hg_consts():
    C = HG_CHUNK
    r, c = _iota((C, C), 0), _iota((C, C), 1)
    tri = (c <= r)
    same = (r // HG_SUB) == (c // HG_SUB)
    return tri, (tri & same)


def _hg_chunk_fwd(q, f, lb, tri_b, sub_b):
    sgq = _sigmoid(q)
    qt = q * sgq
    sg = _sigmoid(f)
    fg = lb + (1.0 - lb) * sg
    kf = (1.0 - lb) * (1.0 - sg)
    logf = jnp.log(fg)
    b = _split_dot(tri_b, logf, 1, 0)
    w = _split_dot(sub_b, logf, 1, 0)
    return sgq, qt, sg, fg, kf, b, w


def _hg_scores(qs_b, kf, b, row):
    C, S = HG_CHUNK, HG_SUB
    parts, ks = [], []
    for blk in range(C // S):
        ref = jnp.zeros_like(b[0:1]) if blk == 0 else b[blk * S - 1:blk * S]
        e = jnp.exp(jnp.minimum(ref - b, EXP_CLAMP))
        e = jnp.where(row < (blk + 1) * S, e, 0.0)
        k_b = (kf * e).astype(BF16)
        ks.append((e, k_b))
        parts.append(_dot(qs_b[blk * S:(blk + 1) * S], k_b, 1, 1))
    return jnp.concatenate(parts, axis=0), ks


def _hgrn_fwd(z, lb_logits, gain, blocks):
    T = z.shape[0]
    RB = min(ROW_BLOCK, T)
    nb, cpb = T // RB, RB // HG_CHUNK
    C, DK = HG_CHUNK, HG_DK
    n = len(blocks)

    def body(*refs):
        z_ref, lg_ref, g_ref = refs[:3]
        o_ref, y_ref, st_ref = refs[3 + n:6 + n]
        s_ref = refs[6 + 2 * n]
        g_start, g_forward, g_finish = _gather_steps(refs[3:3 + n], refs[6 + n:6 + 2 * n], *refs[7 + 2 * n:])

        @pl.when(pl.program_id(0) == 0)
        def _():
            s_ref[...] = jnp.zeros_like(s_ref)
            g_start()

        @pl.when(pl.program_id(0) == nb // 2)
        def _():
            g_forward()

        lg = lg_ref[...]
        lb_all = 1.0 / (1.0 + jnp.exp(lg[1:2] - lg[0:1]))
        gain_v = g_ref[...]
        tri, sub = _hg_consts()
        tri_b, sub_b = tri.astype(BF16), sub.astype(BF16)
        row = _iota((C, DK), 0)

        def chunk(ci, carry):
            r0 = pl.multiple_of(ci * C, C)
            rows = pl.ds(r0, C)
            for h in range(HG_HEADS):
                cs = slice(h * DK, (h + 1) * DK)
                q = z_ref[rows, cs]
                f = z_ref[rows, HG_W + h * DK:HG_W + (h + 1) * DK]
                v = z_ref[rows, 2 * HG_W + h * DK:2 * HG_W + (h + 1) * DK]
                g = z_ref[rows, 3 * HG_W + h * DK:3 * HG_W + (h + 1) * DK]
                lb = lb_all[:, cs]
                _, qt, _, _, kf, b, w = _hg_chunk_fwd(q, f, lb, tri_b, sub_b)
                st = s_ref[h]
                st_ref[pl.ds(pl.multiple_of((ci * HG_HEADS + h) * DK, DK), DK), :] = st
                v_b = v.astype(BF16)
                qs_b = (qt * jnp.exp(w)).astype(BF16)
                a, _ = _hg_scores(qs_b, kf, b, row)
                a = jnp.where(tri, a, 0.0)
                qd = qt * jnp.exp(b)
                o = _dot(qd.astype(BF16), st.astype(BF16), 1, 1) + _dot(a.astype(BF16), v_b, 1, 0)
                bl = b[C - 1:C]
                kd = kf * jnp.exp(bl - b)
                s_ref[h] = st * jnp.exp(bl) + _dot(v_b, kd.astype(BF16), 0, 0)
                o_ref[rows, cs] = o
                rstd = lax.rsqrt(jnp.mean(o * o, axis=-1, keepdims=True) + EPS)
                y_ref[rows, cs] = (o * rstd * gain_v * (g * _sigmoid(g))).astype(BF16)
            return carry

        lax.fori_loop(0, cpb, chunk, 0)

        @pl.when(pl.program_id(0) == nb - 1)
        def _():
            g_finish()

    hbm = pl.BlockSpec(memory_space=pl.ANY)
    res = _pallas(
        body, name="hgrn_fwd", grid=(nb,),
        in_specs=[pl.BlockSpec((RB, HG_COLS), lambda i: (i, 0)), pl.BlockSpec((2, HG_W), lambda i: (0, 0)),
                  pl.BlockSpec((1, DK), lambda i: (0, 0))] + [hbm] * n,
        out_specs=[pl.BlockSpec((RB, HG_W), lambda i: (i, 0)), pl.BlockSpec((RB, HG_W), lambda i: (i, 0)),
                   pl.BlockSpec((cpb * HG_HEADS * DK, DK), lambda i: (i, 0))] + [hbm] * n,
        out_shape=[jax.ShapeDtypeStruct((T, HG_W), F32), jax.ShapeDtypeStruct((T, HG_W), BF16),
                   jax.ShapeDtypeStruct((T // C * HG_HEADS * DK, DK), F32)]
        + [jax.ShapeDtypeStruct((N_DEV,) + b.shape, b.dtype) for b in blocks],
        scratch_shapes=[pltpu.VMEM((HG_HEADS, DK, DK), F32)] + _gather_scratch(n),
        compiler_params=_params(dimension_semantics=("arbitrary",)),
    )(z, lb_logits, gain, *blocks)
    return res[0], res[1], res[2], res[3:]


def _hgrn_bwd(z, o_raw, dy, states, lb_logits, gain, dz_buf):
    T = z.shape[0]
    RB = min(ROW_BLOCK, T)
    nb, cpb = T // RB, RB // HG_CHUNK
    C, DK, S = HG_CHUNK, HG_DK, HG_SUB

    def body(z_ref, o_ref, dy_ref, st_ref, lg_ref, g_ref, _buf_ref, dz_ref, dlg_ref, dg_ref, ds_ref, dlb_ref):
        step = pl.program_id(0)

        @pl.when(step == 0)
        def _():
            ds_ref[...] = jnp.zeros_like(ds_ref)
            dlb_ref[...] = jnp.zeros_like(dlb_ref)
            dg_ref[...] = jnp.zeros_like(dg_ref)

        lg = lg_ref[...]
        lb_all = 1.0 / (1.0 + jnp.exp(lg[1:2] - lg[0:1]))
        gain_v = g_ref[...]
        tri, sub = _hg_consts()
        tri_b, sub_b = tri.astype(BF16), sub.astype(BF16)
        row = _iota((C, DK), 0)

        def chunk(cj, carry):
            ci = cpb - 1 - cj
            r0 = pl.multiple_of(ci * C, C)
            rows = pl.ds(r0, C)
            for h in range(HG_HEADS):
                cs = slice(h * DK, (h + 1) * DK)
                q = z_ref[rows, cs]
                f = z_ref[rows, HG_W + h * DK:HG_W + (h + 1) * DK]
                v = z_ref[rows, 2 * HG_W + h * DK:2 * HG_W + (h + 1) * DK]
                g = z_ref[rows, 3 * HG_W + h * DK:3 * HG_W + (h + 1) * DK]
                lb = lb_all[:, cs]
                sgq, qt, sg, fg, kf, b, w = _hg_chunk_fwd(q, f, lb, tri_b, sub_b)
                st = st_ref[pl.ds(pl.multiple_of((ci * HG_HEADS + h) * DK, DK), DK), :]
                dst = ds_ref[h]
                o = o_ref[rows, cs]
                dyv = dy_ref[rows, cs]
                rstd = lax.rsqrt(jnp.mean(o * o, axis=-1, keepdims=True) + EPS)
                n = o * rstd
                sgg = _sigmoid(g)
                t1 = dyv * (g * sgg)
                dg_ref[...] += jnp.sum(t1 * n, axis=0, keepdims=True)
                dn = t1 * gain_v
                do = rstd * (dn - n * jnp.mean(dn * n, axis=-1, keepdims=True))
                dgate = dyv * n * gain_v * sgg * (1.0 + g * (1.0 - sgg))
                do_b = do.astype(BF16)
                v_b = v.astype(BF16)
                ew = jnp.exp(w)
                qs_b = (qt * ew).astype(BF16)
                a, ks = _hg_scores(qs_b, kf, b, row)
                a = jnp.where(tri, a, 0.0)
                eb = jnp.exp(b)
                qd = qt * eb
                bl = b[C - 1:C]
                ebl = jnp.exp(bl)
                ekd = jnp.exp(bl - b)
                kd = kf * ekd
                kd_b = kd.astype(BF16)
                qd_b = qd.astype(BF16)
                dst_b = dst.astype(BF16)
                dqd = _dot(do_b, st.astype(BF16), 1, 0)
                da = jnp.where(tri, _dot(do_b, v_b, 1, 1), 0.0)
                dv = _dot(a.astype(BF16), do_b, 0, 0) + _dot(kd_b, dst_b, 1, 1)
                dkd = _dot(v_b, dst_b, 1, 0)
                ds_ref[h] = dst * ebl + _dot(do_b, qd_b, 0, 0)
                dkd_kd = dkd * kd_b.astype(F32)
                dbl = ebl * jnp.sum(dst * st, axis=0, keepdims=True) + jnp.sum(dkd_kd, axis=0, keepdims=True)
                da_b = da.astype(BF16)
                dqs_parts = []
                dk_in = jnp.zeros((C, DK), F32)
                db_k = jnp.zeros((C, DK), F32)
                for blk in range(C // S):
                    e, k_b = ks[blk]
                    da_blk = da_b[blk * S:(blk + 1) * S]
                    dqs_parts.append(_dot(da_blk, k_b, 1, 0))
                    dks = _dot(da_blk, qs_b[blk * S:(blk + 1) * S], 0, 0)
                    dk_in = dk_in + dks * e
                    db_k = db_k + dks * k_b.astype(F32)
                dqs = jnp.concatenate(dqs_parts, axis=0)
                dqt_in = dqs * ew
                db = qs_b.astype(F32) * dqs - db_k + dqd * qd_b.astype(F32) - dkd_kd
                db = db + jnp.where(row == C - 1, dbl, 0.0)
                dlogf = _split_dot(tri_b, db, 0, 0)
                dqt = dqt_in + dqd * eb
                dkf = dk_in + dkd * ekd
                dfg = dlogf / fg - dkf
                dlb_ref[:, cs] += jnp.sum(dfg * (1.0 - sg), axis=0, keepdims=True)
                dz_ref[rows, cs] = (dqt * sgq * (1.0 + q * (1.0 - sgq))).astype(BF16)
                dz_ref[rows, HG_W + h * DK:HG_W + (h + 1) * DK] = (dfg * (1.0 - lb) * sg * (1.0 - sg)).astype(BF16)
                dz_ref[rows, 2 * HG_W + h * DK:2 * HG_W + (h + 1) * DK] = dv.astype(BF16)
                dz_ref[rows, 3 * HG_W + h * DK:3 * HG_W + (h + 1) * DK] = dgate.astype(BF16)
            return carry

        lax.fori_loop(0, cpb, chunk, 0)

        @pl.when(step == nb - 1)
        def _():
            d0 = dlb_ref[...] * lb_all * (1.0 - lb_all)
            dlg_ref[0:1, :] = d0
            dlg_ref[1:2, :] = -d0

    rev = lambda i: (nb - 1 - i, 0)
    fix = lambda i: (0, 0)
    return _pallas(
        body, name="hgrn_bwd", grid=(nb,),
        in_specs=[pl.BlockSpec((RB, HG_COLS), rev), pl.BlockSpec((RB, HG_W), rev), pl.BlockSpec((RB, HG_W), rev),
                  pl.BlockSpec((cpb * HG_HEADS * DK, DK), rev), pl.BlockSpec((2, HG_W), fix),
                  pl.BlockSpec((1, DK), fix), pl.BlockSpec(memory_space=pl.ANY)],
        out_specs=[pl.BlockSpec((RB, HG_COLS), rev), pl.BlockSpec((2, HG_W), fix), pl.BlockSpec((1, DK), fix)],
        out_shape=[jax.ShapeDtypeStruct(dz_buf.shape, BF16), jax.ShapeDtypeStruct((2, HG_W), F32),
                   jax.ShapeDtypeStruct((1, DK), F32)],
        scratch_shapes=[pltpu.VMEM((HG_HEADS, DK, DK), F32), pltpu.VMEM((1, HG_W), F32)],
        input_output_aliases={6: 0},
        compiler_params=_params(dimension_semantics=("arbitrary",)),
    )(z, o_raw, dy, states, lb_logits, gain, dz_buf)


def _head_ones():
    r, c = _iota((FOX_W, FOX_W), 0), _iota((FOX_W, FOX_W), 1)
    return ((r // FOX_DH) == (c // FOX_DH)).astype(BF16)


def _log_sigmoid(x):
    return jnp.minimum(x, 0.0) - jnp.log(1.0 + jnp.exp(-jnp.abs(x)))


def _fox_prep(z, bias, qg, kg):
    T = z.shape[0]
    tm = min(ROW_BLOCK, T)
    nb = T // tm

    def body(z_ref, b_ref, qg_ref, kg_ref, q_ref, k_ref, v_ref, qa_ref, ka_ref, carry_ref):
        @pl.when(pl.program_id(0) == 0)
        def _():
            carry_ref[...] = jnp.zeros_like(carry_ref)

        ones = _head_ones()
        normed = []
        for src, g_ref in ((0, qg_ref), (1, kg_ref)):
            xv = z_ref[:, src * FOX_W:(src + 1) * FOX_W]
            ms = _split_dot(ones, xv * xv, 1, 0, mat_first=False) * (1.0 / FOX_DH)
            normed.append(xv * lax.rsqrt(ms + EPS) * g_ref[...])
        qn, kn = normed
        q_ref[...] = (qn * FOX_DH ** -0.5).astype(BF16)
        k_b = kn.astype(BF16)
        k_ref[...] = k_b
        v_ref[...] = z_ref[:, 2 * FOX_W:3 * FOX_W].astype(BF16)
        logf = _log_sigmoid(z_ref[:, 3 * FOX_W:FOX_COLS] + b_ref[...])
        r, c = _iota((tm, tm), 0), _iota((tm, tm), 1)
        tri_b = (c <= r).astype(BF16)
        cum = _split_dot(tri_b, logf, 1, 0, terms=3) + carry_ref[...]
        carry_ref[...] = cum[tm - 1:tm]
        c2 = cum * LOG2E
        hi = c2.astype(BF16)
        rem = c2 - hi.astype(F32)
        mid = rem.astype(BF16)
        lo = (rem - mid.astype(F32)).astype(BF16)
        hrow, col = _iota((LANES, 2 * FOX_W), 0), _iota((LANES, 2 * FOX_W), 1)
        base = hrow * LANES + jnp.where(hrow % 2 == 0, FOX_DH, 0)
        placed = None
        for t, part in enumerate((hi, mid, lo)):
            place = jnp.logical_and(col == base + t, hrow < FOX_HEADS).astype(BF16)
            term = _dot(part, place, 1, 0)
            placed = term if placed is None else placed + term
        colw = _iota((tm, 2 * FOX_W), 1)
        head, lane = colw // LANES, colw % LANES
        own = (lane < FOX_DH) == (head % 2 == 0)
        other = jnp.where(head % 2 == 0, lane - FOX_DH, lane)
        ones_q = jnp.where(jnp.logical_and(other >= 0, other < 3), -1.0, 0.0)
        q2 = (qn * (FOX_DH ** -0.5 * LOG2E)).astype(BF16)
        q_exp = jnp.concatenate([q2[:, (h // 2) * LANES:(h // 2 + 1) * LANES] for h in range(FOX_HEADS)], axis=1)
        k_exp = jnp.concatenate([k_b[:, (h // 2) * LANES:(h // 2 + 1) * LANES] for h in range(FOX_HEADS)], axis=1)
        qa_ref[...] = jnp.where(own, q_exp, ones_q.astype(BF16))
        ka_ref[...] = jnp.where(own, k_exp, placed.astype(BF16))

    wide = pl.BlockSpec((tm, 2 * FOX_W), lambda i: (i, 0))
    return _pallas(
        body, name="fox_prep", grid=(nb,),
        in_specs=[pl.BlockSpec((tm, SEG), lambda i: (i, 1)), pl.BlockSpec((1, LANES), lambda i: (0, 0)),
                  pl.BlockSpec((1, FOX_W), lambda i: (0, 0)), pl.BlockSpec((1, FOX_W), lambda i: (0, 0))],
        out_specs=[pl.BlockSpec((tm, FOX_W), lambda i: (i, 0))] * 3 + [wide] * 2,
        out_shape=[jax.ShapeDtypeStruct((T, FOX_W), BF16)] * 3 + [jax.ShapeDtypeStruct((T, 2 * FOX_W), BF16)] * 2,
        scratch_shapes=[pltpu.VMEM((1, LANES), F32)],
        compiler_params=_params(dimension_semantics=("arbitrary",)),
    )(z, bias, qg, kg)


def _fox_fwd(qa, ka, vb, blocks):
    T = qa.shape[0]
    tq = min(ROW_BLOCK, T)
    nq = T // tq
    NEG = -1e30
    n = len(blocks)
    n_pairs = FOX_HEADS // 2

    n_in = 3

    def body(*refs):
        q_ref, k_ref, v_ref = refs[:n_in]
        o_ref, lse_ref = refs[n_in + n:n_in + n + 2]
        m_sc, l_sc, acc_sc = refs[n_in + 2 * n + 2:n_in + 2 * n + 5]
        pr, qi, ki = pl.program_id(0), pl.program_id(1), pl.program_id(2)
        g_start, g_forward, g_finish = _gather_steps(
            refs[n_in:n_in + n], refs[n_in + n + 2:n_in + 2 * n + 2], *refs[n_in + 2 * n + 5:])
        row_start = jnp.logical_and(qi == 0, ki == 0)

        @pl.when(jnp.logical_and(pr == 0, row_start))
        def _():
            g_start()

        @pl.when(jnp.logical_and(pr == n_pairs // 2, row_start))
        def _():
            g_forward()

        @pl.when(ki == 0)
        def _():
            m_sc[...] = jnp.full_like(m_sc, NEG)
            l_sc[...] = jnp.zeros_like(l_sc)
            acc_sc[...] = jnp.zeros_like(acc_sc)

        lane = _iota((tq, LANES), 1)

        def block(masked):
            vv = v_ref[...]
            for hh in range(2):
                hs = slice(hh * LANES, (hh + 1) * LANES)
                s = _dot(q_ref[:, hs], k_ref[:, hs], 1, 1)
                tiles = [s[:, j * LANES:(j + 1) * LANES] for j in range(tq // LANES)]
                if masked:
                    row, col = _iota((tq, LANES), 0), _iota((tq, LANES), 1)
                    tiles = [jnp.where(row >= col + j * LANES, t, NEG) for j, t in enumerate(tiles)]
                m_old = m_sc[hh]
                top = jnp.broadcast_to(jnp.max(functools.reduce(jnp.maximum, tiles), axis=-1, keepdims=True),
                                       (tq, LANES))
                m_new = jnp.maximum(m_old, top)
                alpha = jnp.exp2(m_old - m_new)
                ps = [jnp.exp2(t - m_new) for t in tiles]
                l_sc[hh] = alpha * l_sc[hh] + functools.reduce(jnp.add, ps)
                m_sc[hh] = m_new
                p_b = jnp.concatenate([p.astype(BF16) for p in ps], axis=1)
                acc_sc[hh] = alpha * acc_sc[hh] + _dot(p_b, vv, 1, 0)

        @pl.when(ki < qi)
        def _():
            block(False)

        @pl.when(ki == qi)
        def _():
            block(True)
            l0 = jnp.sum(l_sc[0], axis=-1, keepdims=True)
            l1 = jnp.sum(l_sc[1], axis=-1, keepdims=True)
            o_ref[...] = jnp.where(lane < FOX_DH, acc_sc[0] * (1.0 / l0), acc_sc[1] * (1.0 / l1))
            lse_ref[:, :LANES] = m_sc[0] + jnp.log2(l0)
            lse_ref[:, LANES:] = m_sc[1] + jnp.log2(l1)

        @pl.when(jnp.logical_and(pr == n_pairs - 1, jnp.logical_and(qi == nq - 1, ki == nq - 1)))
        def _():
            g_finish()

    kmap = lambda p, i, j: (jnp.minimum(i, j), p)
    hbm = pl.BlockSpec(memory_space=pl.ANY)
    qrow = pl.BlockSpec((tq, 2 * LANES), lambda p, i, j: (i, p))
    res = _pallas(
        body, name="fox_fwd", grid=(n_pairs, nq, nq),
        in_specs=[qrow, pl.BlockSpec((tq, 2 * LANES), kmap), pl.BlockSpec((tq, LANES), kmap)] + [hbm] * n,
        out_specs=[pl.BlockSpec((tq, LANES), lambda p, i, j: (i, p)), qrow] + [hbm] * n,
        out_shape=[jax.ShapeDtypeStruct((T, FOX_W), F32), jax.ShapeDtypeStruct((T, 2 * FOX_W), F32)]
        + [jax.ShapeDtypeStruct((N_DEV,) + b.shape, b.dtype) for b in blocks],
        scratch_shapes=[pltpu.VMEM((2, tq, LANES), F32)] * 3 + _gather_scratch(n),
        compiler_params=_params(dimension_semantics=("arbitrary",) * 3),
    )(qa, ka, vb, *blocks)
    return res[0], res[1], res[2:]


def _fox_bwd(qs, kn, vb, qa, ka, o, do, lse, hs):
    T = qs.shape[0]
    tq = min(ROW_BLOCK, T)
    nq = T // tq
    n = len(hs)
    n_pairs = FOX_HEADS // 2

    def body(*refs):
        q_ref, k_ref, v_ref, qa_ref, ka_ref, o_ref, do_ref, lse_ref = refs[:8]
        dq_ref, dk_ref, dv_ref, dcs_ref, drs_ref = refs[8 + n:13 + n]
        pr, ki, qi = pl.program_id(0), pl.program_id(1), pl.program_id(2)
        x_start, x_finish = _chip_exchange_steps(refs[8:8 + n], refs[13 + n:13 + 2 * n], *refs[13 + 2 * n:])

        @pl.when(jnp.logical_and(pr == 0, jnp.logical_and(ki == 0, qi == 0)))
        def _():
            x_start()

        @pl.when(jnp.logical_and(ki == 0, qi == 0))
        def _():
            dq_ref[...] = jnp.zeros_like(dq_ref)

        @pl.when(jnp.logical_and(pr == 0, jnp.logical_and(ki == 0, qi == 0)))
        def _():
            drs_ref[...] = jnp.zeros_like(drs_ref)

        @pl.when(qi == ki)
        def _():
            dk_ref[...] = jnp.zeros_like(dk_ref)
            dv_ref[...] = jnp.zeros_like(dv_ref)
            dcs_ref[...] = jnp.zeros_like(dcs_ref)

        def block(masked):
            lane = _iota((tq, LANES), 1)
            qv, kv, vv = q_ref[...], k_ref[...], v_ref[...]
            ov, dov = o_ref[...], do_ref[...]
            qrows = pl.ds(pl.multiple_of(qi * tq, tq), tq)
            dq_acc = jnp.zeros((tq, LANES), F32)
            dk_acc = jnp.zeros((tq, LANES), F32)
            dv_acc = jnp.zeros((tq, LANES), F32)
            dcs_acc = jnp.zeros((8, tq), F32)
            drs_acc = jnp.zeros((tq, LANES), F32)
            prod = dov * ov
            nt = tq // LANES
            for hh in range(2):
                in_head = (lane < FOX_DH) if hh == 0 else (lane >= FOX_DH)
                hs_ = slice(hh * LANES, (hh + 1) * LANES)
                zb = jnp.zeros_like(qv)
                qm = jnp.where(in_head, qv, zb)
                km = jnp.where(in_head, kv, zb)
                dom = jnp.where(in_head, dov, 0.0).astype(BF16)
                delta_b = jnp.broadcast_to(jnp.sum(jnp.where(in_head, prod, 0.0), axis=1, keepdims=True), (tq, LANES))
                lse_b = lse_ref[:, hs_]
                s = _dot(qa_ref[:, hs_], ka_ref[:, hs_], 1, 1)
                dp = _dot(dom, vv, 1, 1)
                p_tiles, ds_tiles, col_tiles = [], [], []
                row_part = jnp.zeros((tq, LANES), F32)
                for j in range(nt):
                    js = slice(j * LANES, (j + 1) * LANES)
                    p = jnp.exp2(s[:, js] - lse_b)
                    if masked:
                        p = jnp.where(_iota((tq, LANES), 0) >= _iota((tq, LANES), 1) + j * LANES, p, 0.0)
                    ds = p * (dp[:, js] - delta_b)
                    p_tiles.append(p.astype(BF16))
                    ds_tiles.append(ds.astype(BF16))
                    col_tiles.append(jnp.sum(ds, axis=0, keepdims=True))
                    row_part = row_part + ds
                p_b = jnp.concatenate(p_tiles, axis=1)
                ds_b = jnp.concatenate(ds_tiles, axis=1)
                dv_acc = dv_acc + _dot(p_b, dom, 0, 0)
                dq_acc = dq_acc + _dot(ds_b, km, 1, 0)
                dk_acc = dk_acc + _dot(ds_b, qm, 0, 0)
                dcs_acc = dcs_acc + jnp.where(_iota((8, tq), 0) == hh, jnp.concatenate(col_tiles, axis=1), 0.0)
                rowsum = jnp.sum(row_part, axis=1, keepdims=True)
                drs_acc = drs_acc + jnp.where(lane == 2 * pr + hh, rowsum, 0.0)
            drs_ref[qrows, :] += drs_acc
            dq_ref[qrows, :] += dq_acc
            dk_ref[...] += dk_acc
            dv_ref[...] += dv_acc
            dcs_ref[0] += dcs_acc

        @pl.when(qi > ki)
        def _():
            block(False)

        @pl.when(qi == ki)
        def _():
            block(True)

        @pl.when(jnp.logical_and(pr == n_pairs - 1, jnp.logical_and(ki == nq - 1, qi == nq - 1)))
        def _():
            x_finish()

    qmap = lambda p, j, i: (jnp.maximum(i, j), p)
    hbm = pl.BlockSpec(memory_space=pl.ANY)
    res = _pallas(
        body, name="fox_bwd", grid=(n_pairs, nq, nq),
        in_specs=[pl.BlockSpec((tq, LANES), qmap), pl.BlockSpec((tq, LANES), lambda p, j, i: (j, p)),
                  pl.BlockSpec((tq, LANES), lambda p, j, i: (j, p)),
                  pl.BlockSpec((tq, 2 * LANES), qmap), pl.BlockSpec((tq, 2 * LANES), lambda p, j, i: (j, p)),
                  pl.BlockSpec((tq, LANES), qmap), pl.BlockSpec((tq, LANES), qmap), pl.BlockSpec((tq, 2 * LANES), qmap)]
        + [hbm] * n,
        out_specs=[pl.BlockSpec((T, LANES), lambda p, j, i: (0, p)), pl.BlockSpec((tq, LANES), lambda p, j, i: (j, p)),
                   pl.BlockSpec((tq, LANES), lambda p, j, i: (j, p)), pl.BlockSpec((1, 8, tq), lambda p, j, i: (p, 0, j)),
                   pl.BlockSpec((T, LANES), lambda p, j, i: (0, 0))] + [hbm] * n,
        out_shape=[jax.ShapeDtypeStruct((T, FOX_W), F32)] * 3
        + [jax.ShapeDtypeStruct((n_pairs, 8, T), F32), jax.ShapeDtypeStruct((T, LANES), F32)]
        + _chip_exchange_shapes(hs),
        scratch_shapes=_chip_exchange_scratch(n),
        compiler_params=_params(dimension_semantics=("arbitrary",) * 3),
    )(qs, kn, vb, qa, ka, o, do, lse, *hs)
    res = list(res)
    return res[:5] + [res[5:]]


def _fox_post(z, dq, dk, dv, dcs, drs, bias, qg, kg, dz_buf):
    T = z.shape[0]
    tm = min(ROW_BLOCK, T)
    nb = T // tm

    def body(z_ref, dq_ref, dk_ref, dv_ref, dcs_ref, drs_ref, b_ref, qg_ref, kg_ref, _buf_ref, dz_ref, dqg_ref, dkg_ref,
             db_ref, carry_ref):
        @pl.when(pl.program_id(0) == 0)
        def _():
            carry_ref[...] = jnp.zeros_like(carry_ref)
            dqg_ref[...] = jnp.zeros_like(dqg_ref)
            dkg_ref[...] = jnp.zeros_like(dkg_ref)
            db_ref[...] = jnp.zeros_like(db_ref)

        ones = _head_ones()
        for src, g_ref, d_ref, dg_ref, scale in ((0, qg_ref, dq_ref, dqg_ref, FOX_DH ** -0.5), (1, kg_ref, dk_ref, dkg_ref, 1.0)):
            xv = z_ref[:, src * FOX_W:(src + 1) * FOX_W]
            ms = _split_dot(ones, xv * xv, 1, 0, mat_first=False) * (1.0 / FOX_DH)
            rstd = lax.rsqrt(ms + EPS)
            xh = xv * rstd
            dn = d_ref[...] * scale
            dg_ref[...] += jnp.sum(dn * xh, axis=0, keepdims=True)
            dxh = dn * g_ref[...]
            mean = _split_dot(ones, dxh * xh, 1, 0, mat_first=False) * (1.0 / FOX_DH)
            dz_ref[:, src * FOX_W:(src + 1) * FOX_W] = (rstd * (dxh - xh * mean)).astype(BF16)
        dz_ref[:, 2 * FOX_W:3 * FOX_W] = dv_ref[...].astype(BF16)
        row8 = _iota((8, tm), 0)
        dct = jnp.zeros((8, tm), F32)
        for h in range(FOX_HEADS):
            src_row = dcs_ref[h // 2][h % 2:h % 2 + 1, :]
            dct = dct + jnp.where(row8 == h, src_row, 0.0)
        dct = drs_ref[...].T[0:8] - dct
        r, c = _iota((tm, tm), 0), _iota((tm, tm), 1)
        upper_b = (r >= c).astype(BF16)
        rc = _split_dot(upper_b, dct, 1, 0, mat_first=False) + carry_ref[...]
        carry_ref[...] = rc[:, 0:1]
        full = jnp.concatenate([rc, jnp.zeros((LANES - 8, tm), F32)], axis=0)
        dlogf = full.T
        xf = z_ref[:, 3 * FOX_W:FOX_COLS] + b_ref[...]
        df = dlogf * (1.0 - _sigmoid(xf))
        dz_ref[:, 3 * FOX_W:FOX_COLS] = df.astype(BF16)
        dz_ref[:, FOX_COLS:] = jnp.zeros((tm, SEG - FOX_COLS), BF16)
        db_ref[...] += jnp.sum(df, axis=0, keepdims=True)

    rev = lambda i: (nb - 1 - i, 0)
    fix2 = lambda i: (0, 0)
    return _pallas(
        body, name="fox_post", grid=(nb,),
        in_specs=[pl.BlockSpec((tm, SEG), lambda i: (nb - 1 - i, 1)), pl.BlockSpec((tm, FOX_W), rev),
                  pl.BlockSpec((tm, FOX_W), rev),
                  pl.BlockSpec((tm, FOX_W), rev), pl.BlockSpec((FOX_HEADS // 2, 8, tm), lambda i: (0, 0, nb - 1 - i)),
                  pl.BlockSpec((tm, LANES), rev),
                  pl.BlockSpec((1, LANES), fix2), pl.BlockSpec((1, FOX_W), fix2), pl.BlockSpec((1, FOX_W), fix2),
                  pl.BlockSpec(memory_space=pl.ANY)],
        out_specs=[pl.BlockSpec((tm, SEG), lambda i: (nb - 1 - i, 1)), pl.BlockSpec((1, FOX_W), fix2),
                   pl.BlockSpec((1, FOX_W), fix2), pl.BlockSpec((1, LANES), fix2)],
        out_shape=[jax.ShapeDtypeStruct(dz_buf.shape, BF16), jax.ShapeDtypeStruct((1, FOX_W), F32),
                   jax.ShapeDtypeStruct((1, FOX_W), F32), jax.ShapeDtypeStruct((1, LANES), F32)],
        scratch_shapes=[pltpu.VMEM((8, 1), F32)],
        input_output_aliases={9: 0},
        compiler_params=_params(dimension_semantics=("arbitrary",)),
    )(z, dq, dk, dv, dcs, drs, bias, qg, kg, dz_buf)


def _local_step(x, p, tgt, sm, W, rest_chunks, core):
    lbl, og, fb = sm["hg_lb_logits"], sm["hg_onorm_g"], sm["fox_f_bias"]
    fbias = jnp.pad(fb, ((0, 0), (0, LANES - FOX_HEADS)))
    qg = jnp.tile(sm["fox_q_norm_g"], (1, FOX_HEADS))
    kg = jnp.tile(sm["fox_k_norm_g"], (1, FOX_HEADS))

    h = _rms_fwd(x, sm["norm_mix_g"], name="rms_mix")
    rest = dict(zip(BIG[1:], rest_chunks))
    first, second = ["w_ffn_gate"], ["w_ffn_up", "w_ffn_down"]
    third = [n for n in BIG[1:] if n not in first + second]
    z, got1 = _matmul(h, W["w_in"], gather=[rest[n] for n in first], name="mm_z")
    o_raw, ya, states, got2 = _hgrn_fwd(z, lbl, og, [rest[n] for n in second])
    qs, kn, vb, qa, ka = _fox_prep(z, fbias, qg, kg)
    yb, lse, got3 = _fox_fwd(qa, ka, vb, [rest[n] for n in third])
    W = dict(W, **{n: _full_of_chunks(n, g)
                   for n, g in zip(first + second + third, list(got1) + list(got2) + list(got3))})
    ua = _matmul(ya, W["w_branch_a"], name="mm_ua")
    ub = _matmul(yb, W["w_branch_b"], name="mm_ub")
    merged = _merge_fwd(ua, ub, z)
    x1 = _matmul(merged, W["w_out"], add=x, name="mm_x1")
    hf = _rms_fwd(x1, sm["norm_ffn_g"], name="rms_ffn")
    a = _matmul(hf, W["w_ffn_gate"], out_dtype=BF16, name="mm_ffn_a")
    b = _matmul(hf, W["w_ffn_up"], out_dtype=BF16, name="mm_ffn_b")
    act = _swiglu_fwd(a, b)
    x2 = _matmul(act, W["w_ffn_down"], add=x1, name="mm_x2")
    hp = _rms_fwd(x2, sm["norm_ple_g"], name="rms_ple")
    sp = _matmul(hp, W["w_ple_gate"], name="mm_sp")
    pp = _matmul(p, W["w_ple_proj"], name="mm_pp")
    dy, dsp, dpp, loss = _ple_loss(x2, sp, pp, tgt)

    G = {}
    G["w_ple_proj"] = _matmul(p, dpp, ta=True, out_dtype=BF16, name="mm_dw_ple_proj")
    G["w_ple_gate"] = _matmul(hp, dsp, ta=True, out_dtype=BF16, name="mm_dw_ple_gate")
    dhp = _matmul(dsp, W["w_ple_gate"], tb=True, name="mm_dhp")
    dx2, d_ple_g = _rms_bwd(x2, sm["norm_ple_g"], dhp, dy, name="rms_ple_bwd")
    dact = _matmul(dx2, W["w_ffn_down"], tb=True, name="mm_dact")
    G["w_ffn_down"] = _matmul(act, dx2, ta=True, out_dtype=BF16, name="mm_dw_ffn_down")
    da, db = _swiglu_bwd(a, b, dact)
    G["w_ffn_gate"] = _matmul(hf, da, ta=True, out_dtype=BF16, name="mm_dw_ffn_gate")
    G["w_ffn_up"] = _matmul(hf, db, ta=True, out_dtype=BF16, name="mm_dw_ffn_up")
    dhf = _matmul(da, W["w_ffn_gate"], tb=True, name="mm_dhf_a")
    dhf = _matmul(db, W["w_ffn_up"], tb=True, add=dhf, name="mm_dhf_b")
    dx1, d_ffn_g = _rms_bwd(x1, sm["norm_ffn_g"], dhf, dx2, name="rms_ffn_bwd")
    dmerged = _matmul(dx1, W["w_out"], tb=True, name="mm_dmerged")
    G["w_out"] = _matmul(merged, dx1, ta=True, out_dtype=BF16, name="mm_dw_out")
    dua, dub, dz = _merge_bwd(dmerged, ua, ub, z)
    G["w_branch_a"] = _matmul(ya, dua, ta=True, out_dtype=BF16, name="mm_dw_branch_a")
    G["w_branch_b"] = _matmul(yb, dub, ta=True, out_dtype=BF16, name="mm_dw_branch_b")
    dya = _matmul(dua, W["w_branch_a"], tb=True, name="mm_dya")
    dyb = _matmul(dub, W["w_branch_b"], tb=True, name="mm_dyb")
    hb_rest = _sibling_sums({n: G[n] for n in BIG[1:]}, core, tag="rest")
    dq, dk, dv, dcs, drs, got_rest = _fox_bwd(qs, kn, vb, qa, ka, yb, dyb, lse, hb_rest)
    dz, d_qg, d_kg, d_fb = _fox_post(z, dq, dk, dv, dcs, drs, fbias, qg, kg, dz)
    dz, d_lbl, d_og = _hgrn_bwd(z, o_raw, dya, states, lbl, og, dz)
    G["w_in"] = _matmul(h, dz, ta=True, out_dtype=BF16, name="mm_dw_in")
    hb_in = _sibling_sums({"w_in": G["w_in"]}, core, tag="w_in")
    dh, got_in = _matmul(dz, W["w_in"], tb=True, exchange=hb_in, name="mm_dh")
    grad_x, d_mix_g = _rms_bwd(x, sm["norm_mix_g"], dh, dx1, name="rms_mix_bwd")

    gs = {"norm_mix_g": d_mix_g, "hg_lb_logits": d_lbl, "hg_onorm_g": d_og, "fox_f_bias": d_fb[:, :FOX_HEADS],
          "fox_q_norm_g": d_qg.reshape(FOX_HEADS, FOX_DH).sum(0, keepdims=True),
          "fox_k_norm_g": d_kg.reshape(FOX_HEADS, FOX_DH).sum(0, keepdims=True),
          "norm_ffn_g": d_ffn_g, "norm_ple_g": d_ple_g}
    return loss, grad_x, gs, hb_in + hb_rest, list(got_in) + list(got_rest)


def _pack_rows(parts, total):
    buf = jnp.concatenate(parts, axis=-2)
    pad = total - buf.shape[-2]
    widths = [(0, 0)] * (buf.ndim - 2) + [(0, pad), (0, 0)]
    return jnp.pad(buf, widths)


def _chunk_of_shard(n, w):
    if n == "w_in":
        return jnp.pad(w, ((0, 0), (0, IN_SHARD_PAD - IN_SHARD)))
    if n in ("w_ffn_gate", "w_ffn_up"):
        return jnp.pad(w, ((0, 0), (0, FF_SHARD_PAD - FF_SHARD)))
    if n == "w_ffn_down":
        return jnp.pad(w, ((0, FF_SHARD_PAD - FF_SHARD), (0, 0)))
    return w


def _full_of_chunks(n, g):
    _, a, b = g.shape
    if BIG_SHAPE[n][2] == 0:
        return g.reshape(N_DEV * a, b)
    if n == "w_in":
        w = g[:, :, :IN_SHARD].transpose(1, 0, 2).reshape(a, IN_COLS)
        gap = jnp.zeros((a, SEG - FOX_LOGICAL), g.dtype)
        return jnp.concatenate([w[:, :HG_COLS], w[:, HG_COLS:HG_COLS + FOX_LOGICAL], gap, w[:, HG_COLS + FOX_LOGICAL:]],
                               axis=1)
    return g.transpose(1, 0, 2).reshape(a, N_DEV * b)


def _chunks_of_full(n, g):
    if BIG_SHAPE[n][2] == 0:
        return g.reshape(N_DEV, g.shape[0] // N_DEV, g.shape[1])
    if n == "w_in":
        a = g.shape[0]
        w = jnp.concatenate([g[:, :HG_COLS], g[:, SEG:SEG + FOX_LOGICAL], g[:, 2 * SEG:]], axis=1)
        w = w.reshape(a, N_DEV, IN_SHARD).transpose(1, 0, 2)
        return jnp.pad(w, ((0, 0), (0, 0), (0, IN_SHARD_PAD - IN_SHARD)))
    return g.reshape(g.shape[0], N_DEV, g.shape[1] // N_DEV).transpose(1, 0, 2)


def _pack_small(vals, loss_row=None):
    parts = [vals[n].reshape(SMALL_ROWS[n], -1) for n in SMALL]
    parts = [jnp.pad(v, ((0, 0), (0, LANES - v.shape[1]))) for v in parts]
    if loss_row is not None:
        parts.append(loss_row)
    return _pack_rows(parts, SMALL_TOTAL)


def _unpack_small(buf, like):
    out, r0 = {}, 0
    for n in SMALL:
        rows, size = SMALL_ROWS[n], like[n].size
        blk = buf[r0:r0 + rows]
        out[n] = (blk if size == rows * LANES else blk[:, :size]).reshape(like[n].shape)
        r0 += rows
    return out


def _place():
    return lax.axis_index("x"), lax.axis_index("y"), lax.axis_index("c")


def _gather_steps(x_refs, out_refs, send_sems, recv_sems, local_sems):
    n = len(x_refs)
    x, y, c = _place()
    me, sibling = (x, y, c), (x, y, 1 - c)
    chips = [(1 - x, y), (x, 1 - y), (1 - x, 1 - y)]

    def slot(i, px, py, pc):
        return out_refs[i].at[4 * px + 2 * py + pc]

    def copy(k, i, blk, to, own=False):
        return pltpu.make_async_remote_copy(
            src_ref=x_refs[i] if own else slot(i, *blk), dst_ref=slot(i, *blk),
            send_sem=send_sems.at[k, i], recv_sem=recv_sems.at[k, i], device_id=to, device_id_type=MESH)

    def mine():
        return [pltpu.make_async_copy(x_refs[i], slot(i, *me), local_sems.at[i]) for i in range(n)]

    def first():
        cps = [copy(0, i, me, sibling, own=True) for i in range(n)]
        return cps + [copy(1 + j, i, me, (*chip, c), own=True) for j, chip in enumerate(chips) for i in range(n)]

    def passed():
        return [copy(4 + j, i, (*chip, c), sibling) for j, chip in enumerate(chips) for i in range(n)]

    def start():
        for cp in mine() + first():
            cp.start()

    def forward():
        fws = passed()
        for j, chip in enumerate(chips):
            for i in range(n):
                copy(1 + j, i, (*chip, c), me).wait_recv()
                fws[j * n + i].start()

    def finish():
        for i in range(n):
            copy(0, i, sibling, me).wait_recv()
        for j, chip in enumerate(chips):
            for i in range(n):
                copy(4 + j, i, (*chip, 1 - c), me).wait_recv()
        for cp in first() + passed():
            cp.wait_send()
        for cp in mine():
            cp.wait()

    return start, forward, finish


def _gather_scratch(n):
    return [pltpu.SemaphoreType.DMA((7, n)), pltpu.SemaphoreType.DMA((7, n)), pltpu.SemaphoreType.DMA((n,))]


def _all_gather(blocks, *, name):
    n = len(blocks)

    def body(*refs):
        for step in _gather_steps(refs[:n], refs[n:2 * n], *refs[2 * n:]):
            step()

    hbm = pl.BlockSpec(memory_space=pl.ANY)
    return _pallas(
        body, name=name, out_shape=[jax.ShapeDtypeStruct((N_DEV,) + b.shape, b.dtype) for b in blocks],
        in_specs=[hbm] * n, out_specs=[hbm] * n, scratch_shapes=_gather_scratch(n),
    )(*blocks)


def _sibling_exchange(gs, *, name):
    n = len(gs)

    def body(*refs):
        g_refs, out_refs = refs[:n], refs[n:2 * n]
        send_sems, recv_sems = refs[2 * n:]
        x, y, c = _place()
        cps = [pltpu.make_async_remote_copy(
            src_ref=g_refs[i].at[:, pl.ds(1 - c, 1)], dst_ref=out_refs[i], send_sem=send_sems.at[i],
            recv_sem=recv_sems.at[i], device_id=(x, y, 1 - c), device_id_type=MESH) for i in range(n)]
        for cp in cps:
            cp.start()
        for cp in cps:
            cp.wait()

    hbm = pl.BlockSpec(memory_space=pl.ANY)
    return _pallas(
        body, name=name, out_shape=[jax.ShapeDtypeStruct((N_CHIP, 1) + g.shape[2:], g.dtype) for g in gs],
        in_specs=[hbm] * n, out_specs=[hbm] * n,
        scratch_shapes=[pltpu.SemaphoreType.DMA((n,)), pltpu.SemaphoreType.DMA((n,))],
    )(*gs)


def _chip_sum(g4, got, core, *, name):
    _, _, a, b = g4.shape

    def body(c_ref, g_ref, r_ref, h_ref):
        h_ref[0] = (g_ref[0, 0].astype(F32) + r_ref[0, 0].astype(F32)).astype(BF16)

    grid_spec = pltpu.PrefetchScalarGridSpec(
        num_scalar_prefetch=1, grid=(N_CHIP,),
        in_specs=[pl.BlockSpec((1, 1, a, b), lambda j, c: (j, c[0], 0, 0)),
                  pl.BlockSpec((1, 1, a, b), lambda j, c: (j, 0, 0, 0))],
        out_specs=[pl.BlockSpec((1, a, b), lambda j, c: (j, 0, 0))])
    return _pallas(
        body, name=name, grid_spec=grid_spec, out_shape=[jax.ShapeDtypeStruct((N_CHIP, a, b), BF16)],
        compiler_params=_params(dimension_semantics=("arbitrary",)),
    )(core, g4, got)[0]


def _chip_exchange(hs, *, name):
    n = len(hs)

    def body(*refs):
        for step in _chip_exchange_steps(refs[:n], refs[n:2 * n], *refs[2 * n:]):
            step()

    hbm = pl.BlockSpec(memory_space=pl.ANY)
    return _pallas(
        body, name=name, out_shape=_chip_exchange_shapes(hs), in_specs=[hbm] * n, out_specs=[hbm] * n,
        scratch_shapes=_chip_exchange_scratch(n),
    )(*hs)


def _chip_exchange_steps(h_refs, out_refs, send_sems, recv_sems):
    n = len(h_refs)
    x, y, c = _place()
    chips = [(1 - x, y), (x, 1 - y), (1 - x, 1 - y)]

    def copies():
        return [pltpu.make_async_remote_copy(
            src_ref=h_refs[i].at[2 * px + py], dst_ref=out_refs[i].at[k], send_sem=send_sems.at[k, i],
            recv_sem=recv_sems.at[k, i], device_id=(px, py, c), device_id_type=MESH)
            for k, (px, py) in enumerate(chips) for i in range(n)]

    def start():
        for cp in copies():
            cp.start()

    def finish():
        for cp in copies():
            cp.wait()

    return start, finish


def _chip_exchange_shapes(hs):
    return [jax.ShapeDtypeStruct((3,) + h.shape[1:], h.dtype) for h in hs]


def _chip_exchange_scratch(n):
    return [pltpu.SemaphoreType.DMA((3, n)), pltpu.SemaphoreType.DMA((3, n))]


def _sibling_sums(G, core, *, tag):
    g4 = []
    for n, g in G.items():
        gc = _chunks_of_full(n, g)
        g4.append(gc.reshape((N_CHIP, 2) + gc.shape[1:]))
    got = _sibling_exchange(g4, name="grads_to_sibling_" + tag)
    return [_chip_sum(g, r, core, name="chip_sum_" + n) for n, g, r in zip(G, g4, got)]


def _adam_math(w, g, m, v):
    m = ADAM_B1 * m + (1.0 - ADAM_B1) * g
    v = ADAM_B2 * v + (1.0 - ADAM_B2) * (g * g)
    m_hat = m / (1.0 - ADAM_B1 ** ADAM_STEP)
    v_hat = v / (1.0 - ADAM_B2 ** ADAM_STEP)
    delta = -ADAM_LR * (m_hat / (jnp.sqrt(v_hat) + ADAM_EPS) + ADAM_WD * w)
    return delta, m, v


def _adam_shard(hb, got, chip, w, m, v, *, name):
    _, r, c = w.shape
    _, a, b = hb.shape
    tr = r if r <= 512 else 256
    ta = tr if r // tr > 1 else a

    def body(j_ref, h_ref, r_ref, w_ref, m_ref, v_ref, g_ref, d_ref, nm_ref, nv_ref):
        parts = [h_ref[0], r_ref[0], r_ref[1], r_ref[2]]
        g = None
        for part in parts:
            part = part[:tr, :c].astype(F32)
            g = part if g is None else g + part
        d, nm, nv = _adam_math(w_ref[0], g, m_ref[0], v_ref[0])
        g_ref[0] = g
        d_ref[0] = d
        nm_ref[0] = nm
        nv_ref[0] = nv

    blk = pl.BlockSpec((1, tr, c), lambda i, j: (0, i, 0))
    grid_spec = pltpu.PrefetchScalarGridSpec(
        num_scalar_prefetch=1, grid=(r // tr,),
        in_specs=[pl.BlockSpec((1, ta, b), lambda i, j: (j[0], i, 0)),
                  pl.BlockSpec((3, ta, b), lambda i, j: (0, i, 0)), blk, blk, blk],
        out_specs=[blk] * 4)
    return _pallas(
        body, name=name, grid_spec=grid_spec, out_shape=[jax.ShapeDtypeStruct((1, r, c), F32)] * 4,
        compiler_params=_params(dimension_semantics=("arbitrary",)),
    )(chip, hb, got, w, m, v)


def _small_all_reduce_adam(gs, w, m, v):
    def body(g_ref, w_ref, m_ref, v_ref, sum_ref, d_ref, nm_ref, nv_ref, gather, send_sems, recv_sems):
        x, y, c = _place()
        my = 4 * x + 2 * y + c
        gather[my] = g_ref[...]
        cps = []
        for k in range(1, N_DEV):
            to = (x ^ (k >> 2), y ^ ((k >> 1) & 1), c ^ (k & 1))
            cps.append(pltpu.make_async_remote_copy(
                src_ref=g_ref, dst_ref=gather.at[my], send_sem=send_sems.at[k - 1], recv_sem=recv_sems.at[k - 1],
                device_id=to, device_id_type=MESH))
        for cp in cps:
            cp.start()
        for cp in cps:
            cp.wait()
        total = gather[0]
        for d in range(1, N_DEV):
            total = total + gather[d]
        dlt, nm, nv = _adam_math(w_ref[...], total, m_ref[...], v_ref[...])
        sum_ref[...] = total
        d_ref[...] = dlt
        nm_ref[...] = nm
        nv_ref[...] = nv

    vm = pl.BlockSpec(memory_space=pltpu.VMEM)
    return _pallas(
        body, name="small_all_reduce_adam", out_shape=[jax.ShapeDtypeStruct((SMALL_TOTAL, LANES), F32)] * 4,
        in_specs=[vm] * 4, out_specs=[vm] * 4,
        scratch_shapes=[pltpu.VMEM((N_DEV, SMALL_TOTAL, LANES), F32), pltpu.SemaphoreType.DMA((7,)),
                        pltpu.SemaphoreType.DMA((7,))],
            )(gs, w, m, v)


def kernel(x, p, norm_mix_g, w_in, hg_lb_logits, hg_onorm_g, fox_f_bias, fox_q_norm_g, fox_k_norm_g, w_branch_a, w_branch_b, w_out, norm_ffn_g, w_ffn_gate, w_ffn_up, w_ffn_down, norm_ple_g, w_ple_gate, w_ple_proj, loss_target, m_norm_mix_g, m_w_in, m_hg_lb_logits, m_hg_onorm_g, m_fox_f_bias, m_fox_q_norm_g, m_fox_k_norm_g, m_w_branch_a, m_w_branch_b, m_w_out, m_norm_ffn_g, m_w_ffn_gate, m_w_ffn_up, m_w_ffn_down, m_norm_ple_g, m_w_ple_gate, m_w_ple_proj, v_norm_mix_g, v_w_in, v_hg_lb_logits, v_hg_onorm_g, v_fox_f_bias, v_fox_q_norm_g, v_fox_k_norm_g, v_w_branch_a, v_w_branch_b, v_w_out, v_norm_ffn_g, v_w_ffn_gate, v_w_ffn_up, v_w_ffn_down, v_norm_ple_g, v_w_ple_gate, v_w_ple_proj):
    args = dict(locals())
    wts = {n: args[n] for n in BIG + SMALL}
    mom = {n: args["m_" + n] for n in BIG + SMALL}
    var = {n: args["v_" + n] for n in BIG + SMALL}
    sm = {n: wts[n] for n in SMALL}

    xi, yi, ci = _place()
    core = jnp.reshape(ci, (1,)).astype(jnp.int32)
    chip = jnp.reshape(2 * xi + yi, (1,)).astype(jnp.int32)
    chunks = [_chunk_of_shard(n, wts[n][0].astype(BF16)) for n in BIG]
    assert BIG[0] == "w_in"
    w_in_full = _full_of_chunks("w_in", _all_gather(chunks[:1], name="w_in_all_gather")[0])

    loss_blk, grad_x, gs, hb, got = _local_step(
        x[0], p[0, 0], loss_target[0], sm, {"w_in": w_in_full}, chunks[1:], core)

    g_big, d_big, nm_big, nv_big = {}, {}, {}, {}
    for n, h, r in zip(BIG, hb, got):
        g_big[n], d_big[n], nm_big[n], nv_big[n] = _adam_shard(h, r, chip, wts[n], mom[n], var[n], name="adam_" + n)

    s_sum, s_d, s_nm, s_nv = _small_all_reduce_adam(
        _pack_small(gs, loss_blk[0:1]), _pack_small(sm), _pack_small({n: mom[n] for n in SMALL}),
        _pack_small({n: var[n] for n in SMALL}))
    loss = s_sum[LOSS_ROW, 0]
    g_small, d_small, nm_small, nv_small = (_unpack_small(t, sm) for t in (s_sum, s_d, s_nm, s_nv))

    order = ["norm_mix_g", "w_in", "hg_lb_logits", "hg_onorm_g", "fox_f_bias", "fox_q_norm_g", "fox_k_norm_g",
             "w_branch_a", "w_branch_b", "w_out", "norm_ffn_g", "w_ffn_gate", "w_ffn_up", "w_ffn_down", "norm_ple_g",
             "w_ple_gate", "w_ple_proj"]
    outs = [loss, grad_x[None]]
    for big, small in ((g_big, g_small), (d_big, d_small), (nm_big, nm_small), (nv_big, nv_small)):
        outs += [big[n] if n in big else small[n] for n in order]
    return tuple(outs)
```

```python
import functools

import jax
import jax.numpy as jnp
from jax import lax
from jax.experimental import pallas as pl
from jax.experimental.pallas import tpu as pltpu

F32 = jnp.float32
BF16 = jnp.bfloat16

D_MODEL = 1024
PLE_DIM = 256
HG_HEADS = 4
HG_DK = 128
HG_CHUNK = 64
HG_SUB = 16
HG_W = HG_HEADS * HG_DK
FOX_HEADS = 8
FOX_DH = 64
FOX_W = FOX_HEADS * FOX_DH
D_FF = 2816
EPS = 1e-6
N_DEV = 8
N_CHIP = 4
LANES = 128
FOX_COLS = 3 * FOX_W + LANES
HG_COLS = 4 * HG_W
GATE_COLS = 2 * D_MODEL
IN_COLS = HG_COLS + 3 * FOX_W + FOX_HEADS + GATE_COLS
FOX_LOGICAL = 3 * FOX_W + FOX_HEADS
SEG = 2048
IN_PAD = 3 * SEG
IN_SHARD = IN_COLS // N_DEV
IN_SHARD_PAD = 768
FF_SHARD = D_FF // N_DEV
FF_SHARD_PAD = 384
FF_PAD = N_DEV * FF_SHARD_PAD
EXP_CLAMP = 80.0
LOG2E = 1.4426950408889634

ADAM_LR = 0.001
ADAM_B1 = 0.9
ADAM_B2 = 0.999
ADAM_EPS = 1e-08
ADAM_WD = 0.01
ADAM_STEP = 10

MESH = pl.DeviceIdType.MESH
VMEM_LIMIT = 56 * 1024 * 1024
ROW_BLOCK = 512

BIG = ["w_in", "w_branch_a", "w_branch_b", "w_out", "w_ffn_gate", "w_ffn_up", "w_ffn_down",
       "w_ple_gate", "w_ple_proj"]
BIG_SHAPE = {
    "w_in": (D_MODEL, IN_COLS, 1), "w_branch_a": (HG_W, D_MODEL, 1), "w_branch_b": (FOX_W, D_MODEL, 1),
    "w_out": (D_MODEL, D_MODEL, 0), "w_ffn_gate": (D_MODEL, D_FF, 1), "w_ffn_up": (D_MODEL, D_FF, 1),
    "w_ffn_down": (D_FF, D_MODEL, 0), "w_ple_gate": (D_MODEL, D_MODEL, 0), "w_ple_proj": (PLE_DIM, D_MODEL, 1),
}

SMALL = ["norm_mix_g", "hg_lb_logits", "hg_onorm_g", "fox_f_bias", "fox_q_norm_g", "fox_k_norm_g",
         "norm_ffn_g", "norm_ple_g"]
SMALL_ROWS = {"norm_mix_g": 8, "hg_lb_logits": 8, "hg_onorm_g": 1, "fox_f_bias": 1, "fox_q_norm_g": 1,
              "fox_k_norm_g": 1, "norm_ffn_g": 8, "norm_ple_g": 8}
SMALL_TOTAL = 40
LOSS_ROW = 36


def _pallas(body, **kw):
    return pl.pallas_call(body, **kw)


def _params(**kw):
    return pltpu.CompilerParams(vmem_limit_bytes=VMEM_LIMIT, **kw)


def _pick(n, target):
    if n <= target:
        return n
    best = None
    for t in range(LANES, target + 1, LANES):
        if n % t == 0:
            best = t
    assert best is not None, (n, target)
    return best


def _dot(a, b, ca, cb):
    return lax.dot_general(a, b, (((ca,), (cb,)), ((), ())), preferred_element_type=F32)


def _split_dot(mat, x, ca, cb, terms=2, mat_first=True):
    acc = None
    rem = x
    for _ in range(terms):
        part = rem.astype(BF16)
        rem = rem - part.astype(F32)
        p = _dot(mat, part, ca, cb) if mat_first else _dot(part, mat, ca, cb)
        acc = p if acc is None else acc + p
    return acc


def _sigmoid(x):
    return 1.0 / (1.0 + jnp.exp(-x))


def _iota(shape, dim):
    return lax.broadcasted_iota(jnp.int32, shape, dim)


def _matmul(a, b, *, name, ta=False, tb=False, out_dtype=F32, add=None, exchange=None, gather=None):
    assert exchange is None or gather is None
    (K, M) = a.shape if ta else a.shape[::-1]
    (N, Kb) = b.shape if tb else b.shape[::-1]
    assert K == Kb, (a.shape, b.shape, ta, tb)
    if ta:
        tm, tn, tk = _pick(M, 2 * ROW_BLOCK), _pick(N, 2 * ROW_BLOCK), _pick(K, 4 * ROW_BLOCK)
    else:
        tm, tn, tk = _pick(M, 2 * ROW_BLOCK), _pick(N, 2048), _pick(K, 3072)
    nk = K // tk
    use_scratch = nk > 1 and out_dtype != F32

    n_in = 2 + (add is not None)
    hs = list(exchange or gather or [])
    n_x = len(hs)
    n_sem = 3 if gather is not None else 2
    grid = (M // tm, N // tn, nk)

    def body(*refs):
        refs = list(refs)
        a_ref, b_ref = refs[:2]
        add_ref = refs[2] if add is not None else None
        o_ref = refs[n_in + n_x]
        k = pl.program_id(2)
        if n_x:
            steps = _gather_steps if gather is not None else _chip_exchange_steps
            ride = steps(refs[n_in:n_in + n_x], refs[n_in + n_x + 1:n_in + 2 * n_x + 1], *refs[-n_sem:])
            at = [pl.program_id(d) for d in range(3)]

            @pl.when(jnp.logical_and(at[0] == 0, jnp.logical_and(at[1] == 0, at[2] == 0)))
            def _():
                ride[0]()
        p = _dot(a_ref[...].astype(BF16), b_ref[...].astype(BF16), 0 if ta else 1, 1 if tb else 0)

        def with_add(r):
            return r if add is None else r + add_ref[...].astype(F32)

        if nk == 1:
            o_ref[...] = with_add(p).astype(out_dtype)
        elif not use_scratch:
            @pl.when(k == 0)
            def _():
                o_ref[...] = with_add(p)

            @pl.when(k > 0)
            def _():
                o_ref[...] += p
        else:
            acc_ref = refs[n_in + 2 * n_x + 1]

            @pl.when(k == 0)
            def _():
                acc_ref[...] = p

            @pl.when(k > 0)
            def _():
                acc_ref[...] += p

            @pl.when(k == nk - 1)
            def _():
                o_ref[...] = with_add(acc_ref[...]).astype(out_dtype)

        if n_x:
            @pl.when(jnp.logical_and(at[0] == grid[0] - 1, jnp.logical_and(at[1] == grid[1] - 1, at[2] == nk - 1)))
            def _():
                for step in ride[1:]:
                    step()

    a_spec = pl.BlockSpec((tk, tm), lambda i, j, k: (k, i)) if ta else pl.BlockSpec((tm, tk), lambda i, j, k: (i, k))
    b_spec = pl.BlockSpec((tn, tk), lambda i, j, k: (j, k)) if tb else pl.BlockSpec((tk, tn), lambda i, j, k: (k, j))
    o_spec = pl.BlockSpec((tm, tn), lambda i, j, k: (i, j))
    hbm = pl.BlockSpec(memory_space=pl.ANY)
    in_specs = [a_spec, b_spec] + ([o_spec] if add is not None else []) + [hbm] * n_x
    args = (a, b) + ((add,) if add is not None else ()) + tuple(hs)
    if gather is not None:
        ride_shapes = [jax.ShapeDtypeStruct((N_DEV,) + h.shape, h.dtype) for h in hs]
        ride_scratch = _gather_scratch(n_x)
    else:
        ride_shapes = _chip_exchange_shapes(hs)
        ride_scratch = _chip_exchange_scratch(n_x) if n_x else []
    res = _pallas(
        body, name=name, grid=grid, in_specs=in_specs, out_specs=[o_spec] + [hbm] * n_x,
        out_shape=[jax.ShapeDtypeStruct((M, N), out_dtype)] + ride_shapes,
        scratch_shapes=([pltpu.VMEM((tm, tn), F32)] if use_scratch else []) + ride_scratch,
        compiler_params=_params(dimension_semantics=("arbitrary",) * 3),
    )(*args)
    return (res[0], list(res[1:])) if n_x else res[0]


def _row_map(nb, reverse, seg):
    if reverse:
        return lambda i: (nb - 1 - i, seg)
    return lambda i: (i, seg)


def _row_call(body, *, name, T, ins, outs, acc_outs=(), tm=ROW_BLOCK, reverse=False):
    tm = min(tm, T)
    nb = T // tm
    in_specs, args = [], []
    for arr, how in ins:
        args.append(arr)
        if how is True:
            in_specs.append(pl.BlockSpec((tm, arr.shape[1]), _row_map(nb, reverse, 0)))
        elif how is False:
            in_specs.append(pl.BlockSpec(arr.shape, lambda i, _n=arr.ndim: (0,) * _n))
        else:
            in_specs.append(pl.BlockSpec((tm, SEG), _row_map(nb, reverse, how[0])))
    out_specs, out_shape = [], []
    for o in outs:
        c, dt = o[0], o[1]
        total, seg = o[2] if len(o) > 2 else (c, 0)
        out_specs.append(pl.BlockSpec((tm, c), _row_map(nb, reverse, seg)))
        out_shape.append(jax.ShapeDtypeStruct((T, total), dt))
    for shp, dt in acc_outs:
        out_specs.append(pl.BlockSpec(shp, lambda i, _n=len(shp): (0,) * _n))
        out_shape.append(jax.ShapeDtypeStruct(shp, dt))
    return _pallas(body, name=name, grid=(nb,), in_specs=in_specs, out_specs=out_specs, out_shape=out_shape,
                   compiler_params=_params(dimension_semantics=("arbitrary",)))(*args)


def _rms_fwd(x, g, *, name):
    T = x.shape[0]

    def body(x_ref, g_ref, h_ref):
        xv = x_ref[...]
        rstd = lax.rsqrt(jnp.mean(xv * xv, axis=-1, keepdims=True) + EPS)
        h_ref[...] = (xv * rstd * g_ref[...]).astype(BF16)

    return _row_call(body, name=name, T=T, ins=[(x, True), (g, False)], outs=[(D_MODEL, BF16)])[0]


def _rms_bwd(x, g, dh, dres, *, name):
    T = x.shape[0]

    def body(x_ref, g_ref, dh_ref, dres_ref, dx_ref, dg_ref):
        xv = x_ref[...]
        rstd = lax.rsqrt(jnp.mean(xv * xv, axis=-1, keepdims=True) + EPS)
        xh = xv * rstd
        dhv = dh_ref[...]
        part = jnp.sum(dhv * xh, axis=0, keepdims=True)

        @pl.when(pl.program_id(0) == 0)
        def _():
            dg_ref[...] = part

        @pl.when(pl.program_id(0) > 0)
        def _():
            dg_ref[...] += part

        dxh = dhv * g_ref[...]
        dx_ref[...] = rstd * (dxh - xh * jnp.mean(dxh * xh, axis=-1, keepdims=True)) + dres_ref[...]

    return _row_call(body, name=name, T=T, ins=[(x, True), (g, False), (dh, True), (dres, True)],
                     outs=[(D_MODEL, F32)], acc_outs=[((1, D_MODEL), F32)])


def _merge_fwd(ua, ub, zg):
    def body(ua_ref, ub_ref, zg_ref, m_ref):
        ga = _sigmoid(zg_ref[:, :D_MODEL])
        gb = _sigmoid(zg_ref[:, D_MODEL:])
        m_ref[...] = (ga * ua_ref[...] + gb * ub_ref[...]).astype(BF16)

    return _row_call(body, name="merge_fwd", T=ua.shape[0], ins=[(ua, True), (ub, True), (zg, (2,))],
                     outs=[(D_MODEL, BF16)])[0]


def _merge_bwd(dm, ua, ub, zg):
    def body(dm_ref, ua_ref, ub_ref, zg_ref, dua_ref, dub_ref, dzg_ref):
        ga = _sigmoid(zg_ref[:, :D_MODEL])
        gb = _sigmoid(zg_ref[:, D_MODEL:])
        dmv = dm_ref[...]
        dua_ref[...] = (dmv * ga).astype(BF16)
        dub_ref[...] = (dmv * gb).astype(BF16)
        dzg_ref[:, :D_MODEL] = (dmv * ua_ref[...] * ga * (1.0 - ga)).astype(BF16)
        dzg_ref[:, D_MODEL:] = (dmv * ub_ref[...] * gb * (1.0 - gb)).astype(BF16)

    return _row_call(body, name="merge_bwd", T=dm.shape[0], ins=[(dm, True), (ua, True), (ub, True), (zg, (2,))],
                     outs=[(D_MODEL, BF16), (D_MODEL, BF16), (SEG, BF16, (IN_PAD, 2))])


def _swiglu_fwd(a, b):
    def body(a_ref, b_ref, o_ref):
        av = a_ref[...].astype(F32)
        o_ref[...] = (av * _sigmoid(av) * b_ref[...].astype(F32)).astype(BF16)

    return _row_call(body, name="swiglu_fwd", T=a.shape[0], ins=[(a, True), (b, True)], outs=[(a.shape[1], BF16)])[0]


def _swiglu_bwd(a, b, dact):
    def body(a_ref, b_ref, d_ref, da_ref, db_ref):
        av = a_ref[...].astype(F32)
        bv = b_ref[...].astype(F32)
        dv = d_ref[...]
        sg = _sigmoid(av)
        da_ref[...] = (dv * bv * sg * (1.0 + av * (1.0 - sg))).astype(BF16)
        db_ref[...] = (dv * av * sg).astype(BF16)

    return _row_call(body, name="swiglu_bwd", T=a.shape[0], ins=[(a, True), (b, True), (dact, True)],
                     outs=[(a.shape[1], BF16), (a.shape[1], BF16)])


def _ple_loss(x2, sp, pp, tgt):
    def body(x_ref, sp_ref, pp_ref, t_ref, dy_ref, dsp_ref, dpp_ref, loss_ref):
        gp = _sigmoid(sp_ref[...])
        ppv = pp_ref[...]
        err = x_ref[...] + gp * ppv - t_ref[...]
        part = 0.5 * jnp.sum(jnp.mean(err * err, axis=-1, keepdims=True), axis=0, keepdims=True)
        part = jnp.broadcast_to(part, loss_ref.shape)

        @pl.when(pl.program_id(0) == 0)
        def _():
            loss_ref[...] = part

        @pl.when(pl.program_id(0) > 0)
        def _():
            loss_ref[...] += part

        dy = err * (1.0 / D_MODEL)
        dy_ref[...] = dy
        dsp_ref[...] = (dy * ppv * gp * (1.0 - gp)).astype(BF16)
        dpp_ref[...] = (dy * gp).astype(BF16)

    return _row_call(body, name="ple_loss", T=x2.shape[0], ins=[(x2, True), (sp, True), (pp, True), (tgt, True)],
                     outs=[(D_MODEL, F32), (D_MODEL, BF16), (D_MODEL, BF16)], acc_outs=[((8, LANES), F32)])


def _hg_consts():
    C = HG_CHUNK
    r, c = _iota((C, C), 0), _iota((C, C), 1)
    tri = (c <= r)
    same = (r // HG_SUB) == (c // HG_SUB)
    return tri, (tri & same)


def _hg_chunk_fwd(q, f, lb, tri_b, sub_b):
    sgq = _sigmoid(q)
    qt = q * sgq
    sg = _sigmoid(f)
    fg = lb + (1.0 - lb) * sg
    kf = (1.0 - lb) * (1.0 - sg)
    logf = jnp.log(fg)
    b = _split_dot(tri_b, logf, 1, 0)
    w = _split_dot(sub_b, logf, 1, 0)
    return sgq, qt, sg, fg, kf, b, w


def _hg_scores(qs_b, kf, b, row):
    C, S = HG_CHUNK, HG_SUB
    parts, ks = [], []
    for blk in range(C // S):
        ref = jnp.zeros_like(b[0:1]) if blk == 0 else b[blk * S - 1:blk * S]
        e = jnp.exp(jnp.minimum(ref - b, EXP_CLAMP))
        e = jnp.where(row < (blk + 1) * S, e, 0.0)
        k_b = (kf * e).astype(BF16)
        ks.append((e, k_b))
        parts.append(_dot(qs_b[blk * S:(blk + 1) * S], k_b, 1, 1))
    return jnp.concatenate(parts, axis=0), ks


def _hgrn_fwd(z, lb_logits, gain, blocks):
    T = z.shape[0]
    RB = min(ROW_BLOCK, T)
    nb, cpb = T // RB, RB // HG_CHUNK
    C, DK = HG_CHUNK, HG_DK
    n = len(blocks)

    def body(*refs):
        z_ref, lg_ref, g_ref = refs[:3]
        o_ref, y_ref, st_ref = refs[3 + n:6 + n]
        s_ref = refs[6 + 2 * n]
        g_start, g_forward, g_finish = _gather_steps(refs[3:3 + n], refs[6 + n:6 + 2 * n], *refs[7 + 2 * n:])

        @pl.when(pl.program_id(0) == 0)
        def _():
            s_ref[...] = jnp.zeros_like(s_ref)
            g_start()

        lg = lg_ref[...]
        lb_all = 1.0 / (1.0 + jnp.exp(lg[1:2] - lg[0:1]))
        gain_v = g_ref[...]
        tri, sub = _hg_consts()
        tri_b, sub_b = tri.astype(BF16), sub.astype(BF16)
        row = _iota((C, DK), 0)

        def chunk(ci, carry):
            r0 = pl.multiple_of(ci * C, C)
            rows = pl.ds(r0, C)
            for h in range(HG_HEADS):
                cs = slice(h * DK, (h + 1) * DK)
                q = z_ref[rows, cs]
                f = z_ref[rows, HG_W + h * DK:HG_W + (h + 1) * DK]
                v = z_ref[rows, 2 * HG_W + h * DK:2 * HG_W + (h + 1) * DK]
                g = z_ref[rows, 3 * HG_W + h * DK:3 * HG_W + (h + 1) * DK]
                lb = lb_all[:, cs]
                _, qt, _, _, kf, b, w = _hg_chunk_fwd(q, f, lb, tri_b, sub_b)
                st = s_ref[h]
                st_ref[pl.ds(pl.multiple_of((ci * HG_HEADS + h) * DK, DK), DK), :] = st
                v_b = v.astype(BF16)
                qs_b = (qt * jnp.exp(w)).astype(BF16)
                a, _ = _hg_scores(qs_b, kf, b, row)
                a = jnp.where(tri, a, 0.0)
                qd = qt * jnp.exp(b)
                o = _dot(qd.astype(BF16), st.astype(BF16), 1, 1) + _dot(a.astype(BF16), v_b, 1, 0)
                bl = b[C - 1:C]
                kd = kf * jnp.exp(bl - b)
                s_ref[h] = st * jnp.exp(bl) + _dot(v_b, kd.astype(BF16), 0, 0)
                o_ref[rows, cs] = o
                rstd = lax.rsqrt(jnp.mean(o * o, axis=-1, keepdims=True) + EPS)
                y_ref[rows, cs] = (o * rstd * gain_v * (g * _sigmoid(g))).astype(BF16)
            return carry

        lax.fori_loop(0, cpb, chunk, 0)

        @pl.when(pl.program_id(0) == nb - 1)
        def _():
            g_forward()
            g_finish()

    hbm = pl.BlockSpec(memory_space=pl.ANY)
    res = _pallas(
        body, name="hgrn_fwd", grid=(nb,),
        in_specs=[pl.BlockSpec((RB, HG_COLS), lambda i: (i, 0)), pl.BlockSpec((2, HG_W), lambda i: (0, 0)),
                  pl.BlockSpec((1, DK), lambda i: (0, 0))] + [hbm] * n,
        out_specs=[pl.BlockSpec((RB, HG_W), lambda i: (i, 0)), pl.BlockSpec((RB, HG_W), lambda i: (i, 0)),
                   pl.BlockSpec((cpb * HG_HEADS * DK, DK), lambda i: (i, 0))] + [hbm] * n,
        out_shape=[jax.ShapeDtypeStruct((T, HG_W), F32), jax.ShapeDtypeStruct((T, HG_W), BF16),
                   jax.ShapeDtypeStruct((T // C * HG_HEADS * DK, DK), F32)]
        + [jax.ShapeDtypeStruct((N_DEV,) + b.shape, b.dtype) for b in blocks],
        scratch_shapes=[pltpu.VMEM((HG_HEADS, DK, DK), F32)] + _gather_scratch(n),
        compiler_params=_params(dimension_semantics=("arbitrary",)),
    )(z, lb_logits, gain, *blocks)
    return res[0], res[1], res[2], res[3:]


def _hgrn_bwd(z, o_raw, dy, states, lb_logits, gain, dz_buf):
    T = z.shape[0]
    RB = min(ROW_BLOCK, T)
    nb, cpb = T // RB, RB // HG_CHUNK
    C, DK, S = HG_CHUNK, HG_DK, HG_SUB

    def body(z_ref, o_ref, dy_ref, st_ref, lg_ref, g_ref, _buf_ref, dz_ref, dlg_ref, dg_ref, ds_ref, dlb_ref):
        step = pl.program_id(0)

        @pl.when(step == 0)
        def _():
            ds_ref[...] = jnp.zeros_like(ds_ref)
            dlb_ref[...] = jnp.zeros_like(dlb_ref)
            dg_ref[...] = jnp.zeros_like(dg_ref)

        lg = lg_ref[...]
        lb_all = 1.0 / (1.0 + jnp.exp(lg[1:2] - lg[0:1]))
        gain_v = g_ref[...]
        tri, sub = _hg_consts()
        tri_b, sub_b = tri.astype(BF16), sub.astype(BF16)
        row = _iota((C, DK), 0)

        def chunk(cj, carry):
            ci = cpb - 1 - cj
            r0 = pl.multiple_of(ci * C, C)
            rows = pl.ds(r0, C)
            for h in range(HG_HEADS):
                cs = slice(h * DK, (h + 1) * DK)
                q = z_ref[rows, cs]
                f = z_ref[rows, HG_W + h * DK:HG_W + (h + 1) * DK]
                v = z_ref[rows, 2 * HG_W + h * DK:2 * HG_W + (h + 1) * DK]
                g = z_ref[rows, 3 * HG_W + h * DK:3 * HG_W + (h + 1) * DK]
                lb = lb_all[:, cs]
                sgq, qt, sg, fg, kf, b, w = _hg_chunk_fwd(q, f, lb, tri_b, sub_b)
                st = st_ref[pl.ds(pl.multiple_of((ci * HG_HEADS + h) * DK, DK), DK), :]
                dst = ds_ref[h]
                o = o_ref[rows, cs]
                dyv = dy_ref[rows, cs]
                rstd = lax.rsqrt(jnp.mean(o * o, axis=-1, keepdims=True) + EPS)
                n = o * rstd
                sgg = _sigmoid(g)
                t1 = dyv * (g * sgg)
                dg_ref[...] += jnp.sum(t1 * n, axis=0, keepdims=True)
                dn = t1 * gain_v
                do = rstd * (dn - n * jnp.mean(dn * n, axis=-1, keepdims=True))
                dgate = dyv * n * gain_v * sgg * (1.0 + g * (1.0 - sgg))
                do_b = do.astype(BF16)
                v_b = v.astype(BF16)
                ew = jnp.exp(w)
                qs_b = (qt * ew).astype(BF16)
                a, ks = _hg_scores(qs_b, kf, b, row)
                a = jnp.where(tri, a, 0.0)
                eb = jnp.exp(b)
                qd = qt * eb
                bl = b[C - 1:C]
                ebl = jnp.exp(bl)
                ekd = jnp.exp(bl - b)
                kd = kf * ekd
                kd_b = kd.astype(BF16)
                qd_b = qd.astype(BF16)
                dst_b = dst.astype(BF16)
                dqd = _dot(do_b, st.astype(BF16), 1, 0)
                da = jnp.where(tri, _dot(do_b, v_b, 1, 1), 0.0)
                dv = _dot(a.astype(BF16), do_b, 0, 0) + _dot(kd_b, dst_b, 1, 1)
                dkd = _dot(v_b, dst_b, 1, 0)
                ds_ref[h] = dst * ebl + _dot(do_b, qd_b, 0, 0)
                dkd_kd = dkd * kd_b.astype(F32)
                dbl = ebl * jnp.sum(dst * st, axis=0, keepdims=True) + jnp.sum(dkd_kd, axis=0, keepdims=True)
                da_b = da.astype(BF16)
                dqs_parts = []
                dk_in = jnp.zeros((C, DK), F32)
                db_k = jnp.zeros((C, DK), F32)
                for blk in range(C // S):
                    e, k_b = ks[blk]
                    da_blk = da_b[blk * S:(blk + 1) * S]
                    dqs_parts.append(_dot(da_blk, k_b, 1, 0))
                    dks = _dot(da_blk, qs_b[blk * S:(blk + 1) * S], 0, 0)
                    dk_in = dk_in + dks * e
                    db_k = db_k + dks * k_b.astype(F32)
                dqs = jnp.concatenate(dqs_parts, axis=0)
                dqt_in = dqs * ew
                db = qs_b.astype(F32) * dqs - db_k + dqd * qd_b.astype(F32) - dkd_kd
                db = db + jnp.where(row == C - 1, dbl, 0.0)
                dlogf = _split_dot(tri_b, db, 0, 0)
                dqt = dqt_in + dqd * eb
                dkf = dk_in + dkd * ekd
                dfg = dlogf / fg - dkf
                dlb_ref[:, cs] += jnp.sum(dfg * (1.0 - sg), axis=0, keepdims=True)
                dz_ref[rows, cs] = (dqt * sgq * (1.0 + q * (1.0 - sgq))).astype(BF16)
                dz_ref[rows, HG_W + h * DK:HG_W + (h + 1) * DK] = (dfg * (1.0 - lb) * sg * (1.0 - sg)).astype(BF16)
                dz_ref[rows, 2 * HG_W + h * DK:2 * HG_W + (h + 1) * DK] = dv.astype(BF16)
                dz_ref[rows, 3 * HG_W + h * DK:3 * HG_W + (h + 1) * DK] = dgate.astype(BF16)
            return carry

        lax.fori_loop(0, cpb, chunk, 0)

        @pl.when(step == nb - 1)
        def _():
            d0 = dlb_ref[...] * lb_all * (1.0 - lb_all)
            dlg_ref[0:1, :] = d0
            dlg_ref[1:2, :] = -d0

    rev = lambda i: (nb - 1 - i, 0)
    fix = lambda i: (0, 0)
    return _pallas(
        body, name="hgrn_bwd", grid=(nb,),
        in_specs=[pl.BlockSpec((RB, HG_COLS), rev), pl.BlockSpec((RB, HG_W), rev), pl.BlockSpec((RB, HG_W), rev),
                  pl.BlockSpec((cpb * HG_HEADS * DK, DK), rev), pl.BlockSpec((2, HG_W), fix),
                  pl.BlockSpec((1, DK), fix), pl.BlockSpec(memory_space=pl.ANY)],
        out_specs=[pl.BlockSpec((RB, HG_COLS), rev), pl.BlockSpec((2, HG_W), fix), pl.BlockSpec((1, DK), fix)],
        out_shape=[jax.ShapeDtypeStruct(dz_buf.shape, BF16), jax.ShapeDtypeStruct((2, HG_W), F32),
                   jax.ShapeDtypeStruct((1, DK), F32)],
        scratch_shapes=[pltpu.VMEM((HG_HEADS, DK, DK), F32), pltpu.VMEM((1, HG_W), F32)],
        input_output_aliases={6: 0},
        compiler_params=_params(dimension_semantics=("arbitrary",)),
    )(z, o_raw, dy, states, lb_logits, gain, dz_buf)


def _head_ones():
    r, c = _iota((FOX_W, FOX_W), 0), _iota((FOX_W, FOX_W), 1)
    return ((r // FOX_DH) == (c // FOX_DH)).astype(BF16)


def _causal_blocks(nq, *, by_query):
    if by_query:
        pairs = [(q, k) for q in range(nq) for k in range(q + 1)]
    else:
        pairs = [(q, k) for k in range(nq) for q in range(k, nq)]
    return (jnp.asarray([q for q, _ in pairs], jnp.int32), jnp.asarray([k for _, k in pairs], jnp.int32))


def _log_sigmoid(x):
    return jnp.minimum(x, 0.0) - jnp.log(1.0 + jnp.exp(-jnp.abs(x)))


def _fox_prep(z, bias, qg, kg):
    T = z.shape[0]
    tm = min(ROW_BLOCK, T)
    nb = T // tm

    def body(z_ref, b_ref, qg_ref, kg_ref, q_ref, k_ref, v_ref, qa_ref, ka_ref, carry_ref):
        @pl.when(pl.program_id(0) == 0)
        def _():
            carry_ref[...] = jnp.zeros_like(carry_ref)

        ones = _head_ones()
        normed = []
        for src, g_ref in ((0, qg_ref), (1, kg_ref)):
            xv = z_ref[:, src * FOX_W:(src + 1) * FOX_W]
            ms = _split_dot(ones, xv * xv, 1, 0, mat_first=False) * (1.0 / FOX_DH)
            normed.append(xv * lax.rsqrt(ms + EPS) * g_ref[...])
        qn, kn = normed
        q_ref[...] = (qn * FOX_DH ** -0.5).astype(BF16)
        k_b = kn.astype(BF16)
        k_ref[...] = k_b
        v_ref[...] = z_ref[:, 2 * FOX_W:3 * FOX_W].astype(BF16)
        logf = _log_sigmoid(z_ref[:, 3 * FOX_W:FOX_COLS] + b_ref[...])
        r, c = _iota((tm, tm), 0), _iota((tm, tm), 1)
        tri_b = (c <= r).astype(BF16)
        cum = _split_dot(tri_b, logf, 1, 0, terms=3) + carry_ref[...]
        carry_ref[...] = cum[tm - 1:tm]
        c2 = cum * LOG2E
        hi = c2.astype(BF16)
        rem = c2 - hi.astype(F32)
        mid = rem.astype(BF16)
        lo = (rem - mid.astype(F32)).astype(BF16)
        hrow, col = _iota((LANES, 2 * FOX_W), 0), _iota((LANES, 2 * FOX_W), 1)
        base = hrow * LANES + jnp.where(hrow % 2 == 0, FOX_DH, 0)
        placed = None
        for t, part in enumerate((hi, mid, lo)):
            place = jnp.logical_and(col == base + t, hrow < FOX_HEADS).astype(BF16)
            term = _dot(part, place, 1, 0)
            placed = term if placed is None else placed + term
        colw = _iota((tm, 2 * FOX_W), 1)
        head, lane = colw // LANES, colw % LANES
        own = (lane < FOX_DH) == (head % 2 == 0)
        other = jnp.where(head % 2 == 0, lane - FOX_DH, lane)
        ones_q = jnp.where(jnp.logical_and(other >= 0, other < 3), -1.0, 0.0)
        q2 = (qn * (FOX_DH ** -0.5 * LOG2E)).astype(BF16)
        q_exp = jnp.concatenate([q2[:, (h // 2) * LANES:(h // 2 + 1) * LANES] for h in range(FOX_HEADS)], axis=1)
        k_exp = jnp.concatenate([k_b[:, (h // 2) * LANES:(h // 2 + 1) * LANES] for h in range(FOX_HEADS)], axis=1)
        qa_ref[...] = jnp.where(own, q_exp, ones_q.astype(BF16))
        ka_ref[...] = jnp.where(own, k_exp, placed.astype(BF16))

    wide = pl.BlockSpec((tm, 2 * FOX_W), lambda i: (i, 0))
    return _pallas(
        body, name="fox_prep", grid=(nb,),
        in_specs=[pl.BlockSpec((tm, SEG), lambda i: (i, 1)), pl.BlockSpec((1, LANES), lambda i: (0, 0)),
                  pl.BlockSpec((1, FOX_W), lambda i: (0, 0)), pl.BlockSpec((1, FOX_W), lambda i: (0, 0))],
        out_specs=[pl.BlockSpec((tm, FOX_W), lambda i: (i, 0))] * 3 + [wide] * 2,
        out_shape=[jax.ShapeDtypeStruct((T, FOX_W), BF16)] * 3 + [jax.ShapeDtypeStruct((T, 2 * FOX_W), BF16)] * 2,
        scratch_shapes=[pltpu.VMEM((1, LANES), F32)],
        compiler_params=_params(dimension_semantics=("arbitrary",)),
    )(z, bias, qg, kg)


def _fox_fwd(qa, ka, vb, blocks):
    T = qa.shape[0]
    tq = min(ROW_BLOCK, T)
    nq = T // tq
    NEG = -1e30
    n = len(blocks)
    n_pairs = FOX_HEADS // 2

    n_in = 3
    q_of, k_of = _causal_blocks(nq, by_query=True)
    n_tri = len(q_of)

    def body(qt_ref, kt_ref, *refs):
        q_ref, k_ref, v_ref = refs[:n_in]
        o_ref, lse_ref = refs[n_in + n:n_in + n + 2]
        m_sc, l_sc, acc_sc = refs[n_in + 2 * n + 2:n_in + 2 * n + 5]
        pr, t = pl.program_id(0), pl.program_id(1)
        qi, ki = qt_ref[t], kt_ref[t]
        g_start, g_forward, g_finish = _gather_steps(
            refs[n_in:n_in + n], refs[n_in + n + 2:n_in + 2 * n + 2], *refs[n_in + 2 * n + 5:])

        @pl.when(jnp.logical_and(pr == 0, t == 0))
        def _():
            g_start()

        @pl.when(jnp.logical_and(pr == n_pairs // 2, t == 0))
        def _():
            g_forward()

        @pl.when(ki == 0)
        def _():
            m_sc[...] = jnp.full_like(m_sc, NEG)
            l_sc[...] = jnp.zeros_like(l_sc)
            acc_sc[...] = jnp.zeros_like(acc_sc)

        lane = _iota((tq, LANES), 1)

        def block(masked):
            vv = v_ref[...]
            for hh in range(2):
                hs = slice(hh * LANES, (hh + 1) * LANES)
                s = _dot(q_ref[:, hs], k_ref[:, hs], 1, 1)
                tiles = [s[:, j * LANES:(j + 1) * LANES] for j in range(tq // LANES)]
                if masked:
                    row, col = _iota((tq, LANES), 0), _iota((tq, LANES), 1)
                    tiles = [jnp.where(row >= col + j * LANES, t, NEG) for j, t in enumerate(tiles)]
                m_old = m_sc[hh]
                top = jnp.broadcast_to(jnp.max(functools.reduce(jnp.maximum, tiles), axis=-1, keepdims=True),
                                       (tq, LANES))
                m_new = jnp.maximum(m_old, top)
                alpha = jnp.exp2(m_old - m_new)
                ps = [jnp.exp2(t - m_new) for t in tiles]
                l_sc[hh] = alpha * l_sc[hh] + functools.reduce(jnp.add, ps)
                m_sc[hh] = m_new
                p_b = jnp.concatenate([p.astype(BF16) for p in ps], axis=1)
                acc_sc[hh] = alpha * acc_sc[hh] + _dot(p_b, vv, 1, 0)

        @pl.when(ki < qi)
        def _():
            block(False)

        @pl.when(ki == qi)
        def _():
            block(True)
            l0 = jnp.sum(l_sc[0], axis=-1, keepdims=True)
            l1 = jnp.sum(l_sc[1], axis=-1, keepdims=True)
            o_ref[...] = jnp.where(lane < FOX_DH, acc_sc[0] * (1.0 / l0), acc_sc[1] * (1.0 / l1))
            lse_ref[:, :LANES] = m_sc[0] + jnp.log2(l0)
            lse_ref[:, LANES:] = m_sc[1] + jnp.log2(l1)

        @pl.when(jnp.logical_and(pr == n_pairs - 1, t == n_tri - 1))
        def _():
            g_finish()

    qmap = lambda p, t, qt, kt: (qt[t], p)
    kmap = lambda p, t, qt, kt: (kt[t], p)
    hbm = pl.BlockSpec(memory_space=pl.ANY)
    grid_spec = pltpu.PrefetchScalarGridSpec(
        num_scalar_prefetch=2, grid=(n_pairs, n_tri),
        in_specs=[pl.BlockSpec((tq, 2 * LANES), qmap), pl.BlockSpec((tq, 2 * LANES), kmap),
                  pl.BlockSpec((tq, LANES), kmap)] + [hbm] * n,
        out_specs=[pl.BlockSpec((tq, LANES), qmap), pl.BlockSpec((tq, 2 * LANES), qmap)] + [hbm] * n,
        scratch_shapes=[pltpu.VMEM((2, tq, LANES), F32)] * 3 + _gather_scratch(n))
    res = _pallas(
        body, name="fox_fwd", grid_spec=grid_spec,
        out_shape=[jax.ShapeDtypeStruct((T, FOX_W), F32), jax.ShapeDtypeStruct((T, 2 * FOX_W), F32)]
        + [jax.ShapeDtypeStruct((N_DEV,) + b.shape, b.dtype) for b in blocks],
        compiler_params=_params(dimension_semantics=("arbitrary",) * 2),
    )(q_of, k_of, qa, ka, vb, *blocks)
    return res[0], res[1], res[2:]


def _fox_bwd(qs, kn, vb, qa, ka, o, do, lse, hs):
    T = qs.shape[0]
    tq = min(ROW_BLOCK, T)
    nq = T // tq
    n = len(hs)
    n_pairs = FOX_HEADS // 2

    q_of, k_of = _causal_blocks(nq, by_query=False)
    n_tri = len(q_of)

    def body(qt_ref, kt_ref, *refs):
        q_ref, k_ref, v_ref, qa_ref, ka_ref, o_ref, do_ref, lse_ref = refs[:8]
        dq_ref, dk_ref, dv_ref, dcs_ref, drs_ref = refs[8 + n:13 + n]
        pr, t = pl.program_id(0), pl.program_id(1)
        qi, ki = qt_ref[t], kt_ref[t]
        x_start, x_finish = _chip_exchange_steps(refs[8:8 + n], refs[13 + n:13 + 2 * n], *refs[13 + 2 * n:])

        @pl.when(jnp.logical_and(pr == 0, t == 0))
        def _():
            x_start()
            drs_ref[...] = jnp.zeros_like(drs_ref)

        @pl.when(t == 0)
        def _():
            dq_ref[...] = jnp.zeros_like(dq_ref)

        @pl.when(qi == ki)
        def _():
            dk_ref[...] = jnp.zeros_like(dk_ref)
            dv_ref[...] = jnp.zeros_like(dv_ref)
            dcs_ref[...] = jnp.zeros_like(dcs_ref)

        def block(masked):
            lane = _iota((tq, LANES), 1)
            qv, kv, vv = q_ref[...], k_ref[...], v_ref[...]
            ov, dov = o_ref[...], do_ref[...]
            qrows = pl.ds(pl.multiple_of(qi * tq, tq), tq)
            dq_acc = jnp.zeros((tq, LANES), F32)
            dk_acc = jnp.zeros((tq, LANES), F32)
            dv_acc = jnp.zeros((tq, LANES), F32)
            dcs_acc = jnp.zeros((8, tq), F32)
            drs_acc = jnp.zeros((tq, LANES), F32)
            prod = dov * ov
            nt = tq // LANES
            for hh in range(2):
                in_head = (lane < FOX_DH) if hh == 0 else (lane >= FOX_DH)
                hs_ = slice(hh * LANES, (hh + 1) * LANES)
                zb = jnp.zeros_like(qv)
                qm = jnp.where(in_head, qv, zb)
                km = jnp.where(in_head, kv, zb)
                dom = jnp.where(in_head, dov, 0.0).astype(BF16)
                delta_b = jnp.broadcast_to(jnp.sum(jnp.where(in_head, prod, 0.0), axis=1, keepdims=True), (tq, LANES))
                lse_b = lse_ref[:, hs_]
                s = _dot(qa_ref[:, hs_], ka_ref[:, hs_], 1, 1)
                dp = _dot(dom, vv, 1, 1)
                p_tiles, ds_tiles, col_tiles = [], [], []
                row_part = jnp.zeros((tq, LANES), F32)
                for j in range(nt):
                    js = slice(j * LANES, (j + 1) * LANES)
                    p = jnp.exp2(s[:, js] - lse_b)
                    if masked:
                        p = jnp.where(_iota((tq, LANES), 0) >= _iota((tq, LANES), 1) + j * LANES, p, 0.0)
                    ds = p * (dp[:, js] - delta_b)
                    p_tiles.append(p.astype(BF16))
                    ds_tiles.append(ds.astype(BF16))
                    col_tiles.append(jnp.sum(ds, axis=0, keepdims=True))
                    row_part = row_part + ds
                p_b = jnp.concatenate(p_tiles, axis=1)
                ds_b = jnp.concatenate(ds_tiles, axis=1)
                dv_acc = dv_acc + _dot(p_b, dom, 0, 0)
                dq_acc = dq_acc + _dot(ds_b, km, 1, 0)
                dk_acc = dk_acc + _dot(ds_b, qm, 0, 0)
                dcs_acc = dcs_acc + jnp.where(_iota((8, tq), 0) == hh, jnp.concatenate(col_tiles, axis=1), 0.0)
                rowsum = jnp.sum(row_part, axis=1, keepdims=True)
                drs_acc = drs_acc + jnp.where(lane == 2 * pr + hh, rowsum, 0.0)
            drs_ref[qrows, :] += drs_acc
            dq_ref[qrows, :] += dq_acc
            dk_ref[...] += dk_acc
            dv_ref[...] += dv_acc
            dcs_ref[0] += dcs_acc

        @pl.when(qi > ki)
        def _():
            block(False)

        @pl.when(qi == ki)
        def _():
            block(True)

        @pl.when(jnp.logical_and(pr == n_pairs - 1, t == n_tri - 1))
        def _():
            x_finish()

    qmap = lambda p, t, qt, kt: (qt[t], p)
    kmap = lambda p, t, qt, kt: (kt[t], p)
    hbm = pl.BlockSpec(memory_space=pl.ANY)
    grid_spec = pltpu.PrefetchScalarGridSpec(
        num_scalar_prefetch=2, grid=(n_pairs, n_tri),
        in_specs=[pl.BlockSpec((tq, LANES), qmap), pl.BlockSpec((tq, LANES), kmap), pl.BlockSpec((tq, LANES), kmap),
                  pl.BlockSpec((tq, 2 * LANES), qmap), pl.BlockSpec((tq, 2 * LANES), kmap),
                  pl.BlockSpec((tq, LANES), qmap), pl.BlockSpec((tq, LANES), qmap), pl.BlockSpec((tq, 2 * LANES), qmap)]
        + [hbm] * n,
        out_specs=[pl.BlockSpec((T, LANES), lambda p, t, qt, kt: (0, p)), pl.BlockSpec((tq, LANES), kmap),
                   pl.BlockSpec((tq, LANES), kmap), pl.BlockSpec((1, 8, tq), lambda p, t, qt, kt: (p, 0, kt[t])),
                   pl.BlockSpec((T, LANES), lambda p, t, qt, kt: (0, 0))] + [hbm] * n,
        scratch_shapes=_chip_exchange_scratch(n))
    res = _pallas(
        body, name="fox_bwd", grid_spec=grid_spec,
        out_shape=[jax.ShapeDtypeStruct((T, FOX_W), F32)] * 3
        + [jax.ShapeDtypeStruct((n_pairs, 8, T), F32), jax.ShapeDtypeStruct((T, LANES), F32)]
        + _chip_exchange_shapes(hs),
        compiler_params=_params(dimension_semantics=("arbitrary",) * 2),
    )(q_of, k_of, qs, kn, vb, qa, ka, o, do, lse, *hs)
    res = list(res)
    return res[:5] + [res[5:]]


def _fox_post(z, dq, dk, dv, dcs, drs, bias, qg, kg, dz_buf):
    T = z.shape[0]
    tm = min(ROW_BLOCK, T)
    nb = T // tm

    def body(z_ref, dq_ref, dk_ref, dv_ref, dcs_ref, drs_ref, b_ref, qg_ref, kg_ref, _buf_ref, dz_ref, dqg_ref, dkg_ref,
             db_ref, carry_ref):
        @pl.when(pl.program_id(0) == 0)
        def _():
            carry_ref[...] = jnp.zeros_like(carry_ref)
            dqg_ref[...] = jnp.zeros_like(dqg_ref)
            dkg_ref[...] = jnp.zeros_like(dkg_ref)
            db_ref[...] = jnp.zeros_like(db_ref)

        ones = _head_ones()
        for src, g_ref, d_ref, dg_ref, scale in ((0, qg_ref, dq_ref, dqg_ref, FOX_DH ** -0.5), (1, kg_ref, dk_ref, dkg_ref, 1.0)):
            xv = z_ref[:, src * FOX_W:(src + 1) * FOX_W]
            ms = _split_dot(ones, xv * xv, 1, 0, mat_first=False) * (1.0 / FOX_DH)
            rstd = lax.rsqrt(ms + EPS)
            xh = xv * rstd
            dn = d_ref[...] * scale
            dg_ref[...] += jnp.sum(dn * xh, axis=0, keepdims=True)
            dxh = dn * g_ref[...]
            mean = _split_dot(ones, dxh * xh, 1, 0, mat_first=False) * (1.0 / FOX_DH)
            dz_ref[:, src * FOX_W:(src + 1) * FOX_W] = (rstd * (dxh - xh * mean)).astype(BF16)
        dz_ref[:, 2 * FOX_W:3 * FOX_W] = dv_ref[...].astype(BF16)
        row8 = _iota((8, tm), 0)
        dct = jnp.zeros((8, tm), F32)
        for h in range(FOX_HEADS):
            src_row = dcs_ref[h // 2][h % 2:h % 2 + 1, :]
            dct = dct + jnp.where(row8 == h, src_row, 0.0)
        dct = drs_ref[...].T[0:8] - dct
        r, c = _iota((tm, tm), 0), _iota((tm, tm), 1)
        upper_b = (r >= c).astype(BF16)
        rc = _split_dot(upper_b, dct, 1, 0, mat_first=False) + carry_ref[...]
        carry_ref[...] = rc[:, 0:1]
        full = jnp.concatenate([rc, jnp.zeros((LANES - 8, tm), F32)], axis=0)
        dlogf = full.T
        xf = z_ref[:, 3 * FOX_W:FOX_COLS] + b_ref[...]
        df = dlogf * (1.0 - _sigmoid(xf))
        dz_ref[:, 3 * FOX_W:FOX_COLS] = df.astype(BF16)
        dz_ref[:, FOX_COLS:] = jnp.zeros((tm, SEG - FOX_COLS), BF16)
        db_ref[...] += jnp.sum(df, axis=0, keepdims=True)

    rev = lambda i: (nb - 1 - i, 0)
    fix2 = lambda i: (0, 0)
    return _pallas(
        body, name="fox_post", grid=(nb,),
        in_specs=[pl.BlockSpec((tm, SEG), lambda i: (nb - 1 - i, 1)), pl.BlockSpec((tm, FOX_W), rev),
                  pl.BlockSpec((tm, FOX_W), rev),
                  pl.BlockSpec((tm, FOX_W), rev), pl.BlockSpec((FOX_HEADS // 2, 8, tm), lambda i: (0, 0, nb - 1 - i)),
                  pl.BlockSpec((tm, LANES), rev),
                  pl.BlockSpec((1, LANES), fix2), pl.BlockSpec((1, FOX_W), fix2), pl.BlockSpec((1, FOX_W), fix2),
                  pl.BlockSpec(memory_space=pl.ANY)],
        out_specs=[pl.BlockSpec((tm, SEG), lambda i: (nb - 1 - i, 1)), pl.BlockSpec((1, FOX_W), fix2),
                   pl.BlockSpec((1, FOX_W), fix2), pl.BlockSpec((1, LANES), fix2)],
        out_shape=[jax.ShapeDtypeStruct(dz_buf.shape, BF16), jax.ShapeDtypeStruct((1, FOX_W), F32),
                   jax.ShapeDtypeStruct((1, FOX_W), F32), jax.ShapeDtypeStruct((1, LANES), F32)],
        scratch_shapes=[pltpu.VMEM((8, 1), F32)],
        input_output_aliases={9: 0},
        compiler_params=_params(dimension_semantics=("arbitrary",)),
    )(z, dq, dk, dv, dcs, drs, bias, qg, kg, dz_buf)


def _local_step(x, p, tgt, sm, W, rest_chunks, core):
    lbl, og, fb = sm["hg_lb_logits"], sm["hg_onorm_g"], sm["fox_f_bias"]
    fbias = jnp.pad(fb, ((0, 0), (0, LANES - FOX_HEADS)))
    qg = jnp.tile(sm["fox_q_norm_g"], (1, FOX_HEADS))
    kg = jnp.tile(sm["fox_k_norm_g"], (1, FOX_HEADS))

    h = _rms_fwd(x, sm["norm_mix_g"], name="rms_mix")
    rest = dict(zip(BIG[1:], rest_chunks))
    first, second = ["w_ffn_gate"], ["w_ffn_up", "w_ffn_down"]
    third = [n for n in BIG[1:] if n not in first + second]
    z, got1 = _matmul(h, W["w_in"], gather=[rest[n] for n in first], name="mm_z")
    o_raw, ya, states, got2 = _hgrn_fwd(z, lbl, og, [rest[n] for n in second])
    qs, kn, vb, qa, ka = _fox_prep(z, fbias, qg, kg)
    yb, lse, got3 = _fox_fwd(qa, ka, vb, [rest[n] for n in third])
    W = dict(W, **{n: _full_of_chunks(n, g)
                   for n, g in zip(first + second + third, list(got1) + list(got2) + list(got3))})
    ua = _matmul(ya, W["w_branch_a"], name="mm_ua")
    ub = _matmul(yb, W["w_branch_b"], name="mm_ub")
    merged = _merge_fwd(ua, ub, z)
    x1 = _matmul(merged, W["w_out"], add=x, name="mm_x1")
    hf = _rms_fwd(x1, sm["norm_ffn_g"], name="rms_ffn")
    a = _matmul(hf, W["w_ffn_gate"], out_dtype=BF16, name="mm_ffn_a")
    b = _matmul(hf, W["w_ffn_up"], out_dtype=BF16, name="mm_ffn_b")
    act = _swiglu_fwd(a, b)
    x2 = _matmul(act, W["w_ffn_down"], add=x1, name="mm_x2")
    hp = _rms_fwd(x2, sm["norm_ple_g"], name="rms_ple")
    sp = _matmul(hp, W["w_ple_gate"], name="mm_sp")
    pp = _matmul(p, W["w_ple_proj"], name="mm_pp")
    dy, dsp, dpp, loss = _ple_loss(x2, sp, pp, tgt)

    G = {}
    G["w_ple_proj"] = _matmul(p, dpp, ta=True, out_dtype=BF16, name="mm_dw_ple_proj")
    G["w_ple_gate"] = _matmul(hp, dsp, ta=True, out_dtype=BF16, name="mm_dw_ple_gate")
    dhp = _matmul(dsp, W["w_ple_gate"], tb=True, name="mm_dhp")
    dx2, d_ple_g = _rms_bwd(x2, sm["norm_ple_g"], dhp, dy, name="rms_ple_bwd")
    dact = _matmul(dx2, W["w_ffn_down"], tb=True, name="mm_dact")
    G["w_ffn_down"] = _matmul(act, dx2, ta=True, out_dtype=BF16, name="mm_dw_ffn_down")
    da, db = _swiglu_bwd(a, b, dact)
    G["w_ffn_gate"] = _matmul(hf, da, ta=True, out_dtype=BF16, name="mm_dw_ffn_gate")
    G["w_ffn_up"] = _matmul(hf, db, ta=True, out_dtype=BF16, name="mm_dw_ffn_up")
    dhf = _matmul(da, W["w_ffn_gate"], tb=True, name="mm_dhf_a")
    dhf = _matmul(db, W["w_ffn_up"], tb=True, add=dhf, name="mm_dhf_b")
    dx1, d_ffn_g = _rms_bwd(x1, sm["norm_ffn_g"], dhf, dx2, name="rms_ffn_bwd")
    dmerged = _matmul(dx1, W["w_out"], tb=True, name="mm_dmerged")
    G["w_out"] = _matmul(merged, dx1, ta=True, out_dtype=BF16, name="mm_dw_out")
    dua, dub, dz = _merge_bwd(dmerged, ua, ub, z)
    G["w_branch_a"] = _matmul(ya, dua, ta=True, out_dtype=BF16, name="mm_dw_branch_a")
    G["w_branch_b"] = _matmul(yb, dub, ta=True, out_dtype=BF16, name="mm_dw_branch_b")
    dya = _matmul(dua, W["w_branch_a"], tb=True, name="mm_dya")
    dyb = _matmul(dub, W["w_branch_b"], tb=True, name="mm_dyb")
    hb_rest = _sibling_sums({n: G[n] for n in BIG[1:]}, core, tag="rest")
    dq, dk, dv, dcs, drs, got_rest = _fox_bwd(qs, kn, vb, qa, ka, yb, dyb, lse, hb_rest)
    dz, d_qg, d_kg, d_fb = _fox_post(z, dq, dk, dv, dcs, drs, fbias, qg, kg, dz)
    dz, d_lbl, d_og = _hgrn_bwd(z, o_raw, dya, states, lbl, og, dz)
    G["w_in"] = _matmul(h, dz, ta=True, out_dtype=BF16, name="mm_dw_in")
    hb_in = _sibling_sums({"w_in": G["w_in"]}, core, tag="w_in")
    dh, got_in = _matmul(dz, W["w_in"], tb=True, exchange=hb_in, name="mm_dh")
    grad_x, d_mix_g = _rms_bwd(x, sm["norm_mix_g"], dh, dx1, name="rms_mix_bwd")

    gs = {"norm_mix_g": d_mix_g, "hg_lb_logits": d_lbl, "hg_onorm_g": d_og, "fox_f_bias": d_fb[:, :FOX_HEADS],
          "fox_q_norm_g": d_qg.reshape(FOX_HEADS, FOX_DH).sum(0, keepdims=True),
          "fox_k_norm_g": d_kg.reshape(FOX_HEADS, FOX_DH).sum(0, keepdims=True),
          "norm_ffn_g": d_ffn_g, "norm_ple_g": d_ple_g}
    return loss, grad_x, gs, hb_in + hb_rest, list(got_in) + list(got_rest)


def _pack_rows(parts, total):
    buf = jnp.concatenate(parts, axis=-2)
    pad = total - buf.shape[-2]
    widths = [(0, 0)] * (buf.ndim - 2) + [(0, pad), (0, 0)]
    return jnp.pad(buf, widths)


def _chunk_of_shard(n, w):
    if n == "w_in":
        return jnp.pad(w, ((0, 0), (0, IN_SHARD_PAD - IN_SHARD)))
    if n in ("w_ffn_gate", "w_ffn_up"):
        return jnp.pad(w, ((0, 0), (0, FF_SHARD_PAD - FF_SHARD)))
    if n == "w_ffn_down":
        return jnp.pad(w, ((0, FF_SHARD_PAD - FF_SHARD), (0, 0)))
    return w


def _full_of_chunks(n, g):
    _, a, b = g.shape
    if BIG_SHAPE[n][2] == 0:
        return g.reshape(N_DEV * a, b)
    if n == "w_in":
        w = g[:, :, :IN_SHARD].transpose(1, 0, 2).reshape(a, IN_COLS)
        gap = jnp.zeros((a, SEG - FOX_LOGICAL), g.dtype)
        return jnp.concatenate([w[:, :HG_COLS], w[:, HG_COLS:HG_COLS + FOX_LOGICAL], gap, w[:, HG_COLS + FOX_LOGICAL:]],
                               axis=1)
    return g.transpose(1, 0, 2).reshape(a, N_DEV * b)


def _chunks_of_full(n, g):
    if BIG_SHAPE[n][2] == 0:
        return g.reshape(N_DEV, g.shape[0] // N_DEV, g.shape[1])
    if n == "w_in":
        a = g.shape[0]
        w = jnp.concatenate([g[:, :HG_COLS], g[:, SEG:SEG + FOX_LOGICAL], g[:, 2 * SEG:]], axis=1)
        w = w.reshape(a, N_DEV, IN_SHARD).transpose(1, 0, 2)
        return jnp.pad(w, ((0, 0), (0, 0), (0, IN_SHARD_PAD - IN_SHARD)))
    return g.reshape(g.shape[0], N_DEV, g.shape[1] // N_DEV).transpose(1, 0, 2)


def _pack_small(vals, loss_row=None):
    parts = [vals[n].reshape(SMALL_ROWS[n], -1) for n in SMALL]
    parts = [jnp.pad(v, ((0, 0), (0, LANES - v.shape[1]))) for v in parts]
    if loss_row is not None:
        parts.append(loss_row)
    return _pack_rows(parts, SMALL_TOTAL)


def _unpack_small(buf, like):
    out, r0 = {}, 0
    for n in SMALL:
        rows, size = SMALL_ROWS[n], like[n].size
        blk = buf[r0:r0 + rows]
        out[n] = (blk if size == rows * LANES else blk[:, :size]).reshape(like[n].shape)
        r0 += rows
    return out


def _place():
    return lax.axis_index("x"), lax.axis_index("y"), lax.axis_index("c")


def _gather_steps(x_refs, out_refs, send_sems, recv_sems, local_sems):
    n = len(x_refs)
    x, y, c = _place()
    me, sibling = (x, y, c), (x, y, 1 - c)
    chips = [(1 - x, y), (x, 1 - y), (1 - x, 1 - y)]

    def slot(i, px, py, pc):
        return out_refs[i].at[4 * px + 2 * py + pc]

    def copy(k, i, blk, to, own=False):
        return pltpu.make_async_remote_copy(
            src_ref=x_refs[i] if own else slot(i, *blk), dst_ref=slot(i, *blk),
            send_sem=send_sems.at[k, i], recv_sem=recv_sems.at[k, i], device_id=to, device_id_type=MESH)

    def mine():
        return [pltpu.make_async_copy(x_refs[i], slot(i, *me), local_sems.at[i]) for i in range(n)]

    def first():
        cps = [copy(0, i, me, sibling, own=True) for i in range(n)]
        return cps + [copy(1 + j, i, me, (*chip, c), own=True) for j, chip in enumerate(chips) for i in range(n)]

    def passed():
        return [copy(4 + j, i, (*chip, c), sibling) for j, chip in enumerate(chips) for i in range(n)]

    def start():
        for cp in mine() + first():
            cp.start()

    def forward():
        fws = passed()
        for j, chip in enumerate(chips):
            for i in range(n):
                copy(1 + j, i, (*chip, c), me).wait_recv()
                fws[j * n + i].start()

    def finish():
        for i in range(n):
            copy(0, i, sibling, me).wait_recv()
        for j, chip in enumerate(chips):
            for i in range(n):
                copy(4 + j, i, (*chip, 1 - c), me).wait_recv()
        for cp in first() + passed():
            cp.wait_send()
        for cp in mine():
            cp.wait()

    return start, forward, finish


def _gather_scratch(n):
    return [pltpu.SemaphoreType.DMA((7, n)), pltpu.SemaphoreType.DMA((7, n)), pltpu.SemaphoreType.DMA((n,))]


def _all_gather(blocks, *, name):
    n = len(blocks)

    def body(*refs):
        for step in _gather_steps(refs[:n], refs[n:2 * n], *refs[2 * n:]):
            step()

    hbm = pl.BlockSpec(memory_space=pl.ANY)
    return _pallas(
        body, name=name, out_shape=[jax.ShapeDtypeStruct((N_DEV,) + b.shape, b.dtype) for b in blocks],
        in_specs=[hbm] * n, out_specs=[hbm] * n, scratch_shapes=_gather_scratch(n),
    )(*blocks)


def _sibling_exchange(gs, *, name):
    n = len(gs)

    def body(*refs):
        g_refs, out_refs = refs[:n], refs[n:2 * n]
        send_sems, recv_sems = refs[2 * n:]
        x, y, c = _place()
        cps = [pltpu.make_async_remote_copy(
            src_ref=g_refs[i].at[:, pl.ds(1 - c, 1)], dst_ref=out_refs[i], send_sem=send_sems.at[i],
            recv_sem=recv_sems.at[i], device_id=(x, y, 1 - c), device_id_type=MESH) for i in range(n)]
        for cp in cps:
            cp.start()
        for cp in cps:
            cp.wait()

    hbm = pl.BlockSpec(memory_space=pl.ANY)
    return _pallas(
        body, name=name, out_shape=[jax.ShapeDtypeStruct((N_CHIP, 1) + g.shape[2:], g.dtype) for g in gs],
        in_specs=[hbm] * n, out_specs=[hbm] * n,
        scratch_shapes=[pltpu.SemaphoreType.DMA((n,)), pltpu.SemaphoreType.DMA((n,))],
    )(*gs)


def _chip_sum(g4, got, core, *, name):
    _, _, a, b = g4.shape

    def body(c_ref, g_ref, r_ref, h_ref):
        h_ref[0] = (g_ref[0, 0].astype(F32) + r_ref[0, 0].astype(F32)).astype(BF16)

    grid_spec = pltpu.PrefetchScalarGridSpec(
        num_scalar_prefetch=1, grid=(N_CHIP,),
        in_specs=[pl.BlockSpec((1, 1, a, b), lambda j, c: (j, c[0], 0, 0)),
                  pl.BlockSpec((1, 1, a, b), lambda j, c: (j, 0, 0, 0))],
        out_specs=[pl.BlockSpec((1, a, b), lambda j, c: (j, 0, 0))])
    return _pallas(
        body, name=name, grid_spec=grid_spec, out_shape=[jax.ShapeDtypeStruct((N_CHIP, a, b), BF16)],
        compiler_params=_params(dimension_semantics=("arbitrary",)),
    )(core, g4, got)[0]


def _chip_exchange(hs, *, name):
    n = len(hs)

    def body(*refs):
        for step in _chip_exchange_steps(refs[:n], refs[n:2 * n], *refs[2 * n:]):
            step()

    hbm = pl.BlockSpec(memory_space=pl.ANY)
    return _pallas(
        body, name=name, out_shape=_chip_exchange_shapes(hs), in_specs=[hbm] * n, out_specs=[hbm] * n,
        scratch_shapes=_chip_exchange_scratch(n),
    )(*hs)


def _chip_exchange_steps(h_refs, out_refs, send_sems, recv_sems):
    n = len(h_refs)
    x, y, c = _place()
    chips = [(1 - x, y), (x, 1 - y), (1 - x, 1 - y)]

    def copies():
        return [pltpu.make_async_remote_copy(
            src_ref=h_refs[i].at[2 * px + py], dst_ref=out_refs[i].at[k], send_sem=send_sems.at[k, i],
            recv_sem=recv_sems.at[k, i], device_id=(px, py, c), device_id_type=MESH)
            for k, (px, py) in enumerate(chips) for i in range(n)]

    def start():
        for cp in copies():
            cp.start()

    def finish():
        for cp in copies():
            cp.wait()

    return start, finish


def _chip_exchange_shapes(hs):
    return [jax.ShapeDtypeStruct((3,) + h.shape[1:], h.dtype) for h in hs]


def _chip_exchange_scratch(n):
    return [pltpu.SemaphoreType.DMA((3, n)), pltpu.SemaphoreType.DMA((3, n))]


def _sibling_sums(G, core, *, tag):
    g4 = []
    for n, g in G.items():
        gc = _chunks_of_full(n, g)
        g4.append(gc.reshape((N_CHIP, 2) + gc.shape[1:]))
    got = _sibling_exchange(g4, name="grads_to_sibling_" + tag)
    return [_chip_sum(g, r, core, name="chip_sum_" + n) for n, g, r in zip(G, g4, got)]


def _adam_math(w, g, m, v):
    m = ADAM_B1 * m + (1.0 - ADAM_B1) * g
    v = ADAM_B2 * v + (1.0 - ADAM_B2) * (g * g)
    m_hat = m / (1.0 - ADAM_B1 ** ADAM_STEP)
    v_hat = v / (1.0 - ADAM_B2 ** ADAM_STEP)
    delta = -ADAM_LR * (m_hat / (jnp.sqrt(v_hat) + ADAM_EPS) + ADAM_WD * w)
    return delta, m, v


def _adam_shard(hb, got, chip, w, m, v, *, name):
    _, r, c = w.shape
    _, a, b = hb.shape
    tr = r if r <= 512 else 256
    ta = tr if r // tr > 1 else a

    def body(j_ref, h_ref, r_ref, w_ref, m_ref, v_ref, g_ref, d_ref, nm_ref, nv_ref):
        parts = [h_ref[0], r_ref[0], r_ref[1], r_ref[2]]
        g = None
        for part in parts:
            part = part[:tr, :c].astype(F32)
            g = part if g is None else g + part
        d, nm, nv = _adam_math(w_ref[0], g, m_ref[0], v_ref[0])
        g_ref[0] = g
        d_ref[0] = d
        nm_ref[0] = nm
        nv_ref[0] = nv

    blk = pl.BlockSpec((1, tr, c), lambda i, j: (0, i, 0))
    grid_spec = pltpu.PrefetchScalarGridSpec(
        num_scalar_prefetch=1, grid=(r // tr,),
        in_specs=[pl.BlockSpec((1, ta, b), lambda i, j: (j[0], i, 0)),
                  pl.BlockSpec((3, ta, b), lambda i, j: (0, i, 0)), blk, blk, blk],
        out_specs=[blk] * 4)
    return _pallas(
        body, name=name, grid_spec=grid_spec, out_shape=[jax.ShapeDtypeStruct((1, r, c), F32)] * 4,
        compiler_params=_params(dimension_semantics=("arbitrary",)),
    )(chip, hb, got, w, m, v)


def _small_all_reduce_adam(gs, w, m, v):
    def body(g_ref, w_ref, m_ref, v_ref, sum_ref, d_ref, nm_ref, nv_ref, gather, send_sems, recv_sems):
        x, y, c = _place()
        my = 4 * x + 2 * y + c
        gather[my] = g_ref[...]
        cps = []
        for k in range(1, N_DEV):
            to = (x ^ (k >> 2), y ^ ((k >> 1) & 1), c ^ (k & 1))
            cps.append(pltpu.make_async_remote_copy(
                src_ref=g_ref, dst_ref=gather.at[my], send_sem=send_sems.at[k - 1], recv_sem=recv_sems.at[k - 1],
                device_id=to, device_id_type=MESH))
        for cp in cps:
            cp.start()
        for cp in cps:
            cp.wait()
        total = gather[0]
        for d in range(1, N_DEV):
            total = total + gather[d]
        dlt, nm, nv = _adam_math(w_ref[...], total, m_ref[...], v_ref[...])
        sum_ref[...] = total
        d_ref[...] = dlt
        nm_ref[...] = nm
        nv_ref[...] = nv

    vm = pl.BlockSpec(memory_space=pltpu.VMEM)
    return _pallas(
        body, name="small_all_reduce_adam", out_shape=[jax.ShapeDtypeStruct((SMALL_TOTAL, LANES), F32)] * 4,
        in_specs=[vm] * 4, out_specs=[vm] * 4,
        scratch_shapes=[pltpu.VMEM((N_DEV, SMALL_TOTAL, LANES), F32), pltpu.SemaphoreType.DMA((7,)),
                        pltpu.SemaphoreType.DMA((7,))],
            )(gs, w, m, v)


def kernel(x, p, norm_mix_g, w_in, hg_lb_logits, hg_onorm_g, fox_f_bias, fox_q_norm_g, fox_k_norm_g, w_branch_a, w_branch_b, w_out, norm_ffn_g, w_ffn_gate, w_ffn_up, w_ffn_down, norm_ple_g, w_ple_gate, w_ple_proj, loss_target, m_norm_mix_g, m_w_in, m_hg_lb_logits, m_hg_onorm_g, m_fox_f_bias, m_fox_q_norm_g, m_fox_k_norm_g, m_w_branch_a, m_w_branch_b, m_w_out, m_norm_ffn_g, m_w_ffn_gate, m_w_ffn_up, m_w_ffn_down, m_norm_ple_g, m_w_ple_gate, m_w_ple_proj, v_norm_mix_g, v_w_in, v_hg_lb_logits, v_hg_onorm_g, v_fox_f_bias, v_fox_q_norm_g, v_fox_k_norm_g, v_w_branch_a, v_w_branch_b, v_w_out, v_norm_ffn_g, v_w_ffn_gate, v_w_ffn_up, v_w_ffn_down, v_norm_ple_g, v_w_ple_gate, v_w_ple_proj):
    args = dict(locals())
    wts = {n: args[n] for n in BIG + SMALL}
    mom = {n: args["m_" + n] for n in BIG + SMALL}
    var = {n: args["v_" + n] for n in BIG + SMALL}
    sm = {n: wts[n] for n in SMALL}

    xi, yi, ci = _place()
    core = jnp.reshape(ci, (1,)).astype(jnp.int32)
    chip = jnp.reshape(2 * xi + yi, (1,)).astype(jnp.int32)
    chunks = [_chunk_of_shard(n, wts[n][0].astype(BF16)) for n in BIG]
    assert BIG[0] == "w_in"
    w_in_full = _full_of_chunks("w_in", _all_gather(chunks[:1], name="w_in_all_gather")[0])

    loss_blk, grad_x, gs, hb, got = _local_step(
        x[0], p[0, 0], loss_target[0], sm, {"w_in": w_in_full}, chunks[1:], core)

    g_big, d_big, nm_big, nv_big = {}, {}, {}, {}
    for n, h, r in zip(BIG, hb, got):
        g_big[n], d_big[n], nm_big[n], nv_big[n] = _adam_shard(h, r, chip, wts[n], mom[n], var[n], name="adam_" + n)

    s_sum, s_d, s_nm, s_nv = _small_all_reduce_adam(
        _pack_small(gs, loss_blk[0:1]), _pack_small(sm), _pack_small({n: mom[n] for n in SMALL}),
        _pack_small({n: var[n] for n in SMALL}))
    loss = s_sum[LOSS_ROW, 0]
    g_small, d_small, nm_small, nv_small = (_unpack_small(t, sm) for t in (s_sum, s_d, s_nm, s_nv))

    order = ["norm_mix_g", "w_in", "hg_lb_logits", "hg_onorm_g", "fox_f_bias", "fox_q_norm_g", "fox_k_norm_g",
             "w_branch_a", "w_branch_b", "w_out", "norm_ffn_g", "w_ffn_gate", "w_ffn_up", "w_ffn_down", "norm_ple_g",
             "w_ple_gate", "w_ple_proj"]
    outs = [loss, grad_x[None]]
    for big, small in ((g_big, g_small), (d_big, d_small), (nm_big, nm_small), (nv_big, nv_small)):
        outs += [big[n] if n in big else small[n] for n in order]
    return tuple(outs)
```

```python
import functools

import jax
import jax.numpy as jnp
from jax import lax
from jax.experimental import pallas as pl
from jax.experimental.pallas import tpu as pltpu

F32 = jnp.float32
BF16 = jnp.bfloat16

D_MODEL = 1024
PLE_DIM = 256
HG_HEADS = 4
HG_DK = 128
HG_CHUNK = 64
HG_SUB = 16
HG_W = HG_HEADS * HG_DK
FOX_HEADS = 8
FOX_DH = 64
FOX_W = FOX_HEADS * FOX_DH
D_FF = 2816
EPS = 1e-6
N_DEV = 8
N_CHIP = 4
LANES = 128
FOX_COLS = 3 * FOX_W + LANES
HG_COLS = 4 * HG_W
GATE_COLS = 2 * D_MODEL
IN_COLS = HG_COLS + 3 * FOX_W + FOX_HEADS + GATE_COLS
FOX_LOGICAL = 3 * FOX_W + FOX_HEADS
SEG = 2048
IN_PAD = 3 * SEG
IN_SHARD = IN_COLS // N_DEV
IN_SHARD_PAD = 768
FF_SHARD = D_FF // N_DEV
FF_SHARD_PAD = 384
FF_PAD = N_DEV * FF_SHARD_PAD
EXP_CLAMP = 80.0
LOG2E = 1.4426950408889634

ADAM_LR = 0.001
ADAM_B1 = 0.9
ADAM_B2 = 0.999
ADAM_EPS = 1e-08
ADAM_WD = 0.01
ADAM_STEP = 10

MESH = pl.DeviceIdType.MESH
VMEM_LIMIT = 56 * 1024 * 1024
ROW_BLOCK = 512

BIG = ["w_in", "w_branch_a", "w_branch_b", "w_out", "w_ffn_gate", "w_ffn_up", "w_ffn_down",
       "w_ple_gate", "w_ple_proj"]
BIG_SHAPE = {
    "w_in": (D_MODEL, IN_COLS, 1), "w_branch_a": (HG_W, D_MODEL, 1), "w_branch_b": (FOX_W, D_MODEL, 1),
    "w_out": (D_MODEL, D_MODEL, 0), "w_ffn_gate": (D_MODEL, D_FF, 1), "w_ffn_up": (D_MODEL, D_FF, 1),
    "w_ffn_down": (D_FF, D_MODEL, 0), "w_ple_gate": (D_MODEL, D_MODEL, 0), "w_ple_proj": (PLE_DIM, D_MODEL, 1),
}

SMALL = ["norm_mix_g", "hg_lb_logits", "hg_onorm_g", "fox_f_bias", "fox_q_norm_g", "fox_k_norm_g",
         "norm_ffn_g", "norm_ple_g"]
SMALL_ROWS = {"norm_mix_g": 8, "hg_lb_logits": 8, "hg_onorm_g": 1, "fox_f_bias": 1, "fox_q_norm_g": 1,
              "fox_k_norm_g": 1, "norm_ffn_g": 8, "norm_ple_g": 8}
SMALL_TOTAL = 40
LOSS_ROW = 36


def _pallas(body, **kw):
    return pl.pallas_call(body, **kw)


def _params(**kw):
    return pltpu.CompilerParams(vmem_limit_bytes=VMEM_LIMIT, **kw)


def _pick(n, target):
    if n <= target:
        return n
    best = None
    for t in range(LANES, target + 1, LANES):
        if n % t == 0:
            best = t
    assert best is not None, (n, target)
    return best


def _dot(a, b, ca, cb):
    return lax.dot_general(a, b, (((ca,), (cb,)), ((), ())), preferred_element_type=F32)


def _split_dot(mat, x, ca, cb, terms=2, mat_first=True):
    acc = None
    rem = x
    for _ in range(terms):
        part = rem.astype(BF16)
        rem = rem - part.astype(F32)
        p = _dot(mat, part, ca, cb) if mat_first else _dot(part, mat, ca, cb)
        acc = p if acc is None else acc + p
    return acc


def _sigmoid(x):
    return 1.0 / (1.0 + jnp.exp(-x))


def _iota(shape, dim):
    return lax.broadcasted_iota(jnp.int32, shape, dim)


def _matmul(a, b, *, name, ta=False, tb=False, out_dtype=F32, add=None, exchange=None, gather=None):
    assert exchange is None or gather is None
    (K, M) = a.shape if ta else a.shape[::-1]
    (N, Kb) = b.shape if tb else b.shape[::-1]
    assert K == Kb, (a.shape, b.shape, ta, tb)
    if ta:
        tm, tn, tk = _pick(M, 2 * ROW_BLOCK), _pick(N, 2 * ROW_BLOCK), _pick(K, 4 * ROW_BLOCK)
    else:
        tm, tn, tk = _pick(M, 2 * ROW_BLOCK), _pick(N, 2048), _pick(K, 3072)
    nk = K // tk
    use_scratch = nk > 1 and out_dtype != F32

    n_in = 2 + (add is not None)
    hs = list(exchange or gather or [])
    n_x = len(hs)
    n_sem = 3 if gather is not None else 2
    grid = (M // tm, N // tn, nk)

    def body(*refs):
        refs = list(refs)
        a_ref, b_ref = refs[:2]
        add_ref = refs[2] if add is not None else None
        o_ref = refs[n_in + n_x]
        k = pl.program_id(2)
        if n_x:
            steps = _gather_steps if gather is not None else _chip_exchange_steps
            ride = steps(refs[n_in:n_in + n_x], refs[n_in + n_x + 1:n_in + 2 * n_x + 1], *refs[-n_sem:])
            at = [pl.program_id(d) for d in range(3)]

            @pl.when(jnp.logical_and(at[0] == 0, jnp.logical_and(at[1] == 0, at[2] == 0)))
            def _():
                ride[0]()
        p = _dot(a_ref[...].astype(BF16), b_ref[...].astype(BF16), 0 if ta else 1, 1 if tb else 0)

        def with_add(r):
            return r if add is None else r + add_ref[...].astype(F32)

        if nk == 1:
            o_ref[...] = with_add(p).astype(out_dtype)
        elif not use_scratch:
            @pl.when(k == 0)
            def _():
                o_ref[...] = with_add(p)

            @pl.when(k > 0)
            def _():
                o_ref[...] += p
        else:
            acc_ref = refs[n_in + 2 * n_x + 1]

            @pl.when(k == 0)
            def _():
                acc_ref[...] = p

            @pl.when(k > 0)
            def _():
                acc_ref[...] += p

            @pl.when(k == nk - 1)
            def _():
                o_ref[...] = with_add(acc_ref[...]).astype(out_dtype)

        if n_x:
            @pl.when(jnp.logical_and(at[0] == grid[0] - 1, jnp.logical_and(at[1] == grid[1] - 1, at[2] == nk - 1)))
            def _():
                for step in ride[1:]:
                    step()

    a_spec = pl.BlockSpec((tk, tm), lambda i, j, k: (k, i)) if ta else pl.BlockSpec((tm, tk), lambda i, j, k: (i, k))
    b_spec = pl.BlockSpec((tn, tk), lambda i, j, k: (j, k)) if tb else pl.BlockSpec((tk, tn), lambda i, j, k: (k, j))
    o_spec = pl.BlockSpec((tm, tn), lambda i, j, k: (i, j))
    hbm = pl.BlockSpec(memory_space=pl.ANY)
    in_specs = [a_spec, b_spec] + ([o_spec] if add is not None else []) + [hbm] * n_x
    args = (a, b) + ((add,) if add is not None else ()) + tuple(hs)
    if gather is not None:
        ride_shapes = [jax.ShapeDtypeStruct((N_DEV,) + h.shape, h.dtype) for h in hs]
        ride_scratch = _gather_scratch(n_x)
    else:
        ride_shapes = _chip_exchange_shapes(hs)
        ride_scratch = _chip_exchange_scratch(n_x) if n_x else []
    res = _pallas(
        body, name=name, grid=grid, in_specs=in_specs, out_specs=[o_spec] + [hbm] * n_x,
        out_shape=[jax.ShapeDtypeStruct((M, N), out_dtype)] + ride_shapes,
        scratch_shapes=([pltpu.VMEM((tm, tn), F32)] if use_scratch else []) + ride_scratch,
        compiler_params=_params(dimension_semantics=("arbitrary",) * 3),
    )(*args)
    return (res[0], list(res[1:])) if n_x else res[0]


def _row_map(nb, reverse, seg):
    if reverse:
        return lambda i: (nb - 1 - i, seg)
    return lambda i: (i, seg)


def _row_call(body, *, name, T, ins, outs, acc_outs=(), tm=ROW_BLOCK, reverse=False):
    tm = min(tm, T)
    nb = T // tm
    in_specs, args = [], []
    for arr, how in ins:
        args.append(arr)
        if how is True:
            in_specs.append(pl.BlockSpec((tm, arr.shape[1]), _row_map(nb, reverse, 0)))
        elif how is False:
            in_specs.append(pl.BlockSpec(arr.shape, lambda i, _n=arr.ndim: (0,) * _n))
        else:
            in_specs.append(pl.BlockSpec((tm, SEG), _row_map(nb, reverse, how[0])))
    out_specs, out_shape = [], []
    for o in outs:
        c, dt = o[0], o[1]
        total, seg = o[2] if len(o) > 2 else (c, 0)
        out_specs.append(pl.BlockSpec((tm, c), _row_map(nb, reverse, seg)))
        out_shape.append(jax.ShapeDtypeStruct((T, total), dt))
    for shp, dt in acc_outs:
        out_specs.append(pl.BlockSpec(shp, lambda i, _n=len(shp): (0,) * _n))
        out_shape.append(jax.ShapeDtypeStruct(shp, dt))
    return _pallas(body, name=name, grid=(nb,), in_specs=in_specs, out_specs=out_specs, out_shape=out_shape,
                   compiler_params=_params(dimension_semantics=("arbitrary",)))(*args)


def _rms_fwd(x, g, *, name):
    T = x.shape[0]

    def body(x_ref, g_ref, h_ref):
        xv = x_ref[...]
        rstd = lax.rsqrt(jnp.mean(xv * xv, axis=-1, keepdims=True) + EPS)
        h_ref[...] = (xv * rstd * g_ref[...]).astype(BF16)

    return _row_call(body, name=name, T=T, ins=[(x, True), (g, False)], outs=[(D_MODEL, BF16)])[0]


def _rms_bwd(x, g, dh, dres, *, name):
    T = x.shape[0]

    def body(x_ref, g_ref, dh_ref, dres_ref, dx_ref, dg_ref):
        xv = x_ref[...]
        rstd = lax.rsqrt(jnp.mean(xv * xv, axis=-1, keepdims=True) + EPS)
        xh = xv * rstd
        dhv = dh_ref[...]
        part = jnp.sum(dhv * xh, axis=0, keepdims=True)

        @pl.when(pl.program_id(0) == 0)
        def _():
            dg_ref[...] = part

        @pl.when(pl.program_id(0) > 0)
        def _():
            dg_ref[...] += part

        dxh = dhv * g_ref[...]
        dx_ref[...] = rstd * (dxh - xh * jnp.mean(dxh * xh, axis=-1, keepdims=True)) + dres_ref[...]

    return _row_call(body, name=name, T=T, ins=[(x, True), (g, False), (dh, True), (dres, True)],
                     outs=[(D_MODEL, F32)], acc_outs=[((1, D_MODEL), F32)])


def _merge_fwd(ya, yb, wa, wb, zg):
    def body(ya_ref, yb_ref, wa_ref, wb_ref, zg_ref, m_ref, ua_ref, ub_ref):
        ua = _dot(ya_ref[...].astype(BF16), wa_ref[...], 1, 0)
        ub = _dot(yb_ref[...].astype(BF16), wb_ref[...], 1, 0)
        ga = _sigmoid(zg_ref[:, :D_MODEL])
        gb = _sigmoid(zg_ref[:, D_MODEL:])
        m_ref[...] = (ga * ua + gb * ub).astype(BF16)
        ua_ref[...] = ua.astype(BF16)
        ub_ref[...] = ub.astype(BF16)

    return _row_call(body, name="merge_fwd", T=ya.shape[0],
                     ins=[(ya, True), (yb, True), (wa, False), (wb, False), (zg, (2,))],
                     outs=[(D_MODEL, BF16)] * 3)


def _merge_bwd(dm, ua, ub, zg):
    def body(dm_ref, ua_ref, ub_ref, zg_ref, dua_ref, dub_ref, dzg_ref):
        ga = _sigmoid(zg_ref[:, :D_MODEL])
        gb = _sigmoid(zg_ref[:, D_MODEL:])
        dmv = dm_ref[...]
        dua_ref[...] = (dmv * ga).astype(BF16)
        dub_ref[...] = (dmv * gb).astype(BF16)
        dzg_ref[:, :D_MODEL] = (dmv * ua_ref[...].astype(F32) * ga * (1.0 - ga)).astype(BF16)
        dzg_ref[:, D_MODEL:] = (dmv * ub_ref[...].astype(F32) * gb * (1.0 - gb)).astype(BF16)

    return _row_call(body, name="merge_bwd", T=dm.shape[0], ins=[(dm, True), (ua, True), (ub, True), (zg, (2,))],
                     outs=[(D_MODEL, BF16), (D_MODEL, BF16), (SEG, BF16, (IN_PAD, 2))])


def _swiglu_fwd(hf, w_gate, w_up):
    T, D = hf.shape
    F = w_gate.shape[1]
    tm, tn = _pick(T, 2 * ROW_BLOCK), _pick(F, 768)

    def body(h_ref, wg_ref, wu_ref, a_ref, b_ref, o_ref):
        hv = h_ref[...]
        a_b = _dot(hv, wg_ref[...], 1, 0).astype(BF16)
        b_b = _dot(hv, wu_ref[...], 1, 0).astype(BF16)
        a_ref[...] = a_b
        b_ref[...] = b_b
        av = a_b.astype(F32)
        o_ref[...] = (av * _sigmoid(av) * b_b.astype(F32)).astype(BF16)

    tile = pl.BlockSpec((tm, tn), lambda i, j: (i, j))
    wcol = pl.BlockSpec((D, tn), lambda i, j: (0, j))
    return _pallas(
        body, name="swiglu_fwd", grid=(T // tm, F // tn),
        in_specs=[pl.BlockSpec((tm, D), lambda i, j: (i, 0)), wcol, wcol],
        out_specs=[tile] * 3, out_shape=[jax.ShapeDtypeStruct((T, F), BF16)] * 3,
        compiler_params=_params(dimension_semantics=("arbitrary",) * 2),
    )(hf, w_gate, w_up)


def _swiglu_bwd(dx, w_down, a, b):
    T, D = dx.shape
    F = w_down.shape[0]
    tm, tn = _pick(T, 2 * ROW_BLOCK), _pick(F, 768)

    def body(dx_ref, w_ref, a_ref, b_ref, da_ref, db_ref):
        dact = _dot(dx_ref[...].astype(BF16), w_ref[...], 1, 1)
        av = a_ref[...].astype(F32)
        bv = b_ref[...].astype(F32)
        sg = _sigmoid(av)
        da_ref[...] = (dact * bv * sg * (1.0 + av * (1.0 - sg))).astype(BF16)
        db_ref[...] = (dact * av * sg).astype(BF16)

    tile = pl.BlockSpec((tm, tn), lambda i, j: (i, j))
    return _pallas(
        body, name="swiglu_bwd", grid=(T // tm, F // tn),
        in_specs=[pl.BlockSpec((tm, D), lambda i, j: (i, 0)), pl.BlockSpec((tn, D), lambda i, j: (j, 0)), tile, tile],
        out_specs=[tile, tile], out_shape=[jax.ShapeDtypeStruct((T, F), BF16)] * 2,
        compiler_params=_params(dimension_semantics=("arbitrary",) * 2),
    )(dx, w_down, a, b)


def _ple_loss(x2, p, g, w_gate, w_proj, tgt):
    def body(x_ref, p_ref, g_ref, wg_ref, wp_ref, t_ref, hp_ref, dy_ref, dsp_ref, dpp_ref, loss_ref):
        xv = x_ref[...]
        rstd = lax.rsqrt(jnp.mean(xv * xv, axis=-1, keepdims=True) + EPS)
        hp = (xv * rstd * g_ref[...]).astype(BF16)
        hp_ref[...] = hp
        gp = _sigmoid(_dot(hp, wg_ref[...], 1, 0))
        ppv = _dot(p_ref[...].astype(BF16), wp_ref[...], 1, 0)
        err = xv + gp * ppv - t_ref[...]
        part = 0.5 * jnp.sum(jnp.mean(err * err, axis=-1, keepdims=True), axis=0, keepdims=True)
        part = jnp.broadcast_to(part, loss_ref.shape)

        @pl.when(pl.program_id(0) == 0)
        def _():
            loss_ref[...] = part

        @pl.when(pl.program_id(0) > 0)
        def _():
            loss_ref[...] += part

        dy = err * (1.0 / D_MODEL)
        dy_ref[...] = dy
        dsp_ref[...] = (dy * ppv * gp * (1.0 - gp)).astype(BF16)
        dpp_ref[...] = (dy * gp).astype(BF16)

    return _row_call(body, name="ple_loss", T=x2.shape[0],
                     ins=[(x2, True), (p, True), (g, False), (w_gate, False), (w_proj, False), (tgt, True)],
                     outs=[(D_MODEL, BF16), (D_MODEL, F32), (D_MODEL, BF16), (D_MODEL, BF16)],
                     acc_outs=[((8, LANES), F32)])


def _hg_consts():
    C = HG_CHUNK
    r, c = _iota((C, C), 0), _iota((C, C), 1)
    tri = (c <= r)
    same = (r // HG_SUB) == (c // HG_SUB)
    return tri, (tri & same)


def _hg_chunk_fwd(q, f, lb, tri_b, sub_b):
    sgq = _sigmoid(q)
    qt = q * sgq
    sg = _sigmoid(f)
    fg = lb + (1.0 - lb) * sg
    kf = (1.0 - lb) * (1.0 - sg)
    logf = jnp.log(fg)
    b = _split_dot(tri_b, logf, 1, 0)
    w = _split_dot(sub_b, logf, 1, 0)
    return sgq, qt, sg, fg, kf, b, w


def _hg_scores(qs_b, kf, b, row):
    C, S = HG_CHUNK, HG_SUB
    parts, ks = [], []
    for blk in range(C // S):
        ref = jnp.zeros_like(b[0:1]) if blk == 0 else b[blk * S - 1:blk * S]
        e = jnp.exp(jnp.minimum(ref - b, EXP_CLAMP))
        e = jnp.where(row < (blk + 1) * S, e, 0.0)
        k_b = (kf * e).astype(BF16)
        ks.append((e, k_b))
        parts.append(_dot(qs_b[blk * S:(blk + 1) * S], k_b, 1, 1))
    return jnp.concatenate(parts, axis=0), ks


def _hgrn_fwd(z, lb_logits, gain, blocks):
    T = z.shape[0]
    RB = min(ROW_BLOCK, T)
    nb, cpb = T // RB, RB // HG_CHUNK
    C, DK = HG_CHUNK, HG_DK
    n = len(blocks)

    def body(*refs):
        z_ref, lg_ref, g_ref = refs[:3]
        o_ref, y_ref, st_ref = refs[3 + n:6 + n]
        s_ref = refs[6 + 2 * n]
        g_start, g_forward, g_finish = _gather_steps(refs[3:3 + n], refs[6 + n:6 + 2 * n], *refs[7 + 2 * n:])

        @pl.when(pl.program_id(0) == 0)
        def _():
            s_ref[...] = jnp.zeros_like(s_ref)
            g_start()

        lg = lg_ref[...]
        lb_all = 1.0 / (1.0 + jnp.exp(lg[1:2] - lg[0:1]))
        gain_v = g_ref[...]
        tri, sub = _hg_consts()
        tri_b, sub_b = tri.astype(BF16), sub.astype(BF16)
        row = _iota((C, DK), 0)

        def chunk(ci, carry):
            r0 = pl.multiple_of(ci * C, C)
            rows = pl.ds(r0, C)
            for h in range(HG_HEADS):
                cs = slice(h * DK, (h + 1) * DK)
                q = z_ref[rows, cs]
                f = z_ref[rows, HG_W + h * DK:HG_W + (h + 1) * DK]
                v = z_ref[rows, 2 * HG_W + h * DK:2 * HG_W + (h + 1) * DK]
                g = z_ref[rows, 3 * HG_W + h * DK:3 * HG_W + (h + 1) * DK]
                lb = lb_all[:, cs]
                _, qt, _, _, kf, b, w = _hg_chunk_fwd(q, f, lb, tri_b, sub_b)
                st = s_ref[h]
                st_ref[pl.ds(pl.multiple_of((ci * HG_HEADS + h) * DK, DK), DK), :] = st
                v_b = v.astype(BF16)
                qs_b = (qt * jnp.exp(w)).astype(BF16)
                a, _ = _hg_scores(qs_b, kf, b, row)
                a = jnp.where(tri, a, 0.0)
                qd = qt * jnp.exp(b)
                o = _dot(qd.astype(BF16), st.astype(BF16), 1, 1) + _dot(a.astype(BF16), v_b, 1, 0)
                bl = b[C - 1:C]
                kd = kf * jnp.exp(bl - b)
                s_ref[h] = st * jnp.exp(bl) + _dot(v_b, kd.astype(BF16), 0, 0)
                o_ref[rows, cs] = o
                rstd = lax.rsqrt(jnp.mean(o * o, axis=-1, keepdims=True) + EPS)
                y_ref[rows, cs] = (o * rstd * gain_v * (g * _sigmoid(g))).astype(BF16)
            return carry

        lax.fori_loop(0, cpb, chunk, 0)

        @pl.when(pl.program_id(0) == nb - 1)
        def _():
            g_forward()
            g_finish()

    hbm = pl.BlockSpec(memory_space=pl.ANY)
    res = _pallas(
        body, name="hgrn_fwd", grid=(nb,),
        in_specs=[pl.BlockSpec((RB, HG_COLS), lambda i: (i, 0)), pl.BlockSpec((2, HG_W), lambda i: (0, 0)),
                  pl.BlockSpec((1, DK), lambda i: (0, 0))] + [hbm] * n,
        out_specs=[pl.BlockSpec((RB, HG_W), lambda i: (i, 0)), pl.BlockSpec((RB, HG_W), lambda i: (i, 0)),
                   pl.BlockSpec((cpb * HG_HEADS * DK, DK), lambda i: (i, 0))] + [hbm] * n,
        out_shape=[jax.ShapeDtypeStruct((T, HG_W), F32), jax.ShapeDtypeStruct((T, HG_W), BF16),
                   jax.ShapeDtypeStruct((T // C * HG_HEADS * DK, DK), F32)]
        + [jax.ShapeDtypeStruct((N_DEV,) + b.shape, b.dtype) for b in blocks],
        scratch_shapes=[pltpu.VMEM((HG_HEADS, DK, DK), F32)] + _gather_scratch(n),
        compiler_params=_params(dimension_semantics=("arbitrary",)),
    )(z, lb_logits, gain, *blocks)
    return res[0], res[1], res[2], res[3:]


def _hgrn_bwd(z, o_raw, dy, states, lb_logits, gain, dz_buf):
    T = z.shape[0]
    RB = min(ROW_BLOCK, T)
    nb, cpb = T // RB, RB // HG_CHUNK
    C, DK, S = HG_CHUNK, HG_DK, HG_SUB

    def body(z_ref, o_ref, dy_ref, st_ref, lg_ref, g_ref, _buf_ref, dz_ref, dlg_ref, dg_ref, ds_ref, dlb_ref):
        step = pl.program_id(0)

        @pl.when(step == 0)
        def _():
            ds_ref[...] = jnp.zeros_like(ds_ref)
            dlb_ref[...] = jnp.zeros_like(dlb_ref)
            dg_ref[...] = jnp.zeros_like(dg_ref)

        lg = lg_ref[...]
        lb_all = 1.0 / (1.0 + jnp.exp(lg[1:2] - lg[0:1]))
        gain_v = g_ref[...]
        tri, sub = _hg_consts()
        tri_b, sub_b = tri.astype(BF16), sub.astype(BF16)
        row = _iota((C, DK), 0)

        def chunk(cj, carry):
            ci = cpb - 1 - cj
            r0 = pl.multiple_of(ci * C, C)
            rows = pl.ds(r0, C)
            for h in range(HG_HEADS):
                cs = slice(h * DK, (h + 1) * DK)
                q = z_ref[rows, cs]
                f = z_ref[rows, HG_W + h * DK:HG_W + (h + 1) * DK]
                v = z_ref[rows, 2 * HG_W + h * DK:2 * HG_W + (h + 1) * DK]
                g = z_ref[rows, 3 * HG_W + h * DK:3 * HG_W + (h + 1) * DK]
                lb = lb_all[:, cs]
                sgq, qt, sg, fg, kf, b, w = _hg_chunk_fwd(q, f, lb, tri_b, sub_b)
                st = st_ref[pl.ds(pl.multiple_of((ci * HG_HEADS + h) * DK, DK), DK), :]
                dst = ds_ref[h]
                o = o_ref[rows, cs]
                dyv = dy_ref[rows, cs]
                rstd = lax.rsqrt(jnp.mean(o * o, axis=-1, keepdims=True) + EPS)
                n = o * rstd
                sgg = _sigmoid(g)
                t1 = dyv * (g * sgg)
                dg_ref[...] += jnp.sum(t1 * n, axis=0, keepdims=True)
                dn = t1 * gain_v
                do = rstd * (dn - n * jnp.mean(dn * n, axis=-1, keepdims=True))
                dgate = dyv * n * gain_v * sgg * (1.0 + g * (1.0 - sgg))
                do_b = do.astype(BF16)
                v_b = v.astype(BF16)
                ew = jnp.exp(w)
                qs_b = (qt * ew).astype(BF16)
                a, ks = _hg_scores(qs_b, kf, b, row)
                a = jnp.where(tri, a, 0.0)
                eb = jnp.exp(b)
                qd = qt * eb
                bl = b[C - 1:C]
                ebl = jnp.exp(bl)
                ekd = jnp.exp(bl - b)
                kd = kf * ekd
                kd_b = kd.astype(BF16)
                qd_b = qd.astype(BF16)
                dst_b = dst.astype(BF16)
                dqd = _dot(do_b, st.astype(BF16), 1, 0)
                da = jnp.where(tri, _dot(do_b, v_b, 1, 1), 0.0)
                dv = _dot(a.astype(BF16), do_b, 0, 0) + _dot(kd_b, dst_b, 1, 1)
                dkd = _dot(v_b, dst_b, 1, 0)
                ds_ref[h] = dst * ebl + _dot(do_b, qd_b, 0, 0)
                dkd_kd = dkd * kd_b.astype(F32)
                dbl = ebl * jnp.sum(dst * st, axis=0, keepdims=True) + jnp.sum(dkd_kd, axis=0, keepdims=True)
                da_b = da.astype(BF16)
                dqs_parts = []
                dk_in = jnp.zeros((C, DK), F32)
                db_k = jnp.zeros((C, DK), F32)
                for blk in range(C // S):
                    e, k_b = ks[blk]
                    da_blk = da_b[blk * S:(blk + 1) * S]
                    dqs_parts.append(_dot(da_blk, k_b, 1, 0))
                    dks = _dot(da_blk, qs_b[blk * S:(blk + 1) * S], 0, 0)
                    dk_in = dk_in + dks * e
                    db_k = db_k + dks * k_b.astype(F32)
                dqs = jnp.concatenate(dqs_parts, axis=0)
                dqt_in = dqs * ew
                db = qs_b.astype(F32) * dqs - db_k + dqd * qd_b.astype(F32) - dkd_kd
                db = db + jnp.where(row == C - 1, dbl, 0.0)
                dlogf = _split_dot(tri_b, db, 0, 0)
                dqt = dqt_in + dqd * eb
                dkf = dk_in + dkd * ekd
                dfg = dlogf / fg - dkf
                dlb_ref[:, cs] += jnp.sum(dfg * (1.0 - sg), axis=0, keepdims=True)
                dz_ref[rows, cs] = (dqt * sgq * (1.0 + q * (1.0 - sgq))).astype(BF16)
                dz_ref[rows, HG_W + h * DK:HG_W + (h + 1) * DK] = (dfg * (1.0 - lb) * sg * (1.0 - sg)).astype(BF16)
                dz_ref[rows, 2 * HG_W + h * DK:2 * HG_W + (h + 1) * DK] = dv.astype(BF16)
                dz_ref[rows, 3 * HG_W + h * DK:3 * HG_W + (h + 1) * DK] = dgate.astype(BF16)
            return carry

        lax.fori_loop(0, cpb, chunk, 0)

        @pl.when(step == nb - 1)
        def _():
            d0 = dlb_ref[...] * lb_all * (1.0 - lb_all)
            dlg_ref[0:1, :] = d0
            dlg_ref[1:2, :] = -d0

    rev = lambda i: (nb - 1 - i, 0)
    fix = lambda i: (0, 0)
    return _pallas(
        body, name="hgrn_bwd", grid=(nb,),
        in_specs=[pl.BlockSpec((RB, HG_COLS), rev), pl.BlockSpec((RB, HG_W), rev), pl.BlockSpec((RB, HG_W), rev),
                  pl.BlockSpec((cpb * HG_HEADS * DK, DK), rev), pl.BlockSpec((2, HG_W), fix),
                  pl.BlockSpec((1, DK), fix), pl.BlockSpec(memory_space=pl.ANY)],
        out_specs=[pl.BlockSpec((RB, HG_COLS), rev), pl.BlockSpec((2, HG_W), fix), pl.BlockSpec((1, DK), fix)],
        out_shape=[jax.ShapeDtypeStruct(dz_buf.shape, BF16), jax.ShapeDtypeStruct((2, HG_W), F32),
                   jax.ShapeDtypeStruct((1, DK), F32)],
        scratch_shapes=[pltpu.VMEM((HG_HEADS, DK, DK), F32), pltpu.VMEM((1, HG_W), F32)],
        input_output_aliases={6: 0},
        compiler_params=_params(dimension_semantics=("arbitrary",)),
    )(z, o_raw, dy, states, lb_logits, gain, dz_buf)


def _head_ones():
    r, c = _iota((FOX_W, FOX_W), 0), _iota((FOX_W, FOX_W), 1)
    return ((r // FOX_DH) == (c // FOX_DH)).astype(BF16)


def _causal_blocks(nq, *, by_query):
    if by_query:
        pairs = [(q, k) for q in range(nq) for k in range(q + 1)]
    else:
        pairs = [(q, k) for k in range(nq) for q in range(k, nq)]
    return (jnp.asarray([q for q, _ in pairs], jnp.int32), jnp.asarray([k for _, k in pairs], jnp.int32))


def _log_sigmoid(x):
    return jnp.minimum(x, 0.0) - jnp.log(1.0 + jnp.exp(-jnp.abs(x)))


def _fox_prep(z, bias, qg, kg):
    T = z.shape[0]
    tm = min(ROW_BLOCK, T)
    nb = T // tm

    def body(z_ref, b_ref, qg_ref, kg_ref, q_ref, k_ref, v_ref, qa_ref, ka_ref, carry_ref):
        @pl.when(pl.program_id(0) == 0)
        def _():
            carry_ref[...] = jnp.zeros_like(carry_ref)

        ones = _head_ones()
        normed = []
        for src, g_ref in ((0, qg_ref), (1, kg_ref)):
            xv = z_ref[:, src * FOX_W:(src + 1) * FOX_W]
            ms = _split_dot(ones, xv * xv, 1, 0, mat_first=False) * (1.0 / FOX_DH)
            normed.append(xv * lax.rsqrt(ms + EPS) * g_ref[...])
        qn, kn = normed
        q_ref[...] = (qn * FOX_DH ** -0.5).astype(BF16)
        k_b = kn.astype(BF16)
        k_ref[...] = k_b
        v_ref[...] = z_ref[:, 2 * FOX_W:3 * FOX_W].astype(BF16)
        logf = _log_sigmoid(z_ref[:, 3 * FOX_W:FOX_COLS] + b_ref[...])
        r, c = _iota((tm, tm), 0), _iota((tm, tm), 1)
        tri_b = (c <= r).astype(BF16)
        cum = _split_dot(tri_b, logf, 1, 0, terms=3) + carry_ref[...]
        carry_ref[...] = cum[tm - 1:tm]
        c2 = cum * LOG2E
        hi = c2.astype(BF16)
        rem = c2 - hi.astype(F32)
        mid = rem.astype(BF16)
        lo = (rem - mid.astype(F32)).astype(BF16)
        hrow, col = _iota((LANES, 2 * FOX_W), 0), _iota((LANES, 2 * FOX_W), 1)
        base = hrow * LANES + jnp.where(hrow % 2 == 0, FOX_DH, 0)
        placed = None
        for t, part in enumerate((hi, mid, lo)):
            place = jnp.logical_and(col == base + t, hrow < FOX_HEADS).astype(BF16)
            term = _dot(part, place, 1, 0)
            placed = term if placed is None else placed + term
        colw = _iota((tm, 2 * FOX_W), 1)
        head, lane = colw // LANES, colw % LANES
        own = (lane < FOX_DH) == (head % 2 == 0)
        other = jnp.where(head % 2 == 0, lane - FOX_DH, lane)
        ones_q = jnp.where(jnp.logical_and(other >= 0, other < 3), -1.0, 0.0)
        q2 = (qn * (FOX_DH ** -0.5 * LOG2E)).astype(BF16)
        q_exp = jnp.concatenate([q2[:, (h // 2) * LANES:(h // 2 + 1) * LANES] for h in range(FOX_HEADS)], axis=1)
        k_exp = jnp.concatenate([k_b[:, (h // 2) * LANES:(h // 2 + 1) * LANES] for h in range(FOX_HEADS)], axis=1)
        qa_ref[...] = jnp.where(own, q_exp, ones_q.astype(BF16))
        ka_ref[...] = jnp.where(own, k_exp, placed.astype(BF16))

    wide = pl.BlockSpec((tm, 2 * FOX_W), lambda i: (i, 0))
    return _pallas(
        body, name="fox_prep", grid=(nb,),
        in_specs=[pl.BlockSpec((tm, SEG), lambda i: (i, 1)), pl.BlockSpec((1, LANES), lambda i: (0, 0)),
                  pl.BlockSpec((1, FOX_W), lambda i: (0, 0)), pl.BlockSpec((1, FOX_W), lambda i: (0, 0))],
        out_specs=[pl.BlockSpec((tm, FOX_W), lambda i: (i, 0))] * 3 + [wide] * 2,
        out_shape=[jax.ShapeDtypeStruct((T, FOX_W), BF16)] * 3 + [jax.ShapeDtypeStruct((T, 2 * FOX_W), BF16)] * 2,
        scratch_shapes=[pltpu.VMEM((1, LANES), F32)],
        compiler_params=_params(dimension_semantics=("arbitrary",)),
    )(z, bias, qg, kg)


def _fox_fwd(qa, ka, vb, blocks):
    T = qa.shape[0]
    tq = min(ROW_BLOCK, T)
    nq = T // tq
    NEG = -1e30
    n = len(blocks)
    n_pairs = FOX_HEADS // 2

    n_in = 3
    q_of, k_of = _causal_blocks(nq, by_query=True)
    n_tri = len(q_of)

    def body(qt_ref, kt_ref, *refs):
        q_ref, k_ref, v_ref = refs[:n_in]
        o_ref, lse_ref = refs[n_in + n:n_in + n + 2]
        m_sc, l_sc, acc_sc = refs[n_in + 2 * n + 2:n_in + 2 * n + 5]
        pr, t = pl.program_id(0), pl.program_id(1)
        qi, ki = qt_ref[t], kt_ref[t]
        g_start, g_forward, g_finish = _gather_steps(
            refs[n_in:n_in + n], refs[n_in + n + 2:n_in + 2 * n + 2], *refs[n_in + 2 * n + 5:])

        @pl.when(jnp.logical_and(pr == 0, t == 0))
        def _():
            g_start()

        @pl.when(jnp.logical_and(pr == n_pairs // 2, t == 0))
        def _():
            g_forward()

        @pl.when(ki == 0)
        def _():
            m_sc[...] = jnp.full_like(m_sc, NEG)
            l_sc[...] = jnp.zeros_like(l_sc)
            acc_sc[...] = jnp.zeros_like(acc_sc)

        lane = _iota((tq, LANES), 1)

        def block(masked):
            vv = v_ref[...]
            for hh in range(2):
                hs = slice(hh * LANES, (hh + 1) * LANES)
                s = _dot(q_ref[:, hs], k_ref[:, hs], 1, 1)
                tiles = [s[:, j * LANES:(j + 1) * LANES] for j in range(tq // LANES)]
                if masked:
                    row, col = _iota((tq, LANES), 0), _iota((tq, LANES), 1)
                    tiles = [jnp.where(row >= col + j * LANES, t, NEG) for j, t in enumerate(tiles)]
                m_old = m_sc[hh]
                top = jnp.broadcast_to(jnp.max(functools.reduce(jnp.maximum, tiles), axis=-1, keepdims=True),
                                       (tq, LANES))
                m_new = jnp.maximum(m_old, top)
                alpha = jnp.exp2(m_old - m_new)
                ps = [jnp.exp2(t - m_new) for t in tiles]
                l_sc[hh] = alpha * l_sc[hh] + functools.reduce(jnp.add, ps)
                m_sc[hh] = m_new
                p_b = jnp.concatenate([p.astype(BF16) for p in ps], axis=1)
                acc_sc[hh] = alpha * acc_sc[hh] + _dot(p_b, vv, 1, 0)

        @pl.when(ki < qi)
        def _():
            block(False)

        @pl.when(ki == qi)
        def _():
            block(True)
            l0 = jnp.sum(l_sc[0], axis=-1, keepdims=True)
            l1 = jnp.sum(l_sc[1], axis=-1, keepdims=True)
            o_ref[...] = jnp.where(lane < FOX_DH, acc_sc[0] * (1.0 / l0), acc_sc[1] * (1.0 / l1))
            lse_ref[:, :LANES] = m_sc[0] + jnp.log2(l0)
            lse_ref[:, LANES:] = m_sc[1] + jnp.log2(l1)

        @pl.when(jnp.logical_and(pr == n_pairs - 1, t == n_tri - 1))
        def _():
            g_finish()

    qmap = lambda p, t, qt, kt: (qt[t], p)
    kmap = lambda p, t, qt, kt: (kt[t], p)
    hbm = pl.BlockSpec(memory_space=pl.ANY)
    grid_spec = pltpu.PrefetchScalarGridSpec(
        num_scalar_prefetch=2, grid=(n_pairs, n_tri),
        in_specs=[pl.BlockSpec((tq, 2 * LANES), qmap), pl.BlockSpec((tq, 2 * LANES), kmap),
                  pl.BlockSpec((tq, LANES), kmap)] + [hbm] * n,
        out_specs=[pl.BlockSpec((tq, LANES), qmap), pl.BlockSpec((tq, 2 * LANES), qmap)] + [hbm] * n,
        scratch_shapes=[pltpu.VMEM((2, tq, LANES), F32)] * 3 + _gather_scratch(n))
    res = _pallas(
        body, name="fox_fwd", grid_spec=grid_spec,
        out_shape=[jax.ShapeDtypeStruct((T, FOX_W), F32), jax.ShapeDtypeStruct((T, 2 * FOX_W), F32)]
        + [jax.ShapeDtypeStruct((N_DEV,) + b.shape, b.dtype) for b in blocks],
        compiler_params=_params(dimension_semantics=("arbitrary",) * 2),
    )(q_of, k_of, qa, ka, vb, *blocks)
    return res[0], res[1], res[2:]


def _fox_bwd(qs, kn, vb, qa, ka, o, do, lse, hs):
    T = qs.shape[0]
    tq = min(ROW_BLOCK, T)
    nq = T // tq
    n = len(hs)
    n_pairs = FOX_HEADS // 2

    q_of, k_of = _causal_blocks(nq, by_query=False)
    n_tri = len(q_of)

    def body(qt_ref, kt_ref, *refs):
        q_ref, k_ref, v_ref, qa_ref, ka_ref, o_ref, do_ref, lse_ref = refs[:8]
        dq_ref, dk_ref, dv_ref, dcs_ref, drs_ref = refs[8 + n:13 + n]
        pr, t = pl.program_id(0), pl.program_id(1)
        qi, ki = qt_ref[t], kt_ref[t]
        x_start, x_finish = _chip_exchange_steps(refs[8:8 + n], refs[13 + n:13 + 2 * n], *refs[13 + 2 * n:])

        @pl.when(jnp.logical_and(pr == 0, t == 0))
        def _():
            x_start()
            drs_ref[...] = jnp.zeros_like(drs_ref)

        @pl.when(t == 0)
        def _():
            dq_ref[...] = jnp.zeros_like(dq_ref)

        @pl.when(qi == ki)
        def _():
            dk_ref[...] = jnp.zeros_like(dk_ref)
            dv_ref[...] = jnp.zeros_like(dv_ref)
            dcs_ref[...] = jnp.zeros_like(dcs_ref)

        def block(masked):
            lane = _iota((tq, LANES), 1)
            qv, kv, vv = q_ref[...], k_ref[...], v_ref[...]
            ov, dov = o_ref[...], do_ref[...]
            qrows = pl.ds(pl.multiple_of(qi * tq, tq), tq)
            dq_acc = jnp.zeros((tq, LANES), F32)
            dk_acc = jnp.zeros((tq, LANES), F32)
            dv_acc = jnp.zeros((tq, LANES), F32)
            dcs_acc = jnp.zeros((8, tq), F32)
            drs_acc = jnp.zeros((tq, LANES), F32)
            prod = dov * ov
            nt = tq // LANES
            for hh in range(2):
                in_head = (lane < FOX_DH) if hh == 0 else (lane >= FOX_DH)
                hs_ = slice(hh * LANES, (hh + 1) * LANES)
                zb = jnp.zeros_like(qv)
                qm = jnp.where(in_head, qv, zb)
                km = jnp.where(in_head, kv, zb)
                dom = jnp.where(in_head, dov, 0.0).astype(BF16)
                delta_b = jnp.broadcast_to(jnp.sum(jnp.where(in_head, prod, 0.0), axis=1, keepdims=True), (tq, LANES))
                lse_b = lse_ref[:, hs_]
                s = _dot(qa_ref[:, hs_], ka_ref[:, hs_], 1, 1)
                dp = _dot(dom, vv, 1, 1)
                p_tiles, ds_tiles, col_tiles = [], [], []
                row_part = jnp.zeros((tq, LANES), F32)
                for j in range(nt):
                    js = slice(j * LANES, (j + 1) * LANES)
                    p = jnp.exp2(s[:, js] - lse_b)
                    if masked:
                        p = jnp.where(_iota((tq, LANES), 0) >= _iota((tq, LANES), 1) + j * LANES, p, 0.0)
                    ds = p * (dp[:, js] - delta_b)
                    p_tiles.append(p.astype(BF16))
                    ds_tiles.append(ds.astype(BF16))
                    col_tiles.append(jnp.sum(ds, axis=0, keepdims=True))
                    row_part = row_part + ds
                p_b = jnp.concatenate(p_tiles, axis=1)
                ds_b = jnp.concatenate(ds_tiles, axis=1)
                dv_acc = dv_acc + _dot(p_b, dom, 0, 0)
                dq_acc = dq_acc + _dot(ds_b, km, 1, 0)
                dk_acc = dk_acc + _dot(ds_b, qm, 0, 0)
                dcs_acc = dcs_acc + jnp.where(_iota((8, tq), 0) == hh, jnp.concatenate(col_tiles, axis=1), 0.0)
                rowsum = jnp.sum(row_part, axis=1, keepdims=True)
                drs_acc = drs_acc + jnp.where(lane == 2 * pr + hh, rowsum, 0.0)
            drs_ref[qrows, :] += drs_acc
            dq_ref[qrows, :] += dq_acc
            dk_ref[...] += dk_acc
            dv_ref[...] += dv_acc
            dcs_ref[0] += dcs_acc

        @pl.when(qi > ki)
        def _():
            block(False)

        @pl.when(qi == ki)
        def _():
            block(True)

        @pl.when(jnp.logical_and(pr == n_pairs - 1, t == n_tri - 1))
        def _():
            x_finish()

    qmap = lambda p, t, qt, kt: (qt[t], p)
    kmap = lambda p, t, qt, kt: (kt[t], p)
    hbm = pl.BlockSpec(memory_space=pl.ANY)
    grid_spec = pltpu.PrefetchScalarGridSpec(
        num_scalar_prefetch=2, grid=(n_pairs, n_tri),
        in_specs=[pl.BlockSpec((tq, LANES), qmap), pl.BlockSpec((tq, LANES), kmap), pl.BlockSpec((tq, LANES), kmap),
                  pl.BlockSpec((tq, 2 * LANES), qmap), pl.BlockSpec((tq, 2 * LANES), kmap),
                  pl.BlockSpec((tq, LANES), qmap), pl.BlockSpec((tq, LANES), qmap), pl.BlockSpec((tq, 2 * LANES), qmap)]
        + [hbm] * n,
        out_specs=[pl.BlockSpec((T, LANES), lambda p, t, qt, kt: (0, p)), pl.BlockSpec((tq, LANES), kmap),
                   pl.BlockSpec((tq, LANES), kmap), pl.BlockSpec((1, 8, tq), lambda p, t, qt, kt: (p, 0, kt[t])),
                   pl.BlockSpec((T, LANES), lambda p, t, qt, kt: (0, 0))] + [hbm] * n,
        scratch_shapes=_chip_exchange_scratch(n))
    res = _pallas(
        body, name="fox_bwd", grid_spec=grid_spec,
        out_shape=[jax.ShapeDtypeStruct((T, FOX_W), F32)] * 3
        + [jax.ShapeDtypeStruct((n_pairs, 8, T), F32), jax.ShapeDtypeStruct((T, LANES), F32)]
        + _chip_exchange_shapes(hs),
        compiler_params=_params(dimension_semantics=("arbitrary",) * 2),
    )(q_of, k_of, qs, kn, vb, qa, ka, o, do, lse, *hs)
    res = list(res)
    return res[:5] + [res[5:]]


def _fox_post(z, dq, dk, dv, dcs, drs, bias, qg, kg, dz_buf):
    T = z.shape[0]
    tm = min(ROW_BLOCK, T)
    nb = T // tm

    def body(z_ref, dq_ref, dk_ref, dv_ref, dcs_ref, drs_ref, b_ref, qg_ref, kg_ref, _buf_ref, dz_ref, dqg_ref, dkg_ref,
             db_ref, carry_ref):
        @pl.when(pl.program_id(0) == 0)
        def _():
            carry_ref[...] = jnp.zeros_like(carry_ref)
            dqg_ref[...] = jnp.zeros_like(dqg_ref)
            dkg_ref[...] = jnp.zeros_like(dkg_ref)
            db_ref[...] = jnp.zeros_like(db_ref)

        ones = _head_ones()
        for src, g_ref, d_ref, dg_ref, scale in ((0, qg_ref, dq_ref, dqg_ref, FOX_DH ** -0.5), (1, kg_ref, dk_ref, dkg_ref, 1.0)):
            xv = z_ref[:, src * FOX_W:(src + 1) * FOX_W]
            ms = _split_dot(ones, xv * xv, 1, 0, mat_first=False) * (1.0 / FOX_DH)
            rstd = lax.rsqrt(ms + EPS)
            xh = xv * rstd
            dn = d_ref[...] * scale
            dg_ref[...] += jnp.sum(dn * xh, axis=0, keepdims=True)
            dxh = dn * g_ref[...]
            mean = _split_dot(ones, dxh * xh, 1, 0, mat_first=False) * (1.0 / FOX_DH)
            dz_ref[:, src * FOX_W:(src + 1) * FOX_W] = (rstd * (dxh - xh * mean)).astype(BF16)
        dz_ref[:, 2 * FOX_W:3 * FOX_W] = dv_ref[...].astype(BF16)
        row8 = _iota((8, tm), 0)
        dct = jnp.zeros((8, tm), F32)
        for h in range(FOX_HEADS):
            src_row = dcs_ref[h // 2][h % 2:h % 2 + 1, :]
            dct = dct + jnp.where(row8 == h, src_row, 0.0)
        dct = drs_ref[...].T[0:8] - dct
        r, c = _iota((tm, tm), 0), _iota((tm, tm), 1)
        upper_b = (r >= c).astype(BF16)
        rc = _split_dot(upper_b, dct, 1, 0, mat_first=False) + carry_ref[...]
        carry_ref[...] = rc[:, 0:1]
        full = jnp.concatenate([rc, jnp.zeros((LANES - 8, tm), F32)], axis=0)
        dlogf = full.T
        xf = z_ref[:, 3 * FOX_W:FOX_COLS] + b_ref[...]
        df = dlogf * (1.0 - _sigmoid(xf))
        dz_ref[:, 3 * FOX_W:FOX_COLS] = df.astype(BF16)
        dz_ref[:, FOX_COLS:] = jnp.zeros((tm, SEG - FOX_COLS), BF16)
        db_ref[...] += jnp.sum(df, axis=0, keepdims=True)

    rev = lambda i: (nb - 1 - i, 0)
    fix2 = lambda i: (0, 0)
    return _pallas(
        body, name="fox_post", grid=(nb,),
        in_specs=[pl.BlockSpec((tm, SEG), lambda i: (nb - 1 - i, 1)), pl.BlockSpec((tm, FOX_W), rev),
                  pl.BlockSpec((tm, FOX_W), rev),
                  pl.BlockSpec((tm, FOX_W), rev), pl.BlockSpec((FOX_HEADS // 2, 8, tm), lambda i: (0, 0, nb - 1 - i)),
                  pl.BlockSpec((tm, LANES), rev),
                  pl.BlockSpec((1, LANES), fix2), pl.BlockSpec((1, FOX_W), fix2), pl.BlockSpec((1, FOX_W), fix2),
                  pl.BlockSpec(memory_space=pl.ANY)],
        out_specs=[pl.BlockSpec((tm, SEG), lambda i: (nb - 1 - i, 1)), pl.BlockSpec((1, FOX_W), fix2),
                   pl.BlockSpec((1, FOX_W), fix2), pl.BlockSpec((1, LANES), fix2)],
        out_shape=[jax.ShapeDtypeStruct(dz_buf.shape, BF16), jax.ShapeDtypeStruct((1, FOX_W), F32),
                   jax.ShapeDtypeStruct((1, FOX_W), F32), jax.ShapeDtypeStruct((1, LANES), F32)],
        scratch_shapes=[pltpu.VMEM((8, 1), F32)],
        input_output_aliases={9: 0},
        compiler_params=_params(dimension_semantics=("arbitrary",)),
    )(z, dq, dk, dv, dcs, drs, bias, qg, kg, dz_buf)


def _local_step(x, p, tgt, sm, W, rest_chunks, core):
    lbl, og, fb = sm["hg_lb_logits"], sm["hg_onorm_g"], sm["fox_f_bias"]
    fbias = jnp.pad(fb, ((0, 0), (0, LANES - FOX_HEADS)))
    qg = jnp.tile(sm["fox_q_norm_g"], (1, FOX_HEADS))
    kg = jnp.tile(sm["fox_k_norm_g"], (1, FOX_HEADS))

    h = _rms_fwd(x, sm["norm_mix_g"], name="rms_mix")
    rest = dict(zip(BIG[1:], rest_chunks))
    first, second = ["w_ffn_gate"], ["w_ffn_up", "w_ffn_down"]
    third = [n for n in BIG[1:] if n not in first + second]
    z, got1 = _matmul(h, W["w_in"], gather=[rest[n] for n in first], name="mm_z")
    o_raw, ya, states, got2 = _hgrn_fwd(z, lbl, og, [rest[n] for n in second])
    qs, kn, vb, qa, ka = _fox_prep(z, fbias, qg, kg)
    yb, lse, got3 = _fox_fwd(qa, ka, vb, [rest[n] for n in third])
    W = dict(W, **{n: _full_of_chunks(n, g)
                   for n, g in zip(first + second + third, list(got1) + list(got2) + list(got3))})
    merged, ua, ub = _merge_fwd(ya, yb, W["w_branch_a"], W["w_branch_b"], z)
    x1 = _matmul(merged, W["w_out"], add=x, name="mm_x1")
    hf = _rms_fwd(x1, sm["norm_ffn_g"], name="rms_ffn")
    a, b, act = _swiglu_fwd(hf, W["w_ffn_gate"], W["w_ffn_up"])
    x2 = _matmul(act, W["w_ffn_down"], add=x1, name="mm_x2")
    hp, dy, dsp, dpp, loss = _ple_loss(x2, p, sm["norm_ple_g"], W["w_ple_gate"], W["w_ple_proj"], tgt)

    G = {}
    G["w_ple_proj"] = _matmul(p, dpp, ta=True, out_dtype=BF16, name="mm_dw_ple_proj")
    G["w_ple_gate"] = _matmul(hp, dsp, ta=True, out_dtype=BF16, name="mm_dw_ple_gate")
    dhp = _matmul(dsp, W["w_ple_gate"], tb=True, name="mm_dhp")
    dx2, d_ple_g = _rms_bwd(x2, sm["norm_ple_g"], dhp, dy, name="rms_ple_bwd")
    G["w_ffn_down"] = _matmul(act, dx2, ta=True, out_dtype=BF16, name="mm_dw_ffn_down")
    da, db = _swiglu_bwd(dx2, W["w_ffn_down"], a, b)
    G["w_ffn_gate"] = _matmul(hf, da, ta=True, out_dtype=BF16, name="mm_dw_ffn_gate")
    G["w_ffn_up"] = _matmul(hf, db, ta=True, out_dtype=BF16, name="mm_dw_ffn_up")
    dhf = _matmul(da, W["w_ffn_gate"], tb=True, name="mm_dhf_a")
    dhf = _matmul(db, W["w_ffn_up"], tb=True, add=dhf, name="mm_dhf_b")
    dx1, d_ffn_g = _rms_bwd(x1, sm["norm_ffn_g"], dhf, dx2, name="rms_ffn_bwd")
    dmerged = _matmul(dx1, W["w_out"], tb=True, name="mm_dmerged")
    G["w_out"] = _matmul(merged, dx1, ta=True, out_dtype=BF16, name="mm_dw_out")
    dua, dub, dz = _merge_bwd(dmerged, ua, ub, z)
    G["w_branch_a"] = _matmul(ya, dua, ta=True, out_dtype=BF16, name="mm_dw_branch_a")
    G["w_branch_b"] = _matmul(yb, dub, ta=True, out_dtype=BF16, name="mm_dw_branch_b")
    dya = _matmul(dua, W["w_branch_a"], tb=True, name="mm_dya")
    dyb = _matmul(dub, W["w_branch_b"], tb=True, name="mm_dyb")
    hb_rest = _sibling_sums({n: G[n] for n in BIG[1:]}, core, tag="rest")
    dq, dk, dv, dcs, drs, got_rest = _fox_bwd(qs, kn, vb, qa, ka, yb, dyb, lse, hb_rest)
    dz, d_qg, d_kg, d_fb = _fox_post(z, dq, dk, dv, dcs, drs, fbias, qg, kg, dz)
    dz, d_lbl, d_og = _hgrn_bwd(z, o_raw, dya, states, lbl, og, dz)
    G["w_in"] = _matmul(h, dz, ta=True, out_dtype=BF16, name="mm_dw_in")
    hb_in = _sibling_sums({"w_in": G["w_in"]}, core, tag="w_in")
    dh, got_in = _matmul(dz, W["w_in"], tb=True, exchange=hb_in, name="mm_dh")
    grad_x, d_mix_g = _rms_bwd(x, sm["norm_mix_g"], dh, dx1, name="rms_mix_bwd")

    gs = {"norm_mix_g": d_mix_g, "hg_lb_logits": d_lbl, "hg_onorm_g": d_og, "fox_f_bias": d_fb[:, :FOX_HEADS],
          "fox_q_norm_g": d_qg.reshape(FOX_HEADS, FOX_DH).sum(0, keepdims=True),
          "fox_k_norm_g": d_kg.reshape(FOX_HEADS, FOX_DH).sum(0, keepdims=True),
          "norm_ffn_g": d_ffn_g, "norm_ple_g": d_ple_g}
    return loss, grad_x, gs, hb_in + hb_rest, list(got_in) + list(got_rest)


def _pack_rows(parts, total):
    buf = jnp.concatenate(parts, axis=-2)
    pad = total - buf.shape[-2]
    widths = [(0, 0)] * (buf.ndim - 2) + [(0, pad), (0, 0)]
    return jnp.pad(buf, widths)


def _chunk_of_shard(n, w):
    if n == "w_in":
        return jnp.pad(w, ((0, 0), (0, IN_SHARD_PAD - IN_SHARD)))
    if n in ("w_ffn_gate", "w_ffn_up"):
        return jnp.pad(w, ((0, 0), (0, FF_SHARD_PAD - FF_SHARD)))
    if n == "w_ffn_down":
        return jnp.pad(w, ((0, FF_SHARD_PAD - FF_SHARD), (0, 0)))
    return w


def _full_of_chunks(n, g):
    _, a, b = g.shape
    if BIG_SHAPE[n][2] == 0:
        return g.reshape(N_DEV * a, b)
    if n == "w_in":
        w = g[:, :, :IN_SHARD].transpose(1, 0, 2).reshape(a, IN_COLS)
        gap = jnp.zeros((a, SEG - FOX_LOGICAL), g.dtype)
        return jnp.concatenate([w[:, :HG_COLS], w[:, HG_COLS:HG_COLS + FOX_LOGICAL], gap, w[:, HG_COLS + FOX_LOGICAL:]],
                               axis=1)
    return g.transpose(1, 0, 2).reshape(a, N_DEV * b)


def _chunks_of_full(n, g):
    if BIG_SHAPE[n][2] == 0:
        return g.reshape(N_DEV, g.shape[0] // N_DEV, g.shape[1])
    if n == "w_in":
        a = g.shape[0]
        w = jnp.concatenate([g[:, :HG_COLS], g[:, SEG:SEG + FOX_LOGICAL], g[:, 2 * SEG:]], axis=1)
        w = w.reshape(a, N_DEV, IN_SHARD).transpose(1, 0, 2)
        return jnp.pad(w, ((0, 0), (0, 0), (0, IN_SHARD_PAD - IN_SHARD)))
    return g.reshape(g.shape[0], N_DEV, g.shape[1] // N_DEV).transpose(1, 0, 2)


def _pack_small(vals, loss_row=None):
    parts = [vals[n].reshape(SMALL_ROWS[n], -1) for n in SMALL]
    parts = [jnp.pad(v, ((0, 0), (0, LANES - v.shape[1]))) for v in parts]
    if loss_row is not None:
        parts.append(loss_row)
    return _pack_rows(parts, SMALL_TOTAL)


def _unpack_small(buf, like):
    out, r0 = {}, 0
    for n in SMALL:
        rows, size = SMALL_ROWS[n], like[n].size
        blk = buf[r0:r0 + rows]
        out[n] = (blk if size == rows * LANES else blk[:, :size]).reshape(like[n].shape)
        r0 += rows
    return out


def _place():
    return lax.axis_index("x"), lax.axis_index("y"), lax.axis_index("c")


def _gather_steps(x_refs, out_refs, send_sems, recv_sems, local_sems):
    n = len(x_refs)
    x, y, c = _place()
    me, sibling = (x, y, c), (x, y, 1 - c)
    chips = [(1 - x, y), (x, 1 - y), (1 - x, 1 - y)]

    def slot(i, px, py, pc):
        return out_refs[i].at[4 * px + 2 * py + pc]

    def copy(k, i, blk, to, own=False):
        return pltpu.make_async_remote_copy(
            src_ref=x_refs[i] if own else slot(i, *blk), dst_ref=slot(i, *blk),
            send_sem=send_sems.at[k, i], recv_sem=recv_sems.at[k, i], device_id=to, device_id_type=MESH)

    def mine():
        return [pltpu.make_async_copy(x_refs[i], slot(i, *me), local_sems.at[i]) for i in range(n)]

    def first():
        cps = [copy(0, i, me, sibling, own=True) for i in range(n)]
        return cps + [copy(1 + j, i, me, (*chip, c), own=True) for j, chip in enumerate(chips) for i in range(n)]

    def passed():
        return [copy(4 + j, i, (*chip, c), sibling) for j, chip in enumerate(chips) for i in range(n)]

    def start():
        for cp in mine() + first():
            cp.start()

    def forward():
        fws = passed()
        for j, chip in enumerate(chips):
            for i in range(n):
                copy(1 + j, i, (*chip, c), me).wait_recv()
                fws[j * n + i].start()

    def finish():
        for i in range(n):
            copy(0, i, sibling, me).wait_recv()
        for j, chip in enumerate(chips):
            for i in range(n):
                copy(4 + j, i, (*chip, 1 - c), me).wait_recv()
        for cp in first() + passed():
            cp.wait_send()
        for cp in mine():
            cp.wait()

    return start, forward, finish


def _gather_scratch(n):
    return [pltpu.SemaphoreType.DMA((7, n)), pltpu.SemaphoreType.DMA((7, n)), pltpu.SemaphoreType.DMA((n,))]


def _all_gather(blocks, *, name):
    n = len(blocks)

    def body(*refs):
        for step in _gather_steps(refs[:n], refs[n:2 * n], *refs[2 * n:]):
            step()

    hbm = pl.BlockSpec(memory_space=pl.ANY)
    return _pallas(
        body, name=name, out_shape=[jax.ShapeDtypeStruct((N_DEV,) + b.shape, b.dtype) for b in blocks],
        in_specs=[hbm] * n, out_specs=[hbm] * n, scratch_shapes=_gather_scratch(n),
    )(*blocks)


def _sibling_exchange(gs, *, name):
    n = len(gs)

    def body(*refs):
        g_refs, out_refs = refs[:n], refs[n:2 * n]
        send_sems, recv_sems = refs[2 * n:]
        x, y, c = _place()
        cps = [pltpu.make_async_remote_copy(
            src_ref=g_refs[i].at[:, pl.ds(1 - c, 1)], dst_ref=out_refs[i], send_sem=send_sems.at[i],
            recv_sem=recv_sems.at[i], device_id=(x, y, 1 - c), device_id_type=MESH) for i in range(n)]
        for cp in cps:
            cp.start()
        for cp in cps:
            cp.wait()

    hbm = pl.BlockSpec(memory_space=pl.ANY)
    return _pallas(
        body, name=name, out_shape=[jax.ShapeDtypeStruct((N_CHIP, 1) + g.shape[2:], g.dtype) for g in gs],
        in_specs=[hbm] * n, out_specs=[hbm] * n,
        scratch_shapes=[pltpu.SemaphoreType.DMA((n,)), pltpu.SemaphoreType.DMA((n,))],
    )(*gs)


def _chip_sum(g4, got, core, *, name):
    _, _, a, b = g4.shape

    def body(c_ref, g_ref, r_ref, h_ref):
        h_ref[0] = (g_ref[0, 0].astype(F32) + r_ref[0, 0].astype(F32)).astype(BF16)

    grid_spec = pltpu.PrefetchScalarGridSpec(
        num_scalar_prefetch=1, grid=(N_CHIP,),
        in_specs=[pl.BlockSpec((1, 1, a, b), lambda j, c: (j, c[0], 0, 0)),
                  pl.BlockSpec((1, 1, a, b), lambda j, c: (j, 0, 0, 0))],
        out_specs=[pl.BlockSpec((1, a, b), lambda j, c: (j, 0, 0))])
    return _pallas(
        body, name=name, grid_spec=grid_spec, out_shape=[jax.ShapeDtypeStruct((N_CHIP, a, b), BF16)],
        compiler_params=_params(dimension_semantics=("arbitrary",)),
    )(core, g4, got)[0]


def _chip_exchange(hs, *, name):
    n = len(hs)

    def body(*refs):
        for step in _chip_exchange_steps(refs[:n], refs[n:2 * n], *refs[2 * n:]):
            step()

    hbm = pl.BlockSpec(memory_space=pl.ANY)
    return _pallas(
        body, name=name, out_shape=_chip_exchange_shapes(hs), in_specs=[hbm] * n, out_specs=[hbm] * n,
        scratch_shapes=_chip_exchange_scratch(n),
    )(*hs)


def _chip_exchange_steps(h_refs, out_refs, send_sems, recv_sems):
    n = len(h_refs)
    x, y, c = _place()
    chips = [(1 - x, y), (x, 1 - y), (1 - x, 1 - y)]

    def copies():
        return [pltpu.make_async_remote_copy(
            src_ref=h_refs[i].at[2 * px + py], dst_ref=out_refs[i].at[k], send_sem=send_sems.at[k, i],
            recv_sem=recv_sems.at[k, i], device_id=(px, py, c), device_id_type=MESH)
            for k, (px, py) in enumerate(chips) for i in range(n)]

    def start():
        for cp in copies():
            cp.start()

    def finish():
        for cp in copies():
            cp.wait()

    return start, finish


def _chip_exchange_shapes(hs):
    return [jax.ShapeDtypeStruct((3,) + h.shape[1:], h.dtype) for h in hs]


def _chip_exchange_scratch(n):
    return [pltpu.SemaphoreType.DMA((3, n)), pltpu.SemaphoreType.DMA((3, n))]


def _sibling_sums(G, core, *, tag):
    g4 = []
    for n, g in G.items():
        gc = _chunks_of_full(n, g)
        g4.append(gc.reshape((N_CHIP, 2) + gc.shape[1:]))
    got = _sibling_exchange(g4, name="grads_to_sibling_" + tag)
    return [_chip_sum(g, r, core, name="chip_sum_" + n) for n, g, r in zip(G, g4, got)]


def _adam_math(w, g, m, v):
    m = ADAM_B1 * m + (1.0 - ADAM_B1) * g
    v = ADAM_B2 * v + (1.0 - ADAM_B2) * (g * g)
    m_hat = m / (1.0 - ADAM_B1 ** ADAM_STEP)
    v_hat = v / (1.0 - ADAM_B2 ** ADAM_STEP)
    delta = -ADAM_LR * (m_hat / (jnp.sqrt(v_hat) + ADAM_EPS) + ADAM_WD * w)
    return delta, m, v


def _adam_shard(hb, got, chip, w, m, v, *, name):
    _, r, c = w.shape
    _, a, b = hb.shape
    tr = r if r <= 512 else 256
    ta = tr if r // tr > 1 else a

    def body(j_ref, h_ref, r_ref, w_ref, m_ref, v_ref, g_ref, d_ref, nm_ref, nv_ref):
        parts = [h_ref[0], r_ref[0], r_ref[1], r_ref[2]]
        g = None
        for part in parts:
            part = part[:tr, :c].astype(F32)
            g = part if g is None else g + part
        d, nm, nv = _adam_math(w_ref[0], g, m_ref[0], v_ref[0])
        g_ref[0] = g
        d_ref[0] = d
        nm_ref[0] = nm
        nv_ref[0] = nv

    blk = pl.BlockSpec((1, tr, c), lambda i, j: (0, i, 0))
    grid_spec = pltpu.PrefetchScalarGridSpec(
        num_scalar_prefetch=1, grid=(r // tr,),
        in_specs=[pl.BlockSpec((1, ta, b), lambda i, j: (j[0], i, 0)),
                  pl.BlockSpec((3, ta, b), lambda i, j: (0, i, 0)), blk, blk, blk],
        out_specs=[blk] * 4)
    return _pallas(
        body, name=name, grid_spec=grid_spec, out_shape=[jax.ShapeDtypeStruct((1, r, c), F32)] * 4,
        compiler_params=_params(dimension_semantics=("arbitrary",)),
    )(chip, hb, got, w, m, v)


def _small_all_reduce_adam(gs, w, m, v):
    def body(g_ref, w_ref, m_ref, v_ref, sum_ref, d_ref, nm_ref, nv_ref, gather, send_sems, recv_sems):
        x, y, c = _place()
        my = 4 * x + 2 * y + c
        gather[my] = g_ref[...]
        cps = []
        for k in range(1, N_DEV):
            to = (x ^ (k >> 2), y ^ ((k >> 1) & 1), c ^ (k & 1))
            cps.append(pltpu.make_async_remote_copy(
                src_ref=g_ref, dst_ref=gather.at[my], send_sem=send_sems.at[k - 1], recv_sem=recv_sems.at[k - 1],
                device_id=to, device_id_type=MESH))
        for cp in cps:
            cp.start()
        for cp in cps:
            cp.wait()
        total = gather[0]
        for d in range(1, N_DEV):
            total = total + gather[d]
        dlt, nm, nv = _adam_math(w_ref[...], total, m_ref[...], v_ref[...])
        sum_ref[...] = total
        d_ref[...] = dlt
        nm_ref[...] = nm
        nv_ref[...] = nv

    vm = pl.BlockSpec(memory_space=pltpu.VMEM)
    return _pallas(
        body, name="small_all_reduce_adam", out_shape=[jax.ShapeDtypeStruct((SMALL_TOTAL, LANES), F32)] * 4,
        in_specs=[vm] * 4, out_specs=[vm] * 4,
        scratch_shapes=[pltpu.VMEM((N_DEV, SMALL_TOTAL, LANES), F32), pltpu.SemaphoreType.DMA((7,)),
                        pltpu.SemaphoreType.DMA((7,))],
            )(gs, w, m, v)


def kernel(x, p, norm_mix_g, w_in, hg_lb_logits, hg_onorm_g, fox_f_bias, fox_q_norm_g, fox_k_norm_g, w_branch_a, w_branch_b, w_out, norm_ffn_g, w_ffn_gate, w_ffn_up, w_ffn_down, norm_ple_g, w_ple_gate, w_ple_proj, loss_target, m_norm_mix_g, m_w_in, m_hg_lb_logits, m_hg_onorm_g, m_fox_f_bias, m_fox_q_norm_g, m_fox_k_norm_g, m_w_branch_a, m_w_branch_b, m_w_out, m_norm_ffn_g, m_w_ffn_gate, m_w_ffn_up, m_w_ffn_down, m_norm_ple_g, m_w_ple_gate, m_w_ple_proj, v_norm_mix_g, v_w_in, v_hg_lb_logits, v_hg_onorm_g, v_fox_f_bias, v_fox_q_norm_g, v_fox_k_norm_g, v_w_branch_a, v_w_branch_b, v_w_out, v_norm_ffn_g, v_w_ffn_gate, v_w_ffn_up, v_w_ffn_down, v_norm_ple_g, v_w_ple_gate, v_w_ple_proj):
    args = dict(locals())
    wts = {n: args[n] for n in BIG + SMALL}
    mom = {n: args["m_" + n] for n in BIG + SMALL}
    var = {n: args["v_" + n] for n in BIG + SMALL}
    sm = {n: wts[n] for n in SMALL}

    xi, yi, ci = _place()
    core = jnp.reshape(ci, (1,)).astype(jnp.int32)
    chip = jnp.reshape(2 * xi + yi, (1,)).astype(jnp.int32)
    chunks = [_chunk_of_shard(n, wts[n][0].astype(BF16)) for n in BIG]
    assert BIG[0] == "w_in"
    w_in_full = _full_of_chunks("w_in", _all_gather(chunks[:1], name="w_in_all_gather")[0])

    loss_blk, grad_x, gs, hb, got = _local_step(
        x[0], p[0, 0], loss_target[0], sm, {"w_in": w_in_full}, chunks[1:], core)

    g_big, d_big, nm_big, nv_big = {}, {}, {}, {}
    for n, h, r in zip(BIG, hb, got):
        g_big[n], d_big[n], nm_big[n], nv_big[n] = _adam_shard(h, r, chip, wts[n], mom[n], var[n], name="adam_" + n)

    s_sum, s_d, s_nm, s_nv = _small_all_reduce_adam(
        _pack_small(gs, loss_blk[0:1]), _pack_small(sm), _pack_small({n: mom[n] for n in SMALL}),
        _pack_small({n: var[n] for n in SMALL}))
    loss = s_sum[LOSS_ROW, 0]
    g_small, d_small, nm_small, nv_small = (_unpack_small(t, sm) for t in (s_sum, s_d, s_nm, s_nv))

    order = ["norm_mix_g", "w_in", "hg_lb_logits", "hg_onorm_g", "fox_f_bias", "fox_q_norm_g", "fox_k_norm_g",
             "w_branch_a", "w_branch_b", "w_out", "norm_ffn_g", "w_ffn_gate", "w_ffn_up", "w_ffn_down", "norm_ple_g",
             "w_ple_gate", "w_ple_proj"]
    outs = [loss, grad_x[None]]
    for big, small in ((g_big, g_small), (d_big, d_small), (nm_big, nm_small), (nv_big, nv_small)):
        outs += [big[n] if n in big else small[n] for n in order]
    return tuple(outs)
```

```python
import functools

import jax
import jax.numpy as jnp
from jax import lax
from jax.experimental import pallas as pl
from jax.experimental.pallas import tpu as pltpu

F32 = jnp.float32
BF16 = jnp.bfloat16

D_MODEL = 1024
PLE_DIM = 256
HG_HEADS = 4
HG_DK = 128
HG_CHUNK = 64
HG_SUB = 16
HG_W = HG_HEADS * HG_DK
FOX_HEADS = 8
FOX_DH = 64
FOX_W = FOX_HEADS * FOX_DH
D_FF = 2816
EPS = 1e-6
N_DEV = 8
N_CHIP = 4
LANES = 128
FOX_COLS = 3 * FOX_W + LANES
HG_COLS = 4 * HG_W
GATE_COLS = 2 * D_MODEL
IN_COLS = HG_COLS + 3 * FOX_W + FOX_HEADS + GATE_COLS
FOX_LOGICAL = 3 * FOX_W + FOX_HEADS
SEG = 2048
IN_PAD = 3 * SEG
IN_SHARD = IN_COLS // N_DEV
IN_SHARD_PAD = 768
FF_SHARD = D_FF // N_DEV
FF_SHARD_PAD = 384
FF_PAD = N_DEV * FF_SHARD_PAD
EXP_CLAMP = 80.0
LOG2E = 1.4426950408889634

ADAM_LR = 0.001
ADAM_B1 = 0.9
ADAM_B2 = 0.999
ADAM_EPS = 1e-08
ADAM_WD = 0.01
ADAM_STEP = 10

MESH = pl.DeviceIdType.MESH
VMEM_LIMIT = 56 * 1024 * 1024
ROW_BLOCK = 512

BIG = ["w_in", "w_branch_a", "w_branch_b", "w_out", "w_ffn_gate", "w_ffn_up", "w_ffn_down",
       "w_ple_gate", "w_ple_proj"]
BIG_SHAPE = {
    "w_in": (D_MODEL, IN_COLS, 1), "w_branch_a": (HG_W, D_MODEL, 1), "w_branch_b": (FOX_W, D_MODEL, 1),
    "w_out": (D_MODEL, D_MODEL, 0), "w_ffn_gate": (D_MODEL, D_FF, 1), "w_ffn_up": (D_MODEL, D_FF, 1),
    "w_ffn_down": (D_FF, D_MODEL, 0), "w_ple_gate": (D_MODEL, D_MODEL, 0), "w_ple_proj": (PLE_DIM, D_MODEL, 1),
}

SMALL = ["norm_mix_g", "hg_lb_logits", "hg_onorm_g", "fox_f_bias", "fox_q_norm_g", "fox_k_norm_g",
         "norm_ffn_g", "norm_ple_g"]
SMALL_ROWS = {"norm_mix_g": 8, "hg_lb_logits": 8, "hg_onorm_g": 1, "fox_f_bias": 1, "fox_q_norm_g": 1,
              "fox_k_norm_g": 1, "norm_ffn_g": 8, "norm_ple_g": 8}
SMALL_TOTAL = 40
LOSS_ROW = 36


def _pallas(body, **kw):
    return pl.pallas_call(body, **kw)


def _params(**kw):
    return pltpu.CompilerParams(vmem_limit_bytes=VMEM_LIMIT, **kw)


def _pick(n, target):
    if n <= target:
        return n
    best = None
    for t in range(LANES, target + 1, LANES):
        if n % t == 0:
            best = t
    assert best is not None, (n, target)
    return best


def _dot(a, b, ca, cb):
    return lax.dot_general(a, b, (((ca,), (cb,)), ((), ())), preferred_element_type=F32)


def _split_dot(mat, x, ca, cb, terms=2, mat_first=True):
    acc = None
    rem = x
    for _ in range(terms):
        part = rem.astype(BF16)
        rem = rem - part.astype(F32)
        p = _dot(mat, part, ca, cb) if mat_first else _dot(part, mat, ca, cb)
        acc = p if acc is None else acc + p
    return acc


def _sigmoid(x):
    return 1.0 / (1.0 + jnp.exp(-x))


def _iota(shape, dim):
    return lax.broadcasted_iota(jnp.int32, shape, dim)


def _matmul(a, b, *, name, ta=False, tb=False, out_dtype=F32, add=None, exchange=None, gather=None,
            norm_fwd=None, norm_bwd=None):
    assert exchange is None or gather is None
    (K, M) = a.shape if ta else a.shape[::-1]
    (N, Kb) = b.shape if tb else b.shape[::-1]
    assert K == Kb, (a.shape, b.shape, ta, tb)
    if ta:
        tm, tn, tk = _pick(M, 2 * ROW_BLOCK), _pick(N, 2 * ROW_BLOCK), _pick(K, 4 * ROW_BLOCK)
    else:
        tm, tn, tk = _pick(M, 2 * ROW_BLOCK), _pick(N, 2048), _pick(K, 3072)
    if norm_bwd is not None:
        tm = _pick(M, ROW_BLOCK)
    nk = K // tk
    use_scratch = nk > 1 and out_dtype != F32
    if norm_fwd is not None or norm_bwd is not None:
        assert tn == N and not use_scratch and out_dtype == F32

    hs = list(exchange or gather or [])
    n_x = len(hs)
    grid = (M // tm, N // tn, nk)
    a_spec = pl.BlockSpec((tk, tm), lambda i, j, k: (k, i)) if ta else pl.BlockSpec((tm, tk), lambda i, j, k: (i, k))
    b_spec = pl.BlockSpec((tn, tk), lambda i, j, k: (j, k)) if tb else pl.BlockSpec((tk, tn), lambda i, j, k: (k, j))
    o_spec = pl.BlockSpec((tm, tn), lambda i, j, k: (i, j))
    row_vec = pl.BlockSpec((1, N), lambda i, j, k: (0, 0))
    hbm = pl.BlockSpec(memory_space=pl.ANY)
    extra_in = [(add, o_spec)] if add is not None else []
    extra_out = []
    if norm_fwd is not None:
        extra_in += [(norm_fwd, row_vec)]
        extra_out += [(jax.ShapeDtypeStruct((M, N), BF16), o_spec)]
    if norm_bwd is not None:
        extra_in += [(norm_bwd[0], o_spec), (norm_bwd[1], row_vec), (norm_bwd[2], o_spec)]
        extra_out += [(jax.ShapeDtypeStruct((1, N), F32), row_vec)]
    if gather is not None:
        ride_shapes, ride_scratch = [jax.ShapeDtypeStruct((N_DEV,) + h.shape, h.dtype) for h in hs], _gather_scratch(n_x)
    else:
        ride_shapes, ride_scratch = _chip_exchange_shapes(hs), (_chip_exchange_scratch(n_x) if n_x else [])
    n_ex_in, n_ex_out = len(extra_in), len(extra_out)

    def body(*refs):
        refs = list(refs)
        a_ref, b_ref = refs[:2]
        ex_in = refs[2:2 + n_ex_in]
        ride_in = refs[2 + n_ex_in:2 + n_ex_in + n_x]
        base = 2 + n_ex_in + n_x
        o_ref = refs[base]
        ex_out = refs[base + 1:base + 1 + n_ex_out]
        ride_out = refs[base + 1 + n_ex_out:base + 1 + n_ex_out + n_x]
        scratch = refs[base + 1 + n_ex_out + n_x:]
        at = [pl.program_id(d) for d in range(3)]
        k = at[2]
        if n_x:
            steps = _gather_steps if gather is not None else _chip_exchange_steps
            ride = steps(ride_in, ride_out, *scratch[-len(ride_scratch):])

            @pl.when(jnp.logical_and(at[0] == 0, jnp.logical_and(at[1] == 0, at[2] == 0)))
            def _():
                ride[0]()
        p = _dot(a_ref[...].astype(BF16), b_ref[...].astype(BF16), 0 if ta else 1, 1 if tb else 0)

        def finish(r):
            ins = list(ex_in)
            outs = list(ex_out)
            if add is not None:
                r = r + ins.pop(0)[...].astype(F32)
            if norm_fwd is not None:
                g_ref = ins.pop(0)
                rstd = lax.rsqrt(jnp.mean(r * r, axis=-1, keepdims=True) + EPS)
                outs.pop(0)[...] = (r * rstd * g_ref[...]).astype(BF16)
            if norm_bwd is not None:
                x_ref, g_ref, dres_ref = ins.pop(0), ins.pop(0), ins.pop(0)
                dg_ref = outs.pop(0)
                xv = x_ref[...]
                rstd = lax.rsqrt(jnp.mean(xv * xv, axis=-1, keepdims=True) + EPS)
                xh = xv * rstd
                part = jnp.sum(r * xh, axis=0, keepdims=True)

                @pl.when(at[0] == 0)
                def _():
                    dg_ref[...] = part

                @pl.when(at[0] > 0)
                def _():
                    dg_ref[...] += part

                dxh = r * g_ref[...]
                r = rstd * (dxh - xh * jnp.mean(dxh * xh, axis=-1, keepdims=True)) + dres_ref[...]
            o_ref[...] = r.astype(out_dtype)

        if nk == 1:
            finish(p)
        elif not use_scratch:
            @pl.when(k == 0)
            def _():
                o_ref[...] = p

            @pl.when(jnp.logical_and(k > 0, k < nk - 1))
            def _():
                o_ref[...] += p

            @pl.when(k == nk - 1)
            def _():
                finish(o_ref[...] + p)
        else:
            acc_ref = scratch[0]

            @pl.when(k == 0)
            def _():
                acc_ref[...] = p

            @pl.when(k > 0)
            def _():
                acc_ref[...] += p

            @pl.when(k == nk - 1)
            def _():
                finish(acc_ref[...])

        if n_x:
            @pl.when(jnp.logical_and(at[0] == grid[0] - 1, jnp.logical_and(at[1] == grid[1] - 1, at[2] == nk - 1)))
            def _():
                for step in ride[1:]:
                    step()

    res = _pallas(
        body, name=name, grid=grid,
        in_specs=[a_spec, b_spec] + [s for _, s in extra_in] + [hbm] * n_x,
        out_specs=[o_spec] + [s for _, s in extra_out] + [hbm] * n_x,
        out_shape=[jax.ShapeDtypeStruct((M, N), out_dtype)] + [s for s, _ in extra_out] + ride_shapes,
        scratch_shapes=([pltpu.VMEM((tm, tn), F32)] if use_scratch else []) + ride_scratch,
        compiler_params=_params(dimension_semantics=("arbitrary",) * 3),
    )(a, b, *[v for v, _ in extra_in], *hs)
    res = list(res)
    main = res[0] if n_ex_out == 0 else tuple(res[:1 + n_ex_out])
    return (main, res[1 + n_ex_out:]) if n_x else main


def _row_map(nb, reverse, seg):
    if reverse:
        return lambda i: (nb - 1 - i, seg)
    return lambda i: (i, seg)


def _row_call(body, *, name, T, ins, outs, acc_outs=(), tm=ROW_BLOCK, reverse=False):
    tm = min(tm, T)
    nb = T // tm
    in_specs, args = [], []
    for arr, how in ins:
        args.append(arr)
        if how is True:
            in_specs.append(pl.BlockSpec((tm, arr.shape[1]), _row_map(nb, reverse, 0)))
        elif how is False:
            in_specs.append(pl.BlockSpec(arr.shape, lambda i, _n=arr.ndim: (0,) * _n))
        else:
            in_specs.append(pl.BlockSpec((tm, SEG), _row_map(nb, reverse, how[0])))
    out_specs, out_shape = [], []
    for o in outs:
        c, dt = o[0], o[1]
        total, seg = o[2] if len(o) > 2 else (c, 0)
        out_specs.append(pl.BlockSpec((tm, c), _row_map(nb, reverse, seg)))
        out_shape.append(jax.ShapeDtypeStruct((T, total), dt))
    for shp, dt in acc_outs:
        out_specs.append(pl.BlockSpec(shp, lambda i, _n=len(shp): (0,) * _n))
        out_shape.append(jax.ShapeDtypeStruct(shp, dt))
    return _pallas(body, name=name, grid=(nb,), in_specs=in_specs, out_specs=out_specs, out_shape=out_shape,
                   compiler_params=_params(dimension_semantics=("arbitrary",)))(*args)


def _rms_fwd(x, g, *, name):
    T = x.shape[0]

    def body(x_ref, g_ref, h_ref):
        xv = x_ref[...]
        rstd = lax.rsqrt(jnp.mean(xv * xv, axis=-1, keepdims=True) + EPS)
        h_ref[...] = (xv * rstd * g_ref[...]).astype(BF16)

    return _row_call(body, name=name, T=T, ins=[(x, True), (g, False)], outs=[(D_MODEL, BF16)])[0]


def _rms_bwd(x, g, dh, dres, *, name):
    T = x.shape[0]

    def body(x_ref, g_ref, dh_ref, dres_ref, dx_ref, dg_ref):
        xv = x_ref[...]
        rstd = lax.rsqrt(jnp.mean(xv * xv, axis=-1, keepdims=True) + EPS)
        xh = xv * rstd
        dhv = dh_ref[...]
        part = jnp.sum(dhv * xh, axis=0, keepdims=True)

        @pl.when(pl.program_id(0) == 0)
        def _():
            dg_ref[...] = part

        @pl.when(pl.program_id(0) > 0)
        def _():
            dg_ref[...] += part

        dxh = dhv * g_ref[...]
        dx_ref[...] = rstd * (dxh - xh * jnp.mean(dxh * xh, axis=-1, keepdims=True)) + dres_ref[...]

    return _row_call(body, name=name, T=T, ins=[(x, True), (g, False), (dh, True), (dres, True)],
                     outs=[(D_MODEL, F32)], acc_outs=[((1, D_MODEL), F32)])


def _merge_fwd(ya, yb, wa, wb, zg):
    def body(ya_ref, yb_ref, wa_ref, wb_ref, zg_ref, m_ref, ua_ref, ub_ref):
        ua = _dot(ya_ref[...].astype(BF16), wa_ref[...], 1, 0)
        ub = _dot(yb_ref[...].astype(BF16), wb_ref[...], 1, 0)
        ga = _sigmoid(zg_ref[:, :D_MODEL])
        gb = _sigmoid(zg_ref[:, D_MODEL:])
        m_ref[...] = (ga * ua + gb * ub).astype(BF16)
        ua_ref[...] = ua.astype(BF16)
        ub_ref[...] = ub.astype(BF16)

    return _row_call(body, name="merge_fwd", T=ya.shape[0],
                     ins=[(ya, True), (yb, True), (wa, False), (wb, False), (zg, (2,))],
                     outs=[(D_MODEL, BF16)] * 3)


def _merge_bwd(dx1, w_out, wa, wb, ua, ub, zg):
    def body(dx_ref, wo_ref, wa_ref, wb_ref, ua_ref, ub_ref, zg_ref, dua_ref, dub_ref, dzg_ref, dya_ref, dyb_ref):
        dmv = _dot(dx_ref[...].astype(BF16), wo_ref[...], 1, 1)
        ga = _sigmoid(zg_ref[:, :D_MODEL])
        gb = _sigmoid(zg_ref[:, D_MODEL:])
        dua = (dmv * ga).astype(BF16)
        dub = (dmv * gb).astype(BF16)
        dua_ref[...] = dua
        dub_ref[...] = dub
        dzg_ref[:, :D_MODEL] = (dmv * ua_ref[...].astype(F32) * ga * (1.0 - ga)).astype(BF16)
        dzg_ref[:, D_MODEL:] = (dmv * ub_ref[...].astype(F32) * gb * (1.0 - gb)).astype(BF16)
        dya_ref[...] = _dot(dua, wa_ref[...], 1, 1)
        dyb_ref[...] = _dot(dub, wb_ref[...], 1, 1)

    return _row_call(body, name="merge_bwd", T=dx1.shape[0],
                     ins=[(dx1, True), (w_out, False), (wa, False), (wb, False), (ua, True), (ub, True), (zg, (2,))],
                     outs=[(D_MODEL, BF16), (D_MODEL, BF16), (SEG, BF16, (IN_PAD, 2)), (HG_W, F32), (FOX_W, F32)])


def _swiglu_fwd(hf, w_gate, w_up):
    T, D = hf.shape
    F = w_gate.shape[1]
    tm, tn = _pick(T, 2 * ROW_BLOCK), _pick(F, 768)

    def body(h_ref, wg_ref, wu_ref, a_ref, b_ref, o_ref):
        hv = h_ref[...]
        a_b = _dot(hv, wg_ref[...], 1, 0).astype(BF16)
        b_b = _dot(hv, wu_ref[...], 1, 0).astype(BF16)
        a_ref[...] = a_b
        b_ref[...] = b_b
        av = a_b.astype(F32)
        o_ref[...] = (av * _sigmoid(av) * b_b.astype(F32)).astype(BF16)

    tile = pl.BlockSpec((tm, tn), lambda i, j: (i, j))
    wcol = pl.BlockSpec((D, tn), lambda i, j: (0, j))
    return _pallas(
        body, name="swiglu_fwd", grid=(T // tm, F // tn),
        in_specs=[pl.BlockSpec((tm, D), lambda i, j: (i, 0)), wcol, wcol],
        out_specs=[tile] * 3, out_shape=[jax.ShapeDtypeStruct((T, F), BF16)] * 3,
        compiler_params=_params(dimension_semantics=("arbitrary",) * 2),
    )(hf, w_gate, w_up)


def _swiglu_bwd(dx, w_down, a, b):
    T, D = dx.shape
    F = w_down.shape[0]
    tm, tn = _pick(T, 2 * ROW_BLOCK), _pick(F, 768)

    def body(dx_ref, w_ref, a_ref, b_ref, da_ref, db_ref):
        dact = _dot(dx_ref[...].astype(BF16), w_ref[...], 1, 1)
        av = a_ref[...].astype(F32)
        bv = b_ref[...].astype(F32)
        sg = _sigmoid(av)
        da_ref[...] = (dact * bv * sg * (1.0 + av * (1.0 - sg))).astype(BF16)
        db_ref[...] = (dact * av * sg).astype(BF16)

    tile = pl.BlockSpec((tm, tn), lambda i, j: (i, j))
    return _pallas(
        body, name="swiglu_bwd", grid=(T // tm, F // tn),
        in_specs=[pl.BlockSpec((tm, D), lambda i, j: (i, 0)), pl.BlockSpec((tn, D), lambda i, j: (j, 0)), tile, tile],
        out_specs=[tile, tile], out_shape=[jax.ShapeDtypeStruct((T, F), BF16)] * 2,
        compiler_params=_params(dimension_semantics=("arbitrary",) * 2),
    )(dx, w_down, a, b)


def _ple_loss(x2, p, g, w_gate, w_proj, tgt):
    def body(x_ref, p_ref, g_ref, wg_ref, wp_ref, t_ref, hp_ref, dy_ref, dsp_ref, dpp_ref, loss_ref):
        xv = x_ref[...]
        rstd = lax.rsqrt(jnp.mean(xv * xv, axis=-1, keepdims=True) + EPS)
        hp = (xv * rstd * g_ref[...]).astype(BF16)
        hp_ref[...] = hp
        gp = _sigmoid(_dot(hp, wg_ref[...], 1, 0))
        ppv = _dot(p_ref[...].astype(BF16), wp_ref[...], 1, 0)
        err = xv + gp * ppv - t_ref[...]
        part = 0.5 * jnp.sum(jnp.mean(err * err, axis=-1, keepdims=True), axis=0, keepdims=True)
        part = jnp.broadcast_to(part, loss_ref.shape)

        @pl.when(pl.program_id(0) == 0)
        def _():
            loss_ref[...] = part

        @pl.when(pl.program_id(0) > 0)
        def _():
            loss_ref[...] += part

        dy = err * (1.0 / D_MODEL)
        dy_ref[...] = dy
        dsp_ref[...] = (dy * ppv * gp * (1.0 - gp)).astype(BF16)
        dpp_ref[...] = (dy * gp).astype(BF16)

    return _row_call(body, name="ple_loss", T=x2.shape[0],
                     ins=[(x2, True), (p, True), (g, False), (w_gate, False), (w_proj, False), (tgt, True)],
                     outs=[(D_MODEL, BF16), (D_MODEL, F32), (D_MODEL, BF16), (D_MODEL, BF16)],
                     acc_outs=[((8, LANES), F32)])


def _hg_consts():
    C = HG_CHUNK
    r, c = _iota((C, C), 0), _iota((C, C), 1)
    tri = (c <= r)
    same = (r // HG_SUB) == (c // HG_SUB)
    return tri, (tri & same)


def _hg_chunk_fwd(q, f, lb, tri_b, sub_b):
    sgq = _sigmoid(q)
    qt = q * sgq
    sg = _sigmoid(f)
    fg = lb + (1.0 - lb) * sg
    kf = (1.0 - lb) * (1.0 - sg)
    logf = jnp.log(fg)
    b = _split_dot(tri_b, logf, 1, 0)
    w = _split_dot(sub_b, logf, 1, 0)
    return sgq, qt, sg, fg, kf, b, w


def _hg_scores(qs_b, kf, b, row):
    C, S = HG_CHUNK, HG_SUB
    parts, ks = [], []
    for blk in range(C // S):
        ref = jnp.zeros_like(b[0:1]) if blk == 0 else b[blk * S - 1:blk * S]
        e = jnp.exp(jnp.minimum(ref - b, EXP_CLAMP))
        e = jnp.where(row < (blk + 1) * S, e, 0.0)
        k_b = (kf * e).astype(BF16)
        ks.append((e, k_b))
        parts.append(_dot(qs_b[blk * S:(blk + 1) * S], k_b, 1, 1))
    return jnp.concatenate(parts, axis=0), ks


def _hgrn_fwd(z, lb_logits, gain, blocks):
    T = z.shape[0]
    RB = min(ROW_BLOCK, T)
    nb, cpb = T // RB, RB // HG_CHUNK
    C, DK = HG_CHUNK, HG_DK
    n = len(blocks)

    def body(*refs):
        z_ref, lg_ref, g_ref = refs[:3]
        o_ref, y_ref, st_ref = refs[3 + n:6 + n]
        s_ref = refs[6 + 2 * n]
        g_start, g_forward, g_finish = _gather_steps(refs[3:3 + n], refs[6 + n:6 + 2 * n], *refs[7 + 2 * n:])

        @pl.when(pl.program_id(0) == 0)
        def _():
            s_ref[...] = jnp.zeros_like(s_ref)
            g_start()

        lg = lg_ref[...]
        lb_all = 1.0 / (1.0 + jnp.exp(lg[1:2] - lg[0:1]))
        gain_v = g_ref[...]
        tri, sub = _hg_consts()
        tri_b, sub_b = tri.astype(BF16), sub.astype(BF16)
        row = _iota((C, DK), 0)

        def chunk(ci, carry):
            r0 = pl.multiple_of(ci * C, C)
            rows = pl.ds(r0, C)
            for h in range(HG_HEADS):
                cs = slice(h * DK, (h + 1) * DK)
                q = z_ref[rows, cs]
                f = z_ref[rows, HG_W + h * DK:HG_W + (h + 1) * DK]
                v = z_ref[rows, 2 * HG_W + h * DK:2 * HG_W + (h + 1) * DK]
                g = z_ref[rows, 3 * HG_W + h * DK:3 * HG_W + (h + 1) * DK]
                lb = lb_all[:, cs]
                _, qt, _, _, kf, b, w = _hg_chunk_fwd(q, f, lb, tri_b, sub_b)
                st = s_ref[h]
                st_ref[pl.ds(pl.multiple_of((ci * HG_HEADS + h) * DK, DK), DK), :] = st
                v_b = v.astype(BF16)
                qs_b = (qt * jnp.exp(w)).astype(BF16)
                a, _ = _hg_scores(qs_b, kf, b, row)
                a = jnp.where(tri, a, 0.0)
                qd = qt * jnp.exp(b)
                o = _dot(qd.astype(BF16), st.astype(BF16), 1, 1) + _dot(a.astype(BF16), v_b, 1, 0)
                bl = b[C - 1:C]
                kd = kf * jnp.exp(bl - b)
                s_ref[h] = st * jnp.exp(bl) + _dot(v_b, kd.astype(BF16), 0, 0)
                o_ref[rows, cs] = o
                rstd = lax.rsqrt(jnp.mean(o * o, axis=-1, keepdims=True) + EPS)
                y_ref[rows, cs] = (o * rstd * gain_v * (g * _sigmoid(g))).astype(BF16)
            return carry

        lax.fori_loop(0, cpb, chunk, 0)

        @pl.when(pl.program_id(0) == nb - 1)
        def _():
            g_forward()
            g_finish()

    hbm = pl.BlockSpec(memory_space=pl.ANY)
    res = _pallas(
        body, name="hgrn_fwd", grid=(nb,),
        in_specs=[pl.BlockSpec((RB, HG_COLS), lambda i: (i, 0)), pl.BlockSpec((2, HG_W), lambda i: (0, 0)),
                  pl.BlockSpec((1, DK), lambda i: (0, 0))] + [hbm] * n,
        out_specs=[pl.BlockSpec((RB, HG_W), lambda i: (i, 0)), pl.BlockSpec((RB, HG_W), lambda i: (i, 0)),
                   pl.BlockSpec((cpb * HG_HEADS * DK, DK), lambda i: (i, 0))] + [hbm] * n,
        out_shape=[jax.ShapeDtypeStruct((T, HG_W), F32), jax.ShapeDtypeStruct((T, HG_W), BF16),
                   jax.ShapeDtypeStruct((T // C * HG_HEADS * DK, DK), F32)]
        + [jax.ShapeDtypeStruct((N_DEV,) + b.shape, b.dtype) for b in blocks],
        scratch_shapes=[pltpu.VMEM((HG_HEADS, DK, DK), F32)] + _gather_scratch(n),
        compiler_params=_params(dimension_semantics=("arbitrary",)),
    )(z, lb_logits, gain, *blocks)
    return res[0], res[1], res[2], res[3:]


def _hgrn_bwd(z, o_raw, dy, states, lb_logits, gain, dz_buf):
    T = z.shape[0]
    RB = min(ROW_BLOCK, T)
    nb, cpb = T // RB, RB // HG_CHUNK
    C, DK, S = HG_CHUNK, HG_DK, HG_SUB

    def body(z_ref, o_ref, dy_ref, st_ref, lg_ref, g_ref, _buf_ref, dz_ref, dlg_ref, dg_ref, ds_ref, dlb_ref):
        step = pl.program_id(0)

        @pl.when(step == 0)
        def _():
            ds_ref[...] = jnp.zeros_like(ds_ref)
            dlb_ref[...] = jnp.zeros_like(dlb_ref)
            dg_ref[...] = jnp.zeros_like(dg_ref)

        lg = lg_ref[...]
        lb_all = 1.0 / (1.0 + jnp.exp(lg[1:2] - lg[0:1]))
        gain_v = g_ref[...]
        tri, sub = _hg_consts()
        tri_b, sub_b = tri.astype(BF16), sub.astype(BF16)
        row = _iota((C, DK), 0)

        def chunk(cj, carry):
            ci = cpb - 1 - cj
            r0 = pl.multiple_of(ci * C, C)
            rows = pl.ds(r0, C)
            for h in range(HG_HEADS):
                cs = slice(h * DK, (h + 1) * DK)
                q = z_ref[rows, cs]
                f = z_ref[rows, HG_W + h * DK:HG_W + (h + 1) * DK]
                v = z_ref[rows, 2 * HG_W + h * DK:2 * HG_W + (h + 1) * DK]
                g = z_ref[rows, 3 * HG_W + h * DK:3 * HG_W + (h + 1) * DK]
                lb = lb_all[:, cs]
                sgq, qt, sg, fg, kf, b, w = _hg_chunk_fwd(q, f, lb, tri_b, sub_b)
                st = st_ref[pl.ds(pl.multiple_of((ci * HG_HEADS + h) * DK, DK), DK), :]
                dst = ds_ref[h]
                o = o_ref[rows, cs]
                dyv = dy_ref[rows, cs]
                rstd = lax.rsqrt(jnp.mean(o * o, axis=-1, keepdims=True) + EPS)
                n = o * rstd
                sgg = _sigmoid(g)
                t1 = dyv * (g * sgg)
                dg_ref[...] += jnp.sum(t1 * n, axis=0, keepdims=True)
                dn = t1 * gain_v
                do = rstd * (dn - n * jnp.mean(dn * n, axis=-1, keepdims=True))
                dgate = dyv * n * gain_v * sgg * (1.0 + g * (1.0 - sgg))
                do_b = do.astype(BF16)
                v_b = v.astype(BF16)
                ew = jnp.exp(w)
                qs_b = (qt * ew).astype(BF16)
                a, ks = _hg_scores(qs_b, kf, b, row)
                a = jnp.where(tri, a, 0.0)
                eb = jnp.exp(b)
                qd = qt * eb
                bl = b[C - 1:C]
                ebl = jnp.exp(bl)
                ekd = jnp.exp(bl - b)
                kd = kf * ekd
                kd_b = kd.astype(BF16)
                qd_b = qd.astype(BF16)
                dst_b = dst.astype(BF16)
                dqd = _dot(do_b, st.astype(BF16), 1, 0)
                da = jnp.where(tri, _dot(do_b, v_b, 1, 1), 0.0)
                dv = _dot(a.astype(BF16), do_b, 0, 0) + _dot(kd_b, dst_b, 1, 1)
                dkd = _dot(v_b, dst_b, 1, 0)
                ds_ref[h] = dst * ebl + _dot(do_b, qd_b, 0, 0)
                dkd_kd = dkd * kd_b.astype(F32)
                dbl = ebl * jnp.sum(dst * st, axis=0, keepdims=True) + jnp.sum(dkd_kd, axis=0, keepdims=True)
                da_b = da.astype(BF16)
                dqs_parts = []
                dk_in = jnp.zeros((C, DK), F32)
                db_k = jnp.zeros((C, DK), F32)
                for blk in range(C // S):
                    e, k_b = ks[blk]
                    da_blk = da_b[blk * S:(blk + 1) * S]
                    dqs_parts.append(_dot(da_blk, k_b, 1, 0))
                    dks = _dot(da_blk, qs_b[blk * S:(blk + 1) * S], 0, 0)
                    dk_in = dk_in + dks * e
                    db_k = db_k + dks * k_b.astype(F32)
                dqs = jnp.concatenate(dqs_parts, axis=0)
                dqt_in = dqs * ew
                db = qs_b.astype(F32) * dqs - db_k + dqd * qd_b.astype(F32) - dkd_kd
                db = db + jnp.where(row == C - 1, dbl, 0.0)
                dlogf = _split_dot(tri_b, db, 0, 0)
                dqt = dqt_in + dqd * eb
                dkf = dk_in + dkd * ekd
                dfg = dlogf / fg - dkf
                dlb_ref[:, cs] += jnp.sum(dfg * (1.0 - sg), axis=0, keepdims=True)
                dz_ref[rows, cs] = (dqt * sgq * (1.0 + q * (1.0 - sgq))).astype(BF16)
                dz_ref[rows, HG_W + h * DK:HG_W + (h + 1) * DK] = (dfg * (1.0 - lb) * sg * (1.0 - sg)).astype(BF16)
                dz_ref[rows, 2 * HG_W + h * DK:2 * HG_W + (h + 1) * DK] = dv.astype(BF16)
                dz_ref[rows, 3 * HG_W + h * DK:3 * HG_W + (h + 1) * DK] = dgate.astype(BF16)
            return carry

        lax.fori_loop(0, cpb, chunk, 0)

        @pl.when(step == nb - 1)
        def _():
            d0 = dlb_ref[...] * lb_all * (1.0 - lb_all)
            dlg_ref[0:1, :] = d0
            dlg_ref[1:2, :] = -d0

    rev = lambda i: (nb - 1 - i, 0)
    fix = lambda i: (0, 0)
    return _pallas(
        body, name="hgrn_bwd", grid=(nb,),
        in_specs=[pl.BlockSpec((RB, HG_COLS), rev), pl.BlockSpec((RB, HG_W), rev), pl.BlockSpec((RB, HG_W), rev),
                  pl.BlockSpec((cpb * HG_HEADS * DK, DK), rev), pl.BlockSpec((2, HG_W), fix),
                  pl.BlockSpec((1, DK), fix), pl.BlockSpec(memory_space=pl.ANY)],
        out_specs=[pl.BlockSpec((RB, HG_COLS), rev), pl.BlockSpec((2, HG_W), fix), pl.BlockSpec((1, DK), fix)],
        out_shape=[jax.ShapeDtypeStruct(dz_buf.shape, BF16), jax.ShapeDtypeStruct((2, HG_W), F32),
                   jax.ShapeDtypeStruct((1, DK), F32)],
        scratch_shapes=[pltpu.VMEM((HG_HEADS, DK, DK), F32), pltpu.VMEM((1, HG_W), F32)],
        input_output_aliases={6: 0},
        compiler_params=_params(dimension_semantics=("arbitrary",)),
    )(z, o_raw, dy, states, lb_logits, gain, dz_buf)


def _head_ones():
    r, c = _iota((FOX_W, FOX_W), 0), _iota((FOX_W, FOX_W), 1)
    return ((r // FOX_DH) == (c // FOX_DH)).astype(BF16)


def _causal_blocks(nq, *, by_query):
    if by_query:
        pairs = [(q, k) for q in range(nq) for k in range(q + 1)]
    else:
        pairs = [(q, k) for k in range(nq) for q in range(k, nq)]
    return (jnp.asarray([q for q, _ in pairs], jnp.int32), jnp.asarray([k for _, k in pairs], jnp.int32))


def _log_sigmoid(x):
    return jnp.minimum(x, 0.0) - jnp.log(1.0 + jnp.exp(-jnp.abs(x)))


def _fox_prep(z, bias, qg, kg):
    T = z.shape[0]
    tm = min(ROW_BLOCK, T)
    nb = T // tm

    def body(z_ref, b_ref, qg_ref, kg_ref, q_ref, k_ref, v_ref, qa_ref, ka_ref, carry_ref):
        @pl.when(pl.program_id(0) == 0)
        def _():
            carry_ref[...] = jnp.zeros_like(carry_ref)

        ones = _head_ones()
        normed = []
        for src, g_ref in ((0, qg_ref), (1, kg_ref)):
            xv = z_ref[:, src * FOX_W:(src + 1) * FOX_W]
            ms = _split_dot(ones, xv * xv, 1, 0, mat_first=False) * (1.0 / FOX_DH)
            normed.append(xv * lax.rsqrt(ms + EPS) * g_ref[...])
        qn, kn = normed
        q_ref[...] = (qn * FOX_DH ** -0.5).astype(BF16)
        k_b = kn.astype(BF16)
        k_ref[...] = k_b
        v_ref[...] = z_ref[:, 2 * FOX_W:3 * FOX_W].astype(BF16)
        logf = _log_sigmoid(z_ref[:, 3 * FOX_W:FOX_COLS] + b_ref[...])
        r, c = _iota((tm, tm), 0), _iota((tm, tm), 1)
        tri_b = (c <= r).astype(BF16)
        cum = _split_dot(tri_b, logf, 1, 0, terms=3) + carry_ref[...]
        carry_ref[...] = cum[tm - 1:tm]
        c2 = cum * LOG2E
        hi = c2.astype(BF16)
        rem = c2 - hi.astype(F32)
        mid = rem.astype(BF16)
        lo = (rem - mid.astype(F32)).astype(BF16)
        hrow, col = _iota((LANES, 2 * FOX_W), 0), _iota((LANES, 2 * FOX_W), 1)
        base = hrow * LANES + jnp.where(hrow % 2 == 0, FOX_DH, 0)
        placed = None
        for t, part in enumerate((hi, mid, lo)):
            place = jnp.logical_and(col == base + t, hrow < FOX_HEADS).astype(BF16)
            term = _dot(part, place, 1, 0)
            placed = term if placed is None else placed + term
        colw = _iota((tm, 2 * FOX_W), 1)
        head, lane = colw // LANES, colw % LANES
        own = (lane < FOX_DH) == (head % 2 == 0)
        other = jnp.where(head % 2 == 0, lane - FOX_DH, lane)
        ones_q = jnp.where(jnp.logical_and(other >= 0, other < 3), -1.0, 0.0)
        q2 = (qn * (FOX_DH ** -0.5 * LOG2E)).astype(BF16)
        q_exp = jnp.concatenate([q2[:, (h // 2) * LANES:(h // 2 + 1) * LANES] for h in range(FOX_HEADS)], axis=1)
        k_exp = jnp.concatenate([k_b[:, (h // 2) * LANES:(h // 2 + 1) * LANES] for h in range(FOX_HEADS)], axis=1)
        qa_ref[...] = jnp.where(own, q_exp, ones_q.astype(BF16))
        ka_ref[...] = jnp.where(own, k_exp, placed.astype(BF16))

    wide = pl.BlockSpec((tm, 2 * FOX_W), lambda i: (i, 0))
    return _pallas(
        body, name="fox_prep", grid=(nb,),
        in_specs=[pl.BlockSpec((tm, SEG), lambda i: (i, 1)), pl.BlockSpec((1, LANES), lambda i: (0, 0)),
                  pl.BlockSpec((1, FOX_W), lambda i: (0, 0)), pl.BlockSpec((1, FOX_W), lambda i: (0, 0))],
        out_specs=[pl.BlockSpec((tm, FOX_W), lambda i: (i, 0))] * 3 + [wide] * 2,
        out_shape=[jax.ShapeDtypeStruct((T, FOX_W), BF16)] * 3 + [jax.ShapeDtypeStruct((T, 2 * FOX_W), BF16)] * 2,
        scratch_shapes=[pltpu.VMEM((1, LANES), F32)],
        compiler_params=_params(dimension_semantics=("arbitrary",)),
    )(z, bias, qg, kg)


def _fox_fwd(qa, ka, vb, blocks):
    T = qa.shape[0]
    tq = min(ROW_BLOCK, T)
    nq = T // tq
    NEG = -1e30
    n = len(blocks)
    n_pairs = FOX_HEADS // 2

    n_in = 3
    q_of, k_of = _causal_blocks(nq, by_query=True)
    n_tri = len(q_of)

    def body(qt_ref, kt_ref, *refs):
        q_ref, k_ref, v_ref = refs[:n_in]
        o_ref, lse_ref = refs[n_in + n:n_in + n + 2]
        m_sc, l_sc, acc_sc = refs[n_in + 2 * n + 2:n_in + 2 * n + 5]
        pr, t = pl.program_id(0), pl.program_id(1)
        qi, ki = qt_ref[t], kt_ref[t]
        g_start, g_forward, g_finish = _gather_steps(
            refs[n_in:n_in + n], refs[n_in + n + 2:n_in + 2 * n + 2], *refs[n_in + 2 * n + 5:])

        @pl.when(jnp.logical_and(pr == 0, t == 0))
        def _():
            g_start()

        @pl.when(jnp.logical_and(pr == n_pairs // 2, t == 0))
        def _():
            g_forward()

        @pl.when(ki == 0)
        def _():
            m_sc[...] = jnp.full_like(m_sc, NEG)
            l_sc[...] = jnp.zeros_like(l_sc)
            acc_sc[...] = jnp.zeros_like(acc_sc)

        lane = _iota((tq, LANES), 1)

        def block(masked):
            vv = v_ref[...]
            for hh in range(2):
                hs = slice(hh * LANES, (hh + 1) * LANES)
                s = _dot(q_ref[:, hs], k_ref[:, hs], 1, 1)
                tiles = [s[:, j * LANES:(j + 1) * LANES] for j in range(tq // LANES)]
                if masked:
                    row, col = _iota((tq, LANES), 0), _iota((tq, LANES), 1)
                    tiles = [jnp.where(row >= col + j * LANES, t, NEG) for j, t in enumerate(tiles)]
                m_old = m_sc[hh]
                top = jnp.broadcast_to(jnp.max(functools.reduce(jnp.maximum, tiles), axis=-1, keepdims=True),
                                       (tq, LANES))
                m_new = jnp.maximum(m_old, top)
                alpha = jnp.exp2(m_old - m_new)
                ps = [jnp.exp2(t - m_new) for t in tiles]
                l_sc[hh] = alpha * l_sc[hh] + functools.reduce(jnp.add, ps)
                m_sc[hh] = m_new
                p_b = jnp.concatenate([p.astype(BF16) for p in ps], axis=1)
                acc_sc[hh] = alpha * acc_sc[hh] + _dot(p_b, vv, 1, 0)

        @pl.when(ki < qi)
        def _():
            block(False)

        @pl.when(ki == qi)
        def _():
            block(True)
            l0 = jnp.sum(l_sc[0], axis=-1, keepdims=True)
            l1 = jnp.sum(l_sc[1], axis=-1, keepdims=True)
            o_ref[...] = jnp.where(lane < FOX_DH, acc_sc[0] * (1.0 / l0), acc_sc[1] * (1.0 / l1))
            lse_ref[:, :LANES] = m_sc[0] + jnp.log2(l0)
            lse_ref[:, LANES:] = m_sc[1] + jnp.log2(l1)

        @pl.when(jnp.logical_and(pr == n_pairs - 1, t == n_tri - 1))
        def _():
            g_finish()

    qmap = lambda p, t, qt, kt: (qt[t], p)
    kmap = lambda p, t, qt, kt: (kt[t], p)
    hbm = pl.BlockSpec(memory_space=pl.ANY)
    grid_spec = pltpu.PrefetchScalarGridSpec(
        num_scalar_prefetch=2, grid=(n_pairs, n_tri),
        in_specs=[pl.BlockSpec((tq, 2 * LANES), qmap), pl.BlockSpec((tq, 2 * LANES), kmap),
                  pl.BlockSpec((tq, LANES), kmap)] + [hbm] * n,
        out_specs=[pl.BlockSpec((tq, LANES), qmap), pl.BlockSpec((tq, 2 * LANES), qmap)] + [hbm] * n,
        scratch_shapes=[pltpu.VMEM((2, tq, LANES), F32)] * 3 + _gather_scratch(n))
    res = _pallas(
        body, name="fox_fwd", grid_spec=grid_spec,
        out_shape=[jax.ShapeDtypeStruct((T, FOX_W), F32), jax.ShapeDtypeStruct((T, 2 * FOX_W), F32)]
        + [jax.ShapeDtypeStruct((N_DEV,) + b.shape, b.dtype) for b in blocks],
        compiler_params=_params(dimension_semantics=("arbitrary",) * 2),
    )(q_of, k_of, qa, ka, vb, *blocks)
    return res[0], res[1], res[2:]


def _fox_bwd(qs, kn, vb, qa, ka, o, do, lse, hs):
    T = qs.shape[0]
    tq = min(ROW_BLOCK, T)
    nq = T // tq
    n = len(hs)
    n_pairs = FOX_HEADS // 2

    q_of, k_of = _causal_blocks(nq, by_query=False)
    n_tri = len(q_of)

    def body(qt_ref, kt_ref, *refs):
        q_ref, k_ref, v_ref, qa_ref, ka_ref, o_ref, do_ref, lse_ref = refs[:8]
        dq_ref, dk_ref, dv_ref, dcs_ref, drs_ref = refs[8 + n:13 + n]
        pr, t = pl.program_id(0), pl.program_id(1)
        qi, ki = qt_ref[t], kt_ref[t]
        x_start, x_finish = _chip_exchange_steps(refs[8:8 + n], refs[13 + n:13 + 2 * n], *refs[13 + 2 * n:])

        @pl.when(jnp.logical_and(pr == 0, t == 0))
        def _():
            x_start()
            drs_ref[...] = jnp.zeros_like(drs_ref)

        @pl.when(t == 0)
        def _():
            dq_ref[...] = jnp.zeros_like(dq_ref)

        @pl.when(qi == ki)
        def _():
            dk_ref[...] = jnp.zeros_like(dk_ref)
            dv_ref[...] = jnp.zeros_like(dv_ref)
            dcs_ref[...] = jnp.zeros_like(dcs_ref)

        def block(masked):
            lane = _iota((tq, LANES), 1)
            qv, kv, vv = q_ref[...], k_ref[...], v_ref[...]
            ov, dov = o_ref[...], do_ref[...]
            qrows = pl.ds(pl.multiple_of(qi * tq, tq), tq)
            dq_acc = jnp.zeros((tq, LANES), F32)
            dk_acc = jnp.zeros((tq, LANES), F32)
            dv_acc = jnp.zeros((tq, LANES), F32)
            dcs_acc = jnp.zeros((8, tq), F32)
            drs_acc = jnp.zeros((tq, LANES), F32)
            prod = dov * ov
            nt = tq // LANES
            for hh in range(2):
                in_head = (lane < FOX_DH) if hh == 0 else (lane >= FOX_DH)
                hs_ = slice(hh * LANES, (hh + 1) * LANES)
                zb = jnp.zeros_like(qv)
                qm = jnp.where(in_head, qv, zb)
                km = jnp.where(in_head, kv, zb)
                dom = jnp.where(in_head, dov, 0.0).astype(BF16)
                delta_b = jnp.broadcast_to(jnp.sum(jnp.where(in_head, prod, 0.0), axis=1, keepdims=True), (tq, LANES))
                lse_b = lse_ref[:, hs_]
                s = _dot(qa_ref[:, hs_], ka_ref[:, hs_], 1, 1)
                dp = _dot(dom, vv, 1, 1)
                p_tiles, ds_tiles, col_tiles = [], [], []
                row_part = jnp.zeros((tq, LANES), F32)
                for j in range(nt):
                    js = slice(j * LANES, (j + 1) * LANES)
                    p = jnp.exp2(s[:, js] - lse_b)
                    if masked:
                        p = jnp.where(_iota((tq, LANES), 0) >= _iota((tq, LANES), 1) + j * LANES, p, 0.0)
                    ds = p * (dp[:, js] - delta_b)
                    p_tiles.append(p.astype(BF16))
                    ds_tiles.append(ds.astype(BF16))
                    col_tiles.append(jnp.sum(ds, axis=0, keepdims=True))
                    row_part = row_part + ds
                p_b = jnp.concatenate(p_tiles, axis=1)
                ds_b = jnp.concatenate(ds_tiles, axis=1)
                dv_acc = dv_acc + _dot(p_b, dom, 0, 0)
                dq_acc = dq_acc + _dot(ds_b, km, 1, 0)
                dk_acc = dk_acc + _dot(ds_b, qm, 0, 0)
                dcs_acc = dcs_acc + jnp.where(_iota((8, tq), 0) == hh, jnp.concatenate(col_tiles, axis=1), 0.0)
                rowsum = jnp.sum(row_part, axis=1, keepdims=True)
                drs_acc = drs_acc + jnp.where(lane == 2 * pr + hh, rowsum, 0.0)
            drs_ref[qrows, :] += drs_acc
            dq_ref[qrows, :] += dq_acc
            dk_ref[...] += dk_acc
            dv_ref[...] += dv_acc
            dcs_ref[0] += dcs_acc

        @pl.when(qi > ki)
        def _():
            block(False)

        @pl.when(qi == ki)
        def _():
            block(True)

        @pl.when(jnp.logical_and(pr == n_pairs - 1, t == n_tri - 1))
        def _():
            x_finish()

    qmap = lambda p, t, qt, kt: (qt[t], p)
    kmap = lambda p, t, qt, kt: (kt[t], p)
    hbm = pl.BlockSpec(memory_space=pl.ANY)
    grid_spec = pltpu.PrefetchScalarGridSpec(
        num_scalar_prefetch=2, grid=(n_pairs, n_tri),
        in_specs=[pl.BlockSpec((tq, LANES), qmap), pl.BlockSpec((tq, LANES), kmap), pl.BlockSpec((tq, LANES), kmap),
                  pl.BlockSpec((tq, 2 * LANES), qmap), pl.BlockSpec((tq, 2 * LANES), kmap),
                  pl.BlockSpec((tq, LANES), qmap), pl.BlockSpec((tq, LANES), qmap), pl.BlockSpec((tq, 2 * LANES), qmap)]
        + [hbm] * n,
        out_specs=[pl.BlockSpec((T, LANES), lambda p, t, qt, kt: (0, p)), pl.BlockSpec((tq, LANES), kmap),
                   pl.BlockSpec((tq, LANES), kmap), pl.BlockSpec((1, 8, tq), lambda p, t, qt, kt: (p, 0, kt[t])),
                   pl.BlockSpec((T, LANES), lambda p, t, qt, kt: (0, 0))] + [hbm] * n,
        scratch_shapes=_chip_exchange_scratch(n))
    res = _pallas(
        body, name="fox_bwd", grid_spec=grid_spec,
        out_shape=[jax.ShapeDtypeStruct((T, FOX_W), F32)] * 3
        + [jax.ShapeDtypeStruct((n_pairs, 8, T), F32), jax.ShapeDtypeStruct((T, LANES), F32)]
        + _chip_exchange_shapes(hs),
        compiler_params=_params(dimension_semantics=("arbitrary",) * 2),
    )(q_of, k_of, qs, kn, vb, qa, ka, o, do, lse, *hs)
    res = list(res)
    return res[:5] + [res[5:]]


def _fox_post(z, dq, dk, dv, dcs, drs, bias, qg, kg, dz_buf):
    T = z.shape[0]
    tm = min(ROW_BLOCK, T)
    nb = T // tm

    def body(z_ref, dq_ref, dk_ref, dv_ref, dcs_ref, drs_ref, b_ref, qg_ref, kg_ref, _buf_ref, dz_ref, dqg_ref, dkg_ref,
             db_ref, carry_ref):
        @pl.when(pl.program_id(0) == 0)
        def _():
            carry_ref[...] = jnp.zeros_like(carry_ref)
            dqg_ref[...] = jnp.zeros_like(dqg_ref)
            dkg_ref[...] = jnp.zeros_like(dkg_ref)
            db_ref[...] = jnp.zeros_like(db_ref)

        ones = _head_ones()
        for src, g_ref, d_ref, dg_ref, scale in ((0, qg_ref, dq_ref, dqg_ref, FOX_DH ** -0.5), (1, kg_ref, dk_ref, dkg_ref, 1.0)):
            xv = z_ref[:, src * FOX_W:(src + 1) * FOX_W]
            ms = _split_dot(ones, xv * xv, 1, 0, mat_first=False) * (1.0 / FOX_DH)
            rstd = lax.rsqrt(ms + EPS)
            xh = xv * rstd
            dn = d_ref[...] * scale
            dg_ref[...] += jnp.sum(dn * xh, axis=0, keepdims=True)
            dxh = dn * g_ref[...]
            mean = _split_dot(ones, dxh * xh, 1, 0, mat_first=False) * (1.0 / FOX_DH)
            dz_ref[:, src * FOX_W:(src + 1) * FOX_W] = (rstd * (dxh - xh * mean)).astype(BF16)
        dz_ref[:, 2 * FOX_W:3 * FOX_W] = dv_ref[...].astype(BF16)
        row8 = _iota((8, tm), 0)
        dct = jnp.zeros((8, tm), F32)
        for h in range(FOX_HEADS):
            src_row = dcs_ref[h // 2][h % 2:h % 2 + 1, :]
            dct = dct + jnp.where(row8 == h, src_row, 0.0)
        dct = drs_ref[...].T[0:8] - dct
        r, c = _iota((tm, tm), 0), _iota((tm, tm), 1)
        upper_b = (r >= c).astype(BF16)
        rc = _split_dot(upper_b, dct, 1, 0, mat_first=False) + carry_ref[...]
        carry_ref[...] = rc[:, 0:1]
        full = jnp.concatenate([rc, jnp.zeros((LANES - 8, tm), F32)], axis=0)
        dlogf = full.T
        xf = z_ref[:, 3 * FOX_W:FOX_COLS] + b_ref[...]
        df = dlogf * (1.0 - _sigmoid(xf))
        dz_ref[:, 3 * FOX_W:FOX_COLS] = df.astype(BF16)
        dz_ref[:, FOX_COLS:] = jnp.zeros((tm, SEG - FOX_COLS), BF16)
        db_ref[...] += jnp.sum(df, axis=0, keepdims=True)

    rev = lambda i: (nb - 1 - i, 0)
    fix2 = lambda i: (0, 0)
    return _pallas(
        body, name="fox_post", grid=(nb,),
        in_specs=[pl.BlockSpec((tm, SEG), lambda i: (nb - 1 - i, 1)), pl.BlockSpec((tm, FOX_W), rev),
                  pl.BlockSpec((tm, FOX_W), rev),
                  pl.BlockSpec((tm, FOX_W), rev), pl.BlockSpec((FOX_HEADS // 2, 8, tm), lambda i: (0, 0, nb - 1 - i)),
                  pl.BlockSpec((tm, LANES), rev),
                  pl.BlockSpec((1, LANES), fix2), pl.BlockSpec((1, FOX_W), fix2), pl.BlockSpec((1, FOX_W), fix2),
                  pl.BlockSpec(memory_space=pl.ANY)],
        out_specs=[pl.BlockSpec((tm, SEG), lambda i: (nb - 1 - i, 1)), pl.BlockSpec((1, FOX_W), fix2),
                   pl.BlockSpec((1, FOX_W), fix2), pl.BlockSpec((1, LANES), fix2)],
        out_shape=[jax.ShapeDtypeStruct(dz_buf.shape, BF16), jax.ShapeDtypeStruct((1, FOX_W), F32),
                   jax.ShapeDtypeStruct((1, FOX_W), F32), jax.ShapeDtypeStruct((1, LANES), F32)],
        scratch_shapes=[pltpu.VMEM((8, 1), F32)],
        input_output_aliases={9: 0},
        compiler_params=_params(dimension_semantics=("arbitrary",)),
    )(z, dq, dk, dv, dcs, drs, bias, qg, kg, dz_buf)


def _local_step(x, p, tgt, sm, W, rest_chunks, core):
    lbl, og, fb = sm["hg_lb_logits"], sm["hg_onorm_g"], sm["fox_f_bias"]
    fbias = jnp.pad(fb, ((0, 0), (0, LANES - FOX_HEADS)))
    qg = jnp.tile(sm["fox_q_norm_g"], (1, FOX_HEADS))
    kg = jnp.tile(sm["fox_k_norm_g"], (1, FOX_HEADS))

    h = _rms_fwd(x, sm["norm_mix_g"], name="rms_mix")
    rest = dict(zip(BIG[1:], rest_chunks))
    first, second = ["w_ffn_gate"], ["w_ffn_up", "w_ffn_down"]
    third = [n for n in BIG[1:] if n not in first + second]
    z, got1 = _matmul(h, W["w_in"], gather=[rest[n] for n in first], name="mm_z")
    o_raw, ya, states, got2 = _hgrn_fwd(z, lbl, og, [rest[n] for n in second])
    qs, kn, vb, qa, ka = _fox_prep(z, fbias, qg, kg)
    yb, lse, got3 = _fox_fwd(qa, ka, vb, [rest[n] for n in third])
    W = dict(W, **{n: _full_of_chunks(n, g)
                   for n, g in zip(first + second + third, list(got1) + list(got2) + list(got3))})
    merged, ua, ub = _merge_fwd(ya, yb, W["w_branch_a"], W["w_branch_b"], z)
    x1, hf = _matmul(merged, W["w_out"], add=x, norm_fwd=sm["norm_ffn_g"], name="mm_x1")
    a, b, act = _swiglu_fwd(hf, W["w_ffn_gate"], W["w_ffn_up"])
    x2 = _matmul(act, W["w_ffn_down"], add=x1, name="mm_x2")
    hp, dy, dsp, dpp, loss = _ple_loss(x2, p, sm["norm_ple_g"], W["w_ple_gate"], W["w_ple_proj"], tgt)

    G = {}
    G["w_ple_proj"] = _matmul(p, dpp, ta=True, out_dtype=BF16, name="mm_dw_ple_proj")
    G["w_ple_gate"] = _matmul(hp, dsp, ta=True, out_dtype=BF16, name="mm_dw_ple_gate")
    dx2, d_ple_g = _matmul(dsp, W["w_ple_gate"], tb=True, norm_bwd=(x2, sm["norm_ple_g"], dy), name="mm_dx2")
    G["w_ffn_down"] = _matmul(act, dx2, ta=True, out_dtype=BF16, name="mm_dw_ffn_down")
    da, db = _swiglu_bwd(dx2, W["w_ffn_down"], a, b)
    G["w_ffn_gate"] = _matmul(hf, da, ta=True, out_dtype=BF16, name="mm_dw_ffn_gate")
    G["w_ffn_up"] = _matmul(hf, db, ta=True, out_dtype=BF16, name="mm_dw_ffn_up")
    dhf = _matmul(da, W["w_ffn_gate"], tb=True, name="mm_dhf_a")
    dx1, d_ffn_g = _matmul(db, W["w_ffn_up"], tb=True, add=dhf, norm_bwd=(x1, sm["norm_ffn_g"], dx2), name="mm_dx1")
    G["w_out"] = _matmul(merged, dx1, ta=True, out_dtype=BF16, name="mm_dw_out")
    dua, dub, dz, dya, dyb = _merge_bwd(dx1, W["w_out"], W["w_branch_a"], W["w_branch_b"], ua, ub, z)
    G["w_branch_a"] = _matmul(ya, dua, ta=True, out_dtype=BF16, name="mm_dw_branch_a")
    G["w_branch_b"] = _matmul(yb, dub, ta=True, out_dtype=BF16, name="mm_dw_branch_b")
    hb_rest = _sibling_sums({n: G[n] for n in BIG[1:]}, core, tag="rest")
    dq, dk, dv, dcs, drs, got_rest = _fox_bwd(qs, kn, vb, qa, ka, yb, dyb, lse, hb_rest)
    dz, d_qg, d_kg, d_fb = _fox_post(z, dq, dk, dv, dcs, drs, fbias, qg, kg, dz)
    dz, d_lbl, d_og = _hgrn_bwd(z, o_raw, dya, states, lbl, og, dz)
    G["w_in"] = _matmul(h, dz, ta=True, out_dtype=BF16, name="mm_dw_in")
    hb_in = _sibling_sums({"w_in": G["w_in"]}, core, tag="w_in")
    (grad_x, d_mix_g), got_in = _matmul(dz, W["w_in"], tb=True, exchange=hb_in,
                                        norm_bwd=(x, sm["norm_mix_g"], dx1), name="mm_dx")

    gs = {"norm_mix_g": d_mix_g, "hg_lb_logits": d_lbl, "hg_onorm_g": d_og, "fox_f_bias": d_fb[:, :FOX_HEADS],
          "fox_q_norm_g": d_qg.reshape(FOX_HEADS, FOX_DH).sum(0, keepdims=True),
          "fox_k_norm_g": d_kg.reshape(FOX_HEADS, FOX_DH).sum(0, keepdims=True),
          "norm_ffn_g": d_ffn_g, "norm_ple_g": d_ple_g}
    return loss, grad_x, gs, hb_in + hb_rest, list(got_in) + list(got_rest)


def _pack_rows(parts, total):
    buf = jnp.concatenate(parts, axis=-2)
    pad = total - buf.shape[-2]
    widths = [(0, 0)] * (buf.ndim - 2) + [(0, pad), (0, 0)]
    return jnp.pad(buf, widths)


def _chunk_of_shard(n, w):
    if n == "w_in":
        return jnp.pad(w, ((0, 0), (0, IN_SHARD_PAD - IN_SHARD)))
    if n in ("w_ffn_gate", "w_ffn_up"):
        return jnp.pad(w, ((0, 0), (0, FF_SHARD_PAD - FF_SHARD)))
    if n == "w_ffn_down":
        return jnp.pad(w, ((0, FF_SHARD_PAD - FF_SHARD), (0, 0)))
    return w


def _full_of_chunks(n, g):
    _, a, b = g.shape
    if BIG_SHAPE[n][2] == 0:
        return g.reshape(N_DEV * a, b)
    if n == "w_in":
        w = g[:, :, :IN_SHARD].transpose(1, 0, 2).reshape(a, IN_COLS)
        gap = jnp.zeros((a, SEG - FOX_LOGICAL), g.dtype)
        return jnp.concatenate([w[:, :HG_COLS], w[:, HG_COLS:HG_COLS + FOX_LOGICAL], gap, w[:, HG_COLS + FOX_LOGICAL:]],
                               axis=1)
    return g.transpose(1, 0, 2).reshape(a, N_DEV * b)


def _chunks_of_full(n, g):
    if BIG_SHAPE[n][2] == 0:
        return g.reshape(N_DEV, g.shape[0] // N_DEV, g.shape[1])
    if n == "w_in":
        a = g.shape[0]
        w = jnp.concatenate([g[:, :HG_COLS], g[:, SEG:SEG + FOX_LOGICAL], g[:, 2 * SEG:]], axis=1)
        w = w.reshape(a, N_DEV, IN_SHARD).transpose(1, 0, 2)
        return jnp.pad(w, ((0, 0), (0, 0), (0, IN_SHARD_PAD - IN_SHARD)))
    return g.reshape(g.shape[0], N_DEV, g.shape[1] // N_DEV).transpose(1, 0, 2)


def _pack_small(vals, loss_row=None):
    parts = [vals[n].reshape(SMALL_ROWS[n], -1) for n in SMALL]
    parts = [jnp.pad(v, ((0, 0), (0, LANES - v.shape[1]))) for v in parts]
    if loss_row is not None:
        parts.append(loss_row)
    return _pack_rows(parts, SMALL_TOTAL)


def _unpack_small(buf, like):
    out, r0 = {}, 0
    for n in SMALL:
        rows, size = SMALL_ROWS[n], like[n].size
        blk = buf[r0:r0 + rows]
        out[n] = (blk if size == rows * LANES else blk[:, :size]).reshape(like[n].shape)
        r0 += rows
    return out


def _place():
    return lax.axis_index("x"), lax.axis_index("y"), lax.axis_index("c")


def _gather_steps(x_refs, out_refs, send_sems, recv_sems, local_sems):
    n = len(x_refs)
    x, y, c = _place()
    me, sibling = (x, y, c), (x, y, 1 - c)
    chips = [(1 - x, y), (x, 1 - y), (1 - x, 1 - y)]

    def slot(i, px, py, pc):
        return out_refs[i].at[4 * px + 2 * py + pc]

    def copy(k, i, blk, to, own=False):
        return pltpu.make_async_remote_copy(
            src_ref=x_refs[i] if own else slot(i, *blk), dst_ref=slot(i, *blk),
            send_sem=send_sems.at[k, i], recv_sem=recv_sems.at[k, i], device_id=to, device_id_type=MESH)

    def mine():
        return [pltpu.make_async_copy(x_refs[i], slot(i, *me), local_sems.at[i]) for i in range(n)]

    def first():
        cps = [copy(0, i, me, sibling, own=True) for i in range(n)]
        return cps + [copy(1 + j, i, me, (*chip, c), own=True) for j, chip in enumerate(chips) for i in range(n)]

    def passed():
        return [copy(4 + j, i, (*chip, c), sibling) for j, chip in enumerate(chips) for i in range(n)]

    def start():
        for cp in mine() + first():
            cp.start()

    def forward():
        fws = passed()
        for j, chip in enumerate(chips):
            for i in range(n):
                copy(1 + j, i, (*chip, c), me).wait_recv()
                fws[j * n + i].start()

    def finish():
        for i in range(n):
            copy(0, i, sibling, me).wait_recv()
        for j, chip in enumerate(chips):
            for i in range(n):
                copy(4 + j, i, (*chip, 1 - c), me).wait_recv()
        for cp in first() + passed():
            cp.wait_send()
        for cp in mine():
            cp.wait()

    return start, forward, finish


def _gather_scratch(n):
    return [pltpu.SemaphoreType.DMA((7, n)), pltpu.SemaphoreType.DMA((7, n)), pltpu.SemaphoreType.DMA((n,))]


def _all_gather(blocks, *, name):
    n = len(blocks)

    def body(*refs):
        for step in _gather_steps(refs[:n], refs[n:2 * n], *refs[2 * n:]):
            step()

    hbm = pl.BlockSpec(memory_space=pl.ANY)
    return _pallas(
        body, name=name, out_shape=[jax.ShapeDtypeStruct((N_DEV,) + b.shape, b.dtype) for b in blocks],
        in_specs=[hbm] * n, out_specs=[hbm] * n, scratch_shapes=_gather_scratch(n),
    )(*blocks)


def _sibling_exchange(gs, *, name):
    n = len(gs)

    def body(*refs):
        g_refs, out_refs = refs[:n], refs[n:2 * n]
        send_sems, recv_sems = refs[2 * n:]
        x, y, c = _place()
        cps = [pltpu.make_async_remote_copy(
            src_ref=g_refs[i].at[:, pl.ds(1 - c, 1)], dst_ref=out_refs[i], send_sem=send_sems.at[i],
            recv_sem=recv_sems.at[i], device_id=(x, y, 1 - c), device_id_type=MESH) for i in range(n)]
        for cp in cps:
            cp.start()
        for cp in cps:
            cp.wait()

    hbm = pl.BlockSpec(memory_space=pl.ANY)
    return _pallas(
        body, name=name, out_shape=[jax.ShapeDtypeStruct((N_CHIP, 1) + g.shape[2:], g.dtype) for g in gs],
        in_specs=[hbm] * n, out_specs=[hbm] * n,
        scratch_shapes=[pltpu.SemaphoreType.DMA((n,)), pltpu.SemaphoreType.DMA((n,))],
    )(*gs)


def _chip_sum(g4, got, core, *, name):
    _, _, a, b = g4.shape

    def body(c_ref, g_ref, r_ref, h_ref):
        h_ref[0] = (g_ref[0, 0].astype(F32) + r_ref[0, 0].astype(F32)).astype(BF16)

    grid_spec = pltpu.PrefetchScalarGridSpec(
        num_scalar_prefetch=1, grid=(N_CHIP,),
        in_specs=[pl.BlockSpec((1, 1, a, b), lambda j, c: (j, c[0], 0, 0)),
                  pl.BlockSpec((1, 1, a, b), lambda j, c: (j, 0, 0, 0))],
        out_specs=[pl.BlockSpec((1, a, b), lambda j, c: (j, 0, 0))])
    return _pallas(
        body, name=name, grid_spec=grid_spec, out_shape=[jax.ShapeDtypeStruct((N_CHIP, a, b), BF16)],
        compiler_params=_params(dimension_semantics=("arbitrary",)),
    )(core, g4, got)[0]


def _chip_exchange(hs, *, name):
    n = len(hs)

    def body(*refs):
        for step in _chip_exchange_steps(refs[:n], refs[n:2 * n], *refs[2 * n:]):
            step()

    hbm = pl.BlockSpec(memory_space=pl.ANY)
    return _pallas(
        body, name=name, out_shape=_chip_exchange_shapes(hs), in_specs=[hbm] * n, out_specs=[hbm] * n,
        scratch_shapes=_chip_exchange_scratch(n),
    )(*hs)


def _chip_exchange_steps(h_refs, out_refs, send_sems, recv_sems):
    n = len(h_refs)
    x, y, c = _place()
    chips = [(1 - x, y), (x, 1 - y), (1 - x, 1 - y)]

    def copies():
        return [pltpu.make_async_remote_copy(
            src_ref=h_refs[i].at[2 * px + py], dst_ref=out_refs[i].at[k], send_sem=send_sems.at[k, i],
            recv_sem=recv_sems.at[k, i], device_id=(px, py, c), device_id_type=MESH)
            for k, (px, py) in enumerate(chips) for i in range(n)]

    def start():
        for cp in copies():
            cp.start()

    def finish():
        for cp in copies():
            cp.wait()

    return start, finish


def _chip_exchange_shapes(hs):
    return [jax.ShapeDtypeStruct((3,) + h.shape[1:], h.dtype) for h in hs]


def _chip_exchange_scratch(n):
    return [pltpu.SemaphoreType.DMA((3, n)), pltpu.SemaphoreType.DMA((3, n))]


def _sibling_sums(G, core, *, tag):
    g4 = []
    for n, g in G.items():
        gc = _chunks_of_full(n, g)
        g4.append(gc.reshape((N_CHIP, 2) + gc.shape[1:]))
    got = _sibling_exchange(g4, name="grads_to_sibling_" + tag)
    return [_chip_sum(g, r, core, name="chip_sum_" + n) for n, g, r in zip(G, g4, got)]


def _adam_math(w, g, m, v):
    m = ADAM_B1 * m + (1.0 - ADAM_B1) * g
    v = ADAM_B2 * v + (1.0 - ADAM_B2) * (g * g)
    m_hat = m / (1.0 - ADAM_B1 ** ADAM_STEP)
    v_hat = v / (1.0 - ADAM_B2 ** ADAM_STEP)
    delta = -ADAM_LR * (m_hat / (jnp.sqrt(v_hat) + ADAM_EPS) + ADAM_WD * w)
    return delta, m, v


def _adam_shard(hb, got, chip, w, m, v, *, name):
    _, r, c = w.shape
    _, a, b = hb.shape
    tr = r if r <= 512 else 256
    ta = tr if r // tr > 1 else a

    def body(j_ref, h_ref, r_ref, w_ref, m_ref, v_ref, g_ref, d_ref, nm_ref, nv_ref):
        parts = [h_ref[0], r_ref[0], r_ref[1], r_ref[2]]
        g = None
        for part in parts:
            part = part[:tr, :c].astype(F32)
            g = part if g is None else g + part
        d, nm, nv = _adam_math(w_ref[0], g, m_ref[0], v_ref[0])
        g_ref[0] = g
        d_ref[0] = d
        nm_ref[0] = nm
        nv_ref[0] = nv

    blk = pl.BlockSpec((1, tr, c), lambda i, j: (0, i, 0))
    grid_spec = pltpu.PrefetchScalarGridSpec(
        num_scalar_prefetch=1, grid=(r // tr,),
        in_specs=[pl.BlockSpec((1, ta, b), lambda i, j: (j[0], i, 0)),
                  pl.BlockSpec((3, ta, b), lambda i, j: (0, i, 0)), blk, blk, blk],
        out_specs=[blk] * 4)
    return _pallas(
        body, name=name, grid_spec=grid_spec, out_shape=[jax.ShapeDtypeStruct((1, r, c), F32)] * 4,
        compiler_params=_params(dimension_semantics=("arbitrary",)),
    )(chip, hb, got, w, m, v)


def _small_all_reduce_adam(gs, w, m, v):
    def body(g_ref, w_ref, m_ref, v_ref, sum_ref, d_ref, nm_ref, nv_ref, gather, send_sems, recv_sems):
        x, y, c = _place()
        my = 4 * x + 2 * y + c
        gather[my] = g_ref[...]
        cps = []
        for k in range(1, N_DEV):
            to = (x ^ (k >> 2), y ^ ((k >> 1) & 1), c ^ (k & 1))
            cps.append(pltpu.make_async_remote_copy(
                src_ref=g_ref, dst_ref=gather.at[my], send_sem=send_sems.at[k - 1], recv_sem=recv_sems.at[k - 1],
                device_id=to, device_id_type=MESH))
        for cp in cps:
            cp.start()
        for cp in cps:
            cp.wait()
        total = gather[0]
        for d in range(1, N_DEV):
            total = total + gather[d]
        dlt, nm, nv = _adam_math(w_ref[...], total, m_ref[...], v_ref[...])
        sum_ref[...] = total
        d_ref[...] = dlt
        nm_ref[...] = nm
        nv_ref[...] = nv

    vm = pl.BlockSpec(memory_space=pltpu.VMEM)
    return _pallas(
        body, name="small_all_reduce_adam", out_shape=[jax.ShapeDtypeStruct((SMALL_TOTAL, LANES), F32)] * 4,
        in_specs=[vm] * 4, out_specs=[vm] * 4,
        scratch_shapes=[pltpu.VMEM((N_DEV, SMALL_TOTAL, LANES), F32), pltpu.SemaphoreType.DMA((7,)),
                        pltpu.SemaphoreType.DMA((7,))],
            )(gs, w, m, v)


def kernel(x, p, norm_mix_g, w_in, hg_lb_logits, hg_onorm_g, fox_f_bias, fox_q_norm_g, fox_k_norm_g, w_branch_a, w_branch_b, w_out, norm_ffn_g, w_ffn_gate, w_ffn_up, w_ffn_down, norm_ple_g, w_ple_gate, w_ple_proj, loss_target, m_norm_mix_g, m_w_in, m_hg_lb_logits, m_hg_onorm_g, m_fox_f_bias, m_fox_q_norm_g, m_fox_k_norm_g, m_w_branch_a, m_w_branch_b, m_w_out, m_norm_ffn_g, m_w_ffn_gate, m_w_ffn_up, m_w_ffn_down, m_norm_ple_g, m_w_ple_gate, m_w_ple_proj, v_norm_mix_g, v_w_in, v_hg_lb_logits, v_hg_onorm_g, v_fox_f_bias, v_fox_q_norm_g, v_fox_k_norm_g, v_w_branch_a, v_w_branch_b, v_w_out, v_norm_ffn_g, v_w_ffn_gate, v_w_ffn_up, v_w_ffn_down, v_norm_ple_g, v_w_ple_gate, v_w_ple_proj):
    args = dict(locals())
    wts = {n: args[n] for n in BIG + SMALL}
    mom = {n: args["m_" + n] for n in BIG + SMALL}
    var = {n: args["v_" + n] for n in BIG + SMALL}
    sm = {n: wts[n] for n in SMALL}

    xi, yi, ci = _place()
    core = jnp.reshape(ci, (1,)).astype(jnp.int32)
    chip = jnp.reshape(2 * xi + yi, (1,)).astype(jnp.int32)
    chunks = [_chunk_of_shard(n, wts[n][0].astype(BF16)) for n in BIG]
    assert BIG[0] == "w_in"
    w_in_full = _full_of_chunks("w_in", _all_gather(chunks[:1], name="w_in_all_gather")[0])

    loss_blk, grad_x, gs, hb, got = _local_step(
        x[0], p[0, 0], loss_target[0], sm, {"w_in": w_in_full}, chunks[1:], core)

    g_big, d_big, nm_big, nv_big = {}, {}, {}, {}
    for n, h, r in zip(BIG, hb, got):
        g_big[n], d_big[n], nm_big[n], nv_big[n] = _adam_shard(h, r, chip, wts[n], mom[n], var[n], name="adam_" + n)

    s_sum, s_d, s_nm, s_nv = _small_all_reduce_adam(
        _pack_small(gs, loss_blk[0:1]), _pack_small(sm), _pack_small({n: mom[n] for n in SMALL}),
        _pack_small({n: var[n] for n in SMALL}))
    loss = s_sum[LOSS_ROW, 0]
    g_small, d_small, nm_small, nv_small = (_unpack_small(t, sm) for t in (s_sum, s_d, s_nm, s_nv))

    order = ["norm_mix_g", "w_in", "hg_lb_logits", "hg_onorm_g", "fox_f_bias", "fox_q_norm_g", "fox_k_norm_g",
             "w_branch_a", "w_branch_b", "w_out", "norm_ffn_g", "w_ffn_gate", "w_ffn_up", "w_ffn_down", "norm_ple_g",
             "w_ple_gate", "w_ple_proj"]
    outs = [loss, grad_x[None]]
    for big, small in ((g_big, g_small), (d_big, d_small), (nm_big, nm_small), (nv_big, nv_small)):
        outs += [big[n] if n in big else small[n] for n in order]
    return tuple(outs)
```

```python
import functools

import jax
import jax.numpy as jnp
from jax import lax
from jax.experimental import pallas as pl
from jax.experimental.pallas import tpu as pltpu

F32 = jnp.float32
BF16 = jnp.bfloat16

D_MODEL = 1024
PLE_DIM = 256
HG_HEADS = 4
HG_DK = 128
HG_CHUNK = 64
HG_SUB = 16
HG_W = HG_HEADS * HG_DK
FOX_HEADS = 8
FOX_DH = 64
FOX_W = FOX_HEADS * FOX_DH
D_FF = 2816
EPS = 1e-6
N_DEV = 8
N_CHIP = 4
LANES = 128
FOX_COLS = 3 * FOX_W + LANES
HG_COLS = 4 * HG_W
GATE_COLS = 2 * D_MODEL
IN_COLS = HG_COLS + 3 * FOX_W + FOX_HEADS + GATE_COLS
FOX_LOGICAL = 3 * FOX_W + FOX_HEADS
SEG = 2048
IN_PAD = 3 * SEG
IN_SHARD = IN_COLS // N_DEV
IN_SHARD_PAD = 768
FF_SHARD = D_FF // N_DEV
FF_SHARD_PAD = 384
FF_PAD = N_DEV * FF_SHARD_PAD
EXP_CLAMP = 80.0
LOG2E = 1.4426950408889634

ADAM_LR = 0.001
ADAM_B1 = 0.9
ADAM_B2 = 0.999
ADAM_EPS = 1e-08
ADAM_WD = 0.01
ADAM_STEP = 10

MESH = pl.DeviceIdType.MESH
VMEM_LIMIT = 56 * 1024 * 1024
ROW_BLOCK = 512

BIG = ["w_in", "w_branch_a", "w_branch_b", "w_out", "w_ffn_gate", "w_ffn_up", "w_ffn_down",
       "w_ple_gate", "w_ple_proj"]
TRANSPOSED = ("w_in", "w_ffn_gate", "w_ffn_up")
BIG_SHAPE = {
    "w_in": (IN_COLS, D_MODEL, 0), "w_branch_a": (HG_W, D_MODEL, 1), "w_branch_b": (FOX_W, D_MODEL, 1),
    "w_out": (D_MODEL, D_MODEL, 0), "w_ffn_gate": (D_FF, D_MODEL, 0), "w_ffn_up": (D_FF, D_MODEL, 0),
    "w_ffn_down": (D_FF, D_MODEL, 0), "w_ple_gate": (D_MODEL, D_MODEL, 0), "w_ple_proj": (PLE_DIM, D_MODEL, 1),
}

SMALL = ["norm_mix_g", "hg_lb_logits", "hg_onorm_g", "fox_f_bias", "fox_q_norm_g", "fox_k_norm_g",
         "norm_ffn_g", "norm_ple_g"]
SMALL_ROWS = {"norm_mix_g": 8, "hg_lb_logits": 8, "hg_onorm_g": 1, "fox_f_bias": 1, "fox_q_norm_g": 1,
              "fox_k_norm_g": 1, "norm_ffn_g": 8, "norm_ple_g": 8}
SMALL_TOTAL = 40
LOSS_ROW = 36


def _pallas(body, **kw):
    return pl.pallas_call(body, **kw)


def _params(**kw):
    return pltpu.CompilerParams(vmem_limit_bytes=VMEM_LIMIT, **kw)


def _pick(n, target):
    if n <= target:
        return n
    best = None
    for t in range(LANES, target + 1, LANES):
        if n % t == 0:
            best = t
    assert best is not None, (n, target)
    return best


def _dot(a, b, ca, cb):
    return lax.dot_general(a, b, (((ca,), (cb,)), ((), ())), preferred_element_type=F32)


def _split_dot(mat, x, ca, cb, terms=2, mat_first=True):
    acc = None
    rem = x
    for _ in range(terms):
        part = rem.astype(BF16)
        rem = rem - part.astype(F32)
        p = _dot(mat, part, ca, cb) if mat_first else _dot(part, mat, ca, cb)
        acc = p if acc is None else acc + p
    return acc


def _sigmoid(x):
    return 1.0 / (1.0 + jnp.exp(-x))


def _iota(shape, dim):
    return lax.broadcasted_iota(jnp.int32, shape, dim)


def _matmul(a, b, *, name, ta=False, tb=False, out_dtype=F32, add=None, exchange=None, gather=None,
            norm_fwd=None, norm_bwd=None):
    assert exchange is None or gather is None
    (K, M) = a.shape if ta else a.shape[::-1]
    (N, Kb) = b.shape if tb else b.shape[::-1]
    assert K == Kb, (a.shape, b.shape, ta, tb)
    if ta:
        tm, tn, tk = _pick(M, 2 * ROW_BLOCK), _pick(N, 2 * ROW_BLOCK), _pick(K, 4 * ROW_BLOCK)
    else:
        tm, tn, tk = _pick(M, 2 * ROW_BLOCK), _pick(N, 2048), _pick(K, 3072)
    if norm_bwd is not None:
        tm = _pick(M, ROW_BLOCK)
    nk = K // tk
    use_scratch = nk > 1 and out_dtype != F32
    if norm_fwd is not None or norm_bwd is not None:
        assert tn == N and not use_scratch and out_dtype == F32

    hs = list(exchange or gather or [])
    n_x = len(hs)
    grid = (M // tm, N // tn, nk)
    a_spec = pl.BlockSpec((tk, tm), lambda i, j, k: (k, i)) if ta else pl.BlockSpec((tm, tk), lambda i, j, k: (i, k))
    b_spec = pl.BlockSpec((tn, tk), lambda i, j, k: (j, k)) if tb else pl.BlockSpec((tk, tn), lambda i, j, k: (k, j))
    o_spec = pl.BlockSpec((tm, tn), lambda i, j, k: (i, j))
    row_vec = pl.BlockSpec((1, N), lambda i, j, k: (0, 0))
    hbm = pl.BlockSpec(memory_space=pl.ANY)
    extra_in = [(add, o_spec)] if add is not None else []
    extra_out = []
    if norm_fwd is not None:
        extra_in += [(norm_fwd, row_vec)]
        extra_out += [(jax.ShapeDtypeStruct((M, N), BF16), o_spec)]
    if norm_bwd is not None:
        extra_in += [(norm_bwd[0], o_spec), (norm_bwd[1], row_vec), (norm_bwd[2], o_spec)]
        extra_out += [(jax.ShapeDtypeStruct((1, N), F32), row_vec)]
    if gather is not None:
        ride_shapes, ride_scratch = [jax.ShapeDtypeStruct((N_DEV,) + h.shape, h.dtype) for h in hs], _gather_scratch(n_x)
    else:
        ride_shapes, ride_scratch = _chip_exchange_shapes(hs), (_chip_exchange_scratch(n_x) if n_x else [])
    n_ex_in, n_ex_out = len(extra_in), len(extra_out)

    def body(*refs):
        refs = list(refs)
        a_ref, b_ref = refs[:2]
        ex_in = refs[2:2 + n_ex_in]
        ride_in = refs[2 + n_ex_in:2 + n_ex_in + n_x]
        base = 2 + n_ex_in + n_x
        o_ref = refs[base]
        ex_out = refs[base + 1:base + 1 + n_ex_out]
        ride_out = refs[base + 1 + n_ex_out:base + 1 + n_ex_out + n_x]
        scratch = refs[base + 1 + n_ex_out + n_x:]
        at = [pl.program_id(d) for d in range(3)]
        k = at[2]
        if n_x:
            steps = _gather_steps if gather is not None else _chip_exchange_steps
            ride = steps(ride_in, ride_out, *scratch[-len(ride_scratch):])

            @pl.when(jnp.logical_and(at[0] == 0, jnp.logical_and(at[1] == 0, at[2] == 0)))
            def _():
                ride[0]()
        p = _dot(a_ref[...].astype(BF16), b_ref[...].astype(BF16), 0 if ta else 1, 1 if tb else 0)

        def finish(r):
            ins = list(ex_in)
            outs = list(ex_out)
            if add is not None:
                r = r + ins.pop(0)[...].astype(F32)
            if norm_fwd is not None:
                g_ref = ins.pop(0)
                rstd = lax.rsqrt(jnp.mean(r * r, axis=-1, keepdims=True) + EPS)
                outs.pop(0)[...] = (r * rstd * g_ref[...]).astype(BF16)
            if norm_bwd is not None:
                x_ref, g_ref, dres_ref = ins.pop(0), ins.pop(0), ins.pop(0)
                dg_ref = outs.pop(0)
                xv = x_ref[...]
                rstd = lax.rsqrt(jnp.mean(xv * xv, axis=-1, keepdims=True) + EPS)
                xh = xv * rstd
                part = jnp.sum(r * xh, axis=0, keepdims=True)

                @pl.when(at[0] == 0)
                def _():
                    dg_ref[...] = part

                @pl.when(at[0] > 0)
                def _():
                    dg_ref[...] += part

                dxh = r * g_ref[...]
                r = rstd * (dxh - xh * jnp.mean(dxh * xh, axis=-1, keepdims=True)) + dres_ref[...]
            o_ref[...] = r.astype(out_dtype)

        if nk == 1:
            finish(p)
        elif not use_scratch:
            @pl.when(k == 0)
            def _():
                o_ref[...] = p

            @pl.when(jnp.logical_and(k > 0, k < nk - 1))
            def _():
                o_ref[...] += p

            @pl.when(k == nk - 1)
            def _():
                finish(o_ref[...] + p)
        else:
            acc_ref = scratch[0]

            @pl.when(k == 0)
            def _():
                acc_ref[...] = p

            @pl.when(k > 0)
            def _():
                acc_ref[...] += p

            @pl.when(k == nk - 1)
            def _():
                finish(acc_ref[...])

        if n_x:
            @pl.when(jnp.logical_and(at[0] == grid[0] - 1, jnp.logical_and(at[1] == grid[1] - 1, at[2] == nk - 1)))
            def _():
                for step in ride[1:]:
                    step()

    res = _pallas(
        body, name=name, grid=grid,
        in_specs=[a_spec, b_spec] + [s for _, s in extra_in] + [hbm] * n_x,
        out_specs=[o_spec] + [s for _, s in extra_out] + [hbm] * n_x,
        out_shape=[jax.ShapeDtypeStruct((M, N), out_dtype)] + [s for s, _ in extra_out] + ride_shapes,
        scratch_shapes=([pltpu.VMEM((tm, tn), F32)] if use_scratch else []) + ride_scratch,
        compiler_params=_params(dimension_semantics=("arbitrary",) * 3),
    )(a, b, *[v for v, _ in extra_in], *hs)
    res = list(res)
    main = res[0] if n_ex_out == 0 else tuple(res[:1 + n_ex_out])
    return (main, res[1 + n_ex_out:]) if n_x else main


def _row_map(nb, reverse, seg):
    if reverse:
        return lambda i: (nb - 1 - i, seg)
    return lambda i: (i, seg)


def _row_call(body, *, name, T, ins, outs, acc_outs=(), tm=ROW_BLOCK, reverse=False):
    tm = min(tm, T)
    nb = T // tm
    in_specs, args = [], []
    for arr, how in ins:
        args.append(arr)
        if how is True:
            in_specs.append(pl.BlockSpec((tm, arr.shape[1]), _row_map(nb, reverse, 0)))
        elif how is False:
            in_specs.append(pl.BlockSpec(arr.shape, lambda i, _n=arr.ndim: (0,) * _n))
        else:
            in_specs.append(pl.BlockSpec((tm, SEG), _row_map(nb, reverse, how[0])))
    out_specs, out_shape = [], []
    for o in outs:
        c, dt = o[0], o[1]
        total, seg = o[2] if len(o) > 2 else (c, 0)
        out_specs.append(pl.BlockSpec((tm, c), _row_map(nb, reverse, seg)))
        out_shape.append(jax.ShapeDtypeStruct((T, total), dt))
    for shp, dt in acc_outs:
        out_specs.append(pl.BlockSpec(shp, lambda i, _n=len(shp): (0,) * _n))
        out_shape.append(jax.ShapeDtypeStruct(shp, dt))
    return _pallas(body, name=name, grid=(nb,), in_specs=in_specs, out_specs=out_specs, out_shape=out_shape,
                   compiler_params=_params(dimension_semantics=("arbitrary",)))(*args)


def _rms_fwd(x, g, *, name):
    T = x.shape[0]

    def body(x_ref, g_ref, h_ref):
        xv = x_ref[...]
        rstd = lax.rsqrt(jnp.mean(xv * xv, axis=-1, keepdims=True) + EPS)
        h_ref[...] = (xv * rstd * g_ref[...]).astype(BF16)

    return _row_call(body, name=name, T=T, ins=[(x, True), (g, False)], outs=[(D_MODEL, BF16)])[0]


def _rms_bwd(x, g, dh, dres, *, name):
    T = x.shape[0]

    def body(x_ref, g_ref, dh_ref, dres_ref, dx_ref, dg_ref):
        xv = x_ref[...]
        rstd = lax.rsqrt(jnp.mean(xv * xv, axis=-1, keepdims=True) + EPS)
        xh = xv * rstd
        dhv = dh_ref[...]
        part = jnp.sum(dhv * xh, axis=0, keepdims=True)

        @pl.when(pl.program_id(0) == 0)
        def _():
            dg_ref[...] = part

        @pl.when(pl.program_id(0) > 0)
        def _():
            dg_ref[...] += part

        dxh = dhv * g_ref[...]
        dx_ref[...] = rstd * (dxh - xh * jnp.mean(dxh * xh, axis=-1, keepdims=True)) + dres_ref[...]

    return _row_call(body, name=name, T=T, ins=[(x, True), (g, False), (dh, True), (dres, True)],
                     outs=[(D_MODEL, F32)], acc_outs=[((1, D_MODEL), F32)])


def _merge_fwd(ya, yb, wa, wb, zg):
    def body(ya_ref, yb_ref, wa_ref, wb_ref, zg_ref, m_ref, ua_ref, ub_ref):
        ua = _dot(ya_ref[...].astype(BF16), wa_ref[...], 1, 0)
        ub = _dot(yb_ref[...].astype(BF16), wb_ref[...], 1, 0)
        ga = _sigmoid(zg_ref[:, :D_MODEL])
        gb = _sigmoid(zg_ref[:, D_MODEL:])
        m_ref[...] = (ga * ua + gb * ub).astype(BF16)
        ua_ref[...] = ua.astype(BF16)
        ub_ref[...] = ub.astype(BF16)

    return _row_call(body, name="merge_fwd", T=ya.shape[0],
                     ins=[(ya, True), (yb, True), (wa, False), (wb, False), (zg, (2,))],
                     outs=[(D_MODEL, BF16)] * 3)


def _merge_bwd(dx1, w_out, wa, wb, ua, ub, zg):
    def body(dx_ref, wo_ref, wa_ref, wb_ref, ua_ref, ub_ref, zg_ref, dua_ref, dub_ref, dzg_ref, dya_ref, dyb_ref):
        dmv = _dot(dx_ref[...].astype(BF16), wo_ref[...], 1, 1)
        ga = _sigmoid(zg_ref[:, :D_MODEL])
        gb = _sigmoid(zg_ref[:, D_MODEL:])
        dua = (dmv * ga).astype(BF16)
        dub = (dmv * gb).astype(BF16)
        dua_ref[...] = dua
        dub_ref[...] = dub
        dzg_ref[:, :D_MODEL] = (dmv * ua_ref[...].astype(F32) * ga * (1.0 - ga)).astype(BF16)
        dzg_ref[:, D_MODEL:] = (dmv * ub_ref[...].astype(F32) * gb * (1.0 - gb)).astype(BF16)
        dya_ref[...] = _dot(dua, wa_ref[...], 1, 1)
        dyb_ref[...] = _dot(dub, wb_ref[...], 1, 1)

    return _row_call(body, name="merge_bwd", T=dx1.shape[0],
                     ins=[(dx1, True), (w_out, False), (wa, False), (wb, False), (ua, True), (ub, True), (zg, (2,))],
                     outs=[(D_MODEL, BF16), (D_MODEL, BF16), (SEG, BF16, (IN_PAD, 2)), (HG_W, F32), (FOX_W, F32)])


def _swiglu_fwd(hf, w_gate, w_up):
    T, D = hf.shape
    F = w_gate.shape[0]
    tm, tn = _pick(T, 2 * ROW_BLOCK), _pick(F, 768)

    def body(h_ref, wg_ref, wu_ref, a_ref, b_ref, o_ref):
        hv = h_ref[...]
        a_b = _dot(hv, wg_ref[...], 1, 1).astype(BF16)
        b_b = _dot(hv, wu_ref[...], 1, 1).astype(BF16)
        a_ref[...] = a_b
        b_ref[...] = b_b
        av = a_b.astype(F32)
        o_ref[...] = (av * _sigmoid(av) * b_b.astype(F32)).astype(BF16)

    tile = pl.BlockSpec((tm, tn), lambda i, j: (i, j))
    wcol = pl.BlockSpec((tn, D), lambda i, j: (j, 0))
    return _pallas(
        body, name="swiglu_fwd", grid=(T // tm, F // tn),
        in_specs=[pl.BlockSpec((tm, D), lambda i, j: (i, 0)), wcol, wcol],
        out_specs=[tile] * 3, out_shape=[jax.ShapeDtypeStruct((T, F), BF16)] * 3,
        compiler_params=_params(dimension_semantics=("arbitrary",) * 2),
    )(hf, w_gate, w_up)


def _swiglu_bwd(dx, w_down, a, b):
    T, D = dx.shape
    F = w_down.shape[0]
    tm, tn = _pick(T, 2 * ROW_BLOCK), _pick(F, 768)

    def body(dx_ref, w_ref, a_ref, b_ref, da_ref, db_ref):
        dact = _dot(dx_ref[...].astype(BF16), w_ref[...], 1, 1)
        av = a_ref[...].astype(F32)
        bv = b_ref[...].astype(F32)
        sg = _sigmoid(av)
        da_ref[...] = (dact * bv * sg * (1.0 + av * (1.0 - sg))).astype(BF16)
        db_ref[...] = (dact * av * sg).astype(BF16)

    tile = pl.BlockSpec((tm, tn), lambda i, j: (i, j))
    return _pallas(
        body, name="swiglu_bwd", grid=(T // tm, F // tn),
        in_specs=[pl.BlockSpec((tm, D), lambda i, j: (i, 0)), pl.BlockSpec((tn, D), lambda i, j: (j, 0)), tile, tile],
        out_specs=[tile, tile], out_shape=[jax.ShapeDtypeStruct((T, F), BF16)] * 2,
        compiler_params=_params(dimension_semantics=("arbitrary",) * 2),
    )(dx, w_down, a, b)


def _ple_loss(x2, p, g, w_gate, w_proj, tgt):
    def body(x_ref, p_ref, g_ref, wg_ref, wp_ref, t_ref, hp_ref, dy_ref, dsp_ref, dpp_ref, loss_ref):
        xv = x_ref[...]
        rstd = lax.rsqrt(jnp.mean(xv * xv, axis=-1, keepdims=True) + EPS)
        hp = (xv * rstd * g_ref[...]).astype(BF16)
        hp_ref[...] = hp
        gp = _sigmoid(_dot(hp, wg_ref[...], 1, 0))
        ppv = _dot(p_ref[...].astype(BF16), wp_ref[...], 1, 0)
        err = xv + gp * ppv - t_ref[...]
        part = 0.5 * jnp.sum(jnp.mean(err * err, axis=-1, keepdims=True), axis=0, keepdims=True)
        part = jnp.broadcast_to(part, loss_ref.shape)

        @pl.when(pl.program_id(0) == 0)
        def _():
            loss_ref[...] = part

        @pl.when(pl.program_id(0) > 0)
        def _():
            loss_ref[...] += part

        dy = err * (1.0 / D_MODEL)
        dy_ref[...] = dy
        dsp_ref[...] = (dy * ppv * gp * (1.0 - gp)).astype(BF16)
        dpp_ref[...] = (dy * gp).astype(BF16)

    return _row_call(body, name="ple_loss", T=x2.shape[0],
                     ins=[(x2, True), (p, True), (g, False), (w_gate, False), (w_proj, False), (tgt, True)],
                     outs=[(D_MODEL, BF16), (D_MODEL, F32), (D_MODEL, BF16), (D_MODEL, BF16)],
                     acc_outs=[((8, LANES), F32)])


def _hg_consts():
    C = HG_CHUNK
    r, c = _iota((C, C), 0), _iota((C, C), 1)
    tri = (c <= r)
    same = (r // HG_SUB) == (c // HG_SUB)
    return tri, (tri & same)


def _hg_chunk_fwd(q, f, lb, tri_b, sub_b):
    sgq = _sigmoid(q)
    qt = q * sgq
    sg = _sigmoid(f)
    fg = lb + (1.0 - lb) * sg
    kf = (1.0 - lb) * (1.0 - sg)
    logf = jnp.log(fg)
    b = _split_dot(tri_b, logf, 1, 0)
    w = _split_dot(sub_b, logf, 1, 0)
    return sgq, qt, sg, fg, kf, b, w


def _hg_scores(qs_b, kf, b, row):
    C, S = HG_CHUNK, HG_SUB
    parts, ks = [], []
    for blk in range(C // S):
        ref = jnp.zeros_like(b[0:1]) if blk == 0 else b[blk * S - 1:blk * S]
        e = jnp.exp(jnp.minimum(ref - b, EXP_CLAMP))
        e = jnp.where(row < (blk + 1) * S, e, 0.0)
        k_b = (kf * e).astype(BF16)
        ks.append((e, k_b))
        parts.append(_dot(qs_b[blk * S:(blk + 1) * S], k_b, 1, 1))
    return jnp.concatenate(parts, axis=0), ks


def _hgrn_fwd(z, lb_logits, gain, blocks):
    T = z.shape[0]
    RB = min(ROW_BLOCK, T)
    nb, cpb = T // RB, RB // HG_CHUNK
    C, DK = HG_CHUNK, HG_DK
    n = len(blocks)

    def body(*refs):
        z_ref, lg_ref, g_ref = refs[:3]
        o_ref, y_ref, st_ref = refs[3 + n:6 + n]
        s_ref = refs[6 + 2 * n]
        g_start, g_forward, g_finish = _gather_steps(refs[3:3 + n], refs[6 + n:6 + 2 * n], *refs[7 + 2 * n:])

        @pl.when(pl.program_id(0) == 0)
        def _():
            s_ref[...] = jnp.zeros_like(s_ref)
            g_start()

        lg = lg_ref[...]
        lb_all = 1.0 / (1.0 + jnp.exp(lg[1:2] - lg[0:1]))
        gain_v = g_ref[...]
        tri, sub = _hg_consts()
        tri_b, sub_b = tri.astype(BF16), sub.astype(BF16)
        row = _iota((C, DK), 0)

        def chunk(ci, carry):
            r0 = pl.multiple_of(ci * C, C)
            rows = pl.ds(r0, C)
            for h in range(HG_HEADS):
                cs = slice(h * DK, (h + 1) * DK)
                q = z_ref[rows, cs]
                f = z_ref[rows, HG_W + h * DK:HG_W + (h + 1) * DK]
                v = z_ref[rows, 2 * HG_W + h * DK:2 * HG_W + (h + 1) * DK]
                g = z_ref[rows, 3 * HG_W + h * DK:3 * HG_W + (h + 1) * DK]
                lb = lb_all[:, cs]
                _, qt, _, _, kf, b, w = _hg_chunk_fwd(q, f, lb, tri_b, sub_b)
                st = s_ref[h]
                st_ref[pl.ds(pl.multiple_of((ci * HG_HEADS + h) * DK, DK), DK), :] = st
                v_b = v.astype(BF16)
                qs_b = (qt * jnp.exp(w)).astype(BF16)
                a, _ = _hg_scores(qs_b, kf, b, row)
                a = jnp.where(tri, a, 0.0)
                qd = qt * jnp.exp(b)
                o = _dot(qd.astype(BF16), st.astype(BF16), 1, 1) + _dot(a.astype(BF16), v_b, 1, 0)
                bl = b[C - 1:C]
                kd = kf * jnp.exp(bl - b)
                s_ref[h] = st * jnp.exp(bl) + _dot(v_b, kd.astype(BF16), 0, 0)
                o_ref[rows, cs] = o
                rstd = lax.rsqrt(jnp.mean(o * o, axis=-1, keepdims=True) + EPS)
                y_ref[rows, cs] = (o * rstd * gain_v * (g * _sigmoid(g))).astype(BF16)
            return carry

        lax.fori_loop(0, cpb, chunk, 0)

        @pl.when(pl.program_id(0) == nb - 1)
        def _():
            g_forward()
            g_finish()

    hbm = pl.BlockSpec(memory_space=pl.ANY)
    res = _pallas(
        body, name="hgrn_fwd", grid=(nb,),
        in_specs=[pl.BlockSpec((RB, HG_COLS), lambda i: (i, 0)), pl.BlockSpec((2, HG_W), lambda i: (0, 0)),
                  pl.BlockSpec((1, DK), lambda i: (0, 0))] + [hbm] * n,
        out_specs=[pl.BlockSpec((RB, HG_W), lambda i: (i, 0)), pl.BlockSpec((RB, HG_W), lambda i: (i, 0)),
                   pl.BlockSpec((cpb * HG_HEADS * DK, DK), lambda i: (i, 0))] + [hbm] * n,
        out_shape=[jax.ShapeDtypeStruct((T, HG_W), F32), jax.ShapeDtypeStruct((T, HG_W), BF16),
                   jax.ShapeDtypeStruct((T // C * HG_HEADS * DK, DK), F32)]
        + [jax.ShapeDtypeStruct((N_DEV,) + b.shape, b.dtype) for b in blocks],
        scratch_shapes=[pltpu.VMEM((HG_HEADS, DK, DK), F32)] + _gather_scratch(n),
        compiler_params=_params(dimension_semantics=("arbitrary",)),
    )(z, lb_logits, gain, *blocks)
    return res[0], res[1], res[2], res[3:]


def _hgrn_bwd(z, o_raw, dy, states, lb_logits, gain, dz_buf):
    T = z.shape[0]
    RB = min(ROW_BLOCK, T)
    nb, cpb = T // RB, RB // HG_CHUNK
    C, DK, S = HG_CHUNK, HG_DK, HG_SUB

    def body(z_ref, o_ref, dy_ref, st_ref, lg_ref, g_ref, _buf_ref, dz_ref, dlg_ref, dg_ref, ds_ref, dlb_ref):
        step = pl.program_id(0)

        @pl.when(step == 0)
        def _():
            ds_ref[...] = jnp.zeros_like(ds_ref)
            dlb_ref[...] = jnp.zeros_like(dlb_ref)
            dg_ref[...] = jnp.zeros_like(dg_ref)

        lg = lg_ref[...]
        lb_all = 1.0 / (1.0 + jnp.exp(lg[1:2] - lg[0:1]))
        gain_v = g_ref[...]
        tri, sub = _hg_consts()
        tri_b, sub_b = tri.astype(BF16), sub.astype(BF16)
        row = _iota((C, DK), 0)

        def chunk(cj, carry):
            ci = cpb - 1 - cj
            r0 = pl.multiple_of(ci * C, C)
            rows = pl.ds(r0, C)
            for h in range(HG_HEADS):
                cs = slice(h * DK, (h + 1) * DK)
                q = z_ref[rows, cs]
                f = z_ref[rows, HG_W + h * DK:HG_W + (h + 1) * DK]
                v = z_ref[rows, 2 * HG_W + h * DK:2 * HG_W + (h + 1) * DK]
                g = z_ref[rows, 3 * HG_W + h * DK:3 * HG_W + (h + 1) * DK]
                lb = lb_all[:, cs]
                sgq, qt, sg, fg, kf, b, w = _hg_chunk_fwd(q, f, lb, tri_b, sub_b)
                st = st_ref[pl.ds(pl.multiple_of((ci * HG_HEADS + h) * DK, DK), DK), :]
                dst = ds_ref[h]
                o = o_ref[rows, cs]
                dyv = dy_ref[rows, cs]
                rstd = lax.rsqrt(jnp.mean(o * o, axis=-1, keepdims=True) + EPS)
                n = o * rstd
                sgg = _sigmoid(g)
                t1 = dyv * (g * sgg)
                dg_ref[...] += jnp.sum(t1 * n, axis=0, keepdims=True)
                dn = t1 * gain_v
                do = rstd * (dn - n * jnp.mean(dn * n, axis=-1, keepdims=True))
                dgate = dyv * n * gain_v * sgg * (1.0 + g * (1.0 - sgg))
                do_b = do.astype(BF16)
                v_b = v.astype(BF16)
                ew = jnp.exp(w)
                qs_b = (qt * ew).astype(BF16)
                a, ks = _hg_scores(qs_b, kf, b, row)
                a = jnp.where(tri, a, 0.0)
                eb = jnp.exp(b)
                qd = qt * eb
                bl = b[C - 1:C]
                ebl = jnp.exp(bl)
                ekd = jnp.exp(bl - b)
                kd = kf * ekd
                kd_b = kd.astype(BF16)
                qd_b = qd.astype(BF16)
                dst_b = dst.astype(BF16)
                dqd = _dot(do_b, st.astype(BF16), 1, 0)
                da = jnp.where(tri, _dot(do_b, v_b, 1, 1), 0.0)
                dv = _dot(a.astype(BF16), do_b, 0, 0) + _dot(kd_b, dst_b, 1, 1)
                dkd = _dot(v_b, dst_b, 1, 0)
                ds_ref[h] = dst * ebl + _dot(do_b, qd_b, 0, 0)
                dkd_kd = dkd * kd_b.astype(F32)
                dbl = ebl * jnp.sum(dst * st, axis=0, keepdims=True) + jnp.sum(dkd_kd, axis=0, keepdims=True)
                da_b = da.astype(BF16)
                dqs_parts = []
                dk_in = jnp.zeros((C, DK), F32)
                db_k = jnp.zeros((C, DK), F32)
                for blk in range(C // S):
                    e, k_b = ks[blk]
                    da_blk = da_b[blk * S:(blk + 1) * S]
                    dqs_parts.append(_dot(da_blk, k_b, 1, 0))
                    dks = _dot(da_blk, qs_b[blk * S:(blk + 1) * S], 0, 0)
                    dk_in = dk_in + dks * e
                    db_k = db_k + dks * k_b.astype(F32)
                dqs = jnp.concatenate(dqs_parts, axis=0)
                dqt_in = dqs * ew
                db = qs_b.astype(F32) * dqs - db_k + dqd * qd_b.astype(F32) - dkd_kd
                db = db + jnp.where(row == C - 1, dbl, 0.0)
                dlogf = _split_dot(tri_b, db, 0, 0)
                dqt = dqt_in + dqd * eb
                dkf = dk_in + dkd * ekd
                dfg = dlogf / fg - dkf
                dlb_ref[:, cs] += jnp.sum(dfg * (1.0 - sg), axis=0, keepdims=True)
                dz_ref[rows, cs] = (dqt * sgq * (1.0 + q * (1.0 - sgq))).astype(BF16)
                dz_ref[rows, HG_W + h * DK:HG_W + (h + 1) * DK] = (dfg * (1.0 - lb) * sg * (1.0 - sg)).astype(BF16)
                dz_ref[rows, 2 * HG_W + h * DK:2 * HG_W + (h + 1) * DK] = dv.astype(BF16)
                dz_ref[rows, 3 * HG_W + h * DK:3 * HG_W + (h + 1) * DK] = dgate.astype(BF16)
            return carry

        lax.fori_loop(0, cpb, chunk, 0)

        @pl.when(step == nb - 1)
        def _():
            d0 = dlb_ref[...] * lb_all * (1.0 - lb_all)
            dlg_ref[0:1, :] = d0
            dlg_ref[1:2, :] = -d0

    rev = lambda i: (nb - 1 - i, 0)
    fix = lambda i: (0, 0)
    return _pallas(
        body, name="hgrn_bwd", grid=(nb,),
        in_specs=[pl.BlockSpec((RB, HG_COLS), rev), pl.BlockSpec((RB, HG_W), rev), pl.BlockSpec((RB, HG_W), rev),
                  pl.BlockSpec((cpb * HG_HEADS * DK, DK), rev), pl.BlockSpec((2, HG_W), fix),
                  pl.BlockSpec((1, DK), fix), pl.BlockSpec(memory_space=pl.ANY)],
        out_specs=[pl.BlockSpec((RB, HG_COLS), rev), pl.BlockSpec((2, HG_W), fix), pl.BlockSpec((1, DK), fix)],
        out_shape=[jax.ShapeDtypeStruct(dz_buf.shape, BF16), jax.ShapeDtypeStruct((2, HG_W), F32),
                   jax.ShapeDtypeStruct((1, DK), F32)],
        scratch_shapes=[pltpu.VMEM((HG_HEADS, DK, DK), F32), pltpu.VMEM((1, HG_W), F32)],
        input_output_aliases={6: 0},
        compiler_params=_params(dimension_semantics=("arbitrary",)),
    )(z, o_raw, dy, states, lb_logits, gain, dz_buf)


def _head_ones():
    r, c = _iota((FOX_W, FOX_W), 0), _iota((FOX_W, FOX_W), 1)
    return ((r // FOX_DH) == (c // FOX_DH)).astype(BF16)


def _causal_blocks(nq, *, by_query):
    if by_query:
        pairs = [(q, k) for q in range(nq) for k in range(q + 1)]
    else:
        pairs = [(q, k) for k in range(nq) for q in range(k, nq)]
    return (jnp.asarray([q for q, _ in pairs], jnp.int32), jnp.asarray([k for _, k in pairs], jnp.int32))


def _log_sigmoid(x):
    return jnp.minimum(x, 0.0) - jnp.log(1.0 + jnp.exp(-jnp.abs(x)))


def _fox_prep(z, bias, qg, kg):
    T = z.shape[0]
    tm = min(ROW_BLOCK, T)
    nb = T // tm

    def body(z_ref, b_ref, qg_ref, kg_ref, q_ref, k_ref, v_ref, qa_ref, ka_ref, carry_ref):
        @pl.when(pl.program_id(0) == 0)
        def _():
            carry_ref[...] = jnp.zeros_like(carry_ref)

        ones = _head_ones()
        normed = []
        for src, g_ref in ((0, qg_ref), (1, kg_ref)):
            xv = z_ref[:, src * FOX_W:(src + 1) * FOX_W]
            ms = _split_dot(ones, xv * xv, 1, 0, mat_first=False) * (1.0 / FOX_DH)
            normed.append(xv * lax.rsqrt(ms + EPS) * g_ref[...])
        qn, kn = normed
        q_ref[...] = (qn * FOX_DH ** -0.5).astype(BF16)
        k_b = kn.astype(BF16)
        k_ref[...] = k_b
        v_ref[...] = z_ref[:, 2 * FOX_W:3 * FOX_W].astype(BF16)
        logf = _log_sigmoid(z_ref[:, 3 * FOX_W:FOX_COLS] + b_ref[...])
        r, c = _iota((tm, tm), 0), _iota((tm, tm), 1)
        tri_b = (c <= r).astype(BF16)
        cum = _split_dot(tri_b, logf, 1, 0, terms=3) + carry_ref[...]
        carry_ref[...] = cum[tm - 1:tm]
        c2 = cum * LOG2E
        hi = c2.astype(BF16)
        rem = c2 - hi.astype(F32)
        mid = rem.astype(BF16)
        lo = (rem - mid.astype(F32)).astype(BF16)
        hrow, col = _iota((LANES, 2 * FOX_W), 0), _iota((LANES, 2 * FOX_W), 1)
        base = hrow * LANES + jnp.where(hrow % 2 == 0, FOX_DH, 0)
        placed = None
        for t, part in enumerate((hi, mid, lo)):
            place = jnp.logical_and(col == base + t, hrow < FOX_HEADS).astype(BF16)
            term = _dot(part, place, 1, 0)
            placed = term if placed is None else placed + term
        colw = _iota((tm, 2 * FOX_W), 1)
        head, lane = colw // LANES, colw % LANES
        own = (lane < FOX_DH) == (head % 2 == 0)
        other = jnp.where(head % 2 == 0, lane - FOX_DH, lane)
        ones_q = jnp.where(jnp.logical_and(other >= 0, other < 3), -1.0, 0.0)
        q2 = (qn * (FOX_DH ** -0.5 * LOG2E)).astype(BF16)
        q_exp = jnp.concatenate([q2[:, (h // 2) * LANES:(h // 2 + 1) * LANES] for h in range(FOX_HEADS)], axis=1)
        k_exp = jnp.concatenate([k_b[:, (h // 2) * LANES:(h // 2 + 1) * LANES] for h in range(FOX_HEADS)], axis=1)
        qa_ref[...] = jnp.where(own, q_exp, ones_q.astype(BF16))
        ka_ref[...] = jnp.where(own, k_exp, placed.astype(BF16))

    wide = pl.BlockSpec((tm, 2 * FOX_W), lambda i: (i, 0))
    return _pallas(
        body, name="fox_prep", grid=(nb,),
        in_specs=[pl.BlockSpec((tm, SEG), lambda i: (i, 1)), pl.BlockSpec((1, LANES), lambda i: (0, 0)),
                  pl.BlockSpec((1, FOX_W), lambda i: (0, 0)), pl.BlockSpec((1, FOX_W), lambda i: (0, 0))],
        out_specs=[pl.BlockSpec((tm, FOX_W), lambda i: (i, 0))] * 3 + [wide] * 2,
        out_shape=[jax.ShapeDtypeStruct((T, FOX_W), BF16)] * 3 + [jax.ShapeDtypeStruct((T, 2 * FOX_W), BF16)] * 2,
        scratch_shapes=[pltpu.VMEM((1, LANES), F32)],
        compiler_params=_params(dimension_semantics=("arbitrary",)),
    )(z, bias, qg, kg)


def _fox_fwd(qa, ka, vb, blocks):
    T = qa.shape[0]
    tq = min(ROW_BLOCK, T)
    nq = T // tq
    NEG = -1e30
    n = len(blocks)
    n_pairs = FOX_HEADS // 2

    n_in = 3
    q_of, k_of = _causal_blocks(nq, by_query=True)
    n_tri = len(q_of)

    def body(qt_ref, kt_ref, *refs):
        q_ref, k_ref, v_ref = refs[:n_in]
        o_ref, lse_ref = refs[n_in + n:n_in + n + 2]
        m_sc, l_sc, acc_sc = refs[n_in + 2 * n + 2:n_in + 2 * n + 5]
        pr, t = pl.program_id(0), pl.program_id(1)
        qi, ki = qt_ref[t], kt_ref[t]
        g_start, g_forward, g_finish = _gather_steps(
            refs[n_in:n_in + n], refs[n_in + n + 2:n_in + 2 * n + 2], *refs[n_in + 2 * n + 5:])

        @pl.when(jnp.logical_and(pr == 0, t == 0))
        def _():
            g_start()

        @pl.when(jnp.logical_and(pr == n_pairs // 2, t == 0))
        def _():
            g_forward()

        @pl.when(ki == 0)
        def _():
            m_sc[...] = jnp.full_like(m_sc, NEG)
            l_sc[...] = jnp.zeros_like(l_sc)
            acc_sc[...] = jnp.zeros_like(acc_sc)

        lane = _iota((tq, LANES), 1)

        def block(masked):
            vv = v_ref[...]
            for hh in range(2):
                hs = slice(hh * LANES, (hh + 1) * LANES)
                s = _dot(q_ref[:, hs], k_ref[:, hs], 1, 1)
                tiles = [s[:, j * LANES:(j + 1) * LANES] for j in range(tq // LANES)]
                if masked:
                    row, col = _iota((tq, LANES), 0), _iota((tq, LANES), 1)
                    tiles = [jnp.where(row >= col + j * LANES, t, NEG) for j, t in enumerate(tiles)]
                m_old = m_sc[hh]
                top = jnp.broadcast_to(jnp.max(functools.reduce(jnp.maximum, tiles), axis=-1, keepdims=True),
                                       (tq, LANES))
                m_new = jnp.maximum(m_old, top)
                alpha = jnp.exp2(m_old - m_new)
                ps = [jnp.exp2(t - m_new) for t in tiles]
                l_sc[hh] = alpha * l_sc[hh] + functools.reduce(jnp.add, ps)
                m_sc[hh] = m_new
                p_b = jnp.concatenate([p.astype(BF16) for p in ps], axis=1)
                acc_sc[hh] = alpha * acc_sc[hh] + _dot(p_b, vv, 1, 0)

        @pl.when(ki < qi)
        def _():
            block(False)

        @pl.when(ki == qi)
        def _():
            block(True)
            l0 = jnp.sum(l_sc[0], axis=-1, keepdims=True)
            l1 = jnp.sum(l_sc[1], axis=-1, keepdims=True)
            o_ref[...] = jnp.where(lane < FOX_DH, acc_sc[0] * (1.0 / l0), acc_sc[1] * (1.0 / l1))
            lse_ref[:, :LANES] = m_sc[0] + jnp.log2(l0)
            lse_ref[:, LANES:] = m_sc[1] + jnp.log2(l1)

        @pl.when(jnp.logical_and(pr == n_pairs - 1, t == n_tri - 1))
        def _():
            g_finish()

    qmap = lambda p, t, qt, kt: (qt[t], p)
    kmap = lambda p, t, qt, kt: (kt[t], p)
    hbm = pl.BlockSpec(memory_space=pl.ANY)
    grid_spec = pltpu.PrefetchScalarGridSpec(
        num_scalar_prefetch=2, grid=(n_pairs, n_tri),
        in_specs=[pl.BlockSpec((tq, 2 * LANES), qmap), pl.BlockSpec((tq, 2 * LANES), kmap),
                  pl.BlockSpec((tq, LANES), kmap)] + [hbm] * n,
        out_specs=[pl.BlockSpec((tq, LANES), qmap), pl.BlockSpec((tq, 2 * LANES), qmap)] + [hbm] * n,
        scratch_shapes=[pltpu.VMEM((2, tq, LANES), F32)] * 3 + _gather_scratch(n))
    res = _pallas(
        body, name="fox_fwd", grid_spec=grid_spec,
        out_shape=[jax.ShapeDtypeStruct((T, FOX_W), F32), jax.ShapeDtypeStruct((T, 2 * FOX_W), F32)]
        + [jax.ShapeDtypeStruct((N_DEV,) + b.shape, b.dtype) for b in blocks],
        compiler_params=_params(dimension_semantics=("arbitrary",) * 2),
    )(q_of, k_of, qa, ka, vb, *blocks)
    return res[0], res[1], res[2:]


def _fox_bwd(qs, kn, vb, qa, ka, o, do, lse, hs):
    T = qs.shape[0]
    tq = min(ROW_BLOCK, T)
    nq = T // tq
    n = len(hs)
    n_pairs = FOX_HEADS // 2

    q_of, k_of = _causal_blocks(nq, by_query=False)
    n_tri = len(q_of)

    def body(qt_ref, kt_ref, *refs):
        q_ref, k_ref, v_ref, qa_ref, ka_ref, o_ref, do_ref, lse_ref = refs[:8]
        dq_ref, dk_ref, dv_ref, dcs_ref, drs_ref = refs[8 + n:13 + n]
        pr, t = pl.program_id(0), pl.program_id(1)
        qi, ki = qt_ref[t], kt_ref[t]
        x_start, x_finish = _chip_exchange_steps(refs[8:8 + n], refs[13 + n:13 + 2 * n], *refs[13 + 2 * n:])

        @pl.when(jnp.logical_and(pr == 0, t == 0))
        def _():
            x_start()
            drs_ref[...] = jnp.zeros_like(drs_ref)

        @pl.when(t == 0)
        def _():
            dq_ref[...] = jnp.zeros_like(dq_ref)

        @pl.when(qi == ki)
        def _():
            dk_ref[...] = jnp.zeros_like(dk_ref)
            dv_ref[...] = jnp.zeros_like(dv_ref)
            dcs_ref[...] = jnp.zeros_like(dcs_ref)

        def block(masked):
            lane = _iota((tq, LANES), 1)
            qv, kv, vv = q_ref[...], k_ref[...], v_ref[...]
            ov, dov = o_ref[...], do_ref[...]
            qrows = pl.ds(pl.multiple_of(qi * tq, tq), tq)
            dq_acc = jnp.zeros((tq, LANES), F32)
            dk_acc = jnp.zeros((tq, LANES), F32)
            dv_acc = jnp.zeros((tq, LANES), F32)
            dcs_acc = jnp.zeros((8, tq), F32)
            drs_acc = jnp.zeros((tq, LANES), F32)
            prod = dov * ov
            nt = tq // LANES
            for hh in range(2):
                in_head = (lane < FOX_DH) if hh == 0 else (lane >= FOX_DH)
                hs_ = slice(hh * LANES, (hh + 1) * LANES)
                zb = jnp.zeros_like(qv)
                qm = jnp.where(in_head, qv, zb)
                km = jnp.where(in_head, kv, zb)
                dom = jnp.where(in_head, dov, 0.0).astype(BF16)
                delta_b = jnp.broadcast_to(jnp.sum(jnp.where(in_head, prod, 0.0), axis=1, keepdims=True), (tq, LANES))
                lse_b = lse_ref[:, hs_]
                s = _dot(qa_ref[:, hs_], ka_ref[:, hs_], 1, 1)
                dp = _dot(dom, vv, 1, 1)
                p_tiles, ds_tiles, col_tiles = [], [], []
                row_part = jnp.zeros((tq, LANES), F32)
                for j in range(nt):
                    js = slice(j * LANES, (j + 1) * LANES)
                    p = jnp.exp2(s[:, js] - lse_b)
                    if masked:
                        p = jnp.where(_iota((tq, LANES), 0) >= _iota((tq, LANES), 1) + j * LANES, p, 0.0)
                    ds = p * (dp[:, js] - delta_b)
                    p_tiles.append(p.astype(BF16))
                    ds_tiles.append(ds.astype(BF16))
                    col_tiles.append(jnp.sum(ds, axis=0, keepdims=True))
                    row_part = row_part + ds
                p_b = jnp.concatenate(p_tiles, axis=1)
                ds_b = jnp.concatenate(ds_tiles, axis=1)
                dv_acc = dv_acc + _dot(p_b, dom, 0, 0)
                dq_acc = dq_acc + _dot(ds_b, km, 1, 0)
                dk_acc = dk_acc + _dot(ds_b, qm, 0, 0)
                dcs_acc = dcs_acc + jnp.where(_iota((8, tq), 0) == hh, jnp.concatenate(col_tiles, axis=1), 0.0)
                rowsum = jnp.sum(row_part, axis=1, keepdims=True)
                drs_acc = drs_acc + jnp.where(lane == 2 * pr + hh, rowsum, 0.0)
            drs_ref[qrows, :] += drs_acc
            dq_ref[qrows, :] += dq_acc
            dk_ref[...] += dk_acc
            dv_ref[...] += dv_acc
            dcs_ref[0] += dcs_acc

        @pl.when(qi > ki)
        def _():
            block(False)

        @pl.when(qi == ki)
        def _():
            block(True)

        @pl.when(jnp.logical_and(pr == n_pairs - 1, t == n_tri - 1))
        def _():
            x_finish()

    qmap = lambda p, t, qt, kt: (qt[t], p)
    kmap = lambda p, t, qt, kt: (kt[t], p)
    hbm = pl.BlockSpec(memory_space=pl.ANY)
    grid_spec = pltpu.PrefetchScalarGridSpec(
        num_scalar_prefetch=2, grid=(n_pairs, n_tri),
        in_specs=[pl.BlockSpec((tq, LANES), qmap), pl.BlockSpec((tq, LANES), kmap), pl.BlockSpec((tq, LANES), kmap),
                  pl.BlockSpec((tq, 2 * LANES), qmap), pl.BlockSpec((tq, 2 * LANES), kmap),
                  pl.BlockSpec((tq, LANES), qmap), pl.BlockSpec((tq, LANES), qmap), pl.BlockSpec((tq, 2 * LANES), qmap)]
        + [hbm] * n,
        out_specs=[pl.BlockSpec((T, LANES), lambda p, t, qt, kt: (0, p)), pl.BlockSpec((tq, LANES), kmap),
                   pl.BlockSpec((tq, LANES), kmap), pl.BlockSpec((1, 8, tq), lambda p, t, qt, kt: (p, 0, kt[t])),
                   pl.BlockSpec((T, LANES), lambda p, t, qt, kt: (0, 0))] + [hbm] * n,
        scratch_shapes=_chip_exchange_scratch(n))
    res = _pallas(
        body, name="fox_bwd", grid_spec=grid_spec,
        out_shape=[jax.ShapeDtypeStruct((T, FOX_W), F32)] * 3
        + [jax.ShapeDtypeStruct((n_pairs, 8, T), F32), jax.ShapeDtypeStruct((T, LANES), F32)]
        + _chip_exchange_shapes(hs),
        compiler_params=_params(dimension_semantics=("arbitrary",) * 2),
    )(q_of, k_of, qs, kn, vb, qa, ka, o, do, lse, *hs)
    res = list(res)
    return res[:5] + [res[5:]]


def _fox_post(z, dq, dk, dv, dcs, drs, bias, qg, kg, dz_buf):
    T = z.shape[0]
    tm = min(ROW_BLOCK, T)
    nb = T // tm

    def body(z_ref, dq_ref, dk_ref, dv_ref, dcs_ref, drs_ref, b_ref, qg_ref, kg_ref, _buf_ref, dz_ref, dqg_ref, dkg_ref,
             db_ref, carry_ref):
        @pl.when(pl.program_id(0) == 0)
        def _():
            carry_ref[...] = jnp.zeros_like(carry_ref)
            dqg_ref[...] = jnp.zeros_like(dqg_ref)
            dkg_ref[...] = jnp.zeros_like(dkg_ref)
            db_ref[...] = jnp.zeros_like(db_ref)

        ones = _head_ones()
        for src, g_ref, d_ref, dg_ref, scale in ((0, qg_ref, dq_ref, dqg_ref, FOX_DH ** -0.5), (1, kg_ref, dk_ref, dkg_ref, 1.0)):
            xv = z_ref[:, src * FOX_W:(src + 1) * FOX_W]
            ms = _split_dot(ones, xv * xv, 1, 0, mat_first=False) * (1.0 / FOX_DH)
            rstd = lax.rsqrt(ms + EPS)
            xh = xv * rstd
            dn = d_ref[...] * scale
            dg_ref[...] += jnp.sum(dn * xh, axis=0, keepdims=True)
            dxh = dn * g_ref[...]
            mean = _split_dot(ones, dxh * xh, 1, 0, mat_first=False) * (1.0 / FOX_DH)
            dz_ref[:, src * FOX_W:(src + 1) * FOX_W] = (rstd * (dxh - xh * mean)).astype(BF16)
        dz_ref[:, 2 * FOX_W:3 * FOX_W] = dv_ref[...].astype(BF16)
        row8 = _iota((8, tm), 0)
        dct = jnp.zeros((8, tm), F32)
        for h in range(FOX_HEADS):
            src_row = dcs_ref[h // 2][h % 2:h % 2 + 1, :]
            dct = dct + jnp.where(row8 == h, src_row, 0.0)
        dct = drs_ref[...].T[0:8] - dct
        r, c = _iota((tm, tm), 0), _iota((tm, tm), 1)
        upper_b = (r >= c).astype(BF16)
        rc = _split_dot(upper_b, dct, 1, 0, mat_first=False) + carry_ref[...]
        carry_ref[...] = rc[:, 0:1]
        full = jnp.concatenate([rc, jnp.zeros((LANES - 8, tm), F32)], axis=0)
        dlogf = full.T
        xf = z_ref[:, 3 * FOX_W:FOX_COLS] + b_ref[...]
        df = dlogf * (1.0 - _sigmoid(xf))
        dz_ref[:, 3 * FOX_W:FOX_COLS] = df.astype(BF16)
        dz_ref[:, FOX_COLS:] = jnp.zeros((tm, SEG - FOX_COLS), BF16)
        db_ref[...] += jnp.sum(df, axis=0, keepdims=True)

    rev = lambda i: (nb - 1 - i, 0)
    fix2 = lambda i: (0, 0)
    return _pallas(
        body, name="fox_post", grid=(nb,),
        in_specs=[pl.BlockSpec((tm, SEG), lambda i: (nb - 1 - i, 1)), pl.BlockSpec((tm, FOX_W), rev),
                  pl.BlockSpec((tm, FOX_W), rev),
                  pl.BlockSpec((tm, FOX_W), rev), pl.BlockSpec((FOX_HEADS // 2, 8, tm), lambda i: (0, 0, nb - 1 - i)),
                  pl.BlockSpec((tm, LANES), rev),
                  pl.BlockSpec((1, LANES), fix2), pl.BlockSpec((1, FOX_W), fix2), pl.BlockSpec((1, FOX_W), fix2),
                  pl.BlockSpec(memory_space=pl.ANY)],
        out_specs=[pl.BlockSpec((tm, SEG), lambda i: (nb - 1 - i, 1)), pl.BlockSpec((1, FOX_W), fix2),
                   pl.BlockSpec((1, FOX_W), fix2), pl.BlockSpec((1, LANES), fix2)],
        out_shape=[jax.ShapeDtypeStruct(dz_buf.shape, BF16), jax.ShapeDtypeStruct((1, FOX_W), F32),
                   jax.ShapeDtypeStruct((1, FOX_W), F32), jax.ShapeDtypeStruct((1, LANES), F32)],
        scratch_shapes=[pltpu.VMEM((8, 1), F32)],
        input_output_aliases={9: 0},
        compiler_params=_params(dimension_semantics=("arbitrary",)),
    )(z, dq, dk, dv, dcs, drs, bias, qg, kg, dz_buf)


def _local_step(x, p, tgt, sm, W, rest_chunks, core):
    lbl, og, fb = sm["hg_lb_logits"], sm["hg_onorm_g"], sm["fox_f_bias"]
    fbias = jnp.pad(fb, ((0, 0), (0, LANES - FOX_HEADS)))
    qg = jnp.tile(sm["fox_q_norm_g"], (1, FOX_HEADS))
    kg = jnp.tile(sm["fox_k_norm_g"], (1, FOX_HEADS))

    h = _rms_fwd(x, sm["norm_mix_g"], name="rms_mix")
    rest = dict(zip(BIG[1:], rest_chunks))
    first, second = ["w_ffn_gate"], ["w_ffn_up", "w_ffn_down"]
    third = [n for n in BIG[1:] if n not in first + second]
    z, got1 = _matmul(h, W["w_in"], tb=True, gather=[rest[n] for n in first], name="mm_z")
    o_raw, ya, states, got2 = _hgrn_fwd(z, lbl, og, [rest[n] for n in second])
    qs, kn, vb, qa, ka = _fox_prep(z, fbias, qg, kg)
    yb, lse, got3 = _fox_fwd(qa, ka, vb, [rest[n] for n in third])
    W = dict(W, **{n: _full_of_chunks(n, g)
                   for n, g in zip(first + second + third, list(got1) + list(got2) + list(got3))})
    merged, ua, ub = _merge_fwd(ya, yb, W["w_branch_a"], W["w_branch_b"], z)
    x1, hf = _matmul(merged, W["w_out"], add=x, norm_fwd=sm["norm_ffn_g"], name="mm_x1")
    a, b, act = _swiglu_fwd(hf, W["w_ffn_gate"], W["w_ffn_up"])
    x2 = _matmul(act, W["w_ffn_down"], add=x1, name="mm_x2")
    hp, dy, dsp, dpp, loss = _ple_loss(x2, p, sm["norm_ple_g"], W["w_ple_gate"], W["w_ple_proj"], tgt)

    G = {}
    G["w_ple_proj"] = _matmul(p, dpp, ta=True, out_dtype=BF16, name="mm_dw_ple_proj")
    G["w_ple_gate"] = _matmul(hp, dsp, ta=True, out_dtype=BF16, name="mm_dw_ple_gate")
    dx2, d_ple_g = _matmul(dsp, W["w_ple_gate"], tb=True, norm_bwd=(x2, sm["norm_ple_g"], dy), name="mm_dx2")
    G["w_ffn_down"] = _matmul(act, dx2, ta=True, out_dtype=BF16, name="mm_dw_ffn_down")
    da, db = _swiglu_bwd(dx2, W["w_ffn_down"], a, b)
    G["w_ffn_gate"] = _matmul(da, hf, ta=True, out_dtype=BF16, name="mm_dw_ffn_gate")
    G["w_ffn_up"] = _matmul(db, hf, ta=True, out_dtype=BF16, name="mm_dw_ffn_up")
    dhf = _matmul(da, W["w_ffn_gate"], name="mm_dhf_a")
    dx1, d_ffn_g = _matmul(db, W["w_ffn_up"], add=dhf, norm_bwd=(x1, sm["norm_ffn_g"], dx2), name="mm_dx1")
    G["w_out"] = _matmul(merged, dx1, ta=True, out_dtype=BF16, name="mm_dw_out")
    dua, dub, dz, dya, dyb = _merge_bwd(dx1, W["w_out"], W["w_branch_a"], W["w_branch_b"], ua, ub, z)
    G["w_branch_a"] = _matmul(ya, dua, ta=True, out_dtype=BF16, name="mm_dw_branch_a")
    G["w_branch_b"] = _matmul(yb, dub, ta=True, out_dtype=BF16, name="mm_dw_branch_b")
    hb_rest = _sibling_sums({n: G[n] for n in BIG[1:]}, core, tag="rest")
    dq, dk, dv, dcs, drs, got_rest = _fox_bwd(qs, kn, vb, qa, ka, yb, dyb, lse, hb_rest)
    dz, d_qg, d_kg, d_fb = _fox_post(z, dq, dk, dv, dcs, drs, fbias, qg, kg, dz)
    dz, d_lbl, d_og = _hgrn_bwd(z, o_raw, dya, states, lbl, og, dz)
    G["w_in"] = _matmul(dz, h, ta=True, out_dtype=BF16, name="mm_dw_in")
    hb_in = _sibling_sums({"w_in": G["w_in"]}, core, tag="w_in")
    (grad_x, d_mix_g), got_in = _matmul(dz, W["w_in"], exchange=hb_in,
                                        norm_bwd=(x, sm["norm_mix_g"], dx1), name="mm_dx")

    gs = {"norm_mix_g": d_mix_g, "hg_lb_logits": d_lbl, "hg_onorm_g": d_og, "fox_f_bias": d_fb[:, :FOX_HEADS],
          "fox_q_norm_g": d_qg.reshape(FOX_HEADS, FOX_DH).sum(0, keepdims=True),
          "fox_k_norm_g": d_kg.reshape(FOX_HEADS, FOX_DH).sum(0, keepdims=True),
          "norm_ffn_g": d_ffn_g, "norm_ple_g": d_ple_g}
    return loss, grad_x, gs, hb_in + hb_rest, list(got_in) + list(got_rest)


def _pack_rows(parts, total):
    buf = jnp.concatenate(parts, axis=-2)
    pad = total - buf.shape[-2]
    widths = [(0, 0)] * (buf.ndim - 2) + [(0, pad), (0, 0)]
    return jnp.pad(buf, widths)


def _chunk_of_shard(n, w):
    if n == "w_in":
        return jnp.pad(w, ((0, IN_SHARD_PAD - IN_SHARD), (0, 0)))
    if n in ("w_ffn_gate", "w_ffn_up", "w_ffn_down"):
        return jnp.pad(w, ((0, FF_SHARD_PAD - FF_SHARD), (0, 0)))
    return w


def _full_of_chunks(n, g):
    _, a, b = g.shape
    if n == "w_in":
        w = g[:, :IN_SHARD].reshape(IN_COLS, b)
        gap = jnp.zeros((SEG - FOX_LOGICAL, b), g.dtype)
        return jnp.concatenate([w[:HG_COLS + FOX_LOGICAL], gap, w[HG_COLS + FOX_LOGICAL:]], axis=0)
    if BIG_SHAPE[n][2] == 0:
        return g.reshape(N_DEV * a, b)
    return g.transpose(1, 0, 2).reshape(a, N_DEV * b)


def _chunks_of_full(n, g):
    if n == "w_in":
        w = jnp.concatenate([g[:HG_COLS + FOX_LOGICAL], g[2 * SEG:]], axis=0).reshape(N_DEV, IN_SHARD, g.shape[1])
        return jnp.pad(w, ((0, 0), (0, IN_SHARD_PAD - IN_SHARD), (0, 0)))
    if BIG_SHAPE[n][2] == 0:
        return g.reshape(N_DEV, g.shape[0] // N_DEV, g.shape[1])
    return g.reshape(g.shape[0], N_DEV, g.shape[1] // N_DEV).transpose(1, 0, 2)


def _pack_small(vals, loss_row=None):
    parts = [vals[n].reshape(SMALL_ROWS[n], -1) for n in SMALL]
    parts = [jnp.pad(v, ((0, 0), (0, LANES - v.shape[1]))) for v in parts]
    if loss_row is not None:
        parts.append(loss_row)
    return _pack_rows(parts, SMALL_TOTAL)


def _unpack_small(buf, like):
    out, r0 = {}, 0
    for n in SMALL:
        rows, size = SMALL_ROWS[n], like[n].size
        blk = buf[r0:r0 + rows]
        out[n] = (blk if size == rows * LANES else blk[:, :size]).reshape(like[n].shape)
        r0 += rows
    return out


def _place():
    return lax.axis_index("x"), lax.axis_index("y"), lax.axis_index("c")


def _gather_steps(x_refs, out_refs, send_sems, recv_sems, local_sems):
    n = len(x_refs)
    x, y, c = _place()
    me, sibling = (x, y, c), (x, y, 1 - c)
    chips = [(1 - x, y), (x, 1 - y), (1 - x, 1 - y)]

    def slot(i, px, py, pc):
        return out_refs[i].at[4 * px + 2 * py + pc]

    def copy(k, i, blk, to, own=False):
        return pltpu.make_async_remote_copy(
            src_ref=x_refs[i] if own else slot(i, *blk), dst_ref=slot(i, *blk),
            send_sem=send_sems.at[k, i], recv_sem=recv_sems.at[k, i], device_id=to, device_id_type=MESH)

    def mine():
        return [pltpu.make_async_copy(x_refs[i], slot(i, *me), local_sems.at[i]) for i in range(n)]

    def first():
        cps = [copy(0, i, me, sibling, own=True) for i in range(n)]
        return cps + [copy(1 + j, i, me, (*chip, c), own=True) for j, chip in enumerate(chips) for i in range(n)]

    def passed():
        return [copy(4 + j, i, (*chip, c), sibling) for j, chip in enumerate(chips) for i in range(n)]

    def start():
        for cp in mine() + first():
            cp.start()

    def forward():
        fws = passed()
        for j, chip in enumerate(chips):
            for i in range(n):
                copy(1 + j, i, (*chip, c), me).wait_recv()
                fws[j * n + i].start()

    def finish():
        for i in range(n):
            copy(0, i, sibling, me).wait_recv()
        for j, chip in enumerate(chips):
            for i in range(n):
                copy(4 + j, i, (*chip, 1 - c), me).wait_recv()
        for cp in first() + passed():
            cp.wait_send()
        for cp in mine():
            cp.wait()

    return start, forward, finish


def _gather_scratch(n):
    return [pltpu.SemaphoreType.DMA((7, n)), pltpu.SemaphoreType.DMA((7, n)), pltpu.SemaphoreType.DMA((n,))]


def _all_gather(blocks, *, name):
    n = len(blocks)

    def body(*refs):
        for step in _gather_steps(refs[:n], refs[n:2 * n], *refs[2 * n:]):
            step()

    hbm = pl.BlockSpec(memory_space=pl.ANY)
    return _pallas(
        body, name=name, out_shape=[jax.ShapeDtypeStruct((N_DEV,) + b.shape, b.dtype) for b in blocks],
        in_specs=[hbm] * n, out_specs=[hbm] * n, scratch_shapes=_gather_scratch(n),
    )(*blocks)


def _sibling_exchange(gs, *, name):
    n = len(gs)

    def body(*refs):
        g_refs, out_refs = refs[:n], refs[n:2 * n]
        send_sems, recv_sems = refs[2 * n:]
        x, y, c = _place()
        cps = [pltpu.make_async_remote_copy(
            src_ref=g_refs[i].at[:, pl.ds(1 - c, 1)], dst_ref=out_refs[i], send_sem=send_sems.at[i],
            recv_sem=recv_sems.at[i], device_id=(x, y, 1 - c), device_id_type=MESH) for i in range(n)]
        for cp in cps:
            cp.start()
        for cp in cps:
            cp.wait()

    hbm = pl.BlockSpec(memory_space=pl.ANY)
    return _pallas(
        body, name=name, out_shape=[jax.ShapeDtypeStruct((N_CHIP, 1) + g.shape[2:], g.dtype) for g in gs],
        in_specs=[hbm] * n, out_specs=[hbm] * n,
        scratch_shapes=[pltpu.SemaphoreType.DMA((n,)), pltpu.SemaphoreType.DMA((n,))],
    )(*gs)


def _chip_sum(g4, got, core, *, name):
    _, _, a, b = g4.shape

    def body(c_ref, g_ref, r_ref, h_ref):
        h_ref[0] = (g_ref[0, 0].astype(F32) + r_ref[0, 0].astype(F32)).astype(BF16)

    grid_spec = pltpu.PrefetchScalarGridSpec(
        num_scalar_prefetch=1, grid=(N_CHIP,),
        in_specs=[pl.BlockSpec((1, 1, a, b), lambda j, c: (j, c[0], 0, 0)),
                  pl.BlockSpec((1, 1, a, b), lambda j, c: (j, 0, 0, 0))],
        out_specs=[pl.BlockSpec((1, a, b), lambda j, c: (j, 0, 0))])
    return _pallas(
        body, name=name, grid_spec=grid_spec, out_shape=[jax.ShapeDtypeStruct((N_CHIP, a, b), BF16)],
        compiler_params=_params(dimension_semantics=("arbitrary",)),
    )(core, g4, got)[0]


def _chip_exchange(hs, *, name):
    n = len(hs)

    def body(*refs):
        for step in _chip_exchange_steps(refs[:n], refs[n:2 * n], *refs[2 * n:]):
            step()

    hbm = pl.BlockSpec(memory_space=pl.ANY)
    return _pallas(
        body, name=name, out_shape=_chip_exchange_shapes(hs), in_specs=[hbm] * n, out_specs=[hbm] * n,
        scratch_shapes=_chip_exchange_scratch(n),
    )(*hs)


def _chip_exchange_steps(h_refs, out_refs, send_sems, recv_sems):
    n = len(h_refs)
    x, y, c = _place()
    chips = [(1 - x, y), (x, 1 - y), (1 - x, 1 - y)]

    def copies():
        return [pltpu.make_async_remote_copy(
            src_ref=h_refs[i].at[2 * px + py], dst_ref=out_refs[i].at[k], send_sem=send_sems.at[k, i],
            recv_sem=recv_sems.at[k, i], device_id=(px, py, c), device_id_type=MESH)
            for k, (px, py) in enumerate(chips) for i in range(n)]

    def start():
        for cp in copies():
            cp.start()

    def finish():
        for cp in copies():
            cp.wait()

    return start, finish


def _chip_exchange_shapes(hs):
    return [jax.ShapeDtypeStruct((3,) + h.shape[1:], h.dtype) for h in hs]


def _chip_exchange_scratch(n):
    return [pltpu.SemaphoreType.DMA((3, n)), pltpu.SemaphoreType.DMA((3, n))]


def _sibling_sums(G, core, *, tag):
    g4 = []
    for n, g in G.items():
        gc = _chunks_of_full(n, g)
        g4.append(gc.reshape((N_CHIP, 2) + gc.shape[1:]))
    got = _sibling_exchange(g4, name="grads_to_sibling_" + tag)
    return [_chip_sum(g, r, core, name="chip_sum_" + n) for n, g, r in zip(G, g4, got)]


def _adam_math(w, g, m, v):
    m = ADAM_B1 * m + (1.0 - ADAM_B1) * g
    v = ADAM_B2 * v + (1.0 - ADAM_B2) * (g * g)
    m_hat = m / (1.0 - ADAM_B1 ** ADAM_STEP)
    v_hat = v / (1.0 - ADAM_B2 ** ADAM_STEP)
    delta = -ADAM_LR * (m_hat / (jnp.sqrt(v_hat) + ADAM_EPS) + ADAM_WD * w)
    return delta, m, v


def _adam_shard(hb, got, chip, w, m, v, *, name):
    _, r, c = w.shape
    _, a, cb = hb.shape
    assert cb == c and c % LANES == 0, (hb.shape, w.shape)
    tc = _pick(c, 2 * LANES)

    def body(j_ref, h_ref, r_ref, w_ref, m_ref, v_ref, g_ref, d_ref, nm_ref, nv_ref):
        parts = [h_ref[0], r_ref[0], r_ref[1], r_ref[2]]
        g = None
        for part in parts:
            part = part[:r].astype(F32)
            g = part if g is None else g + part
        d, nm, nv = _adam_math(w_ref[0], g, m_ref[0], v_ref[0])
        g_ref[0] = g
        d_ref[0] = d
        nm_ref[0] = nm
        nv_ref[0] = nv

    blk = pl.BlockSpec((1, r, tc), lambda i, j: (0, 0, i))
    grid_spec = pltpu.PrefetchScalarGridSpec(
        num_scalar_prefetch=1, grid=(c // tc,),
        in_specs=[pl.BlockSpec((1, a, tc), lambda i, j: (j[0], 0, i)),
                  pl.BlockSpec((3, a, tc), lambda i, j: (0, 0, i)), blk, blk, blk],
        out_specs=[blk] * 4)
    return _pallas(
        body, name=name, grid_spec=grid_spec, out_shape=[jax.ShapeDtypeStruct((1, r, c), F32)] * 4,
        compiler_params=_params(dimension_semantics=("arbitrary",)),
    )(chip, hb, got, w, m, v)


def _small_all_reduce_adam(gs, w, m, v):
    def body(g_ref, w_ref, m_ref, v_ref, sum_ref, d_ref, nm_ref, nv_ref, gather, send_sems, recv_sems):
        x, y, c = _place()
        my = 4 * x + 2 * y + c
        gather[my] = g_ref[...]
        cps = []
        for k in range(1, N_DEV):
            to = (x ^ (k >> 2), y ^ ((k >> 1) & 1), c ^ (k & 1))
            cps.append(pltpu.make_async_remote_copy(
                src_ref=g_ref, dst_ref=gather.at[my], send_sem=send_sems.at[k - 1], recv_sem=recv_sems.at[k - 1],
                device_id=to, device_id_type=MESH))
        for cp in cps:
            cp.start()
        for cp in cps:
            cp.wait()
        total = gather[0]
        for d in range(1, N_DEV):
            total = total + gather[d]
        dlt, nm, nv = _adam_math(w_ref[...], total, m_ref[...], v_ref[...])
        sum_ref[...] = total
        d_ref[...] = dlt
        nm_ref[...] = nm
        nv_ref[...] = nv

    vm = pl.BlockSpec(memory_space=pltpu.VMEM)
    return _pallas(
        body, name="small_all_reduce_adam", out_shape=[jax.ShapeDtypeStruct((SMALL_TOTAL, LANES), F32)] * 4,
        in_specs=[vm] * 4, out_specs=[vm] * 4,
        scratch_shapes=[pltpu.VMEM((N_DEV, SMALL_TOTAL, LANES), F32), pltpu.SemaphoreType.DMA((7,)),
                        pltpu.SemaphoreType.DMA((7,))],
            )(gs, w, m, v)


def kernel(x, p, norm_mix_g, w_in, hg_lb_logits, hg_onorm_g, fox_f_bias, fox_q_norm_g, fox_k_norm_g, w_branch_a, w_branch_b, w_out, norm_ffn_g, w_ffn_gate, w_ffn_up, w_ffn_down, norm_ple_g, w_ple_gate, w_ple_proj, loss_target, m_norm_mix_g, m_w_in, m_hg_lb_logits, m_hg_onorm_g, m_fox_f_bias, m_fox_q_norm_g, m_fox_k_norm_g, m_w_branch_a, m_w_branch_b, m_w_out, m_norm_ffn_g, m_w_ffn_gate, m_w_ffn_up, m_w_ffn_down, m_norm_ple_g, m_w_ple_gate, m_w_ple_proj, v_norm_mix_g, v_w_in, v_hg_lb_logits, v_hg_onorm_g, v_fox_f_bias, v_fox_q_norm_g, v_fox_k_norm_g, v_w_branch_a, v_w_branch_b, v_w_out, v_norm_ffn_g, v_w_ffn_gate, v_w_ffn_up, v_w_ffn_down, v_norm_ple_g, v_w_ple_gate, v_w_ple_proj):
    args = dict(locals())
    wts = {n: args[n] for n in BIG + SMALL}
    mom = {n: args["m_" + n] for n in BIG + SMALL}
    var = {n: args["v_" + n] for n in BIG + SMALL}
    for group in (wts, mom, var):
        for n in TRANSPOSED:
            group[n] = jnp.swapaxes(group[n], 1, 2)
    sm = {n: wts[n] for n in SMALL}

    xi, yi, ci = _place()
    core = jnp.reshape(ci, (1,)).astype(jnp.int32)
    chip = jnp.reshape(2 * xi + yi, (1,)).astype(jnp.int32)
    chunks = [_chunk_of_shard(n, wts[n][0].astype(BF16)) for n in BIG]
    assert BIG[0] == "w_in"
    w_in_full = _full_of_chunks("w_in", _all_gather(chunks[:1], name="w_in_all_gather")[0])

    loss_blk, grad_x, gs, hb, got = _local_step(
        x[0], p[0, 0], loss_target[0], sm, {"w_in": w_in_full}, chunks[1:], core)

    g_big, d_big, nm_big, nv_big = {}, {}, {}, {}
    for n, h, r in zip(BIG, hb, got):
        res = _adam_shard(h, r, chip, wts[n], mom[n], var[n], name="adam_" + n)
        if n in TRANSPOSED:
            res = [jnp.swapaxes(t, 1, 2) for t in res]
        g_big[n], d_big[n], nm_big[n], nv_big[n] = res

    s_sum, s_d, s_nm, s_nv = _small_all_reduce_adam(
        _pack_small(gs, loss_blk[0:1]), _pack_small(sm), _pack_small({n: mom[n] for n in SMALL}),
        _pack_small({n: var[n] for n in SMALL}))
    loss = s_sum[LOSS_ROW, 0]
    g_small, d_small, nm_small, nv_small = (_unpack_small(t, sm) for t in (s_sum, s_d, s_nm, s_nv))

    order = ["norm_mix_g", "w_in", "hg_lb_logits", "hg_onorm_g", "fox_f_bias", "fox_q_norm_g", "fox_k_norm_g",
             "w_branch_a", "w_branch_b", "w_out", "norm_ffn_g", "w_ffn_gate", "w_ffn_up", "w_ffn_down", "norm_ple_g",
             "w_ple_gate", "w_ple_proj"]
    outs = [loss, grad_x[None]]
    for big, small in ((g_big, g_small), (d_big, d_small), (nm_big, nm_small), (nv_big, nv_small)):
        outs += [big[n] if n in big else small[n] for n in order]
    return tuple(outs)
```

```python
import functools

import jax
import jax.numpy as jnp
from jax import lax
from jax.experimental import pallas as pl
from jax.experimental.pallas import tpu as pltpu

F32 = jnp.float32
BF16 = jnp.bfloat16

D_MODEL = 1024
PLE_DIM = 256
HG_HEADS = 4
HG_DK = 128
HG_CHUNK = 64
HG_SUB = 16
HG_W = HG_HEADS * HG_DK
FOX_HEADS = 8
FOX_DH = 64
FOX_W = FOX_HEADS * FOX_DH
D_FF = 2816
EPS = 1e-6
N_DEV = 8
N_CHIP = 4
LANES = 128
FOX_COLS = 3 * FOX_W + LANES
HG_COLS = 4 * HG_W
GATE_COLS = 2 * D_MODEL
IN_COLS = HG_COLS + 3 * FOX_W + FOX_HEADS + GATE_COLS
FOX_LOGICAL = 3 * FOX_W + FOX_HEADS
SEG = 2048
IN_PAD = 3 * SEG
IN_SHARD = IN_COLS // N_DEV
IN_SHARD_PAD = 768
FF_SHARD = D_FF // N_DEV
FF_SHARD_PAD = 384
FF_PAD = N_DEV * FF_SHARD_PAD
EXP_CLAMP = 80.0
LOG2E = 1.4426950408889634

ADAM_LR = 0.001
ADAM_B1 = 0.9
ADAM_B2 = 0.999
ADAM_EPS = 1e-08
ADAM_WD = 0.01
ADAM_STEP = 10

MESH = pl.DeviceIdType.MESH
VMEM_LIMIT = 56 * 1024 * 1024
ROW_BLOCK = 512

BIG = ["w_in", "w_branch_a", "w_branch_b", "w_out", "w_ffn_gate", "w_ffn_up", "w_ffn_down",
       "w_ple_gate", "w_ple_proj"]
TRANSPOSED = ("w_in", "w_ffn_gate", "w_ffn_up")
BIG_SHAPE = {
    "w_in": (IN_COLS, D_MODEL, 0), "w_branch_a": (HG_W, D_MODEL, 1), "w_branch_b": (FOX_W, D_MODEL, 1),
    "w_out": (D_MODEL, D_MODEL, 0), "w_ffn_gate": (D_FF, D_MODEL, 0), "w_ffn_up": (D_FF, D_MODEL, 0),
    "w_ffn_down": (D_FF, D_MODEL, 0), "w_ple_gate": (D_MODEL, D_MODEL, 0), "w_ple_proj": (PLE_DIM, D_MODEL, 1),
}

SMALL = ["norm_mix_g", "hg_lb_logits", "hg_onorm_g", "fox_f_bias", "fox_q_norm_g", "fox_k_norm_g",
         "norm_ffn_g", "norm_ple_g"]
SMALL_ROWS = {"norm_mix_g": 8, "hg_lb_logits": 8, "hg_onorm_g": 1, "fox_f_bias": 1, "fox_q_norm_g": 1,
              "fox_k_norm_g": 1, "norm_ffn_g": 8, "norm_ple_g": 8}
SMALL_TOTAL = 40
LOSS_ROW = 36


def _pallas(body, **kw):
    return pl.pallas_call(body, **kw)


def _params(**kw):
    return pltpu.CompilerParams(vmem_limit_bytes=VMEM_LIMIT, **kw)


def _pick(n, target):
    if n <= target:
        return n
    best = None
    for t in range(LANES, target + 1, LANES):
        if n % t == 0:
            best = t
    assert best is not None, (n, target)
    return best


def _dot(a, b, ca, cb):
    return lax.dot_general(a, b, (((ca,), (cb,)), ((), ())), preferred_element_type=F32)


def _split_dot(mat, x, ca, cb, terms=2, mat_first=True):
    acc = None
    rem = x
    for _ in range(terms):
        part = rem.astype(BF16)
        rem = rem - part.astype(F32)
        p = _dot(mat, part, ca, cb) if mat_first else _dot(part, mat, ca, cb)
        acc = p if acc is None else acc + p
    return acc


def _sigmoid(x):
    return 1.0 / (1.0 + jnp.exp(-x))


def _iota(shape, dim):
    return lax.broadcasted_iota(jnp.int32, shape, dim)


def _matmul(a, b, *, name, ta=False, tb=False, out_dtype=F32, add=None, exchange=None, gather=None,
            norm_fwd=None, norm_bwd=None):
    assert exchange is None or gather is None
    (K, M) = a.shape if ta else a.shape[::-1]
    (N, Kb) = b.shape if tb else b.shape[::-1]
    assert K == Kb, (a.shape, b.shape, ta, tb)
    if ta:
        tm, tn, tk = _pick(M, 2 * ROW_BLOCK), _pick(N, 2 * ROW_BLOCK), _pick(K, 4 * ROW_BLOCK)
    else:
        tm, tn, tk = _pick(M, 2 * ROW_BLOCK), _pick(N, 2048), _pick(K, 3072)
    if norm_bwd is not None:
        tm = _pick(M, ROW_BLOCK)
    nk = K // tk
    use_scratch = nk > 1 and out_dtype != F32
    if norm_fwd is not None or norm_bwd is not None:
        assert tn == N and not use_scratch and out_dtype == F32

    hs = list(exchange or gather or [])
    n_x = len(hs)
    grid = (M // tm, N // tn, nk)
    a_spec = pl.BlockSpec((tk, tm), lambda i, j, k: (k, i)) if ta else pl.BlockSpec((tm, tk), lambda i, j, k: (i, k))
    b_spec = pl.BlockSpec((tn, tk), lambda i, j, k: (j, k)) if tb else pl.BlockSpec((tk, tn), lambda i, j, k: (k, j))
    o_spec = pl.BlockSpec((tm, tn), lambda i, j, k: (i, j))
    row_vec = pl.BlockSpec((1, N), lambda i, j, k: (0, 0))
    hbm = pl.BlockSpec(memory_space=pl.ANY)
    extra_in = [(add, o_spec)] if add is not None else []
    extra_out = []
    if norm_fwd is not None:
        extra_in += [(norm_fwd, row_vec)]
        extra_out += [(jax.ShapeDtypeStruct((M, N), BF16), o_spec)]
    if norm_bwd is not None:
        extra_in += [(norm_bwd[0], o_spec), (norm_bwd[1], row_vec), (norm_bwd[2], o_spec)]
        extra_out += [(jax.ShapeDtypeStruct((1, N), F32), row_vec)]
    if gather is not None:
        ride_shapes, ride_scratch = [jax.ShapeDtypeStruct((N_DEV,) + h.shape, h.dtype) for h in hs], _gather_scratch(n_x)
    else:
        ride_shapes, ride_scratch = _chip_exchange_shapes(hs), (_chip_exchange_scratch(n_x) if n_x else [])
    n_ex_in, n_ex_out = len(extra_in), len(extra_out)

    def body(*refs):
        refs = list(refs)
        a_ref, b_ref = refs[:2]
        ex_in = refs[2:2 + n_ex_in]
        ride_in = refs[2 + n_ex_in:2 + n_ex_in + n_x]
        base = 2 + n_ex_in + n_x
        o_ref = refs[base]
        ex_out = refs[base + 1:base + 1 + n_ex_out]
        ride_out = refs[base + 1 + n_ex_out:base + 1 + n_ex_out + n_x]
        scratch = refs[base + 1 + n_ex_out + n_x:]
        at = [pl.program_id(d) for d in range(3)]
        k = at[2]
        if n_x:
            steps = _gather_steps if gather is not None else _chip_exchange_steps
            ride = steps(ride_in, ride_out, *scratch[-len(ride_scratch):])

            @pl.when(jnp.logical_and(at[0] == 0, jnp.logical_and(at[1] == 0, at[2] == 0)))
            def _():
                ride[0]()
        p = _dot(a_ref[...].astype(BF16), b_ref[...].astype(BF16), 0 if ta else 1, 1 if tb else 0)

        def finish(r):
            ins = list(ex_in)
            outs = list(ex_out)
            if add is not None:
                r = r + ins.pop(0)[...].astype(F32)
            if norm_fwd is not None:
                g_ref = ins.pop(0)
                rstd = lax.rsqrt(jnp.mean(r * r, axis=-1, keepdims=True) + EPS)
                outs.pop(0)[...] = (r * rstd * g_ref[...]).astype(BF16)
            if norm_bwd is not None:
                x_ref, g_ref, dres_ref = ins.pop(0), ins.pop(0), ins.pop(0)
                dg_ref = outs.pop(0)
                xv = x_ref[...]
                rstd = lax.rsqrt(jnp.mean(xv * xv, axis=-1, keepdims=True) + EPS)
                xh = xv * rstd
                part = jnp.sum(r * xh, axis=0, keepdims=True)

                @pl.when(at[0] == 0)
                def _():
                    dg_ref[...] = part

                @pl.when(at[0] > 0)
                def _():
                    dg_ref[...] += part

                dxh = r * g_ref[...]
                r = rstd * (dxh - xh * jnp.mean(dxh * xh, axis=-1, keepdims=True)) + dres_ref[...]
            o_ref[...] = r.astype(out_dtype)

        if nk == 1:
            finish(p)
        elif not use_scratch:
            @pl.when(k == 0)
            def _():
                o_ref[...] = p

            @pl.when(jnp.logical_and(k > 0, k < nk - 1))
            def _():
                o_ref[...] += p

            @pl.when(k == nk - 1)
            def _():
                finish(o_ref[...] + p)
        else:
            acc_ref = scratch[0]

            @pl.when(k == 0)
            def _():
                acc_ref[...] = p

            @pl.when(k > 0)
            def _():
                acc_ref[...] += p

            @pl.when(k == nk - 1)
            def _():
                finish(acc_ref[...])

        if n_x:
            @pl.when(jnp.logical_and(at[0] == grid[0] - 1, jnp.logical_and(at[1] == grid[1] - 1, at[2] == nk - 1)))
            def _():
                for step in ride[1:]:
                    step()

    res = _pallas(
        body, name=name, grid=grid,
        in_specs=[a_spec, b_spec] + [s for _, s in extra_in] + [hbm] * n_x,
        out_specs=[o_spec] + [s for _, s in extra_out] + [hbm] * n_x,
        out_shape=[jax.ShapeDtypeStruct((M, N), out_dtype)] + [s for s, _ in extra_out] + ride_shapes,
        scratch_shapes=([pltpu.VMEM((tm, tn), F32)] if use_scratch else []) + ride_scratch,
        compiler_params=_params(dimension_semantics=("arbitrary",) * 3),
    )(a, b, *[v for v, _ in extra_in], *hs)
    res = list(res)
    main = res[0] if n_ex_out == 0 else tuple(res[:1 + n_ex_out])
    return (main, res[1 + n_ex_out:]) if n_x else main


def _row_map(nb, reverse, seg):
    if reverse:
        return lambda i: (nb - 1 - i, seg)
    return lambda i: (i, seg)


def _row_call(body, *, name, T, ins, outs, acc_outs=(), tm=ROW_BLOCK, reverse=False):
    tm = min(tm, T)
    nb = T // tm
    in_specs, args = [], []
    for arr, how in ins:
        args.append(arr)
        if how is True:
            in_specs.append(pl.BlockSpec((tm, arr.shape[1]), _row_map(nb, reverse, 0)))
        elif how is False:
            in_specs.append(pl.BlockSpec(arr.shape, lambda i, _n=arr.ndim: (0,) * _n))
        else:
            in_specs.append(pl.BlockSpec((tm, SEG), _row_map(nb, reverse, how[0])))
    out_specs, out_shape = [], []
    for o in outs:
        c, dt = o[0], o[1]
        total, seg = o[2] if len(o) > 2 else (c, 0)
        out_specs.append(pl.BlockSpec((tm, c), _row_map(nb, reverse, seg)))
        out_shape.append(jax.ShapeDtypeStruct((T, total), dt))
    for shp, dt in acc_outs:
        out_specs.append(pl.BlockSpec(shp, lambda i, _n=len(shp): (0,) * _n))
        out_shape.append(jax.ShapeDtypeStruct(shp, dt))
    return _pallas(body, name=name, grid=(nb,), in_specs=in_specs, out_specs=out_specs, out_shape=out_shape,
                   compiler_params=_params(dimension_semantics=("arbitrary",)))(*args)


def _rms_fwd(x, g, *, name):
    T = x.shape[0]

    def body(x_ref, g_ref, h_ref):
        xv = x_ref[...]
        rstd = lax.rsqrt(jnp.mean(xv * xv, axis=-1, keepdims=True) + EPS)
        h_ref[...] = (xv * rstd * g_ref[...]).astype(BF16)

    return _row_call(body, name=name, T=T, ins=[(x, True), (g, False)], outs=[(D_MODEL, BF16)])[0]


def _rms_bwd(x, g, dh, dres, *, name):
    T = x.shape[0]

    def body(x_ref, g_ref, dh_ref, dres_ref, dx_ref, dg_ref):
        xv = x_ref[...]
        rstd = lax.rsqrt(jnp.mean(xv * xv, axis=-1, keepdims=True) + EPS)
        xh = xv * rstd
        dhv = dh_ref[...]
        part = jnp.sum(dhv * xh, axis=0, keepdims=True)

        @pl.when(pl.program_id(0) == 0)
        def _():
            dg_ref[...] = part

        @pl.when(pl.program_id(0) > 0)
        def _():
            dg_ref[...] += part

        dxh = dhv * g_ref[...]
        dx_ref[...] = rstd * (dxh - xh * jnp.mean(dxh * xh, axis=-1, keepdims=True)) + dres_ref[...]

    return _row_call(body, name=name, T=T, ins=[(x, True), (g, False), (dh, True), (dres, True)],
                     outs=[(D_MODEL, F32)], acc_outs=[((1, D_MODEL), F32)])


def _merge_fwd(ya, yb, wa, wb, zg):
    def body(ya_ref, yb_ref, wa_ref, wb_ref, zg_ref, m_ref, ua_ref, ub_ref):
        ua = _dot(ya_ref[...].astype(BF16), wa_ref[...], 1, 0)
        ub = _dot(yb_ref[...].astype(BF16), wb_ref[...], 1, 0)
        ga = _sigmoid(zg_ref[:, :D_MODEL])
        gb = _sigmoid(zg_ref[:, D_MODEL:])
        m_ref[...] = (ga * ua + gb * ub).astype(BF16)
        ua_ref[...] = ua.astype(BF16)
        ub_ref[...] = ub.astype(BF16)

    return _row_call(body, name="merge_fwd", T=ya.shape[0],
                     ins=[(ya, True), (yb, True), (wa, False), (wb, False), (zg, (2,))],
                     outs=[(D_MODEL, BF16)] * 3)


def _merge_bwd(dx1, w_out, wa, wb, ua, ub, zg):
    def body(dx_ref, wo_ref, wa_ref, wb_ref, ua_ref, ub_ref, zg_ref, dua_ref, dub_ref, dzg_ref, dya_ref, dyb_ref):
        dmv = _dot(dx_ref[...].astype(BF16), wo_ref[...], 1, 1)
        ga = _sigmoid(zg_ref[:, :D_MODEL])
        gb = _sigmoid(zg_ref[:, D_MODEL:])
        dua = (dmv * ga).astype(BF16)
        dub = (dmv * gb).astype(BF16)
        dua_ref[...] = dua
        dub_ref[...] = dub
        dzg_ref[:, :D_MODEL] = (dmv * ua_ref[...].astype(F32) * ga * (1.0 - ga)).astype(BF16)
        dzg_ref[:, D_MODEL:] = (dmv * ub_ref[...].astype(F32) * gb * (1.0 - gb)).astype(BF16)
        dya_ref[...] = _dot(dua, wa_ref[...], 1, 1)
        dyb_ref[...] = _dot(dub, wb_ref[...], 1, 1)

    return _row_call(body, name="merge_bwd", T=dx1.shape[0],
                     ins=[(dx1, True), (w_out, False), (wa, False), (wb, False), (ua, True), (ub, True), (zg, (2,))],
                     outs=[(D_MODEL, BF16), (D_MODEL, BF16), (SEG, BF16, (IN_PAD, 2)), (HG_W, F32), (FOX_W, F32)])


def _swiglu_fwd(hf, w_gate, w_up):
    T, D = hf.shape
    F = w_gate.shape[0]
    tm, tn = _pick(T, 2 * ROW_BLOCK), _pick(F, 768)

    def body(h_ref, wg_ref, wu_ref, a_ref, b_ref, o_ref):
        hv = h_ref[...]
        a_b = _dot(hv, wg_ref[...], 1, 1).astype(BF16)
        b_b = _dot(hv, wu_ref[...], 1, 1).astype(BF16)
        a_ref[...] = a_b
        b_ref[...] = b_b
        av = a_b.astype(F32)
        o_ref[...] = (av * _sigmoid(av) * b_b.astype(F32)).astype(BF16)

    tile = pl.BlockSpec((tm, tn), lambda i, j: (i, j))
    wcol = pl.BlockSpec((tn, D), lambda i, j: (j, 0))
    return _pallas(
        body, name="swiglu_fwd", grid=(T // tm, F // tn),
        in_specs=[pl.BlockSpec((tm, D), lambda i, j: (i, 0)), wcol, wcol],
        out_specs=[tile] * 3, out_shape=[jax.ShapeDtypeStruct((T, F), BF16)] * 3,
        compiler_params=_params(dimension_semantics=("arbitrary",) * 2),
    )(hf, w_gate, w_up)


def _swiglu_bwd(dx, w_down, a, b):
    T, D = dx.shape
    F = w_down.shape[0]
    tm, tn = _pick(T, 2 * ROW_BLOCK), _pick(F, 768)

    def body(dx_ref, w_ref, a_ref, b_ref, da_ref, db_ref):
        dact = _dot(dx_ref[...].astype(BF16), w_ref[...], 1, 1)
        av = a_ref[...].astype(F32)
        bv = b_ref[...].astype(F32)
        sg = _sigmoid(av)
        da_ref[...] = (dact * bv * sg * (1.0 + av * (1.0 - sg))).astype(BF16)
        db_ref[...] = (dact * av * sg).astype(BF16)

    tile = pl.BlockSpec((tm, tn), lambda i, j: (i, j))
    return _pallas(
        body, name="swiglu_bwd", grid=(T // tm, F // tn),
        in_specs=[pl.BlockSpec((tm, D), lambda i, j: (i, 0)), pl.BlockSpec((tn, D), lambda i, j: (j, 0)), tile, tile],
        out_specs=[tile, tile], out_shape=[jax.ShapeDtypeStruct((T, F), BF16)] * 2,
        compiler_params=_params(dimension_semantics=("arbitrary",) * 2),
    )(dx, w_down, a, b)


def _ple_loss(x2, p, g, w_gate, w_proj, tgt):
    def body(x_ref, p_ref, g_ref, wg_ref, wp_ref, t_ref, hp_ref, dy_ref, dsp_ref, dpp_ref, loss_ref):
        xv = x_ref[...]
        rstd = lax.rsqrt(jnp.mean(xv * xv, axis=-1, keepdims=True) + EPS)
        hp = (xv * rstd * g_ref[...]).astype(BF16)
        hp_ref[...] = hp
        gp = _sigmoid(_dot(hp, wg_ref[...], 1, 0))
        ppv = _dot(p_ref[...].astype(BF16), wp_ref[...], 1, 0)
        err = xv + gp * ppv - t_ref[...]
        part = 0.5 * jnp.sum(jnp.mean(err * err, axis=-1, keepdims=True), axis=0, keepdims=True)
        part = jnp.broadcast_to(part, loss_ref.shape)

        @pl.when(pl.program_id(0) == 0)
        def _():
            loss_ref[...] = part

        @pl.when(pl.program_id(0) > 0)
        def _():
            loss_ref[...] += part

        dy = err * (1.0 / D_MODEL)
        dy_ref[...] = dy
        dsp_ref[...] = (dy * ppv * gp * (1.0 - gp)).astype(BF16)
        dpp_ref[...] = (dy * gp).astype(BF16)

    return _row_call(body, name="ple_loss", T=x2.shape[0],
                     ins=[(x2, True), (p, True), (g, False), (w_gate, False), (w_proj, False), (tgt, True)],
                     outs=[(D_MODEL, BF16), (D_MODEL, F32), (D_MODEL, BF16), (D_MODEL, BF16)],
                     acc_outs=[((8, LANES), F32)])


def _hg_consts():
    C = HG_CHUNK
    r, c = _iota((C, C), 0), _iota((C, C), 1)
    tri = (c <= r)
    same = (r // HG_SUB) == (c // HG_SUB)
    return tri, (tri & same)


def _hg_chunk_fwd(q, f, lb, tri_b, sub_b):
    sgq = _sigmoid(q)
    qt = q * sgq
    sg = _sigmoid(f)
    fg = lb + (1.0 - lb) * sg
    kf = (1.0 - lb) * (1.0 - sg)
    logf = jnp.log(fg)
    b = _split_dot(tri_b, logf, 1, 0)
    w = _split_dot(sub_b, logf, 1, 0)
    return sgq, qt, sg, fg, kf, b, w


def _hg_scores(qs_b, kf, b, row):
    C, S = HG_CHUNK, HG_SUB
    parts, ks = [], []
    for blk in range(C // S):
        ref = jnp.zeros_like(b[0:1]) if blk == 0 else b[blk * S - 1:blk * S]
        e = jnp.exp(jnp.minimum(ref - b, EXP_CLAMP))
        e = jnp.where(row < (blk + 1) * S, e, 0.0)
        k_b = (kf * e).astype(BF16)
        ks.append((e, k_b))
        parts.append(_dot(qs_b[blk * S:(blk + 1) * S], k_b, 1, 1))
    return jnp.concatenate(parts, axis=0), ks


def _hgrn_fwd(z, lb_logits, gain, blocks):
    T = z.shape[0]
    RB = min(ROW_BLOCK, T)
    nb, cpb = T // RB, RB // HG_CHUNK
    C, DK = HG_CHUNK, HG_DK
    n = len(blocks)

    def body(*refs):
        z_ref, lg_ref, g_ref = refs[:3]
        o_ref, y_ref, st_ref = refs[3 + n:6 + n]
        s_ref = refs[6 + 2 * n]
        g_start, g_forward, g_finish = _gather_steps(refs[3:3 + n], refs[6 + n:6 + 2 * n], *refs[7 + 2 * n:])

        @pl.when(pl.program_id(0) == 0)
        def _():
            s_ref[...] = jnp.zeros_like(s_ref)
            g_start()

        lg = lg_ref[...]
        lb_all = 1.0 / (1.0 + jnp.exp(lg[1:2] - lg[0:1]))
        gain_v = g_ref[...]
        tri, sub = _hg_consts()
        tri_b, sub_b = tri.astype(BF16), sub.astype(BF16)
        row = _iota((C, DK), 0)

        def chunk(ci, carry):
            r0 = pl.multiple_of(ci * C, C)
            rows = pl.ds(r0, C)
            _, qt, _, _, kf_all, b_all, w = _hg_chunk_fwd(z_ref[rows, 0:HG_W], z_ref[rows, HG_W:2 * HG_W], lb_all,
                                                          tri_b, sub_b)
            qs_all = (qt * jnp.exp(w)).astype(BF16)
            qd_all = (qt * jnp.exp(b_all)).astype(BF16)
            bl_all = b_all[C - 1:C]
            kd_all = (kf_all * jnp.exp(bl_all - b_all)).astype(BF16)
            ebl_all = jnp.exp(bl_all)
            v_all = z_ref[rows, 2 * HG_W:3 * HG_W].astype(BF16)
            g_all = z_ref[rows, 3 * HG_W:4 * HG_W]
            gate_all = g_all * _sigmoid(g_all)
            for h in range(HG_HEADS):
                cs = slice(h * DK, (h + 1) * DK)
                st = s_ref[h]
                st_ref[pl.ds(pl.multiple_of((ci * HG_HEADS + h) * DK, DK), DK), :] = st
                v_b = v_all[:, cs]
                a, _ = _hg_scores(qs_all[:, cs], kf_all[:, cs], b_all[:, cs], row)
                a = jnp.where(tri, a, 0.0)
                o = _dot(qd_all[:, cs], st.astype(BF16), 1, 1) + _dot(a.astype(BF16), v_b, 1, 0)
                s_ref[h] = st * ebl_all[:, cs] + _dot(v_b, kd_all[:, cs], 0, 0)
                o_ref[rows, cs] = o
                rstd = lax.rsqrt(jnp.mean(o * o, axis=-1, keepdims=True) + EPS)
                y_ref[rows, cs] = (o * rstd * gain_v * gate_all[:, cs]).astype(BF16)
            return carry

        lax.fori_loop(0, cpb, chunk, 0, unroll=2)

        @pl.when(pl.program_id(0) == nb - 1)
        def _():
            g_forward()
            g_finish()

    hbm = pl.BlockSpec(memory_space=pl.ANY)
    res = _pallas(
        body, name="hgrn_fwd", grid=(nb,),
        in_specs=[pl.BlockSpec((RB, HG_COLS), lambda i: (i, 0)), pl.BlockSpec((2, HG_W), lambda i: (0, 0)),
                  pl.BlockSpec((1, DK), lambda i: (0, 0))] + [hbm] * n,
        out_specs=[pl.BlockSpec((RB, HG_W), lambda i: (i, 0)), pl.BlockSpec((RB, HG_W), lambda i: (i, 0)),
                   pl.BlockSpec((cpb * HG_HEADS * DK, DK), lambda i: (i, 0))] + [hbm] * n,
        out_shape=[jax.ShapeDtypeStruct((T, HG_W), F32), jax.ShapeDtypeStruct((T, HG_W), BF16),
                   jax.ShapeDtypeStruct((T // C * HG_HEADS * DK, DK), F32)]
        + [jax.ShapeDtypeStruct((N_DEV,) + b.shape, b.dtype) for b in blocks],
        scratch_shapes=[pltpu.VMEM((HG_HEADS, DK, DK), F32)] + _gather_scratch(n),
        compiler_params=_params(dimension_semantics=("arbitrary",)),
    )(z, lb_logits, gain, *blocks)
    return res[0], res[1], res[2], res[3:]


def _hgrn_bwd(z, o_raw, dy, states, lb_logits, gain, dz_buf):
    T = z.shape[0]
    RB = min(ROW_BLOCK, T)
    nb, cpb = T // RB, RB // HG_CHUNK
    C, DK, S = HG_CHUNK, HG_DK, HG_SUB

    def body(z_ref, o_ref, dy_ref, st_ref, lg_ref, g_ref, _buf_ref, dz_ref, dlg_ref, dg_ref, ds_ref, dlb_ref):
        step = pl.program_id(0)

        @pl.when(step == 0)
        def _():
            ds_ref[...] = jnp.zeros_like(ds_ref)
            dlb_ref[...] = jnp.zeros_like(dlb_ref)
            dg_ref[...] = jnp.zeros_like(dg_ref)

        lg = lg_ref[...]
        lb_all = 1.0 / (1.0 + jnp.exp(lg[1:2] - lg[0:1]))
        gain_v = g_ref[...]
        tri, sub = _hg_consts()
        tri_b, sub_b = tri.astype(BF16), sub.astype(BF16)
        row = _iota((C, DK), 0)

        def chunk(cj, carry):
            ci = cpb - 1 - cj
            r0 = pl.multiple_of(ci * C, C)
            rows = pl.ds(r0, C)
            q_all = z_ref[rows, 0:HG_W]
            g_all = z_ref[rows, 3 * HG_W:4 * HG_W]
            sgq_all, qt_all, sg_all, fg_all, kf_all, b_all, w_all = _hg_chunk_fwd(
                q_all, z_ref[rows, HG_W:2 * HG_W], lb_all, tri_b, sub_b)
            ew_all = jnp.exp(w_all)
            eb_all = jnp.exp(b_all)
            bl_all = b_all[C - 1:C]
            ebl_all = jnp.exp(bl_all)
            ekd_all = jnp.exp(bl_all - b_all)
            qs_all = (qt_all * ew_all).astype(BF16)
            qd_all = (qt_all * eb_all).astype(BF16)
            kd_all = (kf_all * ekd_all).astype(BF16)
            v_all = z_ref[rows, 2 * HG_W:3 * HG_W].astype(BF16)
            sgg_all = _sigmoid(g_all)
            t1_all = dy_ref[rows, :] * (g_all * sgg_all)
            db_heads, dqt_heads, dkf_heads, dv_heads, n_heads = [], [], [], [], []
            for h in range(HG_HEADS):
                cs = slice(h * DK, (h + 1) * DK)
                kf, b, ew, eb, ebl, ekd = kf_all[:, cs], b_all[:, cs], ew_all[:, cs], eb_all[:, cs], ebl_all[:, cs], \
                    ekd_all[:, cs]
                qs_b, qd_b, kd_b, v_b = qs_all[:, cs], qd_all[:, cs], kd_all[:, cs], v_all[:, cs]
                st = st_ref[pl.ds(pl.multiple_of((ci * HG_HEADS + h) * DK, DK), DK), :]
                dst = ds_ref[h]
                o = o_ref[rows, cs]
                rstd = lax.rsqrt(jnp.mean(o * o, axis=-1, keepdims=True) + EPS)
                n = o * rstd
                n_heads.append(n)
                t1 = t1_all[:, cs]
                dg_ref[...] += jnp.sum(t1 * n, axis=0, keepdims=True)
                dn = t1 * gain_v
                do = rstd * (dn - n * jnp.mean(dn * n, axis=-1, keepdims=True))
                do_b = do.astype(BF16)
                a, ks = _hg_scores(qs_b, kf, b, row)
                a = jnp.where(tri, a, 0.0)
                dst_b = dst.astype(BF16)
                dqd = _dot(do_b, st.astype(BF16), 1, 0)
                da = jnp.where(tri, _dot(do_b, v_b, 1, 1), 0.0)
                dv = _dot(a.astype(BF16), do_b, 0, 0) + _dot(kd_b, dst_b, 1, 1)
                dkd = _dot(v_b, dst_b, 1, 0)
                ds_ref[h] = dst * ebl + _dot(do_b, qd_b, 0, 0)
                dkd_kd = dkd * kd_b.astype(F32)
                dbl = ebl * jnp.sum(dst * st, axis=0, keepdims=True) + jnp.sum(dkd_kd, axis=0, keepdims=True)
                da_b = da.astype(BF16)
                dqs_parts = []
                dk_in = jnp.zeros((C, DK), F32)
                db_k = jnp.zeros((C, DK), F32)
                for blk in range(C // S):
                    e, k_b = ks[blk]
                    da_blk = da_b[blk * S:(blk + 1) * S]
                    dqs_parts.append(_dot(da_blk, k_b, 1, 0))
                    dks = _dot(da_blk, qs_b[blk * S:(blk + 1) * S], 0, 0)
                    dk_in = dk_in + dks * e
                    db_k = db_k + dks * k_b.astype(F32)
                dqs = jnp.concatenate(dqs_parts, axis=0)
                db = qs_b.astype(F32) * dqs - db_k + dqd * qd_b.astype(F32) - dkd_kd
                db_heads.append(db + jnp.where(row == C - 1, dbl, 0.0))
                dqt_heads.append(dqs * ew + dqd * eb)
                dkf_heads.append(dk_in + dkd * ekd)
                dv_heads.append(dv)
            dlogf = _split_dot(tri_b, jnp.concatenate(db_heads, axis=1), 0, 0)
            dfg = dlogf / fg_all - jnp.concatenate(dkf_heads, axis=1)
            dlb_ref[...] += jnp.sum(dfg * (1.0 - sg_all), axis=0, keepdims=True)
            dqt = jnp.concatenate(dqt_heads, axis=1)
            n_all = jnp.concatenate(n_heads, axis=1)
            gain_all = jnp.concatenate([gain_v] * HG_HEADS, axis=1)
            dz_ref[rows, 0:HG_W] = (dqt * sgq_all * (1.0 + q_all * (1.0 - sgq_all))).astype(BF16)
            dz_ref[rows, HG_W:2 * HG_W] = (dfg * (1.0 - lb_all) * sg_all * (1.0 - sg_all)).astype(BF16)
            dz_ref[rows, 2 * HG_W:3 * HG_W] = jnp.concatenate(dv_heads, axis=1).astype(BF16)
            dz_ref[rows, 3 * HG_W:4 * HG_W] = (dy_ref[rows, :] * n_all * gain_all * sgg_all
                                               * (1.0 + g_all * (1.0 - sgg_all))).astype(BF16)
            return carry

        lax.fori_loop(0, cpb, chunk, 0, unroll=2)

        @pl.when(step == nb - 1)
        def _():
            d0 = dlb_ref[...] * lb_all * (1.0 - lb_all)
            dlg_ref[0:1, :] = d0
            dlg_ref[1:2, :] = -d0

    rev = lambda i: (nb - 1 - i, 0)
    fix = lambda i: (0, 0)
    return _pallas(
        body, name="hgrn_bwd", grid=(nb,),
        in_specs=[pl.BlockSpec((RB, HG_COLS), rev), pl.BlockSpec((RB, HG_W), rev), pl.BlockSpec((RB, HG_W), rev),
                  pl.BlockSpec((cpb * HG_HEADS * DK, DK), rev), pl.BlockSpec((2, HG_W), fix),
                  pl.BlockSpec((1, DK), fix), pl.BlockSpec(memory_space=pl.ANY)],
        out_specs=[pl.BlockSpec((RB, HG_COLS), rev), pl.BlockSpec((2, HG_W), fix), pl.BlockSpec((1, DK), fix)],
        out_shape=[jax.ShapeDtypeStruct(dz_buf.shape, BF16), jax.ShapeDtypeStruct((2, HG_W), F32),
                   jax.ShapeDtypeStruct((1, DK), F32)],
        scratch_shapes=[pltpu.VMEM((HG_HEADS, DK, DK), F32), pltpu.VMEM((1, HG_W), F32)],
        input_output_aliases={6: 0},
        compiler_params=_params(dimension_semantics=("arbitrary",)),
    )(z, o_raw, dy, states, lb_logits, gain, dz_buf)


def _head_ones():
    r, c = _iota((FOX_W, FOX_W), 0), _iota((FOX_W, FOX_W), 1)
    return ((r // FOX_DH) == (c // FOX_DH)).astype(BF16)


def _causal_blocks(nq, *, by_query):
    if by_query:
        pairs = [(q, k) for q in range(nq) for k in range(q + 1)]
    else:
        pairs = [(q, k) for k in range(nq) for q in range(k, nq)]
    return (jnp.asarray([q for q, _ in pairs], jnp.int32), jnp.asarray([k for _, k in pairs], jnp.int32))


def _log_sigmoid(x):
    return jnp.minimum(x, 0.0) - jnp.log(1.0 + jnp.exp(-jnp.abs(x)))


def _fox_prep(z, bias, qg, kg):
    T = z.shape[0]
    tm = min(ROW_BLOCK, T)
    nb = T // tm

    def body(z_ref, b_ref, qg_ref, kg_ref, q_ref, k_ref, v_ref, qa_ref, ka_ref, carry_ref):
        @pl.when(pl.program_id(0) == 0)
        def _():
            carry_ref[...] = jnp.zeros_like(carry_ref)

        ones = _head_ones()
        normed = []
        for src, g_ref in ((0, qg_ref), (1, kg_ref)):
            xv = z_ref[:, src * FOX_W:(src + 1) * FOX_W]
            ms = _split_dot(ones, xv * xv, 1, 0, mat_first=False) * (1.0 / FOX_DH)
            normed.append(xv * lax.rsqrt(ms + EPS) * g_ref[...])
        qn, kn = normed
        q_ref[...] = (qn * FOX_DH ** -0.5).astype(BF16)
        k_b = kn.astype(BF16)
        k_ref[...] = k_b
        v_ref[...] = z_ref[:, 2 * FOX_W:3 * FOX_W].astype(BF16)
        logf = _log_sigmoid(z_ref[:, 3 * FOX_W:FOX_COLS] + b_ref[...])
        r, c = _iota((tm, tm), 0), _iota((tm, tm), 1)
        tri_b = (c <= r).astype(BF16)
        cum = _split_dot(tri_b, logf, 1, 0, terms=3) + carry_ref[...]
        carry_ref[...] = cum[tm - 1:tm]
        c2 = cum * LOG2E
        hi = c2.astype(BF16)
        rem = c2 - hi.astype(F32)
        mid = rem.astype(BF16)
        lo = (rem - mid.astype(F32)).astype(BF16)
        hrow, col = _iota((LANES, 2 * FOX_W), 0), _iota((LANES, 2 * FOX_W), 1)
        base = hrow * LANES + jnp.where(hrow % 2 == 0, FOX_DH, 0)
        placed = None
        for t, part in enumerate((hi, mid, lo)):
            place = jnp.logical_and(col == base + t, hrow < FOX_HEADS).astype(BF16)
            term = _dot(part, place, 1, 0)
            placed = term if placed is None else placed + term
        colw = _iota((tm, 2 * FOX_W), 1)
        head, lane = colw // LANES, colw % LANES
        own = (lane < FOX_DH) == (head % 2 == 0)
        other = jnp.where(head % 2 == 0, lane - FOX_DH, lane)
        ones_q = jnp.where(jnp.logical_and(other >= 0, other < 3), -1.0, 0.0)
        q2 = (qn * (FOX_DH ** -0.5 * LOG2E)).astype(BF16)
        q_exp = jnp.concatenate([q2[:, (h // 2) * LANES:(h // 2 + 1) * LANES] for h in range(FOX_HEADS)], axis=1)
        k_exp = jnp.concatenate([k_b[:, (h // 2) * LANES:(h // 2 + 1) * LANES] for h in range(FOX_HEADS)], axis=1)
        qa_ref[...] = jnp.where(own, q_exp, ones_q.astype(BF16))
        ka_ref[...] = jnp.where(own, k_exp, placed.astype(BF16))

    wide = pl.BlockSpec((tm, 2 * FOX_W), lambda i: (i, 0))
    return _pallas(
        body, name="fox_prep", grid=(nb,),
        in_specs=[pl.BlockSpec((tm, SEG), lambda i: (i, 1)), pl.BlockSpec((1, LANES), lambda i: (0, 0)),
                  pl.BlockSpec((1, FOX_W), lambda i: (0, 0)), pl.BlockSpec((1, FOX_W), lambda i: (0, 0))],
        out_specs=[pl.BlockSpec((tm, FOX_W), lambda i: (i, 0))] * 3 + [wide] * 2,
        out_shape=[jax.ShapeDtypeStruct((T, FOX_W), BF16)] * 3 + [jax.ShapeDtypeStruct((T, 2 * FOX_W), BF16)] * 2,
        scratch_shapes=[pltpu.VMEM((1, LANES), F32)],
        compiler_params=_params(dimension_semantics=("arbitrary",)),
    )(z, bias, qg, kg)


def _fox_fwd(qa, ka, vb, blocks):
    T = qa.shape[0]
    tq = min(ROW_BLOCK, T)
    nq = T // tq
    NEG = -1e30
    n = len(blocks)
    n_pairs = FOX_HEADS // 2

    n_in = 3
    q_of, k_of = _causal_blocks(nq, by_query=True)
    n_tri = len(q_of)

    def body(qt_ref, kt_ref, *refs):
        q_ref, k_ref, v_ref = refs[:n_in]
        o_ref, lse_ref = refs[n_in + n:n_in + n + 2]
        m_sc, l_sc, acc_sc = refs[n_in + 2 * n + 2:n_in + 2 * n + 5]
        pr, t = pl.program_id(0), pl.program_id(1)
        qi, ki = qt_ref[t], kt_ref[t]
        g_start, g_forward, g_finish = _gather_steps(
            refs[n_in:n_in + n], refs[n_in + n + 2:n_in + 2 * n + 2], *refs[n_in + 2 * n + 5:])

        @pl.when(jnp.logical_and(pr == 0, t == 0))
        def _():
            g_start()

        @pl.when(jnp.logical_and(pr == n_pairs // 2, t == 0))
        def _():
            g_forward()

        @pl.when(ki == 0)
        def _():
            m_sc[...] = jnp.full_like(m_sc, NEG)
            l_sc[...] = jnp.zeros_like(l_sc)
            acc_sc[...] = jnp.zeros_like(acc_sc)

        lane = _iota((tq, LANES), 1)

        def block(masked):
            vv = v_ref[...]
            for hh in range(2):
                hs = slice(hh * LANES, (hh + 1) * LANES)
                s = _dot(q_ref[:, hs], k_ref[:, hs], 1, 1)
                tiles = [s[:, j * LANES:(j + 1) * LANES] for j in range(tq // LANES)]
                if masked:
                    row, col = _iota((tq, LANES), 0), _iota((tq, LANES), 1)
                    tiles = [jnp.where(row >= col + j * LANES, t, NEG) for j, t in enumerate(tiles)]
                m_old = m_sc[hh]
                top = jnp.broadcast_to(jnp.max(functools.reduce(jnp.maximum, tiles), axis=-1, keepdims=True),
                                       (tq, LANES))
                m_new = jnp.maximum(m_old, top)
                alpha = jnp.exp2(m_old - m_new)
                ps = [jnp.exp2(t - m_new) for t in tiles]
                l_sc[hh] = alpha * l_sc[hh] + functools.reduce(jnp.add, ps)
                m_sc[hh] = m_new
                p_b = jnp.concatenate([p.astype(BF16) for p in ps], axis=1)
                acc_sc[hh] = alpha * acc_sc[hh] + _dot(p_b, vv, 1, 0)

        @pl.when(ki < qi)
        def _():
            block(False)

        @pl.when(ki == qi)
        def _():
            block(True)
            l0 = jnp.sum(l_sc[0], axis=-1, keepdims=True)
            l1 = jnp.sum(l_sc[1], axis=-1, keepdims=True)
            o_ref[...] = jnp.where(lane < FOX_DH, acc_sc[0] * (1.0 / l0), acc_sc[1] * (1.0 / l1))
            lse_ref[:, :LANES] = m_sc[0] + jnp.log2(l0)
            lse_ref[:, LANES:] = m_sc[1] + jnp.log2(l1)

        @pl.when(jnp.logical_and(pr == n_pairs - 1, t == n_tri - 1))
        def _():
            g_finish()

    qmap = lambda p, t, qt, kt: (qt[t], p)
    kmap = lambda p, t, qt, kt: (kt[t], p)
    hbm = pl.BlockSpec(memory_space=pl.ANY)
    grid_spec = pltpu.PrefetchScalarGridSpec(
        num_scalar_prefetch=2, grid=(n_pairs, n_tri),
        in_specs=[pl.BlockSpec((tq, 2 * LANES), qmap), pl.BlockSpec((tq, 2 * LANES), kmap),
                  pl.BlockSpec((tq, LANES), kmap)] + [hbm] * n,
        out_specs=[pl.BlockSpec((tq, LANES), qmap), pl.BlockSpec((tq, 2 * LANES), qmap)] + [hbm] * n,
        scratch_shapes=[pltpu.VMEM((2, tq, LANES), F32)] * 3 + _gather_scratch(n))
    res = _pallas(
        body, name="fox_fwd", grid_spec=grid_spec,
        out_shape=[jax.ShapeDtypeStruct((T, FOX_W), F32), jax.ShapeDtypeStruct((T, 2 * FOX_W), F32)]
        + [jax.ShapeDtypeStruct((N_DEV,) + b.shape, b.dtype) for b in blocks],
        compiler_params=_params(dimension_semantics=("arbitrary",) * 2),
    )(q_of, k_of, qa, ka, vb, *blocks)
    return res[0], res[1], res[2:]


def _fox_bwd(qs, kn, vb, qa, ka, o, do, lse, hs):
    T = qs.shape[0]
    tq = min(ROW_BLOCK, T)
    nq = T // tq
    n = len(hs)
    n_pairs = FOX_HEADS // 2

    q_of, k_of = _causal_blocks(nq, by_query=False)
    n_tri = len(q_of)

    def body(qt_ref, kt_ref, *refs):
        q_ref, k_ref, v_ref, qa_ref, ka_ref, o_ref, do_ref, lse_ref = refs[:8]
        dq_ref, dk_ref, dv_ref, dcs_ref, drs_ref = refs[8 + n:13 + n]
        pr, t = pl.program_id(0), pl.program_id(1)
        qi, ki = qt_ref[t], kt_ref[t]
        x_start, x_finish = _chip_exchange_steps(refs[8:8 + n], refs[13 + n:13 + 2 * n], *refs[13 + 2 * n:])

        @pl.when(jnp.logical_and(pr == 0, t == 0))
        def _():
            x_start()
            drs_ref[...] = jnp.zeros_like(drs_ref)

        @pl.when(t == 0)
        def _():
            dq_ref[...] = jnp.zeros_like(dq_ref)

        @pl.when(qi == ki)
        def _():
            dk_ref[...] = jnp.zeros_like(dk_ref)
            dv_ref[...] = jnp.zeros_like(dv_ref)
            dcs_ref[...] = jnp.zeros_like(dcs_ref)

        def block(masked):
            lane = _iota((tq, LANES), 1)
            qv, kv, vv = q_ref[...], k_ref[...], v_ref[...]
            ov, dov = o_ref[...], do_ref[...]
            qrows = pl.ds(pl.multiple_of(qi * tq, tq), tq)
            dq_acc = jnp.zeros((tq, LANES), F32)
            dk_acc = jnp.zeros((tq, LANES), F32)
            dv_acc = jnp.zeros((tq, LANES), F32)
            dcs_acc = jnp.zeros((8, tq), F32)
            drs_acc = jnp.zeros((tq, LANES), F32)
            prod = dov * ov
            nt = tq // LANES
            for hh in range(2):
                in_head = (lane < FOX_DH) if hh == 0 else (lane >= FOX_DH)
                hs_ = slice(hh * LANES, (hh + 1) * LANES)
                zb = jnp.zeros_like(qv)
                qm = jnp.where(in_head, qv, zb)
                km = jnp.where(in_head, kv, zb)
                dom = jnp.where(in_head, dov, 0.0).astype(BF16)
                delta_b = jnp.broadcast_to(jnp.sum(jnp.where(in_head, prod, 0.0), axis=1, keepdims=True), (tq, LANES))
                lse_b = lse_ref[:, hs_]
                s = _dot(qa_ref[:, hs_], ka_ref[:, hs_], 1, 1)
                dp = _dot(dom, vv, 1, 1)
                p_tiles, ds_tiles, col_tiles = [], [], []
                row_part = jnp.zeros((tq, LANES), F32)
                for j in range(nt):
                    js = slice(j * LANES, (j + 1) * LANES)
                    p = jnp.exp2(s[:, js] - lse_b)
                    if masked:
                        p = jnp.where(_iota((tq, LANES), 0) >= _iota((tq, LANES), 1) + j * LANES, p, 0.0)
                    ds = p * (dp[:, js] - delta_b)
                    p_tiles.append(p.astype(BF16))
                    ds_tiles.append(ds.astype(BF16))
                    col_tiles.append(jnp.sum(ds, axis=0, keepdims=True))
                    row_part = row_part + ds
                p_b = jnp.concatenate(p_tiles, axis=1)
                ds_b = jnp.concatenate(ds_tiles, axis=1)
                dv_acc = dv_acc + _dot(p_b, dom, 0, 0)
                dq_acc = dq_acc + _dot(ds_b, km, 1, 0)
                dk_acc = dk_acc + _dot(ds_b, qm, 0, 0)
                dcs_acc = dcs_acc + jnp.where(_iota((8, tq), 0) == hh, jnp.concatenate(col_tiles, axis=1), 0.0)
                rowsum = jnp.sum(row_part, axis=1, keepdims=True)
                drs_acc = drs_acc + jnp.where(lane == 2 * pr + hh, rowsum, 0.0)
            drs_ref[qrows, :] += drs_acc
            dq_ref[qrows, :] += dq_acc
            dk_ref[...] += dk_acc
            dv_ref[...] += dv_acc
            dcs_ref[0] += dcs_acc

        @pl.when(qi > ki)
        def _():
            block(False)

        @pl.when(qi == ki)
        def _():
            block(True)

        @pl.when(jnp.logical_and(pr == n_pairs - 1, t == n_tri - 1))
        def _():
            x_finish()

    qmap = lambda p, t, qt, kt: (qt[t], p)
    kmap = lambda p, t, qt, kt: (kt[t], p)
    hbm = pl.BlockSpec(memory_space=pl.ANY)
    grid_spec = pltpu.PrefetchScalarGridSpec(
        num_scalar_prefetch=2, grid=(n_pairs, n_tri),
        in_specs=[pl.BlockSpec((tq, LANES), qmap), pl.BlockSpec((tq, LANES), kmap), pl.BlockSpec((tq, LANES), kmap),
                  pl.BlockSpec((tq, 2 * LANES), qmap), pl.BlockSpec((tq, 2 * LANES), kmap),
                  pl.BlockSpec((tq, LANES), qmap), pl.BlockSpec((tq, LANES), qmap), pl.BlockSpec((tq, 2 * LANES), qmap)]
        + [hbm] * n,
        out_specs=[pl.BlockSpec((T, LANES), lambda p, t, qt, kt: (0, p)), pl.BlockSpec((tq, LANES), kmap),
                   pl.BlockSpec((tq, LANES), kmap), pl.BlockSpec((1, 8, tq), lambda p, t, qt, kt: (p, 0, kt[t])),
                   pl.BlockSpec((T, LANES), lambda p, t, qt, kt: (0, 0))] + [hbm] * n,
        scratch_shapes=_chip_exchange_scratch(n))
    res = _pallas(
        body, name="fox_bwd", grid_spec=grid_spec,
        out_shape=[jax.ShapeDtypeStruct((T, FOX_W), F32)] * 3
        + [jax.ShapeDtypeStruct((n_pairs, 8, T), F32), jax.ShapeDtypeStruct((T, LANES), F32)]
        + _chip_exchange_shapes(hs),
        compiler_params=_params(dimension_semantics=("arbitrary",) * 2),
    )(q_of, k_of, qs, kn, vb, qa, ka, o, do, lse, *hs)
    res = list(res)
    return res[:5] + [res[5:]]


def _fox_post(z, dq, dk, dv, dcs, drs, bias, qg, kg, dz_buf):
    T = z.shape[0]
    tm = min(ROW_BLOCK, T)
    nb = T // tm

    def body(z_ref, dq_ref, dk_ref, dv_ref, dcs_ref, drs_ref, b_ref, qg_ref, kg_ref, _buf_ref, dz_ref, dqg_ref, dkg_ref,
             db_ref, carry_ref):
        @pl.when(pl.program_id(0) == 0)
        def _():
            carry_ref[...] = jnp.zeros_like(carry_ref)
            dqg_ref[...] = jnp.zeros_like(dqg_ref)
            dkg_ref[...] = jnp.zeros_like(dkg_ref)
            db_ref[...] = jnp.zeros_like(db_ref)

        ones = _head_ones()
        for src, g_ref, d_ref, dg_ref, scale in ((0, qg_ref, dq_ref, dqg_ref, FOX_DH ** -0.5), (1, kg_ref, dk_ref, dkg_ref, 1.0)):
            xv = z_ref[:, src * FOX_W:(src + 1) * FOX_W]
            ms = _split_dot(ones, xv * xv, 1, 0, mat_first=False) * (1.0 / FOX_DH)
            rstd = lax.rsqrt(ms + EPS)
            xh = xv * rstd
            dn = d_ref[...] * scale
            dg_ref[...] += jnp.sum(dn * xh, axis=0, keepdims=True)
            dxh = dn * g_ref[...]
            mean = _split_dot(ones, dxh * xh, 1, 0, mat_first=False) * (1.0 / FOX_DH)
            dz_ref[:, src * FOX_W:(src + 1) * FOX_W] = (rstd * (dxh - xh * mean)).astype(BF16)
        dz_ref[:, 2 * FOX_W:3 * FOX_W] = dv_ref[...].astype(BF16)
        row8 = _iota((8, tm), 0)
        dct = jnp.zeros((8, tm), F32)
        for h in range(FOX_HEADS):
            src_row = dcs_ref[h // 2][h % 2:h % 2 + 1, :]
            dct = dct + jnp.where(row8 == h, src_row, 0.0)
        dct = drs_ref[...].T[0:8] - dct
        r, c = _iota((tm, tm), 0), _iota((tm, tm), 1)
        upper_b = (r >= c).astype(BF16)
        rc = _split_dot(upper_b, dct, 1, 0, mat_first=False) + carry_ref[...]
        carry_ref[...] = rc[:, 0:1]
        full = jnp.concatenate([rc, jnp.zeros((LANES - 8, tm), F32)], axis=0)
        dlogf = full.T
        xf = z_ref[:, 3 * FOX_W:FOX_COLS] + b_ref[...]
        df = dlogf * (1.0 - _sigmoid(xf))
        dz_ref[:, 3 * FOX_W:FOX_COLS] = df.astype(BF16)
        dz_ref[:, FOX_COLS:] = jnp.zeros((tm, SEG - FOX_COLS), BF16)
        db_ref[...] += jnp.sum(df, axis=0, keepdims=True)

    rev = lambda i: (nb - 1 - i, 0)
    fix2 = lambda i: (0, 0)
    return _pallas(
        body, name="fox_post", grid=(nb,),
        in_specs=[pl.BlockSpec((tm, SEG), lambda i: (nb - 1 - i, 1)), pl.BlockSpec((tm, FOX_W), rev),
                  pl.BlockSpec((tm, FOX_W), rev),
                  pl.BlockSpec((tm, FOX_W), rev), pl.BlockSpec((FOX_HEADS // 2, 8, tm), lambda i: (0, 0, nb - 1 - i)),
                  pl.BlockSpec((tm, LANES), rev),
                  pl.BlockSpec((1, LANES), fix2), pl.BlockSpec((1, FOX_W), fix2), pl.BlockSpec((1, FOX_W), fix2),
                  pl.BlockSpec(memory_space=pl.ANY)],
        out_specs=[pl.BlockSpec((tm, SEG), lambda i: (nb - 1 - i, 1)), pl.BlockSpec((1, FOX_W), fix2),
                   pl.BlockSpec((1, FOX_W), fix2), pl.BlockSpec((1, LANES), fix2)],
        out_shape=[jax.ShapeDtypeStruct(dz_buf.shape, BF16), jax.ShapeDtypeStruct((1, FOX_W), F32),
                   jax.ShapeDtypeStruct((1, FOX_W), F32), jax.ShapeDtypeStruct((1, LANES), F32)],
        scratch_shapes=[pltpu.VMEM((8, 1), F32)],
        input_output_aliases={9: 0},
        compiler_params=_params(dimension_semantics=("arbitrary",)),
    )(z, dq, dk, dv, dcs, drs, bias, qg, kg, dz_buf)


def _local_step(x, p, tgt, sm, W, rest_chunks, core):
    lbl, og, fb = sm["hg_lb_logits"], sm["hg_onorm_g"], sm["fox_f_bias"]
    fbias = jnp.pad(fb, ((0, 0), (0, LANES - FOX_HEADS)))
    qg = jnp.tile(sm["fox_q_norm_g"], (1, FOX_HEADS))
    kg = jnp.tile(sm["fox_k_norm_g"], (1, FOX_HEADS))

    h = _rms_fwd(x, sm["norm_mix_g"], name="rms_mix")
    rest = dict(zip(BIG[1:], rest_chunks))
    first, second = ["w_ffn_gate"], ["w_ffn_up", "w_ffn_down"]
    third = [n for n in BIG[1:] if n not in first + second]
    z, got1 = _matmul(h, W["w_in"], tb=True, gather=[rest[n] for n in first], name="mm_z")
    o_raw, ya, states, got2 = _hgrn_fwd(z, lbl, og, [rest[n] for n in second])
    qs, kn, vb, qa, ka = _fox_prep(z, fbias, qg, kg)
    yb, lse, got3 = _fox_fwd(qa, ka, vb, [rest[n] for n in third])
    W = dict(W, **{n: _full_of_chunks(n, g)
                   for n, g in zip(first + second + third, list(got1) + list(got2) + list(got3))})
    merged, ua, ub = _merge_fwd(ya, yb, W["w_branch_a"], W["w_branch_b"], z)
    x1, hf = _matmul(merged, W["w_out"], add=x, norm_fwd=sm["norm_ffn_g"], name="mm_x1")
    a, b, act = _swiglu_fwd(hf, W["w_ffn_gate"], W["w_ffn_up"])
    x2 = _matmul(act, W["w_ffn_down"], add=x1, name="mm_x2")
    hp, dy, dsp, dpp, loss = _ple_loss(x2, p, sm["norm_ple_g"], W["w_ple_gate"], W["w_ple_proj"], tgt)

    G = {}
    G["w_ple_proj"] = _matmul(p, dpp, ta=True, out_dtype=BF16, name="mm_dw_ple_proj")
    G["w_ple_gate"] = _matmul(hp, dsp, ta=True, out_dtype=BF16, name="mm_dw_ple_gate")
    dx2, d_ple_g = _matmul(dsp, W["w_ple_gate"], tb=True, norm_bwd=(x2, sm["norm_ple_g"], dy), name="mm_dx2")
    G["w_ffn_down"] = _matmul(act, dx2, ta=True, out_dtype=BF16, name="mm_dw_ffn_down")
    da, db = _swiglu_bwd(dx2, W["w_ffn_down"], a, b)
    G["w_ffn_gate"] = _matmul(da, hf, ta=True, out_dtype=BF16, name="mm_dw_ffn_gate")
    G["w_ffn_up"] = _matmul(db, hf, ta=True, out_dtype=BF16, name="mm_dw_ffn_up")
    dhf = _matmul(da, W["w_ffn_gate"], name="mm_dhf_a")
    dx1, d_ffn_g = _matmul(db, W["w_ffn_up"], add=dhf, norm_bwd=(x1, sm["norm_ffn_g"], dx2), name="mm_dx1")
    G["w_out"] = _matmul(merged, dx1, ta=True, out_dtype=BF16, name="mm_dw_out")
    dua, dub, dz, dya, dyb = _merge_bwd(dx1, W["w_out"], W["w_branch_a"], W["w_branch_b"], ua, ub, z)
    G["w_branch_a"] = _matmul(ya, dua, ta=True, out_dtype=BF16, name="mm_dw_branch_a")
    G["w_branch_b"] = _matmul(yb, dub, ta=True, out_dtype=BF16, name="mm_dw_branch_b")
    hb_rest = _sibling_sums({n: G[n] for n in BIG[1:]}, core, tag="rest")
    dq, dk, dv, dcs, drs, got_rest = _fox_bwd(qs, kn, vb, qa, ka, yb, dyb, lse, hb_rest)
    dz, d_qg, d_kg, d_fb = _fox_post(z, dq, dk, dv, dcs, drs, fbias, qg, kg, dz)
    dz, d_lbl, d_og = _hgrn_bwd(z, o_raw, dya, states, lbl, og, dz)
    G["w_in"] = _matmul(dz, h, ta=True, out_dtype=BF16, name="mm_dw_in")
    hb_in = _sibling_sums({"w_in": G["w_in"]}, core, tag="w_in")
    (grad_x, d_mix_g), got_in = _matmul(dz, W["w_in"], exchange=hb_in,
                                        norm_bwd=(x, sm["norm_mix_g"], dx1), name="mm_dx")

    gs = {"norm_mix_g": d_mix_g, "hg_lb_logits": d_lbl, "hg_onorm_g": d_og, "fox_f_bias": d_fb[:, :FOX_HEADS],
          "fox_q_norm_g": d_qg.reshape(FOX_HEADS, FOX_DH).sum(0, keepdims=True),
          "fox_k_norm_g": d_kg.reshape(FOX_HEADS, FOX_DH).sum(0, keepdims=True),
          "norm_ffn_g": d_ffn_g, "norm_ple_g": d_ple_g}
    return loss, grad_x, gs, hb_in + hb_rest, list(got_in) + list(got_rest)


def _pack_rows(parts, total):
    buf = jnp.concatenate(parts, axis=-2)
    pad = total - buf.shape[-2]
    widths = [(0, 0)] * (buf.ndim - 2) + [(0, pad), (0, 0)]
    return jnp.pad(buf, widths)


def _chunk_of_shard(n, w):
    if n == "w_in":
        return jnp.pad(w, ((0, IN_SHARD_PAD - IN_SHARD), (0, 0)))
    if n in ("w_ffn_gate", "w_ffn_up", "w_ffn_down"):
        return jnp.pad(w, ((0, FF_SHARD_PAD - FF_SHARD), (0, 0)))
    return w


def _full_of_chunks(n, g):
    _, a, b = g.shape
    if n == "w_in":
        w = g[:, :IN_SHARD].reshape(IN_COLS, b)
        gap = jnp.zeros((SEG - FOX_LOGICAL, b), g.dtype)
        return jnp.concatenate([w[:HG_COLS + FOX_LOGICAL], gap, w[HG_COLS + FOX_LOGICAL:]], axis=0)
    if BIG_SHAPE[n][2] == 0:
        return g.reshape(N_DEV * a, b)
    return g.transpose(1, 0, 2).reshape(a, N_DEV * b)


def _chunks_of_full(n, g):
    if n == "w_in":
        w = jnp.concatenate([g[:HG_COLS + FOX_LOGICAL], g[2 * SEG:]], axis=0).reshape(N_DEV, IN_SHARD, g.shape[1])
        return jnp.pad(w, ((0, 0), (0, IN_SHARD_PAD - IN_SHARD), (0, 0)))
    if BIG_SHAPE[n][2] == 0:
        return g.reshape(N_DEV, g.shape[0] // N_DEV, g.shape[1])
    return g.reshape(g.shape[0], N_DEV, g.shape[1] // N_DEV).transpose(1, 0, 2)


def _pack_small(vals, loss_row=None):
    parts = [vals[n].reshape(SMALL_ROWS[n], -1) for n in SMALL]
    parts = [jnp.pad(v, ((0, 0), (0, LANES - v.shape[1]))) for v in parts]
    if loss_row is not None:
        parts.append(loss_row)
    return _pack_rows(parts, SMALL_TOTAL)


def _unpack_small(buf, like):
    out, r0 = {}, 0
    for n in SMALL:
        rows, size = SMALL_ROWS[n], like[n].size
        blk = buf[r0:r0 + rows]
        out[n] = (blk if size == rows * LANES else blk[:, :size]).reshape(like[n].shape)
        r0 += rows
    return out


def _place():
    return lax.axis_index("x"), lax.axis_index("y"), lax.axis_index("c")


def _gather_steps(x_refs, out_refs, send_sems, recv_sems, local_sems):
    n = len(x_refs)
    x, y, c = _place()
    me, sibling = (x, y, c), (x, y, 1 - c)
    chips = [(1 - x, y), (x, 1 - y), (1 - x, 1 - y)]

    def slot(i, px, py, pc):
        return out_refs[i].at[4 * px + 2 * py + pc]

    def copy(k, i, blk, to, own=False):
        return pltpu.make_async_remote_copy(
            src_ref=x_refs[i] if own else slot(i, *blk), dst_ref=slot(i, *blk),
            send_sem=send_sems.at[k, i], recv_sem=recv_sems.at[k, i], device_id=to, device_id_type=MESH)

    def mine():
        return [pltpu.make_async_copy(x_refs[i], slot(i, *me), local_sems.at[i]) for i in range(n)]

    def first():
        cps = [copy(0, i, me, sibling, own=True) for i in range(n)]
        return cps + [copy(1 + j, i, me, (*chip, c), own=True) for j, chip in enumerate(chips) for i in range(n)]

    def passed():
        return [copy(4 + j, i, (*chip, c), sibling) for j, chip in enumerate(chips) for i in range(n)]

    def start():
        for cp in mine() + first():
            cp.start()

    def forward():
        fws = passed()
        for j, chip in enumerate(chips):
            for i in range(n):
                copy(1 + j, i, (*chip, c), me).wait_recv()
                fws[j * n + i].start()

    def finish():
        for i in range(n):
            copy(0, i, sibling, me).wait_recv()
        for j, chip in enumerate(chips):
            for i in range(n):
                copy(4 + j, i, (*chip, 1 - c), me).wait_recv()
        for cp in first() + passed():
            cp.wait_send()
        for cp in mine():
            cp.wait()

    return start, forward, finish


def _gather_scratch(n):
    return [pltpu.SemaphoreType.DMA((7, n)), pltpu.SemaphoreType.DMA((7, n)), pltpu.SemaphoreType.DMA((n,))]


def _all_gather(blocks, *, name):
    n = len(blocks)

    def body(*refs):
        for step in _gather_steps(refs[:n], refs[n:2 * n], *refs[2 * n:]):
            step()

    hbm = pl.BlockSpec(memory_space=pl.ANY)
    return _pallas(
        body, name=name, out_shape=[jax.ShapeDtypeStruct((N_DEV,) + b.shape, b.dtype) for b in blocks],
        in_specs=[hbm] * n, out_specs=[hbm] * n, scratch_shapes=_gather_scratch(n),
    )(*blocks)


def _sibling_exchange(gs, *, name):
    n = len(gs)

    def body(*refs):
        g_refs, out_refs = refs[:n], refs[n:2 * n]
        send_sems, recv_sems = refs[2 * n:]
        x, y, c = _place()
        cps = [pltpu.make_async_remote_copy(
            src_ref=g_refs[i].at[:, pl.ds(1 - c, 1)], dst_ref=out_refs[i], send_sem=send_sems.at[i],
            recv_sem=recv_sems.at[i], device_id=(x, y, 1 - c), device_id_type=MESH) for i in range(n)]
        for cp in cps:
            cp.start()
        for cp in cps:
            cp.wait()

    hbm = pl.BlockSpec(memory_space=pl.ANY)
    return _pallas(
        body, name=name, out_shape=[jax.ShapeDtypeStruct((N_CHIP, 1) + g.shape[2:], g.dtype) for g in gs],
        in_specs=[hbm] * n, out_specs=[hbm] * n,
        scratch_shapes=[pltpu.SemaphoreType.DMA((n,)), pltpu.SemaphoreType.DMA((n,))],
    )(*gs)


def _chip_sum(g4, got, core, *, name):
    _, _, a, b = g4.shape

    def body(c_ref, g_ref, r_ref, h_ref):
        h_ref[0] = (g_ref[0, 0].astype(F32) + r_ref[0, 0].astype(F32)).astype(BF16)

    grid_spec = pltpu.PrefetchScalarGridSpec(
        num_scalar_prefetch=1, grid=(N_CHIP,),
        in_specs=[pl.BlockSpec((1, 1, a, b), lambda j, c: (j, c[0], 0, 0)),
                  pl.BlockSpec((1, 1, a, b), lambda j, c: (j, 0, 0, 0))],
        out_specs=[pl.BlockSpec((1, a, b), lambda j, c: (j, 0, 0))])
    return _pallas(
        body, name=name, grid_spec=grid_spec, out_shape=[jax.ShapeDtypeStruct((N_CHIP, a, b), BF16)],
        compiler_params=_params(dimension_semantics=("arbitrary",)),
    )(core, g4, got)[0]


def _chip_exchange(hs, *, name):
    n = len(hs)

    def body(*refs):
        for step in _chip_exchange_steps(refs[:n], refs[n:2 * n], *refs[2 * n:]):
            step()

    hbm = pl.BlockSpec(memory_space=pl.ANY)
    return _pallas(
        body, name=name, out_shape=_chip_exchange_shapes(hs), in_specs=[hbm] * n, out_specs=[hbm] * n,
        scratch_shapes=_chip_exchange_scratch(n),
    )(*hs)


def _chip_exchange_steps(h_refs, out_refs, send_sems, recv_sems):
    n = len(h_refs)
    x, y, c = _place()
    chips = [(1 - x, y), (x, 1 - y), (1 - x, 1 - y)]

    def copies():
        return [pltpu.make_async_remote_copy(
            src_ref=h_refs[i].at[2 * px + py], dst_ref=out_refs[i].at[k], send_sem=send_sems.at[k, i],
            recv_sem=recv_sems.at[k, i], device_id=(px, py, c), device_id_type=MESH)
            for k, (px, py) in enumerate(chips) for i in range(n)]

    def start():
        for cp in copies():
            cp.start()

    def finish():
        for cp in copies():
            cp.wait()

    return start, finish


def _chip_exchange_shapes(hs):
    return [jax.ShapeDtypeStruct((3,) + h.shape[1:], h.dtype) for h in hs]


def _chip_exchange_scratch(n):
    return [pltpu.SemaphoreType.DMA((3, n)), pltpu.SemaphoreType.DMA((3, n))]


def _sibling_sums(G, core, *, tag):
    g4 = []
    for n, g in G.items():
        gc = _chunks_of_full(n, g)
        g4.append(gc.reshape((N_CHIP, 2) + gc.shape[1:]))
    got = _sibling_exchange(g4, name="grads_to_sibling_" + tag)
    return [_chip_sum(g, r, core, name="chip_sum_" + n) for n, g, r in zip(G, g4, got)]


def _adam_math(w, g, m, v):
    m = ADAM_B1 * m + (1.0 - ADAM_B1) * g
    v = ADAM_B2 * v + (1.0 - ADAM_B2) * (g * g)
    m_hat = m / (1.0 - ADAM_B1 ** ADAM_STEP)
    v_hat = v / (1.0 - ADAM_B2 ** ADAM_STEP)
    delta = -ADAM_LR * (m_hat / (jnp.sqrt(v_hat) + ADAM_EPS) + ADAM_WD * w)
    return delta, m, v


def _adam_shard(hb, got, chip, w, m, v, *, name):
    _, r, c = w.shape
    _, a, cb = hb.shape
    assert cb == c and c % LANES == 0, (hb.shape, w.shape)
    tc = _pick(c, 2 * LANES)

    def body(j_ref, h_ref, r_ref, w_ref, m_ref, v_ref, g_ref, d_ref, nm_ref, nv_ref):
        parts = [h_ref[0], r_ref[0], r_ref[1], r_ref[2]]
        g = None
        for part in parts:
            part = part[:r].astype(F32)
            g = part if g is None else g + part
        d, nm, nv = _adam_math(w_ref[0], g, m_ref[0], v_ref[0])
        g_ref[0] = g
        d_ref[0] = d
        nm_ref[0] = nm
        nv_ref[0] = nv

    blk = pl.BlockSpec((1, r, tc), lambda i, j: (0, 0, i))
    grid_spec = pltpu.PrefetchScalarGridSpec(
        num_scalar_prefetch=1, grid=(c // tc,),
        in_specs=[pl.BlockSpec((1, a, tc), lambda i, j: (j[0], 0, i)),
                  pl.BlockSpec((3, a, tc), lambda i, j: (0, 0, i)), blk, blk, blk],
        out_specs=[blk] * 4)
    return _pallas(
        body, name=name, grid_spec=grid_spec, out_shape=[jax.ShapeDtypeStruct((1, r, c), F32)] * 4,
        compiler_params=_params(dimension_semantics=("arbitrary",)),
    )(chip, hb, got, w, m, v)


def _small_all_reduce_adam(gs, w, m, v):
    def body(g_ref, w_ref, m_ref, v_ref, sum_ref, d_ref, nm_ref, nv_ref, gather, send_sems, recv_sems):
        x, y, c = _place()
        my = 4 * x + 2 * y + c
        gather[my] = g_ref[...]
        cps = []
        for k in range(1, N_DEV):
            to = (x ^ (k >> 2), y ^ ((k >> 1) & 1), c ^ (k & 1))
            cps.append(pltpu.make_async_remote_copy(
                src_ref=g_ref, dst_ref=gather.at[my], send_sem=send_sems.at[k - 1], recv_sem=recv_sems.at[k - 1],
                device_id=to, device_id_type=MESH))
        for cp in cps:
            cp.start()
        for cp in cps:
            cp.wait()
        total = gather[0]
        for d in range(1, N_DEV):
            total = total + gather[d]
        dlt, nm, nv = _adam_math(w_ref[...], total, m_ref[...], v_ref[...])
        sum_ref[...] = total
        d_ref[...] = dlt
        nm_ref[...] = nm
        nv_ref[...] = nv

    vm = pl.BlockSpec(memory_space=pltpu.VMEM)
    return _pallas(
        body, name="small_all_reduce_adam", out_shape=[jax.ShapeDtypeStruct((SMALL_TOTAL, LANES), F32)] * 4,
        in_specs=[vm] * 4, out_specs=[vm] * 4,
        scratch_shapes=[pltpu.VMEM((N_DEV, SMALL_TOTAL, LANES), F32), pltpu.SemaphoreType.DMA((7,)),
                        pltpu.SemaphoreType.DMA((7,))],
            )(gs, w, m, v)


def kernel(x, p, norm_mix_g, w_in, hg_lb_logits, hg_onorm_g, fox_f_bias, fox_q_norm_g, fox_k_norm_g, w_branch_a, w_branch_b, w_out, norm_ffn_g, w_ffn_gate, w_ffn_up, w_ffn_down, norm_ple_g, w_ple_gate, w_ple_proj, loss_target, m_norm_mix_g, m_w_in, m_hg_lb_logits, m_hg_onorm_g, m_fox_f_bias, m_fox_q_norm_g, m_fox_k_norm_g, m_w_branch_a, m_w_branch_b, m_w_out, m_norm_ffn_g, m_w_ffn_gate, m_w_ffn_up, m_w_ffn_down, m_norm_ple_g, m_w_ple_gate, m_w_ple_proj, v_norm_mix_g, v_w_in, v_hg_lb_logits, v_hg_onorm_g, v_fox_f_bias, v_fox_q_norm_g, v_fox_k_norm_g, v_w_branch_a, v_w_branch_b, v_w_out, v_norm_ffn_g, v_w_ffn_gate, v_w_ffn_up, v_w_ffn_down, v_norm_ple_g, v_w_ple_gate, v_w_ple_proj):
    args = dict(locals())
    wts = {n: args[n] for n in BIG + SMALL}
    mom = {n: args["m_" + n] for n in BIG + SMALL}
    var = {n: args["v_" + n] for n in BIG + SMALL}
    for group in (wts, mom, var):
        for n in TRANSPOSED:
            group[n] = jnp.swapaxes(group[n], 1, 2)
    sm = {n: wts[n] for n in SMALL}

    xi, yi, ci = _place()
    core = jnp.reshape(ci, (1,)).astype(jnp.int32)
    chip = jnp.reshape(2 * xi + yi, (1,)).astype(jnp.int32)
    chunks = [_chunk_of_shard(n, wts[n][0].astype(BF16)) for n in BIG]
    assert BIG[0] == "w_in"
    w_in_full = _full_of_chunks("w_in", _all_gather(chunks[:1], name="w_in_all_gather")[0])

    loss_blk, grad_x, gs, hb, got = _local_step(
        x[0], p[0, 0], loss_target[0], sm, {"w_in": w_in_full}, chunks[1:], core)

    g_big, d_big, nm_big, nv_big = {}, {}, {}, {}
    for n, h, r in zip(BIG, hb, got):
        res = _adam_shard(h, r, chip, wts[n], mom[n], var[n], name="adam_" + n)
        if n in TRANSPOSED:
            res = [jnp.swapaxes(t, 1, 2) for t in res]
        g_big[n], d_big[n], nm_big[n], nv_big[n] = res

    s_sum, s_d, s_nm, s_nv = _small_all_reduce_adam(
        _pack_small(gs, loss_blk[0:1]), _pack_small(sm), _pack_small({n: mom[n] for n in SMALL}),
        _pack_small({n: var[n] for n in SMALL}))
    loss = s_sum[LOSS_ROW, 0]
    g_small, d_small, nm_small, nv_small = (_unpack_small(t, sm) for t in (s_sum, s_d, s_nm, s_nv))

    order = ["norm_mix_g", "w_in", "hg_lb_logits", "hg_onorm_g", "fox_f_bias", "fox_q_norm_g", "fox_k_norm_g",
             "w_branch_a", "w_branch_b", "w_out", "norm_ffn_g", "w_ffn_gate", "w_ffn_up", "w_ffn_down", "norm_ple_g",
             "w_ple_gate", "w_ple_proj"]
    outs = [loss, grad_x[None]]
    for big, small in ((g_big, g_small), (d_big, d_small), (nm_big, nm_small), (nv_big, nv_small)):
        outs += [big[n] if n in big else small[n] for n in order]
    return tuple(outs)
```

```python
import functools

import jax
import jax.numpy as jnp
from jax import lax
from jax.experimental import pallas as pl
from jax.experimental.pallas import tpu as pltpu

F32 = jnp.float32
BF16 = jnp.bfloat16

D_MODEL = 1024
PLE_DIM = 256
HG_HEADS = 4
HG_DK = 128
HG_CHUNK = 64
HG_SUB = 16
HG_W = HG_HEADS * HG_DK
FOX_HEADS = 8
FOX_DH = 64
FOX_W = FOX_HEADS * FOX_DH
D_FF = 2816
EPS = 1e-6
N_DEV = 8
N_CHIP = 4
LANES = 128
FOX_COLS = 3 * FOX_W + LANES
HG_COLS = 4 * HG_W
GATE_COLS = 2 * D_MODEL
IN_COLS = HG_COLS + 3 * FOX_W + FOX_HEADS + GATE_COLS
FOX_LOGICAL = 3 * FOX_W + FOX_HEADS
SEG = 2048
IN_PAD = 3 * SEG
IN_SHARD = IN_COLS // N_DEV
IN_SHARD_PAD = 768
FF_SHARD = D_FF // N_DEV
FF_SHARD_PAD = 384
EXP_CLAMP = 80.0
LOG2E = 1.4426950408889634

ADAM_LR = 0.001
ADAM_B1 = 0.9
ADAM_B2 = 0.999
ADAM_EPS = 1e-08
ADAM_WD = 0.01
ADAM_STEP = 10

MESH = pl.DeviceIdType.MESH
VMEM_LIMIT = 56 * 1024 * 1024
ROW_BLOCK = 512

BIG = ["w_in", "w_branch_a", "w_branch_b", "w_out", "w_ffn_gate", "w_ffn_up", "w_ffn_down",
       "w_ple_gate", "w_ple_proj"]
TRANSPOSED = ("w_in", "w_ffn_gate", "w_ffn_up")
BIG_SHAPE = {
    "w_in": (IN_COLS, D_MODEL, 0), "w_branch_a": (HG_W, D_MODEL, 1), "w_branch_b": (FOX_W, D_MODEL, 1),
    "w_out": (D_MODEL, D_MODEL, 0), "w_ffn_gate": (D_FF, D_MODEL, 0), "w_ffn_up": (D_FF, D_MODEL, 0),
    "w_ffn_down": (D_FF, D_MODEL, 0), "w_ple_gate": (D_MODEL, D_MODEL, 0), "w_ple_proj": (PLE_DIM, D_MODEL, 1),
}

SMALL = ["norm_mix_g", "hg_lb_logits", "hg_onorm_g", "fox_f_bias", "fox_q_norm_g", "fox_k_norm_g",
         "norm_ffn_g", "norm_ple_g"]
SMALL_ROWS = {"norm_mix_g": 8, "hg_lb_logits": 8, "hg_onorm_g": 1, "fox_f_bias": 1, "fox_q_norm_g": 1,
              "fox_k_norm_g": 1, "norm_ffn_g": 8, "norm_ple_g": 8}
SMALL_TOTAL = 40
LOSS_ROW = 36


def _pallas(body, **kw):
    return pl.pallas_call(body, **kw)


def _params(**kw):
    return pltpu.CompilerParams(vmem_limit_bytes=VMEM_LIMIT, **kw)


def _pick(n, target):
    if n <= target:
        return n
    best = None
    for t in range(LANES, target + 1, LANES):
        if n % t == 0:
            best = t
    assert best is not None, (n, target)
    return best


def _dot(a, b, ca, cb):
    return lax.dot_general(a, b, (((ca,), (cb,)), ((), ())), preferred_element_type=F32)


def _split_dot(mat, x, ca, cb, terms=2, mat_first=True):
    acc = None
    rem = x
    for _ in range(terms):
        part = rem.astype(BF16)
        rem = rem - part.astype(F32)
        p = _dot(mat, part, ca, cb) if mat_first else _dot(part, mat, ca, cb)
        acc = p if acc is None else acc + p
    return acc


def _sigmoid(x):
    return 1.0 / (1.0 + jnp.exp(-x))


def _iota(shape, dim):
    return lax.broadcasted_iota(jnp.int32, shape, dim)


def _matmul(a, b, *, name, ta=False, tb=False, out_dtype=F32, add=None, exchange=None, gather=None,
            norm_fwd=None, norm_bwd=None):
    assert exchange is None or gather is None
    (K, M) = a.shape if ta else a.shape[::-1]
    (N, Kb) = b.shape if tb else b.shape[::-1]
    assert K == Kb, (a.shape, b.shape, ta, tb)
    if ta:
        tm, tn, tk = _pick(M, 2 * ROW_BLOCK), _pick(N, 2 * ROW_BLOCK), _pick(K, 4 * ROW_BLOCK)
    else:
        tm, tn, tk = _pick(M, 2 * ROW_BLOCK), _pick(N, 2048), _pick(K, 3072)
    if norm_bwd is not None:
        tm = _pick(M, ROW_BLOCK)
    nk = K // tk
    use_scratch = nk > 1 and out_dtype != F32
    if norm_fwd is not None or norm_bwd is not None:
        assert tn == N and not use_scratch and out_dtype == F32

    hs = list(exchange or gather or [])
    n_x = len(hs)
    grid = (M // tm, N // tn, nk)
    a_spec = pl.BlockSpec((tk, tm), lambda i, j, k: (k, i)) if ta else pl.BlockSpec((tm, tk), lambda i, j, k: (i, k))
    b_spec = pl.BlockSpec((tn, tk), lambda i, j, k: (j, k)) if tb else pl.BlockSpec((tk, tn), lambda i, j, k: (k, j))
    o_spec = pl.BlockSpec((tm, tn), lambda i, j, k: (i, j))
    row_vec = pl.BlockSpec((1, N), lambda i, j, k: (0, 0))
    hbm = pl.BlockSpec(memory_space=pl.ANY)
    extra_in = [(add, o_spec)] if add is not None else []
    extra_out = []
    if norm_fwd is not None:
        extra_in += [(norm_fwd, row_vec)]
        extra_out += [(jax.ShapeDtypeStruct((M, N), BF16), o_spec)]
    if norm_bwd is not None:
        extra_in += [(norm_bwd[0], o_spec), (norm_bwd[1], row_vec), (norm_bwd[2], o_spec)]
        extra_out += [(jax.ShapeDtypeStruct((1, N), F32), row_vec)]
    if gather is not None:
        ride_shapes, ride_scratch = [jax.ShapeDtypeStruct((N_DEV,) + h.shape, h.dtype) for h in hs], _gather_scratch(n_x)
    else:
        ride_shapes, ride_scratch = _chip_exchange_shapes(hs), (_chip_exchange_scratch(n_x) if n_x else [])
    n_ex_in, n_ex_out = len(extra_in), len(extra_out)

    def body(*refs):
        refs = list(refs)
        a_ref, b_ref = refs[:2]
        ex_in = refs[2:2 + n_ex_in]
        ride_in = refs[2 + n_ex_in:2 + n_ex_in + n_x]
        base = 2 + n_ex_in + n_x
        o_ref = refs[base]
        ex_out = refs[base + 1:base + 1 + n_ex_out]
        ride_out = refs[base + 1 + n_ex_out:base + 1 + n_ex_out + n_x]
        scratch = refs[base + 1 + n_ex_out + n_x:]
        at = [pl.program_id(d) for d in range(3)]
        k = at[2]
        if n_x:
            steps = _gather_steps if gather is not None else _chip_exchange_steps
            ride = steps(ride_in, ride_out, *scratch[-len(ride_scratch):])

            @pl.when(jnp.logical_and(at[0] == 0, jnp.logical_and(at[1] == 0, at[2] == 0)))
            def _():
                ride[0]()
        p = _dot(a_ref[...].astype(BF16), b_ref[...].astype(BF16), 0 if ta else 1, 1 if tb else 0)

        def finish(r):
            ins = list(ex_in)
            outs = list(ex_out)
            if add is not None:
                r = r + ins.pop(0)[...].astype(F32)
            if norm_fwd is not None:
                g_ref = ins.pop(0)
                rstd = lax.rsqrt(jnp.mean(r * r, axis=-1, keepdims=True) + EPS)
                outs.pop(0)[...] = (r * rstd * g_ref[...]).astype(BF16)
            if norm_bwd is not None:
                x_ref, g_ref, dres_ref = ins.pop(0), ins.pop(0), ins.pop(0)
                dg_ref = outs.pop(0)
                xv = x_ref[...]
                rstd = lax.rsqrt(jnp.mean(xv * xv, axis=-1, keepdims=True) + EPS)
                xh = xv * rstd
                part = jnp.sum(r * xh, axis=0, keepdims=True)

                @pl.when(at[0] == 0)
                def _():
                    dg_ref[...] = part

                @pl.when(at[0] > 0)
                def _():
                    dg_ref[...] += part

                dxh = r * g_ref[...]
                r = rstd * (dxh - xh * jnp.mean(dxh * xh, axis=-1, keepdims=True)) + dres_ref[...]
            o_ref[...] = r.astype(out_dtype)

        if nk == 1:
            finish(p)
        elif not use_scratch:
            @pl.when(k == 0)
            def _():
                o_ref[...] = p

            @pl.when(jnp.logical_and(k > 0, k < nk - 1))
            def _():
                o_ref[...] += p

            @pl.when(k == nk - 1)
            def _():
                finish(o_ref[...] + p)
        else:
            acc_ref = scratch[0]

            @pl.when(k == 0)
            def _():
                acc_ref[...] = p

            @pl.when(k > 0)
            def _():
                acc_ref[...] += p

            @pl.when(k == nk - 1)
            def _():
                finish(acc_ref[...])

        if n_x:
            @pl.when(jnp.logical_and(at[0] == grid[0] - 1, jnp.logical_and(at[1] == grid[1] - 1, at[2] == nk - 1)))
            def _():
                for step in ride[1:]:
                    step()

    res = _pallas(
        body, name=name, grid=grid,
        in_specs=[a_spec, b_spec] + [s for _, s in extra_in] + [hbm] * n_x,
        out_specs=[o_spec] + [s for _, s in extra_out] + [hbm] * n_x,
        out_shape=[jax.ShapeDtypeStruct((M, N), out_dtype)] + [s for s, _ in extra_out] + ride_shapes,
        scratch_shapes=([pltpu.VMEM((tm, tn), F32)] if use_scratch else []) + ride_scratch,
        compiler_params=_params(dimension_semantics=("arbitrary",) * 3),
    )(a, b, *[v for v, _ in extra_in], *hs)
    res = list(res)
    main = res[0] if n_ex_out == 0 else tuple(res[:1 + n_ex_out])
    return (main, res[1 + n_ex_out:]) if n_x else main


def _row_map(nb, reverse, seg):
    if reverse:
        return lambda i: (nb - 1 - i, seg)
    return lambda i: (i, seg)


def _row_call(body, *, name, T, ins, outs, acc_outs=(), tm=ROW_BLOCK, reverse=False):
    tm = min(tm, T)
    nb = T // tm
    in_specs, args = [], []
    for arr, how in ins:
        args.append(arr)
        if how is True:
            in_specs.append(pl.BlockSpec((tm, arr.shape[1]), _row_map(nb, reverse, 0)))
        elif how is False:
            in_specs.append(pl.BlockSpec(arr.shape, lambda i, _n=arr.ndim: (0,) * _n))
        else:
            in_specs.append(pl.BlockSpec((tm, SEG), _row_map(nb, reverse, how[0])))
    out_specs, out_shape = [], []
    for o in outs:
        c, dt = o[0], o[1]
        total, seg = o[2] if len(o) > 2 else (c, 0)
        out_specs.append(pl.BlockSpec((tm, c), _row_map(nb, reverse, seg)))
        out_shape.append(jax.ShapeDtypeStruct((T, total), dt))
    for shp, dt in acc_outs:
        out_specs.append(pl.BlockSpec(shp, lambda i, _n=len(shp): (0,) * _n))
        out_shape.append(jax.ShapeDtypeStruct(shp, dt))
    return _pallas(body, name=name, grid=(nb,), in_specs=in_specs, out_specs=out_specs, out_shape=out_shape,
                   compiler_params=_params(dimension_semantics=("arbitrary",)))(*args)


def _rms_fwd(x, g, *, name):
    T = x.shape[0]

    def body(x_ref, g_ref, h_ref):
        xv = x_ref[...]
        rstd = lax.rsqrt(jnp.mean(xv * xv, axis=-1, keepdims=True) + EPS)
        h_ref[...] = (xv * rstd * g_ref[...]).astype(BF16)

    return _row_call(body, name=name, T=T, ins=[(x, True), (g, False)], outs=[(D_MODEL, BF16)])[0]


def _merge_fwd(ya, yb, wa, wb, zg):
    def body(ya_ref, yb_ref, wa_ref, wb_ref, zg_ref, m_ref, ua_ref, ub_ref):
        ua = _dot(ya_ref[...].astype(BF16), wa_ref[...], 1, 0)
        ub = _dot(yb_ref[...].astype(BF16), wb_ref[...], 1, 0)
        ga = _sigmoid(zg_ref[:, :D_MODEL])
        gb = _sigmoid(zg_ref[:, D_MODEL:])
        m_ref[...] = (ga * ua + gb * ub).astype(BF16)
        ua_ref[...] = ua.astype(BF16)
        ub_ref[...] = ub.astype(BF16)

    return _row_call(body, name="merge_fwd", T=ya.shape[0],
                     ins=[(ya, True), (yb, True), (wa, False), (wb, False), (zg, (2,))],
                     outs=[(D_MODEL, BF16)] * 3)


def _merge_bwd(dx1, w_out, wa, wb, ua, ub, zg):
    def body(dx_ref, wo_ref, wa_ref, wb_ref, ua_ref, ub_ref, zg_ref, dua_ref, dub_ref, dzg_ref, dya_ref, dyb_ref):
        dmv = _dot(dx_ref[...].astype(BF16), wo_ref[...], 1, 1)
        ga = _sigmoid(zg_ref[:, :D_MODEL])
        gb = _sigmoid(zg_ref[:, D_MODEL:])
        dua = (dmv * ga).astype(BF16)
        dub = (dmv * gb).astype(BF16)
        dua_ref[...] = dua
        dub_ref[...] = dub
        dzg_ref[:, :D_MODEL] = (dmv * ua_ref[...].astype(F32) * ga * (1.0 - ga)).astype(BF16)
        dzg_ref[:, D_MODEL:] = (dmv * ub_ref[...].astype(F32) * gb * (1.0 - gb)).astype(BF16)
        dya_ref[...] = _dot(dua, wa_ref[...], 1, 1)
        dyb_ref[...] = _dot(dub, wb_ref[...], 1, 1)

    return _row_call(body, name="merge_bwd", T=dx1.shape[0],
                     ins=[(dx1, True), (w_out, False), (wa, False), (wb, False), (ua, True), (ub, True), (zg, (2,))],
                     outs=[(D_MODEL, BF16), (D_MODEL, BF16), (SEG, BF16, (IN_PAD, 2)), (HG_W, F32), (FOX_W, F32)])


def _swiglu_fwd(hf, w_gate, w_up):
    T, D = hf.shape
    F = w_gate.shape[0]
    tm, tn = _pick(T, 2 * ROW_BLOCK), _pick(F, 768)

    def body(h_ref, wg_ref, wu_ref, a_ref, b_ref, o_ref):
        hv = h_ref[...]
        a_b = _dot(hv, wg_ref[...], 1, 1).astype(BF16)
        b_b = _dot(hv, wu_ref[...], 1, 1).astype(BF16)
        a_ref[...] = a_b
        b_ref[...] = b_b
        av = a_b.astype(F32)
        o_ref[...] = (av * _sigmoid(av) * b_b.astype(F32)).astype(BF16)

    tile = pl.BlockSpec((tm, tn), lambda i, j: (i, j))
    wcol = pl.BlockSpec((tn, D), lambda i, j: (j, 0))
    return _pallas(
        body, name="swiglu_fwd", grid=(T // tm, F // tn),
        in_specs=[pl.BlockSpec((tm, D), lambda i, j: (i, 0)), wcol, wcol],
        out_specs=[tile] * 3, out_shape=[jax.ShapeDtypeStruct((T, F), BF16)] * 3,
        compiler_params=_params(dimension_semantics=("arbitrary",) * 2),
    )(hf, w_gate, w_up)


def _swiglu_bwd(dx, w_down, a, b):
    T, D = dx.shape
    F = w_down.shape[0]
    tm, tn = _pick(T, 2 * ROW_BLOCK), _pick(F, 768)

    def body(dx_ref, w_ref, a_ref, b_ref, da_ref, db_ref):
        dact = _dot(dx_ref[...].astype(BF16), w_ref[...], 1, 1)
        av = a_ref[...].astype(F32)
        bv = b_ref[...].astype(F32)
        sg = _sigmoid(av)
        da_ref[...] = (dact * bv * sg * (1.0 + av * (1.0 - sg))).astype(BF16)
        db_ref[...] = (dact * av * sg).astype(BF16)

    tile = pl.BlockSpec((tm, tn), lambda i, j: (i, j))
    return _pallas(
        body, name="swiglu_bwd", grid=(T // tm, F // tn),
        in_specs=[pl.BlockSpec((tm, D), lambda i, j: (i, 0)), pl.BlockSpec((tn, D), lambda i, j: (j, 0)), tile, tile],
        out_specs=[tile, tile], out_shape=[jax.ShapeDtypeStruct((T, F), BF16)] * 2,
        compiler_params=_params(dimension_semantics=("arbitrary",) * 2),
    )(dx, w_down, a, b)


def _ple_loss(x2, p, g, w_gate, w_proj, tgt):
    def body(x_ref, p_ref, g_ref, wg_ref, wp_ref, t_ref, hp_ref, dy_ref, dsp_ref, dpp_ref, loss_ref):
        xv = x_ref[...]
        rstd = lax.rsqrt(jnp.mean(xv * xv, axis=-1, keepdims=True) + EPS)
        hp = (xv * rstd * g_ref[...]).astype(BF16)
        hp_ref[...] = hp
        gp = _sigmoid(_dot(hp, wg_ref[...], 1, 0))
        ppv = _dot(p_ref[...].astype(BF16), wp_ref[...], 1, 0)
        err = xv + gp * ppv - t_ref[...]
        part = 0.5 * jnp.sum(jnp.mean(err * err, axis=-1, keepdims=True), axis=0, keepdims=True)
        part = jnp.broadcast_to(part, loss_ref.shape)

        @pl.when(pl.program_id(0) == 0)
        def _():
            loss_ref[...] = part

        @pl.when(pl.program_id(0) > 0)
        def _():
            loss_ref[...] += part

        dy = err * (1.0 / D_MODEL)
        dy_ref[...] = dy
        dsp_ref[...] = (dy * ppv * gp * (1.0 - gp)).astype(BF16)
        dpp_ref[...] = (dy * gp).astype(BF16)

    return _row_call(body, name="ple_loss", T=x2.shape[0],
                     ins=[(x2, True), (p, True), (g, False), (w_gate, False), (w_proj, False), (tgt, True)],
                     outs=[(D_MODEL, BF16), (D_MODEL, F32), (D_MODEL, BF16), (D_MODEL, BF16)],
                     acc_outs=[((8, LANES), F32)])


def _hg_consts():
    C = HG_CHUNK
    r, c = _iota((C, C), 0), _iota((C, C), 1)
    tri = (c <= r)
    same = (r // HG_SUB) == (c // HG_SUB)
    return tri, (tri & same)


def _hg_chunk_fwd(q, f, lb, tri_b, sub_b):
    sgq = _sigmoid(q)
    qt = q * sgq
    sg = _sigmoid(f)
    fg = lb + (1.0 - lb) * sg
    kf = (1.0 - lb) * (1.0 - sg)
    logf = jnp.log(fg)
    b = _split_dot(tri_b, logf, 1, 0)
    w = _split_dot(sub_b, logf, 1, 0)
    return sgq, qt, sg, fg, kf, b, w


def _hg_scores(qs_b, kf, b, row):
    C, S = HG_CHUNK, HG_SUB
    parts, ks = [], []
    for blk in range(C // S):
        ref = jnp.zeros_like(b[0:1]) if blk == 0 else b[blk * S - 1:blk * S]
        e = jnp.exp(jnp.minimum(ref - b, EXP_CLAMP))
        e = jnp.where(row < (blk + 1) * S, e, 0.0)
        k_b = (kf * e).astype(BF16)
        ks.append((e, k_b))
        parts.append(_dot(qs_b[blk * S:(blk + 1) * S], k_b, 1, 1))
    return jnp.concatenate(parts, axis=0), ks


def _hgrn_fwd(z, lb_logits, gain, blocks):
    T = z.shape[0]
    RB = min(ROW_BLOCK, T)
    nb, cpb = T // RB, RB // HG_CHUNK
    C, DK = HG_CHUNK, HG_DK
    n = len(blocks)

    def body(*refs):
        z_ref, lg_ref, g_ref = refs[:3]
        o_ref, y_ref, st_ref = refs[3 + n:6 + n]
        s_ref = refs[6 + 2 * n]
        g_start, g_forward, g_finish = _gather_steps(refs[3:3 + n], refs[6 + n:6 + 2 * n], *refs[7 + 2 * n:])

        @pl.when(pl.program_id(0) == 0)
        def _():
            s_ref[...] = jnp.zeros_like(s_ref)
            g_start()

        lg = lg_ref[...]
        lb_all = 1.0 / (1.0 + jnp.exp(lg[1:2] - lg[0:1]))
        gain_v = g_ref[...]
        tri, sub = _hg_consts()
        tri_b, sub_b = tri.astype(BF16), sub.astype(BF16)
        row = _iota((C, DK), 0)

        def chunk(ci, carry):
            r0 = pl.multiple_of(ci * C, C)
            rows = pl.ds(r0, C)
            _, qt, _, _, kf_all, b_all, w = _hg_chunk_fwd(z_ref[rows, 0:HG_W], z_ref[rows, HG_W:2 * HG_W], lb_all,
                                                          tri_b, sub_b)
            qs_all = (qt * jnp.exp(w)).astype(BF16)
            qd_all = (qt * jnp.exp(b_all)).astype(BF16)
            bl_all = b_all[C - 1:C]
            kd_all = (kf_all * jnp.exp(bl_all - b_all)).astype(BF16)
            ebl_all = jnp.exp(bl_all)
            v_all = z_ref[rows, 2 * HG_W:3 * HG_W].astype(BF16)
            g_all = z_ref[rows, 3 * HG_W:4 * HG_W]
            gate_all = g_all * _sigmoid(g_all)
            for h in range(HG_HEADS):
                cs = slice(h * DK, (h + 1) * DK)
                st = s_ref[h]
                st_ref[pl.ds(pl.multiple_of((ci * HG_HEADS + h) * DK, DK), DK), :] = st
                v_b = v_all[:, cs]
                a, _ = _hg_scores(qs_all[:, cs], kf_all[:, cs], b_all[:, cs], row)
                a = jnp.where(tri, a, 0.0)
                o = _dot(qd_all[:, cs], st.astype(BF16), 1, 1) + _dot(a.astype(BF16), v_b, 1, 0)
                s_ref[h] = st * ebl_all[:, cs] + _dot(v_b, kd_all[:, cs], 0, 0)
                o_ref[rows, cs] = o
                rstd = lax.rsqrt(jnp.mean(o * o, axis=-1, keepdims=True) + EPS)
                y_ref[rows, cs] = (o * rstd * gain_v * gate_all[:, cs]).astype(BF16)
            return carry

        lax.fori_loop(0, cpb, chunk, 0, unroll=2)

        @pl.when(pl.program_id(0) == nb - 1)
        def _():
            g_forward()
            g_finish()

    hbm = pl.BlockSpec(memory_space=pl.ANY)
    res = _pallas(
        body, name="hgrn_fwd", grid=(nb,),
        in_specs=[pl.BlockSpec((RB, HG_COLS), lambda i: (i, 0)), pl.BlockSpec((2, HG_W), lambda i: (0, 0)),
                  pl.BlockSpec((1, DK), lambda i: (0, 0))] + [hbm] * n,
        out_specs=[pl.BlockSpec((RB, HG_W), lambda i: (i, 0)), pl.BlockSpec((RB, HG_W), lambda i: (i, 0)),
                   pl.BlockSpec((cpb * HG_HEADS * DK, DK), lambda i: (i, 0))] + [hbm] * n,
        out_shape=[jax.ShapeDtypeStruct((T, HG_W), F32), jax.ShapeDtypeStruct((T, HG_W), BF16),
                   jax.ShapeDtypeStruct((T // C * HG_HEADS * DK, DK), F32)]
        + [jax.ShapeDtypeStruct((N_DEV,) + b.shape, b.dtype) for b in blocks],
        scratch_shapes=[pltpu.VMEM((HG_HEADS, DK, DK), F32)] + _gather_scratch(n),
        compiler_params=_params(dimension_semantics=("arbitrary",)),
    )(z, lb_logits, gain, *blocks)
    return res[0], res[1], res[2], res[3:]


def _hgrn_bwd(z, o_raw, dy, states, lb_logits, gain, dz_buf):
    T = z.shape[0]
    RB = min(ROW_BLOCK, T)
    nb, cpb = T // RB, RB // HG_CHUNK
    C, DK, S = HG_CHUNK, HG_DK, HG_SUB

    def body(z_ref, o_ref, dy_ref, st_ref, lg_ref, g_ref, _buf_ref, dz_ref, dlg_ref, dg_ref, ds_ref, dlb_ref):
        step = pl.program_id(0)

        @pl.when(step == 0)
        def _():
            ds_ref[...] = jnp.zeros_like(ds_ref)
            dlb_ref[...] = jnp.zeros_like(dlb_ref)
            dg_ref[...] = jnp.zeros_like(dg_ref)

        lg = lg_ref[...]
        lb_all = 1.0 / (1.0 + jnp.exp(lg[1:2] - lg[0:1]))
        gain_v = g_ref[...]
        tri, sub = _hg_consts()
        tri_b, sub_b = tri.astype(BF16), sub.astype(BF16)
        row = _iota((C, DK), 0)

        def chunk(cj, carry):
            ci = cpb - 1 - cj
            r0 = pl.multiple_of(ci * C, C)
            rows = pl.ds(r0, C)
            q_all = z_ref[rows, 0:HG_W]
            g_all = z_ref[rows, 3 * HG_W:4 * HG_W]
            sgq_all, qt_all, sg_all, fg_all, kf_all, b_all, w_all = _hg_chunk_fwd(
                q_all, z_ref[rows, HG_W:2 * HG_W], lb_all, tri_b, sub_b)
            ew_all = jnp.exp(w_all)
            eb_all = jnp.exp(b_all)
            bl_all = b_all[C - 1:C]
            ebl_all = jnp.exp(bl_all)
            ekd_all = jnp.exp(bl_all - b_all)
            qs_all = (qt_all * ew_all).astype(BF16)
            qd_all = (qt_all * eb_all).astype(BF16)
            kd_all = (kf_all * ekd_all).astype(BF16)
            v_all = z_ref[rows, 2 * HG_W:3 * HG_W].astype(BF16)
            sgg_all = _sigmoid(g_all)
            t1_all = dy_ref[rows, :] * (g_all * sgg_all)
            db_heads, dqt_heads, dkf_heads, dv_heads, n_heads = [], [], [], [], []
            for h in range(HG_HEADS):
                cs = slice(h * DK, (h + 1) * DK)
                kf, b, ew, eb, ebl, ekd = kf_all[:, cs], b_all[:, cs], ew_all[:, cs], eb_all[:, cs], ebl_all[:, cs], \
                    ekd_all[:, cs]
                qs_b, qd_b, kd_b, v_b = qs_all[:, cs], qd_all[:, cs], kd_all[:, cs], v_all[:, cs]
                st = st_ref[pl.ds(pl.multiple_of((ci * HG_HEADS + h) * DK, DK), DK), :]
                dst = ds_ref[h]
                o = o_ref[rows, cs]
                rstd = lax.rsqrt(jnp.mean(o * o, axis=-1, keepdims=True) + EPS)
                n = o * rstd
                n_heads.append(n)
                t1 = t1_all[:, cs]
                dg_ref[...] += jnp.sum(t1 * n, axis=0, keepdims=True)
                dn = t1 * gain_v
                do = rstd * (dn - n * jnp.mean(dn * n, axis=-1, keepdims=True))
                do_b = do.astype(BF16)
                a, ks = _hg_scores(qs_b, kf, b, row)
                a = jnp.where(tri, a, 0.0)
                dst_b = dst.astype(BF16)
                dqd = _dot(do_b, st.astype(BF16), 1, 0)
                da = jnp.where(tri, _dot(do_b, v_b, 1, 1), 0.0)
                dv = _dot(a.astype(BF16), do_b, 0, 0) + _dot(kd_b, dst_b, 1, 1)
                dkd = _dot(v_b, dst_b, 1, 0)
                ds_ref[h] = dst * ebl + _dot(do_b, qd_b, 0, 0)
                dkd_kd = dkd * kd_b.astype(F32)
                dbl = ebl * jnp.sum(dst * st, axis=0, keepdims=True) + jnp.sum(dkd_kd, axis=0, keepdims=True)
                da_b = da.astype(BF16)
                dqs_parts = []
                dk_in = jnp.zeros((C, DK), F32)
                db_k = jnp.zeros((C, DK), F32)
                for blk in range(C // S):
                    e, k_b = ks[blk]
                    da_blk = da_b[blk * S:(blk + 1) * S]
                    dqs_parts.append(_dot(da_blk, k_b, 1, 0))
                    dks = _dot(da_blk, qs_b[blk * S:(blk + 1) * S], 0, 0)
                    dk_in = dk_in + dks * e
                    db_k = db_k + dks * k_b.astype(F32)
                dqs = jnp.concatenate(dqs_parts, axis=0)
                db = qs_b.astype(F32) * dqs - db_k + dqd * qd_b.astype(F32) - dkd_kd
                db_heads.append(db + jnp.where(row == C - 1, dbl, 0.0))
                dqt_heads.append(dqs * ew + dqd * eb)
                dkf_heads.append(dk_in + dkd * ekd)
                dv_heads.append(dv)
            dlogf = _split_dot(tri_b, jnp.concatenate(db_heads, axis=1), 0, 0)
            dfg = dlogf / fg_all - jnp.concatenate(dkf_heads, axis=1)
            dlb_ref[...] += jnp.sum(dfg * (1.0 - sg_all), axis=0, keepdims=True)
            dqt = jnp.concatenate(dqt_heads, axis=1)
            n_all = jnp.concatenate(n_heads, axis=1)
            gain_all = jnp.concatenate([gain_v] * HG_HEADS, axis=1)
            dz_ref[rows, 0:HG_W] = (dqt * sgq_all * (1.0 + q_all * (1.0 - sgq_all))).astype(BF16)
            dz_ref[rows, HG_W:2 * HG_W] = (dfg * (1.0 - lb_all) * sg_all * (1.0 - sg_all)).astype(BF16)
            dz_ref[rows, 2 * HG_W:3 * HG_W] = jnp.concatenate(dv_heads, axis=1).astype(BF16)
            dz_ref[rows, 3 * HG_W:4 * HG_W] = (dy_ref[rows, :] * n_all * gain_all * sgg_all
                                               * (1.0 + g_all * (1.0 - sgg_all))).astype(BF16)
            return carry

        lax.fori_loop(0, cpb, chunk, 0, unroll=2)

        @pl.when(step == nb - 1)
        def _():
            d0 = dlb_ref[...] * lb_all * (1.0 - lb_all)
            dlg_ref[0:1, :] = d0
            dlg_ref[1:2, :] = -d0

    rev = lambda i: (nb - 1 - i, 0)
    fix = lambda i: (0, 0)
    return _pallas(
        body, name="hgrn_bwd", grid=(nb,),
        in_specs=[pl.BlockSpec((RB, HG_COLS), rev), pl.BlockSpec((RB, HG_W), rev), pl.BlockSpec((RB, HG_W), rev),
                  pl.BlockSpec((cpb * HG_HEADS * DK, DK), rev), pl.BlockSpec((2, HG_W), fix),
                  pl.BlockSpec((1, DK), fix), pl.BlockSpec(memory_space=pl.ANY)],
        out_specs=[pl.BlockSpec((RB, HG_COLS), rev), pl.BlockSpec((2, HG_W), fix), pl.BlockSpec((1, DK), fix)],
        out_shape=[jax.ShapeDtypeStruct(dz_buf.shape, BF16), jax.ShapeDtypeStruct((2, HG_W), F32),
                   jax.ShapeDtypeStruct((1, DK), F32)],
        scratch_shapes=[pltpu.VMEM((HG_HEADS, DK, DK), F32), pltpu.VMEM((1, HG_W), F32)],
        input_output_aliases={6: 0},
        compiler_params=_params(dimension_semantics=("arbitrary",)),
    )(z, o_raw, dy, states, lb_logits, gain, dz_buf)


def _head_ones():
    r, c = _iota((FOX_W, FOX_W), 0), _iota((FOX_W, FOX_W), 1)
    return ((r // FOX_DH) == (c // FOX_DH)).astype(BF16)


def _causal_blocks(nq, *, by_query):
    if by_query:
        pairs = [(q, k) for q in range(nq) for k in range(q + 1)]
    else:
        pairs = [(q, k) for k in range(nq) for q in range(k, nq)]
    return (jnp.asarray([q for q, _ in pairs], jnp.int32), jnp.asarray([k for _, k in pairs], jnp.int32))


def _log_sigmoid(x):
    return jnp.minimum(x, 0.0) - jnp.log(1.0 + jnp.exp(-jnp.abs(x)))


def _fox_prep(z, bias, qg, kg):
    T = z.shape[0]
    tm = min(ROW_BLOCK, T)
    nb = T // tm

    def body(z_ref, b_ref, qg_ref, kg_ref, q_ref, k_ref, v_ref, qa_ref, ka_ref, carry_ref):
        @pl.when(pl.program_id(0) == 0)
        def _():
            carry_ref[...] = jnp.zeros_like(carry_ref)

        ones = _head_ones()
        normed = []
        for src, g_ref in ((0, qg_ref), (1, kg_ref)):
            xv = z_ref[:, src * FOX_W:(src + 1) * FOX_W]
            ms = _split_dot(ones, xv * xv, 1, 0, mat_first=False) * (1.0 / FOX_DH)
            normed.append(xv * lax.rsqrt(ms + EPS) * g_ref[...])
        qn, kn = normed
        q_ref[...] = (qn * FOX_DH ** -0.5).astype(BF16)
        k_b = kn.astype(BF16)
        k_ref[...] = k_b
        v_ref[...] = z_ref[:, 2 * FOX_W:3 * FOX_W].astype(BF16)
        logf = _log_sigmoid(z_ref[:, 3 * FOX_W:FOX_COLS] + b_ref[...])
        r, c = _iota((tm, tm), 0), _iota((tm, tm), 1)
        tri_b = (c <= r).astype(BF16)
        cum = _split_dot(tri_b, logf, 1, 0, terms=3) + carry_ref[...]
        carry_ref[...] = cum[tm - 1:tm]
        c2 = cum * LOG2E
        hi = c2.astype(BF16)
        rem = c2 - hi.astype(F32)
        mid = rem.astype(BF16)
        lo = (rem - mid.astype(F32)).astype(BF16)
        hrow, col = _iota((LANES, 2 * FOX_W), 0), _iota((LANES, 2 * FOX_W), 1)
        base = hrow * LANES + jnp.where(hrow % 2 == 0, FOX_DH, 0)
        placed = None
        for t, part in enumerate((hi, mid, lo)):
            place = jnp.logical_and(col == base + t, hrow < FOX_HEADS).astype(BF16)
            term = _dot(part, place, 1, 0)
            placed = term if placed is None else placed + term
        colw = _iota((tm, 2 * FOX_W), 1)
        head, lane = colw // LANES, colw % LANES
        own = (lane < FOX_DH) == (head % 2 == 0)
        other = jnp.where(head % 2 == 0, lane - FOX_DH, lane)
        ones_q = jnp.where(jnp.logical_and(other >= 0, other < 3), -1.0, 0.0)
        q2 = (qn * (FOX_DH ** -0.5 * LOG2E)).astype(BF16)
        q_exp = jnp.concatenate([q2[:, (h // 2) * LANES:(h // 2 + 1) * LANES] for h in range(FOX_HEADS)], axis=1)
        k_exp = jnp.concatenate([k_b[:, (h // 2) * LANES:(h // 2 + 1) * LANES] for h in range(FOX_HEADS)], axis=1)
        qa_ref[...] = jnp.where(own, q_exp, ones_q.astype(BF16))
        ka_ref[...] = jnp.where(own, k_exp, placed.astype(BF16))

    wide = pl.BlockSpec((tm, 2 * FOX_W), lambda i: (i, 0))
    return _pallas(
        body, name="fox_prep", grid=(nb,),
        in_specs=[pl.BlockSpec((tm, SEG), lambda i: (i, 1)), pl.BlockSpec((1, LANES), lambda i: (0, 0)),
                  pl.BlockSpec((1, FOX_W), lambda i: (0, 0)), pl.BlockSpec((1, FOX_W), lambda i: (0, 0))],
        out_specs=[pl.BlockSpec((tm, FOX_W), lambda i: (i, 0))] * 3 + [wide] * 2,
        out_shape=[jax.ShapeDtypeStruct((T, FOX_W), BF16)] * 3 + [jax.ShapeDtypeStruct((T, 2 * FOX_W), BF16)] * 2,
        scratch_shapes=[pltpu.VMEM((1, LANES), F32)],
        compiler_params=_params(dimension_semantics=("arbitrary",)),
    )(z, bias, qg, kg)


def _fox_fwd(qa, ka, vb, blocks):
    T = qa.shape[0]
    tq = min(ROW_BLOCK, T)
    nq = T // tq
    NEG = -1e30
    n = len(blocks)
    n_pairs = FOX_HEADS // 2

    n_in = 3
    q_of, k_of = _causal_blocks(nq, by_query=True)
    n_tri = len(q_of)

    def body(qt_ref, kt_ref, *refs):
        q_ref, k_ref, v_ref = refs[:n_in]
        o_ref, lse_ref = refs[n_in + n:n_in + n + 2]
        m_sc, l_sc, acc_sc = refs[n_in + 2 * n + 2:n_in + 2 * n + 5]
        pr, t = pl.program_id(0), pl.program_id(1)
        qi, ki = qt_ref[t], kt_ref[t]
        g_start, g_forward, g_finish = _gather_steps(
            refs[n_in:n_in + n], refs[n_in + n + 2:n_in + 2 * n + 2], *refs[n_in + 2 * n + 5:])

        @pl.when(jnp.logical_and(pr == 0, t == 0))
        def _():
            g_start()

        @pl.when(jnp.logical_and(pr == n_pairs // 2, t == 0))
        def _():
            g_forward()

        @pl.when(ki == 0)
        def _():
            m_sc[...] = jnp.full_like(m_sc, NEG)
            l_sc[...] = jnp.zeros_like(l_sc)
            acc_sc[...] = jnp.zeros_like(acc_sc)

        lane = _iota((tq, LANES), 1)

        def block(masked):
            vv = v_ref[...]
            for hh in range(2):
                hs = slice(hh * LANES, (hh + 1) * LANES)
                s = _dot(q_ref[:, hs], k_ref[:, hs], 1, 1)
                tiles = [s[:, j * LANES:(j + 1) * LANES] for j in range(tq // LANES)]
                if masked:
                    row, col = _iota((tq, LANES), 0), _iota((tq, LANES), 1)
                    tiles = [jnp.where(row >= col + j * LANES, t, NEG) for j, t in enumerate(tiles)]
                m_old = m_sc[hh]
                top = jnp.broadcast_to(jnp.max(functools.reduce(jnp.maximum, tiles), axis=-1, keepdims=True),
                                       (tq, LANES))
                m_new = jnp.maximum(m_old, top)
                alpha = jnp.exp2(m_old - m_new)
                ps = [jnp.exp2(t - m_new) for t in tiles]
                l_sc[hh] = alpha * l_sc[hh] + functools.reduce(jnp.add, ps)
                m_sc[hh] = m_new
                p_b = jnp.concatenate([p.astype(BF16) for p in ps], axis=1)
                acc_sc[hh] = alpha * acc_sc[hh] + _dot(p_b, vv, 1, 0)

        @pl.when(ki < qi)
        def _():
            block(False)

        @pl.when(ki == qi)
        def _():
            block(True)
            l0 = jnp.sum(l_sc[0], axis=-1, keepdims=True)
            l1 = jnp.sum(l_sc[1], axis=-1, keepdims=True)
            o_ref[...] = jnp.where(lane < FOX_DH, acc_sc[0] * (1.0 / l0), acc_sc[1] * (1.0 / l1))
            lse_ref[:, :LANES] = m_sc[0] + jnp.log2(l0)
            lse_ref[:, LANES:] = m_sc[1] + jnp.log2(l1)

        @pl.when(jnp.logical_and(pr == n_pairs - 1, t == n_tri - 1))
        def _():
            g_finish()

    qmap = lambda p, t, qt, kt: (qt[t], p)
    kmap = lambda p, t, qt, kt: (kt[t], p)
    hbm = pl.BlockSpec(memory_space=pl.ANY)
    grid_spec = pltpu.PrefetchScalarGridSpec(
        num_scalar_prefetch=2, grid=(n_pairs, n_tri),
        in_specs=[pl.BlockSpec((tq, 2 * LANES), qmap), pl.BlockSpec((tq, 2 * LANES), kmap),
                  pl.BlockSpec((tq, LANES), kmap)] + [hbm] * n,
        out_specs=[pl.BlockSpec((tq, LANES), qmap), pl.BlockSpec((tq, 2 * LANES), qmap)] + [hbm] * n,
        scratch_shapes=[pltpu.VMEM((2, tq, LANES), F32)] * 3 + _gather_scratch(n))
    res = _pallas(
        body, name="fox_fwd", grid_spec=grid_spec,
        out_shape=[jax.ShapeDtypeStruct((T, FOX_W), F32), jax.ShapeDtypeStruct((T, 2 * FOX_W), F32)]
        + [jax.ShapeDtypeStruct((N_DEV,) + b.shape, b.dtype) for b in blocks],
        compiler_params=_params(dimension_semantics=("arbitrary",) * 2),
    )(q_of, k_of, qa, ka, vb, *blocks)
    return res[0], res[1], res[2:]


def _fox_bwd(qs, kn, vb, qa, ka, o, do, lse, hs):
    T = qs.shape[0]
    tq = min(ROW_BLOCK, T)
    nq = T // tq
    n = len(hs)
    n_pairs = FOX_HEADS // 2

    q_of, k_of = _causal_blocks(nq, by_query=False)
    n_tri = len(q_of)

    def body(qt_ref, kt_ref, *refs):
        q_ref, k_ref, v_ref, qa_ref, ka_ref, o_ref, do_ref, lse_ref = refs[:8]
        dq_ref, dk_ref, dv_ref, dcs_ref, drs_ref = refs[8 + n:13 + n]
        pr, t = pl.program_id(0), pl.program_id(1)
        qi, ki = qt_ref[t], kt_ref[t]
        x_start, x_finish = _chip_exchange_steps(refs[8:8 + n], refs[13 + n:13 + 2 * n], *refs[13 + 2 * n:])

        @pl.when(jnp.logical_and(pr == 0, t == 0))
        def _():
            x_start()
            drs_ref[...] = jnp.zeros_like(drs_ref)

        @pl.when(t == 0)
        def _():
            dq_ref[...] = jnp.zeros_like(dq_ref)

        @pl.when(qi == ki)
        def _():
            dk_ref[...] = jnp.zeros_like(dk_ref)
            dv_ref[...] = jnp.zeros_like(dv_ref)
            dcs_ref[...] = jnp.zeros_like(dcs_ref)

        def block(masked):
            lane = _iota((tq, LANES), 1)
            qv, kv, vv = q_ref[...], k_ref[...], v_ref[...]
            ov, dov = o_ref[...], do_ref[...]
            qrows = pl.ds(pl.multiple_of(qi * tq, tq), tq)
            dq_acc = jnp.zeros((tq, LANES), F32)
            dk_acc = jnp.zeros((tq, LANES), F32)
            dv_acc = jnp.zeros((tq, LANES), F32)
            dcs_acc = jnp.zeros((8, tq), F32)
            drs_acc = jnp.zeros((tq, LANES), F32)
            prod = dov * ov
            nt = tq // LANES
            for hh in range(2):
                in_head = (lane < FOX_DH) if hh == 0 else (lane >= FOX_DH)
                hs_ = slice(hh * LANES, (hh + 1) * LANES)
                zb = jnp.zeros_like(qv)
                qm = jnp.where(in_head, qv, zb)
                km = jnp.where(in_head, kv, zb)
                dom = jnp.where(in_head, dov, 0.0).astype(BF16)
                delta_b = jnp.broadcast_to(jnp.sum(jnp.where(in_head, prod, 0.0), axis=1, keepdims=True), (tq, LANES))
                lse_b = lse_ref[:, hs_]
                s = _dot(qa_ref[:, hs_], ka_ref[:, hs_], 1, 1)
                dp = _dot(dom, vv, 1, 1)
                p_tiles, ds_tiles, col_tiles = [], [], []
                row_part = jnp.zeros((tq, LANES), F32)
                for j in range(nt):
                    js = slice(j * LANES, (j + 1) * LANES)
                    p = jnp.exp2(s[:, js] - lse_b)
                    if masked:
                        p = jnp.where(_iota((tq, LANES), 0) >= _iota((tq, LANES), 1) + j * LANES, p, 0.0)
                    ds = p * (dp[:, js] - delta_b)
                    p_tiles.append(p.astype(BF16))
                    ds_tiles.append(ds.astype(BF16))
                    col_tiles.append(jnp.sum(ds, axis=0, keepdims=True))
                    row_part = row_part + ds
                p_b = jnp.concatenate(p_tiles, axis=1)
                ds_b = jnp.concatenate(ds_tiles, axis=1)
                dv_acc = dv_acc + _dot(p_b, dom, 0, 0)
                dq_acc = dq_acc + _dot(ds_b, km, 1, 0)
                dk_acc = dk_acc + _dot(ds_b, qm, 0, 0)
                dcs_acc = dcs_acc + jnp.where(_iota((8, tq), 0) == hh, jnp.concatenate(col_tiles, axis=1), 0.0)
                rowsum = jnp.sum(row_part, axis=1, keepdims=True)
                drs_acc = drs_acc + jnp.where(lane == 2 * pr + hh, rowsum, 0.0)
            drs_ref[qrows, :] += drs_acc
            dq_ref[qrows, :] += dq_acc
            dk_ref[...] += dk_acc
            dv_ref[...] += dv_acc
            dcs_ref[0] += dcs_acc

        @pl.when(qi > ki)
        def _():
            block(False)

        @pl.when(qi == ki)
        def _():
            block(True)

        @pl.when(jnp.logical_and(pr == n_pairs - 1, t == n_tri - 1))
        def _():
            x_finish()

    qmap = lambda p, t, qt, kt: (qt[t], p)
    kmap = lambda p, t, qt, kt: (kt[t], p)
    hbm = pl.BlockSpec(memory_space=pl.ANY)
    grid_spec = pltpu.PrefetchScalarGridSpec(
        num_scalar_prefetch=2, grid=(n_pairs, n_tri),
        in_specs=[pl.BlockSpec((tq, LANES), qmap), pl.BlockSpec((tq, LANES), kmap), pl.BlockSpec((tq, LANES), kmap),
                  pl.BlockSpec((tq, 2 * LANES), qmap), pl.BlockSpec((tq, 2 * LANES), kmap),
                  pl.BlockSpec((tq, LANES), qmap), pl.BlockSpec((tq, LANES), qmap), pl.BlockSpec((tq, 2 * LANES), qmap)]
        + [hbm] * n,
        out_specs=[pl.BlockSpec((T, LANES), lambda p, t, qt, kt: (0, p)), pl.BlockSpec((tq, LANES), kmap),
                   pl.BlockSpec((tq, LANES), kmap), pl.BlockSpec((1, 8, tq), lambda p, t, qt, kt: (p, 0, kt[t])),
                   pl.BlockSpec((T, LANES), lambda p, t, qt, kt: (0, 0))] + [hbm] * n,
        scratch_shapes=_chip_exchange_scratch(n))
    res = _pallas(
        body, name="fox_bwd", grid_spec=grid_spec,
        out_shape=[jax.ShapeDtypeStruct((T, FOX_W), F32)] * 3
        + [jax.ShapeDtypeStruct((n_pairs, 8, T), F32), jax.ShapeDtypeStruct((T, LANES), F32)]
        + _chip_exchange_shapes(hs),
        compiler_params=_params(dimension_semantics=("arbitrary",) * 2),
    )(q_of, k_of, qs, kn, vb, qa, ka, o, do, lse, *hs)
    res = list(res)
    return res[:5] + [res[5:]]


def _fox_post(z, dq, dk, dv, dcs, drs, bias, qg, kg, dz_buf):
    T = z.shape[0]
    tm = min(ROW_BLOCK, T)
    nb = T // tm

    def body(z_ref, dq_ref, dk_ref, dv_ref, dcs_ref, drs_ref, b_ref, qg_ref, kg_ref, _buf_ref, dz_ref, dqg_ref, dkg_ref,
             db_ref, carry_ref):
        @pl.when(pl.program_id(0) == 0)
        def _():
            carry_ref[...] = jnp.zeros_like(carry_ref)
            dqg_ref[...] = jnp.zeros_like(dqg_ref)
            dkg_ref[...] = jnp.zeros_like(dkg_ref)
            db_ref[...] = jnp.zeros_like(db_ref)

        ones = _head_ones()
        for src, g_ref, d_ref, dg_ref, scale in ((0, qg_ref, dq_ref, dqg_ref, FOX_DH ** -0.5), (1, kg_ref, dk_ref, dkg_ref, 1.0)):
            xv = z_ref[:, src * FOX_W:(src + 1) * FOX_W]
            ms = _split_dot(ones, xv * xv, 1, 0, mat_first=False) * (1.0 / FOX_DH)
            rstd = lax.rsqrt(ms + EPS)
            xh = xv * rstd
            dn = d_ref[...] * scale
            dg_ref[...] += jnp.sum(dn * xh, axis=0, keepdims=True)
            dxh = dn * g_ref[...]
            mean = _split_dot(ones, dxh * xh, 1, 0, mat_first=False) * (1.0 / FOX_DH)
            dz_ref[:, src * FOX_W:(src + 1) * FOX_W] = (rstd * (dxh - xh * mean)).astype(BF16)
        dz_ref[:, 2 * FOX_W:3 * FOX_W] = dv_ref[...].astype(BF16)
        row8 = _iota((8, tm), 0)
        dct = jnp.zeros((8, tm), F32)
        for h in range(FOX_HEADS):
            src_row = dcs_ref[h // 2][h % 2:h % 2 + 1, :]
            dct = dct + jnp.where(row8 == h, src_row, 0.0)
        dct = drs_ref[...].T[0:8] - dct
        r, c = _iota((tm, tm), 0), _iota((tm, tm), 1)
        upper_b = (r >= c).astype(BF16)
        rc = _split_dot(upper_b, dct, 1, 0, mat_first=False) + carry_ref[...]
        carry_ref[...] = rc[:, 0:1]
        full = jnp.concatenate([rc, jnp.zeros((LANES - 8, tm), F32)], axis=0)
        dlogf = full.T
        xf = z_ref[:, 3 * FOX_W:FOX_COLS] + b_ref[...]
        df = dlogf * (1.0 - _sigmoid(xf))
        dz_ref[:, 3 * FOX_W:FOX_COLS] = df.astype(BF16)
        dz_ref[:, FOX_COLS:] = jnp.zeros((tm, SEG - FOX_COLS), BF16)
        db_ref[...] += jnp.sum(df, axis=0, keepdims=True)

    rev = lambda i: (nb - 1 - i, 0)
    fix2 = lambda i: (0, 0)
    return _pallas(
        body, name="fox_post", grid=(nb,),
        in_specs=[pl.BlockSpec((tm, SEG), lambda i: (nb - 1 - i, 1)), pl.BlockSpec((tm, FOX_W), rev),
                  pl.BlockSpec((tm, FOX_W), rev),
                  pl.BlockSpec((tm, FOX_W), rev), pl.BlockSpec((FOX_HEADS // 2, 8, tm), lambda i: (0, 0, nb - 1 - i)),
                  pl.BlockSpec((tm, LANES), rev),
                  pl.BlockSpec((1, LANES), fix2), pl.BlockSpec((1, FOX_W), fix2), pl.BlockSpec((1, FOX_W), fix2),
                  pl.BlockSpec(memory_space=pl.ANY)],
        out_specs=[pl.BlockSpec((tm, SEG), lambda i: (nb - 1 - i, 1)), pl.BlockSpec((1, FOX_W), fix2),
                   pl.BlockSpec((1, FOX_W), fix2), pl.BlockSpec((1, LANES), fix2)],
        out_shape=[jax.ShapeDtypeStruct(dz_buf.shape, BF16), jax.ShapeDtypeStruct((1, FOX_W), F32),
                   jax.ShapeDtypeStruct((1, FOX_W), F32), jax.ShapeDtypeStruct((1, LANES), F32)],
        scratch_shapes=[pltpu.VMEM((8, 1), F32)],
        input_output_aliases={9: 0},
        compiler_params=_params(dimension_semantics=("arbitrary",)),
    )(z, dq, dk, dv, dcs, drs, bias, qg, kg, dz_buf)


def _local_step(x, p, tgt, sm, W, rest_chunks, core):
    lbl, og, fb = sm["hg_lb_logits"], sm["hg_onorm_g"], sm["fox_f_bias"]
    fbias = jnp.pad(fb, ((0, 0), (0, LANES - FOX_HEADS)))
    qg = jnp.tile(sm["fox_q_norm_g"], (1, FOX_HEADS))
    kg = jnp.tile(sm["fox_k_norm_g"], (1, FOX_HEADS))

    h = _rms_fwd(x, sm["norm_mix_g"], name="rms_mix")
    rest = dict(zip(BIG[1:], rest_chunks))
    first, second = ["w_ffn_gate"], ["w_ffn_up"]
    third = [n for n in BIG[1:] if n not in first + second]
    z, got1 = _matmul(h, W["w_in"], tb=True, gather=[rest[n] for n in first], name="mm_z")
    o_raw, ya, states, got2 = _hgrn_fwd(z, lbl, og, [rest[n] for n in second])
    qs, kn, vb, qa, ka = _fox_prep(z, fbias, qg, kg)
    yb, lse, got3 = _fox_fwd(qa, ka, vb, [rest[n] for n in third])
    W = dict(W, **{n: _full_of_chunks(n, g)
                   for n, g in zip(first + second + third, list(got1) + list(got2) + list(got3))})
    merged, ua, ub = _merge_fwd(ya, yb, W["w_branch_a"], W["w_branch_b"], z)
    x1, hf = _matmul(merged, W["w_out"], add=x, norm_fwd=sm["norm_ffn_g"], name="mm_x1")
    a, b, act = _swiglu_fwd(hf, W["w_ffn_gate"], W["w_ffn_up"])
    x2 = _matmul(act, W["w_ffn_down"], add=x1, name="mm_x2")
    hp, dy, dsp, dpp, loss = _ple_loss(x2, p, sm["norm_ple_g"], W["w_ple_gate"], W["w_ple_proj"], tgt)

    G = {}
    G["w_ple_proj"] = _matmul(p, dpp, ta=True, out_dtype=BF16, name="mm_dw_ple_proj")
    G["w_ple_gate"] = _matmul(hp, dsp, ta=True, out_dtype=BF16, name="mm_dw_ple_gate")
    dx2, d_ple_g = _matmul(dsp, W["w_ple_gate"], tb=True, norm_bwd=(x2, sm["norm_ple_g"], dy), name="mm_dx2")
    G["w_ffn_down"] = _matmul(act, dx2, ta=True, out_dtype=BF16, name="mm_dw_ffn_down")
    da, db = _swiglu_bwd(dx2, W["w_ffn_down"], a, b)
    G["w_ffn_gate"] = _matmul(da, hf, ta=True, out_dtype=BF16, name="mm_dw_ffn_gate")
    G["w_ffn_up"] = _matmul(db, hf, ta=True, out_dtype=BF16, name="mm_dw_ffn_up")
    dhf = _matmul(da, W["w_ffn_gate"], name="mm_dhf_a")
    dx1, d_ffn_g = _matmul(db, W["w_ffn_up"], add=dhf, norm_bwd=(x1, sm["norm_ffn_g"], dx2), name="mm_dx1")
    G["w_out"] = _matmul(merged, dx1, ta=True, out_dtype=BF16, name="mm_dw_out")
    dua, dub, dz, dya, dyb = _merge_bwd(dx1, W["w_out"], W["w_branch_a"], W["w_branch_b"], ua, ub, z)
    G["w_branch_a"] = _matmul(ya, dua, ta=True, out_dtype=BF16, name="mm_dw_branch_a")
    G["w_branch_b"] = _matmul(yb, dub, ta=True, out_dtype=BF16, name="mm_dw_branch_b")
    hb_rest = _sibling_sums({n: G[n] for n in BIG[1:]}, core, tag="rest")
    dq, dk, dv, dcs, drs, got_rest = _fox_bwd(qs, kn, vb, qa, ka, yb, dyb, lse, hb_rest)
    dz, d_qg, d_kg, d_fb = _fox_post(z, dq, dk, dv, dcs, drs, fbias, qg, kg, dz)
    dz, d_lbl, d_og = _hgrn_bwd(z, o_raw, dya, states, lbl, og, dz)
    G["w_in"] = _matmul(dz, h, ta=True, out_dtype=BF16, name="mm_dw_in")
    hb_in = _sibling_sums({"w_in": G["w_in"]}, core, tag="w_in")
    (grad_x, d_mix_g), got_in = _matmul(dz, W["w_in"], exchange=hb_in,
                                        norm_bwd=(x, sm["norm_mix_g"], dx1), name="mm_dx")

    gs = {"norm_mix_g": d_mix_g, "hg_lb_logits": d_lbl, "hg_onorm_g": d_og, "fox_f_bias": d_fb[:, :FOX_HEADS],
          "fox_q_norm_g": d_qg.reshape(FOX_HEADS, FOX_DH).sum(0, keepdims=True),
          "fox_k_norm_g": d_kg.reshape(FOX_HEADS, FOX_DH).sum(0, keepdims=True),
          "norm_ffn_g": d_ffn_g, "norm_ple_g": d_ple_g}
    return loss, grad_x, gs, hb_in + hb_rest, list(got_in) + list(got_rest)


def _pack_rows(parts, total):
    buf = jnp.concatenate(parts, axis=-2)
    pad = total - buf.shape[-2]
    widths = [(0, 0)] * (buf.ndim - 2) + [(0, pad), (0, 0)]
    return jnp.pad(buf, widths)


def _chunk_of_shard(n, w):
    if n == "w_in":
        return jnp.pad(w, ((0, IN_SHARD_PAD - IN_SHARD), (0, 0)))
    if n in ("w_ffn_gate", "w_ffn_up", "w_ffn_down"):
        return jnp.pad(w, ((0, FF_SHARD_PAD - FF_SHARD), (0, 0)))
    return w


def _full_of_chunks(n, g):
    _, a, b = g.shape
    if n == "w_in":
        w = g[:, :IN_SHARD].reshape(IN_COLS, b)
        gap = jnp.zeros((SEG - FOX_LOGICAL, b), g.dtype)
        return jnp.concatenate([w[:HG_COLS + FOX_LOGICAL], gap, w[HG_COLS + FOX_LOGICAL:]], axis=0)
    if BIG_SHAPE[n][2] == 0:
        return g.reshape(N_DEV * a, b)
    return g.transpose(1, 0, 2).reshape(a, N_DEV * b)


def _chunks_of_full(n, g):
    if n == "w_in":
        w = jnp.concatenate([g[:HG_COLS + FOX_LOGICAL], g[2 * SEG:]], axis=0).reshape(N_DEV, IN_SHARD, g.shape[1])
        return jnp.pad(w, ((0, 0), (0, IN_SHARD_PAD - IN_SHARD), (0, 0)))
    if BIG_SHAPE[n][2] == 0:
        return g.reshape(N_DEV, g.shape[0] // N_DEV, g.shape[1])
    return g.reshape(g.shape[0], N_DEV, g.shape[1] // N_DEV).transpose(1, 0, 2)


def _pack_small(vals, loss_row=None):
    parts = [vals[n].reshape(SMALL_ROWS[n], -1) for n in SMALL]
    parts = [jnp.pad(v, ((0, 0), (0, LANES - v.shape[1]))) for v in parts]
    if loss_row is not None:
        parts.append(loss_row)
    return _pack_rows(parts, SMALL_TOTAL)


def _unpack_small(buf, like):
    out, r0 = {}, 0
    for n in SMALL:
        rows, size = SMALL_ROWS[n], like[n].size
        blk = buf[r0:r0 + rows]
        out[n] = (blk if size == rows * LANES else blk[:, :size]).reshape(like[n].shape)
        r0 += rows
    return out


def _place():
    return lax.axis_index("x"), lax.axis_index("y"), lax.axis_index("c")


def _gather_steps(x_refs, out_refs, send_sems, recv_sems, local_sems):
    n = len(x_refs)
    x, y, c = _place()
    me, sibling = (x, y, c), (x, y, 1 - c)
    chips = [(1 - x, y), (x, 1 - y), (1 - x, 1 - y)]

    def slot(i, px, py, pc):
        return out_refs[i].at[4 * px + 2 * py + pc]

    def copy(k, i, blk, to, own=False):
        return pltpu.make_async_remote_copy(
            src_ref=x_refs[i] if own else slot(i, *blk), dst_ref=slot(i, *blk),
            send_sem=send_sems.at[k, i], recv_sem=recv_sems.at[k, i], device_id=to, device_id_type=MESH)

    def mine():
        return [pltpu.make_async_copy(x_refs[i], slot(i, *me), local_sems.at[i]) for i in range(n)]

    def first():
        cps = [copy(0, i, me, sibling, own=True) for i in range(n)]
        return cps + [copy(1 + j, i, me, (*chip, c), own=True) for j, chip in enumerate(chips) for i in range(n)]

    def passed():
        return [copy(4 + j, i, (*chip, c), sibling) for j, chip in enumerate(chips) for i in range(n)]

    def start():
        for cp in mine() + first():
            cp.start()

    def forward():
        fws = passed()
        for j, chip in enumerate(chips):
            for i in range(n):
                copy(1 + j, i, (*chip, c), me).wait_recv()
                fws[j * n + i].start()

    def finish():
        for i in range(n):
            copy(0, i, sibling, me).wait_recv()
        for j, chip in enumerate(chips):
            for i in range(n):
                copy(4 + j, i, (*chip, 1 - c), me).wait_recv()
        for cp in first() + passed():
            cp.wait_send()
        for cp in mine():
            cp.wait()

    return start, forward, finish


def _gather_scratch(n):
    return [pltpu.SemaphoreType.DMA((7, n)), pltpu.SemaphoreType.DMA((7, n)), pltpu.SemaphoreType.DMA((n,))]


def _all_gather(blocks, *, name):
    n = len(blocks)

    def body(*refs):
        for step in _gather_steps(refs[:n], refs[n:2 * n], *refs[2 * n:]):
            step()

    hbm = pl.BlockSpec(memory_space=pl.ANY)
    return _pallas(
        body, name=name, out_shape=[jax.ShapeDtypeStruct((N_DEV,) + b.shape, b.dtype) for b in blocks],
        in_specs=[hbm] * n, out_specs=[hbm] * n, scratch_shapes=_gather_scratch(n),
    )(*blocks)


def _sibling_exchange(gs, *, name):
    n = len(gs)

    def body(*refs):
        g_refs, out_refs = refs[:n], refs[n:2 * n]
        send_sems, recv_sems = refs[2 * n:]
        x, y, c = _place()
        cps = [pltpu.make_async_remote_copy(
            src_ref=g_refs[i].at[:, pl.ds(1 - c, 1)], dst_ref=out_refs[i], send_sem=send_sems.at[i],
            recv_sem=recv_sems.at[i], device_id=(x, y, 1 - c), device_id_type=MESH) for i in range(n)]
        for cp in cps:
            cp.start()
        for cp in cps:
            cp.wait()

    hbm = pl.BlockSpec(memory_space=pl.ANY)
    return _pallas(
        body, name=name, out_shape=[jax.ShapeDtypeStruct((N_CHIP, 1) + g.shape[2:], g.dtype) for g in gs],
        in_specs=[hbm] * n, out_specs=[hbm] * n,
        scratch_shapes=[pltpu.SemaphoreType.DMA((n,)), pltpu.SemaphoreType.DMA((n,))],
    )(*gs)


def _chip_sum(g4s, gots, core, *, name):
    n = len(g4s)

    def body(c_ref, *refs):
        for g_ref, r_ref, h_ref in zip(refs[:n], refs[n:2 * n], refs[2 * n:]):
            h_ref[0] = (g_ref[0, 0].astype(F32) + r_ref[0, 0].astype(F32)).astype(BF16)

    shapes = [g.shape[2:] for g in g4s]
    grid_spec = pltpu.PrefetchScalarGridSpec(
        num_scalar_prefetch=1, grid=(N_CHIP,),
        in_specs=[pl.BlockSpec((1, 1) + s, lambda j, c: (j, c[0], 0, 0)) for s in shapes]
        + [pl.BlockSpec((1, 1) + s, lambda j, c: (j, 0, 0, 0)) for s in shapes],
        out_specs=[pl.BlockSpec((1,) + s, lambda j, c: (j, 0, 0)) for s in shapes])
    return list(_pallas(
        body, name=name, grid_spec=grid_spec, out_shape=[jax.ShapeDtypeStruct((N_CHIP,) + s, BF16) for s in shapes],
        compiler_params=_params(dimension_semantics=("arbitrary",)),
    )(core, *g4s, *gots))


def _chip_exchange_steps(h_refs, out_refs, send_sems, recv_sems):
    n = len(h_refs)
    x, y, c = _place()
    chips = [(1 - x, y), (x, 1 - y), (1 - x, 1 - y)]

    def copies():
        return [pltpu.make_async_remote_copy(
            src_ref=h_refs[i].at[2 * px + py], dst_ref=out_refs[i].at[k], send_sem=send_sems.at[k, i],
            recv_sem=recv_sems.at[k, i], device_id=(px, py, c), device_id_type=MESH)
            for k, (px, py) in enumerate(chips) for i in range(n)]

    def start():
        for cp in copies():
            cp.start()

    def finish():
        for cp in copies():
            cp.wait()

    return start, finish


def _chip_exchange_shapes(hs):
    return [jax.ShapeDtypeStruct((3,) + h.shape[1:], h.dtype) for h in hs]


def _chip_exchange_scratch(n):
    return [pltpu.SemaphoreType.DMA((3, n)), pltpu.SemaphoreType.DMA((3, n))]


def _sibling_sums(G, core, *, tag):
    g4 = []
    for n, g in G.items():
        gc = _chunks_of_full(n, g)
        g4.append(gc.reshape((N_CHIP, 2) + gc.shape[1:]))
    got = _sibling_exchange(g4, name="grads_to_sibling_" + tag)
    return _chip_sum(g4, got, core, name="chip_sum_" + tag)


def _adam_math(w, g, m, v):
    m = ADAM_B1 * m + (1.0 - ADAM_B1) * g
    v = ADAM_B2 * v + (1.0 - ADAM_B2) * (g * g)
    m_hat = m / (1.0 - ADAM_B1 ** ADAM_STEP)
    v_hat = v / (1.0 - ADAM_B2 ** ADAM_STEP)
    delta = -ADAM_LR * (m_hat / (jnp.sqrt(v_hat) + ADAM_EPS) + ADAM_WD * w)
    return delta, m, v


def _adam_shard(hb, got, chip, w, m, v, *, name):
    _, r, c = w.shape
    _, a, cb = hb.shape
    assert cb == c and c % LANES == 0, (hb.shape, w.shape)
    tc = _pick(c, 2 * LANES)

    def body(j_ref, h_ref, r_ref, w_ref, m_ref, v_ref, g_ref, d_ref, nm_ref, nv_ref):
        parts = [h_ref[0], r_ref[0], r_ref[1], r_ref[2]]
        g = None
        for part in parts:
            part = part[:r].astype(F32)
            g = part if g is None else g + part
        d, nm, nv = _adam_math(w_ref[0], g, m_ref[0], v_ref[0])
        g_ref[0] = g
        d_ref[0] = d
        nm_ref[0] = nm
        nv_ref[0] = nv

    blk = pl.BlockSpec((1, r, tc), lambda i, j: (0, 0, i))
    grid_spec = pltpu.PrefetchScalarGridSpec(
        num_scalar_prefetch=1, grid=(c // tc,),
        in_specs=[pl.BlockSpec((1, a, tc), lambda i, j: (j[0], 0, i)),
                  pl.BlockSpec((3, a, tc), lambda i, j: (0, 0, i)), blk, blk, blk],
        out_specs=[blk] * 4)
    return _pallas(
        body, name=name, grid_spec=grid_spec, out_shape=[jax.ShapeDtypeStruct((1, r, c), F32)] * 4,
        compiler_params=_params(dimension_semantics=("arbitrary",)),
    )(chip, hb, got, w, m, v)


def _small_all_reduce_adam(gs, w, m, v):
    def body(g_ref, w_ref, m_ref, v_ref, sum_ref, d_ref, nm_ref, nv_ref, gather, send_sems, recv_sems):
        x, y, c = _place()
        my = 4 * x + 2 * y + c
        gather[my] = g_ref[...]
        cps = []
        for k in range(1, N_DEV):
            to = (x ^ (k >> 2), y ^ ((k >> 1) & 1), c ^ (k & 1))
            cps.append(pltpu.make_async_remote_copy(
                src_ref=g_ref, dst_ref=gather.at[my], send_sem=send_sems.at[k - 1], recv_sem=recv_sems.at[k - 1],
                device_id=to, device_id_type=MESH))
        for cp in cps:
            cp.start()
        for cp in cps:
            cp.wait()
        total = gather[0]
        for d in range(1, N_DEV):
            total = total + gather[d]
        dlt, nm, nv = _adam_math(w_ref[...], total, m_ref[...], v_ref[...])
        sum_ref[...] = total
        d_ref[...] = dlt
        nm_ref[...] = nm
        nv_ref[...] = nv

    vm = pl.BlockSpec(memory_space=pltpu.VMEM)
    return _pallas(
        body, name="small_all_reduce_adam", out_shape=[jax.ShapeDtypeStruct((SMALL_TOTAL, LANES), F32)] * 4,
        in_specs=[vm] * 4, out_specs=[vm] * 4,
        scratch_shapes=[pltpu.VMEM((N_DEV, SMALL_TOTAL, LANES), F32), pltpu.SemaphoreType.DMA((7,)),
                        pltpu.SemaphoreType.DMA((7,))],
            )(gs, w, m, v)


def kernel(x, p, norm_mix_g, w_in, hg_lb_logits, hg_onorm_g, fox_f_bias, fox_q_norm_g, fox_k_norm_g, w_branch_a, w_branch_b, w_out, norm_ffn_g, w_ffn_gate, w_ffn_up, w_ffn_down, norm_ple_g, w_ple_gate, w_ple_proj, loss_target, m_norm_mix_g, m_w_in, m_hg_lb_logits, m_hg_onorm_g, m_fox_f_bias, m_fox_q_norm_g, m_fox_k_norm_g, m_w_branch_a, m_w_branch_b, m_w_out, m_norm_ffn_g, m_w_ffn_gate, m_w_ffn_up, m_w_ffn_down, m_norm_ple_g, m_w_ple_gate, m_w_ple_proj, v_norm_mix_g, v_w_in, v_hg_lb_logits, v_hg_onorm_g, v_fox_f_bias, v_fox_q_norm_g, v_fox_k_norm_g, v_w_branch_a, v_w_branch_b, v_w_out, v_norm_ffn_g, v_w_ffn_gate, v_w_ffn_up, v_w_ffn_down, v_norm_ple_g, v_w_ple_gate, v_w_ple_proj):
    args = dict(locals())
    wts = {n: args[n] for n in BIG + SMALL}
    mom = {n: args["m_" + n] for n in BIG + SMALL}
    var = {n: args["v_" + n] for n in BIG + SMALL}
    for group in (wts, mom, var):
        for n in TRANSPOSED:
            group[n] = jnp.swapaxes(group[n], 1, 2)
    sm = {n: wts[n] for n in SMALL}

    xi, yi, ci = _place()
    core = jnp.reshape(ci, (1,)).astype(jnp.int32)
    chip = jnp.reshape(2 * xi + yi, (1,)).astype(jnp.int32)
    chunks = [_chunk_of_shard(n, wts[n][0].astype(BF16)) for n in BIG]
    assert BIG[0] == "w_in"
    w_in_full = _full_of_chunks("w_in", _all_gather(chunks[:1], name="w_in_all_gather")[0])

    loss_blk, grad_x, gs, hb, got = _local_step(
        x[0], p[0, 0], loss_target[0], sm, {"w_in": w_in_full}, chunks[1:], core)

    g_big, d_big, nm_big, nv_big = {}, {}, {}, {}
    for n, h, r in zip(BIG, hb, got):
        res = _adam_shard(h, r, chip, wts[n], mom[n], var[n], name="adam_" + n)
        if n in TRANSPOSED:
            res = [jnp.swapaxes(t, 1, 2) for t in res]
        g_big[n], d_big[n], nm_big[n], nv_big[n] = res

    s_sum, s_d, s_nm, s_nv = _small_all_reduce_adam(
        _pack_small(gs, loss_blk[0:1]), _pack_small(sm), _pack_small({n: mom[n] for n in SMALL}),
        _pack_small({n: var[n] for n in SMALL}))
    loss = s_sum[LOSS_ROW, 0]
    g_small, d_small, nm_small, nv_small = (_unpack_small(t, sm) for t in (s_sum, s_d, s_nm, s_nv))

    order = ["norm_mix_g", "w_in", "hg_lb_logits", "hg_onorm_g", "fox_f_bias", "fox_q_norm_g", "fox_k_norm_g",
             "w_branch_a", "w_branch_b", "w_out", "norm_ffn_g", "w_ffn_gate", "w_ffn_up", "w_ffn_down", "norm_ple_g",
             "w_ple_gate", "w_ple_proj"]
    outs = [loss, grad_x[None]]
    for big, small in ((g_big, g_small), (d_big, d_small), (nm_big, nm_small), (nv_big, nv_small)):
        outs += [big[n] if n in big else small[n] for n in order]
    return tuple(outs)
```

```python
import functools

import jax
import jax.numpy as jnp
from jax import lax
from jax.experimental import pallas as pl
from jax.experimental.pallas import tpu as pltpu

F32 = jnp.float32
BF16 = jnp.bfloat16

D_MODEL = 1024
PLE_DIM = 256
HG_HEADS = 4
HG_DK = 128
HG_CHUNK = 64
HG_SUB = 16
HG_W = HG_HEADS * HG_DK
FOX_HEADS = 8
FOX_DH = 64
FOX_W = FOX_HEADS * FOX_DH
D_FF = 2816
EPS = 1e-6
N_DEV = 8
N_CHIP = 4
LANES = 128
FOX_COLS = 3 * FOX_W + LANES
HG_COLS = 4 * HG_W
GATE_COLS = 2 * D_MODEL
IN_COLS = HG_COLS + 3 * FOX_W + FOX_HEADS + GATE_COLS
FOX_LOGICAL = 3 * FOX_W + FOX_HEADS
SEG = 2048
IN_PAD = 3 * SEG
IN_SHARD = IN_COLS // N_DEV
IN_SHARD_PAD = 768
FF_SHARD = D_FF // N_DEV
FF_SHARD_PAD = 384
EXP_CLAMP = 80.0
LOG2E = 1.4426950408889634

ADAM_LR = 0.001
ADAM_B1 = 0.9
ADAM_B2 = 0.999
ADAM_EPS = 1e-08
ADAM_WD = 0.01
ADAM_STEP = 10

MESH = pl.DeviceIdType.MESH
VMEM_LIMIT = 56 * 1024 * 1024
ROW_BLOCK = 512

BIG = ["w_in", "w_branch_a", "w_branch_b", "w_out", "w_ffn_gate", "w_ffn_up", "w_ffn_down",
       "w_ple_gate", "w_ple_proj"]
TRANSPOSED = ("w_in", "w_ffn_gate", "w_ffn_up")
BIG_SHAPE = {
    "w_in": (IN_COLS, D_MODEL, 0), "w_branch_a": (HG_W, D_MODEL, 1), "w_branch_b": (FOX_W, D_MODEL, 1),
    "w_out": (D_MODEL, D_MODEL, 0), "w_ffn_gate": (D_FF, D_MODEL, 0), "w_ffn_up": (D_FF, D_MODEL, 0),
    "w_ffn_down": (D_FF, D_MODEL, 0), "w_ple_gate": (D_MODEL, D_MODEL, 0), "w_ple_proj": (PLE_DIM, D_MODEL, 1),
}

SMALL = ["norm_mix_g", "hg_lb_logits", "hg_onorm_g", "fox_f_bias", "fox_q_norm_g", "fox_k_norm_g",
         "norm_ffn_g", "norm_ple_g"]
SMALL_ROWS = {"norm_mix_g": 8, "hg_lb_logits": 8, "hg_onorm_g": 1, "fox_f_bias": 1, "fox_q_norm_g": 1,
              "fox_k_norm_g": 1, "norm_ffn_g": 8, "norm_ple_g": 8}
SMALL_TOTAL = 40
LOSS_ROW = 36


def _pallas(body, **kw):
    return pl.pallas_call(body, **kw)


def _params(**kw):
    return pltpu.CompilerParams(vmem_limit_bytes=VMEM_LIMIT, **kw)


def _pick(n, target):
    if n <= target:
        return n
    best = None
    for t in range(LANES, target + 1, LANES):
        if n % t == 0:
            best = t
    assert best is not None, (n, target)
    return best


def _dot(a, b, ca, cb):
    return lax.dot_general(a, b, (((ca,), (cb,)), ((), ())), preferred_element_type=F32)


def _split_dot(mat, x, ca, cb, terms=2, mat_first=True):
    acc = None
    rem = x
    for _ in range(terms):
        part = rem.astype(BF16)
        rem = rem - part.astype(F32)
        p = _dot(mat, part, ca, cb) if mat_first else _dot(part, mat, ca, cb)
        acc = p if acc is None else acc + p
    return acc


def _sigmoid(x):
    return 1.0 / (1.0 + jnp.exp(-x))


def _iota(shape, dim):
    return lax.broadcasted_iota(jnp.int32, shape, dim)


def _matmul(a, b, *, name, ta=False, tb=False, out_dtype=F32, add=None, exchange=None, gather=None,
            norm_fwd=None, norm_bwd=None):
    assert exchange is None or gather is None
    (K, M) = a.shape if ta else a.shape[::-1]
    (N, Kb) = b.shape if tb else b.shape[::-1]
    assert K == Kb, (a.shape, b.shape, ta, tb)
    if ta:
        tm, tn, tk = _pick(M, 2 * ROW_BLOCK), _pick(N, 2 * ROW_BLOCK), _pick(K, 4 * ROW_BLOCK)
    else:
        tm, tn, tk = _pick(M, 2 * ROW_BLOCK), _pick(N, 2048), _pick(K, 3072)
    if norm_bwd is not None:
        tm = _pick(M, ROW_BLOCK)
    nk = K // tk
    use_scratch = nk > 1 and out_dtype != F32
    if norm_fwd is not None or norm_bwd is not None:
        assert tn == N and not use_scratch and out_dtype == F32

    hs = list(exchange or gather or [])
    n_x = len(hs)
    grid = (M // tm, N // tn, nk)
    a_spec = pl.BlockSpec((tk, tm), lambda i, j, k: (k, i)) if ta else pl.BlockSpec((tm, tk), lambda i, j, k: (i, k))
    b_spec = pl.BlockSpec((tn, tk), lambda i, j, k: (j, k)) if tb else pl.BlockSpec((tk, tn), lambda i, j, k: (k, j))
    o_spec = pl.BlockSpec((tm, tn), lambda i, j, k: (i, j))
    row_vec = pl.BlockSpec((1, N), lambda i, j, k: (0, 0))
    hbm = pl.BlockSpec(memory_space=pl.ANY)
    extra_in = [(add, o_spec)] if add is not None else []
    extra_out = []
    if norm_fwd is not None:
        extra_in += [(norm_fwd, row_vec)]
        extra_out += [(jax.ShapeDtypeStruct((M, N), BF16), o_spec)]
    if norm_bwd is not None:
        extra_in += [(norm_bwd[0], o_spec), (norm_bwd[1], row_vec), (norm_bwd[2], o_spec)]
        extra_out += [(jax.ShapeDtypeStruct((1, N), F32), row_vec)]
    if gather is not None:
        ride_shapes, ride_scratch = [jax.ShapeDtypeStruct((N_DEV,) + h.shape, h.dtype) for h in hs], _gather_scratch(n_x)
    else:
        ride_shapes, ride_scratch = _chip_exchange_shapes(hs), (_chip_exchange_scratch(n_x) if n_x else [])
    n_ex_in, n_ex_out = len(extra_in), len(extra_out)

    def body(*refs):
        refs = list(refs)
        a_ref, b_ref = refs[:2]
        ex_in = refs[2:2 + n_ex_in]
        ride_in = refs[2 + n_ex_in:2 + n_ex_in + n_x]
        base = 2 + n_ex_in + n_x
        o_ref = refs[base]
        ex_out = refs[base + 1:base + 1 + n_ex_out]
        ride_out = refs[base + 1 + n_ex_out:base + 1 + n_ex_out + n_x]
        scratch = refs[base + 1 + n_ex_out + n_x:]
        at = [pl.program_id(d) for d in range(3)]
        k = at[2]
        if n_x:
            steps = _gather_steps if gather is not None else _chip_exchange_steps
            ride = steps(ride_in, ride_out, *scratch[-len(ride_scratch):])

            @pl.when(jnp.logical_and(at[0] == 0, jnp.logical_and(at[1] == 0, at[2] == 0)))
            def _():
                ride[0]()
        p = _dot(a_ref[...].astype(BF16), b_ref[...].astype(BF16), 0 if ta else 1, 1 if tb else 0)

        def finish(r):
            ins = list(ex_in)
            outs = list(ex_out)
            if add is not None:
                r = r + ins.pop(0)[...].astype(F32)
            if norm_fwd is not None:
                g_ref = ins.pop(0)
                rstd = lax.rsqrt(jnp.mean(r * r, axis=-1, keepdims=True) + EPS)
                outs.pop(0)[...] = (r * rstd * g_ref[...]).astype(BF16)
            if norm_bwd is not None:
                x_ref, g_ref, dres_ref = ins.pop(0), ins.pop(0), ins.pop(0)
                dg_ref = outs.pop(0)
                xv = x_ref[...]
                rstd = lax.rsqrt(jnp.mean(xv * xv, axis=-1, keepdims=True) + EPS)
                xh = xv * rstd
                part = jnp.sum(r * xh, axis=0, keepdims=True)

                @pl.when(at[0] == 0)
                def _():
                    dg_ref[...] = part

                @pl.when(at[0] > 0)
                def _():
                    dg_ref[...] += part

                dxh = r * g_ref[...]
                r = rstd * (dxh - xh * jnp.mean(dxh * xh, axis=-1, keepdims=True)) + dres_ref[...]
            o_ref[...] = r.astype(out_dtype)

        if nk == 1:
            finish(p)
        elif not use_scratch:
            @pl.when(k == 0)
            def _():
                o_ref[...] = p

            @pl.when(jnp.logical_and(k > 0, k < nk - 1))
            def _():
                o_ref[...] += p

            @pl.when(k == nk - 1)
            def _():
                finish(o_ref[...] + p)
        else:
            acc_ref = scratch[0]

            @pl.when(k == 0)
            def _():
                acc_ref[...] = p

            @pl.when(k > 0)
            def _():
                acc_ref[...] += p

            @pl.when(k == nk - 1)
            def _():
                finish(acc_ref[...])

        if n_x:
            @pl.when(jnp.logical_and(at[0] == grid[0] - 1, jnp.logical_and(at[1] == grid[1] - 1, at[2] == nk - 1)))
            def _():
                for step in ride[1:]:
                    step()

    res = _pallas(
        body, name=name, grid=grid,
        in_specs=[a_spec, b_spec] + [s for _, s in extra_in] + [hbm] * n_x,
        out_specs=[o_spec] + [s for _, s in extra_out] + [hbm] * n_x,
        out_shape=[jax.ShapeDtypeStruct((M, N), out_dtype)] + [s for s, _ in extra_out] + ride_shapes,
        scratch_shapes=([pltpu.VMEM((tm, tn), F32)] if use_scratch else []) + ride_scratch,
        compiler_params=_params(dimension_semantics=("arbitrary",) * 3),
    )(a, b, *[v for v, _ in extra_in], *hs)
    res = list(res)
    main = res[0] if n_ex_out == 0 else tuple(res[:1 + n_ex_out])
    return (main, res[1 + n_ex_out:]) if n_x else main


def _row_map(nb, reverse, seg):
    if reverse:
        return lambda i: (nb - 1 - i, seg)
    return lambda i: (i, seg)


def _row_call(body, *, name, T, ins, outs, acc_outs=(), tm=ROW_BLOCK, reverse=False):
    tm = min(tm, T)
    nb = T // tm
    in_specs, args = [], []
    for arr, how in ins:
        args.append(arr)
        if how is True:
            in_specs.append(pl.BlockSpec((tm, arr.shape[1]), _row_map(nb, reverse, 0)))
        elif how is False:
            in_specs.append(pl.BlockSpec(arr.shape, lambda i, _n=arr.ndim: (0,) * _n))
        else:
            in_specs.append(pl.BlockSpec((tm, SEG), _row_map(nb, reverse, how[0])))
    out_specs, out_shape = [], []
    for o in outs:
        c, dt = o[0], o[1]
        total, seg = o[2] if len(o) > 2 else (c, 0)
        out_specs.append(pl.BlockSpec((tm, c), _row_map(nb, reverse, seg)))
        out_shape.append(jax.ShapeDtypeStruct((T, total), dt))
    for shp, dt in acc_outs:
        out_specs.append(pl.BlockSpec(shp, lambda i, _n=len(shp): (0,) * _n))
        out_shape.append(jax.ShapeDtypeStruct(shp, dt))
    return _pallas(body, name=name, grid=(nb,), in_specs=in_specs, out_specs=out_specs, out_shape=out_shape,
                   compiler_params=_params(dimension_semantics=("arbitrary",)))(*args)


def _rms_fwd(x, g, *, name):
    T = x.shape[0]

    def body(x_ref, g_ref, h_ref):
        xv = x_ref[...]
        rstd = lax.rsqrt(jnp.mean(xv * xv, axis=-1, keepdims=True) + EPS)
        h_ref[...] = (xv * rstd * g_ref[...]).astype(BF16)

    return _row_call(body, name=name, T=T, ins=[(x, True), (g, False)], outs=[(D_MODEL, BF16)])[0]


def _merge_fwd(ya, yb, wa, wb, zg):
    def body(ya_ref, yb_ref, wa_ref, wb_ref, zg_ref, m_ref, ua_ref, ub_ref):
        ua = _dot(ya_ref[...].astype(BF16), wa_ref[...], 1, 0)
        ub = _dot(yb_ref[...].astype(BF16), wb_ref[...], 1, 0)
        ga = _sigmoid(zg_ref[:, :D_MODEL])
        gb = _sigmoid(zg_ref[:, D_MODEL:])
        m_ref[...] = (ga * ua + gb * ub).astype(BF16)
        ua_ref[...] = ua.astype(BF16)
        ub_ref[...] = ub.astype(BF16)

    return _row_call(body, name="merge_fwd", T=ya.shape[0],
                     ins=[(ya, True), (yb, True), (wa, False), (wb, False), (zg, (2,))],
                     outs=[(D_MODEL, BF16)] * 3)


def _merge_bwd(dx1, w_out, wa, wb, ua, ub, zg):
    def body(dx_ref, wo_ref, wa_ref, wb_ref, ua_ref, ub_ref, zg_ref, dua_ref, dub_ref, dzg_ref, dya_ref, dyb_ref):
        dmv = _dot(dx_ref[...].astype(BF16), wo_ref[...], 1, 1)
        ga = _sigmoid(zg_ref[:, :D_MODEL])
        gb = _sigmoid(zg_ref[:, D_MODEL:])
        dua = (dmv * ga).astype(BF16)
        dub = (dmv * gb).astype(BF16)
        dua_ref[...] = dua
        dub_ref[...] = dub
        dzg_ref[:, :D_MODEL] = (dmv * ua_ref[...].astype(F32) * ga * (1.0 - ga)).astype(BF16)
        dzg_ref[:, D_MODEL:] = (dmv * ub_ref[...].astype(F32) * gb * (1.0 - gb)).astype(BF16)
        dya_ref[...] = _dot(dua, wa_ref[...], 1, 1)
        dyb_ref[...] = _dot(dub, wb_ref[...], 1, 1)

    return _row_call(body, name="merge_bwd", T=dx1.shape[0],
                     ins=[(dx1, True), (w_out, False), (wa, False), (wb, False), (ua, True), (ub, True), (zg, (2,))],
                     outs=[(D_MODEL, BF16), (D_MODEL, BF16), (SEG, BF16, (IN_PAD, 2)), (HG_W, F32), (FOX_W, F32)])


def _swiglu_fwd(hf, w_gate, w_up):
    T, D = hf.shape
    F = w_gate.shape[0]
    tm, tn = _pick(T, 2 * ROW_BLOCK), _pick(F, 768)

    def body(h_ref, wg_ref, wu_ref, a_ref, b_ref, o_ref):
        hv = h_ref[...]
        a_b = _dot(hv, wg_ref[...], 1, 1).astype(BF16)
        b_b = _dot(hv, wu_ref[...], 1, 1).astype(BF16)
        a_ref[...] = a_b
        b_ref[...] = b_b
        av = a_b.astype(F32)
        o_ref[...] = (av * _sigmoid(av) * b_b.astype(F32)).astype(BF16)

    tile = pl.BlockSpec((tm, tn), lambda i, j: (i, j))
    wcol = pl.BlockSpec((tn, D), lambda i, j: (j, 0))
    return _pallas(
        body, name="swiglu_fwd", grid=(T // tm, F // tn),
        in_specs=[pl.BlockSpec((tm, D), lambda i, j: (i, 0)), wcol, wcol],
        out_specs=[tile] * 3, out_shape=[jax.ShapeDtypeStruct((T, F), BF16)] * 3,
        compiler_params=_params(dimension_semantics=("arbitrary",) * 2),
    )(hf, w_gate, w_up)


def _swiglu_bwd(dx, w_down, a, b):
    T, D = dx.shape
    F = w_down.shape[0]
    tm, tn = _pick(T, 2 * ROW_BLOCK), _pick(F, 768)

    def body(dx_ref, w_ref, a_ref, b_ref, da_ref, db_ref):
        dact = _dot(dx_ref[...].astype(BF16), w_ref[...], 1, 1)
        av = a_ref[...].astype(F32)
        bv = b_ref[...].astype(F32)
        sg = _sigmoid(av)
        da_ref[...] = (dact * bv * sg * (1.0 + av * (1.0 - sg))).astype(BF16)
        db_ref[...] = (dact * av * sg).astype(BF16)

    tile = pl.BlockSpec((tm, tn), lambda i, j: (i, j))
    return _pallas(
        body, name="swiglu_bwd", grid=(T // tm, F // tn),
        in_specs=[pl.BlockSpec((tm, D), lambda i, j: (i, 0)), pl.BlockSpec((tn, D), lambda i, j: (j, 0)), tile, tile],
        out_specs=[tile, tile], out_shape=[jax.ShapeDtypeStruct((T, F), BF16)] * 2,
        compiler_params=_params(dimension_semantics=("arbitrary",) * 2),
    )(dx, w_down, a, b)


def _ple_loss(x2, p, g, w_gate, w_proj, tgt):
    def body(x_ref, p_ref, g_ref, wg_ref, wp_ref, t_ref, hp_ref, dy_ref, dsp_ref, dpp_ref, loss_ref):
        xv = x_ref[...]
        rstd = lax.rsqrt(jnp.mean(xv * xv, axis=-1, keepdims=True) + EPS)
        hp = (xv * rstd * g_ref[...]).astype(BF16)
        hp_ref[...] = hp
        gp = _sigmoid(_dot(hp, wg_ref[...], 1, 0))
        ppv = _dot(p_ref[...].astype(BF16), wp_ref[...], 1, 0)
        err = xv + gp * ppv - t_ref[...]
        part = 0.5 * jnp.sum(jnp.mean(err * err, axis=-1, keepdims=True), axis=0, keepdims=True)
        part = jnp.broadcast_to(part, loss_ref.shape)

        @pl.when(pl.program_id(0) == 0)
        def _():
            loss_ref[...] = part

        @pl.when(pl.program_id(0) > 0)
        def _():
            loss_ref[...] += part

        dy = err * (1.0 / D_MODEL)
        dy_ref[...] = dy
        dsp_ref[...] = (dy * ppv * gp * (1.0 - gp)).astype(BF16)
        dpp_ref[...] = (dy * gp).astype(BF16)

    return _row_call(body, name="ple_loss", T=x2.shape[0],
                     ins=[(x2, True), (p, True), (g, False), (w_gate, False), (w_proj, False), (tgt, True)],
                     outs=[(D_MODEL, BF16), (D_MODEL, F32), (D_MODEL, BF16), (D_MODEL, BF16)],
                     acc_outs=[((8, LANES), F32)])


def _hg_consts():
    C = HG_CHUNK
    r, c = _iota((C, C), 0), _iota((C, C), 1)
    tri = (c <= r)
    same = (r // HG_SUB) == (c // HG_SUB)
    return tri, (tri & same)


def _hg_chunk_fwd(q, f, lb, tri_b, sub_b):
    sgq = _sigmoid(q)
    qt = q * sgq
    sg = _sigmoid(f)
    fg = lb + (1.0 - lb) * sg
    kf = (1.0 - lb) * (1.0 - sg)
    logf = jnp.log(fg)
    b = _split_dot(tri_b, logf, 1, 0)
    w = _split_dot(sub_b, logf, 1, 0)
    return sgq, qt, sg, fg, kf, b, w


def _hg_scores(qs_b, kf, b, row):
    C, S = HG_CHUNK, HG_SUB
    parts, ks = [], []
    for blk in range(C // S):
        ref = jnp.zeros_like(b[0:1]) if blk == 0 else b[blk * S - 1:blk * S]
        e = jnp.exp(jnp.minimum(ref - b, EXP_CLAMP))
        e = jnp.where(row < (blk + 1) * S, e, 0.0)
        k_b = (kf * e).astype(BF16)
        ks.append((e, k_b))
        parts.append(_dot(qs_b[blk * S:(blk + 1) * S], k_b, 1, 1))
    return jnp.concatenate(parts, axis=0), ks


def _hgrn_fwd(z, lb_logits, gain, blocks):
    T = z.shape[0]
    RB = min(ROW_BLOCK, T)
    nb, cpb = T // RB, RB // HG_CHUNK
    C, DK = HG_CHUNK, HG_DK
    n = len(blocks)

    def body(*refs):
        z_ref, lg_ref, g_ref = refs[:3]
        o_ref, y_ref, st_ref = refs[3 + n:6 + n]
        s_ref = refs[6 + 2 * n]
        g_start, g_forward, g_finish = _gather_steps(refs[3:3 + n], refs[6 + n:6 + 2 * n], *refs[7 + 2 * n:])

        @pl.when(pl.program_id(0) == 0)
        def _():
            s_ref[...] = jnp.zeros_like(s_ref)
            g_start()

        lg = lg_ref[...]
        lb_all = 1.0 / (1.0 + jnp.exp(lg[1:2] - lg[0:1]))
        gain_v = g_ref[...]
        tri, sub = _hg_consts()
        tri_b, sub_b = tri.astype(BF16), sub.astype(BF16)
        row = _iota((C, DK), 0)

        def chunk(ci, carry):
            r0 = pl.multiple_of(ci * C, C)
            rows = pl.ds(r0, C)
            _, qt, _, _, kf_all, b_all, w = _hg_chunk_fwd(z_ref[rows, 0:HG_W], z_ref[rows, HG_W:2 * HG_W], lb_all,
                                                          tri_b, sub_b)
            qs_all = (qt * jnp.exp(w)).astype(BF16)
            qd_all = (qt * jnp.exp(b_all)).astype(BF16)
            bl_all = b_all[C - 1:C]
            kd_all = (kf_all * jnp.exp(bl_all - b_all)).astype(BF16)
            ebl_all = jnp.exp(bl_all)
            v_all = z_ref[rows, 2 * HG_W:3 * HG_W].astype(BF16)
            g_all = z_ref[rows, 3 * HG_W:4 * HG_W]
            gate_all = g_all * _sigmoid(g_all)
            for h in range(HG_HEADS):
                cs = slice(h * DK, (h + 1) * DK)
                st = s_ref[h]
                st_ref[pl.ds(pl.multiple_of((ci * HG_HEADS + h) * DK, DK), DK), :] = st
                v_b = v_all[:, cs]
                a, _ = _hg_scores(qs_all[:, cs], kf_all[:, cs], b_all[:, cs], row)
                a = jnp.where(tri, a, 0.0)
                o = _dot(qd_all[:, cs], st.astype(BF16), 1, 1) + _dot(a.astype(BF16), v_b, 1, 0)
                s_ref[h] = st * ebl_all[:, cs] + _dot(v_b, kd_all[:, cs], 0, 0)
                o_ref[rows, cs] = o
                rstd = lax.rsqrt(jnp.mean(o * o, axis=-1, keepdims=True) + EPS)
                y_ref[rows, cs] = (o * rstd * gain_v * gate_all[:, cs]).astype(BF16)
            return carry

        lax.fori_loop(0, cpb, chunk, 0, unroll=2)

        @pl.when(pl.program_id(0) == nb - 1)
        def _():
            g_forward()
            g_finish()

    hbm = pl.BlockSpec(memory_space=pl.ANY)
    res = _pallas(
        body, name="hgrn_fwd", grid=(nb,),
        in_specs=[pl.BlockSpec((RB, HG_COLS), lambda i: (i, 0)), pl.BlockSpec((2, HG_W), lambda i: (0, 0)),
                  pl.BlockSpec((1, DK), lambda i: (0, 0))] + [hbm] * n,
        out_specs=[pl.BlockSpec((RB, HG_W), lambda i: (i, 0)), pl.BlockSpec((RB, HG_W), lambda i: (i, 0)),
                   pl.BlockSpec((cpb * HG_HEADS * DK, DK), lambda i: (i, 0))] + [hbm] * n,
        out_shape=[jax.ShapeDtypeStruct((T, HG_W), F32), jax.ShapeDtypeStruct((T, HG_W), BF16),
                   jax.ShapeDtypeStruct((T // C * HG_HEADS * DK, DK), F32)]
        + [jax.ShapeDtypeStruct((N_DEV,) + b.shape, b.dtype) for b in blocks],
        scratch_shapes=[pltpu.VMEM((HG_HEADS, DK, DK), F32)] + _gather_scratch(n),
        compiler_params=_params(dimension_semantics=("arbitrary",)),
    )(z, lb_logits, gain, *blocks)
    return res[0], res[1], res[2], res[3:]


def _hgrn_bwd(z, o_raw, dy, states, lb_logits, gain, dz_buf):
    T = z.shape[0]
    RB = min(ROW_BLOCK, T)
    nb, cpb = T // RB, RB // HG_CHUNK
    C, DK, S = HG_CHUNK, HG_DK, HG_SUB

    def body(z_ref, o_ref, dy_ref, st_ref, lg_ref, g_ref, _buf_ref, dz_ref, dlg_ref, dg_ref, ds_ref, dlb_ref):
        step = pl.program_id(0)

        @pl.when(step == 0)
        def _():
            ds_ref[...] = jnp.zeros_like(ds_ref)
            dlb_ref[...] = jnp.zeros_like(dlb_ref)
            dg_ref[...] = jnp.zeros_like(dg_ref)

        lg = lg_ref[...]
        lb_all = 1.0 / (1.0 + jnp.exp(lg[1:2] - lg[0:1]))
        gain_v = g_ref[...]
        tri, sub = _hg_consts()
        tri_b, sub_b = tri.astype(BF16), sub.astype(BF16)
        row = _iota((C, DK), 0)

        def chunk(cj, carry):
            ci = cpb - 1 - cj
            r0 = pl.multiple_of(ci * C, C)
            rows = pl.ds(r0, C)
            q_all = z_ref[rows, 0:HG_W]
            g_all = z_ref[rows, 3 * HG_W:4 * HG_W]
            sgq_all, qt_all, sg_all, fg_all, kf_all, b_all, w_all = _hg_chunk_fwd(
                q_all, z_ref[rows, HG_W:2 * HG_W], lb_all, tri_b, sub_b)
            ew_all = jnp.exp(w_all)
            eb_all = jnp.exp(b_all)
            bl_all = b_all[C - 1:C]
            ebl_all = jnp.exp(bl_all)
            ekd_all = jnp.exp(bl_all - b_all)
            qs_all = (qt_all * ew_all).astype(BF16)
            qd_all = (qt_all * eb_all).astype(BF16)
            kd_all = (kf_all * ekd_all).astype(BF16)
            v_all = z_ref[rows, 2 * HG_W:3 * HG_W].astype(BF16)
            sgg_all = _sigmoid(g_all)
            t1_all = dy_ref[rows, :] * (g_all * sgg_all)
            db_heads, dqt_heads, dkf_heads, dv_heads, n_heads = [], [], [], [], []
            for h in range(HG_HEADS):
                cs = slice(h * DK, (h + 1) * DK)
                kf, b, ew, eb, ebl, ekd = kf_all[:, cs], b_all[:, cs], ew_all[:, cs], eb_all[:, cs], ebl_all[:, cs], \
                    ekd_all[:, cs]
                qs_b, qd_b, kd_b, v_b = qs_all[:, cs], qd_all[:, cs], kd_all[:, cs], v_all[:, cs]
                st = st_ref[pl.ds(pl.multiple_of((ci * HG_HEADS + h) * DK, DK), DK), :]
                dst = ds_ref[h]
                o = o_ref[rows, cs]
                rstd = lax.rsqrt(jnp.mean(o * o, axis=-1, keepdims=True) + EPS)
                n = o * rstd
                n_heads.append(n)
                t1 = t1_all[:, cs]
                dg_ref[...] += jnp.sum(t1 * n, axis=0, keepdims=True)
                dn = t1 * gain_v
                do = rstd * (dn - n * jnp.mean(dn * n, axis=-1, keepdims=True))
                do_b = do.astype(BF16)
                a, ks = _hg_scores(qs_b, kf, b, row)
                a = jnp.where(tri, a, 0.0)
                dst_b = dst.astype(BF16)
                dqd = _dot(do_b, st.astype(BF16), 1, 0)
                da = jnp.where(tri, _dot(do_b, v_b, 1, 1), 0.0)
                dv = _dot(a.astype(BF16), do_b, 0, 0) + _dot(kd_b, dst_b, 1, 1)
                dkd = _dot(v_b, dst_b, 1, 0)
                ds_ref[h] = dst * ebl + _dot(do_b, qd_b, 0, 0)
                dkd_kd = dkd * kd_b.astype(F32)
                dbl = ebl * jnp.sum(dst * st, axis=0, keepdims=True) + jnp.sum(dkd_kd, axis=0, keepdims=True)
                da_b = da.astype(BF16)
                dqs_parts = []
                dk_in = jnp.zeros((C, DK), F32)
                db_k = jnp.zeros((C, DK), F32)
                for blk in range(C // S):
                    e, k_b = ks[blk]
                    da_blk = da_b[blk * S:(blk + 1) * S]
                    dqs_parts.append(_dot(da_blk, k_b, 1, 0))
                    dks = _dot(da_blk, qs_b[blk * S:(blk + 1) * S], 0, 0)
                    dk_in = dk_in + dks * e
                    db_k = db_k + dks * k_b.astype(F32)
                dqs = jnp.concatenate(dqs_parts, axis=0)
                db = qs_b.astype(F32) * dqs - db_k + dqd * qd_b.astype(F32) - dkd_kd
                db_heads.append(db + jnp.where(row == C - 1, dbl, 0.0))
                dqt_heads.append(dqs * ew + dqd * eb)
                dkf_heads.append(dk_in + dkd * ekd)
                dv_heads.append(dv)
            dlogf = _split_dot(tri_b, jnp.concatenate(db_heads, axis=1), 0, 0)
            dfg = dlogf / fg_all - jnp.concatenate(dkf_heads, axis=1)
            dlb_ref[...] += jnp.sum(dfg * (1.0 - sg_all), axis=0, keepdims=True)
            dqt = jnp.concatenate(dqt_heads, axis=1)
            n_all = jnp.concatenate(n_heads, axis=1)
            gain_all = jnp.concatenate([gain_v] * HG_HEADS, axis=1)
            dz_ref[rows, 0:HG_W] = (dqt * sgq_all * (1.0 + q_all * (1.0 - sgq_all))).astype(BF16)
            dz_ref[rows, HG_W:2 * HG_W] = (dfg * (1.0 - lb_all) * sg_all * (1.0 - sg_all)).astype(BF16)
            dz_ref[rows, 2 * HG_W:3 * HG_W] = jnp.concatenate(dv_heads, axis=1).astype(BF16)
            dz_ref[rows, 3 * HG_W:4 * HG_W] = (dy_ref[rows, :] * n_all * gain_all * sgg_all
                                               * (1.0 + g_all * (1.0 - sgg_all))).astype(BF16)
            return carry

        lax.fori_loop(0, cpb, chunk, 0, unroll=2)

        @pl.when(step == nb - 1)
        def _():
            d0 = dlb_ref[...] * lb_all * (1.0 - lb_all)
            dlg_ref[0:1, :] = d0
            dlg_ref[1:2, :] = -d0

    rev = lambda i: (nb - 1 - i, 0)
    fix = lambda i: (0, 0)
    return _pallas(
        body, name="hgrn_bwd", grid=(nb,),
        in_specs=[pl.BlockSpec((RB, HG_COLS), rev), pl.BlockSpec((RB, HG_W), rev), pl.BlockSpec((RB, HG_W), rev),
                  pl.BlockSpec((cpb * HG_HEADS * DK, DK), rev), pl.BlockSpec((2, HG_W), fix),
                  pl.BlockSpec((1, DK), fix), pl.BlockSpec(memory_space=pl.ANY)],
        out_specs=[pl.BlockSpec((RB, HG_COLS), rev), pl.BlockSpec((2, HG_W), fix), pl.BlockSpec((1, DK), fix)],
        out_shape=[jax.ShapeDtypeStruct(dz_buf.shape, BF16), jax.ShapeDtypeStruct((2, HG_W), F32),
                   jax.ShapeDtypeStruct((1, DK), F32)],
        scratch_shapes=[pltpu.VMEM((HG_HEADS, DK, DK), F32), pltpu.VMEM((1, HG_W), F32)],
        input_output_aliases={6: 0},
        compiler_params=_params(dimension_semantics=("arbitrary",)),
    )(z, o_raw, dy, states, lb_logits, gain, dz_buf)


def _head_ones():
    r, c = _iota((FOX_W, FOX_W), 0), _iota((FOX_W, FOX_W), 1)
    return ((r // FOX_DH) == (c // FOX_DH)).astype(BF16)


def _causal_blocks(nq, *, by_query):
    if by_query:
        pairs = [(q, k) for q in range(nq) for k in range(q + 1)]
    else:
        pairs = [(q, k) for k in range(nq) for q in range(k, nq)]
    return (jnp.asarray([q for q, _ in pairs], jnp.int32), jnp.asarray([k for _, k in pairs], jnp.int32))


def _log_sigmoid(x):
    return jnp.minimum(x, 0.0) - jnp.log(1.0 + jnp.exp(-jnp.abs(x)))


def _fox_prep(z, bias, qg, kg):
    T = z.shape[0]
    tm = min(ROW_BLOCK, T)
    nb = T // tm

    def body(z_ref, b_ref, qg_ref, kg_ref, q_ref, k_ref, v_ref, qa_ref, ka_ref, carry_ref):
        @pl.when(pl.program_id(0) == 0)
        def _():
            carry_ref[...] = jnp.zeros_like(carry_ref)

        ones = _head_ones()
        normed = []
        for src, g_ref in ((0, qg_ref), (1, kg_ref)):
            xv = z_ref[:, src * FOX_W:(src + 1) * FOX_W]
            ms = _split_dot(ones, xv * xv, 1, 0, mat_first=False) * (1.0 / FOX_DH)
            normed.append(xv * lax.rsqrt(ms + EPS) * g_ref[...])
        qn, kn = normed
        q_ref[...] = (qn * FOX_DH ** -0.5).astype(BF16)
        k_b = kn.astype(BF16)
        k_ref[...] = k_b
        v_ref[...] = z_ref[:, 2 * FOX_W:3 * FOX_W].astype(BF16)
        logf = _log_sigmoid(z_ref[:, 3 * FOX_W:FOX_COLS] + b_ref[...])
        r, c = _iota((tm, tm), 0), _iota((tm, tm), 1)
        tri_b = (c <= r).astype(BF16)
        cum = _split_dot(tri_b, logf, 1, 0, terms=3) + carry_ref[...]
        carry_ref[...] = cum[tm - 1:tm]
        c2 = cum * LOG2E
        hi = c2.astype(BF16)
        rem = c2 - hi.astype(F32)
        mid = rem.astype(BF16)
        lo = (rem - mid.astype(F32)).astype(BF16)
        hrow, col = _iota((LANES, 2 * FOX_W), 0), _iota((LANES, 2 * FOX_W), 1)
        base = hrow * LANES + jnp.where(hrow % 2 == 0, FOX_DH, 0)
        placed = None
        for t, part in enumerate((hi, mid, lo)):
            place = jnp.logical_and(col == base + t, hrow < FOX_HEADS).astype(BF16)
            term = _dot(part, place, 1, 0)
            placed = term if placed is None else placed + term
        colw = _iota((tm, 2 * FOX_W), 1)
        head, lane = colw // LANES, colw % LANES
        own = (lane < FOX_DH) == (head % 2 == 0)
        other = jnp.where(head % 2 == 0, lane - FOX_DH, lane)
        ones_q = jnp.where(jnp.logical_and(other >= 0, other < 3), -1.0, 0.0)
        q2 = (qn * (FOX_DH ** -0.5 * LOG2E)).astype(BF16)
        q_exp = jnp.concatenate([q2[:, (h // 2) * LANES:(h // 2 + 1) * LANES] for h in range(FOX_HEADS)], axis=1)
        k_exp = jnp.concatenate([k_b[:, (h // 2) * LANES:(h // 2 + 1) * LANES] for h in range(FOX_HEADS)], axis=1)
        qa_ref[...] = jnp.where(own, q_exp, ones_q.astype(BF16))
        ka_ref[...] = jnp.where(own, k_exp, placed.astype(BF16))

    wide = pl.BlockSpec((tm, 2 * FOX_W), lambda i: (i, 0))
    return _pallas(
        body, name="fox_prep", grid=(nb,),
        in_specs=[pl.BlockSpec((tm, SEG), lambda i: (i, 1)), pl.BlockSpec((1, LANES), lambda i: (0, 0)),
                  pl.BlockSpec((1, FOX_W), lambda i: (0, 0)), pl.BlockSpec((1, FOX_W), lambda i: (0, 0))],
        out_specs=[pl.BlockSpec((tm, FOX_W), lambda i: (i, 0))] * 3 + [wide] * 2,
        out_shape=[jax.ShapeDtypeStruct((T, FOX_W), BF16)] * 3 + [jax.ShapeDtypeStruct((T, 2 * FOX_W), BF16)] * 2,
        scratch_shapes=[pltpu.VMEM((1, LANES), F32)],
        compiler_params=_params(dimension_semantics=("arbitrary",)),
    )(z, bias, qg, kg)


def _fox_fwd(qa, ka, vb, blocks):
    T = qa.shape[0]
    tq = min(ROW_BLOCK, T)
    nq = T // tq
    NEG = -1e30
    n = len(blocks)
    n_pairs = FOX_HEADS // 2

    n_in = 3
    q_of, k_of = _causal_blocks(nq, by_query=True)
    n_tri = len(q_of)

    def body(qt_ref, kt_ref, *refs):
        q_ref, k_ref, v_ref = refs[:n_in]
        o_ref, lse_ref = refs[n_in + n:n_in + n + 2]
        m_sc, l_sc, acc_sc = refs[n_in + 2 * n + 2:n_in + 2 * n + 5]
        pr, t = pl.program_id(0), pl.program_id(1)
        qi, ki = qt_ref[t], kt_ref[t]
        g_start, g_forward, g_finish = _gather_steps(
            refs[n_in:n_in + n], refs[n_in + n + 2:n_in + 2 * n + 2], *refs[n_in + 2 * n + 5:])

        @pl.when(jnp.logical_and(pr == 0, t == 0))
        def _():
            g_start()

        @pl.when(jnp.logical_and(pr == n_pairs // 2, t == 0))
        def _():
            g_forward()

        @pl.when(ki == 0)
        def _():
            m_sc[...] = jnp.full_like(m_sc, NEG)
            l_sc[...] = jnp.zeros_like(l_sc)
            acc_sc[...] = jnp.zeros_like(acc_sc)

        lane = _iota((tq, LANES), 1)

        def block(masked):
            keys = pl.ds(pl.multiple_of(ki * tq, tq), tq)
            vv = v_ref[keys, :]
            for hh in range(2):
                hs = slice(hh * LANES, (hh + 1) * LANES)
                s = _dot(q_ref[:, hs], k_ref[keys, hs], 1, 1)
                tiles = [s[:, j * LANES:(j + 1) * LANES] for j in range(tq // LANES)]
                if masked:
                    row, col = _iota((tq, LANES), 0), _iota((tq, LANES), 1)
                    tiles = [jnp.where(row >= col + j * LANES, t, NEG) for j, t in enumerate(tiles)]
                m_old = m_sc[hh]
                top = jnp.broadcast_to(jnp.max(functools.reduce(jnp.maximum, tiles), axis=-1, keepdims=True),
                                       (tq, LANES))
                m_new = jnp.maximum(m_old, top)
                alpha = jnp.exp2(m_old - m_new)
                ps = [jnp.exp2(t - m_new) for t in tiles]
                l_sc[hh] = alpha * l_sc[hh] + functools.reduce(jnp.add, ps)
                m_sc[hh] = m_new
                p_b = jnp.concatenate([p.astype(BF16) for p in ps], axis=1)
                acc_sc[hh] = alpha * acc_sc[hh] + _dot(p_b, vv, 1, 0)

        @pl.when(ki < qi)
        def _():
            block(False)

        @pl.when(ki == qi)
        def _():
            block(True)
            l0 = jnp.sum(l_sc[0], axis=-1, keepdims=True)
            l1 = jnp.sum(l_sc[1], axis=-1, keepdims=True)
            o_ref[...] = jnp.where(lane < FOX_DH, acc_sc[0] * (1.0 / l0), acc_sc[1] * (1.0 / l1))
            lse_ref[:, :LANES] = m_sc[0] + jnp.log2(l0)
            lse_ref[:, LANES:] = m_sc[1] + jnp.log2(l1)

        @pl.when(jnp.logical_and(pr == n_pairs - 1, t == n_tri - 1))
        def _():
            g_finish()

    qmap = lambda p, t, qt, kt: (qt[t], p)
    whole = lambda p, t, qt, kt: (0, p)
    hbm = pl.BlockSpec(memory_space=pl.ANY)
    grid_spec = pltpu.PrefetchScalarGridSpec(
        num_scalar_prefetch=2, grid=(n_pairs, n_tri),
        in_specs=[pl.BlockSpec((tq, 2 * LANES), qmap), pl.BlockSpec((T, 2 * LANES), whole),
                  pl.BlockSpec((T, LANES), whole)] + [hbm] * n,
        out_specs=[pl.BlockSpec((tq, LANES), qmap), pl.BlockSpec((tq, 2 * LANES), qmap)] + [hbm] * n,
        scratch_shapes=[pltpu.VMEM((2, tq, LANES), F32)] * 3 + _gather_scratch(n))
    res = _pallas(
        body, name="fox_fwd", grid_spec=grid_spec,
        out_shape=[jax.ShapeDtypeStruct((T, FOX_W), F32), jax.ShapeDtypeStruct((T, 2 * FOX_W), F32)]
        + [jax.ShapeDtypeStruct((N_DEV,) + b.shape, b.dtype) for b in blocks],
        compiler_params=_params(dimension_semantics=("arbitrary",) * 2),
    )(q_of, k_of, qa, ka, vb, *blocks)
    return res[0], res[1], res[2:]


def _fox_bwd(qs, kn, vb, qa, ka, o, do, lse, hs):
    T = qs.shape[0]
    tq = min(ROW_BLOCK, T)
    nq = T // tq
    n = len(hs)
    n_pairs = FOX_HEADS // 2

    q_of, k_of = _causal_blocks(nq, by_query=False)
    n_tri = len(q_of)

    def body(qt_ref, kt_ref, *refs):
        q_ref, k_ref, v_ref, qa_ref, ka_ref, o_ref, do_ref, lse_ref = refs[:8]
        dq_ref, dk_ref, dv_ref, dcs_ref, drs_ref = refs[8 + n:13 + n]
        pr, t = pl.program_id(0), pl.program_id(1)
        qi, ki = qt_ref[t], kt_ref[t]
        x_start, x_finish = _chip_exchange_steps(refs[8:8 + n], refs[13 + n:13 + 2 * n], *refs[13 + 2 * n:])

        @pl.when(jnp.logical_and(pr == 0, t == 0))
        def _():
            x_start()
            drs_ref[...] = jnp.zeros_like(drs_ref)

        @pl.when(t == 0)
        def _():
            dq_ref[...] = jnp.zeros_like(dq_ref)

        @pl.when(qi == ki)
        def _():
            dk_ref[...] = jnp.zeros_like(dk_ref)
            dv_ref[...] = jnp.zeros_like(dv_ref)
            dcs_ref[...] = jnp.zeros_like(dcs_ref)

        def block(masked):
            lane = _iota((tq, LANES), 1)
            qv, kv, vv = q_ref[...], k_ref[...], v_ref[...]
            ov, dov = o_ref[...], do_ref[...]
            qrows = pl.ds(pl.multiple_of(qi * tq, tq), tq)
            dq_acc = jnp.zeros((tq, LANES), F32)
            dk_acc = jnp.zeros((tq, LANES), F32)
            dv_acc = jnp.zeros((tq, LANES), F32)
            dcs_acc = jnp.zeros((8, tq), F32)
            drs_acc = jnp.zeros((tq, LANES), F32)
            prod = dov * ov
            nt = tq // LANES
            for hh in range(2):
                in_head = (lane < FOX_DH) if hh == 0 else (lane >= FOX_DH)
                hs_ = slice(hh * LANES, (hh + 1) * LANES)
                zb = jnp.zeros_like(qv)
                qm = jnp.where(in_head, qv, zb)
                km = jnp.where(in_head, kv, zb)
                dom = jnp.where(in_head, dov, 0.0).astype(BF16)
                delta_b = jnp.broadcast_to(jnp.sum(jnp.where(in_head, prod, 0.0), axis=1, keepdims=True), (tq, LANES))
                lse_b = lse_ref[:, hs_]
                s = _dot(qa_ref[:, hs_], ka_ref[:, hs_], 1, 1)
                dp = _dot(dom, vv, 1, 1)
                p_tiles, ds_tiles, col_tiles = [], [], []
                row_part = jnp.zeros((tq, LANES), F32)
                for j in range(nt):
                    js = slice(j * LANES, (j + 1) * LANES)
                    p = jnp.exp2(s[:, js] - lse_b)
                    if masked:
                        p = jnp.where(_iota((tq, LANES), 0) >= _iota((tq, LANES), 1) + j * LANES, p, 0.0)
                    ds = p * (dp[:, js] - delta_b)
                    p_tiles.append(p.astype(BF16))
                    ds_tiles.append(ds.astype(BF16))
                    col_tiles.append(jnp.sum(ds, axis=0, keepdims=True))
                    row_part = row_part + ds
                p_b = jnp.concatenate(p_tiles, axis=1)
                ds_b = jnp.concatenate(ds_tiles, axis=1)
                dv_acc = dv_acc + _dot(p_b, dom, 0, 0)
                dq_acc = dq_acc + _dot(ds_b, km, 1, 0)
                dk_acc = dk_acc + _dot(ds_b, qm, 0, 0)
                dcs_acc = dcs_acc + jnp.where(_iota((8, tq), 0) == hh, jnp.concatenate(col_tiles, axis=1), 0.0)
                rowsum = jnp.sum(row_part, axis=1, keepdims=True)
                drs_acc = drs_acc + jnp.where(lane == 2 * pr + hh, rowsum, 0.0)
            drs_ref[qrows, :] += drs_acc
            dq_ref[qrows, :] += dq_acc
            dk_ref[...] += dk_acc
            dv_ref[...] += dv_acc
            dcs_ref[0] += dcs_acc

        @pl.when(qi > ki)
        def _():
            block(False)

        @pl.when(qi == ki)
        def _():
            block(True)

        @pl.when(jnp.logical_and(pr == n_pairs - 1, t == n_tri - 1))
        def _():
            x_finish()

    qmap = lambda p, t, qt, kt: (qt[t], p)
    kmap = lambda p, t, qt, kt: (kt[t], p)
    hbm = pl.BlockSpec(memory_space=pl.ANY)
    grid_spec = pltpu.PrefetchScalarGridSpec(
        num_scalar_prefetch=2, grid=(n_pairs, n_tri),
        in_specs=[pl.BlockSpec((tq, LANES), qmap), pl.BlockSpec((tq, LANES), kmap), pl.BlockSpec((tq, LANES), kmap),
                  pl.BlockSpec((tq, 2 * LANES), qmap), pl.BlockSpec((tq, 2 * LANES), kmap),
                  pl.BlockSpec((tq, LANES), qmap), pl.BlockSpec((tq, LANES), qmap), pl.BlockSpec((tq, 2 * LANES), qmap)]
        + [hbm] * n,
        out_specs=[pl.BlockSpec((T, LANES), lambda p, t, qt, kt: (0, p)), pl.BlockSpec((tq, LANES), kmap),
                   pl.BlockSpec((tq, LANES), kmap), pl.BlockSpec((1, 8, tq), lambda p, t, qt, kt: (p, 0, kt[t])),
                   pl.BlockSpec((T, LANES), lambda p, t, qt, kt: (0, 0))] + [hbm] * n,
        scratch_shapes=_chip_exchange_scratch(n))
    res = _pallas(
        body, name="fox_bwd", grid_spec=grid_spec,
        out_shape=[jax.ShapeDtypeStruct((T, FOX_W), F32)] * 3
        + [jax.ShapeDtypeStruct((n_pairs, 8, T), F32), jax.ShapeDtypeStruct((T, LANES), F32)]
        + _chip_exchange_shapes(hs),
        compiler_params=_params(dimension_semantics=("arbitrary",) * 2),
    )(q_of, k_of, qs, kn, vb, qa, ka, o, do, lse, *hs)
    res = list(res)
    return res[:5] + [res[5:]]


def _fox_post(z, dq, dk, dv, dcs, drs, bias, qg, kg, dz_buf):
    T = z.shape[0]
    tm = min(ROW_BLOCK, T)
    nb = T // tm

    def body(z_ref, dq_ref, dk_ref, dv_ref, dcs_ref, drs_ref, b_ref, qg_ref, kg_ref, _buf_ref, dz_ref, dqg_ref, dkg_ref,
             db_ref, carry_ref):
        @pl.when(pl.program_id(0) == 0)
        def _():
            carry_ref[...] = jnp.zeros_like(carry_ref)
            dqg_ref[...] = jnp.zeros_like(dqg_ref)
            dkg_ref[...] = jnp.zeros_like(dkg_ref)
            db_ref[...] = jnp.zeros_like(db_ref)

        ones = _head_ones()
        for src, g_ref, d_ref, dg_ref, scale in ((0, qg_ref, dq_ref, dqg_ref, FOX_DH ** -0.5), (1, kg_ref, dk_ref, dkg_ref, 1.0)):
            xv = z_ref[:, src * FOX_W:(src + 1) * FOX_W]
            ms = _split_dot(ones, xv * xv, 1, 0, mat_first=False) * (1.0 / FOX_DH)
            rstd = lax.rsqrt(ms + EPS)
            xh = xv * rstd
            dn = d_ref[...] * scale
            dg_ref[...] += jnp.sum(dn * xh, axis=0, keepdims=True)
            dxh = dn * g_ref[...]
            mean = _split_dot(ones, dxh * xh, 1, 0, mat_first=False) * (1.0 / FOX_DH)
            dz_ref[:, src * FOX_W:(src + 1) * FOX_W] = (rstd * (dxh - xh * mean)).astype(BF16)
        dz_ref[:, 2 * FOX_W:3 * FOX_W] = dv_ref[...].astype(BF16)
        row8 = _iota((8, tm), 0)
        dct = jnp.zeros((8, tm), F32)
        for h in range(FOX_HEADS):
            src_row = dcs_ref[h // 2][h % 2:h % 2 + 1, :]
            dct = dct + jnp.where(row8 == h, src_row, 0.0)
        dct = drs_ref[...].T[0:8] - dct
        r, c = _iota((tm, tm), 0), _iota((tm, tm), 1)
        upper_b = (r >= c).astype(BF16)
        rc = _split_dot(upper_b, dct, 1, 0, mat_first=False) + carry_ref[...]
        carry_ref[...] = rc[:, 0:1]
        full = jnp.concatenate([rc, jnp.zeros((LANES - 8, tm), F32)], axis=0)
        dlogf = full.T
        xf = z_ref[:, 3 * FOX_W:FOX_COLS] + b_ref[...]
        df = dlogf * (1.0 - _sigmoid(xf))
        dz_ref[:, 3 * FOX_W:FOX_COLS] = df.astype(BF16)
        dz_ref[:, FOX_COLS:] = jnp.zeros((tm, SEG - FOX_COLS), BF16)
        db_ref[...] += jnp.sum(df, axis=0, keepdims=True)

    rev = lambda i: (nb - 1 - i, 0)
    fix2 = lambda i: (0, 0)
    return _pallas(
        body, name="fox_post", grid=(nb,),
        in_specs=[pl.BlockSpec((tm, SEG), lambda i: (nb - 1 - i, 1)), pl.BlockSpec((tm, FOX_W), rev),
                  pl.BlockSpec((tm, FOX_W), rev),
                  pl.BlockSpec((tm, FOX_W), rev), pl.BlockSpec((FOX_HEADS // 2, 8, tm), lambda i: (0, 0, nb - 1 - i)),
                  pl.BlockSpec((tm, LANES), rev),
                  pl.BlockSpec((1, LANES), fix2), pl.BlockSpec((1, FOX_W), fix2), pl.BlockSpec((1, FOX_W), fix2),
                  pl.BlockSpec(memory_space=pl.ANY)],
        out_specs=[pl.BlockSpec((tm, SEG), lambda i: (nb - 1 - i, 1)), pl.BlockSpec((1, FOX_W), fix2),
                   pl.BlockSpec((1, FOX_W), fix2), pl.BlockSpec((1, LANES), fix2)],
        out_shape=[jax.ShapeDtypeStruct(dz_buf.shape, BF16), jax.ShapeDtypeStruct((1, FOX_W), F32),
                   jax.ShapeDtypeStruct((1, FOX_W), F32), jax.ShapeDtypeStruct((1, LANES), F32)],
        scratch_shapes=[pltpu.VMEM((8, 1), F32)],
        input_output_aliases={9: 0},
        compiler_params=_params(dimension_semantics=("arbitrary",)),
    )(z, dq, dk, dv, dcs, drs, bias, qg, kg, dz_buf)


def _local_step(x, p, tgt, sm, W, rest_chunks, core):
    lbl, og, fb = sm["hg_lb_logits"], sm["hg_onorm_g"], sm["fox_f_bias"]
    fbias = jnp.pad(fb, ((0, 0), (0, LANES - FOX_HEADS)))
    qg = jnp.tile(sm["fox_q_norm_g"], (1, FOX_HEADS))
    kg = jnp.tile(sm["fox_k_norm_g"], (1, FOX_HEADS))

    h = _rms_fwd(x, sm["norm_mix_g"], name="rms_mix")
    rest = dict(zip(BIG[1:], rest_chunks))
    first, second = ["w_ffn_gate"], ["w_ffn_up"]
    third = [n for n in BIG[1:] if n not in first + second]
    z, got1 = _matmul(h, W["w_in"], tb=True, gather=[rest[n] for n in first], name="mm_z")
    o_raw, ya, states, got2 = _hgrn_fwd(z, lbl, og, [rest[n] for n in second])
    qs, kn, vb, qa, ka = _fox_prep(z, fbias, qg, kg)
    yb, lse, got3 = _fox_fwd(qa, ka, vb, [rest[n] for n in third])
    W = dict(W, **{n: _full_of_chunks(n, g)
                   for n, g in zip(first + second + third, list(got1) + list(got2) + list(got3))})
    merged, ua, ub = _merge_fwd(ya, yb, W["w_branch_a"], W["w_branch_b"], z)
    x1, hf = _matmul(merged, W["w_out"], add=x, norm_fwd=sm["norm_ffn_g"], name="mm_x1")
    a, b, act = _swiglu_fwd(hf, W["w_ffn_gate"], W["w_ffn_up"])
    x2 = _matmul(act, W["w_ffn_down"], add=x1, name="mm_x2")
    hp, dy, dsp, dpp, loss = _ple_loss(x2, p, sm["norm_ple_g"], W["w_ple_gate"], W["w_ple_proj"], tgt)

    G = {}
    G["w_ple_proj"] = _matmul(p, dpp, ta=True, out_dtype=BF16, name="mm_dw_ple_proj")
    G["w_ple_gate"] = _matmul(hp, dsp, ta=True, out_dtype=BF16, name="mm_dw_ple_gate")
    dx2, d_ple_g = _matmul(dsp, W["w_ple_gate"], tb=True, norm_bwd=(x2, sm["norm_ple_g"], dy), name="mm_dx2")
    G["w_ffn_down"] = _matmul(act, dx2, ta=True, out_dtype=BF16, name="mm_dw_ffn_down")
    da, db = _swiglu_bwd(dx2, W["w_ffn_down"], a, b)
    G["w_ffn_gate"] = _matmul(da, hf, ta=True, out_dtype=BF16, name="mm_dw_ffn_gate")
    G["w_ffn_up"] = _matmul(db, hf, ta=True, out_dtype=BF16, name="mm_dw_ffn_up")
    dhf = _matmul(da, W["w_ffn_gate"], name="mm_dhf_a")
    dx1, d_ffn_g = _matmul(db, W["w_ffn_up"], add=dhf, norm_bwd=(x1, sm["norm_ffn_g"], dx2), name="mm_dx1")
    G["w_out"] = _matmul(merged, dx1, ta=True, out_dtype=BF16, name="mm_dw_out")
    dua, dub, dz, dya, dyb = _merge_bwd(dx1, W["w_out"], W["w_branch_a"], W["w_branch_b"], ua, ub, z)
    G["w_branch_a"] = _matmul(ya, dua, ta=True, out_dtype=BF16, name="mm_dw_branch_a")
    G["w_branch_b"] = _matmul(yb, dub, ta=True, out_dtype=BF16, name="mm_dw_branch_b")
    hb_rest = _sibling_sums({n: G[n] for n in BIG[1:]}, core, tag="rest")
    dq, dk, dv, dcs, drs, got_rest = _fox_bwd(qs, kn, vb, qa, ka, yb, dyb, lse, hb_rest)
    dz, d_qg, d_kg, d_fb = _fox_post(z, dq, dk, dv, dcs, drs, fbias, qg, kg, dz)
    dz, d_lbl, d_og = _hgrn_bwd(z, o_raw, dya, states, lbl, og, dz)
    G["w_in"] = _matmul(dz, h, ta=True, out_dtype=BF16, name="mm_dw_in")
    hb_in = _sibling_sums({"w_in": G["w_in"]}, core, tag="w_in")
    (grad_x, d_mix_g), got_in = _matmul(dz, W["w_in"], exchange=hb_in,
                                        norm_bwd=(x, sm["norm_mix_g"], dx1), name="mm_dx")

    gs = {"norm_mix_g": d_mix_g, "hg_lb_logits": d_lbl, "hg_onorm_g": d_og, "fox_f_bias": d_fb[:, :FOX_HEADS],
          "fox_q_norm_g": d_qg.reshape(FOX_HEADS, FOX_DH).sum(0, keepdims=True),
          "fox_k_norm_g": d_kg.reshape(FOX_HEADS, FOX_DH).sum(0, keepdims=True),
          "norm_ffn_g": d_ffn_g, "norm_ple_g": d_ple_g}
    return loss, grad_x, gs, hb_in + hb_rest, list(got_in) + list(got_rest)


def _pack_rows(parts, total):
    buf = jnp.concatenate(parts, axis=-2)
    pad = total - buf.shape[-2]
    widths = [(0, 0)] * (buf.ndim - 2) + [(0, pad), (0, 0)]
    return jnp.pad(buf, widths)


def _chunk_of_shard(n, w):
    if n == "w_in":
        return jnp.pad(w, ((0, IN_SHARD_PAD - IN_SHARD), (0, 0)))
    if n in ("w_ffn_gate", "w_ffn_up", "w_ffn_down"):
        return jnp.pad(w, ((0, FF_SHARD_PAD - FF_SHARD), (0, 0)))
    return w


def _full_of_chunks(n, g):
    _, a, b = g.shape
    if n == "w_in":
        w = g[:, :IN_SHARD].reshape(IN_COLS, b)
        gap = jnp.zeros((SEG - FOX_LOGICAL, b), g.dtype)
        return jnp.concatenate([w[:HG_COLS + FOX_LOGICAL], gap, w[HG_COLS + FOX_LOGICAL:]], axis=0)
    if BIG_SHAPE[n][2] == 0:
        return g.reshape(N_DEV * a, b)
    return g.transpose(1, 0, 2).reshape(a, N_DEV * b)


def _chunks_of_full(n, g):
    if n == "w_in":
        w = jnp.concatenate([g[:HG_COLS + FOX_LOGICAL], g[2 * SEG:]], axis=0).reshape(N_DEV, IN_SHARD, g.shape[1])
        return jnp.pad(w, ((0, 0), (0, IN_SHARD_PAD - IN_SHARD), (0, 0)))
    if BIG_SHAPE[n][2] == 0:
        return g.reshape(N_DEV, g.shape[0] // N_DEV, g.shape[1])
    return g.reshape(g.shape[0], N_DEV, g.shape[1] // N_DEV).transpose(1, 0, 2)


def _pack_small(vals, loss_row=None):
    parts = [vals[n].reshape(SMALL_ROWS[n], -1) for n in SMALL]
    parts = [jnp.pad(v, ((0, 0), (0, LANES - v.shape[1]))) for v in parts]
    if loss_row is not None:
        parts.append(loss_row)
    return _pack_rows(parts, SMALL_TOTAL)


def _unpack_small(buf, like):
    out, r0 = {}, 0
    for n in SMALL:
        rows, size = SMALL_ROWS[n], like[n].size
        blk = buf[r0:r0 + rows]
        out[n] = (blk if size == rows * LANES else blk[:, :size]).reshape(like[n].shape)
        r0 += rows
    return out


def _place():
    return lax.axis_index("x"), lax.axis_index("y"), lax.axis_index("c")


def _gather_steps(x_refs, out_refs, send_sems, recv_sems, local_sems):
    n = len(x_refs)
    x, y, c = _place()
    me, sibling = (x, y, c), (x, y, 1 - c)
    chips = [(1 - x, y), (x, 1 - y), (1 - x, 1 - y)]

    def slot(i, px, py, pc):
        return out_refs[i].at[4 * px + 2 * py + pc]

    def copy(k, i, blk, to, own=False):
        return pltpu.make_async_remote_copy(
            src_ref=x_refs[i] if own else slot(i, *blk), dst_ref=slot(i, *blk),
            send_sem=send_sems.at[k, i], recv_sem=recv_sems.at[k, i], device_id=to, device_id_type=MESH)

    def mine():
        return [pltpu.make_async_copy(x_refs[i], slot(i, *me), local_sems.at[i]) for i in range(n)]

    def first():
        cps = [copy(0, i, me, sibling, own=True) for i in range(n)]
        return cps + [copy(1 + j, i, me, (*chip, c), own=True) for j, chip in enumerate(chips) for i in range(n)]

    def passed():
        return [copy(4 + j, i, (*chip, c), sibling) for j, chip in enumerate(chips) for i in range(n)]

    def start():
        for cp in mine() + first():
            cp.start()

    def forward():
        fws = passed()
        for j, chip in enumerate(chips):
            for i in range(n):
                copy(1 + j, i, (*chip, c), me).wait_recv()
                fws[j * n + i].start()

    def finish():
        for i in range(n):
            copy(0, i, sibling, me).wait_recv()
        for j, chip in enumerate(chips):
            for i in range(n):
                copy(4 + j, i, (*chip, 1 - c), me).wait_recv()
        for cp in first() + passed():
            cp.wait_send()
        for cp in mine():
            cp.wait()

    return start, forward, finish


def _gather_scratch(n):
    return [pltpu.SemaphoreType.DMA((7, n)), pltpu.SemaphoreType.DMA((7, n)), pltpu.SemaphoreType.DMA((n,))]


def _all_gather(blocks, *, name):
    n = len(blocks)

    def body(*refs):
        for step in _gather_steps(refs[:n], refs[n:2 * n], *refs[2 * n:]):
            step()

    hbm = pl.BlockSpec(memory_space=pl.ANY)
    return _pallas(
        body, name=name, out_shape=[jax.ShapeDtypeStruct((N_DEV,) + b.shape, b.dtype) for b in blocks],
        in_specs=[hbm] * n, out_specs=[hbm] * n, scratch_shapes=_gather_scratch(n),
    )(*blocks)


def _sibling_exchange(gs, *, name):
    n = len(gs)

    def body(*refs):
        g_refs, out_refs = refs[:n], refs[n:2 * n]
        send_sems, recv_sems = refs[2 * n:]
        x, y, c = _place()
        cps = [pltpu.make_async_remote_copy(
            src_ref=g_refs[i].at[:, pl.ds(1 - c, 1)], dst_ref=out_refs[i], send_sem=send_sems.at[i],
            recv_sem=recv_sems.at[i], device_id=(x, y, 1 - c), device_id_type=MESH) for i in range(n)]
        for cp in cps:
            cp.start()
        for cp in cps:
            cp.wait()

    hbm = pl.BlockSpec(memory_space=pl.ANY)
    return _pallas(
        body, name=name, out_shape=[jax.ShapeDtypeStruct((N_CHIP, 1) + g.shape[2:], g.dtype) for g in gs],
        in_specs=[hbm] * n, out_specs=[hbm] * n,
        scratch_shapes=[pltpu.SemaphoreType.DMA((n,)), pltpu.SemaphoreType.DMA((n,))],
    )(*gs)


def _chip_sum(g4s, gots, core, *, name):
    n = len(g4s)

    def body(c_ref, *refs):
        for g_ref, r_ref, h_ref in zip(refs[:n], refs[n:2 * n], refs[2 * n:]):
            h_ref[0] = (g_ref[0, 0].astype(F32) + r_ref[0, 0].astype(F32)).astype(BF16)

    shapes = [g.shape[2:] for g in g4s]
    grid_spec = pltpu.PrefetchScalarGridSpec(
        num_scalar_prefetch=1, grid=(N_CHIP,),
        in_specs=[pl.BlockSpec((1, 1) + s, lambda j, c: (j, c[0], 0, 0)) for s in shapes]
        + [pl.BlockSpec((1, 1) + s, lambda j, c: (j, 0, 0, 0)) for s in shapes],
        out_specs=[pl.BlockSpec((1,) + s, lambda j, c: (j, 0, 0)) for s in shapes])
    return list(_pallas(
        body, name=name, grid_spec=grid_spec, out_shape=[jax.ShapeDtypeStruct((N_CHIP,) + s, BF16) for s in shapes],
        compiler_params=_params(dimension_semantics=("arbitrary",)),
    )(core, *g4s, *gots))


def _chip_exchange_steps(h_refs, out_refs, send_sems, recv_sems):
    n = len(h_refs)
    x, y, c = _place()
    chips = [(1 - x, y), (x, 1 - y), (1 - x, 1 - y)]

    def copies():
        return [pltpu.make_async_remote_copy(
            src_ref=h_refs[i].at[2 * px + py], dst_ref=out_refs[i].at[k], send_sem=send_sems.at[k, i],
            recv_sem=recv_sems.at[k, i], device_id=(px, py, c), device_id_type=MESH)
            for k, (px, py) in enumerate(chips) for i in range(n)]

    def start():
        for cp in copies():
            cp.start()

    def finish():
        for cp in copies():
            cp.wait()

    return start, finish


def _chip_exchange_shapes(hs):
    return [jax.ShapeDtypeStruct((3,) + h.shape[1:], h.dtype) for h in hs]


def _chip_exchange_scratch(n):
    return [pltpu.SemaphoreType.DMA((3, n)), pltpu.SemaphoreType.DMA((3, n))]


def _sibling_sums(G, core, *, tag):
    g4 = []
    for n, g in G.items():
        gc = _chunks_of_full(n, g)
        g4.append(gc.reshape((N_CHIP, 2) + gc.shape[1:]))
    got = _sibling_exchange(g4, name="grads_to_sibling_" + tag)
    return _chip_sum(g4, got, core, name="chip_sum_" + tag)


def _adam_math(w, g, m, v):
    m = ADAM_B1 * m + (1.0 - ADAM_B1) * g
    v = ADAM_B2 * v + (1.0 - ADAM_B2) * (g * g)
    m_hat = m / (1.0 - ADAM_B1 ** ADAM_STEP)
    v_hat = v / (1.0 - ADAM_B2 ** ADAM_STEP)
    delta = -ADAM_LR * (m_hat / (jnp.sqrt(v_hat) + ADAM_EPS) + ADAM_WD * w)
    return delta, m, v


def _adam_shard(hb, got, chip, w, m, v, *, name):
    _, r, c = w.shape
    _, a, cb = hb.shape
    assert cb == c and c % LANES == 0, (hb.shape, w.shape)
    tc = _pick(c, 2 * LANES)

    def body(j_ref, h_ref, r_ref, w_ref, m_ref, v_ref, g_ref, d_ref, nm_ref, nv_ref):
        parts = [h_ref[0], r_ref[0], r_ref[1], r_ref[2]]
        g = None
        for part in parts:
            part = part[:r].astype(F32)
            g = part if g is None else g + part
        d, nm, nv = _adam_math(w_ref[0], g, m_ref[0], v_ref[0])
        g_ref[0] = g
        d_ref[0] = d
        nm_ref[0] = nm
        nv_ref[0] = nv

    blk = pl.BlockSpec((1, r, tc), lambda i, j: (0, 0, i))
    grid_spec = pltpu.PrefetchScalarGridSpec(
        num_scalar_prefetch=1, grid=(c // tc,),
        in_specs=[pl.BlockSpec((1, a, tc), lambda i, j: (j[0], 0, i)),
                  pl.BlockSpec((3, a, tc), lambda i, j: (0, 0, i)), blk, blk, blk],
        out_specs=[blk] * 4)
    return _pallas(
        body, name=name, grid_spec=grid_spec, out_shape=[jax.ShapeDtypeStruct((1, r, c), F32)] * 4,
        compiler_params=_params(dimension_semantics=("arbitrary",)),
    )(chip, hb, got, w, m, v)


def _small_all_reduce_adam(gs, w, m, v):
    def body(g_ref, w_ref, m_ref, v_ref, sum_ref, d_ref, nm_ref, nv_ref, gather, send_sems, recv_sems):
        x, y, c = _place()
        my = 4 * x + 2 * y + c
        gather[my] = g_ref[...]
        cps = []
        for k in range(1, N_DEV):
            to = (x ^ (k >> 2), y ^ ((k >> 1) & 1), c ^ (k & 1))
            cps.append(pltpu.make_async_remote_copy(
                src_ref=g_ref, dst_ref=gather.at[my], send_sem=send_sems.at[k - 1], recv_sem=recv_sems.at[k - 1],
                device_id=to, device_id_type=MESH))
        for cp in cps:
            cp.start()
        for cp in cps:
            cp.wait()
        total = gather[0]
        for d in range(1, N_DEV):
            total = total + gather[d]
        dlt, nm, nv = _adam_math(w_ref[...], total, m_ref[...], v_ref[...])
        sum_ref[...] = total
        d_ref[...] = dlt
        nm_ref[...] = nm
        nv_ref[...] = nv

    vm = pl.BlockSpec(memory_space=pltpu.VMEM)
    return _pallas(
        body, name="small_all_reduce_adam", out_shape=[jax.ShapeDtypeStruct((SMALL_TOTAL, LANES), F32)] * 4,
        in_specs=[vm] * 4, out_specs=[vm] * 4,
        scratch_shapes=[pltpu.VMEM((N_DEV, SMALL_TOTAL, LANES), F32), pltpu.SemaphoreType.DMA((7,)),
                        pltpu.SemaphoreType.DMA((7,))],
            )(gs, w, m, v)


def kernel(x, p, norm_mix_g, w_in, hg_lb_logits, hg_onorm_g, fox_f_bias, fox_q_norm_g, fox_k_norm_g, w_branch_a, w_branch_b, w_out, norm_ffn_g, w_ffn_gate, w_ffn_up, w_ffn_down, norm_ple_g, w_ple_gate, w_ple_proj, loss_target, m_norm_mix_g, m_w_in, m_hg_lb_logits, m_hg_onorm_g, m_fox_f_bias, m_fox_q_norm_g, m_fox_k_norm_g, m_w_branch_a, m_w_branch_b, m_w_out, m_norm_ffn_g, m_w_ffn_gate, m_w_ffn_up, m_w_ffn_down, m_norm_ple_g, m_w_ple_gate, m_w_ple_proj, v_norm_mix_g, v_w_in, v_hg_lb_logits, v_hg_onorm_g, v_fox_f_bias, v_fox_q_norm_g, v_fox_k_norm_g, v_w_branch_a, v_w_branch_b, v_w_out, v_norm_ffn_g, v_w_ffn_gate, v_w_ffn_up, v_w_ffn_down, v_norm_ple_g, v_w_ple_gate, v_w_ple_proj):
    args = dict(locals())
    wts = {n: args[n] for n in BIG + SMALL}
    mom = {n: args["m_" + n] for n in BIG + SMALL}
    var = {n: args["v_" + n] for n in BIG + SMALL}
    for group in (wts, mom, var):
        for n in TRANSPOSED:
            group[n] = jnp.swapaxes(group[n], 1, 2)
    sm = {n: wts[n] for n in SMALL}

    xi, yi, ci = _place()
    core = jnp.reshape(ci, (1,)).astype(jnp.int32)
    chip = jnp.reshape(2 * xi + yi, (1,)).astype(jnp.int32)
    chunks = [_chunk_of_shard(n, wts[n][0].astype(BF16)) for n in BIG]
    assert BIG[0] == "w_in"
    w_in_full = _full_of_chunks("w_in", _all_gather(chunks[:1], name="w_in_all_gather")[0])

    loss_blk, grad_x, gs, hb, got = _local_step(
        x[0], p[0, 0], loss_target[0], sm, {"w_in": w_in_full}, chunks[1:], core)

    g_big, d_big, nm_big, nv_big = {}, {}, {}, {}
    for n, h, r in zip(BIG, hb, got):
        res = _adam_shard(h, r, chip, wts[n], mom[n], var[n], name="adam_" + n)
        if n in TRANSPOSED:
            res = [jnp.swapaxes(t, 1, 2) for t in res]
        g_big[n], d_big[n], nm_big[n], nv_big[n] = res

    s_sum, s_d, s_nm, s_nv = _small_all_reduce_adam(
        _pack_small(gs, loss_blk[0:1]), _pack_small(sm), _pack_small({n: mom[n] for n in SMALL}),
        _pack_small({n: var[n] for n in SMALL}))
    loss = s_sum[LOSS_ROW, 0]
    g_small, d_small, nm_small, nv_small = (_unpack_small(t, sm) for t in (s_sum, s_d, s_nm, s_nv))

    order = ["norm_mix_g", "w_in", "hg_lb_logits", "hg_onorm_g", "fox_f_bias", "fox_q_norm_g", "fox_k_norm_g",
             "w_branch_a", "w_branch_b", "w_out", "norm_ffn_g", "w_ffn_gate", "w_ffn_up", "w_ffn_down", "norm_ple_g",
             "w_ple_gate", "w_ple_proj"]
    outs = [loss, grad_x[None]]
    for big, small in ((g_big, g_small), (d_big, d_small), (nm_big, nm_small), (nv_big, nv_small)):
        outs += [big[n] if n in big else small[n] for n in order]
    return tuple(outs)
```

```python
import functools

import jax
import jax.numpy as jnp
from jax import lax
from jax.experimental import pallas as pl
from jax.experimental.pallas import tpu as pltpu

F32 = jnp.float32
BF16 = jnp.bfloat16

D_MODEL = 1024
PLE_DIM = 256
HG_HEADS = 4
HG_DK = 128
HG_CHUNK = 64
HG_SUB = 16
HG_W = HG_HEADS * HG_DK
FOX_HEADS = 8
FOX_DH = 64
FOX_W = FOX_HEADS * FOX_DH
D_FF = 2816
EPS = 1e-6
N_DEV = 8
N_CHIP = 4
LANES = 128
FOX_COLS = 3 * FOX_W + LANES
HG_COLS = 4 * HG_W
GATE_COLS = 2 * D_MODEL
IN_COLS = HG_COLS + 3 * FOX_W + FOX_HEADS + GATE_COLS
FOX_LOGICAL = 3 * FOX_W + FOX_HEADS
SEG = 2048
IN_PAD = 3 * SEG
IN_SHARD = IN_COLS // N_DEV
IN_SHARD_PAD = 768
FF_SHARD = D_FF // N_DEV
FF_SHARD_PAD = 384
EXP_CLAMP = 80.0
LOG2E = 1.4426950408889634

ADAM_LR = 0.001
ADAM_B1 = 0.9
ADAM_B2 = 0.999
ADAM_EPS = 1e-08
ADAM_WD = 0.01
ADAM_STEP = 10

MESH = pl.DeviceIdType.MESH
VMEM_LIMIT = 56 * 1024 * 1024
ROW_BLOCK = 512

BIG = ["w_in", "w_branch_a", "w_branch_b", "w_out", "w_ffn_gate", "w_ffn_up", "w_ffn_down",
       "w_ple_gate", "w_ple_proj"]
TRANSPOSED = ("w_in", "w_ffn_gate", "w_ffn_up")
BIG_SHAPE = {
    "w_in": (IN_COLS, D_MODEL, 0), "w_branch_a": (HG_W, D_MODEL, 1), "w_branch_b": (FOX_W, D_MODEL, 1),
    "w_out": (D_MODEL, D_MODEL, 0), "w_ffn_gate": (D_FF, D_MODEL, 0), "w_ffn_up": (D_FF, D_MODEL, 0),
    "w_ffn_down": (D_FF, D_MODEL, 0), "w_ple_gate": (D_MODEL, D_MODEL, 0), "w_ple_proj": (PLE_DIM, D_MODEL, 1),
}

SMALL = ["norm_mix_g", "hg_lb_logits", "hg_onorm_g", "fox_f_bias", "fox_q_norm_g", "fox_k_norm_g",
         "norm_ffn_g", "norm_ple_g"]
SMALL_ROWS = {"norm_mix_g": 8, "hg_lb_logits": 8, "hg_onorm_g": 1, "fox_f_bias": 1, "fox_q_norm_g": 1,
              "fox_k_norm_g": 1, "norm_ffn_g": 8, "norm_ple_g": 8}
SMALL_TOTAL = 40
LOSS_ROW = 36


def _pallas(body, **kw):
    return pl.pallas_call(body, **kw)


def _params(**kw):
    return pltpu.CompilerParams(vmem_limit_bytes=VMEM_LIMIT, **kw)


def _pick(n, target):
    if n <= target:
        return n
    best = None
    for t in range(LANES, target + 1, LANES):
        if n % t == 0:
            best = t
    assert best is not None, (n, target)
    return best


def _dot(a, b, ca, cb):
    return lax.dot_general(a, b, (((ca,), (cb,)), ((), ())), preferred_element_type=F32)


def _split_dot(mat, x, ca, cb, terms=2, mat_first=True):
    acc = None
    rem = x
    for _ in range(terms):
        part = rem.astype(BF16)
        rem = rem - part.astype(F32)
        p = _dot(mat, part, ca, cb) if mat_first else _dot(part, mat, ca, cb)
        acc = p if acc is None else acc + p
    return acc


def _sigmoid(x):
    return 1.0 / (1.0 + jnp.exp(-x))


def _iota(shape, dim):
    return lax.broadcasted_iota(jnp.int32, shape, dim)


def _matmul(a, b, *, name, ta=False, tb=False, out_dtype=F32, add=None, exchange=None, gather=None,
            norm_fwd=None, norm_bwd=None):
    assert exchange is None or gather is None
    (K, M) = a.shape if ta else a.shape[::-1]
    (N, Kb) = b.shape if tb else b.shape[::-1]
    assert K == Kb, (a.shape, b.shape, ta, tb)
    if ta:
        tm, tn, tk = _pick(M, 2 * ROW_BLOCK), _pick(N, 2 * ROW_BLOCK), _pick(K, 4 * ROW_BLOCK)
    else:
        tm, tn, tk = _pick(M, 2 * ROW_BLOCK), _pick(N, 2048), _pick(K, 3072)
    if norm_bwd is not None:
        tm = _pick(M, ROW_BLOCK)
    nk = K // tk
    use_scratch = nk > 1 and out_dtype != F32
    if norm_fwd is not None or norm_bwd is not None:
        assert tn == N and not use_scratch and out_dtype == F32

    hs = list(exchange or gather or [])
    n_x = len(hs)
    grid = (M // tm, N // tn, nk)
    a_spec = pl.BlockSpec((tk, tm), lambda i, j, k: (k, i)) if ta else pl.BlockSpec((tm, tk), lambda i, j, k: (i, k))
    b_spec = pl.BlockSpec((tn, tk), lambda i, j, k: (j, k)) if tb else pl.BlockSpec((tk, tn), lambda i, j, k: (k, j))
    o_spec = pl.BlockSpec((tm, tn), lambda i, j, k: (i, j))
    row_vec = pl.BlockSpec((1, N), lambda i, j, k: (0, 0))
    hbm = pl.BlockSpec(memory_space=pl.ANY)
    extra_in = [(add, o_spec)] if add is not None else []
    extra_out = []
    if norm_fwd is not None:
        extra_in += [(norm_fwd, row_vec)]
        extra_out += [(jax.ShapeDtypeStruct((M, N), BF16), o_spec)]
    if norm_bwd is not None:
        extra_in += [(norm_bwd[0], o_spec), (norm_bwd[1], row_vec), (norm_bwd[2], o_spec)]
        extra_out += [(jax.ShapeDtypeStruct((1, N), F32), row_vec)]
    if gather is not None:
        ride_shapes, ride_scratch = [jax.ShapeDtypeStruct((N_DEV,) + h.shape, h.dtype) for h in hs], _gather_scratch(n_x)
    else:
        ride_shapes, ride_scratch = _chip_exchange_shapes(hs), (_chip_exchange_scratch(n_x) if n_x else [])
    n_ex_in, n_ex_out = len(extra_in), len(extra_out)

    def body(*refs):
        refs = list(refs)
        a_ref, b_ref = refs[:2]
        ex_in = refs[2:2 + n_ex_in]
        ride_in = refs[2 + n_ex_in:2 + n_ex_in + n_x]
        base = 2 + n_ex_in + n_x
        o_ref = refs[base]
        ex_out = refs[base + 1:base + 1 + n_ex_out]
        ride_out = refs[base + 1 + n_ex_out:base + 1 + n_ex_out + n_x]
        scratch = refs[base + 1 + n_ex_out + n_x:]
        at = [pl.program_id(d) for d in range(3)]
        k = at[2]
        if n_x:
            steps = _gather_steps if gather is not None else _chip_exchange_steps
            ride = steps(ride_in, ride_out, *scratch[-len(ride_scratch):])

            @pl.when(jnp.logical_and(at[0] == 0, jnp.logical_and(at[1] == 0, at[2] == 0)))
            def _():
                ride[0]()
        p = _dot(a_ref[...].astype(BF16), b_ref[...].astype(BF16), 0 if ta else 1, 1 if tb else 0)

        def finish(r):
            ins = list(ex_in)
            outs = list(ex_out)
            if add is not None:
                r = r + ins.pop(0)[...].astype(F32)
            if norm_fwd is not None:
                g_ref = ins.pop(0)
                rstd = lax.rsqrt(jnp.mean(r * r, axis=-1, keepdims=True) + EPS)
                outs.pop(0)[...] = (r * rstd * g_ref[...]).astype(BF16)
            if norm_bwd is not None:
                x_ref, g_ref, dres_ref = ins.pop(0), ins.pop(0), ins.pop(0)
                dg_ref = outs.pop(0)
                xv = x_ref[...]
                rstd = lax.rsqrt(jnp.mean(xv * xv, axis=-1, keepdims=True) + EPS)
                xh = xv * rstd
                part = jnp.sum(r * xh, axis=0, keepdims=True)

                @pl.when(at[0] == 0)
                def _():
                    dg_ref[...] = part

                @pl.when(at[0] > 0)
                def _():
                    dg_ref[...] += part

                dxh = r * g_ref[...]
                r = rstd * (dxh - xh * jnp.mean(dxh * xh, axis=-1, keepdims=True)) + dres_ref[...]
            o_ref[...] = r.astype(out_dtype)

        if nk == 1:
            finish(p)
        elif not use_scratch:
            @pl.when(k == 0)
            def _():
                o_ref[...] = p

            @pl.when(jnp.logical_and(k > 0, k < nk - 1))
            def _():
                o_ref[...] += p

            @pl.when(k == nk - 1)
            def _():
                finish(o_ref[...] + p)
        else:
            acc_ref = scratch[0]

            @pl.when(k == 0)
            def _():
                acc_ref[...] = p

            @pl.when(k > 0)
            def _():
                acc_ref[...] += p

            @pl.when(k == nk - 1)
            def _():
                finish(acc_ref[...])

        if n_x:
            @pl.when(jnp.logical_and(at[0] == grid[0] - 1, jnp.logical_and(at[1] == grid[1] - 1, at[2] == nk - 1)))
            def _():
                for step in ride[1:]:
                    step()

    res = _pallas(
        body, name=name, grid=grid,
        in_specs=[a_spec, b_spec] + [s for _, s in extra_in] + [hbm] * n_x,
        out_specs=[o_spec] + [s for _, s in extra_out] + [hbm] * n_x,
        out_shape=[jax.ShapeDtypeStruct((M, N), out_dtype)] + [s for s, _ in extra_out] + ride_shapes,
        scratch_shapes=([pltpu.VMEM((tm, tn), F32)] if use_scratch else []) + ride_scratch,
        compiler_params=_params(dimension_semantics=("arbitrary",) * 3),
    )(a, b, *[v for v, _ in extra_in], *hs)
    res = list(res)
    main = res[0] if n_ex_out == 0 else tuple(res[:1 + n_ex_out])
    return (main, res[1 + n_ex_out:]) if n_x else main


def _row_map(nb, reverse, seg):
    if reverse:
        return lambda i: (nb - 1 - i, seg)
    return lambda i: (i, seg)


def _row_call(body, *, name, T, ins, outs, acc_outs=(), tm=ROW_BLOCK, reverse=False):
    tm = min(tm, T)
    nb = T // tm
    in_specs, args = [], []
    for arr, how in ins:
        args.append(arr)
        if how is True:
            in_specs.append(pl.BlockSpec((tm, arr.shape[1]), _row_map(nb, reverse, 0)))
        elif how is False:
            in_specs.append(pl.BlockSpec(arr.shape, lambda i, _n=arr.ndim: (0,) * _n))
        else:
            in_specs.append(pl.BlockSpec((tm, SEG), _row_map(nb, reverse, how[0])))
    out_specs, out_shape = [], []
    for o in outs:
        c, dt = o[0], o[1]
        total, seg = o[2] if len(o) > 2 else (c, 0)
        out_specs.append(pl.BlockSpec((tm, c), _row_map(nb, reverse, seg)))
        out_shape.append(jax.ShapeDtypeStruct((T, total), dt))
    for shp, dt in acc_outs:
        out_specs.append(pl.BlockSpec(shp, lambda i, _n=len(shp): (0,) * _n))
        out_shape.append(jax.ShapeDtypeStruct(shp, dt))
    return _pallas(body, name=name, grid=(nb,), in_specs=in_specs, out_specs=out_specs, out_shape=out_shape,
                   compiler_params=_params(dimension_semantics=("arbitrary",)))(*args)


def _rms_fwd(x, g, *, name):
    T = x.shape[0]

    def body(x_ref, g_ref, h_ref):
        xv = x_ref[...]
        rstd = lax.rsqrt(jnp.mean(xv * xv, axis=-1, keepdims=True) + EPS)
        h_ref[...] = (xv * rstd * g_ref[...]).astype(BF16)

    return _row_call(body, name=name, T=T, ins=[(x, True), (g, False)], outs=[(D_MODEL, BF16)])[0]


def _merge_fwd(ya, yb, wa, wb, zg):
    def body(ya_ref, yb_ref, wa_ref, wb_ref, zg_ref, m_ref, ua_ref, ub_ref):
        ua = _dot(ya_ref[...].astype(BF16), wa_ref[...], 1, 0)
        ub = _dot(yb_ref[...].astype(BF16), wb_ref[...], 1, 0)
        ga = _sigmoid(zg_ref[:, :D_MODEL])
        gb = _sigmoid(zg_ref[:, D_MODEL:])
        m_ref[...] = (ga * ua + gb * ub).astype(BF16)
        ua_ref[...] = ua.astype(BF16)
        ub_ref[...] = ub.astype(BF16)

    return _row_call(body, name="merge_fwd", T=ya.shape[0],
                     ins=[(ya, True), (yb, True), (wa, False), (wb, False), (zg, (2,))],
                     outs=[(D_MODEL, BF16)] * 3)


def _merge_bwd(dx1, w_out, wa, wb, ua, ub, zg):
    def body(dx_ref, wo_ref, wa_ref, wb_ref, ua_ref, ub_ref, zg_ref, dua_ref, dub_ref, dzg_ref, dya_ref, dyb_ref):
        dmv = _dot(dx_ref[...].astype(BF16), wo_ref[...], 1, 1)
        ga = _sigmoid(zg_ref[:, :D_MODEL])
        gb = _sigmoid(zg_ref[:, D_MODEL:])
        dua = (dmv * ga).astype(BF16)
        dub = (dmv * gb).astype(BF16)
        dua_ref[...] = dua
        dub_ref[...] = dub
        dzg_ref[:, :D_MODEL] = (dmv * ua_ref[...].astype(F32) * ga * (1.0 - ga)).astype(BF16)
        dzg_ref[:, D_MODEL:] = (dmv * ub_ref[...].astype(F32) * gb * (1.0 - gb)).astype(BF16)
        dya_ref[...] = _dot(dua, wa_ref[...], 1, 1)
        dyb_ref[...] = _dot(dub, wb_ref[...], 1, 1)

    return _row_call(body, name="merge_bwd", T=dx1.shape[0],
                     ins=[(dx1, True), (w_out, False), (wa, False), (wb, False), (ua, True), (ub, True), (zg, (2,))],
                     outs=[(D_MODEL, BF16), (D_MODEL, BF16), (SEG, BF16, (IN_PAD, 2)), (HG_W, F32), (FOX_W, F32)])


def _swiglu_fwd(hf, w_gate, w_up):
    T, D = hf.shape
    F = w_gate.shape[0]
    tm, tn = _pick(T, 2 * ROW_BLOCK), _pick(F, 768)

    def body(h_ref, wg_ref, wu_ref, a_ref, b_ref, o_ref):
        hv = h_ref[...]
        a_b = _dot(hv, wg_ref[...], 1, 1).astype(BF16)
        b_b = _dot(hv, wu_ref[...], 1, 1).astype(BF16)
        a_ref[...] = a_b
        b_ref[...] = b_b
        av = a_b.astype(F32)
        o_ref[...] = (av * _sigmoid(av) * b_b.astype(F32)).astype(BF16)

    tile = pl.BlockSpec((tm, tn), lambda i, j: (i, j))
    wcol = pl.BlockSpec((tn, D), lambda i, j: (j, 0))
    return _pallas(
        body, name="swiglu_fwd", grid=(T // tm, F // tn),
        in_specs=[pl.BlockSpec((tm, D), lambda i, j: (i, 0)), wcol, wcol],
        out_specs=[tile] * 3, out_shape=[jax.ShapeDtypeStruct((T, F), BF16)] * 3,
        compiler_params=_params(dimension_semantics=("arbitrary",) * 2),
    )(hf, w_gate, w_up)


def _swiglu_bwd(dx, w_down, a, b):
    T, D = dx.shape
    F = w_down.shape[0]
    tm, tn = _pick(T, 2 * ROW_BLOCK), _pick(F, 768)

    def body(dx_ref, w_ref, a_ref, b_ref, da_ref, db_ref):
        dact = _dot(dx_ref[...].astype(BF16), w_ref[...], 1, 1)
        av = a_ref[...].astype(F32)
        bv = b_ref[...].astype(F32)
        sg = _sigmoid(av)
        da_ref[...] = (dact * bv * sg * (1.0 + av * (1.0 - sg))).astype(BF16)
        db_ref[...] = (dact * av * sg).astype(BF16)

    tile = pl.BlockSpec((tm, tn), lambda i, j: (i, j))
    return _pallas(
        body, name="swiglu_bwd", grid=(T // tm, F // tn),
        in_specs=[pl.BlockSpec((tm, D), lambda i, j: (i, 0)), pl.BlockSpec((tn, D), lambda i, j: (j, 0)), tile, tile],
        out_specs=[tile, tile], out_shape=[jax.ShapeDtypeStruct((T, F), BF16)] * 2,
        compiler_params=_params(dimension_semantics=("arbitrary",) * 2),
    )(dx, w_down, a, b)


def _ple_loss(x2, p, g, w_gate, w_proj, tgt):
    def body(x_ref, p_ref, g_ref, wg_ref, wp_ref, t_ref, hp_ref, dy_ref, dsp_ref, dpp_ref, loss_ref):
        xv = x_ref[...]
        rstd = lax.rsqrt(jnp.mean(xv * xv, axis=-1, keepdims=True) + EPS)
        hp = (xv * rstd * g_ref[...]).astype(BF16)
        hp_ref[...] = hp
        gp = _sigmoid(_dot(hp, wg_ref[...], 1, 0))
        ppv = _dot(p_ref[...].astype(BF16), wp_ref[...], 1, 0)
        err = xv + gp * ppv - t_ref[...]
        part = 0.5 * jnp.sum(jnp.mean(err * err, axis=-1, keepdims=True), axis=0, keepdims=True)
        part = jnp.broadcast_to(part, loss_ref.shape)

        @pl.when(pl.program_id(0) == 0)
        def _():
            loss_ref[...] = part

        @pl.when(pl.program_id(0) > 0)
        def _():
            loss_ref[...] += part

        dy = err * (1.0 / D_MODEL)
        dy_ref[...] = dy
        dsp_ref[...] = (dy * ppv * gp * (1.0 - gp)).astype(BF16)
        dpp_ref[...] = (dy * gp).astype(BF16)

    return _row_call(body, name="ple_loss", T=x2.shape[0],
                     ins=[(x2, True), (p, True), (g, False), (w_gate, False), (w_proj, False), (tgt, True)],
                     outs=[(D_MODEL, BF16), (D_MODEL, F32), (D_MODEL, BF16), (D_MODEL, BF16)],
                     acc_outs=[((8, LANES), F32)])


def _hg_consts():
    C = HG_CHUNK
    r, c = _iota((C, C), 0), _iota((C, C), 1)
    tri = (c <= r)
    same = (r // HG_SUB) == (c // HG_SUB)
    return tri, (tri & same)


def _hg_chunk_fwd(q, f, lb, tri_b, sub_b):
    sgq = _sigmoid(q)
    qt = q * sgq
    sg = _sigmoid(f)
    fg = lb + (1.0 - lb) * sg
    kf = (1.0 - lb) * (1.0 - sg)
    logf = jnp.log(fg)
    b = _split_dot(tri_b, logf, 1, 0)
    w = _split_dot(sub_b, logf, 1, 0)
    return sgq, qt, sg, fg, kf, b, w


def _hg_scores(qs_b, kf, b, row):
    C, S = HG_CHUNK, HG_SUB
    parts, ks = [], []
    for blk in range(C // S):
        ref = jnp.zeros_like(b[0:1]) if blk == 0 else b[blk * S - 1:blk * S]
        e = jnp.exp(jnp.minimum(ref - b, EXP_CLAMP))
        e = jnp.where(row < (blk + 1) * S, e, 0.0)
        k_b = (kf * e).astype(BF16)
        ks.append((e, k_b))
        parts.append(_dot(qs_b[blk * S:(blk + 1) * S], k_b, 1, 1))
    return jnp.concatenate(parts, axis=0), ks


def _hgrn_fwd(z, lb_logits, gain, blocks):
    T = z.shape[0]
    RB = min(ROW_BLOCK, T)
    nb, cpb = T // RB, RB // HG_CHUNK
    C, DK = HG_CHUNK, HG_DK
    n = len(blocks)

    def body(*refs):
        z_ref, lg_ref, g_ref = refs[:3]
        o_ref, y_ref, st_ref = refs[3 + n:6 + n]
        s_ref = refs[6 + 2 * n]
        g_start, g_forward, g_finish = _gather_steps(refs[3:3 + n], refs[6 + n:6 + 2 * n], *refs[7 + 2 * n:])

        @pl.when(pl.program_id(0) == 0)
        def _():
            s_ref[...] = jnp.zeros_like(s_ref)
            g_start()

        lg = lg_ref[...]
        lb_all = 1.0 / (1.0 + jnp.exp(lg[1:2] - lg[0:1]))
        gain_v = g_ref[...]
        tri, sub = _hg_consts()
        tri_b, sub_b = tri.astype(BF16), sub.astype(BF16)
        row = _iota((C, DK), 0)

        def chunk(ci, carry):
            r0 = pl.multiple_of(ci * C, C)
            rows = pl.ds(r0, C)
            _, qt, _, _, kf_all, b_all, w = _hg_chunk_fwd(z_ref[rows, 0:HG_W], z_ref[rows, HG_W:2 * HG_W], lb_all,
                                                          tri_b, sub_b)
            qs_all = (qt * jnp.exp(w)).astype(BF16)
            qd_all = (qt * jnp.exp(b_all)).astype(BF16)
            bl_all = b_all[C - 1:C]
            kd_all = (kf_all * jnp.exp(bl_all - b_all)).astype(BF16)
            ebl_all = jnp.exp(bl_all)
            v_all = z_ref[rows, 2 * HG_W:3 * HG_W].astype(BF16)
            g_all = z_ref[rows, 3 * HG_W:4 * HG_W]
            gate_all = g_all * _sigmoid(g_all)
            for h in range(HG_HEADS):
                cs = slice(h * DK, (h + 1) * DK)
                st = s_ref[h]
                st_ref[pl.ds(pl.multiple_of((ci * HG_HEADS + h) * DK, DK), DK), :] = st
                v_b = v_all[:, cs]
                a, _ = _hg_scores(qs_all[:, cs], kf_all[:, cs], b_all[:, cs], row)
                a = jnp.where(tri, a, 0.0)
                o = _dot(qd_all[:, cs], st.astype(BF16), 1, 1) + _dot(a.astype(BF16), v_b, 1, 0)
                s_ref[h] = st * ebl_all[:, cs] + _dot(v_b, kd_all[:, cs], 0, 0)
                o_ref[rows, cs] = o
                rstd = lax.rsqrt(jnp.mean(o * o, axis=-1, keepdims=True) + EPS)
                y_ref[rows, cs] = (o * rstd * gain_v * gate_all[:, cs]).astype(BF16)
            return carry

        lax.fori_loop(0, cpb, chunk, 0, unroll=2)

        @pl.when(pl.program_id(0) == nb - 1)
        def _():
            g_forward()
            g_finish()

    hbm = pl.BlockSpec(memory_space=pl.ANY)
    res = _pallas(
        body, name="hgrn_fwd", grid=(nb,),
        in_specs=[pl.BlockSpec((RB, HG_COLS), lambda i: (i, 0)), pl.BlockSpec((2, HG_W), lambda i: (0, 0)),
                  pl.BlockSpec((1, DK), lambda i: (0, 0))] + [hbm] * n,
        out_specs=[pl.BlockSpec((RB, HG_W), lambda i: (i, 0)), pl.BlockSpec((RB, HG_W), lambda i: (i, 0)),
                   pl.BlockSpec((cpb * HG_HEADS * DK, DK), lambda i: (i, 0))] + [hbm] * n,
        out_shape=[jax.ShapeDtypeStruct((T, HG_W), F32), jax.ShapeDtypeStruct((T, HG_W), BF16),
                   jax.ShapeDtypeStruct((T // C * HG_HEADS * DK, DK), F32)]
        + [jax.ShapeDtypeStruct((N_DEV,) + b.shape, b.dtype) for b in blocks],
        scratch_shapes=[pltpu.VMEM((HG_HEADS, DK, DK), F32)] + _gather_scratch(n),
        compiler_params=_params(dimension_semantics=("arbitrary",)),
    )(z, lb_logits, gain, *blocks)
    return res[0], res[1], res[2], res[3:]


def _hgrn_bwd(z, o_raw, dy, states, lb_logits, gain, dz_buf):
    T = z.shape[0]
    RB = min(ROW_BLOCK, T)
    nb, cpb = T // RB, RB // HG_CHUNK
    C, DK, S = HG_CHUNK, HG_DK, HG_SUB

    def body(z_ref, o_ref, dy_ref, st_ref, lg_ref, g_ref, _buf_ref, dz_ref, dlg_ref, dg_ref, ds_ref, dlb_ref):
        step = pl.program_id(0)

        @pl.when(step == 0)
        def _():
            ds_ref[...] = jnp.zeros_like(ds_ref)
            dlb_ref[...] = jnp.zeros_like(dlb_ref)
            dg_ref[...] = jnp.zeros_like(dg_ref)

        lg = lg_ref[...]
        lb_all = 1.0 / (1.0 + jnp.exp(lg[1:2] - lg[0:1]))
        gain_v = g_ref[...]
        tri, sub = _hg_consts()
        tri_b, sub_b = tri.astype(BF16), sub.astype(BF16)
        row = _iota((C, DK), 0)

        def chunk(cj, carry):
            ci = cpb - 1 - cj
            r0 = pl.multiple_of(ci * C, C)
            rows = pl.ds(r0, C)
            q_all = z_ref[rows, 0:HG_W]
            g_all = z_ref[rows, 3 * HG_W:4 * HG_W]
            sgq_all, qt_all, sg_all, fg_all, kf_all, b_all, w_all = _hg_chunk_fwd(
                q_all, z_ref[rows, HG_W:2 * HG_W], lb_all, tri_b, sub_b)
            ew_all = jnp.exp(w_all)
            eb_all = jnp.exp(b_all)
            bl_all = b_all[C - 1:C]
            ebl_all = jnp.exp(bl_all)
            ekd_all = jnp.exp(bl_all - b_all)
            qs_all = (qt_all * ew_all).astype(BF16)
            qd_all = (qt_all * eb_all).astype(BF16)
            kd_all = (kf_all * ekd_all).astype(BF16)
            v_all = z_ref[rows, 2 * HG_W:3 * HG_W].astype(BF16)
            sgg_all = _sigmoid(g_all)
            t1_all = dy_ref[rows, :] * (g_all * sgg_all)
            db_heads, dqt_heads, dkf_heads, dv_heads, n_heads = [], [], [], [], []
            for h in range(HG_HEADS):
                cs = slice(h * DK, (h + 1) * DK)
                kf, b, ew, eb, ebl, ekd = kf_all[:, cs], b_all[:, cs], ew_all[:, cs], eb_all[:, cs], ebl_all[:, cs], \
                    ekd_all[:, cs]
                qs_b, qd_b, kd_b, v_b = qs_all[:, cs], qd_all[:, cs], kd_all[:, cs], v_all[:, cs]
                st = st_ref[pl.ds(pl.multiple_of((ci * HG_HEADS + h) * DK, DK), DK), :]
                dst = ds_ref[h]
                o = o_ref[rows, cs]
                rstd = lax.rsqrt(jnp.mean(o * o, axis=-1, keepdims=True) + EPS)
                n = o * rstd
                n_heads.append(n)
                t1 = t1_all[:, cs]
                dg_ref[...] += jnp.sum(t1 * n, axis=0, keepdims=True)
                dn = t1 * gain_v
                do = rstd * (dn - n * jnp.mean(dn * n, axis=-1, keepdims=True))
                do_b = do.astype(BF16)
                a, ks = _hg_scores(qs_b, kf, b, row)
                a = jnp.where(tri, a, 0.0)
                dst_b = dst.astype(BF16)
                dqd = _dot(do_b, st.astype(BF16), 1, 0)
                da = jnp.where(tri, _dot(do_b, v_b, 1, 1), 0.0)
                dv = _dot(a.astype(BF16), do_b, 0, 0) + _dot(kd_b, dst_b, 1, 1)
                dkd = _dot(v_b, dst_b, 1, 0)
                ds_ref[h] = dst * ebl + _dot(do_b, qd_b, 0, 0)
                dkd_kd = dkd * kd_b.astype(F32)
                dbl = ebl * jnp.sum(dst * st, axis=0, keepdims=True) + jnp.sum(dkd_kd, axis=0, keepdims=True)
                da_b = da.astype(BF16)
                dqs_parts = []
                dk_in = jnp.zeros((C, DK), F32)
                db_k = jnp.zeros((C, DK), F32)
                for blk in range(C // S):
                    e, k_b = ks[blk]
                    da_blk = da_b[blk * S:(blk + 1) * S]
                    dqs_parts.append(_dot(da_blk, k_b, 1, 0))
                    dks = _dot(da_blk, qs_b[blk * S:(blk + 1) * S], 0, 0)
                    dk_in = dk_in + dks * e
                    db_k = db_k + dks * k_b.astype(F32)
                dqs = jnp.concatenate(dqs_parts, axis=0)
                db = qs_b.astype(F32) * dqs - db_k + dqd * qd_b.astype(F32) - dkd_kd
                db_heads.append(db + jnp.where(row == C - 1, dbl, 0.0))
                dqt_heads.append(dqs * ew + dqd * eb)
                dkf_heads.append(dk_in + dkd * ekd)
                dv_heads.append(dv)
            dlogf = _split_dot(tri_b, jnp.concatenate(db_heads, axis=1), 0, 0)
            dfg = dlogf / fg_all - jnp.concatenate(dkf_heads, axis=1)
            dlb_ref[...] += jnp.sum(dfg * (1.0 - sg_all), axis=0, keepdims=True)
            dqt = jnp.concatenate(dqt_heads, axis=1)
            n_all = jnp.concatenate(n_heads, axis=1)
            gain_all = jnp.concatenate([gain_v] * HG_HEADS, axis=1)
            dz_ref[rows, 0:HG_W] = (dqt * sgq_all * (1.0 + q_all * (1.0 - sgq_all))).astype(BF16)
            dz_ref[rows, HG_W:2 * HG_W] = (dfg * (1.0 - lb_all) * sg_all * (1.0 - sg_all)).astype(BF16)
            dz_ref[rows, 2 * HG_W:3 * HG_W] = jnp.concatenate(dv_heads, axis=1).astype(BF16)
            dz_ref[rows, 3 * HG_W:4 * HG_W] = (dy_ref[rows, :] * n_all * gain_all * sgg_all
                                               * (1.0 + g_all * (1.0 - sgg_all))).astype(BF16)
            return carry

        lax.fori_loop(0, cpb, chunk, 0, unroll=2)

        @pl.when(step == nb - 1)
        def _():
            d0 = dlb_ref[...] * lb_all * (1.0 - lb_all)
            dlg_ref[0:1, :] = d0
            dlg_ref[1:2, :] = -d0

    rev = lambda i: (nb - 1 - i, 0)
    fix = lambda i: (0, 0)
    return _pallas(
        body, name="hgrn_bwd", grid=(nb,),
        in_specs=[pl.BlockSpec((RB, HG_COLS), rev), pl.BlockSpec((RB, HG_W), rev), pl.BlockSpec((RB, HG_W), rev),
                  pl.BlockSpec((cpb * HG_HEADS * DK, DK), rev), pl.BlockSpec((2, HG_W), fix),
                  pl.BlockSpec((1, DK), fix), pl.BlockSpec(memory_space=pl.ANY)],
        out_specs=[pl.BlockSpec((RB, HG_COLS), rev), pl.BlockSpec((2, HG_W), fix), pl.BlockSpec((1, DK), fix)],
        out_shape=[jax.ShapeDtypeStruct(dz_buf.shape, BF16), jax.ShapeDtypeStruct((2, HG_W), F32),
                   jax.ShapeDtypeStruct((1, DK), F32)],
        scratch_shapes=[pltpu.VMEM((HG_HEADS, DK, DK), F32), pltpu.VMEM((1, HG_W), F32)],
        input_output_aliases={6: 0},
        compiler_params=_params(dimension_semantics=("arbitrary",)),
    )(z, o_raw, dy, states, lb_logits, gain, dz_buf)


def _head_ones():
    r, c = _iota((FOX_W, FOX_W), 0), _iota((FOX_W, FOX_W), 1)
    return ((r // FOX_DH) == (c // FOX_DH)).astype(BF16)


def _causal_blocks(nq, *, by_query):
    if by_query:
        pairs = [(q, k) for q in range(nq) for k in range(q + 1)]
    else:
        pairs = [(q, k) for k in range(nq) for q in range(k, nq)]
    return (jnp.asarray([q for q, _ in pairs], jnp.int32), jnp.asarray([k for _, k in pairs], jnp.int32))


def _log_sigmoid(x):
    return jnp.minimum(x, 0.0) - jnp.log(1.0 + jnp.exp(-jnp.abs(x)))


def _fox_prep(z, bias, qg, kg):
    T = z.shape[0]
    tm = min(ROW_BLOCK, T)
    nb = T // tm

    def body(z_ref, b_ref, qg_ref, kg_ref, q_ref, k_ref, v_ref, qa_ref, ka_ref, carry_ref):
        @pl.when(pl.program_id(0) == 0)
        def _():
            carry_ref[...] = jnp.zeros_like(carry_ref)

        ones = _head_ones()
        normed = []
        for src, g_ref in ((0, qg_ref), (1, kg_ref)):
            xv = z_ref[:, src * FOX_W:(src + 1) * FOX_W]
            ms = _split_dot(ones, xv * xv, 1, 0, mat_first=False) * (1.0 / FOX_DH)
            normed.append(xv * lax.rsqrt(ms + EPS) * g_ref[...])
        qn, kn = normed
        q_ref[...] = (qn * FOX_DH ** -0.5).astype(BF16)
        k_b = kn.astype(BF16)
        k_ref[...] = k_b
        v_ref[...] = z_ref[:, 2 * FOX_W:3 * FOX_W].astype(BF16)
        logf = _log_sigmoid(z_ref[:, 3 * FOX_W:FOX_COLS] + b_ref[...])
        r, c = _iota((tm, tm), 0), _iota((tm, tm), 1)
        tri_b = (c <= r).astype(BF16)
        cum = _split_dot(tri_b, logf, 1, 0, terms=3) + carry_ref[...]
        carry_ref[...] = cum[tm - 1:tm]
        c2 = cum * LOG2E
        hi = c2.astype(BF16)
        rem = c2 - hi.astype(F32)
        mid = rem.astype(BF16)
        lo = (rem - mid.astype(F32)).astype(BF16)
        hrow, col = _iota((LANES, 2 * FOX_W), 0), _iota((LANES, 2 * FOX_W), 1)
        base = hrow * LANES + jnp.where(hrow % 2 == 0, FOX_DH, 0)
        placed = None
        for t, part in enumerate((hi, mid, lo)):
            place = jnp.logical_and(col == base + t, hrow < FOX_HEADS).astype(BF16)
            term = _dot(part, place, 1, 0)
            placed = term if placed is None else placed + term
        colw = _iota((tm, 2 * FOX_W), 1)
        head, lane = colw // LANES, colw % LANES
        own = (lane < FOX_DH) == (head % 2 == 0)
        other = jnp.where(head % 2 == 0, lane - FOX_DH, lane)
        ones_q = jnp.where(jnp.logical_and(other >= 0, other < 3), -1.0, 0.0)
        q2 = (qn * (FOX_DH ** -0.5 * LOG2E)).astype(BF16)
        q_exp = jnp.concatenate([q2[:, (h // 2) * LANES:(h // 2 + 1) * LANES] for h in range(FOX_HEADS)], axis=1)
        k_exp = jnp.concatenate([k_b[:, (h // 2) * LANES:(h // 2 + 1) * LANES] for h in range(FOX_HEADS)], axis=1)
        qa_ref[...] = jnp.where(own, q_exp, ones_q.astype(BF16))
        ka_ref[...] = jnp.where(own, k_exp, placed.astype(BF16))

    wide = pl.BlockSpec((tm, 2 * FOX_W), lambda i: (i, 0))
    return _pallas(
        body, name="fox_prep", grid=(nb,),
        in_specs=[pl.BlockSpec((tm, SEG), lambda i: (i, 1)), pl.BlockSpec((1, LANES), lambda i: (0, 0)),
                  pl.BlockSpec((1, FOX_W), lambda i: (0, 0)), pl.BlockSpec((1, FOX_W), lambda i: (0, 0))],
        out_specs=[pl.BlockSpec((tm, FOX_W), lambda i: (i, 0))] * 3 + [wide] * 2,
        out_shape=[jax.ShapeDtypeStruct((T, FOX_W), BF16)] * 3 + [jax.ShapeDtypeStruct((T, 2 * FOX_W), BF16)] * 2,
        scratch_shapes=[pltpu.VMEM((1, LANES), F32)],
        compiler_params=_params(dimension_semantics=("arbitrary",)),
    )(z, bias, qg, kg)


def _fox_fwd(qa, ka, vb, blocks):
    T = qa.shape[0]
    tq = min(ROW_BLOCK, T)
    nq = T // tq
    NEG = -1e30
    n = len(blocks)
    n_pairs = FOX_HEADS // 2

    n_in = 3

    def body(*refs):
        q_ref, k_ref, v_ref = refs[:n_in]
        o_ref, lse_ref = refs[n_in + n:n_in + n + 2]
        m_sc, l_sc, acc_sc = refs[n_in + 2 * n + 2:n_in + 2 * n + 5]
        pr, qi = pl.program_id(0), pl.program_id(1)
        g_start, g_forward, g_finish = _gather_steps(
            refs[n_in:n_in + n], refs[n_in + n + 2:n_in + 2 * n + 2], *refs[n_in + 2 * n + 5:])

        @pl.when(jnp.logical_and(pr == 0, qi == 0))
        def _():
            g_start()

        @pl.when(jnp.logical_and(pr == n_pairs // 2, qi == 0))
        def _():
            g_forward()

        m_sc[...] = jnp.full_like(m_sc, NEG)
        l_sc[...] = jnp.zeros_like(l_sc)
        acc_sc[...] = jnp.zeros_like(acc_sc)
        lane = _iota((tq, LANES), 1)

        def block(masked, ki):
            keys = pl.ds(pl.multiple_of(ki * tq, tq), tq)
            vv = v_ref[keys, :]
            for hh in range(2):
                hs = slice(hh * LANES, (hh + 1) * LANES)
                s = _dot(q_ref[:, hs], k_ref[keys, hs], 1, 1)
                tiles = [s[:, j * LANES:(j + 1) * LANES] for j in range(tq // LANES)]
                if masked:
                    row, col = _iota((tq, LANES), 0), _iota((tq, LANES), 1)
                    tiles = [jnp.where(row >= col + j * LANES, t, NEG) for j, t in enumerate(tiles)]
                m_old = m_sc[hh]
                top = jnp.broadcast_to(jnp.max(functools.reduce(jnp.maximum, tiles), axis=-1, keepdims=True),
                                       (tq, LANES))
                m_new = jnp.maximum(m_old, top)
                alpha = jnp.exp2(m_old - m_new)
                ps = [jnp.exp2(t - m_new) for t in tiles]
                l_sc[hh] = alpha * l_sc[hh] + functools.reduce(jnp.add, ps)
                m_sc[hh] = m_new
                p_b = jnp.concatenate([p.astype(BF16) for p in ps], axis=1)
                acc_sc[hh] = alpha * acc_sc[hh] + _dot(p_b, vv, 1, 0)

        def before(ki, carry):
            block(False, ki)
            return carry

        lax.fori_loop(0, qi, before, 0)
        block(True, qi)
        l0 = jnp.sum(l_sc[0], axis=-1, keepdims=True)
        l1 = jnp.sum(l_sc[1], axis=-1, keepdims=True)
        o_ref[...] = jnp.where(lane < FOX_DH, acc_sc[0] * (1.0 / l0), acc_sc[1] * (1.0 / l1))
        lse_ref[:, :LANES] = m_sc[0] + jnp.log2(l0)
        lse_ref[:, LANES:] = m_sc[1] + jnp.log2(l1)

        @pl.when(jnp.logical_and(pr == n_pairs - 1, qi == nq - 1))
        def _():
            g_finish()

    qmap = lambda p, i: (i, p)
    whole = lambda p, i: (0, p)
    hbm = pl.BlockSpec(memory_space=pl.ANY)
    res = _pallas(
        body, name="fox_fwd", grid=(n_pairs, nq),
        in_specs=[pl.BlockSpec((tq, 2 * LANES), qmap), pl.BlockSpec((T, 2 * LANES), whole),
                  pl.BlockSpec((T, LANES), whole)] + [hbm] * n,
        out_specs=[pl.BlockSpec((tq, LANES), qmap), pl.BlockSpec((tq, 2 * LANES), qmap)] + [hbm] * n,
        out_shape=[jax.ShapeDtypeStruct((T, FOX_W), F32), jax.ShapeDtypeStruct((T, 2 * FOX_W), F32)]
        + [jax.ShapeDtypeStruct((N_DEV,) + b.shape, b.dtype) for b in blocks],
        scratch_shapes=[pltpu.VMEM((2, tq, LANES), F32)] * 3 + _gather_scratch(n),
        compiler_params=_params(dimension_semantics=("arbitrary",) * 2),
    )(qa, ka, vb, *blocks)
    return res[0], res[1], res[2:]


def _fox_bwd(qs, kn, vb, qa, ka, o, do, lse, hs):
    T = qs.shape[0]
    tq = min(ROW_BLOCK, T)
    nq = T // tq
    n = len(hs)
    n_pairs = FOX_HEADS // 2

    q_of, k_of = _causal_blocks(nq, by_query=False)
    n_tri = len(q_of)

    def body(qt_ref, kt_ref, *refs):
        q_ref, k_ref, v_ref, qa_ref, ka_ref, o_ref, do_ref, lse_ref = refs[:8]
        dq_ref, dk_ref, dv_ref, dcs_ref, drs_ref = refs[8 + n:13 + n]
        pr, t = pl.program_id(0), pl.program_id(1)
        qi, ki = qt_ref[t], kt_ref[t]
        x_start, x_finish = _chip_exchange_steps(refs[8:8 + n], refs[13 + n:13 + 2 * n], *refs[13 + 2 * n:])

        @pl.when(jnp.logical_and(pr == 0, t == 0))
        def _():
            x_start()
            drs_ref[...] = jnp.zeros_like(drs_ref)

        @pl.when(t == 0)
        def _():
            dq_ref[...] = jnp.zeros_like(dq_ref)

        @pl.when(qi == ki)
        def _():
            dk_ref[...] = jnp.zeros_like(dk_ref)
            dv_ref[...] = jnp.zeros_like(dv_ref)
            dcs_ref[...] = jnp.zeros_like(dcs_ref)

        def block(masked):
            lane = _iota((tq, LANES), 1)
            qv, kv, vv = q_ref[...], k_ref[...], v_ref[...]
            ov, dov = o_ref[...], do_ref[...]
            qrows = pl.ds(pl.multiple_of(qi * tq, tq), tq)
            dq_acc = jnp.zeros((tq, LANES), F32)
            dk_acc = jnp.zeros((tq, LANES), F32)
            dv_acc = jnp.zeros((tq, LANES), F32)
            dcs_acc = jnp.zeros((8, tq), F32)
            drs_acc = jnp.zeros((tq, LANES), F32)
            prod = dov * ov
            nt = tq // LANES
            for hh in range(2):
                in_head = (lane < FOX_DH) if hh == 0 else (lane >= FOX_DH)
                hs_ = slice(hh * LANES, (hh + 1) * LANES)
                zb = jnp.zeros_like(qv)
                qm = jnp.where(in_head, qv, zb)
                km = jnp.where(in_head, kv, zb)
                dom = jnp.where(in_head, dov, 0.0).astype(BF16)
                delta_b = jnp.broadcast_to(jnp.sum(jnp.where(in_head, prod, 0.0), axis=1, keepdims=True), (tq, LANES))
                lse_b = lse_ref[:, hs_]
                s = _dot(qa_ref[:, hs_], ka_ref[:, hs_], 1, 1)
                dp = _dot(dom, vv, 1, 1)
                p_tiles, ds_tiles, col_tiles = [], [], []
                row_part = jnp.zeros((tq, LANES), F32)
                for j in range(nt):
                    js = slice(j * LANES, (j + 1) * LANES)
                    p = jnp.exp2(s[:, js] - lse_b)
                    if masked:
                        p = jnp.where(_iota((tq, LANES), 0) >= _iota((tq, LANES), 1) + j * LANES, p, 0.0)
                    ds = p * (dp[:, js] - delta_b)
                    p_tiles.append(p.astype(BF16))
                    ds_tiles.append(ds.astype(BF16))
                    col_tiles.append(jnp.sum(ds, axis=0, keepdims=True))
                    row_part = row_part + ds
                p_b = jnp.concatenate(p_tiles, axis=1)
                ds_b = jnp.concatenate(ds_tiles, axis=1)
                dv_acc = dv_acc + _dot(p_b, dom, 0, 0)
                dq_acc = dq_acc + _dot(ds_b, km, 1, 0)
                dk_acc = dk_acc + _dot(ds_b, qm, 0, 0)
                dcs_acc = dcs_acc + jnp.where(_iota((8, tq), 0) == hh, jnp.concatenate(col_tiles, axis=1), 0.0)
                rowsum = jnp.sum(row_part, axis=1, keepdims=True)
                drs_acc = drs_acc + jnp.where(lane == 2 * pr + hh, rowsum, 0.0)
            drs_ref[qrows, :] += drs_acc
            dq_ref[qrows, :] += dq_acc
            dk_ref[...] += dk_acc
            dv_ref[...] += dv_acc
            dcs_ref[0] += dcs_acc

        @pl.when(qi > ki)
        def _():
            block(False)

        @pl.when(qi == ki)
        def _():
            block(True)

        @pl.when(jnp.logical_and(pr == n_pairs - 1, t == n_tri - 1))
        def _():
            x_finish()

    qmap = lambda p, t, qt, kt: (qt[t], p)
    kmap = lambda p, t, qt, kt: (kt[t], p)
    hbm = pl.BlockSpec(memory_space=pl.ANY)
    grid_spec = pltpu.PrefetchScalarGridSpec(
        num_scalar_prefetch=2, grid=(n_pairs, n_tri),
        in_specs=[pl.BlockSpec((tq, LANES), qmap), pl.BlockSpec((tq, LANES), kmap), pl.BlockSpec((tq, LANES), kmap),
                  pl.BlockSpec((tq, 2 * LANES), qmap), pl.BlockSpec((tq, 2 * LANES), kmap),
                  pl.BlockSpec((tq, LANES), qmap), pl.BlockSpec((tq, LANES), qmap), pl.BlockSpec((tq, 2 * LANES), qmap)]
        + [hbm] * n,
        out_specs=[pl.BlockSpec((T, LANES), lambda p, t, qt, kt: (0, p)), pl.BlockSpec((tq, LANES), kmap),
                   pl.BlockSpec((tq, LANES), kmap), pl.BlockSpec((1, 8, tq), lambda p, t, qt, kt: (p, 0, kt[t])),
                   pl.BlockSpec((T, LANES), lambda p, t, qt, kt: (0, 0))] + [hbm] * n,
        scratch_shapes=_chip_exchange_scratch(n))
    res = _pallas(
        body, name="fox_bwd", grid_spec=grid_spec,
        out_shape=[jax.ShapeDtypeStruct((T, FOX_W), F32)] * 3
        + [jax.ShapeDtypeStruct((n_pairs, 8, T), F32), jax.ShapeDtypeStruct((T, LANES), F32)]
        + _chip_exchange_shapes(hs),
        compiler_params=_params(dimension_semantics=("arbitrary",) * 2),
    )(q_of, k_of, qs, kn, vb, qa, ka, o, do, lse, *hs)
    res = list(res)
    return res[:5] + [res[5:]]


def _fox_post(z, dq, dk, dv, dcs, drs, bias, qg, kg, dz_buf):
    T = z.shape[0]
    tm = min(ROW_BLOCK, T)
    nb = T // tm

    def body(z_ref, dq_ref, dk_ref, dv_ref, dcs_ref, drs_ref, b_ref, qg_ref, kg_ref, _buf_ref, dz_ref, dqg_ref, dkg_ref,
             db_ref, carry_ref):
        @pl.when(pl.program_id(0) == 0)
        def _():
            carry_ref[...] = jnp.zeros_like(carry_ref)
            dqg_ref[...] = jnp.zeros_like(dqg_ref)
            dkg_ref[...] = jnp.zeros_like(dkg_ref)
            db_ref[...] = jnp.zeros_like(db_ref)

        ones = _head_ones()
        for src, g_ref, d_ref, dg_ref, scale in ((0, qg_ref, dq_ref, dqg_ref, FOX_DH ** -0.5), (1, kg_ref, dk_ref, dkg_ref, 1.0)):
            xv = z_ref[:, src * FOX_W:(src + 1) * FOX_W]
            ms = _split_dot(ones, xv * xv, 1, 0, mat_first=False) * (1.0 / FOX_DH)
            rstd = lax.rsqrt(ms + EPS)
            xh = xv * rstd
            dn = d_ref[...] * scale
            dg_ref[...] += jnp.sum(dn * xh, axis=0, keepdims=True)
            dxh = dn * g_ref[...]
            mean = _split_dot(ones, dxh * xh, 1, 0, mat_first=False) * (1.0 / FOX_DH)
            dz_ref[:, src * FOX_W:(src + 1) * FOX_W] = (rstd * (dxh - xh * mean)).astype(BF16)
        dz_ref[:, 2 * FOX_W:3 * FOX_W] = dv_ref[...].astype(BF16)
        row8 = _iota((8, tm), 0)
        dct = jnp.zeros((8, tm), F32)
        for h in range(FOX_HEADS):
            src_row = dcs_ref[h // 2][h % 2:h % 2 + 1, :]
            dct = dct + jnp.where(row8 == h, src_row, 0.0)
        dct = drs_ref[...].T[0:8] - dct
        r, c = _iota((tm, tm), 0), _iota((tm, tm), 1)
        upper_b = (r >= c).astype(BF16)
        rc = _split_dot(upper_b, dct, 1, 0, mat_first=False) + carry_ref[...]
        carry_ref[...] = rc[:, 0:1]
        full = jnp.concatenate([rc, jnp.zeros((LANES - 8, tm), F32)], axis=0)
        dlogf = full.T
        xf = z_ref[:, 3 * FOX_W:FOX_COLS] + b_ref[...]
        df = dlogf * (1.0 - _sigmoid(xf))
        dz_ref[:, 3 * FOX_W:FOX_COLS] = df.astype(BF16)
        dz_ref[:, FOX_COLS:] = jnp.zeros((tm, SEG - FOX_COLS), BF16)
        db_ref[...] += jnp.sum(df, axis=0, keepdims=True)

    rev = lambda i: (nb - 1 - i, 0)
    fix2 = lambda i: (0, 0)
    return _pallas(
        body, name="fox_post", grid=(nb,),
        in_specs=[pl.BlockSpec((tm, SEG), lambda i: (nb - 1 - i, 1)), pl.BlockSpec((tm, FOX_W), rev),
                  pl.BlockSpec((tm, FOX_W), rev),
                  pl.BlockSpec((tm, FOX_W), rev), pl.BlockSpec((FOX_HEADS // 2, 8, tm), lambda i: (0, 0, nb - 1 - i)),
                  pl.BlockSpec((tm, LANES), rev),
                  pl.BlockSpec((1, LANES), fix2), pl.BlockSpec((1, FOX_W), fix2), pl.BlockSpec((1, FOX_W), fix2),
                  pl.BlockSpec(memory_space=pl.ANY)],
        out_specs=[pl.BlockSpec((tm, SEG), lambda i: (nb - 1 - i, 1)), pl.BlockSpec((1, FOX_W), fix2),
                   pl.BlockSpec((1, FOX_W), fix2), pl.BlockSpec((1, LANES), fix2)],
        out_shape=[jax.ShapeDtypeStruct(dz_buf.shape, BF16), jax.ShapeDtypeStruct((1, FOX_W), F32),
                   jax.ShapeDtypeStruct((1, FOX_W), F32), jax.ShapeDtypeStruct((1, LANES), F32)],
        scratch_shapes=[pltpu.VMEM((8, 1), F32)],
        input_output_aliases={9: 0},
        compiler_params=_params(dimension_semantics=("arbitrary",)),
    )(z, dq, dk, dv, dcs, drs, bias, qg, kg, dz_buf)


def _local_step(x, p, tgt, sm, W, rest_chunks, core):
    lbl, og, fb = sm["hg_lb_logits"], sm["hg_onorm_g"], sm["fox_f_bias"]
    fbias = jnp.pad(fb, ((0, 0), (0, LANES - FOX_HEADS)))
    qg = jnp.tile(sm["fox_q_norm_g"], (1, FOX_HEADS))
    kg = jnp.tile(sm["fox_k_norm_g"], (1, FOX_HEADS))

    h = _rms_fwd(x, sm["norm_mix_g"], name="rms_mix")
    rest = dict(zip(BIG[1:], rest_chunks))
    first, second = ["w_ffn_gate"], ["w_ffn_up"]
    third = [n for n in BIG[1:] if n not in first + second]
    z, got1 = _matmul(h, W["w_in"], tb=True, gather=[rest[n] for n in first], name="mm_z")
    o_raw, ya, states, got2 = _hgrn_fwd(z, lbl, og, [rest[n] for n in second])
    qs, kn, vb, qa, ka = _fox_prep(z, fbias, qg, kg)
    yb, lse, got3 = _fox_fwd(qa, ka, vb, [rest[n] for n in third])
    W = dict(W, **{n: _full_of_chunks(n, g)
                   for n, g in zip(first + second + third, list(got1) + list(got2) + list(got3))})
    merged, ua, ub = _merge_fwd(ya, yb, W["w_branch_a"], W["w_branch_b"], z)
    x1, hf = _matmul(merged, W["w_out"], add=x, norm_fwd=sm["norm_ffn_g"], name="mm_x1")
    a, b, act = _swiglu_fwd(hf, W["w_ffn_gate"], W["w_ffn_up"])
    x2 = _matmul(act, W["w_ffn_down"], add=x1, name="mm_x2")
    hp, dy, dsp, dpp, loss = _ple_loss(x2, p, sm["norm_ple_g"], W["w_ple_gate"], W["w_ple_proj"], tgt)

    G = {}
    G["w_ple_proj"] = _matmul(p, dpp, ta=True, out_dtype=BF16, name="mm_dw_ple_proj")
    G["w_ple_gate"] = _matmul(hp, dsp, ta=True, out_dtype=BF16, name="mm_dw_ple_gate")
    dx2, d_ple_g = _matmul(dsp, W["w_ple_gate"], tb=True, norm_bwd=(x2, sm["norm_ple_g"], dy), name="mm_dx2")
    G["w_ffn_down"] = _matmul(act, dx2, ta=True, out_dtype=BF16, name="mm_dw_ffn_down")
    da, db = _swiglu_bwd(dx2, W["w_ffn_down"], a, b)
    G["w_ffn_gate"] = _matmul(da, hf, ta=True, out_dtype=BF16, name="mm_dw_ffn_gate")
    G["w_ffn_up"] = _matmul(db, hf, ta=True, out_dtype=BF16, name="mm_dw_ffn_up")
    dhf = _matmul(da, W["w_ffn_gate"], name="mm_dhf_a")
    dx1, d_ffn_g = _matmul(db, W["w_ffn_up"], add=dhf, norm_bwd=(x1, sm["norm_ffn_g"], dx2), name="mm_dx1")
    G["w_out"] = _matmul(merged, dx1, ta=True, out_dtype=BF16, name="mm_dw_out")
    dua, dub, dz, dya, dyb = _merge_bwd(dx1, W["w_out"], W["w_branch_a"], W["w_branch_b"], ua, ub, z)
    G["w_branch_a"] = _matmul(ya, dua, ta=True, out_dtype=BF16, name="mm_dw_branch_a")
    G["w_branch_b"] = _matmul(yb, dub, ta=True, out_dtype=BF16, name="mm_dw_branch_b")
    hb_rest = _sibling_sums({n: G[n] for n in BIG[1:]}, core, tag="rest")
    dq, dk, dv, dcs, drs, got_rest = _fox_bwd(qs, kn, vb, qa, ka, yb, dyb, lse, hb_rest)
    dz, d_qg, d_kg, d_fb = _fox_post(z, dq, dk, dv, dcs, drs, fbias, qg, kg, dz)
    dz, d_lbl, d_og = _hgrn_bwd(z, o_raw, dya, states, lbl, og, dz)
    G["w_in"] = _matmul(dz, h, ta=True, out_dtype=BF16, name="mm_dw_in")
    hb_in = _sibling_sums({"w_in": G["w_in"]}, core, tag="w_in")
    (grad_x, d_mix_g), got_in = _matmul(dz, W["w_in"], exchange=hb_in,
                                        norm_bwd=(x, sm["norm_mix_g"], dx1), name="mm_dx")

    gs = {"norm_mix_g": d_mix_g, "hg_lb_logits": d_lbl, "hg_onorm_g": d_og, "fox_f_bias": d_fb[:, :FOX_HEADS],
          "fox_q_norm_g": d_qg.reshape(FOX_HEADS, FOX_DH).sum(0, keepdims=True),
          "fox_k_norm_g": d_kg.reshape(FOX_HEADS, FOX_DH).sum(0, keepdims=True),
          "norm_ffn_g": d_ffn_g, "norm_ple_g": d_ple_g}
    return loss, grad_x, gs, hb_in + hb_rest, list(got_in) + list(got_rest)


def _pack_rows(parts, total):
    buf = jnp.concatenate(parts, axis=-2)
    pad = total - buf.shape[-2]
    widths = [(0, 0)] * (buf.ndim - 2) + [(0, pad), (0, 0)]
    return jnp.pad(buf, widths)


def _chunk_of_shard(n, w):
    if n == "w_in":
        return jnp.pad(w, ((0, IN_SHARD_PAD - IN_SHARD), (0, 0)))
    if n in ("w_ffn_gate", "w_ffn_up", "w_ffn_down"):
        return jnp.pad(w, ((0, FF_SHARD_PAD - FF_SHARD), (0, 0)))
    return w


def _full_of_chunks(n, g):
    _, a, b = g.shape
    if n == "w_in":
        w = g[:, :IN_SHARD].reshape(IN_COLS, b)
        gap = jnp.zeros((SEG - FOX_LOGICAL, b), g.dtype)
        return jnp.concatenate([w[:HG_COLS + FOX_LOGICAL], gap, w[HG_COLS + FOX_LOGICAL:]], axis=0)
    if BIG_SHAPE[n][2] == 0:
        return g.reshape(N_DEV * a, b)
    return g.transpose(1, 0, 2).reshape(a, N_DEV * b)


def _chunks_of_full(n, g):
    if n == "w_in":
        w = jnp.concatenate([g[:HG_COLS + FOX_LOGICAL], g[2 * SEG:]], axis=0).reshape(N_DEV, IN_SHARD, g.shape[1])
        return jnp.pad(w, ((0, 0), (0, IN_SHARD_PAD - IN_SHARD), (0, 0)))
    if BIG_SHAPE[n][2] == 0:
        return g.reshape(N_DEV, g.shape[0] // N_DEV, g.shape[1])
    return g.reshape(g.shape[0], N_DEV, g.shape[1] // N_DEV).transpose(1, 0, 2)


def _pack_small(vals, loss_row=None):
    parts = [vals[n].reshape(SMALL_ROWS[n], -1) for n in SMALL]
    parts = [jnp.pad(v, ((0, 0), (0, LANES - v.shape[1]))) for v in parts]
    if loss_row is not None:
        parts.append(loss_row)
    return _pack_rows(parts, SMALL_TOTAL)


def _unpack_small(buf, like):
    out, r0 = {}, 0
    for n in SMALL:
        rows, size = SMALL_ROWS[n], like[n].size
        blk = buf[r0:r0 + rows]
        out[n] = (blk if size == rows * LANES else blk[:, :size]).reshape(like[n].shape)
        r0 += rows
    return out


def _place():
    return lax.axis_index("x"), lax.axis_index("y"), lax.axis_index("c")


def _gather_steps(x_refs, out_refs, send_sems, recv_sems, local_sems):
    n = len(x_refs)
    x, y, c = _place()
    me, sibling = (x, y, c), (x, y, 1 - c)
    chips = [(1 - x, y), (x, 1 - y), (1 - x, 1 - y)]

    def slot(i, px, py, pc):
        return out_refs[i].at[4 * px + 2 * py + pc]

    def copy(k, i, blk, to, own=False):
        return pltpu.make_async_remote_copy(
            src_ref=x_refs[i] if own else slot(i, *blk), dst_ref=slot(i, *blk),
            send_sem=send_sems.at[k, i], recv_sem=recv_sems.at[k, i], device_id=to, device_id_type=MESH)

    def mine():
        return [pltpu.make_async_copy(x_refs[i], slot(i, *me), local_sems.at[i]) for i in range(n)]

    def first():
        cps = [copy(0, i, me, sibling, own=True) for i in range(n)]
        return cps + [copy(1 + j, i, me, (*chip, c), own=True) for j, chip in enumerate(chips) for i in range(n)]

    def passed():
        return [copy(4 + j, i, (*chip, c), sibling) for j, chip in enumerate(chips) for i in range(n)]

    def start():
        for cp in mine() + first():
            cp.start()

    def forward():
        fws = passed()
        for j, chip in enumerate(chips):
            for i in range(n):
                copy(1 + j, i, (*chip, c), me).wait_recv()
                fws[j * n + i].start()

    def finish():
        for i in range(n):
            copy(0, i, sibling, me).wait_recv()
        for j, chip in enumerate(chips):
            for i in range(n):
                copy(4 + j, i, (*chip, 1 - c), me).wait_recv()
        for cp in first() + passed():
            cp.wait_send()
        for cp in mine():
            cp.wait()

    return start, forward, finish


def _gather_scratch(n):
    return [pltpu.SemaphoreType.DMA((7, n)), pltpu.SemaphoreType.DMA((7, n)), pltpu.SemaphoreType.DMA((n,))]


def _all_gather(blocks, *, name):
    n = len(blocks)

    def body(*refs):
        for step in _gather_steps(refs[:n], refs[n:2 * n], *refs[2 * n:]):
            step()

    hbm = pl.BlockSpec(memory_space=pl.ANY)
    return _pallas(
        body, name=name, out_shape=[jax.ShapeDtypeStruct((N_DEV,) + b.shape, b.dtype) for b in blocks],
        in_specs=[hbm] * n, out_specs=[hbm] * n, scratch_shapes=_gather_scratch(n),
    )(*blocks)


def _sibling_exchange(gs, *, name):
    n = len(gs)

    def body(*refs):
        g_refs, out_refs = refs[:n], refs[n:2 * n]
        send_sems, recv_sems = refs[2 * n:]
        x, y, c = _place()
        cps = [pltpu.make_async_remote_copy(
            src_ref=g_refs[i].at[:, pl.ds(1 - c, 1)], dst_ref=out_refs[i], send_sem=send_sems.at[i],
            recv_sem=recv_sems.at[i], device_id=(x, y, 1 - c), device_id_type=MESH) for i in range(n)]
        for cp in cps:
            cp.start()
        for cp in cps:
            cp.wait()

    hbm = pl.BlockSpec(memory_space=pl.ANY)
    return _pallas(
        body, name=name, out_shape=[jax.ShapeDtypeStruct((N_CHIP, 1) + g.shape[2:], g.dtype) for g in gs],
        in_specs=[hbm] * n, out_specs=[hbm] * n,
        scratch_shapes=[pltpu.SemaphoreType.DMA((n,)), pltpu.SemaphoreType.DMA((n,))],
    )(*gs)


def _chip_sum(g4s, gots, core, *, name):
    n = len(g4s)

    def body(c_ref, *refs):
        for g_ref, r_ref, h_ref in zip(refs[:n], refs[n:2 * n], refs[2 * n:]):
            h_ref[0] = (g_ref[0, 0].astype(F32) + r_ref[0, 0].astype(F32)).astype(BF16)

    shapes = [g.shape[2:] for g in g4s]
    grid_spec = pltpu.PrefetchScalarGridSpec(
        num_scalar_prefetch=1, grid=(N_CHIP,),
        in_specs=[pl.BlockSpec((1, 1) + s, lambda j, c: (j, c[0], 0, 0)) for s in shapes]
        + [pl.BlockSpec((1, 1) + s, lambda j, c: (j, 0, 0, 0)) for s in shapes],
        out_specs=[pl.BlockSpec((1,) + s, lambda j, c: (j, 0, 0)) for s in shapes])
    return list(_pallas(
        body, name=name, grid_spec=grid_spec, out_shape=[jax.ShapeDtypeStruct((N_CHIP,) + s, BF16) for s in shapes],
        compiler_params=_params(dimension_semantics=("arbitrary",)),
    )(core, *g4s, *gots))


def _chip_exchange_steps(h_refs, out_refs, send_sems, recv_sems):
    n = len(h_refs)
    x, y, c = _place()
    chips = [(1 - x, y), (x, 1 - y), (1 - x, 1 - y)]

    def copies():
        return [pltpu.make_async_remote_copy(
            src_ref=h_refs[i].at[2 * px + py], dst_ref=out_refs[i].at[k], send_sem=send_sems.at[k, i],
            recv_sem=recv_sems.at[k, i], device_id=(px, py, c), device_id_type=MESH)
            for k, (px, py) in enumerate(chips) for i in range(n)]

    def start():
        for cp in copies():
            cp.start()

    def finish():
        for cp in copies():
            cp.wait()

    return start, finish


def _chip_exchange_shapes(hs):
    return [jax.ShapeDtypeStruct((3,) + h.shape[1:], h.dtype) for h in hs]


def _chip_exchange_scratch(n):
    return [pltpu.SemaphoreType.DMA((3, n)), pltpu.SemaphoreType.DMA((3, n))]


def _sibling_sums(G, core, *, tag):
    g4 = []
    for n, g in G.items():
        gc = _chunks_of_full(n, g)
        g4.append(gc.reshape((N_CHIP, 2) + gc.shape[1:]))
    got = _sibling_exchange(g4, name="grads_to_sibling_" + tag)
    return _chip_sum(g4, got, core, name="chip_sum_" + tag)


def _adam_math(w, g, m, v):
    m = ADAM_B1 * m + (1.0 - ADAM_B1) * g
    v = ADAM_B2 * v + (1.0 - ADAM_B2) * (g * g)
    m_hat = m / (1.0 - ADAM_B1 ** ADAM_STEP)
    v_hat = v / (1.0 - ADAM_B2 ** ADAM_STEP)
    delta = -ADAM_LR * (m_hat / (jnp.sqrt(v_hat) + ADAM_EPS) + ADAM_WD * w)
    return delta, m, v


def _adam_shard(hb, got, chip, w, m, v, *, name):
    _, r, c = w.shape
    _, a, cb = hb.shape
    assert cb == c and c % LANES == 0, (hb.shape, w.shape)
    tc = _pick(c, 2 * LANES)

    def body(j_ref, h_ref, r_ref, w_ref, m_ref, v_ref, g_ref, d_ref, nm_ref, nv_ref):
        parts = [h_ref[0], r_ref[0], r_ref[1], r_ref[2]]
        g = None
        for part in parts:
            part = part[:r].astype(F32)
            g = part if g is None else g + part
        d, nm, nv = _adam_math(w_ref[0], g, m_ref[0], v_ref[0])
        g_ref[0] = g
        d_ref[0] = d
        nm_ref[0] = nm
        nv_ref[0] = nv

    blk = pl.BlockSpec((1, r, tc), lambda i, j: (0, 0, i))
    grid_spec = pltpu.PrefetchScalarGridSpec(
        num_scalar_prefetch=1, grid=(c // tc,),
        in_specs=[pl.BlockSpec((1, a, tc), lambda i, j: (j[0], 0, i)),
                  pl.BlockSpec((3, a, tc), lambda i, j: (0, 0, i)), blk, blk, blk],
        out_specs=[blk] * 4)
    return _pallas(
        body, name=name, grid_spec=grid_spec, out_shape=[jax.ShapeDtypeStruct((1, r, c), F32)] * 4,
        compiler_params=_params(dimension_semantics=("arbitrary",)),
    )(chip, hb, got, w, m, v)


def _small_all_reduce_adam(gs, w, m, v):
    def body(g_ref, w_ref, m_ref, v_ref, sum_ref, d_ref, nm_ref, nv_ref, gather, send_sems, recv_sems):
        x, y, c = _place()
        my = 4 * x + 2 * y + c
        gather[my] = g_ref[...]
        cps = []
        for k in range(1, N_DEV):
            to = (x ^ (k >> 2), y ^ ((k >> 1) & 1), c ^ (k & 1))
            cps.append(pltpu.make_async_remote_copy(
                src_ref=g_ref, dst_ref=gather.at[my], send_sem=send_sems.at[k - 1], recv_sem=recv_sems.at[k - 1],
                device_id=to, device_id_type=MESH))
        for cp in cps:
            cp.start()
        for cp in cps:
            cp.wait()
        total = gather[0]
        for d in range(1, N_DEV):
            total = total + gather[d]
        dlt, nm, nv = _adam_math(w_ref[...], total, m_ref[...], v_ref[...])
        sum_ref[...] = total
        d_ref[...] = dlt
        nm_ref[...] = nm
        nv_ref[...] = nv

    vm = pl.BlockSpec(memory_space=pltpu.VMEM)
    return _pallas(
        body, name="small_all_reduce_adam", out_shape=[jax.ShapeDtypeStruct((SMALL_TOTAL, LANES), F32)] * 4,
        in_specs=[vm] * 4, out_specs=[vm] * 4,
        scratch_shapes=[pltpu.VMEM((N_DEV, SMALL_TOTAL, LANES), F32), pltpu.SemaphoreType.DMA((7,)),
                        pltpu.SemaphoreType.DMA((7,))],
            )(gs, w, m, v)


def kernel(x, p, norm_mix_g, w_in, hg_lb_logits, hg_onorm_g, fox_f_bias, fox_q_norm_g, fox_k_norm_g, w_branch_a, w_branch_b, w_out, norm_ffn_g, w_ffn_gate, w_ffn_up, w_ffn_down, norm_ple_g, w_ple_gate, w_ple_proj, loss_target, m_norm_mix_g, m_w_in, m_hg_lb_logits, m_hg_onorm_g, m_fox_f_bias, m_fox_q_norm_g, m_fox_k_norm_g, m_w_branch_a, m_w_branch_b, m_w_out, m_norm_ffn_g, m_w_ffn_gate, m_w_ffn_up, m_w_ffn_down, m_norm_ple_g, m_w_ple_gate, m_w_ple_proj, v_norm_mix_g, v_w_in, v_hg_lb_logits, v_hg_onorm_g, v_fox_f_bias, v_fox_q_norm_g, v_fox_k_norm_g, v_w_branch_a, v_w_branch_b, v_w_out, v_norm_ffn_g, v_w_ffn_gate, v_w_ffn_up, v_w_ffn_down, v_norm_ple_g, v_w_ple_gate, v_w_ple_proj):
    args = dict(locals())
    wts = {n: args[n] for n in BIG + SMALL}
    mom = {n: args["m_" + n] for n in BIG + SMALL}
    var = {n: args["v_" + n] for n in BIG + SMALL}
    for group in (wts, mom, var):
        for n in TRANSPOSED:
            group[n] = jnp.swapaxes(group[n], 1, 2)
    sm = {n: wts[n] for n in SMALL}

    xi, yi, ci = _place()
    core = jnp.reshape(ci, (1,)).astype(jnp.int32)
    chip = jnp.reshape(2 * xi + yi, (1,)).astype(jnp.int32)
    chunks = [_chunk_of_shard(n, wts[n][0].astype(BF16)) for n in BIG]
    assert BIG[0] == "w_in"
    w_in_full = _full_of_chunks("w_in", _all_gather(chunks[:1], name="w_in_all_gather")[0])

    loss_blk, grad_x, gs, hb, got = _local_step(
        x[0], p[0, 0], loss_target[0], sm, {"w_in": w_in_full}, chunks[1:], core)

    g_big, d_big, nm_big, nv_big = {}, {}, {}, {}
    for n, h, r in zip(BIG, hb, got):
        res = _adam_shard(h, r, chip, wts[n], mom[n], var[n], name="adam_" + n)
        if n in TRANSPOSED:
            res = [jnp.swapaxes(t, 1, 2) for t in res]
        g_big[n], d_big[n], nm_big[n], nv_big[n] = res

    s_sum, s_d, s_nm, s_nv = _small_all_reduce_adam(
        _pack_small(gs, loss_blk[0:1]), _pack_small(sm), _pack_small({n: mom[n] for n in SMALL}),
        _pack_small({n: var[n] for n in SMALL}))
    loss = s_sum[LOSS_ROW, 0]
    g_small, d_small, nm_small, nv_small = (_unpack_small(t, sm) for t in (s_sum, s_d, s_nm, s_nv))

    order = ["norm_mix_g", "w_in", "hg_lb_logits", "hg_onorm_g", "fox_f_bias", "fox_q_norm_g", "fox_k_norm_g",
             "w_branch_a", "w_branch_b", "w_out", "norm_ffn_g", "w_ffn_gate", "w_ffn_up", "w_ffn_down", "norm_ple_g",
             "w_ple_gate", "w_ple_proj"]
    outs = [loss, grad_x[None]]
    for big, small in ((g_big, g_small), (d_big, d_small), (nm_big, nm_small), (nv_big, nv_small)):
        outs += [big[n] if n in big else small[n] for n in order]
    return tuple(outs)
```

```python
import functools

import jax
import jax.numpy as jnp
from jax import lax
from jax.experimental import pallas as pl
from jax.experimental.pallas import tpu as pltpu

F32 = jnp.float32
BF16 = jnp.bfloat16

D_MODEL = 1024
PLE_DIM = 256
HG_HEADS = 4
HG_DK = 128
HG_CHUNK = 64
HG_SUB = 16
HG_W = HG_HEADS * HG_DK
FOX_HEADS = 8
FOX_DH = 64
FOX_W = FOX_HEADS * FOX_DH
D_FF = 2816
EPS = 1e-6
N_DEV = 8
N_CHIP = 4
LANES = 128
FOX_COLS = 3 * FOX_W + LANES
HG_COLS = 4 * HG_W
GATE_COLS = 2 * D_MODEL
IN_COLS = HG_COLS + 3 * FOX_W + FOX_HEADS + GATE_COLS
FOX_LOGICAL = 3 * FOX_W + FOX_HEADS
SEG = 2048
IN_PAD = 3 * SEG
IN_SHARD = IN_COLS // N_DEV
IN_SHARD_PAD = 768
FF_SHARD = D_FF // N_DEV
FF_SHARD_PAD = 384
EXP_CLAMP = 80.0
LOG2E = 1.4426950408889634

ADAM_LR = 0.001
ADAM_B1 = 0.9
ADAM_B2 = 0.999
ADAM_EPS = 1e-08
ADAM_WD = 0.01
ADAM_STEP = 10

MESH = pl.DeviceIdType.MESH
VMEM_LIMIT = 56 * 1024 * 1024
ROW_BLOCK = 512

BIG = ["w_in", "w_branch_a", "w_branch_b", "w_out", "w_ffn_gate", "w_ffn_up", "w_ffn_down",
       "w_ple_gate", "w_ple_proj"]
TRANSPOSED = ("w_in", "w_ffn_gate", "w_ffn_up")
BIG_SHAPE = {
    "w_in": (IN_COLS, D_MODEL, 0), "w_branch_a": (HG_W, D_MODEL, 1), "w_branch_b": (FOX_W, D_MODEL, 1),
    "w_out": (D_MODEL, D_MODEL, 0), "w_ffn_gate": (D_FF, D_MODEL, 0), "w_ffn_up": (D_FF, D_MODEL, 0),
    "w_ffn_down": (D_FF, D_MODEL, 0), "w_ple_gate": (D_MODEL, D_MODEL, 0), "w_ple_proj": (PLE_DIM, D_MODEL, 1),
}

SMALL = ["norm_mix_g", "hg_lb_logits", "hg_onorm_g", "fox_f_bias", "fox_q_norm_g", "fox_k_norm_g",
         "norm_ffn_g", "norm_ple_g"]
SMALL_ROWS = {"norm_mix_g": 8, "hg_lb_logits": 8, "hg_onorm_g": 1, "fox_f_bias": 1, "fox_q_norm_g": 1,
              "fox_k_norm_g": 1, "norm_ffn_g": 8, "norm_ple_g": 8}
SMALL_TOTAL = 40
LOSS_ROW = 36


def _pallas(body, **kw):
    return pl.pallas_call(body, **kw)


def _params(**kw):
    return pltpu.CompilerParams(vmem_limit_bytes=VMEM_LIMIT, **kw)


def _pick(n, target):
    if n <= target:
        return n
    best = None
    for t in range(LANES, target + 1, LANES):
        if n % t == 0:
            best = t
    assert best is not None, (n, target)
    return best


def _dot(a, b, ca, cb):
    return lax.dot_general(a, b, (((ca,), (cb,)), ((), ())), preferred_element_type=F32)


def _split_dot(mat, x, ca, cb, terms=2, mat_first=True):
    acc = None
    rem = x
    for _ in range(terms):
        part = rem.astype(BF16)
        rem = rem - part.astype(F32)
        p = _dot(mat, part, ca, cb) if mat_first else _dot(part, mat, ca, cb)
        acc = p if acc is None else acc + p
    return acc


def _sigmoid(x):
    return 1.0 / (1.0 + jnp.exp(-x))


def _iota(shape, dim):
    return lax.broadcasted_iota(jnp.int32, shape, dim)


def _matmul(a, b, *, name, ta=False, tb=False, out_dtype=F32, add=None, exchange=None, gather=None,
            norm_fwd=None, norm_bwd=None):
    assert exchange is None or gather is None
    (K, M) = a.shape if ta else a.shape[::-1]
    (N, Kb) = b.shape if tb else b.shape[::-1]
    assert K == Kb, (a.shape, b.shape, ta, tb)
    if ta:
        tm, tn, tk = _pick(M, 2 * ROW_BLOCK), _pick(N, 2 * ROW_BLOCK), _pick(K, 4 * ROW_BLOCK)
    else:
        tm, tn, tk = _pick(M, 2 * ROW_BLOCK), _pick(N, 2048), _pick(K, 3072)
    if norm_bwd is not None:
        tm = _pick(M, ROW_BLOCK)
    nk = K // tk
    use_scratch = nk > 1 and out_dtype != F32
    if norm_fwd is not None or norm_bwd is not None:
        assert tn == N and not use_scratch and out_dtype == F32

    hs = list(exchange or gather or [])
    n_x = len(hs)
    grid = (M // tm, N // tn, nk)
    a_spec = pl.BlockSpec((tk, tm), lambda i, j, k: (k, i)) if ta else pl.BlockSpec((tm, tk), lambda i, j, k: (i, k))
    b_spec = pl.BlockSpec((tn, tk), lambda i, j, k: (j, k)) if tb else pl.BlockSpec((tk, tn), lambda i, j, k: (k, j))
    o_spec = pl.BlockSpec((tm, tn), lambda i, j, k: (i, j))
    row_vec = pl.BlockSpec((1, N), lambda i, j, k: (0, 0))
    hbm = pl.BlockSpec(memory_space=pl.ANY)
    extra_in = [(add, o_spec)] if add is not None else []
    extra_out = []
    if norm_fwd is not None:
        extra_in += [(norm_fwd, row_vec)]
        extra_out += [(jax.ShapeDtypeStruct((M, N), BF16), o_spec)]
    if norm_bwd is not None:
        extra_in += [(norm_bwd[0], o_spec), (norm_bwd[1], row_vec), (norm_bwd[2], o_spec)]
        extra_out += [(jax.ShapeDtypeStruct((1, N), F32), row_vec)]
    if gather is not None:
        ride_shapes, ride_scratch = [jax.ShapeDtypeStruct((N_DEV,) + h.shape, h.dtype) for h in hs], _gather_scratch(n_x)
    else:
        ride_shapes, ride_scratch = _chip_exchange_shapes(hs), (_chip_exchange_scratch(n_x) if n_x else [])
    n_ex_in, n_ex_out = len(extra_in), len(extra_out)

    def body(*refs):
        refs = list(refs)
        a_ref, b_ref = refs[:2]
        ex_in = refs[2:2 + n_ex_in]
        ride_in = refs[2 + n_ex_in:2 + n_ex_in + n_x]
        base = 2 + n_ex_in + n_x
        o_ref = refs[base]
        ex_out = refs[base + 1:base + 1 + n_ex_out]
        ride_out = refs[base + 1 + n_ex_out:base + 1 + n_ex_out + n_x]
        scratch = refs[base + 1 + n_ex_out + n_x:]
        at = [pl.program_id(d) for d in range(3)]
        k = at[2]
        if n_x:
            steps = _gather_steps if gather is not None else _chip_exchange_steps
            ride = steps(ride_in, ride_out, *scratch[-len(ride_scratch):])

            @pl.when(jnp.logical_and(at[0] == 0, jnp.logical_and(at[1] == 0, at[2] == 0)))
            def _():
                ride[0]()
        p = _dot(a_ref[...].astype(BF16), b_ref[...].astype(BF16), 0 if ta else 1, 1 if tb else 0)

        def finish(r):
            ins = list(ex_in)
            outs = list(ex_out)
            if add is not None:
                r = r + ins.pop(0)[...].astype(F32)
            if norm_fwd is not None:
                g_ref = ins.pop(0)
                rstd = lax.rsqrt(jnp.mean(r * r, axis=-1, keepdims=True) + EPS)
                outs.pop(0)[...] = (r * rstd * g_ref[...]).astype(BF16)
            if norm_bwd is not None:
                x_ref, g_ref, dres_ref = ins.pop(0), ins.pop(0), ins.pop(0)
                dg_ref = outs.pop(0)
                xv = x_ref[...]
                rstd = lax.rsqrt(jnp.mean(xv * xv, axis=-1, keepdims=True) + EPS)
                xh = xv * rstd
                part = jnp.sum(r * xh, axis=0, keepdims=True)

                @pl.when(at[0] == 0)
                def _():
                    dg_ref[...] = part

                @pl.when(at[0] > 0)
                def _():
                    dg_ref[...] += part

                dxh = r * g_ref[...]
                r = rstd * (dxh - xh * jnp.mean(dxh * xh, axis=-1, keepdims=True)) + dres_ref[...]
            o_ref[...] = r.astype(out_dtype)

        if nk == 1:
            finish(p)
        elif not use_scratch:
            @pl.when(k == 0)
            def _():
                o_ref[...] = p

            @pl.when(jnp.logical_and(k > 0, k < nk - 1))
            def _():
                o_ref[...] += p

            @pl.when(k == nk - 1)
            def _():
                finish(o_ref[...] + p)
        else:
            acc_ref = scratch[0]

            @pl.when(k == 0)
            def _():
                acc_ref[...] = p

            @pl.when(k > 0)
            def _():
                acc_ref[...] += p

            @pl.when(k == nk - 1)
            def _():
                finish(acc_ref[...])

        if n_x:
            @pl.when(jnp.logical_and(at[0] == grid[0] - 1, jnp.logical_and(at[1] == grid[1] - 1, at[2] == nk - 1)))
            def _():
                for step in ride[1:]:
                    step()

    res = _pallas(
        body, name=name, grid=grid,
        in_specs=[a_spec, b_spec] + [s for _, s in extra_in] + [hbm] * n_x,
        out_specs=[o_spec] + [s for _, s in extra_out] + [hbm] * n_x,
        out_shape=[jax.ShapeDtypeStruct((M, N), out_dtype)] + [s for s, _ in extra_out] + ride_shapes,
        scratch_shapes=([pltpu.VMEM((tm, tn), F32)] if use_scratch else []) + ride_scratch,
        compiler_params=_params(dimension_semantics=("arbitrary",) * 3),
    )(a, b, *[v for v, _ in extra_in], *hs)
    res = list(res)
    main = res[0] if n_ex_out == 0 else tuple(res[:1 + n_ex_out])
    return (main, res[1 + n_ex_out:]) if n_x else main


def _row_map(nb, reverse, seg):
    if reverse:
        return lambda i: (nb - 1 - i, seg)
    return lambda i: (i, seg)


def _row_call(body, *, name, T, ins, outs, acc_outs=(), tm=ROW_BLOCK, reverse=False):
    tm = min(tm, T)
    nb = T // tm
    in_specs, args = [], []
    for arr, how in ins:
        args.append(arr)
        if how is True:
            in_specs.append(pl.BlockSpec((tm, arr.shape[1]), _row_map(nb, reverse, 0)))
        elif how is False:
            in_specs.append(pl.BlockSpec(arr.shape, lambda i, _n=arr.ndim: (0,) * _n))
        else:
            in_specs.append(pl.BlockSpec((tm, SEG), _row_map(nb, reverse, how[0])))
    out_specs, out_shape = [], []
    for o in outs:
        c, dt = o[0], o[1]
        total, seg = o[2] if len(o) > 2 else (c, 0)
        out_specs.append(pl.BlockSpec((tm, c), _row_map(nb, reverse, seg)))
        out_shape.append(jax.ShapeDtypeStruct((T, total), dt))
    for shp, dt in acc_outs:
        out_specs.append(pl.BlockSpec(shp, lambda i, _n=len(shp): (0,) * _n))
        out_shape.append(jax.ShapeDtypeStruct(shp, dt))
    return _pallas(body, name=name, grid=(nb,), in_specs=in_specs, out_specs=out_specs, out_shape=out_shape,
                   compiler_params=_params(dimension_semantics=("arbitrary",)))(*args)


def _rms_fwd(x, g, *, name):
    T = x.shape[0]

    def body(x_ref, g_ref, h_ref):
        xv = x_ref[...]
        rstd = lax.rsqrt(jnp.mean(xv * xv, axis=-1, keepdims=True) + EPS)
        h_ref[...] = (xv * rstd * g_ref[...]).astype(BF16)

    return _row_call(body, name=name, T=T, ins=[(x, True), (g, False)], outs=[(D_MODEL, BF16)])[0]


def _merge_fwd(ya, yb, wa, wb, zg):
    def body(ya_ref, yb_ref, wa_ref, wb_ref, zg_ref, m_ref, ua_ref, ub_ref):
        ua = _dot(ya_ref[...].astype(BF16), wa_ref[...], 1, 0)
        ub = _dot(yb_ref[...].astype(BF16), wb_ref[...], 1, 0)
        ga = _sigmoid(zg_ref[:, :D_MODEL])
        gb = _sigmoid(zg_ref[:, D_MODEL:])
        m_ref[...] = (ga * ua + gb * ub).astype(BF16)
        ua_ref[...] = ua.astype(BF16)
        ub_ref[...] = ub.astype(BF16)

    return _row_call(body, name="merge_fwd", T=ya.shape[0],
                     ins=[(ya, True), (yb, True), (wa, False), (wb, False), (zg, (2,))],
                     outs=[(D_MODEL, BF16)] * 3)


def _merge_bwd(dx1, w_out, wa, wb, ua, ub, zg):
    def body(dx_ref, wo_ref, wa_ref, wb_ref, ua_ref, ub_ref, zg_ref, dua_ref, dub_ref, dzg_ref, dya_ref, dyb_ref):
        dmv = _dot(dx_ref[...].astype(BF16), wo_ref[...], 1, 1)
        ga = _sigmoid(zg_ref[:, :D_MODEL])
        gb = _sigmoid(zg_ref[:, D_MODEL:])
        dua = (dmv * ga).astype(BF16)
        dub = (dmv * gb).astype(BF16)
        dua_ref[...] = dua
        dub_ref[...] = dub
        dzg_ref[:, :D_MODEL] = (dmv * ua_ref[...].astype(F32) * ga * (1.0 - ga)).astype(BF16)
        dzg_ref[:, D_MODEL:] = (dmv * ub_ref[...].astype(F32) * gb * (1.0 - gb)).astype(BF16)
        dya_ref[...] = _dot(dua, wa_ref[...], 1, 1)
        dyb_ref[...] = _dot(dub, wb_ref[...], 1, 1)

    return _row_call(body, name="merge_bwd", T=dx1.shape[0],
                     ins=[(dx1, True), (w_out, False), (wa, False), (wb, False), (ua, True), (ub, True), (zg, (2,))],
                     outs=[(D_MODEL, BF16), (D_MODEL, BF16), (SEG, BF16, (IN_PAD, 2)), (HG_W, F32), (FOX_W, F32)])


def _swiglu_fwd(hf, w_gate, w_up):
    T, D = hf.shape
    F = w_gate.shape[0]
    tm, tn = _pick(T, 2 * ROW_BLOCK), _pick(F, 768)

    def body(h_ref, wg_ref, wu_ref, a_ref, b_ref, o_ref):
        hv = h_ref[...]
        a_b = _dot(hv, wg_ref[...], 1, 1).astype(BF16)
        b_b = _dot(hv, wu_ref[...], 1, 1).astype(BF16)
        a_ref[...] = a_b
        b_ref[...] = b_b
        av = a_b.astype(F32)
        o_ref[...] = (av * _sigmoid(av) * b_b.astype(F32)).astype(BF16)

    tile = pl.BlockSpec((tm, tn), lambda i, j: (i, j))
    wcol = pl.BlockSpec((tn, D), lambda i, j: (j, 0))
    return _pallas(
        body, name="swiglu_fwd", grid=(T // tm, F // tn),
        in_specs=[pl.BlockSpec((tm, D), lambda i, j: (i, 0)), wcol, wcol],
        out_specs=[tile] * 3, out_shape=[jax.ShapeDtypeStruct((T, F), BF16)] * 3,
        compiler_params=_params(dimension_semantics=("arbitrary",) * 2),
    )(hf, w_gate, w_up)


def _swiglu_bwd(dx, w_down, a, b):
    T, D = dx.shape
    F = w_down.shape[0]
    tm, tn = _pick(T, 2 * ROW_BLOCK), _pick(F, 768)

    def body(dx_ref, w_ref, a_ref, b_ref, da_ref, db_ref):
        dact = _dot(dx_ref[...].astype(BF16), w_ref[...], 1, 1)
        av = a_ref[...].astype(F32)
        bv = b_ref[...].astype(F32)
        sg = _sigmoid(av)
        da_ref[...] = (dact * bv * sg * (1.0 + av * (1.0 - sg))).astype(BF16)
        db_ref[...] = (dact * av * sg).astype(BF16)

    tile = pl.BlockSpec((tm, tn), lambda i, j: (i, j))
    return _pallas(
        body, name="swiglu_bwd", grid=(T // tm, F // tn),
        in_specs=[pl.BlockSpec((tm, D), lambda i, j: (i, 0)), pl.BlockSpec((tn, D), lambda i, j: (j, 0)), tile, tile],
        out_specs=[tile, tile], out_shape=[jax.ShapeDtypeStruct((T, F), BF16)] * 2,
        compiler_params=_params(dimension_semantics=("arbitrary",) * 2),
    )(dx, w_down, a, b)


def _ple_loss(x2, p, g, w_gate, w_proj, tgt):
    def body(x_ref, p_ref, g_ref, wg_ref, wp_ref, t_ref, hp_ref, dy_ref, dsp_ref, dpp_ref, loss_ref):
        xv = x_ref[...]
        rstd = lax.rsqrt(jnp.mean(xv * xv, axis=-1, keepdims=True) + EPS)
        hp = (xv * rstd * g_ref[...]).astype(BF16)
        hp_ref[...] = hp
        gp = _sigmoid(_dot(hp, wg_ref[...], 1, 0))
        ppv = _dot(p_ref[...].astype(BF16), wp_ref[...], 1, 0)
        err = xv + gp * ppv - t_ref[...]
        part = 0.5 * jnp.sum(jnp.mean(err * err, axis=-1, keepdims=True), axis=0, keepdims=True)
        part = jnp.broadcast_to(part, loss_ref.shape)

        @pl.when(pl.program_id(0) == 0)
        def _():
            loss_ref[...] = part

        @pl.when(pl.program_id(0) > 0)
        def _():
            loss_ref[...] += part

        dy = err * (1.0 / D_MODEL)
        dy_ref[...] = dy
        dsp_ref[...] = (dy * ppv * gp * (1.0 - gp)).astype(BF16)
        dpp_ref[...] = (dy * gp).astype(BF16)

    return _row_call(body, name="ple_loss", T=x2.shape[0],
                     ins=[(x2, True), (p, True), (g, False), (w_gate, False), (w_proj, False), (tgt, True)],
                     outs=[(D_MODEL, BF16), (D_MODEL, F32), (D_MODEL, BF16), (D_MODEL, BF16)],
                     acc_outs=[((8, LANES), F32)])


def _hg_consts():
    C = HG_CHUNK
    r, c = _iota((C, C), 0), _iota((C, C), 1)
    tri = (c <= r)
    same = (r // HG_SUB) == (c // HG_SUB)
    return tri, (tri & same)


def _hg_chunk_fwd(q, f, lb, tri_b, sub_b):
    sgq = _sigmoid(q)
    qt = q * sgq
    sg = _sigmoid(f)
    fg = lb + (1.0 - lb) * sg
    kf = (1.0 - lb) * (1.0 - sg)
    logf = jnp.log(fg)
    b = _split_dot(tri_b, logf, 1, 0)
    w = _split_dot(sub_b, logf, 1, 0)
    return sgq, qt, sg, fg, kf, b, w


def _hg_scores(qs_b, kf, b, row):
    C, S = HG_CHUNK, HG_SUB
    parts, ks = [], []
    for blk in range(C // S):
        ref = jnp.zeros_like(b[0:1]) if blk == 0 else b[blk * S - 1:blk * S]
        e = jnp.exp(jnp.minimum(ref - b, EXP_CLAMP))
        e = jnp.where(row < (blk + 1) * S, e, 0.0)
        k_b = (kf * e).astype(BF16)
        ks.append((e, k_b))
        parts.append(_dot(qs_b[blk * S:(blk + 1) * S], k_b, 1, 1))
    return jnp.concatenate(parts, axis=0), ks


def _hgrn_fwd(z, lb_logits, gain, blocks):
    T = z.shape[0]
    RB = min(ROW_BLOCK, T)
    nb, cpb = T // RB, RB // HG_CHUNK
    C, DK = HG_CHUNK, HG_DK
    n = len(blocks)

    def body(*refs):
        z_ref, lg_ref, g_ref = refs[:3]
        o_ref, y_ref, st_ref = refs[3 + n:6 + n]
        s_ref = refs[6 + 2 * n]
        g_start, g_forward, g_finish = _gather_steps(refs[3:3 + n], refs[6 + n:6 + 2 * n], *refs[7 + 2 * n:])

        @pl.when(pl.program_id(0) == 0)
        def _():
            s_ref[...] = jnp.zeros_like(s_ref)
            g_start()

        lg = lg_ref[...]
        lb_all = 1.0 / (1.0 + jnp.exp(lg[1:2] - lg[0:1]))
        gain_v = g_ref[...]
        tri, sub = _hg_consts()
        tri_b, sub_b = tri.astype(BF16), sub.astype(BF16)
        row = _iota((C, DK), 0)

        def chunk(ci, carry):
            r0 = pl.multiple_of(ci * C, C)
            rows = pl.ds(r0, C)
            _, qt, _, _, kf_all, b_all, w = _hg_chunk_fwd(z_ref[rows, 0:HG_W], z_ref[rows, HG_W:2 * HG_W], lb_all,
                                                          tri_b, sub_b)
            qs_all = (qt * jnp.exp(w)).astype(BF16)
            qd_all = (qt * jnp.exp(b_all)).astype(BF16)
            bl_all = b_all[C - 1:C]
            kd_all = (kf_all * jnp.exp(bl_all - b_all)).astype(BF16)
            ebl_all = jnp.exp(bl_all)
            v_all = z_ref[rows, 2 * HG_W:3 * HG_W].astype(BF16)
            g_all = z_ref[rows, 3 * HG_W:4 * HG_W]
            gate_all = g_all * _sigmoid(g_all)
            for h in range(HG_HEADS):
                cs = slice(h * DK, (h + 1) * DK)
                st = s_ref[h]
                st_ref[pl.ds(pl.multiple_of((ci * HG_HEADS + h) * DK, DK), DK), :] = st
                v_b = v_all[:, cs]
                a, _ = _hg_scores(qs_all[:, cs], kf_all[:, cs], b_all[:, cs], row)
                a = jnp.where(tri, a, 0.0)
                o = _dot(qd_all[:, cs], st.astype(BF16), 1, 1) + _dot(a.astype(BF16), v_b, 1, 0)
                s_ref[h] = st * ebl_all[:, cs] + _dot(v_b, kd_all[:, cs], 0, 0)
                o_ref[rows, cs] = o
                rstd = lax.rsqrt(jnp.mean(o * o, axis=-1, keepdims=True) + EPS)
                y_ref[rows, cs] = (o * rstd * gain_v * gate_all[:, cs]).astype(BF16)
            return carry

        lax.fori_loop(0, cpb, chunk, 0, unroll=2)

        @pl.when(pl.program_id(0) == nb - 1)
        def _():
            g_forward()
            g_finish()

    hbm = pl.BlockSpec(memory_space=pl.ANY)
    res = _pallas(
        body, name="hgrn_fwd", grid=(nb,),
        in_specs=[pl.BlockSpec((RB, HG_COLS), lambda i: (i, 0)), pl.BlockSpec((2, HG_W), lambda i: (0, 0)),
                  pl.BlockSpec((1, DK), lambda i: (0, 0))] + [hbm] * n,
        out_specs=[pl.BlockSpec((RB, HG_W), lambda i: (i, 0)), pl.BlockSpec((RB, HG_W), lambda i: (i, 0)),
                   pl.BlockSpec((cpb * HG_HEADS * DK, DK), lambda i: (i, 0))] + [hbm] * n,
        out_shape=[jax.ShapeDtypeStruct((T, HG_W), F32), jax.ShapeDtypeStruct((T, HG_W), BF16),
                   jax.ShapeDtypeStruct((T // C * HG_HEADS * DK, DK), F32)]
        + [jax.ShapeDtypeStruct((N_DEV,) + b.shape, b.dtype) for b in blocks],
        scratch_shapes=[pltpu.VMEM((HG_HEADS, DK, DK), F32)] + _gather_scratch(n),
        compiler_params=_params(dimension_semantics=("arbitrary",)),
    )(z, lb_logits, gain, *blocks)
    return res[0], res[1], res[2], res[3:]


def _hgrn_bwd(z, o_raw, dy, states, lb_logits, gain, dz_buf):
    T = z.shape[0]
    RB = min(ROW_BLOCK, T)
    nb, cpb = T // RB, RB // HG_CHUNK
    C, DK, S = HG_CHUNK, HG_DK, HG_SUB

    def body(z_ref, o_ref, dy_ref, st_ref, lg_ref, g_ref, _buf_ref, dz_ref, dlg_ref, dg_ref, ds_ref, dlb_ref):
        step = pl.program_id(0)

        @pl.when(step == 0)
        def _():
            ds_ref[...] = jnp.zeros_like(ds_ref)
            dlb_ref[...] = jnp.zeros_like(dlb_ref)
            dg_ref[...] = jnp.zeros_like(dg_ref)

        lg = lg_ref[...]
        lb_all = 1.0 / (1.0 + jnp.exp(lg[1:2] - lg[0:1]))
        gain_v = g_ref[...]
        tri, sub = _hg_consts()
        tri_b, sub_b = tri.astype(BF16), sub.astype(BF16)
        row = _iota((C, DK), 0)

        def chunk(cj, carry):
            ci = cpb - 1 - cj
            r0 = pl.multiple_of(ci * C, C)
            rows = pl.ds(r0, C)
            q_all = z_ref[rows, 0:HG_W]
            g_all = z_ref[rows, 3 * HG_W:4 * HG_W]
            sgq_all, qt_all, sg_all, fg_all, kf_all, b_all, w_all = _hg_chunk_fwd(
                q_all, z_ref[rows, HG_W:2 * HG_W], lb_all, tri_b, sub_b)
            ew_all = jnp.exp(w_all)
            eb_all = jnp.exp(b_all)
            bl_all = b_all[C - 1:C]
            ebl_all = jnp.exp(bl_all)
            ekd_all = jnp.exp(bl_all - b_all)
            qs_all = (qt_all * ew_all).astype(BF16)
            qd_all = (qt_all * eb_all).astype(BF16)
            kd_all = (kf_all * ekd_all).astype(BF16)
            v_all = z_ref[rows, 2 * HG_W:3 * HG_W].astype(BF16)
            sgg_all = _sigmoid(g_all)
            t1_all = dy_ref[rows, :] * (g_all * sgg_all)
            db_heads, dqt_heads, dkf_heads, dv_heads, n_heads = [], [], [], [], []
            for h in range(HG_HEADS):
                cs = slice(h * DK, (h + 1) * DK)
                kf, b, ew, eb, ebl, ekd = kf_all[:, cs], b_all[:, cs], ew_all[:, cs], eb_all[:, cs], ebl_all[:, cs], \
                    ekd_all[:, cs]
                qs_b, qd_b, kd_b, v_b = qs_all[:, cs], qd_all[:, cs], kd_all[:, cs], v_all[:, cs]
                st = st_ref[pl.ds(pl.multiple_of((ci * HG_HEADS + h) * DK, DK), DK), :]
                dst = ds_ref[h]
                o = o_ref[rows, cs]
                rstd = lax.rsqrt(jnp.mean(o * o, axis=-1, keepdims=True) + EPS)
                n = o * rstd
                n_heads.append(n)
                t1 = t1_all[:, cs]
                dg_ref[...] += jnp.sum(t1 * n, axis=0, keepdims=True)
                dn = t1 * gain_v
                do = rstd * (dn - n * jnp.mean(dn * n, axis=-1, keepdims=True))
                do_b = do.astype(BF16)
                a, ks = _hg_scores(qs_b, kf, b, row)
                a = jnp.where(tri, a, 0.0)
                dst_b = dst.astype(BF16)
                dqd = _dot(do_b, st.astype(BF16), 1, 0)
                da = jnp.where(tri, _dot(do_b, v_b, 1, 1), 0.0)
                dv = _dot(a.astype(BF16), do_b, 0, 0) + _dot(kd_b, dst_b, 1, 1)
                dkd = _dot(v_b, dst_b, 1, 0)
                ds_ref[h] = dst * ebl + _dot(do_b, qd_b, 0, 0)
                dkd_kd = dkd * kd_b.astype(F32)
                dbl = ebl * jnp.sum(dst * st, axis=0, keepdims=True) + jnp.sum(dkd_kd, axis=0, keepdims=True)
                da_b = da.astype(BF16)
                dqs_parts = []
                dk_in = jnp.zeros((C, DK), F32)
                db_k = jnp.zeros((C, DK), F32)
                for blk in range(C // S):
                    e, k_b = ks[blk]
                    da_blk = da_b[blk * S:(blk + 1) * S]
                    dqs_parts.append(_dot(da_blk, k_b, 1, 0))
                    dks = _dot(da_blk, qs_b[blk * S:(blk + 1) * S], 0, 0)
                    dk_in = dk_in + dks * e
                    db_k = db_k + dks * k_b.astype(F32)
                dqs = jnp.concatenate(dqs_parts, axis=0)
                db = qs_b.astype(F32) * dqs - db_k + dqd * qd_b.astype(F32) - dkd_kd
                db_heads.append(db + jnp.where(row == C - 1, dbl, 0.0))
                dqt_heads.append(dqs * ew + dqd * eb)
                dkf_heads.append(dk_in + dkd * ekd)
                dv_heads.append(dv)
            dlogf = _split_dot(tri_b, jnp.concatenate(db_heads, axis=1), 0, 0)
            dfg = dlogf / fg_all - jnp.concatenate(dkf_heads, axis=1)
            dlb_ref[...] += jnp.sum(dfg * (1.0 - sg_all), axis=0, keepdims=True)
            dqt = jnp.concatenate(dqt_heads, axis=1)
            n_all = jnp.concatenate(n_heads, axis=1)
            gain_all = jnp.concatenate([gain_v] * HG_HEADS, axis=1)
            dz_ref[rows, 0:HG_W] = (dqt * sgq_all * (1.0 + q_all * (1.0 - sgq_all))).astype(BF16)
            dz_ref[rows, HG_W:2 * HG_W] = (dfg * (1.0 - lb_all) * sg_all * (1.0 - sg_all)).astype(BF16)
            dz_ref[rows, 2 * HG_W:3 * HG_W] = jnp.concatenate(dv_heads, axis=1).astype(BF16)
            dz_ref[rows, 3 * HG_W:4 * HG_W] = (dy_ref[rows, :] * n_all * gain_all * sgg_all
                                               * (1.0 + g_all * (1.0 - sgg_all))).astype(BF16)
            return carry

        lax.fori_loop(0, cpb, chunk, 0, unroll=2)

        @pl.when(step == nb - 1)
        def _():
            d0 = dlb_ref[...] * lb_all * (1.0 - lb_all)
            dlg_ref[0:1, :] = d0
            dlg_ref[1:2, :] = -d0

    rev = lambda i: (nb - 1 - i, 0)
    fix = lambda i: (0, 0)
    return _pallas(
        body, name="hgrn_bwd", grid=(nb,),
        in_specs=[pl.BlockSpec((RB, HG_COLS), rev), pl.BlockSpec((RB, HG_W), rev), pl.BlockSpec((RB, HG_W), rev),
                  pl.BlockSpec((cpb * HG_HEADS * DK, DK), rev), pl.BlockSpec((2, HG_W), fix),
                  pl.BlockSpec((1, DK), fix), pl.BlockSpec(memory_space=pl.ANY)],
        out_specs=[pl.BlockSpec((RB, HG_COLS), rev), pl.BlockSpec((2, HG_W), fix), pl.BlockSpec((1, DK), fix)],
        out_shape=[jax.ShapeDtypeStruct(dz_buf.shape, BF16), jax.ShapeDtypeStruct((2, HG_W), F32),
                   jax.ShapeDtypeStruct((1, DK), F32)],
        scratch_shapes=[pltpu.VMEM((HG_HEADS, DK, DK), F32), pltpu.VMEM((1, HG_W), F32)],
        input_output_aliases={6: 0},
        compiler_params=_params(dimension_semantics=("arbitrary",)),
    )(z, o_raw, dy, states, lb_logits, gain, dz_buf)


def _head_ones():
    r, c = _iota((FOX_W, FOX_W), 0), _iota((FOX_W, FOX_W), 1)
    return ((r // FOX_DH) == (c // FOX_DH)).astype(BF16)


def _log_sigmoid(x):
    return jnp.minimum(x, 0.0) - jnp.log(1.0 + jnp.exp(-jnp.abs(x)))


def _fox_prep(z, bias, qg, kg):
    T = z.shape[0]
    tm = min(ROW_BLOCK, T)
    nb = T // tm

    def body(z_ref, b_ref, qg_ref, kg_ref, q_ref, k_ref, v_ref, qa_ref, ka_ref, carry_ref):
        @pl.when(pl.program_id(0) == 0)
        def _():
            carry_ref[...] = jnp.zeros_like(carry_ref)

        ones = _head_ones()
        normed = []
        for src, g_ref in ((0, qg_ref), (1, kg_ref)):
            xv = z_ref[:, src * FOX_W:(src + 1) * FOX_W]
            ms = _split_dot(ones, xv * xv, 1, 0, mat_first=False) * (1.0 / FOX_DH)
            normed.append(xv * lax.rsqrt(ms + EPS) * g_ref[...])
        qn, kn = normed
        q_ref[...] = (qn * FOX_DH ** -0.5).astype(BF16)
        k_b = kn.astype(BF16)
        k_ref[...] = k_b
        v_ref[...] = z_ref[:, 2 * FOX_W:3 * FOX_W].astype(BF16)
        logf = _log_sigmoid(z_ref[:, 3 * FOX_W:FOX_COLS] + b_ref[...])
        r, c = _iota((tm, tm), 0), _iota((tm, tm), 1)
        tri_b = (c <= r).astype(BF16)
        cum = _split_dot(tri_b, logf, 1, 0, terms=3) + carry_ref[...]
        carry_ref[...] = cum[tm - 1:tm]
        c2 = cum * LOG2E
        hi = c2.astype(BF16)
        rem = c2 - hi.astype(F32)
        mid = rem.astype(BF16)
        lo = (rem - mid.astype(F32)).astype(BF16)
        hrow, col = _iota((LANES, 2 * FOX_W), 0), _iota((LANES, 2 * FOX_W), 1)
        base = hrow * LANES + jnp.where(hrow % 2 == 0, FOX_DH, 0)
        placed = None
        for t, part in enumerate((hi, mid, lo)):
            place = jnp.logical_and(col == base + t, hrow < FOX_HEADS).astype(BF16)
            term = _dot(part, place, 1, 0)
            placed = term if placed is None else placed + term
        colw = _iota((tm, 2 * FOX_W), 1)
        head, lane = colw // LANES, colw % LANES
        own = (lane < FOX_DH) == (head % 2 == 0)
        other = jnp.where(head % 2 == 0, lane - FOX_DH, lane)
        ones_q = jnp.where(jnp.logical_and(other >= 0, other < 3), -1.0, 0.0)
        q2 = (qn * (FOX_DH ** -0.5 * LOG2E)).astype(BF16)
        q_exp = jnp.concatenate([q2[:, (h // 2) * LANES:(h // 2 + 1) * LANES] for h in range(FOX_HEADS)], axis=1)
        k_exp = jnp.concatenate([k_b[:, (h // 2) * LANES:(h // 2 + 1) * LANES] for h in range(FOX_HEADS)], axis=1)
        qa_ref[...] = jnp.where(own, q_exp, ones_q.astype(BF16))
        ka_ref[...] = jnp.where(own, k_exp, placed.astype(BF16))

    wide = pl.BlockSpec((tm, 2 * FOX_W), lambda i: (i, 0))
    return _pallas(
        body, name="fox_prep", grid=(nb,),
        in_specs=[pl.BlockSpec((tm, SEG), lambda i: (i, 1)), pl.BlockSpec((1, LANES), lambda i: (0, 0)),
                  pl.BlockSpec((1, FOX_W), lambda i: (0, 0)), pl.BlockSpec((1, FOX_W), lambda i: (0, 0))],
        out_specs=[pl.BlockSpec((tm, FOX_W), lambda i: (i, 0))] * 3 + [wide] * 2,
        out_shape=[jax.ShapeDtypeStruct((T, FOX_W), BF16)] * 3 + [jax.ShapeDtypeStruct((T, 2 * FOX_W), BF16)] * 2,
        scratch_shapes=[pltpu.VMEM((1, LANES), F32)],
        compiler_params=_params(dimension_semantics=("arbitrary",)),
    )(z, bias, qg, kg)


def _fox_fwd(qa, ka, vb, blocks):
    T = qa.shape[0]
    tq = min(ROW_BLOCK, T)
    nq = T // tq
    NEG = -1e30
    n = len(blocks)
    n_pairs = FOX_HEADS // 2

    n_in = 3

    def body(*refs):
        q_ref, k_ref, v_ref = refs[:n_in]
        o_ref, lse_ref = refs[n_in + n:n_in + n + 2]
        m_sc, l_sc, acc_sc = refs[n_in + 2 * n + 2:n_in + 2 * n + 5]
        pr, qi = pl.program_id(0), pl.program_id(1)
        g_start, g_forward, g_finish = _gather_steps(
            refs[n_in:n_in + n], refs[n_in + n + 2:n_in + 2 * n + 2], *refs[n_in + 2 * n + 5:])

        @pl.when(jnp.logical_and(pr == 0, qi == 0))
        def _():
            g_start()

        @pl.when(jnp.logical_and(pr == n_pairs // 2, qi == 0))
        def _():
            g_forward()

        m_sc[...] = jnp.full_like(m_sc, NEG)
        l_sc[...] = jnp.zeros_like(l_sc)
        acc_sc[...] = jnp.zeros_like(acc_sc)
        lane = _iota((tq, LANES), 1)

        def block(masked, ki):
            keys = pl.ds(pl.multiple_of(ki * tq, tq), tq)
            vv = v_ref[keys, :]
            for hh in range(2):
                hs = slice(hh * LANES, (hh + 1) * LANES)
                s = _dot(q_ref[:, hs], k_ref[keys, hs], 1, 1)
                tiles = [s[:, j * LANES:(j + 1) * LANES] for j in range(tq // LANES)]
                if masked:
                    row, col = _iota((tq, LANES), 0), _iota((tq, LANES), 1)
                    tiles = [jnp.where(row >= col + j * LANES, t, NEG) for j, t in enumerate(tiles)]
                m_old = m_sc[hh]
                top = jnp.broadcast_to(jnp.max(functools.reduce(jnp.maximum, tiles), axis=-1, keepdims=True),
                                       (tq, LANES))
                m_new = jnp.maximum(m_old, top)
                alpha = jnp.exp2(m_old - m_new)
                ps = [jnp.exp2(t - m_new) for t in tiles]
                l_sc[hh] = alpha * l_sc[hh] + functools.reduce(jnp.add, ps)
                m_sc[hh] = m_new
                p_b = jnp.concatenate([p.astype(BF16) for p in ps], axis=1)
                acc_sc[hh] = alpha * acc_sc[hh] + _dot(p_b, vv, 1, 0)

        def before(ki, carry):
            block(False, ki)
            return carry

        lax.fori_loop(0, qi, before, 0)
        block(True, qi)
        l0 = jnp.sum(l_sc[0], axis=-1, keepdims=True)
        l1 = jnp.sum(l_sc[1], axis=-1, keepdims=True)
        o_ref[...] = jnp.where(lane < FOX_DH, acc_sc[0] * (1.0 / l0), acc_sc[1] * (1.0 / l1))
        lse_ref[:, :LANES] = m_sc[0] + jnp.log2(l0)
        lse_ref[:, LANES:] = m_sc[1] + jnp.log2(l1)

        @pl.when(jnp.logical_and(pr == n_pairs - 1, qi == nq - 1))
        def _():
            g_finish()

    qmap = lambda p, i: (i, p)
    whole = lambda p, i: (0, p)
    hbm = pl.BlockSpec(memory_space=pl.ANY)
    res = _pallas(
        body, name="fox_fwd", grid=(n_pairs, nq),
        in_specs=[pl.BlockSpec((tq, 2 * LANES), qmap), pl.BlockSpec((T, 2 * LANES), whole),
                  pl.BlockSpec((T, LANES), whole)] + [hbm] * n,
        out_specs=[pl.BlockSpec((tq, LANES), qmap), pl.BlockSpec((tq, 2 * LANES), qmap)] + [hbm] * n,
        out_shape=[jax.ShapeDtypeStruct((T, FOX_W), F32), jax.ShapeDtypeStruct((T, 2 * FOX_W), F32)]
        + [jax.ShapeDtypeStruct((N_DEV,) + b.shape, b.dtype) for b in blocks],
        scratch_shapes=[pltpu.VMEM((2, tq, LANES), F32)] * 3 + _gather_scratch(n),
        compiler_params=_params(dimension_semantics=("arbitrary",) * 2),
    )(qa, ka, vb, *blocks)
    return res[0], res[1], res[2:]


def _fox_bwd(qs, kn, vb, qa, ka, o, do, lse, hs):
    T = qs.shape[0]
    tq = min(ROW_BLOCK, T)
    nq = T // tq
    n = len(hs)
    n_pairs = FOX_HEADS // 2

    def body(*refs):
        q_ref, k_ref, v_ref, qa_ref, ka_ref, o_ref, do_ref, lse_ref = refs[:8]
        dq_ref, dk_ref, dv_ref, dcs_ref, drs_ref = refs[8 + n:13 + n]
        pr, ki = pl.program_id(0), pl.program_id(1)
        x_start, x_finish = _chip_exchange_steps(refs[8:8 + n], refs[13 + n:13 + 2 * n], *refs[13 + 2 * n:])

        @pl.when(jnp.logical_and(pr == 0, ki == 0))
        def _():
            x_start()
            drs_ref[...] = jnp.zeros_like(drs_ref)

        @pl.when(ki == 0)
        def _():
            dq_ref[...] = jnp.zeros_like(dq_ref)

        dk_ref[...] = jnp.zeros_like(dk_ref)
        dv_ref[...] = jnp.zeros_like(dv_ref)
        dcs_ref[...] = jnp.zeros_like(dcs_ref)

        def block(masked, qi):
            lane = _iota((tq, LANES), 1)
            qrows = pl.ds(pl.multiple_of(qi * tq, tq), tq)
            qv, kv, vv = q_ref[qrows, :], k_ref[...], v_ref[...]
            ov, dov = o_ref[qrows, :], do_ref[qrows, :]
            dq_acc = jnp.zeros((tq, LANES), F32)
            dk_acc = jnp.zeros((tq, LANES), F32)
            dv_acc = jnp.zeros((tq, LANES), F32)
            dcs_acc = jnp.zeros((8, tq), F32)
            drs_acc = jnp.zeros((tq, LANES), F32)
            prod = dov * ov
            nt = tq // LANES
            for hh in range(2):
                in_head = (lane < FOX_DH) if hh == 0 else (lane >= FOX_DH)
                hs_ = slice(hh * LANES, (hh + 1) * LANES)
                zb = jnp.zeros_like(qv)
                qm = jnp.where(in_head, qv, zb)
                km = jnp.where(in_head, kv, zb)
                dom = jnp.where(in_head, dov, 0.0).astype(BF16)
                delta_b = jnp.broadcast_to(jnp.sum(jnp.where(in_head, prod, 0.0), axis=1, keepdims=True), (tq, LANES))
                lse_b = lse_ref[qrows, hs_]
                s = _dot(qa_ref[qrows, hs_], ka_ref[:, hs_], 1, 1)
                dp = _dot(dom, vv, 1, 1)
                p_tiles, ds_tiles, col_tiles = [], [], []
                row_part = jnp.zeros((tq, LANES), F32)
                for j in range(nt):
                    js = slice(j * LANES, (j + 1) * LANES)
                    p = jnp.exp2(s[:, js] - lse_b)
                    if masked:
                        p = jnp.where(_iota((tq, LANES), 0) >= _iota((tq, LANES), 1) + j * LANES, p, 0.0)
                    ds = p * (dp[:, js] - delta_b)
                    p_tiles.append(p.astype(BF16))
                    ds_tiles.append(ds.astype(BF16))
                    col_tiles.append(jnp.sum(ds, axis=0, keepdims=True))
                    row_part = row_part + ds
                p_b = jnp.concatenate(p_tiles, axis=1)
                ds_b = jnp.concatenate(ds_tiles, axis=1)
                dv_acc = dv_acc + _dot(p_b, dom, 0, 0)
                dq_acc = dq_acc + _dot(ds_b, km, 1, 0)
                dk_acc = dk_acc + _dot(ds_b, qm, 0, 0)
                dcs_acc = dcs_acc + jnp.where(_iota((8, tq), 0) == hh, jnp.concatenate(col_tiles, axis=1), 0.0)
                rowsum = jnp.sum(row_part, axis=1, keepdims=True)
                drs_acc = drs_acc + jnp.where(lane == 2 * pr + hh, rowsum, 0.0)
            drs_ref[qrows, :] += drs_acc
            dq_ref[qrows, :] += dq_acc
            dk_ref[...] += dk_acc
            dv_ref[...] += dv_acc
            dcs_ref[0] += dcs_acc

        block(True, ki)

        def after(qi, carry):
            block(False, qi)
            return carry

        lax.fori_loop(ki + 1, nq, after, 0)

        @pl.when(jnp.logical_and(pr == n_pairs - 1, ki == nq - 1))
        def _():
            x_finish()

    whole = lambda p, j: (0, p)
    kmap = lambda p, j: (j, p)
    hbm = pl.BlockSpec(memory_space=pl.ANY)
    res = _pallas(
        body, name="fox_bwd", grid=(n_pairs, nq),
        in_specs=[pl.BlockSpec((T, LANES), whole), pl.BlockSpec((tq, LANES), kmap), pl.BlockSpec((tq, LANES), kmap),
                  pl.BlockSpec((T, 2 * LANES), whole), pl.BlockSpec((tq, 2 * LANES), kmap),
                  pl.BlockSpec((T, LANES), whole), pl.BlockSpec((T, LANES), whole), pl.BlockSpec((T, 2 * LANES), whole)]
        + [hbm] * n,
        out_specs=[pl.BlockSpec((T, LANES), whole), pl.BlockSpec((tq, LANES), kmap),
                   pl.BlockSpec((tq, LANES), kmap), pl.BlockSpec((1, 8, tq), lambda p, j: (p, 0, j)),
                   pl.BlockSpec((T, LANES), lambda p, j: (0, 0))] + [hbm] * n,
        out_shape=[jax.ShapeDtypeStruct((T, FOX_W), F32)] * 3
        + [jax.ShapeDtypeStruct((n_pairs, 8, T), F32), jax.ShapeDtypeStruct((T, LANES), F32)]
        + _chip_exchange_shapes(hs),
        scratch_shapes=_chip_exchange_scratch(n),
        compiler_params=_params(dimension_semantics=("arbitrary",) * 2),
    )(qs, kn, vb, qa, ka, o, do, lse, *hs)
    res = list(res)
    return res[:5] + [res[5:]]


def _fox_post(z, dq, dk, dv, dcs, drs, bias, qg, kg, dz_buf):
    T = z.shape[0]
    tm = min(ROW_BLOCK, T)
    nb = T // tm

    def body(z_ref, dq_ref, dk_ref, dv_ref, dcs_ref, drs_ref, b_ref, qg_ref, kg_ref, _buf_ref, dz_ref, dqg_ref, dkg_ref,
             db_ref, carry_ref):
        @pl.when(pl.program_id(0) == 0)
        def _():
            carry_ref[...] = jnp.zeros_like(carry_ref)
            dqg_ref[...] = jnp.zeros_like(dqg_ref)
            dkg_ref[...] = jnp.zeros_like(dkg_ref)
            db_ref[...] = jnp.zeros_like(db_ref)

        ones = _head_ones()
        for src, g_ref, d_ref, dg_ref, scale in ((0, qg_ref, dq_ref, dqg_ref, FOX_DH ** -0.5), (1, kg_ref, dk_ref, dkg_ref, 1.0)):
            xv = z_ref[:, src * FOX_W:(src + 1) * FOX_W]
            ms = _split_dot(ones, xv * xv, 1, 0, mat_first=False) * (1.0 / FOX_DH)
            rstd = lax.rsqrt(ms + EPS)
            xh = xv * rstd
            dn = d_ref[...] * scale
            dg_ref[...] += jnp.sum(dn * xh, axis=0, keepdims=True)
            dxh = dn * g_ref[...]
            mean = _split_dot(ones, dxh * xh, 1, 0, mat_first=False) * (1.0 / FOX_DH)
            dz_ref[:, src * FOX_W:(src + 1) * FOX_W] = (rstd * (dxh - xh * mean)).astype(BF16)
        dz_ref[:, 2 * FOX_W:3 * FOX_W] = dv_ref[...].astype(BF16)
        row8 = _iota((8, tm), 0)
        dct = jnp.zeros((8, tm), F32)
        for h in range(FOX_HEADS):
            src_row = dcs_ref[h // 2][h % 2:h % 2 + 1, :]
            dct = dct + jnp.where(row8 == h, src_row, 0.0)
        dct = drs_ref[...].T[0:8] - dct
        r, c = _iota((tm, tm), 0), _iota((tm, tm), 1)
        upper_b = (r >= c).astype(BF16)
        rc = _split_dot(upper_b, dct, 1, 0, mat_first=False) + carry_ref[...]
        carry_ref[...] = rc[:, 0:1]
        full = jnp.concatenate([rc, jnp.zeros((LANES - 8, tm), F32)], axis=0)
        dlogf = full.T
        xf = z_ref[:, 3 * FOX_W:FOX_COLS] + b_ref[...]
        df = dlogf * (1.0 - _sigmoid(xf))
        dz_ref[:, 3 * FOX_W:FOX_COLS] = df.astype(BF16)
        dz_ref[:, FOX_COLS:] = jnp.zeros((tm, SEG - FOX_COLS), BF16)
        db_ref[...] += jnp.sum(df, axis=0, keepdims=True)

    rev = lambda i: (nb - 1 - i, 0)
    fix2 = lambda i: (0, 0)
    return _pallas(
        body, name="fox_post", grid=(nb,),
        in_specs=[pl.BlockSpec((tm, SEG), lambda i: (nb - 1 - i, 1)), pl.BlockSpec((tm, FOX_W), rev),
                  pl.BlockSpec((tm, FOX_W), rev),
                  pl.BlockSpec((tm, FOX_W), rev), pl.BlockSpec((FOX_HEADS // 2, 8, tm), lambda i: (0, 0, nb - 1 - i)),
                  pl.BlockSpec((tm, LANES), rev),
                  pl.BlockSpec((1, LANES), fix2), pl.BlockSpec((1, FOX_W), fix2), pl.BlockSpec((1, FOX_W), fix2),
                  pl.BlockSpec(memory_space=pl.ANY)],
        out_specs=[pl.BlockSpec((tm, SEG), lambda i: (nb - 1 - i, 1)), pl.BlockSpec((1, FOX_W), fix2),
                   pl.BlockSpec((1, FOX_W), fix2), pl.BlockSpec((1, LANES), fix2)],
        out_shape=[jax.ShapeDtypeStruct(dz_buf.shape, BF16), jax.ShapeDtypeStruct((1, FOX_W), F32),
                   jax.ShapeDtypeStruct((1, FOX_W), F32), jax.ShapeDtypeStruct((1, LANES), F32)],
        scratch_shapes=[pltpu.VMEM((8, 1), F32)],
        input_output_aliases={9: 0},
        compiler_params=_params(dimension_semantics=("arbitrary",)),
    )(z, dq, dk, dv, dcs, drs, bias, qg, kg, dz_buf)


def _local_step(x, p, tgt, sm, W, rest_chunks, core):
    lbl, og, fb = sm["hg_lb_logits"], sm["hg_onorm_g"], sm["fox_f_bias"]
    fbias = jnp.pad(fb, ((0, 0), (0, LANES - FOX_HEADS)))
    qg = jnp.tile(sm["fox_q_norm_g"], (1, FOX_HEADS))
    kg = jnp.tile(sm["fox_k_norm_g"], (1, FOX_HEADS))

    h = _rms_fwd(x, sm["norm_mix_g"], name="rms_mix")
    rest = dict(zip(BIG[1:], rest_chunks))
    first, second = ["w_ffn_gate"], ["w_ffn_up"]
    third = [n for n in BIG[1:] if n not in first + second]
    z, got1 = _matmul(h, W["w_in"], tb=True, gather=[rest[n] for n in first], name="mm_z")
    o_raw, ya, states, got2 = _hgrn_fwd(z, lbl, og, [rest[n] for n in second])
    qs, kn, vb, qa, ka = _fox_prep(z, fbias, qg, kg)
    yb, lse, got3 = _fox_fwd(qa, ka, vb, [rest[n] for n in third])
    W = dict(W, **{n: _full_of_chunks(n, g)
                   for n, g in zip(first + second + third, list(got1) + list(got2) + list(got3))})
    merged, ua, ub = _merge_fwd(ya, yb, W["w_branch_a"], W["w_branch_b"], z)
    x1, hf = _matmul(merged, W["w_out"], add=x, norm_fwd=sm["norm_ffn_g"], name="mm_x1")
    a, b, act = _swiglu_fwd(hf, W["w_ffn_gate"], W["w_ffn_up"])
    x2 = _matmul(act, W["w_ffn_down"], add=x1, name="mm_x2")
    hp, dy, dsp, dpp, loss = _ple_loss(x2, p, sm["norm_ple_g"], W["w_ple_gate"], W["w_ple_proj"], tgt)

    G = {}
    G["w_ple_proj"] = _matmul(p, dpp, ta=True, out_dtype=BF16, name="mm_dw_ple_proj")
    G["w_ple_gate"] = _matmul(hp, dsp, ta=True, out_dtype=BF16, name="mm_dw_ple_gate")
    dx2, d_ple_g = _matmul(dsp, W["w_ple_gate"], tb=True, norm_bwd=(x2, sm["norm_ple_g"], dy), name="mm_dx2")
    G["w_ffn_down"] = _matmul(act, dx2, ta=True, out_dtype=BF16, name="mm_dw_ffn_down")
    da, db = _swiglu_bwd(dx2, W["w_ffn_down"], a, b)
    G["w_ffn_gate"] = _matmul(da, hf, ta=True, out_dtype=BF16, name="mm_dw_ffn_gate")
    G["w_ffn_up"] = _matmul(db, hf, ta=True, out_dtype=BF16, name="mm_dw_ffn_up")
    dhf = _matmul(da, W["w_ffn_gate"], name="mm_dhf_a")
    dx1, d_ffn_g = _matmul(db, W["w_ffn_up"], add=dhf, norm_bwd=(x1, sm["norm_ffn_g"], dx2), name="mm_dx1")
    G["w_out"] = _matmul(merged, dx1, ta=True, out_dtype=BF16, name="mm_dw_out")
    dua, dub, dz, dya, dyb = _merge_bwd(dx1, W["w_out"], W["w_branch_a"], W["w_branch_b"], ua, ub, z)
    G["w_branch_a"] = _matmul(ya, dua, ta=True, out_dtype=BF16, name="mm_dw_branch_a")
    G["w_branch_b"] = _matmul(yb, dub, ta=True, out_dtype=BF16, name="mm_dw_branch_b")
    hb_rest = _sibling_sums({n: G[n] for n in BIG[1:]}, core, tag="rest")
    dq, dk, dv, dcs, drs, got_rest = _fox_bwd(qs, kn, vb, qa, ka, yb, dyb, lse, hb_rest)
    dz, d_qg, d_kg, d_fb = _fox_post(z, dq, dk, dv, dcs, drs, fbias, qg, kg, dz)
    dz, d_lbl, d_og = _hgrn_bwd(z, o_raw, dya, states, lbl, og, dz)
    G["w_in"] = _matmul(dz, h, ta=True, out_dtype=BF16, name="mm_dw_in")
    hb_in = _sibling_sums({"w_in": G["w_in"]}, core, tag="w_in")
    (grad_x, d_mix_g), got_in = _matmul(dz, W["w_in"], exchange=hb_in,
                                        norm_bwd=(x, sm["norm_mix_g"], dx1), name="mm_dx")

    gs = {"norm_mix_g": d_mix_g, "hg_lb_logits": d_lbl, "hg_onorm_g": d_og, "fox_f_bias": d_fb[:, :FOX_HEADS],
          "fox_q_norm_g": d_qg.reshape(FOX_HEADS, FOX_DH).sum(0, keepdims=True),
          "fox_k_norm_g": d_kg.reshape(FOX_HEADS, FOX_DH).sum(0, keepdims=True),
          "norm_ffn_g": d_ffn_g, "norm_ple_g": d_ple_g}
    return loss, grad_x, gs, hb_in + hb_rest, list(got_in) + list(got_rest)


def _pack_rows(parts, total):
    buf = jnp.concatenate(parts, axis=-2)
    pad = total - buf.shape[-2]
    widths = [(0, 0)] * (buf.ndim - 2) + [(0, pad), (0, 0)]
    return jnp.pad(buf, widths)


def _chunk_of_shard(n, w):
    if n == "w_in":
        return jnp.pad(w, ((0, IN_SHARD_PAD - IN_SHARD), (0, 0)))
    if n in ("w_ffn_gate", "w_ffn_up", "w_ffn_down"):
        return jnp.pad(w, ((0, FF_SHARD_PAD - FF_SHARD), (0, 0)))
    return w


def _full_of_chunks(n, g):
    _, a, b = g.shape
    if n == "w_in":
        w = g[:, :IN_SHARD].reshape(IN_COLS, b)
        gap = jnp.zeros((SEG - FOX_LOGICAL, b), g.dtype)
        return jnp.concatenate([w[:HG_COLS + FOX_LOGICAL], gap, w[HG_COLS + FOX_LOGICAL:]], axis=0)
    if BIG_SHAPE[n][2] == 0:
        return g.reshape(N_DEV * a, b)
    return g.transpose(1, 0, 2).reshape(a, N_DEV * b)


def _chunks_of_full(n, g):
    if n == "w_in":
        w = jnp.concatenate([g[:HG_COLS + FOX_LOGICAL], g[2 * SEG:]], axis=0).reshape(N_DEV, IN_SHARD, g.shape[1])
        return jnp.pad(w, ((0, 0), (0, IN_SHARD_PAD - IN_SHARD), (0, 0)))
    if BIG_SHAPE[n][2] == 0:
        return g.reshape(N_DEV, g.shape[0] // N_DEV, g.shape[1])
    return g.reshape(g.shape[0], N_DEV, g.shape[1] // N_DEV).transpose(1, 0, 2)


def _pack_small(vals, loss_row=None):
    parts = [vals[n].reshape(SMALL_ROWS[n], -1) for n in SMALL]
    parts = [jnp.pad(v, ((0, 0), (0, LANES - v.shape[1]))) for v in parts]
    if loss_row is not None:
        parts.append(loss_row)
    return _pack_rows(parts, SMALL_TOTAL)


def _unpack_small(buf, like):
    out, r0 = {}, 0
    for n in SMALL:
        rows, size = SMALL_ROWS[n], like[n].size
        blk = buf[r0:r0 + rows]
        out[n] = (blk if size == rows * LANES else blk[:, :size]).reshape(like[n].shape)
        r0 += rows
    return out


def _place():
    return lax.axis_index("x"), lax.axis_index("y"), lax.axis_index("c")


def _gather_steps(x_refs, out_refs, send_sems, recv_sems, local_sems):
    n = len(x_refs)
    x, y, c = _place()
    me, sibling = (x, y, c), (x, y, 1 - c)
    chips = [(1 - x, y), (x, 1 - y), (1 - x, 1 - y)]

    def slot(i, px, py, pc):
        return out_refs[i].at[4 * px + 2 * py + pc]

    def copy(k, i, blk, to, own=False):
        return pltpu.make_async_remote_copy(
            src_ref=x_refs[i] if own else slot(i, *blk), dst_ref=slot(i, *blk),
            send_sem=send_sems.at[k, i], recv_sem=recv_sems.at[k, i], device_id=to, device_id_type=MESH)

    def mine():
        return [pltpu.make_async_copy(x_refs[i], slot(i, *me), local_sems.at[i]) for i in range(n)]

    def first():
        cps = [copy(0, i, me, sibling, own=True) for i in range(n)]
        return cps + [copy(1 + j, i, me, (*chip, c), own=True) for j, chip in enumerate(chips) for i in range(n)]

    def passed():
        return [copy(4 + j, i, (*chip, c), sibling) for j, chip in enumerate(chips) for i in range(n)]

    def start():
        for cp in mine() + first():
            cp.start()

    def forward():
        fws = passed()
        for j, chip in enumerate(chips):
            for i in range(n):
                copy(1 + j, i, (*chip, c), me).wait_recv()
                fws[j * n + i].start()

    def finish():
        for i in range(n):
            copy(0, i, sibling, me).wait_recv()
        for j, chip in enumerate(chips):
            for i in range(n):
                copy(4 + j, i, (*chip, 1 - c), me).wait_recv()
        for cp in first() + passed():
            cp.wait_send()
        for cp in mine():
            cp.wait()

    return start, forward, finish


def _gather_scratch(n):
    return [pltpu.SemaphoreType.DMA((7, n)), pltpu.SemaphoreType.DMA((7, n)), pltpu.SemaphoreType.DMA((n,))]


def _all_gather(blocks, *, name):
    n = len(blocks)

    def body(*refs):
        for step in _gather_steps(refs[:n], refs[n:2 * n], *refs[2 * n:]):
            step()

    hbm = pl.BlockSpec(memory_space=pl.ANY)
    return _pallas(
        body, name=name, out_shape=[jax.ShapeDtypeStruct((N_DEV,) + b.shape, b.dtype) for b in blocks],
        in_specs=[hbm] * n, out_specs=[hbm] * n, scratch_shapes=_gather_scratch(n),
    )(*blocks)


def _sibling_exchange(gs, *, name):
    n = len(gs)

    def body(*refs):
        g_refs, out_refs = refs[:n], refs[n:2 * n]
        send_sems, recv_sems = refs[2 * n:]
        x, y, c = _place()
        cps = [pltpu.make_async_remote_copy(
            src_ref=g_refs[i].at[:, pl.ds(1 - c, 1)], dst_ref=out_refs[i], send_sem=send_sems.at[i],
            recv_sem=recv_sems.at[i], device_id=(x, y, 1 - c), device_id_type=MESH) for i in range(n)]
        for cp in cps:
            cp.start()
        for cp in cps:
            cp.wait()

    hbm = pl.BlockSpec(memory_space=pl.ANY)
    return _pallas(
        body, name=name, out_shape=[jax.ShapeDtypeStruct((N_CHIP, 1) + g.shape[2:], g.dtype) for g in gs],
        in_specs=[hbm] * n, out_specs=[hbm] * n,
        scratch_shapes=[pltpu.SemaphoreType.DMA((n,)), pltpu.SemaphoreType.DMA((n,))],
    )(*gs)


def _chip_sum(g4s, gots, core, *, name):
    n = len(g4s)

    def body(c_ref, *refs):
        for g_ref, r_ref, h_ref in zip(refs[:n], refs[n:2 * n], refs[2 * n:]):
            h_ref[0] = (g_ref[0, 0].astype(F32) + r_ref[0, 0].astype(F32)).astype(BF16)

    shapes = [g.shape[2:] for g in g4s]
    grid_spec = pltpu.PrefetchScalarGridSpec(
        num_scalar_prefetch=1, grid=(N_CHIP,),
        in_specs=[pl.BlockSpec((1, 1) + s, lambda j, c: (j, c[0], 0, 0)) for s in shapes]
        + [pl.BlockSpec((1, 1) + s, lambda j, c: (j, 0, 0, 0)) for s in shapes],
        out_specs=[pl.BlockSpec((1,) + s, lambda j, c: (j, 0, 0)) for s in shapes])
    return list(_pallas(
        body, name=name, grid_spec=grid_spec, out_shape=[jax.ShapeDtypeStruct((N_CHIP,) + s, BF16) for s in shapes],
        compiler_params=_params(dimension_semantics=("arbitrary",)),
    )(core, *g4s, *gots))


def _chip_exchange_steps(h_refs, out_refs, send_sems, recv_sems):
    n = len(h_refs)
    x, y, c = _place()
    chips = [(1 - x, y), (x, 1 - y), (1 - x, 1 - y)]

    def copies():
        return [pltpu.make_async_remote_copy(
            src_ref=h_refs[i].at[2 * px + py], dst_ref=out_refs[i].at[k], send_sem=send_sems.at[k, i],
            recv_sem=recv_sems.at[k, i], device_id=(px, py, c), device_id_type=MESH)
            for k, (px, py) in enumerate(chips) for i in range(n)]

    def start():
        for cp in copies():
            cp.start()

    def finish():
        for cp in copies():
            cp.wait()

    return start, finish


def _chip_exchange_shapes(hs):
    return [jax.ShapeDtypeStruct((3,) + h.shape[1:], h.dtype) for h in hs]


def _chip_exchange_scratch(n):
    return [pltpu.SemaphoreType.DMA((3, n)), pltpu.SemaphoreType.DMA((3, n))]


def _sibling_sums(G, core, *, tag):
    g4 = []
    for n, g in G.items():
        gc = _chunks_of_full(n, g)
        g4.append(gc.reshape((N_CHIP, 2) + gc.shape[1:]))
    got = _sibling_exchange(g4, name="grads_to_sibling_" + tag)
    return _chip_sum(g4, got, core, name="chip_sum_" + tag)


def _adam_math(w, g, m, v):
    m = ADAM_B1 * m + (1.0 - ADAM_B1) * g
    v = ADAM_B2 * v + (1.0 - ADAM_B2) * (g * g)
    m_hat = m / (1.0 - ADAM_B1 ** ADAM_STEP)
    v_hat = v / (1.0 - ADAM_B2 ** ADAM_STEP)
    delta = -ADAM_LR * (m_hat / (jnp.sqrt(v_hat) + ADAM_EPS) + ADAM_WD * w)
    return delta, m, v


def _adam_shard(hb, got, chip, w, m, v, *, name):
    _, r, c = w.shape
    _, a, cb = hb.shape
    assert cb == c and c % LANES == 0, (hb.shape, w.shape)
    tc = _pick(c, 2 * LANES)

    def body(j_ref, h_ref, r_ref, w_ref, m_ref, v_ref, g_ref, d_ref, nm_ref, nv_ref):
        parts = [h_ref[0], r_ref[0], r_ref[1], r_ref[2]]
        g = None
        for part in parts:
            part = part[:r].astype(F32)
            g = part if g is None else g + part
        d, nm, nv = _adam_math(w_ref[0], g, m_ref[0], v_ref[0])
        g_ref[0] = g
        d_ref[0] = d
        nm_ref[0] = nm
        nv_ref[0] = nv

    blk = pl.BlockSpec((1, r, tc), lambda i, j: (0, 0, i))
    grid_spec = pltpu.PrefetchScalarGridSpec(
        num_scalar_prefetch=1, grid=(c // tc,),
        in_specs=[pl.BlockSpec((1, a, tc), lambda i, j: (j[0], 0, i)),
                  pl.BlockSpec((3, a, tc), lambda i, j: (0, 0, i)), blk, blk, blk],
        out_specs=[blk] * 4)
    return _pallas(
        body, name=name, grid_spec=grid_spec, out_shape=[jax.ShapeDtypeStruct((1, r, c), F32)] * 4,
        compiler_params=_params(dimension_semantics=("arbitrary",)),
    )(chip, hb, got, w, m, v)


def _small_all_reduce_adam(gs, w, m, v):
    def body(g_ref, w_ref, m_ref, v_ref, sum_ref, d_ref, nm_ref, nv_ref, gather, send_sems, recv_sems):
        x, y, c = _place()
        my = 4 * x + 2 * y + c
        gather[my] = g_ref[...]
        cps = []
        for k in range(1, N_DEV):
            to = (x ^ (k >> 2), y ^ ((k >> 1) & 1), c ^ (k & 1))
            cps.append(pltpu.make_async_remote_copy(
                src_ref=g_ref, dst_ref=gather.at[my], send_sem=send_sems.at[k - 1], recv_sem=recv_sems.at[k - 1],
                device_id=to, device_id_type=MESH))
        for cp in cps:
            cp.start()
        for cp in cps:
            cp.wait()
        total = gather[0]
        for d in range(1, N_DEV):
            total = total + gather[d]
        dlt, nm, nv = _adam_math(w_ref[...], total, m_ref[...], v_ref[...])
        sum_ref[...] = total
        d_ref[...] = dlt
        nm_ref[...] = nm
        nv_ref[...] = nv

    vm = pl.BlockSpec(memory_space=pltpu.VMEM)
    return _pallas(
        body, name="small_all_reduce_adam", out_shape=[jax.ShapeDtypeStruct((SMALL_TOTAL, LANES), F32)] * 4,
        in_specs=[vm] * 4, out_specs=[vm] * 4,
        scratch_shapes=[pltpu.VMEM((N_DEV, SMALL_TOTAL, LANES), F32), pltpu.SemaphoreType.DMA((7,)),
                        pltpu.SemaphoreType.DMA((7,))],
            )(gs, w, m, v)


def kernel(x, p, norm_mix_g, w_in, hg_lb_logits, hg_onorm_g, fox_f_bias, fox_q_norm_g, fox_k_norm_g, w_branch_a, w_branch_b, w_out, norm_ffn_g, w_ffn_gate, w_ffn_up, w_ffn_down, norm_ple_g, w_ple_gate, w_ple_proj, loss_target, m_norm_mix_g, m_w_in, m_hg_lb_logits, m_hg_onorm_g, m_fox_f_bias, m_fox_q_norm_g, m_fox_k_norm_g, m_w_branch_a, m_w_branch_b, m_w_out, m_norm_ffn_g, m_w_ffn_gate, m_w_ffn_up, m_w_ffn_down, m_norm_ple_g, m_w_ple_gate, m_w_ple_proj, v_norm_mix_g, v_w_in, v_hg_lb_logits, v_hg_onorm_g, v_fox_f_bias, v_fox_q_norm_g, v_fox_k_norm_g, v_w_branch_a, v_w_branch_b, v_w_out, v_norm_ffn_g, v_w_ffn_gate, v_w_ffn_up, v_w_ffn_down, v_norm_ple_g, v_w_ple_gate, v_w_ple_proj):
    args = dict(locals())
    wts = {n: args[n] for n in BIG + SMALL}
    mom = {n: args["m_" + n] for n in BIG + SMALL}
    var = {n: args["v_" + n] for n in BIG + SMALL}
    for group in (wts, mom, var):
        for n in TRANSPOSED:
            group[n] = jnp.swapaxes(group[n], 1, 2)
    sm = {n: wts[n] for n in SMALL}

    xi, yi, ci = _place()
    core = jnp.reshape(ci, (1,)).astype(jnp.int32)
    chip = jnp.reshape(2 * xi + yi, (1,)).astype(jnp.int32)
    chunks = [_chunk_of_shard(n, wts[n][0].astype(BF16)) for n in BIG]
    assert BIG[0] == "w_in"
    w_in_full = _full_of_chunks("w_in", _all_gather(chunks[:1], name="w_in_all_gather")[0])

    loss_blk, grad_x, gs, hb, got = _local_step(
        x[0], p[0, 0], loss_target[0], sm, {"w_in": w_in_full}, chunks[1:], core)

    g_big, d_big, nm_big, nv_big = {}, {}, {}, {}
    for n, h, r in zip(BIG, hb, got):
        res = _adam_shard(h, r, chip, wts[n], mom[n], var[n], name="adam_" + n)
        if n in TRANSPOSED:
            res = [jnp.swapaxes(t, 1, 2) for t in res]
        g_big[n], d_big[n], nm_big[n], nv_big[n] = res

    s_sum, s_d, s_nm, s_nv = _small_all_reduce_adam(
        _pack_small(gs, loss_blk[0:1]), _pack_small(sm), _pack_small({n: mom[n] for n in SMALL}),
        _pack_small({n: var[n] for n in SMALL}))
    loss = s_sum[LOSS_ROW, 0]
    g_small, d_small, nm_small, nv_small = (_unpack_small(t, sm) for t in (s_sum, s_d, s_nm, s_nv))

    order = ["norm_mix_g", "w_in", "hg_lb_logits", "hg_onorm_g", "fox_f_bias", "fox_q_norm_g", "fox_k_norm_g",
             "w_branch_a", "w_branch_b", "w_out", "norm_ffn_g", "w_ffn_gate", "w_ffn_up", "w_ffn_down", "norm_ple_g",
             "w_ple_gate", "w_ple_proj"]
    outs = [loss, grad_x[None]]
    for big, small in ((g_big, g_small), (d_big, d_small), (nm_big, nm_small), (nv_big, nv_small)):
        outs += [big[n] if n in big else small[n] for n in order]
    return tuple(outs)
```

```python
import functools

import jax
import jax.numpy as jnp
from jax import lax
from jax.experimental import pallas as pl
from jax.experimental.pallas import tpu as pltpu

F32 = jnp.float32
BF16 = jnp.bfloat16

D_MODEL = 1024
PLE_DIM = 256
HG_HEADS = 4
HG_DK = 128
HG_CHUNK = 64
HG_SUB = 16
HG_W = HG_HEADS * HG_DK
FOX_HEADS = 8
FOX_DH = 64
FOX_W = FOX_HEADS * FOX_DH
D_FF = 2816
EPS = 1e-6
N_DEV = 8
N_CHIP = 4
LANES = 128
FOX_COLS = 3 * FOX_W + LANES
HG_COLS = 4 * HG_W
GATE_COLS = 2 * D_MODEL
IN_COLS = HG_COLS + 3 * FOX_W + FOX_HEADS + GATE_COLS
FOX_LOGICAL = 3 * FOX_W + FOX_HEADS
SEG = 2048
IN_PAD = 3 * SEG
IN_SHARD = IN_COLS // N_DEV
IN_SHARD_PAD = 768
FF_SHARD = D_FF // N_DEV
FF_SHARD_PAD = 384
EXP_CLAMP = 80.0
LOG2E = 1.4426950408889634

ADAM_LR = 0.001
ADAM_B1 = 0.9
ADAM_B2 = 0.999
ADAM_EPS = 1e-08
ADAM_WD = 0.01
ADAM_STEP = 10

MESH = pl.DeviceIdType.MESH
VMEM_LIMIT = 56 * 1024 * 1024
ROW_BLOCK = 512

BIG = ["w_in", "w_branch_a", "w_branch_b", "w_out", "w_ffn_gate", "w_ffn_up", "w_ffn_down",
       "w_ple_gate", "w_ple_proj"]
TRANSPOSED = ("w_in", "w_ffn_gate", "w_ffn_up")
BIG_SHAPE = {
    "w_in": (IN_COLS, D_MODEL, 0), "w_branch_a": (HG_W, D_MODEL, 1), "w_branch_b": (FOX_W, D_MODEL, 1),
    "w_out": (D_MODEL, D_MODEL, 0), "w_ffn_gate": (D_FF, D_MODEL, 0), "w_ffn_up": (D_FF, D_MODEL, 0),
    "w_ffn_down": (D_FF, D_MODEL, 0), "w_ple_gate": (D_MODEL, D_MODEL, 0), "w_ple_proj": (PLE_DIM, D_MODEL, 1),
}

SMALL = ["norm_mix_g", "hg_lb_logits", "hg_onorm_g", "fox_f_bias", "fox_q_norm_g", "fox_k_norm_g",
         "norm_ffn_g", "norm_ple_g"]
SMALL_ROWS = {"norm_mix_g": 8, "hg_lb_logits": 8, "hg_onorm_g": 1, "fox_f_bias": 1, "fox_q_norm_g": 1,
              "fox_k_norm_g": 1, "norm_ffn_g": 8, "norm_ple_g": 8}
SMALL_TOTAL = 40
LOSS_ROW = 36


def _pallas(body, **kw):
    return pl.pallas_call(body, **kw)


def _params(**kw):
    return pltpu.CompilerParams(vmem_limit_bytes=VMEM_LIMIT, **kw)


def _pick(n, target):
    if n <= target:
        return n
    best = None
    for t in range(LANES, target + 1, LANES):
        if n % t == 0:
            best = t
    assert best is not None, (n, target)
    return best


def _dot(a, b, ca, cb):
    return lax.dot_general(a, b, (((ca,), (cb,)), ((), ())), preferred_element_type=F32)


def _split_dot(mat, x, ca, cb, terms=2, mat_first=True):
    acc = None
    rem = x
    for _ in range(terms):
        part = rem.astype(BF16)
        rem = rem - part.astype(F32)
        p = _dot(mat, part, ca, cb) if mat_first else _dot(part, mat, ca, cb)
        acc = p if acc is None else acc + p
    return acc


def _sigmoid(x):
    return 1.0 / (1.0 + jnp.exp(-x))


def _iota(shape, dim):
    return lax.broadcasted_iota(jnp.int32, shape, dim)


def _matmul(a, b, *, name, ta=False, tb=False, out_dtype=F32, add=None, exchange=None, gather=None,
            norm_fwd=None, norm_bwd=None):
    assert exchange is None or gather is None
    (K, M) = a.shape if ta else a.shape[::-1]
    (N, Kb) = b.shape if tb else b.shape[::-1]
    assert K == Kb, (a.shape, b.shape, ta, tb)
    if ta:
        tm, tn, tk = _pick(M, 2 * ROW_BLOCK), _pick(N, 2 * ROW_BLOCK), _pick(K, 4 * ROW_BLOCK)
    else:
        tm, tn, tk = _pick(M, 2 * ROW_BLOCK), _pick(N, 2048), _pick(K, 3072)
    if norm_bwd is not None:
        tm = _pick(M, ROW_BLOCK)
    nk = K // tk
    use_scratch = nk > 1 and out_dtype != F32
    if norm_fwd is not None or norm_bwd is not None:
        assert tn == N and not use_scratch and out_dtype == F32

    hs = list(exchange or gather or [])
    n_x = len(hs)
    grid = (M // tm, N // tn, nk)
    a_spec = pl.BlockSpec((tk, tm), lambda i, j, k: (k, i)) if ta else pl.BlockSpec((tm, tk), lambda i, j, k: (i, k))
    b_spec = pl.BlockSpec((tn, tk), lambda i, j, k: (j, k)) if tb else pl.BlockSpec((tk, tn), lambda i, j, k: (k, j))
    o_spec = pl.BlockSpec((tm, tn), lambda i, j, k: (i, j))
    row_vec = pl.BlockSpec((1, N), lambda i, j, k: (0, 0))
    hbm = pl.BlockSpec(memory_space=pl.ANY)
    extra_in = [(add, o_spec)] if add is not None else []
    extra_out = []
    if norm_fwd is not None:
        extra_in += [(norm_fwd, row_vec)]
        extra_out += [(jax.ShapeDtypeStruct((M, N), BF16), o_spec)]
    if norm_bwd is not None:
        extra_in += [(norm_bwd[0], o_spec), (norm_bwd[1], row_vec), (norm_bwd[2], o_spec)]
        extra_out += [(jax.ShapeDtypeStruct((1, N), F32), row_vec)]
    if gather is not None:
        ride_shapes, ride_scratch = [jax.ShapeDtypeStruct((N_DEV,) + h.shape, h.dtype) for h in hs], _gather_scratch(n_x)
    else:
        ride_shapes, ride_scratch = _chip_exchange_shapes(hs), (_chip_exchange_scratch(n_x) if n_x else [])
    n_ex_in, n_ex_out = len(extra_in), len(extra_out)

    def body(*refs):
        refs = list(refs)
        a_ref, b_ref = refs[:2]
        ex_in = refs[2:2 + n_ex_in]
        ride_in = refs[2 + n_ex_in:2 + n_ex_in + n_x]
        base = 2 + n_ex_in + n_x
        o_ref = refs[base]
        ex_out = refs[base + 1:base + 1 + n_ex_out]
        ride_out = refs[base + 1 + n_ex_out:base + 1 + n_ex_out + n_x]
        scratch = refs[base + 1 + n_ex_out + n_x:]
        at = [pl.program_id(d) for d in range(3)]
        k = at[2]
        if n_x:
            steps = _gather_steps if gather is not None else _chip_exchange_steps
            ride = steps(ride_in, ride_out, *scratch[-len(ride_scratch):])

            @pl.when(jnp.logical_and(at[0] == 0, jnp.logical_and(at[1] == 0, at[2] == 0)))
            def _():
                ride[0]()
        p = _dot(a_ref[...].astype(BF16), b_ref[...].astype(BF16), 0 if ta else 1, 1 if tb else 0)

        def finish(r):
            ins = list(ex_in)
            outs = list(ex_out)
            if add is not None:
                r = r + ins.pop(0)[...].astype(F32)
            if norm_fwd is not None:
                g_ref = ins.pop(0)
                rstd = lax.rsqrt(jnp.mean(r * r, axis=-1, keepdims=True) + EPS)
                outs.pop(0)[...] = (r * rstd * g_ref[...]).astype(BF16)
            if norm_bwd is not None:
                x_ref, g_ref, dres_ref = ins.pop(0), ins.pop(0), ins.pop(0)
                dg_ref = outs.pop(0)
                xv = x_ref[...]
                rstd = lax.rsqrt(jnp.mean(xv * xv, axis=-1, keepdims=True) + EPS)
                xh = xv * rstd
                part = jnp.sum(r * xh, axis=0, keepdims=True)

                @pl.when(at[0] == 0)
                def _():
                    dg_ref[...] = part

                @pl.when(at[0] > 0)
                def _():
                    dg_ref[...] += part

                dxh = r * g_ref[...]
                r = rstd * (dxh - xh * jnp.mean(dxh * xh, axis=-1, keepdims=True)) + dres_ref[...]
            o_ref[...] = r.astype(out_dtype)

        if nk == 1:
            finish(p)
        elif not use_scratch:
            @pl.when(k == 0)
            def _():
                o_ref[...] = p

            @pl.when(jnp.logical_and(k > 0, k < nk - 1))
            def _():
                o_ref[...] += p

            @pl.when(k == nk - 1)
            def _():
                finish(o_ref[...] + p)
        else:
            acc_ref = scratch[0]

            @pl.when(k == 0)
            def _():
                acc_ref[...] = p

            @pl.when(k > 0)
            def _():
                acc_ref[...] += p

            @pl.when(k == nk - 1)
            def _():
                finish(acc_ref[...])

        if n_x:
            @pl.when(jnp.logical_and(at[0] == grid[0] - 1, jnp.logical_and(at[1] == grid[1] - 1, at[2] == nk - 1)))
            def _():
                for step in ride[1:]:
                    step()

    res = _pallas(
        body, name=name, grid=grid,
        in_specs=[a_spec, b_spec] + [s for _, s in extra_in] + [hbm] * n_x,
        out_specs=[o_spec] + [s for _, s in extra_out] + [hbm] * n_x,
        out_shape=[jax.ShapeDtypeStruct((M, N), out_dtype)] + [s for s, _ in extra_out] + ride_shapes,
        scratch_shapes=([pltpu.VMEM((tm, tn), F32)] if use_scratch else []) + ride_scratch,
        compiler_params=_params(dimension_semantics=("arbitrary",) * 3),
    )(a, b, *[v for v, _ in extra_in], *hs)
    res = list(res)
    main = res[0] if n_ex_out == 0 else tuple(res[:1 + n_ex_out])
    return (main, res[1 + n_ex_out:]) if n_x else main


def _row_map(nb, reverse, seg):
    if reverse:
        return lambda i: (nb - 1 - i, seg)
    return lambda i: (i, seg)


def _row_call(body, *, name, T, ins, outs, acc_outs=(), tm=ROW_BLOCK, reverse=False):
    tm = min(tm, T)
    nb = T // tm
    in_specs, args = [], []
    for arr, how in ins:
        args.append(arr)
        if how is True:
            in_specs.append(pl.BlockSpec((tm, arr.shape[1]), _row_map(nb, reverse, 0)))
        elif how is False:
            in_specs.append(pl.BlockSpec(arr.shape, lambda i, _n=arr.ndim: (0,) * _n))
        else:
            in_specs.append(pl.BlockSpec((tm, SEG), _row_map(nb, reverse, how[0])))
    out_specs, out_shape = [], []
    for o in outs:
        c, dt = o[0], o[1]
        total, seg = o[2] if len(o) > 2 else (c, 0)
        out_specs.append(pl.BlockSpec((tm, c), _row_map(nb, reverse, seg)))
        out_shape.append(jax.ShapeDtypeStruct((T, total), dt))
    for shp, dt in acc_outs:
        out_specs.append(pl.BlockSpec(shp, lambda i, _n=len(shp): (0,) * _n))
        out_shape.append(jax.ShapeDtypeStruct(shp, dt))
    return _pallas(body, name=name, grid=(nb,), in_specs=in_specs, out_specs=out_specs, out_shape=out_shape,
                   compiler_params=_params(dimension_semantics=("arbitrary",)))(*args)


def _rms_fwd(x, g, *, name):
    T = x.shape[0]

    def body(x_ref, g_ref, h_ref):
        xv = x_ref[...]
        rstd = lax.rsqrt(jnp.mean(xv * xv, axis=-1, keepdims=True) + EPS)
        h_ref[...] = (xv * rstd * g_ref[...]).astype(BF16)

    return _row_call(body, name=name, T=T, ins=[(x, True), (g, False)], outs=[(D_MODEL, BF16)])[0]


def _merge_fwd(ya, yb, wa, wb, zg):
    def body(ya_ref, yb_ref, wa_ref, wb_ref, zg_ref, m_ref, ua_ref, ub_ref):
        ua = _dot(ya_ref[...].astype(BF16), wa_ref[...], 1, 0)
        ub = _dot(yb_ref[...].astype(BF16), wb_ref[...], 1, 0)
        ga = _sigmoid(zg_ref[:, :D_MODEL])
        gb = _sigmoid(zg_ref[:, D_MODEL:])
        m_ref[...] = (ga * ua + gb * ub).astype(BF16)
        ua_ref[...] = ua.astype(BF16)
        ub_ref[...] = ub.astype(BF16)

    return _row_call(body, name="merge_fwd", T=ya.shape[0],
                     ins=[(ya, True), (yb, True), (wa, False), (wb, False), (zg, (2,))],
                     outs=[(D_MODEL, BF16)] * 3)


def _merge_bwd(dx1, w_out, wa, wb, ua, ub, zg):
    def body(dx_ref, wo_ref, wa_ref, wb_ref, ua_ref, ub_ref, zg_ref, dua_ref, dub_ref, dzg_ref, dya_ref, dyb_ref):
        dmv = _dot(dx_ref[...].astype(BF16), wo_ref[...], 1, 1)
        ga = _sigmoid(zg_ref[:, :D_MODEL])
        gb = _sigmoid(zg_ref[:, D_MODEL:])
        dua = (dmv * ga).astype(BF16)
        dub = (dmv * gb).astype(BF16)
        dua_ref[...] = dua
        dub_ref[...] = dub
        dzg_ref[:, :D_MODEL] = (dmv * ua_ref[...].astype(F32) * ga * (1.0 - ga)).astype(BF16)
        dzg_ref[:, D_MODEL:] = (dmv * ub_ref[...].astype(F32) * gb * (1.0 - gb)).astype(BF16)
        dya_ref[...] = _dot(dua, wa_ref[...], 1, 1)
        dyb_ref[...] = _dot(dub, wb_ref[...], 1, 1)

    return _row_call(body, name="merge_bwd", T=dx1.shape[0],
                     ins=[(dx1, True), (w_out, False), (wa, False), (wb, False), (ua, True), (ub, True), (zg, (2,))],
                     outs=[(D_MODEL, BF16), (D_MODEL, BF16), (SEG, BF16, (IN_PAD, 2)), (HG_W, F32), (FOX_W, F32)])


def _swiglu_fwd(hf, w_gate, w_up):
    T, D = hf.shape
    F = w_gate.shape[0]
    tm, tn = _pick(T, 2 * ROW_BLOCK), _pick(F, 768)

    def body(h_ref, wg_ref, wu_ref, a_ref, b_ref, o_ref):
        hv = h_ref[...]
        a_b = _dot(hv, wg_ref[...], 1, 1).astype(BF16)
        b_b = _dot(hv, wu_ref[...], 1, 1).astype(BF16)
        a_ref[...] = a_b
        b_ref[...] = b_b
        av = a_b.astype(F32)
        o_ref[...] = (av * _sigmoid(av) * b_b.astype(F32)).astype(BF16)

    tile = pl.BlockSpec((tm, tn), lambda i, j: (i, j))
    wcol = pl.BlockSpec((tn, D), lambda i, j: (j, 0))
    return _pallas(
        body, name="swiglu_fwd", grid=(T // tm, F // tn),
        in_specs=[pl.BlockSpec((tm, D), lambda i, j: (i, 0)), wcol, wcol],
        out_specs=[tile] * 3, out_shape=[jax.ShapeDtypeStruct((T, F), BF16)] * 3,
        compiler_params=_params(dimension_semantics=("arbitrary",) * 2),
    )(hf, w_gate, w_up)


def _swiglu_bwd(dx, w_down, a, b):
    T, D = dx.shape
    F = w_down.shape[0]
    tm, tn = _pick(T, 2 * ROW_BLOCK), _pick(F, 768)

    def body(dx_ref, w_ref, a_ref, b_ref, da_ref, db_ref):
        dact = _dot(dx_ref[...].astype(BF16), w_ref[...], 1, 1)
        av = a_ref[...].astype(F32)
        bv = b_ref[...].astype(F32)
        sg = _sigmoid(av)
        da_ref[...] = (dact * bv * sg * (1.0 + av * (1.0 - sg))).astype(BF16)
        db_ref[...] = (dact * av * sg).astype(BF16)

    tile = pl.BlockSpec((tm, tn), lambda i, j: (i, j))
    return _pallas(
        body, name="swiglu_bwd", grid=(T // tm, F // tn),
        in_specs=[pl.BlockSpec((tm, D), lambda i, j: (i, 0)), pl.BlockSpec((tn, D), lambda i, j: (j, 0)), tile, tile],
        out_specs=[tile, tile], out_shape=[jax.ShapeDtypeStruct((T, F), BF16)] * 2,
        compiler_params=_params(dimension_semantics=("arbitrary",) * 2),
    )(dx, w_down, a, b)


def _ple_loss(x2, p, g, w_gate, w_proj, tgt):
    def body(x_ref, p_ref, g_ref, wg_ref, wp_ref, t_ref, hp_ref, dy_ref, dsp_ref, dpp_ref, loss_ref):
        xv = x_ref[...]
        rstd = lax.rsqrt(jnp.mean(xv * xv, axis=-1, keepdims=True) + EPS)
        hp = (xv * rstd * g_ref[...]).astype(BF16)
        hp_ref[...] = hp
        gp = _sigmoid(_dot(hp, wg_ref[...], 1, 0))
        ppv = _dot(p_ref[...].astype(BF16), wp_ref[...], 1, 0)
        err = xv + gp * ppv - t_ref[...]
        part = 0.5 * jnp.sum(jnp.mean(err * err, axis=-1, keepdims=True), axis=0, keepdims=True)
        part = jnp.broadcast_to(part, loss_ref.shape)

        @pl.when(pl.program_id(0) == 0)
        def _():
            loss_ref[...] = part

        @pl.when(pl.program_id(0) > 0)
        def _():
            loss_ref[...] += part

        dy = err * (1.0 / D_MODEL)
        dy_ref[...] = dy
        dsp_ref[...] = (dy * ppv * gp * (1.0 - gp)).astype(BF16)
        dpp_ref[...] = (dy * gp).astype(BF16)

    return _row_call(body, name="ple_loss", T=x2.shape[0],
                     ins=[(x2, True), (p, True), (g, False), (w_gate, False), (w_proj, False), (tgt, True)],
                     outs=[(D_MODEL, BF16), (D_MODEL, F32), (D_MODEL, BF16), (D_MODEL, BF16)],
                     acc_outs=[((8, LANES), F32)])


def _hg_tri():
    C = HG_CHUNK
    return _iota((C, C), 1) <= _iota((C, C), 0)


def _hg_ref_row(b, blk):
    mid = blk * HG_SUB + HG_SUB // 2 - 1
    return b[mid:mid + 1]


def _hg_chunk_fwd(q, f, lb, tri_b):
    sgq = _sigmoid(q)
    qt = q * sgq
    sg = _sigmoid(f)
    fg = lb + (1.0 - lb) * sg
    kf = (1.0 - lb) * (1.0 - sg)
    logf = jnp.log(fg)
    b = _split_dot(tri_b, logf, 1, 0)
    refs = [jnp.broadcast_to(_hg_ref_row(b, blk), (HG_SUB, b.shape[1])) for blk in range(HG_CHUNK // HG_SUB)]
    w = jnp.minimum(b - jnp.concatenate(refs, axis=0), EXP_CLAMP)
    return sgq, qt, sg, fg, kf, b, w


def _hg_scores(qs_b, kf, b, row):
    C, S = HG_CHUNK, HG_SUB
    parts, ks = [], []
    for blk in range(C // S):
        e = jnp.exp(jnp.minimum(_hg_ref_row(b, blk) - b, EXP_CLAMP))
        e = jnp.where(row < (blk + 1) * S, e, 0.0)
        k_b = (kf * e).astype(BF16)
        ks.append((e, k_b))
        parts.append(_dot(qs_b[blk * S:(blk + 1) * S], k_b, 1, 1))
    return jnp.concatenate(parts, axis=0), ks


def _hgrn_fwd(z, lb_logits, gain, blocks):
    T = z.shape[0]
    RB = min(ROW_BLOCK, T)
    nb, cpb = T // RB, RB // HG_CHUNK
    C, DK = HG_CHUNK, HG_DK
    n = len(blocks)

    def body(*refs):
        z_ref, lg_ref, g_ref = refs[:3]
        o_ref, y_ref, st_ref = refs[3 + n:6 + n]
        s_ref = refs[6 + 2 * n]
        g_start, g_forward, g_finish = _gather_steps(refs[3:3 + n], refs[6 + n:6 + 2 * n], *refs[7 + 2 * n:])

        @pl.when(pl.program_id(0) == 0)
        def _():
            s_ref[...] = jnp.zeros_like(s_ref)
            g_start()

        lg = lg_ref[...]
        lb_all = 1.0 / (1.0 + jnp.exp(lg[1:2] - lg[0:1]))
        gain_v = g_ref[...]
        tri = _hg_tri()
        tri_b = tri.astype(BF16)
        row = _iota((C, DK), 0)

        def chunk(ci, carry):
            r0 = pl.multiple_of(ci * C, C)
            rows = pl.ds(r0, C)
            _, qt, _, _, kf_all, b_all, w = _hg_chunk_fwd(z_ref[rows, 0:HG_W], z_ref[rows, HG_W:2 * HG_W], lb_all,
                                                          tri_b)
            qs_all = (qt * jnp.exp(w)).astype(BF16)
            qd_all = (qt * jnp.exp(b_all)).astype(BF16)
            bl_all = b_all[C - 1:C]
            kd_all = (kf_all * jnp.exp(bl_all - b_all)).astype(BF16)
            ebl_all = jnp.exp(bl_all)
            v_all = z_ref[rows, 2 * HG_W:3 * HG_W].astype(BF16)
            g_all = z_ref[rows, 3 * HG_W:4 * HG_W]
            gate_all = g_all * _sigmoid(g_all)
            for h in range(HG_HEADS):
                cs = slice(h * DK, (h + 1) * DK)
                st = s_ref[h]
                st_ref[pl.ds(pl.multiple_of((ci * HG_HEADS + h) * DK, DK), DK), :] = st
                v_b = v_all[:, cs]
                a, _ = _hg_scores(qs_all[:, cs], kf_all[:, cs], b_all[:, cs], row)
                a = jnp.where(tri, a, 0.0)
                o = _dot(qd_all[:, cs], st.astype(BF16), 1, 1) + _dot(a.astype(BF16), v_b, 1, 0)
                s_ref[h] = st * ebl_all[:, cs] + _dot(v_b, kd_all[:, cs], 0, 0)
                o_ref[rows, cs] = o
                rstd = lax.rsqrt(jnp.mean(o * o, axis=-1, keepdims=True) + EPS)
                y_ref[rows, cs] = (o * rstd * gain_v * gate_all[:, cs]).astype(BF16)
            return carry

        lax.fori_loop(0, cpb, chunk, 0, unroll=2)

        @pl.when(pl.program_id(0) == nb - 1)
        def _():
            g_forward()
            g_finish()

    hbm = pl.BlockSpec(memory_space=pl.ANY)
    res = _pallas(
        body, name="hgrn_fwd", grid=(nb,),
        in_specs=[pl.BlockSpec((RB, HG_COLS), lambda i: (i, 0)), pl.BlockSpec((2, HG_W), lambda i: (0, 0)),
                  pl.BlockSpec((1, DK), lambda i: (0, 0))] + [hbm] * n,
        out_specs=[pl.BlockSpec((RB, HG_W), lambda i: (i, 0)), pl.BlockSpec((RB, HG_W), lambda i: (i, 0)),
                   pl.BlockSpec((cpb * HG_HEADS * DK, DK), lambda i: (i, 0))] + [hbm] * n,
        out_shape=[jax.ShapeDtypeStruct((T, HG_W), F32), jax.ShapeDtypeStruct((T, HG_W), BF16),
                   jax.ShapeDtypeStruct((T // C * HG_HEADS * DK, DK), F32)]
        + [jax.ShapeDtypeStruct((N_DEV,) + b.shape, b.dtype) for b in blocks],
        scratch_shapes=[pltpu.VMEM((HG_HEADS, DK, DK), F32)] + _gather_scratch(n),
        compiler_params=_params(dimension_semantics=("arbitrary",)),
    )(z, lb_logits, gain, *blocks)
    return res[0], res[1], res[2], res[3:]


def _hgrn_bwd(z, o_raw, dy, states, lb_logits, gain, dz_buf):
    T = z.shape[0]
    RB = min(ROW_BLOCK, T)
    nb, cpb = T // RB, RB // HG_CHUNK
    C, DK, S = HG_CHUNK, HG_DK, HG_SUB

    def body(z_ref, o_ref, dy_ref, st_ref, lg_ref, g_ref, _buf_ref, dz_ref, dlg_ref, dg_ref, ds_ref, dlb_ref):
        step = pl.program_id(0)

        @pl.when(step == 0)
        def _():
            ds_ref[...] = jnp.zeros_like(ds_ref)
            dlb_ref[...] = jnp.zeros_like(dlb_ref)
            dg_ref[...] = jnp.zeros_like(dg_ref)

        lg = lg_ref[...]
        lb_all = 1.0 / (1.0 + jnp.exp(lg[1:2] - lg[0:1]))
        gain_v = g_ref[...]
        tri = _hg_tri()
        tri_b = tri.astype(BF16)
        row = _iota((C, DK), 0)

        def chunk(cj, carry):
            ci = cpb - 1 - cj
            r0 = pl.multiple_of(ci * C, C)
            rows = pl.ds(r0, C)
            q_all = z_ref[rows, 0:HG_W]
            g_all = z_ref[rows, 3 * HG_W:4 * HG_W]
            sgq_all, qt_all, sg_all, fg_all, kf_all, b_all, w_all = _hg_chunk_fwd(
                q_all, z_ref[rows, HG_W:2 * HG_W], lb_all, tri_b)
            ew_all = jnp.exp(w_all)
            eb_all = jnp.exp(b_all)
            bl_all = b_all[C - 1:C]
            ebl_all = jnp.exp(bl_all)
            ekd_all = jnp.exp(bl_all - b_all)
            qs_all = (qt_all * ew_all).astype(BF16)
            qd_all = (qt_all * eb_all).astype(BF16)
            kd_all = (kf_all * ekd_all).astype(BF16)
            v_all = z_ref[rows, 2 * HG_W:3 * HG_W].astype(BF16)
            sgg_all = _sigmoid(g_all)
            t1_all = dy_ref[rows, :] * (g_all * sgg_all)
            db_heads, dqt_heads, dkf_heads, dv_heads, n_heads = [], [], [], [], []
            for h in range(HG_HEADS):
                cs = slice(h * DK, (h + 1) * DK)
                kf, b, ew, eb, ebl, ekd = kf_all[:, cs], b_all[:, cs], ew_all[:, cs], eb_all[:, cs], ebl_all[:, cs], \
                    ekd_all[:, cs]
                qs_b, qd_b, kd_b, v_b = qs_all[:, cs], qd_all[:, cs], kd_all[:, cs], v_all[:, cs]
                st = st_ref[pl.ds(pl.multiple_of((ci * HG_HEADS + h) * DK, DK), DK), :]
                dst = ds_ref[h]
                o = o_ref[rows, cs]
                rstd = lax.rsqrt(jnp.mean(o * o, axis=-1, keepdims=True) + EPS)
                n = o * rstd
                n_heads.append(n)
                t1 = t1_all[:, cs]
                dg_ref[...] += jnp.sum(t1 * n, axis=0, keepdims=True)
                dn = t1 * gain_v
                do = rstd * (dn - n * jnp.mean(dn * n, axis=-1, keepdims=True))
                do_b = do.astype(BF16)
                a, ks = _hg_scores(qs_b, kf, b, row)
                a = jnp.where(tri, a, 0.0)
                dst_b = dst.astype(BF16)
                dqd = _dot(do_b, st.astype(BF16), 1, 0)
                da = jnp.where(tri, _dot(do_b, v_b, 1, 1), 0.0)
                dv = _dot(a.astype(BF16), do_b, 0, 0) + _dot(kd_b, dst_b, 1, 1)
                dkd = _dot(v_b, dst_b, 1, 0)
                ds_ref[h] = dst * ebl + _dot(do_b, qd_b, 0, 0)
                dkd_kd = dkd * kd_b.astype(F32)
                dbl = ebl * jnp.sum(dst * st, axis=0, keepdims=True) + jnp.sum(dkd_kd, axis=0, keepdims=True)
                da_b = da.astype(BF16)
                dqs_parts = []
                dk_in = jnp.zeros((C, DK), F32)
                db_k = jnp.zeros((C, DK), F32)
                for blk in range(C // S):
                    e, k_b = ks[blk]
                    da_blk = da_b[blk * S:(blk + 1) * S]
                    dqs_parts.append(_dot(da_blk, k_b, 1, 0))
                    dks = _dot(da_blk, qs_b[blk * S:(blk + 1) * S], 0, 0)
                    dk_in = dk_in + dks * e
                    db_k = db_k + dks * k_b.astype(F32)
                dqs = jnp.concatenate(dqs_parts, axis=0)
                db = qs_b.astype(F32) * dqs - db_k + dqd * qd_b.astype(F32) - dkd_kd
                db_heads.append(db + jnp.where(row == C - 1, dbl, 0.0))
                dqt_heads.append(dqs * ew + dqd * eb)
                dkf_heads.append(dk_in + dkd * ekd)
                dv_heads.append(dv)
            dlogf = _split_dot(tri_b, jnp.concatenate(db_heads, axis=1), 0, 0)
            dfg = dlogf / fg_all - jnp.concatenate(dkf_heads, axis=1)
            dlb_ref[...] += jnp.sum(dfg * (1.0 - sg_all), axis=0, keepdims=True)
            dqt = jnp.concatenate(dqt_heads, axis=1)
            n_all = jnp.concatenate(n_heads, axis=1)
            gain_all = jnp.concatenate([gain_v] * HG_HEADS, axis=1)
            dz_ref[rows, 0:HG_W] = (dqt * sgq_all * (1.0 + q_all * (1.0 - sgq_all))).astype(BF16)
            dz_ref[rows, HG_W:2 * HG_W] = (dfg * (1.0 - lb_all) * sg_all * (1.0 - sg_all)).astype(BF16)
            dz_ref[rows, 2 * HG_W:3 * HG_W] = jnp.concatenate(dv_heads, axis=1).astype(BF16)
            dz_ref[rows, 3 * HG_W:4 * HG_W] = (dy_ref[rows, :] * n_all * gain_all * sgg_all
                                               * (1.0 + g_all * (1.0 - sgg_all))).astype(BF16)
            return carry

        lax.fori_loop(0, cpb, chunk, 0, unroll=2)

        @pl.when(step == nb - 1)
        def _():
            d0 = dlb_ref[...] * lb_all * (1.0 - lb_all)
            dlg_ref[0:1, :] = d0
            dlg_ref[1:2, :] = -d0

    rev = lambda i: (nb - 1 - i, 0)
    fix = lambda i: (0, 0)
    return _pallas(
        body, name="hgrn_bwd", grid=(nb,),
        in_specs=[pl.BlockSpec((RB, HG_COLS), rev), pl.BlockSpec((RB, HG_W), rev), pl.BlockSpec((RB, HG_W), rev),
                  pl.BlockSpec((cpb * HG_HEADS * DK, DK), rev), pl.BlockSpec((2, HG_W), fix),
                  pl.BlockSpec((1, DK), fix), pl.BlockSpec(memory_space=pl.ANY)],
        out_specs=[pl.BlockSpec((RB, HG_COLS), rev), pl.BlockSpec((2, HG_W), fix), pl.BlockSpec((1, DK), fix)],
        out_shape=[jax.ShapeDtypeStruct(dz_buf.shape, BF16), jax.ShapeDtypeStruct((2, HG_W), F32),
                   jax.ShapeDtypeStruct((1, DK), F32)],
        scratch_shapes=[pltpu.VMEM((HG_HEADS, DK, DK), F32), pltpu.VMEM((1, HG_W), F32)],
        input_output_aliases={6: 0},
        compiler_params=_params(dimension_semantics=("arbitrary",)),
    )(z, o_raw, dy, states, lb_logits, gain, dz_buf)


def _head_ones():
    r, c = _iota((FOX_W, FOX_W), 0), _iota((FOX_W, FOX_W), 1)
    return ((r // FOX_DH) == (c // FOX_DH)).astype(BF16)


def _log_sigmoid(x):
    return jnp.minimum(x, 0.0) - jnp.log(1.0 + jnp.exp(-jnp.abs(x)))


def _fox_prep(z, bias, qg, kg):
    T = z.shape[0]
    tm = min(ROW_BLOCK, T)
    nb = T // tm

    def body(z_ref, b_ref, qg_ref, kg_ref, q_ref, k_ref, v_ref, qa_ref, ka_ref, carry_ref):
        @pl.when(pl.program_id(0) == 0)
        def _():
            carry_ref[...] = jnp.zeros_like(carry_ref)

        ones = _head_ones()
        normed = []
        for src, g_ref in ((0, qg_ref), (1, kg_ref)):
            xv = z_ref[:, src * FOX_W:(src + 1) * FOX_W]
            ms = _split_dot(ones, xv * xv, 1, 0, mat_first=False) * (1.0 / FOX_DH)
            normed.append(xv * lax.rsqrt(ms + EPS) * g_ref[...])
        qn, kn = normed
        q_ref[...] = (qn * FOX_DH ** -0.5).astype(BF16)
        k_b = kn.astype(BF16)
        k_ref[...] = k_b
        v_ref[...] = z_ref[:, 2 * FOX_W:3 * FOX_W].astype(BF16)
        logf = _log_sigmoid(z_ref[:, 3 * FOX_W:FOX_COLS] + b_ref[...])
        r, c = _iota((tm, tm), 0), _iota((tm, tm), 1)
        tri_b = (c <= r).astype(BF16)
        cum = _split_dot(tri_b, logf, 1, 0, terms=3) + carry_ref[...]
        carry_ref[...] = cum[tm - 1:tm]
        c2 = cum * LOG2E
        hi = c2.astype(BF16)
        rem = c2 - hi.astype(F32)
        mid = rem.astype(BF16)
        lo = (rem - mid.astype(F32)).astype(BF16)
        hrow, col = _iota((LANES, 2 * FOX_W), 0), _iota((LANES, 2 * FOX_W), 1)
        base = hrow * LANES + jnp.where(hrow % 2 == 0, FOX_DH, 0)
        placed = None
        for t, part in enumerate((hi, mid, lo)):
            place = jnp.logical_and(col == base + t, hrow < FOX_HEADS).astype(BF16)
            term = _dot(part, place, 1, 0)
            placed = term if placed is None else placed + term
        colw = _iota((tm, 2 * FOX_W), 1)
        head, lane = colw // LANES, colw % LANES
        own = (lane < FOX_DH) == (head % 2 == 0)
        other = jnp.where(head % 2 == 0, lane - FOX_DH, lane)
        ones_q = jnp.where(jnp.logical_and(other >= 0, other < 3), -1.0, 0.0)
        q2 = (qn * (FOX_DH ** -0.5 * LOG2E)).astype(BF16)
        q_exp = jnp.concatenate([q2[:, (h // 2) * LANES:(h // 2 + 1) * LANES] for h in range(FOX_HEADS)], axis=1)
        k_exp = jnp.concatenate([k_b[:, (h // 2) * LANES:(h // 2 + 1) * LANES] for h in range(FOX_HEADS)], axis=1)
        qa_ref[...] = jnp.where(own, q_exp, ones_q.astype(BF16))
        ka_ref[...] = jnp.where(own, k_exp, placed.astype(BF16))

    wide = pl.BlockSpec((tm, 2 * FOX_W), lambda i: (i, 0))
    return _pallas(
        body, name="fox_prep", grid=(nb,),
        in_specs=[pl.BlockSpec((tm, SEG), lambda i: (i, 1)), pl.BlockSpec((1, LANES), lambda i: (0, 0)),
                  pl.BlockSpec((1, FOX_W), lambda i: (0, 0)), pl.BlockSpec((1, FOX_W), lambda i: (0, 0))],
        out_specs=[pl.BlockSpec((tm, FOX_W), lambda i: (i, 0))] * 3 + [wide] * 2,
        out_shape=[jax.ShapeDtypeStruct((T, FOX_W), BF16)] * 3 + [jax.ShapeDtypeStruct((T, 2 * FOX_W), BF16)] * 2,
        scratch_shapes=[pltpu.VMEM((1, LANES), F32)],
        compiler_params=_params(dimension_semantics=("arbitrary",)),
    )(z, bias, qg, kg)


def _fox_fwd(qa, ka, vb, blocks):
    T = qa.shape[0]
    tq = min(ROW_BLOCK, T)
    nq = T // tq
    NEG = -1e30
    n = len(blocks)
    n_pairs = FOX_HEADS // 2

    n_in = 3

    def body(*refs):
        q_ref, k_ref, v_ref = refs[:n_in]
        o_ref, lse_ref = refs[n_in + n:n_in + n + 2]
        m_sc, l_sc, acc_sc = refs[n_in + 2 * n + 2:n_in + 2 * n + 5]
        pr, qi = pl.program_id(0), pl.program_id(1)
        g_start, g_forward, g_finish = _gather_steps(
            refs[n_in:n_in + n], refs[n_in + n + 2:n_in + 2 * n + 2], *refs[n_in + 2 * n + 5:])

        @pl.when(jnp.logical_and(pr == 0, qi == 0))
        def _():
            g_start()

        @pl.when(jnp.logical_and(pr == n_pairs // 2, qi == 0))
        def _():
            g_forward()

        m_sc[...] = jnp.full_like(m_sc, NEG)
        l_sc[...] = jnp.zeros_like(l_sc)
        acc_sc[...] = jnp.zeros_like(acc_sc)
        lane = _iota((tq, LANES), 1)

        def block(masked, ki):
            keys = pl.ds(pl.multiple_of(ki * tq, tq), tq)
            vv = v_ref[keys, :]
            for hh in range(2):
                hs = slice(hh * LANES, (hh + 1) * LANES)
                s = _dot(q_ref[:, hs], k_ref[keys, hs], 1, 1)
                tiles = [s[:, j * LANES:(j + 1) * LANES] for j in range(tq // LANES)]
                if masked:
                    row, col = _iota((tq, LANES), 0), _iota((tq, LANES), 1)
                    tiles = [jnp.where(row >= col + j * LANES, t, NEG) for j, t in enumerate(tiles)]
                m_old = m_sc[hh]
                top = jnp.broadcast_to(jnp.max(functools.reduce(jnp.maximum, tiles), axis=-1, keepdims=True),
                                       (tq, LANES))
                m_new = jnp.maximum(m_old, top)
                alpha = jnp.exp2(m_old - m_new)
                ps = [jnp.exp2(t - m_new) for t in tiles]
                l_sc[hh] = alpha * l_sc[hh] + functools.reduce(jnp.add, ps)
                m_sc[hh] = m_new
                p_b = jnp.concatenate([p.astype(BF16) for p in ps], axis=1)
                acc_sc[hh] = alpha * acc_sc[hh] + _dot(p_b, vv, 1, 0)

        def before(ki, carry):
            block(False, ki)
            return carry

        lax.fori_loop(0, qi, before, 0)
        block(True, qi)
        l0 = jnp.sum(l_sc[0], axis=-1, keepdims=True)
        l1 = jnp.sum(l_sc[1], axis=-1, keepdims=True)
        o_ref[...] = jnp.where(lane < FOX_DH, acc_sc[0] * (1.0 / l0), acc_sc[1] * (1.0 / l1))
        lse_ref[:, :LANES] = m_sc[0] + jnp.log2(l0)
        lse_ref[:, LANES:] = m_sc[1] + jnp.log2(l1)

        @pl.when(jnp.logical_and(pr == n_pairs - 1, qi == nq - 1))
        def _():
            g_finish()

    qmap = lambda p, i: (i, p)
    whole = lambda p, i: (0, p)
    hbm = pl.BlockSpec(memory_space=pl.ANY)
    res = _pallas(
        body, name="fox_fwd", grid=(n_pairs, nq),
        in_specs=[pl.BlockSpec((tq, 2 * LANES), qmap), pl.BlockSpec((T, 2 * LANES), whole),
                  pl.BlockSpec((T, LANES), whole)] + [hbm] * n,
        out_specs=[pl.BlockSpec((tq, LANES), qmap), pl.BlockSpec((tq, 2 * LANES), qmap)] + [hbm] * n,
        out_shape=[jax.ShapeDtypeStruct((T, FOX_W), F32), jax.ShapeDtypeStruct((T, 2 * FOX_W), F32)]
        + [jax.ShapeDtypeStruct((N_DEV,) + b.shape, b.dtype) for b in blocks],
        scratch_shapes=[pltpu.VMEM((2, tq, LANES), F32)] * 3 + _gather_scratch(n),
        compiler_params=_params(dimension_semantics=("arbitrary",) * 2),
    )(qa, ka, vb, *blocks)
    return res[0], res[1], res[2:]


def _fox_bwd(qs, kn, vb, qa, ka, o, do, lse, hs):
    T = qs.shape[0]
    tq = min(ROW_BLOCK, T)
    nq = T // tq
    n = len(hs)
    n_pairs = FOX_HEADS // 2

    def body(*refs):
        q_ref, k_ref, v_ref, qa_ref, ka_ref, o_ref, do_ref, lse_ref = refs[:8]
        dq_ref, dk_ref, dv_ref, dcs_ref, drs_ref = refs[8 + n:13 + n]
        pr, ki = pl.program_id(0), pl.program_id(1)
        x_start, x_finish = _chip_exchange_steps(refs[8:8 + n], refs[13 + n:13 + 2 * n], *refs[13 + 2 * n:])

        @pl.when(jnp.logical_and(pr == 0, ki == 0))
        def _():
            x_start()
            drs_ref[...] = jnp.zeros_like(drs_ref)

        @pl.when(ki == 0)
        def _():
            dq_ref[...] = jnp.zeros_like(dq_ref)

        dk_ref[...] = jnp.zeros_like(dk_ref)
        dv_ref[...] = jnp.zeros_like(dv_ref)
        dcs_ref[...] = jnp.zeros_like(dcs_ref)

        def block(masked, qi):
            lane = _iota((tq, LANES), 1)
            qrows = pl.ds(pl.multiple_of(qi * tq, tq), tq)
            qv, kv, vv = q_ref[qrows, :], k_ref[...], v_ref[...]
            ov, dov = o_ref[qrows, :], do_ref[qrows, :]
            dq_acc = jnp.zeros((tq, LANES), F32)
            dk_acc = jnp.zeros((tq, LANES), F32)
            dv_acc = jnp.zeros((tq, LANES), F32)
            dcs_acc = jnp.zeros((8, tq), F32)
            drs_acc = jnp.zeros((tq, LANES), F32)
            prod = dov * ov
            nt = tq // LANES
            for hh in range(2):
                in_head = (lane < FOX_DH) if hh == 0 else (lane >= FOX_DH)
                hs_ = slice(hh * LANES, (hh + 1) * LANES)
                zb = jnp.zeros_like(qv)
                qm = jnp.where(in_head, qv, zb)
                km = jnp.where(in_head, kv, zb)
                dom = jnp.where(in_head, dov, 0.0).astype(BF16)
                delta_b = jnp.broadcast_to(jnp.sum(jnp.where(in_head, prod, 0.0), axis=1, keepdims=True), (tq, LANES))
                lse_b = lse_ref[qrows, hs_]
                s = _dot(qa_ref[qrows, hs_], ka_ref[:, hs_], 1, 1)
                dp = _dot(dom, vv, 1, 1)
                p_tiles, ds_tiles, col_tiles = [], [], []
                row_part = jnp.zeros((tq, LANES), F32)
                for j in range(nt):
                    js = slice(j * LANES, (j + 1) * LANES)
                    p = jnp.exp2(s[:, js] - lse_b)
                    if masked:
                        p = jnp.where(_iota((tq, LANES), 0) >= _iota((tq, LANES), 1) + j * LANES, p, 0.0)
                    ds = p * (dp[:, js] - delta_b)
                    p_tiles.append(p.astype(BF16))
                    ds_tiles.append(ds.astype(BF16))
                    col_tiles.append(jnp.sum(ds, axis=0, keepdims=True))
                    row_part = row_part + ds
                p_b = jnp.concatenate(p_tiles, axis=1)
                ds_b = jnp.concatenate(ds_tiles, axis=1)
                dv_acc = dv_acc + _dot(p_b, dom, 0, 0)
                dq_acc = dq_acc + _dot(ds_b, km, 1, 0)
                dk_acc = dk_acc + _dot(ds_b, qm, 0, 0)
                dcs_acc = dcs_acc + jnp.where(_iota((8, tq), 0) == hh, jnp.concatenate(col_tiles, axis=1), 0.0)
                rowsum = jnp.sum(row_part, axis=1, keepdims=True)
                drs_acc = drs_acc + jnp.where(lane == 2 * pr + hh, rowsum, 0.0)
            drs_ref[qrows, :] += drs_acc
            dq_ref[qrows, :] += dq_acc
            dk_ref[...] += dk_acc
            dv_ref[...] += dv_acc
            dcs_ref[0] += dcs_acc

        block(True, ki)

        def after(qi, carry):
            block(False, qi)
            return carry

        lax.fori_loop(ki + 1, nq, after, 0)

        @pl.when(jnp.logical_and(pr == n_pairs - 1, ki == nq - 1))
        def _():
            x_finish()

    whole = lambda p, j: (0, p)
    kmap = lambda p, j: (j, p)
    hbm = pl.BlockSpec(memory_space=pl.ANY)
    res = _pallas(
        body, name="fox_bwd", grid=(n_pairs, nq),
        in_specs=[pl.BlockSpec((T, LANES), whole), pl.BlockSpec((tq, LANES), kmap), pl.BlockSpec((tq, LANES), kmap),
                  pl.BlockSpec((T, 2 * LANES), whole), pl.BlockSpec((tq, 2 * LANES), kmap),
                  pl.BlockSpec((T, LANES), whole), pl.BlockSpec((T, LANES), whole), pl.BlockSpec((T, 2 * LANES), whole)]
        + [hbm] * n,
        out_specs=[pl.BlockSpec((T, LANES), whole), pl.BlockSpec((tq, LANES), kmap),
                   pl.BlockSpec((tq, LANES), kmap), pl.BlockSpec((1, 8, tq), lambda p, j: (p, 0, j)),
                   pl.BlockSpec((T, LANES), lambda p, j: (0, 0))] + [hbm] * n,
        out_shape=[jax.ShapeDtypeStruct((T, FOX_W), F32)] * 3
        + [jax.ShapeDtypeStruct((n_pairs, 8, T), F32), jax.ShapeDtypeStruct((T, LANES), F32)]
        + _chip_exchange_shapes(hs),
        scratch_shapes=_chip_exchange_scratch(n),
        compiler_params=_params(dimension_semantics=("arbitrary",) * 2),
    )(qs, kn, vb, qa, ka, o, do, lse, *hs)
    res = list(res)
    return res[:5] + [res[5:]]


def _fox_post(z, dq, dk, dv, dcs, drs, bias, qg, kg, dz_buf):
    T = z.shape[0]
    tm = min(ROW_BLOCK, T)
    nb = T // tm

    def body(z_ref, dq_ref, dk_ref, dv_ref, dcs_ref, drs_ref, b_ref, qg_ref, kg_ref, _buf_ref, dz_ref, dqg_ref, dkg_ref,
             db_ref, carry_ref):
        @pl.when(pl.program_id(0) == 0)
        def _():
            carry_ref[...] = jnp.zeros_like(carry_ref)
            dqg_ref[...] = jnp.zeros_like(dqg_ref)
            dkg_ref[...] = jnp.zeros_like(dkg_ref)
            db_ref[...] = jnp.zeros_like(db_ref)

        ones = _head_ones()
        for src, g_ref, d_ref, dg_ref, scale in ((0, qg_ref, dq_ref, dqg_ref, FOX_DH ** -0.5), (1, kg_ref, dk_ref, dkg_ref, 1.0)):
            xv = z_ref[:, src * FOX_W:(src + 1) * FOX_W]
            ms = _split_dot(ones, xv * xv, 1, 0, mat_first=False) * (1.0 / FOX_DH)
            rstd = lax.rsqrt(ms + EPS)
            xh = xv * rstd
            dn = d_ref[...] * scale
            dg_ref[...] += jnp.sum(dn * xh, axis=0, keepdims=True)
            dxh = dn * g_ref[...]
            mean = _split_dot(ones, dxh * xh, 1, 0, mat_first=False) * (1.0 / FOX_DH)
            dz_ref[:, src * FOX_W:(src + 1) * FOX_W] = (rstd * (dxh - xh * mean)).astype(BF16)
        dz_ref[:, 2 * FOX_W:3 * FOX_W] = dv_ref[...].astype(BF16)
        row8 = _iota((8, tm), 0)
        dct = jnp.zeros((8, tm), F32)
        for h in range(FOX_HEADS):
            src_row = dcs_ref[h // 2][h % 2:h % 2 + 1, :]
            dct = dct + jnp.where(row8 == h, src_row, 0.0)
        dct = drs_ref[...].T[0:8] - dct
        r, c = _iota((tm, tm), 0), _iota((tm, tm), 1)
        upper_b = (r >= c).astype(BF16)
        rc = _split_dot(upper_b, dct, 1, 0, mat_first=False) + carry_ref[...]
        carry_ref[...] = rc[:, 0:1]
        full = jnp.concatenate([rc, jnp.zeros((LANES - 8, tm), F32)], axis=0)
        dlogf = full.T
        xf = z_ref[:, 3 * FOX_W:FOX_COLS] + b_ref[...]
        df = dlogf * (1.0 - _sigmoid(xf))
        dz_ref[:, 3 * FOX_W:FOX_COLS] = df.astype(BF16)
        dz_ref[:, FOX_COLS:] = jnp.zeros((tm, SEG - FOX_COLS), BF16)
        db_ref[...] += jnp.sum(df, axis=0, keepdims=True)

    rev = lambda i: (nb - 1 - i, 0)
    fix2 = lambda i: (0, 0)
    return _pallas(
        body, name="fox_post", grid=(nb,),
        in_specs=[pl.BlockSpec((tm, SEG), lambda i: (nb - 1 - i, 1)), pl.BlockSpec((tm, FOX_W), rev),
                  pl.BlockSpec((tm, FOX_W), rev),
                  pl.BlockSpec((tm, FOX_W), rev), pl.BlockSpec((FOX_HEADS // 2, 8, tm), lambda i: (0, 0, nb - 1 - i)),
                  pl.BlockSpec((tm, LANES), rev),
                  pl.BlockSpec((1, LANES), fix2), pl.BlockSpec((1, FOX_W), fix2), pl.BlockSpec((1, FOX_W), fix2),
                  pl.BlockSpec(memory_space=pl.ANY)],
        out_specs=[pl.BlockSpec((tm, SEG), lambda i: (nb - 1 - i, 1)), pl.BlockSpec((1, FOX_W), fix2),
                   pl.BlockSpec((1, FOX_W), fix2), pl.BlockSpec((1, LANES), fix2)],
        out_shape=[jax.ShapeDtypeStruct(dz_buf.shape, BF16), jax.ShapeDtypeStruct((1, FOX_W), F32),
                   jax.ShapeDtypeStruct((1, FOX_W), F32), jax.ShapeDtypeStruct((1, LANES), F32)],
        scratch_shapes=[pltpu.VMEM((8, 1), F32)],
        input_output_aliases={9: 0},
        compiler_params=_params(dimension_semantics=("arbitrary",)),
    )(z, dq, dk, dv, dcs, drs, bias, qg, kg, dz_buf)


def _local_step(x, p, tgt, sm, W, rest_chunks, core):
    lbl, og, fb = sm["hg_lb_logits"], sm["hg_onorm_g"], sm["fox_f_bias"]
    fbias = jnp.pad(fb, ((0, 0), (0, LANES - FOX_HEADS)))
    qg = jnp.tile(sm["fox_q_norm_g"], (1, FOX_HEADS))
    kg = jnp.tile(sm["fox_k_norm_g"], (1, FOX_HEADS))

    h = _rms_fwd(x, sm["norm_mix_g"], name="rms_mix")
    rest = dict(zip(BIG[1:], rest_chunks))
    first, second = ["w_ffn_gate"], ["w_ffn_up"]
    third = [n for n in BIG[1:] if n not in first + second]
    z, got1 = _matmul(h, W["w_in"], tb=True, gather=[rest[n] for n in first], name="mm_z")
    o_raw, ya, states, got2 = _hgrn_fwd(z, lbl, og, [rest[n] for n in second])
    qs, kn, vb, qa, ka = _fox_prep(z, fbias, qg, kg)
    yb, lse, got3 = _fox_fwd(qa, ka, vb, [rest[n] for n in third])
    W = dict(W, **{n: _full_of_chunks(n, g)
                   for n, g in zip(first + second + third, list(got1) + list(got2) + list(got3))})
    merged, ua, ub = _merge_fwd(ya, yb, W["w_branch_a"], W["w_branch_b"], z)
    x1, hf = _matmul(merged, W["w_out"], add=x, norm_fwd=sm["norm_ffn_g"], name="mm_x1")
    a, b, act = _swiglu_fwd(hf, W["w_ffn_gate"], W["w_ffn_up"])
    x2 = _matmul(act, W["w_ffn_down"], add=x1, name="mm_x2")
    hp, dy, dsp, dpp, loss = _ple_loss(x2, p, sm["norm_ple_g"], W["w_ple_gate"], W["w_ple_proj"], tgt)

    G = {}
    G["w_ple_proj"] = _matmul(p, dpp, ta=True, out_dtype=BF16, name="mm_dw_ple_proj")
    G["w_ple_gate"] = _matmul(hp, dsp, ta=True, out_dtype=BF16, name="mm_dw_ple_gate")
    dx2, d_ple_g = _matmul(dsp, W["w_ple_gate"], tb=True, norm_bwd=(x2, sm["norm_ple_g"], dy), name="mm_dx2")
    G["w_ffn_down"] = _matmul(act, dx2, ta=True, out_dtype=BF16, name="mm_dw_ffn_down")
    da, db = _swiglu_bwd(dx2, W["w_ffn_down"], a, b)
    G["w_ffn_gate"] = _matmul(da, hf, ta=True, out_dtype=BF16, name="mm_dw_ffn_gate")
    G["w_ffn_up"] = _matmul(db, hf, ta=True, out_dtype=BF16, name="mm_dw_ffn_up")
    dhf = _matmul(da, W["w_ffn_gate"], name="mm_dhf_a")
    dx1, d_ffn_g = _matmul(db, W["w_ffn_up"], add=dhf, norm_bwd=(x1, sm["norm_ffn_g"], dx2), name="mm_dx1")
    G["w_out"] = _matmul(merged, dx1, ta=True, out_dtype=BF16, name="mm_dw_out")
    dua, dub, dz, dya, dyb = _merge_bwd(dx1, W["w_out"], W["w_branch_a"], W["w_branch_b"], ua, ub, z)
    G["w_branch_a"] = _matmul(ya, dua, ta=True, out_dtype=BF16, name="mm_dw_branch_a")
    G["w_branch_b"] = _matmul(yb, dub, ta=True, out_dtype=BF16, name="mm_dw_branch_b")
    hb_rest = _sibling_sums({n: G[n] for n in BIG[1:]}, core, tag="rest")
    dq, dk, dv, dcs, drs, got_rest = _fox_bwd(qs, kn, vb, qa, ka, yb, dyb, lse, hb_rest)
    dz, d_qg, d_kg, d_fb = _fox_post(z, dq, dk, dv, dcs, drs, fbias, qg, kg, dz)
    dz, d_lbl, d_og = _hgrn_bwd(z, o_raw, dya, states, lbl, og, dz)
    G["w_in"] = _matmul(dz, h, ta=True, out_dtype=BF16, name="mm_dw_in")
    hb_in = _sibling_sums({"w_in": G["w_in"]}, core, tag="w_in")
    (grad_x, d_mix_g), got_in = _matmul(dz, W["w_in"], exchange=hb_in,
                                        norm_bwd=(x, sm["norm_mix_g"], dx1), name="mm_dx")

    gs = {"norm_mix_g": d_mix_g, "hg_lb_logits": d_lbl, "hg_onorm_g": d_og, "fox_f_bias": d_fb[:, :FOX_HEADS],
          "fox_q_norm_g": d_qg.reshape(FOX_HEADS, FOX_DH).sum(0, keepdims=True),
          "fox_k_norm_g": d_kg.reshape(FOX_HEADS, FOX_DH).sum(0, keepdims=True),
          "norm_ffn_g": d_ffn_g, "norm_ple_g": d_ple_g}
    return loss, grad_x, gs, hb_in + hb_rest, list(got_in) + list(got_rest)


def _pack_rows(parts, total):
    buf = jnp.concatenate(parts, axis=-2)
    pad = total - buf.shape[-2]
    widths = [(0, 0)] * (buf.ndim - 2) + [(0, pad), (0, 0)]
    return jnp.pad(buf, widths)


def _chunk_of_shard(n, w):
    if n == "w_in":
        return jnp.pad(w, ((0, IN_SHARD_PAD - IN_SHARD), (0, 0)))
    if n in ("w_ffn_gate", "w_ffn_up", "w_ffn_down"):
        return jnp.pad(w, ((0, FF_SHARD_PAD - FF_SHARD), (0, 0)))
    return w


def _full_of_chunks(n, g):
    _, a, b = g.shape
    if n == "w_in":
        w = g[:, :IN_SHARD].reshape(IN_COLS, b)
        gap = jnp.zeros((SEG - FOX_LOGICAL, b), g.dtype)
        return jnp.concatenate([w[:HG_COLS + FOX_LOGICAL], gap, w[HG_COLS + FOX_LOGICAL:]], axis=0)
    if BIG_SHAPE[n][2] == 0:
        return g.reshape(N_DEV * a, b)
    return g.transpose(1, 0, 2).reshape(a, N_DEV * b)


def _chunks_of_full(n, g):
    if n == "w_in":
        w = jnp.concatenate([g[:HG_COLS + FOX_LOGICAL], g[2 * SEG:]], axis=0).reshape(N_DEV, IN_SHARD, g.shape[1])
        return jnp.pad(w, ((0, 0), (0, IN_SHARD_PAD - IN_SHARD), (0, 0)))
    if BIG_SHAPE[n][2] == 0:
        return g.reshape(N_DEV, g.shape[0] // N_DEV, g.shape[1])
    return g.reshape(g.shape[0], N_DEV, g.shape[1] // N_DEV).transpose(1, 0, 2)


def _pack_small(vals, loss_row=None):
    parts = [vals[n].reshape(SMALL_ROWS[n], -1) for n in SMALL]
    parts = [jnp.pad(v, ((0, 0), (0, LANES - v.shape[1]))) for v in parts]
    if loss_row is not None:
        parts.append(loss_row)
    return _pack_rows(parts, SMALL_TOTAL)


def _unpack_small(buf, like):
    out, r0 = {}, 0
    for n in SMALL:
        rows, size = SMALL_ROWS[n], like[n].size
        blk = buf[r0:r0 + rows]
        out[n] = (blk if size == rows * LANES else blk[:, :size]).reshape(like[n].shape)
        r0 += rows
    return out


def _place():
    return lax.axis_index("x"), lax.axis_index("y"), lax.axis_index("c")


def _gather_steps(x_refs, out_refs, send_sems, recv_sems, local_sems):
    n = len(x_refs)
    x, y, c = _place()
    me, sibling = (x, y, c), (x, y, 1 - c)
    chips = [(1 - x, y), (x, 1 - y), (1 - x, 1 - y)]

    def slot(i, px, py, pc):
        return out_refs[i].at[4 * px + 2 * py + pc]

    def copy(k, i, blk, to, own=False):
        return pltpu.make_async_remote_copy(
            src_ref=x_refs[i] if own else slot(i, *blk), dst_ref=slot(i, *blk),
            send_sem=send_sems.at[k, i], recv_sem=recv_sems.at[k, i], device_id=to, device_id_type=MESH)

    def mine():
        return [pltpu.make_async_copy(x_refs[i], slot(i, *me), local_sems.at[i]) for i in range(n)]

    def first():
        cps = [copy(0, i, me, sibling, own=True) for i in range(n)]
        return cps + [copy(1 + j, i, me, (*chip, c), own=True) for j, chip in enumerate(chips) for i in range(n)]

    def passed():
        return [copy(4 + j, i, (*chip, c), sibling) for j, chip in enumerate(chips) for i in range(n)]

    def start():
        for cp in mine() + first():
            cp.start()

    def forward():
        fws = passed()
        for j, chip in enumerate(chips):
            for i in range(n):
                copy(1 + j, i, (*chip, c), me).wait_recv()
                fws[j * n + i].start()

    def finish():
        for i in range(n):
            copy(0, i, sibling, me).wait_recv()
        for j, chip in enumerate(chips):
            for i in range(n):
                copy(4 + j, i, (*chip, 1 - c), me).wait_recv()
        for cp in first() + passed():
            cp.wait_send()
        for cp in mine():
            cp.wait()

    return start, forward, finish


def _gather_scratch(n):
    return [pltpu.SemaphoreType.DMA((7, n)), pltpu.SemaphoreType.DMA((7, n)), pltpu.SemaphoreType.DMA((n,))]


def _all_gather(blocks, *, name):
    n = len(blocks)

    def body(*refs):
        for step in _gather_steps(refs[:n], refs[n:2 * n], *refs[2 * n:]):
            step()

    hbm = pl.BlockSpec(memory_space=pl.ANY)
    return _pallas(
        body, name=name, out_shape=[jax.ShapeDtypeStruct((N_DEV,) + b.shape, b.dtype) for b in blocks],
        in_specs=[hbm] * n, out_specs=[hbm] * n, scratch_shapes=_gather_scratch(n),
    )(*blocks)


def _sibling_exchange(gs, *, name):
    n = len(gs)

    def body(*refs):
        g_refs, out_refs = refs[:n], refs[n:2 * n]
        send_sems, recv_sems = refs[2 * n:]
        x, y, c = _place()
        cps = [pltpu.make_async_remote_copy(
            src_ref=g_refs[i].at[:, pl.ds(1 - c, 1)], dst_ref=out_refs[i], send_sem=send_sems.at[i],
            recv_sem=recv_sems.at[i], device_id=(x, y, 1 - c), device_id_type=MESH) for i in range(n)]
        for cp in cps:
            cp.start()
        for cp in cps:
            cp.wait()

    hbm = pl.BlockSpec(memory_space=pl.ANY)
    return _pallas(
        body, name=name, out_shape=[jax.ShapeDtypeStruct((N_CHIP, 1) + g.shape[2:], g.dtype) for g in gs],
        in_specs=[hbm] * n, out_specs=[hbm] * n,
        scratch_shapes=[pltpu.SemaphoreType.DMA((n,)), pltpu.SemaphoreType.DMA((n,))],
    )(*gs)


def _chip_sum(g4s, gots, core, *, name):
    n = len(g4s)

    def body(c_ref, *refs):
        for g_ref, r_ref, h_ref in zip(refs[:n], refs[n:2 * n], refs[2 * n:]):
            h_ref[0] = (g_ref[0, 0].astype(F32) + r_ref[0, 0].astype(F32)).astype(BF16)

    shapes = [g.shape[2:] for g in g4s]
    grid_spec = pltpu.PrefetchScalarGridSpec(
        num_scalar_prefetch=1, grid=(N_CHIP,),
        in_specs=[pl.BlockSpec((1, 1) + s, lambda j, c: (j, c[0], 0, 0)) for s in shapes]
        + [pl.BlockSpec((1, 1) + s, lambda j, c: (j, 0, 0, 0)) for s in shapes],
        out_specs=[pl.BlockSpec((1,) + s, lambda j, c: (j, 0, 0)) for s in shapes])
    return list(_pallas(
        body, name=name, grid_spec=grid_spec, out_shape=[jax.ShapeDtypeStruct((N_CHIP,) + s, BF16) for s in shapes],
        compiler_params=_params(dimension_semantics=("arbitrary",)),
    )(core, *g4s, *gots))


def _chip_exchange_steps(h_refs, out_refs, send_sems, recv_sems):
    n = len(h_refs)
    x, y, c = _place()
    chips = [(1 - x, y), (x, 1 - y), (1 - x, 1 - y)]

    def copies():
        return [pltpu.make_async_remote_copy(
            src_ref=h_refs[i].at[2 * px + py], dst_ref=out_refs[i].at[k], send_sem=send_sems.at[k, i],
            recv_sem=recv_sems.at[k, i], device_id=(px, py, c), device_id_type=MESH)
            for k, (px, py) in enumerate(chips) for i in range(n)]

    def start():
        for cp in copies():
            cp.start()

    def finish():
        for cp in copies():
            cp.wait()

    return start, finish


def _chip_exchange_shapes(hs):
    return [jax.ShapeDtypeStruct((3,) + h.shape[1:], h.dtype) for h in hs]


def _chip_exchange_scratch(n):
    return [pltpu.SemaphoreType.DMA((3, n)), pltpu.SemaphoreType.DMA((3, n))]


def _sibling_sums(G, core, *, tag):
    g4 = []
    for n, g in G.items():
        gc = _chunks_of_full(n, g)
        g4.append(gc.reshape((N_CHIP, 2) + gc.shape[1:]))
    got = _sibling_exchange(g4, name="grads_to_sibling_" + tag)
    return _chip_sum(g4, got, core, name="chip_sum_" + tag)


def _adam_math(w, g, m, v):
    m = ADAM_B1 * m + (1.0 - ADAM_B1) * g
    v = ADAM_B2 * v + (1.0 - ADAM_B2) * (g * g)
    m_hat = m / (1.0 - ADAM_B1 ** ADAM_STEP)
    v_hat = v / (1.0 - ADAM_B2 ** ADAM_STEP)
    delta = -ADAM_LR * (m_hat / (jnp.sqrt(v_hat) + ADAM_EPS) + ADAM_WD * w)
    return delta, m, v


def _adam_shard(hb, got, chip, w, m, v, *, name):
    _, r, c = w.shape
    _, a, cb = hb.shape
    assert cb == c and c % LANES == 0, (hb.shape, w.shape)
    tc = _pick(c, 2 * LANES)

    def body(j_ref, h_ref, r_ref, w_ref, m_ref, v_ref, g_ref, d_ref, nm_ref, nv_ref):
        parts = [h_ref[0], r_ref[0], r_ref[1], r_ref[2]]
        g = None
        for part in parts:
            part = part[:r].astype(F32)
            g = part if g is None else g + part
        d, nm, nv = _adam_math(w_ref[0], g, m_ref[0], v_ref[0])
        g_ref[0] = g
        d_ref[0] = d
        nm_ref[0] = nm
        nv_ref[0] = nv

    blk = pl.BlockSpec((1, r, tc), lambda i, j: (0, 0, i))
    grid_spec = pltpu.PrefetchScalarGridSpec(
        num_scalar_prefetch=1, grid=(c // tc,),
        in_specs=[pl.BlockSpec((1, a, tc), lambda i, j: (j[0], 0, i)),
                  pl.BlockSpec((3, a, tc), lambda i, j: (0, 0, i)), blk, blk, blk],
        out_specs=[blk] * 4)
    return _pallas(
        body, name=name, grid_spec=grid_spec, out_shape=[jax.ShapeDtypeStruct((1, r, c), F32)] * 4,
        compiler_params=_params(dimension_semantics=("arbitrary",)),
    )(chip, hb, got, w, m, v)


def _small_all_reduce_adam(gs, w, m, v):
    def body(g_ref, w_ref, m_ref, v_ref, sum_ref, d_ref, nm_ref, nv_ref, gather, send_sems, recv_sems):
        x, y, c = _place()
        my = 4 * x + 2 * y + c
        gather[my] = g_ref[...]
        cps = []
        for k in range(1, N_DEV):
            to = (x ^ (k >> 2), y ^ ((k >> 1) & 1), c ^ (k & 1))
            cps.append(pltpu.make_async_remote_copy(
                src_ref=g_ref, dst_ref=gather.at[my], send_sem=send_sems.at[k - 1], recv_sem=recv_sems.at[k - 1],
                device_id=to, device_id_type=MESH))
        for cp in cps:
            cp.start()
        for cp in cps:
            cp.wait()
        total = gather[0]
        for d in range(1, N_DEV):
            total = total + gather[d]
        dlt, nm, nv = _adam_math(w_ref[...], total, m_ref[...], v_ref[...])
        sum_ref[...] = total
        d_ref[...] = dlt
        nm_ref[...] = nm
        nv_ref[...] = nv

    vm = pl.BlockSpec(memory_space=pltpu.VMEM)
    return _pallas(
        body, name="small_all_reduce_adam", out_shape=[jax.ShapeDtypeStruct((SMALL_TOTAL, LANES), F32)] * 4,
        in_specs=[vm] * 4, out_specs=[vm] * 4,
        scratch_shapes=[pltpu.VMEM((N_DEV, SMALL_TOTAL, LANES), F32), pltpu.SemaphoreType.DMA((7,)),
                        pltpu.SemaphoreType.DMA((7,))],
            )(gs, w, m, v)


def kernel(x, p, norm_mix_g, w_in, hg_lb_logits, hg_onorm_g, fox_f_bias, fox_q_norm_g, fox_k_norm_g, w_branch_a, w_branch_b, w_out, norm_ffn_g, w_ffn_gate, w_ffn_up, w_ffn_down, norm_ple_g, w_ple_gate, w_ple_proj, loss_target, m_norm_mix_g, m_w_in, m_hg_lb_logits, m_hg_onorm_g, m_fox_f_bias, m_fox_q_norm_g, m_fox_k_norm_g, m_w_branch_a, m_w_branch_b, m_w_out, m_norm_ffn_g, m_w_ffn_gate, m_w_ffn_up, m_w_ffn_down, m_norm_ple_g, m_w_ple_gate, m_w_ple_proj, v_norm_mix_g, v_w_in, v_hg_lb_logits, v_hg_onorm_g, v_fox_f_bias, v_fox_q_norm_g, v_fox_k_norm_g, v_w_branch_a, v_w_branch_b, v_w_out, v_norm_ffn_g, v_w_ffn_gate, v_w_ffn_up, v_w_ffn_down, v_norm_ple_g, v_w_ple_gate, v_w_ple_proj):
    args = dict(locals())
    wts = {n: args[n] for n in BIG + SMALL}
    mom = {n: args["m_" + n] for n in BIG + SMALL}
    var = {n: args["v_" + n] for n in BIG + SMALL}
    for group in (wts, mom, var):
        for n in TRANSPOSED:
            group[n] = jnp.swapaxes(group[n], 1, 2)
    sm = {n: wts[n] for n in SMALL}

    xi, yi, ci = _place()
    core = jnp.reshape(ci, (1,)).astype(jnp.int32)
    chip = jnp.reshape(2 * xi + yi, (1,)).astype(jnp.int32)
    chunks = [_chunk_of_shard(n, wts[n][0].astype(BF16)) for n in BIG]
    assert BIG[0] == "w_in"
    w_in_full = _full_of_chunks("w_in", _all_gather(chunks[:1], name="w_in_all_gather")[0])

    loss_blk, grad_x, gs, hb, got = _local_step(
        x[0], p[0, 0], loss_target[0], sm, {"w_in": w_in_full}, chunks[1:], core)

    g_big, d_big, nm_big, nv_big = {}, {}, {}, {}
    for n, h, r in zip(BIG, hb, got):
        res = _adam_shard(h, r, chip, wts[n], mom[n], var[n], name="adam_" + n)
        if n in TRANSPOSED:
            res = [jnp.swapaxes(t, 1, 2) for t in res]
        g_big[n], d_big[n], nm_big[n], nv_big[n] = res

    s_sum, s_d, s_nm, s_nv = _small_all_reduce_adam(
        _pack_small(gs, loss_blk[0:1]), _pack_small(sm), _pack_small({n: mom[n] for n in SMALL}),
        _pack_small({n: var[n] for n in SMALL}))
    loss = s_sum[LOSS_ROW, 0]
    g_small, d_small, nm_small, nv_small = (_unpack_small(t, sm) for t in (s_sum, s_d, s_nm, s_nv))

    order = ["norm_mix_g", "w_in", "hg_lb_logits", "hg_onorm_g", "fox_f_bias", "fox_q_norm_g", "fox_k_norm_g",
             "w_branch_a", "w_branch_b", "w_out", "norm_ffn_g", "w_ffn_gate", "w_ffn_up", "w_ffn_down", "norm_ple_g",
             "w_ple_gate", "w_ple_proj"]
    outs = [loss, grad_x[None]]
    for big, small in ((g_big, g_small), (d_big, d_small), (nm_big, nm_small), (nv_big, nv_small)):
        outs += [big[n] if n in big else small[n] for n in order]
    return tuple(outs)
```

```python
import functools

import jax
import jax.numpy as jnp
from jax import lax
from jax.experimental import pallas as pl
from jax.experimental.pallas import tpu as pltpu

F32 = jnp.float32
BF16 = jnp.bfloat16

D_MODEL = 1024
PLE_DIM = 256
HG_HEADS = 4
HG_DK = 128
HG_CHUNK = 64
HG_SUB = 16
HG_W = HG_HEADS * HG_DK
FOX_HEADS = 8
FOX_DH = 64
FOX_W = FOX_HEADS * FOX_DH
D_FF = 2816
EPS = 1e-6
N_DEV = 8
N_CHIP = 4
LANES = 128
FOX_COLS = 3 * FOX_W + LANES
HG_COLS = 4 * HG_W
GATE_COLS = 2 * D_MODEL
IN_COLS = HG_COLS + 3 * FOX_W + FOX_HEADS + GATE_COLS
FOX_LOGICAL = 3 * FOX_W + FOX_HEADS
SEG = 2048
IN_PAD = 3 * SEG
IN_SHARD = IN_COLS // N_DEV
IN_SHARD_PAD = 768
FF_SHARD = D_FF // N_DEV
FF_SHARD_PAD = 384
EXP_CLAMP = 80.0
LOG2E = 1.4426950408889634

ADAM_LR = 0.001
ADAM_B1 = 0.9
ADAM_B2 = 0.999
ADAM_EPS = 1e-08
ADAM_WD = 0.01
ADAM_STEP = 10

MESH = pl.DeviceIdType.MESH
VMEM_LIMIT = 56 * 1024 * 1024
ROW_BLOCK = 512
MATMUL_TN = 2048
MATMUL_TK = 3072
FFN_TILE = 768

BIG = ["w_in", "w_branch_a", "w_branch_b", "w_out", "w_ffn_gate", "w_ffn_up", "w_ffn_down",
       "w_ple_gate", "w_ple_proj"]
TRANSPOSED = ("w_in", "w_ffn_gate", "w_ffn_up")
BIG_SHAPE = {
    "w_in": (IN_COLS, D_MODEL, 0), "w_branch_a": (HG_W, D_MODEL, 1), "w_branch_b": (FOX_W, D_MODEL, 1),
    "w_out": (D_MODEL, D_MODEL, 0), "w_ffn_gate": (D_FF, D_MODEL, 0), "w_ffn_up": (D_FF, D_MODEL, 0),
    "w_ffn_down": (D_FF, D_MODEL, 0), "w_ple_gate": (D_MODEL, D_MODEL, 0), "w_ple_proj": (PLE_DIM, D_MODEL, 1),
}

SMALL = ["norm_mix_g", "hg_lb_logits", "hg_onorm_g", "fox_f_bias", "fox_q_norm_g", "fox_k_norm_g",
         "norm_ffn_g", "norm_ple_g"]
SMALL_ROWS = {"norm_mix_g": 8, "hg_lb_logits": 8, "hg_onorm_g": 1, "fox_f_bias": 1, "fox_q_norm_g": 1,
              "fox_k_norm_g": 1, "norm_ffn_g": 8, "norm_ple_g": 8}
SMALL_TOTAL = 40
LOSS_ROW = 36


def _pallas(body, **kw):
    return pl.pallas_call(body, **kw)


def _params(**kw):
    return pltpu.CompilerParams(vmem_limit_bytes=VMEM_LIMIT, **kw)


def _pick(n, target):
    if n <= target:
        return n
    best = None
    for t in range(LANES, target + 1, LANES):
        if n % t == 0:
            best = t
    assert best is not None, (n, target)
    return best


def _dot(a, b, ca, cb):
    return lax.dot_general(a, b, (((ca,), (cb,)), ((), ())), preferred_element_type=F32)


def _split_dot(mat, x, ca, cb, terms=2, mat_first=True):
    acc = None
    rem = x
    for _ in range(terms):
        part = rem.astype(BF16)
        rem = rem - part.astype(F32)
        p = _dot(mat, part, ca, cb) if mat_first else _dot(part, mat, ca, cb)
        acc = p if acc is None else acc + p
    return acc


def _sigmoid(x):
    return 1.0 / (1.0 + jnp.exp(-x))


def _iota(shape, dim):
    return lax.broadcasted_iota(jnp.int32, shape, dim)


def _matmul(a, b, *, name, ta=False, tb=False, out_dtype=F32, add=None, exchange=None, gather=None,
            norm_fwd=None, norm_bwd=None):
    assert exchange is None or gather is None
    (K, M) = a.shape if ta else a.shape[::-1]
    (N, Kb) = b.shape if tb else b.shape[::-1]
    assert K == Kb, (a.shape, b.shape, ta, tb)
    if ta:
        tm, tn, tk = _pick(M, 2 * ROW_BLOCK), _pick(N, 2 * ROW_BLOCK), _pick(K, 4 * ROW_BLOCK)
    else:
        tm, tn, tk = _pick(M, 2 * ROW_BLOCK), _pick(N, MATMUL_TN), _pick(K, MATMUL_TK)
    if norm_bwd is not None:
        tm = _pick(M, ROW_BLOCK)
    nk = K // tk
    use_scratch = nk > 1 and out_dtype != F32
    if norm_fwd is not None or norm_bwd is not None:
        assert tn == N and not use_scratch and out_dtype == F32

    hs = list(exchange or gather or [])
    n_x = len(hs)
    grid = (M // tm, N // tn, nk)
    a_spec = pl.BlockSpec((tk, tm), lambda i, j, k: (k, i)) if ta else pl.BlockSpec((tm, tk), lambda i, j, k: (i, k))
    b_spec = pl.BlockSpec((tn, tk), lambda i, j, k: (j, k)) if tb else pl.BlockSpec((tk, tn), lambda i, j, k: (k, j))
    o_spec = pl.BlockSpec((tm, tn), lambda i, j, k: (i, j))
    row_vec = pl.BlockSpec((1, N), lambda i, j, k: (0, 0))
    hbm = pl.BlockSpec(memory_space=pl.ANY)
    extra_in = [(add, o_spec)] if add is not None else []
    extra_out = []
    if norm_fwd is not None:
        extra_in += [(norm_fwd, row_vec)]
        extra_out += [(jax.ShapeDtypeStruct((M, N), BF16), o_spec)]
    if norm_bwd is not None:
        extra_in += [(norm_bwd[0], o_spec), (norm_bwd[1], row_vec), (norm_bwd[2], o_spec)]
        extra_out += [(jax.ShapeDtypeStruct((1, N), F32), row_vec)]
    if gather is not None:
        ride_shapes, ride_scratch = [jax.ShapeDtypeStruct((N_DEV,) + h.shape, h.dtype) for h in hs], _gather_scratch(n_x)
    else:
        ride_shapes, ride_scratch = _chip_exchange_shapes(hs), (_chip_exchange_scratch(n_x) if n_x else [])
    n_ex_in, n_ex_out = len(extra_in), len(extra_out)

    def body(*refs):
        refs = list(refs)
        a_ref, b_ref = refs[:2]
        ex_in = refs[2:2 + n_ex_in]
        ride_in = refs[2 + n_ex_in:2 + n_ex_in + n_x]
        base = 2 + n_ex_in + n_x
        o_ref = refs[base]
        ex_out = refs[base + 1:base + 1 + n_ex_out]
        ride_out = refs[base + 1 + n_ex_out:base + 1 + n_ex_out + n_x]
        scratch = refs[base + 1 + n_ex_out + n_x:]
        at = [pl.program_id(d) for d in range(3)]
        k = at[2]
        if n_x:
            steps = _gather_steps if gather is not None else _chip_exchange_steps
            ride = steps(ride_in, ride_out, *scratch[-len(ride_scratch):])

            @pl.when(jnp.logical_and(at[0] == 0, jnp.logical_and(at[1] == 0, at[2] == 0)))
            def _():
                ride[0]()
        p = _dot(a_ref[...].astype(BF16), b_ref[...].astype(BF16), 0 if ta else 1, 1 if tb else 0)

        def finish(r):
            ins = list(ex_in)
            outs = list(ex_out)
            if add is not None:
                r = r + ins.pop(0)[...].astype(F32)
            if norm_fwd is not None:
                g_ref = ins.pop(0)
                rstd = lax.rsqrt(jnp.mean(r * r, axis=-1, keepdims=True) + EPS)
                outs.pop(0)[...] = (r * rstd * g_ref[...]).astype(BF16)
            if norm_bwd is not None:
                x_ref, g_ref, dres_ref = ins.pop(0), ins.pop(0), ins.pop(0)
                dg_ref = outs.pop(0)
                xv = x_ref[...]
                rstd = lax.rsqrt(jnp.mean(xv * xv, axis=-1, keepdims=True) + EPS)
                xh = xv * rstd
                part = jnp.sum(r * xh, axis=0, keepdims=True)

                @pl.when(at[0] == 0)
                def _():
                    dg_ref[...] = part

                @pl.when(at[0] > 0)
                def _():
                    dg_ref[...] += part

                dxh = r * g_ref[...]
                r = rstd * (dxh - xh * jnp.mean(dxh * xh, axis=-1, keepdims=True)) + dres_ref[...]
            o_ref[...] = r.astype(out_dtype)

        if nk == 1:
            finish(p)
        elif not use_scratch:
            @pl.when(k == 0)
            def _():
                o_ref[...] = p

            @pl.when(jnp.logical_and(k > 0, k < nk - 1))
            def _():
                o_ref[...] += p

            @pl.when(k == nk - 1)
            def _():
                finish(o_ref[...] + p)
        else:
            acc_ref = scratch[0]

            @pl.when(k == 0)
            def _():
                acc_ref[...] = p

            @pl.when(k > 0)
            def _():
                acc_ref[...] += p

            @pl.when(k == nk - 1)
            def _():
                finish(acc_ref[...])

        if n_x:
            @pl.when(jnp.logical_and(at[0] == grid[0] - 1, jnp.logical_and(at[1] == grid[1] - 1, at[2] == nk - 1)))
            def _():
                for step in ride[1:]:
                    step()

    res = _pallas(
        body, name=name, grid=grid,
        in_specs=[a_spec, b_spec] + [s for _, s in extra_in] + [hbm] * n_x,
        out_specs=[o_spec] + [s for _, s in extra_out] + [hbm] * n_x,
        out_shape=[jax.ShapeDtypeStruct((M, N), out_dtype)] + [s for s, _ in extra_out] + ride_shapes,
        scratch_shapes=([pltpu.VMEM((tm, tn), F32)] if use_scratch else []) + ride_scratch,
        compiler_params=_params(dimension_semantics=("arbitrary",) * 3),
    )(a, b, *[v for v, _ in extra_in], *hs)
    res = list(res)
    main = res[0] if n_ex_out == 0 else tuple(res[:1 + n_ex_out])
    return (main, res[1 + n_ex_out:]) if n_x else main


def _row_map(nb, reverse, seg):
    if reverse:
        return lambda i: (nb - 1 - i, seg)
    return lambda i: (i, seg)


def _row_call(body, *, name, T, ins, outs, acc_outs=(), tm=ROW_BLOCK, reverse=False):
    tm = min(tm, T)
    nb = T // tm
    in_specs, args = [], []
    for arr, how in ins:
        args.append(arr)
        if how is True:
            in_specs.append(pl.BlockSpec((tm, arr.shape[1]), _row_map(nb, reverse, 0)))
        elif how is False:
            in_specs.append(pl.BlockSpec(arr.shape, lambda i, _n=arr.ndim: (0,) * _n))
        else:
            in_specs.append(pl.BlockSpec((tm, SEG), _row_map(nb, reverse, how[0])))
    out_specs, out_shape = [], []
    for o in outs:
        c, dt = o[0], o[1]
        total, seg = o[2] if len(o) > 2 else (c, 0)
        out_specs.append(pl.BlockSpec((tm, c), _row_map(nb, reverse, seg)))
        out_shape.append(jax.ShapeDtypeStruct((T, total), dt))
    for shp, dt in acc_outs:
        out_specs.append(pl.BlockSpec(shp, lambda i, _n=len(shp): (0,) * _n))
        out_shape.append(jax.ShapeDtypeStruct(shp, dt))
    return _pallas(body, name=name, grid=(nb,), in_specs=in_specs, out_specs=out_specs, out_shape=out_shape,
                   compiler_params=_params(dimension_semantics=("arbitrary",)))(*args)


def _rms_fwd(x, g, blocks, *, name):
    T, D = x.shape
    tm = min(ROW_BLOCK, T)
    nb = T // tm
    n = len(blocks)

    def body(*refs):
        x_ref, g_ref = refs[:2]
        h_ref = refs[2 + n]
        g_start, g_forward, g_finish = _gather_steps(refs[2:2 + n], refs[3 + n:3 + 2 * n], *refs[3 + 2 * n:])

        @pl.when(pl.program_id(0) == 0)
        def _():
            g_start()

        xv = x_ref[...]
        rstd = lax.rsqrt(jnp.mean(xv * xv, axis=-1, keepdims=True) + EPS)
        h_ref[...] = (xv * rstd * g_ref[...]).astype(BF16)

        @pl.when(pl.program_id(0) == nb - 1)
        def _():
            g_forward()
            g_finish()

    hbm = pl.BlockSpec(memory_space=pl.ANY)
    res = _pallas(
        body, name=name, grid=(nb,),
        in_specs=[pl.BlockSpec((tm, D), lambda i: (i, 0)), pl.BlockSpec((1, D), lambda i: (0, 0))] + [hbm] * n,
        out_specs=[pl.BlockSpec((tm, D), lambda i: (i, 0))] + [hbm] * n,
        out_shape=[jax.ShapeDtypeStruct((T, D), BF16)] + [jax.ShapeDtypeStruct((N_DEV,) + b.shape, b.dtype) for b in blocks],
        scratch_shapes=_gather_scratch(n),
        compiler_params=_params(dimension_semantics=("arbitrary",)),
    )(x, g, *blocks)
    return res[0], list(res[1:])


def _merge_fwd(ya, yb, wa, wb, zg):
    def body(ya_ref, yb_ref, wa_ref, wb_ref, zg_ref, m_ref, ua_ref, ub_ref):
        ua = _dot(ya_ref[...].astype(BF16), wa_ref[...], 1, 0)
        ub = _dot(yb_ref[...].astype(BF16), wb_ref[...], 1, 0)
        ga = _sigmoid(zg_ref[:, :D_MODEL])
        gb = _sigmoid(zg_ref[:, D_MODEL:])
        m_ref[...] = (ga * ua + gb * ub).astype(BF16)
        ua_ref[...] = ua.astype(BF16)
        ub_ref[...] = ub.astype(BF16)

    return _row_call(body, name="merge_fwd", T=ya.shape[0],
                     ins=[(ya, True), (yb, True), (wa, False), (wb, False), (zg, (2,))],
                     outs=[(D_MODEL, BF16)] * 3)


def _merge_bwd(dx1, w_out, wa, wb, ua, ub, zg):
    def body(dx_ref, wo_ref, wa_ref, wb_ref, ua_ref, ub_ref, zg_ref, dua_ref, dub_ref, dzg_ref, dya_ref, dyb_ref):
        dmv = _dot(dx_ref[...].astype(BF16), wo_ref[...], 1, 1)
        ga = _sigmoid(zg_ref[:, :D_MODEL])
        gb = _sigmoid(zg_ref[:, D_MODEL:])
        dua = (dmv * ga).astype(BF16)
        dub = (dmv * gb).astype(BF16)
        dua_ref[...] = dua
        dub_ref[...] = dub
        dzg_ref[:, :D_MODEL] = (dmv * ua_ref[...].astype(F32) * ga * (1.0 - ga)).astype(BF16)
        dzg_ref[:, D_MODEL:] = (dmv * ub_ref[...].astype(F32) * gb * (1.0 - gb)).astype(BF16)
        dya_ref[...] = _dot(dua, wa_ref[...], 1, 1)
        dyb_ref[...] = _dot(dub, wb_ref[...], 1, 1)

    return _row_call(body, name="merge_bwd", T=dx1.shape[0],
                     ins=[(dx1, True), (w_out, False), (wa, False), (wb, False), (ua, True), (ub, True), (zg, (2,))],
                     outs=[(D_MODEL, BF16), (D_MODEL, BF16), (SEG, BF16, (IN_PAD, 2)), (HG_W, F32), (FOX_W, F32)])


def _swiglu_fwd(hf, w_gate, w_up):
    T, D = hf.shape
    F = w_gate.shape[0]
    tm, tn = _pick(T, 2 * ROW_BLOCK), _pick(F, FFN_TILE)

    def body(h_ref, wg_ref, wu_ref, a_ref, b_ref, o_ref):
        hv = h_ref[...]
        a_b = _dot(hv, wg_ref[...], 1, 1).astype(BF16)
        b_b = _dot(hv, wu_ref[...], 1, 1).astype(BF16)
        a_ref[...] = a_b
        b_ref[...] = b_b
        av = a_b.astype(F32)
        o_ref[...] = (av * _sigmoid(av) * b_b.astype(F32)).astype(BF16)

    tile = pl.BlockSpec((tm, tn), lambda i, j: (i, j))
    wcol = pl.BlockSpec((tn, D), lambda i, j: (j, 0))
    return _pallas(
        body, name="swiglu_fwd", grid=(T // tm, F // tn),
        in_specs=[pl.BlockSpec((tm, D), lambda i, j: (i, 0)), wcol, wcol],
        out_specs=[tile] * 3, out_shape=[jax.ShapeDtypeStruct((T, F), BF16)] * 3,
        compiler_params=_params(dimension_semantics=("arbitrary",) * 2),
    )(hf, w_gate, w_up)


def _swiglu_bwd(dx, w_down, a, b):
    T, D = dx.shape
    F = w_down.shape[0]
    tm, tn = _pick(T, 2 * ROW_BLOCK), _pick(F, FFN_TILE)

    def body(dx_ref, w_ref, a_ref, b_ref, da_ref, db_ref):
        dact = _dot(dx_ref[...].astype(BF16), w_ref[...], 1, 1)
        av = a_ref[...].astype(F32)
        bv = b_ref[...].astype(F32)
        sg = _sigmoid(av)
        da_ref[...] = (dact * bv * sg * (1.0 + av * (1.0 - sg))).astype(BF16)
        db_ref[...] = (dact * av * sg).astype(BF16)

    tile = pl.BlockSpec((tm, tn), lambda i, j: (i, j))
    return _pallas(
        body, name="swiglu_bwd", grid=(T // tm, F // tn),
        in_specs=[pl.BlockSpec((tm, D), lambda i, j: (i, 0)), pl.BlockSpec((tn, D), lambda i, j: (j, 0)), tile, tile],
        out_specs=[tile, tile], out_shape=[jax.ShapeDtypeStruct((T, F), BF16)] * 2,
        compiler_params=_params(dimension_semantics=("arbitrary",) * 2),
    )(dx, w_down, a, b)


def _ple_loss(x2, p, g, w_gate, w_proj, tgt):
    def body(x_ref, p_ref, g_ref, wg_ref, wp_ref, t_ref, hp_ref, dy_ref, dsp_ref, dpp_ref, loss_ref):
        xv = x_ref[...]
        rstd = lax.rsqrt(jnp.mean(xv * xv, axis=-1, keepdims=True) + EPS)
        hp = (xv * rstd * g_ref[...]).astype(BF16)
        hp_ref[...] = hp
        gp = _sigmoid(_dot(hp, wg_ref[...], 1, 0))
        ppv = _dot(p_ref[...].astype(BF16), wp_ref[...], 1, 0)
        err = xv + gp * ppv - t_ref[...]
        part = 0.5 * jnp.sum(jnp.mean(err * err, axis=-1, keepdims=True), axis=0, keepdims=True)
        part = jnp.broadcast_to(part, loss_ref.shape)

        @pl.when(pl.program_id(0) == 0)
        def _():
            loss_ref[...] = part

        @pl.when(pl.program_id(0) > 0)
        def _():
            loss_ref[...] += part

        dy = err * (1.0 / D_MODEL)
        dy_ref[...] = dy
        dsp_ref[...] = (dy * ppv * gp * (1.0 - gp)).astype(BF16)
        dpp_ref[...] = (dy * gp).astype(BF16)

    return _row_call(body, name="ple_loss", T=x2.shape[0],
                     ins=[(x2, True), (p, True), (g, False), (w_gate, False), (w_proj, False), (tgt, True)],
                     outs=[(D_MODEL, BF16), (D_MODEL, F32), (D_MODEL, BF16), (D_MODEL, BF16)],
                     acc_outs=[((8, LANES), F32)])


def _hg_tri():
    C = HG_CHUNK
    return _iota((C, C), 1) <= _iota((C, C), 0)


def _hg_ref_row(b, blk):
    mid = blk * HG_SUB + HG_SUB // 2 - 1
    return b[mid:mid + 1]


def _hg_chunk_fwd(q, f, lb, tri_b):
    sgq = _sigmoid(q)
    qt = q * sgq
    sg = _sigmoid(f)
    fg = lb + (1.0 - lb) * sg
    kf = (1.0 - lb) * (1.0 - sg)
    logf = jnp.log(fg)
    b = _split_dot(tri_b, logf, 1, 0)
    refs = [jnp.broadcast_to(_hg_ref_row(b, blk), (HG_SUB, b.shape[1])) for blk in range(HG_CHUNK // HG_SUB)]
    w = jnp.minimum(b - jnp.concatenate(refs, axis=0), EXP_CLAMP)
    return sgq, qt, sg, fg, kf, b, w


def _hg_scores(qs_b, kf, b, row):
    C, S = HG_CHUNK, HG_SUB
    parts, ks = [], []
    for blk in range(C // S):
        e = jnp.exp(jnp.minimum(_hg_ref_row(b, blk) - b, EXP_CLAMP))
        e = jnp.where(row < (blk + 1) * S, e, 0.0)
        k_b = (kf * e).astype(BF16)
        ks.append((e, k_b))
        parts.append(_dot(qs_b[blk * S:(blk + 1) * S], k_b, 1, 1))
    return jnp.concatenate(parts, axis=0), ks


def _hgrn_fwd(z, lb_logits, gain, blocks):
    T = z.shape[0]
    RB = min(ROW_BLOCK, T)
    nb, cpb = T // RB, RB // HG_CHUNK
    C, DK = HG_CHUNK, HG_DK
    n = len(blocks)

    def body(*refs):
        z_ref, lg_ref, g_ref = refs[:3]
        o_ref, y_ref, st_ref = refs[3 + n:6 + n]
        s_ref = refs[6 + 2 * n]
        g_start, g_forward, g_finish = _gather_steps(refs[3:3 + n], refs[6 + n:6 + 2 * n], *refs[7 + 2 * n:])

        @pl.when(pl.program_id(0) == 0)
        def _():
            s_ref[...] = jnp.zeros_like(s_ref)
            g_start()

        lg = lg_ref[...]
        lb_all = 1.0 / (1.0 + jnp.exp(lg[1:2] - lg[0:1]))
        gain_v = g_ref[...]
        tri = _hg_tri()
        tri_b = tri.astype(BF16)
        row = _iota((C, DK), 0)

        def chunk(ci, carry):
            r0 = pl.multiple_of(ci * C, C)
            rows = pl.ds(r0, C)
            _, qt, _, _, kf_all, b_all, w = _hg_chunk_fwd(z_ref[rows, 0:HG_W], z_ref[rows, HG_W:2 * HG_W], lb_all,
                                                          tri_b)
            qs_all = (qt * jnp.exp(w)).astype(BF16)
            qd_all = (qt * jnp.exp(b_all)).astype(BF16)
            bl_all = b_all[C - 1:C]
            kd_all = (kf_all * jnp.exp(bl_all - b_all)).astype(BF16)
            ebl_all = jnp.exp(bl_all)
            v_all = z_ref[rows, 2 * HG_W:3 * HG_W].astype(BF16)
            g_all = z_ref[rows, 3 * HG_W:4 * HG_W]
            gate_all = g_all * _sigmoid(g_all)
            for h in range(HG_HEADS):
                cs = slice(h * DK, (h + 1) * DK)
                st = s_ref[h]
                st_ref[pl.ds(pl.multiple_of((ci * HG_HEADS + h) * DK, DK), DK), :] = st
                v_b = v_all[:, cs]
                a, _ = _hg_scores(qs_all[:, cs], kf_all[:, cs], b_all[:, cs], row)
                a = jnp.where(tri, a, 0.0)
                o = _dot(qd_all[:, cs], st.astype(BF16), 1, 1) + _dot(a.astype(BF16), v_b, 1, 0)
                s_ref[h] = st * ebl_all[:, cs] + _dot(v_b, kd_all[:, cs], 0, 0)
                o_ref[rows, cs] = o
                rstd = lax.rsqrt(jnp.mean(o * o, axis=-1, keepdims=True) + EPS)
                y_ref[rows, cs] = (o * rstd * gain_v * gate_all[:, cs]).astype(BF16)
            return carry

        lax.fori_loop(0, cpb, chunk, 0, unroll=2)

        @pl.when(pl.program_id(0) == nb - 1)
        def _():
            g_forward()
            g_finish()

    hbm = pl.BlockSpec(memory_space=pl.ANY)
    res = _pallas(
        body, name="hgrn_fwd", grid=(nb,),
        in_specs=[pl.BlockSpec((RB, HG_COLS), lambda i: (i, 0)), pl.BlockSpec((2, HG_W), lambda i: (0, 0)),
                  pl.BlockSpec((1, DK), lambda i: (0, 0))] + [hbm] * n,
        out_specs=[pl.BlockSpec((RB, HG_W), lambda i: (i, 0)), pl.BlockSpec((RB, HG_W), lambda i: (i, 0)),
                   pl.BlockSpec((cpb * HG_HEADS * DK, DK), lambda i: (i, 0))] + [hbm] * n,
        out_shape=[jax.ShapeDtypeStruct((T, HG_W), F32), jax.ShapeDtypeStruct((T, HG_W), BF16),
                   jax.ShapeDtypeStruct((T // C * HG_HEADS * DK, DK), F32)]
        + [jax.ShapeDtypeStruct((N_DEV,) + b.shape, b.dtype) for b in blocks],
        scratch_shapes=[pltpu.VMEM((HG_HEADS, DK, DK), F32)] + _gather_scratch(n),
        compiler_params=_params(dimension_semantics=("arbitrary",)),
    )(z, lb_logits, gain, *blocks)
    return res[0], res[1], res[2], res[3:]


def _hgrn_bwd(z, o_raw, dy, states, lb_logits, gain, dz_buf):
    T = z.shape[0]
    RB = min(ROW_BLOCK, T)
    nb, cpb = T // RB, RB // HG_CHUNK
    C, DK, S = HG_CHUNK, HG_DK, HG_SUB

    def body(z_ref, o_ref, dy_ref, st_ref, lg_ref, g_ref, _buf_ref, dz_ref, dlg_ref, dg_ref, ds_ref, dlb_ref):
        step = pl.program_id(0)

        @pl.when(step == 0)
        def _():
            ds_ref[...] = jnp.zeros_like(ds_ref)
            dlb_ref[...] = jnp.zeros_like(dlb_ref)
            dg_ref[...] = jnp.zeros_like(dg_ref)

        lg = lg_ref[...]
        lb_all = 1.0 / (1.0 + jnp.exp(lg[1:2] - lg[0:1]))
        gain_v = g_ref[...]
        tri = _hg_tri()
        tri_b = tri.astype(BF16)
        row = _iota((C, DK), 0)

        def chunk(cj, carry):
            ci = cpb - 1 - cj
            r0 = pl.multiple_of(ci * C, C)
            rows = pl.ds(r0, C)
            q_all = z_ref[rows, 0:HG_W]
            g_all = z_ref[rows, 3 * HG_W:4 * HG_W]
            sgq_all, qt_all, sg_all, fg_all, kf_all, b_all, w_all = _hg_chunk_fwd(
                q_all, z_ref[rows, HG_W:2 * HG_W], lb_all, tri_b)
            ew_all = jnp.exp(w_all)
            eb_all = jnp.exp(b_all)
            bl_all = b_all[C - 1:C]
            ebl_all = jnp.exp(bl_all)
            ekd_all = jnp.exp(bl_all - b_all)
            qs_all = (qt_all * ew_all).astype(BF16)
            qd_all = (qt_all * eb_all).astype(BF16)
            kd_all = (kf_all * ekd_all).astype(BF16)
            v_all = z_ref[rows, 2 * HG_W:3 * HG_W].astype(BF16)
            sgg_all = _sigmoid(g_all)
            t1_all = dy_ref[rows, :] * (g_all * sgg_all)
            db_heads, dqt_heads, dkf_heads, dv_heads, n_heads = [], [], [], [], []
            for h in range(HG_HEADS):
                cs = slice(h * DK, (h + 1) * DK)
                kf, b, ew, eb, ebl, ekd = kf_all[:, cs], b_all[:, cs], ew_all[:, cs], eb_all[:, cs], ebl_all[:, cs], \
                    ekd_all[:, cs]
                qs_b, qd_b, kd_b, v_b = qs_all[:, cs], qd_all[:, cs], kd_all[:, cs], v_all[:, cs]
                st = st_ref[pl.ds(pl.multiple_of((ci * HG_HEADS + h) * DK, DK), DK), :]
                dst = ds_ref[h]
                o = o_ref[rows, cs]
                rstd = lax.rsqrt(jnp.mean(o * o, axis=-1, keepdims=True) + EPS)
                n = o * rstd
                n_heads.append(n)
                t1 = t1_all[:, cs]
                dg_ref[...] += jnp.sum(t1 * n, axis=0, keepdims=True)
                dn = t1 * gain_v
                do = rstd * (dn - n * jnp.mean(dn * n, axis=-1, keepdims=True))
                do_b = do.astype(BF16)
                a, ks = _hg_scores(qs_b, kf, b, row)
                a = jnp.where(tri, a, 0.0)
                dst_b = dst.astype(BF16)
                dqd = _dot(do_b, st.astype(BF16), 1, 0)
                da = jnp.where(tri, _dot(do_b, v_b, 1, 1), 0.0)
                dv = _dot(a.astype(BF16), do_b, 0, 0) + _dot(kd_b, dst_b, 1, 1)
                dkd = _dot(v_b, dst_b, 1, 0)
                ds_ref[h] = dst * ebl + _dot(do_b, qd_b, 0, 0)
                dkd_kd = dkd * kd_b.astype(F32)
                dbl = ebl * jnp.sum(dst * st, axis=0, keepdims=True) + jnp.sum(dkd_kd, axis=0, keepdims=True)
                da_b = da.astype(BF16)
                dqs_parts = []
                dk_in = jnp.zeros((C, DK), F32)
                db_k = jnp.zeros((C, DK), F32)
                for blk in range(C // S):
                    e, k_b = ks[blk]
                    da_blk = da_b[blk * S:(blk + 1) * S]
                    dqs_parts.append(_dot(da_blk, k_b, 1, 0))
                    dks = _dot(da_blk, qs_b[blk * S:(blk + 1) * S], 0, 0)
                    dk_in = dk_in + dks * e
                    db_k = db_k + dks * k_b.astype(F32)
                dqs = jnp.concatenate(dqs_parts, axis=0)
                db = qs_b.astype(F32) * dqs - db_k + dqd * qd_b.astype(F32) - dkd_kd
                db_heads.append(db + jnp.where(row == C - 1, dbl, 0.0))
                dqt_heads.append(dqs * ew + dqd * eb)
                dkf_heads.append(dk_in + dkd * ekd)
                dv_heads.append(dv)
            dlogf = _split_dot(tri_b, jnp.concatenate(db_heads, axis=1), 0, 0)
            dfg = dlogf / fg_all - jnp.concatenate(dkf_heads, axis=1)
            dlb_ref[...] += jnp.sum(dfg * (1.0 - sg_all), axis=0, keepdims=True)
            dqt = jnp.concatenate(dqt_heads, axis=1)
            n_all = jnp.concatenate(n_heads, axis=1)
            gain_all = jnp.concatenate([gain_v] * HG_HEADS, axis=1)
            dz_ref[rows, 0:HG_W] = (dqt * sgq_all * (1.0 + q_all * (1.0 - sgq_all))).astype(BF16)
            dz_ref[rows, HG_W:2 * HG_W] = (dfg * (1.0 - lb_all) * sg_all * (1.0 - sg_all)).astype(BF16)
            dz_ref[rows, 2 * HG_W:3 * HG_W] = jnp.concatenate(dv_heads, axis=1).astype(BF16)
            dz_ref[rows, 3 * HG_W:4 * HG_W] = (dy_ref[rows, :] * n_all * gain_all * sgg_all
                                               * (1.0 + g_all * (1.0 - sgg_all))).astype(BF16)
            return carry

        lax.fori_loop(0, cpb, chunk, 0, unroll=2)

        @pl.when(step == nb - 1)
        def _():
            d0 = dlb_ref[...] * lb_all * (1.0 - lb_all)
            dlg_ref[0:1, :] = d0
            dlg_ref[1:2, :] = -d0

    rev = lambda i: (nb - 1 - i, 0)
    fix = lambda i: (0, 0)
    return _pallas(
        body, name="hgrn_bwd", grid=(nb,),
        in_specs=[pl.BlockSpec((RB, HG_COLS), rev), pl.BlockSpec((RB, HG_W), rev), pl.BlockSpec((RB, HG_W), rev),
                  pl.BlockSpec((cpb * HG_HEADS * DK, DK), rev), pl.BlockSpec((2, HG_W), fix),
                  pl.BlockSpec((1, DK), fix), pl.BlockSpec(memory_space=pl.ANY)],
        out_specs=[pl.BlockSpec((RB, HG_COLS), rev), pl.BlockSpec((2, HG_W), fix), pl.BlockSpec((1, DK), fix)],
        out_shape=[jax.ShapeDtypeStruct(dz_buf.shape, BF16), jax.ShapeDtypeStruct((2, HG_W), F32),
                   jax.ShapeDtypeStruct((1, DK), F32)],
        scratch_shapes=[pltpu.VMEM((HG_HEADS, DK, DK), F32), pltpu.VMEM((1, HG_W), F32)],
        input_output_aliases={6: 0},
        compiler_params=_params(dimension_semantics=("arbitrary",)),
    )(z, o_raw, dy, states, lb_logits, gain, dz_buf)


def _head_ones():
    r, c = _iota((FOX_W, FOX_W), 0), _iota((FOX_W, FOX_W), 1)
    return ((r // FOX_DH) == (c // FOX_DH)).astype(BF16)


def _log_sigmoid(x):
    return jnp.minimum(x, 0.0) - jnp.log(1.0 + jnp.exp(-jnp.abs(x)))


def _fox_prep(z, bias, qg, kg):
    T = z.shape[0]
    tm = min(ROW_BLOCK, T)
    nb = T // tm

    def body(z_ref, b_ref, qg_ref, kg_ref, q_ref, k_ref, v_ref, qa_ref, ka_ref, carry_ref):
        @pl.when(pl.program_id(0) == 0)
        def _():
            carry_ref[...] = jnp.zeros_like(carry_ref)

        ones = _head_ones()
        normed = []
        for src, g_ref in ((0, qg_ref), (1, kg_ref)):
            xv = z_ref[:, src * FOX_W:(src + 1) * FOX_W]
            ms = _split_dot(ones, xv * xv, 1, 0, mat_first=False) * (1.0 / FOX_DH)
            normed.append(xv * lax.rsqrt(ms + EPS) * g_ref[...])
        qn, kn = normed
        q_ref[...] = (qn * FOX_DH ** -0.5).astype(BF16)
        k_b = kn.astype(BF16)
        k_ref[...] = k_b
        v_ref[...] = z_ref[:, 2 * FOX_W:3 * FOX_W].astype(BF16)
        logf = _log_sigmoid(z_ref[:, 3 * FOX_W:FOX_COLS] + b_ref[...])
        r, c = _iota((tm, tm), 0), _iota((tm, tm), 1)
        tri_b = (c <= r).astype(BF16)
        cum = _split_dot(tri_b, logf, 1, 0, terms=3) + carry_ref[...]
        carry_ref[...] = cum[tm - 1:tm]
        c2 = cum * LOG2E
        hi = c2.astype(BF16)
        rem = c2 - hi.astype(F32)
        mid = rem.astype(BF16)
        lo = (rem - mid.astype(F32)).astype(BF16)
        hrow, col = _iota((LANES, 2 * FOX_W), 0), _iota((LANES, 2 * FOX_W), 1)
        base = hrow * LANES + jnp.where(hrow % 2 == 0, FOX_DH, 0)
        placed = None
        for t, part in enumerate((hi, mid, lo)):
            place = jnp.logical_and(col == base + t, hrow < FOX_HEADS).astype(BF16)
            term = _dot(part, place, 1, 0)
            placed = term if placed is None else placed + term
        colw = _iota((tm, 2 * FOX_W), 1)
        head, lane = colw // LANES, colw % LANES
        own = (lane < FOX_DH) == (head % 2 == 0)
        other = jnp.where(head % 2 == 0, lane - FOX_DH, lane)
        ones_q = jnp.where(jnp.logical_and(other >= 0, other < 3), -1.0, 0.0)
        q2 = (qn * (FOX_DH ** -0.5 * LOG2E)).astype(BF16)
        q_exp = jnp.concatenate([q2[:, (h // 2) * LANES:(h // 2 + 1) * LANES] for h in range(FOX_HEADS)], axis=1)
        k_exp = jnp.concatenate([k_b[:, (h // 2) * LANES:(h // 2 + 1) * LANES] for h in range(FOX_HEADS)], axis=1)
        qa_ref[...] = jnp.where(own, q_exp, ones_q.astype(BF16))
        ka_ref[...] = jnp.where(own, k_exp, placed.astype(BF16))

    wide = pl.BlockSpec((tm, 2 * FOX_W), lambda i: (i, 0))
    return _pallas(
        body, name="fox_prep", grid=(nb,),
        in_specs=[pl.BlockSpec((tm, SEG), lambda i: (i, 1)), pl.BlockSpec((1, LANES), lambda i: (0, 0)),
                  pl.BlockSpec((1, FOX_W), lambda i: (0, 0)), pl.BlockSpec((1, FOX_W), lambda i: (0, 0))],
        out_specs=[pl.BlockSpec((tm, FOX_W), lambda i: (i, 0))] * 3 + [wide] * 2,
        out_shape=[jax.ShapeDtypeStruct((T, FOX_W), BF16)] * 3 + [jax.ShapeDtypeStruct((T, 2 * FOX_W), BF16)] * 2,
        scratch_shapes=[pltpu.VMEM((1, LANES), F32)],
        compiler_params=_params(dimension_semantics=("arbitrary",)),
    )(z, bias, qg, kg)


def _fox_fwd(qa, ka, vb, blocks):
    T = qa.shape[0]
    tq = min(ROW_BLOCK, T)
    nq = T // tq
    NEG = -1e30
    n = len(blocks)
    n_pairs = FOX_HEADS // 2

    n_in = 3

    def body(*refs):
        q_ref, k_ref, v_ref = refs[:n_in]
        o_ref, lse_ref = refs[n_in + n:n_in + n + 2]
        m_sc, l_sc, acc_sc = refs[n_in + 2 * n + 2:n_in + 2 * n + 5]
        pr, qi = pl.program_id(0), pl.program_id(1)
        g_start, g_forward, g_finish = _gather_steps(
            refs[n_in:n_in + n], refs[n_in + n + 2:n_in + 2 * n + 2], *refs[n_in + 2 * n + 5:])

        @pl.when(jnp.logical_and(pr == 0, qi == 0))
        def _():
            g_start()

        @pl.when(jnp.logical_and(pr == n_pairs // 2, qi == 0))
        def _():
            g_forward()

        m_sc[...] = jnp.full_like(m_sc, NEG)
        l_sc[...] = jnp.zeros_like(l_sc)
        acc_sc[...] = jnp.zeros_like(acc_sc)
        lane = _iota((tq, LANES), 1)

        def block(masked, ki):
            keys = pl.ds(pl.multiple_of(ki * tq, tq), tq)
            vv = v_ref[keys, :]
            for hh in range(2):
                hs = slice(hh * LANES, (hh + 1) * LANES)
                s = _dot(q_ref[:, hs], k_ref[keys, hs], 1, 1)
                tiles = [s[:, j * LANES:(j + 1) * LANES] for j in range(tq // LANES)]
                if masked:
                    row, col = _iota((tq, LANES), 0), _iota((tq, LANES), 1)
                    tiles = [jnp.where(row >= col + j * LANES, t, NEG) for j, t in enumerate(tiles)]
                m_old = m_sc[hh]
                top = jnp.broadcast_to(jnp.max(functools.reduce(jnp.maximum, tiles), axis=-1, keepdims=True),
                                       (tq, LANES))
                m_new = jnp.maximum(m_old, top)
                alpha = jnp.exp2(m_old - m_new)
                ps = [jnp.exp2(t - m_new) for t in tiles]
                l_sc[hh] = alpha * l_sc[hh] + functools.reduce(jnp.add, ps)
                m_sc[hh] = m_new
                p_b = jnp.concatenate([p.astype(BF16) for p in ps], axis=1)
                acc_sc[hh] = alpha * acc_sc[hh] + _dot(p_b, vv, 1, 0)

        def before(ki, carry):
            block(False, ki)
            return carry

        lax.fori_loop(0, qi, before, 0)
        block(True, qi)
        l0 = jnp.sum(l_sc[0], axis=-1, keepdims=True)
        l1 = jnp.sum(l_sc[1], axis=-1, keepdims=True)
        o_ref[...] = jnp.where(lane < FOX_DH, acc_sc[0] * (1.0 / l0), acc_sc[1] * (1.0 / l1))
        lse_ref[:, :LANES] = m_sc[0] + jnp.log2(l0)
        lse_ref[:, LANES:] = m_sc[1] + jnp.log2(l1)

        @pl.when(jnp.logical_and(pr == n_pairs - 1, qi == nq - 1))
        def _():
            g_finish()

    qmap = lambda p, i: (i, p)
    whole = lambda p, i: (0, p)
    hbm = pl.BlockSpec(memory_space=pl.ANY)
    res = _pallas(
        body, name="fox_fwd", grid=(n_pairs, nq),
        in_specs=[pl.BlockSpec((tq, 2 * LANES), qmap), pl.BlockSpec((T, 2 * LANES), whole),
                  pl.BlockSpec((T, LANES), whole)] + [hbm] * n,
        out_specs=[pl.BlockSpec((tq, LANES), qmap), pl.BlockSpec((tq, 2 * LANES), qmap)] + [hbm] * n,
        out_shape=[jax.ShapeDtypeStruct((T, FOX_W), F32), jax.ShapeDtypeStruct((T, 2 * FOX_W), F32)]
        + [jax.ShapeDtypeStruct((N_DEV,) + b.shape, b.dtype) for b in blocks],
        scratch_shapes=[pltpu.VMEM((2, tq, LANES), F32)] * 3 + _gather_scratch(n),
        compiler_params=_params(dimension_semantics=("arbitrary",) * 2),
    )(qa, ka, vb, *blocks)
    return res[0], res[1], res[2:]


def _fox_bwd(qs, kn, vb, qa, ka, o, do, lse, hs):
    T = qs.shape[0]
    tq = min(ROW_BLOCK, T)
    nq = T // tq
    n = len(hs)
    n_pairs = FOX_HEADS // 2

    def body(*refs):
        q_ref, k_ref, v_ref, qa_ref, ka_ref, o_ref, do_ref, lse_ref = refs[:8]
        dq_ref, dk_ref, dv_ref, dcs_ref, drs_ref = refs[8 + n:13 + n]
        pr, ki = pl.program_id(0), pl.program_id(1)
        x_start, x_finish = _chip_exchange_steps(refs[8:8 + n], refs[13 + n:13 + 2 * n], *refs[13 + 2 * n:])

        @pl.when(jnp.logical_and(pr == 0, ki == 0))
        def _():
            x_start()
            drs_ref[...] = jnp.zeros_like(drs_ref)

        @pl.when(ki == 0)
        def _():
            dq_ref[...] = jnp.zeros_like(dq_ref)

        dk_ref[...] = jnp.zeros_like(dk_ref)
        dv_ref[...] = jnp.zeros_like(dv_ref)
        dcs_ref[...] = jnp.zeros_like(dcs_ref)

        def block(masked, qi):
            lane = _iota((tq, LANES), 1)
            qrows = pl.ds(pl.multiple_of(qi * tq, tq), tq)
            qv, kv, vv = q_ref[qrows, :], k_ref[...], v_ref[...]
            ov, dov = o_ref[qrows, :], do_ref[qrows, :]
            dq_acc = jnp.zeros((tq, LANES), F32)
            dk_acc = jnp.zeros((tq, LANES), F32)
            dv_acc = jnp.zeros((tq, LANES), F32)
            dcs_acc = jnp.zeros((8, tq), F32)
            drs_acc = jnp.zeros((tq, LANES), F32)
            prod = dov * ov
            nt = tq // LANES
            for hh in range(2):
                in_head = (lane < FOX_DH) if hh == 0 else (lane >= FOX_DH)
                hs_ = slice(hh * LANES, (hh + 1) * LANES)
                zb = jnp.zeros_like(qv)
                qm = jnp.where(in_head, qv, zb)
                km = jnp.where(in_head, kv, zb)
                dom = jnp.where(in_head, dov, 0.0).astype(BF16)
                delta_b = jnp.broadcast_to(jnp.sum(jnp.where(in_head, prod, 0.0), axis=1, keepdims=True), (tq, LANES))
                lse_b = lse_ref[qrows, hs_]
                s = _dot(qa_ref[qrows, hs_], ka_ref[:, hs_], 1, 1)
                dp = _dot(dom, vv, 1, 1)
                p_tiles, ds_tiles, col_tiles = [], [], []
                row_part = jnp.zeros((tq, LANES), F32)
                for j in range(nt):
                    js = slice(j * LANES, (j + 1) * LANES)
                    p = jnp.exp2(s[:, js] - lse_b)
                    if masked:
                        p = jnp.where(_iota((tq, LANES), 0) >= _iota((tq, LANES), 1) + j * LANES, p, 0.0)
                    ds = p * (dp[:, js] - delta_b)
                    p_tiles.append(p.astype(BF16))
                    ds_tiles.append(ds.astype(BF16))
                    col_tiles.append(jnp.sum(ds, axis=0, keepdims=True))
                    row_part = row_part + ds
                p_b = jnp.concatenate(p_tiles, axis=1)
                ds_b = jnp.concatenate(ds_tiles, axis=1)
                dv_acc = dv_acc + _dot(p_b, dom, 0, 0)
                dq_acc = dq_acc + _dot(ds_b, km, 1, 0)
                dk_acc = dk_acc + _dot(ds_b, qm, 0, 0)
                dcs_acc = dcs_acc + jnp.where(_iota((8, tq), 0) == hh, jnp.concatenate(col_tiles, axis=1), 0.0)
                rowsum = jnp.sum(row_part, axis=1, keepdims=True)
                drs_acc = drs_acc + jnp.where(lane == 2 * pr + hh, rowsum, 0.0)
            drs_ref[qrows, :] += drs_acc
            dq_ref[qrows, :] += dq_acc
            dk_ref[...] += dk_acc
            dv_ref[...] += dv_acc
            dcs_ref[0] += dcs_acc

        block(True, ki)

        def after(qi, carry):
            block(False, qi)
            return carry

        lax.fori_loop(ki + 1, nq, after, 0)

        @pl.when(jnp.logical_and(pr == n_pairs - 1, ki == nq - 1))
        def _():
            x_finish()

    whole = lambda p, j: (0, p)
    kmap = lambda p, j: (j, p)
    hbm = pl.BlockSpec(memory_space=pl.ANY)
    res = _pallas(
        body, name="fox_bwd", grid=(n_pairs, nq),
        in_specs=[pl.BlockSpec((T, LANES), whole), pl.BlockSpec((tq, LANES), kmap), pl.BlockSpec((tq, LANES), kmap),
                  pl.BlockSpec((T, 2 * LANES), whole), pl.BlockSpec((tq, 2 * LANES), kmap),
                  pl.BlockSpec((T, LANES), whole), pl.BlockSpec((T, LANES), whole), pl.BlockSpec((T, 2 * LANES), whole)]
        + [hbm] * n,
        out_specs=[pl.BlockSpec((T, LANES), whole), pl.BlockSpec((tq, LANES), kmap),
                   pl.BlockSpec((tq, LANES), kmap), pl.BlockSpec((1, 8, tq), lambda p, j: (p, 0, j)),
                   pl.BlockSpec((T, LANES), lambda p, j: (0, 0))] + [hbm] * n,
        out_shape=[jax.ShapeDtypeStruct((T, FOX_W), F32)] * 3
        + [jax.ShapeDtypeStruct((n_pairs, 8, T), F32), jax.ShapeDtypeStruct((T, LANES), F32)]
        + _chip_exchange_shapes(hs),
        scratch_shapes=_chip_exchange_scratch(n),
        compiler_params=_params(dimension_semantics=("arbitrary",) * 2),
    )(qs, kn, vb, qa, ka, o, do, lse, *hs)
    res = list(res)
    return res[:5] + [res[5:]]


def _fox_post(z, dq, dk, dv, dcs, drs, bias, qg, kg, dz_buf):
    T = z.shape[0]
    tm = min(ROW_BLOCK, T)
    nb = T // tm

    def body(z_ref, dq_ref, dk_ref, dv_ref, dcs_ref, drs_ref, b_ref, qg_ref, kg_ref, _buf_ref, dz_ref, dqg_ref, dkg_ref,
             db_ref, carry_ref):
        @pl.when(pl.program_id(0) == 0)
        def _():
            carry_ref[...] = jnp.zeros_like(carry_ref)
            dqg_ref[...] = jnp.zeros_like(dqg_ref)
            dkg_ref[...] = jnp.zeros_like(dkg_ref)
            db_ref[...] = jnp.zeros_like(db_ref)

        ones = _head_ones()
        for src, g_ref, d_ref, dg_ref, scale in ((0, qg_ref, dq_ref, dqg_ref, FOX_DH ** -0.5), (1, kg_ref, dk_ref, dkg_ref, 1.0)):
            xv = z_ref[:, src * FOX_W:(src + 1) * FOX_W]
            ms = _split_dot(ones, xv * xv, 1, 0, mat_first=False) * (1.0 / FOX_DH)
            rstd = lax.rsqrt(ms + EPS)
            xh = xv * rstd
            dn = d_ref[...] * scale
            dg_ref[...] += jnp.sum(dn * xh, axis=0, keepdims=True)
            dxh = dn * g_ref[...]
            mean = _split_dot(ones, dxh * xh, 1, 0, mat_first=False) * (1.0 / FOX_DH)
            dz_ref[:, src * FOX_W:(src + 1) * FOX_W] = (rstd * (dxh - xh * mean)).astype(BF16)
        dz_ref[:, 2 * FOX_W:3 * FOX_W] = dv_ref[...].astype(BF16)
        row8 = _iota((8, tm), 0)
        dct = jnp.zeros((8, tm), F32)
        for h in range(FOX_HEADS):
            src_row = dcs_ref[h // 2][h % 2:h % 2 + 1, :]
            dct = dct + jnp.where(row8 == h, src_row, 0.0)
        dct = drs_ref[...].T[0:8] - dct
        r, c = _iota((tm, tm), 0), _iota((tm, tm), 1)
        upper_b = (r >= c).astype(BF16)
        rc = _split_dot(upper_b, dct, 1, 0, mat_first=False) + carry_ref[...]
        carry_ref[...] = rc[:, 0:1]
        full = jnp.concatenate([rc, jnp.zeros((LANES - 8, tm), F32)], axis=0)
        dlogf = full.T
        xf = z_ref[:, 3 * FOX_W:FOX_COLS] + b_ref[...]
        df = dlogf * (1.0 - _sigmoid(xf))
        dz_ref[:, 3 * FOX_W:FOX_COLS] = df.astype(BF16)
        dz_ref[:, FOX_COLS:] = jnp.zeros((tm, SEG - FOX_COLS), BF16)
        db_ref[...] += jnp.sum(df, axis=0, keepdims=True)

    rev = lambda i: (nb - 1 - i, 0)
    fix2 = lambda i: (0, 0)
    return _pallas(
        body, name="fox_post", grid=(nb,),
        in_specs=[pl.BlockSpec((tm, SEG), lambda i: (nb - 1 - i, 1)), pl.BlockSpec((tm, FOX_W), rev),
                  pl.BlockSpec((tm, FOX_W), rev),
                  pl.BlockSpec((tm, FOX_W), rev), pl.BlockSpec((FOX_HEADS // 2, 8, tm), lambda i: (0, 0, nb - 1 - i)),
                  pl.BlockSpec((tm, LANES), rev),
                  pl.BlockSpec((1, LANES), fix2), pl.BlockSpec((1, FOX_W), fix2), pl.BlockSpec((1, FOX_W), fix2),
                  pl.BlockSpec(memory_space=pl.ANY)],
        out_specs=[pl.BlockSpec((tm, SEG), lambda i: (nb - 1 - i, 1)), pl.BlockSpec((1, FOX_W), fix2),
                   pl.BlockSpec((1, FOX_W), fix2), pl.BlockSpec((1, LANES), fix2)],
        out_shape=[jax.ShapeDtypeStruct(dz_buf.shape, BF16), jax.ShapeDtypeStruct((1, FOX_W), F32),
                   jax.ShapeDtypeStruct((1, FOX_W), F32), jax.ShapeDtypeStruct((1, LANES), F32)],
        scratch_shapes=[pltpu.VMEM((8, 1), F32)],
        input_output_aliases={9: 0},
        compiler_params=_params(dimension_semantics=("arbitrary",)),
    )(z, dq, dk, dv, dcs, drs, bias, qg, kg, dz_buf)


def _local_step(x, p, tgt, sm, chunks, core):
    lbl, og, fb = sm["hg_lb_logits"], sm["hg_onorm_g"], sm["fox_f_bias"]
    fbias = jnp.pad(fb, ((0, 0), (0, LANES - FOX_HEADS)))
    qg = jnp.tile(sm["fox_q_norm_g"], (1, FOX_HEADS))
    kg = jnp.tile(sm["fox_k_norm_g"], (1, FOX_HEADS))

    assert BIG[0] == "w_in"
    h, got0 = _rms_fwd(x, sm["norm_mix_g"], chunks[:1], name="rms_mix")
    W = {"w_in": _full_of_chunks("w_in", got0[0])}
    rest = dict(zip(BIG[1:], chunks[1:]))
    first, second = ["w_ffn_gate"], ["w_ffn_up"]
    third = [n for n in BIG[1:] if n not in first + second]
    z, got1 = _matmul(h, W["w_in"], tb=True, gather=[rest[n] for n in first], name="mm_z")
    o_raw, ya, states, got2 = _hgrn_fwd(z, lbl, og, [rest[n] for n in second])
    qs, kn, vb, qa, ka = _fox_prep(z, fbias, qg, kg)
    yb, lse, got3 = _fox_fwd(qa, ka, vb, [rest[n] for n in third])
    W = dict(W, **{n: _full_of_chunks(n, g)
                   for n, g in zip(first + second + third, list(got1) + list(got2) + list(got3))})
    merged, ua, ub = _merge_fwd(ya, yb, W["w_branch_a"], W["w_branch_b"], z)
    x1, hf = _matmul(merged, W["w_out"], add=x, norm_fwd=sm["norm_ffn_g"], name="mm_x1")
    a, b, act = _swiglu_fwd(hf, W["w_ffn_gate"], W["w_ffn_up"])
    x2 = _matmul(act, W["w_ffn_down"], add=x1, name="mm_x2")
    hp, dy, dsp, dpp, loss = _ple_loss(x2, p, sm["norm_ple_g"], W["w_ple_gate"], W["w_ple_proj"], tgt)

    G = {}
    G["w_ple_proj"] = _matmul(p, dpp, ta=True, out_dtype=BF16, name="mm_dw_ple_proj")
    G["w_ple_gate"] = _matmul(hp, dsp, ta=True, out_dtype=BF16, name="mm_dw_ple_gate")
    dx2, d_ple_g = _matmul(dsp, W["w_ple_gate"], tb=True, norm_bwd=(x2, sm["norm_ple_g"], dy), name="mm_dx2")
    G["w_ffn_down"] = _matmul(act, dx2, ta=True, out_dtype=BF16, name="mm_dw_ffn_down")
    da, db = _swiglu_bwd(dx2, W["w_ffn_down"], a, b)
    G["w_ffn_gate"] = _matmul(da, hf, ta=True, out_dtype=BF16, name="mm_dw_ffn_gate")
    G["w_ffn_up"] = _matmul(db, hf, ta=True, out_dtype=BF16, name="mm_dw_ffn_up")
    dhf = _matmul(da, W["w_ffn_gate"], name="mm_dhf_a")
    dx1, d_ffn_g = _matmul(db, W["w_ffn_up"], add=dhf, norm_bwd=(x1, sm["norm_ffn_g"], dx2), name="mm_dx1")
    G["w_out"] = _matmul(merged, dx1, ta=True, out_dtype=BF16, name="mm_dw_out")
    dua, dub, dz, dya, dyb = _merge_bwd(dx1, W["w_out"], W["w_branch_a"], W["w_branch_b"], ua, ub, z)
    G["w_branch_a"] = _matmul(ya, dua, ta=True, out_dtype=BF16, name="mm_dw_branch_a")
    G["w_branch_b"] = _matmul(yb, dub, ta=True, out_dtype=BF16, name="mm_dw_branch_b")
    hb_rest = _sibling_sums({n: G[n] for n in BIG[1:]}, core, tag="rest")
    dq, dk, dv, dcs, drs, got_rest = _fox_bwd(qs, kn, vb, qa, ka, yb, dyb, lse, hb_rest)
    dz, d_qg, d_kg, d_fb = _fox_post(z, dq, dk, dv, dcs, drs, fbias, qg, kg, dz)
    dz, d_lbl, d_og = _hgrn_bwd(z, o_raw, dya, states, lbl, og, dz)
    G["w_in"] = _matmul(dz, h, ta=True, out_dtype=BF16, name="mm_dw_in")
    hb_in = _sibling_sums({"w_in": G["w_in"]}, core, tag="w_in")
    (grad_x, d_mix_g), got_in = _matmul(dz, W["w_in"], exchange=hb_in,
                                        norm_bwd=(x, sm["norm_mix_g"], dx1), name="mm_dx")

    gs = {"norm_mix_g": d_mix_g, "hg_lb_logits": d_lbl, "hg_onorm_g": d_og, "fox_f_bias": d_fb[:, :FOX_HEADS],
          "fox_q_norm_g": d_qg.reshape(FOX_HEADS, FOX_DH).sum(0, keepdims=True),
          "fox_k_norm_g": d_kg.reshape(FOX_HEADS, FOX_DH).sum(0, keepdims=True),
          "norm_ffn_g": d_ffn_g, "norm_ple_g": d_ple_g}
    return loss, grad_x, gs, hb_in + hb_rest, list(got_in) + list(got_rest)


def _pack_rows(parts, total):
    buf = jnp.concatenate(parts, axis=-2)
    pad = total - buf.shape[-2]
    widths = [(0, 0)] * (buf.ndim - 2) + [(0, pad), (0, 0)]
    return jnp.pad(buf, widths)


def _chunk_of_shard(n, w):
    if n == "w_in":
        return jnp.pad(w, ((0, IN_SHARD_PAD - IN_SHARD), (0, 0)))
    if n in ("w_ffn_gate", "w_ffn_up", "w_ffn_down"):
        return jnp.pad(w, ((0, FF_SHARD_PAD - FF_SHARD), (0, 0)))
    return w


def _full_of_chunks(n, g):
    _, a, b = g.shape
    if n == "w_in":
        w = g[:, :IN_SHARD].reshape(IN_COLS, b)
        gap = jnp.zeros((SEG - FOX_LOGICAL, b), g.dtype)
        return jnp.concatenate([w[:HG_COLS + FOX_LOGICAL], gap, w[HG_COLS + FOX_LOGICAL:]], axis=0)
    if BIG_SHAPE[n][2] == 0:
        return g.reshape(N_DEV * a, b)
    return g.transpose(1, 0, 2).reshape(a, N_DEV * b)


def _chunks_of_full(n, g):
    if n == "w_in":
        w = jnp.concatenate([g[:HG_COLS + FOX_LOGICAL], g[2 * SEG:]], axis=0).reshape(N_DEV, IN_SHARD, g.shape[1])
        return jnp.pad(w, ((0, 0), (0, IN_SHARD_PAD - IN_SHARD), (0, 0)))
    if BIG_SHAPE[n][2] == 0:
        return g.reshape(N_DEV, g.shape[0] // N_DEV, g.shape[1])
    return g.reshape(g.shape[0], N_DEV, g.shape[1] // N_DEV).transpose(1, 0, 2)


def _pack_small(vals, loss_row=None):
    parts = [vals[n].reshape(SMALL_ROWS[n], -1) for n in SMALL]
    parts = [jnp.pad(v, ((0, 0), (0, LANES - v.shape[1]))) for v in parts]
    if loss_row is not None:
        parts.append(loss_row)
    return _pack_rows(parts, SMALL_TOTAL)


def _unpack_small(buf, like):
    out, r0 = {}, 0
    for n in SMALL:
        rows, size = SMALL_ROWS[n], like[n].size
        blk = buf[r0:r0 + rows]
        out[n] = (blk if size == rows * LANES else blk[:, :size]).reshape(like[n].shape)
        r0 += rows
    return out


def _place():
    return lax.axis_index("x"), lax.axis_index("y"), lax.axis_index("c")


def _gather_steps(x_refs, out_refs, send_sems, recv_sems, local_sems):
    n = len(x_refs)
    x, y, c = _place()
    me, sibling = (x, y, c), (x, y, 1 - c)
    chips = [(1 - x, y), (x, 1 - y), (1 - x, 1 - y)]

    def slot(i, px, py, pc):
        return out_refs[i].at[4 * px + 2 * py + pc]

    def copy(k, i, blk, to, own=False):
        return pltpu.make_async_remote_copy(
            src_ref=x_refs[i] if own else slot(i, *blk), dst_ref=slot(i, *blk),
            send_sem=send_sems.at[k, i], recv_sem=recv_sems.at[k, i], device_id=to, device_id_type=MESH)

    def mine():
        return [pltpu.make_async_copy(x_refs[i], slot(i, *me), local_sems.at[i]) for i in range(n)]

    def first():
        cps = [copy(0, i, me, sibling, own=True) for i in range(n)]
        return cps + [copy(1 + j, i, me, (*chip, c), own=True) for j, chip in enumerate(chips) for i in range(n)]

    def passed():
        return [copy(4 + j, i, (*chip, c), sibling) for j, chip in enumerate(chips) for i in range(n)]

    def start():
        for cp in mine() + first():
            cp.start()

    def forward():
        fws = passed()
        for j, chip in enumerate(chips):
            for i in range(n):
                copy(1 + j, i, (*chip, c), me).wait_recv()
                fws[j * n + i].start()

    def finish():
        for i in range(n):
            copy(0, i, sibling, me).wait_recv()
        for j, chip in enumerate(chips):
            for i in range(n):
                copy(4 + j, i, (*chip, 1 - c), me).wait_recv()
        for cp in first() + passed():
            cp.wait_send()
        for cp in mine():
            cp.wait()

    return start, forward, finish


def _gather_scratch(n):
    return [pltpu.SemaphoreType.DMA((7, n)), pltpu.SemaphoreType.DMA((7, n)), pltpu.SemaphoreType.DMA((n,))]


def _sibling_exchange(gs, *, name):
    n = len(gs)

    def body(*refs):
        g_refs, out_refs = refs[:n], refs[n:2 * n]
        send_sems, recv_sems = refs[2 * n:]
        x, y, c = _place()
        cps = [pltpu.make_async_remote_copy(
            src_ref=g_refs[i].at[:, pl.ds(1 - c, 1)], dst_ref=out_refs[i], send_sem=send_sems.at[i],
            recv_sem=recv_sems.at[i], device_id=(x, y, 1 - c), device_id_type=MESH) for i in range(n)]
        for cp in cps:
            cp.start()
        for cp in cps:
            cp.wait()

    hbm = pl.BlockSpec(memory_space=pl.ANY)
    return _pallas(
        body, name=name, out_shape=[jax.ShapeDtypeStruct((N_CHIP, 1) + g.shape[2:], g.dtype) for g in gs],
        in_specs=[hbm] * n, out_specs=[hbm] * n,
        scratch_shapes=[pltpu.SemaphoreType.DMA((n,)), pltpu.SemaphoreType.DMA((n,))],
    )(*gs)


def _chip_sum(g4s, gots, core, *, name):
    n = len(g4s)

    def body(c_ref, *refs):
        for g_ref, r_ref, h_ref in zip(refs[:n], refs[n:2 * n], refs[2 * n:]):
            h_ref[0] = (g_ref[0, 0].astype(F32) + r_ref[0, 0].astype(F32)).astype(BF16)

    shapes = [g.shape[2:] for g in g4s]
    grid_spec = pltpu.PrefetchScalarGridSpec(
        num_scalar_prefetch=1, grid=(N_CHIP,),
        in_specs=[pl.BlockSpec((1, 1) + s, lambda j, c: (j, c[0], 0, 0)) for s in shapes]
        + [pl.BlockSpec((1, 1) + s, lambda j, c: (j, 0, 0, 0)) for s in shapes],
        out_specs=[pl.BlockSpec((1,) + s, lambda j, c: (j, 0, 0)) for s in shapes])
    return list(_pallas(
        body, name=name, grid_spec=grid_spec, out_shape=[jax.ShapeDtypeStruct((N_CHIP,) + s, BF16) for s in shapes],
        compiler_params=_params(dimension_semantics=("arbitrary",)),
    )(core, *g4s, *gots))


def _chip_exchange_steps(h_refs, out_refs, send_sems, recv_sems):
    n = len(h_refs)
    x, y, c = _place()
    chips = [(1 - x, y), (x, 1 - y), (1 - x, 1 - y)]

    def copies():
        return [pltpu.make_async_remote_copy(
            src_ref=h_refs[i].at[2 * px + py], dst_ref=out_refs[i].at[k], send_sem=send_sems.at[k, i],
            recv_sem=recv_sems.at[k, i], device_id=(px, py, c), device_id_type=MESH)
            for k, (px, py) in enumerate(chips) for i in range(n)]

    def start():
        for cp in copies():
            cp.start()

    def finish():
        for cp in copies():
            cp.wait()

    return start, finish


def _chip_exchange_shapes(hs):
    return [jax.ShapeDtypeStruct((3,) + h.shape[1:], h.dtype) for h in hs]


def _chip_exchange_scratch(n):
    return [pltpu.SemaphoreType.DMA((3, n)), pltpu.SemaphoreType.DMA((3, n))]


def _sibling_sums(G, core, *, tag):
    g4 = []
    for n, g in G.items():
        gc = _chunks_of_full(n, g)
        g4.append(gc.reshape((N_CHIP, 2) + gc.shape[1:]))
    got = _sibling_exchange(g4, name="grads_to_sibling_" + tag)
    return _chip_sum(g4, got, core, name="chip_sum_" + tag)


def _adam_math(w, g, m, v):
    m = ADAM_B1 * m + (1.0 - ADAM_B1) * g
    v = ADAM_B2 * v + (1.0 - ADAM_B2) * (g * g)
    m_hat = m / (1.0 - ADAM_B1 ** ADAM_STEP)
    v_hat = v / (1.0 - ADAM_B2 ** ADAM_STEP)
    delta = -ADAM_LR * (m_hat / (jnp.sqrt(v_hat) + ADAM_EPS) + ADAM_WD * w)
    return delta, m, v


def _adam_shard(hb, got, chip, w, m, v, *, name):
    _, r, c = w.shape
    _, a, cb = hb.shape
    assert cb == c and c % LANES == 0, (hb.shape, w.shape)
    tc = _pick(c, 2 * LANES)

    def body(j_ref, h_ref, r_ref, w_ref, m_ref, v_ref, g_ref, d_ref, nm_ref, nv_ref):
        parts = [h_ref[0], r_ref[0], r_ref[1], r_ref[2]]
        g = None
        for part in parts:
            part = part[:r].astype(F32)
            g = part if g is None else g + part
        d, nm, nv = _adam_math(w_ref[0], g, m_ref[0], v_ref[0])
        g_ref[0] = g
        d_ref[0] = d
        nm_ref[0] = nm
        nv_ref[0] = nv

    blk = pl.BlockSpec((1, r, tc), lambda i, j: (0, 0, i))
    grid_spec = pltpu.PrefetchScalarGridSpec(
        num_scalar_prefetch=1, grid=(c // tc,),
        in_specs=[pl.BlockSpec((1, a, tc), lambda i, j: (j[0], 0, i)),
                  pl.BlockSpec((3, a, tc), lambda i, j: (0, 0, i)), blk, blk, blk],
        out_specs=[blk] * 4)
    return _pallas(
        body, name=name, grid_spec=grid_spec, out_shape=[jax.ShapeDtypeStruct((1, r, c), F32)] * 4,
        compiler_params=_params(dimension_semantics=("arbitrary",)),
    )(chip, hb, got, w, m, v)


def _small_all_reduce_adam(gs, w, m, v):
    def body(g_ref, w_ref, m_ref, v_ref, sum_ref, d_ref, nm_ref, nv_ref, gather, send_sems, recv_sems):
        x, y, c = _place()
        my = 4 * x + 2 * y + c
        gather[my] = g_ref[...]
        cps = []
        for k in range(1, N_DEV):
            to = (x ^ (k >> 2), y ^ ((k >> 1) & 1), c ^ (k & 1))
            cps.append(pltpu.make_async_remote_copy(
                src_ref=g_ref, dst_ref=gather.at[my], send_sem=send_sems.at[k - 1], recv_sem=recv_sems.at[k - 1],
                device_id=to, device_id_type=MESH))
        for cp in cps:
            cp.start()
        for cp in cps:
            cp.wait()
        total = gather[0]
        for d in range(1, N_DEV):
            total = total + gather[d]
        dlt, nm, nv = _adam_math(w_ref[...], total, m_ref[...], v_ref[...])
        sum_ref[...] = total
        d_ref[...] = dlt
        nm_ref[...] = nm
        nv_ref[...] = nv

    vm = pl.BlockSpec(memory_space=pltpu.VMEM)
    return _pallas(
        body, name="small_all_reduce_adam", out_shape=[jax.ShapeDtypeStruct((SMALL_TOTAL, LANES), F32)] * 4,
        in_specs=[vm] * 4, out_specs=[vm] * 4,
        scratch_shapes=[pltpu.VMEM((N_DEV, SMALL_TOTAL, LANES), F32), pltpu.SemaphoreType.DMA((7,)),
                        pltpu.SemaphoreType.DMA((7,))],
            )(gs, w, m, v)


def kernel(x, p, norm_mix_g, w_in, hg_lb_logits, hg_onorm_g, fox_f_bias, fox_q_norm_g, fox_k_norm_g, w_branch_a, w_branch_b, w_out, norm_ffn_g, w_ffn_gate, w_ffn_up, w_ffn_down, norm_ple_g, w_ple_gate, w_ple_proj, loss_target, m_norm_mix_g, m_w_in, m_hg_lb_logits, m_hg_onorm_g, m_fox_f_bias, m_fox_q_norm_g, m_fox_k_norm_g, m_w_branch_a, m_w_branch_b, m_w_out, m_norm_ffn_g, m_w_ffn_gate, m_w_ffn_up, m_w_ffn_down, m_norm_ple_g, m_w_ple_gate, m_w_ple_proj, v_norm_mix_g, v_w_in, v_hg_lb_logits, v_hg_onorm_g, v_fox_f_bias, v_fox_q_norm_g, v_fox_k_norm_g, v_w_branch_a, v_w_branch_b, v_w_out, v_norm_ffn_g, v_w_ffn_gate, v_w_ffn_up, v_w_ffn_down, v_norm_ple_g, v_w_ple_gate, v_w_ple_proj):
    args = dict(locals())
    wts = {n: args[n] for n in BIG + SMALL}
    mom = {n: args["m_" + n] for n in BIG + SMALL}
    var = {n: args["v_" + n] for n in BIG + SMALL}
    for group in (wts, mom, var):
        for n in TRANSPOSED:
            group[n] = jnp.swapaxes(group[n], 1, 2)
    sm = {n: wts[n] for n in SMALL}

    xi, yi, ci = _place()
    core = jnp.reshape(ci, (1,)).astype(jnp.int32)
    chip = jnp.reshape(2 * xi + yi, (1,)).astype(jnp.int32)
    chunks = [_chunk_of_shard(n, wts[n][0].astype(BF16)) for n in BIG]
    loss_blk, grad_x, gs, hb, got = _local_step(x[0], p[0, 0], loss_target[0], sm, chunks, core)

    g_big, d_big, nm_big, nv_big = {}, {}, {}, {}
    for n, h, r in zip(BIG, hb, got):
        res = _adam_shard(h, r, chip, wts[n], mom[n], var[n], name="adam_" + n)
        if n in TRANSPOSED:
            res = [jnp.swapaxes(t, 1, 2) for t in res]
        g_big[n], d_big[n], nm_big[n], nv_big[n] = res

    s_sum, s_d, s_nm, s_nv = _small_all_reduce_adam(
        _pack_small(gs, loss_blk[0:1]), _pack_small(sm), _pack_small({n: mom[n] for n in SMALL}),
        _pack_small({n: var[n] for n in SMALL}))
    loss = s_sum[LOSS_ROW, 0]
    g_small, d_small, nm_small, nv_small = (_unpack_small(t, sm) for t in (s_sum, s_d, s_nm, s_nv))

    order = ["norm_mix_g", "w_in", "hg_lb_logits", "hg_onorm_g", "fox_f_bias", "fox_q_norm_g", "fox_k_norm_g",
             "w_branch_a", "w_branch_b", "w_out", "norm_ffn_g", "w_ffn_gate", "w_ffn_up", "w_ffn_down", "norm_ple_g",
             "w_ple_gate", "w_ple_proj"]
    outs = [loss, grad_x[None]]
    for big, small in ((g_big, g_small), (d_big, d_small), (nm_big, nm_small), (nv_big, nv_small)):
        outs += [big[n] if n in big else small[n] for n in order]
    return tuple(outs)
```

```python
import functools

import jax
import jax.numpy as jnp
from jax import lax
from jax.experimental import pallas as pl
from jax.experimental.pallas import tpu as pltpu

F32 = jnp.float32
BF16 = jnp.bfloat16

D_MODEL = 1024
PLE_DIM = 256
HG_HEADS = 4
HG_DK = 128
HG_CHUNK = 64
HG_SUB = 16
HG_W = HG_HEADS * HG_DK
FOX_HEADS = 8
FOX_DH = 64
FOX_W = FOX_HEADS * FOX_DH
D_FF = 2816
EPS = 1e-6
N_DEV = 8
N_CHIP = 4
LANES = 128
FOX_COLS = 3 * FOX_W + LANES
HG_COLS = 4 * HG_W
GATE_COLS = 2 * D_MODEL
IN_COLS = HG_COLS + 3 * FOX_W + FOX_HEADS + GATE_COLS
FOX_LOGICAL = 3 * FOX_W + FOX_HEADS
SEG = 2048
IN_PAD = 3 * SEG
IN_SHARD = IN_COLS // N_DEV
IN_SHARD_PAD = 720
EXP_CLAMP = 80.0
LOG2E = 1.4426950408889634

ADAM_LR = 0.001
ADAM_B1 = 0.9
ADAM_B2 = 0.999
ADAM_EPS = 1e-08
ADAM_WD = 0.01
ADAM_STEP = 10

MESH = pl.DeviceIdType.MESH
VMEM_LIMIT = 56 * 1024 * 1024
ROW_BLOCK = 512
MATMUL_TN = 2048
MATMUL_TK = 3072
MATMUL_TM_T = 1408
FFN_TILE = 1408

BIG = ["w_in", "w_branch_a", "w_branch_b", "w_out", "w_ffn_gate", "w_ffn_up", "w_ffn_down",
       "w_ple_gate", "w_ple_proj"]
TRANSPOSED = ("w_in", "w_ffn_gate", "w_ffn_up")
BIG_SHAPE = {
    "w_in": (IN_COLS, D_MODEL, 0), "w_branch_a": (HG_W, D_MODEL, 1), "w_branch_b": (FOX_W, D_MODEL, 1),
    "w_out": (D_MODEL, D_MODEL, 0), "w_ffn_gate": (D_FF, D_MODEL, 0), "w_ffn_up": (D_FF, D_MODEL, 0),
    "w_ffn_down": (D_FF, D_MODEL, 0), "w_ple_gate": (D_MODEL, D_MODEL, 0), "w_ple_proj": (PLE_DIM, D_MODEL, 1),
}

SMALL = ["norm_mix_g", "hg_lb_logits", "hg_onorm_g", "fox_f_bias", "fox_q_norm_g", "fox_k_norm_g",
         "norm_ffn_g", "norm_ple_g"]
SMALL_ROWS = {"norm_mix_g": 8, "hg_lb_logits": 8, "hg_onorm_g": 1, "fox_f_bias": 1, "fox_q_norm_g": 1,
              "fox_k_norm_g": 1, "norm_ffn_g": 8, "norm_ple_g": 8}
SMALL_TOTAL = 40
LOSS_ROW = 36


def _pallas(body, **kw):
    return pl.pallas_call(body, **kw)


def _params(**kw):
    return pltpu.CompilerParams(vmem_limit_bytes=VMEM_LIMIT, **kw)


def _pick(n, target):
    if n <= target:
        return n
    best = None
    for t in range(LANES, target + 1, LANES):
        if n % t == 0:
            best = t
    assert best is not None, (n, target)
    return best


def _dot(a, b, ca, cb):
    return lax.dot_general(a, b, (((ca,), (cb,)), ((), ())), preferred_element_type=F32)


def _split_dot(mat, x, ca, cb, terms=2, mat_first=True):
    acc = None
    rem = x
    for _ in range(terms):
        part = rem.astype(BF16)
        rem = rem - part.astype(F32)
        p = _dot(mat, part, ca, cb) if mat_first else _dot(part, mat, ca, cb)
        acc = p if acc is None else acc + p
    return acc


def _sigmoid(x):
    return 1.0 / (1.0 + jnp.exp(-x))


def _iota(shape, dim):
    return lax.broadcasted_iota(jnp.int32, shape, dim)


def _matmul(a, b, *, name, ta=False, tb=False, out_dtype=F32, add=None, exchange=None, gather=None,
            norm_fwd=None, norm_bwd=None):
    assert exchange is None or gather is None
    (K, M) = a.shape if ta else a.shape[::-1]
    (N, Kb) = b.shape if tb else b.shape[::-1]
    assert K == Kb, (a.shape, b.shape, ta, tb)
    if ta:
        tm, tn, tk = _pick(M, MATMUL_TM_T), _pick(N, 2 * ROW_BLOCK), _pick(K, 4 * ROW_BLOCK)
    else:
        tm, tn, tk = _pick(M, 2 * ROW_BLOCK), _pick(N, MATMUL_TN), _pick(K, MATMUL_TK)
    if norm_bwd is not None:
        tm = _pick(M, ROW_BLOCK)
    nk = K // tk
    use_scratch = nk > 1 and out_dtype != F32
    if norm_fwd is not None or norm_bwd is not None:
        assert tn == N and not use_scratch and out_dtype == F32

    hs = list(exchange or gather or [])
    n_x = len(hs)
    grid = (M // tm, N // tn, nk)
    a_spec = pl.BlockSpec((tk, tm), lambda i, j, k: (k, i)) if ta else pl.BlockSpec((tm, tk), lambda i, j, k: (i, k))
    b_spec = pl.BlockSpec((tn, tk), lambda i, j, k: (j, k)) if tb else pl.BlockSpec((tk, tn), lambda i, j, k: (k, j))
    o_spec = pl.BlockSpec((tm, tn), lambda i, j, k: (i, j))
    row_vec = pl.BlockSpec((1, N), lambda i, j, k: (0, 0))
    hbm = pl.BlockSpec(memory_space=pl.ANY)
    extra_in = [(add, o_spec)] if add is not None else []
    extra_out = []
    if norm_fwd is not None:
        extra_in += [(norm_fwd, row_vec)]
        extra_out += [(jax.ShapeDtypeStruct((M, N), BF16), o_spec)]
    if norm_bwd is not None:
        extra_in += [(norm_bwd[0], o_spec), (norm_bwd[1], row_vec), (norm_bwd[2], o_spec)]
        extra_out += [(jax.ShapeDtypeStruct((1, N), F32), row_vec)]
    if gather is not None:
        ride_shapes, ride_scratch = [jax.ShapeDtypeStruct((N_DEV,) + h.shape, h.dtype) for h in hs], _gather_scratch(n_x)
    else:
        ride_shapes, ride_scratch = _chip_exchange_shapes(hs), (_chip_exchange_scratch(n_x) if n_x else [])
    n_ex_in, n_ex_out = len(extra_in), len(extra_out)

    def body(*refs):
        refs = list(refs)
        a_ref, b_ref = refs[:2]
        ex_in = refs[2:2 + n_ex_in]
        ride_in = refs[2 + n_ex_in:2 + n_ex_in + n_x]
        base = 2 + n_ex_in + n_x
        o_ref = refs[base]
        ex_out = refs[base + 1:base + 1 + n_ex_out]
        ride_out = refs[base + 1 + n_ex_out:base + 1 + n_ex_out + n_x]
        scratch = refs[base + 1 + n_ex_out + n_x:]
        at = [pl.program_id(d) for d in range(3)]
        k = at[2]
        if n_x:
            steps = _gather_steps if gather is not None else _chip_exchange_steps
            ride = steps(ride_in, ride_out, *scratch[-len(ride_scratch):])

            @pl.when(jnp.logical_and(at[0] == 0, jnp.logical_and(at[1] == 0, at[2] == 0)))
            def _():
                ride[0]()
        p = _dot(a_ref[...].astype(BF16), b_ref[...].astype(BF16), 0 if ta else 1, 1 if tb else 0)

        def finish(r):
            ins = list(ex_in)
            outs = list(ex_out)
            if add is not None:
                r = r + ins.pop(0)[...].astype(F32)
            if norm_fwd is not None:
                g_ref = ins.pop(0)
                rstd = lax.rsqrt(jnp.mean(r * r, axis=-1, keepdims=True) + EPS)
                outs.pop(0)[...] = (r * rstd * g_ref[...]).astype(BF16)
            if norm_bwd is not None:
                x_ref, g_ref, dres_ref = ins.pop(0), ins.pop(0), ins.pop(0)
                dg_ref = outs.pop(0)
                xv = x_ref[...]
                rstd = lax.rsqrt(jnp.mean(xv * xv, axis=-1, keepdims=True) + EPS)
                xh = xv * rstd
                part = jnp.sum(r * xh, axis=0, keepdims=True)

                @pl.when(at[0] == 0)
                def _():
                    dg_ref[...] = part

                @pl.when(at[0] > 0)
                def _():
                    dg_ref[...] += part

                dxh = r * g_ref[...]
                r = rstd * (dxh - xh * jnp.mean(dxh * xh, axis=-1, keepdims=True)) + dres_ref[...]
            o_ref[...] = r.astype(out_dtype)

        if nk == 1:
            finish(p)
        elif not use_scratch:
            @pl.when(k == 0)
            def _():
                o_ref[...] = p

            @pl.when(jnp.logical_and(k > 0, k < nk - 1))
            def _():
                o_ref[...] += p

            @pl.when(k == nk - 1)
            def _():
                finish(o_ref[...] + p)
        else:
            acc_ref = scratch[0]

            @pl.when(k == 0)
            def _():
                acc_ref[...] = p

            @pl.when(k > 0)
            def _():
                acc_ref[...] += p

            @pl.when(k == nk - 1)
            def _():
                finish(acc_ref[...])

        if n_x:
            @pl.when(jnp.logical_and(at[0] == grid[0] - 1, jnp.logical_and(at[1] == grid[1] - 1, at[2] == nk - 1)))
            def _():
                for step in ride[1:]:
                    step()

    res = _pallas(
        body, name=name, grid=grid,
        in_specs=[a_spec, b_spec] + [s for _, s in extra_in] + [hbm] * n_x,
        out_specs=[o_spec] + [s for _, s in extra_out] + [hbm] * n_x,
        out_shape=[jax.ShapeDtypeStruct((M, N), out_dtype)] + [s for s, _ in extra_out] + ride_shapes,
        scratch_shapes=([pltpu.VMEM((tm, tn), F32)] if use_scratch else []) + ride_scratch,
        compiler_params=_params(dimension_semantics=("arbitrary",) * 3),
    )(a, b, *[v for v, _ in extra_in], *hs)
    res = list(res)
    main = res[0] if n_ex_out == 0 else tuple(res[:1 + n_ex_out])
    return (main, res[1 + n_ex_out:]) if n_x else main


def _row_map(nb, reverse, seg):
    if reverse:
        return lambda i: (nb - 1 - i, seg)
    return lambda i: (i, seg)


def _row_call(body, *, name, T, ins, outs, acc_outs=(), tm=ROW_BLOCK, reverse=False):
    tm = min(tm, T)
    nb = T // tm
    in_specs, args = [], []
    for arr, how in ins:
        args.append(arr)
        if how is True:
            in_specs.append(pl.BlockSpec((tm, arr.shape[1]), _row_map(nb, reverse, 0)))
        elif how is False:
            in_specs.append(pl.BlockSpec(arr.shape, lambda i, _n=arr.ndim: (0,) * _n))
        else:
            in_specs.append(pl.BlockSpec((tm, SEG), _row_map(nb, reverse, how[0])))
    out_specs, out_shape = [], []
    for o in outs:
        c, dt = o[0], o[1]
        total, seg = o[2] if len(o) > 2 else (c, 0)
        out_specs.append(pl.BlockSpec((tm, c), _row_map(nb, reverse, seg)))
        out_shape.append(jax.ShapeDtypeStruct((T, total), dt))
    for shp, dt in acc_outs:
        out_specs.append(pl.BlockSpec(shp, lambda i, _n=len(shp): (0,) * _n))
        out_shape.append(jax.ShapeDtypeStruct(shp, dt))
    return _pallas(body, name=name, grid=(nb,), in_specs=in_specs, out_specs=out_specs, out_shape=out_shape,
                   compiler_params=_params(dimension_semantics=("arbitrary",)))(*args)


def _rms_fwd(x, g, blocks, *, name):
    T, D = x.shape
    tm = min(ROW_BLOCK, T)
    nb = T // tm
    n = len(blocks)

    def body(*refs):
        x_ref, g_ref = refs[:2]
        h_ref = refs[2 + n]
        g_start, g_forward, g_finish = _gather_steps(refs[2:2 + n], refs[3 + n:3 + 2 * n], *refs[3 + 2 * n:])

        @pl.when(pl.program_id(0) == 0)
        def _():
            g_start()

        xv = x_ref[...]
        rstd = lax.rsqrt(jnp.mean(xv * xv, axis=-1, keepdims=True) + EPS)
        h_ref[...] = (xv * rstd * g_ref[...]).astype(BF16)

        @pl.when(pl.program_id(0) == nb - 1)
        def _():
            g_forward()
            g_finish()

    hbm = pl.BlockSpec(memory_space=pl.ANY)
    res = _pallas(
        body, name=name, grid=(nb,),
        in_specs=[pl.BlockSpec((tm, D), lambda i: (i, 0)), pl.BlockSpec((1, D), lambda i: (0, 0))] + [hbm] * n,
        out_specs=[pl.BlockSpec((tm, D), lambda i: (i, 0))] + [hbm] * n,
        out_shape=[jax.ShapeDtypeStruct((T, D), BF16)] + [jax.ShapeDtypeStruct((N_DEV,) + b.shape, b.dtype) for b in blocks],
        scratch_shapes=_gather_scratch(n),
        compiler_params=_params(dimension_semantics=("arbitrary",)),
    )(x, g, *blocks)
    return res[0], list(res[1:])


def _merge_fwd(ya, yb, wa, wb, zg):
    def body(ya_ref, yb_ref, wa_ref, wb_ref, zg_ref, m_ref, ua_ref, ub_ref):
        ua = _dot(ya_ref[...].astype(BF16), wa_ref[...], 1, 0)
        ub = _dot(yb_ref[...].astype(BF16), wb_ref[...], 1, 0)
        ga = _sigmoid(zg_ref[:, :D_MODEL])
        gb = _sigmoid(zg_ref[:, D_MODEL:])
        m_ref[...] = (ga * ua + gb * ub).astype(BF16)
        ua_ref[...] = ua.astype(BF16)
        ub_ref[...] = ub.astype(BF16)

    return _row_call(body, name="merge_fwd", T=ya.shape[0],
                     ins=[(ya, True), (yb, True), (wa, False), (wb, False), (zg, (2,))],
                     outs=[(D_MODEL, BF16)] * 3)


def _merge_bwd(dx1, w_out, wa, wb, ua, ub, zg):
    def body(dx_ref, wo_ref, wa_ref, wb_ref, ua_ref, ub_ref, zg_ref, dua_ref, dub_ref, dzg_ref, dya_ref, dyb_ref):
        dmv = _dot(dx_ref[...].astype(BF16), wo_ref[...], 1, 1)
        ga = _sigmoid(zg_ref[:, :D_MODEL])
        gb = _sigmoid(zg_ref[:, D_MODEL:])
        dua = (dmv * ga).astype(BF16)
        dub = (dmv * gb).astype(BF16)
        dua_ref[...] = dua
        dub_ref[...] = dub
        dzg_ref[:, :D_MODEL] = (dmv * ua_ref[...].astype(F32) * ga * (1.0 - ga)).astype(BF16)
        dzg_ref[:, D_MODEL:] = (dmv * ub_ref[...].astype(F32) * gb * (1.0 - gb)).astype(BF16)
        dya_ref[...] = _dot(dua, wa_ref[...], 1, 1)
        dyb_ref[...] = _dot(dub, wb_ref[...], 1, 1)

    return _row_call(body, name="merge_bwd", T=dx1.shape[0],
                     ins=[(dx1, True), (w_out, False), (wa, False), (wb, False), (ua, True), (ub, True), (zg, (2,))],
                     outs=[(D_MODEL, BF16), (D_MODEL, BF16), (SEG, BF16, (IN_PAD, 2)), (HG_W, F32), (FOX_W, F32)])


def _swiglu_fwd(hf, w_gate, w_up):
    T, D = hf.shape
    F = w_gate.shape[0]
    tm, tn = _pick(T, ROW_BLOCK), _pick(F, FFN_TILE)

    def body(h_ref, wg_ref, wu_ref, a_ref, b_ref, o_ref):
        hv = h_ref[...]
        a_b = _dot(hv, wg_ref[...], 1, 1).astype(BF16)
        b_b = _dot(hv, wu_ref[...], 1, 1).astype(BF16)
        a_ref[...] = a_b
        b_ref[...] = b_b
        av = a_b.astype(F32)
        o_ref[...] = (av * _sigmoid(av) * b_b.astype(F32)).astype(BF16)

    tile = pl.BlockSpec((tm, tn), lambda i, j: (i, j))
    wcol = pl.BlockSpec((tn, D), lambda i, j: (j, 0))
    return _pallas(
        body, name="swiglu_fwd", grid=(T // tm, F // tn),
        in_specs=[pl.BlockSpec((tm, D), lambda i, j: (i, 0)), wcol, wcol],
        out_specs=[tile] * 3, out_shape=[jax.ShapeDtypeStruct((T, F), BF16)] * 3,
        compiler_params=_params(dimension_semantics=("arbitrary",) * 2),
    )(hf, w_gate, w_up)


def _swiglu_bwd(dx, w_down, a, b):
    T, D = dx.shape
    F = w_down.shape[0]
    tm, tn = _pick(T, ROW_BLOCK), _pick(F, FFN_TILE)

    def body(dx_ref, w_ref, a_ref, b_ref, da_ref, db_ref):
        dact = _dot(dx_ref[...].astype(BF16), w_ref[...], 1, 1)
        av = a_ref[...].astype(F32)
        bv = b_ref[...].astype(F32)
        sg = _sigmoid(av)
        da_ref[...] = (dact * bv * sg * (1.0 + av * (1.0 - sg))).astype(BF16)
        db_ref[...] = (dact * av * sg).astype(BF16)

    tile = pl.BlockSpec((tm, tn), lambda i, j: (i, j))
    return _pallas(
        body, name="swiglu_bwd", grid=(T // tm, F // tn),
        in_specs=[pl.BlockSpec((tm, D), lambda i, j: (i, 0)), pl.BlockSpec((tn, D), lambda i, j: (j, 0)), tile, tile],
        out_specs=[tile, tile], out_shape=[jax.ShapeDtypeStruct((T, F), BF16)] * 2,
        compiler_params=_params(dimension_semantics=("arbitrary",) * 2),
    )(dx, w_down, a, b)


def _ple_loss(x2, p, g, w_gate, w_proj, tgt):
    def body(x_ref, p_ref, g_ref, wg_ref, wp_ref, t_ref, hp_ref, dy_ref, dsp_ref, dpp_ref, loss_ref):
        xv = x_ref[...]
        rstd = lax.rsqrt(jnp.mean(xv * xv, axis=-1, keepdims=True) + EPS)
        hp = (xv * rstd * g_ref[...]).astype(BF16)
        hp_ref[...] = hp
        gp = _sigmoid(_dot(hp, wg_ref[...], 1, 0))
        ppv = _dot(p_ref[...].astype(BF16), wp_ref[...], 1, 0)
        err = xv + gp * ppv - t_ref[...]
        part = 0.5 * jnp.sum(jnp.mean(err * err, axis=-1, keepdims=True), axis=0, keepdims=True)
        part = jnp.broadcast_to(part, loss_ref.shape)

        @pl.when(pl.program_id(0) == 0)
        def _():
            loss_ref[...] = part

        @pl.when(pl.program_id(0) > 0)
        def _():
            loss_ref[...] += part

        dy = err * (1.0 / D_MODEL)
        dy_ref[...] = dy
        dsp_ref[...] = (dy * ppv * gp * (1.0 - gp)).astype(BF16)
        dpp_ref[...] = (dy * gp).astype(BF16)

    return _row_call(body, name="ple_loss", T=x2.shape[0],
                     ins=[(x2, True), (p, True), (g, False), (w_gate, False), (w_proj, False), (tgt, True)],
                     outs=[(D_MODEL, BF16), (D_MODEL, F32), (D_MODEL, BF16), (D_MODEL, BF16)],
                     acc_outs=[((8, LANES), F32)])


def _hg_tri():
    C = HG_CHUNK
    return _iota((C, C), 1) <= _iota((C, C), 0)


def _hg_ref_row(b, blk):
    mid = blk * HG_SUB + HG_SUB // 2 - 1
    return b[mid:mid + 1]


def _hg_chunk_fwd(q, f, lb, tri_b):
    sgq = _sigmoid(q)
    qt = q * sgq
    sg = _sigmoid(f)
    fg = lb + (1.0 - lb) * sg
    kf = (1.0 - lb) * (1.0 - sg)
    logf = jnp.log(fg)
    b = _split_dot(tri_b, logf, 1, 0)
    refs = [jnp.broadcast_to(_hg_ref_row(b, blk), (HG_SUB, b.shape[1])) for blk in range(HG_CHUNK // HG_SUB)]
    w = jnp.minimum(b - jnp.concatenate(refs, axis=0), EXP_CLAMP)
    return sgq, qt, sg, fg, kf, b, w


def _hg_scores(qs_b, kf, b, row):
    C, S = HG_CHUNK, HG_SUB
    parts, ks = [], []
    for blk in range(C // S):
        e = jnp.exp(jnp.minimum(_hg_ref_row(b, blk) - b, EXP_CLAMP))
        e = jnp.where(row < (blk + 1) * S, e, 0.0)
        k_b = (kf * e).astype(BF16)
        ks.append((e, k_b))
        parts.append(_dot(qs_b[blk * S:(blk + 1) * S], k_b, 1, 1))
    return jnp.concatenate(parts, axis=0), ks


def _hgrn_fwd(z, lb_logits, gain, blocks):
    T = z.shape[0]
    RB = min(ROW_BLOCK, T)
    nb, cpb = T // RB, RB // HG_CHUNK
    C, DK = HG_CHUNK, HG_DK
    n = len(blocks)

    def body(*refs):
        z_ref, lg_ref, g_ref = refs[:3]
        o_ref, y_ref, st_ref = refs[3 + n:6 + n]
        s_ref = refs[6 + 2 * n]
        g_start, g_forward, g_finish = _gather_steps(refs[3:3 + n], refs[6 + n:6 + 2 * n], *refs[7 + 2 * n:])

        @pl.when(pl.program_id(0) == 0)
        def _():
            s_ref[...] = jnp.zeros_like(s_ref)
            g_start()

        lg = lg_ref[...]
        lb_all = 1.0 / (1.0 + jnp.exp(lg[1:2] - lg[0:1]))
        gain_v = g_ref[...]
        tri = _hg_tri()
        tri_b = tri.astype(BF16)
        row = _iota((C, DK), 0)

        def chunk(ci, carry):
            r0 = pl.multiple_of(ci * C, C)
            rows = pl.ds(r0, C)
            _, qt, _, _, kf_all, b_all, w = _hg_chunk_fwd(z_ref[rows, 0:HG_W], z_ref[rows, HG_W:2 * HG_W], lb_all,
                                                          tri_b)
            qs_all = (qt * jnp.exp(w)).astype(BF16)
            qd_all = (qt * jnp.exp(b_all)).astype(BF16)
            bl_all = b_all[C - 1:C]
            kd_all = (kf_all * jnp.exp(bl_all - b_all)).astype(BF16)
            ebl_all = jnp.exp(bl_all)
            v_all = z_ref[rows, 2 * HG_W:3 * HG_W].astype(BF16)
            g_all = z_ref[rows, 3 * HG_W:4 * HG_W]
            gate_all = g_all * _sigmoid(g_all)
            for h in range(HG_HEADS):
                cs = slice(h * DK, (h + 1) * DK)
                st = s_ref[h]
                st_ref[pl.ds(pl.multiple_of((ci * HG_HEADS + h) * DK, DK), DK), :] = st
                v_b = v_all[:, cs]
                a, _ = _hg_scores(qs_all[:, cs], kf_all[:, cs], b_all[:, cs], row)
                a = jnp.where(tri, a, 0.0)
                o = _dot(qd_all[:, cs], st.astype(BF16), 1, 1) + _dot(a.astype(BF16), v_b, 1, 0)
                s_ref[h] = st * ebl_all[:, cs] + _dot(v_b, kd_all[:, cs], 0, 0)
                o_ref[rows, cs] = o
                rstd = lax.rsqrt(jnp.mean(o * o, axis=-1, keepdims=True) + EPS)
                y_ref[rows, cs] = (o * rstd * gain_v * gate_all[:, cs]).astype(BF16)
            return carry

        lax.fori_loop(0, cpb, chunk, 0, unroll=2)

        @pl.when(pl.program_id(0) == nb - 1)
        def _():
            g_forward()
            g_finish()

    hbm = pl.BlockSpec(memory_space=pl.ANY)
    res = _pallas(
        body, name="hgrn_fwd", grid=(nb,),
        in_specs=[pl.BlockSpec((RB, HG_COLS), lambda i: (i, 0)), pl.BlockSpec((2, HG_W), lambda i: (0, 0)),
                  pl.BlockSpec((1, DK), lambda i: (0, 0))] + [hbm] * n,
        out_specs=[pl.BlockSpec((RB, HG_W), lambda i: (i, 0)), pl.BlockSpec((RB, HG_W), lambda i: (i, 0)),
                   pl.BlockSpec((cpb * HG_HEADS * DK, DK), lambda i: (i, 0))] + [hbm] * n,
        out_shape=[jax.ShapeDtypeStruct((T, HG_W), F32), jax.ShapeDtypeStruct((T, HG_W), BF16),
                   jax.ShapeDtypeStruct((T // C * HG_HEADS * DK, DK), F32)]
        + [jax.ShapeDtypeStruct((N_DEV,) + b.shape, b.dtype) for b in blocks],
        scratch_shapes=[pltpu.VMEM((HG_HEADS, DK, DK), F32)] + _gather_scratch(n),
        compiler_params=_params(dimension_semantics=("arbitrary",)),
    )(z, lb_logits, gain, *blocks)
    return res[0], res[1], res[2], res[3:]


def _hgrn_bwd(z, o_raw, dy, states, lb_logits, gain, dz_buf):
    T = z.shape[0]
    RB = min(ROW_BLOCK, T)
    nb, cpb = T // RB, RB // HG_CHUNK
    C, DK, S = HG_CHUNK, HG_DK, HG_SUB

    def body(z_ref, o_ref, dy_ref, st_ref, lg_ref, g_ref, _buf_ref, dz_ref, dlg_ref, dg_ref, ds_ref, dlb_ref):
        step = pl.program_id(0)

        @pl.when(step == 0)
        def _():
            ds_ref[...] = jnp.zeros_like(ds_ref)
            dlb_ref[...] = jnp.zeros_like(dlb_ref)
            dg_ref[...] = jnp.zeros_like(dg_ref)

        lg = lg_ref[...]
        lb_all = 1.0 / (1.0 + jnp.exp(lg[1:2] - lg[0:1]))
        gain_v = g_ref[...]
        tri = _hg_tri()
        tri_b = tri.astype(BF16)
        row = _iota((C, DK), 0)

        def chunk(cj, carry):
            ci = cpb - 1 - cj
            r0 = pl.multiple_of(ci * C, C)
            rows = pl.ds(r0, C)
            q_all = z_ref[rows, 0:HG_W]
            g_all = z_ref[rows, 3 * HG_W:4 * HG_W]
            sgq_all, qt_all, sg_all, fg_all, kf_all, b_all, w_all = _hg_chunk_fwd(
                q_all, z_ref[rows, HG_W:2 * HG_W], lb_all, tri_b)
            ew_all = jnp.exp(w_all)
            eb_all = jnp.exp(b_all)
            bl_all = b_all[C - 1:C]
            ebl_all = jnp.exp(bl_all)
            ekd_all = jnp.exp(bl_all - b_all)
            qs_all = (qt_all * ew_all).astype(BF16)
            qd_all = (qt_all * eb_all).astype(BF16)
            kd_all = (kf_all * ekd_all).astype(BF16)
            v_all = z_ref[rows, 2 * HG_W:3 * HG_W].astype(BF16)
            sgg_all = _sigmoid(g_all)
            t1_all = dy_ref[rows, :] * (g_all * sgg_all)
            db_heads, dqt_heads, dkf_heads, dv_heads, n_heads = [], [], [], [], []
            for h in range(HG_HEADS):
                cs = slice(h * DK, (h + 1) * DK)
                kf, b, ew, eb, ebl, ekd = kf_all[:, cs], b_all[:, cs], ew_all[:, cs], eb_all[:, cs], ebl_all[:, cs], \
                    ekd_all[:, cs]
                qs_b, qd_b, kd_b, v_b = qs_all[:, cs], qd_all[:, cs], kd_all[:, cs], v_all[:, cs]
                st = st_ref[pl.ds(pl.multiple_of((ci * HG_HEADS + h) * DK, DK), DK), :]
                dst = ds_ref[h]
                o = o_ref[rows, cs]
                rstd = lax.rsqrt(jnp.mean(o * o, axis=-1, keepdims=True) + EPS)
                n = o * rstd
                n_heads.append(n)
                t1 = t1_all[:, cs]
                dg_ref[...] += jnp.sum(t1 * n, axis=0, keepdims=True)
                dn = t1 * gain_v
                do = rstd * (dn - n * jnp.mean(dn * n, axis=-1, keepdims=True))
                do_b = do.astype(BF16)
                a, ks = _hg_scores(qs_b, kf, b, row)
                a = jnp.where(tri, a, 0.0)
                dst_b = dst.astype(BF16)
                dqd = _dot(do_b, st.astype(BF16), 1, 0)
                da = jnp.where(tri, _dot(do_b, v_b, 1, 1), 0.0)
                dv = _dot(a.astype(BF16), do_b, 0, 0) + _dot(kd_b, dst_b, 1, 1)
                dkd = _dot(v_b, dst_b, 1, 0)
                ds_ref[h] = dst * ebl + _dot(do_b, qd_b, 0, 0)
                dkd_kd = dkd * kd_b.astype(F32)
                dbl = ebl * jnp.sum(dst * st, axis=0, keepdims=True) + jnp.sum(dkd_kd, axis=0, keepdims=True)
                da_b = da.astype(BF16)
                dqs_parts = []
                dk_in = jnp.zeros((C, DK), F32)
                db_k = jnp.zeros((C, DK), F32)
                for blk in range(C // S):
                    e, k_b = ks[blk]
                    da_blk = da_b[blk * S:(blk + 1) * S]
                    dqs_parts.append(_dot(da_blk, k_b, 1, 0))
                    dks = _dot(da_blk, qs_b[blk * S:(blk + 1) * S], 0, 0)
                    dk_in = dk_in + dks * e
                    db_k = db_k + dks * k_b.astype(F32)
                dqs = jnp.concatenate(dqs_parts, axis=0)
                db = qs_b.astype(F32) * dqs - db_k + dqd * qd_b.astype(F32) - dkd_kd
                db_heads.append(db + jnp.where(row == C - 1, dbl, 0.0))
                dqt_heads.append(dqs * ew + dqd * eb)
                dkf_heads.append(dk_in + dkd * ekd)
                dv_heads.append(dv)
            dlogf = _split_dot(tri_b, jnp.concatenate(db_heads, axis=1), 0, 0)
            dfg = dlogf / fg_all - jnp.concatenate(dkf_heads, axis=1)
            dlb_ref[...] += jnp.sum(dfg * (1.0 - sg_all), axis=0, keepdims=True)
            dqt = jnp.concatenate(dqt_heads, axis=1)
            n_all = jnp.concatenate(n_heads, axis=1)
            gain_all = jnp.concatenate([gain_v] * HG_HEADS, axis=1)
            dz_ref[rows, 0:HG_W] = (dqt * sgq_all * (1.0 + q_all * (1.0 - sgq_all))).astype(BF16)
            dz_ref[rows, HG_W:2 * HG_W] = (dfg * (1.0 - lb_all) * sg_all * (1.0 - sg_all)).astype(BF16)
            dz_ref[rows, 2 * HG_W:3 * HG_W] = jnp.concatenate(dv_heads, axis=1).astype(BF16)
            dz_ref[rows, 3 * HG_W:4 * HG_W] = (dy_ref[rows, :] * n_all * gain_all * sgg_all
                                               * (1.0 + g_all * (1.0 - sgg_all))).astype(BF16)
            return carry

        lax.fori_loop(0, cpb, chunk, 0, unroll=2)

        @pl.when(step == nb - 1)
        def _():
            d0 = dlb_ref[...] * lb_all * (1.0 - lb_all)
            dlg_ref[0:1, :] = d0
            dlg_ref[1:2, :] = -d0

    rev = lambda i: (nb - 1 - i, 0)
    fix = lambda i: (0, 0)
    return _pallas(
        body, name="hgrn_bwd", grid=(nb,),
        in_specs=[pl.BlockSpec((RB, HG_COLS), rev), pl.BlockSpec((RB, HG_W), rev), pl.BlockSpec((RB, HG_W), rev),
                  pl.BlockSpec((cpb * HG_HEADS * DK, DK), rev), pl.BlockSpec((2, HG_W), fix),
                  pl.BlockSpec((1, DK), fix), pl.BlockSpec(memory_space=pl.ANY)],
        out_specs=[pl.BlockSpec((RB, HG_COLS), rev), pl.BlockSpec((2, HG_W), fix), pl.BlockSpec((1, DK), fix)],
        out_shape=[jax.ShapeDtypeStruct(dz_buf.shape, BF16), jax.ShapeDtypeStruct((2, HG_W), F32),
                   jax.ShapeDtypeStruct((1, DK), F32)],
        scratch_shapes=[pltpu.VMEM((HG_HEADS, DK, DK), F32), pltpu.VMEM((1, HG_W), F32)],
        input_output_aliases={6: 0},
        compiler_params=_params(dimension_semantics=("arbitrary",)),
    )(z, o_raw, dy, states, lb_logits, gain, dz_buf)


def _head_ones():
    r, c = _iota((FOX_W, FOX_W), 0), _iota((FOX_W, FOX_W), 1)
    return ((r // FOX_DH) == (c // FOX_DH)).astype(BF16)


def _log_sigmoid(x):
    return jnp.minimum(x, 0.0) - jnp.log(1.0 + jnp.exp(-jnp.abs(x)))


def _fox_prep(z, bias, qg, kg):
    T = z.shape[0]
    tm = min(ROW_BLOCK, T)
    nb = T // tm

    def body(z_ref, b_ref, qg_ref, kg_ref, q_ref, k_ref, v_ref, qa_ref, ka_ref, carry_ref):
        @pl.when(pl.program_id(0) == 0)
        def _():
            carry_ref[...] = jnp.zeros_like(carry_ref)

        ones = _head_ones()
        normed = []
        for src, g_ref in ((0, qg_ref), (1, kg_ref)):
            xv = z_ref[:, src * FOX_W:(src + 1) * FOX_W]
            ms = _split_dot(ones, xv * xv, 1, 0, mat_first=False) * (1.0 / FOX_DH)
            normed.append(xv * lax.rsqrt(ms + EPS) * g_ref[...])
        qn, kn = normed
        q_ref[...] = (qn * FOX_DH ** -0.5).astype(BF16)
        k_b = kn.astype(BF16)
        k_ref[...] = k_b
        v_ref[...] = z_ref[:, 2 * FOX_W:3 * FOX_W].astype(BF16)
        logf = _log_sigmoid(z_ref[:, 3 * FOX_W:FOX_COLS] + b_ref[...])
        r, c = _iota((tm, tm), 0), _iota((tm, tm), 1)
        tri_b = (c <= r).astype(BF16)
        cum = _split_dot(tri_b, logf, 1, 0, terms=3) + carry_ref[...]
        carry_ref[...] = cum[tm - 1:tm]
        c2 = cum * LOG2E
        hi = c2.astype(BF16)
        rem = c2 - hi.astype(F32)
        mid = rem.astype(BF16)
        lo = (rem - mid.astype(F32)).astype(BF16)
        hrow, col = _iota((LANES, 2 * FOX_W), 0), _iota((LANES, 2 * FOX_W), 1)
        base = hrow * LANES + jnp.where(hrow % 2 == 0, FOX_DH, 0)
        placed = None
        for t, part in enumerate((hi, mid, lo)):
            place = jnp.logical_and(col == base + t, hrow < FOX_HEADS).astype(BF16)
            term = _dot(part, place, 1, 0)
            placed = term if placed is None else placed + term
        colw = _iota((tm, 2 * FOX_W), 1)
        head, lane = colw // LANES, colw % LANES
        own = (lane < FOX_DH) == (head % 2 == 0)
        other = jnp.where(head % 2 == 0, lane - FOX_DH, lane)
        ones_q = jnp.where(jnp.logical_and(other >= 0, other < 3), -1.0, 0.0)
        q2 = (qn * (FOX_DH ** -0.5 * LOG2E)).astype(BF16)
        q_exp = jnp.concatenate([q2[:, (h // 2) * LANES:(h // 2 + 1) * LANES] for h in range(FOX_HEADS)], axis=1)
        k_exp = jnp.concatenate([k_b[:, (h // 2) * LANES:(h // 2 + 1) * LANES] for h in range(FOX_HEADS)], axis=1)
        qa_ref[...] = jnp.where(own, q_exp, ones_q.astype(BF16))
        ka_ref[...] = jnp.where(own, k_exp, placed.astype(BF16))

    wide = pl.BlockSpec((tm, 2 * FOX_W), lambda i: (i, 0))
    return _pallas(
        body, name="fox_prep", grid=(nb,),
        in_specs=[pl.BlockSpec((tm, SEG), lambda i: (i, 1)), pl.BlockSpec((1, LANES), lambda i: (0, 0)),
                  pl.BlockSpec((1, FOX_W), lambda i: (0, 0)), pl.BlockSpec((1, FOX_W), lambda i: (0, 0))],
        out_specs=[pl.BlockSpec((tm, FOX_W), lambda i: (i, 0))] * 3 + [wide] * 2,
        out_shape=[jax.ShapeDtypeStruct((T, FOX_W), BF16)] * 3 + [jax.ShapeDtypeStruct((T, 2 * FOX_W), BF16)] * 2,
        scratch_shapes=[pltpu.VMEM((1, LANES), F32)],
        compiler_params=_params(dimension_semantics=("arbitrary",)),
    )(z, bias, qg, kg)


def _fox_fwd(qa, ka, vb, blocks):
    T = qa.shape[0]
    tq = min(ROW_BLOCK, T)
    nq = T // tq
    NEG = -1e30
    n = len(blocks)
    n_pairs = FOX_HEADS // 2

    n_in = 3

    def body(*refs):
        q_ref, k_ref, v_ref = refs[:n_in]
        o_ref, lse_ref = refs[n_in + n:n_in + n + 2]
        m_sc, l_sc, acc_sc = refs[n_in + 2 * n + 2:n_in + 2 * n + 5]
        pr, qi = pl.program_id(0), pl.program_id(1)
        g_start, g_forward, g_finish = _gather_steps(
            refs[n_in:n_in + n], refs[n_in + n + 2:n_in + 2 * n + 2], *refs[n_in + 2 * n + 5:])

        @pl.when(jnp.logical_and(pr == 0, qi == 0))
        def _():
            g_start()

        @pl.when(jnp.logical_and(pr == n_pairs // 2, qi == 0))
        def _():
            g_forward()

        m_sc[...] = jnp.full_like(m_sc, NEG)
        l_sc[...] = jnp.zeros_like(l_sc)
        acc_sc[...] = jnp.zeros_like(acc_sc)
        lane = _iota((tq, LANES), 1)

        def block(masked, ki):
            keys = pl.ds(pl.multiple_of(ki * tq, tq), tq)
            vv = v_ref[keys, :]
            for hh in range(2):
                hs = slice(hh * LANES, (hh + 1) * LANES)
                s = _dot(q_ref[:, hs], k_ref[keys, hs], 1, 1)
                tiles = [s[:, j * LANES:(j + 1) * LANES] for j in range(tq // LANES)]
                if masked:
                    row, col = _iota((tq, LANES), 0), _iota((tq, LANES), 1)
                    tiles = [jnp.where(row >= col + j * LANES, t, NEG) for j, t in enumerate(tiles)]
                m_old = m_sc[hh]
                top = jnp.broadcast_to(jnp.max(functools.reduce(jnp.maximum, tiles), axis=-1, keepdims=True),
                                       (tq, LANES))
                m_new = jnp.maximum(m_old, top)
                alpha = jnp.exp2(m_old - m_new)
                ps = [jnp.exp2(t - m_new) for t in tiles]
                l_sc[hh] = alpha * l_sc[hh] + functools.reduce(jnp.add, ps)
                m_sc[hh] = m_new
                p_b = jnp.concatenate([p.astype(BF16) for p in ps], axis=1)
                acc_sc[hh] = alpha * acc_sc[hh] + _dot(p_b, vv, 1, 0)

        def before(ki, carry):
            block(False, ki)
            return carry

        lax.fori_loop(0, qi, before, 0)
        block(True, qi)
        l0 = jnp.sum(l_sc[0], axis=-1, keepdims=True)
        l1 = jnp.sum(l_sc[1], axis=-1, keepdims=True)
        o_ref[...] = jnp.where(lane < FOX_DH, acc_sc[0] * (1.0 / l0), acc_sc[1] * (1.0 / l1))
        lse_ref[:, :LANES] = m_sc[0] + jnp.log2(l0)
        lse_ref[:, LANES:] = m_sc[1] + jnp.log2(l1)

        @pl.when(jnp.logical_and(pr == n_pairs - 1, qi == nq - 1))
        def _():
            g_finish()

    qmap = lambda p, i: (i, p)
    whole = lambda p, i: (0, p)
    hbm = pl.BlockSpec(memory_space=pl.ANY)
    res = _pallas(
        body, name="fox_fwd", grid=(n_pairs, nq),
        in_specs=[pl.BlockSpec((tq, 2 * LANES), qmap), pl.BlockSpec((T, 2 * LANES), whole),
                  pl.BlockSpec((T, LANES), whole)] + [hbm] * n,
        out_specs=[pl.BlockSpec((tq, LANES), qmap), pl.BlockSpec((tq, 2 * LANES), qmap)] + [hbm] * n,
        out_shape=[jax.ShapeDtypeStruct((T, FOX_W), F32), jax.ShapeDtypeStruct((T, 2 * FOX_W), F32)]
        + [jax.ShapeDtypeStruct((N_DEV,) + b.shape, b.dtype) for b in blocks],
        scratch_shapes=[pltpu.VMEM((2, tq, LANES), F32)] * 3 + _gather_scratch(n),
        compiler_params=_params(dimension_semantics=("arbitrary",) * 2),
    )(qa, ka, vb, *blocks)
    return res[0], res[1], res[2:]


def _fox_bwd(qs, kn, vb, qa, ka, o, do, lse, hs):
    T = qs.shape[0]
    tq = min(ROW_BLOCK, T)
    nq = T // tq
    n = len(hs)
    n_pairs = FOX_HEADS // 2

    def body(*refs):
        q_ref, k_ref, v_ref, qa_ref, ka_ref, o_ref, do_ref, lse_ref = refs[:8]
        dq_ref, dk_ref, dv_ref, dcs_ref, drs_ref = refs[8 + n:13 + n]
        pr, ki = pl.program_id(0), pl.program_id(1)
        x_start, x_finish = _chip_exchange_steps(refs[8:8 + n], refs[13 + n:13 + 2 * n], *refs[13 + 2 * n:])

        @pl.when(jnp.logical_and(pr == 0, ki == 0))
        def _():
            x_start()
            drs_ref[...] = jnp.zeros_like(drs_ref)

        @pl.when(ki == 0)
        def _():
            dq_ref[...] = jnp.zeros_like(dq_ref)

        dk_ref[...] = jnp.zeros_like(dk_ref)
        dv_ref[...] = jnp.zeros_like(dv_ref)
        dcs_ref[...] = jnp.zeros_like(dcs_ref)

        def block(masked, qi):
            lane = _iota((tq, LANES), 1)
            qrows = pl.ds(pl.multiple_of(qi * tq, tq), tq)
            qv, kv, vv = q_ref[qrows, :], k_ref[...], v_ref[...]
            ov, dov = o_ref[qrows, :], do_ref[qrows, :]
            dq_acc = jnp.zeros((tq, LANES), F32)
            dk_acc = jnp.zeros((tq, LANES), F32)
            dv_acc = jnp.zeros((tq, LANES), F32)
            dcs_acc = jnp.zeros((8, tq), F32)
            drs_acc = jnp.zeros((tq, LANES), F32)
            prod = dov * ov
            nt = tq // LANES
            for hh in range(2):
                in_head = (lane < FOX_DH) if hh == 0 else (lane >= FOX_DH)
                hs_ = slice(hh * LANES, (hh + 1) * LANES)
                zb = jnp.zeros_like(qv)
                qm = jnp.where(in_head, qv, zb)
                km = jnp.where(in_head, kv, zb)
                dom = jnp.where(in_head, dov, 0.0).astype(BF16)
                delta_b = jnp.broadcast_to(jnp.sum(jnp.where(in_head, prod, 0.0), axis=1, keepdims=True), (tq, LANES))
                lse_b = lse_ref[qrows, hs_]
                s = _dot(qa_ref[qrows, hs_], ka_ref[:, hs_], 1, 1)
                dp = _dot(dom, vv, 1, 1)
                p_tiles, ds_tiles, col_tiles = [], [], []
                row_part = jnp.zeros((tq, LANES), F32)
                for j in range(nt):
                    js = slice(j * LANES, (j + 1) * LANES)
                    p = jnp.exp2(s[:, js] - lse_b)
                    if masked:
                        p = jnp.where(_iota((tq, LANES), 0) >= _iota((tq, LANES), 1) + j * LANES, p, 0.0)
                    ds = p * (dp[:, js] - delta_b)
                    p_tiles.append(p.astype(BF16))
                    ds_tiles.append(ds.astype(BF16))
                    col_tiles.append(jnp.sum(ds, axis=0, keepdims=True))
                    row_part = row_part + ds
                p_b = jnp.concatenate(p_tiles, axis=1)
                ds_b = jnp.concatenate(ds_tiles, axis=1)
                dv_acc = dv_acc + _dot(p_b, dom, 0, 0)
                dq_acc = dq_acc + _dot(ds_b, km, 1, 0)
                dk_acc = dk_acc + _dot(ds_b, qm, 0, 0)
                dcs_acc = dcs_acc + jnp.where(_iota((8, tq), 0) == hh, jnp.concatenate(col_tiles, axis=1), 0.0)
                rowsum = jnp.sum(row_part, axis=1, keepdims=True)
                drs_acc = drs_acc + jnp.where(lane == 2 * pr + hh, rowsum, 0.0)
            drs_ref[qrows, :] += drs_acc
            dq_ref[qrows, :] += dq_acc
            dk_ref[...] += dk_acc
            dv_ref[...] += dv_acc
            dcs_ref[0] += dcs_acc

        block(True, ki)

        def after(qi, carry):
            block(False, qi)
            return carry

        lax.fori_loop(ki + 1, nq, after, 0)

        @pl.when(jnp.logical_and(pr == n_pairs - 1, ki == nq - 1))
        def _():
            x_finish()

    whole = lambda p, j: (0, p)
    kmap = lambda p, j: (j, p)
    hbm = pl.BlockSpec(memory_space=pl.ANY)
    res = _pallas(
        body, name="fox_bwd", grid=(n_pairs, nq),
        in_specs=[pl.BlockSpec((T, LANES), whole), pl.BlockSpec((tq, LANES), kmap), pl.BlockSpec((tq, LANES), kmap),
                  pl.BlockSpec((T, 2 * LANES), whole), pl.BlockSpec((tq, 2 * LANES), kmap),
                  pl.BlockSpec((T, LANES), whole), pl.BlockSpec((T, LANES), whole), pl.BlockSpec((T, 2 * LANES), whole)]
        + [hbm] * n,
        out_specs=[pl.BlockSpec((T, LANES), whole), pl.BlockSpec((tq, LANES), kmap),
                   pl.BlockSpec((tq, LANES), kmap), pl.BlockSpec((1, 8, tq), lambda p, j: (p, 0, j)),
                   pl.BlockSpec((T, LANES), lambda p, j: (0, 0))] + [hbm] * n,
        out_shape=[jax.ShapeDtypeStruct((T, FOX_W), F32)] * 3
        + [jax.ShapeDtypeStruct((n_pairs, 8, T), F32), jax.ShapeDtypeStruct((T, LANES), F32)]
        + _chip_exchange_shapes(hs),
        scratch_shapes=_chip_exchange_scratch(n),
        compiler_params=_params(dimension_semantics=("arbitrary",) * 2),
    )(qs, kn, vb, qa, ka, o, do, lse, *hs)
    res = list(res)
    return res[:5] + [res[5:]]


def _fox_post(z, dq, dk, dv, dcs, drs, bias, qg, kg, dz_buf):
    T = z.shape[0]
    tm = min(ROW_BLOCK, T)
    nb = T // tm

    def body(z_ref, dq_ref, dk_ref, dv_ref, dcs_ref, drs_ref, b_ref, qg_ref, kg_ref, _buf_ref, dz_ref, dqg_ref, dkg_ref,
             db_ref, carry_ref):
        @pl.when(pl.program_id(0) == 0)
        def _():
            carry_ref[...] = jnp.zeros_like(carry_ref)
            dqg_ref[...] = jnp.zeros_like(dqg_ref)
            dkg_ref[...] = jnp.zeros_like(dkg_ref)
            db_ref[...] = jnp.zeros_like(db_ref)

        ones = _head_ones()
        for src, g_ref, d_ref, dg_ref, scale in ((0, qg_ref, dq_ref, dqg_ref, FOX_DH ** -0.5), (1, kg_ref, dk_ref, dkg_ref, 1.0)):
            xv = z_ref[:, src * FOX_W:(src + 1) * FOX_W]
            ms = _split_dot(ones, xv * xv, 1, 0, mat_first=False) * (1.0 / FOX_DH)
            rstd = lax.rsqrt(ms + EPS)
            xh = xv * rstd
            dn = d_ref[...] * scale
            dg_ref[...] += jnp.sum(dn * xh, axis=0, keepdims=True)
            dxh = dn * g_ref[...]
            mean = _split_dot(ones, dxh * xh, 1, 0, mat_first=False) * (1.0 / FOX_DH)
            dz_ref[:, src * FOX_W:(src + 1) * FOX_W] = (rstd * (dxh - xh * mean)).astype(BF16)
        dz_ref[:, 2 * FOX_W:3 * FOX_W] = dv_ref[...].astype(BF16)
        row8 = _iota((8, tm), 0)
        dct = jnp.zeros((8, tm), F32)
        for h in range(FOX_HEADS):
            src_row = dcs_ref[h // 2][h % 2:h % 2 + 1, :]
            dct = dct + jnp.where(row8 == h, src_row, 0.0)
        dct = drs_ref[...].T[0:8] - dct
        r, c = _iota((tm, tm), 0), _iota((tm, tm), 1)
        upper_b = (r >= c).astype(BF16)
        rc = _split_dot(upper_b, dct, 1, 0, mat_first=False) + carry_ref[...]
        carry_ref[...] = rc[:, 0:1]
        full = jnp.concatenate([rc, jnp.zeros((LANES - 8, tm), F32)], axis=0)
        dlogf = full.T
        xf = z_ref[:, 3 * FOX_W:FOX_COLS] + b_ref[...]
        df = dlogf * (1.0 - _sigmoid(xf))
        dz_ref[:, 3 * FOX_W:FOX_COLS] = df.astype(BF16)
        dz_ref[:, FOX_COLS:] = jnp.zeros((tm, SEG - FOX_COLS), BF16)
        db_ref[...] += jnp.sum(df, axis=0, keepdims=True)

    rev = lambda i: (nb - 1 - i, 0)
    fix2 = lambda i: (0, 0)
    return _pallas(
        body, name="fox_post", grid=(nb,),
        in_specs=[pl.BlockSpec((tm, SEG), lambda i: (nb - 1 - i, 1)), pl.BlockSpec((tm, FOX_W), rev),
                  pl.BlockSpec((tm, FOX_W), rev),
                  pl.BlockSpec((tm, FOX_W), rev), pl.BlockSpec((FOX_HEADS // 2, 8, tm), lambda i: (0, 0, nb - 1 - i)),
                  pl.BlockSpec((tm, LANES), rev),
                  pl.BlockSpec((1, LANES), fix2), pl.BlockSpec((1, FOX_W), fix2), pl.BlockSpec((1, FOX_W), fix2),
                  pl.BlockSpec(memory_space=pl.ANY)],
        out_specs=[pl.BlockSpec((tm, SEG), lambda i: (nb - 1 - i, 1)), pl.BlockSpec((1, FOX_W), fix2),
                   pl.BlockSpec((1, FOX_W), fix2), pl.BlockSpec((1, LANES), fix2)],
        out_shape=[jax.ShapeDtypeStruct(dz_buf.shape, BF16), jax.ShapeDtypeStruct((1, FOX_W), F32),
                   jax.ShapeDtypeStruct((1, FOX_W), F32), jax.ShapeDtypeStruct((1, LANES), F32)],
        scratch_shapes=[pltpu.VMEM((8, 1), F32)],
        input_output_aliases={9: 0},
        compiler_params=_params(dimension_semantics=("arbitrary",)),
    )(z, dq, dk, dv, dcs, drs, bias, qg, kg, dz_buf)


def _local_step(x, p, tgt, sm, chunks, core):
    lbl, og, fb = sm["hg_lb_logits"], sm["hg_onorm_g"], sm["fox_f_bias"]
    fbias = jnp.pad(fb, ((0, 0), (0, LANES - FOX_HEADS)))
    qg = jnp.tile(sm["fox_q_norm_g"], (1, FOX_HEADS))
    kg = jnp.tile(sm["fox_k_norm_g"], (1, FOX_HEADS))

    assert BIG[0] == "w_in"
    h, got0 = _rms_fwd(x, sm["norm_mix_g"], chunks[:1], name="rms_mix")
    W = {"w_in": _full_of_chunks("w_in", got0[0])}
    rest = dict(zip(BIG[1:], chunks[1:]))
    first, second = ["w_ffn_gate"], ["w_ffn_up"]
    third = [n for n in BIG[1:] if n not in first + second]
    z, got1 = _matmul(h, W["w_in"], tb=True, gather=[rest[n] for n in first], name="mm_z")
    o_raw, ya, states, got2 = _hgrn_fwd(z, lbl, og, [rest[n] for n in second])
    qs, kn, vb, qa, ka = _fox_prep(z, fbias, qg, kg)
    yb, lse, got3 = _fox_fwd(qa, ka, vb, [rest[n] for n in third])
    W = dict(W, **{n: _full_of_chunks(n, g)
                   for n, g in zip(first + second + third, list(got1) + list(got2) + list(got3))})
    merged, ua, ub = _merge_fwd(ya, yb, W["w_branch_a"], W["w_branch_b"], z)
    x1, hf = _matmul(merged, W["w_out"], add=x, norm_fwd=sm["norm_ffn_g"], name="mm_x1")
    a, b, act = _swiglu_fwd(hf, W["w_ffn_gate"], W["w_ffn_up"])
    x2 = _matmul(act, W["w_ffn_down"], add=x1, name="mm_x2")
    hp, dy, dsp, dpp, loss = _ple_loss(x2, p, sm["norm_ple_g"], W["w_ple_gate"], W["w_ple_proj"], tgt)

    G = {}
    G["w_ple_proj"] = _matmul(p, dpp, ta=True, out_dtype=BF16, name="mm_dw_ple_proj")
    G["w_ple_gate"] = _matmul(hp, dsp, ta=True, out_dtype=BF16, name="mm_dw_ple_gate")
    dx2, d_ple_g = _matmul(dsp, W["w_ple_gate"], tb=True, norm_bwd=(x2, sm["norm_ple_g"], dy), name="mm_dx2")
    G["w_ffn_down"] = _matmul(act, dx2, ta=True, out_dtype=BF16, name="mm_dw_ffn_down")
    da, db = _swiglu_bwd(dx2, W["w_ffn_down"], a, b)
    G["w_ffn_gate"] = _matmul(da, hf, ta=True, out_dtype=BF16, name="mm_dw_ffn_gate")
    G["w_ffn_up"] = _matmul(db, hf, ta=True, out_dtype=BF16, name="mm_dw_ffn_up")
    dhf = _matmul(da, W["w_ffn_gate"], name="mm_dhf_a")
    dx1, d_ffn_g = _matmul(db, W["w_ffn_up"], add=dhf, norm_bwd=(x1, sm["norm_ffn_g"], dx2), name="mm_dx1")
    G["w_out"] = _matmul(merged, dx1, ta=True, out_dtype=BF16, name="mm_dw_out")
    dua, dub, dz, dya, dyb = _merge_bwd(dx1, W["w_out"], W["w_branch_a"], W["w_branch_b"], ua, ub, z)
    G["w_branch_a"] = _matmul(ya, dua, ta=True, out_dtype=BF16, name="mm_dw_branch_a")
    G["w_branch_b"] = _matmul(yb, dub, ta=True, out_dtype=BF16, name="mm_dw_branch_b")
    hb_rest = _sibling_sums({n: G[n] for n in BIG[1:]}, core, tag="rest")
    dq, dk, dv, dcs, drs, got_rest = _fox_bwd(qs, kn, vb, qa, ka, yb, dyb, lse, hb_rest)
    dz, d_qg, d_kg, d_fb = _fox_post(z, dq, dk, dv, dcs, drs, fbias, qg, kg, dz)
    dz, d_lbl, d_og = _hgrn_bwd(z, o_raw, dya, states, lbl, og, dz)
    G["w_in"] = _matmul(dz, h, ta=True, out_dtype=BF16, name="mm_dw_in")
    hb_in = _sibling_sums({"w_in": G["w_in"]}, core, tag="w_in")
    (grad_x, d_mix_g), got_in = _matmul(dz, W["w_in"], exchange=hb_in,
                                        norm_bwd=(x, sm["norm_mix_g"], dx1), name="mm_dx")

    gs = {"norm_mix_g": d_mix_g, "hg_lb_logits": d_lbl, "hg_onorm_g": d_og, "fox_f_bias": d_fb[:, :FOX_HEADS],
          "fox_q_norm_g": d_qg.reshape(FOX_HEADS, FOX_DH).sum(0, keepdims=True),
          "fox_k_norm_g": d_kg.reshape(FOX_HEADS, FOX_DH).sum(0, keepdims=True),
          "norm_ffn_g": d_ffn_g, "norm_ple_g": d_ple_g}
    return loss, grad_x, gs, hb_in + hb_rest, list(got_in) + list(got_rest)


def _pack_rows(parts, total):
    buf = jnp.concatenate(parts, axis=-2)
    pad = total - buf.shape[-2]
    widths = [(0, 0)] * (buf.ndim - 2) + [(0, pad), (0, 0)]
    return jnp.pad(buf, widths)


def _chunk_of_shard(n, w):
    if n == "w_in":
        return jnp.pad(w, ((0, IN_SHARD_PAD - IN_SHARD), (0, 0)))
    return w


def _full_of_chunks(n, g):
    _, a, b = g.shape
    if n == "w_in":
        w = g[:, :IN_SHARD].reshape(IN_COLS, b)
        gap = jnp.zeros((SEG - FOX_LOGICAL, b), g.dtype)
        return jnp.concatenate([w[:HG_COLS + FOX_LOGICAL], gap, w[HG_COLS + FOX_LOGICAL:]], axis=0)
    if BIG_SHAPE[n][2] == 0:
        return g.reshape(N_DEV * a, b)
    return g.transpose(1, 0, 2).reshape(a, N_DEV * b)


def _chunks_of_full(n, g):
    if n == "w_in":
        w = jnp.concatenate([g[:HG_COLS + FOX_LOGICAL], g[2 * SEG:]], axis=0).reshape(N_DEV, IN_SHARD, g.shape[1])
        return jnp.pad(w, ((0, 0), (0, IN_SHARD_PAD - IN_SHARD), (0, 0)))
    if BIG_SHAPE[n][2] == 0:
        return g.reshape(N_DEV, g.shape[0] // N_DEV, g.shape[1])
    return g.reshape(g.shape[0], N_DEV, g.shape[1] // N_DEV).transpose(1, 0, 2)


def _pack_small(vals, loss_row=None):
    parts = [vals[n].reshape(SMALL_ROWS[n], -1) for n in SMALL]
    parts = [jnp.pad(v, ((0, 0), (0, LANES - v.shape[1]))) for v in parts]
    if loss_row is not None:
        parts.append(loss_row)
    return _pack_rows(parts, SMALL_TOTAL)


def _unpack_small(buf, like):
    out, r0 = {}, 0
    for n in SMALL:
        rows, size = SMALL_ROWS[n], like[n].size
        blk = buf[r0:r0 + rows]
        out[n] = (blk if size == rows * LANES else blk[:, :size]).reshape(like[n].shape)
        r0 += rows
    return out


def _place():
    return lax.axis_index("x"), lax.axis_index("y"), lax.axis_index("c")


def _gather_steps(x_refs, out_refs, send_sems, recv_sems, local_sems):
    n = len(x_refs)
    x, y, c = _place()
    me, sibling = (x, y, c), (x, y, 1 - c)
    chips = [(1 - x, y), (x, 1 - y), (1 - x, 1 - y)]

    def slot(i, px, py, pc):
        return out_refs[i].at[4 * px + 2 * py + pc]

    def copy(k, i, blk, to, own=False):
        return pltpu.make_async_remote_copy(
            src_ref=x_refs[i] if own else slot(i, *blk), dst_ref=slot(i, *blk),
            send_sem=send_sems.at[k, i], recv_sem=recv_sems.at[k, i], device_id=to, device_id_type=MESH)

    def mine():
        return [pltpu.make_async_copy(x_refs[i], slot(i, *me), local_sems.at[i]) for i in range(n)]

    def first():
        cps = [copy(0, i, me, sibling, own=True) for i in range(n)]
        return cps + [copy(1 + j, i, me, (*chip, c), own=True) for j, chip in enumerate(chips) for i in range(n)]

    def passed():
        return [copy(4 + j, i, (*chip, c), sibling) for j, chip in enumerate(chips) for i in range(n)]

    def start():
        for cp in mine() + first():
            cp.start()

    def forward():
        fws = passed()
        for j, chip in enumerate(chips):
            for i in range(n):
                copy(1 + j, i, (*chip, c), me).wait_recv()
                fws[j * n + i].start()

    def finish():
        for i in range(n):
            copy(0, i, sibling, me).wait_recv()
        for j, chip in enumerate(chips):
            for i in range(n):
                copy(4 + j, i, (*chip, 1 - c), me).wait_recv()
        for cp in first() + passed():
            cp.wait_send()
        for cp in mine():
            cp.wait()

    return start, forward, finish


def _gather_scratch(n):
    return [pltpu.SemaphoreType.DMA((7, n)), pltpu.SemaphoreType.DMA((7, n)), pltpu.SemaphoreType.DMA((n,))]


def _sibling_exchange(gs, *, name):
    n = len(gs)

    def body(*refs):
        g_refs, out_refs = refs[:n], refs[n:2 * n]
        send_sems, recv_sems = refs[2 * n:]
        x, y, c = _place()
        cps = [pltpu.make_async_remote_copy(
            src_ref=g_refs[i].at[:, pl.ds(1 - c, 1)], dst_ref=out_refs[i], send_sem=send_sems.at[i],
            recv_sem=recv_sems.at[i], device_id=(x, y, 1 - c), device_id_type=MESH) for i in range(n)]
        for cp in cps:
            cp.start()
        for cp in cps:
            cp.wait()

    hbm = pl.BlockSpec(memory_space=pl.ANY)
    return _pallas(
        body, name=name, out_shape=[jax.ShapeDtypeStruct((N_CHIP, 1) + g.shape[2:], g.dtype) for g in gs],
        in_specs=[hbm] * n, out_specs=[hbm] * n,
        scratch_shapes=[pltpu.SemaphoreType.DMA((n,)), pltpu.SemaphoreType.DMA((n,))],
    )(*gs)


def _chip_sum(g4s, gots, core, *, name):
    n = len(g4s)

    def body(c_ref, *refs):
        for g_ref, r_ref, h_ref in zip(refs[:n], refs[n:2 * n], refs[2 * n:]):
            h_ref[0] = (g_ref[0, 0].astype(F32) + r_ref[0, 0].astype(F32)).astype(BF16)

    shapes = [g.shape[2:] for g in g4s]
    grid_spec = pltpu.PrefetchScalarGridSpec(
        num_scalar_prefetch=1, grid=(N_CHIP,),
        in_specs=[pl.BlockSpec((1, 1) + s, lambda j, c: (j, c[0], 0, 0)) for s in shapes]
        + [pl.BlockSpec((1, 1) + s, lambda j, c: (j, 0, 0, 0)) for s in shapes],
        out_specs=[pl.BlockSpec((1,) + s, lambda j, c: (j, 0, 0)) for s in shapes])
    return list(_pallas(
        body, name=name, grid_spec=grid_spec, out_shape=[jax.ShapeDtypeStruct((N_CHIP,) + s, BF16) for s in shapes],
        compiler_params=_params(dimension_semantics=("arbitrary",)),
    )(core, *g4s, *gots))


def _chip_exchange_steps(h_refs, out_refs, send_sems, recv_sems):
    n = len(h_refs)
    x, y, c = _place()
    chips = [(1 - x, y), (x, 1 - y), (1 - x, 1 - y)]

    def copies():
        return [pltpu.make_async_remote_copy(
            src_ref=h_refs[i].at[2 * px + py], dst_ref=out_refs[i].at[k], send_sem=send_sems.at[k, i],
            recv_sem=recv_sems.at[k, i], device_id=(px, py, c), device_id_type=MESH)
            for k, (px, py) in enumerate(chips) for i in range(n)]

    def start():
        for cp in copies():
            cp.start()

    def finish():
        for cp in copies():
            cp.wait()

    return start, finish


def _chip_exchange_shapes(hs):
    return [jax.ShapeDtypeStruct((3,) + h.shape[1:], h.dtype) for h in hs]


def _chip_exchange_scratch(n):
    return [pltpu.SemaphoreType.DMA((3, n)), pltpu.SemaphoreType.DMA((3, n))]


def _sibling_sums(G, core, *, tag):
    g4 = []
    for n, g in G.items():
        gc = _chunks_of_full(n, g)
        g4.append(gc.reshape((N_CHIP, 2) + gc.shape[1:]))
    got = _sibling_exchange(g4, name="grads_to_sibling_" + tag)
    return _chip_sum(g4, got, core, name="chip_sum_" + tag)


def _adam_math(w, g, m, v):
    m = ADAM_B1 * m + (1.0 - ADAM_B1) * g
    v = ADAM_B2 * v + (1.0 - ADAM_B2) * (g * g)
    m_hat = m / (1.0 - ADAM_B1 ** ADAM_STEP)
    v_hat = v / (1.0 - ADAM_B2 ** ADAM_STEP)
    delta = -ADAM_LR * (m_hat / (jnp.sqrt(v_hat) + ADAM_EPS) + ADAM_WD * w)
    return delta, m, v


def _adam_shard(hb, got, chip, w, m, v, *, name):
    _, r, c = w.shape
    _, a, cb = hb.shape
    assert cb == c and c % LANES == 0, (hb.shape, w.shape)
    tc = _pick(c, 2 * LANES)

    def body(j_ref, h_ref, r_ref, w_ref, m_ref, v_ref, g_ref, d_ref, nm_ref, nv_ref):
        parts = [h_ref[0], r_ref[0], r_ref[1], r_ref[2]]
        g = None
        for part in parts:
            part = part[:r].astype(F32)
            g = part if g is None else g + part
        d, nm, nv = _adam_math(w_ref[0], g, m_ref[0], v_ref[0])
        g_ref[0] = g
        d_ref[0] = d
        nm_ref[0] = nm
        nv_ref[0] = nv

    blk = pl.BlockSpec((1, r, tc), lambda i, j: (0, 0, i))
    grid_spec = pltpu.PrefetchScalarGridSpec(
        num_scalar_prefetch=1, grid=(c // tc,),
        in_specs=[pl.BlockSpec((1, a, tc), lambda i, j: (j[0], 0, i)),
                  pl.BlockSpec((3, a, tc), lambda i, j: (0, 0, i)), blk, blk, blk],
        out_specs=[blk] * 4)
    return _pallas(
        body, name=name, grid_spec=grid_spec, out_shape=[jax.ShapeDtypeStruct((1, r, c), F32)] * 4,
        compiler_params=_params(dimension_semantics=("arbitrary",)),
    )(chip, hb, got, w, m, v)


def _small_all_reduce_adam(gs, w, m, v):
    def body(g_ref, w_ref, m_ref, v_ref, sum_ref, d_ref, nm_ref, nv_ref, gather, send_sems, recv_sems):
        x, y, c = _place()
        my = 4 * x + 2 * y + c
        gather[my] = g_ref[...]
        cps = []
        for k in range(1, N_DEV):
            to = (x ^ (k >> 2), y ^ ((k >> 1) & 1), c ^ (k & 1))
            cps.append(pltpu.make_async_remote_copy(
                src_ref=g_ref, dst_ref=gather.at[my], send_sem=send_sems.at[k - 1], recv_sem=recv_sems.at[k - 1],
                device_id=to, device_id_type=MESH))
        for cp in cps:
            cp.start()
        for cp in cps:
            cp.wait()
        total = gather[0]
        for d in range(1, N_DEV):
            total = total + gather[d]
        dlt, nm, nv = _adam_math(w_ref[...], total, m_ref[...], v_ref[...])
        sum_ref[...] = total
        d_ref[...] = dlt
        nm_ref[...] = nm
        nv_ref[...] = nv

    vm = pl.BlockSpec(memory_space=pltpu.VMEM)
    return _pallas(
        body, name="small_all_reduce_adam", out_shape=[jax.ShapeDtypeStruct((SMALL_TOTAL, LANES), F32)] * 4,
        in_specs=[vm] * 4, out_specs=[vm] * 4,
        scratch_shapes=[pltpu.VMEM((N_DEV, SMALL_TOTAL, LANES), F32), pltpu.SemaphoreType.DMA((7,)),
                        pltpu.SemaphoreType.DMA((7,))],
            )(gs, w, m, v)


def kernel(x, p, norm_mix_g, w_in, hg_lb_logits, hg_onorm_g, fox_f_bias, fox_q_norm_g, fox_k_norm_g, w_branch_a, w_branch_b, w_out, norm_ffn_g, w_ffn_gate, w_ffn_up, w_ffn_down, norm_ple_g, w_ple_gate, w_ple_proj, loss_target, m_norm_mix_g, m_w_in, m_hg_lb_logits, m_hg_onorm_g, m_fox_f_bias, m_fox_q_norm_g, m_fox_k_norm_g, m_w_branch_a, m_w_branch_b, m_w_out, m_norm_ffn_g, m_w_ffn_gate, m_w_ffn_up, m_w_ffn_down, m_norm_ple_g, m_w_ple_gate, m_w_ple_proj, v_norm_mix_g, v_w_in, v_hg_lb_logits, v_hg_onorm_g, v_fox_f_bias, v_fox_q_norm_g, v_fox_k_norm_g, v_w_branch_a, v_w_branch_b, v_w_out, v_norm_ffn_g, v_w_ffn_gate, v_w_ffn_up, v_w_ffn_down, v_norm_ple_g, v_w_ple_gate, v_w_ple_proj):
    args = dict(locals())
    wts = {n: args[n] for n in BIG + SMALL}
    mom = {n: args["m_" + n] for n in BIG + SMALL}
    var = {n: args["v_" + n] for n in BIG + SMALL}
    for group in (wts, mom, var):
        for n in TRANSPOSED:
            group[n] = jnp.swapaxes(group[n], 1, 2)
    sm = {n: wts[n] for n in SMALL}

    xi, yi, ci = _place()
    core = jnp.reshape(ci, (1,)).astype(jnp.int32)
    chip = jnp.reshape(2 * xi + yi, (1,)).astype(jnp.int32)
    chunks = [_chunk_of_shard(n, wts[n][0].astype(BF16)) for n in BIG]
    loss_blk, grad_x, gs, hb, got = _local_step(x[0], p[0, 0], loss_target[0], sm, chunks, core)

    g_big, d_big, nm_big, nv_big = {}, {}, {}, {}
    for n, h, r in zip(BIG, hb, got):
        res = _adam_shard(h, r, chip, wts[n], mom[n], var[n], name="adam_" + n)
        if n in TRANSPOSED:
            res = [jnp.swapaxes(t, 1, 2) for t in res]
        g_big[n], d_big[n], nm_big[n], nv_big[n] = res

    s_sum, s_d, s_nm, s_nv = _small_all_reduce_adam(
        _pack_small(gs, loss_blk[0:1]), _pack_small(sm), _pack_small({n: mom[n] for n in SMALL}),
        _pack_small({n: var[n] for n in SMALL}))
    loss = s_sum[LOSS_ROW, 0]
    g_small, d_small, nm_small, nv_small = (_unpack_small(t, sm) for t in (s_sum, s_d, s_nm, s_nv))

    order = ["norm_mix_g", "w_in", "hg_lb_logits", "hg_onorm_g", "fox_f_bias", "fox_q_norm_g", "fox_k_norm_g",
             "w_branch_a", "w_branch_b", "w_out", "norm_ffn_g", "w_ffn_gate", "w_ffn_up", "w_ffn_down", "norm_ple_g",
             "w_ple_gate", "w_ple_proj"]
    outs = [loss, grad_x[None]]
    for big, small in ((g_big, g_small), (d_big, d_small), (nm_big, nm_small), (nv_big, nv_small)):
        outs += [big[n] if n in big else small[n] for n in order]
    return tuple(outs)
```

```python
import functools

import jax
import jax.numpy as jnp
from jax import lax
from jax.experimental import pallas as pl
from jax.experimental.pallas import tpu as pltpu

F32 = jnp.float32
BF16 = jnp.bfloat16

D_MODEL = 1024
PLE_DIM = 256
HG_HEADS = 4
HG_DK = 128
HG_CHUNK = 64
HG_SUB = 16
HG_W = HG_HEADS * HG_DK
FOX_HEADS = 8
FOX_DH = 64
FOX_W = FOX_HEADS * FOX_DH
D_FF = 2816
EPS = 1e-6
N_DEV = 8
N_CHIP = 4
LANES = 128
FOX_COLS = 3 * FOX_W + LANES
HG_COLS = 4 * HG_W
GATE_COLS = 2 * D_MODEL
IN_COLS = HG_COLS + 3 * FOX_W + FOX_HEADS + GATE_COLS
FOX_LOGICAL = 3 * FOX_W + FOX_HEADS
SEG = 2048
IN_PAD = 3 * SEG
IN_SHARD = IN_COLS // N_DEV
IN_SHARD_PAD = 720
EXP_CLAMP = 80.0
LOG2E = 1.4426950408889634

ADAM_LR = 0.001
ADAM_B1 = 0.9
ADAM_B2 = 0.999
ADAM_EPS = 1e-08
ADAM_WD = 0.01
ADAM_STEP = 10

MESH = pl.DeviceIdType.MESH
VMEM_LIMIT = 56 * 1024 * 1024
ROW_BLOCK = 512
MATMUL_TN = 2048
MATMUL_TK = 3072
MATMUL_TM_T = 1408
FFN_TILE = 1408

BIG = ["w_in", "w_branch_a", "w_branch_b", "w_out", "w_ffn_gate", "w_ffn_up", "w_ffn_down",
       "w_ple_gate", "w_ple_proj"]
TRANSPOSED = ("w_in", "w_ffn_gate", "w_ffn_up")
BIG_SHAPE = {
    "w_in": (IN_COLS, D_MODEL, 0), "w_branch_a": (HG_W, D_MODEL, 1), "w_branch_b": (FOX_W, D_MODEL, 1),
    "w_out": (D_MODEL, D_MODEL, 0), "w_ffn_gate": (D_FF, D_MODEL, 0), "w_ffn_up": (D_FF, D_MODEL, 0),
    "w_ffn_down": (D_FF, D_MODEL, 0), "w_ple_gate": (D_MODEL, D_MODEL, 0), "w_ple_proj": (PLE_DIM, D_MODEL, 1),
}

SMALL = ["norm_mix_g", "hg_lb_logits", "hg_onorm_g", "fox_f_bias", "fox_q_norm_g", "fox_k_norm_g",
         "norm_ffn_g", "norm_ple_g"]
SMALL_ROWS = {"norm_mix_g": 8, "hg_lb_logits": 8, "hg_onorm_g": 1, "fox_f_bias": 1, "fox_q_norm_g": 1,
              "fox_k_norm_g": 1, "norm_ffn_g": 8, "norm_ple_g": 8}
SMALL_TOTAL = 40
LOSS_ROW = 36


def _pallas(body, **kw):
    return pl.pallas_call(body, **kw)


def _params(**kw):
    return pltpu.CompilerParams(vmem_limit_bytes=VMEM_LIMIT, **kw)


def _pick(n, target):
    if n <= target:
        return n
    best = None
    for t in range(LANES, target + 1, LANES):
        if n % t == 0:
            best = t
    assert best is not None, (n, target)
    return best


def _dot(a, b, ca, cb):
    return lax.dot_general(a, b, (((ca,), (cb,)), ((), ())), preferred_element_type=F32)


def _split_dot(mat, x, ca, cb, terms=2, mat_first=True):
    acc = None
    rem = x
    for _ in range(terms):
        part = rem.astype(BF16)
        rem = rem - part.astype(F32)
        p = _dot(mat, part, ca, cb) if mat_first else _dot(part, mat, ca, cb)
        acc = p if acc is None else acc + p
    return acc


def _sigmoid(x):
    return 1.0 / (1.0 + jnp.exp(-x))


def _iota(shape, dim):
    return lax.broadcasted_iota(jnp.int32, shape, dim)


def _matmul(a, b, *, name, ta=False, tb=False, out_dtype=F32, add=None, exchange=None, gather=None,
            norm_fwd=None, norm_bwd=None):
    assert exchange is None or gather is None
    (K, M) = a.shape if ta else a.shape[::-1]
    (N, Kb) = b.shape if tb else b.shape[::-1]
    assert K == Kb, (a.shape, b.shape, ta, tb)
    if ta:
        tm, tn, tk = _pick(M, MATMUL_TM_T), _pick(N, 2 * ROW_BLOCK), _pick(K, 4 * ROW_BLOCK)
    else:
        tm, tn, tk = _pick(M, 2 * ROW_BLOCK), _pick(N, MATMUL_TN), _pick(K, MATMUL_TK)
    if norm_bwd is not None:
        tm = _pick(M, ROW_BLOCK)
    nk = K // tk
    use_scratch = nk > 1 and out_dtype != F32
    if norm_fwd is not None or norm_bwd is not None:
        assert tn == N and not use_scratch and out_dtype == F32

    hs = list(exchange or gather or [])
    n_x = len(hs)
    grid = (M // tm, N // tn, nk)
    a_spec = pl.BlockSpec((tk, tm), lambda i, j, k: (k, i)) if ta else pl.BlockSpec((tm, tk), lambda i, j, k: (i, k))
    b_spec = pl.BlockSpec((tn, tk), lambda i, j, k: (j, k)) if tb else pl.BlockSpec((tk, tn), lambda i, j, k: (k, j))
    o_spec = pl.BlockSpec((tm, tn), lambda i, j, k: (i, j))
    row_vec = pl.BlockSpec((1, N), lambda i, j, k: (0, 0))
    hbm = pl.BlockSpec(memory_space=pl.ANY)
    extra_in = [(add, o_spec)] if add is not None else []
    extra_out = []
    if norm_fwd is not None:
        extra_in += [(norm_fwd, row_vec)]
        extra_out += [(jax.ShapeDtypeStruct((M, N), BF16), o_spec)]
    if norm_bwd is not None:
        extra_in += [(norm_bwd[0], o_spec), (norm_bwd[1], row_vec), (norm_bwd[2], o_spec)]
        extra_out += [(jax.ShapeDtypeStruct((1, N), F32), row_vec)]
    if gather is not None:
        ride_shapes, ride_scratch = [jax.ShapeDtypeStruct((N_DEV,) + h.shape, h.dtype) for h in hs], _gather_scratch(n_x)
    else:
        ride_shapes, ride_scratch = _chip_exchange_shapes(hs), (_chip_exchange_scratch(n_x) if n_x else [])
    n_ex_in, n_ex_out = len(extra_in), len(extra_out)

    def body(*refs):
        refs = list(refs)
        a_ref, b_ref = refs[:2]
        ex_in = refs[2:2 + n_ex_in]
        ride_in = refs[2 + n_ex_in:2 + n_ex_in + n_x]
        base = 2 + n_ex_in + n_x
        o_ref = refs[base]
        ex_out = refs[base + 1:base + 1 + n_ex_out]
        ride_out = refs[base + 1 + n_ex_out:base + 1 + n_ex_out + n_x]
        scratch = refs[base + 1 + n_ex_out + n_x:]
        at = [pl.program_id(d) for d in range(3)]
        k = at[2]
        if n_x:
            steps = _gather_steps if gather is not None else _chip_exchange_steps
            ride = steps(ride_in, ride_out, *scratch[-len(ride_scratch):])

            @pl.when(jnp.logical_and(at[0] == 0, jnp.logical_and(at[1] == 0, at[2] == 0)))
            def _():
                ride[0]()
        p = _dot(a_ref[...].astype(BF16), b_ref[...].astype(BF16), 0 if ta else 1, 1 if tb else 0)

        def finish(r):
            ins = list(ex_in)
            outs = list(ex_out)
            if add is not None:
                r = r + ins.pop(0)[...].astype(F32)
            if norm_fwd is not None:
                g_ref = ins.pop(0)
                rstd = lax.rsqrt(jnp.mean(r * r, axis=-1, keepdims=True) + EPS)
                outs.pop(0)[...] = (r * rstd * g_ref[...]).astype(BF16)
            if norm_bwd is not None:
                x_ref, g_ref, dres_ref = ins.pop(0), ins.pop(0), ins.pop(0)
                dg_ref = outs.pop(0)
                xv = x_ref[...]
                rstd = lax.rsqrt(jnp.mean(xv * xv, axis=-1, keepdims=True) + EPS)
                xh = xv * rstd
                part = jnp.sum(r * xh, axis=0, keepdims=True)

                @pl.when(at[0] == 0)
                def _():
                    dg_ref[...] = part

                @pl.when(at[0] > 0)
                def _():
                    dg_ref[...] += part

                dxh = r * g_ref[...]
                r = rstd * (dxh - xh * jnp.mean(dxh * xh, axis=-1, keepdims=True)) + dres_ref[...]
            o_ref[...] = r.astype(out_dtype)

        if nk == 1:
            finish(p)
        elif not use_scratch:
            @pl.when(k == 0)
            def _():
                o_ref[...] = p

            @pl.when(jnp.logical_and(k > 0, k < nk - 1))
            def _():
                o_ref[...] += p

            @pl.when(k == nk - 1)
            def _():
                finish(o_ref[...] + p)
        else:
            acc_ref = scratch[0]

            @pl.when(k == 0)
            def _():
                acc_ref[...] = p

            @pl.when(k > 0)
            def _():
                acc_ref[...] += p

            @pl.when(k == nk - 1)
            def _():
                finish(acc_ref[...])

        if n_x:
            @pl.when(jnp.logical_and(at[0] == grid[0] - 1, jnp.logical_and(at[1] == grid[1] - 1, at[2] == nk - 1)))
            def _():
                for step in ride[1:]:
                    step()

    res = _pallas(
        body, name=name, grid=grid,
        in_specs=[a_spec, b_spec] + [s for _, s in extra_in] + [hbm] * n_x,
        out_specs=[o_spec] + [s for _, s in extra_out] + [hbm] * n_x,
        out_shape=[jax.ShapeDtypeStruct((M, N), out_dtype)] + [s for s, _ in extra_out] + ride_shapes,
        scratch_shapes=([pltpu.VMEM((tm, tn), F32)] if use_scratch else []) + ride_scratch,
        compiler_params=_params(dimension_semantics=("arbitrary",) * 3),
    )(a, b, *[v for v, _ in extra_in], *hs)
    res = list(res)
    main = res[0] if n_ex_out == 0 else tuple(res[:1 + n_ex_out])
    return (main, res[1 + n_ex_out:]) if n_x else main


def _row_map(nb, reverse, seg):
    if reverse:
        return lambda i: (nb - 1 - i, seg)
    return lambda i: (i, seg)


def _row_call(body, *, name, T, ins, outs, acc_outs=(), tm=ROW_BLOCK, reverse=False):
    tm = min(tm, T)
    nb = T // tm
    in_specs, args = [], []
    for arr, how in ins:
        args.append(arr)
        if how is True:
            in_specs.append(pl.BlockSpec((tm, arr.shape[1]), _row_map(nb, reverse, 0)))
        elif how is False:
            in_specs.append(pl.BlockSpec(arr.shape, lambda i, _n=arr.ndim: (0,) * _n))
        else:
            in_specs.append(pl.BlockSpec((tm, SEG), _row_map(nb, reverse, how[0])))
    out_specs, out_shape = [], []
    for o in outs:
        c, dt = o[0], o[1]
        total, seg = o[2] if len(o) > 2 else (c, 0)
        out_specs.append(pl.BlockSpec((tm, c), _row_map(nb, reverse, seg)))
        out_shape.append(jax.ShapeDtypeStruct((T, total), dt))
    for shp, dt in acc_outs:
        out_specs.append(pl.BlockSpec(shp, lambda i, _n=len(shp): (0,) * _n))
        out_shape.append(jax.ShapeDtypeStruct(shp, dt))
    return _pallas(body, name=name, grid=(nb,), in_specs=in_specs, out_specs=out_specs, out_shape=out_shape,
                   compiler_params=_params(dimension_semantics=("arbitrary",)))(*args)


def _rms_fwd(x, g, blocks, *, name):
    T, D = x.shape
    tm = min(ROW_BLOCK, T)
    nb = T // tm
    n = len(blocks)

    def body(*refs):
        x_ref, g_ref = refs[:2]
        h_ref = refs[2 + n]
        g_start, g_forward, g_finish = _gather_steps(refs[2:2 + n], refs[3 + n:3 + 2 * n], *refs[3 + 2 * n:])

        @pl.when(pl.program_id(0) == 0)
        def _():
            g_start()

        xv = x_ref[...]
        rstd = lax.rsqrt(jnp.mean(xv * xv, axis=-1, keepdims=True) + EPS)
        h_ref[...] = (xv * rstd * g_ref[...]).astype(BF16)

        @pl.when(pl.program_id(0) == nb - 1)
        def _():
            g_forward()
            g_finish()

    hbm = pl.BlockSpec(memory_space=pl.ANY)
    res = _pallas(
        body, name=name, grid=(nb,),
        in_specs=[pl.BlockSpec((tm, D), lambda i: (i, 0)), pl.BlockSpec((1, D), lambda i: (0, 0))] + [hbm] * n,
        out_specs=[pl.BlockSpec((tm, D), lambda i: (i, 0))] + [hbm] * n,
        out_shape=[jax.ShapeDtypeStruct((T, D), BF16)] + [jax.ShapeDtypeStruct((N_DEV,) + b.shape, b.dtype) for b in blocks],
        scratch_shapes=_gather_scratch(n),
        compiler_params=_params(dimension_semantics=("arbitrary",)),
    )(x, g, *blocks)
    return res[0], list(res[1:])


def _merge_fwd(ya, yb, wa, wb, zg):
    def body(ya_ref, yb_ref, wa_ref, wb_ref, zg_ref, m_ref, ua_ref, ub_ref):
        ua = _dot(ya_ref[...].astype(BF16), wa_ref[...], 1, 0)
        ub = _dot(yb_ref[...].astype(BF16), wb_ref[...], 1, 0)
        ga = _sigmoid(zg_ref[:, :D_MODEL])
        gb = _sigmoid(zg_ref[:, D_MODEL:])
        m_ref[...] = (ga * ua + gb * ub).astype(BF16)
        ua_ref[...] = ua.astype(BF16)
        ub_ref[...] = ub.astype(BF16)

    return _row_call(body, name="merge_fwd", T=ya.shape[0],
                     ins=[(ya, True), (yb, True), (wa, False), (wb, False), (zg, (2,))],
                     outs=[(D_MODEL, BF16)] * 3)


def _merge_bwd(dx1, w_out, wa, wb, ua, ub, zg):
    def body(dx_ref, wo_ref, wa_ref, wb_ref, ua_ref, ub_ref, zg_ref, dua_ref, dub_ref, dzg_ref, dya_ref, dyb_ref):
        dmv = _dot(dx_ref[...].astype(BF16), wo_ref[...], 1, 1)
        ga = _sigmoid(zg_ref[:, :D_MODEL])
        gb = _sigmoid(zg_ref[:, D_MODEL:])
        dua = (dmv * ga).astype(BF16)
        dub = (dmv * gb).astype(BF16)
        dua_ref[...] = dua
        dub_ref[...] = dub
        dzg_ref[:, :D_MODEL] = (dmv * ua_ref[...].astype(F32) * ga * (1.0 - ga)).astype(BF16)
        dzg_ref[:, D_MODEL:] = (dmv * ub_ref[...].astype(F32) * gb * (1.0 - gb)).astype(BF16)
        dya_ref[...] = _dot(dua, wa_ref[...], 1, 1)
        dyb_ref[...] = _dot(dub, wb_ref[...], 1, 1)

    return _row_call(body, name="merge_bwd", T=dx1.shape[0],
                     ins=[(dx1, True), (w_out, False), (wa, False), (wb, False), (ua, True), (ub, True), (zg, (2,))],
                     outs=[(D_MODEL, BF16), (D_MODEL, BF16), (SEG, BF16, (IN_PAD, 2)), (HG_W, F32), (FOX_W, F32)])


def _swiglu_fwd(hf, w_gate, w_up):
    T, D = hf.shape
    F = w_gate.shape[0]
    tm, tn = _pick(T, ROW_BLOCK), _pick(F, FFN_TILE)

    def body(h_ref, wg_ref, wu_ref, a_ref, b_ref, o_ref):
        hv = h_ref[...]
        a_b = _dot(hv, wg_ref[...], 1, 1).astype(BF16)
        b_b = _dot(hv, wu_ref[...], 1, 1).astype(BF16)
        a_ref[...] = a_b
        b_ref[...] = b_b
        av = a_b.astype(F32)
        o_ref[...] = (av * _sigmoid(av) * b_b.astype(F32)).astype(BF16)

    tile = pl.BlockSpec((tm, tn), lambda j, i: (i, j))
    wcol = pl.BlockSpec((tn, D), lambda j, i: (j, 0))
    return _pallas(
        body, name="swiglu_fwd", grid=(F // tn, T // tm),
        in_specs=[pl.BlockSpec((tm, D), lambda j, i: (i, 0)), wcol, wcol],
        out_specs=[tile] * 3, out_shape=[jax.ShapeDtypeStruct((T, F), BF16)] * 3,
        compiler_params=_params(dimension_semantics=("arbitrary",) * 2),
    )(hf, w_gate, w_up)


def _swiglu_bwd(dx, w_down, a, b):
    T, D = dx.shape
    F = w_down.shape[0]
    tm, tn = _pick(T, ROW_BLOCK), _pick(F, FFN_TILE)

    def body(dx_ref, w_ref, a_ref, b_ref, da_ref, db_ref):
        dact = _dot(dx_ref[...].astype(BF16), w_ref[...], 1, 1)
        av = a_ref[...].astype(F32)
        bv = b_ref[...].astype(F32)
        sg = _sigmoid(av)
        da_ref[...] = (dact * bv * sg * (1.0 + av * (1.0 - sg))).astype(BF16)
        db_ref[...] = (dact * av * sg).astype(BF16)

    tile = pl.BlockSpec((tm, tn), lambda j, i: (i, j))
    return _pallas(
        body, name="swiglu_bwd", grid=(F // tn, T // tm),
        in_specs=[pl.BlockSpec((tm, D), lambda j, i: (i, 0)), pl.BlockSpec((tn, D), lambda j, i: (j, 0)), tile, tile],
        out_specs=[tile, tile], out_shape=[jax.ShapeDtypeStruct((T, F), BF16)] * 2,
        compiler_params=_params(dimension_semantics=("arbitrary",) * 2),
    )(dx, w_down, a, b)


def _ple_loss(x2, p, g, w_gate, w_proj, tgt):
    def body(x_ref, p_ref, g_ref, wg_ref, wp_ref, t_ref, hp_ref, dy_ref, dsp_ref, dpp_ref, loss_ref):
        xv = x_ref[...]
        rstd = lax.rsqrt(jnp.mean(xv * xv, axis=-1, keepdims=True) + EPS)
        hp = (xv * rstd * g_ref[...]).astype(BF16)
        hp_ref[...] = hp
        gp = _sigmoid(_dot(hp, wg_ref[...], 1, 0))
        ppv = _dot(p_ref[...].astype(BF16), wp_ref[...], 1, 0)
        err = xv + gp * ppv - t_ref[...]
        part = 0.5 * jnp.sum(jnp.mean(err * err, axis=-1, keepdims=True), axis=0, keepdims=True)
        part = jnp.broadcast_to(part, loss_ref.shape)

        @pl.when(pl.program_id(0) == 0)
        def _():
            loss_ref[...] = part

        @pl.when(pl.program_id(0) > 0)
        def _():
            loss_ref[...] += part

        dy = err * (1.0 / D_MODEL)
        dy_ref[...] = dy
        dsp_ref[...] = (dy * ppv * gp * (1.0 - gp)).astype(BF16)
        dpp_ref[...] = (dy * gp).astype(BF16)

    return _row_call(body, name="ple_loss", T=x2.shape[0],
                     ins=[(x2, True), (p, True), (g, False), (w_gate, False), (w_proj, False), (tgt, True)],
                     outs=[(D_MODEL, BF16), (D_MODEL, F32), (D_MODEL, BF16), (D_MODEL, BF16)],
                     acc_outs=[((8, LANES), F32)])


def _hg_tri():
    C = HG_CHUNK
    return _iota((C, C), 1) <= _iota((C, C), 0)


def _hg_ref_row(b, blk):
    mid = blk * HG_SUB + HG_SUB // 2 - 1
    return b[mid:mid + 1]


def _hg_chunk_fwd(q, f, lb, tri_b):
    sgq = _sigmoid(q)
    qt = q * sgq
    sg = _sigmoid(f)
    fg = lb + (1.0 - lb) * sg
    kf = (1.0 - lb) * (1.0 - sg)
    logf = jnp.log(fg)
    b = _split_dot(tri_b, logf, 1, 0)
    refs = [jnp.broadcast_to(_hg_ref_row(b, blk), (HG_SUB, b.shape[1])) for blk in range(HG_CHUNK // HG_SUB)]
    w = jnp.minimum(b - jnp.concatenate(refs, axis=0), EXP_CLAMP)
    return sgq, qt, sg, fg, kf, b, w


def _hg_scores(qs_b, kf, b, row):
    C, S = HG_CHUNK, HG_SUB
    parts, ks = [], []
    for blk in range(C // S):
        e = jnp.exp(jnp.minimum(_hg_ref_row(b, blk) - b, EXP_CLAMP))
        e = jnp.where(row < (blk + 1) * S, e, 0.0)
        k_b = (kf * e).astype(BF16)
        ks.append((e, k_b))
        parts.append(_dot(qs_b[blk * S:(blk + 1) * S], k_b, 1, 1))
    return jnp.concatenate(parts, axis=0), ks


def _hgrn_fwd(z, lb_logits, gain, blocks):
    T = z.shape[0]
    RB = min(ROW_BLOCK, T)
    nb, cpb = T // RB, RB // HG_CHUNK
    C, DK = HG_CHUNK, HG_DK
    n = len(blocks)

    def body(*refs):
        z_ref, lg_ref, g_ref = refs[:3]
        o_ref, y_ref, st_ref = refs[3 + n:6 + n]
        s_ref = refs[6 + 2 * n]
        g_start, g_forward, g_finish = _gather_steps(refs[3:3 + n], refs[6 + n:6 + 2 * n], *refs[7 + 2 * n:])

        @pl.when(pl.program_id(0) == 0)
        def _():
            s_ref[...] = jnp.zeros_like(s_ref)
            g_start()

        lg = lg_ref[...]
        lb_all = 1.0 / (1.0 + jnp.exp(lg[1:2] - lg[0:1]))
        gain_v = g_ref[...]
        tri = _hg_tri()
        tri_b = tri.astype(BF16)
        row = _iota((C, DK), 0)

        def chunk(ci, carry):
            r0 = pl.multiple_of(ci * C, C)
            rows = pl.ds(r0, C)
            _, qt, _, _, kf_all, b_all, w = _hg_chunk_fwd(z_ref[rows, 0:HG_W], z_ref[rows, HG_W:2 * HG_W], lb_all,
                                                          tri_b)
            qs_all = (qt * jnp.exp(w)).astype(BF16)
            qd_all = (qt * jnp.exp(b_all)).astype(BF16)
            bl_all = b_all[C - 1:C]
            kd_all = (kf_all * jnp.exp(bl_all - b_all)).astype(BF16)
            ebl_all = jnp.exp(bl_all)
            v_all = z_ref[rows, 2 * HG_W:3 * HG_W].astype(BF16)
            g_all = z_ref[rows, 3 * HG_W:4 * HG_W]
            gate_all = g_all * _sigmoid(g_all)
            for h in range(HG_HEADS):
                cs = slice(h * DK, (h + 1) * DK)
                st = s_ref[h]
                st_ref[pl.ds(pl.multiple_of((ci * HG_HEADS + h) * DK, DK), DK), :] = st
                v_b = v_all[:, cs]
                a, _ = _hg_scores(qs_all[:, cs], kf_all[:, cs], b_all[:, cs], row)
                a = jnp.where(tri, a, 0.0)
                o = _dot(qd_all[:, cs], st.astype(BF16), 1, 1) + _dot(a.astype(BF16), v_b, 1, 0)
                s_ref[h] = st * ebl_all[:, cs] + _dot(v_b, kd_all[:, cs], 0, 0)
                o_ref[rows, cs] = o
                rstd = lax.rsqrt(jnp.mean(o * o, axis=-1, keepdims=True) + EPS)
                y_ref[rows, cs] = (o * rstd * gain_v * gate_all[:, cs]).astype(BF16)
            return carry

        lax.fori_loop(0, cpb, chunk, 0, unroll=2)

        @pl.when(pl.program_id(0) == nb - 1)
        def _():
            g_forward()
            g_finish()

    hbm = pl.BlockSpec(memory_space=pl.ANY)
    res = _pallas(
        body, name="hgrn_fwd", grid=(nb,),
        in_specs=[pl.BlockSpec((RB, HG_COLS), lambda i: (i, 0)), pl.BlockSpec((2, HG_W), lambda i: (0, 0)),
                  pl.BlockSpec((1, DK), lambda i: (0, 0))] + [hbm] * n,
        out_specs=[pl.BlockSpec((RB, HG_W), lambda i: (i, 0)), pl.BlockSpec((RB, HG_W), lambda i: (i, 0)),
                   pl.BlockSpec((cpb * HG_HEADS * DK, DK), lambda i: (i, 0))] + [hbm] * n,
        out_shape=[jax.ShapeDtypeStruct((T, HG_W), F32), jax.ShapeDtypeStruct((T, HG_W), BF16),
                   jax.ShapeDtypeStruct((T // C * HG_HEADS * DK, DK), F32)]
        + [jax.ShapeDtypeStruct((N_DEV,) + b.shape, b.dtype) for b in blocks],
        scratch_shapes=[pltpu.VMEM((HG_HEADS, DK, DK), F32)] + _gather_scratch(n),
        compiler_params=_params(dimension_semantics=("arbitrary",)),
    )(z, lb_logits, gain, *blocks)
    return res[0], res[1], res[2], res[3:]


def _hgrn_bwd(z, o_raw, dy, states, lb_logits, gain, dz_buf):
    T = z.shape[0]
    RB = min(ROW_BLOCK, T)
    nb, cpb = T // RB, RB // HG_CHUNK
    C, DK, S = HG_CHUNK, HG_DK, HG_SUB

    def body(z_ref, o_ref, dy_ref, st_ref, lg_ref, g_ref, _buf_ref, dz_ref, dlg_ref, dg_ref, ds_ref, dlb_ref):
        step = pl.program_id(0)

        @pl.when(step == 0)
        def _():
            ds_ref[...] = jnp.zeros_like(ds_ref)
            dlb_ref[...] = jnp.zeros_like(dlb_ref)
            dg_ref[...] = jnp.zeros_like(dg_ref)

        lg = lg_ref[...]
        lb_all = 1.0 / (1.0 + jnp.exp(lg[1:2] - lg[0:1]))
        gain_v = g_ref[...]
        tri = _hg_tri()
        tri_b = tri.astype(BF16)
        row = _iota((C, DK), 0)

        def chunk(cj, carry):
            ci = cpb - 1 - cj
            r0 = pl.multiple_of(ci * C, C)
            rows = pl.ds(r0, C)
            q_all = z_ref[rows, 0:HG_W]
            g_all = z_ref[rows, 3 * HG_W:4 * HG_W]
            sgq_all, qt_all, sg_all, fg_all, kf_all, b_all, w_all = _hg_chunk_fwd(
                q_all, z_ref[rows, HG_W:2 * HG_W], lb_all, tri_b)
            ew_all = jnp.exp(w_all)
            eb_all = jnp.exp(b_all)
            bl_all = b_all[C - 1:C]
            ebl_all = jnp.exp(bl_all)
            ekd_all = jnp.exp(bl_all - b_all)
            qs_all = (qt_all * ew_all).astype(BF16)
            qd_all = (qt_all * eb_all).astype(BF16)
            kd_all = (kf_all * ekd_all).astype(BF16)
            v_all = z_ref[rows, 2 * HG_W:3 * HG_W].astype(BF16)
            sgg_all = _sigmoid(g_all)
            t1_all = dy_ref[rows, :] * (g_all * sgg_all)
            db_heads, dqt_heads, dkf_heads, dv_heads, n_heads = [], [], [], [], []
            for h in range(HG_HEADS):
                cs = slice(h * DK, (h + 1) * DK)
                kf, b, ew, eb, ebl, ekd = kf_all[:, cs], b_all[:, cs], ew_all[:, cs], eb_all[:, cs], ebl_all[:, cs], \
                    ekd_all[:, cs]
                qs_b, qd_b, kd_b, v_b = qs_all[:, cs], qd_all[:, cs], kd_all[:, cs], v_all[:, cs]
                st = st_ref[pl.ds(pl.multiple_of((ci * HG_HEADS + h) * DK, DK), DK), :]
                dst = ds_ref[h]
                o = o_ref[rows, cs]
                rstd = lax.rsqrt(jnp.mean(o * o, axis=-1, keepdims=True) + EPS)
                n = o * rstd
                n_heads.append(n)
                t1 = t1_all[:, cs]
                dg_ref[...] += jnp.sum(t1 * n, axis=0, keepdims=True)
                dn = t1 * gain_v
                do = rstd * (dn - n * jnp.mean(dn * n, axis=-1, keepdims=True))
                do_b = do.astype(BF16)
                a, ks = _hg_scores(qs_b, kf, b, row)
                a = jnp.where(tri, a, 0.0)
                dst_b = dst.astype(BF16)
                dqd = _dot(do_b, st.astype(BF16), 1, 0)
                da = jnp.where(tri, _dot(do_b, v_b, 1, 1), 0.0)
                dv = _dot(a.astype(BF16), do_b, 0, 0) + _dot(kd_b, dst_b, 1, 1)
                dkd = _dot(v_b, dst_b, 1, 0)
                ds_ref[h] = dst * ebl + _dot(do_b, qd_b, 0, 0)
                dkd_kd = dkd * kd_b.astype(F32)
                dbl = ebl * jnp.sum(dst * st, axis=0, keepdims=True) + jnp.sum(dkd_kd, axis=0, keepdims=True)
                da_b = da.astype(BF16)
                dqs_parts = []
                dk_in = jnp.zeros((C, DK), F32)
                db_k = jnp.zeros((C, DK), F32)
                for blk in range(C // S):
                    e, k_b = ks[blk]
                    da_blk = da_b[blk * S:(blk + 1) * S]
                    dqs_parts.append(_dot(da_blk, k_b, 1, 0))
                    dks = _dot(da_blk, qs_b[blk * S:(blk + 1) * S], 0, 0)
                    dk_in = dk_in + dks * e
                    db_k = db_k + dks * k_b.astype(F32)
                dqs = jnp.concatenate(dqs_parts, axis=0)
                db = qs_b.astype(F32) * dqs - db_k + dqd * qd_b.astype(F32) - dkd_kd
                db_heads.append(db + jnp.where(row == C - 1, dbl, 0.0))
                dqt_heads.append(dqs * ew + dqd * eb)
                dkf_heads.append(dk_in + dkd * ekd)
                dv_heads.append(dv)
            dlogf = _split_dot(tri_b, jnp.concatenate(db_heads, axis=1), 0, 0)
            dfg = dlogf / fg_all - jnp.concatenate(dkf_heads, axis=1)
            dlb_ref[...] += jnp.sum(dfg * (1.0 - sg_all), axis=0, keepdims=True)
            dqt = jnp.concatenate(dqt_heads, axis=1)
            n_all = jnp.concatenate(n_heads, axis=1)
            gain_all = jnp.concatenate([gain_v] * HG_HEADS, axis=1)
            dz_ref[rows, 0:HG_W] = (dqt * sgq_all * (1.0 + q_all * (1.0 - sgq_all))).astype(BF16)
            dz_ref[rows, HG_W:2 * HG_W] = (dfg * (1.0 - lb_all) * sg_all * (1.0 - sg_all)).astype(BF16)
            dz_ref[rows, 2 * HG_W:3 * HG_W] = jnp.concatenate(dv_heads, axis=1).astype(BF16)
            dz_ref[rows, 3 * HG_W:4 * HG_W] = (dy_ref[rows, :] * n_all * gain_all * sgg_all
                                               * (1.0 + g_all * (1.0 - sgg_all))).astype(BF16)
            return carry

        lax.fori_loop(0, cpb, chunk, 0, unroll=2)

        @pl.when(step == nb - 1)
        def _():
            d0 = dlb_ref[...] * lb_all * (1.0 - lb_all)
            dlg_ref[0:1, :] = d0
            dlg_ref[1:2, :] = -d0

    rev = lambda i: (nb - 1 - i, 0)
    fix = lambda i: (0, 0)
    return _pallas(
        body, name="hgrn_bwd", grid=(nb,),
        in_specs=[pl.BlockSpec((RB, HG_COLS), rev), pl.BlockSpec((RB, HG_W), rev), pl.BlockSpec((RB, HG_W), rev),
                  pl.BlockSpec((cpb * HG_HEADS * DK, DK), rev), pl.BlockSpec((2, HG_W), fix),
                  pl.BlockSpec((1, DK), fix), pl.BlockSpec(memory_space=pl.ANY)],
        out_specs=[pl.BlockSpec((RB, HG_COLS), rev), pl.BlockSpec((2, HG_W), fix), pl.BlockSpec((1, DK), fix)],
        out_shape=[jax.ShapeDtypeStruct(dz_buf.shape, BF16), jax.ShapeDtypeStruct((2, HG_W), F32),
                   jax.ShapeDtypeStruct((1, DK), F32)],
        scratch_shapes=[pltpu.VMEM((HG_HEADS, DK, DK), F32), pltpu.VMEM((1, HG_W), F32)],
        input_output_aliases={6: 0},
        compiler_params=_params(dimension_semantics=("arbitrary",)),
    )(z, o_raw, dy, states, lb_logits, gain, dz_buf)


def _head_ones():
    r, c = _iota((FOX_W, FOX_W), 0), _iota((FOX_W, FOX_W), 1)
    return ((r // FOX_DH) == (c // FOX_DH)).astype(BF16)


def _log_sigmoid(x):
    return jnp.minimum(x, 0.0) - jnp.log(1.0 + jnp.exp(-jnp.abs(x)))


def _fox_prep(z, bias, qg, kg):
    T = z.shape[0]
    tm = min(ROW_BLOCK, T)
    nb = T // tm

    def body(z_ref, b_ref, qg_ref, kg_ref, q_ref, k_ref, v_ref, qa_ref, ka_ref, carry_ref):
        @pl.when(pl.program_id(0) == 0)
        def _():
            carry_ref[...] = jnp.zeros_like(carry_ref)

        ones = _head_ones()
        normed = []
        for src, g_ref in ((0, qg_ref), (1, kg_ref)):
            xv = z_ref[:, src * FOX_W:(src + 1) * FOX_W]
            ms = _split_dot(ones, xv * xv, 1, 0, mat_first=False) * (1.0 / FOX_DH)
            normed.append(xv * lax.rsqrt(ms + EPS) * g_ref[...])
        qn, kn = normed
        q_ref[...] = (qn * FOX_DH ** -0.5).astype(BF16)
        k_b = kn.astype(BF16)
        k_ref[...] = k_b
        v_ref[...] = z_ref[:, 2 * FOX_W:3 * FOX_W].astype(BF16)
        logf = _log_sigmoid(z_ref[:, 3 * FOX_W:FOX_COLS] + b_ref[...])
        r, c = _iota((tm, tm), 0), _iota((tm, tm), 1)
        tri_b = (c <= r).astype(BF16)
        cum = _split_dot(tri_b, logf, 1, 0, terms=3) + carry_ref[...]
        carry_ref[...] = cum[tm - 1:tm]
        c2 = cum * LOG2E
        hi = c2.astype(BF16)
        rem = c2 - hi.astype(F32)
        mid = rem.astype(BF16)
        lo = (rem - mid.astype(F32)).astype(BF16)
        hrow, col = _iota((LANES, 2 * FOX_W), 0), _iota((LANES, 2 * FOX_W), 1)
        base = hrow * LANES + jnp.where(hrow % 2 == 0, FOX_DH, 0)
        placed = None
        for t, part in enumerate((hi, mid, lo)):
            place = jnp.logical_and(col == base + t, hrow < FOX_HEADS).astype(BF16)
            term = _dot(part, place, 1, 0)
            placed = term if placed is None else placed + term
        colw = _iota((tm, 2 * FOX_W), 1)
        head, lane = colw // LANES, colw % LANES
        own = (lane < FOX_DH) == (head % 2 == 0)
        other = jnp.where(head % 2 == 0, lane - FOX_DH, lane)
        ones_q = jnp.where(jnp.logical_and(other >= 0, other < 3), -1.0, 0.0)
        q2 = (qn * (FOX_DH ** -0.5 * LOG2E)).astype(BF16)
        q_exp = jnp.concatenate([q2[:, (h // 2) * LANES:(h // 2 + 1) * LANES] for h in range(FOX_HEADS)], axis=1)
        k_exp = jnp.concatenate([k_b[:, (h // 2) * LANES:(h // 2 + 1) * LANES] for h in range(FOX_HEADS)], axis=1)
        qa_ref[...] = jnp.where(own, q_exp, ones_q.astype(BF16))
        ka_ref[...] = jnp.where(own, k_exp, placed.astype(BF16))

    wide = pl.BlockSpec((tm, 2 * FOX_W), lambda i: (i, 0))
    return _pallas(
        body, name="fox_prep", grid=(nb,),
        in_specs=[pl.BlockSpec((tm, SEG), lambda i: (i, 1)), pl.BlockSpec((1, LANES), lambda i: (0, 0)),
                  pl.BlockSpec((1, FOX_W), lambda i: (0, 0)), pl.BlockSpec((1, FOX_W), lambda i: (0, 0))],
        out_specs=[pl.BlockSpec((tm, FOX_W), lambda i: (i, 0))] * 3 + [wide] * 2,
        out_shape=[jax.ShapeDtypeStruct((T, FOX_W), BF16)] * 3 + [jax.ShapeDtypeStruct((T, 2 * FOX_W), BF16)] * 2,
        scratch_shapes=[pltpu.VMEM((1, LANES), F32)],
        compiler_params=_params(dimension_semantics=("arbitrary",)),
    )(z, bias, qg, kg)


def _fox_fwd(qa, ka, vb, blocks):
    T = qa.shape[0]
    tq = min(ROW_BLOCK, T)
    nq = T // tq
    NEG = -1e30
    n = len(blocks)
    n_pairs = FOX_HEADS // 2

    n_in = 3

    def body(*refs):
        q_ref, k_ref, v_ref = refs[:n_in]
        o_ref, lse_ref = refs[n_in + n:n_in + n + 2]
        m_sc, l_sc, acc_sc = refs[n_in + 2 * n + 2:n_in + 2 * n + 5]
        pr, qi = pl.program_id(0), pl.program_id(1)
        g_start, g_forward, g_finish = _gather_steps(
            refs[n_in:n_in + n], refs[n_in + n + 2:n_in + 2 * n + 2], *refs[n_in + 2 * n + 5:])

        @pl.when(jnp.logical_and(pr == 0, qi == 0))
        def _():
            g_start()

        @pl.when(jnp.logical_and(pr == n_pairs // 2, qi == 0))
        def _():
            g_forward()

        m_sc[...] = jnp.full_like(m_sc, NEG)
        l_sc[...] = jnp.zeros_like(l_sc)
        acc_sc[...] = jnp.zeros_like(acc_sc)
        lane = _iota((tq, LANES), 1)

        def block(masked, ki):
            keys = pl.ds(pl.multiple_of(ki * tq, tq), tq)
            vv = v_ref[keys, :]
            for hh in range(2):
                hs = slice(hh * LANES, (hh + 1) * LANES)
                s = _dot(q_ref[:, hs], k_ref[keys, hs], 1, 1)
                tiles = [s[:, j * LANES:(j + 1) * LANES] for j in range(tq // LANES)]
                if masked:
                    row, col = _iota((tq, LANES), 0), _iota((tq, LANES), 1)
                    tiles = [jnp.where(row >= col + j * LANES, t, NEG) for j, t in enumerate(tiles)]
                m_old = m_sc[hh]
                top = jnp.broadcast_to(jnp.max(functools.reduce(jnp.maximum, tiles), axis=-1, keepdims=True),
                                       (tq, LANES))
                m_new = jnp.maximum(m_old, top)
                alpha = jnp.exp2(m_old - m_new)
                ps = [jnp.exp2(t - m_new) for t in tiles]
                l_sc[hh] = alpha * l_sc[hh] + functools.reduce(jnp.add, ps)
                m_sc[hh] = m_new
                p_b = jnp.concatenate([p.astype(BF16) for p in ps], axis=1)
                acc_sc[hh] = alpha * acc_sc[hh] + _dot(p_b, vv, 1, 0)

        def before(ki, carry):
            block(False, ki)
            return carry

        lax.fori_loop(0, qi, before, 0)
        block(True, qi)
        l0 = jnp.sum(l_sc[0], axis=-1, keepdims=True)
        l1 = jnp.sum(l_sc[1], axis=-1, keepdims=True)
        o_ref[...] = jnp.where(lane < FOX_DH, acc_sc[0] * (1.0 / l0), acc_sc[1] * (1.0 / l1))
        lse_ref[:, :LANES] = m_sc[0] + jnp.log2(l0)
        lse_ref[:, LANES:] = m_sc[1] + jnp.log2(l1)

        @pl.when(jnp.logical_and(pr == n_pairs - 1, qi == nq - 1))
        def _():
            g_finish()

    qmap = lambda p, i: (i, p)
    whole = lambda p, i: (0, p)
    hbm = pl.BlockSpec(memory_space=pl.ANY)
    res = _pallas(
        body, name="fox_fwd", grid=(n_pairs, nq),
        in_specs=[pl.BlockSpec((tq, 2 * LANES), qmap), pl.BlockSpec((T, 2 * LANES), whole),
                  pl.BlockSpec((T, LANES), whole)] + [hbm] * n,
        out_specs=[pl.BlockSpec((tq, LANES), qmap), pl.BlockSpec((tq, 2 * LANES), qmap)] + [hbm] * n,
        out_shape=[jax.ShapeDtypeStruct((T, FOX_W), F32), jax.ShapeDtypeStruct((T, 2 * FOX_W), F32)]
        + [jax.ShapeDtypeStruct((N_DEV,) + b.shape, b.dtype) for b in blocks],
        scratch_shapes=[pltpu.VMEM((2, tq, LANES), F32)] * 3 + _gather_scratch(n),
        compiler_params=_params(dimension_semantics=("arbitrary",) * 2),
    )(qa, ka, vb, *blocks)
    return res[0], res[1], res[2:]


def _fox_bwd(qs, kn, vb, qa, ka, o, do, lse, hs):
    T = qs.shape[0]
    tq = min(ROW_BLOCK, T)
    nq = T // tq
    n = len(hs)
    n_pairs = FOX_HEADS // 2

    def body(*refs):
        q_ref, k_ref, v_ref, qa_ref, ka_ref, o_ref, do_ref, lse_ref = refs[:8]
        dq_ref, dk_ref, dv_ref, dcs_ref, drs_ref = refs[8 + n:13 + n]
        pr, ki = pl.program_id(0), pl.program_id(1)
        x_start, x_finish = _chip_exchange_steps(refs[8:8 + n], refs[13 + n:13 + 2 * n], *refs[13 + 2 * n:])

        @pl.when(jnp.logical_and(pr == 0, ki == 0))
        def _():
            x_start()
            drs_ref[...] = jnp.zeros_like(drs_ref)

        @pl.when(ki == 0)
        def _():
            dq_ref[...] = jnp.zeros_like(dq_ref)

        dk_ref[...] = jnp.zeros_like(dk_ref)
        dv_ref[...] = jnp.zeros_like(dv_ref)
        dcs_ref[...] = jnp.zeros_like(dcs_ref)

        def block(masked, qi):
            lane = _iota((tq, LANES), 1)
            qrows = pl.ds(pl.multiple_of(qi * tq, tq), tq)
            qv, kv, vv = q_ref[qrows, :], k_ref[...], v_ref[...]
            ov, dov = o_ref[qrows, :], do_ref[qrows, :]
            dq_acc = jnp.zeros((tq, LANES), F32)
            dk_acc = jnp.zeros((tq, LANES), F32)
            dv_acc = jnp.zeros((tq, LANES), F32)
            dcs_acc = jnp.zeros((8, tq), F32)
            drs_acc = jnp.zeros((tq, LANES), F32)
            prod = dov * ov
            nt = tq // LANES
            for hh in range(2):
                in_head = (lane < FOX_DH) if hh == 0 else (lane >= FOX_DH)
                hs_ = slice(hh * LANES, (hh + 1) * LANES)
                zb = jnp.zeros_like(qv)
                qm = jnp.where(in_head, qv, zb)
                km = jnp.where(in_head, kv, zb)
                dom = jnp.where(in_head, dov, 0.0).astype(BF16)
                delta_b = jnp.broadcast_to(jnp.sum(jnp.where(in_head, prod, 0.0), axis=1, keepdims=True), (tq, LANES))
                lse_b = lse_ref[qrows, hs_]
                s = _dot(qa_ref[qrows, hs_], ka_ref[:, hs_], 1, 1)
                dp = _dot(dom, vv, 1, 1)
                p_tiles, ds_tiles, col_tiles = [], [], []
                row_part = jnp.zeros((tq, LANES), F32)
                for j in range(nt):
                    js = slice(j * LANES, (j + 1) * LANES)
                    p = jnp.exp2(s[:, js] - lse_b)
                    if masked:
                        p = jnp.where(_iota((tq, LANES), 0) >= _iota((tq, LANES), 1) + j * LANES, p, 0.0)
                    ds = p * (dp[:, js] - delta_b)
                    p_tiles.append(p.astype(BF16))
                    ds_tiles.append(ds.astype(BF16))
                    col_tiles.append(jnp.sum(ds, axis=0, keepdims=True))
                    row_part = row_part + ds
                p_b = jnp.concatenate(p_tiles, axis=1)
                ds_b = jnp.concatenate(ds_tiles, axis=1)
                dv_acc = dv_acc + _dot(p_b, dom, 0, 0)
                dq_acc = dq_acc + _dot(ds_b, km, 1, 0)
                dk_acc = dk_acc + _dot(ds_b, qm, 0, 0)
                dcs_acc = dcs_acc + jnp.where(_iota((8, tq), 0) == hh, jnp.concatenate(col_tiles, axis=1), 0.0)
                rowsum = jnp.sum(row_part, axis=1, keepdims=True)
                drs_acc = drs_acc + jnp.where(lane == 2 * pr + hh, rowsum, 0.0)
            drs_ref[qrows, :] += drs_acc
            dq_ref[qrows, :] += dq_acc
            dk_ref[...] += dk_acc
            dv_ref[...] += dv_acc
            dcs_ref[0] += dcs_acc

        block(True, ki)

        def after(qi, carry):
            block(False, qi)
            return carry

        lax.fori_loop(ki + 1, nq, after, 0)

        @pl.when(jnp.logical_and(pr == n_pairs - 1, ki == nq - 1))
        def _():
            x_finish()

    whole = lambda p, j: (0, p)
    kmap = lambda p, j: (j, p)
    hbm = pl.BlockSpec(memory_space=pl.ANY)
    res = _pallas(
        body, name="fox_bwd", grid=(n_pairs, nq),
        in_specs=[pl.BlockSpec((T, LANES), whole), pl.BlockSpec((tq, LANES), kmap), pl.BlockSpec((tq, LANES), kmap),
                  pl.BlockSpec((T, 2 * LANES), whole), pl.BlockSpec((tq, 2 * LANES), kmap),
                  pl.BlockSpec((T, LANES), whole), pl.BlockSpec((T, LANES), whole), pl.BlockSpec((T, 2 * LANES), whole)]
        + [hbm] * n,
        out_specs=[pl.BlockSpec((T, LANES), whole), pl.BlockSpec((tq, LANES), kmap),
                   pl.BlockSpec((tq, LANES), kmap), pl.BlockSpec((1, 8, tq), lambda p, j: (p, 0, j)),
                   pl.BlockSpec((T, LANES), lambda p, j: (0, 0))] + [hbm] * n,
        out_shape=[jax.ShapeDtypeStruct((T, FOX_W), F32)] * 3
        + [jax.ShapeDtypeStruct((n_pairs, 8, T), F32), jax.ShapeDtypeStruct((T, LANES), F32)]
        + _chip_exchange_shapes(hs),
        scratch_shapes=_chip_exchange_scratch(n),
        compiler_params=_params(dimension_semantics=("arbitrary",) * 2),
    )(qs, kn, vb, qa, ka, o, do, lse, *hs)
    res = list(res)
    return res[:5] + [res[5:]]


def _fox_post(z, dq, dk, dv, dcs, drs, bias, qg, kg, dz_buf):
    T = z.shape[0]
    tm = min(ROW_BLOCK, T)
    nb = T // tm

    def body(z_ref, dq_ref, dk_ref, dv_ref, dcs_ref, drs_ref, b_ref, qg_ref, kg_ref, _buf_ref, dz_ref, dqg_ref, dkg_ref,
             db_ref, carry_ref):
        @pl.when(pl.program_id(0) == 0)
        def _():
            carry_ref[...] = jnp.zeros_like(carry_ref)
            dqg_ref[...] = jnp.zeros_like(dqg_ref)
            dkg_ref[...] = jnp.zeros_like(dkg_ref)
            db_ref[...] = jnp.zeros_like(db_ref)

        ones = _head_ones()
        for src, g_ref, d_ref, dg_ref, scale in ((0, qg_ref, dq_ref, dqg_ref, FOX_DH ** -0.5), (1, kg_ref, dk_ref, dkg_ref, 1.0)):
            xv = z_ref[:, src * FOX_W:(src + 1) * FOX_W]
            ms = _split_dot(ones, xv * xv, 1, 0, mat_first=False) * (1.0 / FOX_DH)
            rstd = lax.rsqrt(ms + EPS)
            xh = xv * rstd
            dn = d_ref[...] * scale
            dg_ref[...] += jnp.sum(dn * xh, axis=0, keepdims=True)
            dxh = dn * g_ref[...]
            mean = _split_dot(ones, dxh * xh, 1, 0, mat_first=False) * (1.0 / FOX_DH)
            dz_ref[:, src * FOX_W:(src + 1) * FOX_W] = (rstd * (dxh - xh * mean)).astype(BF16)
        dz_ref[:, 2 * FOX_W:3 * FOX_W] = dv_ref[...].astype(BF16)
        row8 = _iota((8, tm), 0)
        dct = jnp.zeros((8, tm), F32)
        for h in range(FOX_HEADS):
            src_row = dcs_ref[h // 2][h % 2:h % 2 + 1, :]
            dct = dct + jnp.where(row8 == h, src_row, 0.0)
        dct = drs_ref[...].T[0:8] - dct
        r, c = _iota((tm, tm), 0), _iota((tm, tm), 1)
        upper_b = (r >= c).astype(BF16)
        rc = _split_dot(upper_b, dct, 1, 0, mat_first=False) + carry_ref[...]
        carry_ref[...] = rc[:, 0:1]
        full = jnp.concatenate([rc, jnp.zeros((LANES - 8, tm), F32)], axis=0)
        dlogf = full.T
        xf = z_ref[:, 3 * FOX_W:FOX_COLS] + b_ref[...]
        df = dlogf * (1.0 - _sigmoid(xf))
        dz_ref[:, 3 * FOX_W:FOX_COLS] = df.astype(BF16)
        dz_ref[:, FOX_COLS:] = jnp.zeros((tm, SEG - FOX_COLS), BF16)
        db_ref[...] += jnp.sum(df, axis=0, keepdims=True)

    rev = lambda i: (nb - 1 - i, 0)
    fix2 = lambda i: (0, 0)
    return _pallas(
        body, name="fox_post", grid=(nb,),
        in_specs=[pl.BlockSpec((tm, SEG), lambda i: (nb - 1 - i, 1)), pl.BlockSpec((tm, FOX_W), rev),
                  pl.BlockSpec((tm, FOX_W), rev),
                  pl.BlockSpec((tm, FOX_W), rev), pl.BlockSpec((FOX_HEADS // 2, 8, tm), lambda i: (0, 0, nb - 1 - i)),
                  pl.BlockSpec((tm, LANES), rev),
                  pl.BlockSpec((1, LANES), fix2), pl.BlockSpec((1, FOX_W), fix2), pl.BlockSpec((1, FOX_W), fix2),
                  pl.BlockSpec(memory_space=pl.ANY)],
        out_specs=[pl.BlockSpec((tm, SEG), lambda i: (nb - 1 - i, 1)), pl.BlockSpec((1, FOX_W), fix2),
                   pl.BlockSpec((1, FOX_W), fix2), pl.BlockSpec((1, LANES), fix2)],
        out_shape=[jax.ShapeDtypeStruct(dz_buf.shape, BF16), jax.ShapeDtypeStruct((1, FOX_W), F32),
                   jax.ShapeDtypeStruct((1, FOX_W), F32), jax.ShapeDtypeStruct((1, LANES), F32)],
        scratch_shapes=[pltpu.VMEM((8, 1), F32)],
        input_output_aliases={9: 0},
        compiler_params=_params(dimension_semantics=("arbitrary",)),
    )(z, dq, dk, dv, dcs, drs, bias, qg, kg, dz_buf)


def _local_step(x, p, tgt, sm, chunks, core):
    lbl, og, fb = sm["hg_lb_logits"], sm["hg_onorm_g"], sm["fox_f_bias"]
    fbias = jnp.pad(fb, ((0, 0), (0, LANES - FOX_HEADS)))
    qg = jnp.tile(sm["fox_q_norm_g"], (1, FOX_HEADS))
    kg = jnp.tile(sm["fox_k_norm_g"], (1, FOX_HEADS))

    assert BIG[0] == "w_in"
    h, got0 = _rms_fwd(x, sm["norm_mix_g"], chunks[:1], name="rms_mix")
    W = {"w_in": _full_of_chunks("w_in", got0[0])}
    rest = dict(zip(BIG[1:], chunks[1:]))
    first, second = ["w_ffn_gate"], ["w_ffn_up"]
    third = [n for n in BIG[1:] if n not in first + second]
    z, got1 = _matmul(h, W["w_in"], tb=True, gather=[rest[n] for n in first], name="mm_z")
    o_raw, ya, states, got2 = _hgrn_fwd(z, lbl, og, [rest[n] for n in second])
    qs, kn, vb, qa, ka = _fox_prep(z, fbias, qg, kg)
    yb, lse, got3 = _fox_fwd(qa, ka, vb, [rest[n] for n in third])
    W = dict(W, **{n: _full_of_chunks(n, g)
                   for n, g in zip(first + second + third, list(got1) + list(got2) + list(got3))})
    merged, ua, ub = _merge_fwd(ya, yb, W["w_branch_a"], W["w_branch_b"], z)
    x1, hf = _matmul(merged, W["w_out"], add=x, norm_fwd=sm["norm_ffn_g"], name="mm_x1")
    a, b, act = _swiglu_fwd(hf, W["w_ffn_gate"], W["w_ffn_up"])
    x2 = _matmul(act, W["w_ffn_down"], add=x1, name="mm_x2")
    hp, dy, dsp, dpp, loss = _ple_loss(x2, p, sm["norm_ple_g"], W["w_ple_gate"], W["w_ple_proj"], tgt)

    G = {}
    G["w_ple_proj"] = _matmul(p, dpp, ta=True, out_dtype=BF16, name="mm_dw_ple_proj")
    G["w_ple_gate"] = _matmul(hp, dsp, ta=True, out_dtype=BF16, name="mm_dw_ple_gate")
    dx2, d_ple_g = _matmul(dsp, W["w_ple_gate"], tb=True, norm_bwd=(x2, sm["norm_ple_g"], dy), name="mm_dx2")
    G["w_ffn_down"] = _matmul(act, dx2, ta=True, out_dtype=BF16, name="mm_dw_ffn_down")
    da, db = _swiglu_bwd(dx2, W["w_ffn_down"], a, b)
    G["w_ffn_gate"] = _matmul(da, hf, ta=True, out_dtype=BF16, name="mm_dw_ffn_gate")
    G["w_ffn_up"] = _matmul(db, hf, ta=True, out_dtype=BF16, name="mm_dw_ffn_up")
    dhf = _matmul(da, W["w_ffn_gate"], name="mm_dhf_a")
    dx1, d_ffn_g = _matmul(db, W["w_ffn_up"], add=dhf, norm_bwd=(x1, sm["norm_ffn_g"], dx2), name="mm_dx1")
    G["w_out"] = _matmul(merged, dx1, ta=True, out_dtype=BF16, name="mm_dw_out")
    dua, dub, dz, dya, dyb = _merge_bwd(dx1, W["w_out"], W["w_branch_a"], W["w_branch_b"], ua, ub, z)
    G["w_branch_a"] = _matmul(ya, dua, ta=True, out_dtype=BF16, name="mm_dw_branch_a")
    G["w_branch_b"] = _matmul(yb, dub, ta=True, out_dtype=BF16, name="mm_dw_branch_b")
    hb_rest = _sibling_sums({n: G[n] for n in BIG[1:]}, core, tag="rest")
    dq, dk, dv, dcs, drs, got_rest = _fox_bwd(qs, kn, vb, qa, ka, yb, dyb, lse, hb_rest)
    dz, d_qg, d_kg, d_fb = _fox_post(z, dq, dk, dv, dcs, drs, fbias, qg, kg, dz)
    dz, d_lbl, d_og = _hgrn_bwd(z, o_raw, dya, states, lbl, og, dz)
    G["w_in"] = _matmul(dz, h, ta=True, out_dtype=BF16, name="mm_dw_in")
    hb_in = _sibling_sums({"w_in": G["w_in"]}, core, tag="w_in")
    (grad_x, d_mix_g), got_in = _matmul(dz, W["w_in"], exchange=hb_in,
                                        norm_bwd=(x, sm["norm_mix_g"], dx1), name="mm_dx")

    gs = {"norm_mix_g": d_mix_g, "hg_lb_logits": d_lbl, "hg_onorm_g": d_og, "fox_f_bias": d_fb[:, :FOX_HEADS],
          "fox_q_norm_g": d_qg.reshape(FOX_HEADS, FOX_DH).sum(0, keepdims=True),
          "fox_k_norm_g": d_kg.reshape(FOX_HEADS, FOX_DH).sum(0, keepdims=True),
          "norm_ffn_g": d_ffn_g, "norm_ple_g": d_ple_g}
    return loss, grad_x, gs, hb_in + hb_rest, list(got_in) + list(got_rest)


def _pack_rows(parts, total):
    buf = jnp.concatenate(parts, axis=-2)
    pad = total - buf.shape[-2]
    widths = [(0, 0)] * (buf.ndim - 2) + [(0, pad), (0, 0)]
    return jnp.pad(buf, widths)


def _chunk_of_shard(n, w):
    if n == "w_in":
        return jnp.pad(w, ((0, IN_SHARD_PAD - IN_SHARD), (0, 0)))
    return w


def _full_of_chunks(n, g):
    _, a, b = g.shape
    if n == "w_in":
        w = g[:, :IN_SHARD].reshape(IN_COLS, b)
        gap = jnp.zeros((SEG - FOX_LOGICAL, b), g.dtype)
        return jnp.concatenate([w[:HG_COLS + FOX_LOGICAL], gap, w[HG_COLS + FOX_LOGICAL:]], axis=0)
    if BIG_SHAPE[n][2] == 0:
        return g.reshape(N_DEV * a, b)
    return g.transpose(1, 0, 2).reshape(a, N_DEV * b)


def _chunks_of_full(n, g):
    if n == "w_in":
        w = jnp.concatenate([g[:HG_COLS + FOX_LOGICAL], g[2 * SEG:]], axis=0).reshape(N_DEV, IN_SHARD, g.shape[1])
        return jnp.pad(w, ((0, 0), (0, IN_SHARD_PAD - IN_SHARD), (0, 0)))
    if BIG_SHAPE[n][2] == 0:
        return g.reshape(N_DEV, g.shape[0] // N_DEV, g.shape[1])
    return g.reshape(g.shape[0], N_DEV, g.shape[1] // N_DEV).transpose(1, 0, 2)


def _pack_small(vals, loss_row=None):
    parts = [vals[n].reshape(SMALL_ROWS[n], -1) for n in SMALL]
    parts = [jnp.pad(v, ((0, 0), (0, LANES - v.shape[1]))) for v in parts]
    if loss_row is not None:
        parts.append(loss_row)
    return _pack_rows(parts, SMALL_TOTAL)


def _unpack_small(buf, like):
    out, r0 = {}, 0
    for n in SMALL:
        rows, size = SMALL_ROWS[n], like[n].size
        blk = buf[r0:r0 + rows]
        out[n] = (blk if size == rows * LANES else blk[:, :size]).reshape(like[n].shape)
        r0 += rows
    return out


def _place():
    return lax.axis_index("x"), lax.axis_index("y"), lax.axis_index("c")


def _gather_steps(x_refs, out_refs, send_sems, recv_sems, local_sems):
    n = len(x_refs)
    x, y, c = _place()
    me, sibling = (x, y, c), (x, y, 1 - c)
    chips = [(1 - x, y), (x, 1 - y), (1 - x, 1 - y)]

    def slot(i, px, py, pc):
        return out_refs[i].at[4 * px + 2 * py + pc]

    def copy(k, i, blk, to, own=False):
        return pltpu.make_async_remote_copy(
            src_ref=x_refs[i] if own else slot(i, *blk), dst_ref=slot(i, *blk),
            send_sem=send_sems.at[k, i], recv_sem=recv_sems.at[k, i], device_id=to, device_id_type=MESH)

    def mine():
        return [pltpu.make_async_copy(x_refs[i], slot(i, *me), local_sems.at[i]) for i in range(n)]

    def first():
        cps = [copy(0, i, me, sibling, own=True) for i in range(n)]
        return cps + [copy(1 + j, i, me, (*chip, c), own=True) for j, chip in enumerate(chips) for i in range(n)]

    def passed():
        return [copy(4 + j, i, (*chip, c), sibling) for j, chip in enumerate(chips) for i in range(n)]

    def start():
        for cp in mine() + first():
            cp.start()

    def forward():
        fws = passed()
        for j, chip in enumerate(chips):
            for i in range(n):
                copy(1 + j, i, (*chip, c), me).wait_recv()
                fws[j * n + i].start()

    def finish():
        for i in range(n):
            copy(0, i, sibling, me).wait_recv()
        for j, chip in enumerate(chips):
            for i in range(n):
                copy(4 + j, i, (*chip, 1 - c), me).wait_recv()
        for cp in first() + passed():
            cp.wait_send()
        for cp in mine():
            cp.wait()

    return start, forward, finish


def _gather_scratch(n):
    return [pltpu.SemaphoreType.DMA((7, n)), pltpu.SemaphoreType.DMA((7, n)), pltpu.SemaphoreType.DMA((n,))]


def _sibling_exchange(gs, *, name):
    n = len(gs)

    def body(*refs):
        g_refs, out_refs = refs[:n], refs[n:2 * n]
        send_sems, recv_sems = refs[2 * n:]
        x, y, c = _place()
        cps = [pltpu.make_async_remote_copy(
            src_ref=g_refs[i].at[:, pl.ds(1 - c, 1)], dst_ref=out_refs[i], send_sem=send_sems.at[i],
            recv_sem=recv_sems.at[i], device_id=(x, y, 1 - c), device_id_type=MESH) for i in range(n)]
        for cp in cps:
            cp.start()
        for cp in cps:
            cp.wait()

    hbm = pl.BlockSpec(memory_space=pl.ANY)
    return _pallas(
        body, name=name, out_shape=[jax.ShapeDtypeStruct((N_CHIP, 1) + g.shape[2:], g.dtype) for g in gs],
        in_specs=[hbm] * n, out_specs=[hbm] * n,
        scratch_shapes=[pltpu.SemaphoreType.DMA((n,)), pltpu.SemaphoreType.DMA((n,))],
    )(*gs)


def _chip_sum(g4s, gots, core, *, name):
    n = len(g4s)

    def body(c_ref, *refs):
        for g_ref, r_ref, h_ref in zip(refs[:n], refs[n:2 * n], refs[2 * n:]):
            h_ref[0] = (g_ref[0, 0].astype(F32) + r_ref[0, 0].astype(F32)).astype(BF16)

    shapes = [g.shape[2:] for g in g4s]
    grid_spec = pltpu.PrefetchScalarGridSpec(
        num_scalar_prefetch=1, grid=(N_CHIP,),
        in_specs=[pl.BlockSpec((1, 1) + s, lambda j, c: (j, c[0], 0, 0)) for s in shapes]
        + [pl.BlockSpec((1, 1) + s, lambda j, c: (j, 0, 0, 0)) for s in shapes],
        out_specs=[pl.BlockSpec((1,) + s, lambda j, c: (j, 0, 0)) for s in shapes])
    return list(_pallas(
        body, name=name, grid_spec=grid_spec, out_shape=[jax.ShapeDtypeStruct((N_CHIP,) + s, BF16) for s in shapes],
        compiler_params=_params(dimension_semantics=("arbitrary",)),
    )(core, *g4s, *gots))


def _chip_exchange_steps(h_refs, out_refs, send_sems, recv_sems):
    n = len(h_refs)
    x, y, c = _place()
    chips = [(1 - x, y), (x, 1 - y), (1 - x, 1 - y)]

    def copies():
        return [pltpu.make_async_remote_copy(
            src_ref=h_refs[i].at[2 * px + py], dst_ref=out_refs[i].at[k], send_sem=send_sems.at[k, i],
            recv_sem=recv_sems.at[k, i], device_id=(px, py, c), device_id_type=MESH)
            for k, (px, py) in enumerate(chips) for i in range(n)]

    def start():
        for cp in copies():
            cp.start()

    def finish():
        for cp in copies():
            cp.wait()

    return start, finish


def _chip_exchange_shapes(hs):
    return [jax.ShapeDtypeStruct((3,) + h.shape[1:], h.dtype) for h in hs]


def _chip_exchange_scratch(n):
    return [pltpu.SemaphoreType.DMA((3, n)), pltpu.SemaphoreType.DMA((3, n))]


def _sibling_sums(G, core, *, tag):
    g4 = []
    for n, g in G.items():
        gc = _chunks_of_full(n, g)
        g4.append(gc.reshape((N_CHIP, 2) + gc.shape[1:]))
    got = _sibling_exchange(g4, name="grads_to_sibling_" + tag)
    return _chip_sum(g4, got, core, name="chip_sum_" + tag)


def _adam_math(w, g, m, v):
    m = ADAM_B1 * m + (1.0 - ADAM_B1) * g
    v = ADAM_B2 * v + (1.0 - ADAM_B2) * (g * g)
    m_hat = m / (1.0 - ADAM_B1 ** ADAM_STEP)
    v_hat = v / (1.0 - ADAM_B2 ** ADAM_STEP)
    delta = -ADAM_LR * (m_hat / (jnp.sqrt(v_hat) + ADAM_EPS) + ADAM_WD * w)
    return delta, m, v


def _adam_shard(hb, got, chip, w, m, v, *, name):
    _, r, c = w.shape
    _, a, cb = hb.shape
    assert cb == c and c % LANES == 0, (hb.shape, w.shape)
    tc = _pick(c, 2 * LANES)

    def body(j_ref, h_ref, r_ref, w_ref, m_ref, v_ref, g_ref, d_ref, nm_ref, nv_ref):
        parts = [h_ref[0], r_ref[0], r_ref[1], r_ref[2]]
        g = None
        for part in parts:
            part = part[:r].astype(F32)
            g = part if g is None else g + part
        d, nm, nv = _adam_math(w_ref[0], g, m_ref[0], v_ref[0])
        g_ref[0] = g
        d_ref[0] = d
        nm_ref[0] = nm
        nv_ref[0] = nv

    blk = pl.BlockSpec((1, r, tc), lambda i, j: (0, 0, i))
    grid_spec = pltpu.PrefetchScalarGridSpec(
        num_scalar_prefetch=1, grid=(c // tc,),
        in_specs=[pl.BlockSpec((1, a, tc), lambda i, j: (j[0], 0, i)),
                  pl.BlockSpec((3, a, tc), lambda i, j: (0, 0, i)), blk, blk, blk],
        out_specs=[blk] * 4)
    return _pallas(
        body, name=name, grid_spec=grid_spec, out_shape=[jax.ShapeDtypeStruct((1, r, c), F32)] * 4,
        compiler_params=_params(dimension_semantics=("arbitrary",)),
    )(chip, hb, got, w, m, v)


def _small_all_reduce_adam(gs, w, m, v):
    def body(g_ref, w_ref, m_ref, v_ref, sum_ref, d_ref, nm_ref, nv_ref, gather, send_sems, recv_sems):
        x, y, c = _place()
        my = 4 * x + 2 * y + c
        gather[my] = g_ref[...]
        cps = []
        for k in range(1, N_DEV):
            to = (x ^ (k >> 2), y ^ ((k >> 1) & 1), c ^ (k & 1))
            cps.append(pltpu.make_async_remote_copy(
                src_ref=g_ref, dst_ref=gather.at[my], send_sem=send_sems.at[k - 1], recv_sem=recv_sems.at[k - 1],
                device_id=to, device_id_type=MESH))
        for cp in cps:
            cp.start()
        for cp in cps:
            cp.wait()
        total = gather[0]
        for d in range(1, N_DEV):
            total = total + gather[d]
        dlt, nm, nv = _adam_math(w_ref[...], total, m_ref[...], v_ref[...])
        sum_ref[...] = total
        d_ref[...] = dlt
        nm_ref[...] = nm
        nv_ref[...] = nv

    vm = pl.BlockSpec(memory_space=pltpu.VMEM)
    return _pallas(
        body, name="small_all_reduce_adam", out_shape=[jax.ShapeDtypeStruct((SMALL_TOTAL, LANES), F32)] * 4,
        in_specs=[vm] * 4, out_specs=[vm] * 4,
        scratch_shapes=[pltpu.VMEM((N_DEV, SMALL_TOTAL, LANES), F32), pltpu.SemaphoreType.DMA((7,)),
                        pltpu.SemaphoreType.DMA((7,))],
            )(gs, w, m, v)


def kernel(x, p, norm_mix_g, w_in, hg_lb_logits, hg_onorm_g, fox_f_bias, fox_q_norm_g, fox_k_norm_g, w_branch_a, w_branch_b, w_out, norm_ffn_g, w_ffn_gate, w_ffn_up, w_ffn_down, norm_ple_g, w_ple_gate, w_ple_proj, loss_target, m_norm_mix_g, m_w_in, m_hg_lb_logits, m_hg_onorm_g, m_fox_f_bias, m_fox_q_norm_g, m_fox_k_norm_g, m_w_branch_a, m_w_branch_b, m_w_out, m_norm_ffn_g, m_w_ffn_gate, m_w_ffn_up, m_w_ffn_down, m_norm_ple_g, m_w_ple_gate, m_w_ple_proj, v_norm_mix_g, v_w_in, v_hg_lb_logits, v_hg_onorm_g, v_fox_f_bias, v_fox_q_norm_g, v_fox_k_norm_g, v_w_branch_a, v_w_branch_b, v_w_out, v_norm_ffn_g, v_w_ffn_gate, v_w_ffn_up, v_w_ffn_down, v_norm_ple_g, v_w_ple_gate, v_w_ple_proj):
    args = dict(locals())
    wts = {n: args[n] for n in BIG + SMALL}
    mom = {n: args["m_" + n] for n in BIG + SMALL}
    var = {n: args["v_" + n] for n in BIG + SMALL}
    for group in (wts, mom, var):
        for n in TRANSPOSED:
            group[n] = jnp.swapaxes(group[n], 1, 2)
    sm = {n: wts[n] for n in SMALL}

    xi, yi, ci = _place()
    core = jnp.reshape(ci, (1,)).astype(jnp.int32)
    chip = jnp.reshape(2 * xi + yi, (1,)).astype(jnp.int32)
    chunks = [_chunk_of_shard(n, wts[n][0].astype(BF16)) for n in BIG]
    loss_blk, grad_x, gs, hb, got = _local_step(x[0], p[0, 0], loss_target[0], sm, chunks, core)

    g_big, d_big, nm_big, nv_big = {}, {}, {}, {}
    for n, h, r in zip(BIG, hb, got):
        res = _adam_shard(h, r, chip, wts[n], mom[n], var[n], name="adam_" + n)
        if n in TRANSPOSED:
            res = [jnp.swapaxes(t, 1, 2) for t in res]
        g_big[n], d_big[n], nm_big[n], nv_big[n] = res

    s_sum, s_d, s_nm, s_nv = _small_all_reduce_adam(
        _pack_small(gs, loss_blk[0:1]), _pack_small(sm), _pack_small({n: mom[n] for n in SMALL}),
        _pack_small({n: var[n] for n in SMALL}))
    loss = s_sum[LOSS_ROW, 0]
    g_small, d_small, nm_small, nv_small = (_unpack_small(t, sm) for t in (s_sum, s_d, s_nm, s_nv))

    order = ["norm_mix_g", "w_in", "hg_lb_logits", "hg_onorm_g", "fox_f_bias", "fox_q_norm_g", "fox_k_norm_g",
             "w_branch_a", "w_branch_b", "w_out", "norm_ffn_g", "w_ffn_gate", "w_ffn_up", "w_ffn_down", "norm_ple_g",
             "w_ple_gate", "w_ple_proj"]
    outs = [loss, grad_x[None]]
    for big, small in ((g_big, g_small), (d_big, d_small), (nm_big, nm_small), (nv_big, nv_small)):
        outs += [big[n] if n in big else small[n] for n in order]
    return tuple(outs)
```

```python
import functools

import jax
import jax.numpy as jnp
from jax import lax
from jax.experimental import pallas as pl
from jax.experimental.pallas import tpu as pltpu

F32 = jnp.float32
BF16 = jnp.bfloat16

D_MODEL = 1024
PLE_DIM = 256
HG_HEADS = 4
HG_DK = 128
HG_CHUNK = 64
HG_SUB = 16
HG_W = HG_HEADS * HG_DK
FOX_HEADS = 8
FOX_DH = 64
FOX_W = FOX_HEADS * FOX_DH
D_FF = 2816
EPS = 1e-6
N_DEV = 8
N_CHIP = 4
LANES = 128
FOX_COLS = 3 * FOX_W + LANES
HG_COLS = 4 * HG_W
GATE_COLS = 2 * D_MODEL
IN_COLS = HG_COLS + 3 * FOX_W + FOX_HEADS + GATE_COLS
FOX_LOGICAL = 3 * FOX_W + FOX_HEADS
SEG = 2048
IN_PAD = 3 * SEG
IN_SHARD = IN_COLS // N_DEV
IN_SHARD_PAD = 720
EXP_CLAMP = 80.0
LOG2E = 1.4426950408889634

ADAM_LR = 0.001
ADAM_B1 = 0.9
ADAM_B2 = 0.999
ADAM_EPS = 1e-08
ADAM_WD = 0.01
ADAM_STEP = 10

MESH = pl.DeviceIdType.MESH
VMEM_LIMIT = 56 * 1024 * 1024
ROW_BLOCK = 512
MATMUL_TN = 2048
MATMUL_TK = 3072
MATMUL_TM_T = 1408
FFN_TILE = 1408

BIG = ["w_in", "w_branch_a", "w_branch_b", "w_out", "w_ffn_gate", "w_ffn_up", "w_ffn_down",
       "w_ple_gate", "w_ple_proj"]
TRANSPOSED = ("w_in", "w_ffn_gate", "w_ffn_up")
BIG_SHAPE = {
    "w_in": (IN_COLS, D_MODEL, 0), "w_branch_a": (HG_W, D_MODEL, 1), "w_branch_b": (FOX_W, D_MODEL, 1),
    "w_out": (D_MODEL, D_MODEL, 0), "w_ffn_gate": (D_FF, D_MODEL, 0), "w_ffn_up": (D_FF, D_MODEL, 0),
    "w_ffn_down": (D_FF, D_MODEL, 0), "w_ple_gate": (D_MODEL, D_MODEL, 0), "w_ple_proj": (PLE_DIM, D_MODEL, 1),
}

SMALL = ["norm_mix_g", "hg_lb_logits", "hg_onorm_g", "fox_f_bias", "fox_q_norm_g", "fox_k_norm_g",
         "norm_ffn_g", "norm_ple_g"]
SMALL_ROWS = {"norm_mix_g": 8, "hg_lb_logits": 8, "hg_onorm_g": 1, "fox_f_bias": 1, "fox_q_norm_g": 1,
              "fox_k_norm_g": 1, "norm_ffn_g": 8, "norm_ple_g": 8}
SMALL_TOTAL = 40
LOSS_ROW = 36


def _pallas(body, **kw):
    return pl.pallas_call(body, **kw)


def _params(**kw):
    return pltpu.CompilerParams(vmem_limit_bytes=VMEM_LIMIT, **kw)


def _pick(n, target):
    if n <= target:
        return n
    best = None
    for t in range(LANES, target + 1, LANES):
        if n % t == 0:
            best = t
    assert best is not None, (n, target)
    return best


def _dot(a, b, ca, cb):
    return lax.dot_general(a, b, (((ca,), (cb,)), ((), ())), preferred_element_type=F32)


def _split_dot(mat, x, ca, cb, terms=2, mat_first=True):
    acc = None
    rem = x
    for _ in range(terms):
        part = rem.astype(BF16)
        rem = rem - part.astype(F32)
        p = _dot(mat, part, ca, cb) if mat_first else _dot(part, mat, ca, cb)
        acc = p if acc is None else acc + p
    return acc


def _sigmoid(x):
    return 1.0 / (1.0 + jnp.exp(-x))


def _iota(shape, dim):
    return lax.broadcasted_iota(jnp.int32, shape, dim)


def _matmul(a, b, *, name, ta=False, tb=False, out_dtype=F32, add=None, exchange=None, gather=None,
            norm_fwd=None, norm_bwd=None):
    assert exchange is None or gather is None
    (K, M) = a.shape if ta else a.shape[::-1]
    (N, Kb) = b.shape if tb else b.shape[::-1]
    assert K == Kb, (a.shape, b.shape, ta, tb)
    if ta:
        tm, tn, tk = _pick(M, MATMUL_TM_T), _pick(N, 2 * ROW_BLOCK), _pick(K, 4 * ROW_BLOCK)
    else:
        tm, tn, tk = _pick(M, 2 * ROW_BLOCK), _pick(N, MATMUL_TN), _pick(K, MATMUL_TK)
    if norm_bwd is not None:
        tm = _pick(M, ROW_BLOCK)
    nk = K // tk
    use_scratch = nk > 1 and out_dtype != F32
    if norm_fwd is not None or norm_bwd is not None:
        assert tn == N and not use_scratch and out_dtype == F32

    hs = list(exchange or gather or [])
    n_x = len(hs)
    grid = (M // tm, N // tn, nk)
    a_spec = pl.BlockSpec((tk, tm), lambda i, j, k: (k, i)) if ta else pl.BlockSpec((tm, tk), lambda i, j, k: (i, k))
    b_spec = pl.BlockSpec((tn, tk), lambda i, j, k: (j, k)) if tb else pl.BlockSpec((tk, tn), lambda i, j, k: (k, j))
    o_spec = pl.BlockSpec((tm, tn), lambda i, j, k: (i, j))
    row_vec = pl.BlockSpec((1, N), lambda i, j, k: (0, 0))
    hbm = pl.BlockSpec(memory_space=pl.ANY)
    extra_in = [(add, o_spec)] if add is not None else []
    extra_out = []
    if norm_fwd is not None:
        extra_in += [(norm_fwd, row_vec)]
        extra_out += [(jax.ShapeDtypeStruct((M, N), BF16), o_spec)]
    if norm_bwd is not None:
        extra_in += [(norm_bwd[0], o_spec), (norm_bwd[1], row_vec), (norm_bwd[2], o_spec)]
        extra_out += [(jax.ShapeDtypeStruct((1, N), F32), row_vec)]
    if gather is not None:
        ride_shapes, ride_scratch = [jax.ShapeDtypeStruct((N_DEV,) + h.shape, h.dtype) for h in hs], _gather_scratch(n_x)
    else:
        ride_shapes, ride_scratch = _chip_exchange_shapes(hs), (_chip_exchange_scratch(n_x) if n_x else [])
    n_ex_in, n_ex_out = len(extra_in), len(extra_out)

    def body(*refs):
        refs = list(refs)
        a_ref, b_ref = refs[:2]
        ex_in = refs[2:2 + n_ex_in]
        ride_in = refs[2 + n_ex_in:2 + n_ex_in + n_x]
        base = 2 + n_ex_in + n_x
        o_ref = refs[base]
        ex_out = refs[base + 1:base + 1 + n_ex_out]
        ride_out = refs[base + 1 + n_ex_out:base + 1 + n_ex_out + n_x]
        scratch = refs[base + 1 + n_ex_out + n_x:]
        at = [pl.program_id(d) for d in range(3)]
        k = at[2]
        if n_x:
            steps = _gather_steps if gather is not None else _chip_exchange_steps
            ride = steps(ride_in, ride_out, *scratch[-len(ride_scratch):])

            @pl.when(jnp.logical_and(at[0] == 0, jnp.logical_and(at[1] == 0, at[2] == 0)))
            def _():
                ride[0]()
        p = _dot(a_ref[...].astype(BF16), b_ref[...].astype(BF16), 0 if ta else 1, 1 if tb else 0)

        def finish(r):
            ins = list(ex_in)
            outs = list(ex_out)
            if add is not None:
                r = r + ins.pop(0)[...].astype(F32)
            if norm_fwd is not None:
                g_ref = ins.pop(0)
                rstd = lax.rsqrt(jnp.mean(r * r, axis=-1, keepdims=True) + EPS)
                outs.pop(0)[...] = (r * rstd * g_ref[...]).astype(BF16)
            if norm_bwd is not None:
                x_ref, g_ref, dres_ref = ins.pop(0), ins.pop(0), ins.pop(0)
                dg_ref = outs.pop(0)
                xv = x_ref[...]
                rstd = lax.rsqrt(jnp.mean(xv * xv, axis=-1, keepdims=True) + EPS)
                xh = xv * rstd
                part = jnp.sum(r * xh, axis=0, keepdims=True)

                @pl.when(at[0] == 0)
                def _():
                    dg_ref[...] = part

                @pl.when(at[0] > 0)
                def _():
                    dg_ref[...] += part

                dxh = r * g_ref[...]
                r = rstd * (dxh - xh * jnp.mean(dxh * xh, axis=-1, keepdims=True)) + dres_ref[...]
            o_ref[...] = r.astype(out_dtype)

        if nk == 1:
            finish(p)
        elif not use_scratch:
            @pl.when(k == 0)
            def _():
                o_ref[...] = p

            @pl.when(jnp.logical_and(k > 0, k < nk - 1))
            def _():
                o_ref[...] += p

            @pl.when(k == nk - 1)
            def _():
                finish(o_ref[...] + p)
        else:
            acc_ref = scratch[0]

            @pl.when(k == 0)
            def _():
                acc_ref[...] = p

            @pl.when(k > 0)
            def _():
                acc_ref[...] += p

            @pl.when(k == nk - 1)
            def _():
                finish(acc_ref[...])

        if n_x:
            @pl.when(jnp.logical_and(at[0] == grid[0] - 1, jnp.logical_and(at[1] == grid[1] - 1, at[2] == nk - 1)))
            def _():
                for step in ride[1:]:
                    step()

    res = _pallas(
        body, name=name, grid=grid,
        in_specs=[a_spec, b_spec] + [s for _, s in extra_in] + [hbm] * n_x,
        out_specs=[o_spec] + [s for _, s in extra_out] + [hbm] * n_x,
        out_shape=[jax.ShapeDtypeStruct((M, N), out_dtype)] + [s for s, _ in extra_out] + ride_shapes,
        scratch_shapes=([pltpu.VMEM((tm, tn), F32)] if use_scratch else []) + ride_scratch,
        compiler_params=_params(dimension_semantics=("arbitrary",) * 3),
    )(a, b, *[v for v, _ in extra_in], *hs)
    res = list(res)
    main = res[0] if n_ex_out == 0 else tuple(res[:1 + n_ex_out])
    return (main, res[1 + n_ex_out:]) if n_x else main


def _row_map(nb, reverse, seg):
    if reverse:
        return lambda i: (nb - 1 - i, seg)
    return lambda i: (i, seg)


def _row_call(body, *, name, T, ins, outs, acc_outs=(), tm=ROW_BLOCK, reverse=False, swap=None):
    tm = min(tm, T)
    nb = T // tm
    in_specs, args = [], []
    for arr, how in ins:
        args.append(arr)
        if how is True:
            in_specs.append(pl.BlockSpec((tm, arr.shape[1]), _row_map(nb, reverse, 0)))
        elif how is False:
            in_specs.append(pl.BlockSpec(arr.shape, lambda i, _n=arr.ndim: (0,) * _n))
        else:
            in_specs.append(pl.BlockSpec((tm, SEG), _row_map(nb, reverse, how[0])))
    out_specs, out_shape = [], []
    for o in outs:
        c, dt = o[0], o[1]
        total, seg = o[2] if len(o) > 2 else (c, 0)
        out_specs.append(pl.BlockSpec((tm, c), _row_map(nb, reverse, seg)))
        out_shape.append(jax.ShapeDtypeStruct((T, total), dt))
    for shp, dt in acc_outs:
        out_specs.append(pl.BlockSpec(shp, lambda i, _n=len(shp): (0,) * _n))
        out_shape.append(jax.ShapeDtypeStruct(shp, dt))
    if swap is None:
        return _pallas(body, name=name, grid=(nb,), in_specs=in_specs, out_specs=out_specs, out_shape=out_shape,
                       compiler_params=_params(dimension_semantics=("arbitrary",)))(*args)
    n, n_in, n_out = len(swap), len(in_specs), len(out_specs)
    hbm = pl.BlockSpec(memory_space=pl.ANY)

    def riding(*refs):
        mine = refs[:n_in] + refs[n_in + n:n_in + n + n_out]
        start, finish = _sibling_steps(refs[n_in:n_in + n], refs[n_in + n + n_out:n_in + 2 * n + n_out], *refs[-2:])

        @pl.when(pl.program_id(0) == 0)
        def _():
            start()

        body(*mine)

        @pl.when(pl.program_id(0) == nb - 1)
        def _():
            finish()

    res = list(_pallas(
        riding, name=name, grid=(nb,), in_specs=in_specs + [hbm] * n, out_specs=out_specs + [hbm] * n,
        out_shape=out_shape + _sibling_shapes(swap), scratch_shapes=_sibling_scratch(n),
        compiler_params=_params(dimension_semantics=("arbitrary",)))(*args, *swap))
    return res[:n_out], res[n_out:]


def _rms_fwd(x, g, blocks, *, name):
    T, D = x.shape
    tm = min(ROW_BLOCK, T)
    nb = T // tm
    n = len(blocks)

    def body(*refs):
        x_ref, g_ref = refs[:2]
        h_ref = refs[2 + n]
        g_start, g_forward, g_finish = _gather_steps(refs[2:2 + n], refs[3 + n:3 + 2 * n], *refs[3 + 2 * n:])

        @pl.when(pl.program_id(0) == 0)
        def _():
            g_start()

        xv = x_ref[...]
        rstd = lax.rsqrt(jnp.mean(xv * xv, axis=-1, keepdims=True) + EPS)
        h_ref[...] = (xv * rstd * g_ref[...]).astype(BF16)

        @pl.when(pl.program_id(0) == nb - 1)
        def _():
            g_forward()
            g_finish()

    hbm = pl.BlockSpec(memory_space=pl.ANY)
    res = _pallas(
        body, name=name, grid=(nb,),
        in_specs=[pl.BlockSpec((tm, D), lambda i: (i, 0)), pl.BlockSpec((1, D), lambda i: (0, 0))] + [hbm] * n,
        out_specs=[pl.BlockSpec((tm, D), lambda i: (i, 0))] + [hbm] * n,
        out_shape=[jax.ShapeDtypeStruct((T, D), BF16)] + [jax.ShapeDtypeStruct((N_DEV,) + b.shape, b.dtype) for b in blocks],
        scratch_shapes=_gather_scratch(n),
        compiler_params=_params(dimension_semantics=("arbitrary",)),
    )(x, g, *blocks)
    return res[0], list(res[1:])


def _merge_fwd(ya, yb, wa, wb, zg):
    def body(ya_ref, yb_ref, wa_ref, wb_ref, zg_ref, m_ref, ua_ref, ub_ref):
        ua = _dot(ya_ref[...].astype(BF16), wa_ref[...], 1, 0)
        ub = _dot(yb_ref[...].astype(BF16), wb_ref[...], 1, 0)
        ga = _sigmoid(zg_ref[:, :D_MODEL])
        gb = _sigmoid(zg_ref[:, D_MODEL:])
        m_ref[...] = (ga * ua + gb * ub).astype(BF16)
        ua_ref[...] = ua.astype(BF16)
        ub_ref[...] = ub.astype(BF16)

    return _row_call(body, name="merge_fwd", T=ya.shape[0],
                     ins=[(ya, True), (yb, True), (wa, False), (wb, False), (zg, (2,))],
                     outs=[(D_MODEL, BF16)] * 3)


def _merge_bwd(dx1, w_out, wa, wb, ua, ub, zg, swap):
    def body(dx_ref, wo_ref, wa_ref, wb_ref, ua_ref, ub_ref, zg_ref, dua_ref, dub_ref, dzg_ref, dya_ref, dyb_ref):
        dmv = _dot(dx_ref[...].astype(BF16), wo_ref[...], 1, 1)
        ga = _sigmoid(zg_ref[:, :D_MODEL])
        gb = _sigmoid(zg_ref[:, D_MODEL:])
        dua = (dmv * ga).astype(BF16)
        dub = (dmv * gb).astype(BF16)
        dua_ref[...] = dua
        dub_ref[...] = dub
        dzg_ref[:, :D_MODEL] = (dmv * ua_ref[...].astype(F32) * ga * (1.0 - ga)).astype(BF16)
        dzg_ref[:, D_MODEL:] = (dmv * ub_ref[...].astype(F32) * gb * (1.0 - gb)).astype(BF16)
        dya_ref[...] = _dot(dua, wa_ref[...], 1, 1)
        dyb_ref[...] = _dot(dub, wb_ref[...], 1, 1)

    return _row_call(body, name="merge_bwd", T=dx1.shape[0],
                     ins=[(dx1, True), (w_out, False), (wa, False), (wb, False), (ua, True), (ub, True), (zg, (2,))],
                     outs=[(D_MODEL, BF16), (D_MODEL, BF16), (SEG, BF16, (IN_PAD, 2)), (HG_W, F32), (FOX_W, F32)],
                     swap=swap)


def _swiglu_fwd(hf, w_gate, w_up):
    T, D = hf.shape
    F = w_gate.shape[0]
    tm, tn = _pick(T, ROW_BLOCK), _pick(F, FFN_TILE)

    def body(h_ref, wg_ref, wu_ref, a_ref, b_ref, o_ref):
        hv = h_ref[...]
        a_b = _dot(hv, wg_ref[...], 1, 1).astype(BF16)
        b_b = _dot(hv, wu_ref[...], 1, 1).astype(BF16)
        a_ref[...] = a_b
        b_ref[...] = b_b
        av = a_b.astype(F32)
        o_ref[...] = (av * _sigmoid(av) * b_b.astype(F32)).astype(BF16)

    tile = pl.BlockSpec((tm, tn), lambda j, i: (i, j))
    wcol = pl.BlockSpec((tn, D), lambda j, i: (j, 0))
    return _pallas(
        body, name="swiglu_fwd", grid=(F // tn, T // tm),
        in_specs=[pl.BlockSpec((tm, D), lambda j, i: (i, 0)), wcol, wcol],
        out_specs=[tile] * 3, out_shape=[jax.ShapeDtypeStruct((T, F), BF16)] * 3,
        compiler_params=_params(dimension_semantics=("arbitrary",) * 2),
    )(hf, w_gate, w_up)


def _swiglu_bwd(dx, w_down, a, b):
    T, D = dx.shape
    F = w_down.shape[0]
    tm, tn = _pick(T, ROW_BLOCK), _pick(F, FFN_TILE)

    def body(dx_ref, w_ref, a_ref, b_ref, da_ref, db_ref):
        dact = _dot(dx_ref[...].astype(BF16), w_ref[...], 1, 1)
        av = a_ref[...].astype(F32)
        bv = b_ref[...].astype(F32)
        sg = _sigmoid(av)
        da_ref[...] = (dact * bv * sg * (1.0 + av * (1.0 - sg))).astype(BF16)
        db_ref[...] = (dact * av * sg).astype(BF16)

    tile = pl.BlockSpec((tm, tn), lambda j, i: (i, j))
    return _pallas(
        body, name="swiglu_bwd", grid=(F // tn, T // tm),
        in_specs=[pl.BlockSpec((tm, D), lambda j, i: (i, 0)), pl.BlockSpec((tn, D), lambda j, i: (j, 0)), tile, tile],
        out_specs=[tile, tile], out_shape=[jax.ShapeDtypeStruct((T, F), BF16)] * 2,
        compiler_params=_params(dimension_semantics=("arbitrary",) * 2),
    )(dx, w_down, a, b)


def _ple_loss(x2, p, g, w_gate, w_proj, tgt):
    def body(x_ref, p_ref, g_ref, wg_ref, wp_ref, t_ref, hp_ref, dy_ref, dsp_ref, dpp_ref, loss_ref):
        xv = x_ref[...]
        rstd = lax.rsqrt(jnp.mean(xv * xv, axis=-1, keepdims=True) + EPS)
        hp = (xv * rstd * g_ref[...]).astype(BF16)
        hp_ref[...] = hp
        gp = _sigmoid(_dot(hp, wg_ref[...], 1, 0))
        ppv = _dot(p_ref[...].astype(BF16), wp_ref[...], 1, 0)
        err = xv + gp * ppv - t_ref[...]
        part = 0.5 * jnp.sum(jnp.mean(err * err, axis=-1, keepdims=True), axis=0, keepdims=True)
        part = jnp.broadcast_to(part, loss_ref.shape)

        @pl.when(pl.program_id(0) == 0)
        def _():
            loss_ref[...] = part

        @pl.when(pl.program_id(0) > 0)
        def _():
            loss_ref[...] += part

        dy = err * (1.0 / D_MODEL)
        dy_ref[...] = dy
        dsp_ref[...] = (dy * ppv * gp * (1.0 - gp)).astype(BF16)
        dpp_ref[...] = (dy * gp).astype(BF16)

    return _row_call(body, name="ple_loss", T=x2.shape[0],
                     ins=[(x2, True), (p, True), (g, False), (w_gate, False), (w_proj, False), (tgt, True)],
                     outs=[(D_MODEL, BF16), (D_MODEL, F32), (D_MODEL, BF16), (D_MODEL, BF16)],
                     acc_outs=[((8, LANES), F32)])


def _hg_tri():
    C = HG_CHUNK
    return _iota((C, C), 1) <= _iota((C, C), 0)


def _hg_ref_row(b, blk):
    mid = blk * HG_SUB + HG_SUB // 2 - 1
    return b[mid:mid + 1]


def _hg_chunk_fwd(q, f, lb, tri_b):
    sgq = _sigmoid(q)
    qt = q * sgq
    sg = _sigmoid(f)
    fg = lb + (1.0 - lb) * sg
    kf = (1.0 - lb) * (1.0 - sg)
    logf = jnp.log(fg)
    b = _split_dot(tri_b, logf, 1, 0)
    refs = [jnp.broadcast_to(_hg_ref_row(b, blk), (HG_SUB, b.shape[1])) for blk in range(HG_CHUNK // HG_SUB)]
    w = jnp.minimum(b - jnp.concatenate(refs, axis=0), EXP_CLAMP)
    return sgq, qt, sg, fg, kf, b, w


def _hg_scores(qs_b, kf, b, row):
    C, S = HG_CHUNK, HG_SUB
    parts, ks = [], []
    for blk in range(C // S):
        e = jnp.exp(jnp.minimum(_hg_ref_row(b, blk) - b, EXP_CLAMP))
        e = jnp.where(row < (blk + 1) * S, e, 0.0)
        k_b = (kf * e).astype(BF16)
        ks.append((e, k_b))
        parts.append(_dot(qs_b[blk * S:(blk + 1) * S], k_b, 1, 1))
    return jnp.concatenate(parts, axis=0), ks


def _hgrn_fwd(z, lb_logits, gain, blocks):
    T = z.shape[0]
    RB = min(ROW_BLOCK, T)
    nb, cpb = T // RB, RB // HG_CHUNK
    C, DK = HG_CHUNK, HG_DK
    n = len(blocks)

    def body(*refs):
        z_ref, lg_ref, g_ref = refs[:3]
        o_ref, y_ref, st_ref = refs[3 + n:6 + n]
        s_ref = refs[6 + 2 * n]
        g_start, g_forward, g_finish = _gather_steps(refs[3:3 + n], refs[6 + n:6 + 2 * n], *refs[7 + 2 * n:])

        @pl.when(pl.program_id(0) == 0)
        def _():
            s_ref[...] = jnp.zeros_like(s_ref)
            g_start()

        lg = lg_ref[...]
        lb_all = 1.0 / (1.0 + jnp.exp(lg[1:2] - lg[0:1]))
        gain_v = g_ref[...]
        tri = _hg_tri()
        tri_b = tri.astype(BF16)
        row = _iota((C, DK), 0)

        def chunk(ci, carry):
            r0 = pl.multiple_of(ci * C, C)
            rows = pl.ds(r0, C)
            _, qt, _, _, kf_all, b_all, w = _hg_chunk_fwd(z_ref[rows, 0:HG_W], z_ref[rows, HG_W:2 * HG_W], lb_all,
                                                          tri_b)
            qs_all = (qt * jnp.exp(w)).astype(BF16)
            qd_all = (qt * jnp.exp(b_all)).astype(BF16)
            bl_all = b_all[C - 1:C]
            kd_all = (kf_all * jnp.exp(bl_all - b_all)).astype(BF16)
            ebl_all = jnp.exp(bl_all)
            v_all = z_ref[rows, 2 * HG_W:3 * HG_W].astype(BF16)
            g_all = z_ref[rows, 3 * HG_W:4 * HG_W]
            gate_all = g_all * _sigmoid(g_all)
            for h in range(HG_HEADS):
                cs = slice(h * DK, (h + 1) * DK)
                st = s_ref[h]
                st_ref[pl.ds(pl.multiple_of((ci * HG_HEADS + h) * DK, DK), DK), :] = st
                v_b = v_all[:, cs]
                a, _ = _hg_scores(qs_all[:, cs], kf_all[:, cs], b_all[:, cs], row)
                a = jnp.where(tri, a, 0.0)
                o = _dot(qd_all[:, cs], st.astype(BF16), 1, 1) + _dot(a.astype(BF16), v_b, 1, 0)
                s_ref[h] = st * ebl_all[:, cs] + _dot(v_b, kd_all[:, cs], 0, 0)
                o_ref[rows, cs] = o
                rstd = lax.rsqrt(jnp.mean(o * o, axis=-1, keepdims=True) + EPS)
                y_ref[rows, cs] = (o * rstd * gain_v * gate_all[:, cs]).astype(BF16)
            return carry

        lax.fori_loop(0, cpb, chunk, 0, unroll=2)

        @pl.when(pl.program_id(0) == nb - 1)
        def _():
            g_forward()
            g_finish()

    hbm = pl.BlockSpec(memory_space=pl.ANY)
    res = _pallas(
        body, name="hgrn_fwd", grid=(nb,),
        in_specs=[pl.BlockSpec((RB, HG_COLS), lambda i: (i, 0)), pl.BlockSpec((2, HG_W), lambda i: (0, 0)),
                  pl.BlockSpec((1, DK), lambda i: (0, 0))] + [hbm] * n,
        out_specs=[pl.BlockSpec((RB, HG_W), lambda i: (i, 0)), pl.BlockSpec((RB, HG_W), lambda i: (i, 0)),
                   pl.BlockSpec((cpb * HG_HEADS * DK, DK), lambda i: (i, 0))] + [hbm] * n,
        out_shape=[jax.ShapeDtypeStruct((T, HG_W), F32), jax.ShapeDtypeStruct((T, HG_W), BF16),
                   jax.ShapeDtypeStruct((T // C * HG_HEADS * DK, DK), F32)]
        + [jax.ShapeDtypeStruct((N_DEV,) + b.shape, b.dtype) for b in blocks],
        scratch_shapes=[pltpu.VMEM((HG_HEADS, DK, DK), F32)] + _gather_scratch(n),
        compiler_params=_params(dimension_semantics=("arbitrary",)),
    )(z, lb_logits, gain, *blocks)
    return res[0], res[1], res[2], res[3:]


def _hgrn_bwd(z, o_raw, dy, states, lb_logits, gain, dz_buf):
    T = z.shape[0]
    RB = min(ROW_BLOCK, T)
    nb, cpb = T // RB, RB // HG_CHUNK
    C, DK, S = HG_CHUNK, HG_DK, HG_SUB

    def body(z_ref, o_ref, dy_ref, st_ref, lg_ref, g_ref, _buf_ref, dz_ref, dlg_ref, dg_ref, ds_ref, dlb_ref):
        step = pl.program_id(0)

        @pl.when(step == 0)
        def _():
            ds_ref[...] = jnp.zeros_like(ds_ref)
            dlb_ref[...] = jnp.zeros_like(dlb_ref)
            dg_ref[...] = jnp.zeros_like(dg_ref)

        lg = lg_ref[...]
        lb_all = 1.0 / (1.0 + jnp.exp(lg[1:2] - lg[0:1]))
        gain_v = g_ref[...]
        tri = _hg_tri()
        tri_b = tri.astype(BF16)
        row = _iota((C, DK), 0)

        def chunk(cj, carry):
            ci = cpb - 1 - cj
            r0 = pl.multiple_of(ci * C, C)
            rows = pl.ds(r0, C)
            q_all = z_ref[rows, 0:HG_W]
            g_all = z_ref[rows, 3 * HG_W:4 * HG_W]
            sgq_all, qt_all, sg_all, fg_all, kf_all, b_all, w_all = _hg_chunk_fwd(
                q_all, z_ref[rows, HG_W:2 * HG_W], lb_all, tri_b)
            ew_all = jnp.exp(w_all)
            eb_all = jnp.exp(b_all)
            bl_all = b_all[C - 1:C]
            ebl_all = jnp.exp(bl_all)
            ekd_all = jnp.exp(bl_all - b_all)
            qs_all = (qt_all * ew_all).astype(BF16)
            qd_all = (qt_all * eb_all).astype(BF16)
            kd_all = (kf_all * ekd_all).astype(BF16)
            v_all = z_ref[rows, 2 * HG_W:3 * HG_W].astype(BF16)
            sgg_all = _sigmoid(g_all)
            t1_all = dy_ref[rows, :] * (g_all * sgg_all)
            db_heads, dqt_heads, dkf_heads, dv_heads, n_heads = [], [], [], [], []
            for h in range(HG_HEADS):
                cs = slice(h * DK, (h + 1) * DK)
                kf, b, ew, eb, ebl, ekd = kf_all[:, cs], b_all[:, cs], ew_all[:, cs], eb_all[:, cs], ebl_all[:, cs], \
                    ekd_all[:, cs]
                qs_b, qd_b, kd_b, v_b = qs_all[:, cs], qd_all[:, cs], kd_all[:, cs], v_all[:, cs]
                st = st_ref[pl.ds(pl.multiple_of((ci * HG_HEADS + h) * DK, DK), DK), :]
                dst = ds_ref[h]
                o = o_ref[rows, cs]
                rstd = lax.rsqrt(jnp.mean(o * o, axis=-1, keepdims=True) + EPS)
                n = o * rstd
                n_heads.append(n)
                t1 = t1_all[:, cs]
                dg_ref[...] += jnp.sum(t1 * n, axis=0, keepdims=True)
                dn = t1 * gain_v
                do = rstd * (dn - n * jnp.mean(dn * n, axis=-1, keepdims=True))
                do_b = do.astype(BF16)
                a, ks = _hg_scores(qs_b, kf, b, row)
                a = jnp.where(tri, a, 0.0)
                dst_b = dst.astype(BF16)
                dqd = _dot(do_b, st.astype(BF16), 1, 0)
                da = jnp.where(tri, _dot(do_b, v_b, 1, 1), 0.0)
                dv = _dot(a.astype(BF16), do_b, 0, 0) + _dot(kd_b, dst_b, 1, 1)
                dkd = _dot(v_b, dst_b, 1, 0)
                ds_ref[h] = dst * ebl + _dot(do_b, qd_b, 0, 0)
                dkd_kd = dkd * kd_b.astype(F32)
                dbl = ebl * jnp.sum(dst * st, axis=0, keepdims=True) + jnp.sum(dkd_kd, axis=0, keepdims=True)
                da_b = da.astype(BF16)
                dqs_parts = []
                dk_in = jnp.zeros((C, DK), F32)
                db_k = jnp.zeros((C, DK), F32)
                for blk in range(C // S):
                    e, k_b = ks[blk]
                    da_blk = da_b[blk * S:(blk + 1) * S]
                    dqs_parts.append(_dot(da_blk, k_b, 1, 0))
                    dks = _dot(da_blk, qs_b[blk * S:(blk + 1) * S], 0, 0)
                    dk_in = dk_in + dks * e
                    db_k = db_k + dks * k_b.astype(F32)
                dqs = jnp.concatenate(dqs_parts, axis=0)
                db = qs_b.astype(F32) * dqs - db_k + dqd * qd_b.astype(F32) - dkd_kd
                db_heads.append(db + jnp.where(row == C - 1, dbl, 0.0))
                dqt_heads.append(dqs * ew + dqd * eb)
                dkf_heads.append(dk_in + dkd * ekd)
                dv_heads.append(dv)
            dlogf = _split_dot(tri_b, jnp.concatenate(db_heads, axis=1), 0, 0)
            dfg = dlogf / fg_all - jnp.concatenate(dkf_heads, axis=1)
            dlb_ref[...] += jnp.sum(dfg * (1.0 - sg_all), axis=0, keepdims=True)
            dqt = jnp.concatenate(dqt_heads, axis=1)
            n_all = jnp.concatenate(n_heads, axis=1)
            gain_all = jnp.concatenate([gain_v] * HG_HEADS, axis=1)
            dz_ref[rows, 0:HG_W] = (dqt * sgq_all * (1.0 + q_all * (1.0 - sgq_all))).astype(BF16)
            dz_ref[rows, HG_W:2 * HG_W] = (dfg * (1.0 - lb_all) * sg_all * (1.0 - sg_all)).astype(BF16)
            dz_ref[rows, 2 * HG_W:3 * HG_W] = jnp.concatenate(dv_heads, axis=1).astype(BF16)
            dz_ref[rows, 3 * HG_W:4 * HG_W] = (dy_ref[rows, :] * n_all * gain_all * sgg_all
                                               * (1.0 + g_all * (1.0 - sgg_all))).astype(BF16)
            return carry

        lax.fori_loop(0, cpb, chunk, 0, unroll=2)

        @pl.when(step == nb - 1)
        def _():
            d0 = dlb_ref[...] * lb_all * (1.0 - lb_all)
            dlg_ref[0:1, :] = d0
            dlg_ref[1:2, :] = -d0

    rev = lambda i: (nb - 1 - i, 0)
    fix = lambda i: (0, 0)
    return _pallas(
        body, name="hgrn_bwd", grid=(nb,),
        in_specs=[pl.BlockSpec((RB, HG_COLS), rev), pl.BlockSpec((RB, HG_W), rev), pl.BlockSpec((RB, HG_W), rev),
                  pl.BlockSpec((cpb * HG_HEADS * DK, DK), rev), pl.BlockSpec((2, HG_W), fix),
                  pl.BlockSpec((1, DK), fix), pl.BlockSpec(memory_space=pl.ANY)],
        out_specs=[pl.BlockSpec((RB, HG_COLS), rev), pl.BlockSpec((2, HG_W), fix), pl.BlockSpec((1, DK), fix)],
        out_shape=[jax.ShapeDtypeStruct(dz_buf.shape, BF16), jax.ShapeDtypeStruct((2, HG_W), F32),
                   jax.ShapeDtypeStruct((1, DK), F32)],
        scratch_shapes=[pltpu.VMEM((HG_HEADS, DK, DK), F32), pltpu.VMEM((1, HG_W), F32)],
        input_output_aliases={6: 0},
        compiler_params=_params(dimension_semantics=("arbitrary",)),
    )(z, o_raw, dy, states, lb_logits, gain, dz_buf)


def _head_ones():
    r, c = _iota((FOX_W, FOX_W), 0), _iota((FOX_W, FOX_W), 1)
    return ((r // FOX_DH) == (c // FOX_DH)).astype(BF16)


def _log_sigmoid(x):
    return jnp.minimum(x, 0.0) - jnp.log(1.0 + jnp.exp(-jnp.abs(x)))


def _fox_prep(z, bias, qg, kg):
    T = z.shape[0]
    tm = min(ROW_BLOCK, T)
    nb = T // tm

    def body(z_ref, b_ref, qg_ref, kg_ref, q_ref, k_ref, v_ref, qa_ref, ka_ref, carry_ref):
        @pl.when(pl.program_id(0) == 0)
        def _():
            carry_ref[...] = jnp.zeros_like(carry_ref)

        ones = _head_ones()
        normed = []
        for src, g_ref in ((0, qg_ref), (1, kg_ref)):
            xv = z_ref[:, src * FOX_W:(src + 1) * FOX_W]
            ms = _split_dot(ones, xv * xv, 1, 0, mat_first=False) * (1.0 / FOX_DH)
            normed.append(xv * lax.rsqrt(ms + EPS) * g_ref[...])
        qn, kn = normed
        q_ref[...] = (qn * FOX_DH ** -0.5).astype(BF16)
        k_b = kn.astype(BF16)
        k_ref[...] = k_b
        v_ref[...] = z_ref[:, 2 * FOX_W:3 * FOX_W].astype(BF16)
        logf = _log_sigmoid(z_ref[:, 3 * FOX_W:FOX_COLS] + b_ref[...])
        r, c = _iota((tm, tm), 0), _iota((tm, tm), 1)
        tri_b = (c <= r).astype(BF16)
        cum = _split_dot(tri_b, logf, 1, 0, terms=3) + carry_ref[...]
        carry_ref[...] = cum[tm - 1:tm]
        c2 = cum * LOG2E
        hi = c2.astype(BF16)
        rem = c2 - hi.astype(F32)
        mid = rem.astype(BF16)
        lo = (rem - mid.astype(F32)).astype(BF16)
        hrow, col = _iota((LANES, 2 * FOX_W), 0), _iota((LANES, 2 * FOX_W), 1)
        base = hrow * LANES + jnp.where(hrow % 2 == 0, FOX_DH, 0)
        placed = None
        for t, part in enumerate((hi, mid, lo)):
            place = jnp.logical_and(col == base + t, hrow < FOX_HEADS).astype(BF16)
            term = _dot(part, place, 1, 0)
            placed = term if placed is None else placed + term
        colw = _iota((tm, 2 * FOX_W), 1)
        head, lane = colw // LANES, colw % LANES
        own = (lane < FOX_DH) == (head % 2 == 0)
        other = jnp.where(head % 2 == 0, lane - FOX_DH, lane)
        ones_q = jnp.where(jnp.logical_and(other >= 0, other < 3), -1.0, 0.0)
        q2 = (qn * (FOX_DH ** -0.5 * LOG2E)).astype(BF16)
        q_exp = jnp.concatenate([q2[:, (h // 2) * LANES:(h // 2 + 1) * LANES] for h in range(FOX_HEADS)], axis=1)
        k_exp = jnp.concatenate([k_b[:, (h // 2) * LANES:(h // 2 + 1) * LANES] for h in range(FOX_HEADS)], axis=1)
        qa_ref[...] = jnp.where(own, q_exp, ones_q.astype(BF16))
        ka_ref[...] = jnp.where(own, k_exp, placed.astype(BF16))

    wide = pl.BlockSpec((tm, 2 * FOX_W), lambda i: (i, 0))
    return _pallas(
        body, name="fox_prep", grid=(nb,),
        in_specs=[pl.BlockSpec((tm, SEG), lambda i: (i, 1)), pl.BlockSpec((1, LANES), lambda i: (0, 0)),
                  pl.BlockSpec((1, FOX_W), lambda i: (0, 0)), pl.BlockSpec((1, FOX_W), lambda i: (0, 0))],
        out_specs=[pl.BlockSpec((tm, FOX_W), lambda i: (i, 0))] * 3 + [wide] * 2,
        out_shape=[jax.ShapeDtypeStruct((T, FOX_W), BF16)] * 3 + [jax.ShapeDtypeStruct((T, 2 * FOX_W), BF16)] * 2,
        scratch_shapes=[pltpu.VMEM((1, LANES), F32)],
        compiler_params=_params(dimension_semantics=("arbitrary",)),
    )(z, bias, qg, kg)


def _fox_fwd(qa, ka, vb, blocks):
    T = qa.shape[0]
    tq = min(ROW_BLOCK, T)
    nq = T // tq
    NEG = -1e30
    n = len(blocks)
    n_pairs = FOX_HEADS // 2

    n_in = 3

    def body(*refs):
        q_ref, k_ref, v_ref = refs[:n_in]
        o_ref, lse_ref = refs[n_in + n:n_in + n + 2]
        m_sc, l_sc, acc_sc = refs[n_in + 2 * n + 2:n_in + 2 * n + 5]
        pr, qi = pl.program_id(0), pl.program_id(1)
        g_start, g_forward, g_finish = _gather_steps(
            refs[n_in:n_in + n], refs[n_in + n + 2:n_in + 2 * n + 2], *refs[n_in + 2 * n + 5:])

        @pl.when(jnp.logical_and(pr == 0, qi == 0))
        def _():
            g_start()

        @pl.when(jnp.logical_and(pr == n_pairs // 2, qi == 0))
        def _():
            g_forward()

        m_sc[...] = jnp.full_like(m_sc, NEG)
        l_sc[...] = jnp.zeros_like(l_sc)
        acc_sc[...] = jnp.zeros_like(acc_sc)
        lane = _iota((tq, LANES), 1)

        def block(masked, ki):
            keys = pl.ds(pl.multiple_of(ki * tq, tq), tq)
            vv = v_ref[keys, :]
            for hh in range(2):
                hs = slice(hh * LANES, (hh + 1) * LANES)
                s = _dot(q_ref[:, hs], k_ref[keys, hs], 1, 1)
                tiles = [s[:, j * LANES:(j + 1) * LANES] for j in range(tq // LANES)]
                if masked:
                    row, col = _iota((tq, LANES), 0), _iota((tq, LANES), 1)
                    tiles = [jnp.where(row >= col + j * LANES, t, NEG) for j, t in enumerate(tiles)]
                m_old = m_sc[hh]
                top = jnp.broadcast_to(jnp.max(functools.reduce(jnp.maximum, tiles), axis=-1, keepdims=True),
                                       (tq, LANES))
                m_new = jnp.maximum(m_old, top)
                alpha = jnp.exp2(m_old - m_new)
                ps = [jnp.exp2(t - m_new) for t in tiles]
                l_sc[hh] = alpha * l_sc[hh] + functools.reduce(jnp.add, ps)
                m_sc[hh] = m_new
                p_b = jnp.concatenate([p.astype(BF16) for p in ps], axis=1)
                acc_sc[hh] = alpha * acc_sc[hh] + _dot(p_b, vv, 1, 0)

        def before(ki, carry):
            block(False, ki)
            return carry

        lax.fori_loop(0, qi, before, 0)
        block(True, qi)
        l0 = jnp.sum(l_sc[0], axis=-1, keepdims=True)
        l1 = jnp.sum(l_sc[1], axis=-1, keepdims=True)
        o_ref[...] = jnp.where(lane < FOX_DH, acc_sc[0] * (1.0 / l0), acc_sc[1] * (1.0 / l1))
        lse_ref[:, :LANES] = m_sc[0] + jnp.log2(l0)
        lse_ref[:, LANES:] = m_sc[1] + jnp.log2(l1)

        @pl.when(jnp.logical_and(pr == n_pairs - 1, qi == nq - 1))
        def _():
            g_finish()

    qmap = lambda p, i: (i, p)
    whole = lambda p, i: (0, p)
    hbm = pl.BlockSpec(memory_space=pl.ANY)
    res = _pallas(
        body, name="fox_fwd", grid=(n_pairs, nq),
        in_specs=[pl.BlockSpec((tq, 2 * LANES), qmap), pl.BlockSpec((T, 2 * LANES), whole),
                  pl.BlockSpec((T, LANES), whole)] + [hbm] * n,
        out_specs=[pl.BlockSpec((tq, LANES), qmap), pl.BlockSpec((tq, 2 * LANES), qmap)] + [hbm] * n,
        out_shape=[jax.ShapeDtypeStruct((T, FOX_W), F32), jax.ShapeDtypeStruct((T, 2 * FOX_W), F32)]
        + [jax.ShapeDtypeStruct((N_DEV,) + b.shape, b.dtype) for b in blocks],
        scratch_shapes=[pltpu.VMEM((2, tq, LANES), F32)] * 3 + _gather_scratch(n),
        compiler_params=_params(dimension_semantics=("arbitrary",) * 2),
    )(qa, ka, vb, *blocks)
    return res[0], res[1], res[2:]


def _fox_bwd(qs, kn, vb, qa, ka, o, do, lse, hs):
    T = qs.shape[0]
    tq = min(ROW_BLOCK, T)
    nq = T // tq
    n = len(hs)
    n_pairs = FOX_HEADS // 2

    def body(*refs):
        q_ref, k_ref, v_ref, qa_ref, ka_ref, o_ref, do_ref, lse_ref = refs[:8]
        dq_ref, dk_ref, dv_ref, dcs_ref, drs_ref = refs[8 + n:13 + n]
        pr, ki = pl.program_id(0), pl.program_id(1)
        x_start, x_finish = _chip_exchange_steps(refs[8:8 + n], refs[13 + n:13 + 2 * n], *refs[13 + 2 * n:])

        @pl.when(jnp.logical_and(pr == 0, ki == 0))
        def _():
            x_start()
            drs_ref[...] = jnp.zeros_like(drs_ref)

        @pl.when(ki == 0)
        def _():
            dq_ref[...] = jnp.zeros_like(dq_ref)

        dk_ref[...] = jnp.zeros_like(dk_ref)
        dv_ref[...] = jnp.zeros_like(dv_ref)
        dcs_ref[...] = jnp.zeros_like(dcs_ref)

        def block(masked, qi):
            lane = _iota((tq, LANES), 1)
            qrows = pl.ds(pl.multiple_of(qi * tq, tq), tq)
            qv, kv, vv = q_ref[qrows, :], k_ref[...], v_ref[...]
            ov, dov = o_ref[qrows, :], do_ref[qrows, :]
            dq_acc = jnp.zeros((tq, LANES), F32)
            dk_acc = jnp.zeros((tq, LANES), F32)
            dv_acc = jnp.zeros((tq, LANES), F32)
            dcs_acc = jnp.zeros((8, tq), F32)
            drs_acc = jnp.zeros((tq, LANES), F32)
            prod = dov * ov
            nt = tq // LANES
            for hh in range(2):
                in_head = (lane < FOX_DH) if hh == 0 else (lane >= FOX_DH)
                hs_ = slice(hh * LANES, (hh + 1) * LANES)
                zb = jnp.zeros_like(qv)
                qm = jnp.where(in_head, qv, zb)
                km = jnp.where(in_head, kv, zb)
                dom = jnp.where(in_head, dov, 0.0).astype(BF16)
                delta_b = jnp.broadcast_to(jnp.sum(jnp.where(in_head, prod, 0.0), axis=1, keepdims=True), (tq, LANES))
                lse_b = lse_ref[qrows, hs_]
                s = _dot(qa_ref[qrows, hs_], ka_ref[:, hs_], 1, 1)
                dp = _dot(dom, vv, 1, 1)
                p_tiles, ds_tiles, col_tiles = [], [], []
                row_part = jnp.zeros((tq, LANES), F32)
                for j in range(nt):
                    js = slice(j * LANES, (j + 1) * LANES)
                    p = jnp.exp2(s[:, js] - lse_b)
                    if masked:
                        p = jnp.where(_iota((tq, LANES), 0) >= _iota((tq, LANES), 1) + j * LANES, p, 0.0)
                    ds = p * (dp[:, js] - delta_b)
                    p_tiles.append(p.astype(BF16))
                    ds_tiles.append(ds.astype(BF16))
                    col_tiles.append(jnp.sum(ds, axis=0, keepdims=True))
                    row_part = row_part + ds
                p_b = jnp.concatenate(p_tiles, axis=1)
                ds_b = jnp.concatenate(ds_tiles, axis=1)
                dv_acc = dv_acc + _dot(p_b, dom, 0, 0)
                dq_acc = dq_acc + _dot(ds_b, km, 1, 0)
                dk_acc = dk_acc + _dot(ds_b, qm, 0, 0)
                dcs_acc = dcs_acc + jnp.where(_iota((8, tq), 0) == hh, jnp.concatenate(col_tiles, axis=1), 0.0)
                rowsum = jnp.sum(row_part, axis=1, keepdims=True)
                drs_acc = drs_acc + jnp.where(lane == 2 * pr + hh, rowsum, 0.0)
            drs_ref[qrows, :] += drs_acc
            dq_ref[qrows, :] += dq_acc
            dk_ref[...] += dk_acc
            dv_ref[...] += dv_acc
            dcs_ref[0] += dcs_acc

        block(True, ki)

        def after(qi, carry):
            block(False, qi)
            return carry

        lax.fori_loop(ki + 1, nq, after, 0)

        @pl.when(jnp.logical_and(pr == n_pairs - 1, ki == nq - 1))
        def _():
            x_finish()

    whole = lambda p, j: (0, p)
    kmap = lambda p, j: (j, p)
    hbm = pl.BlockSpec(memory_space=pl.ANY)
    res = _pallas(
        body, name="fox_bwd", grid=(n_pairs, nq),
        in_specs=[pl.BlockSpec((T, LANES), whole), pl.BlockSpec((tq, LANES), kmap), pl.BlockSpec((tq, LANES), kmap),
                  pl.BlockSpec((T, 2 * LANES), whole), pl.BlockSpec((tq, 2 * LANES), kmap),
                  pl.BlockSpec((T, LANES), whole), pl.BlockSpec((T, LANES), whole), pl.BlockSpec((T, 2 * LANES), whole)]
        + [hbm] * n,
        out_specs=[pl.BlockSpec((T, LANES), whole), pl.BlockSpec((tq, LANES), kmap),
                   pl.BlockSpec((tq, LANES), kmap), pl.BlockSpec((1, 8, tq), lambda p, j: (p, 0, j)),
                   pl.BlockSpec((T, LANES), lambda p, j: (0, 0))] + [hbm] * n,
        out_shape=[jax.ShapeDtypeStruct((T, FOX_W), F32)] * 3
        + [jax.ShapeDtypeStruct((n_pairs, 8, T), F32), jax.ShapeDtypeStruct((T, LANES), F32)]
        + _chip_exchange_shapes(hs),
        scratch_shapes=_chip_exchange_scratch(n),
        compiler_params=_params(dimension_semantics=("arbitrary",) * 2),
    )(qs, kn, vb, qa, ka, o, do, lse, *hs)
    res = list(res)
    return res[:5] + [res[5:]]


def _fox_post(z, dq, dk, dv, dcs, drs, bias, qg, kg, dz_buf):
    T = z.shape[0]
    tm = min(ROW_BLOCK, T)
    nb = T // tm

    def body(z_ref, dq_ref, dk_ref, dv_ref, dcs_ref, drs_ref, b_ref, qg_ref, kg_ref, _buf_ref, dz_ref, dqg_ref, dkg_ref,
             db_ref, carry_ref):
        @pl.when(pl.program_id(0) == 0)
        def _():
            carry_ref[...] = jnp.zeros_like(carry_ref)
            dqg_ref[...] = jnp.zeros_like(dqg_ref)
            dkg_ref[...] = jnp.zeros_like(dkg_ref)
            db_ref[...] = jnp.zeros_like(db_ref)

        ones = _head_ones()
        for src, g_ref, d_ref, dg_ref, scale in ((0, qg_ref, dq_ref, dqg_ref, FOX_DH ** -0.5), (1, kg_ref, dk_ref, dkg_ref, 1.0)):
            xv = z_ref[:, src * FOX_W:(src + 1) * FOX_W]
            ms = _split_dot(ones, xv * xv, 1, 0, mat_first=False) * (1.0 / FOX_DH)
            rstd = lax.rsqrt(ms + EPS)
            xh = xv * rstd
            dn = d_ref[...] * scale
            dg_ref[...] += jnp.sum(dn * xh, axis=0, keepdims=True)
            dxh = dn * g_ref[...]
            mean = _split_dot(ones, dxh * xh, 1, 0, mat_first=False) * (1.0 / FOX_DH)
            dz_ref[:, src * FOX_W:(src + 1) * FOX_W] = (rstd * (dxh - xh * mean)).astype(BF16)
        dz_ref[:, 2 * FOX_W:3 * FOX_W] = dv_ref[...].astype(BF16)
        row8 = _iota((8, tm), 0)
        dct = jnp.zeros((8, tm), F32)
        for h in range(FOX_HEADS):
            src_row = dcs_ref[h // 2][h % 2:h % 2 + 1, :]
            dct = dct + jnp.where(row8 == h, src_row, 0.0)
        dct = drs_ref[...].T[0:8] - dct
        r, c = _iota((tm, tm), 0), _iota((tm, tm), 1)
        upper_b = (r >= c).astype(BF16)
        rc = _split_dot(upper_b, dct, 1, 0, mat_first=False) + carry_ref[...]
        carry_ref[...] = rc[:, 0:1]
        full = jnp.concatenate([rc, jnp.zeros((LANES - 8, tm), F32)], axis=0)
        dlogf = full.T
        xf = z_ref[:, 3 * FOX_W:FOX_COLS] + b_ref[...]
        df = dlogf * (1.0 - _sigmoid(xf))
        dz_ref[:, 3 * FOX_W:FOX_COLS] = df.astype(BF16)
        dz_ref[:, FOX_COLS:] = jnp.zeros((tm, SEG - FOX_COLS), BF16)
        db_ref[...] += jnp.sum(df, axis=0, keepdims=True)

    rev = lambda i: (nb - 1 - i, 0)
    fix2 = lambda i: (0, 0)
    return _pallas(
        body, name="fox_post", grid=(nb,),
        in_specs=[pl.BlockSpec((tm, SEG), lambda i: (nb - 1 - i, 1)), pl.BlockSpec((tm, FOX_W), rev),
                  pl.BlockSpec((tm, FOX_W), rev),
                  pl.BlockSpec((tm, FOX_W), rev), pl.BlockSpec((FOX_HEADS // 2, 8, tm), lambda i: (0, 0, nb - 1 - i)),
                  pl.BlockSpec((tm, LANES), rev),
                  pl.BlockSpec((1, LANES), fix2), pl.BlockSpec((1, FOX_W), fix2), pl.BlockSpec((1, FOX_W), fix2),
                  pl.BlockSpec(memory_space=pl.ANY)],
        out_specs=[pl.BlockSpec((tm, SEG), lambda i: (nb - 1 - i, 1)), pl.BlockSpec((1, FOX_W), fix2),
                   pl.BlockSpec((1, FOX_W), fix2), pl.BlockSpec((1, LANES), fix2)],
        out_shape=[jax.ShapeDtypeStruct(dz_buf.shape, BF16), jax.ShapeDtypeStruct((1, FOX_W), F32),
                   jax.ShapeDtypeStruct((1, FOX_W), F32), jax.ShapeDtypeStruct((1, LANES), F32)],
        scratch_shapes=[pltpu.VMEM((8, 1), F32)],
        input_output_aliases={9: 0},
        compiler_params=_params(dimension_semantics=("arbitrary",)),
    )(z, dq, dk, dv, dcs, drs, bias, qg, kg, dz_buf)


def _local_step(x, p, tgt, sm, chunks, core):
    lbl, og, fb = sm["hg_lb_logits"], sm["hg_onorm_g"], sm["fox_f_bias"]
    fbias = jnp.pad(fb, ((0, 0), (0, LANES - FOX_HEADS)))
    qg = jnp.tile(sm["fox_q_norm_g"], (1, FOX_HEADS))
    kg = jnp.tile(sm["fox_k_norm_g"], (1, FOX_HEADS))

    assert BIG[0] == "w_in"
    h, got0 = _rms_fwd(x, sm["norm_mix_g"], chunks[:1], name="rms_mix")
    W = {"w_in": _full_of_chunks("w_in", got0[0])}
    rest = dict(zip(BIG[1:], chunks[1:]))
    first, second = ["w_ffn_gate"], ["w_ffn_up"]
    third = [n for n in BIG[1:] if n not in first + second]
    z, got1 = _matmul(h, W["w_in"], tb=True, gather=[rest[n] for n in first], name="mm_z")
    o_raw, ya, states, got2 = _hgrn_fwd(z, lbl, og, [rest[n] for n in second])
    qs, kn, vb, qa, ka = _fox_prep(z, fbias, qg, kg)
    yb, lse, got3 = _fox_fwd(qa, ka, vb, [rest[n] for n in third])
    W = dict(W, **{n: _full_of_chunks(n, g)
                   for n, g in zip(first + second + third, list(got1) + list(got2) + list(got3))})
    merged, ua, ub = _merge_fwd(ya, yb, W["w_branch_a"], W["w_branch_b"], z)
    x1, hf = _matmul(merged, W["w_out"], add=x, norm_fwd=sm["norm_ffn_g"], name="mm_x1")
    a, b, act = _swiglu_fwd(hf, W["w_ffn_gate"], W["w_ffn_up"])
    x2 = _matmul(act, W["w_ffn_down"], add=x1, name="mm_x2")
    hp, dy, dsp, dpp, loss = _ple_loss(x2, p, sm["norm_ple_g"], W["w_ple_gate"], W["w_ple_proj"], tgt)

    G = {}
    G["w_ple_proj"] = _matmul(p, dpp, ta=True, out_dtype=BF16, name="mm_dw_ple_proj")
    G["w_ple_gate"] = _matmul(hp, dsp, ta=True, out_dtype=BF16, name="mm_dw_ple_gate")
    dx2, d_ple_g = _matmul(dsp, W["w_ple_gate"], tb=True, norm_bwd=(x2, sm["norm_ple_g"], dy), name="mm_dx2")
    G["w_ffn_down"] = _matmul(act, dx2, ta=True, out_dtype=BF16, name="mm_dw_ffn_down")
    da, db = _swiglu_bwd(dx2, W["w_ffn_down"], a, b)
    G["w_ffn_gate"] = _matmul(da, hf, ta=True, out_dtype=BF16, name="mm_dw_ffn_gate")
    G["w_ffn_up"] = _matmul(db, hf, ta=True, out_dtype=BF16, name="mm_dw_ffn_up")
    dhf = _matmul(da, W["w_ffn_gate"], name="mm_dhf_a")
    dx1, d_ffn_g = _matmul(db, W["w_ffn_up"], add=dhf, norm_bwd=(x1, sm["norm_ffn_g"], dx2), name="mm_dx1")
    G["w_out"] = _matmul(merged, dx1, ta=True, out_dtype=BF16, name="mm_dw_out")
    late = ["w_branch_a", "w_branch_b"]
    early = [n for n in BIG[1:] if n not in late]
    g4 = {n: _chunks4(n, G[n]) for n in early}
    (dua, dub, dz, dya, dyb), got_early = _merge_bwd(dx1, W["w_out"], W["w_branch_a"], W["w_branch_b"], ua, ub, z,
                                                     [g4[n] for n in early])
    G["w_branch_a"] = _matmul(ya, dua, ta=True, out_dtype=BF16, name="mm_dw_branch_a")
    G["w_branch_b"] = _matmul(yb, dub, ta=True, out_dtype=BF16, name="mm_dw_branch_b")
    g4.update({n: _chunks4(n, G[n]) for n in late})
    got4 = dict(zip(early, got_early))
    got4.update(zip(late, _sibling_exchange([g4[n] for n in late], name="grads_to_sibling_branch")))
    hb_rest = _chip_sum([g4[n] for n in BIG[1:]], [got4[n] for n in BIG[1:]], core, name="chip_sum_rest")
    dq, dk, dv, dcs, drs, got_rest = _fox_bwd(qs, kn, vb, qa, ka, yb, dyb, lse, hb_rest)
    dz, d_qg, d_kg, d_fb = _fox_post(z, dq, dk, dv, dcs, drs, fbias, qg, kg, dz)
    dz, d_lbl, d_og = _hgrn_bwd(z, o_raw, dya, states, lbl, og, dz)
    G["w_in"] = _matmul(dz, h, ta=True, out_dtype=BF16, name="mm_dw_in")
    hb_in = _sibling_sums({"w_in": G["w_in"]}, core, tag="w_in")
    (grad_x, d_mix_g), got_in = _matmul(dz, W["w_in"], exchange=hb_in,
                                        norm_bwd=(x, sm["norm_mix_g"], dx1), name="mm_dx")

    gs = {"norm_mix_g": d_mix_g, "hg_lb_logits": d_lbl, "hg_onorm_g": d_og, "fox_f_bias": d_fb[:, :FOX_HEADS],
          "fox_q_norm_g": d_qg.reshape(FOX_HEADS, FOX_DH).sum(0, keepdims=True),
          "fox_k_norm_g": d_kg.reshape(FOX_HEADS, FOX_DH).sum(0, keepdims=True),
          "norm_ffn_g": d_ffn_g, "norm_ple_g": d_ple_g}
    return loss, grad_x, gs, hb_in + hb_rest, list(got_in) + list(got_rest)


def _pack_rows(parts, total):
    buf = jnp.concatenate(parts, axis=-2)
    pad = total - buf.shape[-2]
    widths = [(0, 0)] * (buf.ndim - 2) + [(0, pad), (0, 0)]
    return jnp.pad(buf, widths)


def _chunk_of_shard(n, w):
    if n == "w_in":
        return jnp.pad(w, ((0, IN_SHARD_PAD - IN_SHARD), (0, 0)))
    return w


def _full_of_chunks(n, g):
    _, a, b = g.shape
    if n == "w_in":
        w = g[:, :IN_SHARD].reshape(IN_COLS, b)
        gap = jnp.zeros((SEG - FOX_LOGICAL, b), g.dtype)
        return jnp.concatenate([w[:HG_COLS + FOX_LOGICAL], gap, w[HG_COLS + FOX_LOGICAL:]], axis=0)
    if BIG_SHAPE[n][2] == 0:
        return g.reshape(N_DEV * a, b)
    return g.transpose(1, 0, 2).reshape(a, N_DEV * b)


def _chunks_of_full(n, g):
    if n == "w_in":
        w = jnp.concatenate([g[:HG_COLS + FOX_LOGICAL], g[2 * SEG:]], axis=0).reshape(N_DEV, IN_SHARD, g.shape[1])
        return jnp.pad(w, ((0, 0), (0, IN_SHARD_PAD - IN_SHARD), (0, 0)))
    if BIG_SHAPE[n][2] == 0:
        return g.reshape(N_DEV, g.shape[0] // N_DEV, g.shape[1])
    return g.reshape(g.shape[0], N_DEV, g.shape[1] // N_DEV).transpose(1, 0, 2)


def _pack_small(vals, loss_row=None):
    parts = [vals[n].reshape(SMALL_ROWS[n], -1) for n in SMALL]
    parts = [jnp.pad(v, ((0, 0), (0, LANES - v.shape[1]))) for v in parts]
    if loss_row is not None:
        parts.append(loss_row)
    return _pack_rows(parts, SMALL_TOTAL)


def _unpack_small(buf, like):
    out, r0 = {}, 0
    for n in SMALL:
        rows, size = SMALL_ROWS[n], like[n].size
        blk = buf[r0:r0 + rows]
        out[n] = (blk if size == rows * LANES else blk[:, :size]).reshape(like[n].shape)
        r0 += rows
    return out


def _place():
    return lax.axis_index("x"), lax.axis_index("y"), lax.axis_index("c")


def _gather_steps(x_refs, out_refs, send_sems, recv_sems, local_sems):
    n = len(x_refs)
    x, y, c = _place()
    me, sibling = (x, y, c), (x, y, 1 - c)
    chips = [(1 - x, y), (x, 1 - y), (1 - x, 1 - y)]

    def slot(i, px, py, pc):
        return out_refs[i].at[4 * px + 2 * py + pc]

    def copy(k, i, blk, to, own=False):
        return pltpu.make_async_remote_copy(
            src_ref=x_refs[i] if own else slot(i, *blk), dst_ref=slot(i, *blk),
            send_sem=send_sems.at[k, i], recv_sem=recv_sems.at[k, i], device_id=to, device_id_type=MESH)

    def mine():
        return [pltpu.make_async_copy(x_refs[i], slot(i, *me), local_sems.at[i]) for i in range(n)]

    def first():
        cps = [copy(0, i, me, sibling, own=True) for i in range(n)]
        return cps + [copy(1 + j, i, me, (*chip, c), own=True) for j, chip in enumerate(chips) for i in range(n)]

    def passed():
        return [copy(4 + j, i, (*chip, c), sibling) for j, chip in enumerate(chips) for i in range(n)]

    def start():
        for cp in mine() + first():
            cp.start()

    def forward():
        fws = passed()
        for j, chip in enumerate(chips):
            for i in range(n):
                copy(1 + j, i, (*chip, c), me).wait_recv()
                fws[j * n + i].start()

    def finish():
        for i in range(n):
            copy(0, i, sibling, me).wait_recv()
        for j, chip in enumerate(chips):
            for i in range(n):
                copy(4 + j, i, (*chip, 1 - c), me).wait_recv()
        for cp in first() + passed():
            cp.wait_send()
        for cp in mine():
            cp.wait()

    return start, forward, finish


def _gather_scratch(n):
    return [pltpu.SemaphoreType.DMA((7, n)), pltpu.SemaphoreType.DMA((7, n)), pltpu.SemaphoreType.DMA((n,))]


def _sibling_steps(g_refs, out_refs, send_sems, recv_sems):
    n = len(g_refs)
    x, y, c = _place()

    def copies():
        return [pltpu.make_async_remote_copy(
            src_ref=g_refs[i].at[:, pl.ds(1 - c, 1)], dst_ref=out_refs[i], send_sem=send_sems.at[i],
            recv_sem=recv_sems.at[i], device_id=(x, y, 1 - c), device_id_type=MESH) for i in range(n)]

    def start():
        for cp in copies():
            cp.start()

    def finish():
        for cp in copies():
            cp.wait()

    return start, finish


def _sibling_shapes(gs):
    return [jax.ShapeDtypeStruct((N_CHIP, 1) + g.shape[2:], g.dtype) for g in gs]


def _sibling_scratch(n):
    return [pltpu.SemaphoreType.DMA((n,)), pltpu.SemaphoreType.DMA((n,))]


def _sibling_exchange(gs, *, name):
    n = len(gs)

    def body(*refs):
        for step in _sibling_steps(refs[:n], refs[n:2 * n], *refs[2 * n:]):
            step()

    hbm = pl.BlockSpec(memory_space=pl.ANY)
    return _pallas(
        body, name=name, out_shape=_sibling_shapes(gs), in_specs=[hbm] * n, out_specs=[hbm] * n,
        scratch_shapes=_sibling_scratch(n),
    )(*gs)


def _chip_sum(g4s, gots, core, *, name):
    n = len(g4s)

    def body(c_ref, *refs):
        for g_ref, r_ref, h_ref in zip(refs[:n], refs[n:2 * n], refs[2 * n:]):
            h_ref[0] = (g_ref[0, 0].astype(F32) + r_ref[0, 0].astype(F32)).astype(BF16)

    shapes = [g.shape[2:] for g in g4s]
    grid_spec = pltpu.PrefetchScalarGridSpec(
        num_scalar_prefetch=1, grid=(N_CHIP,),
        in_specs=[pl.BlockSpec((1, 1) + s, lambda j, c: (j, c[0], 0, 0)) for s in shapes]
        + [pl.BlockSpec((1, 1) + s, lambda j, c: (j, 0, 0, 0)) for s in shapes],
        out_specs=[pl.BlockSpec((1,) + s, lambda j, c: (j, 0, 0)) for s in shapes])
    return list(_pallas(
        body, name=name, grid_spec=grid_spec, out_shape=[jax.ShapeDtypeStruct((N_CHIP,) + s, BF16) for s in shapes],
        compiler_params=_params(dimension_semantics=("arbitrary",)),
    )(core, *g4s, *gots))


def _chip_exchange_steps(h_refs, out_refs, send_sems, recv_sems):
    n = len(h_refs)
    x, y, c = _place()
    chips = [(1 - x, y), (x, 1 - y), (1 - x, 1 - y)]

    def copies():
        return [pltpu.make_async_remote_copy(
            src_ref=h_refs[i].at[2 * px + py], dst_ref=out_refs[i].at[k], send_sem=send_sems.at[k, i],
            recv_sem=recv_sems.at[k, i], device_id=(px, py, c), device_id_type=MESH)
            for k, (px, py) in enumerate(chips) for i in range(n)]

    def start():
        for cp in copies():
            cp.start()

    def finish():
        for cp in copies():
            cp.wait()

    return start, finish


def _chip_exchange_shapes(hs):
    return [jax.ShapeDtypeStruct((3,) + h.shape[1:], h.dtype) for h in hs]


def _chip_exchange_scratch(n):
    return [pltpu.SemaphoreType.DMA((3, n)), pltpu.SemaphoreType.DMA((3, n))]


def _chunks4(n, g):
    gc = _chunks_of_full(n, g)
    return gc.reshape((N_CHIP, 2) + gc.shape[1:])


def _sibling_sums(G, core, *, tag):
    g4 = [_chunks4(n, g) for n, g in G.items()]
    got = _sibling_exchange(g4, name="grads_to_sibling_" + tag)
    return _chip_sum(g4, got, core, name="chip_sum_" + tag)


def _adam_math(w, g, m, v):
    m = ADAM_B1 * m + (1.0 - ADAM_B1) * g
    v = ADAM_B2 * v + (1.0 - ADAM_B2) * (g * g)
    m_hat = m / (1.0 - ADAM_B1 ** ADAM_STEP)
    v_hat = v / (1.0 - ADAM_B2 ** ADAM_STEP)
    delta = -ADAM_LR * (m_hat / (jnp.sqrt(v_hat) + ADAM_EPS) + ADAM_WD * w)
    return delta, m, v


def _adam_shard(hb, got, chip, w, m, v, *, name):
    _, r, c = w.shape
    _, a, cb = hb.shape
    assert cb == c and c % LANES == 0, (hb.shape, w.shape)
    tc = _pick(c, 2 * LANES)

    def body(j_ref, h_ref, r_ref, w_ref, m_ref, v_ref, g_ref, d_ref, nm_ref, nv_ref):
        parts = [h_ref[0], r_ref[0], r_ref[1], r_ref[2]]
        g = None
        for part in parts:
            part = part[:r].astype(F32)
            g = part if g is None else g + part
        d, nm, nv = _adam_math(w_ref[0], g, m_ref[0], v_ref[0])
        g_ref[0] = g
        d_ref[0] = d
        nm_ref[0] = nm
        nv_ref[0] = nv

    blk = pl.BlockSpec((1, r, tc), lambda i, j: (0, 0, i))
    grid_spec = pltpu.PrefetchScalarGridSpec(
        num_scalar_prefetch=1, grid=(c // tc,),
        in_specs=[pl.BlockSpec((1, a, tc), lambda i, j: (j[0], 0, i)),
                  pl.BlockSpec((3, a, tc), lambda i, j: (0, 0, i)), blk, blk, blk],
        out_specs=[blk] * 4)
    return _pallas(
        body, name=name, grid_spec=grid_spec, out_shape=[jax.ShapeDtypeStruct((1, r, c), F32)] * 4,
        compiler_params=_params(dimension_semantics=("arbitrary",)),
    )(chip, hb, got, w, m, v)


def _small_all_reduce_adam(gs, w, m, v):
    def body(g_ref, w_ref, m_ref, v_ref, sum_ref, d_ref, nm_ref, nv_ref, gather, send_sems, recv_sems):
        x, y, c = _place()
        my = 4 * x + 2 * y + c
        gather[my] = g_ref[...]
        cps = []
        for k in range(1, N_DEV):
            to = (x ^ (k >> 2), y ^ ((k >> 1) & 1), c ^ (k & 1))
            cps.append(pltpu.make_async_remote_copy(
                src_ref=g_ref, dst_ref=gather.at[my], send_sem=send_sems.at[k - 1], recv_sem=recv_sems.at[k - 1],
                device_id=to, device_id_type=MESH))
        for cp in cps:
            cp.start()
        for cp in cps:
            cp.wait()
        total = gather[0]
        for d in range(1, N_DEV):
            total = total + gather[d]
        dlt, nm, nv = _adam_math(w_ref[...], total, m_ref[...], v_ref[...])
        sum_ref[...] = total
        d_ref[...] = dlt
        nm_ref[...] = nm
        nv_ref[...] = nv

    vm = pl.BlockSpec(memory_space=pltpu.VMEM)
    return _pallas(
        body, name="small_all_reduce_adam", out_shape=[jax.ShapeDtypeStruct((SMALL_TOTAL, LANES), F32)] * 4,
        in_specs=[vm] * 4, out_specs=[vm] * 4,
        scratch_shapes=[pltpu.VMEM((N_DEV, SMALL_TOTAL, LANES), F32), pltpu.SemaphoreType.DMA((7,)),
                        pltpu.SemaphoreType.DMA((7,))],
            )(gs, w, m, v)


def kernel(x, p, norm_mix_g, w_in, hg_lb_logits, hg_onorm_g, fox_f_bias, fox_q_norm_g, fox_k_norm_g, w_branch_a, w_branch_b, w_out, norm_ffn_g, w_ffn_gate, w_ffn_up, w_ffn_down, norm_ple_g, w_ple_gate, w_ple_proj, loss_target, m_norm_mix_g, m_w_in, m_hg_lb_logits, m_hg_onorm_g, m_fox_f_bias, m_fox_q_norm_g, m_fox_k_norm_g, m_w_branch_a, m_w_branch_b, m_w_out, m_norm_ffn_g, m_w_ffn_gate, m_w_ffn_up, m_w_ffn_down, m_norm_ple_g, m_w_ple_gate, m_w_ple_proj, v_norm_mix_g, v_w_in, v_hg_lb_logits, v_hg_onorm_g, v_fox_f_bias, v_fox_q_norm_g, v_fox_k_norm_g, v_w_branch_a, v_w_branch_b, v_w_out, v_norm_ffn_g, v_w_ffn_gate, v_w_ffn_up, v_w_ffn_down, v_norm_ple_g, v_w_ple_gate, v_w_ple_proj):
    args = dict(locals())
    wts = {n: args[n] for n in BIG + SMALL}
    mom = {n: args["m_" + n] for n in BIG + SMALL}
    var = {n: args["v_" + n] for n in BIG + SMALL}
    for group in (wts, mom, var):
        for n in TRANSPOSED:
            group[n] = jnp.swapaxes(group[n], 1, 2)
    sm = {n: wts[n] for n in SMALL}

    xi, yi, ci = _place()
    core = jnp.reshape(ci, (1,)).astype(jnp.int32)
    chip = jnp.reshape(2 * xi + yi, (1,)).astype(jnp.int32)
    chunks = [_chunk_of_shard(n, wts[n][0].astype(BF16)) for n in BIG]
    loss_blk, grad_x, gs, hb, got = _local_step(x[0], p[0, 0], loss_target[0], sm, chunks, core)

    g_big, d_big, nm_big, nv_big = {}, {}, {}, {}
    for n, h, r in zip(BIG, hb, got):
        res = _adam_shard(h, r, chip, wts[n], mom[n], var[n], name="adam_" + n)
        if n in TRANSPOSED:
            res = [jnp.swapaxes(t, 1, 2) for t in res]
        g_big[n], d_big[n], nm_big[n], nv_big[n] = res

    s_sum, s_d, s_nm, s_nv = _small_all_reduce_adam(
        _pack_small(gs, loss_blk[0:1]), _pack_small(sm), _pack_small({n: mom[n] for n in SMALL}),
        _pack_small({n: var[n] for n in SMALL}))
    loss = s_sum[LOSS_ROW, 0]
    g_small, d_small, nm_small, nv_small = (_unpack_small(t, sm) for t in (s_sum, s_d, s_nm, s_nv))

    order = ["norm_mix_g", "w_in", "hg_lb_logits", "hg_onorm_g", "fox_f_bias", "fox_q_norm_g", "fox_k_norm_g",
             "w_branch_a", "w_branch_b", "w_out", "norm_ffn_g", "w_ffn_gate", "w_ffn_up", "w_ffn_down", "norm_ple_g",
             "w_ple_gate", "w_ple_proj"]
    outs = [loss, grad_x[None]]
    for big, small in ((g_big, g_small), (d_big, d_small), (nm_big, nm_small), (nv_big, nv_small)):
        outs += [big[n] if n in big else small[n] for n in order]
    return tuple(outs)
```

```python
import functools

import jax
import jax.numpy as jnp
from jax import lax
from jax.experimental import pallas as pl
from jax.experimental.pallas import tpu as pltpu

F32 = jnp.float32
BF16 = jnp.bfloat16

D_MODEL = 1024
PLE_DIM = 256
HG_HEADS = 4
HG_DK = 128
HG_CHUNK = 64
HG_SUB = 16
HG_W = HG_HEADS * HG_DK
FOX_HEADS = 8
FOX_DH = 64
FOX_W = FOX_HEADS * FOX_DH
D_FF = 2816
EPS = 1e-6
N_DEV = 8
N_CHIP = 4
LANES = 128
FOX_COLS = 3 * FOX_W + LANES
HG_COLS = 4 * HG_W
GATE_COLS = 2 * D_MODEL
IN_COLS = HG_COLS + 3 * FOX_W + FOX_HEADS + GATE_COLS
FOX_LOGICAL = 3 * FOX_W + FOX_HEADS
SEG = 2048
IN_PAD = 3 * SEG
IN_SHARD = IN_COLS // N_DEV
IN_SHARD_PAD = 720
EXP_CLAMP = 80.0
LOG2E = 1.4426950408889634

ADAM_LR = 0.001
ADAM_B1 = 0.9
ADAM_B2 = 0.999
ADAM_EPS = 1e-08
ADAM_WD = 0.01
ADAM_STEP = 10

MESH = pl.DeviceIdType.MESH
VMEM_LIMIT = 56 * 1024 * 1024
ROW_BLOCK = 512
MATMUL_TN = 2048
MATMUL_TK = 3072
MATMUL_TM_T = 1408
FFN_TILE = 1408

BIG = ["w_in", "w_branch_a", "w_branch_b", "w_out", "w_ffn_gate", "w_ffn_up", "w_ffn_down",
       "w_ple_gate", "w_ple_proj"]
TRANSPOSED = ("w_in", "w_ffn_gate", "w_ffn_up")
BIG_SHAPE = {
    "w_in": (IN_COLS, D_MODEL, 0), "w_branch_a": (HG_W, D_MODEL, 1), "w_branch_b": (FOX_W, D_MODEL, 1),
    "w_out": (D_MODEL, D_MODEL, 0), "w_ffn_gate": (D_FF, D_MODEL, 0), "w_ffn_up": (D_FF, D_MODEL, 0),
    "w_ffn_down": (D_FF, D_MODEL, 0), "w_ple_gate": (D_MODEL, D_MODEL, 0), "w_ple_proj": (PLE_DIM, D_MODEL, 1),
}

SMALL = ["norm_mix_g", "hg_lb_logits", "hg_onorm_g", "fox_f_bias", "fox_q_norm_g", "fox_k_norm_g",
         "norm_ffn_g", "norm_ple_g"]
SMALL_ROWS = {"norm_mix_g": 8, "hg_lb_logits": 8, "hg_onorm_g": 1, "fox_f_bias": 1, "fox_q_norm_g": 1,
              "fox_k_norm_g": 1, "norm_ffn_g": 8, "norm_ple_g": 8}
SMALL_TOTAL = 40
LOSS_ROW = 36


def _pallas(body, **kw):
    return pl.pallas_call(body, **kw)


def _params(**kw):
    return pltpu.CompilerParams(vmem_limit_bytes=VMEM_LIMIT, **kw)


def _pick(n, target):
    if n <= target:
        return n
    best = None
    for t in range(LANES, target + 1, LANES):
        if n % t == 0:
            best = t
    assert best is not None, (n, target)
    return best


def _dot(a, b, ca, cb):
    return lax.dot_general(a, b, (((ca,), (cb,)), ((), ())), preferred_element_type=F32)


def _split_dot(mat, x, ca, cb, terms=2, mat_first=True):
    acc = None
    rem = x
    for _ in range(terms):
        part = rem.astype(BF16)
        rem = rem - part.astype(F32)
        p = _dot(mat, part, ca, cb) if mat_first else _dot(part, mat, ca, cb)
        acc = p if acc is None else acc + p
    return acc


def _sigmoid(x):
    return 1.0 / (1.0 + jnp.exp(-x))


def _iota(shape, dim):
    return lax.broadcasted_iota(jnp.int32, shape, dim)


def _matmul(a, b, *, name, ta=False, tb=False, out_dtype=F32, add=None, exchange=None, gather=None,
            norm_fwd=None, norm_bwd=None):
    assert exchange is None or gather is None
    (K, M) = a.shape if ta else a.shape[::-1]
    (N, Kb) = b.shape if tb else b.shape[::-1]
    assert K == Kb, (a.shape, b.shape, ta, tb)
    if ta:
        tm, tn, tk = _pick(M, MATMUL_TM_T), _pick(N, 2 * ROW_BLOCK), _pick(K, 4 * ROW_BLOCK)
    else:
        tm, tn, tk = _pick(M, 2 * ROW_BLOCK), _pick(N, MATMUL_TN), _pick(K, MATMUL_TK)
    if norm_bwd is not None:
        tm = _pick(M, ROW_BLOCK)
    nk = K // tk
    use_scratch = nk > 1 and out_dtype != F32
    if norm_fwd is not None or norm_bwd is not None:
        assert tn == N and not use_scratch and out_dtype == F32

    hs = list(exchange or gather or [])
    n_x = len(hs)
    grid = (M // tm, N // tn, nk)
    a_spec = pl.BlockSpec((tk, tm), lambda i, j, k: (k, i)) if ta else pl.BlockSpec((tm, tk), lambda i, j, k: (i, k))
    b_spec = pl.BlockSpec((tn, tk), lambda i, j, k: (j, k)) if tb else pl.BlockSpec((tk, tn), lambda i, j, k: (k, j))
    o_spec = pl.BlockSpec((tm, tn), lambda i, j, k: (i, j))
    row_vec = pl.BlockSpec((1, N), lambda i, j, k: (0, 0))
    hbm = pl.BlockSpec(memory_space=pl.ANY)
    extra_in = [(add, o_spec)] if add is not None else []
    extra_out = []
    if norm_fwd is not None:
        extra_in += [(norm_fwd, row_vec)]
        extra_out += [(jax.ShapeDtypeStruct((M, N), BF16), o_spec)]
    if norm_bwd is not None:
        extra_in += [(norm_bwd[0], o_spec), (norm_bwd[1], row_vec), (norm_bwd[2], o_spec)]
        extra_out += [(jax.ShapeDtypeStruct((1, N), F32), row_vec)]
    if gather is not None:
        ride_shapes, ride_scratch = [jax.ShapeDtypeStruct((N_DEV,) + h.shape, h.dtype) for h in hs], _gather_scratch(n_x)
    else:
        ride_shapes, ride_scratch = _chip_exchange_shapes(hs), (_chip_exchange_scratch(n_x) if n_x else [])
    n_ex_in, n_ex_out = len(extra_in), len(extra_out)

    def body(*refs):
        refs = list(refs)
        a_ref, b_ref = refs[:2]
        ex_in = refs[2:2 + n_ex_in]
        ride_in = refs[2 + n_ex_in:2 + n_ex_in + n_x]
        base = 2 + n_ex_in + n_x
        o_ref = refs[base]
        ex_out = refs[base + 1:base + 1 + n_ex_out]
        ride_out = refs[base + 1 + n_ex_out:base + 1 + n_ex_out + n_x]
        scratch = refs[base + 1 + n_ex_out + n_x:]
        at = [pl.program_id(d) for d in range(3)]
        k = at[2]
        if n_x:
            steps = _gather_steps if gather is not None else _chip_exchange_steps
            ride = steps(ride_in, ride_out, *scratch[-len(ride_scratch):])

            @pl.when(jnp.logical_and(at[0] == 0, jnp.logical_and(at[1] == 0, at[2] == 0)))
            def _():
                ride[0]()
        p = _dot(a_ref[...].astype(BF16), b_ref[...].astype(BF16), 0 if ta else 1, 1 if tb else 0)

        def finish(r):
            ins = list(ex_in)
            outs = list(ex_out)
            if add is not None:
                r = r + ins.pop(0)[...].astype(F32)
            if norm_fwd is not None:
                g_ref = ins.pop(0)
                rstd = lax.rsqrt(jnp.mean(r * r, axis=-1, keepdims=True) + EPS)
                outs.pop(0)[...] = (r * rstd * g_ref[...]).astype(BF16)
            if norm_bwd is not None:
                x_ref, g_ref, dres_ref = ins.pop(0), ins.pop(0), ins.pop(0)
                dg_ref = outs.pop(0)
                xv = x_ref[...]
                rstd = lax.rsqrt(jnp.mean(xv * xv, axis=-1, keepdims=True) + EPS)
                xh = xv * rstd
                part = jnp.sum(r * xh, axis=0, keepdims=True)

                @pl.when(at[0] == 0)
                def _():
                    dg_ref[...] = part

                @pl.when(at[0] > 0)
                def _():
                    dg_ref[...] += part

                dxh = r * g_ref[...]
                r = rstd * (dxh - xh * jnp.mean(dxh * xh, axis=-1, keepdims=True)) + dres_ref[...]
            o_ref[...] = r.astype(out_dtype)

        if nk == 1:
            finish(p)
        elif not use_scratch:
            @pl.when(k == 0)
            def _():
                o_ref[...] = p

            @pl.when(jnp.logical_and(k > 0, k < nk - 1))
            def _():
                o_ref[...] += p

            @pl.when(k == nk - 1)
            def _():
                finish(o_ref[...] + p)
        else:
            acc_ref = scratch[0]

            @pl.when(k == 0)
            def _():
                acc_ref[...] = p

            @pl.when(k > 0)
            def _():
                acc_ref[...] += p

            @pl.when(k == nk - 1)
            def _():
                finish(acc_ref[...])

        if n_x:
            @pl.when(jnp.logical_and(at[0] == grid[0] - 1, jnp.logical_and(at[1] == grid[1] - 1, at[2] == nk - 1)))
            def _():
                for step in ride[1:]:
                    step()

    res = _pallas(
        body, name=name, grid=grid,
        in_specs=[a_spec, b_spec] + [s for _, s in extra_in] + [hbm] * n_x,
        out_specs=[o_spec] + [s for _, s in extra_out] + [hbm] * n_x,
        out_shape=[jax.ShapeDtypeStruct((M, N), out_dtype)] + [s for s, _ in extra_out] + ride_shapes,
        scratch_shapes=([pltpu.VMEM((tm, tn), F32)] if use_scratch else []) + ride_scratch,
        compiler_params=_params(dimension_semantics=("arbitrary",) * 3),
    )(a, b, *[v for v, _ in extra_in], *hs)
    res = list(res)
    main = res[0] if n_ex_out == 0 else tuple(res[:1 + n_ex_out])
    return (main, res[1 + n_ex_out:]) if n_x else main


def _row_map(nb, reverse, seg):
    if reverse:
        return lambda i: (nb - 1 - i, seg)
    return lambda i: (i, seg)


def _row_call(body, *, name, T, ins, outs, acc_outs=(), tm=ROW_BLOCK, reverse=False, swap=None):
    tm = min(tm, T)
    nb = T // tm
    in_specs, args = [], []
    for arr, how in ins:
        args.append(arr)
        if how is True:
            in_specs.append(pl.BlockSpec((tm, arr.shape[1]), _row_map(nb, reverse, 0)))
        elif how is False:
            in_specs.append(pl.BlockSpec(arr.shape, lambda i, _n=arr.ndim: (0,) * _n))
        else:
            in_specs.append(pl.BlockSpec((tm, SEG), _row_map(nb, reverse, how[0])))
    out_specs, out_shape = [], []
    for o in outs:
        c, dt = o[0], o[1]
        total, seg = o[2] if len(o) > 2 else (c, 0)
        out_specs.append(pl.BlockSpec((tm, c), _row_map(nb, reverse, seg)))
        out_shape.append(jax.ShapeDtypeStruct((T, total), dt))
    for shp, dt in acc_outs:
        out_specs.append(pl.BlockSpec(shp, lambda i, _n=len(shp): (0,) * _n))
        out_shape.append(jax.ShapeDtypeStruct(shp, dt))
    if swap is None:
        return _pallas(body, name=name, grid=(nb,), in_specs=in_specs, out_specs=out_specs, out_shape=out_shape,
                       compiler_params=_params(dimension_semantics=("arbitrary",)))(*args)
    n, n_in, n_out = len(swap), len(in_specs), len(out_specs)
    hbm = pl.BlockSpec(memory_space=pl.ANY)

    def riding(*refs):
        mine = refs[:n_in] + refs[n_in + n:n_in + n + n_out]
        start, finish = _sibling_steps(refs[n_in:n_in + n], refs[n_in + n + n_out:n_in + 2 * n + n_out], *refs[-2:])

        @pl.when(pl.program_id(0) == 0)
        def _():
            start()

        body(*mine)

        @pl.when(pl.program_id(0) == nb - 1)
        def _():
            finish()

    res = list(_pallas(
        riding, name=name, grid=(nb,), in_specs=in_specs + [hbm] * n, out_specs=out_specs + [hbm] * n,
        out_shape=out_shape + _sibling_shapes(swap), scratch_shapes=_sibling_scratch(n),
        compiler_params=_params(dimension_semantics=("arbitrary",)))(*args, *swap))
    return res[:n_out], res[n_out:]


def _rms_fwd(x, g, blocks, *, name):
    T, D = x.shape
    tm = min(ROW_BLOCK, T)
    nb = T // tm
    n = len(blocks)

    def body(*refs):
        x_ref, g_ref = refs[:2]
        h_ref = refs[2 + n]
        g_start, g_forward, g_finish = _gather_steps(refs[2:2 + n], refs[3 + n:3 + 2 * n], *refs[3 + 2 * n:])

        @pl.when(pl.program_id(0) == 0)
        def _():
            g_start()

        xv = x_ref[...]
        rstd = lax.rsqrt(jnp.mean(xv * xv, axis=-1, keepdims=True) + EPS)
        h_ref[...] = (xv * rstd * g_ref[...]).astype(BF16)

        @pl.when(pl.program_id(0) == nb - 1)
        def _():
            g_forward()
            g_finish()

    hbm = pl.BlockSpec(memory_space=pl.ANY)
    res = _pallas(
        body, name=name, grid=(nb,),
        in_specs=[pl.BlockSpec((tm, D), lambda i: (i, 0)), pl.BlockSpec((1, D), lambda i: (0, 0))] + [hbm] * n,
        out_specs=[pl.BlockSpec((tm, D), lambda i: (i, 0))] + [hbm] * n,
        out_shape=[jax.ShapeDtypeStruct((T, D), BF16)] + [jax.ShapeDtypeStruct((N_DEV,) + b.shape, b.dtype) for b in blocks],
        scratch_shapes=_gather_scratch(n),
        compiler_params=_params(dimension_semantics=("arbitrary",)),
    )(x, g, *blocks)
    return res[0], list(res[1:])


def _merge_fwd(ya, yb, wa, wb, zg):
    def body(ya_ref, yb_ref, wa_ref, wb_ref, zg_ref, m_ref, ua_ref, ub_ref):
        ua = _dot(ya_ref[...].astype(BF16), wa_ref[...], 1, 0)
        ub = _dot(yb_ref[...].astype(BF16), wb_ref[...], 1, 0)
        ga = _sigmoid(zg_ref[:, :D_MODEL])
        gb = _sigmoid(zg_ref[:, D_MODEL:])
        m_ref[...] = (ga * ua + gb * ub).astype(BF16)
        ua_ref[...] = ua.astype(BF16)
        ub_ref[...] = ub.astype(BF16)

    return _row_call(body, name="merge_fwd", T=ya.shape[0],
                     ins=[(ya, True), (yb, True), (wa, False), (wb, False), (zg, (2,))],
                     outs=[(D_MODEL, BF16)] * 3)


def _merge_bwd(dx1, w_out, wa, wb, ua, ub, zg, swap):
    def body(dx_ref, wo_ref, wa_ref, wb_ref, ua_ref, ub_ref, zg_ref, dua_ref, dub_ref, dzg_ref, dya_ref, dyb_ref):
        dmv = _dot(dx_ref[...].astype(BF16), wo_ref[...], 1, 1)
        ga = _sigmoid(zg_ref[:, :D_MODEL])
        gb = _sigmoid(zg_ref[:, D_MODEL:])
        dua = (dmv * ga).astype(BF16)
        dub = (dmv * gb).astype(BF16)
        dua_ref[...] = dua
        dub_ref[...] = dub
        dzg_ref[:, :D_MODEL] = (dmv * ua_ref[...].astype(F32) * ga * (1.0 - ga)).astype(BF16)
        dzg_ref[:, D_MODEL:] = (dmv * ub_ref[...].astype(F32) * gb * (1.0 - gb)).astype(BF16)
        dya_ref[...] = _dot(dua, wa_ref[...], 1, 1)
        dyb_ref[...] = _dot(dub, wb_ref[...], 1, 1)

    return _row_call(body, name="merge_bwd", T=dx1.shape[0],
                     ins=[(dx1, True), (w_out, False), (wa, False), (wb, False), (ua, True), (ub, True), (zg, (2,))],
                     outs=[(D_MODEL, BF16), (D_MODEL, BF16), (SEG, BF16, (IN_PAD, 2)), (HG_W, F32), (FOX_W, F32)],
                     swap=swap)


def _swiglu_fwd(hf, w_gate, w_up):
    T, D = hf.shape
    F = w_gate.shape[0]
    tm, tn = _pick(T, ROW_BLOCK), _pick(F, FFN_TILE)

    def body(h_ref, wg_ref, wu_ref, a_ref, b_ref, o_ref):
        hv = h_ref[...]
        a_b = _dot(hv, wg_ref[...], 1, 1).astype(BF16)
        b_b = _dot(hv, wu_ref[...], 1, 1).astype(BF16)
        a_ref[...] = a_b
        b_ref[...] = b_b
        av = a_b.astype(F32)
        o_ref[...] = (av * _sigmoid(av) * b_b.astype(F32)).astype(BF16)

    tile = pl.BlockSpec((tm, tn), lambda j, i: (i, j))
    wcol = pl.BlockSpec((tn, D), lambda j, i: (j, 0))
    return _pallas(
        body, name="swiglu_fwd", grid=(F // tn, T // tm),
        in_specs=[pl.BlockSpec((tm, D), lambda j, i: (i, 0)), wcol, wcol],
        out_specs=[tile] * 3, out_shape=[jax.ShapeDtypeStruct((T, F), BF16)] * 3,
        compiler_params=_params(dimension_semantics=("arbitrary",) * 2),
    )(hf, w_gate, w_up)


def _swiglu_bwd(dx, w_down, a, b):
    T, D = dx.shape
    F = w_down.shape[0]
    tm, tn = _pick(T, ROW_BLOCK), _pick(F, FFN_TILE)

    def body(dx_ref, w_ref, a_ref, b_ref, da_ref, db_ref):
        dact = _dot(dx_ref[...].astype(BF16), w_ref[...], 1, 1)
        av = a_ref[...].astype(F32)
        bv = b_ref[...].astype(F32)
        sg = _sigmoid(av)
        da_ref[...] = (dact * bv * sg * (1.0 + av * (1.0 - sg))).astype(BF16)
        db_ref[...] = (dact * av * sg).astype(BF16)

    tile = pl.BlockSpec((tm, tn), lambda j, i: (i, j))
    return _pallas(
        body, name="swiglu_bwd", grid=(F // tn, T // tm),
        in_specs=[pl.BlockSpec((tm, D), lambda j, i: (i, 0)), pl.BlockSpec((tn, D), lambda j, i: (j, 0)), tile, tile],
        out_specs=[tile, tile], out_shape=[jax.ShapeDtypeStruct((T, F), BF16)] * 2,
        compiler_params=_params(dimension_semantics=("arbitrary",) * 2),
    )(dx, w_down, a, b)


def _ple_loss(x2, p, g, w_gate, w_proj, tgt):
    def body(x_ref, p_ref, g_ref, wg_ref, wp_ref, t_ref, hp_ref, dy_ref, dsp_ref, dpp_ref, loss_ref):
        xv = x_ref[...]
        rstd = lax.rsqrt(jnp.mean(xv * xv, axis=-1, keepdims=True) + EPS)
        hp = (xv * rstd * g_ref[...]).astype(BF16)
        hp_ref[...] = hp
        gp = _sigmoid(_dot(hp, wg_ref[...], 1, 0))
        ppv = _dot(p_ref[...].astype(BF16), wp_ref[...], 1, 0)
        err = xv + gp * ppv - t_ref[...]
        part = 0.5 * jnp.sum(jnp.mean(err * err, axis=-1, keepdims=True), axis=0, keepdims=True)
        part = jnp.broadcast_to(part, loss_ref.shape)

        @pl.when(pl.program_id(0) == 0)
        def _():
            loss_ref[...] = part

        @pl.when(pl.program_id(0) > 0)
        def _():
            loss_ref[...] += part

        dy = err * (1.0 / D_MODEL)
        dy_ref[...] = dy
        dsp_ref[...] = (dy * ppv * gp * (1.0 - gp)).astype(BF16)
        dpp_ref[...] = (dy * gp).astype(BF16)

    return _row_call(body, name="ple_loss", T=x2.shape[0],
                     ins=[(x2, True), (p, True), (g, False), (w_gate, False), (w_proj, False), (tgt, True)],
                     outs=[(D_MODEL, BF16), (D_MODEL, F32), (D_MODEL, BF16), (D_MODEL, BF16)],
                     acc_outs=[((8, LANES), F32)])


def _hg_tri():
    C = HG_CHUNK
    return _iota((C, C), 1) <= _iota((C, C), 0)


def _hg_ref_row(b, blk):
    mid = blk * HG_SUB + HG_SUB // 2 - 1
    return b[mid:mid + 1]


def _hg_chunk_fwd(q, f, lb, tri_b):
    sgq = _sigmoid(q)
    qt = q * sgq
    sg = _sigmoid(f)
    fg = lb + (1.0 - lb) * sg
    kf = (1.0 - lb) * (1.0 - sg)
    logf = jnp.log(fg)
    b = _split_dot(tri_b, logf, 1, 0)
    refs = [jnp.broadcast_to(_hg_ref_row(b, blk), (HG_SUB, b.shape[1])) for blk in range(HG_CHUNK // HG_SUB)]
    w = jnp.minimum(b - jnp.concatenate(refs, axis=0), EXP_CLAMP)
    return sgq, qt, sg, fg, kf, b, w


def _hg_scores(qs_b, kf, b, row):
    C, S = HG_CHUNK, HG_SUB
    parts, ks = [], []
    for blk in range(C // S):
        e = jnp.exp(jnp.minimum(_hg_ref_row(b, blk) - b, EXP_CLAMP))
        e = jnp.where(row < (blk + 1) * S, e, 0.0)
        k_b = (kf * e).astype(BF16)
        ks.append((e, k_b))
        parts.append(_dot(qs_b[blk * S:(blk + 1) * S], k_b, 1, 1))
    return jnp.concatenate(parts, axis=0), ks


def _hgrn_fwd(z, lb_logits, gain, blocks):
    T = z.shape[0]
    RB = min(ROW_BLOCK, T)
    nb, cpb = T // RB, RB // HG_CHUNK
    C, DK = HG_CHUNK, HG_DK
    n = len(blocks)

    def body(*refs):
        z_ref, lg_ref, g_ref = refs[:3]
        o_ref, y_ref, st_ref = refs[3 + n:6 + n]
        s_ref = refs[6 + 2 * n]
        g_start, g_forward, g_finish = _gather_steps(refs[3:3 + n], refs[6 + n:6 + 2 * n], *refs[7 + 2 * n:])

        @pl.when(pl.program_id(0) == 0)
        def _():
            s_ref[...] = jnp.zeros_like(s_ref)
            g_start()

        lg = lg_ref[...]
        lb_all = 1.0 / (1.0 + jnp.exp(lg[1:2] - lg[0:1]))
        gain_v = g_ref[...]
        tri = _hg_tri()
        tri_b = tri.astype(BF16)
        row = _iota((C, DK), 0)

        def chunk(ci, carry):
            r0 = pl.multiple_of(ci * C, C)
            rows = pl.ds(r0, C)
            _, qt, _, _, kf_all, b_all, w = _hg_chunk_fwd(z_ref[rows, 0:HG_W], z_ref[rows, HG_W:2 * HG_W], lb_all,
                                                          tri_b)
            qs_all = (qt * jnp.exp(w)).astype(BF16)
            qd_all = (qt * jnp.exp(b_all)).astype(BF16)
            bl_all = b_all[C - 1:C]
            kd_all = (kf_all * jnp.exp(bl_all - b_all)).astype(BF16)
            ebl_all = jnp.exp(bl_all)
            v_all = z_ref[rows, 2 * HG_W:3 * HG_W].astype(BF16)
            g_all = z_ref[rows, 3 * HG_W:4 * HG_W]
            gate_all = g_all * _sigmoid(g_all)
            for h in range(HG_HEADS):
                cs = slice(h * DK, (h + 1) * DK)
                st = s_ref[h]
                st_ref[pl.ds(pl.multiple_of((ci * HG_HEADS + h) * DK, DK), DK), :] = st
                v_b = v_all[:, cs]
                a, _ = _hg_scores(qs_all[:, cs], kf_all[:, cs], b_all[:, cs], row)
                a = jnp.where(tri, a, 0.0)
                o = _dot(qd_all[:, cs], st.astype(BF16), 1, 1) + _dot(a.astype(BF16), v_b, 1, 0)
                s_ref[h] = st * ebl_all[:, cs] + _dot(v_b, kd_all[:, cs], 0, 0)
                o_ref[rows, cs] = o
                rstd = lax.rsqrt(jnp.mean(o * o, axis=-1, keepdims=True) + EPS)
                y_ref[rows, cs] = (o * rstd * gain_v * gate_all[:, cs]).astype(BF16)
            return carry

        lax.fori_loop(0, cpb, chunk, 0, unroll=4)

        @pl.when(pl.program_id(0) == nb - 1)
        def _():
            g_forward()
            g_finish()

    hbm = pl.BlockSpec(memory_space=pl.ANY)
    res = _pallas(
        body, name="hgrn_fwd", grid=(nb,),
        in_specs=[pl.BlockSpec((RB, HG_COLS), lambda i: (i, 0)), pl.BlockSpec((2, HG_W), lambda i: (0, 0)),
                  pl.BlockSpec((1, DK), lambda i: (0, 0))] + [hbm] * n,
        out_specs=[pl.BlockSpec((RB, HG_W), lambda i: (i, 0)), pl.BlockSpec((RB, HG_W), lambda i: (i, 0)),
                   pl.BlockSpec((cpb * HG_HEADS * DK, DK), lambda i: (i, 0))] + [hbm] * n,
        out_shape=[jax.ShapeDtypeStruct((T, HG_W), F32), jax.ShapeDtypeStruct((T, HG_W), BF16),
                   jax.ShapeDtypeStruct((T // C * HG_HEADS * DK, DK), F32)]
        + [jax.ShapeDtypeStruct((N_DEV,) + b.shape, b.dtype) for b in blocks],
        scratch_shapes=[pltpu.VMEM((HG_HEADS, DK, DK), F32)] + _gather_scratch(n),
        compiler_params=_params(dimension_semantics=("arbitrary",)),
    )(z, lb_logits, gain, *blocks)
    return res[0], res[1], res[2], res[3:]


def _hgrn_bwd(z, o_raw, dy, states, lb_logits, gain, dz_buf):
    T = z.shape[0]
    RB = min(ROW_BLOCK, T)
    nb, cpb = T // RB, RB // HG_CHUNK
    C, DK, S = HG_CHUNK, HG_DK, HG_SUB

    def body(z_ref, o_ref, dy_ref, st_ref, lg_ref, g_ref, _buf_ref, dz_ref, dlg_ref, dg_ref, ds_ref, dlb_ref):
        step = pl.program_id(0)

        @pl.when(step == 0)
        def _():
            ds_ref[...] = jnp.zeros_like(ds_ref)
            dlb_ref[...] = jnp.zeros_like(dlb_ref)
            dg_ref[...] = jnp.zeros_like(dg_ref)

        lg = lg_ref[...]
        lb_all = 1.0 / (1.0 + jnp.exp(lg[1:2] - lg[0:1]))
        gain_v = g_ref[...]
        tri = _hg_tri()
        tri_b = tri.astype(BF16)
        row = _iota((C, DK), 0)

        def chunk(cj, carry):
            ci = cpb - 1 - cj
            r0 = pl.multiple_of(ci * C, C)
            rows = pl.ds(r0, C)
            q_all = z_ref[rows, 0:HG_W]
            g_all = z_ref[rows, 3 * HG_W:4 * HG_W]
            sgq_all, qt_all, sg_all, fg_all, kf_all, b_all, w_all = _hg_chunk_fwd(
                q_all, z_ref[rows, HG_W:2 * HG_W], lb_all, tri_b)
            ew_all = jnp.exp(w_all)
            eb_all = jnp.exp(b_all)
            bl_all = b_all[C - 1:C]
            ebl_all = jnp.exp(bl_all)
            ekd_all = jnp.exp(bl_all - b_all)
            qs_all = (qt_all * ew_all).astype(BF16)
            qd_all = (qt_all * eb_all).astype(BF16)
            kd_all = (kf_all * ekd_all).astype(BF16)
            v_all = z_ref[rows, 2 * HG_W:3 * HG_W].astype(BF16)
            sgg_all = _sigmoid(g_all)
            t1_all = dy_ref[rows, :] * (g_all * sgg_all)
            db_heads, dqt_heads, dkf_heads, dv_heads, n_heads = [], [], [], [], []
            for h in range(HG_HEADS):
                cs = slice(h * DK, (h + 1) * DK)
                kf, b, ew, eb, ebl, ekd = kf_all[:, cs], b_all[:, cs], ew_all[:, cs], eb_all[:, cs], ebl_all[:, cs], \
                    ekd_all[:, cs]
                qs_b, qd_b, kd_b, v_b = qs_all[:, cs], qd_all[:, cs], kd_all[:, cs], v_all[:, cs]
                st = st_ref[pl.ds(pl.multiple_of((ci * HG_HEADS + h) * DK, DK), DK), :]
                dst = ds_ref[h]
                o = o_ref[rows, cs]
                rstd = lax.rsqrt(jnp.mean(o * o, axis=-1, keepdims=True) + EPS)
                n = o * rstd
                n_heads.append(n)
                t1 = t1_all[:, cs]
                dg_ref[...] += jnp.sum(t1 * n, axis=0, keepdims=True)
                dn = t1 * gain_v
                do = rstd * (dn - n * jnp.mean(dn * n, axis=-1, keepdims=True))
                do_b = do.astype(BF16)
                a, ks = _hg_scores(qs_b, kf, b, row)
                a = jnp.where(tri, a, 0.0)
                dst_b = dst.astype(BF16)
                dqd = _dot(do_b, st.astype(BF16), 1, 0)
                da = jnp.where(tri, _dot(do_b, v_b, 1, 1), 0.0)
                dv = _dot(a.astype(BF16), do_b, 0, 0) + _dot(kd_b, dst_b, 1, 1)
                dkd = _dot(v_b, dst_b, 1, 0)
                ds_ref[h] = dst * ebl + _dot(do_b, qd_b, 0, 0)
                dkd_kd = dkd * kd_b.astype(F32)
                dbl = ebl * jnp.sum(dst * st, axis=0, keepdims=True) + jnp.sum(dkd_kd, axis=0, keepdims=True)
                da_b = da.astype(BF16)
                dqs_parts = []
                dk_in = jnp.zeros((C, DK), F32)
                db_k = jnp.zeros((C, DK), F32)
                for blk in range(C // S):
                    e, k_b = ks[blk]
                    da_blk = da_b[blk * S:(blk + 1) * S]
                    dqs_parts.append(_dot(da_blk, k_b, 1, 0))
                    dks = _dot(da_blk, qs_b[blk * S:(blk + 1) * S], 0, 0)
                    dk_in = dk_in + dks * e
                    db_k = db_k + dks * k_b.astype(F32)
                dqs = jnp.concatenate(dqs_parts, axis=0)
                db = qs_b.astype(F32) * dqs - db_k + dqd * qd_b.astype(F32) - dkd_kd
                db_heads.append(db + jnp.where(row == C - 1, dbl, 0.0))
                dqt_heads.append(dqs * ew + dqd * eb)
                dkf_heads.append(dk_in + dkd * ekd)
                dv_heads.append(dv)
            dlogf = _split_dot(tri_b, jnp.concatenate(db_heads, axis=1), 0, 0)
            dfg = dlogf / fg_all - jnp.concatenate(dkf_heads, axis=1)
            dlb_ref[...] += jnp.sum(dfg * (1.0 - sg_all), axis=0, keepdims=True)
            dqt = jnp.concatenate(dqt_heads, axis=1)
            n_all = jnp.concatenate(n_heads, axis=1)
            gain_all = jnp.concatenate([gain_v] * HG_HEADS, axis=1)
            dz_ref[rows, 0:HG_W] = (dqt * sgq_all * (1.0 + q_all * (1.0 - sgq_all))).astype(BF16)
            dz_ref[rows, HG_W:2 * HG_W] = (dfg * (1.0 - lb_all) * sg_all * (1.0 - sg_all)).astype(BF16)
            dz_ref[rows, 2 * HG_W:3 * HG_W] = jnp.concatenate(dv_heads, axis=1).astype(BF16)
            dz_ref[rows, 3 * HG_W:4 * HG_W] = (dy_ref[rows, :] * n_all * gain_all * sgg_all
                                               * (1.0 + g_all * (1.0 - sgg_all))).astype(BF16)
            return carry

        lax.fori_loop(0, cpb, chunk, 0, unroll=4)

        @pl.when(step == nb - 1)
        def _():
            d0 = dlb_ref[...] * lb_all * (1.0 - lb_all)
            dlg_ref[0:1, :] = d0
            dlg_ref[1:2, :] = -d0

    rev = lambda i: (nb - 1 - i, 0)
    fix = lambda i: (0, 0)
    return _pallas(
        body, name="hgrn_bwd", grid=(nb,),
        in_specs=[pl.BlockSpec((RB, HG_COLS), rev), pl.BlockSpec((RB, HG_W), rev), pl.BlockSpec((RB, HG_W), rev),
                  pl.BlockSpec((cpb * HG_HEADS * DK, DK), rev), pl.BlockSpec((2, HG_W), fix),
                  pl.BlockSpec((1, DK), fix), pl.BlockSpec(memory_space=pl.ANY)],
        out_specs=[pl.BlockSpec((RB, HG_COLS), rev), pl.BlockSpec((2, HG_W), fix), pl.BlockSpec((1, DK), fix)],
        out_shape=[jax.ShapeDtypeStruct(dz_buf.shape, BF16), jax.ShapeDtypeStruct((2, HG_W), F32),
                   jax.ShapeDtypeStruct((1, DK), F32)],
        scratch_shapes=[pltpu.VMEM((HG_HEADS, DK, DK), F32), pltpu.VMEM((1, HG_W), F32)],
        input_output_aliases={6: 0},
        compiler_params=_params(dimension_semantics=("arbitrary",)),
    )(z, o_raw, dy, states, lb_logits, gain, dz_buf)


def _head_ones():
    r, c = _iota((FOX_W, FOX_W), 0), _iota((FOX_W, FOX_W), 1)
    return ((r // FOX_DH) == (c // FOX_DH)).astype(BF16)


def _log_sigmoid(x):
    return jnp.minimum(x, 0.0) - jnp.log(1.0 + jnp.exp(-jnp.abs(x)))


def _fox_prep(z, bias, qg, kg):
    T = z.shape[0]
    tm = min(ROW_BLOCK, T)
    nb = T // tm

    def body(z_ref, b_ref, qg_ref, kg_ref, q_ref, k_ref, v_ref, qa_ref, ka_ref, carry_ref):
        @pl.when(pl.program_id(0) == 0)
        def _():
            carry_ref[...] = jnp.zeros_like(carry_ref)

        ones = _head_ones()
        normed = []
        for src, g_ref in ((0, qg_ref), (1, kg_ref)):
            xv = z_ref[:, src * FOX_W:(src + 1) * FOX_W]
            ms = _split_dot(ones, xv * xv, 1, 0, mat_first=False) * (1.0 / FOX_DH)
            normed.append(xv * lax.rsqrt(ms + EPS) * g_ref[...])
        qn, kn = normed
        q_ref[...] = (qn * FOX_DH ** -0.5).astype(BF16)
        k_b = kn.astype(BF16)
        k_ref[...] = k_b
        v_ref[...] = z_ref[:, 2 * FOX_W:3 * FOX_W].astype(BF16)
        logf = _log_sigmoid(z_ref[:, 3 * FOX_W:FOX_COLS] + b_ref[...])
        r, c = _iota((tm, tm), 0), _iota((tm, tm), 1)
        tri_b = (c <= r).astype(BF16)
        cum = _split_dot(tri_b, logf, 1, 0, terms=3) + carry_ref[...]
        carry_ref[...] = cum[tm - 1:tm]
        c2 = cum * LOG2E
        hi = c2.astype(BF16)
        rem = c2 - hi.astype(F32)
        mid = rem.astype(BF16)
        lo = (rem - mid.astype(F32)).astype(BF16)
        hrow, col = _iota((LANES, 2 * FOX_W), 0), _iota((LANES, 2 * FOX_W), 1)
        base = hrow * LANES + jnp.where(hrow % 2 == 0, FOX_DH, 0)
        placed = None
        for t, part in enumerate((hi, mid, lo)):
            place = jnp.logical_and(col == base + t, hrow < FOX_HEADS).astype(BF16)
            term = _dot(part, place, 1, 0)
            placed = term if placed is None else placed + term
        colw = _iota((tm, 2 * FOX_W), 1)
        head, lane = colw // LANES, colw % LANES
        own = (lane < FOX_DH) == (head % 2 == 0)
        other = jnp.where(head % 2 == 0, lane - FOX_DH, lane)
        ones_q = jnp.where(jnp.logical_and(other >= 0, other < 3), -1.0, 0.0)
        q2 = (qn * (FOX_DH ** -0.5 * LOG2E)).astype(BF16)
        q_exp = jnp.concatenate([q2[:, (h // 2) * LANES:(h // 2 + 1) * LANES] for h in range(FOX_HEADS)], axis=1)
        k_exp = jnp.concatenate([k_b[:, (h // 2) * LANES:(h // 2 + 1) * LANES] for h in range(FOX_HEADS)], axis=1)
        qa_ref[...] = jnp.where(own, q_exp, ones_q.astype(BF16))
        ka_ref[...] = jnp.where(own, k_exp, placed.astype(BF16))

    wide = pl.BlockSpec((tm, 2 * FOX_W), lambda i: (i, 0))
    return _pallas(
        body, name="fox_prep", grid=(nb,),
        in_specs=[pl.BlockSpec((tm, SEG), lambda i: (i, 1)), pl.BlockSpec((1, LANES), lambda i: (0, 0)),
                  pl.BlockSpec((1, FOX_W), lambda i: (0, 0)), pl.BlockSpec((1, FOX_W), lambda i: (0, 0))],
        out_specs=[pl.BlockSpec((tm, FOX_W), lambda i: (i, 0))] * 3 + [wide] * 2,
        out_shape=[jax.ShapeDtypeStruct((T, FOX_W), BF16)] * 3 + [jax.ShapeDtypeStruct((T, 2 * FOX_W), BF16)] * 2,
        scratch_shapes=[pltpu.VMEM((1, LANES), F32)],
        compiler_params=_params(dimension_semantics=("arbitrary",)),
    )(z, bias, qg, kg)


def _fox_fwd(qa, ka, vb, blocks):
    T = qa.shape[0]
    tq = min(ROW_BLOCK, T)
    nq = T // tq
    NEG = -1e30
    n = len(blocks)
    n_pairs = FOX_HEADS // 2

    n_in = 3

    def body(*refs):
        q_ref, k_ref, v_ref = refs[:n_in]
        o_ref, lse_ref = refs[n_in + n:n_in + n + 2]
        m_sc, l_sc, acc_sc = refs[n_in + 2 * n + 2:n_in + 2 * n + 5]
        pr, qi = pl.program_id(0), pl.program_id(1)
        g_start, g_forward, g_finish = _gather_steps(
            refs[n_in:n_in + n], refs[n_in + n + 2:n_in + 2 * n + 2], *refs[n_in + 2 * n + 5:])

        @pl.when(jnp.logical_and(pr == 0, qi == 0))
        def _():
            g_start()

        @pl.when(jnp.logical_and(pr == n_pairs // 2, qi == 0))
        def _():
            g_forward()

        m_sc[...] = jnp.full_like(m_sc, NEG)
        l_sc[...] = jnp.zeros_like(l_sc)
        acc_sc[...] = jnp.zeros_like(acc_sc)
        lane = _iota((tq, LANES), 1)

        def block(masked, ki):
            keys = pl.ds(pl.multiple_of(ki * tq, tq), tq)
            vv = v_ref[keys, :]
            for hh in range(2):
                hs = slice(hh * LANES, (hh + 1) * LANES)
                s = _dot(q_ref[:, hs], k_ref[keys, hs], 1, 1)
                tiles = [s[:, j * LANES:(j + 1) * LANES] for j in range(tq // LANES)]
                if masked:
                    row, col = _iota((tq, LANES), 0), _iota((tq, LANES), 1)
                    tiles = [jnp.where(row >= col + j * LANES, t, NEG) for j, t in enumerate(tiles)]
                m_old = m_sc[hh]
                top = jnp.broadcast_to(jnp.max(functools.reduce(jnp.maximum, tiles), axis=-1, keepdims=True),
                                       (tq, LANES))
                m_new = jnp.maximum(m_old, top)
                alpha = jnp.exp2(m_old - m_new)
                ps = [jnp.exp2(t - m_new) for t in tiles]
                l_sc[hh] = alpha * l_sc[hh] + functools.reduce(jnp.add, ps)
                m_sc[hh] = m_new
                p_b = jnp.concatenate([p.astype(BF16) for p in ps], axis=1)
                acc_sc[hh] = alpha * acc_sc[hh] + _dot(p_b, vv, 1, 0)

        def before(ki, carry):
            block(False, ki)
            return carry

        lax.fori_loop(0, qi, before, 0)
        block(True, qi)
        l0 = jnp.sum(l_sc[0], axis=-1, keepdims=True)
        l1 = jnp.sum(l_sc[1], axis=-1, keepdims=True)
        o_ref[...] = jnp.where(lane < FOX_DH, acc_sc[0] * (1.0 / l0), acc_sc[1] * (1.0 / l1))
        lse_ref[:, :LANES] = m_sc[0] + jnp.log2(l0)
        lse_ref[:, LANES:] = m_sc[1] + jnp.log2(l1)

        @pl.when(jnp.logical_and(pr == n_pairs - 1, qi == nq - 1))
        def _():
            g_finish()

    qmap = lambda p, i: (i, p)
    whole = lambda p, i: (0, p)
    hbm = pl.BlockSpec(memory_space=pl.ANY)
    res = _pallas(
        body, name="fox_fwd", grid=(n_pairs, nq),
        in_specs=[pl.BlockSpec((tq, 2 * LANES), qmap), pl.BlockSpec((T, 2 * LANES), whole),
                  pl.BlockSpec((T, LANES), whole)] + [hbm] * n,
        out_specs=[pl.BlockSpec((tq, LANES), qmap), pl.BlockSpec((tq, 2 * LANES), qmap)] + [hbm] * n,
        out_shape=[jax.ShapeDtypeStruct((T, FOX_W), F32), jax.ShapeDtypeStruct((T, 2 * FOX_W), F32)]
        + [jax.ShapeDtypeStruct((N_DEV,) + b.shape, b.dtype) for b in blocks],
        scratch_shapes=[pltpu.VMEM((2, tq, LANES), F32)] * 3 + _gather_scratch(n),
        compiler_params=_params(dimension_semantics=("arbitrary",) * 2),
    )(qa, ka, vb, *blocks)
    return res[0], res[1], res[2:]


def _fox_bwd(qs, kn, vb, qa, ka, o, do, lse, hs):
    T = qs.shape[0]
    tq = min(ROW_BLOCK, T)
    nq = T // tq
    n = len(hs)
    n_pairs = FOX_HEADS // 2

    def body(*refs):
        q_ref, k_ref, v_ref, qa_ref, ka_ref, o_ref, do_ref, lse_ref = refs[:8]
        dq_ref, dk_ref, dv_ref, dcs_ref, drs_ref = refs[8 + n:13 + n]
        pr, ki = pl.program_id(0), pl.program_id(1)
        x_start, x_finish = _chip_exchange_steps(refs[8:8 + n], refs[13 + n:13 + 2 * n], *refs[13 + 2 * n:])

        @pl.when(jnp.logical_and(pr == 0, ki == 0))
        def _():
            x_start()
            drs_ref[...] = jnp.zeros_like(drs_ref)

        @pl.when(ki == 0)
        def _():
            dq_ref[...] = jnp.zeros_like(dq_ref)

        dk_ref[...] = jnp.zeros_like(dk_ref)
        dv_ref[...] = jnp.zeros_like(dv_ref)
        dcs_ref[...] = jnp.zeros_like(dcs_ref)

        def block(masked, qi):
            lane = _iota((tq, LANES), 1)
            qrows = pl.ds(pl.multiple_of(qi * tq, tq), tq)
            qv, kv, vv = q_ref[qrows, :], k_ref[...], v_ref[...]
            ov, dov = o_ref[qrows, :], do_ref[qrows, :]
            dq_acc = jnp.zeros((tq, LANES), F32)
            dk_acc = jnp.zeros((tq, LANES), F32)
            dv_acc = jnp.zeros((tq, LANES), F32)
            dcs_acc = jnp.zeros((8, tq), F32)
            drs_acc = jnp.zeros((tq, LANES), F32)
            prod = dov * ov
            nt = tq // LANES
            for hh in range(2):
                in_head = (lane < FOX_DH) if hh == 0 else (lane >= FOX_DH)
                hs_ = slice(hh * LANES, (hh + 1) * LANES)
                zb = jnp.zeros_like(qv)
                qm = jnp.where(in_head, qv, zb)
                km = jnp.where(in_head, kv, zb)
                dom = jnp.where(in_head, dov, 0.0).astype(BF16)
                delta_b = jnp.broadcast_to(jnp.sum(jnp.where(in_head, prod, 0.0), axis=1, keepdims=True), (tq, LANES))
                lse_b = lse_ref[qrows, hs_]
                s = _dot(qa_ref[qrows, hs_], ka_ref[:, hs_], 1, 1)
                dp = _dot(dom, vv, 1, 1)
                p_tiles, ds_tiles, col_tiles = [], [], []
                row_part = jnp.zeros((tq, LANES), F32)
                for j in range(nt):
                    js = slice(j * LANES, (j + 1) * LANES)
                    p = jnp.exp2(s[:, js] - lse_b)
                    if masked:
                        p = jnp.where(_iota((tq, LANES), 0) >= _iota((tq, LANES), 1) + j * LANES, p, 0.0)
                    ds = p * (dp[:, js] - delta_b)
                    p_tiles.append(p.astype(BF16))
                    ds_tiles.append(ds.astype(BF16))
                    col_tiles.append(jnp.sum(ds, axis=0, keepdims=True))
                    row_part = row_part + ds
                p_b = jnp.concatenate(p_tiles, axis=1)
                ds_b = jnp.concatenate(ds_tiles, axis=1)
                dv_acc = dv_acc + _dot(p_b, dom, 0, 0)
                dq_acc = dq_acc + _dot(ds_b, km, 1, 0)
                dk_acc = dk_acc + _dot(ds_b, qm, 0, 0)
                dcs_acc = dcs_acc + jnp.where(_iota((8, tq), 0) == hh, jnp.concatenate(col_tiles, axis=1), 0.0)
                rowsum = jnp.sum(row_part, axis=1, keepdims=True)
                drs_acc = drs_acc + jnp.where(lane == 2 * pr + hh, rowsum, 0.0)
            drs_ref[qrows, :] += drs_acc
            dq_ref[qrows, :] += dq_acc
            dk_ref[...] += dk_acc
            dv_ref[...] += dv_acc
            dcs_ref[0] += dcs_acc

        block(True, ki)

        def after(qi, carry):
            block(False, qi)
            return carry

        lax.fori_loop(ki + 1, nq, after, 0)

        @pl.when(jnp.logical_and(pr == n_pairs - 1, ki == nq - 1))
        def _():
            x_finish()

    whole = lambda p, j: (0, p)
    kmap = lambda p, j: (j, p)
    hbm = pl.BlockSpec(memory_space=pl.ANY)
    res = _pallas(
        body, name="fox_bwd", grid=(n_pairs, nq),
        in_specs=[pl.BlockSpec((T, LANES), whole), pl.BlockSpec((tq, LANES), kmap), pl.BlockSpec((tq, LANES), kmap),
                  pl.BlockSpec((T, 2 * LANES), whole), pl.BlockSpec((tq, 2 * LANES), kmap),
                  pl.BlockSpec((T, LANES), whole), pl.BlockSpec((T, LANES), whole), pl.BlockSpec((T, 2 * LANES), whole)]
        + [hbm] * n,
        out_specs=[pl.BlockSpec((T, LANES), whole), pl.BlockSpec((tq, LANES), kmap),
                   pl.BlockSpec((tq, LANES), kmap), pl.BlockSpec((1, 8, tq), lambda p, j: (p, 0, j)),
                   pl.BlockSpec((T, LANES), lambda p, j: (0, 0))] + [hbm] * n,
        out_shape=[jax.ShapeDtypeStruct((T, FOX_W), F32)] * 3
        + [jax.ShapeDtypeStruct((n_pairs, 8, T), F32), jax.ShapeDtypeStruct((T, LANES), F32)]
        + _chip_exchange_shapes(hs),
        scratch_shapes=_chip_exchange_scratch(n),
        compiler_params=_params(dimension_semantics=("arbitrary",) * 2),
    )(qs, kn, vb, qa, ka, o, do, lse, *hs)
    res = list(res)
    return res[:5] + [res[5:]]


def _fox_post(z, dq, dk, dv, dcs, drs, bias, qg, kg, dz_buf):
    T = z.shape[0]
    tm = min(ROW_BLOCK, T)
    nb = T // tm

    def body(z_ref, dq_ref, dk_ref, dv_ref, dcs_ref, drs_ref, b_ref, qg_ref, kg_ref, _buf_ref, dz_ref, dqg_ref, dkg_ref,
             db_ref, carry_ref):
        @pl.when(pl.program_id(0) == 0)
        def _():
            carry_ref[...] = jnp.zeros_like(carry_ref)
            dqg_ref[...] = jnp.zeros_like(dqg_ref)
            dkg_ref[...] = jnp.zeros_like(dkg_ref)
            db_ref[...] = jnp.zeros_like(db_ref)

        ones = _head_ones()
        for src, g_ref, d_ref, dg_ref, scale in ((0, qg_ref, dq_ref, dqg_ref, FOX_DH ** -0.5), (1, kg_ref, dk_ref, dkg_ref, 1.0)):
            xv = z_ref[:, src * FOX_W:(src + 1) * FOX_W]
            ms = _split_dot(ones, xv * xv, 1, 0, mat_first=False) * (1.0 / FOX_DH)
            rstd = lax.rsqrt(ms + EPS)
            xh = xv * rstd
            dn = d_ref[...] * scale
            dg_ref[...] += jnp.sum(dn * xh, axis=0, keepdims=True)
            dxh = dn * g_ref[...]
            mean = _split_dot(ones, dxh * xh, 1, 0, mat_first=False) * (1.0 / FOX_DH)
            dz_ref[:, src * FOX_W:(src + 1) * FOX_W] = (rstd * (dxh - xh * mean)).astype(BF16)
        dz_ref[:, 2 * FOX_W:3 * FOX_W] = dv_ref[...].astype(BF16)
        row8 = _iota((8, tm), 0)
        dct = jnp.zeros((8, tm), F32)
        for h in range(FOX_HEADS):
            src_row = dcs_ref[h // 2][h % 2:h % 2 + 1, :]
            dct = dct + jnp.where(row8 == h, src_row, 0.0)
        dct = drs_ref[...].T[0:8] - dct
        r, c = _iota((tm, tm), 0), _iota((tm, tm), 1)
        upper_b = (r >= c).astype(BF16)
        rc = _split_dot(upper_b, dct, 1, 0, mat_first=False) + carry_ref[...]
        carry_ref[...] = rc[:, 0:1]
        full = jnp.concatenate([rc, jnp.zeros((LANES - 8, tm), F32)], axis=0)
        dlogf = full.T
        xf = z_ref[:, 3 * FOX_W:FOX_COLS] + b_ref[...]
        df = dlogf * (1.0 - _sigmoid(xf))
        dz_ref[:, 3 * FOX_W:FOX_COLS] = df.astype(BF16)
        dz_ref[:, FOX_COLS:] = jnp.zeros((tm, SEG - FOX_COLS), BF16)
        db_ref[...] += jnp.sum(df, axis=0, keepdims=True)

    rev = lambda i: (nb - 1 - i, 0)
    fix2 = lambda i: (0, 0)
    return _pallas(
        body, name="fox_post", grid=(nb,),
        in_specs=[pl.BlockSpec((tm, SEG), lambda i: (nb - 1 - i, 1)), pl.BlockSpec((tm, FOX_W), rev),
                  pl.BlockSpec((tm, FOX_W), rev),
                  pl.BlockSpec((tm, FOX_W), rev), pl.BlockSpec((FOX_HEADS // 2, 8, tm), lambda i: (0, 0, nb - 1 - i)),
                  pl.BlockSpec((tm, LANES), rev),
                  pl.BlockSpec((1, LANES), fix2), pl.BlockSpec((1, FOX_W), fix2), pl.BlockSpec((1, FOX_W), fix2),
                  pl.BlockSpec(memory_space=pl.ANY)],
        out_specs=[pl.BlockSpec((tm, SEG), lambda i: (nb - 1 - i, 1)), pl.BlockSpec((1, FOX_W), fix2),
                   pl.BlockSpec((1, FOX_W), fix2), pl.BlockSpec((1, LANES), fix2)],
        out_shape=[jax.ShapeDtypeStruct(dz_buf.shape, BF16), jax.ShapeDtypeStruct((1, FOX_W), F32),
                   jax.ShapeDtypeStruct((1, FOX_W), F32), jax.ShapeDtypeStruct((1, LANES), F32)],
        scratch_shapes=[pltpu.VMEM((8, 1), F32)],
        input_output_aliases={9: 0},
        compiler_params=_params(dimension_semantics=("arbitrary",)),
    )(z, dq, dk, dv, dcs, drs, bias, qg, kg, dz_buf)


def _local_step(x, p, tgt, sm, chunks, core):
    lbl, og, fb = sm["hg_lb_logits"], sm["hg_onorm_g"], sm["fox_f_bias"]
    fbias = jnp.pad(fb, ((0, 0), (0, LANES - FOX_HEADS)))
    qg = jnp.tile(sm["fox_q_norm_g"], (1, FOX_HEADS))
    kg = jnp.tile(sm["fox_k_norm_g"], (1, FOX_HEADS))

    assert BIG[0] == "w_in"
    h, got0 = _rms_fwd(x, sm["norm_mix_g"], chunks[:1], name="rms_mix")
    W = {"w_in": _full_of_chunks("w_in", got0[0])}
    rest = dict(zip(BIG[1:], chunks[1:]))
    first, second = ["w_ffn_gate"], ["w_ffn_up"]
    third = [n for n in BIG[1:] if n not in first + second]
    z, got1 = _matmul(h, W["w_in"], tb=True, gather=[rest[n] for n in first], name="mm_z")
    o_raw, ya, states, got2 = _hgrn_fwd(z, lbl, og, [rest[n] for n in second])
    qs, kn, vb, qa, ka = _fox_prep(z, fbias, qg, kg)
    yb, lse, got3 = _fox_fwd(qa, ka, vb, [rest[n] for n in third])
    W = dict(W, **{n: _full_of_chunks(n, g)
                   for n, g in zip(first + second + third, list(got1) + list(got2) + list(got3))})
    merged, ua, ub = _merge_fwd(ya, yb, W["w_branch_a"], W["w_branch_b"], z)
    x1, hf = _matmul(merged, W["w_out"], add=x, norm_fwd=sm["norm_ffn_g"], name="mm_x1")
    a, b, act = _swiglu_fwd(hf, W["w_ffn_gate"], W["w_ffn_up"])
    x2 = _matmul(act, W["w_ffn_down"], add=x1, name="mm_x2")
    hp, dy, dsp, dpp, loss = _ple_loss(x2, p, sm["norm_ple_g"], W["w_ple_gate"], W["w_ple_proj"], tgt)

    G = {}
    G["w_ple_proj"] = _matmul(p, dpp, ta=True, out_dtype=BF16, name="mm_dw_ple_proj")
    G["w_ple_gate"] = _matmul(hp, dsp, ta=True, out_dtype=BF16, name="mm_dw_ple_gate")
    dx2, d_ple_g = _matmul(dsp, W["w_ple_gate"], tb=True, norm_bwd=(x2, sm["norm_ple_g"], dy), name="mm_dx2")
    G["w_ffn_down"] = _matmul(act, dx2, ta=True, out_dtype=BF16, name="mm_dw_ffn_down")
    da, db = _swiglu_bwd(dx2, W["w_ffn_down"], a, b)
    G["w_ffn_gate"] = _matmul(da, hf, ta=True, out_dtype=BF16, name="mm_dw_ffn_gate")
    G["w_ffn_up"] = _matmul(db, hf, ta=True, out_dtype=BF16, name="mm_dw_ffn_up")
    dhf = _matmul(da, W["w_ffn_gate"], name="mm_dhf_a")
    dx1, d_ffn_g = _matmul(db, W["w_ffn_up"], add=dhf, norm_bwd=(x1, sm["norm_ffn_g"], dx2), name="mm_dx1")
    G["w_out"] = _matmul(merged, dx1, ta=True, out_dtype=BF16, name="mm_dw_out")
    late = ["w_branch_a", "w_branch_b"]
    early = [n for n in BIG[1:] if n not in late]
    g4 = {n: _chunks4(n, G[n]) for n in early}
    (dua, dub, dz, dya, dyb), got_early = _merge_bwd(dx1, W["w_out"], W["w_branch_a"], W["w_branch_b"], ua, ub, z,
                                                     [g4[n] for n in early])
    G["w_branch_a"] = _matmul(ya, dua, ta=True, out_dtype=BF16, name="mm_dw_branch_a")
    G["w_branch_b"] = _matmul(yb, dub, ta=True, out_dtype=BF16, name="mm_dw_branch_b")
    g4.update({n: _chunks4(n, G[n]) for n in late})
    got4 = dict(zip(early, got_early))
    got4.update(zip(late, _sibling_exchange([g4[n] for n in late], name="grads_to_sibling_branch")))
    hb_rest = _chip_sum([g4[n] for n in BIG[1:]], [got4[n] for n in BIG[1:]], core, name="chip_sum_rest")
    dq, dk, dv, dcs, drs, got_rest = _fox_bwd(qs, kn, vb, qa, ka, yb, dyb, lse, hb_rest)
    dz, d_qg, d_kg, d_fb = _fox_post(z, dq, dk, dv, dcs, drs, fbias, qg, kg, dz)
    dz, d_lbl, d_og = _hgrn_bwd(z, o_raw, dya, states, lbl, og, dz)
    G["w_in"] = _matmul(dz, h, ta=True, out_dtype=BF16, name="mm_dw_in")
    hb_in = _sibling_sums({"w_in": G["w_in"]}, core, tag="w_in")
    (grad_x, d_mix_g), got_in = _matmul(dz, W["w_in"], exchange=hb_in,
                                        norm_bwd=(x, sm["norm_mix_g"], dx1), name="mm_dx")

    gs = {"norm_mix_g": d_mix_g, "hg_lb_logits": d_lbl, "hg_onorm_g": d_og, "fox_f_bias": d_fb[:, :FOX_HEADS],
          "fox_q_norm_g": d_qg.reshape(FOX_HEADS, FOX_DH).sum(0, keepdims=True),
          "fox_k_norm_g": d_kg.reshape(FOX_HEADS, FOX_DH).sum(0, keepdims=True),
          "norm_ffn_g": d_ffn_g, "norm_ple_g": d_ple_g}
    return loss, grad_x, gs, hb_in + hb_rest, list(got_in) + list(got_rest)


def _pack_rows(parts, total):
    buf = jnp.concatenate(parts, axis=-2)
    pad = total - buf.shape[-2]
    widths = [(0, 0)] * (buf.ndim - 2) + [(0, pad), (0, 0)]
    return jnp.pad(buf, widths)


def _chunk_of_shard(n, w):
    if n == "w_in":
        return jnp.pad(w, ((0, IN_SHARD_PAD - IN_SHARD), (0, 0)))
    return w


def _full_of_chunks(n, g):
    _, a, b = g.shape
    if n == "w_in":
        w = g[:, :IN_SHARD].reshape(IN_COLS, b)
        gap = jnp.zeros((SEG - FOX_LOGICAL, b), g.dtype)
        return jnp.concatenate([w[:HG_COLS + FOX_LOGICAL], gap, w[HG_COLS + FOX_LOGICAL:]], axis=0)
    if BIG_SHAPE[n][2] == 0:
        return g.reshape(N_DEV * a, b)
    return g.transpose(1, 0, 2).reshape(a, N_DEV * b)


def _chunks_of_full(n, g):
    if n == "w_in":
        w = jnp.concatenate([g[:HG_COLS + FOX_LOGICAL], g[2 * SEG:]], axis=0).reshape(N_DEV, IN_SHARD, g.shape[1])
        return jnp.pad(w, ((0, 0), (0, IN_SHARD_PAD - IN_SHARD), (0, 0)))
    if BIG_SHAPE[n][2] == 0:
        return g.reshape(N_DEV, g.shape[0] // N_DEV, g.shape[1])
    return g.reshape(g.shape[0], N_DEV, g.shape[1] // N_DEV).transpose(1, 0, 2)


def _pack_small(vals, loss_row=None):
    parts = [vals[n].reshape(SMALL_ROWS[n], -1) for n in SMALL]
    parts = [jnp.pad(v, ((0, 0), (0, LANES - v.shape[1]))) for v in parts]
    if loss_row is not None:
        parts.append(loss_row)
    return _pack_rows(parts, SMALL_TOTAL)


def _unpack_small(buf, like):
    out, r0 = {}, 0
    for n in SMALL:
        rows, size = SMALL_ROWS[n], like[n].size
        blk = buf[r0:r0 + rows]
        out[n] = (blk if size == rows * LANES else blk[:, :size]).reshape(like[n].shape)
        r0 += rows
    return out


def _place():
    return lax.axis_index("x"), lax.axis_index("y"), lax.axis_index("c")


def _gather_steps(x_refs, out_refs, send_sems, recv_sems, local_sems):
    n = len(x_refs)
    x, y, c = _place()
    me, sibling = (x, y, c), (x, y, 1 - c)
    chips = [(1 - x, y), (x, 1 - y), (1 - x, 1 - y)]

    def slot(i, px, py, pc):
        return out_refs[i].at[4 * px + 2 * py + pc]

    def copy(k, i, blk, to, own=False):
        return pltpu.make_async_remote_copy(
            src_ref=x_refs[i] if own else slot(i, *blk), dst_ref=slot(i, *blk),
            send_sem=send_sems.at[k, i], recv_sem=recv_sems.at[k, i], device_id=to, device_id_type=MESH)

    def mine():
        return [pltpu.make_async_copy(x_refs[i], slot(i, *me), local_sems.at[i]) for i in range(n)]

    def first():
        cps = [copy(0, i, me, sibling, own=True) for i in range(n)]
        return cps + [copy(1 + j, i, me, (*chip, c), own=True) for j, chip in enumerate(chips) for i in range(n)]

    def passed():
        return [copy(4 + j, i, (*chip, c), sibling) for j, chip in enumerate(chips) for i in range(n)]

    def start():
        for cp in mine() + first():
            cp.start()

    def forward():
        fws = passed()
        for j, chip in enumerate(chips):
            for i in range(n):
                copy(1 + j, i, (*chip, c), me).wait_recv()
                fws[j * n + i].start()

    def finish():
        for i in range(n):
            copy(0, i, sibling, me).wait_recv()
        for j, chip in enumerate(chips):
            for i in range(n):
                copy(4 + j, i, (*chip, 1 - c), me).wait_recv()
        for cp in first() + passed():
            cp.wait_send()
        for cp in mine():
            cp.wait()

    return start, forward, finish


def _gather_scratch(n):
    return [pltpu.SemaphoreType.DMA((7, n)), pltpu.SemaphoreType.DMA((7, n)), pltpu.SemaphoreType.DMA((n,))]


def _sibling_steps(g_refs, out_refs, send_sems, recv_sems):
    n = len(g_refs)
    x, y, c = _place()

    def copies():
        return [pltpu.make_async_remote_copy(
            src_ref=g_refs[i].at[:, pl.ds(1 - c, 1)], dst_ref=out_refs[i], send_sem=send_sems.at[i],
            recv_sem=recv_sems.at[i], device_id=(x, y, 1 - c), device_id_type=MESH) for i in range(n)]

    def start():
        for cp in copies():
            cp.start()

    def finish():
        for cp in copies():
            cp.wait()

    return start, finish


def _sibling_shapes(gs):
    return [jax.ShapeDtypeStruct((N_CHIP, 1) + g.shape[2:], g.dtype) for g in gs]


def _sibling_scratch(n):
    return [pltpu.SemaphoreType.DMA((n,)), pltpu.SemaphoreType.DMA((n,))]


def _sibling_exchange(gs, *, name):
    n = len(gs)

    def body(*refs):
        for step in _sibling_steps(refs[:n], refs[n:2 * n], *refs[2 * n:]):
            step()

    hbm = pl.BlockSpec(memory_space=pl.ANY)
    return _pallas(
        body, name=name, out_shape=_sibling_shapes(gs), in_specs=[hbm] * n, out_specs=[hbm] * n,
        scratch_shapes=_sibling_scratch(n),
    )(*gs)


def _chip_sum(g4s, gots, core, *, name):
    n = len(g4s)

    def body(c_ref, *refs):
        for g_ref, r_ref, h_ref in zip(refs[:n], refs[n:2 * n], refs[2 * n:]):
            h_ref[0] = (g_ref[0, 0].astype(F32) + r_ref[0, 0].astype(F32)).astype(BF16)

    shapes = [g.shape[2:] for g in g4s]
    grid_spec = pltpu.PrefetchScalarGridSpec(
        num_scalar_prefetch=1, grid=(N_CHIP,),
        in_specs=[pl.BlockSpec((1, 1) + s, lambda j, c: (j, c[0], 0, 0)) for s in shapes]
        + [pl.BlockSpec((1, 1) + s, lambda j, c: (j, 0, 0, 0)) for s in shapes],
        out_specs=[pl.BlockSpec((1,) + s, lambda j, c: (j, 0, 0)) for s in shapes])
    return list(_pallas(
        body, name=name, grid_spec=grid_spec, out_shape=[jax.ShapeDtypeStruct((N_CHIP,) + s, BF16) for s in shapes],
        compiler_params=_params(dimension_semantics=("arbitrary",)),
    )(core, *g4s, *gots))


def _chip_exchange_steps(h_refs, out_refs, send_sems, recv_sems):
    n = len(h_refs)
    x, y, c = _place()
    chips = [(1 - x, y), (x, 1 - y), (1 - x, 1 - y)]

    def copies():
        return [pltpu.make_async_remote_copy(
            src_ref=h_refs[i].at[2 * px + py], dst_ref=out_refs[i].at[k], send_sem=send_sems.at[k, i],
            recv_sem=recv_sems.at[k, i], device_id=(px, py, c), device_id_type=MESH)
            for k, (px, py) in enumerate(chips) for i in range(n)]

    def start():
        for cp in copies():
            cp.start()

    def finish():
        for cp in copies():
            cp.wait()

    return start, finish


def _chip_exchange_shapes(hs):
    return [jax.ShapeDtypeStruct((3,) + h.shape[1:], h.dtype) for h in hs]


def _chip_exchange_scratch(n):
    return [pltpu.SemaphoreType.DMA((3, n)), pltpu.SemaphoreType.DMA((3, n))]


def _chunks4(n, g):
    gc = _chunks_of_full(n, g)
    return gc.reshape((N_CHIP, 2) + gc.shape[1:])


def _sibling_sums(G, core, *, tag):
    g4 = [_chunks4(n, g) for n, g in G.items()]
    got = _sibling_exchange(g4, name="grads_to_sibling_" + tag)
    return _chip_sum(g4, got, core, name="chip_sum_" + tag)


def _adam_math(w, g, m, v):
    m = ADAM_B1 * m + (1.0 - ADAM_B1) * g
    v = ADAM_B2 * v + (1.0 - ADAM_B2) * (g * g)
    m_hat = m / (1.0 - ADAM_B1 ** ADAM_STEP)
    v_hat = v / (1.0 - ADAM_B2 ** ADAM_STEP)
    delta = -ADAM_LR * (m_hat / (jnp.sqrt(v_hat) + ADAM_EPS) + ADAM_WD * w)
    return delta, m, v


def _adam_shard(hb, got, chip, w, m, v, *, name):
    _, r, c = w.shape
    _, a, cb = hb.shape
    assert cb == c and c % LANES == 0, (hb.shape, w.shape)
    tc = _pick(c, 2 * LANES)

    def body(j_ref, h_ref, r_ref, w_ref, m_ref, v_ref, g_ref, d_ref, nm_ref, nv_ref):
        parts = [h_ref[0], r_ref[0], r_ref[1], r_ref[2]]
        g = None
        for part in parts:
            part = part[:r].astype(F32)
            g = part if g is None else g + part
        d, nm, nv = _adam_math(w_ref[0], g, m_ref[0], v_ref[0])
        g_ref[0] = g
        d_ref[0] = d
        nm_ref[0] = nm
        nv_ref[0] = nv

    blk = pl.BlockSpec((1, r, tc), lambda i, j: (0, 0, i))
    grid_spec = pltpu.PrefetchScalarGridSpec(
        num_scalar_prefetch=1, grid=(c // tc,),
        in_specs=[pl.BlockSpec((1, a, tc), lambda i, j: (j[0], 0, i)),
                  pl.BlockSpec((3, a, tc), lambda i, j: (0, 0, i)), blk, blk, blk],
        out_specs=[blk] * 4)
    return _pallas(
        body, name=name, grid_spec=grid_spec, out_shape=[jax.ShapeDtypeStruct((1, r, c), F32)] * 4,
        compiler_params=_params(dimension_semantics=("arbitrary",)),
    )(chip, hb, got, w, m, v)


def _small_all_reduce_adam(gs, w, m, v):
    def body(g_ref, w_ref, m_ref, v_ref, sum_ref, d_ref, nm_ref, nv_ref, gather, send_sems, recv_sems):
        x, y, c = _place()
        my = 4 * x + 2 * y + c
        gather[my] = g_ref[...]
        cps = []
        for k in range(1, N_DEV):
            to = (x ^ (k >> 2), y ^ ((k >> 1) & 1), c ^ (k & 1))
            cps.append(pltpu.make_async_remote_copy(
                src_ref=g_ref, dst_ref=gather.at[my], send_sem=send_sems.at[k - 1], recv_sem=recv_sems.at[k - 1],
                device_id=to, device_id_type=MESH))
        for cp in cps:
            cp.start()
        for cp in cps:
            cp.wait()
        total = gather[0]
        for d in range(1, N_DEV):
            total = total + gather[d]
        dlt, nm, nv = _adam_math(w_ref[...], total, m_ref[...], v_ref[...])
        sum_ref[...] = total
        d_ref[...] = dlt
        nm_ref[...] = nm
        nv_ref[...] = nv

    vm = pl.BlockSpec(memory_space=pltpu.VMEM)
    return _pallas(
        body, name="small_all_reduce_adam", out_shape=[jax.ShapeDtypeStruct((SMALL_TOTAL, LANES), F32)] * 4,
        in_specs=[vm] * 4, out_specs=[vm] * 4,
        scratch_shapes=[pltpu.VMEM((N_DEV, SMALL_TOTAL, LANES), F32), pltpu.SemaphoreType.DMA((7,)),
                        pltpu.SemaphoreType.DMA((7,))],
            )(gs, w, m, v)


def kernel(x, p, norm_mix_g, w_in, hg_lb_logits, hg_onorm_g, fox_f_bias, fox_q_norm_g, fox_k_norm_g, w_branch_a, w_branch_b, w_out, norm_ffn_g, w_ffn_gate, w_ffn_up, w_ffn_down, norm_ple_g, w_ple_gate, w_ple_proj, loss_target, m_norm_mix_g, m_w_in, m_hg_lb_logits, m_hg_onorm_g, m_fox_f_bias, m_fox_q_norm_g, m_fox_k_norm_g, m_w_branch_a, m_w_branch_b, m_w_out, m_norm_ffn_g, m_w_ffn_gate, m_w_ffn_up, m_w_ffn_down, m_norm_ple_g, m_w_ple_gate, m_w_ple_proj, v_norm_mix_g, v_w_in, v_hg_lb_logits, v_hg_onorm_g, v_fox_f_bias, v_fox_q_norm_g, v_fox_k_norm_g, v_w_branch_a, v_w_branch_b, v_w_out, v_norm_ffn_g, v_w_ffn_gate, v_w_ffn_up, v_w_ffn_down, v_norm_ple_g, v_w_ple_gate, v_w_ple_proj):
    args = dict(locals())
    wts = {n: args[n] for n in BIG + SMALL}
    mom = {n: args["m_" + n] for n in BIG + SMALL}
    var = {n: args["v_" + n] for n in BIG + SMALL}
    for group in (wts, mom, var):
        for n in TRANSPOSED:
            group[n] = jnp.swapaxes(group[n], 1, 2)
    sm = {n: wts[n] for n in SMALL}

    xi, yi, ci = _place()
    core = jnp.reshape(ci, (1,)).astype(jnp.int32)
    chip = jnp.reshape(2 * xi + yi, (1,)).astype(jnp.int32)
    chunks = [_chunk_of_shard(n, wts[n][0].astype(BF16)) for n in BIG]
    loss_blk, grad_x, gs, hb, got = _local_step(x[0], p[0, 0], loss_target[0], sm, chunks, core)

    g_big, d_big, nm_big, nv_big = {}, {}, {}, {}
    for n, h, r in zip(BIG, hb, got):
        res = _adam_shard(h, r, chip, wts[n], mom[n], var[n], name="adam_" + n)
        if n in TRANSPOSED:
            res = [jnp.swapaxes(t, 1, 2) for t in res]
        g_big[n], d_big[n], nm_big[n], nv_big[n] = res

    s_sum, s_d, s_nm, s_nv = _small_all_reduce_adam(
        _pack_small(gs, loss_blk[0:1]), _pack_small(sm), _pack_small({n: mom[n] for n in SMALL}),
        _pack_small({n: var[n] for n in SMALL}))
    loss = s_sum[LOSS_ROW, 0]
    g_small, d_small, nm_small, nv_small = (_unpack_small(t, sm) for t in (s_sum, s_d, s_nm, s_nv))

    order = ["norm_mix_g", "w_in", "hg_lb_logits", "hg_onorm_g", "fox_f_bias", "fox_q_norm_g", "fox_k_norm_g",
             "w_branch_a", "w_branch_b", "w_out", "norm_ffn_g", "w_ffn_gate", "w_ffn_up", "w_ffn_down", "norm_ple_g",
             "w_ple_gate", "w_ple_proj"]
    outs = [loss, grad_x[None]]
    for big, small in ((g_big, g_small), (d_big, d_small), (nm_big, nm_small), (nv_big, nv_small)):
        outs += [big[n] if n in big else small[n] for n in order]
    return tuple(outs)
```

```python
import functools

import jax
import jax.numpy as jnp
from jax import lax
from jax.experimental import pallas as pl
from jax.experimental.pallas import tpu as pltpu

F32 = jnp.float32
BF16 = jnp.bfloat16

D_MODEL = 1024
PLE_DIM = 256
HG_HEADS = 4
HG_DK = 128
HG_CHUNK = 64
HG_SUB = 16
HG_W = HG_HEADS * HG_DK
FOX_HEADS = 8
FOX_DH = 64
FOX_W = FOX_HEADS * FOX_DH
D_FF = 2816
EPS = 1e-6
N_DEV = 8
N_CHIP = 4
LANES = 128
FOX_COLS = 3 * FOX_W + LANES
HG_COLS = 4 * HG_W
GATE_COLS = 2 * D_MODEL
IN_COLS = HG_COLS + 3 * FOX_W + FOX_HEADS + GATE_COLS
FOX_LOGICAL = 3 * FOX_W + FOX_HEADS
SEG = 2048
IN_PAD = 3 * SEG
IN_SHARD = IN_COLS // N_DEV
IN_SHARD_PAD = 720
EXP_CLAMP = 80.0
LOG2E = 1.4426950408889634

ADAM_LR = 0.001
ADAM_B1 = 0.9
ADAM_B2 = 0.999
ADAM_EPS = 1e-08
ADAM_WD = 0.01
ADAM_STEP = 10

MESH = pl.DeviceIdType.MESH
VMEM_LIMIT = 56 * 1024 * 1024
ROW_BLOCK = 512
MATMUL_TN = 2048
MATMUL_TK = 3072
MATMUL_TM_T = 1408
FFN_TILE = 1408

BIG = ["w_in", "w_branch_a", "w_branch_b", "w_out", "w_ffn_gate", "w_ffn_up", "w_ffn_down",
       "w_ple_gate", "w_ple_proj"]
TRANSPOSED = ("w_in", "w_ffn_gate", "w_ffn_up")
BIG_SHAPE = {
    "w_in": (IN_COLS, D_MODEL, 0), "w_branch_a": (HG_W, D_MODEL, 1), "w_branch_b": (FOX_W, D_MODEL, 1),
    "w_out": (D_MODEL, D_MODEL, 0), "w_ffn_gate": (D_FF, D_MODEL, 0), "w_ffn_up": (D_FF, D_MODEL, 0),
    "w_ffn_down": (D_FF, D_MODEL, 0), "w_ple_gate": (D_MODEL, D_MODEL, 0), "w_ple_proj": (PLE_DIM, D_MODEL, 1),
}

SMALL = ["norm_mix_g", "hg_lb_logits", "hg_onorm_g", "fox_f_bias", "fox_q_norm_g", "fox_k_norm_g",
         "norm_ffn_g", "norm_ple_g"]
SMALL_ROWS = {"norm_mix_g": 8, "hg_lb_logits": 8, "hg_onorm_g": 1, "fox_f_bias": 1, "fox_q_norm_g": 1,
              "fox_k_norm_g": 1, "norm_ffn_g": 8, "norm_ple_g": 8}
SMALL_TOTAL = 40
LOSS_ROW = 36


def _pallas(body, **kw):
    return pl.pallas_call(body, **kw)


def _params(**kw):
    return pltpu.CompilerParams(vmem_limit_bytes=VMEM_LIMIT, **kw)


def _pick(n, target):
    if n <= target:
        return n
    best = None
    for t in range(LANES, target + 1, LANES):
        if n % t == 0:
            best = t
    assert best is not None, (n, target)
    return best


def _dot(a, b, ca, cb):
    return lax.dot_general(a, b, (((ca,), (cb,)), ((), ())), preferred_element_type=F32)


def _split_dot(mat, x, ca, cb, terms=2, mat_first=True):
    acc = None
    rem = x
    for _ in range(terms):
        part = rem.astype(BF16)
        rem = rem - part.astype(F32)
        p = _dot(mat, part, ca, cb) if mat_first else _dot(part, mat, ca, cb)
        acc = p if acc is None else acc + p
    return acc


def _sigmoid(x):
    return 1.0 / (1.0 + jnp.exp(-x))


def _iota(shape, dim):
    return lax.broadcasted_iota(jnp.int32, shape, dim)


def _matmul(a, b, *, name, ta=False, tb=False, out_dtype=F32, add=None, exchange=None, gather=None,
            norm_fwd=None, norm_bwd=None):
    assert exchange is None or gather is None
    (K, M) = a.shape if ta else a.shape[::-1]
    (N, Kb) = b.shape if tb else b.shape[::-1]
    assert K == Kb, (a.shape, b.shape, ta, tb)
    if ta:
        tm, tn, tk = _pick(M, MATMUL_TM_T), _pick(N, 2 * ROW_BLOCK), _pick(K, 4 * ROW_BLOCK)
    else:
        tm, tn, tk = _pick(M, 2 * ROW_BLOCK), _pick(N, MATMUL_TN), _pick(K, MATMUL_TK)
    if norm_bwd is not None:
        tm = _pick(M, ROW_BLOCK)
    nk = K // tk
    use_scratch = nk > 1 and out_dtype != F32
    if norm_fwd is not None or norm_bwd is not None:
        assert tn == N and not use_scratch and out_dtype == F32

    hs = list(exchange or gather or [])
    n_x = len(hs)
    grid = (M // tm, N // tn, nk)
    a_spec = pl.BlockSpec((tk, tm), lambda i, j, k: (k, i)) if ta else pl.BlockSpec((tm, tk), lambda i, j, k: (i, k))
    b_spec = pl.BlockSpec((tn, tk), lambda i, j, k: (j, k)) if tb else pl.BlockSpec((tk, tn), lambda i, j, k: (k, j))
    o_spec = pl.BlockSpec((tm, tn), lambda i, j, k: (i, j))
    row_vec = pl.BlockSpec((1, N), lambda i, j, k: (0, 0))
    hbm = pl.BlockSpec(memory_space=pl.ANY)
    extra_in = [(add, o_spec)] if add is not None else []
    extra_out = []
    if norm_fwd is not None:
        extra_in += [(norm_fwd, row_vec)]
        extra_out += [(jax.ShapeDtypeStruct((M, N), BF16), o_spec)]
    if norm_bwd is not None:
        extra_in += [(norm_bwd[0], o_spec), (norm_bwd[1], row_vec), (norm_bwd[2], o_spec)]
        extra_out += [(jax.ShapeDtypeStruct((1, N), F32), row_vec)]
    if gather is not None:
        ride_shapes, ride_scratch = [jax.ShapeDtypeStruct((N_DEV,) + h.shape, h.dtype) for h in hs], _gather_scratch(n_x)
    else:
        ride_shapes, ride_scratch = _chip_exchange_shapes(hs), (_chip_exchange_scratch(n_x) if n_x else [])
    n_ex_in, n_ex_out = len(extra_in), len(extra_out)

    def body(*refs):
        refs = list(refs)
        a_ref, b_ref = refs[:2]
        ex_in = refs[2:2 + n_ex_in]
        ride_in = refs[2 + n_ex_in:2 + n_ex_in + n_x]
        base = 2 + n_ex_in + n_x
        o_ref = refs[base]
        ex_out = refs[base + 1:base + 1 + n_ex_out]
        ride_out = refs[base + 1 + n_ex_out:base + 1 + n_ex_out + n_x]
        scratch = refs[base + 1 + n_ex_out + n_x:]
        at = [pl.program_id(d) for d in range(3)]
        k = at[2]
        if n_x:
            steps = _gather_steps if gather is not None else _chip_exchange_steps
            ride = steps(ride_in, ride_out, *scratch[-len(ride_scratch):])

            @pl.when(jnp.logical_and(at[0] == 0, jnp.logical_and(at[1] == 0, at[2] == 0)))
            def _():
                ride[0]()

            early_forward = len(ride) == 3 and grid[0] > 1
            if early_forward:
                @pl.when(jnp.logical_and(at[0] == grid[0] - 1, jnp.logical_and(at[1] == 0, at[2] == 0)))
                def _():
                    ride[1]()
        p = _dot(a_ref[...].astype(BF16), b_ref[...].astype(BF16), 0 if ta else 1, 1 if tb else 0)

        def finish(r):
            ins = list(ex_in)
            outs = list(ex_out)
            if add is not None:
                r = r + ins.pop(0)[...].astype(F32)
            if norm_fwd is not None:
                g_ref = ins.pop(0)
                rstd = lax.rsqrt(jnp.mean(r * r, axis=-1, keepdims=True) + EPS)
                outs.pop(0)[...] = (r * rstd * g_ref[...]).astype(BF16)
            if norm_bwd is not None:
                x_ref, g_ref, dres_ref = ins.pop(0), ins.pop(0), ins.pop(0)
                dg_ref = outs.pop(0)
                xv = x_ref[...]
                rstd = lax.rsqrt(jnp.mean(xv * xv, axis=-1, keepdims=True) + EPS)
                xh = xv * rstd
                part = jnp.sum(r * xh, axis=0, keepdims=True)

                @pl.when(at[0] == 0)
                def _():
                    dg_ref[...] = part

                @pl.when(at[0] > 0)
                def _():
                    dg_ref[...] += part

                dxh = r * g_ref[...]
                r = rstd * (dxh - xh * jnp.mean(dxh * xh, axis=-1, keepdims=True)) + dres_ref[...]
            o_ref[...] = r.astype(out_dtype)

        if nk == 1:
            finish(p)
        elif not use_scratch:
            @pl.when(k == 0)
            def _():
                o_ref[...] = p

            @pl.when(jnp.logical_and(k > 0, k < nk - 1))
            def _():
                o_ref[...] += p

            @pl.when(k == nk - 1)
            def _():
                finish(o_ref[...] + p)
        else:
            acc_ref = scratch[0]

            @pl.when(k == 0)
            def _():
                acc_ref[...] = p

            @pl.when(k > 0)
            def _():
                acc_ref[...] += p

            @pl.when(k == nk - 1)
            def _():
                finish(acc_ref[...])

        if n_x:
            @pl.when(jnp.logical_and(at[0] == grid[0] - 1, jnp.logical_and(at[1] == grid[1] - 1, at[2] == nk - 1)))
            def _():
                for step in ride[2 if early_forward else 1:]:
                    step()

    res = _pallas(
        body, name=name, grid=grid,
        in_specs=[a_spec, b_spec] + [s for _, s in extra_in] + [hbm] * n_x,
        out_specs=[o_spec] + [s for _, s in extra_out] + [hbm] * n_x,
        out_shape=[jax.ShapeDtypeStruct((M, N), out_dtype)] + [s for s, _ in extra_out] + ride_shapes,
        scratch_shapes=([pltpu.VMEM((tm, tn), F32)] if use_scratch else []) + ride_scratch,
        compiler_params=_params(dimension_semantics=("arbitrary",) * 3),
    )(a, b, *[v for v, _ in extra_in], *hs)
    res = list(res)
    main = res[0] if n_ex_out == 0 else tuple(res[:1 + n_ex_out])
    return (main, res[1 + n_ex_out:]) if n_x else main


def _row_map(nb, reverse, seg):
    if reverse:
        return lambda i: (nb - 1 - i, seg)
    return lambda i: (i, seg)


def _row_call(body, *, name, T, ins, outs, acc_outs=(), tm=ROW_BLOCK, reverse=False, swap=None):
    tm = min(tm, T)
    nb = T // tm
    in_specs, args = [], []
    for arr, how in ins:
        args.append(arr)
        if how is True:
            in_specs.append(pl.BlockSpec((tm, arr.shape[1]), _row_map(nb, reverse, 0)))
        elif how is False:
            in_specs.append(pl.BlockSpec(arr.shape, lambda i, _n=arr.ndim: (0,) * _n))
        else:
            in_specs.append(pl.BlockSpec((tm, SEG), _row_map(nb, reverse, how[0])))
    out_specs, out_shape = [], []
    for o in outs:
        c, dt = o[0], o[1]
        total, seg = o[2] if len(o) > 2 else (c, 0)
        out_specs.append(pl.BlockSpec((tm, c), _row_map(nb, reverse, seg)))
        out_shape.append(jax.ShapeDtypeStruct((T, total), dt))
    for shp, dt in acc_outs:
        out_specs.append(pl.BlockSpec(shp, lambda i, _n=len(shp): (0,) * _n))
        out_shape.append(jax.ShapeDtypeStruct(shp, dt))
    if swap is None:
        return _pallas(body, name=name, grid=(nb,), in_specs=in_specs, out_specs=out_specs, out_shape=out_shape,
                       compiler_params=_params(dimension_semantics=("arbitrary",)))(*args)
    n, n_in, n_out = len(swap), len(in_specs), len(out_specs)
    hbm = pl.BlockSpec(memory_space=pl.ANY)

    def riding(*refs):
        mine = refs[:n_in] + refs[n_in + n:n_in + n + n_out]
        start, finish = _sibling_steps(refs[n_in:n_in + n], refs[n_in + n + n_out:n_in + 2 * n + n_out], *refs[-2:])

        @pl.when(pl.program_id(0) == 0)
        def _():
            start()

        body(*mine)

        @pl.when(pl.program_id(0) == nb - 1)
        def _():
            finish()

    res = list(_pallas(
        riding, name=name, grid=(nb,), in_specs=in_specs + [hbm] * n, out_specs=out_specs + [hbm] * n,
        out_shape=out_shape + _sibling_shapes(swap), scratch_shapes=_sibling_scratch(n),
        compiler_params=_params(dimension_semantics=("arbitrary",)))(*args, *swap))
    return res[:n_out], res[n_out:]


def _rms_fwd(x, g, blocks, *, name):
    T, D = x.shape
    tm = min(ROW_BLOCK, T)
    nb = T // tm
    n = len(blocks)

    def body(*refs):
        x_ref, g_ref = refs[:2]
        h_ref = refs[2 + n]
        g_start, g_forward, g_finish = _gather_steps(refs[2:2 + n], refs[3 + n:3 + 2 * n], *refs[3 + 2 * n:])

        @pl.when(pl.program_id(0) == 0)
        def _():
            g_start()

        xv = x_ref[...]
        rstd = lax.rsqrt(jnp.mean(xv * xv, axis=-1, keepdims=True) + EPS)
        h_ref[...] = (xv * rstd * g_ref[...]).astype(BF16)

        @pl.when(pl.program_id(0) == nb - 1)
        def _():
            g_forward()
            g_finish()

    hbm = pl.BlockSpec(memory_space=pl.ANY)
    res = _pallas(
        body, name=name, grid=(nb,),
        in_specs=[pl.BlockSpec((tm, D), lambda i: (i, 0)), pl.BlockSpec((1, D), lambda i: (0, 0))] + [hbm] * n,
        out_specs=[pl.BlockSpec((tm, D), lambda i: (i, 0))] + [hbm] * n,
        out_shape=[jax.ShapeDtypeStruct((T, D), BF16)] + [jax.ShapeDtypeStruct((N_DEV,) + b.shape, b.dtype) for b in blocks],
        scratch_shapes=_gather_scratch(n),
        compiler_params=_params(dimension_semantics=("arbitrary",)),
    )(x, g, *blocks)
    return res[0], list(res[1:])


def _merge_fwd(ya, yb, wa, wb, zg):
    def body(ya_ref, yb_ref, wa_ref, wb_ref, zg_ref, m_ref, ua_ref, ub_ref):
        ua = _dot(ya_ref[...].astype(BF16), wa_ref[...], 1, 0)
        ub = _dot(yb_ref[...].astype(BF16), wb_ref[...], 1, 0)
        ga = _sigmoid(zg_ref[:, :D_MODEL])
        gb = _sigmoid(zg_ref[:, D_MODEL:])
        m_ref[...] = (ga * ua + gb * ub).astype(BF16)
        ua_ref[...] = ua.astype(BF16)
        ub_ref[...] = ub.astype(BF16)

    return _row_call(body, name="merge_fwd", T=ya.shape[0],
                     ins=[(ya, True), (yb, True), (wa, False), (wb, False), (zg, (2,))],
                     outs=[(D_MODEL, BF16)] * 3)


def _merge_bwd(dx1, w_out, wa, wb, ua, ub, zg, swap):
    def body(dx_ref, wo_ref, wa_ref, wb_ref, ua_ref, ub_ref, zg_ref, dua_ref, dub_ref, dzg_ref, dya_ref, dyb_ref):
        dmv = _dot(dx_ref[...].astype(BF16), wo_ref[...], 1, 1)
        ga = _sigmoid(zg_ref[:, :D_MODEL])
        gb = _sigmoid(zg_ref[:, D_MODEL:])
        dua = (dmv * ga).astype(BF16)
        dub = (dmv * gb).astype(BF16)
        dua_ref[...] = dua
        dub_ref[...] = dub
        dzg_ref[:, :D_MODEL] = (dmv * ua_ref[...].astype(F32) * ga * (1.0 - ga)).astype(BF16)
        dzg_ref[:, D_MODEL:] = (dmv * ub_ref[...].astype(F32) * gb * (1.0 - gb)).astype(BF16)
        dya_ref[...] = _dot(dua, wa_ref[...], 1, 1)
        dyb_ref[...] = _dot(dub, wb_ref[...], 1, 1)

    return _row_call(body, name="merge_bwd", T=dx1.shape[0],
                     ins=[(dx1, True), (w_out, False), (wa, False), (wb, False), (ua, True), (ub, True), (zg, (2,))],
                     outs=[(D_MODEL, BF16), (D_MODEL, BF16), (SEG, BF16, (IN_PAD, 2)), (HG_W, F32), (FOX_W, F32)],
                     swap=swap)


def _swiglu_fwd(hf, w_gate, w_up):
    T, D = hf.shape
    F = w_gate.shape[0]
    tm, tn = _pick(T, ROW_BLOCK), _pick(F, FFN_TILE)

    def body(h_ref, wg_ref, wu_ref, a_ref, b_ref, o_ref):
        hv = h_ref[...]
        a_b = _dot(hv, wg_ref[...], 1, 1).astype(BF16)
        b_b = _dot(hv, wu_ref[...], 1, 1).astype(BF16)
        a_ref[...] = a_b
        b_ref[...] = b_b
        av = a_b.astype(F32)
        o_ref[...] = (av * _sigmoid(av) * b_b.astype(F32)).astype(BF16)

    tile = pl.BlockSpec((tm, tn), lambda j, i: (i, j))
    wcol = pl.BlockSpec((tn, D), lambda j, i: (j, 0))
    return _pallas(
        body, name="swiglu_fwd", grid=(F // tn, T // tm),
        in_specs=[pl.BlockSpec((tm, D), lambda j, i: (i, 0)), wcol, wcol],
        out_specs=[tile] * 3, out_shape=[jax.ShapeDtypeStruct((T, F), BF16)] * 3,
        compiler_params=_params(dimension_semantics=("arbitrary",) * 2),
    )(hf, w_gate, w_up)


def _swiglu_bwd(dx, w_down, a, b):
    T, D = dx.shape
    F = w_down.shape[0]
    tm, tn = _pick(T, ROW_BLOCK), _pick(F, FFN_TILE)

    def body(dx_ref, w_ref, a_ref, b_ref, da_ref, db_ref):
        dact = _dot(dx_ref[...].astype(BF16), w_ref[...], 1, 1)
        av = a_ref[...].astype(F32)
        bv = b_ref[...].astype(F32)
        sg = _sigmoid(av)
        da_ref[...] = (dact * bv * sg * (1.0 + av * (1.0 - sg))).astype(BF16)
        db_ref[...] = (dact * av * sg).astype(BF16)

    tile = pl.BlockSpec((tm, tn), lambda j, i: (i, j))
    return _pallas(
        body, name="swiglu_bwd", grid=(F // tn, T // tm),
        in_specs=[pl.BlockSpec((tm, D), lambda j, i: (i, 0)), pl.BlockSpec((tn, D), lambda j, i: (j, 0)), tile, tile],
        out_specs=[tile, tile], out_shape=[jax.ShapeDtypeStruct((T, F), BF16)] * 2,
        compiler_params=_params(dimension_semantics=("arbitrary",) * 2),
    )(dx, w_down, a, b)


def _ple_loss(x2, p, g, w_gate, w_proj, tgt):
    def body(x_ref, p_ref, g_ref, wg_ref, wp_ref, t_ref, hp_ref, dy_ref, dsp_ref, dpp_ref, loss_ref):
        xv = x_ref[...]
        rstd = lax.rsqrt(jnp.mean(xv * xv, axis=-1, keepdims=True) + EPS)
        hp = (xv * rstd * g_ref[...]).astype(BF16)
        hp_ref[...] = hp
        gp = _sigmoid(_dot(hp, wg_ref[...], 1, 0))
        ppv = _dot(p_ref[...].astype(BF16), wp_ref[...], 1, 0)
        err = xv + gp * ppv - t_ref[...]
        part = 0.5 * jnp.sum(jnp.mean(err * err, axis=-1, keepdims=True), axis=0, keepdims=True)
        part = jnp.broadcast_to(part, loss_ref.shape)

        @pl.when(pl.program_id(0) == 0)
        def _():
            loss_ref[...] = part

        @pl.when(pl.program_id(0) > 0)
        def _():
            loss_ref[...] += part

        dy = err * (1.0 / D_MODEL)
        dy_ref[...] = dy
        dsp_ref[...] = (dy * ppv * gp * (1.0 - gp)).astype(BF16)
        dpp_ref[...] = (dy * gp).astype(BF16)

    return _row_call(body, name="ple_loss", T=x2.shape[0],
                     ins=[(x2, True), (p, True), (g, False), (w_gate, False), (w_proj, False), (tgt, True)],
                     outs=[(D_MODEL, BF16), (D_MODEL, F32), (D_MODEL, BF16), (D_MODEL, BF16)],
                     acc_outs=[((8, LANES), F32)])


def _hg_tri():
    C = HG_CHUNK
    return _iota((C, C), 1) <= _iota((C, C), 0)


def _hg_ref_row(b, blk):
    mid = blk * HG_SUB + HG_SUB // 2 - 1
    return b[mid:mid + 1]


def _hg_chunk_fwd(q, f, lb, tri_b):
    sgq = _sigmoid(q)
    qt = q * sgq
    sg = _sigmoid(f)
    fg = lb + (1.0 - lb) * sg
    kf = (1.0 - lb) * (1.0 - sg)
    logf = jnp.log(fg)
    b = _split_dot(tri_b, logf, 1, 0)
    refs = [jnp.broadcast_to(_hg_ref_row(b, blk), (HG_SUB, b.shape[1])) for blk in range(HG_CHUNK // HG_SUB)]
    w = jnp.minimum(b - jnp.concatenate(refs, axis=0), EXP_CLAMP)
    return sgq, qt, sg, fg, kf, b, w


def _hg_scores(qs_b, kf, b, row):
    C, S = HG_CHUNK, HG_SUB
    parts, ks = [], []
    for blk in range(C // S):
        e = jnp.exp(jnp.minimum(_hg_ref_row(b, blk) - b, EXP_CLAMP))
        e = jnp.where(row < (blk + 1) * S, e, 0.0)
        k_b = (kf * e).astype(BF16)
        ks.append((e, k_b))
        parts.append(_dot(qs_b[blk * S:(blk + 1) * S], k_b, 1, 1))
    return jnp.concatenate(parts, axis=0), ks


def _hgrn_fwd(z, lb_logits, gain, blocks):
    T = z.shape[0]
    RB = min(ROW_BLOCK, T)
    nb, cpb = T // RB, RB // HG_CHUNK
    C, DK = HG_CHUNK, HG_DK
    n = len(blocks)

    def body(*refs):
        z_ref, lg_ref, g_ref = refs[:3]
        o_ref, y_ref, st_ref = refs[3 + n:6 + n]
        s_ref = refs[6 + 2 * n]
        g_start, g_forward, g_finish = _gather_steps(refs[3:3 + n], refs[6 + n:6 + 2 * n], *refs[7 + 2 * n:])

        @pl.when(pl.program_id(0) == 0)
        def _():
            s_ref[...] = jnp.zeros_like(s_ref)
            g_start()

        lg = lg_ref[...]
        lb_all = 1.0 / (1.0 + jnp.exp(lg[1:2] - lg[0:1]))
        gain_v = g_ref[...]
        tri = _hg_tri()
        tri_b = tri.astype(BF16)
        row = _iota((C, DK), 0)

        def chunk(ci, carry):
            r0 = pl.multiple_of(ci * C, C)
            rows = pl.ds(r0, C)
            _, qt, _, _, kf_all, b_all, w = _hg_chunk_fwd(z_ref[rows, 0:HG_W], z_ref[rows, HG_W:2 * HG_W], lb_all,
                                                          tri_b)
            qs_all = (qt * jnp.exp(w)).astype(BF16)
            qd_all = (qt * jnp.exp(b_all)).astype(BF16)
            bl_all = b_all[C - 1:C]
            kd_all = (kf_all * jnp.exp(bl_all - b_all)).astype(BF16)
            ebl_all = jnp.exp(bl_all)
            v_all = z_ref[rows, 2 * HG_W:3 * HG_W].astype(BF16)
            g_all = z_ref[rows, 3 * HG_W:4 * HG_W]
            gate_all = g_all * _sigmoid(g_all)
            for h in range(HG_HEADS):
                cs = slice(h * DK, (h + 1) * DK)
                st = s_ref[h]
                st_ref[pl.ds(pl.multiple_of((ci * HG_HEADS + h) * DK, DK), DK), :] = st
                v_b = v_all[:, cs]
                a, _ = _hg_scores(qs_all[:, cs], kf_all[:, cs], b_all[:, cs], row)
                a = jnp.where(tri, a, 0.0)
                o = _dot(qd_all[:, cs], st.astype(BF16), 1, 1) + _dot(a.astype(BF16), v_b, 1, 0)
                s_ref[h] = st * ebl_all[:, cs] + _dot(v_b, kd_all[:, cs], 0, 0)
                o_ref[rows, cs] = o
                rstd = lax.rsqrt(jnp.mean(o * o, axis=-1, keepdims=True) + EPS)
                y_ref[rows, cs] = (o * rstd * gain_v * gate_all[:, cs]).astype(BF16)
            return carry

        lax.fori_loop(0, cpb, chunk, 0, unroll=4)

        @pl.when(pl.program_id(0) == max(nb - 2, 0))
        def _():
            g_forward()

        @pl.when(pl.program_id(0) == nb - 1)
        def _():
            g_finish()

    hbm = pl.BlockSpec(memory_space=pl.ANY)
    res = _pallas(
        body, name="hgrn_fwd", grid=(nb,),
        in_specs=[pl.BlockSpec((RB, HG_COLS), lambda i: (i, 0)), pl.BlockSpec((2, HG_W), lambda i: (0, 0)),
                  pl.BlockSpec((1, DK), lambda i: (0, 0))] + [hbm] * n,
        out_specs=[pl.BlockSpec((RB, HG_W), lambda i: (i, 0)), pl.BlockSpec((RB, HG_W), lambda i: (i, 0)),
                   pl.BlockSpec((cpb * HG_HEADS * DK, DK), lambda i: (i, 0))] + [hbm] * n,
        out_shape=[jax.ShapeDtypeStruct((T, HG_W), F32), jax.ShapeDtypeStruct((T, HG_W), BF16),
                   jax.ShapeDtypeStruct((T // C * HG_HEADS * DK, DK), F32)]
        + [jax.ShapeDtypeStruct((N_DEV,) + b.shape, b.dtype) for b in blocks],
        scratch_shapes=[pltpu.VMEM((HG_HEADS, DK, DK), F32)] + _gather_scratch(n),
        compiler_params=_params(dimension_semantics=("arbitrary",)),
    )(z, lb_logits, gain, *blocks)
    return res[0], res[1], res[2], res[3:]


def _hgrn_bwd(z, o_raw, dy, states, lb_logits, gain, dz_buf):
    T = z.shape[0]
    RB = min(ROW_BLOCK, T)
    nb, cpb = T // RB, RB // HG_CHUNK
    C, DK, S = HG_CHUNK, HG_DK, HG_SUB

    def body(z_ref, o_ref, dy_ref, st_ref, lg_ref, g_ref, _buf_ref, dz_ref, dlg_ref, dg_ref, ds_ref, dlb_ref):
        step = pl.program_id(0)

        @pl.when(step == 0)
        def _():
            ds_ref[...] = jnp.zeros_like(ds_ref)
            dlb_ref[...] = jnp.zeros_like(dlb_ref)
            dg_ref[...] = jnp.zeros_like(dg_ref)

        lg = lg_ref[...]
        lb_all = 1.0 / (1.0 + jnp.exp(lg[1:2] - lg[0:1]))
        gain_v = g_ref[...]
        tri = _hg_tri()
        tri_b = tri.astype(BF16)
        row = _iota((C, DK), 0)

        def chunk(cj, carry):
            ci = cpb - 1 - cj
            r0 = pl.multiple_of(ci * C, C)
            rows = pl.ds(r0, C)
            q_all = z_ref[rows, 0:HG_W]
            g_all = z_ref[rows, 3 * HG_W:4 * HG_W]
            sgq_all, qt_all, sg_all, fg_all, kf_all, b_all, w_all = _hg_chunk_fwd(
                q_all, z_ref[rows, HG_W:2 * HG_W], lb_all, tri_b)
            ew_all = jnp.exp(w_all)
            eb_all = jnp.exp(b_all)
            bl_all = b_all[C - 1:C]
            ebl_all = jnp.exp(bl_all)
            ekd_all = jnp.exp(bl_all - b_all)
            qs_all = (qt_all * ew_all).astype(BF16)
            qd_all = (qt_all * eb_all).astype(BF16)
            kd_all = (kf_all * ekd_all).astype(BF16)
            v_all = z_ref[rows, 2 * HG_W:3 * HG_W].astype(BF16)
            sgg_all = _sigmoid(g_all)
            t1_all = dy_ref[rows, :] * (g_all * sgg_all)
            db_heads, dqt_heads, dkf_heads, dv_heads, n_heads = [], [], [], [], []
            for h in range(HG_HEADS):
                cs = slice(h * DK, (h + 1) * DK)
                kf, b, ew, eb, ebl, ekd = kf_all[:, cs], b_all[:, cs], ew_all[:, cs], eb_all[:, cs], ebl_all[:, cs], \
                    ekd_all[:, cs]
                qs_b, qd_b, kd_b, v_b = qs_all[:, cs], qd_all[:, cs], kd_all[:, cs], v_all[:, cs]
                st = st_ref[pl.ds(pl.multiple_of((ci * HG_HEADS + h) * DK, DK), DK), :]
                dst = ds_ref[h]
                o = o_ref[rows, cs]
                rstd = lax.rsqrt(jnp.mean(o * o, axis=-1, keepdims=True) + EPS)
                n = o * rstd
                n_heads.append(n)
                t1 = t1_all[:, cs]
                dg_ref[...] += jnp.sum(t1 * n, axis=0, keepdims=True)
                dn = t1 * gain_v
                do = rstd * (dn - n * jnp.mean(dn * n, axis=-1, keepdims=True))
                do_b = do.astype(BF16)
                a, ks = _hg_scores(qs_b, kf, b, row)
                a = jnp.where(tri, a, 0.0)
                dst_b = dst.astype(BF16)
                dqd = _dot(do_b, st.astype(BF16), 1, 0)
                da = jnp.where(tri, _dot(do_b, v_b, 1, 1), 0.0)
                dv = _dot(a.astype(BF16), do_b, 0, 0) + _dot(kd_b, dst_b, 1, 1)
                dkd = _dot(v_b, dst_b, 1, 0)
                ds_ref[h] = dst * ebl + _dot(do_b, qd_b, 0, 0)
                dkd_kd = dkd * kd_b.astype(F32)
                dbl = ebl * jnp.sum(dst * st, axis=0, keepdims=True) + jnp.sum(dkd_kd, axis=0, keepdims=True)
                da_b = da.astype(BF16)
                dqs_parts = []
                dk_in = jnp.zeros((C, DK), F32)
                db_k = jnp.zeros((C, DK), F32)
                for blk in range(C // S):
                    e, k_b = ks[blk]
                    da_blk = da_b[blk * S:(blk + 1) * S]
                    dqs_parts.append(_dot(da_blk, k_b, 1, 0))
                    dks = _dot(da_blk, qs_b[blk * S:(blk + 1) * S], 0, 0)
                    dk_in = dk_in + dks * e
                    db_k = db_k + dks * k_b.astype(F32)
                dqs = jnp.concatenate(dqs_parts, axis=0)
                db = qs_b.astype(F32) * dqs - db_k + dqd * qd_b.astype(F32) - dkd_kd
                db_heads.append(db + jnp.where(row == C - 1, dbl, 0.0))
                dqt_heads.append(dqs * ew + dqd * eb)
                dkf_heads.append(dk_in + dkd * ekd)
                dv_heads.append(dv)
            dlogf = _split_dot(tri_b, jnp.concatenate(db_heads, axis=1), 0, 0)
            dfg = dlogf / fg_all - jnp.concatenate(dkf_heads, axis=1)
            dlb_ref[...] += jnp.sum(dfg * (1.0 - sg_all), axis=0, keepdims=True)
            dqt = jnp.concatenate(dqt_heads, axis=1)
            n_all = jnp.concatenate(n_heads, axis=1)
            gain_all = jnp.concatenate([gain_v] * HG_HEADS, axis=1)
            dz_ref[rows, 0:HG_W] = (dqt * sgq_all * (1.0 + q_all * (1.0 - sgq_all))).astype(BF16)
            dz_ref[rows, HG_W:2 * HG_W] = (dfg * (1.0 - lb_all) * sg_all * (1.0 - sg_all)).astype(BF16)
            dz_ref[rows, 2 * HG_W:3 * HG_W] = jnp.concatenate(dv_heads, axis=1).astype(BF16)
            dz_ref[rows, 3 * HG_W:4 * HG_W] = (dy_ref[rows, :] * n_all * gain_all * sgg_all
                                               * (1.0 + g_all * (1.0 - sgg_all))).astype(BF16)
            return carry

        lax.fori_loop(0, cpb, chunk, 0, unroll=4)

        @pl.when(step == nb - 1)
        def _():
            d0 = dlb_ref[...] * lb_all * (1.0 - lb_all)
            dlg_ref[0:1, :] = d0
            dlg_ref[1:2, :] = -d0

    rev = lambda i: (nb - 1 - i, 0)
    fix = lambda i: (0, 0)
    return _pallas(
        body, name="hgrn_bwd", grid=(nb,),
        in_specs=[pl.BlockSpec((RB, HG_COLS), rev), pl.BlockSpec((RB, HG_W), rev), pl.BlockSpec((RB, HG_W), rev),
                  pl.BlockSpec((cpb * HG_HEADS * DK, DK), rev), pl.BlockSpec((2, HG_W), fix),
                  pl.BlockSpec((1, DK), fix), pl.BlockSpec(memory_space=pl.ANY)],
        out_specs=[pl.BlockSpec((RB, HG_COLS), rev), pl.BlockSpec((2, HG_W), fix), pl.BlockSpec((1, DK), fix)],
        out_shape=[jax.ShapeDtypeStruct(dz_buf.shape, BF16), jax.ShapeDtypeStruct((2, HG_W), F32),
                   jax.ShapeDtypeStruct((1, DK), F32)],
        scratch_shapes=[pltpu.VMEM((HG_HEADS, DK, DK), F32), pltpu.VMEM((1, HG_W), F32)],
        input_output_aliases={6: 0},
        compiler_params=_params(dimension_semantics=("arbitrary",)),
    )(z, o_raw, dy, states, lb_logits, gain, dz_buf)


def _head_ones():
    r, c = _iota((FOX_W, FOX_W), 0), _iota((FOX_W, FOX_W), 1)
    return ((r // FOX_DH) == (c // FOX_DH)).astype(BF16)


def _log_sigmoid(x):
    return jnp.minimum(x, 0.0) - jnp.log(1.0 + jnp.exp(-jnp.abs(x)))


def _fox_prep(z, bias, qg, kg):
    T = z.shape[0]
    tm = min(ROW_BLOCK, T)
    nb = T // tm

    def body(z_ref, b_ref, qg_ref, kg_ref, q_ref, k_ref, v_ref, qa_ref, ka_ref, carry_ref):
        @pl.when(pl.program_id(0) == 0)
        def _():
            carry_ref[...] = jnp.zeros_like(carry_ref)

        ones = _head_ones()
        normed = []
        for src, g_ref in ((0, qg_ref), (1, kg_ref)):
            xv = z_ref[:, src * FOX_W:(src + 1) * FOX_W]
            ms = _split_dot(ones, xv * xv, 1, 0, mat_first=False) * (1.0 / FOX_DH)
            normed.append(xv * lax.rsqrt(ms + EPS) * g_ref[...])
        qn, kn = normed
        q_ref[...] = (qn * FOX_DH ** -0.5).astype(BF16)
        k_b = kn.astype(BF16)
        k_ref[...] = k_b
        v_ref[...] = z_ref[:, 2 * FOX_W:3 * FOX_W].astype(BF16)
        logf = _log_sigmoid(z_ref[:, 3 * FOX_W:FOX_COLS] + b_ref[...])
        r, c = _iota((tm, tm), 0), _iota((tm, tm), 1)
        tri_b = (c <= r).astype(BF16)
        cum = _split_dot(tri_b, logf, 1, 0, terms=3) + carry_ref[...]
        carry_ref[...] = cum[tm - 1:tm]
        c2 = cum * LOG2E
        hi = c2.astype(BF16)
        rem = c2 - hi.astype(F32)
        mid = rem.astype(BF16)
        lo = (rem - mid.astype(F32)).astype(BF16)
        hrow, col = _iota((LANES, 2 * FOX_W), 0), _iota((LANES, 2 * FOX_W), 1)
        base = hrow * LANES + jnp.where(hrow % 2 == 0, FOX_DH, 0)
        placed = None
        for t, part in enumerate((hi, mid, lo)):
            place = jnp.logical_and(col == base + t, hrow < FOX_HEADS).astype(BF16)
            term = _dot(part, place, 1, 0)
            placed = term if placed is None else placed + term
        colw = _iota((tm, 2 * FOX_W), 1)
        head, lane = colw // LANES, colw % LANES
        own = (lane < FOX_DH) == (head % 2 == 0)
        other = jnp.where(head % 2 == 0, lane - FOX_DH, lane)
        ones_q = jnp.where(jnp.logical_and(other >= 0, other < 3), -1.0, 0.0)
        q2 = (qn * (FOX_DH ** -0.5 * LOG2E)).astype(BF16)
        q_exp = jnp.concatenate([q2[:, (h // 2) * LANES:(h // 2 + 1) * LANES] for h in range(FOX_HEADS)], axis=1)
        k_exp = jnp.concatenate([k_b[:, (h // 2) * LANES:(h // 2 + 1) * LANES] for h in range(FOX_HEADS)], axis=1)
        qa_ref[...] = jnp.where(own, q_exp, ones_q.astype(BF16))
        ka_ref[...] = jnp.where(own, k_exp, placed.astype(BF16))

    wide = pl.BlockSpec((tm, 2 * FOX_W), lambda i: (i, 0))
    return _pallas(
        body, name="fox_prep", grid=(nb,),
        in_specs=[pl.BlockSpec((tm, SEG), lambda i: (i, 1)), pl.BlockSpec((1, LANES), lambda i: (0, 0)),
                  pl.BlockSpec((1, FOX_W), lambda i: (0, 0)), pl.BlockSpec((1, FOX_W), lambda i: (0, 0))],
        out_specs=[pl.BlockSpec((tm, FOX_W), lambda i: (i, 0))] * 3 + [wide] * 2,
        out_shape=[jax.ShapeDtypeStruct((T, FOX_W), BF16)] * 3 + [jax.ShapeDtypeStruct((T, 2 * FOX_W), BF16)] * 2,
        scratch_shapes=[pltpu.VMEM((1, LANES), F32)],
        compiler_params=_params(dimension_semantics=("arbitrary",)),
    )(z, bias, qg, kg)


def _fox_fwd(qa, ka, vb, blocks):
    T = qa.shape[0]
    tq = min(ROW_BLOCK, T)
    nq = T // tq
    NEG = -1e30
    n = len(blocks)
    n_pairs = FOX_HEADS // 2

    n_in = 3

    def body(*refs):
        q_ref, k_ref, v_ref = refs[:n_in]
        o_ref, lse_ref = refs[n_in + n:n_in + n + 2]
        m_sc, l_sc, acc_sc = refs[n_in + 2 * n + 2:n_in + 2 * n + 5]
        pr, qi = pl.program_id(0), pl.program_id(1)
        g_start, g_forward, g_finish = _gather_steps(
            refs[n_in:n_in + n], refs[n_in + n + 2:n_in + 2 * n + 2], *refs[n_in + 2 * n + 5:])

        @pl.when(jnp.logical_and(pr == 0, qi == 0))
        def _():
            g_start()

        @pl.when(jnp.logical_and(pr == n_pairs // 2, qi == 0))
        def _():
            g_forward()

        m_sc[...] = jnp.full_like(m_sc, NEG)
        l_sc[...] = jnp.zeros_like(l_sc)
        acc_sc[...] = jnp.zeros_like(acc_sc)
        lane = _iota((tq, LANES), 1)

        def block(masked, ki):
            keys = pl.ds(pl.multiple_of(ki * tq, tq), tq)
            vv = v_ref[keys, :]
            for hh in range(2):
                hs = slice(hh * LANES, (hh + 1) * LANES)
                s = _dot(q_ref[:, hs], k_ref[keys, hs], 1, 1)
                tiles = [s[:, j * LANES:(j + 1) * LANES] for j in range(tq // LANES)]
                if masked:
                    row, col = _iota((tq, LANES), 0), _iota((tq, LANES), 1)
                    tiles = [jnp.where(row >= col + j * LANES, t, NEG) for j, t in enumerate(tiles)]
                m_old = m_sc[hh]
                top = jnp.broadcast_to(jnp.max(functools.reduce(jnp.maximum, tiles), axis=-1, keepdims=True),
                                       (tq, LANES))
                m_new = jnp.maximum(m_old, top)
                alpha = jnp.exp2(m_old - m_new)
                ps = [jnp.exp2(t - m_new) for t in tiles]
                l_sc[hh] = alpha * l_sc[hh] + functools.reduce(jnp.add, ps)
                m_sc[hh] = m_new
                p_b = jnp.concatenate([p.astype(BF16) for p in ps], axis=1)
                acc_sc[hh] = alpha * acc_sc[hh] + _dot(p_b, vv, 1, 0)

        def before(ki, carry):
            block(False, ki)
            return carry

        lax.fori_loop(0, qi, before, 0)
        block(True, qi)
        l0 = jnp.sum(l_sc[0], axis=-1, keepdims=True)
        l1 = jnp.sum(l_sc[1], axis=-1, keepdims=True)
        o_ref[...] = jnp.where(lane < FOX_DH, acc_sc[0] * (1.0 / l0), acc_sc[1] * (1.0 / l1))
        lse_ref[:, :LANES] = m_sc[0] + jnp.log2(l0)
        lse_ref[:, LANES:] = m_sc[1] + jnp.log2(l1)

        @pl.when(jnp.logical_and(pr == n_pairs - 1, qi == nq - 1))
        def _():
            g_finish()

    qmap = lambda p, i: (i, p)
    whole = lambda p, i: (0, p)
    hbm = pl.BlockSpec(memory_space=pl.ANY)
    res = _pallas(
        body, name="fox_fwd", grid=(n_pairs, nq),
        in_specs=[pl.BlockSpec((tq, 2 * LANES), qmap), pl.BlockSpec((T, 2 * LANES), whole),
                  pl.BlockSpec((T, LANES), whole)] + [hbm] * n,
        out_specs=[pl.BlockSpec((tq, LANES), qmap), pl.BlockSpec((tq, 2 * LANES), qmap)] + [hbm] * n,
        out_shape=[jax.ShapeDtypeStruct((T, FOX_W), F32), jax.ShapeDtypeStruct((T, 2 * FOX_W), F32)]
        + [jax.ShapeDtypeStruct((N_DEV,) + b.shape, b.dtype) for b in blocks],
        scratch_shapes=[pltpu.VMEM((2, tq, LANES), F32)] * 3 + _gather_scratch(n),
        compiler_params=_params(dimension_semantics=("arbitrary",) * 2),
    )(qa, ka, vb, *blocks)
    return res[0], res[1], res[2:]


def _fox_bwd(qs, kn, vb, qa, ka, o, do, lse, hs):
    T = qs.shape[0]
    tq = min(ROW_BLOCK, T)
    nq = T // tq
    n = len(hs)
    n_pairs = FOX_HEADS // 2

    def body(*refs):
        q_ref, k_ref, v_ref, qa_ref, ka_ref, o_ref, do_ref, lse_ref = refs[:8]
        dq_ref, dk_ref, dv_ref, dcs_ref, drs_ref = refs[8 + n:13 + n]
        pr, ki = pl.program_id(0), pl.program_id(1)
        x_start, x_finish = _chip_exchange_steps(refs[8:8 + n], refs[13 + n:13 + 2 * n], *refs[13 + 2 * n:])

        @pl.when(jnp.logical_and(pr == 0, ki == 0))
        def _():
            x_start()
            drs_ref[...] = jnp.zeros_like(drs_ref)

        @pl.when(ki == 0)
        def _():
            dq_ref[...] = jnp.zeros_like(dq_ref)

        dk_ref[...] = jnp.zeros_like(dk_ref)
        dv_ref[...] = jnp.zeros_like(dv_ref)
        dcs_ref[...] = jnp.zeros_like(dcs_ref)

        def block(masked, qi):
            lane = _iota((tq, LANES), 1)
            qrows = pl.ds(pl.multiple_of(qi * tq, tq), tq)
            qv, kv, vv = q_ref[qrows, :], k_ref[...], v_ref[...]
            ov, dov = o_ref[qrows, :], do_ref[qrows, :]
            dq_acc = jnp.zeros((tq, LANES), F32)
            dk_acc = jnp.zeros((tq, LANES), F32)
            dv_acc = jnp.zeros((tq, LANES), F32)
            dcs_acc = jnp.zeros((8, tq), F32)
            drs_acc = jnp.zeros((tq, LANES), F32)
            prod = dov * ov
            nt = tq // LANES
            for hh in range(2):
                in_head = (lane < FOX_DH) if hh == 0 else (lane >= FOX_DH)
                hs_ = slice(hh * LANES, (hh + 1) * LANES)
                zb = jnp.zeros_like(qv)
                qm = jnp.where(in_head, qv, zb)
                km = jnp.where(in_head, kv, zb)
                dom = jnp.where(in_head, dov, 0.0).astype(BF16)
                delta_b = jnp.broadcast_to(jnp.sum(jnp.where(in_head, prod, 0.0), axis=1, keepdims=True), (tq, LANES))
                lse_b = lse_ref[qrows, hs_]
                s = _dot(qa_ref[qrows, hs_], ka_ref[:, hs_], 1, 1)
                dp = _dot(dom, vv, 1, 1)
                p_tiles, ds_tiles, col_tiles = [], [], []
                row_part = jnp.zeros((tq, LANES), F32)
                for j in range(nt):
                    js = slice(j * LANES, (j + 1) * LANES)
                    p = jnp.exp2(s[:, js] - lse_b)
                    if masked:
                        p = jnp.where(_iota((tq, LANES), 0) >= _iota((tq, LANES), 1) + j * LANES, p, 0.0)
                    ds = p * (dp[:, js] - delta_b)
                    p_tiles.append(p.astype(BF16))
                    ds_tiles.append(ds.astype(BF16))
                    col_tiles.append(jnp.sum(ds, axis=0, keepdims=True))
                    row_part = row_part + ds
                p_b = jnp.concatenate(p_tiles, axis=1)
                ds_b = jnp.concatenate(ds_tiles, axis=1)
                dv_acc = dv_acc + _dot(p_b, dom, 0, 0)
                dq_acc = dq_acc + _dot(ds_b, km, 1, 0)
                dk_acc = dk_acc + _dot(ds_b, qm, 0, 0)
                dcs_acc = dcs_acc + jnp.where(_iota((8, tq), 0) == hh, jnp.concatenate(col_tiles, axis=1), 0.0)
                rowsum = jnp.sum(row_part, axis=1, keepdims=True)
                drs_acc = drs_acc + jnp.where(lane == 2 * pr + hh, rowsum, 0.0)
            drs_ref[qrows, :] += drs_acc
            dq_ref[qrows, :] += dq_acc
            dk_ref[...] += dk_acc
            dv_ref[...] += dv_acc
            dcs_ref[0] += dcs_acc

        block(True, ki)

        def after(qi, carry):
            block(False, qi)
            return carry

        lax.fori_loop(ki + 1, nq, after, 0)

        @pl.when(jnp.logical_and(pr == n_pairs - 1, ki == nq - 1))
        def _():
            x_finish()

    whole = lambda p, j: (0, p)
    kmap = lambda p, j: (j, p)
    hbm = pl.BlockSpec(memory_space=pl.ANY)
    res = _pallas(
        body, name="fox_bwd", grid=(n_pairs, nq),
        in_specs=[pl.BlockSpec((T, LANES), whole), pl.BlockSpec((tq, LANES), kmap), pl.BlockSpec((tq, LANES), kmap),
                  pl.BlockSpec((T, 2 * LANES), whole), pl.BlockSpec((tq, 2 * LANES), kmap),
                  pl.BlockSpec((T, LANES), whole), pl.BlockSpec((T, LANES), whole), pl.BlockSpec((T, 2 * LANES), whole)]
        + [hbm] * n,
        out_specs=[pl.BlockSpec((T, LANES), whole), pl.BlockSpec((tq, LANES), kmap),
                   pl.BlockSpec((tq, LANES), kmap), pl.BlockSpec((1, 8, tq), lambda p, j: (p, 0, j)),
                   pl.BlockSpec((T, LANES), lambda p, j: (0, 0))] + [hbm] * n,
        out_shape=[jax.ShapeDtypeStruct((T, FOX_W), F32)] * 3
        + [jax.ShapeDtypeStruct((n_pairs, 8, T), F32), jax.ShapeDtypeStruct((T, LANES), F32)]
        + _chip_exchange_shapes(hs),
        scratch_shapes=_chip_exchange_scratch(n),
        compiler_params=_params(dimension_semantics=("arbitrary",) * 2),
    )(qs, kn, vb, qa, ka, o, do, lse, *hs)
    res = list(res)
    return res[:5] + [res[5:]]


def _fox_post(z, dq, dk, dv, dcs, drs, bias, qg, kg, dz_buf):
    T = z.shape[0]
    tm = min(ROW_BLOCK, T)
    nb = T // tm

    def body(z_ref, dq_ref, dk_ref, dv_ref, dcs_ref, drs_ref, b_ref, qg_ref, kg_ref, _buf_ref, dz_ref, dqg_ref, dkg_ref,
             db_ref, carry_ref):
        @pl.when(pl.program_id(0) == 0)
        def _():
            carry_ref[...] = jnp.zeros_like(carry_ref)
            dqg_ref[...] = jnp.zeros_like(dqg_ref)
            dkg_ref[...] = jnp.zeros_like(dkg_ref)
            db_ref[...] = jnp.zeros_like(db_ref)

        ones = _head_ones()
        for src, g_ref, d_ref, dg_ref, scale in ((0, qg_ref, dq_ref, dqg_ref, FOX_DH ** -0.5), (1, kg_ref, dk_ref, dkg_ref, 1.0)):
            xv = z_ref[:, src * FOX_W:(src + 1) * FOX_W]
            ms = _split_dot(ones, xv * xv, 1, 0, mat_first=False) * (1.0 / FOX_DH)
            rstd = lax.rsqrt(ms + EPS)
            xh = xv * rstd
            dn = d_ref[...] * scale
            dg_ref[...] += jnp.sum(dn * xh, axis=0, keepdims=True)
            dxh = dn * g_ref[...]
            mean = _split_dot(ones, dxh * xh, 1, 0, mat_first=False) * (1.0 / FOX_DH)
            dz_ref[:, src * FOX_W:(src + 1) * FOX_W] = (rstd * (dxh - xh * mean)).astype(BF16)
        dz_ref[:, 2 * FOX_W:3 * FOX_W] = dv_ref[...].astype(BF16)
        row8 = _iota((8, tm), 0)
        dct = jnp.zeros((8, tm), F32)
        for h in range(FOX_HEADS):
            src_row = dcs_ref[h // 2][h % 2:h % 2 + 1, :]
            dct = dct + jnp.where(row8 == h, src_row, 0.0)
        dct = drs_ref[...].T[0:8] - dct
        r, c = _iota((tm, tm), 0), _iota((tm, tm), 1)
        upper_b = (r >= c).astype(BF16)
        rc = _split_dot(upper_b, dct, 1, 0, mat_first=False) + carry_ref[...]
        carry_ref[...] = rc[:, 0:1]
        full = jnp.concatenate([rc, jnp.zeros((LANES - 8, tm), F32)], axis=0)
        dlogf = full.T
        xf = z_ref[:, 3 * FOX_W:FOX_COLS] + b_ref[...]
        df = dlogf * (1.0 - _sigmoid(xf))
        dz_ref[:, 3 * FOX_W:FOX_COLS] = df.astype(BF16)
        dz_ref[:, FOX_COLS:] = jnp.zeros((tm, SEG - FOX_COLS), BF16)
        db_ref[...] += jnp.sum(df, axis=0, keepdims=True)

    rev = lambda i: (nb - 1 - i, 0)
    fix2 = lambda i: (0, 0)
    return _pallas(
        body, name="fox_post", grid=(nb,),
        in_specs=[pl.BlockSpec((tm, SEG), lambda i: (nb - 1 - i, 1)), pl.BlockSpec((tm, FOX_W), rev),
                  pl.BlockSpec((tm, FOX_W), rev),
                  pl.BlockSpec((tm, FOX_W), rev), pl.BlockSpec((FOX_HEADS // 2, 8, tm), lambda i: (0, 0, nb - 1 - i)),
                  pl.BlockSpec((tm, LANES), rev),
                  pl.BlockSpec((1, LANES), fix2), pl.BlockSpec((1, FOX_W), fix2), pl.BlockSpec((1, FOX_W), fix2),
                  pl.BlockSpec(memory_space=pl.ANY)],
        out_specs=[pl.BlockSpec((tm, SEG), lambda i: (nb - 1 - i, 1)), pl.BlockSpec((1, FOX_W), fix2),
                   pl.BlockSpec((1, FOX_W), fix2), pl.BlockSpec((1, LANES), fix2)],
        out_shape=[jax.ShapeDtypeStruct(dz_buf.shape, BF16), jax.ShapeDtypeStruct((1, FOX_W), F32),
                   jax.ShapeDtypeStruct((1, FOX_W), F32), jax.ShapeDtypeStruct((1, LANES), F32)],
        scratch_shapes=[pltpu.VMEM((8, 1), F32)],
        input_output_aliases={9: 0},
        compiler_params=_params(dimension_semantics=("arbitrary",)),
    )(z, dq, dk, dv, dcs, drs, bias, qg, kg, dz_buf)


def _local_step(x, p, tgt, sm, chunks, core):
    lbl, og, fb = sm["hg_lb_logits"], sm["hg_onorm_g"], sm["fox_f_bias"]
    fbias = jnp.pad(fb, ((0, 0), (0, LANES - FOX_HEADS)))
    qg = jnp.tile(sm["fox_q_norm_g"], (1, FOX_HEADS))
    kg = jnp.tile(sm["fox_k_norm_g"], (1, FOX_HEADS))

    assert BIG[0] == "w_in"
    h, got0 = _rms_fwd(x, sm["norm_mix_g"], chunks[:1], name="rms_mix")
    W = {"w_in": _full_of_chunks("w_in", got0[0])}
    rest = dict(zip(BIG[1:], chunks[1:]))
    first, second = ["w_ffn_gate"], ["w_ffn_up"]
    third = [n for n in BIG[1:] if n not in first + second]
    z, got1 = _matmul(h, W["w_in"], tb=True, gather=[rest[n] for n in first], name="mm_z")
    o_raw, ya, states, got2 = _hgrn_fwd(z, lbl, og, [rest[n] for n in second])
    qs, kn, vb, qa, ka = _fox_prep(z, fbias, qg, kg)
    yb, lse, got3 = _fox_fwd(qa, ka, vb, [rest[n] for n in third])
    W = dict(W, **{n: _full_of_chunks(n, g)
                   for n, g in zip(first + second + third, list(got1) + list(got2) + list(got3))})
    merged, ua, ub = _merge_fwd(ya, yb, W["w_branch_a"], W["w_branch_b"], z)
    x1, hf = _matmul(merged, W["w_out"], add=x, norm_fwd=sm["norm_ffn_g"], name="mm_x1")
    a, b, act = _swiglu_fwd(hf, W["w_ffn_gate"], W["w_ffn_up"])
    x2 = _matmul(act, W["w_ffn_down"], add=x1, name="mm_x2")
    hp, dy, dsp, dpp, loss = _ple_loss(x2, p, sm["norm_ple_g"], W["w_ple_gate"], W["w_ple_proj"], tgt)

    G = {}
    G["w_ple_proj"] = _matmul(p, dpp, ta=True, out_dtype=BF16, name="mm_dw_ple_proj")
    G["w_ple_gate"] = _matmul(hp, dsp, ta=True, out_dtype=BF16, name="mm_dw_ple_gate")
    dx2, d_ple_g = _matmul(dsp, W["w_ple_gate"], tb=True, norm_bwd=(x2, sm["norm_ple_g"], dy), name="mm_dx2")
    G["w_ffn_down"] = _matmul(act, dx2, ta=True, out_dtype=BF16, name="mm_dw_ffn_down")
    da, db = _swiglu_bwd(dx2, W["w_ffn_down"], a, b)
    G["w_ffn_gate"] = _matmul(da, hf, ta=True, out_dtype=BF16, name="mm_dw_ffn_gate")
    G["w_ffn_up"] = _matmul(db, hf, ta=True, out_dtype=BF16, name="mm_dw_ffn_up")
    dhf = _matmul(da, W["w_ffn_gate"], name="mm_dhf_a")
    dx1, d_ffn_g = _matmul(db, W["w_ffn_up"], add=dhf, norm_bwd=(x1, sm["norm_ffn_g"], dx2), name="mm_dx1")
    G["w_out"] = _matmul(merged, dx1, ta=True, out_dtype=BF16, name="mm_dw_out")
    late = ["w_branch_a", "w_branch_b"]
    early = [n for n in BIG[1:] if n not in late]
    g4 = {n: _chunks4(n, G[n]) for n in early}
    (dua, dub, dz, dya, dyb), got_early = _merge_bwd(dx1, W["w_out"], W["w_branch_a"], W["w_branch_b"], ua, ub, z,
                                                     [g4[n] for n in early])
    G["w_branch_a"] = _matmul(ya, dua, ta=True, out_dtype=BF16, name="mm_dw_branch_a")
    G["w_branch_b"] = _matmul(yb, dub, ta=True, out_dtype=BF16, name="mm_dw_branch_b")
    g4.update({n: _chunks4(n, G[n]) for n in late})
    got4 = dict(zip(early, got_early))
    got4.update(zip(late, _sibling_exchange([g4[n] for n in late], name="grads_to_sibling_branch")))
    hb_rest = _chip_sum([g4[n] for n in BIG[1:]], [got4[n] for n in BIG[1:]], core, name="chip_sum_rest")
    dq, dk, dv, dcs, drs, got_rest = _fox_bwd(qs, kn, vb, qa, ka, yb, dyb, lse, hb_rest)
    dz, d_qg, d_kg, d_fb = _fox_post(z, dq, dk, dv, dcs, drs, fbias, qg, kg, dz)
    dz, d_lbl, d_og = _hgrn_bwd(z, o_raw, dya, states, lbl, og, dz)
    G["w_in"] = _matmul(dz, h, ta=True, out_dtype=BF16, name="mm_dw_in")
    hb_in = _sibling_sums({"w_in": G["w_in"]}, core, tag="w_in")
    (grad_x, d_mix_g), got_in = _matmul(dz, W["w_in"], exchange=hb_in,
                                        norm_bwd=(x, sm["norm_mix_g"], dx1), name="mm_dx")

    gs = {"norm_mix_g": d_mix_g, "hg_lb_logits": d_lbl, "hg_onorm_g": d_og, "fox_f_bias": d_fb[:, :FOX_HEADS],
          "fox_q_norm_g": d_qg.reshape(FOX_HEADS, FOX_DH).sum(0, keepdims=True),
          "fox_k_norm_g": d_kg.reshape(FOX_HEADS, FOX_DH).sum(0, keepdims=True),
          "norm_ffn_g": d_ffn_g, "norm_ple_g": d_ple_g}
    return loss, grad_x, gs, hb_in + hb_rest, list(got_in) + list(got_rest)


def _pack_rows(parts, total):
    buf = jnp.concatenate(parts, axis=-2)
    pad = total - buf.shape[-2]
    widths = [(0, 0)] * (buf.ndim - 2) + [(0, pad), (0, 0)]
    return jnp.pad(buf, widths)


def _chunk_of_shard(n, w):
    if n == "w_in":
        return jnp.pad(w, ((0, IN_SHARD_PAD - IN_SHARD), (0, 0)))
    return w


def _full_of_chunks(n, g):
    _, a, b = g.shape
    if n == "w_in":
        w = g[:, :IN_SHARD].reshape(IN_COLS, b)
        gap = jnp.zeros((SEG - FOX_LOGICAL, b), g.dtype)
        return jnp.concatenate([w[:HG_COLS + FOX_LOGICAL], gap, w[HG_COLS + FOX_LOGICAL:]], axis=0)
    if BIG_SHAPE[n][2] == 0:
        return g.reshape(N_DEV * a, b)
    return g.transpose(1, 0, 2).reshape(a, N_DEV * b)


def _chunks_of_full(n, g):
    if n == "w_in":
        w = jnp.concatenate([g[:HG_COLS + FOX_LOGICAL], g[2 * SEG:]], axis=0).reshape(N_DEV, IN_SHARD, g.shape[1])
        return jnp.pad(w, ((0, 0), (0, IN_SHARD_PAD - IN_SHARD), (0, 0)))
    if BIG_SHAPE[n][2] == 0:
        return g.reshape(N_DEV, g.shape[0] // N_DEV, g.shape[1])
    return g.reshape(g.shape[0], N_DEV, g.shape[1] // N_DEV).transpose(1, 0, 2)


def _pack_small(vals, loss_row=None):
    parts = [vals[n].reshape(SMALL_ROWS[n], -1) for n in SMALL]
    parts = [jnp.pad(v, ((0, 0), (0, LANES - v.shape[1]))) for v in parts]
    if loss_row is not None:
        parts.append(loss_row)
    return _pack_rows(parts, SMALL_TOTAL)


def _unpack_small(buf, like):
    out, r0 = {}, 0
    for n in SMALL:
        rows, size = SMALL_ROWS[n], like[n].size
        blk = buf[r0:r0 + rows]
        out[n] = (blk if size == rows * LANES else blk[:, :size]).reshape(like[n].shape)
        r0 += rows
    return out


def _place():
    return lax.axis_index("x"), lax.axis_index("y"), lax.axis_index("c")


def _gather_steps(x_refs, out_refs, send_sems, recv_sems, local_sems):
    n = len(x_refs)
    x, y, c = _place()
    me, sibling = (x, y, c), (x, y, 1 - c)
    chips = [(1 - x, y), (x, 1 - y), (1 - x, 1 - y)]

    def slot(i, px, py, pc):
        return out_refs[i].at[4 * px + 2 * py + pc]

    def copy(k, i, blk, to, own=False):
        return pltpu.make_async_remote_copy(
            src_ref=x_refs[i] if own else slot(i, *blk), dst_ref=slot(i, *blk),
            send_sem=send_sems.at[k, i], recv_sem=recv_sems.at[k, i], device_id=to, device_id_type=MESH)

    def mine():
        return [pltpu.make_async_copy(x_refs[i], slot(i, *me), local_sems.at[i]) for i in range(n)]

    def first():
        cps = [copy(0, i, me, sibling, own=True) for i in range(n)]
        return cps + [copy(1 + j, i, me, (*chip, c), own=True) for j, chip in enumerate(chips) for i in range(n)]

    def passed():
        return [copy(4 + j, i, (*chip, c), sibling) for j, chip in enumerate(chips) for i in range(n)]

    def start():
        for cp in mine() + first():
            cp.start()

    def forward():
        fws = passed()
        for j, chip in enumerate(chips):
            for i in range(n):
                copy(1 + j, i, (*chip, c), me).wait_recv()
                fws[j * n + i].start()

    def finish():
        for i in range(n):
            copy(0, i, sibling, me).wait_recv()
        for j, chip in enumerate(chips):
            for i in range(n):
                copy(4 + j, i, (*chip, 1 - c), me).wait_recv()
        for cp in first() + passed():
            cp.wait_send()
        for cp in mine():
            cp.wait()

    return start, forward, finish


def _gather_scratch(n):
    return [pltpu.SemaphoreType.DMA((7, n)), pltpu.SemaphoreType.DMA((7, n)), pltpu.SemaphoreType.DMA((n,))]


def _sibling_steps(g_refs, out_refs, send_sems, recv_sems):
    n = len(g_refs)
    x, y, c = _place()

    def copies():
        return [pltpu.make_async_remote_copy(
            src_ref=g_refs[i].at[:, pl.ds(1 - c, 1)], dst_ref=out_refs[i], send_sem=send_sems.at[i],
            recv_sem=recv_sems.at[i], device_id=(x, y, 1 - c), device_id_type=MESH) for i in range(n)]

    def start():
        for cp in copies():
            cp.start()

    def finish():
        for cp in copies():
            cp.wait()

    return start, finish


def _sibling_shapes(gs):
    return [jax.ShapeDtypeStruct((N_CHIP, 1) + g.shape[2:], g.dtype) for g in gs]


def _sibling_scratch(n):
    return [pltpu.SemaphoreType.DMA((n,)), pltpu.SemaphoreType.DMA((n,))]


def _sibling_exchange(gs, *, name):
    n = len(gs)

    def body(*refs):
        for step in _sibling_steps(refs[:n], refs[n:2 * n], *refs[2 * n:]):
            step()

    hbm = pl.BlockSpec(memory_space=pl.ANY)
    return _pallas(
        body, name=name, out_shape=_sibling_shapes(gs), in_specs=[hbm] * n, out_specs=[hbm] * n,
        scratch_shapes=_sibling_scratch(n),
    )(*gs)


def _chip_sum(g4s, gots, core, *, name):
    n = len(g4s)

    def body(c_ref, *refs):
        for g_ref, r_ref, h_ref in zip(refs[:n], refs[n:2 * n], refs[2 * n:]):
            h_ref[0] = (g_ref[0, 0].astype(F32) + r_ref[0, 0].astype(F32)).astype(BF16)

    shapes = [g.shape[2:] for g in g4s]
    grid_spec = pltpu.PrefetchScalarGridSpec(
        num_scalar_prefetch=1, grid=(N_CHIP,),
        in_specs=[pl.BlockSpec((1, 1) + s, lambda j, c: (j, c[0], 0, 0)) for s in shapes]
        + [pl.BlockSpec((1, 1) + s, lambda j, c: (j, 0, 0, 0)) for s in shapes],
        out_specs=[pl.BlockSpec((1,) + s, lambda j, c: (j, 0, 0)) for s in shapes])
    return list(_pallas(
        body, name=name, grid_spec=grid_spec, out_shape=[jax.ShapeDtypeStruct((N_CHIP,) + s, BF16) for s in shapes],
        compiler_params=_params(dimension_semantics=("arbitrary",)),
    )(core, *g4s, *gots))


def _chip_exchange_steps(h_refs, out_refs, send_sems, recv_sems):
    n = len(h_refs)
    x, y, c = _place()
    chips = [(1 - x, y), (x, 1 - y), (1 - x, 1 - y)]

    def copies():
        return [pltpu.make_async_remote_copy(
            src_ref=h_refs[i].at[2 * px + py], dst_ref=out_refs[i].at[k], send_sem=send_sems.at[k, i],
            recv_sem=recv_sems.at[k, i], device_id=(px, py, c), device_id_type=MESH)
            for k, (px, py) in enumerate(chips) for i in range(n)]

    def start():
        for cp in copies():
            cp.start()

    def finish():
        for cp in copies():
            cp.wait()

    return start, finish


def _chip_exchange_shapes(hs):
    return [jax.ShapeDtypeStruct((3,) + h.shape[1:], h.dtype) for h in hs]


def _chip_exchange_scratch(n):
    return [pltpu.SemaphoreType.DMA((3, n)), pltpu.SemaphoreType.DMA((3, n))]


def _chunks4(n, g):
    gc = _chunks_of_full(n, g)
    return gc.reshape((N_CHIP, 2) + gc.shape[1:])


def _sibling_sums(G, core, *, tag):
    g4 = [_chunks4(n, g) for n, g in G.items()]
    got = _sibling_exchange(g4, name="grads_to_sibling_" + tag)
    return _chip_sum(g4, got, core, name="chip_sum_" + tag)


def _adam_math(w, g, m, v):
    m = ADAM_B1 * m + (1.0 - ADAM_B1) * g
    v = ADAM_B2 * v + (1.0 - ADAM_B2) * (g * g)
    m_hat = m / (1.0 - ADAM_B1 ** ADAM_STEP)
    v_hat = v / (1.0 - ADAM_B2 ** ADAM_STEP)
    delta = -ADAM_LR * (m_hat / (jnp.sqrt(v_hat) + ADAM_EPS) + ADAM_WD * w)
    return delta, m, v


def _adam_shard(hb, got, chip, w, m, v, *, name):
    _, r, c = w.shape
    _, a, cb = hb.shape
    assert cb == c and c % LANES == 0, (hb.shape, w.shape)
    tc = _pick(c, 2 * LANES)

    def body(j_ref, h_ref, r_ref, w_ref, m_ref, v_ref, g_ref, d_ref, nm_ref, nv_ref):
        parts = [h_ref[0], r_ref[0], r_ref[1], r_ref[2]]
        g = None
        for part in parts:
            part = part[:r].astype(F32)
            g = part if g is None else g + part
        d, nm, nv = _adam_math(w_ref[0], g, m_ref[0], v_ref[0])
        g_ref[0] = g
        d_ref[0] = d
        nm_ref[0] = nm
        nv_ref[0] = nv

    blk = pl.BlockSpec((1, r, tc), lambda i, j: (0, 0, i))
    grid_spec = pltpu.PrefetchScalarGridSpec(
        num_scalar_prefetch=1, grid=(c // tc,),
        in_specs=[pl.BlockSpec((1, a, tc), lambda i, j: (j[0], 0, i)),
                  pl.BlockSpec((3, a, tc), lambda i, j: (0, 0, i)), blk, blk, blk],
        out_specs=[blk] * 4)
    return _pallas(
        body, name=name, grid_spec=grid_spec, out_shape=[jax.ShapeDtypeStruct((1, r, c), F32)] * 4,
        compiler_params=_params(dimension_semantics=("arbitrary",)),
    )(chip, hb, got, w, m, v)


def _small_all_reduce_adam(gs, w, m, v):
    def body(g_ref, w_ref, m_ref, v_ref, sum_ref, d_ref, nm_ref, nv_ref, gather, send_sems, recv_sems):
        x, y, c = _place()
        my = 4 * x + 2 * y + c
        gather[my] = g_ref[...]
        cps = []
        for k in range(1, N_DEV):
            to = (x ^ (k >> 2), y ^ ((k >> 1) & 1), c ^ (k & 1))
            cps.append(pltpu.make_async_remote_copy(
                src_ref=g_ref, dst_ref=gather.at[my], send_sem=send_sems.at[k - 1], recv_sem=recv_sems.at[k - 1],
                device_id=to, device_id_type=MESH))
        for cp in cps:
            cp.start()
        for cp in cps:
            cp.wait()
        total = gather[0]
        for d in range(1, N_DEV):
            total = total + gather[d]
        dlt, nm, nv = _adam_math(w_ref[...], total, m_ref[...], v_ref[...])
        sum_ref[...] = total
        d_ref[...] = dlt
        nm_ref[...] = nm
        nv_ref[...] = nv

    vm = pl.BlockSpec(memory_space=pltpu.VMEM)
    return _pallas(
        body, name="small_all_reduce_adam", out_shape=[jax.ShapeDtypeStruct((SMALL_TOTAL, LANES), F32)] * 4,
        in_specs=[vm] * 4, out_specs=[vm] * 4,
        scratch_shapes=[pltpu.VMEM((N_DEV, SMALL_TOTAL, LANES), F32), pltpu.SemaphoreType.DMA((7,)),
                        pltpu.SemaphoreType.DMA((7,))],
            )(gs, w, m, v)


def kernel(x, p, norm_mix_g, w_in, hg_lb_logits, hg_onorm_g, fox_f_bias, fox_q_norm_g, fox_k_norm_g, w_branch_a, w_branch_b, w_out, norm_ffn_g, w_ffn_gate, w_ffn_up, w_ffn_down, norm_ple_g, w_ple_gate, w_ple_proj, loss_target, m_norm_mix_g, m_w_in, m_hg_lb_logits, m_hg_onorm_g, m_fox_f_bias, m_fox_q_norm_g, m_fox_k_norm_g, m_w_branch_a, m_w_branch_b, m_w_out, m_norm_ffn_g, m_w_ffn_gate, m_w_ffn_up, m_w_ffn_down, m_norm_ple_g, m_w_ple_gate, m_w_ple_proj, v_norm_mix_g, v_w_in, v_hg_lb_logits, v_hg_onorm_g, v_fox_f_bias, v_fox_q_norm_g, v_fox_k_norm_g, v_w_branch_a, v_w_branch_b, v_w_out, v_norm_ffn_g, v_w_ffn_gate, v_w_ffn_up, v_w_ffn_down, v_norm_ple_g, v_w_ple_gate, v_w_ple_proj):
    args = dict(locals())
    wts = {n: args[n] for n in BIG + SMALL}
    mom = {n: args["m_" + n] for n in BIG + SMALL}
    var = {n: args["v_" + n] for n in BIG + SMALL}
    for group in (wts, mom, var):
        for n in TRANSPOSED:
            group[n] = jnp.swapaxes(group[n], 1, 2)
    sm = {n: wts[n] for n in SMALL}

    xi, yi, ci = _place()
    core = jnp.reshape(ci, (1,)).astype(jnp.int32)
    chip = jnp.reshape(2 * xi + yi, (1,)).astype(jnp.int32)
    chunks = [_chunk_of_shard(n, wts[n][0].astype(BF16)) for n in BIG]
    loss_blk, grad_x, gs, hb, got = _local_step(x[0], p[0, 0], loss_target[0], sm, chunks, core)

    g_big, d_big, nm_big, nv_big = {}, {}, {}, {}
    for n, h, r in zip(BIG, hb, got):
        res = _adam_shard(h, r, chip, wts[n], mom[n], var[n], name="adam_" + n)
        if n in TRANSPOSED:
            res = [jnp.swapaxes(t, 1, 2) for t in res]
        g_big[n], d_big[n], nm_big[n], nv_big[n] = res

    s_sum, s_d, s_nm, s_nv = _small_all_reduce_adam(
        _pack_small(gs, loss_blk[0:1]), _pack_small(sm), _pack_small({n: mom[n] for n in SMALL}),
        _pack_small({n: var[n] for n in SMALL}))
    loss = s_sum[LOSS_ROW, 0]
    g_small, d_small, nm_small, nv_small = (_unpack_small(t, sm) for t in (s_sum, s_d, s_nm, s_nv))

    order = ["norm_mix_g", "w_in", "hg_lb_logits", "hg_onorm_g", "fox_f_bias", "fox_q_norm_g", "fox_k_norm_g",
             "w_branch_a", "w_branch_b", "w_out", "norm_ffn_g", "w_ffn_gate", "w_ffn_up", "w_ffn_down", "norm_ple_g",
             "w_ple_gate", "w_ple_proj"]
    outs = [loss, grad_x[None]]
    for big, small in ((g_big, g_small), (d_big, d_small), (nm_big, nm_small), (nv_big, nv_small)):
        outs += [big[n] if n in big else small[n] for n in order]
    return tuple(outs)
```

```python
import functools

import jax
import jax.numpy as jnp
from jax import lax
from jax.experimental import pallas as pl
from jax.experimental.pallas import tpu as pltpu

F32 = jnp.float32
BF16 = jnp.bfloat16

D_MODEL = 1024
PLE_DIM = 256
HG_HEADS = 4
HG_DK = 128
HG_CHUNK = 64
HG_SUB = 16
HG_W = HG_HEADS * HG_DK
FOX_HEADS = 8
FOX_DH = 64
FOX_W = FOX_HEADS * FOX_DH
D_FF = 2816
EPS = 1e-6
N_DEV = 8
N_CHIP = 4
LANES = 128
FOX_COLS = 3 * FOX_W + LANES
HG_COLS = 4 * HG_W
GATE_COLS = 2 * D_MODEL
IN_COLS = HG_COLS + 3 * FOX_W + FOX_HEADS + GATE_COLS
FOX_LOGICAL = 3 * FOX_W + FOX_HEADS
SEG = 2048
IN_PAD = 3 * SEG
IN_SHARD = IN_COLS // N_DEV
IN_SHARD_PAD = 720
EXP_CLAMP = 80.0
LOG2E = 1.4426950408889634

ADAM_LR = 0.001
ADAM_B1 = 0.9
ADAM_B2 = 0.999
ADAM_EPS = 1e-08
ADAM_WD = 0.01
ADAM_STEP = 10

MESH = pl.DeviceIdType.MESH
VMEM_LIMIT = 56 * 1024 * 1024
ROW_BLOCK = 512
MATMUL_TN = 2048
MATMUL_TK = 3072
MATMUL_TM_T = 1408
FFN_TILE = 1408

BIG = ["w_in", "w_branch_a", "w_branch_b", "w_out", "w_ffn_gate", "w_ffn_up", "w_ffn_down",
       "w_ple_gate", "w_ple_proj"]
TRANSPOSED = ("w_in", "w_ffn_gate", "w_ffn_up")
BIG_SHAPE = {
    "w_in": (IN_COLS, D_MODEL, 0), "w_branch_a": (HG_W, D_MODEL, 1), "w_branch_b": (FOX_W, D_MODEL, 1),
    "w_out": (D_MODEL, D_MODEL, 0), "w_ffn_gate": (D_FF, D_MODEL, 0), "w_ffn_up": (D_FF, D_MODEL, 0),
    "w_ffn_down": (D_FF, D_MODEL, 0), "w_ple_gate": (D_MODEL, D_MODEL, 0), "w_ple_proj": (PLE_DIM, D_MODEL, 1),
}

SMALL = ["norm_mix_g", "hg_lb_logits", "hg_onorm_g", "fox_f_bias", "fox_q_norm_g", "fox_k_norm_g",
         "norm_ffn_g", "norm_ple_g"]
SMALL_ROWS = {"norm_mix_g": 8, "hg_lb_logits": 8, "hg_onorm_g": 1, "fox_f_bias": 1, "fox_q_norm_g": 1,
              "fox_k_norm_g": 1, "norm_ffn_g": 8, "norm_ple_g": 8}
SMALL_TOTAL = 40
LOSS_ROW = 36


def _pallas(body, **kw):
    return pl.pallas_call(body, **kw)


def _params(**kw):
    return pltpu.CompilerParams(vmem_limit_bytes=VMEM_LIMIT, **kw)


def _pick(n, target):
    if n <= target:
        return n
    best = None
    for t in range(LANES, target + 1, LANES):
        if n % t == 0:
            best = t
    assert best is not None, (n, target)
    return best


def _dot(a, b, ca, cb):
    return lax.dot_general(a, b, (((ca,), (cb,)), ((), ())), preferred_element_type=F32)


def _split_dot(mat, x, ca, cb, terms=2, mat_first=True):
    acc = None
    rem = x
    for _ in range(terms):
        part = rem.astype(BF16)
        rem = rem - part.astype(F32)
        p = _dot(mat, part, ca, cb) if mat_first else _dot(part, mat, ca, cb)
        acc = p if acc is None else acc + p
    return acc


def _sigmoid(x):
    return 1.0 / (1.0 + jnp.exp(-x))


def _iota(shape, dim):
    return lax.broadcasted_iota(jnp.int32, shape, dim)


def _matmul(a, b, *, name, ta=False, tb=False, out_dtype=F32, add=None, exchange=None, gather=None,
            norm_fwd=None, norm_bwd=None):
    assert exchange is None or gather is None
    (K, M) = a.shape if ta else a.shape[::-1]
    (N, Kb) = b.shape if tb else b.shape[::-1]
    assert K == Kb, (a.shape, b.shape, ta, tb)
    if ta:
        tm, tn, tk = _pick(M, MATMUL_TM_T), _pick(N, 2 * ROW_BLOCK), _pick(K, 4 * ROW_BLOCK)
    else:
        tm, tn, tk = _pick(M, 2 * ROW_BLOCK), _pick(N, MATMUL_TN), _pick(K, MATMUL_TK)
    if norm_bwd is not None:
        tm = _pick(M, ROW_BLOCK)
    nk = K // tk
    use_scratch = nk > 1 and out_dtype != F32
    if norm_fwd is not None or norm_bwd is not None:
        assert tn == N and not use_scratch and out_dtype == F32

    hs = list(exchange or gather or [])
    n_x = len(hs)
    grid = (M // tm, N // tn, nk)
    a_spec = pl.BlockSpec((tk, tm), lambda i, j, k: (k, i)) if ta else pl.BlockSpec((tm, tk), lambda i, j, k: (i, k))
    b_spec = pl.BlockSpec((tn, tk), lambda i, j, k: (j, k)) if tb else pl.BlockSpec((tk, tn), lambda i, j, k: (k, j))
    o_spec = pl.BlockSpec((tm, tn), lambda i, j, k: (i, j))
    row_vec = pl.BlockSpec((1, N), lambda i, j, k: (0, 0))
    hbm = pl.BlockSpec(memory_space=pl.ANY)
    extra_in = [(add, o_spec)] if add is not None else []
    extra_out = []
    if norm_fwd is not None:
        extra_in += [(norm_fwd, row_vec)]
        extra_out += [(jax.ShapeDtypeStruct((M, N), BF16), o_spec)]
    if norm_bwd is not None:
        extra_in += [(norm_bwd[0], o_spec), (norm_bwd[1], row_vec), (norm_bwd[2], o_spec)]
        extra_out += [(jax.ShapeDtypeStruct((1, N), F32), row_vec)]
    if gather is not None:
        ride_shapes, ride_scratch = [jax.ShapeDtypeStruct((N_DEV,) + h.shape, h.dtype) for h in hs], _gather_scratch(n_x)
    else:
        ride_shapes, ride_scratch = _chip_exchange_shapes(hs), (_chip_exchange_scratch(n_x) if n_x else [])
    n_ex_in, n_ex_out = len(extra_in), len(extra_out)

    def body(*refs):
        refs = list(refs)
        a_ref, b_ref = refs[:2]
        ex_in = refs[2:2 + n_ex_in]
        ride_in = refs[2 + n_ex_in:2 + n_ex_in + n_x]
        base = 2 + n_ex_in + n_x
        o_ref = refs[base]
        ex_out = refs[base + 1:base + 1 + n_ex_out]
        ride_out = refs[base + 1 + n_ex_out:base + 1 + n_ex_out + n_x]
        scratch = refs[base + 1 + n_ex_out + n_x:]
        at = [pl.program_id(d) for d in range(3)]
        k = at[2]
        if n_x:
            steps = _gather_steps if gather is not None else _chip_exchange_steps
            ride = steps(ride_in, ride_out, *scratch[-len(ride_scratch):])

            @pl.when(jnp.logical_and(at[0] == 0, jnp.logical_and(at[1] == 0, at[2] == 0)))
            def _():
                ride[0]()

            early_forward = len(ride) == 3 and grid[0] > 1
            if early_forward:
                @pl.when(jnp.logical_and(at[0] == grid[0] - 1, jnp.logical_and(at[1] == 0, at[2] == 0)))
                def _():
                    ride[1]()
        p = _dot(a_ref[...].astype(BF16), b_ref[...].astype(BF16), 0 if ta else 1, 1 if tb else 0)

        def finish(r):
            ins = list(ex_in)
            outs = list(ex_out)
            if add is not None:
                r = r + ins.pop(0)[...].astype(F32)
            if norm_fwd is not None:
                g_ref = ins.pop(0)
                rstd = lax.rsqrt(jnp.mean(r * r, axis=-1, keepdims=True) + EPS)
                outs.pop(0)[...] = (r * rstd * g_ref[...]).astype(BF16)
            if norm_bwd is not None:
                x_ref, g_ref, dres_ref = ins.pop(0), ins.pop(0), ins.pop(0)
                dg_ref = outs.pop(0)
                xv = x_ref[...]
                rstd = lax.rsqrt(jnp.mean(xv * xv, axis=-1, keepdims=True) + EPS)
                xh = xv * rstd
                part = jnp.sum(r * xh, axis=0, keepdims=True)

                @pl.when(at[0] == 0)
                def _():
                    dg_ref[...] = part

                @pl.when(at[0] > 0)
                def _():
                    dg_ref[...] += part

                dxh = r * g_ref[...]
                r = rstd * (dxh - xh * jnp.mean(dxh * xh, axis=-1, keepdims=True)) + dres_ref[...]
            o_ref[...] = r.astype(out_dtype)

        if nk == 1:
            finish(p)
        elif not use_scratch:
            @pl.when(k == 0)
            def _():
                o_ref[...] = p

            @pl.when(jnp.logical_and(k > 0, k < nk - 1))
            def _():
                o_ref[...] += p

            @pl.when(k == nk - 1)
            def _():
                finish(o_ref[...] + p)
        else:
            acc_ref = scratch[0]

            @pl.when(k == 0)
            def _():
                acc_ref[...] = p

            @pl.when(k > 0)
            def _():
                acc_ref[...] += p

            @pl.when(k == nk - 1)
            def _():
                finish(acc_ref[...])

        if n_x:
            @pl.when(jnp.logical_and(at[0] == grid[0] - 1, jnp.logical_and(at[1] == grid[1] - 1, at[2] == nk - 1)))
            def _():
                for step in ride[2 if early_forward else 1:]:
                    step()

    res = _pallas(
        body, name=name, grid=grid,
        in_specs=[a_spec, b_spec] + [s for _, s in extra_in] + [hbm] * n_x,
        out_specs=[o_spec] + [s for _, s in extra_out] + [hbm] * n_x,
        out_shape=[jax.ShapeDtypeStruct((M, N), out_dtype)] + [s for s, _ in extra_out] + ride_shapes,
        scratch_shapes=([pltpu.VMEM((tm, tn), F32)] if use_scratch else []) + ride_scratch,
        compiler_params=_params(dimension_semantics=("arbitrary",) * 3),
    )(a, b, *[v for v, _ in extra_in], *hs)
    res = list(res)
    main = res[0] if n_ex_out == 0 else tuple(res[:1 + n_ex_out])
    return (main, res[1 + n_ex_out:]) if n_x else main


def _row_map(nb, reverse, seg):
    if reverse:
        return lambda i: (nb - 1 - i, seg)
    return lambda i: (i, seg)


def _row_call(body, *, name, T, ins, outs, acc_outs=(), tm=ROW_BLOCK, reverse=False, swap=None):
    tm = min(tm, T)
    nb = T // tm
    in_specs, args = [], []
    for arr, how in ins:
        args.append(arr)
        if how is True:
            in_specs.append(pl.BlockSpec((tm, arr.shape[1]), _row_map(nb, reverse, 0)))
        elif how is False:
            in_specs.append(pl.BlockSpec(arr.shape, lambda i, _n=arr.ndim: (0,) * _n))
        else:
            in_specs.append(pl.BlockSpec((tm, SEG), _row_map(nb, reverse, how[0])))
    out_specs, out_shape = [], []
    for o in outs:
        c, dt = o[0], o[1]
        total, seg = o[2] if len(o) > 2 else (c, 0)
        out_specs.append(pl.BlockSpec((tm, c), _row_map(nb, reverse, seg)))
        out_shape.append(jax.ShapeDtypeStruct((T, total), dt))
    for shp, dt in acc_outs:
        out_specs.append(pl.BlockSpec(shp, lambda i, _n=len(shp): (0,) * _n))
        out_shape.append(jax.ShapeDtypeStruct(shp, dt))
    if swap is None:
        return _pallas(body, name=name, grid=(nb,), in_specs=in_specs, out_specs=out_specs, out_shape=out_shape,
                       compiler_params=_params(dimension_semantics=("arbitrary",)))(*args)
    n, n_in, n_out = len(swap), len(in_specs), len(out_specs)
    hbm = pl.BlockSpec(memory_space=pl.ANY)

    def riding(*refs):
        mine = refs[:n_in] + refs[n_in + n:n_in + n + n_out]
        start, finish = _sibling_steps(refs[n_in:n_in + n], refs[n_in + n + n_out:n_in + 2 * n + n_out], *refs[-2:])

        @pl.when(pl.program_id(0) == 0)
        def _():
            start()

        body(*mine)

        @pl.when(pl.program_id(0) == nb - 1)
        def _():
            finish()

    res = list(_pallas(
        riding, name=name, grid=(nb,), in_specs=in_specs + [hbm] * n, out_specs=out_specs + [hbm] * n,
        out_shape=out_shape + _sibling_shapes(swap), scratch_shapes=_sibling_scratch(n),
        compiler_params=_params(dimension_semantics=("arbitrary",)))(*args, *swap))
    return res[:n_out], res[n_out:]


def _rms_fwd(x, g, blocks, *, name):
    T, D = x.shape
    tm = min(ROW_BLOCK, T)
    nb = T // tm
    n = len(blocks)

    def body(*refs):
        x_ref, g_ref = refs[:2]
        h_ref = refs[2 + n]
        g_start, g_forward, g_finish = _gather_steps(refs[2:2 + n], refs[3 + n:3 + 2 * n], *refs[3 + 2 * n:])

        @pl.when(pl.program_id(0) == 0)
        def _():
            g_start()

        xv = x_ref[...]
        rstd = lax.rsqrt(jnp.mean(xv * xv, axis=-1, keepdims=True) + EPS)
        h_ref[...] = (xv * rstd * g_ref[...]).astype(BF16)

        @pl.when(pl.program_id(0) == nb - 1)
        def _():
            g_forward()
            g_finish()

    hbm = pl.BlockSpec(memory_space=pl.ANY)
    res = _pallas(
        body, name=name, grid=(nb,),
        in_specs=[pl.BlockSpec((tm, D), lambda i: (i, 0)), pl.BlockSpec((1, D), lambda i: (0, 0))] + [hbm] * n,
        out_specs=[pl.BlockSpec((tm, D), lambda i: (i, 0))] + [hbm] * n,
        out_shape=[jax.ShapeDtypeStruct((T, D), BF16)] + [jax.ShapeDtypeStruct((N_DEV,) + b.shape, b.dtype) for b in blocks],
        scratch_shapes=_gather_scratch(n),
        compiler_params=_params(dimension_semantics=("arbitrary",)),
    )(x, g, *blocks)
    return res[0], list(res[1:])


def _merge_fwd(ya, yb, wa, wb, zg):
    def body(ya_ref, yb_ref, wa_ref, wb_ref, zg_ref, m_ref, ua_ref, ub_ref):
        ua = _dot(ya_ref[...].astype(BF16), wa_ref[...], 1, 0)
        ub = _dot(yb_ref[...].astype(BF16), wb_ref[...], 1, 0)
        ga = _sigmoid(zg_ref[:, :D_MODEL])
        gb = _sigmoid(zg_ref[:, D_MODEL:])
        m_ref[...] = (ga * ua + gb * ub).astype(BF16)
        ua_ref[...] = ua.astype(BF16)
        ub_ref[...] = ub.astype(BF16)

    return _row_call(body, name="merge_fwd", T=ya.shape[0],
                     ins=[(ya, True), (yb, True), (wa, False), (wb, False), (zg, (2,))],
                     outs=[(D_MODEL, BF16)] * 3)


def _merge_bwd(dx1, w_out, wa, wb, ua, ub, zg, swap):
    def body(dx_ref, wo_ref, wa_ref, wb_ref, ua_ref, ub_ref, zg_ref, dua_ref, dub_ref, dzg_ref, dya_ref, dyb_ref):
        dmv = _dot(dx_ref[...].astype(BF16), wo_ref[...], 1, 1)
        ga = _sigmoid(zg_ref[:, :D_MODEL])
        gb = _sigmoid(zg_ref[:, D_MODEL:])
        dua = (dmv * ga).astype(BF16)
        dub = (dmv * gb).astype(BF16)
        dua_ref[...] = dua
        dub_ref[...] = dub
        dzg_ref[:, :D_MODEL] = (dmv * ua_ref[...].astype(F32) * ga * (1.0 - ga)).astype(BF16)
        dzg_ref[:, D_MODEL:] = (dmv * ub_ref[...].astype(F32) * gb * (1.0 - gb)).astype(BF16)
        dya_ref[...] = _dot(dua, wa_ref[...], 1, 1)
        dyb_ref[...] = _dot(dub, wb_ref[...], 1, 1)

    return _row_call(body, name="merge_bwd", T=dx1.shape[0],
                     ins=[(dx1, True), (w_out, False), (wa, False), (wb, False), (ua, True), (ub, True), (zg, (2,))],
                     outs=[(D_MODEL, BF16), (D_MODEL, BF16), (SEG, BF16, (IN_PAD, 2)), (HG_W, F32), (FOX_W, F32)],
                     swap=swap)


def _swiglu_fwd(hf, w_gate, w_up):
    T, D = hf.shape
    F = w_gate.shape[0]
    tm, tn = _pick(T, ROW_BLOCK), _pick(F, FFN_TILE)

    def body(h_ref, wg_ref, wu_ref, a_ref, b_ref, o_ref):
        hv = h_ref[...]
        a_b = _dot(hv, wg_ref[...], 1, 1).astype(BF16)
        b_b = _dot(hv, wu_ref[...], 1, 1).astype(BF16)
        a_ref[...] = a_b
        b_ref[...] = b_b
        av = a_b.astype(F32)
        o_ref[...] = (av * _sigmoid(av) * b_b.astype(F32)).astype(BF16)

    tile = pl.BlockSpec((tm, tn), lambda j, i: (i, j))
    wcol = pl.BlockSpec((tn, D), lambda j, i: (j, 0))
    return _pallas(
        body, name="swiglu_fwd", grid=(F // tn, T // tm),
        in_specs=[pl.BlockSpec((tm, D), lambda j, i: (i, 0)), wcol, wcol],
        out_specs=[tile] * 3, out_shape=[jax.ShapeDtypeStruct((T, F), BF16)] * 3,
        compiler_params=_params(dimension_semantics=("arbitrary",) * 2),
    )(hf, w_gate, w_up)


def _swiglu_bwd(dx, w_down, a, b):
    T, D = dx.shape
    F = w_down.shape[0]
    tm, tn = _pick(T, ROW_BLOCK), _pick(F, FFN_TILE)

    def body(dx_ref, w_ref, a_ref, b_ref, da_ref, db_ref):
        dact = _dot(dx_ref[...].astype(BF16), w_ref[...], 1, 1)
        av = a_ref[...].astype(F32)
        bv = b_ref[...].astype(F32)
        sg = _sigmoid(av)
        da_ref[...] = (dact * bv * sg * (1.0 + av * (1.0 - sg))).astype(BF16)
        db_ref[...] = (dact * av * sg).astype(BF16)

    tile = pl.BlockSpec((tm, tn), lambda j, i: (i, j))
    return _pallas(
        body, name="swiglu_bwd", grid=(F // tn, T // tm),
        in_specs=[pl.BlockSpec((tm, D), lambda j, i: (i, 0)), pl.BlockSpec((tn, D), lambda j, i: (j, 0)), tile, tile],
        out_specs=[tile, tile], out_shape=[jax.ShapeDtypeStruct((T, F), BF16)] * 2,
        compiler_params=_params(dimension_semantics=("arbitrary",) * 2),
    )(dx, w_down, a, b)


def _ple_loss(x2, p, g, w_gate, w_proj, tgt):
    def body(x_ref, p_ref, g_ref, wg_ref, wp_ref, t_ref, hp_ref, dy_ref, dsp_ref, dpp_ref, loss_ref):
        xv = x_ref[...]
        rstd = lax.rsqrt(jnp.mean(xv * xv, axis=-1, keepdims=True) + EPS)
        hp = (xv * rstd * g_ref[...]).astype(BF16)
        hp_ref[...] = hp
        gp = _sigmoid(_dot(hp, wg_ref[...], 1, 0))
        ppv = _dot(p_ref[...].astype(BF16), wp_ref[...], 1, 0)
        err = xv + gp * ppv - t_ref[...]
        part = 0.5 * jnp.sum(jnp.mean(err * err, axis=-1, keepdims=True), axis=0, keepdims=True)
        part = jnp.broadcast_to(part, loss_ref.shape)

        @pl.when(pl.program_id(0) == 0)
        def _():
            loss_ref[...] = part

        @pl.when(pl.program_id(0) > 0)
        def _():
            loss_ref[...] += part

        dy = err * (1.0 / D_MODEL)
        dy_ref[...] = dy
        dsp_ref[...] = (dy * ppv * gp * (1.0 - gp)).astype(BF16)
        dpp_ref[...] = (dy * gp).astype(BF16)

    return _row_call(body, name="ple_loss", T=x2.shape[0],
                     ins=[(x2, True), (p, True), (g, False), (w_gate, False), (w_proj, False), (tgt, True)],
                     outs=[(D_MODEL, BF16), (D_MODEL, F32), (D_MODEL, BF16), (D_MODEL, BF16)],
                     acc_outs=[((8, LANES), F32)])


def _hg_tri():
    C = HG_CHUNK
    return _iota((C, C), 1) <= _iota((C, C), 0)


def _hg_ref_row(b, blk):
    mid = blk * HG_SUB + HG_SUB // 2 - 1
    return b[mid:mid + 1]


def _hg_chunk_fwd(q, f, lb, tri_b):
    sgq = _sigmoid(q)
    qt = q * sgq
    sg = _sigmoid(f)
    fg = lb + (1.0 - lb) * sg
    kf = (1.0 - lb) * (1.0 - sg)
    logf = jnp.log(fg)
    b = _split_dot(tri_b, logf, 1, 0)
    refs = [jnp.broadcast_to(_hg_ref_row(b, blk), (HG_SUB, b.shape[1])) for blk in range(HG_CHUNK // HG_SUB)]
    w = jnp.minimum(b - jnp.concatenate(refs, axis=0), EXP_CLAMP)
    return sgq, qt, sg, fg, kf, b, w


def _hg_scores(qs_b, kf, b, row):
    C, S = HG_CHUNK, HG_SUB
    parts, ks = [], []
    for blk in range(C // S):
        e = jnp.exp(jnp.minimum(_hg_ref_row(b, blk) - b, EXP_CLAMP))
        e = jnp.where(row < (blk + 1) * S, e, 0.0)
        k_b = (kf * e).astype(BF16)
        ks.append((e, k_b))
        parts.append(_dot(qs_b[blk * S:(blk + 1) * S], k_b, 1, 1))
    return jnp.concatenate(parts, axis=0), ks


def _hgrn_fwd(z, lb_logits, gain, blocks):
    T = z.shape[0]
    RB = min(ROW_BLOCK, T)
    nb, cpb = T // RB, RB // HG_CHUNK
    C, DK = HG_CHUNK, HG_DK
    n = len(blocks)

    def body(*refs):
        z_ref, lg_ref, g_ref = refs[:3]
        o_ref, y_ref, st_ref = refs[3 + n:6 + n]
        s_ref = refs[6 + 2 * n]
        g_start, g_forward, g_finish = _gather_steps(refs[3:3 + n], refs[6 + n:6 + 2 * n], *refs[7 + 2 * n:])

        @pl.when(pl.program_id(0) == 0)
        def _():
            s_ref[...] = jnp.zeros_like(s_ref)
            g_start()

        lg = lg_ref[...]
        lb_all = 1.0 / (1.0 + jnp.exp(lg[1:2] - lg[0:1]))
        gain_v = g_ref[...]
        tri = _hg_tri()
        tri_b = tri.astype(BF16)
        row = _iota((C, DK), 0)

        def chunk(ci, carry):
            r0 = pl.multiple_of(ci * C, C)
            rows = pl.ds(r0, C)
            _, qt, _, _, kf_all, b_all, w = _hg_chunk_fwd(z_ref[rows, 0:HG_W], z_ref[rows, HG_W:2 * HG_W], lb_all,
                                                          tri_b)
            qs_all = (qt * jnp.exp(w)).astype(BF16)
            qd_all = (qt * jnp.exp(b_all)).astype(BF16)
            bl_all = b_all[C - 1:C]
            kd_all = (kf_all * jnp.exp(bl_all - b_all)).astype(BF16)
            ebl_all = jnp.exp(bl_all)
            v_all = z_ref[rows, 2 * HG_W:3 * HG_W].astype(BF16)
            g_all = z_ref[rows, 3 * HG_W:4 * HG_W]
            gate_all = g_all * _sigmoid(g_all)
            for h in range(HG_HEADS):
                cs = slice(h * DK, (h + 1) * DK)
                st = s_ref[h]
                st_ref[pl.ds(pl.multiple_of((ci * HG_HEADS + h) * DK, DK), DK), :] = st
                v_b = v_all[:, cs]
                a, _ = _hg_scores(qs_all[:, cs], kf_all[:, cs], b_all[:, cs], row)
                a = jnp.where(tri, a, 0.0)
                o = _dot(qd_all[:, cs], st.astype(BF16), 1, 1) + _dot(a.astype(BF16), v_b, 1, 0)
                s_ref[h] = st * ebl_all[:, cs] + _dot(v_b, kd_all[:, cs], 0, 0)
                o_ref[rows, cs] = o
                rstd = lax.rsqrt(jnp.mean(o * o, axis=-1, keepdims=True) + EPS)
                y_ref[rows, cs] = (o * rstd * gain_v * gate_all[:, cs]).astype(BF16)
            return carry

        lax.fori_loop(0, cpb, chunk, 0, unroll=4)

        @pl.when(pl.program_id(0) == max(nb - 2, 0))
        def _():
            g_forward()

        @pl.when(pl.program_id(0) == nb - 1)
        def _():
            g_finish()

    hbm = pl.BlockSpec(memory_space=pl.ANY)
    res = _pallas(
        body, name="hgrn_fwd", grid=(nb,),
        in_specs=[pl.BlockSpec((RB, HG_COLS), lambda i: (i, 0)), pl.BlockSpec((2, HG_W), lambda i: (0, 0)),
                  pl.BlockSpec((1, DK), lambda i: (0, 0))] + [hbm] * n,
        out_specs=[pl.BlockSpec((RB, HG_W), lambda i: (i, 0)), pl.BlockSpec((RB, HG_W), lambda i: (i, 0)),
                   pl.BlockSpec((cpb * HG_HEADS * DK, DK), lambda i: (i, 0))] + [hbm] * n,
        out_shape=[jax.ShapeDtypeStruct((T, HG_W), F32), jax.ShapeDtypeStruct((T, HG_W), BF16),
                   jax.ShapeDtypeStruct((T // C * HG_HEADS * DK, DK), F32)]
        + [jax.ShapeDtypeStruct((N_DEV,) + b.shape, b.dtype) for b in blocks],
        scratch_shapes=[pltpu.VMEM((HG_HEADS, DK, DK), F32)] + _gather_scratch(n),
        compiler_params=_params(dimension_semantics=("arbitrary",)),
    )(z, lb_logits, gain, *blocks)
    return res[0], res[1], res[2], res[3:]


def _hgrn_bwd(z, o_raw, dy, states, lb_logits, gain, dz_buf):
    T = z.shape[0]
    RB = min(ROW_BLOCK, T)
    nb, cpb = T // RB, RB // HG_CHUNK
    C, DK, S = HG_CHUNK, HG_DK, HG_SUB

    def body(z_ref, o_ref, dy_ref, st_ref, lg_ref, g_ref, _buf_ref, dz_ref, dlg_ref, dg_ref, ds_ref, dlb_ref):
        step = pl.program_id(0)

        @pl.when(step == 0)
        def _():
            ds_ref[...] = jnp.zeros_like(ds_ref)
            dlb_ref[...] = jnp.zeros_like(dlb_ref)
            dg_ref[...] = jnp.zeros_like(dg_ref)

        lg = lg_ref[...]
        lb_all = 1.0 / (1.0 + jnp.exp(lg[1:2] - lg[0:1]))
        gain_v = g_ref[...]
        tri = _hg_tri()
        tri_b = tri.astype(BF16)
        row = _iota((C, DK), 0)

        def chunk(cj, carry):
            ci = cpb - 1 - cj
            r0 = pl.multiple_of(ci * C, C)
            rows = pl.ds(r0, C)
            q_all = z_ref[rows, 0:HG_W]
            g_all = z_ref[rows, 3 * HG_W:4 * HG_W]
            sgq_all, qt_all, sg_all, fg_all, kf_all, b_all, w_all = _hg_chunk_fwd(
                q_all, z_ref[rows, HG_W:2 * HG_W], lb_all, tri_b)
            ew_all = jnp.exp(w_all)
            eb_all = jnp.exp(b_all)
            bl_all = b_all[C - 1:C]
            ebl_all = jnp.exp(bl_all)
            ekd_all = jnp.exp(bl_all - b_all)
            qs_all = (qt_all * ew_all).astype(BF16)
            qd_all = (qt_all * eb_all).astype(BF16)
            kd_all = (kf_all * ekd_all).astype(BF16)
            v_all = z_ref[rows, 2 * HG_W:3 * HG_W].astype(BF16)
            sgg_all = _sigmoid(g_all)
            t1_all = dy_ref[rows, :] * (g_all * sgg_all)
            db_heads, dqt_heads, dkf_heads, dv_heads, n_heads = [], [], [], [], []
            for h in range(HG_HEADS):
                cs = slice(h * DK, (h + 1) * DK)
                kf, b, ew, eb, ebl, ekd = kf_all[:, cs], b_all[:, cs], ew_all[:, cs], eb_all[:, cs], ebl_all[:, cs], \
                    ekd_all[:, cs]
                qs_b, qd_b, kd_b, v_b = qs_all[:, cs], qd_all[:, cs], kd_all[:, cs], v_all[:, cs]
                st = st_ref[pl.ds(pl.multiple_of((ci * HG_HEADS + h) * DK, DK), DK), :]
                dst = ds_ref[h]
                o = o_ref[rows, cs]
                rstd = lax.rsqrt(jnp.mean(o * o, axis=-1, keepdims=True) + EPS)
                n = o * rstd
                n_heads.append(n)
                t1 = t1_all[:, cs]
                dg_ref[...] += jnp.sum(t1 * n, axis=0, keepdims=True)
                dn = t1 * gain_v
                do = rstd * (dn - n * jnp.mean(dn * n, axis=-1, keepdims=True))
                do_b = do.astype(BF16)
                a, ks = _hg_scores(qs_b, kf, b, row)
                a = jnp.where(tri, a, 0.0)
                dst_b = dst.astype(BF16)
                dqd = _dot(do_b, st.astype(BF16), 1, 0)
                da = jnp.where(tri, _dot(do_b, v_b, 1, 1), 0.0)
                dv = _dot(a.astype(BF16), do_b, 0, 0) + _dot(kd_b, dst_b, 1, 1)
                dkd = _dot(v_b, dst_b, 1, 0)
                ds_ref[h] = dst * ebl + _dot(do_b, qd_b, 0, 0)
                dkd_kd = dkd * kd_b.astype(F32)
                dbl = ebl * jnp.sum(dst * st, axis=0, keepdims=True) + jnp.sum(dkd_kd, axis=0, keepdims=True)
                da_b = da.astype(BF16)
                dqs_parts = []
                dk_in = jnp.zeros((C, DK), F32)
                db_k = jnp.zeros((C, DK), F32)
                for blk in range(C // S):
                    e, k_b = ks[blk]
                    da_blk = da_b[blk * S:(blk + 1) * S]
                    dqs_parts.append(_dot(da_blk, k_b, 1, 0))
                    dks = _dot(da_blk, qs_b[blk * S:(blk + 1) * S], 0, 0)
                    dk_in = dk_in + dks * e
                    db_k = db_k + dks * k_b.astype(F32)
                dqs = jnp.concatenate(dqs_parts, axis=0)
                db = qs_b.astype(F32) * dqs - db_k + dqd * qd_b.astype(F32) - dkd_kd
                db_heads.append(db + jnp.where(row == C - 1, dbl, 0.0))
                dqt_heads.append(dqs * ew + dqd * eb)
                dkf_heads.append(dk_in + dkd * ekd)
                dv_heads.append(dv)
            dlogf = _split_dot(tri_b, jnp.concatenate(db_heads, axis=1), 0, 0)
            dfg = dlogf / fg_all - jnp.concatenate(dkf_heads, axis=1)
            dlb_ref[...] += jnp.sum(dfg * (1.0 - sg_all), axis=0, keepdims=True)
            dqt = jnp.concatenate(dqt_heads, axis=1)
            n_all = jnp.concatenate(n_heads, axis=1)
            gain_all = jnp.concatenate([gain_v] * HG_HEADS, axis=1)
            dz_ref[rows, 0:HG_W] = (dqt * sgq_all * (1.0 + q_all * (1.0 - sgq_all))).astype(BF16)
            dz_ref[rows, HG_W:2 * HG_W] = (dfg * (1.0 - lb_all) * sg_all * (1.0 - sg_all)).astype(BF16)
            dz_ref[rows, 2 * HG_W:3 * HG_W] = jnp.concatenate(dv_heads, axis=1).astype(BF16)
            dz_ref[rows, 3 * HG_W:4 * HG_W] = (dy_ref[rows, :] * n_all * gain_all * sgg_all
                                               * (1.0 + g_all * (1.0 - sgg_all))).astype(BF16)
            return carry

        lax.fori_loop(0, cpb, chunk, 0, unroll=4)

        @pl.when(step == nb - 1)
        def _():
            d0 = dlb_ref[...] * lb_all * (1.0 - lb_all)
            dlg_ref[0:1, :] = d0
            dlg_ref[1:2, :] = -d0

    rev = lambda i: (nb - 1 - i, 0)
    fix = lambda i: (0, 0)
    return _pallas(
        body, name="hgrn_bwd", grid=(nb,),
        in_specs=[pl.BlockSpec((RB, HG_COLS), rev), pl.BlockSpec((RB, HG_W), rev), pl.BlockSpec((RB, HG_W), rev),
                  pl.BlockSpec((cpb * HG_HEADS * DK, DK), rev), pl.BlockSpec((2, HG_W), fix),
                  pl.BlockSpec((1, DK), fix), pl.BlockSpec(memory_space=pl.ANY)],
        out_specs=[pl.BlockSpec((RB, HG_COLS), rev), pl.BlockSpec((2, HG_W), fix), pl.BlockSpec((1, DK), fix)],
        out_shape=[jax.ShapeDtypeStruct(dz_buf.shape, BF16), jax.ShapeDtypeStruct((2, HG_W), F32),
                   jax.ShapeDtypeStruct((1, DK), F32)],
        scratch_shapes=[pltpu.VMEM((HG_HEADS, DK, DK), F32), pltpu.VMEM((1, HG_W), F32)],
        input_output_aliases={6: 0},
        compiler_params=_params(dimension_semantics=("arbitrary",)),
    )(z, o_raw, dy, states, lb_logits, gain, dz_buf)


def _head_ones():
    r, c = _iota((FOX_W, FOX_W), 0), _iota((FOX_W, FOX_W), 1)
    return ((r // FOX_DH) == (c // FOX_DH)).astype(BF16)


def _log_sigmoid(x):
    return jnp.minimum(x, 0.0) - jnp.log(1.0 + jnp.exp(-jnp.abs(x)))


def _fox_prep(z, bias, qg, kg):
    T = z.shape[0]
    tm = min(ROW_BLOCK, T)
    nb = T // tm

    def body(z_ref, b_ref, qg_ref, kg_ref, q_ref, k_ref, v_ref, qa_ref, ka_ref, carry_ref):
        @pl.when(pl.program_id(0) == 0)
        def _():
            carry_ref[...] = jnp.zeros_like(carry_ref)

        ones = _head_ones()
        normed = []
        for src, g_ref in ((0, qg_ref), (1, kg_ref)):
            xv = z_ref[:, src * FOX_W:(src + 1) * FOX_W]
            ms = _split_dot(ones, xv * xv, 1, 0, mat_first=False) * (1.0 / FOX_DH)
            normed.append(xv * lax.rsqrt(ms + EPS) * g_ref[...])
        qn, kn = normed
        q_ref[...] = (qn * FOX_DH ** -0.5).astype(BF16)
        k_b = kn.astype(BF16)
        k_ref[...] = k_b
        v_ref[...] = z_ref[:, 2 * FOX_W:3 * FOX_W].astype(BF16)
        logf = _log_sigmoid(z_ref[:, 3 * FOX_W:FOX_COLS] + b_ref[...])
        r, c = _iota((tm, tm), 0), _iota((tm, tm), 1)
        tri_b = (c <= r).astype(BF16)
        cum = _split_dot(tri_b, logf, 1, 0, terms=3) + carry_ref[...]
        carry_ref[...] = cum[tm - 1:tm]
        c2 = cum * LOG2E
        hi = c2.astype(BF16)
        rem = c2 - hi.astype(F32)
        mid = rem.astype(BF16)
        lo = (rem - mid.astype(F32)).astype(BF16)
        hrow, col = _iota((LANES, 2 * FOX_W), 0), _iota((LANES, 2 * FOX_W), 1)
        base = hrow * LANES + jnp.where(hrow % 2 == 0, FOX_DH, 0)
        placed = None
        for t, part in enumerate((hi, mid, lo)):
            place = jnp.logical_and(col == base + t, hrow < FOX_HEADS).astype(BF16)
            term = _dot(part, place, 1, 0)
            placed = term if placed is None else placed + term
        colw = _iota((tm, 2 * FOX_W), 1)
        head, lane = colw // LANES, colw % LANES
        own = (lane < FOX_DH) == (head % 2 == 0)
        other = jnp.where(head % 2 == 0, lane - FOX_DH, lane)
        ones_q = jnp.where(jnp.logical_and(other >= 0, other < 3), -1.0, 0.0)
        q2 = (qn * (FOX_DH ** -0.5 * LOG2E)).astype(BF16)
        q_exp = jnp.concatenate([q2[:, (h // 2) * LANES:(h // 2 + 1) * LANES] for h in range(FOX_HEADS)], axis=1)
        k_exp = jnp.concatenate([k_b[:, (h // 2) * LANES:(h // 2 + 1) * LANES] for h in range(FOX_HEADS)], axis=1)
        qa_ref[...] = jnp.where(own, q_exp, ones_q.astype(BF16))
        ka_ref[...] = jnp.where(own, k_exp, placed.astype(BF16))

    wide = pl.BlockSpec((tm, 2 * FOX_W), lambda i: (i, 0))
    return _pallas(
        body, name="fox_prep", grid=(nb,),
        in_specs=[pl.BlockSpec((tm, SEG), lambda i: (i, 1)), pl.BlockSpec((1, LANES), lambda i: (0, 0)),
                  pl.BlockSpec((1, FOX_W), lambda i: (0, 0)), pl.BlockSpec((1, FOX_W), lambda i: (0, 0))],
        out_specs=[pl.BlockSpec((tm, FOX_W), lambda i: (i, 0))] * 3 + [wide] * 2,
        out_shape=[jax.ShapeDtypeStruct((T, FOX_W), BF16)] * 3 + [jax.ShapeDtypeStruct((T, 2 * FOX_W), BF16)] * 2,
        scratch_shapes=[pltpu.VMEM((1, LANES), F32)],
        compiler_params=_params(dimension_semantics=("arbitrary",)),
    )(z, bias, qg, kg)


def _fox_fwd(qa, ka, vb, blocks):
    T = qa.shape[0]
    tq = min(ROW_BLOCK, T)
    nq = T // tq
    NEG = -1e30
    n = len(blocks)
    n_pairs = FOX_HEADS // 2

    n_in = 3

    def body(*refs):
        q_ref, k_ref, v_ref = refs[:n_in]
        o_ref, lse_ref = refs[n_in + n:n_in + n + 2]
        m_sc, l_sc, acc_sc = refs[n_in + 2 * n + 2:n_in + 2 * n + 5]
        pr, qi = pl.program_id(0), pl.program_id(1)
        g_start, g_forward, g_finish = _gather_steps(
            refs[n_in:n_in + n], refs[n_in + n + 2:n_in + 2 * n + 2], *refs[n_in + 2 * n + 5:])

        @pl.when(jnp.logical_and(pr == 0, qi == 0))
        def _():
            g_start()

        @pl.when(jnp.logical_and(pr == n_pairs // 2, qi == 0))
        def _():
            g_forward()

        m_sc[...] = jnp.full_like(m_sc, NEG)
        l_sc[...] = jnp.zeros_like(l_sc)
        acc_sc[...] = jnp.zeros_like(acc_sc)
        lane = _iota((tq, LANES), 1)

        def block(masked, ki):
            keys = pl.ds(pl.multiple_of(ki * tq, tq), tq)
            vv = v_ref[keys, :]
            for hh in range(2):
                hs = slice(hh * LANES, (hh + 1) * LANES)
                s = _dot(q_ref[:, hs], k_ref[keys, hs], 1, 1)
                tiles = [s[:, j * LANES:(j + 1) * LANES] for j in range(tq // LANES)]
                if masked:
                    row, col = _iota((tq, LANES), 0), _iota((tq, LANES), 1)
                    tiles = [jnp.where(row >= col + j * LANES, t, NEG) for j, t in enumerate(tiles)]
                m_old = m_sc[hh]
                top = jnp.broadcast_to(jnp.max(functools.reduce(jnp.maximum, tiles), axis=-1, keepdims=True),
                                       (tq, LANES))
                m_new = jnp.maximum(m_old, top)
                alpha = jnp.exp2(m_old - m_new)
                ps = [jnp.exp2(t - m_new) for t in tiles]
                l_sc[hh] = alpha * l_sc[hh] + functools.reduce(jnp.add, ps)
                m_sc[hh] = m_new
                p_b = jnp.concatenate([p.astype(BF16) for p in ps], axis=1)
                acc_sc[hh] = alpha * acc_sc[hh] + _dot(p_b, vv, 1, 0)

        def before(ki, carry):
            block(False, ki)
            return carry

        lax.fori_loop(0, qi, before, 0)
        block(True, qi)
        l0 = jnp.sum(l_sc[0], axis=-1, keepdims=True)
        l1 = jnp.sum(l_sc[1], axis=-1, keepdims=True)
        o_ref[...] = jnp.where(lane < FOX_DH, acc_sc[0] * (1.0 / l0), acc_sc[1] * (1.0 / l1))
        lse_ref[:, :LANES] = m_sc[0] + jnp.log2(l0)
        lse_ref[:, LANES:] = m_sc[1] + jnp.log2(l1)

        @pl.when(jnp.logical_and(pr == n_pairs - 1, qi == nq - 1))
        def _():
            g_finish()

    qmap = lambda p, i: (i, p)
    whole = lambda p, i: (0, p)
    hbm = pl.BlockSpec(memory_space=pl.ANY)
    res = _pallas(
        body, name="fox_fwd", grid=(n_pairs, nq),
        in_specs=[pl.BlockSpec((tq, 2 * LANES), qmap), pl.BlockSpec((T, 2 * LANES), whole),
                  pl.BlockSpec((T, LANES), whole)] + [hbm] * n,
        out_specs=[pl.BlockSpec((tq, LANES), qmap), pl.BlockSpec((tq, 2 * LANES), qmap)] + [hbm] * n,
        out_shape=[jax.ShapeDtypeStruct((T, FOX_W), F32), jax.ShapeDtypeStruct((T, 2 * FOX_W), F32)]
        + [jax.ShapeDtypeStruct((N_DEV,) + b.shape, b.dtype) for b in blocks],
        scratch_shapes=[pltpu.VMEM((2, tq, LANES), F32)] * 3 + _gather_scratch(n),
        compiler_params=_params(dimension_semantics=("arbitrary",) * 2),
    )(qa, ka, vb, *blocks)
    return res[0], res[1], res[2:]


def _fox_bwd(qs, kn, vb, qa, ka, o, do, lse, hs):
    T = qs.shape[0]
    tq = min(ROW_BLOCK, T)
    nq = T // tq
    n = len(hs)
    n_pairs = FOX_HEADS // 2

    def body(*refs):
        q_ref, k_ref, v_ref, qa_ref, ka_ref, o_ref, do_ref, lse_ref = refs[:8]
        dq_ref, dk_ref, dv_ref, dcs_ref, drs_ref = refs[8 + n:13 + n]
        pr, ki = pl.program_id(0), pl.program_id(1)
        x_start, x_finish = _chip_exchange_steps(refs[8:8 + n], refs[13 + n:13 + 2 * n], *refs[13 + 2 * n:])

        @pl.when(jnp.logical_and(pr == 0, ki == 0))
        def _():
            x_start()
            drs_ref[...] = jnp.zeros_like(drs_ref)

        @pl.when(ki == 0)
        def _():
            dq_ref[...] = jnp.zeros_like(dq_ref)

        dk_ref[...] = jnp.zeros_like(dk_ref)
        dv_ref[...] = jnp.zeros_like(dv_ref)
        dcs_ref[...] = jnp.zeros_like(dcs_ref)

        def block(masked, qi):
            kv, vv = k_ref[...], v_ref[...]
            lane_k = _iota((tq, LANES), 1)
            dk_acc = jnp.zeros((tq, LANES), F32)
            dv_acc = jnp.zeros((tq, LANES), F32)
            dcs_acc = jnp.zeros((8, tq), F32)
            nt = tq // LANES
            th = tq // 2
            lane = _iota((th, LANES), 1)
            for half in range(2):
                qrows = pl.ds(pl.multiple_of(qi * tq + half * th, th), th)
                qv, ov, dov = q_ref[qrows, :], o_ref[qrows, :], do_ref[qrows, :]
                dq_acc = jnp.zeros((th, LANES), F32)
                drs_acc = jnp.zeros((th, LANES), F32)
                prod = dov * ov
                for hh in range(2):
                    in_head = (lane < FOX_DH) if hh == 0 else (lane >= FOX_DH)
                    in_head_k = (lane_k < FOX_DH) if hh == 0 else (lane_k >= FOX_DH)
                    hs_ = slice(hh * LANES, (hh + 1) * LANES)
                    qm = jnp.where(in_head, qv, jnp.zeros_like(qv))
                    km = jnp.where(in_head_k, kv, jnp.zeros_like(kv))
                    dom = jnp.where(in_head, dov, 0.0).astype(BF16)
                    delta_b = jnp.broadcast_to(jnp.sum(jnp.where(in_head, prod, 0.0), axis=1, keepdims=True),
                                               (th, LANES))
                    lse_b = lse_ref[qrows, hs_]
                    s = _dot(qa_ref[qrows, hs_], ka_ref[:, hs_], 1, 1)
                    dp = _dot(dom, vv, 1, 1)
                    p_tiles, ds_tiles, col_tiles = [], [], []
                    row_part = jnp.zeros((th, LANES), F32)
                    for j in range(nt):
                        js = slice(j * LANES, (j + 1) * LANES)
                        p = jnp.exp2(s[:, js] - lse_b)
                        if masked:
                            p = jnp.where(_iota((th, LANES), 0) + half * th >= _iota((th, LANES), 1) + j * LANES, p, 0.0)
                        ds = p * (dp[:, js] - delta_b)
                        p_tiles.append(p.astype(BF16))
                        ds_tiles.append(ds.astype(BF16))
                        col_tiles.append(jnp.sum(ds, axis=0, keepdims=True))
                        row_part = row_part + ds
                    p_b = jnp.concatenate(p_tiles, axis=1)
                    ds_b = jnp.concatenate(ds_tiles, axis=1)
                    dv_acc = dv_acc + _dot(p_b, dom, 0, 0)
                    dq_acc = dq_acc + _dot(ds_b, km, 1, 0)
                    dk_acc = dk_acc + _dot(ds_b, qm, 0, 0)
                    dcs_acc = dcs_acc + jnp.where(_iota((8, tq), 0) == hh, jnp.concatenate(col_tiles, axis=1), 0.0)
                    rowsum = jnp.sum(row_part, axis=1, keepdims=True)
                    drs_acc = drs_acc + jnp.where(lane == 2 * pr + hh, rowsum, 0.0)
                drs_ref[qrows, :] += drs_acc
                dq_ref[qrows, :] += dq_acc
            dk_ref[...] += dk_acc
            dv_ref[...] += dv_acc
            dcs_ref[0] += dcs_acc

        block(True, ki)

        def after(qi, carry):
            block(False, qi)
            return carry

        lax.fori_loop(ki + 1, nq, after, 0)

        @pl.when(jnp.logical_and(pr == n_pairs - 1, ki == nq - 1))
        def _():
            x_finish()

    whole = lambda p, j: (0, p)
    kmap = lambda p, j: (j, p)
    hbm = pl.BlockSpec(memory_space=pl.ANY)
    res = _pallas(
        body, name="fox_bwd", grid=(n_pairs, nq),
        in_specs=[pl.BlockSpec((T, LANES), whole), pl.BlockSpec((tq, LANES), kmap), pl.BlockSpec((tq, LANES), kmap),
                  pl.BlockSpec((T, 2 * LANES), whole), pl.BlockSpec((tq, 2 * LANES), kmap),
                  pl.BlockSpec((T, LANES), whole), pl.BlockSpec((T, LANES), whole), pl.BlockSpec((T, 2 * LANES), whole)]
        + [hbm] * n,
        out_specs=[pl.BlockSpec((T, LANES), whole), pl.BlockSpec((tq, LANES), kmap),
                   pl.BlockSpec((tq, LANES), kmap), pl.BlockSpec((1, 8, tq), lambda p, j: (p, 0, j)),
                   pl.BlockSpec((T, LANES), lambda p, j: (0, 0))] + [hbm] * n,
        out_shape=[jax.ShapeDtypeStruct((T, FOX_W), F32)] * 3
        + [jax.ShapeDtypeStruct((n_pairs, 8, T), F32), jax.ShapeDtypeStruct((T, LANES), F32)]
        + _chip_exchange_shapes(hs),
        scratch_shapes=_chip_exchange_scratch(n),
        compiler_params=_params(dimension_semantics=("arbitrary",) * 2),
    )(qs, kn, vb, qa, ka, o, do, lse, *hs)
    res = list(res)
    return res[:5] + [res[5:]]


def _fox_post(z, dq, dk, dv, dcs, drs, bias, qg, kg, dz_buf):
    T = z.shape[0]
    tm = min(ROW_BLOCK, T)
    nb = T // tm

    def body(z_ref, dq_ref, dk_ref, dv_ref, dcs_ref, drs_ref, b_ref, qg_ref, kg_ref, _buf_ref, dz_ref, dqg_ref, dkg_ref,
             db_ref, carry_ref):
        @pl.when(pl.program_id(0) == 0)
        def _():
            carry_ref[...] = jnp.zeros_like(carry_ref)
            dqg_ref[...] = jnp.zeros_like(dqg_ref)
            dkg_ref[...] = jnp.zeros_like(dkg_ref)
            db_ref[...] = jnp.zeros_like(db_ref)

        ones = _head_ones()
        for src, g_ref, d_ref, dg_ref, scale in ((0, qg_ref, dq_ref, dqg_ref, FOX_DH ** -0.5), (1, kg_ref, dk_ref, dkg_ref, 1.0)):
            xv = z_ref[:, src * FOX_W:(src + 1) * FOX_W]
            ms = _split_dot(ones, xv * xv, 1, 0, mat_first=False) * (1.0 / FOX_DH)
            rstd = lax.rsqrt(ms + EPS)
            xh = xv * rstd
            dn = d_ref[...] * scale
            dg_ref[...] += jnp.sum(dn * xh, axis=0, keepdims=True)
            dxh = dn * g_ref[...]
            mean = _split_dot(ones, dxh * xh, 1, 0, mat_first=False) * (1.0 / FOX_DH)
            dz_ref[:, src * FOX_W:(src + 1) * FOX_W] = (rstd * (dxh - xh * mean)).astype(BF16)
        dz_ref[:, 2 * FOX_W:3 * FOX_W] = dv_ref[...].astype(BF16)
        row8 = _iota((8, tm), 0)
        dct = jnp.zeros((8, tm), F32)
        for h in range(FOX_HEADS):
            src_row = dcs_ref[h // 2][h % 2:h % 2 + 1, :]
            dct = dct + jnp.where(row8 == h, src_row, 0.0)
        dct = drs_ref[...].T[0:8] - dct
        r, c = _iota((tm, tm), 0), _iota((tm, tm), 1)
        upper_b = (r >= c).astype(BF16)
        rc = _split_dot(upper_b, dct, 1, 0, mat_first=False) + carry_ref[...]
        carry_ref[...] = rc[:, 0:1]
        full = jnp.concatenate([rc, jnp.zeros((LANES - 8, tm), F32)], axis=0)
        dlogf = full.T
        xf = z_ref[:, 3 * FOX_W:FOX_COLS] + b_ref[...]
        df = dlogf * (1.0 - _sigmoid(xf))
        dz_ref[:, 3 * FOX_W:FOX_COLS] = df.astype(BF16)
        dz_ref[:, FOX_COLS:] = jnp.zeros((tm, SEG - FOX_COLS), BF16)
        db_ref[...] += jnp.sum(df, axis=0, keepdims=True)

    rev = lambda i: (nb - 1 - i, 0)
    fix2 = lambda i: (0, 0)
    return _pallas(
        body, name="fox_post", grid=(nb,),
        in_specs=[pl.BlockSpec((tm, SEG), lambda i: (nb - 1 - i, 1)), pl.BlockSpec((tm, FOX_W), rev),
                  pl.BlockSpec((tm, FOX_W), rev),
                  pl.BlockSpec((tm, FOX_W), rev), pl.BlockSpec((FOX_HEADS // 2, 8, tm), lambda i: (0, 0, nb - 1 - i)),
                  pl.BlockSpec((tm, LANES), rev),
                  pl.BlockSpec((1, LANES), fix2), pl.BlockSpec((1, FOX_W), fix2), pl.BlockSpec((1, FOX_W), fix2),
                  pl.BlockSpec(memory_space=pl.ANY)],
        out_specs=[pl.BlockSpec((tm, SEG), lambda i: (nb - 1 - i, 1)), pl.BlockSpec((1, FOX_W), fix2),
                   pl.BlockSpec((1, FOX_W), fix2), pl.BlockSpec((1, LANES), fix2)],
        out_shape=[jax.ShapeDtypeStruct(dz_buf.shape, BF16), jax.ShapeDtypeStruct((1, FOX_W), F32),
                   jax.ShapeDtypeStruct((1, FOX_W), F32), jax.ShapeDtypeStruct((1, LANES), F32)],
        scratch_shapes=[pltpu.VMEM((8, 1), F32)],
        input_output_aliases={9: 0},
        compiler_params=_params(dimension_semantics=("arbitrary",)),
    )(z, dq, dk, dv, dcs, drs, bias, qg, kg, dz_buf)


def _local_step(x, p, tgt, sm, chunks, core):
    lbl, og, fb = sm["hg_lb_logits"], sm["hg_onorm_g"], sm["fox_f_bias"]
    fbias = jnp.pad(fb, ((0, 0), (0, LANES - FOX_HEADS)))
    qg = jnp.tile(sm["fox_q_norm_g"], (1, FOX_HEADS))
    kg = jnp.tile(sm["fox_k_norm_g"], (1, FOX_HEADS))

    assert BIG[0] == "w_in"
    h, got0 = _rms_fwd(x, sm["norm_mix_g"], chunks[:1], name="rms_mix")
    W = {"w_in": _full_of_chunks("w_in", got0[0])}
    rest = dict(zip(BIG[1:], chunks[1:]))
    first, second = ["w_ffn_gate"], ["w_ffn_up"]
    third = [n for n in BIG[1:] if n not in first + second]
    z, got1 = _matmul(h, W["w_in"], tb=True, gather=[rest[n] for n in first], name="mm_z")
    o_raw, ya, states, got2 = _hgrn_fwd(z, lbl, og, [rest[n] for n in second])
    qs, kn, vb, qa, ka = _fox_prep(z, fbias, qg, kg)
    yb, lse, got3 = _fox_fwd(qa, ka, vb, [rest[n] for n in third])
    W = dict(W, **{n: _full_of_chunks(n, g)
                   for n, g in zip(first + second + third, list(got1) + list(got2) + list(got3))})
    merged, ua, ub = _merge_fwd(ya, yb, W["w_branch_a"], W["w_branch_b"], z)
    x1, hf = _matmul(merged, W["w_out"], add=x, norm_fwd=sm["norm_ffn_g"], name="mm_x1")
    a, b, act = _swiglu_fwd(hf, W["w_ffn_gate"], W["w_ffn_up"])
    x2 = _matmul(act, W["w_ffn_down"], add=x1, name="mm_x2")
    hp, dy, dsp, dpp, loss = _ple_loss(x2, p, sm["norm_ple_g"], W["w_ple_gate"], W["w_ple_proj"], tgt)

    G = {}
    G["w_ple_proj"] = _matmul(p, dpp, ta=True, out_dtype=BF16, name="mm_dw_ple_proj")
    G["w_ple_gate"] = _matmul(hp, dsp, ta=True, out_dtype=BF16, name="mm_dw_ple_gate")
    dx2, d_ple_g = _matmul(dsp, W["w_ple_gate"], tb=True, norm_bwd=(x2, sm["norm_ple_g"], dy), name="mm_dx2")
    G["w_ffn_down"] = _matmul(act, dx2, ta=True, out_dtype=BF16, name="mm_dw_ffn_down")
    da, db = _swiglu_bwd(dx2, W["w_ffn_down"], a, b)
    G["w_ffn_gate"] = _matmul(da, hf, ta=True, out_dtype=BF16, name="mm_dw_ffn_gate")
    G["w_ffn_up"] = _matmul(db, hf, ta=True, out_dtype=BF16, name="mm_dw_ffn_up")
    dhf = _matmul(da, W["w_ffn_gate"], name="mm_dhf_a")
    dx1, d_ffn_g = _matmul(db, W["w_ffn_up"], add=dhf, norm_bwd=(x1, sm["norm_ffn_g"], dx2), name="mm_dx1")
    G["w_out"] = _matmul(merged, dx1, ta=True, out_dtype=BF16, name="mm_dw_out")
    late = ["w_branch_a", "w_branch_b"]
    early = [n for n in BIG[1:] if n not in late]
    g4 = {n: _chunks4(n, G[n]) for n in early}
    (dua, dub, dz, dya, dyb), got_early = _merge_bwd(dx1, W["w_out"], W["w_branch_a"], W["w_branch_b"], ua, ub, z,
                                                     [g4[n] for n in early])
    G["w_branch_a"] = _matmul(ya, dua, ta=True, out_dtype=BF16, name="mm_dw_branch_a")
    G["w_branch_b"] = _matmul(yb, dub, ta=True, out_dtype=BF16, name="mm_dw_branch_b")
    g4.update({n: _chunks4(n, G[n]) for n in late})
    got4 = dict(zip(early, got_early))
    got4.update(zip(late, _sibling_exchange([g4[n] for n in late], name="grads_to_sibling_branch")))
    hb_rest = _chip_sum([g4[n] for n in BIG[1:]], [got4[n] for n in BIG[1:]], core, name="chip_sum_rest")
    dq, dk, dv, dcs, drs, got_rest = _fox_bwd(qs, kn, vb, qa, ka, yb, dyb, lse, hb_rest)
    dz, d_qg, d_kg, d_fb = _fox_post(z, dq, dk, dv, dcs, drs, fbias, qg, kg, dz)
    dz, d_lbl, d_og = _hgrn_bwd(z, o_raw, dya, states, lbl, og, dz)
    G["w_in"] = _matmul(dz, h, ta=True, out_dtype=BF16, name="mm_dw_in")
    hb_in = _sibling_sums({"w_in": G["w_in"]}, core, tag="w_in")
    (grad_x, d_mix_g), got_in = _matmul(dz, W["w_in"], exchange=hb_in,
                                        norm_bwd=(x, sm["norm_mix_g"], dx1), name="mm_dx")

    gs = {"norm_mix_g": d_mix_g, "hg_lb_logits": d_lbl, "hg_onorm_g": d_og, "fox_f_bias": d_fb[:, :FOX_HEADS],
          "fox_q_norm_g": d_qg.reshape(FOX_HEADS, FOX_DH).sum(0, keepdims=True),
          "fox_k_norm_g": d_kg.reshape(FOX_HEADS, FOX_DH).sum(0, keepdims=True),
          "norm_ffn_g": d_ffn_g, "norm_ple_g": d_ple_g}
    return loss, grad_x, gs, hb_in + hb_rest, list(got_in) + list(got_rest)


def _pack_rows(parts, total):
    buf = jnp.concatenate(parts, axis=-2)
    pad = total - buf.shape[-2]
    widths = [(0, 0)] * (buf.ndim - 2) + [(0, pad), (0, 0)]
    return jnp.pad(buf, widths)


def _chunk_of_shard(n, w):
    if n == "w_in":
        return jnp.pad(w, ((0, IN_SHARD_PAD - IN_SHARD), (0, 0)))
    return w


def _full_of_chunks(n, g):
    _, a, b = g.shape
    if n == "w_in":
        w = g[:, :IN_SHARD].reshape(IN_COLS, b)
        gap = jnp.zeros((SEG - FOX_LOGICAL, b), g.dtype)
        return jnp.concatenate([w[:HG_COLS + FOX_LOGICAL], gap, w[HG_COLS + FOX_LOGICAL:]], axis=0)
    if BIG_SHAPE[n][2] == 0:
        return g.reshape(N_DEV * a, b)
    return g.transpose(1, 0, 2).reshape(a, N_DEV * b)


def _chunks_of_full(n, g):
    if n == "w_in":
        w = jnp.concatenate([g[:HG_COLS + FOX_LOGICAL], g[2 * SEG:]], axis=0).reshape(N_DEV, IN_SHARD, g.shape[1])
        return jnp.pad(w, ((0, 0), (0, IN_SHARD_PAD - IN_SHARD), (0, 0)))
    if BIG_SHAPE[n][2] == 0:
        return g.reshape(N_DEV, g.shape[0] // N_DEV, g.shape[1])
    return g.reshape(g.shape[0], N_DEV, g.shape[1] // N_DEV).transpose(1, 0, 2)


def _pack_small(vals, loss_row=None):
    parts = [vals[n].reshape(SMALL_ROWS[n], -1) for n in SMALL]
    parts = [jnp.pad(v, ((0, 0), (0, LANES - v.shape[1]))) for v in parts]
    if loss_row is not None:
        parts.append(loss_row)
    return _pack_rows(parts, SMALL_TOTAL)


def _unpack_small(buf, like):
    out, r0 = {}, 0
    for n in SMALL:
        rows, size = SMALL_ROWS[n], like[n].size
        blk = buf[r0:r0 + rows]
        out[n] = (blk if size == rows * LANES else blk[:, :size]).reshape(like[n].shape)
        r0 += rows
    return out


def _place():
    return lax.axis_index("x"), lax.axis_index("y"), lax.axis_index("c")


def _gather_steps(x_refs, out_refs, send_sems, recv_sems, local_sems):
    n = len(x_refs)
    x, y, c = _place()
    me, sibling = (x, y, c), (x, y, 1 - c)
    chips = [(1 - x, y), (x, 1 - y), (1 - x, 1 - y)]

    def slot(i, px, py, pc):
        return out_refs[i].at[4 * px + 2 * py + pc]

    def copy(k, i, blk, to, own=False):
        return pltpu.make_async_remote_copy(
            src_ref=x_refs[i] if own else slot(i, *blk), dst_ref=slot(i, *blk),
            send_sem=send_sems.at[k, i], recv_sem=recv_sems.at[k, i], device_id=to, device_id_type=MESH)

    def mine():
        return [pltpu.make_async_copy(x_refs[i], slot(i, *me), local_sems.at[i]) for i in range(n)]

    def first():
        cps = [copy(0, i, me, sibling, own=True) for i in range(n)]
        return cps + [copy(1 + j, i, me, (*chip, c), own=True) for j, chip in enumerate(chips) for i in range(n)]

    def passed():
        return [copy(4 + j, i, (*chip, c), sibling) for j, chip in enumerate(chips) for i in range(n)]

    def start():
        for cp in mine() + first():
            cp.start()

    def forward():
        fws = passed()
        for j, chip in enumerate(chips):
            for i in range(n):
                copy(1 + j, i, (*chip, c), me).wait_recv()
                fws[j * n + i].start()

    def finish():
        for i in range(n):
            copy(0, i, sibling, me).wait_recv()
        for j, chip in enumerate(chips):
            for i in range(n):
                copy(4 + j, i, (*chip, 1 - c), me).wait_recv()
        for cp in first() + passed():
            cp.wait_send()
        for cp in mine():
            cp.wait()

    return start, forward, finish


def _gather_scratch(n):
    return [pltpu.SemaphoreType.DMA((7, n)), pltpu.SemaphoreType.DMA((7, n)), pltpu.SemaphoreType.DMA((n,))]


def _sibling_steps(g_refs, out_refs, send_sems, recv_sems):
    n = len(g_refs)
    x, y, c = _place()

    def copies():
        return [pltpu.make_async_remote_copy(
            src_ref=g_refs[i].at[:, pl.ds(1 - c, 1)], dst_ref=out_refs[i], send_sem=send_sems.at[i],
            recv_sem=recv_sems.at[i], device_id=(x, y, 1 - c), device_id_type=MESH) for i in range(n)]

    def start():
        for cp in copies():
            cp.start()

    def finish():
        for cp in copies():
            cp.wait()

    return start, finish


def _sibling_shapes(gs):
    return [jax.ShapeDtypeStruct((N_CHIP, 1) + g.shape[2:], g.dtype) for g in gs]


def _sibling_scratch(n):
    return [pltpu.SemaphoreType.DMA((n,)), pltpu.SemaphoreType.DMA((n,))]


def _sibling_exchange(gs, *, name):
    n = len(gs)

    def body(*refs):
        for step in _sibling_steps(refs[:n], refs[n:2 * n], *refs[2 * n:]):
            step()

    hbm = pl.BlockSpec(memory_space=pl.ANY)
    return _pallas(
        body, name=name, out_shape=_sibling_shapes(gs), in_specs=[hbm] * n, out_specs=[hbm] * n,
        scratch_shapes=_sibling_scratch(n),
    )(*gs)


def _chip_sum(g4s, gots, core, *, name):
    n = len(g4s)

    def body(c_ref, *refs):
        for g_ref, r_ref, h_ref in zip(refs[:n], refs[n:2 * n], refs[2 * n:]):
            h_ref[0] = (g_ref[0, 0].astype(F32) + r_ref[0, 0].astype(F32)).astype(BF16)

    shapes = [g.shape[2:] for g in g4s]
    grid_spec = pltpu.PrefetchScalarGridSpec(
        num_scalar_prefetch=1, grid=(N_CHIP,),
        in_specs=[pl.BlockSpec((1, 1) + s, lambda j, c: (j, c[0], 0, 0)) for s in shapes]
        + [pl.BlockSpec((1, 1) + s, lambda j, c: (j, 0, 0, 0)) for s in shapes],
        out_specs=[pl.BlockSpec((1,) + s, lambda j, c: (j, 0, 0)) for s in shapes])
    return list(_pallas(
        body, name=name, grid_spec=grid_spec, out_shape=[jax.ShapeDtypeStruct((N_CHIP,) + s, BF16) for s in shapes],
        compiler_params=_params(dimension_semantics=("arbitrary",)),
    )(core, *g4s, *gots))


def _chip_exchange_steps(h_refs, out_refs, send_sems, recv_sems):
    n = len(h_refs)
    x, y, c = _place()
    chips = [(1 - x, y), (x, 1 - y), (1 - x, 1 - y)]

    def copies():
        return [pltpu.make_async_remote_copy(
            src_ref=h_refs[i].at[2 * px + py], dst_ref=out_refs[i].at[k], send_sem=send_sems.at[k, i],
            recv_sem=recv_sems.at[k, i], device_id=(px, py, c), device_id_type=MESH)
            for k, (px, py) in enumerate(chips) for i in range(n)]

    def start():
        for cp in copies():
            cp.start()

    def finish():
        for cp in copies():
            cp.wait()

    return start, finish


def _chip_exchange_shapes(hs):
    return [jax.ShapeDtypeStruct((3,) + h.shape[1:], h.dtype) for h in hs]


def _chip_exchange_scratch(n):
    return [pltpu.SemaphoreType.DMA((3, n)), pltpu.SemaphoreType.DMA((3, n))]


def _chunks4(n, g):
    gc = _chunks_of_full(n, g)
    return gc.reshape((N_CHIP, 2) + gc.shape[1:])


def _sibling_sums(G, core, *, tag):
    g4 = [_chunks4(n, g) for n, g in G.items()]
    got = _sibling_exchange(g4, name="grads_to_sibling_" + tag)
    return _chip_sum(g4, got, core, name="chip_sum_" + tag)


def _adam_math(w, g, m, v):
    m = ADAM_B1 * m + (1.0 - ADAM_B1) * g
    v = ADAM_B2 * v + (1.0 - ADAM_B2) * (g * g)
    m_hat = m / (1.0 - ADAM_B1 ** ADAM_STEP)
    v_hat = v / (1.0 - ADAM_B2 ** ADAM_STEP)
    delta = -ADAM_LR * (m_hat / (jnp.sqrt(v_hat) + ADAM_EPS) + ADAM_WD * w)
    return delta, m, v


def _adam_shard(hb, got, chip, w, m, v, *, name):
    _, r, c = w.shape
    _, a, cb = hb.shape
    assert cb == c and c % LANES == 0, (hb.shape, w.shape)
    tc = _pick(c, 2 * LANES)

    def body(j_ref, h_ref, r_ref, w_ref, m_ref, v_ref, g_ref, d_ref, nm_ref, nv_ref):
        parts = [h_ref[0], r_ref[0], r_ref[1], r_ref[2]]
        g = None
        for part in parts:
            part = part[:r].astype(F32)
            g = part if g is None else g + part
        d, nm, nv = _adam_math(w_ref[0], g, m_ref[0], v_ref[0])
        g_ref[0] = g
        d_ref[0] = d
        nm_ref[0] = nm
        nv_ref[0] = nv

    blk = pl.BlockSpec((1, r, tc), lambda i, j: (0, 0, i))
    grid_spec = pltpu.PrefetchScalarGridSpec(
        num_scalar_prefetch=1, grid=(c // tc,),
        in_specs=[pl.BlockSpec((1, a, tc), lambda i, j: (j[0], 0, i)),
                  pl.BlockSpec((3, a, tc), lambda i, j: (0, 0, i)), blk, blk, blk],
        out_specs=[blk] * 4)
    return _pallas(
        body, name=name, grid_spec=grid_spec, out_shape=[jax.ShapeDtypeStruct((1, r, c), F32)] * 4,
        compiler_params=_params(dimension_semantics=("arbitrary",)),
    )(chip, hb, got, w, m, v)


def _small_all_reduce_adam(gs, w, m, v):
    def body(g_ref, w_ref, m_ref, v_ref, sum_ref, d_ref, nm_ref, nv_ref, gather, send_sems, recv_sems):
        x, y, c = _place()
        my = 4 * x + 2 * y + c
        gather[my] = g_ref[...]
        cps = []
        for k in range(1, N_DEV):
            to = (x ^ (k >> 2), y ^ ((k >> 1) & 1), c ^ (k & 1))
            cps.append(pltpu.make_async_remote_copy(
                src_ref=g_ref, dst_ref=gather.at[my], send_sem=send_sems.at[k - 1], recv_sem=recv_sems.at[k - 1],
                device_id=to, device_id_type=MESH))
        for cp in cps:
            cp.start()
        for cp in cps:
            cp.wait()
        total = gather[0]
        for d in range(1, N_DEV):
            total = total + gather[d]
        dlt, nm, nv = _adam_math(w_ref[...], total, m_ref[...], v_ref[...])
        sum_ref[...] = total
        d_ref[...] = dlt
        nm_ref[...] = nm
        nv_ref[...] = nv

    vm = pl.BlockSpec(memory_space=pltpu.VMEM)
    return _pallas(
        body, name="small_all_reduce_adam", out_shape=[jax.ShapeDtypeStruct((SMALL_TOTAL, LANES), F32)] * 4,
        in_specs=[vm] * 4, out_specs=[vm] * 4,
        scratch_shapes=[pltpu.VMEM((N_DEV, SMALL_TOTAL, LANES), F32), pltpu.SemaphoreType.DMA((7,)),
                        pltpu.SemaphoreType.DMA((7,))],
            )(gs, w, m, v)


def kernel(x, p, norm_mix_g, w_in, hg_lb_logits, hg_onorm_g, fox_f_bias, fox_q_norm_g, fox_k_norm_g, w_branch_a, w_branch_b, w_out, norm_ffn_g, w_ffn_gate, w_ffn_up, w_ffn_down, norm_ple_g, w_ple_gate, w_ple_proj, loss_target, m_norm_mix_g, m_w_in, m_hg_lb_logits, m_hg_onorm_g, m_fox_f_bias, m_fox_q_norm_g, m_fox_k_norm_g, m_w_branch_a, m_w_branch_b, m_w_out, m_norm_ffn_g, m_w_ffn_gate, m_w_ffn_up, m_w_ffn_down, m_norm_ple_g, m_w_ple_gate, m_w_ple_proj, v_norm_mix_g, v_w_in, v_hg_lb_logits, v_hg_onorm_g, v_fox_f_bias, v_fox_q_norm_g, v_fox_k_norm_g, v_w_branch_a, v_w_branch_b, v_w_out, v_norm_ffn_g, v_w_ffn_gate, v_w_ffn_up, v_w_ffn_down, v_norm_ple_g, v_w_ple_gate, v_w_ple_proj):
    args = dict(locals())
    wts = {n: args[n] for n in BIG + SMALL}
    mom = {n: args["m_" + n] for n in BIG + SMALL}
    var = {n: args["v_" + n] for n in BIG + SMALL}
    for group in (wts, mom, var):
        for n in TRANSPOSED:
            group[n] = jnp.swapaxes(group[n], 1, 2)
    sm = {n: wts[n] for n in SMALL}

    xi, yi, ci = _place()
    core = jnp.reshape(ci, (1,)).astype(jnp.int32)
    chip = jnp.reshape(2 * xi + yi, (1,)).astype(jnp.int32)
    chunks = [_chunk_of_shard(n, wts[n][0].astype(BF16)) for n in BIG]
    loss_blk, grad_x, gs, hb, got = _local_step(x[0], p[0, 0], loss_target[0], sm, chunks, core)

    g_big, d_big, nm_big, nv_big = {}, {}, {}, {}
    for n, h, r in zip(BIG, hb, got):
        res = _adam_shard(h, r, chip, wts[n], mom[n], var[n], name="adam_" + n)
        if n in TRANSPOSED:
            res = [jnp.swapaxes(t, 1, 2) for t in res]
        g_big[n], d_big[n], nm_big[n], nv_big[n] = res

    s_sum, s_d, s_nm, s_nv = _small_all_reduce_adam(
        _pack_small(gs, loss_blk[0:1]), _pack_small(sm), _pack_small({n: mom[n] for n in SMALL}),
        _pack_small({n: var[n] for n in SMALL}))
    loss = s_sum[LOSS_ROW, 0]
    g_small, d_small, nm_small, nv_small = (_unpack_small(t, sm) for t in (s_sum, s_d, s_nm, s_nv))

    order = ["norm_mix_g", "w_in", "hg_lb_logits", "hg_onorm_g", "fox_f_bias", "fox_q_norm_g", "fox_k_norm_g",
             "w_branch_a", "w_branch_b", "w_out", "norm_ffn_g", "w_ffn_gate", "w_ffn_up", "w_ffn_down", "norm_ple_g",
             "w_ple_gate", "w_ple_proj"]
    outs = [loss, grad_x[None]]
    for big, small in ((g_big, g_small), (d_big, d_small), (nm_big, nm_small), (nv_big, nv_small)):
        outs += [big[n] if n in big else small[n] for n in order]
    return tuple(outs)
```
